```python
import math
import jax, jax.numpy as jnp
from jax import lax
import numpy as np

D_MODEL = 1024
BATCH = 8
SEQ = 4096
DEPTH = 2

N_MEM = 256
NORM_EPS = 1e-6
ROPE_THETA = 500000.0
HEAD_DIM = 64
ROT_DIM = HEAD_DIM // 4
BLOCK = 128
LRU_WIDTH = D_MODEL
LRU_HEADS = 4
LRU_HEAD_DIM = LRU_WIDTH // LRU_HEADS
CONV_WIDTH = 4
LRU_C = 8.0
B_HEADS = 8
B_WIDTH = B_HEADS * HEAD_DIM
DILATED_PATTERN = ((128, 1), (512, 4), (2048, 16))
C_HEADS = 16
C_KV_HEADS = 2
C_WINDOW = 128
C_Q_WIDTH = C_HEADS * HEAD_DIM
C_KV_WIDTH = C_KV_HEADS * HEAD_DIM
C_QKV = C_Q_WIDTH + 2 * C_KV_WIDTH
XA_HEADS = 4
XA_HEAD_DIM = 128
XA_WIDTH = XA_HEADS * XA_HEAD_DIM
D_FF = -(-8 * D_MODEL // (3 * 256)) * 256
AB_IN = 2 * LRU_WIDTH + 3 * B_WIDTH
AB_SPLITS = (LRU_WIDTH, 2 * LRU_WIDTH, 2 * LRU_WIDTH + B_WIDTH, 2 * LRU_WIDTH + 2 * B_WIDTH)
AB_OUT = LRU_WIDTH + B_WIDTH
N_EVEN = (DEPTH + 1) // 2
N_ODD = DEPTH // 2
NEG = -1e30

kernel_name = "hybrid_rglru_dilated_swa_sink_block"


def rms_norm(x, g):
    xf = x.astype(jnp.float32)
    xf = xf * lax.rsqrt(jnp.mean(xf * xf, axis=-1, keepdims=True) + NORM_EPS)
    return (xf * g.astype(jnp.float32)).astype(x.dtype)


def rope_tables(L):
    inv = ROPE_THETA ** (-jnp.arange(0, ROT_DIM, 2, dtype=jnp.float32) / ROT_DIM)
    ang = jnp.arange(L, dtype=jnp.float32)[:, None] * inv[None, :]
    return jnp.cos(ang), jnp.sin(ang)


def partial_rope(x, cos, sin):
    half = ROT_DIM // 2
    cos = cos.astype(x.dtype)
    sin = sin.astype(x.dtype)
    x1 = x[..., :half]
    x2 = x[..., half:ROT_DIM]
    return jnp.concatenate([x1 * cos - x2 * sin, x2 * cos + x1 * sin, x[..., ROT_DIM:]], axis=-1)


def banded_attention(q, k, v, max_dist, sinks=None):
    Bsz, Hq, L, hd = q.shape
    Hkv = k.shape[1]
    G = Hq // Hkv
    nb = -(-L // BLOCK)
    Lp = nb * BLOCK
    pad = ((0, 0), (0, 0), (0, Lp - L), (0, 0))
    qb = jnp.pad(q, pad).reshape(Bsz, Hkv, G, nb, BLOCK, hd)
    kb = jnp.pad(k, pad).reshape(Bsz, Hkv, nb, BLOCK, hd)
    vb = jnp.pad(v, pad).reshape(Bsz, Hkv, nb, BLOCK, hd)

    def band(t):
        prev = jnp.pad(t, ((0, 0), (0, 0), (1, 0), (0, 0), (0, 0)))[:, :, :nb]
        return jnp.concatenate([prev, t], axis=3)

    kband, vband = band(kb), band(vb)
    s = jnp.einsum('bhgnqd,bhnkd->bhgnqk', qb, kband).astype(jnp.float32) * (hd ** -0.5)
    blk = jnp.arange(nb)[:, None, None]
    qpos = blk * BLOCK + jnp.arange(BLOCK)[None, :, None]
    kpos = (blk - 1) * BLOCK + jnp.arange(2 * BLOCK)[None, None, :]
    dist = qpos - kpos
    mask = (dist >= 0) & (dist <= max_dist) & (kpos >= 0)
    s = jnp.where(mask, s, NEG)
    m = jnp.max(s, axis=-1, keepdims=True)
    if sinks is not None:
        sk = sinks.astype(jnp.float32).reshape(1, Hkv, G, 1, 1, 1)
        m = jnp.maximum(m, sk)
        e = jnp.exp(s - m)
        den = jnp.sum(e, axis=-1, keepdims=True) + jnp.exp(sk - m)
    else:
        e = jnp.exp(s - m)
        den = jnp.sum(e, axis=-1, keepdims=True)
    p = e / den
    lse = (m + jnp.log(den))[..., 0]
    o = jnp.einsum('bhgnqk,bhnkd->bhgnqd', p.astype(v.dtype), vband)
    o = o.reshape(Bsz, Hq, Lp, hd)[:, :, :L]
    lse = lse.reshape(Bsz, Hq, Lp)[:, :, :L]
    return o, lse


def dilated_attention(q, k, v):
    Bsz, H, L, hd = q.shape
    outs, lses = [], []
    for window, d in DILATED_PATTERN:
        Ld = L // d

        def to_strided(t):
            return t.reshape(Bsz, H, Ld, d, hd).transpose(0, 1, 3, 2, 4).reshape(Bsz, H * d, Ld, hd)

        o, lse = banded_attention(to_strided(q), to_strided(k), to_strided(v), window // d)
        outs.append(o.reshape(Bsz, H, d, Ld, hd).transpose(0, 1, 3, 2, 4).reshape(Bsz, H, L, hd))
        lses.append(lse.reshape(Bsz, H, d, Ld).transpose(0, 1, 3, 2).reshape(Bsz, H, L))
    w = jax.nn.softmax(jnp.stack(lses, axis=0), axis=0)
    o = jnp.sum(w[..., None] * jnp.stack(outs, axis=0).astype(jnp.float32), axis=0)
    return o.astype(q.dtype)


def causal_depthwise_conv(x, w, b):
    L = x.shape[1]
    xp = jnp.pad(x, ((0, 0), (CONV_WIDTH - 1, 0), (0, 0)))
    y = b + xp[:, 0:L] * w[0]
    for tap in range(1, CONV_WIDTH):
        y = y + xp[:, tap:tap + L] * w[tap]
    return y


def block_diag_linear(x, w, b):
    Bsz, L, _ = x.shape
    xh = x.reshape(Bsz, L, LRU_HEADS, LRU_HEAD_DIM)
    y = jnp.einsum('blhi,hij->blhj', xh, w) + b
    return y.reshape(Bsz, L, LRU_WIDTH)


def rg_lru(x, wa, ba, wx, bx, lam):
    r = jax.nn.sigmoid(block_diag_linear(x, wa, ba).astype(jnp.float32))
    i = jax.nn.sigmoid(block_diag_linear(x, wx, bx).astype(jnp.float32))
    log_a = -LRU_C * r * jax.nn.softplus(-lam.astype(jnp.float32))
    a = jnp.exp(log_a)
    u = jnp.sqrt(-jnp.expm1(2.0 * log_a)) * (i * x.astype(jnp.float32))

    def combine(left, right):
        a1, b1 = left
        a2, b2 = right
        return a1 * a2, a2 * b1 + b2

    _, h = lax.associative_scan(combine, (a, u), axis=1)
    return h.astype(x.dtype)


def lru_dilated_mixer(h, cos, sin, w_in, conv_w, conv_b, wa, ba, wx, bx, lam, w_out):
    Bsz, L, _ = h.shape
    proj = h @ w_in
    x_br, y_br, q, k, v = jnp.split(proj, list(AB_SPLITS), axis=-1)
    rec = rg_lru(causal_depthwise_conv(x_br, conv_w, conv_b), wa, ba, wx, bx, lam) * jax.nn.gelu(y_br)

    def heads(t):
        return t.reshape(Bsz, L, B_HEADS, HEAD_DIM).transpose(0, 2, 1, 3)

    q = partial_rope(heads(q), cos, sin)
    k = partial_rope(heads(k), cos, sin)
    att = dilated_attention(q, k, heads(v)).transpose(0, 2, 1, 3).reshape(Bsz, L, B_WIDTH)
    return jnp.concatenate([rec, att], axis=-1) @ w_out


def swa_sink_mixer(h, cos, sin, w_qkv, b_qkv, sinks, w_out, b_out):
    Bsz, L, _ = h.shape
    proj = h @ w_qkv + b_qkv
    q, k, v = jnp.split(proj, [C_Q_WIDTH, C_Q_WIDTH + C_KV_WIDTH], axis=-1)
    q = partial_rope(q.reshape(Bsz, L, C_HEADS, HEAD_DIM).transpose(0, 2, 1, 3), cos, sin)
    k = partial_rope(k.reshape(Bsz, L, C_KV_HEADS, HEAD_DIM).transpose(0, 2, 1, 3), cos, sin)
    v = v.reshape(Bsz, L, C_KV_HEADS, HEAD_DIM).transpose(0, 2, 1, 3)
    o, _ = banded_attention(q, k, v, C_WINDOW - 1, sinks)
    return o.transpose(0, 2, 1, 3).reshape(Bsz, L, C_Q_WIDTH) @ w_out + b_out


def memory_cross_attention(h, mem_n, wq, wkv, wo):
    Bsz, L, _ = h.shape
    M = mem_n.shape[1]
    q = (h @ wq).reshape(Bsz, L, XA_HEADS, XA_HEAD_DIM)
    k, v = jnp.split(mem_n @ wkv, 2, axis=-1)
    k = k.reshape(Bsz, M, XA_HEADS, XA_HEAD_DIM)
    v = v.reshape(Bsz, M, XA_HEADS, XA_HEAD_DIM)
    s = jnp.einsum('blhd,bmhd->bhlm', q, k).astype(jnp.float32) * (XA_HEAD_DIM ** -0.5)
    p = jax.nn.softmax(s, axis=-1)
    o = jnp.einsum('bhlm,bmhd->blhd', p.astype(v.dtype), v).reshape(Bsz, L, XA_WIDTH)
    return o @ wo


def swiglu(h, w_gate_up, w_down):
    gate, up = jnp.split(h @ w_gate_up, 2, axis=-1)
    return (jax.nn.silu(gate) * up) @ w_down


def _fwd_setup_inputs(seed: int = 0) -> dict:
    key = jax.random.key(seed)
    ks = jax.random.split(key, 32)
    f32 = jnp.float32

    def nrm(k, shape, fan_in):
        return jax.random.normal(k, shape, f32) * (fan_in ** -0.5)

    def gain(k, shape):
        return 1.0 + 0.02 * jax.random.normal(k, shape, f32)

    def small(k, shape):
        return 0.01 * jax.random.normal(k, shape, f32)

    a_c = jax.random.uniform(ks[8], (N_EVEN, LRU_WIDTH), f32, 0.9, 0.999)
    a0 = a_c ** (1.0 / LRU_C)
    lru_lambda = jnp.log(a0) - jnp.log1p(-a0)

    return {
        "x": jax.random.normal(ks[0], (BATCH, SEQ, D_MODEL), f32),
        "mem": jax.random.normal(ks[1], (BATCH, N_MEM, D_MODEL), f32),
        "mix_norm": gain(ks[2], (DEPTH, D_MODEL)),
        "ab_w_in": nrm(ks[3], (N_EVEN, D_MODEL, AB_IN), D_MODEL),
        "lru_conv_w": nrm(ks[4], (N_EVEN, CONV_WIDTH, LRU_WIDTH), CONV_WIDTH),
        "lru_conv_b": small(ks[5], (N_EVEN, LRU_WIDTH)),
        "lru_wa": nrm(ks[6], (N_EVEN, LRU_HEADS, LRU_HEAD_DIM, LRU_HEAD_DIM), LRU_HEAD_DIM),
        "lru_ba": small(ks[7], (N_EVEN, LRU_HEADS, LRU_HEAD_DIM)),
        "lru_wx": nrm(ks[9], (N_EVEN, LRU_HEADS, LRU_HEAD_DIM, LRU_HEAD_DIM), LRU_HEAD_DIM),
        "lru_bx": small(ks[10], (N_EVEN, LRU_HEADS, LRU_HEAD_DIM)),
        "lru_lambda": lru_lambda,
        "ab_w_out": nrm(ks[11], (N_EVEN, AB_OUT, D_MODEL), AB_OUT),
        "c_w_qkv": nrm(ks[12], (N_ODD, D_MODEL, C_QKV), D_MODEL),
        "c_b_qkv": small(ks[13], (N_ODD, C_QKV)),
        "c_sinks": 0.5 * jax.random.normal(ks[14], (N_ODD, C_HEADS), f32),
        "c_w_out": nrm(ks[15], (N_ODD, C_Q_WIDTH, D_MODEL), C_Q_WIDTH),
        "c_b_out": small(ks[16], (N_ODD, D_MODEL)),
        "xa_norm": gain(ks[17], (DEPTH, D_MODEL)),
        "xa_mem_norm": gain(ks[18], (DEPTH, D_MODEL)),
        "xa_wq": nrm(ks[19], (DEPTH, D_MODEL, XA_WIDTH), D_MODEL),
        "xa_wkv": nrm(ks[20], (DEPTH, D_MODEL, 2 * XA_WIDTH), D_MODEL),
        "xa_wo": nrm(ks[21], (DEPTH, XA_WIDTH, D_MODEL), XA_WIDTH),
        "ffn_norm": gain(ks[22], (DEPTH, D_MODEL)),
        "ffn_w_gate_up": nrm(ks[23], (DEPTH, D_MODEL, 2 * D_FF), D_MODEL),
        "ffn_w_down": nrm(ks[24], (DEPTH, D_FF, D_MODEL), D_FF),
        "final_norm": gain(ks[25], (D_MODEL,)),
    }


def _fwd_reference(x, mem, mix_norm, ab_w_in, lru_conv_w, lru_conv_b, lru_wa, lru_ba, lru_wx, lru_bx,
              lru_lambda, ab_w_out, c_w_qkv, c_b_qkv, c_sinks, c_w_out, c_b_out, xa_norm,
              xa_mem_norm, xa_wq, xa_wkv, xa_wo, ffn_norm, ffn_w_gate_up, ffn_w_down, final_norm):
    cos, sin = rope_tables(x.shape[1])
    h = x
    for layer in range(DEPTH):
        j = layer // 2
        hn = rms_norm(h, mix_norm[layer])
        if layer % 2 == 0:
            h = h + lru_dilated_mixer(hn, cos, sin, ab_w_in[j], lru_conv_w[j], lru_conv_b[j],
                                      lru_wa[j], lru_ba[j], lru_wx[j], lru_bx[j], lru_lambda[j],
                                      ab_w_out[j])
        else:
            h = h + swa_sink_mixer(hn, cos, sin, c_w_qkv[j], c_b_qkv[j], c_sinks[j],
                                   c_w_out[j], c_b_out[j])
        h = h + memory_cross_attention(rms_norm(h, xa_norm[layer]), rms_norm(mem, xa_mem_norm[layer]),
                                       xa_wq[layer], xa_wkv[layer], xa_wo[layer])
        h = h + swiglu(rms_norm(h, ffn_norm[layer]), ffn_w_gate_up[layer], ffn_w_down[layer])
    return rms_norm(h, final_norm)


import jax as _jax
import jax.numpy as _jnp

TWIN_FORMAT = 'train_step'
FWD_PARAMS = ['x', 'mem', 'mix_norm', 'ab_w_in', 'lru_conv_w', 'lru_conv_b', 'lru_wa', 'lru_ba', 'lru_wx', 'lru_bx', 'lru_lambda', 'ab_w_out', 'c_w_qkv', 'c_b_qkv', 'c_sinks', 'c_w_out', 'c_b_out', 'xa_norm', 'xa_mem_norm', 'xa_wq', 'xa_wkv', 'xa_wo', 'ffn_norm', 'ffn_w_gate_up', 'ffn_w_down', 'final_norm']
TWIN_WEIGHTS = ['mix_norm', 'ab_w_in', 'lru_conv_w', 'lru_conv_b', 'lru_wa', 'lru_ba', 'lru_wx', 'lru_bx', 'lru_lambda', 'ab_w_out', 'c_w_qkv', 'c_b_qkv', 'c_sinks', 'c_w_out', 'c_b_out', 'xa_norm', 'xa_mem_norm', 'xa_wq', 'xa_wkv', 'xa_wo', 'ffn_norm', 'ffn_w_gate_up', 'ffn_w_down', 'final_norm']
TWIN_DIFF_INPUT = 'x'
TWIN_INPUTS = ['x', 'mem', 'mix_norm', 'ab_w_in', 'lru_conv_w', 'lru_conv_b', 'lru_wa', 'lru_ba', 'lru_wx', 'lru_bx', 'lru_lambda', 'ab_w_out', 'c_w_qkv', 'c_b_qkv', 'c_sinks', 'c_w_out', 'c_b_out', 'xa_norm', 'xa_mem_norm', 'xa_wq', 'xa_wkv', 'xa_wo', 'ffn_norm', 'ffn_w_gate_up', 'ffn_w_down', 'final_norm', 'loss_target', 'm_mix_norm', 'm_ab_w_in', 'm_lru_conv_w', 'm_lru_conv_b', 'm_lru_wa', 'm_lru_ba', 'm_lru_wx', 'm_lru_bx', 'm_lru_lambda', 'm_ab_w_out', 'm_c_w_qkv', 'm_c_b_qkv', 'm_c_sinks', 'm_c_w_out', 'm_c_b_out', 'm_xa_norm', 'm_xa_mem_norm', 'm_xa_wq', 'm_xa_wkv', 'm_xa_wo', 'm_ffn_norm', 'm_ffn_w_gate_up', 'm_ffn_w_down', 'm_final_norm', 'v_mix_norm', 'v_ab_w_in', 'v_lru_conv_w', 'v_lru_conv_b', 'v_lru_wa', 'v_lru_ba', 'v_lru_wx', 'v_lru_bx', 'v_lru_lambda', 'v_ab_w_out', 'v_c_w_qkv', 'v_c_b_qkv', 'v_c_sinks', 'v_c_w_out', 'v_c_b_out', 'v_xa_norm', 'v_xa_mem_norm', 'v_xa_wq', 'v_xa_wkv', 'v_xa_wo', 'v_ffn_norm', 'v_ffn_w_gate_up', 'v_ffn_w_down', 'v_final_norm']
TWIN_OUTPUTS = ['loss', 'grad_x', 'grad_mix_norm', 'grad_ab_w_in', 'grad_lru_conv_w', 'grad_lru_conv_b', 'grad_lru_wa', 'grad_lru_ba', 'grad_lru_wx', 'grad_lru_bx', 'grad_lru_lambda', 'grad_ab_w_out', 'grad_c_w_qkv', 'grad_c_b_qkv', 'grad_c_sinks', 'grad_c_w_out', 'grad_c_b_out', 'grad_xa_norm', 'grad_xa_mem_norm', 'grad_xa_wq', 'grad_xa_wkv', 'grad_xa_wo', 'grad_ffn_norm', 'grad_ffn_w_gate_up', 'grad_ffn_w_down', 'grad_final_norm', 'delta_mix_norm', 'delta_ab_w_in', 'delta_lru_conv_w', 'delta_lru_conv_b', 'delta_lru_wa', 'delta_lru_ba', 'delta_lru_wx', 'delta_lru_bx', 'delta_lru_lambda', 'delta_ab_w_out', 'delta_c_w_qkv', 'delta_c_b_qkv', 'delta_c_sinks', 'delta_c_w_out', 'delta_c_b_out', 'delta_xa_norm', 'delta_xa_mem_norm', 'delta_xa_wq', 'delta_xa_wkv', 'delta_xa_wo', 'delta_ffn_norm', 'delta_ffn_w_gate_up', 'delta_ffn_w_down', 'delta_final_norm', 'new_m_mix_norm', 'new_m_ab_w_in', 'new_m_lru_conv_w', 'new_m_lru_conv_b', 'new_m_lru_wa', 'new_m_lru_ba', 'new_m_lru_wx', 'new_m_lru_bx', 'new_m_lru_lambda', 'new_m_ab_w_out', 'new_m_c_w_qkv', 'new_m_c_b_qkv', 'new_m_c_sinks', 'new_m_c_w_out', 'new_m_c_b_out', 'new_m_xa_norm', 'new_m_xa_mem_norm', 'new_m_xa_wq', 'new_m_xa_wkv', 'new_m_xa_wo', 'new_m_ffn_norm', 'new_m_ffn_w_gate_up', 'new_m_ffn_w_down', 'new_m_final_norm', 'new_v_mix_norm', 'new_v_ab_w_in', 'new_v_lru_conv_w', 'new_v_lru_conv_b', 'new_v_lru_wa', 'new_v_lru_ba', 'new_v_lru_wx', 'new_v_lru_bx', 'new_v_lru_lambda', 'new_v_ab_w_out', 'new_v_c_w_qkv', 'new_v_c_b_qkv', 'new_v_c_sinks', 'new_v_c_w_out', 'new_v_c_b_out', 'new_v_xa_norm', 'new_v_xa_mem_norm', 'new_v_xa_wq', 'new_v_xa_wkv', 'new_v_xa_wo', 'new_v_ffn_norm', 'new_v_ffn_w_gate_up', 'new_v_ffn_w_down', 'new_v_final_norm']
TWIN_LEAF_KINDS = {'loss': 'loss', 'grad_x': 'grad_x', 'grad_mix_norm': 'grad_w', 'grad_ab_w_in': 'grad_w', 'grad_lru_conv_w': 'grad_w', 'grad_lru_conv_b': 'grad_w', 'grad_lru_wa': 'grad_w', 'grad_lru_ba': 'grad_w', 'grad_lru_wx': 'grad_w', 'grad_lru_bx': 'grad_w', 'grad_lru_lambda': 'grad_w', 'grad_ab_w_out': 'grad_w', 'grad_c_w_qkv': 'grad_w', 'grad_c_b_qkv': 'grad_w', 'grad_c_sinks': 'grad_w', 'grad_c_w_out': 'grad_w', 'grad_c_b_out': 'grad_w', 'grad_xa_norm': 'grad_w', 'grad_xa_mem_norm': 'grad_w', 'grad_xa_wq': 'grad_w', 'grad_xa_wkv': 'grad_w', 'grad_xa_wo': 'grad_w', 'grad_ffn_norm': 'grad_w', 'grad_ffn_w_gate_up': 'grad_w', 'grad_ffn_w_down': 'grad_w', 'grad_final_norm': 'grad_w', 'delta_mix_norm': 'delta_w', 'delta_ab_w_in': 'delta_w', 'delta_lru_conv_w': 'delta_w', 'delta_lru_conv_b': 'delta_w', 'delta_lru_wa': 'delta_w', 'delta_lru_ba': 'delta_w', 'delta_lru_wx': 'delta_w', 'delta_lru_bx': 'delta_w', 'delta_lru_lambda': 'delta_w', 'delta_ab_w_out': 'delta_w', 'delta_c_w_qkv': 'delta_w', 'delta_c_b_qkv': 'delta_w', 'delta_c_sinks': 'delta_w', 'delta_c_w_out': 'delta_w', 'delta_c_b_out': 'delta_w', 'delta_xa_norm': 'delta_w', 'delta_xa_mem_norm': 'delta_w', 'delta_xa_wq': 'delta_w', 'delta_xa_wkv': 'delta_w', 'delta_xa_wo': 'delta_w', 'delta_ffn_norm': 'delta_w', 'delta_ffn_w_gate_up': 'delta_w', 'delta_ffn_w_down': 'delta_w', 'delta_final_norm': 'delta_w', 'new_m_mix_norm': 'new_m', 'new_m_ab_w_in': 'new_m', 'new_m_lru_conv_w': 'new_m', 'new_m_lru_conv_b': 'new_m', 'new_m_lru_wa': 'new_m', 'new_m_lru_ba': 'new_m', 'new_m_lru_wx': 'new_m', 'new_m_lru_bx': 'new_m', 'new_m_lru_lambda': 'new_m', 'new_m_ab_w_out': 'new_m', 'new_m_c_w_qkv': 'new_m', 'new_m_c_b_qkv': 'new_m', 'new_m_c_sinks': 'new_m', 'new_m_c_w_out': 'new_m', 'new_m_c_b_out': 'new_m', 'new_m_xa_norm': 'new_m', 'new_m_xa_mem_norm': 'new_m', 'new_m_xa_wq': 'new_m', 'new_m_xa_wkv': 'new_m', 'new_m_xa_wo': 'new_m', 'new_m_ffn_norm': 'new_m', 'new_m_ffn_w_gate_up': 'new_m', 'new_m_ffn_w_down': 'new_m', 'new_m_final_norm': 'new_m', 'new_v_mix_norm': 'new_v', 'new_v_ab_w_in': 'new_v', 'new_v_lru_conv_w': 'new_v', 'new_v_lru_conv_b': 'new_v', 'new_v_lru_wa': 'new_v', 'new_v_lru_ba': 'new_v', 'new_v_lru_wx': 'new_v', 'new_v_lru_bx': 'new_v', 'new_v_lru_lambda': 'new_v', 'new_v_ab_w_out': 'new_v', 'new_v_c_w_qkv': 'new_v', 'new_v_c_b_qkv': 'new_v', 'new_v_c_sinks': 'new_v', 'new_v_c_w_out': 'new_v', 'new_v_c_b_out': 'new_v', 'new_v_xa_norm': 'new_v', 'new_v_xa_mem_norm': 'new_v', 'new_v_xa_wq': 'new_v', 'new_v_xa_wkv': 'new_v', 'new_v_xa_wo': 'new_v', 'new_v_ffn_norm': 'new_v', 'new_v_ffn_w_gate_up': 'new_v', 'new_v_ffn_w_down': 'new_v', 'new_v_final_norm': 'new_v'}


def _forward(args):
    return _fwd_reference(*[args[k] for k in FWD_PARAMS])


def _output_shape():
    def fwd():
        inp = _fwd_setup_inputs(0)
        return _fwd_reference(*[inp[k] for k in FWD_PARAMS])
    out = _jax.eval_shape(fwd)
    return out.shape, out.dtype

N_MICROBATCH = 1
ADAM_LR = 0.001
ADAM_B1 = 0.9
ADAM_B2 = 0.999
ADAM_EPS = 1e-08
ADAM_WD = 0.01
ADAM_STEP = 10
PER_EXAMPLE_BATCH_AXIS = {'x': 0, 'mem': 0, 'loss_target': 0}
SHARED_INPUTS = []
_WEIGHT_DTYPES = {'mix_norm': _jnp.float32, 'ab_w_in': _jnp.float32, 'lru_conv_w': _jnp.float32, 'lru_conv_b': _jnp.float32, 'lru_wa': _jnp.float32, 'lru_ba': _jnp.float32, 'lru_wx': _jnp.float32, 'lru_bx': _jnp.float32, 'lru_lambda': _jnp.float32, 'ab_w_out': _jnp.float32, 'c_w_qkv': _jnp.float32, 'c_b_qkv': _jnp.float32, 'c_sinks': _jnp.float32, 'c_w_out': _jnp.float32, 'c_b_out': _jnp.float32, 'xa_norm': _jnp.float32, 'xa_mem_norm': _jnp.float32, 'xa_wq': _jnp.float32, 'xa_wkv': _jnp.float32, 'xa_wo': _jnp.float32, 'ffn_norm': _jnp.float32, 'ffn_w_gate_up': _jnp.float32, 'ffn_w_down': _jnp.float32, 'final_norm': _jnp.float32}
MOMENT_SCALE = {'mix_norm': 7.971387e-02, 'ab_w_in': 5.290144e-02, 'lru_conv_w': 7.692543e-02, 'lru_conv_b': 8.312300e-01, 'lru_wa': 2.118709e-02, 'lru_ba': 1.684379e-02, 'lru_wx': 3.751620e-02, 'lru_bx': 2.694011e-02, 'lru_lambda': 3.118254e-02, 'ab_w_out': 6.799515e-02, 'c_w_qkv': 4.759507e-02, 'c_b_qkv': 1.569676e-01, 'c_sinks': 2.348156e-02, 'c_w_out': 3.592483e-02, 'c_b_out': 1.654158e-01, 'xa_norm': 2.017192e-02, 'xa_mem_norm': 3.025784e-02, 'xa_wq': 2.748827e-02, 'xa_wkv': 2.849555e-02, 'xa_wo': 2.060179e-02, 'ffn_norm': 1.296228e-01, 'ffn_w_gate_up': 5.377942e-02, 'ffn_w_down': 8.754458e-02, 'final_norm': 3.200791e+01}


def _to_microbatches(a, axis):
    t = _jnp.moveaxis(a, axis, 0)
    t = t.reshape((N_MICROBATCH, t.shape[0] // N_MICROBATCH) + t.shape[1:])
    return _jnp.moveaxis(t, 1, axis + 1)


def setup_inputs(seed: int = 0) -> dict:
    inp = _fwd_setup_inputs(seed)
    key = _jax.random.fold_in(_jax.random.key(seed), 7919)
    shape, _ = _output_shape()
    out = dict(inp)
    out["loss_target"] = _jax.random.normal(_jax.random.fold_in(key, 0), shape, _jnp.float32)
    for i, name in enumerate(TWIN_WEIGHTS):
        w = inp[name].astype(_jnp.float32)
        if MOMENT_SCALE is None:
            s = _jnp.sqrt(_jnp.mean(_jnp.square(w)) + 1e-30)
        else:
            s = MOMENT_SCALE[name]
        km, kv = _jax.random.split(_jax.random.fold_in(key, i + 1))
        out[name] = w
        out["m_" + name] = s * _jax.random.normal(km, w.shape, _jnp.float32)
        out["v_" + name] = (s * s) * _jax.random.uniform(kv, w.shape, _jnp.float32, 0.5, 1.5)
    if N_MICROBATCH > 1:
        for name, axis in PER_EXAMPLE_BATCH_AXIS.items():
            out[name] = _to_microbatches(out[name], axis)
    return {'x': out['x'], 'mem': out['mem'], 'mix_norm': out['mix_norm'], 'ab_w_in': out['ab_w_in'], 'lru_conv_w': out['lru_conv_w'], 'lru_conv_b': out['lru_conv_b'], 'lru_wa': out['lru_wa'], 'lru_ba': out['lru_ba'], 'lru_wx': out['lru_wx'], 'lru_bx': out['lru_bx'], 'lru_lambda': out['lru_lambda'], 'ab_w_out': out['ab_w_out'], 'c_w_qkv': out['c_w_qkv'], 'c_b_qkv': out['c_b_qkv'], 'c_sinks': out['c_sinks'], 'c_w_out': out['c_w_out'], 'c_b_out': out['c_b_out'], 'xa_norm': out['xa_norm'], 'xa_mem_norm': out['xa_mem_norm'], 'xa_wq': out['xa_wq'], 'xa_wkv': out['xa_wkv'], 'xa_wo': out['xa_wo'], 'ffn_norm': out['ffn_norm'], 'ffn_w_gate_up': out['ffn_w_gate_up'], 'ffn_w_down': out['ffn_w_down'], 'final_norm': out['final_norm'], 'loss_target': out['loss_target'], 'm_mix_norm': out['m_mix_norm'], 'm_ab_w_in': out['m_ab_w_in'], 'm_lru_conv_w': out['m_lru_conv_w'], 'm_lru_conv_b': out['m_lru_conv_b'], 'm_lru_wa': out['m_lru_wa'], 'm_lru_ba': out['m_lru_ba'], 'm_lru_wx': out['m_lru_wx'], 'm_lru_bx': out['m_lru_bx'], 'm_lru_lambda': out['m_lru_lambda'], 'm_ab_w_out': out['m_ab_w_out'], 'm_c_w_qkv': out['m_c_w_qkv'], 'm_c_b_qkv': out['m_c_b_qkv'], 'm_c_sinks': out['m_c_sinks'], 'm_c_w_out': out['m_c_w_out'], 'm_c_b_out': out['m_c_b_out'], 'm_xa_norm': out['m_xa_norm'], 'm_xa_mem_norm': out['m_xa_mem_norm'], 'm_xa_wq': out['m_xa_wq'], 'm_xa_wkv': out['m_xa_wkv'], 'm_xa_wo': out['m_xa_wo'], 'm_ffn_norm': out['m_ffn_norm'], 'm_ffn_w_gate_up': out['m_ffn_w_gate_up'], 'm_ffn_w_down': out['m_ffn_w_down'], 'm_final_norm': out['m_final_norm'], 'v_mix_norm': out['v_mix_norm'], 'v_ab_w_in': out['v_ab_w_in'], 'v_lru_conv_w': out['v_lru_conv_w'], 'v_lru_conv_b': out['v_lru_conv_b'], 'v_lru_wa': out['v_lru_wa'], 'v_lru_ba': out['v_lru_ba'], 'v_lru_wx': out['v_lru_wx'], 'v_lru_bx': out['v_lru_bx'], 'v_lru_lambda': out['v_lru_lambda'], 'v_ab_w_out': out['v_ab_w_out'], 'v_c_w_qkv': out['v_c_w_qkv'], 'v_c_b_qkv': out['v_c_b_qkv'], 'v_c_sinks': out['v_c_sinks'], 'v_c_w_out': out['v_c_w_out'], 'v_c_b_out': out['v_c_b_out'], 'v_xa_norm': out['v_xa_norm'], 'v_xa_mem_norm': out['v_xa_mem_norm'], 'v_xa_wq': out['v_xa_wq'], 'v_xa_wkv': out['v_xa_wkv'], 'v_xa_wo': out['v_xa_wo'], 'v_ffn_norm': out['v_ffn_norm'], 'v_ffn_w_gate_up': out['v_ffn_w_gate_up'], 'v_ffn_w_down': out['v_ffn_w_down'], 'v_final_norm': out['v_final_norm']}


def _loss(weights, diff, rest, loss_target):
    with _jax.named_scope("forward"):
        args = {**rest, TWIN_DIFF_INPUT: diff, **{k: w.astype(_WEIGHT_DTYPES[k]) for k, w in weights.items()}}
        y = _forward(args)
    with _jax.named_scope("loss_head"):
        err = _jnp.square(y.astype(_jnp.float32) - loss_target)
        return 0.5 * _jnp.sum(_jnp.mean(err, axis=-1)) if err.ndim else 0.5 * err


def _adamw(w, g, m, v):
    m = ADAM_B1 * m + (1.0 - ADAM_B1) * g
    v = ADAM_B2 * v + (1.0 - ADAM_B2) * _jnp.square(g)
    m_hat = m / (1.0 - ADAM_B1 ** ADAM_STEP)
    v_hat = v / (1.0 - ADAM_B2 ** ADAM_STEP)
    delta = -ADAM_LR * (m_hat / (_jnp.sqrt(v_hat) + ADAM_EPS) + ADAM_WD * w)
    return delta, m, v


def reference(x, mem, mix_norm, ab_w_in, lru_conv_w, lru_conv_b, lru_wa, lru_ba, lru_wx, lru_bx, lru_lambda, ab_w_out, c_w_qkv, c_b_qkv, c_sinks, c_w_out, c_b_out, xa_norm, xa_mem_norm, xa_wq, xa_wkv, xa_wo, ffn_norm, ffn_w_gate_up, ffn_w_down, final_norm, loss_target, m_mix_norm, m_ab_w_in, m_lru_conv_w, m_lru_conv_b, m_lru_wa, m_lru_ba, m_lru_wx, m_lru_bx, m_lru_lambda, m_ab_w_out, m_c_w_qkv, m_c_b_qkv, m_c_sinks, m_c_w_out, m_c_b_out, m_xa_norm, m_xa_mem_norm, m_xa_wq, m_xa_wkv, m_xa_wo, m_ffn_norm, m_ffn_w_gate_up, m_ffn_w_down, m_final_norm, v_mix_norm, v_ab_w_in, v_lru_conv_w, v_lru_conv_b, v_lru_wa, v_lru_ba, v_lru_wx, v_lru_bx, v_lru_lambda, v_ab_w_out, v_c_w_qkv, v_c_b_qkv, v_c_sinks, v_c_w_out, v_c_b_out, v_xa_norm, v_xa_mem_norm, v_xa_wq, v_xa_wkv, v_xa_wo, v_ffn_norm, v_ffn_w_gate_up, v_ffn_w_down, v_final_norm):
    given = dict(x=x, mem=mem, mix_norm=mix_norm, ab_w_in=ab_w_in, lru_conv_w=lru_conv_w, lru_conv_b=lru_conv_b, lru_wa=lru_wa, lru_ba=lru_ba, lru_wx=lru_wx, lru_bx=lru_bx, lru_lambda=lru_lambda, ab_w_out=ab_w_out, c_w_qkv=c_w_qkv, c_b_qkv=c_b_qkv, c_sinks=c_sinks, c_w_out=c_w_out, c_b_out=c_b_out, xa_norm=xa_norm, xa_mem_norm=xa_mem_norm, xa_wq=xa_wq, xa_wkv=xa_wkv, xa_wo=xa_wo, ffn_norm=ffn_norm, ffn_w_gate_up=ffn_w_gate_up, ffn_w_down=ffn_w_down, final_norm=final_norm, loss_target=loss_target, m_mix_norm=m_mix_norm, m_ab_w_in=m_ab_w_in, m_lru_conv_w=m_lru_conv_w, m_lru_conv_b=m_lru_conv_b, m_lru_wa=m_lru_wa, m_lru_ba=m_lru_ba, m_lru_wx=m_lru_wx, m_lru_bx=m_lru_bx, m_lru_lambda=m_lru_lambda, m_ab_w_out=m_ab_w_out, m_c_w_qkv=m_c_w_qkv, m_c_b_qkv=m_c_b_qkv, m_c_sinks=m_c_sinks, m_c_w_out=m_c_w_out, m_c_b_out=m_c_b_out, m_xa_norm=m_xa_norm, m_xa_mem_norm=m_xa_mem_norm, m_xa_wq=m_xa_wq, m_xa_wkv=m_xa_wkv, m_xa_wo=m_xa_wo, m_ffn_norm=m_ffn_norm, m_ffn_w_gate_up=m_ffn_w_gate_up, m_ffn_w_down=m_ffn_w_down, m_final_norm=m_final_norm, v_mix_norm=v_mix_norm, v_ab_w_in=v_ab_w_in, v_lru_conv_w=v_lru_conv_w, v_lru_conv_b=v_lru_conv_b, v_lru_wa=v_lru_wa, v_lru_ba=v_lru_ba, v_lru_wx=v_lru_wx, v_lru_bx=v_lru_bx, v_lru_lambda=v_lru_lambda, v_ab_w_out=v_ab_w_out, v_c_w_qkv=v_c_w_qkv, v_c_b_qkv=v_c_b_qkv, v_c_sinks=v_c_sinks, v_c_w_out=v_c_w_out, v_c_b_out=v_c_b_out, v_xa_norm=v_xa_norm, v_xa_mem_norm=v_xa_mem_norm, v_xa_wq=v_xa_wq, v_xa_wkv=v_xa_wkv, v_xa_wo=v_xa_wo, v_ffn_norm=v_ffn_norm, v_ffn_w_gate_up=v_ffn_w_gate_up, v_ffn_w_down=v_ffn_w_down, v_final_norm=v_final_norm)
    weights = {n: given[n] for n in TWIN_WEIGHTS}
    shared = {n: given[n] for n in SHARED_INPUTS}
    per_example = {n: given[n] for n in ['x', 'mem']}
    grad_fn = _jax.value_and_grad(_loss, argnums=(0, 1))

    def one_microbatch(ex, loss_target):
        ex = dict(ex)
        diff = ex.pop(TWIN_DIFF_INPUT)
        return grad_fn(weights, diff, {**shared, **ex}, loss_target)

    if N_MICROBATCH == 1:
        loss, (grad_w, grad_x) = one_microbatch(per_example, given["loss_target"])
    else:
        def body(carry, xs):
            loss_sum, grad_sum = carry
            l_k, (gw_k, gx_k) = one_microbatch(xs[0], xs[1])
            with _jax.named_scope("update"):
                return (loss_sum + l_k, _jax.tree.map(_jnp.add, grad_sum, gw_k)), gx_k

        init = (_jnp.zeros((), _jnp.float32), _jax.tree.map(_jnp.zeros_like, weights))
        (loss, grad_w), grad_x = _jax.lax.scan(body, init, (per_example, given["loss_target"]))
    with _jax.named_scope("update"):
        delta_w, new_m, new_v = {}, {}, {}
        for n in TWIN_WEIGHTS:
            delta_w[n], new_m[n], new_v[n] = _adamw(weights[n], grad_w[n], given["m_" + n], given["v_" + n])
    return (loss, grad_x, *[grad_w[n] for n in TWIN_WEIGHTS], *[delta_w[n] for n in TWIN_WEIGHTS],
            *[new_m[n] for n in TWIN_WEIGHTS], *[new_v[n] for n in TWIN_WEIGHTS])
```

```python
import jax
import jax.numpy as jnp
from jax import lax
from jax.experimental import pallas as pl
from jax.experimental.pallas import tpu as pltpu

F32, BF16 = jnp.float32, jnp.bfloat16
D_MODEL = 1024
NORM_EPS = 1e-6
ROPE_THETA = 500000.0
HEAD_DIM = 64
ROT_DIM = 16
BLK = 128
LRU_HEADS, LRU_HEAD_DIM, CONV_WIDTH, LRU_C = 4, 256, 4, 8.0
DILATED_PATTERN = ((128, 1), (512, 4), (2048, 16))
B_HEADS, C_HEADS, C_KV_HEADS, C_WINDOW = 8, 16, 2, 128
XA_HEADS, XA_HEAD_DIM, N_MEM = 4, 128, 256
D_FF = 2816
NEG = -1e30
ADAM_LR, ADAM_B1, ADAM_B2, ADAM_EPS, ADAM_WD, ADAM_STEP = 0.001, 0.9, 0.999, 1e-08, 0.01, 10
N_CHIPS = 4
PACK_ROWS = 7296
VMEM_LIMIT_V7X = 56 * 1024 * 1024

NN = (((1,), (0,)), ((), ()))
NT = (((1,), (1,)), ((), ()))
TN = (((0,), (0,)), ((), ()))


def _dot(a, b, dims=NN):
    return lax.dot_general(a, b, dims, preferred_element_type=F32)


def _sigmoid(x):
    return 1.0 / (1.0 + jnp.exp(-x))


def _call(body, *, name, grid, in_specs, out_specs, out_shape, scratch=(), sem=None):
    return pl.pallas_call(
        body, name=name, grid=grid, in_specs=in_specs, out_specs=out_specs, out_shape=out_shape,
        scratch_shapes=list(scratch),
        compiler_params=pltpu.CompilerParams(dimension_semantics=sem, vmem_limit_bytes=VMEM_LIMIT_V7X))


def _rope_tables(L):
    half = ROT_DIM // 2
    inv = ROPE_THETA ** (-jnp.arange(0, ROT_DIM, 2, dtype=F32) / ROT_DIM)
    ang = jnp.arange(L, dtype=F32)[:, None] * inv[None, :]
    cos, sin = jnp.cos(ang), jnp.sin(ang)
    rest = HEAD_DIM - ROT_DIM
    z8, zr, one = jnp.zeros((L, half), F32), jnp.zeros((L, rest), F32), jnp.ones((L, rest), F32)
    c = jnp.concatenate([cos, cos, one], axis=1)
    s1 = jnp.concatenate([-sin, z8, zr], axis=1)
    s2 = jnp.concatenate([z8, sin, zr], axis=1)
    return tuple(jnp.concatenate([t, t], axis=1) for t in (c, s1, s2))


def _rope_fwd(v, c, s1, s2):
    return v * c + pltpu.roll(v, 120, 1) * s1 + pltpu.roll(v, 8, 1) * s2


def _rope_bwd(dv, c, s1, s2):
    return dv * c + pltpu.roll(dv * s1, 8, 1) + pltpu.roll(dv * s2, 120, 1)


def _rowmm(a, w3, *, name, tm=256, gain=None, bias=None, res=None, swiglu=False, rope=None):
    M, K = a.shape
    S, _, Ns = w3.shape
    N = S * Ns
    tm = min(tm, M)
    has_norm, has_bias, has_res, has_rope = gain is not None, bias is not None, res is not None, rope is not None
    row = lambda w: pl.BlockSpec((tm, w), lambda i: (i, 0))
    whole = lambda shape: pl.BlockSpec(shape, lambda i: (0,) * len(shape))
    ins, specs = [a], [row(K)]
    if has_norm:
        ins.append(gain.reshape(1, K)); specs.append(whole((1, K)))
    ins.append(w3); specs.append(whole((S, K, Ns)))
    if has_bias:
        ins.append(bias.reshape(1, N)); specs.append(whole((1, N)))
    if has_res:
        ins.append(res); specs.append(row(N))
    if has_rope:
        ins += list(rope[2]); specs += [row(128)] * 3
    y_dtype = F32 if has_res else BF16
    out_shape, out_specs = [jax.ShapeDtypeStruct((M, N), y_dtype)], [row(N)]
    if has_norm:
        out_shape.append(jax.ShapeDtypeStruct((M, K), BF16)); out_specs.append(row(K))
    if swiglu:
        out_shape.append(jax.ShapeDtypeStruct((M, N // 2), BF16)); out_specs.append(row(N // 2))
    scratch = [pltpu.VMEM((tm, N), F32)] if has_rope else []

    def body(*refs):
        it = iter(refs)
        a_ref = next(it)
        g_ref = next(it) if has_norm else None
        w_ref = next(it)
        b_ref = next(it) if has_bias else None
        r_ref = next(it) if has_res else None
        tabs = [next(it) for _ in range(3)] if has_rope else None
        y_ref = next(it)
        n_ref = next(it) if has_norm else None
        act_ref = next(it) if swiglu else None
        ys_ref = next(it) if has_rope else None
        if has_norm:
            x = a_ref[...].astype(F32)
            ms = jnp.mean(x * x, axis=-1, keepdims=True)
            xb = (x * lax.rsqrt(ms + NORM_EPS) * g_ref[...]).astype(BF16)
            n_ref[...] = xb
        else:
            xb = a_ref[...].astype(BF16)
        if swiglu:
            for s in range(S // 2):
                g = _dot(xb, w_ref[s])
                u = _dot(xb, w_ref[s + S // 2])
                y_ref[:, s * Ns:(s + 1) * Ns] = g.astype(BF16)
                y_ref[:, N // 2 + s * Ns:N // 2 + (s + 1) * Ns] = u.astype(BF16)
                act_ref[:, s * Ns:(s + 1) * Ns] = (g * _sigmoid(g) * u).astype(BF16)
            return
        for s in range(S):
            sl = slice(s * Ns, (s + 1) * Ns)
            acc = _dot(xb, w_ref[s])
            if has_bias:
                acc = acc + b_ref[:, sl]
            if has_res:
                acc = acc + r_ref[:, sl]
            if has_rope:
                ys_ref[:, sl] = acc
            else:
                y_ref[:, sl] = acc.astype(y_dtype)
        if has_rope:
            c, s1, s2 = (t[...] for t in tabs)
            for cb in range(N // 128):
                cs = slice(cb * 128, (cb + 1) * 128)
                v = ys_ref[:, cs]
                if rope[0] <= cb * 128 < rope[1]:
                    v = _rope_fwd(v, c, s1, s2)
                y_ref[:, cs] = v.astype(BF16)

    return _call(body, name=name, grid=(M // tm,), in_specs=specs, out_specs=out_specs, out_shape=out_shape,
                 scratch=scratch, sem=("parallel",))(*ins)


def _mm_nt(dy, w3, *, name, mode, tm=256, kchunk=None, h=None, gain=None, dh=None, gu=None):
    M, N = dy.shape
    S, K, Ns = w3.shape
    kchunk = kchunk or K
    tm = min(tm, M)
    row = lambda w: pl.BlockSpec((tm, w), lambda i: (i, 0))
    whole = lambda shape: pl.BlockSpec(shape, lambda i: (0,) * len(shape))
    ins, specs = [dy, w3], [row(N), whole((S, K, Ns))]
    has_dh = dh is not None
    if mode == "norm":
        ins += [h, gain.reshape(1, K)]; specs += [row(K), whole((1, K))]
        if has_dh:
            ins.append(dh); specs.append(row(K))
        out_shape = [jax.ShapeDtypeStruct((M, K), F32), jax.ShapeDtypeStruct((1, K), F32)]
        out_specs = [row(K), whole((1, K))]
    elif mode == "swiglu":
        ins.append(gu); specs.append(row(2 * K))
        out_shape, out_specs = [jax.ShapeDtypeStruct((M, 2 * K), BF16)], [row(2 * K)]
    else:
        out_shape, out_specs = [jax.ShapeDtypeStruct((M, K), BF16)], [row(K)]

    def body(*refs):
        it = iter(refs)
        dy_ref, w_ref = next(it), next(it)
        if mode == "norm":
            h_ref, g_ref = next(it), next(it)
            dh_ref = next(it) if has_dh else None
            o_ref, dg_ref = next(it), next(it)
        elif mode == "swiglu":
            gu_ref, o_ref = next(it), next(it)
        else:
            o_ref = next(it)
        for kc in range(K // kchunk):
            ks = slice(kc * kchunk, (kc + 1) * kchunk)
            acc = None
            for s in range(S):
                t = _dot(dy_ref[:, s * Ns:(s + 1) * Ns].astype(BF16), w_ref[s, ks, :], NT)
                acc = t if acc is None else acc + t
            if mode == "plain":
                o_ref[:, ks] = acc.astype(BF16)
            elif mode == "swiglu":
                us = slice(K + kc * kchunk, K + (kc + 1) * kchunk)
                g = gu_ref[:, ks].astype(F32)
                u = gu_ref[:, us].astype(F32)
                sg = _sigmoid(g)
                o_ref[:, ks] = (acc * u * (sg * (1.0 + g * (1.0 - sg)))).astype(BF16)
                o_ref[:, us] = (acc * (g * sg)).astype(BF16)
            else:
                x = h_ref[...].astype(F32)
                r = lax.rsqrt(jnp.mean(x * x, axis=-1, keepdims=True) + NORM_EPS)
                xhat = x * r
                dxh = acc * g_ref[...]
                dx = r * (dxh - xhat * jnp.mean(dxh * xhat, axis=-1, keepdims=True))
                o_ref[...] = dx + dh_ref[...] if has_dh else dx

                @pl.when(pl.program_id(0) == 0)
                def _():
                    dg_ref[...] = jnp.zeros_like(dg_ref)

                dg_ref[...] += jnp.sum(acc * xhat, axis=0, keepdims=True)

    sem = ("arbitrary",) if mode == "norm" else ("parallel",)
    return _call(body, name=name, grid=(M // tm,), in_specs=specs, out_specs=out_specs, out_shape=out_shape, sem=sem)(*ins)


def _mm_tn(x, dy, *, S, name, tk=1024, kk=None, bias=False):
    M, K = x.shape
    N = dy.shape[1]
    Ns = N // S
    kk = kk or K
    tk = min(tk, M)
    nl = M // tk
    in_specs = [pl.BlockSpec((tk, kk), lambda s, kc, l: (l, kc)), pl.BlockSpec((tk, Ns), lambda s, kc, l: (l, s))]
    out_shape = [jax.ShapeDtypeStruct((S, K, Ns), BF16)]
    out_specs = [pl.BlockSpec((None, kk, Ns), lambda s, kc, l: (s, kc, 0))]
    if bias:
        out_shape.append(jax.ShapeDtypeStruct((1, N), F32))
        out_specs.append(pl.BlockSpec((1, Ns), lambda s, kc, l: (0, s)))

    def body(x_ref, dy_ref, o_ref, *rest):
        acc_ref = rest[-1]
        kc, l = pl.program_id(1), pl.program_id(2)

        @pl.when(l == 0)
        def _():
            acc_ref[...] = jnp.zeros_like(acc_ref)

        acc_ref[...] += _dot(x_ref[...].astype(BF16), dy_ref[...].astype(BF16), TN)
        if bias:
            b_ref = rest[0]

            @pl.when((kc == 0) & (l == 0))
            def _():
                b_ref[...] = jnp.zeros_like(b_ref)

            @pl.when(kc == 0)
            def _():
                b_ref[...] += jnp.sum(dy_ref[...].astype(F32), axis=0, keepdims=True)

        @pl.when(l == nl - 1)
        def _():
            o_ref[...] = acc_ref[...].astype(BF16)

    return _call(body, name=name, grid=(S, K // kk, nl), in_specs=in_specs, out_specs=out_specs, out_shape=out_shape,
                 scratch=[pltpu.VMEM((kk, Ns), F32)], sem=("arbitrary", "arbitrary", "arbitrary"))(x, dy)


def _band_masks(max_dist):
    rows = lax.broadcasted_iota(jnp.int32, (BLK, BLK), 0)
    cols = lax.broadcasted_iota(jnp.int32, (BLK, BLK), 1)
    dist = rows - cols
    return (dist >= 0) & (dist <= max_dist), (dist + BLK) <= max_dist


def _band_fwd(qa, ka, va, *, d, nq, nkv, qcol, kcol, vcol, max_dist, state=None, finalize=True, sinks=None, name):
    Lr = qa.shape[0]
    nb = Lr // BLK
    qw, kw, G = nq * HEAD_DIM, nkv * HEAD_DIM, nq // nkv
    scale = HEAD_DIM ** -0.5
    cur = lambda colf, w: pl.BlockSpec((BLK, w), lambda r, i: (i, colf(r)))
    prv = lambda colf, w: pl.BlockSpec((BLK, w), lambda r, i: (jnp.maximum(i - 1, 0), colf(r)))
    st = pl.BlockSpec((BLK, qw), lambda r, i: (i, r))
    ins, specs = [qa, ka, ka, va, va], [cur(qcol, qw), cur(kcol, kw), prv(kcol, kw), cur(vcol, kw), prv(vcol, kw)]
    has_state, has_sinks = state is not None, sinks is not None
    if has_state:
        ins += list(state); specs += [st] * 3
    if has_sinks:
        ins.append(sinks); specs.append(pl.BlockSpec(memory_space=pltpu.SMEM))
    if finalize:
        out_shape = [jax.ShapeDtypeStruct((Lr, d * qw), BF16), jax.ShapeDtypeStruct((Lr, d * qw), F32)]
    else:
        out_shape = [jax.ShapeDtypeStruct((Lr, d * qw), F32)] * 3
    out_specs = [st] * len(out_shape)

    def body(*refs):
        it = iter(refs)
        q_ref, kc_ref, kp_ref, vc_ref, vp_ref = (next(it) for _ in range(5))
        m_in, l_in, a_in = (next(it) for _ in range(3)) if has_state else (None,) * 3
        sk_ref = next(it) if has_sinks else None
        outs = list(it)
        i = pl.program_id(1)
        mask_c, mask_p = _band_masks(max_dist)
        mask_p = mask_p & (i > 0)
        for h in range(nq):
            hs = slice(h * HEAD_DIM, (h + 1) * HEAD_DIM)
            ks = slice((h // G) * HEAD_DIM, (h // G + 1) * HEAD_DIM)
            q = q_ref[:, hs]
            sc = jnp.where(mask_c, _dot(q, kc_ref[:, ks], NT) * scale, NEG)
            sp = jnp.where(mask_p, _dot(q, kp_ref[:, ks], NT) * scale, NEG)
            m = jnp.maximum(jnp.max(sc, axis=-1, keepdims=True), jnp.max(sp, axis=-1, keepdims=True))
            if has_state:
                m_prev = m_in[:, h * HEAD_DIM:h * HEAD_DIM + 1]
                m = jnp.maximum(m, m_prev)
            if has_sinks:
                m = jnp.maximum(m, sk_ref[h])
            pc, pp = jnp.exp(sc - m), jnp.exp(sp - m)
            l = jnp.sum(pc, axis=-1, keepdims=True) + jnp.sum(pp, axis=-1, keepdims=True)
            acc = _dot(pc.astype(BF16), vc_ref[:, ks]) + _dot(pp.astype(BF16), vp_ref[:, ks])
            if has_state:
                alpha = jnp.exp(m_prev - m)
                l = l + alpha * l_in[:, h * HEAD_DIM:h * HEAD_DIM + 1]
                acc = acc + alpha * a_in[:, hs]
            if has_sinks:
                l = l + jnp.exp(sk_ref[h] - m)
            if finalize:
                outs[0][:, hs] = (acc / l).astype(BF16)
                outs[1][:, hs] = jnp.broadcast_to(m + jnp.log(l), (BLK, HEAD_DIM))
            else:
                outs[0][:, hs] = jnp.broadcast_to(m, (BLK, HEAD_DIM))
                outs[1][:, hs] = jnp.broadcast_to(l, (BLK, HEAD_DIM))
                outs[2][:, hs] = acc

    return _call(body, name=name, grid=(d, nb), in_specs=specs, out_specs=out_specs, out_shape=out_shape,
                 sem=("parallel", "parallel"))(*ins)


def _band_bwd(qa, ka, va, doa, oa, lsea, *, d, nq, nkv, qcol, kcol, vcol, docol, max_dist, sinks=None, name):
    Lr = qa.shape[0]
    nb = Lr // BLK
    qw, kw, G = nq * HEAD_DIM, nkv * HEAD_DIM, nq // nkv
    scale = HEAD_DIM ** -0.5
    cur = lambda colf, w: pl.BlockSpec((BLK, w), lambda r, i: (i, colf(r)))
    prv = lambda colf, w: pl.BlockSpec((BLK, w), lambda r, i: (jnp.maximum(i - 1, 0), colf(r)))
    nxt = lambda colf, w: pl.BlockSpec((BLK, w), lambda r, i: (jnp.minimum(i + 1, nb - 1), colf(r)))
    own = lambda r: r
    ins = [qa, qa, ka, ka, va, va, doa, doa, oa, oa, lsea, lsea]
    specs = [cur(qcol, qw), nxt(qcol, qw), cur(kcol, kw), prv(kcol, kw), cur(vcol, kw), prv(vcol, kw),
             cur(docol, qw), nxt(docol, qw), cur(own, qw), nxt(own, qw), cur(own, qw), nxt(own, qw)]
    has_sinks = sinks is not None
    if has_sinks:
        ins.append(sinks); specs.append(pl.BlockSpec(memory_space=pltpu.SMEM))
    out_shape = [jax.ShapeDtypeStruct((Lr, d * qw), F32), jax.ShapeDtypeStruct((Lr, d * kw), F32),
                 jax.ShapeDtypeStruct((Lr, d * kw), F32)]
    out_specs = [pl.BlockSpec((BLK, qw), lambda r, i: (i, r)), pl.BlockSpec((BLK, kw), lambda r, i: (i, r)),
                 pl.BlockSpec((BLK, kw), lambda r, i: (i, r))]
    if has_sinks:
        out_shape.append(jax.ShapeDtypeStruct((8, 128), F32))
        out_specs.append(pl.BlockSpec((8, 128), lambda r, i: (0, 0)))

    def body(*refs):
        it = iter(refs)
        q0_ref, q1_ref, kc_ref, kp_ref, vc_ref, vp_ref, do0_ref, do1_ref, o0_ref, o1_ref, ls0_ref, ls1_ref = (
            next(it) for _ in range(12))
        sk_ref = next(it) if has_sinks else None
        dq_ref, dk_ref, dv_ref = next(it), next(it), next(it)
        dsk_ref = next(it) if has_sinks else None
        r_id, i = pl.program_id(0), pl.program_id(1)
        mask_c, mask_b = _band_masks(max_dist)
        mask_p = mask_b & (i > 0)
        mask_n = mask_b & (i + 1 < nb)
        if has_sinks:
            lane = lax.broadcasted_iota(jnp.int32, (8, 128), 1)
            dsk = jnp.zeros((8, 128), F32)
        for kv in range(nkv):
            ks = slice(kv * HEAD_DIM, (kv + 1) * HEAD_DIM)
            kc, kp, vc, vp = kc_ref[:, ks], kp_ref[:, ks], vc_ref[:, ks], vp_ref[:, ks]
            dk = jnp.zeros((BLK, HEAD_DIM), F32)
            dv = jnp.zeros((BLK, HEAD_DIM), F32)
            for g in range(G):
                h = kv * G + g
                hs = slice(h * HEAD_DIM, (h + 1) * HEAD_DIM)
                q0, q1, do0, do1 = q0_ref[:, hs], q1_ref[:, hs], do0_ref[:, hs], do1_ref[:, hs]
                ls0 = ls0_ref[:, h * HEAD_DIM:h * HEAD_DIM + 1]
                ls1 = ls1_ref[:, h * HEAD_DIM:h * HEAD_DIM + 1]
                dl0 = jnp.sum(do0.astype(F32) * o0_ref[:, hs].astype(F32), axis=-1, keepdims=True)
                dl1 = jnp.sum(do1.astype(F32) * o1_ref[:, hs].astype(F32), axis=-1, keepdims=True)
                p_cc = jnp.where(mask_c, jnp.exp(_dot(q0, kc, NT) * scale - ls0), 0.0)
                ds_cc = (p_cc * (_dot(do0, vc, NT) - dl0) * scale).astype(BF16)
                p_cp = jnp.where(mask_p, jnp.exp(_dot(q0, kp, NT) * scale - ls0), 0.0)
                ds_cp = (p_cp * (_dot(do0, vp, NT) - dl0) * scale).astype(BF16)
                p_nc = jnp.where(mask_n, jnp.exp(_dot(q1, kc, NT) * scale - ls1), 0.0)
                ds_nc = (p_nc * (_dot(do1, vc, NT) - dl1) * scale).astype(BF16)
                dq_ref[:, hs] = _dot(ds_cc, kc) + _dot(ds_cp, kp)
                dk = dk + _dot(ds_cc, q0, TN) + _dot(ds_nc, q1, TN)
                dv = dv + _dot(p_cc.astype(BF16), do0, TN) + _dot(p_nc.astype(BF16), do1, TN)
                if has_sinks:
                    val = -jnp.sum(jnp.exp(sk_ref[h] - ls0) * dl0, axis=0, keepdims=True)
                    dsk = dsk + jnp.where(lane == h, val, 0.0)
            dk_ref[:, ks] = dk
            dv_ref[:, ks] = dv
        if has_sinks:
            @pl.when((r_id == 0) & (i == 0))
            def _():
                dsk_ref[...] = jnp.zeros_like(dsk_ref)

            dsk_ref[...] += dsk

    sem = ("arbitrary", "arbitrary") if has_sinks else ("parallel", "parallel")
    return _call(body, name=name, grid=(d, nb), in_specs=specs, out_specs=out_specs, out_shape=out_shape, sem=sem)(*ins)


def _attn_grad_combine(branches, tabs, *, name, tm=256):
    L, qw = branches[0][0].shape
    kw = branches[0][1].shape[1]
    nbr = len(branches)
    row = lambda w: pl.BlockSpec((tm, w), lambda i: (i, 0))
    ins, specs = [], []
    for dq, dk, dv in branches:
        ins += [dq, dk, dv]; specs += [row(qw), row(kw), row(kw)]
    ins += list(tabs); specs += [row(128)] * 3

    def body(*refs):
        c, s1, s2 = (t[...] for t in refs[3 * nbr:3 * nbr + 3])
        o_ref = refs[-1]
        for part, (w, off, rot) in enumerate(((qw, 0, True), (kw, qw, True), (kw, qw + kw, False))):
            for cb in range(w // 128):
                cs = slice(cb * 128, (cb + 1) * 128)
                v = refs[part][:, cs]
                for b in range(1, nbr):
                    v = v + refs[3 * b + part][:, cs]
                if rot:
                    v = _rope_bwd(v, c, s1, s2)
                o_ref[:, off + cb * 128:off + (cb + 1) * 128] = v.astype(BF16)

    return _call(body, name=name, grid=(L // tm,), in_specs=specs, out_specs=row(qw + 2 * kw),
                 out_shape=jax.ShapeDtypeStruct((L, qw + 2 * kw), BF16), sem=("parallel",))(*ins)


def _xattn_fwd(q, kv, *, name, tq=512):
    L, W = q.shape
    scale = XA_HEAD_DIM ** -0.5
    row = pl.BlockSpec((tq, W), lambda i: (i, 0))
    kvs = pl.BlockSpec((N_MEM, 2 * W), lambda i: (0, 0))

    def body(q_ref, kv_ref, o_ref, lse_ref):
        for h in range(XA_HEADS):
            hs = slice(h * XA_HEAD_DIM, (h + 1) * XA_HEAD_DIM)
            vs = slice(W + h * XA_HEAD_DIM, W + (h + 1) * XA_HEAD_DIM)
            s = _dot(q_ref[:, hs], kv_ref[:, hs], NT) * scale
            m = jnp.max(s, axis=-1, keepdims=True)
            p = jnp.exp(s - m)
            l = jnp.sum(p, axis=-1, keepdims=True)
            o_ref[:, hs] = (_dot(p.astype(BF16), kv_ref[:, vs]) / l).astype(BF16)
            lse_ref[:, hs] = jnp.broadcast_to(m + jnp.log(l), (tq, XA_HEAD_DIM))

    return _call(body, name=name, grid=(L // tq,), in_specs=[row, kvs], out_specs=[row, row],
                 out_shape=[jax.ShapeDtypeStruct((L, W), BF16), jax.ShapeDtypeStruct((L, W), F32)], sem=("parallel",))(q, kv)


def _xattn_bwd(q, kv, o, lse, do, *, name, tq=512):
    L, W = q.shape
    scale = XA_HEAD_DIM ** -0.5
    row = pl.BlockSpec((tq, W), lambda i: (i, 0))
    kvs = pl.BlockSpec((N_MEM, 2 * W), lambda i: (0, 0))

    def body(q_ref, kv_ref, o_ref, lse_ref, do_ref, dq_ref, dkv_ref):
        @pl.when(pl.program_id(0) == 0)
        def _():
            dkv_ref[...] = jnp.zeros_like(dkv_ref)

        for h in range(XA_HEADS):
            hs = slice(h * XA_HEAD_DIM, (h + 1) * XA_HEAD_DIM)
            vs = slice(W + h * XA_HEAD_DIM, W + (h + 1) * XA_HEAD_DIM)
            qh, kh, vh, doh = q_ref[:, hs], kv_ref[:, hs], kv_ref[:, vs], do_ref[:, hs]
            p = jnp.exp(_dot(qh, kh, NT) * scale - lse_ref[:, h * XA_HEAD_DIM:h * XA_HEAD_DIM + 1])
            dl = jnp.sum(doh.astype(F32) * o_ref[:, hs].astype(F32), axis=-1, keepdims=True)
            ds = (p * (_dot(doh, vh, NT) - dl) * scale).astype(BF16)
            dq_ref[:, hs] = _dot(ds, kh).astype(BF16)
            dkv_ref[:, hs] += _dot(ds, qh, TN)
            dkv_ref[:, vs] += _dot(p.astype(BF16), doh, TN)

    return _call(body, name=name, grid=(L // tq,), in_specs=[row, kvs, row, row, row], out_specs=[row, kvs],
                 out_shape=[jax.ShapeDtypeStruct((L, W), BF16), jax.ShapeDtypeStruct((N_MEM, 2 * W), F32)],
                 sem=("arbitrary",))(q, kv, o, lse, do)


def _neg_expm1(z):
    series = -(z * (1.0 + z * (0.5 + z * (1.0 / 6.0 + z * (1.0 / 24.0 + z * (1.0 / 120.0))))))
    return jnp.where(z > -0.05, series, 1.0 - jnp.exp(z))


def _softplus(z):
    return jnp.maximum(z, 0.0) + jnp.log(1.0 + jnp.exp(-jnp.abs(z)))


def _gelu_parts(y):
    c = 0.7978845608028654
    t = jnp.tanh(c * (y + 0.044715 * y * y * y))
    gy = 0.5 * y * (1.0 + t)
    dgy = 0.5 * (1.0 + t) + 0.5 * y * (1.0 - t * t) * c * (1.0 + 3.0 * 0.044715 * y * y)
    return gy, dgy


def _lru_gates(xc, wa_ref, ba, wx_ref, bx, sp):
    rs, igs = [], []
    for hd in range(LRU_HEADS):
        sl = slice(hd * LRU_HEAD_DIM, (hd + 1) * LRU_HEAD_DIM)
        xh = xc[:, sl].astype(BF16)
        rs.append(_sigmoid(_dot(xh, wa_ref[hd]) + ba[:, sl]))
        igs.append(_sigmoid(_dot(xh, wx_ref[hd]) + bx[:, sl]))
    r, ig = jnp.concatenate(rs, axis=1), jnp.concatenate(igs, axis=1)
    la = -LRU_C * r * sp
    return r, ig, jnp.exp(la), _neg_expm1(2.0 * la)


def _conv_taps(x_ext, halo):
    n = x_ext.shape[0]
    return [x_ext[halo:] if k == CONV_WIDTH - 1 else pltpu.roll(x_ext, CONV_WIDTH - 1 - k, 0)[halo:]
            for k in range(CONV_WIDTH)]


def _lru_fwd(proj, cw, cb, wa, ba, wx, bx, lam, *, name, tc=512):
    L = proj.shape[0]
    W = LRU_HEADS * LRU_HEAD_DIM
    nb = L // tc
    whole = lambda shape: pl.BlockSpec(shape, lambda i: (0,) * len(shape))
    specs = [pl.BlockSpec((tc, W), lambda i: (i, 0)), pl.BlockSpec((tc, W), lambda i: (i, 1)),
             pl.BlockSpec((16, W), lambda i: (jnp.maximum(i * (tc // 16) - 1, 0), 0)),
             whole((CONV_WIDTH, W)), whole((1, W)), whole((LRU_HEADS, LRU_HEAD_DIM, LRU_HEAD_DIM)), whole((1, W)),
             whole((LRU_HEADS, LRU_HEAD_DIM, LRU_HEAD_DIM)), whole((1, W)), whole((1, W))]
    out_specs = [pl.BlockSpec((tc, W), lambda i: (i, 0))] * 2
    out_shape = [jax.ShapeDtypeStruct((L, W), BF16), jax.ShapeDtypeStruct((L, W), F32)]

    def body(x_ref, y_ref, xh_ref, cw_ref, cb_ref, wa_ref, ba_ref, wx_ref, bx_ref, lam_ref, rec_ref, hs_ref,
             hcar, a_scr, b_scr):
        i = pl.program_id(0)

        @pl.when(i == 0)
        def _():
            hcar[...] = jnp.zeros_like(hcar)

        halo = jnp.where(i > 0, xh_ref[...].astype(F32), 0.0)
        taps = _conv_taps(jnp.concatenate([halo, x_ref[...].astype(F32)], axis=0), 16)
        xc = cb_ref[...] + sum(cw_ref[k:k + 1, :] * taps[k] for k in range(CONV_WIDTH))
        _, ig, a, om = _lru_gates(xc, wa_ref, ba_ref[...], wx_ref, bx_ref[...], _softplus(-lam_ref[...]))
        b = jnp.sqrt(om) * (ig * xc)
        rowmod = lax.broadcasted_iota(jnp.int32, (tc, W), 0) & 7
        for s in (1, 2, 4):
            keep = rowmod >= s
            b = jnp.where(keep, a * pltpu.roll(b, s, 0) + b, b)
            a = jnp.where(keep, a * pltpu.roll(a, s, 0), a)
        a_scr[...] = a
        b_scr[...] = b

        def tile(j, hc):
            rows = pl.ds(pl.multiple_of(j * 8, 8), 8)
            ht = a_scr[rows, :] * hc + b_scr[rows, :]
            hs_ref[rows, :] = ht
            return jnp.broadcast_to(ht[7:8, :], (8, W))

        hcar[...] = lax.fori_loop(0, tc // 8, tile, hcar[...])
        gy, _ = _gelu_parts(y_ref[...].astype(F32))
        rec_ref[...] = (hs_ref[...] * gy).astype(BF16)

    return _call(body, name=name, grid=(nb,), in_specs=specs, out_specs=out_specs, out_shape=out_shape,
                 scratch=[pltpu.VMEM((8, W), F32), pltpu.VMEM((tc, W), F32), pltpu.VMEM((tc, W), F32)],
                 sem=("arbitrary",))(proj, proj, proj, cw, cb, wa, ba, wx, bx, lam)


def _lru_bwd(proj, hs, drec_src, cw, cb, wa, ba, wx, bx, lam, *, name, tc=256):
    L = proj.shape[0]
    W = LRU_HEADS * LRU_HEAD_DIM
    nb = L // tc
    tb = lambda i: nb - 1 - i
    whole = lambda shape: pl.BlockSpec(shape, lambda i: (0,) * len(shape))
    gate_w = (LRU_HEADS, LRU_HEAD_DIM, LRU_HEAD_DIM)
    specs = [pl.BlockSpec((tc, W), lambda i: (tb(i), 0)), pl.BlockSpec((tc, W), lambda i: (tb(i), 1)),
             pl.BlockSpec((16, W), lambda i: (jnp.maximum(tb(i) * (tc // 16) - 1, 0), 0)),
             pl.BlockSpec((tc, W), lambda i: (tb(i), 0)),
             pl.BlockSpec((8, W), lambda i: (jnp.maximum(tb(i) * (tc // 8) - 1, 0), 0)),
             pl.BlockSpec((tc, W), lambda i: (tb(i), 0)),
             whole((CONV_WIDTH, W)), whole((1, W)), whole(gate_w), whole((1, W)), whole(gate_w), whole((1, W)), whole((1, W))]
    out_specs = [pl.BlockSpec((tc, 2 * W), lambda i: (tb(i), 0)), whole((CONV_WIDTH, W)), whole((1, W)), whole(gate_w),
                 whole((1, W)), whole(gate_w), whole((1, W)), whole((1, W))]
    vec = jax.ShapeDtypeStruct((1, W), F32)
    out_shape = [jax.ShapeDtypeStruct((L, 2 * W), BF16), jax.ShapeDtypeStruct((CONV_WIDTH, W), F32), vec,
                 jax.ShapeDtypeStruct(gate_w, F32), vec, jax.ShapeDtypeStruct(gate_w, F32), vec, vec]

    def body(x_ref, y_ref, xh_ref, hs_ref, hh_ref, dr_ref, cw_ref, cb_ref, wa_ref, ba_ref, wx_ref, bx_ref, lam_ref,
             dxy_ref, dcw_ref, dcb_ref, dwa_ref, dba_ref, dwx_ref, dbx_ref, dlam_ref, gcar, dxc_car, a_scr, b_scr, g_scr):
        pid = pl.program_id(0)
        t = tb(pid)
        accs = (dcw_ref, dcb_ref, dwa_ref, dba_ref, dwx_ref, dbx_ref, dlam_ref)

        @pl.when(pid == 0)
        def _():
            gcar[...] = jnp.zeros_like(gcar)
            dxc_car[...] = jnp.zeros_like(dxc_car)
            for r in accs:
                r[...] = jnp.zeros_like(r)

        halo = jnp.where(t > 0, xh_ref[...].astype(F32), 0.0)
        taps = _conv_taps(jnp.concatenate([halo, x_ref[...].astype(F32)], axis=0), 16)
        xc = cb_ref[...] + sum(cw_ref[k:k + 1, :] * taps[k] for k in range(CONV_WIDTH))
        lam = lam_ref[...]
        sp = _softplus(-lam)
        r, ig, a, om = _lru_gates(xc, wa_ref, ba_ref[...], wx_ref, bx_ref[...], sp)
        sq = jnp.sqrt(om)
        hblk = hs_ref[...]
        hprev = pltpu.roll(jnp.concatenate([jnp.where(t > 0, hh_ref[...], 0.0), hblk], axis=0), 1, 0)[8:]
        gy, dgy = _gelu_parts(y_ref[...].astype(F32))
        drec = dr_ref[...].astype(F32)
        dxy_ref[:, W:] = (drec * hblk * dgy).astype(BF16)

        rowidx = lax.broadcasted_iota(jnp.int32, (tc, W), 0)
        rowmod = rowidx & 7
        ca = jnp.where(rowidx == tc - 1, 1.0, pltpu.roll(a, tc - 1, 0))
        cbv = drec * gy
        for s in (1, 2, 4):
            keep = rowmod < 8 - s
            cbv = jnp.where(keep, ca * pltpu.roll(cbv, tc - s, 0) + cbv, cbv)
            ca = jnp.where(keep, ca * pltpu.roll(ca, tc - s, 0), ca)
        a_scr[...] = ca
        b_scr[...] = cbv

        def tile(k, gc):
            j = tc // 8 - 1 - k
            rows = pl.ds(pl.multiple_of(j * 8, 8), 8)
            gt = a_scr[rows, :] * gc + b_scr[rows, :]
            g_scr[rows, :] = gt
            return jnp.broadcast_to(gt[0:1, :], (8, W))

        lax.fori_loop(0, tc // 8, tile, gcar[...])
        G = g_scr[...]
        gcar[...] = jnp.broadcast_to(a[0:1, :] * G[0:1, :], (8, W))

        da = G * hprev
        dsq = G * (ig * xc)
        di = G * (sq * xc)
        dxc = G * (sq * ig)
        dla = da * a - 2.0 * a * a * (dsq * 0.5 * lax.rsqrt(om))
        dlam_ref[...] += jnp.sum(dla * (-LRU_C * r), axis=0, keepdims=True) * (-_sigmoid(-lam))
        dpr = dla * (-LRU_C * sp) * r * (1.0 - r)
        dpi = di * ig * (1.0 - ig)
        dba_ref[...] += jnp.sum(dpr, axis=0, keepdims=True)
        dbx_ref[...] += jnp.sum(dpi, axis=0, keepdims=True)
        back = []
        for hd in range(LRU_HEADS):
            sl = slice(hd * LRU_HEAD_DIM, (hd + 1) * LRU_HEAD_DIM)
            xh, dprh, dpih = xc[:, sl].astype(BF16), dpr[:, sl].astype(BF16), dpi[:, sl].astype(BF16)
            back.append(_dot(dprh, wa_ref[hd], NT) + _dot(dpih, wx_ref[hd], NT))
            dwa_ref[hd] += _dot(xh, dprh, TN)
            dwx_ref[hd] += _dot(xh, dpih, TN)
        dxc = dxc + jnp.concatenate(back, axis=1)
        dcb_ref[...] += jnp.sum(dxc, axis=0, keepdims=True)
        for k in range(CONV_WIDTH):
            dcw_ref[k:k + 1, :] += jnp.sum(dxc * taps[k], axis=0, keepdims=True)
        ext = jnp.concatenate([dxc, dxc_car[...]], axis=0)
        dx = cw_ref[CONV_WIDTH - 1:CONV_WIDTH, :] * dxc
        for k in range(CONV_WIDTH - 1):
            dx = dx + cw_ref[k:k + 1, :] * pltpu.roll(ext, tc + 8 - (CONV_WIDTH - 1 - k), 0)[:tc]
        dxc_car[...] = dxc[0:8, :]
        dxy_ref[:, :W] = dx.astype(BF16)

    scratch = [pltpu.VMEM((8, W), F32), pltpu.VMEM((8, W), F32)] + [pltpu.VMEM((tc, W), F32)] * 3
    return _call(body, name=name, grid=(nb,), in_specs=specs, out_specs=out_specs, out_shape=out_shape, scratch=scratch,
                 sem=("arbitrary",))(proj, proj, proj, hs, hs, drec_src, cw, cb, wa, ba, wx, bx, lam)


def _final_loss(h, gain, target, *, name, tm=256):
    M, K = h.shape
    row = pl.BlockSpec((tm, K), lambda i: (i, 0))
    vec = pl.BlockSpec((1, K), lambda i: (0, 0))
    one = pl.BlockSpec((1, 128), lambda i: (0, 0))

    def body(h_ref, g_ref, t_ref, dh_ref, dg_ref, loss_ref):
        @pl.when(pl.program_id(0) == 0)
        def _():
            dg_ref[...] = jnp.zeros_like(dg_ref)
            loss_ref[...] = jnp.zeros_like(loss_ref)

        x = h_ref[...]
        r = lax.rsqrt(jnp.mean(x * x, axis=-1, keepdims=True) + NORM_EPS)
        xhat = x * r
        err = xhat * g_ref[...] - t_ref[...]
        loss_ref[...] += 0.5 / K * jnp.sum(err * err)
        dy = err * (1.0 / K)
        dg_ref[...] += jnp.sum(dy * xhat, axis=0, keepdims=True)
        dxh = dy * g_ref[...]
        dh_ref[...] = r * (dxh - xhat * jnp.mean(dxh * xhat, axis=-1, keepdims=True))

    return _call(body, name=name, grid=(M // tm,), in_specs=[row, vec, row], out_specs=[row, vec, one],
                 out_shape=[jax.ShapeDtypeStruct((M, K), F32), jax.ShapeDtypeStruct((1, K), F32),
                            jax.ShapeDtypeStruct((1, 128), F32)], sem=("arbitrary",))(h, gain.reshape(1, K), target)


def _dilated_fwd(proj0):
    L = proj0.shape[0]
    qkv = proj0[:, 2 * D_MODEL:]
    W3 = 3 * B_HEADS * HEAD_DIM
    state = None
    for bi, (window, d) in enumerate(DILATED_PATTERN):
        view = qkv.reshape(L // d, d * W3)
        sv = None if state is None else [s.reshape(L // d, d * B_HEADS * HEAD_DIM) for s in state]
        outs = _band_fwd(view, view, view, d=d, nq=B_HEADS, nkv=B_HEADS, qcol=lambda r: 3 * r, kcol=lambda r: 3 * r + 1,
                         vcol=lambda r: 3 * r + 2, max_dist=window // d, state=sv,
                         finalize=bi == len(DILATED_PATTERN) - 1, name=f"dilated_fwd_d{d}")
        state = [o.reshape(L, B_HEADS * HEAD_DIM) for o in outs]
    return state


def _dilated_bwd(proj0, att, lse, datt, tabs):
    L = proj0.shape[0]
    qkv = proj0[:, 2 * D_MODEL:]
    Wh = B_HEADS * HEAD_DIM
    branches = []
    for window, d in DILATED_PATTERN:
        view = qkv.reshape(L // d, d * 3 * Wh)
        v1 = lambda t: t.reshape(L // d, d * Wh)
        outs = _band_bwd(view, view, view, v1(datt), v1(att), v1(lse), d=d, nq=B_HEADS, nkv=B_HEADS,
                         qcol=lambda r: 3 * r, kcol=lambda r: 3 * r + 1, vcol=lambda r: 3 * r + 2, docol=lambda r: r,
                         max_dist=window // d, name=f"dilated_bwd_d{d}")
        branches.append([o.reshape(L, Wh) for o in outs])
    return _attn_grad_combine(branches, tabs, name="dilated_grad_combine")


def _device_step(x, mem, target, w):
    L = x.shape[0]
    tabs = _rope_tables(L)
    g = {}
    saved = []
    h = x
    for layer in range(2):
        sv = {"h_mix": h}
        if layer == 0:
            proj, n = _rowmm(h, w["ab_w_in"], name="l0_in_proj", gain=w["mix_norm"][0],
                             rope=(2 * D_MODEL, 2 * D_MODEL + 2 * B_HEADS * HEAD_DIM, tabs))
            rec, hs = _lru_fwd(proj, w["lru_conv_w"], w["lru_conv_b"], w["lru_wa"], w["lru_ba"], w["lru_wx"], w["lru_bx"],
                               w["lru_lambda"], name="lru_fwd")
            att, lse = _dilated_fwd(proj)
            mix = jnp.concatenate([rec, att], axis=1)
            (h,) = _rowmm(mix, w["ab_w_out"], name="l0_out_proj", res=h)
            sv.update(hs=hs)
        else:
            proj, n = _rowmm(h, w["c_w_qkv"], name="l1_qkv_proj", gain=w["mix_norm"][1], bias=w["c_b_qkv"],
                             rope=(0, (C_HEADS + C_KV_HEADS) * HEAD_DIM, tabs))
            mix, lse = _band_fwd(proj, proj, proj, d=1, nq=C_HEADS, nkv=C_KV_HEADS, qcol=lambda r: 0, kcol=lambda r: 8,
                                 vcol=lambda r: 9, max_dist=C_WINDOW - 1, sinks=w["c_sinks"], name="swa_fwd")
            (h,) = _rowmm(mix, w["c_w_out"], name="l1_out_proj", res=h, bias=w["c_b_out"])
        sv.update(proj=proj, n_mix=n, mix=mix, lse=lse, h_xa=h)
        xq, nx = _rowmm(h, w["xa_wq"][layer][None], name=f"xa_q_proj{layer}", gain=w["xa_norm"][layer])
        kv, nm = _rowmm(mem, w["xa_wkv"][layer][None], name=f"xa_kv_proj{layer}", gain=w["xa_mem_norm"][layer])
        xo, xlse = _xattn_fwd(xq, kv, name=f"xa_fwd{layer}")
        (h,) = _rowmm(xo, w["xa_wo"][layer], name=f"xa_out_proj{layer}", res=h)
        sv.update(xq=xq, nx=nx, kv=kv, nm=nm, xo=xo, xlse=xlse, h_ffn=h)
        gu, nf, act = _rowmm(h, w["ffn_w_gate_up"][layer], name=f"ffn_in{layer}", gain=w["ffn_norm"][layer], swiglu=True)
        (h,) = _rowmm(act, w["ffn_w_down"][layer][None], name=f"ffn_out{layer}", res=h, tm=512)
        sv.update(gu=gu, nf=nf, act=act)
        saved.append(sv)

    dh, g["final_norm"], loss = _final_loss(h, w["final_norm"], target, name="final_loss")

    stk = {k: [None, None] for k in ("xa_norm", "xa_mem_norm", "ffn_norm", "mix_norm", "xa_wq", "xa_wkv", "xa_wo",
                                      "ffn_w_gate_up", "ffn_w_down")}
    for layer in (1, 0):
        sv = saved[layer]
        (stk["ffn_w_down"][layer],) = _mm_tn(sv["act"], dh, S=1, name=f"ffn_down_dw{layer}", kk=D_FF // 2)
        (dgu,) = _mm_nt(dh, w["ffn_w_down"][layer][None], name=f"ffn_dact{layer}", mode="swiglu", kchunk=D_FF // 2, gu=sv["gu"])
        (stk["ffn_w_gate_up"][layer],) = _mm_tn(sv["nf"], dgu, S=N_CHIPS, name=f"ffn_gu_dw{layer}")
        dh, stk["ffn_norm"][layer] = _mm_nt(dgu, w["ffn_w_gate_up"][layer], name=f"ffn_dx{layer}", mode="norm",
                                            h=sv["h_ffn"], gain=w["ffn_norm"][layer], dh=dh)
        (stk["xa_wo"][layer],) = _mm_tn(sv["xo"], dh, S=N_CHIPS, name=f"xa_wo_dw{layer}")
        (dxo,) = _mm_nt(dh, w["xa_wo"][layer], name=f"xa_dxo{layer}", mode="plain")
        dxq, dkv = _xattn_bwd(sv["xq"], sv["kv"], sv["xo"], sv["xlse"], dxo, name=f"xa_bwd{layer}")
        (stk["xa_wq"][layer],) = _mm_tn(sv["nx"], dxq, S=1, name=f"xa_wq_dw{layer}")
        dh, stk["xa_norm"][layer] = _mm_nt(dxq, w["xa_wq"][layer][None], name=f"xa_dx{layer}", mode="norm", h=sv["h_xa"],
                                           gain=w["xa_norm"][layer], dh=dh)
        (stk["xa_wkv"][layer],) = _mm_tn(sv["nm"], dkv, S=1, name=f"xa_wkv_dw{layer}")
        _, stk["xa_mem_norm"][layer] = _mm_nt(dkv, w["xa_wkv"][layer][None], name=f"xa_dmem{layer}", mode="norm", h=mem,
                                              gain=w["xa_mem_norm"][layer])
        if layer == 1:
            g["c_w_out"], g["c_b_out"] = _mm_tn(sv["mix"], dh, S=1, name="l1_out_dw", bias=True)
            (dmix,) = _mm_nt(dh, w["c_w_out"], name="l1_dmix", mode="plain")
            dq, dk, dv, dsk = _band_bwd(sv["proj"], sv["proj"], sv["proj"], dmix, sv["mix"], sv["lse"], d=1, nq=C_HEADS,
                                        nkv=C_KV_HEADS, qcol=lambda r: 0, kcol=lambda r: 8, vcol=lambda r: 9,
                                        docol=lambda r: 0, max_dist=C_WINDOW - 1, sinks=w["c_sinks"], name="swa_bwd")
            g["c_sinks"] = dsk[0, :C_HEADS]
            dproj = _attn_grad_combine([(dq, dk, dv)], tabs, name="swa_grad_combine")
            g["c_w_qkv"], g["c_b_qkv"] = _mm_tn(sv["n_mix"], dproj, S=1, name="l1_qkv_dw", bias=True)
            dh, stk["mix_norm"][1] = _mm_nt(dproj, w["c_w_qkv"], name="l1_dx", mode="norm", h=sv["h_mix"],
                                            gain=w["mix_norm"][1], dh=dh)
        else:
            (g["ab_w_out"],) = _mm_tn(sv["mix"], dh, S=1, name="l0_out_dw", kk=768)
            (dmix,) = _mm_nt(dh, w["ab_w_out"], name="l0_dmix", mode="plain", kchunk=768)
            (dxy, g["lru_conv_w"], g["lru_conv_b"], g["lru_wa"], g["lru_ba"], g["lru_wx"], g["lru_bx"],
             g["lru_lambda"]) = _lru_bwd(sv["proj"], sv["hs"], dmix, w["lru_conv_w"], w["lru_conv_b"], w["lru_wa"],
                                         w["lru_ba"], w["lru_wx"], w["lru_bx"], w["lru_lambda"], name="lru_bwd")
            dqkv = _dilated_bwd(sv["proj"], sv["mix"][:, D_MODEL:], sv["lse"], dmix[:, D_MODEL:], tabs)
            dproj = jnp.concatenate([dxy, dqkv], axis=1)
            (g["ab_w_in"],) = _mm_tn(sv["n_mix"], dproj, S=N_CHIPS, name="l0_in_dw")
            dh, stk["mix_norm"][0] = _mm_nt(dproj, w["ab_w_in"], name="l0_dx", mode="norm", h=sv["h_mix"],
                                            gain=w["mix_norm"][0], dh=dh)
    for k, v in stk.items():
        g[k] = jnp.concatenate(v, axis=0) if v[0].shape[0] == 1 else jnp.stack(v, axis=0)
    return loss[0, 0], dh, g


ANY = pl.BlockSpec(memory_space=pl.ANY)
MESH = pl.DeviceIdType.MESH


def _place():
    x, y, c = lax.axis_index("x"), lax.axis_index("y"), lax.axis_index("c")
    return x, y, c, [(1 - x, y), (x, 1 - y), (1 - x, 1 - y)]


def _remote(send_sems, recv_sems):
    def copy(k, src, dst, to):
        return pltpu.make_async_remote_copy(src_ref=src, dst_ref=dst, send_sem=send_sems.at[k], recv_sem=recv_sems.at[k],
                                            device_id=to, device_id_type=MESH)
    return copy


def _gather_weights(wpack, spack):
    R = wpack.shape[0]
    Rh = R // 2

    def body(w_ref, s_ref, wf_ref, sf_ref, send_sems, recv_sems, local_sems):
        x, y, c, chips = _place()
        me, sib = 2 * x + y, (x, y, 1 - c)
        copy = _remote(send_sems, recv_sems)
        half = lambda chip, hh: wf_ref.at[chip, pl.ds(hh * Rh, Rh), :]
        mine_w = pltpu.make_async_copy(w_ref, wf_ref.at[me], local_sems.at[0])
        mine_s = pltpu.make_async_copy(s_ref, sf_ref.at[me], local_sems.at[1])
        mine_w.start()
        mine_s.start()
        sends = []
        for j, (cx, cy) in enumerate(chips):
            sends.append(copy(j, w_ref.at[pl.ds(c * Rh, Rh), :], half(me, c), (cx, cy, c)))
            sends.append(copy(3 + j, s_ref, sf_ref.at[me], (cx, cy, c)))
        for cp in sends:
            cp.start()
        for j, (cx, cy) in enumerate(chips):
            got = half(2 * cx + cy, c)
            copy(j, got, got, sib).wait_recv()
            fwd = copy(6 + j, got, got, sib)
            fwd.start()
            sends.append(fwd)
        for j, (cx, cy) in enumerate(chips):
            got = half(2 * cx + cy, 1 - c)
            copy(6 + j, got, got, sib).wait_recv()
            copy(3 + j, s_ref, sf_ref.at[2 * cx + cy], sib).wait_recv()
        for cp in sends:
            cp.wait_send()
        mine_w.wait()
        mine_s.wait()

    out_shape = [jax.ShapeDtypeStruct((N_CHIPS,) + wpack.shape, wpack.dtype),
                 jax.ShapeDtypeStruct((N_CHIPS,) + spack.shape, spack.dtype)]
    return pl.pallas_call(body, name="gather_weights", out_shape=out_shape, in_specs=[ANY, ANY], out_specs=[ANY, ANY],
                          scratch_shapes=[pltpu.SemaphoreType.DMA((9,)), pltpu.SemaphoreType.DMA((9,)),
                                          pltpu.SemaphoreType.DMA((2,))])(wpack, spack)


def _rs_pair_exchange(gpack):
    _, R, C = gpack.shape
    Rh = R // 2

    def body(g_ref, ra_ref, send_sems, recv_sems):
        x, y, c, _ = _place()
        copy = _remote(send_sems, recv_sems)
        cps = [copy(j, g_ref.at[j, pl.ds((1 - c) * Rh, Rh), :], ra_ref.at[j], (x, y, 1 - c)) for j in range(N_CHIPS)]
        for cp in cps:
            cp.start()
        for cp in cps:
            cp.wait()

    return pl.pallas_call(body, name="rs_pair_exchange", out_shape=jax.ShapeDtypeStruct((N_CHIPS, Rh, C), gpack.dtype),
                          in_specs=[ANY], out_specs=ANY,
                          scratch_shapes=[pltpu.SemaphoreType.DMA((N_CHIPS,)), pltpu.SemaphoreType.DMA((N_CHIPS,))])(gpack)


def _rs_pair_add(place, gpack, ra, *, tr=PACK_ROWS // 6):
    _, R, C = gpack.shape
    Rh = R // 2
    nrb = Rh // tr

    def body(p_ref, g_ref, ra_ref, pair_ref, own_ref):
        s = g_ref[...].astype(F32) + ra_ref[...].astype(F32)
        pair_ref[...] = s.astype(BF16)

        @pl.when(pl.program_id(1) == p_ref[1])
        def _():
            own_ref[...] = s

    grid_spec = pltpu.PrefetchScalarGridSpec(
        num_scalar_prefetch=1, grid=(nrb, N_CHIPS),
        in_specs=[pl.BlockSpec((None, tr, C), lambda i, j, p: (j, p[0] * nrb + i, 0)),
                  pl.BlockSpec((None, tr, C), lambda i, j, p: (j, i, 0))],
        out_specs=[pl.BlockSpec((None, tr, C), lambda i, j, p: (j, i, 0)), pl.BlockSpec((tr, C), lambda i, j, p: (i, 0))])
    return pl.pallas_call(
        body, name="rs_pair_add", grid_spec=grid_spec,
        out_shape=[jax.ShapeDtypeStruct((N_CHIPS, Rh, C), BF16), jax.ShapeDtypeStruct((Rh, C), F32)],
        compiler_params=pltpu.CompilerParams(dimension_semantics=("arbitrary", "arbitrary"),
                                             vmem_limit_bytes=VMEM_LIMIT_V7X))(place, gpack, ra)


def _rs_chip_exchange(pair, small):
    _, Rh, C = pair.shape

    def body(p_ref, s_ref, rb_ref, rs_ref, send_sems, recv_sems, local_sem):
        x, y, c, chips = _place()
        copy = _remote(send_sems, recv_sems)
        dev = 4 * x + 2 * y + c
        mine = pltpu.make_async_copy(s_ref, rs_ref.at[dev], local_sem.at[0])
        mine.start()
        cps = [copy(j, p_ref.at[2 * cx + cy], rb_ref.at[j], (cx, cy, c)) for j, (cx, cy) in enumerate(chips)]
        peers = []
        for k in range(1, 8):
            px = 1 - x if k & 4 else x
            py = 1 - y if k & 2 else y
            pc = 1 - c if k & 1 else c
            peers.append((px, py, pc))
            cps.append(copy(2 + k, s_ref, rs_ref.at[dev], (px, py, pc)))
        for cp in cps:
            cp.start()
        for j in range(3):
            copy(j, p_ref.at[0], rb_ref.at[j], (x, y, c)).wait_recv()
        for k, (px, py, pc) in enumerate(peers, start=1):
            copy(2 + k, s_ref, rs_ref.at[4 * px + 2 * py + pc], (x, y, c)).wait_recv()
        for cp in cps:
            cp.wait_send()
        mine.wait()

    out_shape = [jax.ShapeDtypeStruct((3, Rh, C), pair.dtype), jax.ShapeDtypeStruct((8,) + small.shape, small.dtype)]
    return pl.pallas_call(body, name="rs_chip_exchange", out_shape=out_shape, in_specs=[ANY, ANY], out_specs=[ANY, ANY],
                          scratch_shapes=[pltpu.SemaphoreType.DMA((10,)), pltpu.SemaphoreType.DMA((10,)),
                                          pltpu.SemaphoreType.DMA((1,))])(pair, small)


def _rs_final_add(own, rb, *, tr=PACK_ROWS // 6):
    Rh, C = own.shape

    def body(o_ref, rb_ref, f_ref):
        f_ref[...] = ((o_ref[...] + rb_ref[0].astype(F32)) + rb_ref[1].astype(F32)) + rb_ref[2].astype(F32)

    return _call(body, name="rs_final_add", grid=(Rh // tr,),
                 in_specs=[pl.BlockSpec((tr, C), lambda i: (i, 0)), pl.BlockSpec((3, tr, C), lambda i: (0, i, 0))],
                 out_specs=pl.BlockSpec((tr, C), lambda i: (i, 0)), out_shape=jax.ShapeDtypeStruct((Rh, C), F32),
                 sem=("parallel",))(own, rb)


def _sum_slots(rs):
    n, rows, C = rs.shape

    def body(r_ref, o_ref):
        acc = r_ref[0]
        for k in range(1, n):
            acc = acc + r_ref[k]
        o_ref[...] = acc

    return _call(body, name="small_grad_sum", grid=(1,), in_specs=[pl.BlockSpec((n, rows, C), lambda i: (0, 0, 0))],
                 out_specs=pl.BlockSpec((rows, C), lambda i: (0, 0)), out_shape=jax.ShapeDtypeStruct((rows, C), F32),
                 sem=("arbitrary",))(rs)


def _rs_sibling_share(fin):
    Rh, C = fin.shape

    def body(f_ref, g_ref, send_sems, recv_sems, local_sem):
        x, y, c, _ = _place()
        copy = _remote(send_sems, recv_sems)
        mine = pltpu.make_async_copy(f_ref, g_ref.at[pl.ds(c * Rh, Rh), :], local_sem.at[0])
        mine.start()
        out = copy(0, f_ref, g_ref.at[pl.ds(c * Rh, Rh), :], (x, y, 1 - c))
        out.start()
        copy(0, f_ref, g_ref.at[pl.ds((1 - c) * Rh, Rh), :], (x, y, c)).wait_recv()
        out.wait_send()
        mine.wait()

    return pl.pallas_call(body, name="rs_sibling_share", out_shape=jax.ShapeDtypeStruct((2 * Rh, C), fin.dtype),
                          in_specs=[ANY], out_specs=ANY,
                          scratch_shapes=[pltpu.SemaphoreType.DMA((1,)), pltpu.SemaphoreType.DMA((1,)),
                                          pltpu.SemaphoreType.DMA((1,))])(fin)


def _adamw(w, g, m, v, *, name):
    rows, cols = w.shape
    tr = rows
    for cand in range(min(rows, 512), 7, -8):
        if rows % cand == 0:
            tr = cand
            break
    spec = pl.BlockSpec((tr, cols), lambda i: (i, 0))

    def body(w_ref, g_ref, m_ref, v_ref, d_ref, nm_ref, nv_ref):
        gg = g_ref[...]
        nm = ADAM_B1 * m_ref[...] + (1.0 - ADAM_B1) * gg
        nv = ADAM_B2 * v_ref[...] + (1.0 - ADAM_B2) * (gg * gg)
        m_hat = nm / (1.0 - ADAM_B1 ** ADAM_STEP)
        v_hat = nv / (1.0 - ADAM_B2 ** ADAM_STEP)
        d_ref[...] = -ADAM_LR * (m_hat / (jnp.sqrt(v_hat) + ADAM_EPS) + ADAM_WD * w_ref[...])
        nm_ref[...] = nm
        nv_ref[...] = nv

    return _call(body, name=name, grid=(rows // tr,), in_specs=[spec] * 4, out_specs=[spec] * 3,
                 out_shape=[jax.ShapeDtypeStruct((rows, cols), F32)] * 3, sem=("parallel",))(w, g, m, v)


WEIGHT_NAMES = ("mix_norm", "ab_w_in", "lru_conv_w", "lru_conv_b", "lru_wa", "lru_ba", "lru_wx", "lru_bx", "lru_lambda",
                "ab_w_out", "c_w_qkv", "c_b_qkv", "c_sinks", "c_w_out", "c_b_out", "xa_norm", "xa_mem_norm", "xa_wq",
                "xa_wkv", "xa_wo", "ffn_norm", "ffn_w_gate_up", "ffn_w_down", "final_norm")
BIG = ("ab_w_in", "lru_wa", "lru_wx", "ab_w_out", "c_w_qkv", "c_w_out", "xa_wq", "xa_wkv", "xa_wo", "ffn_w_gate_up",
       "ffn_w_down")
REPLICATED = ("mix_norm", "lru_conv_b", "lru_lambda", "c_sinks", "xa_norm", "xa_mem_norm", "ffn_norm", "final_norm")
SMALL_SHARDED = ("lru_conv_w", "lru_ba", "lru_bx", "c_b_qkv", "c_b_out")
LANES = 1024


def _rows(v):
    flat = v.reshape(-1)
    pad = -flat.shape[0] % LANES
    if pad:
        flat = jnp.concatenate([flat, jnp.zeros((pad,), flat.dtype)])
    return flat.reshape(-1, LANES)


def _stack_rows(parts, total):
    rows = jnp.concatenate(parts, axis=0)
    return jnp.concatenate([rows, jnp.zeros((total - rows.shape[0], LANES), rows.dtype)], axis=0)


def _unshard(name, t):
    if name in ("ab_w_in", "ab_w_out", "c_w_out"):
        return t if name == "ab_w_in" else t.reshape(1, -1, t.shape[-1])
    if name in ("lru_wa", "lru_wx"):
        return t.transpose(1, 0, 2, 3).reshape(LRU_HEADS, LRU_HEAD_DIM, LRU_HEAD_DIM)
    if name == "c_w_qkv":
        return t.transpose(1, 0, 2).reshape(1, D_MODEL, -1)
    if name in ("xa_wq", "xa_wkv", "ffn_w_down"):
        return t.transpose(1, 0, 2, 3).reshape(2, -1, t.shape[-1])
    return t.transpose(1, 0, 2, 3)


def _to_shards(name, g):
    if name in ("ab_w_in", "ab_w_out", "c_w_out"):
        return g.reshape(N_CHIPS, -1)
    if name in ("lru_wa", "lru_wx"):
        return g.reshape(LRU_HEADS, N_CHIPS, -1).transpose(1, 0, 2).reshape(N_CHIPS, -1)
    if name == "c_w_qkv":
        return g.reshape(D_MODEL, N_CHIPS, -1).transpose(1, 0, 2).reshape(N_CHIPS, -1)
    if name in ("xa_wq", "xa_wkv", "ffn_w_down"):
        return g.reshape(2, N_CHIPS, -1).transpose(1, 0, 2).reshape(N_CHIPS, -1)
    return g.transpose(1, 0, 2, 3).reshape(N_CHIPS, -1)


def kernel(x, mem, mix_norm, ab_w_in, lru_conv_w, lru_conv_b, lru_wa, lru_ba, lru_wx, lru_bx, lru_lambda, ab_w_out, c_w_qkv, c_b_qkv, c_sinks, c_w_out, c_b_out, xa_norm, xa_mem_norm, xa_wq, xa_wkv, xa_wo, ffn_norm, ffn_w_gate_up, ffn_w_down, final_norm, loss_target, m_mix_norm, m_ab_w_in, m_lru_conv_w, m_lru_conv_b, m_lru_wa, m_lru_ba, m_lru_wx, m_lru_bx, m_lru_lambda, m_ab_w_out, m_c_w_qkv, m_c_b_qkv, m_c_sinks, m_c_w_out, m_c_b_out, m_xa_norm, m_xa_mem_norm, m_xa_wq, m_xa_wkv, m_xa_wo, m_ffn_norm, m_ffn_w_gate_up, m_ffn_w_down, m_final_norm, v_mix_norm, v_ab_w_in, v_lru_conv_w, v_lru_conv_b, v_lru_wa, v_lru_ba, v_lru_wx, v_lru_bx, v_lru_lambda, v_ab_w_out, v_c_w_qkv, v_c_b_qkv, v_c_sinks, v_c_w_out, v_c_b_out, v_xa_norm, v_xa_mem_norm, v_xa_wq, v_xa_wkv, v_xa_wo, v_ffn_norm, v_ffn_w_gate_up, v_ffn_w_down, v_final_norm):
    given = dict(locals())
    wl = {n: given[n] for n in WEIGHT_NAMES}
    ml = {n: given["m_" + n] for n in WEIGHT_NAMES}
    vl = {n: given["v_" + n] for n in WEIGHT_NAMES}
    xi, yi, ci = lax.axis_index("x"), lax.axis_index("y"), lax.axis_index("c")
    chip = 2 * xi + yi

    big_rows = {n: wl[n].size // LANES for n in BIG}
    wpack = _stack_rows([wl[n].astype(BF16).reshape(big_rows[n], LANES) for n in BIG], PACK_ROWS)
    spack = _stack_rows([_rows(wl[n]) for n in SMALL_SHARDED], 8)
    wfull, sfull = _gather_weights(wpack, spack)
    w = {n: wl[n] for n in REPLICATED}
    w["c_sinks"] = wl["c_sinks"][0]
    off = 0
    for n in BIG:
        loc = wl[n].shape[1:] if wl[n].shape[0] == 1 else wl[n].shape
        w[n] = _unshard(n, wfull[:, off:off + big_rows[n]].reshape((N_CHIPS,) + loc))
        off += big_rows[n]
    for r, n in enumerate(SMALL_SHARDED):
        loc = wl[n].shape[1:]
        t = sfull[:, r, :wl[n].size].reshape((N_CHIPS,) + loc)
        if n == "lru_conv_w":
            w[n] = t.transpose(1, 0, 2).reshape(CONV_WIDTH, -1)
        elif n in ("lru_ba", "lru_bx"):
            w[n] = t.transpose(1, 0, 2).reshape(1, -1)
        else:
            w[n] = t.reshape(1, -1)

    loss_part, grad_x, g = _device_step(x[0], mem[0], loss_target[0], w)

    gparts = [_to_shards(n, g[n].astype(BF16)).reshape(N_CHIPS, big_rows[n], LANES) for n in BIG]
    gparts.append(jnp.zeros((N_CHIPS, PACK_ROWS - off, LANES), BF16))
    gpack = jnp.concatenate(gparts, axis=1)
    small_parts = [_rows(g[n]) for n in REPLICATED] + [_rows(jnp.broadcast_to(loss_part, (LANES,)))]
    small_parts += [_rows(g[n]) for n in SMALL_SHARDED]
    small = _stack_rows(small_parts, 24)
    place = jnp.stack([ci, chip]).astype(jnp.int32)
    ra = _rs_pair_exchange(gpack)
    pair, own = _rs_pair_add(place, gpack, ra)
    rb, rs = _rs_chip_exchange(pair, small)
    gsum = _rs_sibling_share(_rs_final_add(own, rb))
    ssum = _sum_slots(rs)

    grads = {}
    off = 0
    for n in BIG:
        grads[n] = gsum[off:off + big_rows[n]].reshape(wl[n].shape)
        off += big_rows[n]
    row = 0
    for n in REPLICATED:
        k = _rows(g[n]).shape[0]
        grads[n] = ssum[row:row + k].reshape(-1)[:wl[n].size].reshape(wl[n].shape)
        row += k
    loss = ssum[row, 0]
    row += 1
    for n in SMALL_SHARDED:
        k = _rows(g[n]).shape[0]
        full = ssum[row:row + k].reshape(-1)[:g[n].size]
        row += k
        loc = wl[n].shape
        if n == "lru_conv_w":
            sh = full.reshape(CONV_WIDTH, N_CHIPS, -1)
        elif n in ("lru_ba", "lru_bx"):
            sh = full.reshape(LRU_HEADS, N_CHIPS, -1)
        else:
            sh = full.reshape(1, N_CHIPS, -1)
        grads[n] = lax.dynamic_index_in_dim(sh, chip, axis=1, keepdims=False).reshape(loc)

    delta, new_m, new_v = {}, {}, {}
    for n in BIG:
        shape2 = (wl[n].size // wl[n].shape[-1], wl[n].shape[-1])
        d, nm, nv = _adamw(wl[n].reshape(shape2), grads[n].reshape(shape2), ml[n].reshape(shape2), vl[n].reshape(shape2),
                           name="adamw_" + n)
        delta[n], new_m[n], new_v[n] = (t.reshape(wl[n].shape) for t in (d, nm, nv))
    smalls = REPLICATED + SMALL_SHARDED
    packs = [_stack_rows([_rows(src[n]) for n in smalls], 24) for src in (wl, grads, ml, vl)]
    outs = _adamw(*packs, name="adamw_small")
    row = 0
    for n in smalls:
        k = _rows(wl[n]).shape[0]
        for dst, o in zip((delta, new_m, new_v), outs):
            dst[n] = o[row:row + k].reshape(-1)[:wl[n].size].reshape(wl[n].shape)
        row += k

    return (loss, grad_x[None], *[grads[n] for n in WEIGHT_NAMES], *[delta[n] for n in WEIGHT_NAMES],
            *[new_m[n] for n in WEIGHT_NAMES], *[new_v[n] for n in WEIGHT_NAMES])
```

```python
import jax
import jax.numpy as jnp
from jax import lax
from jax.experimental import pallas as pl
from jax.experimental.pallas import tpu as pltpu

F32, BF16 = jnp.float32, jnp.bfloat16
D_MODEL = 1024
NORM_EPS = 1e-6
ROPE_THETA = 500000.0
HEAD_DIM = 64
ROT_DIM = 16
BLK = 128
LRU_HEADS, LRU_HEAD_DIM, CONV_WIDTH, LRU_C = 4, 256, 4, 8.0
DILATED_PATTERN = ((128, 1), (512, 4), (2048, 16))
B_HEADS, C_HEADS, C_KV_HEADS, C_WINDOW = 8, 16, 2, 128
XA_HEADS, XA_HEAD_DIM, N_MEM = 4, 128, 256
D_FF = 2816
NEG = -1e30
ADAM_LR, ADAM_B1, ADAM_B2, ADAM_EPS, ADAM_WD, ADAM_STEP = 0.001, 0.9, 0.999, 1e-08, 0.01, 10
N_CHIPS = 4
PACK_ROWS = 7296
VMEM_LIMIT_V7X = 56 * 1024 * 1024

NN = (((1,), (0,)), ((), ()))
NT = (((1,), (1,)), ((), ()))
TN = (((0,), (0,)), ((), ()))


def _dot(a, b, dims=NN):
    return lax.dot_general(a, b, dims, preferred_element_type=F32)


def _sigmoid(x):
    return 1.0 / (1.0 + jnp.exp(-x))


def _call(body, *, name, grid, in_specs, out_specs, out_shape, scratch=(), sem=None):
    return pl.pallas_call(
        body, name=name, grid=grid, in_specs=in_specs, out_specs=out_specs, out_shape=out_shape,
        scratch_shapes=list(scratch),
        compiler_params=pltpu.CompilerParams(dimension_semantics=sem, vmem_limit_bytes=VMEM_LIMIT_V7X))


def _rope_tables(L):
    half = ROT_DIM // 2
    inv = ROPE_THETA ** (-jnp.arange(0, ROT_DIM, 2, dtype=F32) / ROT_DIM)
    ang = jnp.arange(L, dtype=F32)[:, None] * inv[None, :]
    cos, sin = jnp.cos(ang), jnp.sin(ang)
    rest = HEAD_DIM - ROT_DIM
    z8, zr, one = jnp.zeros((L, half), F32), jnp.zeros((L, rest), F32), jnp.ones((L, rest), F32)
    c = jnp.concatenate([cos, cos, one], axis=1)
    s1 = jnp.concatenate([-sin, z8, zr], axis=1)
    s2 = jnp.concatenate([z8, sin, zr], axis=1)
    return tuple(jnp.concatenate([t, t], axis=1) for t in (c, s1, s2))


def _rope_fwd(v, c, s1, s2):
    return v * c + pltpu.roll(v, 120, 1) * s1 + pltpu.roll(v, 8, 1) * s2


def _rope_bwd(dv, c, s1, s2):
    return dv * c + pltpu.roll(dv * s1, 8, 1) + pltpu.roll(dv * s2, 120, 1)


def _rowmm(a, w3, *, name, tm=256, gain=None, bias=None, res=None, swiglu=False, rope=None):
    M, K = a.shape
    S, _, Ns = w3.shape
    N = S * Ns
    tm = min(tm, M)
    has_norm, has_bias, has_res, has_rope = gain is not None, bias is not None, res is not None, rope is not None
    row = lambda w: pl.BlockSpec((tm, w), lambda i: (i, 0))
    whole = lambda shape: pl.BlockSpec(shape, lambda i: (0,) * len(shape))
    ins, specs = [a], [row(K)]
    if has_norm:
        ins.append(gain.reshape(1, K)); specs.append(whole((1, K)))
    ins.append(w3); specs.append(whole((S, K, Ns)))
    if has_bias:
        ins.append(bias.reshape(1, N)); specs.append(whole((1, N)))
    if has_res:
        ins.append(res); specs.append(row(N))
    if has_rope:
        ins += list(rope[2]); specs += [row(128)] * 3
    y_dtype = F32 if has_res else BF16
    out_shape, out_specs = [jax.ShapeDtypeStruct((M, N), y_dtype)], [row(N)]
    if has_norm:
        out_shape.append(jax.ShapeDtypeStruct((M, K), BF16)); out_specs.append(row(K))
    if swiglu:
        out_shape.append(jax.ShapeDtypeStruct((M, N // 2), BF16)); out_specs.append(row(N // 2))
    scratch = [pltpu.VMEM((tm, N), F32)] if has_rope else []

    def body(*refs):
        it = iter(refs)
        a_ref = next(it)
        g_ref = next(it) if has_norm else None
        w_ref = next(it)
        b_ref = next(it) if has_bias else None
        r_ref = next(it) if has_res else None
        tabs = [next(it) for _ in range(3)] if has_rope else None
        y_ref = next(it)
        n_ref = next(it) if has_norm else None
        act_ref = next(it) if swiglu else None
        ys_ref = next(it) if has_rope else None
        if has_norm:
            x = a_ref[...].astype(F32)
            ms = jnp.mean(x * x, axis=-1, keepdims=True)
            xb = (x * lax.rsqrt(ms + NORM_EPS) * g_ref[...]).astype(BF16)
            n_ref[...] = xb
        else:
            xb = a_ref[...].astype(BF16)
        if swiglu:
            for s in range(S // 2):
                g = _dot(xb, w_ref[s])
                u = _dot(xb, w_ref[s + S // 2])
                y_ref[:, s * Ns:(s + 1) * Ns] = g.astype(BF16)
                y_ref[:, N // 2 + s * Ns:N // 2 + (s + 1) * Ns] = u.astype(BF16)
                act_ref[:, s * Ns:(s + 1) * Ns] = (g * _sigmoid(g) * u).astype(BF16)
            return
        for s in range(S):
            sl = slice(s * Ns, (s + 1) * Ns)
            acc = _dot(xb, w_ref[s])
            if has_bias:
                acc = acc + b_ref[:, sl]
            if has_res:
                acc = acc + r_ref[:, sl]
            if has_rope:
                ys_ref[:, sl] = acc
            else:
                y_ref[:, sl] = acc.astype(y_dtype)
        if has_rope:
            c, s1, s2 = (t[...] for t in tabs)
            for cb in range(N // 128):
                cs = slice(cb * 128, (cb + 1) * 128)
                v = ys_ref[:, cs]
                if rope[0] <= cb * 128 < rope[1]:
                    v = _rope_fwd(v, c, s1, s2)
                y_ref[:, cs] = v.astype(BF16)

    return _call(body, name=name, grid=(M // tm,), in_specs=specs, out_specs=out_specs, out_shape=out_shape,
                 scratch=scratch, sem=("parallel",))(*ins)


def _mm_nt(dy, w3, *, name, mode, tm=256, kchunk=None, h=None, gain=None, dh=None, gu=None):
    M, N = dy.shape
    S, K, Ns = w3.shape
    kchunk = kchunk or K
    tm = min(tm, M)
    row = lambda w: pl.BlockSpec((tm, w), lambda i: (i, 0))
    whole = lambda shape: pl.BlockSpec(shape, lambda i: (0,) * len(shape))
    ins, specs = [dy, w3], [row(N), whole((S, K, Ns))]
    has_dh = dh is not None
    if mode == "norm":
        ins += [h, gain.reshape(1, K)]; specs += [row(K), whole((1, K))]
        if has_dh:
            ins.append(dh); specs.append(row(K))
        out_shape = [jax.ShapeDtypeStruct((M, K), F32), jax.ShapeDtypeStruct((1, K), F32)]
        out_specs = [row(K), whole((1, K))]
    elif mode == "swiglu":
        ins.append(gu); specs.append(row(2 * K))
        out_shape, out_specs = [jax.ShapeDtypeStruct((M, 2 * K), BF16)], [row(2 * K)]
    else:
        out_shape, out_specs = [jax.ShapeDtypeStruct((M, K), BF16)], [row(K)]

    def body(*refs):
        it = iter(refs)
        dy_ref, w_ref = next(it), next(it)
        if mode == "norm":
            h_ref, g_ref = next(it), next(it)
            dh_ref = next(it) if has_dh else None
            o_ref, dg_ref = next(it), next(it)
        elif mode == "swiglu":
            gu_ref, o_ref = next(it), next(it)
        else:
            o_ref = next(it)
        for kc in range(K // kchunk):
            ks = slice(kc * kchunk, (kc + 1) * kchunk)
            acc = None
            for s in range(S):
                t = _dot(dy_ref[:, s * Ns:(s + 1) * Ns].astype(BF16), w_ref[s, ks, :], NT)
                acc = t if acc is None else acc + t
            if mode == "plain":
                o_ref[:, ks] = acc.astype(BF16)
            elif mode == "swiglu":
                us = slice(K + kc * kchunk, K + (kc + 1) * kchunk)
                g = gu_ref[:, ks].astype(F32)
                u = gu_ref[:, us].astype(F32)
                sg = _sigmoid(g)
                o_ref[:, ks] = (acc * u * (sg * (1.0 + g * (1.0 - sg)))).astype(BF16)
                o_ref[:, us] = (acc * (g * sg)).astype(BF16)
            else:
                x = h_ref[...].astype(F32)
                r = lax.rsqrt(jnp.mean(x * x, axis=-1, keepdims=True) + NORM_EPS)
                xhat = x * r
                dxh = acc * g_ref[...]
                dx = r * (dxh - xhat * jnp.mean(dxh * xhat, axis=-1, keepdims=True))
                o_ref[...] = dx + dh_ref[...] if has_dh else dx

                @pl.when(pl.program_id(0) == 0)
                def _():
                    dg_ref[...] = jnp.zeros_like(dg_ref)

                dg_ref[...] += jnp.sum(acc * xhat, axis=0, keepdims=True)

    sem = ("arbitrary",) if mode == "norm" else ("parallel",)
    return _call(body, name=name, grid=(M // tm,), in_specs=specs, out_specs=out_specs, out_shape=out_shape, sem=sem)(*ins)


def _mm_tn(x, dy, *, S, name, tk=1024, kk=None, bias=False):
    M, K = x.shape
    N = dy.shape[1]
    Ns = N // S
    kk = kk or K
    tk = min(tk, M)
    nl = M // tk
    in_specs = [pl.BlockSpec((tk, kk), lambda s, kc, l: (l, kc)), pl.BlockSpec((tk, Ns), lambda s, kc, l: (l, s))]
    out_shape = [jax.ShapeDtypeStruct((S, K, Ns), BF16)]
    out_specs = [pl.BlockSpec((None, kk, Ns), lambda s, kc, l: (s, kc, 0))]
    if bias:
        out_shape.append(jax.ShapeDtypeStruct((1, N), F32))
        out_specs.append(pl.BlockSpec((1, Ns), lambda s, kc, l: (0, s)))

    def body(x_ref, dy_ref, o_ref, *rest):
        acc_ref = rest[-1]
        kc, l = pl.program_id(1), pl.program_id(2)

        @pl.when(l == 0)
        def _():
            acc_ref[...] = jnp.zeros_like(acc_ref)

        acc_ref[...] += _dot(x_ref[...].astype(BF16), dy_ref[...].astype(BF16), TN)
        if bias:
            b_ref = rest[0]

            @pl.when((kc == 0) & (l == 0))
            def _():
                b_ref[...] = jnp.zeros_like(b_ref)

            @pl.when(kc == 0)
            def _():
                b_ref[...] += jnp.sum(dy_ref[...].astype(F32), axis=0, keepdims=True)

        @pl.when(l == nl - 1)
        def _():
            o_ref[...] = acc_ref[...].astype(BF16)

    return _call(body, name=name, grid=(S, K // kk, nl), in_specs=in_specs, out_specs=out_specs, out_shape=out_shape,
                 scratch=[pltpu.VMEM((kk, Ns), F32)], sem=("arbitrary", "arbitrary", "arbitrary"))(x, dy)


def _band_masks(max_dist):
    rows = lax.broadcasted_iota(jnp.int32, (BLK, BLK), 0)
    cols = lax.broadcasted_iota(jnp.int32, (BLK, BLK), 1)
    dist = rows - cols
    return (dist >= 0) & (dist <= max_dist), (dist + BLK) <= max_dist


def _band_fwd(qa, ka, va, *, d, nq, nkv, qcol, kcol, vcol, max_dist, state=None, finalize=True, sinks=None, name):
    Lr = qa.shape[0]
    nb = Lr // BLK
    qw, kw, G = nq * HEAD_DIM, nkv * HEAD_DIM, nq // nkv
    scale = HEAD_DIM ** -0.5
    cur = lambda colf, w: pl.BlockSpec((BLK, w), lambda r, i: (i, colf(r)))
    prv = lambda colf, w: pl.BlockSpec((BLK, w), lambda r, i: (jnp.maximum(i - 1, 0), colf(r)))
    st = pl.BlockSpec((BLK, qw), lambda r, i: (i, r))
    ins, specs = [qa, ka, ka, va, va], [cur(qcol, qw), cur(kcol, kw), prv(kcol, kw), cur(vcol, kw), prv(vcol, kw)]
    has_state, has_sinks = state is not None, sinks is not None
    if has_state:
        ins += list(state); specs += [st] * 3
    if has_sinks:
        ins.append(sinks); specs.append(pl.BlockSpec(memory_space=pltpu.SMEM))
    if finalize:
        out_shape = [jax.ShapeDtypeStruct((Lr, d * qw), BF16), jax.ShapeDtypeStruct((Lr, d * qw), F32)]
    else:
        out_shape = [jax.ShapeDtypeStruct((Lr, d * qw), F32)] * 3
    out_specs = [st] * len(out_shape)

    def body(*refs):
        it = iter(refs)
        q_ref, kc_ref, kp_ref, vc_ref, vp_ref = (next(it) for _ in range(5))
        m_in, l_in, a_in = (next(it) for _ in range(3)) if has_state else (None,) * 3
        sk_ref = next(it) if has_sinks else None
        outs = list(it)
        i = pl.program_id(1)
        mask_c, mask_p = _band_masks(max_dist)
        mask_p = mask_p & (i > 0)
        for h in range(nq):
            hs = slice(h * HEAD_DIM, (h + 1) * HEAD_DIM)
            ks = slice((h // G) * HEAD_DIM, (h // G + 1) * HEAD_DIM)
            q = q_ref[:, hs]
            sc = jnp.where(mask_c, _dot(q, kc_ref[:, ks], NT) * scale, NEG)
            sp = jnp.where(mask_p, _dot(q, kp_ref[:, ks], NT) * scale, NEG)
            m = jnp.maximum(jnp.max(sc, axis=-1, keepdims=True), jnp.max(sp, axis=-1, keepdims=True))
            if has_state:
                m_prev = m_in[:, h * HEAD_DIM:h * HEAD_DIM + 1]
                m = jnp.maximum(m, m_prev)
            if has_sinks:
                m = jnp.maximum(m, sk_ref[h])
            pc, pp = jnp.exp(sc - m), jnp.exp(sp - m)
            l = jnp.sum(pc, axis=-1, keepdims=True) + jnp.sum(pp, axis=-1, keepdims=True)
            acc = _dot(pc.astype(BF16), vc_ref[:, ks]) + _dot(pp.astype(BF16), vp_ref[:, ks])
            if has_state:
                alpha = jnp.exp(m_prev - m)
                l = l + alpha * l_in[:, h * HEAD_DIM:h * HEAD_DIM + 1]
                acc = acc + alpha * a_in[:, hs]
            if has_sinks:
                l = l + jnp.exp(sk_ref[h] - m)
            if finalize:
                outs[0][:, hs] = (acc / l).astype(BF16)
                outs[1][:, hs] = jnp.broadcast_to(m + jnp.log(l), (BLK, HEAD_DIM))
            else:
                outs[0][:, hs] = jnp.broadcast_to(m, (BLK, HEAD_DIM))
                outs[1][:, hs] = jnp.broadcast_to(l, (BLK, HEAD_DIM))
                outs[2][:, hs] = acc

    return _call(body, name=name, grid=(d, nb), in_specs=specs, out_specs=out_specs, out_shape=out_shape,
                 sem=("parallel", "parallel"))(*ins)


def _band_bwd(qa, ka, va, doa, oa, lsea, *, d, nq, nkv, qcol, kcol, vcol, docol, max_dist, sinks=None, name):
    Lr = qa.shape[0]
    nb = Lr // BLK
    qw, kw, G = nq * HEAD_DIM, nkv * HEAD_DIM, nq // nkv
    scale = HEAD_DIM ** -0.5
    cur = lambda colf, w: pl.BlockSpec((BLK, w), lambda r, i: (i, colf(r)))
    prv = lambda colf, w: pl.BlockSpec((BLK, w), lambda r, i: (jnp.maximum(i - 1, 0), colf(r)))
    nxt = lambda colf, w: pl.BlockSpec((BLK, w), lambda r, i: (jnp.minimum(i + 1, nb - 1), colf(r)))
    own = lambda r: r
    ins = [qa, qa, ka, ka, va, va, doa, doa, oa, oa, lsea, lsea]
    specs = [cur(qcol, qw), nxt(qcol, qw), cur(kcol, kw), prv(kcol, kw), cur(vcol, kw), prv(vcol, kw),
             cur(docol, qw), nxt(docol, qw), cur(own, qw), nxt(own, qw), cur(own, qw), nxt(own, qw)]
    has_sinks = sinks is not None
    if has_sinks:
        ins.append(sinks); specs.append(pl.BlockSpec(memory_space=pltpu.SMEM))
    out_shape = [jax.ShapeDtypeStruct((Lr, d * qw), F32), jax.ShapeDtypeStruct((Lr, d * kw), F32),
                 jax.ShapeDtypeStruct((Lr, d * kw), F32)]
    out_specs = [pl.BlockSpec((BLK, qw), lambda r, i: (i, r)), pl.BlockSpec((BLK, kw), lambda r, i: (i, r)),
                 pl.BlockSpec((BLK, kw), lambda r, i: (i, r))]
    if has_sinks:
        out_shape.append(jax.ShapeDtypeStruct((8, 128), F32))
        out_specs.append(pl.BlockSpec((8, 128), lambda r, i: (0, 0)))

    def body(*refs):
        it = iter(refs)
        q0_ref, q1_ref, kc_ref, kp_ref, vc_ref, vp_ref, do0_ref, do1_ref, o0_ref, o1_ref, ls0_ref, ls1_ref = (
            next(it) for _ in range(12))
        sk_ref = next(it) if has_sinks else None
        dq_ref, dk_ref, dv_ref = next(it), next(it), next(it)
        dsk_ref = next(it) if has_sinks else None
        r_id, i = pl.program_id(0), pl.program_id(1)
        mask_c, mask_b = _band_masks(max_dist)
        mask_p = mask_b & (i > 0)
        mask_n = mask_b & (i + 1 < nb)
        if has_sinks:
            lane = lax.broadcasted_iota(jnp.int32, (8, 128), 1)
            dsk = jnp.zeros((8, 128), F32)
        for kv in range(nkv):
            ks = slice(kv * HEAD_DIM, (kv + 1) * HEAD_DIM)
            kc, kp, vc, vp = kc_ref[:, ks], kp_ref[:, ks], vc_ref[:, ks], vp_ref[:, ks]
            dk = jnp.zeros((BLK, HEAD_DIM), F32)
            dv = jnp.zeros((BLK, HEAD_DIM), F32)
            for g in range(G):
                h = kv * G + g
                hs = slice(h * HEAD_DIM, (h + 1) * HEAD_DIM)
                q0, q1, do0, do1 = q0_ref[:, hs], q1_ref[:, hs], do0_ref[:, hs], do1_ref[:, hs]
                ls0 = ls0_ref[:, h * HEAD_DIM:h * HEAD_DIM + 1]
                ls1 = ls1_ref[:, h * HEAD_DIM:h * HEAD_DIM + 1]
                dl0 = jnp.sum(do0.astype(F32) * o0_ref[:, hs].astype(F32), axis=-1, keepdims=True)
                dl1 = jnp.sum(do1.astype(F32) * o1_ref[:, hs].astype(F32), axis=-1, keepdims=True)
                p_cc = jnp.where(mask_c, jnp.exp(_dot(q0, kc, NT) * scale - ls0), 0.0)
                ds_cc = (p_cc * (_dot(do0, vc, NT) - dl0) * scale).astype(BF16)
                p_cp = jnp.where(mask_p, jnp.exp(_dot(q0, kp, NT) * scale - ls0), 0.0)
                ds_cp = (p_cp * (_dot(do0, vp, NT) - dl0) * scale).astype(BF16)
                p_nc = jnp.where(mask_n, jnp.exp(_dot(q1, kc, NT) * scale - ls1), 0.0)
                ds_nc = (p_nc * (_dot(do1, vc, NT) - dl1) * scale).astype(BF16)
                dq_ref[:, hs] = _dot(ds_cc, kc) + _dot(ds_cp, kp)
                dk = dk + _dot(ds_cc, q0, TN) + _dot(ds_nc, q1, TN)
                dv = dv + _dot(p_cc.astype(BF16), do0, TN) + _dot(p_nc.astype(BF16), do1, TN)
                if has_sinks:
                    val = -jnp.sum(jnp.exp(sk_ref[h] - ls0) * dl0, axis=0, keepdims=True)
                    dsk = dsk + jnp.where(lane == h, val, 0.0)
            dk_ref[:, ks] = dk
            dv_ref[:, ks] = dv
        if has_sinks:
            @pl.when((r_id == 0) & (i == 0))
            def _():
                dsk_ref[...] = jnp.zeros_like(dsk_ref)

            dsk_ref[...] += dsk

    sem = ("arbitrary", "arbitrary") if has_sinks else ("parallel", "parallel")
    return _call(body, name=name, grid=(d, nb), in_specs=specs, out_specs=out_specs, out_shape=out_shape, sem=sem)(*ins)


def _attn_grad_combine(branches, tabs, *, name, tm=256):
    L, qw = branches[0][0].shape
    kw = branches[0][1].shape[1]
    nbr = len(branches)
    row = lambda w: pl.BlockSpec((tm, w), lambda i: (i, 0))
    ins, specs = [], []
    for dq, dk, dv in branches:
        ins += [dq, dk, dv]; specs += [row(qw), row(kw), row(kw)]
    ins += list(tabs); specs += [row(128)] * 3

    def body(*refs):
        c, s1, s2 = (t[...] for t in refs[3 * nbr:3 * nbr + 3])
        o_ref = refs[-1]
        for part, (w, off, rot) in enumerate(((qw, 0, True), (kw, qw, True), (kw, qw + kw, False))):
            for cb in range(w // 128):
                cs = slice(cb * 128, (cb + 1) * 128)
                v = refs[part][:, cs]
                for b in range(1, nbr):
                    v = v + refs[3 * b + part][:, cs]
                if rot:
                    v = _rope_bwd(v, c, s1, s2)
                o_ref[:, off + cb * 128:off + (cb + 1) * 128] = v.astype(BF16)

    return _call(body, name=name, grid=(L // tm,), in_specs=specs, out_specs=row(qw + 2 * kw),
                 out_shape=jax.ShapeDtypeStruct((L, qw + 2 * kw), BF16), sem=("parallel",))(*ins)


def _xattn_fwd(q, kv, *, name, tq=512):
    L, W = q.shape
    scale = XA_HEAD_DIM ** -0.5
    row = pl.BlockSpec((tq, W), lambda i: (i, 0))
    kvs = pl.BlockSpec((N_MEM, 2 * W), lambda i: (0, 0))

    def body(q_ref, kv_ref, o_ref, lse_ref):
        for h in range(XA_HEADS):
            hs = slice(h * XA_HEAD_DIM, (h + 1) * XA_HEAD_DIM)
            vs = slice(W + h * XA_HEAD_DIM, W + (h + 1) * XA_HEAD_DIM)
            s = _dot(q_ref[:, hs], kv_ref[:, hs], NT) * scale
            m = jnp.max(s, axis=-1, keepdims=True)
            p = jnp.exp(s - m)
            l = jnp.sum(p, axis=-1, keepdims=True)
            o_ref[:, hs] = (_dot(p.astype(BF16), kv_ref[:, vs]) / l).astype(BF16)
            lse_ref[:, hs] = jnp.broadcast_to(m + jnp.log(l), (tq, XA_HEAD_DIM))

    return _call(body, name=name, grid=(L // tq,), in_specs=[row, kvs], out_specs=[row, row],
                 out_shape=[jax.ShapeDtypeStruct((L, W), BF16), jax.ShapeDtypeStruct((L, W), F32)], sem=("parallel",))(q, kv)


def _xattn_bwd(q, kv, o, lse, do, *, name, tq=512):
    L, W = q.shape
    scale = XA_HEAD_DIM ** -0.5
    row = pl.BlockSpec((tq, W), lambda i: (i, 0))
    kvs = pl.BlockSpec((N_MEM, 2 * W), lambda i: (0, 0))

    def body(q_ref, kv_ref, o_ref, lse_ref, do_ref, dq_ref, dkv_ref):
        @pl.when(pl.program_id(0) == 0)
        def _():
            dkv_ref[...] = jnp.zeros_like(dkv_ref)

        for h in range(XA_HEADS):
            hs = slice(h * XA_HEAD_DIM, (h + 1) * XA_HEAD_DIM)
            vs = slice(W + h * XA_HEAD_DIM, W + (h + 1) * XA_HEAD_DIM)
            qh, kh, vh, doh = q_ref[:, hs], kv_ref[:, hs], kv_ref[:, vs], do_ref[:, hs]
            p = jnp.exp(_dot(qh, kh, NT) * scale - lse_ref[:, h * XA_HEAD_DIM:h * XA_HEAD_DIM + 1])
            dl = jnp.sum(doh.astype(F32) * o_ref[:, hs].astype(F32), axis=-1, keepdims=True)
            ds = (p * (_dot(doh, vh, NT) - dl) * scale).astype(BF16)
            dq_ref[:, hs] = _dot(ds, kh).astype(BF16)
            dkv_ref[:, hs] += _dot(ds, qh, TN)
            dkv_ref[:, vs] += _dot(p.astype(BF16), doh, TN)

    return _call(body, name=name, grid=(L // tq,), in_specs=[row, kvs, row, row, row], out_specs=[row, kvs],
                 out_shape=[jax.ShapeDtypeStruct((L, W), BF16), jax.ShapeDtypeStruct((N_MEM, 2 * W), F32)],
                 sem=("arbitrary",))(q, kv, o, lse, do)


def _neg_expm1(z):
    series = -(z * (1.0 + z * (0.5 + z * (1.0 / 6.0 + z * (1.0 / 24.0 + z * (1.0 / 120.0))))))
    return jnp.where(z > -0.05, series, 1.0 - jnp.exp(z))


def _softplus(z):
    return jnp.maximum(z, 0.0) + jnp.log(1.0 + jnp.exp(-jnp.abs(z)))


def _gelu_parts(y):
    c = 0.7978845608028654
    t = jnp.tanh(c * (y + 0.044715 * y * y * y))
    gy = 0.5 * y * (1.0 + t)
    dgy = 0.5 * (1.0 + t) + 0.5 * y * (1.0 - t * t) * c * (1.0 + 3.0 * 0.044715 * y * y)
    return gy, dgy


def _lru_gates(xc, wa_ref, ba, wx_ref, bx, sp):
    rs, igs = [], []
    for hd in range(LRU_HEADS):
        sl = slice(hd * LRU_HEAD_DIM, (hd + 1) * LRU_HEAD_DIM)
        xh = xc[:, sl].astype(BF16)
        rs.append(_sigmoid(_dot(xh, wa_ref[hd]) + ba[:, sl]))
        igs.append(_sigmoid(_dot(xh, wx_ref[hd]) + bx[:, sl]))
    r, ig = jnp.concatenate(rs, axis=1), jnp.concatenate(igs, axis=1)
    la = -LRU_C * r * sp
    return r, ig, jnp.exp(la), _neg_expm1(2.0 * la)


def _conv_taps(x_ext, halo):
    n = x_ext.shape[0]
    return [x_ext[halo:] if k == CONV_WIDTH - 1 else pltpu.roll(x_ext, CONV_WIDTH - 1 - k, 0)[halo:]
            for k in range(CONV_WIDTH)]


def _lru_fwd(proj, cw, cb, wa, ba, wx, bx, lam, *, name, tc=512):
    L = proj.shape[0]
    W = LRU_HEADS * LRU_HEAD_DIM
    nb = L // tc
    whole = lambda shape: pl.BlockSpec(shape, lambda i: (0,) * len(shape))
    specs = [pl.BlockSpec((tc, W), lambda i: (i, 0)), pl.BlockSpec((tc, W), lambda i: (i, 1)),
             pl.BlockSpec((16, W), lambda i: (jnp.maximum(i * (tc // 16) - 1, 0), 0)),
             whole((CONV_WIDTH, W)), whole((1, W)), whole((LRU_HEADS, LRU_HEAD_DIM, LRU_HEAD_DIM)), whole((1, W)),
             whole((LRU_HEADS, LRU_HEAD_DIM, LRU_HEAD_DIM)), whole((1, W)), whole((1, W))]
    out_specs = [pl.BlockSpec((tc, W), lambda i: (i, 0))] * 2
    out_shape = [jax.ShapeDtypeStruct((L, W), BF16), jax.ShapeDtypeStruct((L, W), F32)]

    def body(x_ref, y_ref, xh_ref, cw_ref, cb_ref, wa_ref, ba_ref, wx_ref, bx_ref, lam_ref, rec_ref, hs_ref,
             hcar, a_scr, b_scr):
        i = pl.program_id(0)

        @pl.when(i == 0)
        def _():
            hcar[...] = jnp.zeros_like(hcar)

        halo = jnp.where(i > 0, xh_ref[...].astype(F32), 0.0)
        taps = _conv_taps(jnp.concatenate([halo, x_ref[...].astype(F32)], axis=0), 16)
        xc = cb_ref[...] + sum(cw_ref[k:k + 1, :] * taps[k] for k in range(CONV_WIDTH))
        _, ig, a, om = _lru_gates(xc, wa_ref, ba_ref[...], wx_ref, bx_ref[...], _softplus(-lam_ref[...]))
        b = jnp.sqrt(om) * (ig * xc)
        rowmod = lax.broadcasted_iota(jnp.int32, (tc, W), 0) & 7
        for s in (1, 2, 4):
            keep = rowmod >= s
            b = jnp.where(keep, a * pltpu.roll(b, s, 0) + b, b)
            a = jnp.where(keep, a * pltpu.roll(a, s, 0), a)
        a_scr[...] = a
        b_scr[...] = b

        def tile(j, hc):
            rows = pl.ds(pl.multiple_of(j * 8, 8), 8)
            ht = a_scr[rows, :] * hc + b_scr[rows, :]
            hs_ref[rows, :] = ht
            return jnp.broadcast_to(ht[7:8, :], (8, W))

        hcar[...] = lax.fori_loop(0, tc // 8, tile, hcar[...])
        gy, _ = _gelu_parts(y_ref[...].astype(F32))
        rec_ref[...] = (hs_ref[...] * gy).astype(BF16)

    return _call(body, name=name, grid=(nb,), in_specs=specs, out_specs=out_specs, out_shape=out_shape,
                 scratch=[pltpu.VMEM((8, W), F32), pltpu.VMEM((tc, W), F32), pltpu.VMEM((tc, W), F32)],
                 sem=("arbitrary",))(proj, proj, proj, cw, cb, wa, ba, wx, bx, lam)


def _lru_bwd(proj, hs, drec_src, cw, cb, wa, ba, wx, bx, lam, *, name, tc=256):
    L = proj.shape[0]
    W = LRU_HEADS * LRU_HEAD_DIM
    nb = L // tc
    tb = lambda i: nb - 1 - i
    whole = lambda shape: pl.BlockSpec(shape, lambda i: (0,) * len(shape))
    gate_w = (LRU_HEADS, LRU_HEAD_DIM, LRU_HEAD_DIM)
    specs = [pl.BlockSpec((tc, W), lambda i: (tb(i), 0)), pl.BlockSpec((tc, W), lambda i: (tb(i), 1)),
             pl.BlockSpec((16, W), lambda i: (jnp.maximum(tb(i) * (tc // 16) - 1, 0), 0)),
             pl.BlockSpec((tc, W), lambda i: (tb(i), 0)),
             pl.BlockSpec((8, W), lambda i: (jnp.maximum(tb(i) * (tc // 8) - 1, 0), 0)),
             pl.BlockSpec((tc, W), lambda i: (tb(i), 0)),
             whole((CONV_WIDTH, W)), whole((1, W)), whole(gate_w), whole((1, W)), whole(gate_w), whole((1, W)), whole((1, W))]
    out_specs = [pl.BlockSpec((tc, 2 * W), lambda i: (tb(i), 0)), whole((CONV_WIDTH, W)), whole((1, W)), whole(gate_w),
                 whole((1, W)), whole(gate_w), whole((1, W)), whole((1, W))]
    vec = jax.ShapeDtypeStruct((1, W), F32)
    out_shape = [jax.ShapeDtypeStruct((L, 2 * W), BF16), jax.ShapeDtypeStruct((CONV_WIDTH, W), F32), vec,
                 jax.ShapeDtypeStruct(gate_w, F32), vec, jax.ShapeDtypeStruct(gate_w, F32), vec, vec]

    def body(x_ref, y_ref, xh_ref, hs_ref, hh_ref, dr_ref, cw_ref, cb_ref, wa_ref, ba_ref, wx_ref, bx_ref, lam_ref,
             dxy_ref, dcw_ref, dcb_ref, dwa_ref, dba_ref, dwx_ref, dbx_ref, dlam_ref, gcar, dxc_car, a_scr, b_scr, g_scr):
        pid = pl.program_id(0)
        t = tb(pid)
        accs = (dcw_ref, dcb_ref, dwa_ref, dba_ref, dwx_ref, dbx_ref, dlam_ref)

        @pl.when(pid == 0)
        def _():
            gcar[...] = jnp.zeros_like(gcar)
            dxc_car[...] = jnp.zeros_like(dxc_car)
            for r in accs:
                r[...] = jnp.zeros_like(r)

        halo = jnp.where(t > 0, xh_ref[...].astype(F32), 0.0)
        taps = _conv_taps(jnp.concatenate([halo, x_ref[...].astype(F32)], axis=0), 16)
        xc = cb_ref[...] + sum(cw_ref[k:k + 1, :] * taps[k] for k in range(CONV_WIDTH))
        lam = lam_ref[...]
        sp = _softplus(-lam)
        r, ig, a, om = _lru_gates(xc, wa_ref, ba_ref[...], wx_ref, bx_ref[...], sp)
        sq = jnp.sqrt(om)
        hblk = hs_ref[...]
        hprev = pltpu.roll(jnp.concatenate([jnp.where(t > 0, hh_ref[...], 0.0), hblk], axis=0), 1, 0)[8:]
        gy, dgy = _gelu_parts(y_ref[...].astype(F32))
        drec = dr_ref[...].astype(F32)
        dxy_ref[:, W:] = (drec * hblk * dgy).astype(BF16)

        rowidx = lax.broadcasted_iota(jnp.int32, (tc, W), 0)
        rowmod = rowidx & 7
        ca = jnp.where(rowidx == tc - 1, 1.0, pltpu.roll(a, tc - 1, 0))
        cbv = drec * gy
        for s in (1, 2, 4):
            keep = rowmod < 8 - s
            cbv = jnp.where(keep, ca * pltpu.roll(cbv, tc - s, 0) + cbv, cbv)
            ca = jnp.where(keep, ca * pltpu.roll(ca, tc - s, 0), ca)
        a_scr[...] = ca
        b_scr[...] = cbv

        def tile(k, gc):
            j = tc // 8 - 1 - k
            rows = pl.ds(pl.multiple_of(j * 8, 8), 8)
            gt = a_scr[rows, :] * gc + b_scr[rows, :]
            g_scr[rows, :] = gt
            return jnp.broadcast_to(gt[0:1, :], (8, W))

        lax.fori_loop(0, tc // 8, tile, gcar[...])
        G = g_scr[...]
        gcar[...] = jnp.broadcast_to(a[0:1, :] * G[0:1, :], (8, W))

        da = G * hprev
        dsq = G * (ig * xc)
        di = G * (sq * xc)
        dxc = G * (sq * ig)
        dla = da * a - 2.0 * a * a * (dsq * 0.5 * lax.rsqrt(om))
        dlam_ref[...] += jnp.sum(dla * (-LRU_C * r), axis=0, keepdims=True) * (-_sigmoid(-lam))
        dpr = dla * (-LRU_C * sp) * r * (1.0 - r)
        dpi = di * ig * (1.0 - ig)
        dba_ref[...] += jnp.sum(dpr, axis=0, keepdims=True)
        dbx_ref[...] += jnp.sum(dpi, axis=0, keepdims=True)
        back = []
        for hd in range(LRU_HEADS):
            sl = slice(hd * LRU_HEAD_DIM, (hd + 1) * LRU_HEAD_DIM)
            xh, dprh, dpih = xc[:, sl].astype(BF16), dpr[:, sl].astype(BF16), dpi[:, sl].astype(BF16)
            back.append(_dot(dprh, wa_ref[hd], NT) + _dot(dpih, wx_ref[hd], NT))
            dwa_ref[hd] += _dot(xh, dprh, TN)
            dwx_ref[hd] += _dot(xh, dpih, TN)
        dxc = dxc + jnp.concatenate(back, axis=1)
        dcb_ref[...] += jnp.sum(dxc, axis=0, keepdims=True)
        for k in range(CONV_WIDTH):
            dcw_ref[k:k + 1, :] += jnp.sum(dxc * taps[k], axis=0, keepdims=True)
        ext = jnp.concatenate([dxc, dxc_car[...]], axis=0)
        dx = cw_ref[CONV_WIDTH - 1:CONV_WIDTH, :] * dxc
        for k in range(CONV_WIDTH - 1):
            dx = dx + cw_ref[k:k + 1, :] * pltpu.roll(ext, tc + 8 - (CONV_WIDTH - 1 - k), 0)[:tc]
        dxc_car[...] = dxc[0:8, :]
        dxy_ref[:, :W] = dx.astype(BF16)

    scratch = [pltpu.VMEM((8, W), F32), pltpu.VMEM((8, W), F32)] + [pltpu.VMEM((tc, W), F32)] * 3
    return _call(body, name=name, grid=(nb,), in_specs=specs, out_specs=out_specs, out_shape=out_shape, scratch=scratch,
                 sem=("arbitrary",))(proj, proj, proj, hs, hs, drec_src, cw, cb, wa, ba, wx, bx, lam)


def _final_loss(h, gain, target, *, name, tm=256):
    M, K = h.shape
    row = pl.BlockSpec((tm, K), lambda i: (i, 0))
    vec = pl.BlockSpec((1, K), lambda i: (0, 0))
    one = pl.BlockSpec((1, 128), lambda i: (0, 0))

    def body(h_ref, g_ref, t_ref, dh_ref, dg_ref, loss_ref):
        @pl.when(pl.program_id(0) == 0)
        def _():
            dg_ref[...] = jnp.zeros_like(dg_ref)
            loss_ref[...] = jnp.zeros_like(loss_ref)

        x = h_ref[...]
        r = lax.rsqrt(jnp.mean(x * x, axis=-1, keepdims=True) + NORM_EPS)
        xhat = x * r
        err = xhat * g_ref[...] - t_ref[...]
        loss_ref[...] += 0.5 / K * jnp.sum(err * err)
        dy = err * (1.0 / K)
        dg_ref[...] += jnp.sum(dy * xhat, axis=0, keepdims=True)
        dxh = dy * g_ref[...]
        dh_ref[...] = r * (dxh - xhat * jnp.mean(dxh * xhat, axis=-1, keepdims=True))

    return _call(body, name=name, grid=(M // tm,), in_specs=[row, vec, row], out_specs=[row, vec, one],
                 out_shape=[jax.ShapeDtypeStruct((M, K), F32), jax.ShapeDtypeStruct((1, K), F32),
                            jax.ShapeDtypeStruct((1, 128), F32)], sem=("arbitrary",))(h, gain.reshape(1, K), target)


def _dilated_fwd(proj0):
    L = proj0.shape[0]
    qkv = proj0[:, 2 * D_MODEL:]
    W3 = 3 * B_HEADS * HEAD_DIM
    state = None
    for bi, (window, d) in enumerate(DILATED_PATTERN):
        view = qkv.reshape(L // d, d * W3)
        sv = None if state is None else [s.reshape(L // d, d * B_HEADS * HEAD_DIM) for s in state]
        outs = _band_fwd(view, view, view, d=d, nq=B_HEADS, nkv=B_HEADS, qcol=lambda r: 3 * r, kcol=lambda r: 3 * r + 1,
                         vcol=lambda r: 3 * r + 2, max_dist=window // d, state=sv,
                         finalize=bi == len(DILATED_PATTERN) - 1, name=f"dilated_fwd_d{d}")
        state = [o.reshape(L, B_HEADS * HEAD_DIM) for o in outs]
    return state


def _dilated_bwd(proj0, att, lse, datt, tabs):
    L = proj0.shape[0]
    qkv = proj0[:, 2 * D_MODEL:]
    Wh = B_HEADS * HEAD_DIM
    branches = []
    for window, d in DILATED_PATTERN:
        view = qkv.reshape(L // d, d * 3 * Wh)
        v1 = lambda t: t.reshape(L // d, d * Wh)
        outs = _band_bwd(view, view, view, v1(datt), v1(att), v1(lse), d=d, nq=B_HEADS, nkv=B_HEADS,
                         qcol=lambda r: 3 * r, kcol=lambda r: 3 * r + 1, vcol=lambda r: 3 * r + 2, docol=lambda r: r,
                         max_dist=window // d, name=f"dilated_bwd_d{d}")
        branches.append([o.reshape(L, Wh) for o in outs])
    return _attn_grad_combine(branches, tabs, name="dilated_grad_combine")


def _device_step(x, mem, target, w):
    L = x.shape[0]
    tabs = _rope_tables(L)
    g = {}
    saved = []
    h = x
    for layer in range(2):
        sv = {"h_mix": h}
        if layer == 0:
            proj, n = _rowmm(h, w["ab_w_in"], name="l0_in_proj", gain=w["mix_norm"][0],
                             rope=(2 * D_MODEL, 2 * D_MODEL + 2 * B_HEADS * HEAD_DIM, tabs))
            rec, hs = _lru_fwd(proj, w["lru_conv_w"], w["lru_conv_b"], w["lru_wa"], w["lru_ba"], w["lru_wx"], w["lru_bx"],
                               w["lru_lambda"], name="lru_fwd")
            att, lse = _dilated_fwd(proj)
            mix = jnp.concatenate([rec, att], axis=1)
            (h,) = _rowmm(mix, w["ab_w_out"], name="l0_out_proj", res=h)
            sv.update(hs=hs)
        else:
            proj, n = _rowmm(h, w["c_w_qkv"], name="l1_qkv_proj", gain=w["mix_norm"][1], bias=w["c_b_qkv"],
                             rope=(0, (C_HEADS + C_KV_HEADS) * HEAD_DIM, tabs))
            mix, lse = _band_fwd(proj, proj, proj, d=1, nq=C_HEADS, nkv=C_KV_HEADS, qcol=lambda r: 0, kcol=lambda r: 8,
                                 vcol=lambda r: 9, max_dist=C_WINDOW - 1, sinks=w["c_sinks"], name="swa_fwd")
            (h,) = _rowmm(mix, w["c_w_out"], name="l1_out_proj", res=h, bias=w["c_b_out"])
        sv.update(proj=proj, n_mix=n, mix=mix, lse=lse, h_xa=h)
        xq, nx = _rowmm(h, w["xa_wq"][layer][None], name=f"xa_q_proj{layer}", gain=w["xa_norm"][layer])
        kv, nm = _rowmm(mem, w["xa_wkv"][layer][None], name=f"xa_kv_proj{layer}", gain=w["xa_mem_norm"][layer])
        xo, xlse = _xattn_fwd(xq, kv, name=f"xa_fwd{layer}")
        (h,) = _rowmm(xo, w["xa_wo"][layer], name=f"xa_out_proj{layer}", res=h)
        sv.update(xq=xq, nx=nx, kv=kv, nm=nm, xo=xo, xlse=xlse, h_ffn=h)
        gu, nf, act = _rowmm(h, w["ffn_w_gate_up"][layer], name=f"ffn_in{layer}", gain=w["ffn_norm"][layer], swiglu=True)
        (h,) = _rowmm(act, w["ffn_w_down"][layer][None], name=f"ffn_out{layer}", res=h, tm=512)
        sv.update(gu=gu, nf=nf, act=act)
        saved.append(sv)

    dh, g["final_norm"], loss = _final_loss(h, w["final_norm"], target, name="final_loss")

    stk = {k: [None, None] for k in ("xa_norm", "xa_mem_norm", "ffn_norm", "mix_norm", "xa_wq", "xa_wkv", "xa_wo",
                                      "ffn_w_gate_up", "ffn_w_down")}
    for layer in (1, 0):
        sv = saved[layer]
        (stk["ffn_w_down"][layer],) = _mm_tn(sv["act"], dh, S=1, name=f"ffn_down_dw{layer}", kk=D_FF // 2)
        (dgu,) = _mm_nt(dh, w["ffn_w_down"][layer][None], name=f"ffn_dact{layer}", mode="swiglu", kchunk=D_FF // 2, gu=sv["gu"])
        (stk["ffn_w_gate_up"][layer],) = _mm_tn(sv["nf"], dgu, S=N_CHIPS, name=f"ffn_gu_dw{layer}")
        dh, stk["ffn_norm"][layer] = _mm_nt(dgu, w["ffn_w_gate_up"][layer], name=f"ffn_dx{layer}", mode="norm",
                                            h=sv["h_ffn"], gain=w["ffn_norm"][layer], dh=dh)
        (stk["xa_wo"][layer],) = _mm_tn(sv["xo"], dh, S=N_CHIPS, name=f"xa_wo_dw{layer}")
        (dxo,) = _mm_nt(dh, w["xa_wo"][layer], name=f"xa_dxo{layer}", mode="plain")
        dxq, dkv = _xattn_bwd(sv["xq"], sv["kv"], sv["xo"], sv["xlse"], dxo, name=f"xa_bwd{layer}")
        (stk["xa_wq"][layer],) = _mm_tn(sv["nx"], dxq, S=1, name=f"xa_wq_dw{layer}")
        dh, stk["xa_norm"][layer] = _mm_nt(dxq, w["xa_wq"][layer][None], name=f"xa_dx{layer}", mode="norm", h=sv["h_xa"],
                                           gain=w["xa_norm"][layer], dh=dh)
        (stk["xa_wkv"][layer],) = _mm_tn(sv["nm"], dkv, S=1, name=f"xa_wkv_dw{layer}")
        _, stk["xa_mem_norm"][layer] = _mm_nt(dkv, w["xa_wkv"][layer][None], name=f"xa_dmem{layer}", mode="norm", h=mem,
                                              gain=w["xa_mem_norm"][layer])
        if layer == 1:
            g["c_w_out"], g["c_b_out"] = _mm_tn(sv["mix"], dh, S=1, name="l1_out_dw", bias=True)
            (dmix,) = _mm_nt(dh, w["c_w_out"], name="l1_dmix", mode="plain")
            dq, dk, dv, dsk = _band_bwd(sv["proj"], sv["proj"], sv["proj"], dmix, sv["mix"], sv["lse"], d=1, nq=C_HEADS,
                                        nkv=C_KV_HEADS, qcol=lambda r: 0, kcol=lambda r: 8, vcol=lambda r: 9,
                                        docol=lambda r: 0, max_dist=C_WINDOW - 1, sinks=w["c_sinks"], name="swa_bwd")
            g["c_sinks"] = dsk[0, :C_HEADS]
            dproj = _attn_grad_combine([(dq, dk, dv)], tabs, name="swa_grad_combine")
            g["c_w_qkv"], g["c_b_qkv"] = _mm_tn(sv["n_mix"], dproj, S=1, name="l1_qkv_dw", bias=True)
            dh, stk["mix_norm"][1] = _mm_nt(dproj, w["c_w_qkv"], name="l1_dx", mode="norm", h=sv["h_mix"],
                                            gain=w["mix_norm"][1], dh=dh)
        else:
            (g["ab_w_out"],) = _mm_tn(sv["mix"], dh, S=1, name="l0_out_dw", kk=768)
            (dmix,) = _mm_nt(dh, w["ab_w_out"], name="l0_dmix", mode="plain", kchunk=768)
            (dxy, g["lru_conv_w"], g["lru_conv_b"], g["lru_wa"], g["lru_ba"], g["lru_wx"], g["lru_bx"],
             g["lru_lambda"]) = _lru_bwd(sv["proj"], sv["hs"], dmix, w["lru_conv_w"], w["lru_conv_b"], w["lru_wa"],
                                         w["lru_ba"], w["lru_wx"], w["lru_bx"], w["lru_lambda"], name="lru_bwd")
            dqkv = _dilated_bwd(sv["proj"], sv["mix"][:, D_MODEL:], sv["lse"], dmix[:, D_MODEL:], tabs)
            dproj = jnp.concatenate([dxy, dqkv], axis=1)
            (g["ab_w_in"],) = _mm_tn(sv["n_mix"], dproj, S=N_CHIPS, name="l0_in_dw")
            dh, stk["mix_norm"][0] = _mm_nt(dproj, w["ab_w_in"], name="l0_dx", mode="norm", h=sv["h_mix"],
                                            gain=w["mix_norm"][0], dh=dh)
    for k, v in stk.items():
        g[k] = jnp.concatenate(v, axis=0) if v[0].shape[0] == 1 else jnp.stack(v, axis=0)
    return loss[0, 0], dh, g


ANY = pl.BlockSpec(memory_space=pl.ANY)
MESH = pl.DeviceIdType.MESH


def _place():
    x, y, c = lax.axis_index("x"), lax.axis_index("y"), lax.axis_index("c")
    return x, y, c, [(1 - x, y), (x, 1 - y), (1 - x, 1 - y)]


def _remote(send_sems, recv_sems):
    def copy(k, src, dst, to):
        return pltpu.make_async_remote_copy(src_ref=src, dst_ref=dst, send_sem=send_sems.at[k], recv_sem=recv_sems.at[k],
                                            device_id=to, device_id_type=MESH)
    return copy


def _gather_weights(wpack, spack):
    R = wpack.shape[0]
    Rh = R // 2

    def body(w_ref, s_ref, wf_ref, sf_ref, send_sems, recv_sems):
        x, y, c, chips = _place()
        me, sib = 2 * x + y, (x, y, 1 - c)
        copy = _remote(send_sems, recv_sems)
        half = lambda chip, hh: wf_ref.at[chip, pl.ds(hh * Rh, Rh), :]
        sends = []
        for j, (cx, cy) in enumerate(chips):
            sends.append(copy(j, w_ref.at[pl.ds(c * Rh, Rh), :], half(me, c), (cx, cy, c)))
            sends.append(copy(3 + j, s_ref, sf_ref.at[me], (cx, cy, c)))
        for cp in sends:
            cp.start()
        for j, (cx, cy) in enumerate(chips):
            got = half(2 * cx + cy, c)
            copy(j, got, got, sib).wait_recv()
            fwd = copy(6 + j, got, got, sib)
            fwd.start()
            sends.append(fwd)
        for j, (cx, cy) in enumerate(chips):
            got = half(2 * cx + cy, 1 - c)
            copy(6 + j, got, got, sib).wait_recv()
            copy(3 + j, s_ref, sf_ref.at[2 * cx + cy], sib).wait_recv()
        for cp in sends:
            cp.wait_send()

    out_shape = [jax.ShapeDtypeStruct((N_CHIPS,) + wpack.shape, wpack.dtype),
                 jax.ShapeDtypeStruct((N_CHIPS,) + spack.shape, spack.dtype)]
    wfull, sfull = pl.pallas_call(body, name="gather_weights", out_shape=out_shape, in_specs=[ANY, ANY], out_specs=[ANY, ANY],
                                  scratch_shapes=[pltpu.SemaphoreType.DMA((9,)), pltpu.SemaphoreType.DMA((9,))])(wpack, spack)
    chip = 2 * lax.axis_index("x") + lax.axis_index("y")
    return (lax.dynamic_update_index_in_dim(wfull, wpack, chip, 0), lax.dynamic_update_index_in_dim(sfull, spack, chip, 0))


def _rs_pair_exchange(gpack):
    _, R, C = gpack.shape
    Rh = R // 2

    def body(g_ref, ra_ref, send_sems, recv_sems):
        x, y, c, _ = _place()
        copy = _remote(send_sems, recv_sems)
        cps = [copy(j, g_ref.at[j, pl.ds((1 - c) * Rh, Rh), :], ra_ref.at[j], (x, y, 1 - c)) for j in range(N_CHIPS)]
        for cp in cps:
            cp.start()
        for cp in cps:
            cp.wait()

    return pl.pallas_call(body, name="rs_pair_exchange", out_shape=jax.ShapeDtypeStruct((N_CHIPS, Rh, C), gpack.dtype),
                          in_specs=[ANY], out_specs=ANY,
                          scratch_shapes=[pltpu.SemaphoreType.DMA((N_CHIPS,)), pltpu.SemaphoreType.DMA((N_CHIPS,))])(gpack)


def _rs_pair_add(place, gpack, ra, *, tr=PACK_ROWS // 6):
    _, R, C = gpack.shape
    Rh = R // 2
    nrb = Rh // tr

    def body(p_ref, g_ref, ra_ref, pair_ref, own_ref):
        s = g_ref[...].astype(F32) + ra_ref[...].astype(F32)
        pair_ref[...] = s.astype(BF16)

        @pl.when(pl.program_id(1) == p_ref[1])
        def _():
            own_ref[...] = s

    grid_spec = pltpu.PrefetchScalarGridSpec(
        num_scalar_prefetch=1, grid=(nrb, N_CHIPS),
        in_specs=[pl.BlockSpec((None, tr, C), lambda i, j, p: (j, p[0] * nrb + i, 0)),
                  pl.BlockSpec((None, tr, C), lambda i, j, p: (j, i, 0))],
        out_specs=[pl.BlockSpec((None, tr, C), lambda i, j, p: (j, i, 0)), pl.BlockSpec((tr, C), lambda i, j, p: (i, 0))])
    return pl.pallas_call(
        body, name="rs_pair_add", grid_spec=grid_spec,
        out_shape=[jax.ShapeDtypeStruct((N_CHIPS, Rh, C), BF16), jax.ShapeDtypeStruct((Rh, C), F32)],
        compiler_params=pltpu.CompilerParams(dimension_semantics=("arbitrary", "arbitrary"),
                                             vmem_limit_bytes=VMEM_LIMIT_V7X))(place, gpack, ra)


def _rs_chip_exchange(pair, small):
    _, Rh, C = pair.shape

    def body(p_ref, s_ref, rb_ref, rs_ref, send_sems, recv_sems, local_sem):
        x, y, c, chips = _place()
        copy = _remote(send_sems, recv_sems)
        dev = 4 * x + 2 * y + c
        mine = pltpu.make_async_copy(s_ref, rs_ref.at[dev], local_sem.at[0])
        mine.start()
        cps = [copy(j, p_ref.at[2 * cx + cy], rb_ref.at[j], (cx, cy, c)) for j, (cx, cy) in enumerate(chips)]
        peers = []
        for k in range(1, 8):
            px = 1 - x if k & 4 else x
            py = 1 - y if k & 2 else y
            pc = 1 - c if k & 1 else c
            peers.append((px, py, pc))
            cps.append(copy(2 + k, s_ref, rs_ref.at[dev], (px, py, pc)))
        for cp in cps:
            cp.start()
        for j in range(3):
            copy(j, p_ref.at[0], rb_ref.at[j], (x, y, c)).wait_recv()
        for k, (px, py, pc) in enumerate(peers, start=1):
            copy(2 + k, s_ref, rs_ref.at[4 * px + 2 * py + pc], (x, y, c)).wait_recv()
        for cp in cps:
            cp.wait_send()
        mine.wait()

    out_shape = [jax.ShapeDtypeStruct((3, Rh, C), pair.dtype), jax.ShapeDtypeStruct((8,) + small.shape, small.dtype)]
    return pl.pallas_call(body, name="rs_chip_exchange", out_shape=out_shape, in_specs=[ANY, ANY], out_specs=[ANY, ANY],
                          scratch_shapes=[pltpu.SemaphoreType.DMA((10,)), pltpu.SemaphoreType.DMA((10,)),
                                          pltpu.SemaphoreType.DMA((1,))])(pair, small)


def _rs_final_add(place, own, rb, *, tr=PACK_ROWS // 6):
    Rh, C = own.shape
    nrb = Rh // tr

    def body(p_ref, o_ref, rb_ref, f_ref):
        f_ref[...] = ((o_ref[...] + rb_ref[0].astype(F32)) + rb_ref[1].astype(F32)) + rb_ref[2].astype(F32)

    grid_spec = pltpu.PrefetchScalarGridSpec(
        num_scalar_prefetch=1, grid=(nrb,),
        in_specs=[pl.BlockSpec((tr, C), lambda i, p: (i, 0)), pl.BlockSpec((3, tr, C), lambda i, p: (0, i, 0))],
        out_specs=pl.BlockSpec((tr, C), lambda i, p: (p[0] * nrb + i, 0)))
    return pl.pallas_call(
        body, name="rs_final_add", grid_spec=grid_spec, out_shape=jax.ShapeDtypeStruct((2 * Rh, C), F32),
        compiler_params=pltpu.CompilerParams(dimension_semantics=("arbitrary",), vmem_limit_bytes=VMEM_LIMIT_V7X))(place, own, rb)


def _sum_slots(rs):
    n, rows, C = rs.shape

    def body(r_ref, o_ref):
        acc = r_ref[0]
        for k in range(1, n):
            acc = acc + r_ref[k]
        o_ref[...] = acc

    return _call(body, name="small_grad_sum", grid=(1,), in_specs=[pl.BlockSpec((n, rows, C), lambda i: (0, 0, 0))],
                 out_specs=pl.BlockSpec((rows, C), lambda i: (0, 0)), out_shape=jax.ShapeDtypeStruct((rows, C), F32),
                 sem=("arbitrary",))(rs)


def _rs_sibling_share(gbuf):
    R, C = gbuf.shape
    Rh = R // 2

    def body(f_ref, g_ref, send_sems, recv_sems):
        x, y, c, _ = _place()
        copy = _remote(send_sems, recv_sems)
        mine = g_ref.at[pl.ds(c * Rh, Rh), :]
        theirs = g_ref.at[pl.ds((1 - c) * Rh, Rh), :]
        out = copy(0, mine, mine, (x, y, 1 - c))
        out.start()
        copy(0, theirs, theirs, (x, y, c)).wait_recv()
        out.wait_send()

    return pl.pallas_call(body, name="rs_sibling_share", out_shape=jax.ShapeDtypeStruct((R, C), gbuf.dtype),
                          in_specs=[ANY], out_specs=ANY, input_output_aliases={0: 0},
                          scratch_shapes=[pltpu.SemaphoreType.DMA((1,)), pltpu.SemaphoreType.DMA((1,))])(gbuf)


def _adamw(w, g, m, v, *, name):
    rows, cols = w.shape
    tr = rows
    for cand in range(min(rows, 512), 7, -8):
        if rows % cand == 0:
            tr = cand
            break
    spec = pl.BlockSpec((tr, cols), lambda i: (i, 0))

    def body(w_ref, g_ref, m_ref, v_ref, d_ref, nm_ref, nv_ref):
        gg = g_ref[...]
        nm = ADAM_B1 * m_ref[...] + (1.0 - ADAM_B1) * gg
        nv = ADAM_B2 * v_ref[...] + (1.0 - ADAM_B2) * (gg * gg)
        m_hat = nm / (1.0 - ADAM_B1 ** ADAM_STEP)
        v_hat = nv / (1.0 - ADAM_B2 ** ADAM_STEP)
        d_ref[...] = -ADAM_LR * (m_hat / (jnp.sqrt(v_hat) + ADAM_EPS) + ADAM_WD * w_ref[...])
        nm_ref[...] = nm
        nv_ref[...] = nv

    return _call(body, name=name, grid=(rows // tr,), in_specs=[spec] * 4, out_specs=[spec] * 3,
                 out_shape=[jax.ShapeDtypeStruct((rows, cols), F32)] * 3, sem=("parallel",))(w, g, m, v)


WEIGHT_NAMES = ("mix_norm", "ab_w_in", "lru_conv_w", "lru_conv_b", "lru_wa", "lru_ba", "lru_wx", "lru_bx", "lru_lambda",
                "ab_w_out", "c_w_qkv", "c_b_qkv", "c_sinks", "c_w_out", "c_b_out", "xa_norm", "xa_mem_norm", "xa_wq",
                "xa_wkv", "xa_wo", "ffn_norm", "ffn_w_gate_up", "ffn_w_down", "final_norm")
BIG = ("ab_w_in", "lru_wa", "lru_wx", "ab_w_out", "c_w_qkv", "c_w_out", "xa_wq", "xa_wkv", "xa_wo", "ffn_w_gate_up",
       "ffn_w_down")
REPLICATED = ("mix_norm", "lru_conv_b", "lru_lambda", "c_sinks", "xa_norm", "xa_mem_norm", "ffn_norm", "final_norm")
SMALL_SHARDED = ("lru_conv_w", "lru_ba", "lru_bx", "c_b_qkv", "c_b_out")
LANES = 1024


def _rows(v):
    flat = v.reshape(-1)
    pad = -flat.shape[0] % LANES
    if pad:
        flat = jnp.concatenate([flat, jnp.zeros((pad,), flat.dtype)])
    return flat.reshape(-1, LANES)


def _stack_rows(parts, total):
    rows = jnp.concatenate(parts, axis=0)
    return jnp.concatenate([rows, jnp.zeros((total - rows.shape[0], LANES), rows.dtype)], axis=0)


def _unshard(name, t):
    if name in ("ab_w_in", "ab_w_out", "c_w_out"):
        return t if name == "ab_w_in" else t.reshape(1, -1, t.shape[-1])
    if name in ("lru_wa", "lru_wx"):
        return t.transpose(1, 0, 2, 3).reshape(LRU_HEADS, LRU_HEAD_DIM, LRU_HEAD_DIM)
    if name == "c_w_qkv":
        return t.transpose(1, 0, 2).reshape(1, D_MODEL, -1)
    if name in ("xa_wq", "xa_wkv", "ffn_w_down"):
        return t.transpose(1, 0, 2, 3).reshape(2, -1, t.shape[-1])
    return t.transpose(1, 0, 2, 3)


def _to_shards(name, g):
    if name in ("ab_w_in", "ab_w_out", "c_w_out"):
        return g.reshape(N_CHIPS, -1)
    if name in ("lru_wa", "lru_wx"):
        return g.reshape(LRU_HEADS, N_CHIPS, -1).transpose(1, 0, 2).reshape(N_CHIPS, -1)
    if name == "c_w_qkv":
        return g.reshape(D_MODEL, N_CHIPS, -1).transpose(1, 0, 2).reshape(N_CHIPS, -1)
    if name in ("xa_wq", "xa_wkv", "ffn_w_down"):
        return g.reshape(2, N_CHIPS, -1).transpose(1, 0, 2).reshape(N_CHIPS, -1)
    return g.transpose(1, 0, 2, 3).reshape(N_CHIPS, -1)


def kernel(x, mem, mix_norm, ab_w_in, lru_conv_w, lru_conv_b, lru_wa, lru_ba, lru_wx, lru_bx, lru_lambda, ab_w_out, c_w_qkv, c_b_qkv, c_sinks, c_w_out, c_b_out, xa_norm, xa_mem_norm, xa_wq, xa_wkv, xa_wo, ffn_norm, ffn_w_gate_up, ffn_w_down, final_norm, loss_target, m_mix_norm, m_ab_w_in, m_lru_conv_w, m_lru_conv_b, m_lru_wa, m_lru_ba, m_lru_wx, m_lru_bx, m_lru_lambda, m_ab_w_out, m_c_w_qkv, m_c_b_qkv, m_c_sinks, m_c_w_out, m_c_b_out, m_xa_norm, m_xa_mem_norm, m_xa_wq, m_xa_wkv, m_xa_wo, m_ffn_norm, m_ffn_w_gate_up, m_ffn_w_down, m_final_norm, v_mix_norm, v_ab_w_in, v_lru_conv_w, v_lru_conv_b, v_lru_wa, v_lru_ba, v_lru_wx, v_lru_bx, v_lru_lambda, v_ab_w_out, v_c_w_qkv, v_c_b_qkv, v_c_sinks, v_c_w_out, v_c_b_out, v_xa_norm, v_xa_mem_norm, v_xa_wq, v_xa_wkv, v_xa_wo, v_ffn_norm, v_ffn_w_gate_up, v_ffn_w_down, v_final_norm):
    given = dict(locals())
    wl = {n: given[n] for n in WEIGHT_NAMES}
    ml = {n: given["m_" + n] for n in WEIGHT_NAMES}
    vl = {n: given["v_" + n] for n in WEIGHT_NAMES}
    xi, yi, ci = lax.axis_index("x"), lax.axis_index("y"), lax.axis_index("c")
    chip = 2 * xi + yi

    big_rows = {n: wl[n].size // LANES for n in BIG}
    wpack = _stack_rows([wl[n].astype(BF16).reshape(big_rows[n], LANES) for n in BIG], PACK_ROWS)
    spack = _stack_rows([_rows(wl[n]) for n in SMALL_SHARDED], 8)
    wfull, sfull = _gather_weights(wpack, spack)
    w = {n: wl[n] for n in REPLICATED}
    w["c_sinks"] = wl["c_sinks"][0]
    off = 0
    for n in BIG:
        loc = wl[n].shape[1:] if wl[n].shape[0] == 1 else wl[n].shape
        w[n] = _unshard(n, wfull[:, off:off + big_rows[n]].reshape((N_CHIPS,) + loc))
        off += big_rows[n]
    for r, n in enumerate(SMALL_SHARDED):
        loc = wl[n].shape[1:]
        t = sfull[:, r, :wl[n].size].reshape((N_CHIPS,) + loc)
        if n == "lru_conv_w":
            w[n] = t.transpose(1, 0, 2).reshape(CONV_WIDTH, -1)
        elif n in ("lru_ba", "lru_bx"):
            w[n] = t.transpose(1, 0, 2).reshape(1, -1)
        else:
            w[n] = t.reshape(1, -1)

    loss_part, grad_x, g = _device_step(x[0], mem[0], loss_target[0], w)

    gparts = [_to_shards(n, g[n].astype(BF16)).reshape(N_CHIPS, big_rows[n], LANES) for n in BIG]
    gparts.append(jnp.zeros((N_CHIPS, PACK_ROWS - off, LANES), BF16))
    gpack = jnp.concatenate(gparts, axis=1)
    small_parts = [_rows(g[n]) for n in REPLICATED] + [_rows(jnp.broadcast_to(loss_part, (LANES,)))]
    small_parts += [_rows(g[n]) for n in SMALL_SHARDED]
    small = _stack_rows(small_parts, 24)
    place = jnp.stack([ci, chip]).astype(jnp.int32)
    ra = _rs_pair_exchange(gpack)
    pair, own = _rs_pair_add(place, gpack, ra)
    rb, rs = _rs_chip_exchange(pair, small)
    gsum = _rs_sibling_share(_rs_final_add(place, own, rb))
    ssum = _sum_slots(rs)

    grads = {}
    off = 0
    for n in BIG:
        grads[n] = gsum[off:off + big_rows[n]].reshape(wl[n].shape)
        off += big_rows[n]
    row = 0
    for n in REPLICATED:
        k = _rows(g[n]).shape[0]
        grads[n] = ssum[row:row + k].reshape(-1)[:wl[n].size].reshape(wl[n].shape)
        row += k
    loss = ssum[row, 0]
    row += 1
    for n in SMALL_SHARDED:
        k = _rows(g[n]).shape[0]
        full = ssum[row:row + k].reshape(-1)[:g[n].size]
        row += k
        loc = wl[n].shape
        if n == "lru_conv_w":
            sh = full.reshape(CONV_WIDTH, N_CHIPS, -1)
        elif n in ("lru_ba", "lru_bx"):
            sh = full.reshape(LRU_HEADS, N_CHIPS, -1)
        else:
            sh = full.reshape(1, N_CHIPS, -1)
        grads[n] = lax.dynamic_index_in_dim(sh, chip, axis=1, keepdims=False).reshape(loc)

    delta, new_m, new_v = {}, {}, {}
    for n in BIG:
        shape2 = (wl[n].size // wl[n].shape[-1], wl[n].shape[-1])
        d, nm, nv = _adamw(wl[n].reshape(shape2), grads[n].reshape(shape2), ml[n].reshape(shape2), vl[n].reshape(shape2),
                           name="adamw_" + n)
        delta[n], new_m[n], new_v[n] = (t.reshape(wl[n].shape) for t in (d, nm, nv))
    smalls = REPLICATED + SMALL_SHARDED
    packs = [_stack_rows([_rows(src[n]) for n in smalls], 24) for src in (wl, grads, ml, vl)]
    outs = _adamw(*packs, name="adamw_small")
    row = 0
    for n in smalls:
        k = _rows(wl[n]).shape[0]
        for dst, o in zip((delta, new_m, new_v), outs):
            dst[n] = o[row:row + k].reshape(-1)[:wl[n].size].reshape(wl[n].shape)
        row += k

    return (loss, grad_x[None], *[grads[n] for n in WEIGHT_NAMES], *[delta[n] for n in WEIGHT_NAMES],
            *[new_m[n] for n in WEIGHT_NAMES], *[new_v[n] for n in WEIGHT_NAMES])
```

```python
import jax
import jax.numpy as jnp
from jax import lax
from jax.experimental import pallas as pl
from jax.experimental.pallas import tpu as pltpu

F32, BF16 = jnp.float32, jnp.bfloat16
D_MODEL = 1024
NORM_EPS = 1e-6
ROPE_THETA = 500000.0
HEAD_DIM = 64
ROT_DIM = 16
BLK = 128
LRU_HEADS, LRU_HEAD_DIM, CONV_WIDTH, LRU_C = 4, 256, 4, 8.0
DILATED_PATTERN = ((128, 1), (512, 4), (2048, 16))
B_HEADS, C_HEADS, C_KV_HEADS, C_WINDOW = 8, 16, 2, 128
XA_HEADS, XA_HEAD_DIM, N_MEM = 4, 128, 256
D_FF = 2816
NEG = -1e30
ADAM_LR, ADAM_B1, ADAM_B2, ADAM_EPS, ADAM_WD, ADAM_STEP = 0.001, 0.9, 0.999, 1e-08, 0.01, 10
N_CHIPS = 4
PACK_ROWS = 7296
VMEM_LIMIT_V7X = 56 * 1024 * 1024

NN = (((1,), (0,)), ((), ()))
NT = (((1,), (1,)), ((), ()))
TN = (((0,), (0,)), ((), ()))


def _dot(a, b, dims=NN):
    return lax.dot_general(a, b, dims, preferred_element_type=F32)


def _sigmoid(x):
    return 1.0 / (1.0 + jnp.exp(-x))


def _call(body, *, name, grid, in_specs, out_specs, out_shape, scratch=(), sem=None):
    return pl.pallas_call(
        body, name=name, grid=grid, in_specs=in_specs, out_specs=out_specs, out_shape=out_shape,
        scratch_shapes=list(scratch),
        compiler_params=pltpu.CompilerParams(dimension_semantics=sem, vmem_limit_bytes=VMEM_LIMIT_V7X))


def _rope_tables(L):
    half = ROT_DIM // 2
    inv = ROPE_THETA ** (-jnp.arange(0, ROT_DIM, 2, dtype=F32) / ROT_DIM)
    ang = jnp.arange(L, dtype=F32)[:, None] * inv[None, :]
    cos, sin = jnp.cos(ang), jnp.sin(ang)
    rest = HEAD_DIM - ROT_DIM
    z8, zr, one = jnp.zeros((L, half), F32), jnp.zeros((L, rest), F32), jnp.ones((L, rest), F32)
    c = jnp.concatenate([cos, cos, one], axis=1)
    s1 = jnp.concatenate([-sin, z8, zr], axis=1)
    s2 = jnp.concatenate([z8, sin, zr], axis=1)
    return tuple(jnp.concatenate([t, t], axis=1) for t in (c, s1, s2))


def _rope_fwd(v, c, s1, s2):
    return v * c + pltpu.roll(v, 120, 1) * s1 + pltpu.roll(v, 8, 1) * s2


def _rope_bwd(dv, c, s1, s2):
    return dv * c + pltpu.roll(dv * s1, 8, 1) + pltpu.roll(dv * s2, 120, 1)


def _rowmm(a, w3, *, name, tm=256, gain=None, bias=None, res=None, swiglu=False, rope=None):
    M, K = a.shape
    S, _, Ns = w3.shape
    N = S * Ns
    tm = min(tm, M)
    has_norm, has_bias, has_res, has_rope = gain is not None, bias is not None, res is not None, rope is not None
    row = lambda w: pl.BlockSpec((tm, w), lambda i: (i, 0))
    whole = lambda shape: pl.BlockSpec(shape, lambda i: (0,) * len(shape))
    ins, specs = [a], [row(K)]
    if has_norm:
        ins.append(gain.reshape(1, K)); specs.append(whole((1, K)))
    ins.append(w3); specs.append(whole((S, K, Ns)))
    if has_bias:
        ins.append(bias.reshape(1, N)); specs.append(whole((1, N)))
    if has_res:
        ins.append(res); specs.append(row(N))
    if has_rope:
        ins += list(rope[2]); specs += [row(128)] * 3
    y_dtype = F32 if has_res else BF16
    out_shape, out_specs = [jax.ShapeDtypeStruct((M, N), y_dtype)], [row(N)]
    if has_norm:
        out_shape.append(jax.ShapeDtypeStruct((M, K), BF16)); out_specs.append(row(K))
    if swiglu:
        out_shape.append(jax.ShapeDtypeStruct((M, N // 2), BF16)); out_specs.append(row(N // 2))
    scratch = [pltpu.VMEM((tm, N), F32)] if has_rope else []

    def body(*refs):
        it = iter(refs)
        a_ref = next(it)
        g_ref = next(it) if has_norm else None
        w_ref = next(it)
        b_ref = next(it) if has_bias else None
        r_ref = next(it) if has_res else None
        tabs = [next(it) for _ in range(3)] if has_rope else None
        y_ref = next(it)
        n_ref = next(it) if has_norm else None
        act_ref = next(it) if swiglu else None
        ys_ref = next(it) if has_rope else None
        if has_norm:
            x = a_ref[...].astype(F32)
            ms = jnp.mean(x * x, axis=-1, keepdims=True)
            xb = (x * lax.rsqrt(ms + NORM_EPS) * g_ref[...]).astype(BF16)
            n_ref[...] = xb
        else:
            xb = a_ref[...].astype(BF16)
        if swiglu:
            for s in range(S // 2):
                g = _dot(xb, w_ref[s])
                u = _dot(xb, w_ref[s + S // 2])
                y_ref[:, s * Ns:(s + 1) * Ns] = g.astype(BF16)
                y_ref[:, N // 2 + s * Ns:N // 2 + (s + 1) * Ns] = u.astype(BF16)
                act_ref[:, s * Ns:(s + 1) * Ns] = (g * _sigmoid(g) * u).astype(BF16)
            return
        for s in range(S):
            sl = slice(s * Ns, (s + 1) * Ns)
            acc = _dot(xb, w_ref[s])
            if has_bias:
                acc = acc + b_ref[:, sl]
            if has_res:
                acc = acc + r_ref[:, sl]
            if has_rope:
                ys_ref[:, sl] = acc
            else:
                y_ref[:, sl] = acc.astype(y_dtype)
        if has_rope:
            c, s1, s2 = (t[...] for t in tabs)
            for cb in range(N // 128):
                cs = slice(cb * 128, (cb + 1) * 128)
                v = ys_ref[:, cs]
                if rope[0] <= cb * 128 < rope[1]:
                    v = _rope_fwd(v, c, s1, s2)
                y_ref[:, cs] = v.astype(BF16)

    return _call(body, name=name, grid=(M // tm,), in_specs=specs, out_specs=out_specs, out_shape=out_shape,
                 scratch=scratch, sem=("parallel",))(*ins)


def _mm_nt(dy, w3, *, name, mode, tm=256, kchunk=None, h=None, gain=None, dh=None, gu=None):
    M, N = dy.shape
    S, K, Ns = w3.shape
    kchunk = kchunk or K
    tm = min(tm, M)
    row = lambda w: pl.BlockSpec((tm, w), lambda i: (i, 0))
    whole = lambda shape: pl.BlockSpec(shape, lambda i: (0,) * len(shape))
    ins, specs = [dy, w3], [row(N), whole((S, K, Ns))]
    has_dh = dh is not None
    if mode == "norm":
        ins += [h, gain.reshape(1, K)]; specs += [row(K), whole((1, K))]
        if has_dh:
            ins.append(dh); specs.append(row(K))
        out_shape = [jax.ShapeDtypeStruct((M, K), F32), jax.ShapeDtypeStruct((1, K), F32)]
        out_specs = [row(K), whole((1, K))]
    elif mode == "swiglu":
        ins.append(gu); specs.append(row(2 * K))
        out_shape, out_specs = [jax.ShapeDtypeStruct((M, 2 * K), BF16)], [row(2 * K)]
    else:
        out_shape, out_specs = [jax.ShapeDtypeStruct((M, K), BF16)], [row(K)]

    def body(*refs):
        it = iter(refs)
        dy_ref, w_ref = next(it), next(it)
        if mode == "norm":
            h_ref, g_ref = next(it), next(it)
            dh_ref = next(it) if has_dh else None
            o_ref, dg_ref = next(it), next(it)
        elif mode == "swiglu":
            gu_ref, o_ref = next(it), next(it)
        else:
            o_ref = next(it)
        for kc in range(K // kchunk):
            ks = slice(kc * kchunk, (kc + 1) * kchunk)
            acc = None
            for s in range(S):
                t = _dot(dy_ref[:, s * Ns:(s + 1) * Ns].astype(BF16), w_ref[s, ks, :], NT)
                acc = t if acc is None else acc + t
            if mode == "plain":
                o_ref[:, ks] = acc.astype(BF16)
            elif mode == "swiglu":
                us = slice(K + kc * kchunk, K + (kc + 1) * kchunk)
                g = gu_ref[:, ks].astype(F32)
                u = gu_ref[:, us].astype(F32)
                sg = _sigmoid(g)
                o_ref[:, ks] = (acc * u * (sg * (1.0 + g * (1.0 - sg)))).astype(BF16)
                o_ref[:, us] = (acc * (g * sg)).astype(BF16)
            else:
                x = h_ref[...].astype(F32)
                r = lax.rsqrt(jnp.mean(x * x, axis=-1, keepdims=True) + NORM_EPS)
                xhat = x * r
                dxh = acc * g_ref[...]
                dx = r * (dxh - xhat * jnp.mean(dxh * xhat, axis=-1, keepdims=True))
                o_ref[...] = dx + dh_ref[...] if has_dh else dx

                @pl.when(pl.program_id(0) == 0)
                def _():
                    dg_ref[...] = jnp.zeros_like(dg_ref)

                dg_ref[...] += jnp.sum(acc * xhat, axis=0, keepdims=True)

    sem = ("arbitrary",) if mode == "norm" else ("parallel",)
    return _call(body, name=name, grid=(M // tm,), in_specs=specs, out_specs=out_specs, out_shape=out_shape, sem=sem)(*ins)


def _mm_tn(x, dy, *, S, name, tk=1024, kk=None, bias=False):
    M, K = x.shape
    N = dy.shape[1]
    Ns = N // S
    kk = kk or K
    tk = min(tk, M)
    nl = M // tk
    in_specs = [pl.BlockSpec((tk, kk), lambda s, kc, l: (l, kc)), pl.BlockSpec((tk, Ns), lambda s, kc, l: (l, s))]
    out_shape = [jax.ShapeDtypeStruct((S, K, Ns), BF16)]
    out_specs = [pl.BlockSpec((None, kk, Ns), lambda s, kc, l: (s, kc, 0))]
    if bias:
        out_shape.append(jax.ShapeDtypeStruct((1, N), F32))
        out_specs.append(pl.BlockSpec((1, Ns), lambda s, kc, l: (0, s)))

    def body(x_ref, dy_ref, o_ref, *rest):
        acc_ref = rest[-1]
        kc, l = pl.program_id(1), pl.program_id(2)

        @pl.when(l == 0)
        def _():
            acc_ref[...] = jnp.zeros_like(acc_ref)

        acc_ref[...] += _dot(x_ref[...].astype(BF16), dy_ref[...].astype(BF16), TN)
        if bias:
            b_ref = rest[0]

            @pl.when((kc == 0) & (l == 0))
            def _():
                b_ref[...] = jnp.zeros_like(b_ref)

            @pl.when(kc == 0)
            def _():
                b_ref[...] += jnp.sum(dy_ref[...].astype(F32), axis=0, keepdims=True)

        @pl.when(l == nl - 1)
        def _():
            o_ref[...] = acc_ref[...].astype(BF16)

    return _call(body, name=name, grid=(S, K // kk, nl), in_specs=in_specs, out_specs=out_specs, out_shape=out_shape,
                 scratch=[pltpu.VMEM((kk, Ns), F32)], sem=("arbitrary", "arbitrary", "arbitrary"))(x, dy)


def _band_bias(max_dist, has_prev):
    rows = lax.broadcasted_iota(jnp.int32, (BLK, 2 * BLK), 0)
    cols = lax.broadcasted_iota(jnp.int32, (BLK, 2 * BLK), 1)
    dist = rows - cols + BLK
    ok = (dist >= 0) & (dist <= max_dist) & ((cols >= BLK) | has_prev)
    return jnp.where(ok, 0.0, NEG)


Q_SCALE = HEAD_DIM ** -0.5


def _band_fwd(qa, ka, va, *, d, nq, nkv, qcol, kcol, vcol, max_dist, state=None, finalize=True, sinks=None, name):
    Lr = qa.shape[0]
    nb = Lr // BLK
    qw, kw, G = nq * HEAD_DIM, nkv * HEAD_DIM, nq // nkv
    cur = lambda colf, w: pl.BlockSpec((BLK, w), lambda r, i: (i, colf(r)))
    prv = lambda colf, w: pl.BlockSpec((BLK, w), lambda r, i: (jnp.maximum(i - 1, 0), colf(r)))
    st = pl.BlockSpec((BLK, qw), lambda r, i: (i, r))
    ins, specs = [qa, ka, ka, va, va], [cur(qcol, qw), cur(kcol, kw), prv(kcol, kw), cur(vcol, kw), prv(vcol, kw)]
    has_state, has_sinks = state is not None, sinks is not None
    if has_state:
        ins += list(state); specs += [st] * 3
    if has_sinks:
        ins.append(sinks); specs.append(pl.BlockSpec(memory_space=pltpu.SMEM))
    if finalize:
        out_shape = [jax.ShapeDtypeStruct((Lr, d * qw), BF16), jax.ShapeDtypeStruct((Lr, d * qw), F32)]
    else:
        out_shape = [jax.ShapeDtypeStruct((Lr, d * qw), F32)] * 3
    out_specs = [st] * len(out_shape)

    def body(*refs):
        it = iter(refs)
        q_ref, kc_ref, kp_ref, vc_ref, vp_ref = (next(it) for _ in range(5))
        m_in, l_in, a_in = (next(it) for _ in range(3)) if has_state else (None,) * 3
        sk_ref = next(it) if has_sinks else None
        outs = list(it)
        bias = _band_bias(max_dist, pl.program_id(1) > 0)
        k2 = jnp.concatenate([kp_ref[...], kc_ref[...]], axis=0)
        v2 = jnp.concatenate([vp_ref[...], vc_ref[...]], axis=0)
        for h in range(nq):
            hs = slice(h * HEAD_DIM, (h + 1) * HEAD_DIM)
            ks = slice((h // G) * HEAD_DIM, (h // G + 1) * HEAD_DIM)
            s = _dot(q_ref[:, hs] * jnp.asarray(Q_SCALE, BF16), k2[:, ks], NT) + bias
            m = jnp.max(s, axis=-1, keepdims=True)
            if has_state:
                m_prev = m_in[:, h * HEAD_DIM:h * HEAD_DIM + 1]
                m = jnp.maximum(m, m_prev)
            if has_sinks:
                m = jnp.maximum(m, sk_ref[h])
            p = jnp.exp(s - m)
            l = jnp.sum(p, axis=-1, keepdims=True)
            acc = _dot(p.astype(BF16), v2[:, ks])
            if has_state:
                alpha = jnp.exp(m_prev - m)
                l = l + alpha * l_in[:, h * HEAD_DIM:h * HEAD_DIM + 1]
                acc = acc + alpha * a_in[:, hs]
            if has_sinks:
                l = l + jnp.exp(sk_ref[h] - m)
            if finalize:
                outs[0][:, hs] = (acc / l).astype(BF16)
                outs[1][:, hs] = jnp.broadcast_to(m + jnp.log(l), (BLK, HEAD_DIM))
            else:
                outs[0][:, hs] = jnp.broadcast_to(m, (BLK, HEAD_DIM))
                outs[1][:, hs] = jnp.broadcast_to(l, (BLK, HEAD_DIM))
                outs[2][:, hs] = acc

    return _call(body, name=name, grid=(d, nb), in_specs=specs, out_specs=out_specs, out_shape=out_shape,
                 sem=("parallel", "parallel"))(*ins)


def _band_bwd(qa, ka, va, doa, oa, lsea, *, d, nq, nkv, qcol, kcol, vcol, docol, max_dist, sinks=None, name):
    Lr = qa.shape[0]
    nb = Lr // BLK
    qw, kw, G = nq * HEAD_DIM, nkv * HEAD_DIM, nq // nkv
    last = lambda i: jnp.minimum(i, nb - 1)
    cur = lambda colf, w: pl.BlockSpec((BLK, w), lambda r, i: (last(i), colf(r)))
    prv = lambda colf, w: pl.BlockSpec((BLK, w), lambda r, i: (jnp.maximum(last(i) - 1, 0), colf(r)))
    own = lambda r: r
    ins = [qa, ka, ka, va, va, doa, oa, lsea]
    specs = [cur(qcol, qw), cur(kcol, kw), prv(kcol, kw), cur(vcol, kw), prv(vcol, kw), cur(docol, qw), cur(own, qw),
             cur(own, qw)]
    has_sinks = sinks is not None
    if has_sinks:
        ins.append(sinks); specs.append(pl.BlockSpec(memory_space=pltpu.SMEM))
    out_shape = [jax.ShapeDtypeStruct((Lr, d * qw), F32), jax.ShapeDtypeStruct((Lr, d * kw), F32),
                 jax.ShapeDtypeStruct((Lr, d * kw), F32)]
    behind = lambda r, i: (jnp.maximum(i - 1, 0), r)
    out_specs = [pl.BlockSpec((BLK, qw), lambda r, i: (last(i), r)), pl.BlockSpec((BLK, kw), behind),
                 pl.BlockSpec((BLK, kw), behind)]
    if has_sinks:
        out_shape.append(jax.ShapeDtypeStruct((8, 128), F32))
        out_specs.append(pl.BlockSpec((8, 128), lambda r, i: (0, 0)))

    def body(*refs):
        it = iter(refs)
        q_ref, kc_ref, kp_ref, vc_ref, vp_ref, do_ref, o_ref, ls_ref = (next(it) for _ in range(8))
        sk_ref = next(it) if has_sinks else None
        dq_ref, dk_ref, dv_ref = next(it), next(it), next(it)
        dsk_ref = next(it) if has_sinks else None
        dk_car, dv_car = next(it), next(it)
        r_id, i = pl.program_id(0), pl.program_id(1)

        @pl.when(i == 0)
        def _():
            dk_car[...] = jnp.zeros_like(dk_car)
            dv_car[...] = jnp.zeros_like(dv_car)

        if has_sinks:
            @pl.when((r_id == 0) & (i == 0))
            def _():
                dsk_ref[...] = jnp.zeros_like(dsk_ref)

        @pl.when(i == nb)
        def _():
            dk_ref[...] = dk_car[...]
            dv_ref[...] = dv_car[...]

        @pl.when(i < nb)
        def _():
            bias = _band_bias(max_dist, i > 0)
            k2 = jnp.concatenate([kp_ref[...], kc_ref[...]], axis=0)
            v2 = jnp.concatenate([vp_ref[...], vc_ref[...]], axis=0)
            if has_sinks:
                lane = lax.broadcasted_iota(jnp.int32, (8, 128), 1)
                dsk = jnp.zeros((8, 128), F32)
            for kv in range(nkv):
                ks = slice(kv * HEAD_DIM, (kv + 1) * HEAD_DIM)
                kh, vh = k2[:, ks], v2[:, ks]
                dk = jnp.zeros((2 * BLK, HEAD_DIM), F32)
                dv = jnp.zeros((2 * BLK, HEAD_DIM), F32)
                for g in range(G):
                    h = kv * G + g
                    hs = slice(h * HEAD_DIM, (h + 1) * HEAD_DIM)
                    q = q_ref[:, hs] * jnp.asarray(Q_SCALE, BF16)
                    do = do_ref[:, hs]
                    lse = ls_ref[:, h * HEAD_DIM:h * HEAD_DIM + 1]
                    dl = jnp.sum(do.astype(F32) * o_ref[:, hs].astype(F32), axis=-1, keepdims=True)
                    p = jnp.exp(_dot(q, kh, NT) + bias - lse)
                    ds = (p * (_dot(do, vh, NT) - dl)).astype(BF16)
                    dq_ref[:, hs] = _dot(ds, kh) * Q_SCALE
                    dk = dk + _dot(ds, q, TN)
                    dv = dv + _dot(p.astype(BF16), do, TN)
                    if has_sinks:
                        val = -jnp.sum(jnp.exp(sk_ref[h] - lse) * dl, axis=0, keepdims=True)
                        dsk = dsk + jnp.where(lane == h, val, 0.0)
                dk_ref[:, ks] = dk_car[:, ks] + dk[:BLK]
                dv_ref[:, ks] = dv_car[:, ks] + dv[:BLK]
                dk_car[:, ks] = dk[BLK:]
                dv_car[:, ks] = dv[BLK:]
            if has_sinks:
                dsk_ref[...] += dsk

    return _call(body, name=name, grid=(d, nb + 1), in_specs=specs, out_specs=out_specs, out_shape=out_shape,
                 scratch=[pltpu.VMEM((BLK, kw), F32), pltpu.VMEM((BLK, kw), F32)], sem=("arbitrary", "arbitrary"))(*ins)


def _attn_grad_combine(branches, tabs, *, name, tm=256):
    L, qw = branches[0][0].shape
    kw = branches[0][1].shape[1]
    nbr = len(branches)
    row = lambda w: pl.BlockSpec((tm, w), lambda i: (i, 0))
    ins, specs = [], []
    for dq, dk, dv in branches:
        ins += [dq, dk, dv]; specs += [row(qw), row(kw), row(kw)]
    ins += list(tabs); specs += [row(128)] * 3

    def body(*refs):
        c, s1, s2 = (t[...] for t in refs[3 * nbr:3 * nbr + 3])
        o_ref = refs[-1]
        for part, (w, off, rot) in enumerate(((qw, 0, True), (kw, qw, True), (kw, qw + kw, False))):
            for cb in range(w // 128):
                cs = slice(cb * 128, (cb + 1) * 128)
                v = refs[part][:, cs]
                for b in range(1, nbr):
                    v = v + refs[3 * b + part][:, cs]
                if rot:
                    v = _rope_bwd(v, c, s1, s2)
                o_ref[:, off + cb * 128:off + (cb + 1) * 128] = v.astype(BF16)

    return _call(body, name=name, grid=(L // tm,), in_specs=specs, out_specs=row(qw + 2 * kw),
                 out_shape=jax.ShapeDtypeStruct((L, qw + 2 * kw), BF16), sem=("parallel",))(*ins)


def _xattn_fwd(q, kv, *, name, tq=512):
    L, W = q.shape
    scale = XA_HEAD_DIM ** -0.5
    row = pl.BlockSpec((tq, W), lambda i: (i, 0))
    kvs = pl.BlockSpec((N_MEM, 2 * W), lambda i: (0, 0))

    def body(q_ref, kv_ref, o_ref, lse_ref):
        for h in range(XA_HEADS):
            hs = slice(h * XA_HEAD_DIM, (h + 1) * XA_HEAD_DIM)
            vs = slice(W + h * XA_HEAD_DIM, W + (h + 1) * XA_HEAD_DIM)
            s = _dot(q_ref[:, hs], kv_ref[:, hs], NT) * scale
            m = jnp.max(s, axis=-1, keepdims=True)
            p = jnp.exp(s - m)
            l = jnp.sum(p, axis=-1, keepdims=True)
            o_ref[:, hs] = (_dot(p.astype(BF16), kv_ref[:, vs]) / l).astype(BF16)
            lse_ref[:, hs] = jnp.broadcast_to(m + jnp.log(l), (tq, XA_HEAD_DIM))

    return _call(body, name=name, grid=(L // tq,), in_specs=[row, kvs], out_specs=[row, row],
                 out_shape=[jax.ShapeDtypeStruct((L, W), BF16), jax.ShapeDtypeStruct((L, W), F32)], sem=("parallel",))(q, kv)


def _xattn_bwd(q, kv, o, lse, do, *, name, tq=512):
    L, W = q.shape
    scale = XA_HEAD_DIM ** -0.5
    row = pl.BlockSpec((tq, W), lambda i: (i, 0))
    kvs = pl.BlockSpec((N_MEM, 2 * W), lambda i: (0, 0))

    def body(q_ref, kv_ref, o_ref, lse_ref, do_ref, dq_ref, dkv_ref):
        @pl.when(pl.program_id(0) == 0)
        def _():
            dkv_ref[...] = jnp.zeros_like(dkv_ref)

        for h in range(XA_HEADS):
            hs = slice(h * XA_HEAD_DIM, (h + 1) * XA_HEAD_DIM)
            vs = slice(W + h * XA_HEAD_DIM, W + (h + 1) * XA_HEAD_DIM)
            qh, kh, vh, doh = q_ref[:, hs], kv_ref[:, hs], kv_ref[:, vs], do_ref[:, hs]
            p = jnp.exp(_dot(qh, kh, NT) * scale - lse_ref[:, h * XA_HEAD_DIM:h * XA_HEAD_DIM + 1])
            dl = jnp.sum(doh.astype(F32) * o_ref[:, hs].astype(F32), axis=-1, keepdims=True)
            ds = (p * (_dot(doh, vh, NT) - dl) * scale).astype(BF16)
            dq_ref[:, hs] = _dot(ds, kh).astype(BF16)
            dkv_ref[:, hs] += _dot(ds, qh, TN)
            dkv_ref[:, vs] += _dot(p.astype(BF16), doh, TN)

    return _call(body, name=name, grid=(L // tq,), in_specs=[row, kvs, row, row, row], out_specs=[row, kvs],
                 out_shape=[jax.ShapeDtypeStruct((L, W), BF16), jax.ShapeDtypeStruct((N_MEM, 2 * W), F32)],
                 sem=("arbitrary",))(q, kv, o, lse, do)


def _neg_expm1(z):
    series = -(z * (1.0 + z * (0.5 + z * (1.0 / 6.0 + z * (1.0 / 24.0 + z * (1.0 / 120.0))))))
    return jnp.where(z > -0.05, series, 1.0 - jnp.exp(z))


def _softplus(z):
    return jnp.maximum(z, 0.0) + jnp.log(1.0 + jnp.exp(-jnp.abs(z)))


def _gelu_parts(y):
    c = 0.7978845608028654
    t = jnp.tanh(c * (y + 0.044715 * y * y * y))
    gy = 0.5 * y * (1.0 + t)
    dgy = 0.5 * (1.0 + t) + 0.5 * y * (1.0 - t * t) * c * (1.0 + 3.0 * 0.044715 * y * y)
    return gy, dgy


def _lru_gates(xc, wa_ref, ba, wx_ref, bx, sp):
    rs, igs = [], []
    for hd in range(LRU_HEADS):
        sl = slice(hd * LRU_HEAD_DIM, (hd + 1) * LRU_HEAD_DIM)
        xh = xc[:, sl].astype(BF16)
        rs.append(_sigmoid(_dot(xh, wa_ref[hd]) + ba[:, sl]))
        igs.append(_sigmoid(_dot(xh, wx_ref[hd]) + bx[:, sl]))
    r, ig = jnp.concatenate(rs, axis=1), jnp.concatenate(igs, axis=1)
    la = -LRU_C * r * sp
    return r, ig, jnp.exp(la), _neg_expm1(2.0 * la)


def _conv_taps(x_ext, halo):
    n = x_ext.shape[0]
    return [x_ext[halo:] if k == CONV_WIDTH - 1 else pltpu.roll(x_ext, CONV_WIDTH - 1 - k, 0)[halo:]
            for k in range(CONV_WIDTH)]


def _lru_fwd(proj, cw, cb, wa, ba, wx, bx, lam, *, name, tc=512):
    L = proj.shape[0]
    W = LRU_HEADS * LRU_HEAD_DIM
    nb = L // tc
    whole = lambda shape: pl.BlockSpec(shape, lambda i: (0,) * len(shape))
    specs = [pl.BlockSpec((tc, W), lambda i: (i, 0)), pl.BlockSpec((tc, W), lambda i: (i, 1)),
             pl.BlockSpec((16, W), lambda i: (jnp.maximum(i * (tc // 16) - 1, 0), 0)),
             whole((CONV_WIDTH, W)), whole((1, W)), whole((LRU_HEADS, LRU_HEAD_DIM, LRU_HEAD_DIM)), whole((1, W)),
             whole((LRU_HEADS, LRU_HEAD_DIM, LRU_HEAD_DIM)), whole((1, W)), whole((1, W))]
    out_specs = [pl.BlockSpec((tc, W), lambda i: (i, 0))] * 2
    out_shape = [jax.ShapeDtypeStruct((L, W), BF16), jax.ShapeDtypeStruct((L, W), F32)]

    def body(x_ref, y_ref, xh_ref, cw_ref, cb_ref, wa_ref, ba_ref, wx_ref, bx_ref, lam_ref, rec_ref, hs_ref,
             hcar, a_scr, b_scr):
        i = pl.program_id(0)

        @pl.when(i == 0)
        def _():
            hcar[...] = jnp.zeros_like(hcar)

        halo = jnp.where(i > 0, xh_ref[...].astype(F32), 0.0)
        taps = _conv_taps(jnp.concatenate([halo, x_ref[...].astype(F32)], axis=0), 16)
        xc = cb_ref[...] + sum(cw_ref[k:k + 1, :] * taps[k] for k in range(CONV_WIDTH))
        _, ig, a, om = _lru_gates(xc, wa_ref, ba_ref[...], wx_ref, bx_ref[...], _softplus(-lam_ref[...]))
        b = jnp.sqrt(om) * (ig * xc)
        rowmod = lax.broadcasted_iota(jnp.int32, (tc, W), 0) & 7
        for s in (1, 2, 4):
            keep = rowmod >= s
            b = jnp.where(keep, a * pltpu.roll(b, s, 0) + b, b)
            a = jnp.where(keep, a * pltpu.roll(a, s, 0), a)
        a_scr[...] = a
        b_scr[...] = b

        def tile(j, hc):
            rows = pl.ds(pl.multiple_of(j * 8, 8), 8)
            ht = a_scr[rows, :] * hc + b_scr[rows, :]
            hs_ref[rows, :] = ht
            return jnp.broadcast_to(ht[7:8, :], (8, W))

        hcar[...] = lax.fori_loop(0, tc // 8, tile, hcar[...])
        gy, _ = _gelu_parts(y_ref[...].astype(F32))
        rec_ref[...] = (hs_ref[...] * gy).astype(BF16)

    return _call(body, name=name, grid=(nb,), in_specs=specs, out_specs=out_specs, out_shape=out_shape,
                 scratch=[pltpu.VMEM((8, W), F32), pltpu.VMEM((tc, W), F32), pltpu.VMEM((tc, W), F32)],
                 sem=("arbitrary",))(proj, proj, proj, cw, cb, wa, ba, wx, bx, lam)


def _lru_bwd(proj, hs, drec_src, cw, cb, wa, ba, wx, bx, lam, *, name, tc=256):
    L = proj.shape[0]
    W = LRU_HEADS * LRU_HEAD_DIM
    nb = L // tc
    tb = lambda i: nb - 1 - i
    whole = lambda shape: pl.BlockSpec(shape, lambda i: (0,) * len(shape))
    gate_w = (LRU_HEADS, LRU_HEAD_DIM, LRU_HEAD_DIM)
    specs = [pl.BlockSpec((tc, W), lambda i: (tb(i), 0)), pl.BlockSpec((tc, W), lambda i: (tb(i), 1)),
             pl.BlockSpec((16, W), lambda i: (jnp.maximum(tb(i) * (tc // 16) - 1, 0), 0)),
             pl.BlockSpec((tc, W), lambda i: (tb(i), 0)),
             pl.BlockSpec((8, W), lambda i: (jnp.maximum(tb(i) * (tc // 8) - 1, 0), 0)),
             pl.BlockSpec((tc, W), lambda i: (tb(i), 0)),
             whole((CONV_WIDTH, W)), whole((1, W)), whole(gate_w), whole((1, W)), whole(gate_w), whole((1, W)), whole((1, W))]
    out_specs = [pl.BlockSpec((tc, 2 * W), lambda i: (tb(i), 0)), whole((CONV_WIDTH, W)), whole((1, W)), whole(gate_w),
                 whole((1, W)), whole(gate_w), whole((1, W)), whole((1, W))]
    vec = jax.ShapeDtypeStruct((1, W), F32)
    out_shape = [jax.ShapeDtypeStruct((L, 2 * W), BF16), jax.ShapeDtypeStruct((CONV_WIDTH, W), F32), vec,
                 jax.ShapeDtypeStruct(gate_w, F32), vec, jax.ShapeDtypeStruct(gate_w, F32), vec, vec]

    def body(x_ref, y_ref, xh_ref, hs_ref, hh_ref, dr_ref, cw_ref, cb_ref, wa_ref, ba_ref, wx_ref, bx_ref, lam_ref,
             dxy_ref, dcw_ref, dcb_ref, dwa_ref, dba_ref, dwx_ref, dbx_ref, dlam_ref, gcar, dxc_car, a_scr, b_scr, g_scr):
        pid = pl.program_id(0)
        t = tb(pid)
        accs = (dcw_ref, dcb_ref, dwa_ref, dba_ref, dwx_ref, dbx_ref, dlam_ref)

        @pl.when(pid == 0)
        def _():
            gcar[...] = jnp.zeros_like(gcar)
            dxc_car[...] = jnp.zeros_like(dxc_car)
            for r in accs:
                r[...] = jnp.zeros_like(r)

        halo = jnp.where(t > 0, xh_ref[...].astype(F32), 0.0)
        taps = _conv_taps(jnp.concatenate([halo, x_ref[...].astype(F32)], axis=0), 16)
        xc = cb_ref[...] + sum(cw_ref[k:k + 1, :] * taps[k] for k in range(CONV_WIDTH))
        lam = lam_ref[...]
        sp = _softplus(-lam)
        r, ig, a, om = _lru_gates(xc, wa_ref, ba_ref[...], wx_ref, bx_ref[...], sp)
        sq = jnp.sqrt(om)
        hblk = hs_ref[...]
        hprev = pltpu.roll(jnp.concatenate([jnp.where(t > 0, hh_ref[...], 0.0), hblk], axis=0), 1, 0)[8:]
        gy, dgy = _gelu_parts(y_ref[...].astype(F32))
        drec = dr_ref[...].astype(F32)
        dxy_ref[:, W:] = (drec * hblk * dgy).astype(BF16)

        rowidx = lax.broadcasted_iota(jnp.int32, (tc, W), 0)
        rowmod = rowidx & 7
        ca = jnp.where(rowidx == tc - 1, 1.0, pltpu.roll(a, tc - 1, 0))
        cbv = drec * gy
        for s in (1, 2, 4):
            keep = rowmod < 8 - s
            cbv = jnp.where(keep, ca * pltpu.roll(cbv, tc - s, 0) + cbv, cbv)
            ca = jnp.where(keep, ca * pltpu.roll(ca, tc - s, 0), ca)
        a_scr[...] = ca
        b_scr[...] = cbv

        def tile(k, gc):
            j = tc // 8 - 1 - k
            rows = pl.ds(pl.multiple_of(j * 8, 8), 8)
            gt = a_scr[rows, :] * gc + b_scr[rows, :]
            g_scr[rows, :] = gt
            return jnp.broadcast_to(gt[0:1, :], (8, W))

        lax.fori_loop(0, tc // 8, tile, gcar[...])
        G = g_scr[...]
        gcar[...] = jnp.broadcast_to(a[0:1, :] * G[0:1, :], (8, W))

        da = G * hprev
        dsq = G * (ig * xc)
        di = G * (sq * xc)
        dxc = G * (sq * ig)
        dla = da * a - 2.0 * a * a * (dsq * 0.5 * lax.rsqrt(om))
        dlam_ref[...] += jnp.sum(dla * (-LRU_C * r), axis=0, keepdims=True) * (-_sigmoid(-lam))
        dpr = dla * (-LRU_C * sp) * r * (1.0 - r)
        dpi = di * ig * (1.0 - ig)
        dba_ref[...] += jnp.sum(dpr, axis=0, keepdims=True)
        dbx_ref[...] += jnp.sum(dpi, axis=0, keepdims=True)
        back = []
        for hd in range(LRU_HEADS):
            sl = slice(hd * LRU_HEAD_DIM, (hd + 1) * LRU_HEAD_DIM)
            xh, dprh, dpih = xc[:, sl].astype(BF16), dpr[:, sl].astype(BF16), dpi[:, sl].astype(BF16)
            back.append(_dot(dprh, wa_ref[hd], NT) + _dot(dpih, wx_ref[hd], NT))
            dwa_ref[hd] += _dot(xh, dprh, TN)
            dwx_ref[hd] += _dot(xh, dpih, TN)
        dxc = dxc + jnp.concatenate(back, axis=1)
        dcb_ref[...] += jnp.sum(dxc, axis=0, keepdims=True)
        for k in range(CONV_WIDTH):
            dcw_ref[k:k + 1, :] += jnp.sum(dxc * taps[k], axis=0, keepdims=True)
        ext = jnp.concatenate([dxc, dxc_car[...]], axis=0)
        dx = cw_ref[CONV_WIDTH - 1:CONV_WIDTH, :] * dxc
        for k in range(CONV_WIDTH - 1):
            dx = dx + cw_ref[k:k + 1, :] * pltpu.roll(ext, tc + 8 - (CONV_WIDTH - 1 - k), 0)[:tc]
        dxc_car[...] = dxc[0:8, :]
        dxy_ref[:, :W] = dx.astype(BF16)

    scratch = [pltpu.VMEM((8, W), F32), pltpu.VMEM((8, W), F32)] + [pltpu.VMEM((tc, W), F32)] * 3
    return _call(body, name=name, grid=(nb,), in_specs=specs, out_specs=out_specs, out_shape=out_shape, scratch=scratch,
                 sem=("arbitrary",))(proj, proj, proj, hs, hs, drec_src, cw, cb, wa, ba, wx, bx, lam)


def _final_loss(h, gain, target, *, name, tm=256):
    M, K = h.shape
    row = pl.BlockSpec((tm, K), lambda i: (i, 0))
    vec = pl.BlockSpec((1, K), lambda i: (0, 0))
    one = pl.BlockSpec((1, 128), lambda i: (0, 0))

    def body(h_ref, g_ref, t_ref, dh_ref, dg_ref, loss_ref):
        @pl.when(pl.program_id(0) == 0)
        def _():
            dg_ref[...] = jnp.zeros_like(dg_ref)
            loss_ref[...] = jnp.zeros_like(loss_ref)

        x = h_ref[...]
        r = lax.rsqrt(jnp.mean(x * x, axis=-1, keepdims=True) + NORM_EPS)
        xhat = x * r
        err = xhat * g_ref[...] - t_ref[...]
        loss_ref[...] += 0.5 / K * jnp.sum(err * err)
        dy = err * (1.0 / K)
        dg_ref[...] += jnp.sum(dy * xhat, axis=0, keepdims=True)
        dxh = dy * g_ref[...]
        dh_ref[...] = r * (dxh - xhat * jnp.mean(dxh * xhat, axis=-1, keepdims=True))

    return _call(body, name=name, grid=(M // tm,), in_specs=[row, vec, row], out_specs=[row, vec, one],
                 out_shape=[jax.ShapeDtypeStruct((M, K), F32), jax.ShapeDtypeStruct((1, K), F32),
                            jax.ShapeDtypeStruct((1, 128), F32)], sem=("arbitrary",))(h, gain.reshape(1, K), target)


def _dilated_fwd(proj0):
    L = proj0.shape[0]
    qkv = proj0[:, 2 * D_MODEL:]
    W3 = 3 * B_HEADS * HEAD_DIM
    state = None
    for bi, (window, d) in enumerate(DILATED_PATTERN):
        view = qkv.reshape(L // d, d * W3)
        sv = None if state is None else [s.reshape(L // d, d * B_HEADS * HEAD_DIM) for s in state]
        outs = _band_fwd(view, view, view, d=d, nq=B_HEADS, nkv=B_HEADS, qcol=lambda r: 3 * r, kcol=lambda r: 3 * r + 1,
                         vcol=lambda r: 3 * r + 2, max_dist=window // d, state=sv,
                         finalize=bi == len(DILATED_PATTERN) - 1, name=f"dilated_fwd_d{d}")
        state = [o.reshape(L, B_HEADS * HEAD_DIM) for o in outs]
    return state


def _dilated_bwd(proj0, att, lse, datt, tabs):
    L = proj0.shape[0]
    qkv = proj0[:, 2 * D_MODEL:]
    Wh = B_HEADS * HEAD_DIM
    branches = []
    for window, d in DILATED_PATTERN:
        view = qkv.reshape(L // d, d * 3 * Wh)
        v1 = lambda t: t.reshape(L // d, d * Wh)
        outs = _band_bwd(view, view, view, v1(datt), v1(att), v1(lse), d=d, nq=B_HEADS, nkv=B_HEADS,
                         qcol=lambda r: 3 * r, kcol=lambda r: 3 * r + 1, vcol=lambda r: 3 * r + 2, docol=lambda r: r,
                         max_dist=window // d, name=f"dilated_bwd_d{d}")
        branches.append([o.reshape(L, Wh) for o in outs])
    return _attn_grad_combine(branches, tabs, name="dilated_grad_combine")


def _device_step(x, mem, target, w):
    L = x.shape[0]
    tabs = _rope_tables(L)
    g = {}
    saved = []
    h = x
    for layer in range(2):
        sv = {"h_mix": h}
        if layer == 0:
            proj, n = _rowmm(h, w["ab_w_in"], name="l0_in_proj", gain=w["mix_norm"][0],
                             rope=(2 * D_MODEL, 2 * D_MODEL + 2 * B_HEADS * HEAD_DIM, tabs))
            rec, hs = _lru_fwd(proj, w["lru_conv_w"], w["lru_conv_b"], w["lru_wa"], w["lru_ba"], w["lru_wx"], w["lru_bx"],
                               w["lru_lambda"], name="lru_fwd")
            att, lse = _dilated_fwd(proj)
            mix = jnp.concatenate([rec, att], axis=1)
            (h,) = _rowmm(mix, w["ab_w_out"], name="l0_out_proj", res=h)
            sv.update(hs=hs)
        else:
            proj, n = _rowmm(h, w["c_w_qkv"], name="l1_qkv_proj", gain=w["mix_norm"][1], bias=w["c_b_qkv"],
                             rope=(0, (C_HEADS + C_KV_HEADS) * HEAD_DIM, tabs))
            mix, lse = _band_fwd(proj, proj, proj, d=1, nq=C_HEADS, nkv=C_KV_HEADS, qcol=lambda r: 0, kcol=lambda r: 8,
                                 vcol=lambda r: 9, max_dist=C_WINDOW - 1, sinks=w["c_sinks"], name="swa_fwd")
            (h,) = _rowmm(mix, w["c_w_out"], name="l1_out_proj", res=h, bias=w["c_b_out"])
        sv.update(proj=proj, n_mix=n, mix=mix, lse=lse, h_xa=h)
        xq, nx = _rowmm(h, w["xa_wq"][layer][None], name=f"xa_q_proj{layer}", gain=w["xa_norm"][layer])
        kv, nm = _rowmm(mem, w["xa_wkv"][layer][None], name=f"xa_kv_proj{layer}", gain=w["xa_mem_norm"][layer])
        xo, xlse = _xattn_fwd(xq, kv, name=f"xa_fwd{layer}")
        (h,) = _rowmm(xo, w["xa_wo"][layer], name=f"xa_out_proj{layer}", res=h)
        sv.update(xq=xq, nx=nx, kv=kv, nm=nm, xo=xo, xlse=xlse, h_ffn=h)
        gu, nf, act = _rowmm(h, w["ffn_w_gate_up"][layer], name=f"ffn_in{layer}", gain=w["ffn_norm"][layer], swiglu=True)
        (h,) = _rowmm(act, w["ffn_w_down"][layer][None], name=f"ffn_out{layer}", res=h, tm=512)
        sv.update(gu=gu, nf=nf, act=act)
        saved.append(sv)

    dh, g["final_norm"], loss = _final_loss(h, w["final_norm"], target, name="final_loss")

    stk = {k: [None, None] for k in ("xa_norm", "xa_mem_norm", "ffn_norm", "mix_norm", "xa_wq", "xa_wkv", "xa_wo",
                                      "ffn_w_gate_up", "ffn_w_down")}
    for layer in (1, 0):
        sv = saved[layer]
        (stk["ffn_w_down"][layer],) = _mm_tn(sv["act"], dh, S=1, name=f"ffn_down_dw{layer}", kk=D_FF // 2)
        (dgu,) = _mm_nt(dh, w["ffn_w_down"][layer][None], name=f"ffn_dact{layer}", mode="swiglu", kchunk=D_FF // 2, gu=sv["gu"])
        (stk["ffn_w_gate_up"][layer],) = _mm_tn(sv["nf"], dgu, S=N_CHIPS, name=f"ffn_gu_dw{layer}")
        dh, stk["ffn_norm"][layer] = _mm_nt(dgu, w["ffn_w_gate_up"][layer], name=f"ffn_dx{layer}", mode="norm",
                                            h=sv["h_ffn"], gain=w["ffn_norm"][layer], dh=dh)
        (stk["xa_wo"][layer],) = _mm_tn(sv["xo"], dh, S=N_CHIPS, name=f"xa_wo_dw{layer}")
        (dxo,) = _mm_nt(dh, w["xa_wo"][layer], name=f"xa_dxo{layer}", mode="plain")
        dxq, dkv = _xattn_bwd(sv["xq"], sv["kv"], sv["xo"], sv["xlse"], dxo, name=f"xa_bwd{layer}")
        (stk["xa_wq"][layer],) = _mm_tn(sv["nx"], dxq, S=1, name=f"xa_wq_dw{layer}")
        dh, stk["xa_norm"][layer] = _mm_nt(dxq, w["xa_wq"][layer][None], name=f"xa_dx{layer}", mode="norm", h=sv["h_xa"],
                                           gain=w["xa_norm"][layer], dh=dh)
        (stk["xa_wkv"][layer],) = _mm_tn(sv["nm"], dkv, S=1, name=f"xa_wkv_dw{layer}")
        _, stk["xa_mem_norm"][layer] = _mm_nt(dkv, w["xa_wkv"][layer][None], name=f"xa_dmem{layer}", mode="norm", h=mem,
                                              gain=w["xa_mem_norm"][layer])
        if layer == 1:
            g["c_w_out"], g["c_b_out"] = _mm_tn(sv["mix"], dh, S=1, name="l1_out_dw", bias=True)
            (dmix,) = _mm_nt(dh, w["c_w_out"], name="l1_dmix", mode="plain")
            dq, dk, dv, dsk = _band_bwd(sv["proj"], sv["proj"], sv["proj"], dmix, sv["mix"], sv["lse"], d=1, nq=C_HEADS,
                                        nkv=C_KV_HEADS, qcol=lambda r: 0, kcol=lambda r: 8, vcol=lambda r: 9,
                                        docol=lambda r: 0, max_dist=C_WINDOW - 1, sinks=w["c_sinks"], name="swa_bwd")
            g["c_sinks"] = dsk[0, :C_HEADS]
            dproj = _attn_grad_combine([(dq, dk, dv)], tabs, name="swa_grad_combine")
            g["c_w_qkv"], g["c_b_qkv"] = _mm_tn(sv["n_mix"], dproj, S=1, name="l1_qkv_dw", bias=True)
            dh, stk["mix_norm"][1] = _mm_nt(dproj, w["c_w_qkv"], name="l1_dx", mode="norm", h=sv["h_mix"],
                                            gain=w["mix_norm"][1], dh=dh)
        else:
            (g["ab_w_out"],) = _mm_tn(sv["mix"], dh, S=1, name="l0_out_dw", kk=768)
            (dmix,) = _mm_nt(dh, w["ab_w_out"], name="l0_dmix", mode="plain", kchunk=768)
            (dxy, g["lru_conv_w"], g["lru_conv_b"], g["lru_wa"], g["lru_ba"], g["lru_wx"], g["lru_bx"],
             g["lru_lambda"]) = _lru_bwd(sv["proj"], sv["hs"], dmix, w["lru_conv_w"], w["lru_conv_b"], w["lru_wa"],
                                         w["lru_ba"], w["lru_wx"], w["lru_bx"], w["lru_lambda"], name="lru_bwd")
            dqkv = _dilated_bwd(sv["proj"], sv["mix"][:, D_MODEL:], sv["lse"], dmix[:, D_MODEL:], tabs)
            dproj = jnp.concatenate([dxy, dqkv], axis=1)
            (g["ab_w_in"],) = _mm_tn(sv["n_mix"], dproj, S=N_CHIPS, name="l0_in_dw")
            dh, stk["mix_norm"][0] = _mm_nt(dproj, w["ab_w_in"], name="l0_dx", mode="norm", h=sv["h_mix"],
                                            gain=w["mix_norm"][0], dh=dh)
    for k, v in stk.items():
        g[k] = jnp.concatenate(v, axis=0) if v[0].shape[0] == 1 else jnp.stack(v, axis=0)
    return loss[0, 0], dh, g


ANY = pl.BlockSpec(memory_space=pl.ANY)
MESH = pl.DeviceIdType.MESH


def _place():
    x, y, c = lax.axis_index("x"), lax.axis_index("y"), lax.axis_index("c")
    return x, y, c, [(1 - x, y), (x, 1 - y), (1 - x, 1 - y)]


def _remote(send_sems, recv_sems):
    def copy(k, src, dst, to):
        return pltpu.make_async_remote_copy(src_ref=src, dst_ref=dst, send_sem=send_sems.at[k], recv_sem=recv_sems.at[k],
                                            device_id=to, device_id_type=MESH)
    return copy


def _gather_weights(wpack, spack):
    R = wpack.shape[0]
    Rh = R // 2

    def body(w_ref, s_ref, wf_ref, sf_ref, send_sems, recv_sems):
        x, y, c, chips = _place()
        me, sib = 2 * x + y, (x, y, 1 - c)
        copy = _remote(send_sems, recv_sems)
        half = lambda chip, hh: wf_ref.at[chip, pl.ds(hh * Rh, Rh), :]
        sends = []
        for j, (cx, cy) in enumerate(chips):
            sends.append(copy(j, w_ref.at[pl.ds(c * Rh, Rh), :], half(me, c), (cx, cy, c)))
            sends.append(copy(3 + j, s_ref, sf_ref.at[me], (cx, cy, c)))
        for cp in sends:
            cp.start()
        for j, (cx, cy) in enumerate(chips):
            got = half(2 * cx + cy, c)
            copy(j, got, got, sib).wait_recv()
            fwd = copy(6 + j, got, got, sib)
            fwd.start()
            sends.append(fwd)
        for j, (cx, cy) in enumerate(chips):
            got = half(2 * cx + cy, 1 - c)
            copy(6 + j, got, got, sib).wait_recv()
            copy(3 + j, s_ref, sf_ref.at[2 * cx + cy], sib).wait_recv()
        for cp in sends:
            cp.wait_send()

    out_shape = [jax.ShapeDtypeStruct((N_CHIPS,) + wpack.shape, wpack.dtype),
                 jax.ShapeDtypeStruct((N_CHIPS,) + spack.shape, spack.dtype)]
    wfull, sfull = pl.pallas_call(body, name="gather_weights", out_shape=out_shape, in_specs=[ANY, ANY], out_specs=[ANY, ANY],
                                  scratch_shapes=[pltpu.SemaphoreType.DMA((9,)), pltpu.SemaphoreType.DMA((9,))])(wpack, spack)
    chip = 2 * lax.axis_index("x") + lax.axis_index("y")
    return (lax.dynamic_update_index_in_dim(wfull, wpack, chip, 0), lax.dynamic_update_index_in_dim(sfull, spack, chip, 0))


def _rs_pair_exchange(gpack):
    _, R, C = gpack.shape
    Rh = R // 2

    def body(g_ref, ra_ref, send_sems, recv_sems):
        x, y, c, _ = _place()
        copy = _remote(send_sems, recv_sems)
        cps = [copy(j, g_ref.at[j, pl.ds((1 - c) * Rh, Rh), :], ra_ref.at[j], (x, y, 1 - c)) for j in range(N_CHIPS)]
        for cp in cps:
            cp.start()
        for cp in cps:
            cp.wait()

    return pl.pallas_call(body, name="rs_pair_exchange", out_shape=jax.ShapeDtypeStruct((N_CHIPS, Rh, C), gpack.dtype),
                          in_specs=[ANY], out_specs=ANY,
                          scratch_shapes=[pltpu.SemaphoreType.DMA((N_CHIPS,)), pltpu.SemaphoreType.DMA((N_CHIPS,))])(gpack)


def _rs_pair_add(place, gpack, ra, *, tr=PACK_ROWS // 6):
    _, R, C = gpack.shape
    Rh = R // 2
    nrb = Rh // tr

    def body(p_ref, g_ref, ra_ref, pair_ref, own_ref):
        s = g_ref[...].astype(F32) + ra_ref[...].astype(F32)
        pair_ref[...] = s.astype(BF16)

        @pl.when(pl.program_id(1) == p_ref[1])
        def _():
            own_ref[...] = s

    grid_spec = pltpu.PrefetchScalarGridSpec(
        num_scalar_prefetch=1, grid=(nrb, N_CHIPS),
        in_specs=[pl.BlockSpec((None, tr, C), lambda i, j, p: (j, p[0] * nrb + i, 0)),
                  pl.BlockSpec((None, tr, C), lambda i, j, p: (j, i, 0))],
        out_specs=[pl.BlockSpec((None, tr, C), lambda i, j, p: (j, i, 0)), pl.BlockSpec((tr, C), lambda i, j, p: (i, 0))])
    return pl.pallas_call(
        body, name="rs_pair_add", grid_spec=grid_spec,
        out_shape=[jax.ShapeDtypeStruct((N_CHIPS, Rh, C), BF16), jax.ShapeDtypeStruct((Rh, C), F32)],
        compiler_params=pltpu.CompilerParams(dimension_semantics=("arbitrary", "arbitrary"),
                                             vmem_limit_bytes=VMEM_LIMIT_V7X))(place, gpack, ra)


def _rs_chip_exchange(pair, small):
    _, Rh, C = pair.shape

    def body(p_ref, s_ref, rb_ref, rs_ref, send_sems, recv_sems, local_sem):
        x, y, c, chips = _place()
        copy = _remote(send_sems, recv_sems)
        dev = 4 * x + 2 * y + c
        mine = pltpu.make_async_copy(s_ref, rs_ref.at[dev], local_sem.at[0])
        mine.start()
        cps = [copy(j, p_ref.at[2 * cx + cy], rb_ref.at[j], (cx, cy, c)) for j, (cx, cy) in enumerate(chips)]
        peers = []
        for k in range(1, 8):
            px = 1 - x if k & 4 else x
            py = 1 - y if k & 2 else y
            pc = 1 - c if k & 1 else c
            peers.append((px, py, pc))
            cps.append(copy(2 + k, s_ref, rs_ref.at[dev], (px, py, pc)))
        for cp in cps:
            cp.start()
        for j in range(3):
            copy(j, p_ref.at[0], rb_ref.at[j], (x, y, c)).wait_recv()
        for k, (px, py, pc) in enumerate(peers, start=1):
            copy(2 + k, s_ref, rs_ref.at[4 * px + 2 * py + pc], (x, y, c)).wait_recv()
        for cp in cps:
            cp.wait_send()
        mine.wait()

    out_shape = [jax.ShapeDtypeStruct((3, Rh, C), pair.dtype), jax.ShapeDtypeStruct((8,) + small.shape, small.dtype)]
    return pl.pallas_call(body, name="rs_chip_exchange", out_shape=out_shape, in_specs=[ANY, ANY], out_specs=[ANY, ANY],
                          scratch_shapes=[pltpu.SemaphoreType.DMA((10,)), pltpu.SemaphoreType.DMA((10,)),
                                          pltpu.SemaphoreType.DMA((1,))])(pair, small)


def _rs_final_add(place, own, rb, *, tr=PACK_ROWS // 6):
    Rh, C = own.shape
    nrb = Rh // tr

    def body(p_ref, o_ref, rb_ref, f_ref):
        f_ref[...] = ((o_ref[...] + rb_ref[0].astype(F32)) + rb_ref[1].astype(F32)) + rb_ref[2].astype(F32)

    grid_spec = pltpu.PrefetchScalarGridSpec(
        num_scalar_prefetch=1, grid=(nrb,),
        in_specs=[pl.BlockSpec((tr, C), lambda i, p: (i, 0)), pl.BlockSpec((3, tr, C), lambda i, p: (0, i, 0))],
        out_specs=pl.BlockSpec((tr, C), lambda i, p: (p[0] * nrb + i, 0)))
    return pl.pallas_call(
        body, name="rs_final_add", grid_spec=grid_spec, out_shape=jax.ShapeDtypeStruct((2 * Rh, C), F32),
        compiler_params=pltpu.CompilerParams(dimension_semantics=("arbitrary",), vmem_limit_bytes=VMEM_LIMIT_V7X))(place, own, rb)


def _sum_slots(rs):
    n, rows, C = rs.shape

    def body(r_ref, o_ref):
        acc = r_ref[0]
        for k in range(1, n):
            acc = acc + r_ref[k]
        o_ref[...] = acc

    return _call(body, name="small_grad_sum", grid=(1,), in_specs=[pl.BlockSpec((n, rows, C), lambda i: (0, 0, 0))],
                 out_specs=pl.BlockSpec((rows, C), lambda i: (0, 0)), out_shape=jax.ShapeDtypeStruct((rows, C), F32),
                 sem=("arbitrary",))(rs)


def _rs_sibling_share(gbuf):
    R, C = gbuf.shape
    Rh = R // 2

    def body(f_ref, g_ref, send_sems, recv_sems):
        x, y, c, _ = _place()
        copy = _remote(send_sems, recv_sems)
        mine = g_ref.at[pl.ds(c * Rh, Rh), :]
        theirs = g_ref.at[pl.ds((1 - c) * Rh, Rh), :]
        out = copy(0, mine, mine, (x, y, 1 - c))
        out.start()
        copy(0, theirs, theirs, (x, y, c)).wait_recv()
        out.wait_send()

    return pl.pallas_call(body, name="rs_sibling_share", out_shape=jax.ShapeDtypeStruct((R, C), gbuf.dtype),
                          in_specs=[ANY], out_specs=ANY, input_output_aliases={0: 0},
                          scratch_shapes=[pltpu.SemaphoreType.DMA((1,)), pltpu.SemaphoreType.DMA((1,))])(gbuf)


def _adamw(w, g, m, v, *, name):
    rows, cols = w.shape
    tr = rows
    for cand in range(min(rows, 512), 7, -8):
        if rows % cand == 0:
            tr = cand
            break
    spec = pl.BlockSpec((tr, cols), lambda i: (i, 0))

    def body(w_ref, g_ref, m_ref, v_ref, d_ref, nm_ref, nv_ref):
        gg = g_ref[...]
        nm = ADAM_B1 * m_ref[...] + (1.0 - ADAM_B1) * gg
        nv = ADAM_B2 * v_ref[...] + (1.0 - ADAM_B2) * (gg * gg)
        m_hat = nm / (1.0 - ADAM_B1 ** ADAM_STEP)
        v_hat = nv / (1.0 - ADAM_B2 ** ADAM_STEP)
        d_ref[...] = -ADAM_LR * (m_hat / (jnp.sqrt(v_hat) + ADAM_EPS) + ADAM_WD * w_ref[...])
        nm_ref[...] = nm
        nv_ref[...] = nv

    return _call(body, name=name, grid=(rows // tr,), in_specs=[spec] * 4, out_specs=[spec] * 3,
                 out_shape=[jax.ShapeDtypeStruct((rows, cols), F32)] * 3, sem=("parallel",))(w, g, m, v)


WEIGHT_NAMES = ("mix_norm", "ab_w_in", "lru_conv_w", "lru_conv_b", "lru_wa", "lru_ba", "lru_wx", "lru_bx", "lru_lambda",
                "ab_w_out", "c_w_qkv", "c_b_qkv", "c_sinks", "c_w_out", "c_b_out", "xa_norm", "xa_mem_norm", "xa_wq",
                "xa_wkv", "xa_wo", "ffn_norm", "ffn_w_gate_up", "ffn_w_down", "final_norm")
BIG = ("ab_w_in", "lru_wa", "lru_wx", "ab_w_out", "c_w_qkv", "c_w_out", "xa_wq", "xa_wkv", "xa_wo", "ffn_w_gate_up",
       "ffn_w_down")
REPLICATED = ("mix_norm", "lru_conv_b", "lru_lambda", "c_sinks", "xa_norm", "xa_mem_norm", "ffn_norm", "final_norm")
SMALL_SHARDED = ("lru_conv_w", "lru_ba", "lru_bx", "c_b_qkv", "c_b_out")
LANES = 1024


def _rows(v):
    flat = v.reshape(-1)
    return jnp.pad(flat, (0, -flat.shape[0] % LANES)).reshape(-1, LANES)


def _stack_rows(parts, total):
    return jnp.concatenate(parts + [jnp.zeros((total - sum(p.shape[0] for p in parts), LANES), parts[0].dtype)], axis=0)


def _pack_small(parts, total, *, name):
    def body(*refs):
        o_ref = refs[-1]
        o_ref[...] = jnp.zeros_like(o_ref)
        row = 0
        for p_ref in refs[:-1]:
            o_ref[row:row + p_ref.shape[0], :] = p_ref[...]
            row += p_ref.shape[0]

    return _call(body, name=name, grid=(1,), in_specs=[pl.BlockSpec(p.shape, lambda i: (0, 0)) for p in parts],
                 out_specs=pl.BlockSpec((total, LANES), lambda i: (0, 0)),
                 out_shape=jax.ShapeDtypeStruct((total, LANES), F32), sem=("arbitrary",))(*parts)


def _unshard(name, t):
    if name in ("ab_w_in", "ab_w_out", "c_w_out"):
        return t if name == "ab_w_in" else t.reshape(1, -1, t.shape[-1])
    if name in ("lru_wa", "lru_wx"):
        return t.transpose(1, 0, 2, 3).reshape(LRU_HEADS, LRU_HEAD_DIM, LRU_HEAD_DIM)
    if name == "c_w_qkv":
        return t.transpose(1, 0, 2).reshape(1, D_MODEL, -1)
    if name in ("xa_wq", "xa_wkv", "ffn_w_down"):
        return t.transpose(1, 0, 2, 3).reshape(2, -1, t.shape[-1])
    return t.transpose(1, 0, 2, 3)


def _to_shards(name, g):
    if name in ("ab_w_in", "ab_w_out", "c_w_out"):
        return g.reshape(N_CHIPS, -1)
    if name in ("lru_wa", "lru_wx"):
        return g.reshape(LRU_HEADS, N_CHIPS, -1).transpose(1, 0, 2).reshape(N_CHIPS, -1)
    if name == "c_w_qkv":
        return g.reshape(D_MODEL, N_CHIPS, -1).transpose(1, 0, 2).reshape(N_CHIPS, -1)
    if name in ("xa_wq", "xa_wkv", "ffn_w_down"):
        return g.reshape(2, N_CHIPS, -1).transpose(1, 0, 2).reshape(N_CHIPS, -1)
    return g.transpose(1, 0, 2, 3).reshape(N_CHIPS, -1)


def kernel(x, mem, mix_norm, ab_w_in, lru_conv_w, lru_conv_b, lru_wa, lru_ba, lru_wx, lru_bx, lru_lambda, ab_w_out, c_w_qkv, c_b_qkv, c_sinks, c_w_out, c_b_out, xa_norm, xa_mem_norm, xa_wq, xa_wkv, xa_wo, ffn_norm, ffn_w_gate_up, ffn_w_down, final_norm, loss_target, m_mix_norm, m_ab_w_in, m_lru_conv_w, m_lru_conv_b, m_lru_wa, m_lru_ba, m_lru_wx, m_lru_bx, m_lru_lambda, m_ab_w_out, m_c_w_qkv, m_c_b_qkv, m_c_sinks, m_c_w_out, m_c_b_out, m_xa_norm, m_xa_mem_norm, m_xa_wq, m_xa_wkv, m_xa_wo, m_ffn_norm, m_ffn_w_gate_up, m_ffn_w_down, m_final_norm, v_mix_norm, v_ab_w_in, v_lru_conv_w, v_lru_conv_b, v_lru_wa, v_lru_ba, v_lru_wx, v_lru_bx, v_lru_lambda, v_ab_w_out, v_c_w_qkv, v_c_b_qkv, v_c_sinks, v_c_w_out, v_c_b_out, v_xa_norm, v_xa_mem_norm, v_xa_wq, v_xa_wkv, v_xa_wo, v_ffn_norm, v_ffn_w_gate_up, v_ffn_w_down, v_final_norm):
    given = dict(locals())
    wl = {n: given[n] for n in WEIGHT_NAMES}
    ml = {n: given["m_" + n] for n in WEIGHT_NAMES}
    vl = {n: given["v_" + n] for n in WEIGHT_NAMES}
    xi, yi, ci = lax.axis_index("x"), lax.axis_index("y"), lax.axis_index("c")
    chip = 2 * xi + yi

    big_rows = {n: wl[n].size // LANES for n in BIG}
    wpack = _stack_rows([wl[n].astype(BF16).reshape(big_rows[n], LANES) for n in BIG], PACK_ROWS)
    spack = _pack_small([_rows(wl[n]) for n in SMALL_SHARDED], 8, name="pack_small_weights")
    wfull, sfull = _gather_weights(wpack, spack)
    w = {n: wl[n] for n in REPLICATED}
    w["c_sinks"] = wl["c_sinks"][0]
    off = 0
    for n in BIG:
        loc = wl[n].shape[1:] if wl[n].shape[0] == 1 else wl[n].shape
        w[n] = _unshard(n, wfull[:, off:off + big_rows[n]].reshape((N_CHIPS,) + loc))
        off += big_rows[n]
    for r, n in enumerate(SMALL_SHARDED):
        loc = wl[n].shape[1:]
        t = sfull[:, r, :wl[n].size].reshape((N_CHIPS,) + loc)
        if n == "lru_conv_w":
            w[n] = t.transpose(1, 0, 2).reshape(CONV_WIDTH, -1)
        elif n in ("lru_ba", "lru_bx"):
            w[n] = t.transpose(1, 0, 2).reshape(1, -1)
        else:
            w[n] = t.reshape(1, -1)

    loss_part, grad_x, g = _device_step(x[0], mem[0], loss_target[0], w)

    gparts = [_to_shards(n, g[n].astype(BF16)).reshape(N_CHIPS, big_rows[n], LANES) for n in BIG]
    gparts.append(jnp.zeros((N_CHIPS, PACK_ROWS - off, LANES), BF16))
    gpack = jnp.concatenate(gparts, axis=1)
    small_parts = [_rows(g[n]) for n in REPLICATED] + [_rows(jnp.broadcast_to(loss_part, (LANES,)))]
    small_parts += [_rows(g[n]) for n in SMALL_SHARDED]
    small = _pack_small(small_parts, 24, name="pack_small_grads")
    place = jnp.stack([ci, chip]).astype(jnp.int32)
    ra = _rs_pair_exchange(gpack)
    pair, own = _rs_pair_add(place, gpack, ra)
    rb, rs = _rs_chip_exchange(pair, small)
    gsum = _rs_sibling_share(_rs_final_add(place, own, rb))
    ssum = _sum_slots(rs)

    grads = {}
    off = 0
    for n in BIG:
        grads[n] = gsum[off:off + big_rows[n]].reshape(wl[n].shape)
        off += big_rows[n]
    row = 0
    for n in REPLICATED:
        k = _rows(g[n]).shape[0]
        grads[n] = ssum[row:row + k].reshape(-1)[:wl[n].size].reshape(wl[n].shape)
        row += k
    loss = ssum[row, 0]
    row += 1
    for n in SMALL_SHARDED:
        k = _rows(g[n]).shape[0]
        full = ssum[row:row + k].reshape(-1)[:g[n].size]
        row += k
        loc = wl[n].shape
        if n == "lru_conv_w":
            sh = full.reshape(CONV_WIDTH, N_CHIPS, -1)
        elif n in ("lru_ba", "lru_bx"):
            sh = full.reshape(LRU_HEADS, N_CHIPS, -1)
        else:
            sh = full.reshape(1, N_CHIPS, -1)
        grads[n] = lax.dynamic_index_in_dim(sh, chip, axis=1, keepdims=False).reshape(loc)

    delta, new_m, new_v = {}, {}, {}
    for n in BIG:
        shape2 = (wl[n].size // wl[n].shape[-1], wl[n].shape[-1])
        d, nm, nv = _adamw(wl[n].reshape(shape2), grads[n].reshape(shape2), ml[n].reshape(shape2), vl[n].reshape(shape2),
                           name="adamw_" + n)
        delta[n], new_m[n], new_v[n] = (t.reshape(wl[n].shape) for t in (d, nm, nv))
    smalls = REPLICATED + SMALL_SHARDED
    packs = [_pack_small([_rows(src[n]) for n in smalls], 24, name="pack_adamw_" + tag)
             for tag, src in (("w", wl), ("g", grads), ("m", ml), ("v", vl))]
    outs = _adamw(*packs, name="adamw_small")
    row = 0
    for n in smalls:
        k = _rows(wl[n]).shape[0]
        for dst, o in zip((delta, new_m, new_v), outs):
            dst[n] = o[row:row + k].reshape(-1)[:wl[n].size].reshape(wl[n].shape)
        row += k

    return (loss, grad_x[None], *[grads[n] for n in WEIGHT_NAMES], *[delta[n] for n in WEIGHT_NAMES],
            *[new_m[n] for n in WEIGHT_NAMES], *[new_v[n] for n in WEIGHT_NAMES])
```

```python
import jax
import jax.numpy as jnp
from jax import lax
from jax.experimental import pallas as pl
from jax.experimental.pallas import tpu as pltpu

F32, BF16 = jnp.float32, jnp.bfloat16
D_MODEL = 1024
NORM_EPS = 1e-6
ROPE_THETA = 500000.0
HEAD_DIM = 64
ROT_DIM = 16
BLK = 128
LRU_HEADS, LRU_HEAD_DIM, CONV_WIDTH, LRU_C = 4, 256, 4, 8.0
DILATED_PATTERN = ((128, 1), (512, 4), (2048, 16))
B_HEADS, C_HEADS, C_KV_HEADS, C_WINDOW = 8, 16, 2, 128
XA_HEADS, XA_HEAD_DIM, N_MEM = 4, 128, 256
D_FF = 2816
NEG = -1e30
ADAM_LR, ADAM_B1, ADAM_B2, ADAM_EPS, ADAM_WD, ADAM_STEP = 0.001, 0.9, 0.999, 1e-08, 0.01, 10
N_CHIPS = 4
PACK_ROWS = 7296
VMEM_LIMIT_V7X = 56 * 1024 * 1024

NN = (((1,), (0,)), ((), ()))
NT = (((1,), (1,)), ((), ()))
TN = (((0,), (0,)), ((), ()))


def _dot(a, b, dims=NN):
    return lax.dot_general(a, b, dims, preferred_element_type=F32)


def _sigmoid(x):
    return 1.0 / (1.0 + jnp.exp(-x))


def _call(body, *, name, grid, in_specs, out_specs, out_shape, scratch=(), sem=None):
    return pl.pallas_call(
        body, name=name, grid=grid, in_specs=in_specs, out_specs=out_specs, out_shape=out_shape,
        scratch_shapes=list(scratch),
        compiler_params=pltpu.CompilerParams(dimension_semantics=sem, vmem_limit_bytes=VMEM_LIMIT_V7X))


def _rope_tables(L):
    half = ROT_DIM // 2
    inv = ROPE_THETA ** (-jnp.arange(0, ROT_DIM, 2, dtype=F32) / ROT_DIM)
    ang = jnp.arange(L, dtype=F32)[:, None] * inv[None, :]
    cos, sin = jnp.cos(ang), jnp.sin(ang)
    rest = HEAD_DIM - ROT_DIM
    z8, zr, one = jnp.zeros((L, half), F32), jnp.zeros((L, rest), F32), jnp.ones((L, rest), F32)
    c = jnp.concatenate([cos, cos, one], axis=1)
    s1 = jnp.concatenate([-sin, z8, zr], axis=1)
    s2 = jnp.concatenate([z8, sin, zr], axis=1)
    return tuple(jnp.concatenate([t, t], axis=1) for t in (c, s1, s2))


def _rope_fwd(v, c, s1, s2):
    return v * c + pltpu.roll(v, 120, 1) * s1 + pltpu.roll(v, 8, 1) * s2


def _rope_bwd(dv, c, s1, s2):
    return dv * c + pltpu.roll(dv * s1, 8, 1) + pltpu.roll(dv * s2, 120, 1)


def _rowmm(a, w3, *, name, tm=256, gain=None, bias=None, res=None, swiglu=False, rope=None):
    M, K = a.shape
    S, _, Ns = w3.shape
    N = S * Ns
    tm = min(tm, M)
    has_norm, has_bias, has_res, has_rope = gain is not None, bias is not None, res is not None, rope is not None
    row = lambda w: pl.BlockSpec((tm, w), lambda i: (i, 0))
    whole = lambda shape: pl.BlockSpec(shape, lambda i: (0,) * len(shape))
    ins, specs = [a], [row(K)]
    if has_norm:
        ins.append(gain.reshape(1, K)); specs.append(whole((1, K)))
    ins.append(w3); specs.append(whole((S, K, Ns)))
    if has_bias:
        ins.append(bias.reshape(1, N)); specs.append(whole((1, N)))
    if has_res:
        ins.append(res); specs.append(row(N))
    if has_rope:
        ins += list(rope[2]); specs += [row(128)] * 3
    y_dtype = F32 if has_res else BF16
    out_shape, out_specs = [jax.ShapeDtypeStruct((M, N), y_dtype)], [row(N)]
    if has_norm:
        out_shape.append(jax.ShapeDtypeStruct((M, K), BF16)); out_specs.append(row(K))
    if swiglu:
        out_shape.append(jax.ShapeDtypeStruct((M, N // 2), BF16)); out_specs.append(row(N // 2))
    scratch = [pltpu.VMEM((tm, N), F32)] if has_rope else []

    def body(*refs):
        it = iter(refs)
        a_ref = next(it)
        g_ref = next(it) if has_norm else None
        w_ref = next(it)
        b_ref = next(it) if has_bias else None
        r_ref = next(it) if has_res else None
        tabs = [next(it) for _ in range(3)] if has_rope else None
        y_ref = next(it)
        n_ref = next(it) if has_norm else None
        act_ref = next(it) if swiglu else None
        ys_ref = next(it) if has_rope else None
        if has_norm:
            x = a_ref[...].astype(F32)
            ms = jnp.mean(x * x, axis=-1, keepdims=True)
            xb = (x * lax.rsqrt(ms + NORM_EPS) * g_ref[...]).astype(BF16)
            n_ref[...] = xb
        else:
            xb = a_ref[...].astype(BF16)
        if swiglu:
            for s in range(S // 2):
                g = _dot(xb, w_ref[s])
                u = _dot(xb, w_ref[s + S // 2])
                y_ref[:, s * Ns:(s + 1) * Ns] = g.astype(BF16)
                y_ref[:, N // 2 + s * Ns:N // 2 + (s + 1) * Ns] = u.astype(BF16)
                act_ref[:, s * Ns:(s + 1) * Ns] = (g * _sigmoid(g) * u).astype(BF16)
            return
        for s in range(S):
            sl = slice(s * Ns, (s + 1) * Ns)
            acc = _dot(xb, w_ref[s])
            if has_bias:
                acc = acc + b_ref[:, sl]
            if has_res:
                acc = acc + r_ref[:, sl]
            if has_rope:
                ys_ref[:, sl] = acc
            else:
                y_ref[:, sl] = acc.astype(y_dtype)
        if has_rope:
            c, s1, s2 = (t[...] for t in tabs)
            for cb in range(N // 128):
                cs = slice(cb * 128, (cb + 1) * 128)
                v = ys_ref[:, cs]
                if rope[0] <= cb * 128 < rope[1]:
                    v = _rope_fwd(v, c, s1, s2)
                y_ref[:, cs] = v.astype(BF16)

    return _call(body, name=name, grid=(M // tm,), in_specs=specs, out_specs=out_specs, out_shape=out_shape,
                 scratch=scratch, sem=("parallel",))(*ins)


def _mm_nt(dy, w3, *, name, mode, tm=256, kchunk=None, h=None, gain=None, dh=None, gu=None):
    M, N = dy.shape
    S, K, Ns = w3.shape
    kchunk = kchunk or K
    tm = min(tm, M)
    row = lambda w: pl.BlockSpec((tm, w), lambda i: (i, 0))
    whole = lambda shape: pl.BlockSpec(shape, lambda i: (0,) * len(shape))
    ins, specs = [dy, w3], [row(N), whole((S, K, Ns))]
    has_dh = dh is not None
    if mode == "norm":
        ins += [h, gain.reshape(1, K)]; specs += [row(K), whole((1, K))]
        if has_dh:
            ins.append(dh); specs.append(row(K))
        out_shape = [jax.ShapeDtypeStruct((M, K), F32), jax.ShapeDtypeStruct((1, K), F32)]
        out_specs = [row(K), whole((1, K))]
    elif mode == "swiglu":
        ins.append(gu); specs.append(row(2 * K))
        out_shape, out_specs = [jax.ShapeDtypeStruct((M, 2 * K), BF16)], [row(2 * K)]
    else:
        out_shape, out_specs = [jax.ShapeDtypeStruct((M, K), BF16)], [row(K)]

    def body(*refs):
        it = iter(refs)
        dy_ref, w_ref = next(it), next(it)
        if mode == "norm":
            h_ref, g_ref = next(it), next(it)
            dh_ref = next(it) if has_dh else None
            o_ref, dg_ref = next(it), next(it)
        elif mode == "swiglu":
            gu_ref, o_ref = next(it), next(it)
        else:
            o_ref = next(it)
        for kc in range(K // kchunk):
            ks = slice(kc * kchunk, (kc + 1) * kchunk)
            acc = None
            for s in range(S):
                t = _dot(dy_ref[:, s * Ns:(s + 1) * Ns].astype(BF16), w_ref[s, ks, :], NT)
                acc = t if acc is None else acc + t
            if mode == "plain":
                o_ref[:, ks] = acc.astype(BF16)
            elif mode == "swiglu":
                us = slice(K + kc * kchunk, K + (kc + 1) * kchunk)
                g = gu_ref[:, ks].astype(F32)
                u = gu_ref[:, us].astype(F32)
                sg = _sigmoid(g)
                o_ref[:, ks] = (acc * u * (sg * (1.0 + g * (1.0 - sg)))).astype(BF16)
                o_ref[:, us] = (acc * (g * sg)).astype(BF16)
            else:
                x = h_ref[...].astype(F32)
                r = lax.rsqrt(jnp.mean(x * x, axis=-1, keepdims=True) + NORM_EPS)
                xhat = x * r
                dxh = acc * g_ref[...]
                dx = r * (dxh - xhat * jnp.mean(dxh * xhat, axis=-1, keepdims=True))
                o_ref[...] = dx + dh_ref[...] if has_dh else dx

                @pl.when(pl.program_id(0) == 0)
                def _():
                    dg_ref[...] = jnp.zeros_like(dg_ref)

                dg_ref[...] += jnp.sum(acc * xhat, axis=0, keepdims=True)

    sem = ("arbitrary",) if mode == "norm" else ("parallel",)
    return _call(body, name=name, grid=(M // tm,), in_specs=specs, out_specs=out_specs, out_shape=out_shape, sem=sem)(*ins)


def _mm_tn(x, dy, *, S, name, tk=1024, kk=None, bias=False):
    M, K = x.shape
    N = dy.shape[1]
    Ns = N // S
    kk = kk or K
    tk = min(tk, M)
    nl = M // tk
    in_specs = [pl.BlockSpec((tk, kk), lambda s, kc, l: (l, kc)), pl.BlockSpec((tk, Ns), lambda s, kc, l: (l, s))]
    out_shape = [jax.ShapeDtypeStruct((S, K, Ns), BF16)]
    out_specs = [pl.BlockSpec((None, kk, Ns), lambda s, kc, l: (s, kc, 0))]
    if bias:
        out_shape.append(jax.ShapeDtypeStruct((1, N), F32))
        out_specs.append(pl.BlockSpec((1, Ns), lambda s, kc, l: (0, s)))

    def body(x_ref, dy_ref, o_ref, *rest):
        acc_ref = rest[-1]
        kc, l = pl.program_id(1), pl.program_id(2)

        @pl.when(l == 0)
        def _():
            acc_ref[...] = jnp.zeros_like(acc_ref)

        acc_ref[...] += _dot(x_ref[...].astype(BF16), dy_ref[...].astype(BF16), TN)
        if bias:
            b_ref = rest[0]

            @pl.when((kc == 0) & (l == 0))
            def _():
                b_ref[...] = jnp.zeros_like(b_ref)

            @pl.when(kc == 0)
            def _():
                b_ref[...] += jnp.sum(dy_ref[...].astype(F32), axis=0, keepdims=True)

        @pl.when(l == nl - 1)
        def _():
            o_ref[...] = acc_ref[...].astype(BF16)

    return _call(body, name=name, grid=(S, K // kk, nl), in_specs=in_specs, out_specs=out_specs, out_shape=out_shape,
                 scratch=[pltpu.VMEM((kk, Ns), F32)], sem=("arbitrary", "arbitrary", "arbitrary"))(x, dy)


def _band_bias(max_dist, has_prev):
    rows = lax.broadcasted_iota(jnp.int32, (BLK, 2 * BLK), 0)
    cols = lax.broadcasted_iota(jnp.int32, (BLK, 2 * BLK), 1)
    dist = rows - cols + BLK
    ok = (dist >= 0) & (dist <= max_dist) & ((cols >= BLK) | has_prev)
    return jnp.where(ok, 0.0, NEG)


Q_SCALE = HEAD_DIM ** -0.5


def _band_fwd(qa, ka, va, *, d, nq, nkv, qcol, kcol, vcol, max_dist, sinks=None, name):
    Lr = qa.shape[0]
    nb = Lr // BLK
    qw, kw, G = nq * HEAD_DIM, nkv * HEAD_DIM, nq // nkv
    cur = lambda colf, w: pl.BlockSpec((BLK, w), lambda r, i: (i, colf(r)))
    prv = lambda colf, w: pl.BlockSpec((BLK, w), lambda r, i: (jnp.maximum(i - 1, 0), colf(r)))
    out = pl.BlockSpec((BLK, qw), lambda r, i: (i, r))
    ins, specs = [qa, ka, ka, va, va], [cur(qcol, qw), cur(kcol, kw), prv(kcol, kw), cur(vcol, kw), prv(vcol, kw)]
    has_sinks = sinks is not None
    if has_sinks:
        ins.append(sinks); specs.append(pl.BlockSpec(memory_space=pltpu.SMEM))

    def body(*refs):
        q_ref, kc_ref, kp_ref, vc_ref, vp_ref = refs[:5]
        sk_ref = refs[5] if has_sinks else None
        o_ref, lse_ref = refs[-2], refs[-1]
        bias = _band_bias(max_dist, pl.program_id(1) > 0)
        k2 = jnp.concatenate([kp_ref[...], kc_ref[...]], axis=0)
        v2 = jnp.concatenate([vp_ref[...], vc_ref[...]], axis=0)
        for h in range(nq):
            hs = slice(h * HEAD_DIM, (h + 1) * HEAD_DIM)
            ks = slice((h // G) * HEAD_DIM, (h // G + 1) * HEAD_DIM)
            s = _dot(q_ref[:, hs] * jnp.asarray(Q_SCALE, BF16), k2[:, ks], NT) + bias
            m = jnp.max(s, axis=-1, keepdims=True)
            if has_sinks:
                m = jnp.maximum(m, sk_ref[h])
            p = jnp.exp(s - m)
            l = jnp.sum(p, axis=-1, keepdims=True)
            if has_sinks:
                l = l + jnp.exp(sk_ref[h] - m)
            o_ref[:, hs] = (_dot(p.astype(BF16), v2[:, ks]) / l).astype(BF16)
            lse_ref[:, hs] = jnp.broadcast_to(m + jnp.log(l), (BLK, HEAD_DIM))

    return _call(body, name=name, grid=(d, nb), in_specs=specs, out_specs=[out, out],
                 out_shape=[jax.ShapeDtypeStruct((Lr, d * qw), BF16), jax.ShapeDtypeStruct((Lr, d * qw), F32)],
                 sem=("parallel", "parallel"))(*ins)


def _band_bwd(qa, ka, va, doa, oa, lsea, *, d, nq, nkv, qcol, kcol, vcol, docol, max_dist, sinks=None, name):
    Lr = qa.shape[0]
    nb = Lr // BLK
    qw, kw, G = nq * HEAD_DIM, nkv * HEAD_DIM, nq // nkv
    last = lambda i: jnp.minimum(i, nb - 1)
    cur = lambda colf, w: pl.BlockSpec((BLK, w), lambda r, i: (last(i), colf(r)))
    prv = lambda colf, w: pl.BlockSpec((BLK, w), lambda r, i: (jnp.maximum(last(i) - 1, 0), colf(r)))
    own = lambda r: r
    ins = [qa, ka, ka, va, va, doa, oa, lsea]
    specs = [cur(qcol, qw), cur(kcol, kw), prv(kcol, kw), cur(vcol, kw), prv(vcol, kw), cur(docol, qw), cur(own, qw),
             cur(own, qw)]
    has_sinks = sinks is not None
    if has_sinks:
        ins.append(sinks); specs.append(pl.BlockSpec(memory_space=pltpu.SMEM))
    out_shape = [jax.ShapeDtypeStruct((Lr, d * qw), F32), jax.ShapeDtypeStruct((Lr, d * kw), F32),
                 jax.ShapeDtypeStruct((Lr, d * kw), F32)]
    behind = lambda r, i: (jnp.maximum(i - 1, 0), r)
    out_specs = [pl.BlockSpec((BLK, qw), lambda r, i: (last(i), r)), pl.BlockSpec((BLK, kw), behind),
                 pl.BlockSpec((BLK, kw), behind)]
    if has_sinks:
        out_shape.append(jax.ShapeDtypeStruct((8, 128), F32))
        out_specs.append(pl.BlockSpec((8, 128), lambda r, i: (0, 0)))

    def body(*refs):
        it = iter(refs)
        q_ref, kc_ref, kp_ref, vc_ref, vp_ref, do_ref, o_ref, ls_ref = (next(it) for _ in range(8))
        sk_ref = next(it) if has_sinks else None
        dq_ref, dk_ref, dv_ref = next(it), next(it), next(it)
        dsk_ref = next(it) if has_sinks else None
        dk_car, dv_car = next(it), next(it)
        r_id, i = pl.program_id(0), pl.program_id(1)

        @pl.when(i == 0)
        def _():
            dk_car[...] = jnp.zeros_like(dk_car)
            dv_car[...] = jnp.zeros_like(dv_car)

        if has_sinks:
            @pl.when((r_id == 0) & (i == 0))
            def _():
                dsk_ref[...] = jnp.zeros_like(dsk_ref)

        @pl.when(i == nb)
        def _():
            dk_ref[...] = dk_car[...]
            dv_ref[...] = dv_car[...]

        @pl.when(i < nb)
        def _():
            bias = _band_bias(max_dist, i > 0)
            k2 = jnp.concatenate([kp_ref[...], kc_ref[...]], axis=0)
            v2 = jnp.concatenate([vp_ref[...], vc_ref[...]], axis=0)
            if has_sinks:
                lane = lax.broadcasted_iota(jnp.int32, (8, 128), 1)
                dsk = jnp.zeros((8, 128), F32)
            for kv in range(nkv):
                ks = slice(kv * HEAD_DIM, (kv + 1) * HEAD_DIM)
                kh, vh = k2[:, ks], v2[:, ks]
                dkt = jnp.zeros((HEAD_DIM, 2 * BLK), F32)
                dvt = jnp.zeros((HEAD_DIM, 2 * BLK), F32)
                for g in range(G):
                    h = kv * G + g
                    hs = slice(h * HEAD_DIM, (h + 1) * HEAD_DIM)
                    q = q_ref[:, hs] * jnp.asarray(Q_SCALE, BF16)
                    do = do_ref[:, hs]
                    lse = ls_ref[:, h * HEAD_DIM:h * HEAD_DIM + 1]
                    dl = jnp.sum(do.astype(F32) * o_ref[:, hs].astype(F32), axis=-1, keepdims=True)
                    p = jnp.exp(_dot(q, kh, NT) + bias - lse)
                    ds = (p * (_dot(do, vh, NT) - dl)).astype(BF16)
                    dq_ref[:, hs] = _dot(ds, kh) * Q_SCALE
                    dkt = dkt + _dot(q, ds, TN)
                    dvt = dvt + _dot(do, p.astype(BF16), TN)
                    if has_sinks:
                        val = -jnp.sum(jnp.exp(sk_ref[h] - lse) * dl, axis=0, keepdims=True)
                        dsk = dsk + jnp.where(lane == h, val, 0.0)
                dk, dv = dkt.T, dvt.T
                dk_ref[:, ks] = dk_car[:, ks] + dk[:BLK]
                dv_ref[:, ks] = dv_car[:, ks] + dv[:BLK]
                dk_car[:, ks] = dk[BLK:]
                dv_car[:, ks] = dv[BLK:]
            if has_sinks:
                dsk_ref[...] += dsk

    return _call(body, name=name, grid=(d, nb + 1), in_specs=specs, out_specs=out_specs, out_shape=out_shape,
                 scratch=[pltpu.VMEM((BLK, kw), F32), pltpu.VMEM((BLK, kw), F32)], sem=("arbitrary", "arbitrary"))(*ins)


def _attn_grad_combine(branches, tabs, *, name, tm=256):
    L, qw = branches[0][0].shape
    kw = branches[0][1].shape[1]
    nbr = len(branches)
    row = lambda w: pl.BlockSpec((tm, w), lambda i: (i, 0))
    ins, specs = [], []
    for dq, dk, dv in branches:
        ins += [dq, dk, dv]; specs += [row(qw), row(kw), row(kw)]
    ins += list(tabs); specs += [row(128)] * 3

    def body(*refs):
        c, s1, s2 = (t[...] for t in refs[3 * nbr:3 * nbr + 3])
        o_ref = refs[-1]
        for part, (w, off, rot) in enumerate(((qw, 0, True), (kw, qw, True), (kw, qw + kw, False))):
            for cb in range(w // 128):
                cs = slice(cb * 128, (cb + 1) * 128)
                v = refs[part][:, cs]
                for b in range(1, nbr):
                    v = v + refs[3 * b + part][:, cs]
                if rot:
                    v = _rope_bwd(v, c, s1, s2)
                o_ref[:, off + cb * 128:off + (cb + 1) * 128] = v.astype(BF16)

    return _call(body, name=name, grid=(L // tm,), in_specs=specs, out_specs=row(qw + 2 * kw),
                 out_shape=jax.ShapeDtypeStruct((L, qw + 2 * kw), BF16), sem=("parallel",))(*ins)


def _xattn_fwd(q, kv, *, name, tq=512):
    L, W = q.shape
    scale = XA_HEAD_DIM ** -0.5
    row = pl.BlockSpec((tq, W), lambda i: (i, 0))
    kvs = pl.BlockSpec((N_MEM, 2 * W), lambda i: (0, 0))

    def body(q_ref, kv_ref, o_ref, lse_ref):
        for h in range(XA_HEADS):
            hs = slice(h * XA_HEAD_DIM, (h + 1) * XA_HEAD_DIM)
            vs = slice(W + h * XA_HEAD_DIM, W + (h + 1) * XA_HEAD_DIM)
            s = _dot(q_ref[:, hs], kv_ref[:, hs], NT) * scale
            m = jnp.max(s, axis=-1, keepdims=True)
            p = jnp.exp(s - m)
            l = jnp.sum(p, axis=-1, keepdims=True)
            o_ref[:, hs] = (_dot(p.astype(BF16), kv_ref[:, vs]) / l).astype(BF16)
            lse_ref[:, hs] = jnp.broadcast_to(m + jnp.log(l), (tq, XA_HEAD_DIM))

    return _call(body, name=name, grid=(L // tq,), in_specs=[row, kvs], out_specs=[row, row],
                 out_shape=[jax.ShapeDtypeStruct((L, W), BF16), jax.ShapeDtypeStruct((L, W), F32)], sem=("parallel",))(q, kv)


def _xattn_bwd(q, kv, o, lse, do, *, name, tq=512):
    L, W = q.shape
    scale = XA_HEAD_DIM ** -0.5
    row = pl.BlockSpec((tq, W), lambda i: (i, 0))
    kvs = pl.BlockSpec((N_MEM, 2 * W), lambda i: (0, 0))

    def body(q_ref, kv_ref, o_ref, lse_ref, do_ref, dq_ref, dkv_ref):
        @pl.when(pl.program_id(0) == 0)
        def _():
            dkv_ref[...] = jnp.zeros_like(dkv_ref)

        for h in range(XA_HEADS):
            hs = slice(h * XA_HEAD_DIM, (h + 1) * XA_HEAD_DIM)
            vs = slice(W + h * XA_HEAD_DIM, W + (h + 1) * XA_HEAD_DIM)
            qh, kh, vh, doh = q_ref[:, hs], kv_ref[:, hs], kv_ref[:, vs], do_ref[:, hs]
            p = jnp.exp(_dot(qh, kh, NT) * scale - lse_ref[:, h * XA_HEAD_DIM:h * XA_HEAD_DIM + 1])
            dl = jnp.sum(doh.astype(F32) * o_ref[:, hs].astype(F32), axis=-1, keepdims=True)
            ds = (p * (_dot(doh, vh, NT) - dl) * scale).astype(BF16)
            dq_ref[:, hs] = _dot(ds, kh).astype(BF16)
            dkv_ref[:, hs] += _dot(ds, qh, TN)
            dkv_ref[:, vs] += _dot(p.astype(BF16), doh, TN)

    return _call(body, name=name, grid=(L // tq,), in_specs=[row, kvs, row, row, row], out_specs=[row, kvs],
                 out_shape=[jax.ShapeDtypeStruct((L, W), BF16), jax.ShapeDtypeStruct((N_MEM, 2 * W), F32)],
                 sem=("arbitrary",))(q, kv, o, lse, do)


def _neg_expm1(z):
    series = -(z * (1.0 + z * (0.5 + z * (1.0 / 6.0 + z * (1.0 / 24.0 + z * (1.0 / 120.0))))))
    return jnp.where(z > -0.05, series, 1.0 - jnp.exp(z))


def _softplus(z):
    return jnp.maximum(z, 0.0) + jnp.log(1.0 + jnp.exp(-jnp.abs(z)))


def _gelu_parts(y):
    c = 0.7978845608028654
    t = jnp.tanh(c * (y + 0.044715 * y * y * y))
    gy = 0.5 * y * (1.0 + t)
    dgy = 0.5 * (1.0 + t) + 0.5 * y * (1.0 - t * t) * c * (1.0 + 3.0 * 0.044715 * y * y)
    return gy, dgy


def _lru_gates(xc, wa_ref, ba, wx_ref, bx, sp):
    rs, igs = [], []
    for hd in range(LRU_HEADS):
        sl = slice(hd * LRU_HEAD_DIM, (hd + 1) * LRU_HEAD_DIM)
        xh = xc[:, sl].astype(BF16)
        rs.append(_sigmoid(_dot(xh, wa_ref[hd]) + ba[:, sl]))
        igs.append(_sigmoid(_dot(xh, wx_ref[hd]) + bx[:, sl]))
    r, ig = jnp.concatenate(rs, axis=1), jnp.concatenate(igs, axis=1)
    la = -LRU_C * r * sp
    return r, ig, jnp.exp(la), _neg_expm1(2.0 * la)


def _conv_taps(x_ext, halo):
    n = x_ext.shape[0]
    return [x_ext[halo:] if k == CONV_WIDTH - 1 else pltpu.roll(x_ext, CONV_WIDTH - 1 - k, 0)[halo:]
            for k in range(CONV_WIDTH)]


def _lru_fwd(proj, cw, cb, wa, ba, wx, bx, lam, *, name, tc=512):
    L = proj.shape[0]
    W = LRU_HEADS * LRU_HEAD_DIM
    nb = L // tc
    whole = lambda shape: pl.BlockSpec(shape, lambda i: (0,) * len(shape))
    specs = [pl.BlockSpec((tc, W), lambda i: (i, 0)), pl.BlockSpec((tc, W), lambda i: (i, 1)),
             pl.BlockSpec((16, W), lambda i: (jnp.maximum(i * (tc // 16) - 1, 0), 0)),
             whole((CONV_WIDTH, W)), whole((1, W)), whole((LRU_HEADS, LRU_HEAD_DIM, LRU_HEAD_DIM)), whole((1, W)),
             whole((LRU_HEADS, LRU_HEAD_DIM, LRU_HEAD_DIM)), whole((1, W)), whole((1, W))]
    out_specs = [pl.BlockSpec((tc, W), lambda i: (i, 0))] * 2
    out_shape = [jax.ShapeDtypeStruct((L, W), BF16), jax.ShapeDtypeStruct((L, W), F32)]

    def body(x_ref, y_ref, xh_ref, cw_ref, cb_ref, wa_ref, ba_ref, wx_ref, bx_ref, lam_ref, rec_ref, hs_ref,
             hcar, a_scr, b_scr):
        i = pl.program_id(0)

        @pl.when(i == 0)
        def _():
            hcar[...] = jnp.zeros_like(hcar)

        halo = jnp.where(i > 0, xh_ref[...].astype(F32), 0.0)
        taps = _conv_taps(jnp.concatenate([halo, x_ref[...].astype(F32)], axis=0), 16)
        xc = cb_ref[...] + sum(cw_ref[k:k + 1, :] * taps[k] for k in range(CONV_WIDTH))
        _, ig, a, om = _lru_gates(xc, wa_ref, ba_ref[...], wx_ref, bx_ref[...], _softplus(-lam_ref[...]))
        b = jnp.sqrt(om) * (ig * xc)
        rowmod = lax.broadcasted_iota(jnp.int32, (tc, W), 0) & 7
        for s in (1, 2, 4):
            keep = rowmod >= s
            b = jnp.where(keep, a * pltpu.roll(b, s, 0) + b, b)
            a = jnp.where(keep, a * pltpu.roll(a, s, 0), a)
        a_scr[...] = a
        b_scr[...] = b

        def tile(j, hc):
            rows = pl.ds(pl.multiple_of(j * 8, 8), 8)
            ht = a_scr[rows, :] * hc + b_scr[rows, :]
            hs_ref[rows, :] = ht
            return jnp.broadcast_to(ht[7:8, :], (8, W))

        hcar[...] = lax.fori_loop(0, tc // 8, tile, hcar[...])
        gy, _ = _gelu_parts(y_ref[...].astype(F32))
        rec_ref[...] = (hs_ref[...] * gy).astype(BF16)

    return _call(body, name=name, grid=(nb,), in_specs=specs, out_specs=out_specs, out_shape=out_shape,
                 scratch=[pltpu.VMEM((8, W), F32), pltpu.VMEM((tc, W), F32), pltpu.VMEM((tc, W), F32)],
                 sem=("arbitrary",))(proj, proj, proj, cw, cb, wa, ba, wx, bx, lam)


def _lru_bwd(proj, hs, drec_src, cw, cb, wa, ba, wx, bx, lam, *, name, tc=256):
    L = proj.shape[0]
    W = LRU_HEADS * LRU_HEAD_DIM
    nb = L // tc
    tb = lambda i: nb - 1 - i
    whole = lambda shape: pl.BlockSpec(shape, lambda i: (0,) * len(shape))
    gate_w = (LRU_HEADS, LRU_HEAD_DIM, LRU_HEAD_DIM)
    specs = [pl.BlockSpec((tc, W), lambda i: (tb(i), 0)), pl.BlockSpec((tc, W), lambda i: (tb(i), 1)),
             pl.BlockSpec((16, W), lambda i: (jnp.maximum(tb(i) * (tc // 16) - 1, 0), 0)),
             pl.BlockSpec((tc, W), lambda i: (tb(i), 0)),
             pl.BlockSpec((8, W), lambda i: (jnp.maximum(tb(i) * (tc // 8) - 1, 0), 0)),
             pl.BlockSpec((tc, W), lambda i: (tb(i), 0)),
             whole((CONV_WIDTH, W)), whole((1, W)), whole(gate_w), whole((1, W)), whole(gate_w), whole((1, W)), whole((1, W))]
    out_specs = [pl.BlockSpec((tc, 2 * W), lambda i: (tb(i), 0)), whole((CONV_WIDTH, W)), whole((1, W)), whole(gate_w),
                 whole((1, W)), whole(gate_w), whole((1, W)), whole((1, W))]
    vec = jax.ShapeDtypeStruct((1, W), F32)
    out_shape = [jax.ShapeDtypeStruct((L, 2 * W), BF16), jax.ShapeDtypeStruct((CONV_WIDTH, W), F32), vec,
                 jax.ShapeDtypeStruct(gate_w, F32), vec, jax.ShapeDtypeStruct(gate_w, F32), vec, vec]

    def body(x_ref, y_ref, xh_ref, hs_ref, hh_ref, dr_ref, cw_ref, cb_ref, wa_ref, ba_ref, wx_ref, bx_ref, lam_ref,
             dxy_ref, dcw_ref, dcb_ref, dwa_ref, dba_ref, dwx_ref, dbx_ref, dlam_ref, gcar, dxc_car, a_scr, b_scr, g_scr):
        pid = pl.program_id(0)
        t = tb(pid)
        accs = (dcw_ref, dcb_ref, dwa_ref, dba_ref, dwx_ref, dbx_ref, dlam_ref)

        @pl.when(pid == 0)
        def _():
            gcar[...] = jnp.zeros_like(gcar)
            dxc_car[...] = jnp.zeros_like(dxc_car)
            for r in accs:
                r[...] = jnp.zeros_like(r)

        halo = jnp.where(t > 0, xh_ref[...].astype(F32), 0.0)
        taps = _conv_taps(jnp.concatenate([halo, x_ref[...].astype(F32)], axis=0), 16)
        xc = cb_ref[...] + sum(cw_ref[k:k + 1, :] * taps[k] for k in range(CONV_WIDTH))
        lam = lam_ref[...]
        sp = _softplus(-lam)
        r, ig, a, om = _lru_gates(xc, wa_ref, ba_ref[...], wx_ref, bx_ref[...], sp)
        sq = jnp.sqrt(om)
        hblk = hs_ref[...]
        hprev = pltpu.roll(jnp.concatenate([jnp.where(t > 0, hh_ref[...], 0.0), hblk], axis=0), 1, 0)[8:]
        gy, dgy = _gelu_parts(y_ref[...].astype(F32))
        drec = dr_ref[...].astype(F32)
        dxy_ref[:, W:] = (drec * hblk * dgy).astype(BF16)

        rowidx = lax.broadcasted_iota(jnp.int32, (tc, W), 0)
        rowmod = rowidx & 7
        ca = jnp.where(rowidx == tc - 1, 1.0, pltpu.roll(a, tc - 1, 0))
        cbv = drec * gy
        for s in (1, 2, 4):
            keep = rowmod < 8 - s
            cbv = jnp.where(keep, ca * pltpu.roll(cbv, tc - s, 0) + cbv, cbv)
            ca = jnp.where(keep, ca * pltpu.roll(ca, tc - s, 0), ca)
        a_scr[...] = ca
        b_scr[...] = cbv

        def tile(k, gc):
            j = tc // 8 - 1 - k
            rows = pl.ds(pl.multiple_of(j * 8, 8), 8)
            gt = a_scr[rows, :] * gc + b_scr[rows, :]
            g_scr[rows, :] = gt
            return jnp.broadcast_to(gt[0:1, :], (8, W))

        lax.fori_loop(0, tc // 8, tile, gcar[...])
        G = g_scr[...]
        gcar[...] = jnp.broadcast_to(a[0:1, :] * G[0:1, :], (8, W))

        da = G * hprev
        dsq = G * (ig * xc)
        di = G * (sq * xc)
        dxc = G * (sq * ig)
        dla = da * a - 2.0 * a * a * (dsq * 0.5 * lax.rsqrt(om))
        dlam_ref[...] += jnp.sum(dla * (-LRU_C * r), axis=0, keepdims=True) * (-_sigmoid(-lam))
        dpr = dla * (-LRU_C * sp) * r * (1.0 - r)
        dpi = di * ig * (1.0 - ig)
        dba_ref[...] += jnp.sum(dpr, axis=0, keepdims=True)
        dbx_ref[...] += jnp.sum(dpi, axis=0, keepdims=True)
        back = []
        for hd in range(LRU_HEADS):
            sl = slice(hd * LRU_HEAD_DIM, (hd + 1) * LRU_HEAD_DIM)
            xh, dprh, dpih = xc[:, sl].astype(BF16), dpr[:, sl].astype(BF16), dpi[:, sl].astype(BF16)
            back.append(_dot(dprh, wa_ref[hd], NT) + _dot(dpih, wx_ref[hd], NT))
            dwa_ref[hd] += _dot(xh, dprh, TN)
            dwx_ref[hd] += _dot(xh, dpih, TN)
        dxc = dxc + jnp.concatenate(back, axis=1)
        dcb_ref[...] += jnp.sum(dxc, axis=0, keepdims=True)
        for k in range(CONV_WIDTH):
            dcw_ref[k:k + 1, :] += jnp.sum(dxc * taps[k], axis=0, keepdims=True)
        ext = jnp.concatenate([dxc, dxc_car[...]], axis=0)
        dx = cw_ref[CONV_WIDTH - 1:CONV_WIDTH, :] * dxc
        for k in range(CONV_WIDTH - 1):
            dx = dx + cw_ref[k:k + 1, :] * pltpu.roll(ext, tc + 8 - (CONV_WIDTH - 1 - k), 0)[:tc]
        dxc_car[...] = dxc[0:8, :]
        dxy_ref[:, :W] = dx.astype(BF16)

    scratch = [pltpu.VMEM((8, W), F32), pltpu.VMEM((8, W), F32)] + [pltpu.VMEM((tc, W), F32)] * 3
    return _call(body, name=name, grid=(nb,), in_specs=specs, out_specs=out_specs, out_shape=out_shape, scratch=scratch,
                 sem=("arbitrary",))(proj, proj, proj, hs, hs, drec_src, cw, cb, wa, ba, wx, bx, lam)


def _final_loss(h, gain, target, *, name, tm=256):
    M, K = h.shape
    row = pl.BlockSpec((tm, K), lambda i: (i, 0))
    vec = pl.BlockSpec((1, K), lambda i: (0, 0))
    one = pl.BlockSpec((1, 128), lambda i: (0, 0))

    def body(h_ref, g_ref, t_ref, dh_ref, dg_ref, loss_ref):
        @pl.when(pl.program_id(0) == 0)
        def _():
            dg_ref[...] = jnp.zeros_like(dg_ref)
            loss_ref[...] = jnp.zeros_like(loss_ref)

        x = h_ref[...]
        r = lax.rsqrt(jnp.mean(x * x, axis=-1, keepdims=True) + NORM_EPS)
        xhat = x * r
        err = xhat * g_ref[...] - t_ref[...]
        loss_ref[...] += 0.5 / K * jnp.sum(err * err)
        dy = err * (1.0 / K)
        dg_ref[...] += jnp.sum(dy * xhat, axis=0, keepdims=True)
        dxh = dy * g_ref[...]
        dh_ref[...] = r * (dxh - xhat * jnp.mean(dxh * xhat, axis=-1, keepdims=True))

    return _call(body, name=name, grid=(M // tm,), in_specs=[row, vec, row], out_specs=[row, vec, one],
                 out_shape=[jax.ShapeDtypeStruct((M, K), F32), jax.ShapeDtypeStruct((1, K), F32),
                            jax.ShapeDtypeStruct((1, 128), F32)], sem=("arbitrary",))(h, gain.reshape(1, K), target)


def _dilated_merge(branches, *, name, tm=512):
    L, W = branches[0].shape
    nbr = len(branches) // 2
    row = pl.BlockSpec((tm, W), lambda i: (i, 0))

    def body(*refs):
        o_ref, lse_ref = refs[-2], refs[-1]
        lses = [refs[2 * b + 1][...] for b in range(nbr)]
        m = lses[0]
        for t in lses[1:]:
            m = jnp.maximum(m, t)
        ws = [jnp.exp(t - m) for t in lses]
        den = ws[0]
        for t in ws[1:]:
            den = den + t
        acc = ws[0] * refs[0][...].astype(F32)
        for b in range(1, nbr):
            acc = acc + ws[b] * refs[2 * b][...].astype(F32)
        o_ref[...] = (acc / den).astype(BF16)
        lse_ref[...] = m + jnp.log(den)

    return _call(body, name=name, grid=(L // tm,), in_specs=[row] * (2 * nbr), out_specs=[row, row],
                 out_shape=[jax.ShapeDtypeStruct((L, W), BF16), jax.ShapeDtypeStruct((L, W), F32)], sem=("parallel",))(*branches)


def _dilated_fwd(proj0):
    L = proj0.shape[0]
    qkv = proj0[:, 2 * D_MODEL:]
    W = B_HEADS * HEAD_DIM
    outs = []
    for window, d in DILATED_PATTERN:
        view = qkv.reshape(L // d, d * 3 * W)
        o, lse = _band_fwd(view, view, view, d=d, nq=B_HEADS, nkv=B_HEADS, qcol=lambda r: 3 * r, kcol=lambda r: 3 * r + 1,
                           vcol=lambda r: 3 * r + 2, max_dist=window // d, name=f"dilated_fwd_d{d}")
        outs += [o.reshape(L, W), lse.reshape(L, W)]
    return _dilated_merge(outs, name="dilated_merge")


def _dilated_bwd(proj0, att, lse, datt, tabs):
    L = proj0.shape[0]
    qkv = proj0[:, 2 * D_MODEL:]
    Wh = B_HEADS * HEAD_DIM
    branches = []
    for window, d in DILATED_PATTERN:
        view = qkv.reshape(L // d, d * 3 * Wh)
        v1 = lambda t: t.reshape(L // d, d * Wh)
        outs = _band_bwd(view, view, view, v1(datt), v1(att), v1(lse), d=d, nq=B_HEADS, nkv=B_HEADS,
                         qcol=lambda r: 3 * r, kcol=lambda r: 3 * r + 1, vcol=lambda r: 3 * r + 2, docol=lambda r: r,
                         max_dist=window // d, name=f"dilated_bwd_d{d}")
        branches.append([o.reshape(L, Wh) for o in outs])
    return _attn_grad_combine(branches, tabs, name="dilated_grad_combine")


def _device_step(x, mem, target, w):
    L = x.shape[0]
    tabs = _rope_tables(L)
    g = {}
    saved = []
    h = x
    for layer in range(2):
        sv = {"h_mix": h}
        if layer == 0:
            proj, n = _rowmm(h, w["ab_w_in"], name="l0_in_proj", gain=w["mix_norm"][0],
                             rope=(2 * D_MODEL, 2 * D_MODEL + 2 * B_HEADS * HEAD_DIM, tabs))
            rec, hs = _lru_fwd(proj, w["lru_conv_w"], w["lru_conv_b"], w["lru_wa"], w["lru_ba"], w["lru_wx"], w["lru_bx"],
                               w["lru_lambda"], name="lru_fwd")
            att, lse = _dilated_fwd(proj)
            mix = jnp.concatenate([rec, att], axis=1)
            (h,) = _rowmm(mix, w["ab_w_out"], name="l0_out_proj", res=h)
            sv.update(hs=hs)
        else:
            proj, n = _rowmm(h, w["c_w_qkv"], name="l1_qkv_proj", gain=w["mix_norm"][1], bias=w["c_b_qkv"],
                             rope=(0, (C_HEADS + C_KV_HEADS) * HEAD_DIM, tabs))
            mix, lse = _band_fwd(proj, proj, proj, d=1, nq=C_HEADS, nkv=C_KV_HEADS, qcol=lambda r: 0, kcol=lambda r: 8,
                                 vcol=lambda r: 9, max_dist=C_WINDOW - 1, sinks=w["c_sinks"], name="swa_fwd")
            (h,) = _rowmm(mix, w["c_w_out"], name="l1_out_proj", res=h, bias=w["c_b_out"])
        sv.update(proj=proj, n_mix=n, mix=mix, lse=lse, h_xa=h)
        xq, nx = _rowmm(h, w["xa_wq"][layer][None], name=f"xa_q_proj{layer}", gain=w["xa_norm"][layer])
        kv, nm = _rowmm(mem, w["xa_wkv"][layer][None], name=f"xa_kv_proj{layer}", gain=w["xa_mem_norm"][layer])
        xo, xlse = _xattn_fwd(xq, kv, name=f"xa_fwd{layer}")
        (h,) = _rowmm(xo, w["xa_wo"][layer], name=f"xa_out_proj{layer}", res=h)
        sv.update(xq=xq, nx=nx, kv=kv, nm=nm, xo=xo, xlse=xlse, h_ffn=h)
        gu, nf, act = _rowmm(h, w["ffn_w_gate_up"][layer], name=f"ffn_in{layer}", gain=w["ffn_norm"][layer], swiglu=True)
        (h,) = _rowmm(act, w["ffn_w_down"][layer][None], name=f"ffn_out{layer}", res=h, tm=512)
        sv.update(gu=gu, nf=nf, act=act)
        saved.append(sv)

    dh, g["final_norm"], loss = _final_loss(h, w["final_norm"], target, name="final_loss")

    stk = {k: [None, None] for k in ("xa_norm", "xa_mem_norm", "ffn_norm", "mix_norm", "xa_wq", "xa_wkv", "xa_wo",
                                      "ffn_w_gate_up", "ffn_w_down")}
    for layer in (1, 0):
        sv = saved[layer]
        (stk["ffn_w_down"][layer],) = _mm_tn(sv["act"], dh, S=1, name=f"ffn_down_dw{layer}", kk=D_FF // 2)
        (dgu,) = _mm_nt(dh, w["ffn_w_down"][layer][None], name=f"ffn_dact{layer}", mode="swiglu", kchunk=D_FF // 2, gu=sv["gu"])
        (stk["ffn_w_gate_up"][layer],) = _mm_tn(sv["nf"], dgu, S=N_CHIPS, name=f"ffn_gu_dw{layer}")
        dh, stk["ffn_norm"][layer] = _mm_nt(dgu, w["ffn_w_gate_up"][layer], name=f"ffn_dx{layer}", mode="norm",
                                            h=sv["h_ffn"], gain=w["ffn_norm"][layer], dh=dh)
        (stk["xa_wo"][layer],) = _mm_tn(sv["xo"], dh, S=N_CHIPS, name=f"xa_wo_dw{layer}")
        (dxo,) = _mm_nt(dh, w["xa_wo"][layer], name=f"xa_dxo{layer}", mode="plain")
        dxq, dkv = _xattn_bwd(sv["xq"], sv["kv"], sv["xo"], sv["xlse"], dxo, name=f"xa_bwd{layer}")
        (stk["xa_wq"][layer],) = _mm_tn(sv["nx"], dxq, S=1, name=f"xa_wq_dw{layer}")
        dh, stk["xa_norm"][layer] = _mm_nt(dxq, w["xa_wq"][layer][None], name=f"xa_dx{layer}", mode="norm", h=sv["h_xa"],
                                           gain=w["xa_norm"][layer], dh=dh)
        (stk["xa_wkv"][layer],) = _mm_tn(sv["nm"], dkv, S=1, name=f"xa_wkv_dw{layer}")
        _, stk["xa_mem_norm"][layer] = _mm_nt(dkv, w["xa_wkv"][layer][None], name=f"xa_dmem{layer}", mode="norm", h=mem,
                                              gain=w["xa_mem_norm"][layer])
        if layer == 1:
            g["c_w_out"], g["c_b_out"] = _mm_tn(sv["mix"], dh, S=1, name="l1_out_dw", bias=True)
            (dmix,) = _mm_nt(dh, w["c_w_out"], name="l1_dmix", mode="plain")
            dq, dk, dv, dsk = _band_bwd(sv["proj"], sv["proj"], sv["proj"], dmix, sv["mix"], sv["lse"], d=1, nq=C_HEADS,
                                        nkv=C_KV_HEADS, qcol=lambda r: 0, kcol=lambda r: 8, vcol=lambda r: 9,
                                        docol=lambda r: 0, max_dist=C_WINDOW - 1, sinks=w["c_sinks"], name="swa_bwd")
            g["c_sinks"] = dsk[0, :C_HEADS]
            dproj = _attn_grad_combine([(dq, dk, dv)], tabs, name="swa_grad_combine")
            g["c_w_qkv"], g["c_b_qkv"] = _mm_tn(sv["n_mix"], dproj, S=1, name="l1_qkv_dw", bias=True)
            dh, stk["mix_norm"][1] = _mm_nt(dproj, w["c_w_qkv"], name="l1_dx", mode="norm", h=sv["h_mix"],
                                            gain=w["mix_norm"][1], dh=dh)
        else:
            (g["ab_w_out"],) = _mm_tn(sv["mix"], dh, S=1, name="l0_out_dw", kk=768)
            (dmix,) = _mm_nt(dh, w["ab_w_out"], name="l0_dmix", mode="plain", kchunk=768)
            (dxy, g["lru_conv_w"], g["lru_conv_b"], g["lru_wa"], g["lru_ba"], g["lru_wx"], g["lru_bx"],
             g["lru_lambda"]) = _lru_bwd(sv["proj"], sv["hs"], dmix, w["lru_conv_w"], w["lru_conv_b"], w["lru_wa"],
                                         w["lru_ba"], w["lru_wx"], w["lru_bx"], w["lru_lambda"], name="lru_bwd")
            dqkv = _dilated_bwd(sv["proj"], sv["mix"][:, D_MODEL:], sv["lse"], dmix[:, D_MODEL:], tabs)
            dproj = jnp.concatenate([dxy, dqkv], axis=1)
            (g["ab_w_in"],) = _mm_tn(sv["n_mix"], dproj, S=N_CHIPS, name="l0_in_dw")
            dh, stk["mix_norm"][0] = _mm_nt(dproj, w["ab_w_in"], name="l0_dx", mode="norm", h=sv["h_mix"],
                                            gain=w["mix_norm"][0], dh=dh)
    for k, v in stk.items():
        g[k] = jnp.concatenate(v, axis=0) if v[0].shape[0] == 1 else jnp.stack(v, axis=0)
    return loss[0, 0], dh, g


ANY = pl.BlockSpec(memory_space=pl.ANY)
MESH = pl.DeviceIdType.MESH


def _place():
    x, y, c = lax.axis_index("x"), lax.axis_index("y"), lax.axis_index("c")
    return x, y, c, [(1 - x, y), (x, 1 - y), (1 - x, 1 - y)]


def _remote(send_sems, recv_sems):
    def copy(k, src, dst, to):
        return pltpu.make_async_remote_copy(src_ref=src, dst_ref=dst, send_sem=send_sems.at[k], recv_sem=recv_sems.at[k],
                                            device_id=to, device_id_type=MESH)
    return copy


def _gather_weights(wpack, spack):
    R = wpack.shape[0]
    Rh = R // 2

    def body(w_ref, s_ref, wf_ref, sf_ref, send_sems, recv_sems):
        x, y, c, chips = _place()
        me, sib = 2 * x + y, (x, y, 1 - c)
        copy = _remote(send_sems, recv_sems)
        half = lambda chip, hh: wf_ref.at[chip, pl.ds(hh * Rh, Rh), :]
        sends = []
        for j, (cx, cy) in enumerate(chips):
            sends.append(copy(j, w_ref.at[pl.ds(c * Rh, Rh), :], half(me, c), (cx, cy, c)))
            sends.append(copy(3 + j, s_ref, sf_ref.at[me], (cx, cy, c)))
        for cp in sends:
            cp.start()
        for j, (cx, cy) in enumerate(chips):
            got = half(2 * cx + cy, c)
            copy(j, got, got, sib).wait_recv()
            fwd = copy(6 + j, got, got, sib)
            fwd.start()
            sends.append(fwd)
        for j, (cx, cy) in enumerate(chips):
            got = half(2 * cx + cy, 1 - c)
            copy(6 + j, got, got, sib).wait_recv()
            copy(3 + j, s_ref, sf_ref.at[2 * cx + cy], sib).wait_recv()
        for cp in sends:
            cp.wait_send()

    out_shape = [jax.ShapeDtypeStruct((N_CHIPS,) + wpack.shape, wpack.dtype),
                 jax.ShapeDtypeStruct((N_CHIPS,) + spack.shape, spack.dtype)]
    wfull, sfull = pl.pallas_call(body, name="gather_weights", out_shape=out_shape, in_specs=[ANY, ANY], out_specs=[ANY, ANY],
                                  scratch_shapes=[pltpu.SemaphoreType.DMA((9,)), pltpu.SemaphoreType.DMA((9,))])(wpack, spack)
    chip = 2 * lax.axis_index("x") + lax.axis_index("y")
    return (lax.dynamic_update_index_in_dim(wfull, wpack, chip, 0), lax.dynamic_update_index_in_dim(sfull, spack, chip, 0))


def _rs_pair_exchange(gpack):
    _, R, C = gpack.shape
    Rh = R // 2

    def body(g_ref, ra_ref, send_sems, recv_sems):
        x, y, c, _ = _place()
        copy = _remote(send_sems, recv_sems)
        cps = [copy(j, g_ref.at[j, pl.ds((1 - c) * Rh, Rh), :], ra_ref.at[j], (x, y, 1 - c)) for j in range(N_CHIPS)]
        for cp in cps:
            cp.start()
        for cp in cps:
            cp.wait()

    return pl.pallas_call(body, name="rs_pair_exchange", out_shape=jax.ShapeDtypeStruct((N_CHIPS, Rh, C), gpack.dtype),
                          in_specs=[ANY], out_specs=ANY,
                          scratch_shapes=[pltpu.SemaphoreType.DMA((N_CHIPS,)), pltpu.SemaphoreType.DMA((N_CHIPS,))])(gpack)


def _rs_pair_add(place, gpack, ra, *, tr=PACK_ROWS // 6):
    _, R, C = gpack.shape
    Rh = R // 2
    nrb = Rh // tr

    def body(p_ref, g_ref, ra_ref, pair_ref, own_ref):
        s = g_ref[...].astype(F32) + ra_ref[...].astype(F32)
        pair_ref[...] = s.astype(BF16)

        @pl.when(pl.program_id(1) == p_ref[1])
        def _():
            own_ref[...] = s

    grid_spec = pltpu.PrefetchScalarGridSpec(
        num_scalar_prefetch=1, grid=(nrb, N_CHIPS),
        in_specs=[pl.BlockSpec((None, tr, C), lambda i, j, p: (j, p[0] * nrb + i, 0)),
                  pl.BlockSpec((None, tr, C), lambda i, j, p: (j, i, 0))],
        out_specs=[pl.BlockSpec((None, tr, C), lambda i, j, p: (j, i, 0)), pl.BlockSpec((tr, C), lambda i, j, p: (i, 0))])
    return pl.pallas_call(
        body, name="rs_pair_add", grid_spec=grid_spec,
        out_shape=[jax.ShapeDtypeStruct((N_CHIPS, Rh, C), BF16), jax.ShapeDtypeStruct((Rh, C), F32)],
        compiler_params=pltpu.CompilerParams(dimension_semantics=("arbitrary", "arbitrary"),
                                             vmem_limit_bytes=VMEM_LIMIT_V7X))(place, gpack, ra)


def _rs_chip_exchange(pair, small):
    _, Rh, C = pair.shape

    def body(p_ref, s_ref, rb_ref, rs_ref, send_sems, recv_sems, local_sem):
        x, y, c, chips = _place()
        copy = _remote(send_sems, recv_sems)
        dev = 4 * x + 2 * y + c
        mine = pltpu.make_async_copy(s_ref, rs_ref.at[dev], local_sem.at[0])
        mine.start()
        cps = [copy(j, p_ref.at[2 * cx + cy], rb_ref.at[j], (cx, cy, c)) for j, (cx, cy) in enumerate(chips)]
        peers = []
        for k in range(1, 8):
            px = 1 - x if k & 4 else x
            py = 1 - y if k & 2 else y
            pc = 1 - c if k & 1 else c
            peers.append((px, py, pc))
            cps.append(copy(2 + k, s_ref, rs_ref.at[dev], (px, py, pc)))
        for cp in cps:
            cp.start()
        for j in range(3):
            copy(j, p_ref.at[0], rb_ref.at[j], (x, y, c)).wait_recv()
        for k, (px, py, pc) in enumerate(peers, start=1):
            copy(2 + k, s_ref, rs_ref.at[4 * px + 2 * py + pc], (x, y, c)).wait_recv()
        for cp in cps:
            cp.wait_send()
        mine.wait()

    out_shape = [jax.ShapeDtypeStruct((3, Rh, C), pair.dtype), jax.ShapeDtypeStruct((8,) + small.shape, small.dtype)]
    return pl.pallas_call(body, name="rs_chip_exchange", out_shape=out_shape, in_specs=[ANY, ANY], out_specs=[ANY, ANY],
                          scratch_shapes=[pltpu.SemaphoreType.DMA((10,)), pltpu.SemaphoreType.DMA((10,)),
                                          pltpu.SemaphoreType.DMA((1,))])(pair, small)


def _rs_final_add(place, own, rb, *, tr=PACK_ROWS // 6):
    Rh, C = own.shape
    nrb = Rh // tr

    def body(p_ref, o_ref, rb_ref, f_ref):
        f_ref[...] = ((o_ref[...] + rb_ref[0].astype(F32)) + rb_ref[1].astype(F32)) + rb_ref[2].astype(F32)

    grid_spec = pltpu.PrefetchScalarGridSpec(
        num_scalar_prefetch=1, grid=(nrb,),
        in_specs=[pl.BlockSpec((tr, C), lambda i, p: (i, 0)), pl.BlockSpec((3, tr, C), lambda i, p: (0, i, 0))],
        out_specs=pl.BlockSpec((tr, C), lambda i, p: (p[0] * nrb + i, 0)))
    return pl.pallas_call(
        body, name="rs_final_add", grid_spec=grid_spec, out_shape=jax.ShapeDtypeStruct((2 * Rh, C), F32),
        compiler_params=pltpu.CompilerParams(dimension_semantics=("arbitrary",), vmem_limit_bytes=VMEM_LIMIT_V7X))(place, own, rb)


def _sum_slots(rs):
    n, rows, C = rs.shape

    def body(r_ref, o_ref):
        acc = r_ref[0]
        for k in range(1, n):
            acc = acc + r_ref[k]
        o_ref[...] = acc

    return _call(body, name="small_grad_sum", grid=(1,), in_specs=[pl.BlockSpec((n, rows, C), lambda i: (0, 0, 0))],
                 out_specs=pl.BlockSpec((rows, C), lambda i: (0, 0)), out_shape=jax.ShapeDtypeStruct((rows, C), F32),
                 sem=("arbitrary",))(rs)


def _rs_sibling_share(gbuf):
    R, C = gbuf.shape
    Rh = R // 2

    def body(f_ref, g_ref, send_sems, recv_sems):
        x, y, c, _ = _place()
        copy = _remote(send_sems, recv_sems)
        mine = g_ref.at[pl.ds(c * Rh, Rh), :]
        theirs = g_ref.at[pl.ds((1 - c) * Rh, Rh), :]
        out = copy(0, mine, mine, (x, y, 1 - c))
        out.start()
        copy(0, theirs, theirs, (x, y, c)).wait_recv()
        out.wait_send()

    return pl.pallas_call(body, name="rs_sibling_share", out_shape=jax.ShapeDtypeStruct((R, C), gbuf.dtype),
                          in_specs=[ANY], out_specs=ANY, input_output_aliases={0: 0},
                          scratch_shapes=[pltpu.SemaphoreType.DMA((1,)), pltpu.SemaphoreType.DMA((1,))])(gbuf)


def _adamw(w, g, m, v, *, name):
    rows, cols = w.shape
    tr = rows
    for cand in range(min(rows, 512), 7, -8):
        if rows % cand == 0:
            tr = cand
            break
    spec = pl.BlockSpec((tr, cols), lambda i: (i, 0))

    def body(w_ref, g_ref, m_ref, v_ref, d_ref, nm_ref, nv_ref):
        gg = g_ref[...]
        nm = ADAM_B1 * m_ref[...] + (1.0 - ADAM_B1) * gg
        nv = ADAM_B2 * v_ref[...] + (1.0 - ADAM_B2) * (gg * gg)
        m_hat = nm / (1.0 - ADAM_B1 ** ADAM_STEP)
        v_hat = nv / (1.0 - ADAM_B2 ** ADAM_STEP)
        d_ref[...] = -ADAM_LR * (m_hat / (jnp.sqrt(v_hat) + ADAM_EPS) + ADAM_WD * w_ref[...])
        nm_ref[...] = nm
        nv_ref[...] = nv

    return _call(body, name=name, grid=(rows // tr,), in_specs=[spec] * 4, out_specs=[spec] * 3,
                 out_shape=[jax.ShapeDtypeStruct((rows, cols), F32)] * 3, sem=("parallel",))(w, g, m, v)


WEIGHT_NAMES = ("mix_norm", "ab_w_in", "lru_conv_w", "lru_conv_b", "lru_wa", "lru_ba", "lru_wx", "lru_bx", "lru_lambda",
                "ab_w_out", "c_w_qkv", "c_b_qkv", "c_sinks", "c_w_out", "c_b_out", "xa_norm", "xa_mem_norm", "xa_wq",
                "xa_wkv", "xa_wo", "ffn_norm", "ffn_w_gate_up", "ffn_w_down", "final_norm")
BIG = ("ab_w_in", "lru_wa", "lru_wx", "ab_w_out", "c_w_qkv", "c_w_out", "xa_wq", "xa_wkv", "xa_wo", "ffn_w_gate_up",
       "ffn_w_down")
REPLICATED = ("mix_norm", "lru_conv_b", "lru_lambda", "c_sinks", "xa_norm", "xa_mem_norm", "ffn_norm", "final_norm")
SMALL_SHARDED = ("lru_conv_w", "lru_ba", "lru_bx", "c_b_qkv", "c_b_out")
LANES = 1024


def _rows(v):
    flat = v.reshape(-1)
    return jnp.pad(flat, (0, -flat.shape[0] % LANES)).reshape(-1, LANES)


def _stack_rows(parts, total):
    return jnp.concatenate(parts + [jnp.zeros((total - sum(p.shape[0] for p in parts), LANES), parts[0].dtype)], axis=0)


def _pack_small(parts, total, *, name):
    def body(*refs):
        o_ref = refs[-1]
        o_ref[...] = jnp.zeros_like(o_ref)
        row = 0
        for p_ref in refs[:-1]:
            o_ref[row:row + p_ref.shape[0], :] = p_ref[...]
            row += p_ref.shape[0]

    return _call(body, name=name, grid=(1,), in_specs=[pl.BlockSpec(p.shape, lambda i: (0, 0)) for p in parts],
                 out_specs=pl.BlockSpec((total, LANES), lambda i: (0, 0)),
                 out_shape=jax.ShapeDtypeStruct((total, LANES), F32), sem=("arbitrary",))(*parts)


def _unshard(name, t):
    if name in ("ab_w_in", "ab_w_out", "c_w_out"):
        return t if name == "ab_w_in" else t.reshape(1, -1, t.shape[-1])
    if name in ("lru_wa", "lru_wx"):
        return t.transpose(1, 0, 2, 3).reshape(LRU_HEADS, LRU_HEAD_DIM, LRU_HEAD_DIM)
    if name == "c_w_qkv":
        return t.transpose(1, 0, 2).reshape(1, D_MODEL, -1)
    if name in ("xa_wq", "xa_wkv", "ffn_w_down"):
        return t.transpose(1, 0, 2, 3).reshape(2, -1, t.shape[-1])
    return t.transpose(1, 0, 2, 3)


def _to_shards(name, g):
    if name in ("ab_w_in", "ab_w_out", "c_w_out"):
        return g.reshape(N_CHIPS, -1)
    if name in ("lru_wa", "lru_wx"):
        return g.reshape(LRU_HEADS, N_CHIPS, -1).transpose(1, 0, 2).reshape(N_CHIPS, -1)
    if name == "c_w_qkv":
        return g.reshape(D_MODEL, N_CHIPS, -1).transpose(1, 0, 2).reshape(N_CHIPS, -1)
    if name in ("xa_wq", "xa_wkv", "ffn_w_down"):
        return g.reshape(2, N_CHIPS, -1).transpose(1, 0, 2).reshape(N_CHIPS, -1)
    return g.transpose(1, 0, 2, 3).reshape(N_CHIPS, -1)


def kernel(x, mem, mix_norm, ab_w_in, lru_conv_w, lru_conv_b, lru_wa, lru_ba, lru_wx, lru_bx, lru_lambda, ab_w_out, c_w_qkv, c_b_qkv, c_sinks, c_w_out, c_b_out, xa_norm, xa_mem_norm, xa_wq, xa_wkv, xa_wo, ffn_norm, ffn_w_gate_up, ffn_w_down, final_norm, loss_target, m_mix_norm, m_ab_w_in, m_lru_conv_w, m_lru_conv_b, m_lru_wa, m_lru_ba, m_lru_wx, m_lru_bx, m_lru_lambda, m_ab_w_out, m_c_w_qkv, m_c_b_qkv, m_c_sinks, m_c_w_out, m_c_b_out, m_xa_norm, m_xa_mem_norm, m_xa_wq, m_xa_wkv, m_xa_wo, m_ffn_norm, m_ffn_w_gate_up, m_ffn_w_down, m_final_norm, v_mix_norm, v_ab_w_in, v_lru_conv_w, v_lru_conv_b, v_lru_wa, v_lru_ba, v_lru_wx, v_lru_bx, v_lru_lambda, v_ab_w_out, v_c_w_qkv, v_c_b_qkv, v_c_sinks, v_c_w_out, v_c_b_out, v_xa_norm, v_xa_mem_norm, v_xa_wq, v_xa_wkv, v_xa_wo, v_ffn_norm, v_ffn_w_gate_up, v_ffn_w_down, v_final_norm):
    given = dict(locals())
    wl = {n: given[n] for n in WEIGHT_NAMES}
    ml = {n: given["m_" + n] for n in WEIGHT_NAMES}
    vl = {n: given["v_" + n] for n in WEIGHT_NAMES}
    xi, yi, ci = lax.axis_index("x"), lax.axis_index("y"), lax.axis_index("c")
    chip = 2 * xi + yi

    big_rows = {n: wl[n].size // LANES for n in BIG}
    wpack = _stack_rows([wl[n].astype(BF16).reshape(big_rows[n], LANES) for n in BIG], PACK_ROWS)
    spack = _pack_small([_rows(wl[n]) for n in SMALL_SHARDED], 8, name="pack_small_weights")
    wfull, sfull = _gather_weights(wpack, spack)
    w = {n: wl[n] for n in REPLICATED}
    w["c_sinks"] = wl["c_sinks"][0]
    off = 0
    for n in BIG:
        loc = wl[n].shape[1:] if wl[n].shape[0] == 1 else wl[n].shape
        w[n] = _unshard(n, wfull[:, off:off + big_rows[n]].reshape((N_CHIPS,) + loc))
        off += big_rows[n]
    for r, n in enumerate(SMALL_SHARDED):
        loc = wl[n].shape[1:]
        t = sfull[:, r, :wl[n].size].reshape((N_CHIPS,) + loc)
        if n == "lru_conv_w":
            w[n] = t.transpose(1, 0, 2).reshape(CONV_WIDTH, -1)
        elif n in ("lru_ba", "lru_bx"):
            w[n] = t.transpose(1, 0, 2).reshape(1, -1)
        else:
            w[n] = t.reshape(1, -1)

    loss_part, grad_x, g = _device_step(x[0], mem[0], loss_target[0], w)

    gparts = [_to_shards(n, g[n].astype(BF16)).reshape(N_CHIPS, big_rows[n], LANES) for n in BIG]
    gparts.append(jnp.zeros((N_CHIPS, PACK_ROWS - off, LANES), BF16))
    gpack = jnp.concatenate(gparts, axis=1)
    small_parts = [_rows(g[n]) for n in REPLICATED] + [_rows(jnp.broadcast_to(loss_part, (LANES,)))]
    small_parts += [_rows(g[n]) for n in SMALL_SHARDED]
    small = _pack_small(small_parts, 24, name="pack_small_grads")
    place = jnp.stack([ci, chip]).astype(jnp.int32)
    ra = _rs_pair_exchange(gpack)
    pair, own = _rs_pair_add(place, gpack, ra)
    rb, rs = _rs_chip_exchange(pair, small)
    gsum = _rs_sibling_share(_rs_final_add(place, own, rb))
    ssum = _sum_slots(rs)

    grads = {}
    off = 0
    for n in BIG:
        grads[n] = gsum[off:off + big_rows[n]].reshape(wl[n].shape)
        off += big_rows[n]
    row = 0
    for n in REPLICATED:
        k = _rows(g[n]).shape[0]
        grads[n] = ssum[row:row + k].reshape(-1)[:wl[n].size].reshape(wl[n].shape)
        row += k
    loss = ssum[row, 0]
    row += 1
    for n in SMALL_SHARDED:
        k = _rows(g[n]).shape[0]
        full = ssum[row:row + k].reshape(-1)[:g[n].size]
        row += k
        loc = wl[n].shape
        if n == "lru_conv_w":
            sh = full.reshape(CONV_WIDTH, N_CHIPS, -1)
        elif n in ("lru_ba", "lru_bx"):
            sh = full.reshape(LRU_HEADS, N_CHIPS, -1)
        else:
            sh = full.reshape(1, N_CHIPS, -1)
        grads[n] = lax.dynamic_index_in_dim(sh, chip, axis=1, keepdims=False).reshape(loc)

    delta, new_m, new_v = {}, {}, {}
    for n in BIG:
        shape2 = (wl[n].size // wl[n].shape[-1], wl[n].shape[-1])
        d, nm, nv = _adamw(wl[n].reshape(shape2), grads[n].reshape(shape2), ml[n].reshape(shape2), vl[n].reshape(shape2),
                           name="adamw_" + n)
        delta[n], new_m[n], new_v[n] = (t.reshape(wl[n].shape) for t in (d, nm, nv))
    smalls = REPLICATED + SMALL_SHARDED
    packs = [_pack_small([_rows(src[n]) for n in smalls], 24, name="pack_adamw_" + tag)
             for tag, src in (("w", wl), ("g", grads), ("m", ml), ("v", vl))]
    outs = _adamw(*packs, name="adamw_small")
    row = 0
    for n in smalls:
        k = _rows(wl[n]).shape[0]
        for dst, o in zip((delta, new_m, new_v), outs):
            dst[n] = o[row:row + k].reshape(-1)[:wl[n].size].reshape(wl[n].shape)
        row += k

    return (loss, grad_x[None], *[grads[n] for n in WEIGHT_NAMES], *[delta[n] for n in WEIGHT_NAMES],
            *[new_m[n] for n in WEIGHT_NAMES], *[new_v[n] for n in WEIGHT_NAMES])
```

```python
import jax
import jax.numpy as jnp
from jax import lax
from jax.experimental import pallas as pl
from jax.experimental.pallas import tpu as pltpu

F32, BF16 = jnp.float32, jnp.bfloat16
D_MODEL = 1024
NORM_EPS = 1e-6
ROPE_THETA = 500000.0
HEAD_DIM = 64
ROT_DIM = 16
BLK = 128
LRU_HEADS, LRU_HEAD_DIM, CONV_WIDTH, LRU_C = 4, 256, 4, 8.0
DILATED_PATTERN = ((128, 1), (512, 4), (2048, 16))
B_HEADS, C_HEADS, C_KV_HEADS, C_WINDOW = 8, 16, 2, 128
XA_HEADS, XA_HEAD_DIM, N_MEM = 4, 128, 256
D_FF = 2816
NEG = -1e30
ADAM_LR, ADAM_B1, ADAM_B2, ADAM_EPS, ADAM_WD, ADAM_STEP = 0.001, 0.9, 0.999, 1e-08, 0.01, 10
N_CHIPS = 4
VMEM_LIMIT_V7X = 56 * 1024 * 1024

NN = (((1,), (0,)), ((), ()))
NT = (((1,), (1,)), ((), ()))
TN = (((0,), (0,)), ((), ()))


def _dot(a, b, dims=NN):
    return lax.dot_general(a, b, dims, preferred_element_type=F32)


def _sigmoid(x):
    return 1.0 / (1.0 + jnp.exp(-x))


def _call(body, *, name, grid, in_specs, out_specs, out_shape, scratch=(), sem=None):
    return pl.pallas_call(
        body, name=name, grid=grid, in_specs=in_specs, out_specs=out_specs, out_shape=out_shape,
        scratch_shapes=list(scratch),
        compiler_params=pltpu.CompilerParams(dimension_semantics=sem, vmem_limit_bytes=VMEM_LIMIT_V7X))


def _rope_tables(L):
    half = ROT_DIM // 2
    inv = ROPE_THETA ** (-jnp.arange(0, ROT_DIM, 2, dtype=F32) / ROT_DIM)
    ang = jnp.arange(L, dtype=F32)[:, None] * inv[None, :]
    cos, sin = jnp.cos(ang), jnp.sin(ang)
    rest = HEAD_DIM - ROT_DIM
    z8, zr, one = jnp.zeros((L, half), F32), jnp.zeros((L, rest), F32), jnp.ones((L, rest), F32)
    c = jnp.concatenate([cos, cos, one], axis=1)
    s1 = jnp.concatenate([-sin, z8, zr], axis=1)
    s2 = jnp.concatenate([z8, sin, zr], axis=1)
    return tuple(jnp.concatenate([t, t], axis=1) for t in (c, s1, s2))


def _rope_fwd(v, c, s1, s2):
    return v * c + pltpu.roll(v, 120, 1) * s1 + pltpu.roll(v, 8, 1) * s2


def _rope_bwd(dv, c, s1, s2):
    return dv * c + pltpu.roll(dv * s1, 8, 1) + pltpu.roll(dv * s2, 120, 1)


def _weight_spec(w, layer):
    if layer is None:
        return w.shape, pl.BlockSpec(w.shape, lambda i: (0, 0, 0))
    S, _, K, Ns = w.shape
    return (S, K, Ns), pl.BlockSpec((S, None, K, Ns), lambda i: (0, layer, 0, 0))


def _rowmm(a, w3, *, name, tm=256, gain=None, bias=None, res=None, swiglu=False, rope=None, layer=None):
    M, K = a.shape
    (S, _, Ns), w_spec = _weight_spec(w3, layer)
    N = S * Ns
    tm = min(tm, M)
    has_norm, has_bias, has_res, has_rope = gain is not None, bias is not None, res is not None, rope is not None
    row = lambda w: pl.BlockSpec((tm, w), lambda i: (i, 0))
    whole = lambda shape: pl.BlockSpec(shape, lambda i: (0,) * len(shape))
    ins, specs = [a], [row(K)]
    if has_norm:
        ins.append(gain.reshape(1, K)); specs.append(whole((1, K)))
    ins.append(w3); specs.append(w_spec)
    if has_bias:
        ins.append(bias.reshape(1, N)); specs.append(whole((1, N)))
    if has_res:
        ins.append(res); specs.append(row(N))
    if has_rope:
        ins += list(rope[2]); specs += [row(128)] * 3
    y_dtype = F32 if has_res else BF16
    out_shape, out_specs = [jax.ShapeDtypeStruct((M, N), y_dtype)], [row(N)]
    if has_norm:
        out_shape.append(jax.ShapeDtypeStruct((M, K), BF16)); out_specs.append(row(K))
    if swiglu:
        out_shape.append(jax.ShapeDtypeStruct((M, N // 2), BF16)); out_specs.append(row(N // 2))
    scratch = [pltpu.VMEM((tm, N), F32)] if has_rope else []

    def body(*refs):
        it = iter(refs)
        a_ref = next(it)
        g_ref = next(it) if has_norm else None
        w_ref = next(it)
        b_ref = next(it) if has_bias else None
        r_ref = next(it) if has_res else None
        tabs = [next(it) for _ in range(3)] if has_rope else None
        y_ref = next(it)
        n_ref = next(it) if has_norm else None
        act_ref = next(it) if swiglu else None
        ys_ref = next(it) if has_rope else None
        if has_norm:
            x = a_ref[...].astype(F32)
            ms = jnp.mean(x * x, axis=-1, keepdims=True)
            xb = (x * lax.rsqrt(ms + NORM_EPS) * g_ref[...]).astype(BF16)
            n_ref[...] = xb
        else:
            xb = a_ref[...].astype(BF16)
        if swiglu:
            for s in range(S // 2):
                g = _dot(xb, w_ref[s])
                u = _dot(xb, w_ref[s + S // 2])
                y_ref[:, s * Ns:(s + 1) * Ns] = g.astype(BF16)
                y_ref[:, N // 2 + s * Ns:N // 2 + (s + 1) * Ns] = u.astype(BF16)
                act_ref[:, s * Ns:(s + 1) * Ns] = (g * _sigmoid(g) * u).astype(BF16)
            return
        for s in range(S):
            sl = slice(s * Ns, (s + 1) * Ns)
            acc = _dot(xb, w_ref[s])
            if has_bias:
                acc = acc + b_ref[:, sl]
            if has_res:
                acc = acc + r_ref[:, sl]
            if has_rope:
                ys_ref[:, sl] = acc
            else:
                y_ref[:, sl] = acc.astype(y_dtype)
        if has_rope:
            c, s1, s2 = (t[...] for t in tabs)
            for cb in range(N // 128):
                cs = slice(cb * 128, (cb + 1) * 128)
                v = ys_ref[:, cs]
                if rope[0] <= cb * 128 < rope[1]:
                    v = _rope_fwd(v, c, s1, s2)
                y_ref[:, cs] = v.astype(BF16)

    return _call(body, name=name, grid=(M // tm,), in_specs=specs, out_specs=out_specs, out_shape=out_shape,
                 scratch=scratch, sem=("parallel",))(*ins)


def _mm_nt(dy, w3, *, name, mode, tm=256, kchunk=None, h=None, gain=None, dh=None, gu=None, layer=None):
    M, N = dy.shape
    (S, K, Ns), w_spec = _weight_spec(w3, layer)
    kchunk = kchunk or K
    tm = min(tm, M)
    row = lambda w: pl.BlockSpec((tm, w), lambda i: (i, 0))
    whole = lambda shape: pl.BlockSpec(shape, lambda i: (0,) * len(shape))
    ins, specs = [dy, w3], [row(N), w_spec]
    has_dh = dh is not None
    if mode == "norm":
        ins += [h, gain.reshape(1, K)]; specs += [row(K), whole((1, K))]
        if has_dh:
            ins.append(dh); specs.append(row(K))
        out_shape = [jax.ShapeDtypeStruct((M, K), F32), jax.ShapeDtypeStruct((1, K), F32)]
        out_specs = [row(K), whole((1, K))]
    elif mode == "swiglu":
        ins.append(gu); specs.append(row(2 * K))
        out_shape, out_specs = [jax.ShapeDtypeStruct((M, 2 * K), BF16)], [row(2 * K)]
    else:
        out_shape, out_specs = [jax.ShapeDtypeStruct((M, K), BF16)], [row(K)]

    def body(*refs):
        it = iter(refs)
        dy_ref, w_ref = next(it), next(it)
        if mode == "norm":
            h_ref, g_ref = next(it), next(it)
            dh_ref = next(it) if has_dh else None
            o_ref, dg_ref = next(it), next(it)
        elif mode == "swiglu":
            gu_ref, o_ref = next(it), next(it)
        else:
            o_ref = next(it)
        for kc in range(K // kchunk):
            ks = slice(kc * kchunk, (kc + 1) * kchunk)
            acc = None
            for s in range(S):
                t = _dot(dy_ref[:, s * Ns:(s + 1) * Ns].astype(BF16), w_ref[s, ks, :], NT)
                acc = t if acc is None else acc + t
            if mode == "plain":
                o_ref[:, ks] = acc.astype(BF16)
            elif mode == "swiglu":
                us = slice(K + kc * kchunk, K + (kc + 1) * kchunk)
                g = gu_ref[:, ks].astype(F32)
                u = gu_ref[:, us].astype(F32)
                sg = _sigmoid(g)
                o_ref[:, ks] = (acc * u * (sg * (1.0 + g * (1.0 - sg)))).astype(BF16)
                o_ref[:, us] = (acc * (g * sg)).astype(BF16)
            else:
                x = h_ref[...].astype(F32)
                r = lax.rsqrt(jnp.mean(x * x, axis=-1, keepdims=True) + NORM_EPS)
                xhat = x * r
                dxh = acc * g_ref[...]
                dx = r * (dxh - xhat * jnp.mean(dxh * xhat, axis=-1, keepdims=True))
                o_ref[...] = dx + dh_ref[...] if has_dh else dx

                @pl.when(pl.program_id(0) == 0)
                def _():
                    dg_ref[...] = jnp.zeros_like(dg_ref)

                dg_ref[...] += jnp.sum(acc * xhat, axis=0, keepdims=True)

    sem = ("arbitrary",) if mode == "norm" else ("parallel",)
    return _call(body, name=name, grid=(M // tm,), in_specs=specs, out_specs=out_specs, out_shape=out_shape, sem=sem)(*ins)


def _mm_tn(x, dy, *, S, name, tk=1024, kk=None, bias=False):
    M, K = x.shape
    N = dy.shape[1]
    Ns = N // S
    kk = kk or K
    tk = min(tk, M)
    nl = M // tk
    in_specs = [pl.BlockSpec((tk, kk), lambda s, kc, l: (l, kc)), pl.BlockSpec((tk, Ns), lambda s, kc, l: (l, s))]
    out_shape = [jax.ShapeDtypeStruct((S, K, Ns), BF16)]
    out_specs = [pl.BlockSpec((None, kk, Ns), lambda s, kc, l: (s, kc, 0))]
    if bias:
        out_shape.append(jax.ShapeDtypeStruct((1, N), F32))
        out_specs.append(pl.BlockSpec((1, Ns), lambda s, kc, l: (0, s)))

    def body(x_ref, dy_ref, o_ref, *rest):
        acc_ref = rest[-1]
        kc, l = pl.program_id(1), pl.program_id(2)

        @pl.when(l == 0)
        def _():
            acc_ref[...] = jnp.zeros_like(acc_ref)

        acc_ref[...] += _dot(x_ref[...].astype(BF16), dy_ref[...].astype(BF16), TN)
        if bias:
            b_ref = rest[0]

            @pl.when((kc == 0) & (l == 0))
            def _():
                b_ref[...] = jnp.zeros_like(b_ref)

            @pl.when(kc == 0)
            def _():
                b_ref[...] += jnp.sum(dy_ref[...].astype(F32), axis=0, keepdims=True)

        @pl.when(l == nl - 1)
        def _():
            o_ref[...] = acc_ref[...].astype(BF16)

    return _call(body, name=name, grid=(S, K // kk, nl), in_specs=in_specs, out_specs=out_specs, out_shape=out_shape,
                 scratch=[pltpu.VMEM((kk, Ns), F32)], sem=("arbitrary", "arbitrary", "arbitrary"))(x, dy)


def _band_bias(max_dist, has_prev):
    rows = lax.broadcasted_iota(jnp.int32, (BLK, 2 * BLK), 0)
    cols = lax.broadcasted_iota(jnp.int32, (BLK, 2 * BLK), 1)
    dist = rows - cols + BLK
    ok = (dist >= 0) & (dist <= max_dist) & ((cols >= BLK) | has_prev)
    return jnp.where(ok, 0.0, NEG)


Q_SCALE = HEAD_DIM ** -0.5


def _band_fwd(qa, ka, va, *, d, nq, nkv, qcol, kcol, vcol, max_dist, sinks=None, name):
    Lr = qa.shape[0]
    nb = Lr // BLK
    qw, kw, G = nq * HEAD_DIM, nkv * HEAD_DIM, nq // nkv
    cur = lambda colf, w: pl.BlockSpec((BLK, w), lambda r, i: (i, colf(r)))
    prv = lambda colf, w: pl.BlockSpec((BLK, w), lambda r, i: (jnp.maximum(i - 1, 0), colf(r)))
    out = pl.BlockSpec((BLK, qw), lambda r, i: (i, r))
    ins, specs = [qa, ka, ka, va, va], [cur(qcol, qw), cur(kcol, kw), prv(kcol, kw), cur(vcol, kw), prv(vcol, kw)]
    has_sinks = sinks is not None
    if has_sinks:
        ins.append(sinks); specs.append(pl.BlockSpec(memory_space=pltpu.SMEM))

    def body(*refs):
        q_ref, kc_ref, kp_ref, vc_ref, vp_ref = refs[:5]
        sk_ref = refs[5] if has_sinks else None
        o_ref, lse_ref = refs[-2], refs[-1]
        bias = _band_bias(max_dist, pl.program_id(1) > 0)
        k2 = jnp.concatenate([kp_ref[...], kc_ref[...]], axis=0)
        v2 = jnp.concatenate([vp_ref[...], vc_ref[...]], axis=0)
        for h in range(nq):
            hs = slice(h * HEAD_DIM, (h + 1) * HEAD_DIM)
            ks = slice((h // G) * HEAD_DIM, (h // G + 1) * HEAD_DIM)
            s = _dot(q_ref[:, hs] * jnp.asarray(Q_SCALE, BF16), k2[:, ks], NT) + bias
            m = jnp.max(s, axis=-1, keepdims=True)
            if has_sinks:
                m = jnp.maximum(m, sk_ref[h])
            p = jnp.exp(s - m)
            l = jnp.sum(p, axis=-1, keepdims=True)
            if has_sinks:
                l = l + jnp.exp(sk_ref[h] - m)
            o_ref[:, hs] = (_dot(p.astype(BF16), v2[:, ks]) / l).astype(BF16)
            lse_ref[:, hs] = jnp.broadcast_to(m + jnp.log(l), (BLK, HEAD_DIM))

    return _call(body, name=name, grid=(d, nb), in_specs=specs, out_specs=[out, out],
                 out_shape=[jax.ShapeDtypeStruct((Lr, d * qw), BF16), jax.ShapeDtypeStruct((Lr, d * qw), F32)],
                 sem=("parallel", "parallel"))(*ins)


def _band_bwd(qa, ka, va, doa, oa, lsea, *, d, nq, nkv, qcol, kcol, vcol, docol, max_dist, sinks=None, name):
    Lr = qa.shape[0]
    nb = Lr // BLK
    qw, kw, G = nq * HEAD_DIM, nkv * HEAD_DIM, nq // nkv
    transposed = G > 1
    last = lambda i: jnp.minimum(i, nb - 1)
    cur = lambda colf, w: pl.BlockSpec((BLK, w), lambda r, i: (last(i), colf(r)))
    prv = lambda colf, w: pl.BlockSpec((BLK, w), lambda r, i: (jnp.maximum(last(i) - 1, 0), colf(r)))
    own = lambda r: r
    ins = [qa, ka, ka, va, va, doa, oa, lsea]
    specs = [cur(qcol, qw), cur(kcol, kw), prv(kcol, kw), cur(vcol, kw), prv(vcol, kw), cur(docol, qw), cur(own, qw),
             cur(own, qw)]
    has_sinks = sinks is not None
    if has_sinks:
        ins.append(sinks); specs.append(pl.BlockSpec(memory_space=pltpu.SMEM))
    out_shape = [jax.ShapeDtypeStruct((Lr, d * qw), F32), jax.ShapeDtypeStruct((Lr, d * kw), F32),
                 jax.ShapeDtypeStruct((Lr, d * kw), F32)]
    behind = lambda r, i: (jnp.maximum(i - 1, 0), r)
    out_specs = [pl.BlockSpec((BLK, qw), lambda r, i: (last(i), r)), pl.BlockSpec((BLK, kw), behind),
                 pl.BlockSpec((BLK, kw), behind)]
    if has_sinks:
        out_shape.append(jax.ShapeDtypeStruct((8, 128), F32))
        out_specs.append(pl.BlockSpec((8, 128), lambda r, i: (0, 0)))

    def body(*refs):
        it = iter(refs)
        q_ref, kc_ref, kp_ref, vc_ref, vp_ref, do_ref, o_ref, ls_ref = (next(it) for _ in range(8))
        sk_ref = next(it) if has_sinks else None
        dq_ref, dk_ref, dv_ref = next(it), next(it), next(it)
        dsk_ref = next(it) if has_sinks else None
        dk_car, dv_car = next(it), next(it)
        r_id, i = pl.program_id(0), pl.program_id(1)

        @pl.when(i == 0)
        def _():
            dk_car[...] = jnp.zeros_like(dk_car)
            dv_car[...] = jnp.zeros_like(dv_car)

        if has_sinks:
            @pl.when((r_id == 0) & (i == 0))
            def _():
                dsk_ref[...] = jnp.zeros_like(dsk_ref)

        @pl.when(i == nb)
        def _():
            dk_ref[...] = dk_car[...]
            dv_ref[...] = dv_car[...]

        @pl.when(i < nb)
        def _():
            bias = _band_bias(max_dist, i > 0)
            k2 = jnp.concatenate([kp_ref[...], kc_ref[...]], axis=0)
            v2 = jnp.concatenate([vp_ref[...], vc_ref[...]], axis=0)
            if has_sinks:
                lane = lax.broadcasted_iota(jnp.int32, (8, 128), 1)
                dsk = jnp.zeros((8, 128), F32)
            for kv in range(nkv):
                ks = slice(kv * HEAD_DIM, (kv + 1) * HEAD_DIM)
                kh, vh = k2[:, ks], v2[:, ks]
                shape = (HEAD_DIM, 2 * BLK) if transposed else (2 * BLK, HEAD_DIM)
                dk, dv = jnp.zeros(shape, F32), jnp.zeros(shape, F32)
                for g in range(G):
                    h = kv * G + g
                    hs = slice(h * HEAD_DIM, (h + 1) * HEAD_DIM)
                    q = q_ref[:, hs] * jnp.asarray(Q_SCALE, BF16)
                    do = do_ref[:, hs]
                    lse = ls_ref[:, h * HEAD_DIM:h * HEAD_DIM + 1]
                    dl = jnp.sum(do.astype(F32) * o_ref[:, hs].astype(F32), axis=-1, keepdims=True)
                    p = jnp.exp(_dot(q, kh, NT) + bias - lse)
                    ds = (p * (_dot(do, vh, NT) - dl)).astype(BF16)
                    dq_ref[:, hs] = _dot(ds, kh) * Q_SCALE
                    if transposed:
                        dk = dk + _dot(q, ds, TN)
                        dv = dv + _dot(do, p.astype(BF16), TN)
                    else:
                        dk = dk + _dot(ds, q, TN)
                        dv = dv + _dot(p.astype(BF16), do, TN)
                    if has_sinks:
                        val = -jnp.sum(jnp.exp(sk_ref[h] - lse) * dl, axis=0, keepdims=True)
                        dsk = dsk + jnp.where(lane == h, val, 0.0)
                if transposed:
                    dk, dv = dk.T, dv.T
                dk_ref[:, ks] = dk_car[:, ks] + dk[:BLK]
                dv_ref[:, ks] = dv_car[:, ks] + dv[:BLK]
                dk_car[:, ks] = dk[BLK:]
                dv_car[:, ks] = dv[BLK:]
            if has_sinks:
                dsk_ref[...] += dsk

    return _call(body, name=name, grid=(d, nb + 1), in_specs=specs, out_specs=out_specs, out_shape=out_shape,
                 scratch=[pltpu.VMEM((BLK, kw), F32), pltpu.VMEM((BLK, kw), F32)], sem=("arbitrary", "arbitrary"))(*ins)


def _attn_grad_combine(branches, tabs, *, name, tm=256):
    L, qw = branches[0][0].shape
    kw = branches[0][1].shape[1]
    nbr = len(branches)
    row = lambda w: pl.BlockSpec((tm, w), lambda i: (i, 0))
    ins, specs = [], []
    for dq, dk, dv in branches:
        ins += [dq, dk, dv]; specs += [row(qw), row(kw), row(kw)]
    ins += list(tabs); specs += [row(128)] * 3

    def body(*refs):
        c, s1, s2 = (t[...] for t in refs[3 * nbr:3 * nbr + 3])
        o_ref = refs[-1]
        for part, (w, off, rot) in enumerate(((qw, 0, True), (kw, qw, True), (kw, qw + kw, False))):
            for cb in range(w // 128):
                cs = slice(cb * 128, (cb + 1) * 128)
                v = refs[part][:, cs]
                for b in range(1, nbr):
                    v = v + refs[3 * b + part][:, cs]
                if rot:
                    v = _rope_bwd(v, c, s1, s2)
                o_ref[:, off + cb * 128:off + (cb + 1) * 128] = v.astype(BF16)

    return _call(body, name=name, grid=(L // tm,), in_specs=specs, out_specs=row(qw + 2 * kw),
                 out_shape=jax.ShapeDtypeStruct((L, qw + 2 * kw), BF16), sem=("parallel",))(*ins)


def _xattn_fwd(q, kv, *, name, tq=512):
    L, W = q.shape
    scale = XA_HEAD_DIM ** -0.5
    row = pl.BlockSpec((tq, W), lambda i: (i, 0))
    kvs = pl.BlockSpec((N_MEM, 2 * W), lambda i: (0, 0))

    def body(q_ref, kv_ref, o_ref, lse_ref):
        for h in range(XA_HEADS):
            hs = slice(h * XA_HEAD_DIM, (h + 1) * XA_HEAD_DIM)
            vs = slice(W + h * XA_HEAD_DIM, W + (h + 1) * XA_HEAD_DIM)
            s = _dot(q_ref[:, hs], kv_ref[:, hs], NT) * scale
            m = jnp.max(s, axis=-1, keepdims=True)
            p = jnp.exp(s - m)
            l = jnp.sum(p, axis=-1, keepdims=True)
            o_ref[:, hs] = (_dot(p.astype(BF16), kv_ref[:, vs]) / l).astype(BF16)
            lse_ref[:, hs] = jnp.broadcast_to(m + jnp.log(l), (tq, XA_HEAD_DIM))

    return _call(body, name=name, grid=(L // tq,), in_specs=[row, kvs], out_specs=[row, row],
                 out_shape=[jax.ShapeDtypeStruct((L, W), BF16), jax.ShapeDtypeStruct((L, W), F32)], sem=("parallel",))(q, kv)


def _xattn_bwd(q, kv, o, lse, do, *, name, tq=512):
    L, W = q.shape
    scale = XA_HEAD_DIM ** -0.5
    row = pl.BlockSpec((tq, W), lambda i: (i, 0))
    kvs = pl.BlockSpec((N_MEM, 2 * W), lambda i: (0, 0))

    def body(q_ref, kv_ref, o_ref, lse_ref, do_ref, dq_ref, dkv_ref):
        @pl.when(pl.program_id(0) == 0)
        def _():
            dkv_ref[...] = jnp.zeros_like(dkv_ref)

        for h in range(XA_HEADS):
            hs = slice(h * XA_HEAD_DIM, (h + 1) * XA_HEAD_DIM)
            vs = slice(W + h * XA_HEAD_DIM, W + (h + 1) * XA_HEAD_DIM)
            qh, kh, vh, doh = q_ref[:, hs], kv_ref[:, hs], kv_ref[:, vs], do_ref[:, hs]
            p = jnp.exp(_dot(qh, kh, NT) * scale - lse_ref[:, h * XA_HEAD_DIM:h * XA_HEAD_DIM + 1])
            dl = jnp.sum(doh.astype(F32) * o_ref[:, hs].astype(F32), axis=-1, keepdims=True)
            ds = (p * (_dot(doh, vh, NT) - dl) * scale).astype(BF16)
            dq_ref[:, hs] = _dot(ds, kh).astype(BF16)
            dkv_ref[:, hs] += _dot(ds, qh, TN)
            dkv_ref[:, vs] += _dot(p.astype(BF16), doh, TN)

    return _call(body, name=name, grid=(L // tq,), in_specs=[row, kvs, row, row, row], out_specs=[row, kvs],
                 out_shape=[jax.ShapeDtypeStruct((L, W), BF16), jax.ShapeDtypeStruct((N_MEM, 2 * W), F32)],
                 sem=("arbitrary",))(q, kv, o, lse, do)


def _neg_expm1(z):
    series = -(z * (1.0 + z * (0.5 + z * (1.0 / 6.0 + z * (1.0 / 24.0 + z * (1.0 / 120.0))))))
    return jnp.where(z > -0.05, series, 1.0 - jnp.exp(z))


def _softplus(z):
    return jnp.maximum(z, 0.0) + jnp.log(1.0 + jnp.exp(-jnp.abs(z)))


def _gelu_parts(y):
    c = 0.7978845608028654
    t = jnp.tanh(c * (y + 0.044715 * y * y * y))
    gy = 0.5 * y * (1.0 + t)
    dgy = 0.5 * (1.0 + t) + 0.5 * y * (1.0 - t * t) * c * (1.0 + 3.0 * 0.044715 * y * y)
    return gy, dgy


def _lru_gates(xc, wa_ref, ba, wx_ref, bx, sp):
    rs, igs = [], []
    for hd in range(LRU_HEADS):
        sl = slice(hd * LRU_HEAD_DIM, (hd + 1) * LRU_HEAD_DIM)
        xh = xc[:, sl].astype(BF16)
        rs.append(_sigmoid(_dot(xh, wa_ref[hd]) + ba[:, sl]))
        igs.append(_sigmoid(_dot(xh, wx_ref[hd]) + bx[:, sl]))
    r, ig = jnp.concatenate(rs, axis=1), jnp.concatenate(igs, axis=1)
    la = -LRU_C * r * sp
    return r, ig, jnp.exp(la), _neg_expm1(2.0 * la)


def _conv_taps(x_ext, halo):
    n = x_ext.shape[0]
    return [x_ext[halo:] if k == CONV_WIDTH - 1 else pltpu.roll(x_ext, CONV_WIDTH - 1 - k, 0)[halo:]
            for k in range(CONV_WIDTH)]


def _lru_fwd(proj, cw, cb, wa, ba, wx, bx, lam, *, name, tc=512):
    L = proj.shape[0]
    W = LRU_HEADS * LRU_HEAD_DIM
    nb = L // tc
    whole = lambda shape: pl.BlockSpec(shape, lambda i: (0,) * len(shape))
    specs = [pl.BlockSpec((tc, W), lambda i: (i, 0)), pl.BlockSpec((tc, W), lambda i: (i, 1)),
             pl.BlockSpec((16, W), lambda i: (jnp.maximum(i * (tc // 16) - 1, 0), 0)),
             whole((CONV_WIDTH, W)), whole((1, W)), whole((LRU_HEADS, LRU_HEAD_DIM, LRU_HEAD_DIM)), whole((1, W)),
             whole((LRU_HEADS, LRU_HEAD_DIM, LRU_HEAD_DIM)), whole((1, W)), whole((1, W))]
    out_specs = [pl.BlockSpec((tc, W), lambda i: (i, 0))] * 2
    out_shape = [jax.ShapeDtypeStruct((L, W), BF16), jax.ShapeDtypeStruct((L, W), F32)]

    def body(x_ref, y_ref, xh_ref, cw_ref, cb_ref, wa_ref, ba_ref, wx_ref, bx_ref, lam_ref, rec_ref, hs_ref,
             hcar, a_scr, b_scr):
        i = pl.program_id(0)

        @pl.when(i == 0)
        def _():
            hcar[...] = jnp.zeros_like(hcar)

        halo = jnp.where(i > 0, xh_ref[...].astype(F32), 0.0)
        taps = _conv_taps(jnp.concatenate([halo, x_ref[...].astype(F32)], axis=0), 16)
        xc = cb_ref[...] + sum(cw_ref[k:k + 1, :] * taps[k] for k in range(CONV_WIDTH))
        _, ig, a, om = _lru_gates(xc, wa_ref, ba_ref[...], wx_ref, bx_ref[...], _softplus(-lam_ref[...]))
        b = jnp.sqrt(om) * (ig * xc)
        rowmod = lax.broadcasted_iota(jnp.int32, (tc, W), 0) & 7
        for s in (1, 2, 4):
            keep = rowmod >= s
            b = jnp.where(keep, a * pltpu.roll(b, s, 0) + b, b)
            a = jnp.where(keep, a * pltpu.roll(a, s, 0), a)
        a_scr[...] = a
        b_scr[...] = b

        def tile(j, hc):
            rows = pl.ds(pl.multiple_of(j * 8, 8), 8)
            ht = a_scr[rows, :] * hc + b_scr[rows, :]
            hs_ref[rows, :] = ht
            return jnp.broadcast_to(ht[7:8, :], (8, W))

        hcar[...] = lax.fori_loop(0, tc // 8, tile, hcar[...])
        gy, _ = _gelu_parts(y_ref[...].astype(F32))
        rec_ref[...] = (hs_ref[...] * gy).astype(BF16)

    return _call(body, name=name, grid=(nb,), in_specs=specs, out_specs=out_specs, out_shape=out_shape,
                 scratch=[pltpu.VMEM((8, W), F32), pltpu.VMEM((tc, W), F32), pltpu.VMEM((tc, W), F32)],
                 sem=("arbitrary",))(proj, proj, proj, cw, cb, wa, ba, wx, bx, lam)


def _lru_bwd(proj, hs, drec_src, cw, cb, wa, ba, wx, bx, lam, *, name, tc=256):
    L = proj.shape[0]
    W = LRU_HEADS * LRU_HEAD_DIM
    nb = L // tc
    tb = lambda i: nb - 1 - i
    whole = lambda shape: pl.BlockSpec(shape, lambda i: (0,) * len(shape))
    gate_w = (LRU_HEADS, LRU_HEAD_DIM, LRU_HEAD_DIM)
    specs = [pl.BlockSpec((tc, W), lambda i: (tb(i), 0)), pl.BlockSpec((tc, W), lambda i: (tb(i), 1)),
             pl.BlockSpec((16, W), lambda i: (jnp.maximum(tb(i) * (tc // 16) - 1, 0), 0)),
             pl.BlockSpec((tc, W), lambda i: (tb(i), 0)),
             pl.BlockSpec((8, W), lambda i: (jnp.maximum(tb(i) * (tc // 8) - 1, 0), 0)),
             pl.BlockSpec((tc, W), lambda i: (tb(i), 0)),
             whole((CONV_WIDTH, W)), whole((1, W)), whole(gate_w), whole((1, W)), whole(gate_w), whole((1, W)), whole((1, W))]
    out_specs = [pl.BlockSpec((tc, 2 * W), lambda i: (tb(i), 0)), whole((CONV_WIDTH, W)), whole((1, W)), whole(gate_w),
                 whole((1, W)), whole(gate_w), whole((1, W)), whole((1, W))]
    vec = jax.ShapeDtypeStruct((1, W), F32)
    out_shape = [jax.ShapeDtypeStruct((L, 2 * W), BF16), jax.ShapeDtypeStruct((CONV_WIDTH, W), F32), vec,
                 jax.ShapeDtypeStruct(gate_w, F32), vec, jax.ShapeDtypeStruct(gate_w, F32), vec, vec]

    def body(x_ref, y_ref, xh_ref, hs_ref, hh_ref, dr_ref, cw_ref, cb_ref, wa_ref, ba_ref, wx_ref, bx_ref, lam_ref,
             dxy_ref, dcw_ref, dcb_ref, dwa_ref, dba_ref, dwx_ref, dbx_ref, dlam_ref, gcar, dxc_car, a_scr, b_scr, g_scr):
        pid = pl.program_id(0)
        t = tb(pid)
        accs = (dcw_ref, dcb_ref, dwa_ref, dba_ref, dwx_ref, dbx_ref, dlam_ref)

        @pl.when(pid == 0)
        def _():
            gcar[...] = jnp.zeros_like(gcar)
            dxc_car[...] = jnp.zeros_like(dxc_car)
            for r in accs:
                r[...] = jnp.zeros_like(r)

        halo = jnp.where(t > 0, xh_ref[...].astype(F32), 0.0)
        taps = _conv_taps(jnp.concatenate([halo, x_ref[...].astype(F32)], axis=0), 16)
        xc = cb_ref[...] + sum(cw_ref[k:k + 1, :] * taps[k] for k in range(CONV_WIDTH))
        lam = lam_ref[...]
        sp = _softplus(-lam)
        r, ig, a, om = _lru_gates(xc, wa_ref, ba_ref[...], wx_ref, bx_ref[...], sp)
        sq = jnp.sqrt(om)
        hblk = hs_ref[...]
        hprev = pltpu.roll(jnp.concatenate([jnp.where(t > 0, hh_ref[...], 0.0), hblk], axis=0), 1, 0)[8:]
        gy, dgy = _gelu_parts(y_ref[...].astype(F32))
        drec = dr_ref[...].astype(F32)
        dxy_ref[:, W:] = (drec * hblk * dgy).astype(BF16)

        rowidx = lax.broadcasted_iota(jnp.int32, (tc, W), 0)
        rowmod = rowidx & 7
        ca = jnp.where(rowidx == tc - 1, 1.0, pltpu.roll(a, tc - 1, 0))
        cbv = drec * gy
        for s in (1, 2, 4):
            keep = rowmod < 8 - s
            cbv = jnp.where(keep, ca * pltpu.roll(cbv, tc - s, 0) + cbv, cbv)
            ca = jnp.where(keep, ca * pltpu.roll(ca, tc - s, 0), ca)
        a_scr[...] = ca
        b_scr[...] = cbv

        def tile(k, gc):
            j = tc // 8 - 1 - k
            rows = pl.ds(pl.multiple_of(j * 8, 8), 8)
            gt = a_scr[rows, :] * gc + b_scr[rows, :]
            g_scr[rows, :] = gt
            return jnp.broadcast_to(gt[0:1, :], (8, W))

        lax.fori_loop(0, tc // 8, tile, gcar[...])
        G = g_scr[...]
        gcar[...] = jnp.broadcast_to(a[0:1, :] * G[0:1, :], (8, W))

        da = G * hprev
        dsq = G * (ig * xc)
        di = G * (sq * xc)
        dxc = G * (sq * ig)
        dla = da * a - 2.0 * a * a * (dsq * 0.5 * lax.rsqrt(om))
        dlam_ref[...] += jnp.sum(dla * (-LRU_C * r), axis=0, keepdims=True) * (-_sigmoid(-lam))
        dpr = dla * (-LRU_C * sp) * r * (1.0 - r)
        dpi = di * ig * (1.0 - ig)
        dba_ref[...] += jnp.sum(dpr, axis=0, keepdims=True)
        dbx_ref[...] += jnp.sum(dpi, axis=0, keepdims=True)
        back = []
        for hd in range(LRU_HEADS):
            sl = slice(hd * LRU_HEAD_DIM, (hd + 1) * LRU_HEAD_DIM)
            xh, dprh, dpih = xc[:, sl].astype(BF16), dpr[:, sl].astype(BF16), dpi[:, sl].astype(BF16)
            back.append(_dot(dprh, wa_ref[hd], NT) + _dot(dpih, wx_ref[hd], NT))
            dwa_ref[hd] += _dot(xh, dprh, TN)
            dwx_ref[hd] += _dot(xh, dpih, TN)
        dxc = dxc + jnp.concatenate(back, axis=1)
        dcb_ref[...] += jnp.sum(dxc, axis=0, keepdims=True)
        for k in range(CONV_WIDTH):
            dcw_ref[k:k + 1, :] += jnp.sum(dxc * taps[k], axis=0, keepdims=True)
        ext = jnp.concatenate([dxc, dxc_car[...]], axis=0)
        dx = cw_ref[CONV_WIDTH - 1:CONV_WIDTH, :] * dxc
        for k in range(CONV_WIDTH - 1):
            dx = dx + cw_ref[k:k + 1, :] * pltpu.roll(ext, tc + 8 - (CONV_WIDTH - 1 - k), 0)[:tc]
        dxc_car[...] = dxc[0:8, :]
        dxy_ref[:, :W] = dx.astype(BF16)

    scratch = [pltpu.VMEM((8, W), F32), pltpu.VMEM((8, W), F32)] + [pltpu.VMEM((tc, W), F32)] * 3
    return _call(body, name=name, grid=(nb,), in_specs=specs, out_specs=out_specs, out_shape=out_shape, scratch=scratch,
                 sem=("arbitrary",))(proj, proj, proj, hs, hs, drec_src, cw, cb, wa, ba, wx, bx, lam)


def _final_loss(h, gain, target, *, name, tm=256):
    M, K = h.shape
    row = pl.BlockSpec((tm, K), lambda i: (i, 0))
    vec = pl.BlockSpec((1, K), lambda i: (0, 0))
    one = pl.BlockSpec((1, 128), lambda i: (0, 0))

    def body(h_ref, g_ref, t_ref, dh_ref, dg_ref, loss_ref):
        @pl.when(pl.program_id(0) == 0)
        def _():
            dg_ref[...] = jnp.zeros_like(dg_ref)
            loss_ref[...] = jnp.zeros_like(loss_ref)

        x = h_ref[...]
        r = lax.rsqrt(jnp.mean(x * x, axis=-1, keepdims=True) + NORM_EPS)
        xhat = x * r
        err = xhat * g_ref[...] - t_ref[...]
        loss_ref[...] += 0.5 / K * jnp.sum(err * err)
        dy = err * (1.0 / K)
        dg_ref[...] += jnp.sum(dy * xhat, axis=0, keepdims=True)
        dxh = dy * g_ref[...]
        dh_ref[...] = r * (dxh - xhat * jnp.mean(dxh * xhat, axis=-1, keepdims=True))

    return _call(body, name=name, grid=(M // tm,), in_specs=[row, vec, row], out_specs=[row, vec, one],
                 out_shape=[jax.ShapeDtypeStruct((M, K), F32), jax.ShapeDtypeStruct((1, K), F32),
                            jax.ShapeDtypeStruct((1, 128), F32)], sem=("arbitrary",))(h, gain.reshape(1, K), target)


def _dilated_merge(branches, *, name, tm=512):
    L, W = branches[0].shape
    nbr = len(branches) // 2
    row = pl.BlockSpec((tm, W), lambda i: (i, 0))

    def body(*refs):
        o_ref, lse_ref = refs[-2], refs[-1]
        lses = [refs[2 * b + 1][...] for b in range(nbr)]
        m = lses[0]
        for t in lses[1:]:
            m = jnp.maximum(m, t)
        ws = [jnp.exp(t - m) for t in lses]
        den = ws[0]
        for t in ws[1:]:
            den = den + t
        acc = ws[0] * refs[0][...].astype(F32)
        for b in range(1, nbr):
            acc = acc + ws[b] * refs[2 * b][...].astype(F32)
        o_ref[...] = (acc / den).astype(BF16)
        lse_ref[...] = m + jnp.log(den)

    return _call(body, name=name, grid=(L // tm,), in_specs=[row] * (2 * nbr), out_specs=[row, row],
                 out_shape=[jax.ShapeDtypeStruct((L, W), BF16), jax.ShapeDtypeStruct((L, W), F32)], sem=("parallel",))(*branches)


def _dilated_fwd(proj0):
    L = proj0.shape[0]
    qkv = proj0[:, 2 * D_MODEL:]
    W = B_HEADS * HEAD_DIM
    outs = []
    for window, d in DILATED_PATTERN:
        view = qkv.reshape(L // d, d * 3 * W)
        o, lse = _band_fwd(view, view, view, d=d, nq=B_HEADS, nkv=B_HEADS, qcol=lambda r: 3 * r, kcol=lambda r: 3 * r + 1,
                           vcol=lambda r: 3 * r + 2, max_dist=window // d, name=f"dilated_fwd_d{d}")
        outs += [o.reshape(L, W), lse.reshape(L, W)]
    return _dilated_merge(outs, name="dilated_merge")


def _dilated_bwd(proj0, att, lse, datt, tabs):
    L = proj0.shape[0]
    qkv = proj0[:, 2 * D_MODEL:]
    Wh = B_HEADS * HEAD_DIM
    branches = []
    for window, d in DILATED_PATTERN:
        view = qkv.reshape(L // d, d * 3 * Wh)
        v1 = lambda t: t.reshape(L // d, d * Wh)
        outs = _band_bwd(view, view, view, v1(datt), v1(att), v1(lse), d=d, nq=B_HEADS, nkv=B_HEADS,
                         qcol=lambda r: 3 * r, kcol=lambda r: 3 * r + 1, vcol=lambda r: 3 * r + 2, docol=lambda r: r,
                         max_dist=window // d, name=f"dilated_bwd_d{d}")
        branches.append([o.reshape(L, Wh) for o in outs])
    return _attn_grad_combine(branches, tabs, name="dilated_grad_combine")


def _device_step(x, mem, target, w):
    L = x.shape[0]
    tabs = _rope_tables(L)
    g = {}
    saved = []
    h = x
    for layer in range(2):
        sv = {"h_mix": h}
        if layer == 0:
            proj, n = _rowmm(h, w["ab_w_in"], name="l0_in_proj", gain=w["mix_norm"][0],
                             rope=(2 * D_MODEL, 2 * D_MODEL + 2 * B_HEADS * HEAD_DIM, tabs))
            rec, hs = _lru_fwd(proj, w["lru_conv_w"], w["lru_conv_b"], w["lru_wa"], w["lru_ba"], w["lru_wx"], w["lru_bx"],
                               w["lru_lambda"], name="lru_fwd")
            att, lse = _dilated_fwd(proj)
            mix = jnp.concatenate([rec, att], axis=1)
            (h,) = _rowmm(mix, w["ab_w_out"], name="l0_out_proj", res=h)
            sv.update(hs=hs)
        else:
            proj, n = _rowmm(h, w["c_w_qkv"], name="l1_qkv_proj", gain=w["mix_norm"][1], bias=w["c_b_qkv"],
                             rope=(0, (C_HEADS + C_KV_HEADS) * HEAD_DIM, tabs))
            mix, lse = _band_fwd(proj, proj, proj, d=1, nq=C_HEADS, nkv=C_KV_HEADS, qcol=lambda r: 0, kcol=lambda r: 8,
                                 vcol=lambda r: 9, max_dist=C_WINDOW - 1, sinks=w["c_sinks"], name="swa_fwd")
            (h,) = _rowmm(mix, w["c_w_out"], name="l1_out_proj", res=h, bias=w["c_b_out"])
        sv.update(proj=proj, n_mix=n, mix=mix, lse=lse, h_xa=h)
        xq, nx = _rowmm(h, w["xa_wq"][layer][None], name=f"xa_q_proj{layer}", gain=w["xa_norm"][layer])
        kv, nm = _rowmm(mem, w["xa_wkv"][layer][None], name=f"xa_kv_proj{layer}", gain=w["xa_mem_norm"][layer])
        xo, xlse = _xattn_fwd(xq, kv, name=f"xa_fwd{layer}")
        (h,) = _rowmm(xo, w["xa_wo"][layer], name=f"xa_out_proj{layer}", res=h)
        sv.update(xq=xq, nx=nx, kv=kv, nm=nm, xo=xo, xlse=xlse, h_ffn=h)
        gu, nf, act = _rowmm(h, w["ffn_w_gate_up"], layer=layer, name=f"ffn_in{layer}", gain=w["ffn_norm"][layer], swiglu=True)
        (h,) = _rowmm(act, w["ffn_w_down"][layer][None], name=f"ffn_out{layer}", res=h, tm=512)
        sv.update(gu=gu, nf=nf, act=act)
        saved.append(sv)

    dh, g["final_norm"], loss = _final_loss(h, w["final_norm"], target, name="final_loss")

    stk = {k: [None, None] for k in ("xa_norm", "xa_mem_norm", "ffn_norm", "mix_norm", "xa_wq", "xa_wkv", "xa_wo",
                                      "ffn_w_gate_up", "ffn_w_down")}
    for layer in (1, 0):
        sv = saved[layer]
        (stk["ffn_w_down"][layer],) = _mm_tn(sv["act"], dh, S=1, name=f"ffn_down_dw{layer}", kk=D_FF // 2)
        (dgu,) = _mm_nt(dh, w["ffn_w_down"][layer][None], name=f"ffn_dact{layer}", mode="swiglu", kchunk=D_FF // 2, gu=sv["gu"])
        (stk["ffn_w_gate_up"][layer],) = _mm_tn(sv["nf"], dgu, S=N_CHIPS, name=f"ffn_gu_dw{layer}")
        dh, stk["ffn_norm"][layer] = _mm_nt(dgu, w["ffn_w_gate_up"], layer=layer, name=f"ffn_dx{layer}", mode="norm",
                                            h=sv["h_ffn"], gain=w["ffn_norm"][layer], dh=dh)
        (stk["xa_wo"][layer],) = _mm_tn(sv["xo"], dh, S=N_CHIPS, name=f"xa_wo_dw{layer}")
        (dxo,) = _mm_nt(dh, w["xa_wo"][layer], name=f"xa_dxo{layer}", mode="plain")
        dxq, dkv = _xattn_bwd(sv["xq"], sv["kv"], sv["xo"], sv["xlse"], dxo, name=f"xa_bwd{layer}")
        (stk["xa_wq"][layer],) = _mm_tn(sv["nx"], dxq, S=1, name=f"xa_wq_dw{layer}")
        dh, stk["xa_norm"][layer] = _mm_nt(dxq, w["xa_wq"][layer][None], name=f"xa_dx{layer}", mode="norm", h=sv["h_xa"],
                                           gain=w["xa_norm"][layer], dh=dh)
        (stk["xa_wkv"][layer],) = _mm_tn(sv["nm"], dkv, S=1, name=f"xa_wkv_dw{layer}")
        _, stk["xa_mem_norm"][layer] = _mm_nt(dkv, w["xa_wkv"][layer][None], name=f"xa_dmem{layer}", mode="norm", h=mem,
                                              gain=w["xa_mem_norm"][layer])
        if layer == 1:
            g["c_w_out"], g["c_b_out"] = _mm_tn(sv["mix"], dh, S=1, name="l1_out_dw", bias=True)
            (dmix,) = _mm_nt(dh, w["c_w_out"], name="l1_dmix", mode="plain")
            dq, dk, dv, dsk = _band_bwd(sv["proj"], sv["proj"], sv["proj"], dmix, sv["mix"], sv["lse"], d=1, nq=C_HEADS,
                                        nkv=C_KV_HEADS, qcol=lambda r: 0, kcol=lambda r: 8, vcol=lambda r: 9,
                                        docol=lambda r: 0, max_dist=C_WINDOW - 1, sinks=w["c_sinks"], name="swa_bwd")
            g["c_sinks"] = dsk[0, :C_HEADS]
            dproj = _attn_grad_combine([(dq, dk, dv)], tabs, name="swa_grad_combine")
            g["c_w_qkv"], g["c_b_qkv"] = _mm_tn(sv["n_mix"], dproj, S=1, name="l1_qkv_dw", bias=True)
            dh, stk["mix_norm"][1] = _mm_nt(dproj, w["c_w_qkv"], name="l1_dx", mode="norm", h=sv["h_mix"],
                                            gain=w["mix_norm"][1], dh=dh)
        else:
            (g["ab_w_out"],) = _mm_tn(sv["mix"], dh, S=1, name="l0_out_dw", kk=768)
            (dmix,) = _mm_nt(dh, w["ab_w_out"], name="l0_dmix", mode="plain", kchunk=768)
            (dxy, g["lru_conv_w"], g["lru_conv_b"], g["lru_wa"], g["lru_ba"], g["lru_wx"], g["lru_bx"],
             g["lru_lambda"]) = _lru_bwd(sv["proj"], sv["hs"], dmix, w["lru_conv_w"], w["lru_conv_b"], w["lru_wa"],
                                         w["lru_ba"], w["lru_wx"], w["lru_bx"], w["lru_lambda"], name="lru_bwd")
            dqkv = _dilated_bwd(sv["proj"], sv["mix"][:, D_MODEL:], sv["lse"], dmix[:, D_MODEL:], tabs)
            dproj = jnp.concatenate([dxy, dqkv], axis=1)
            (g["ab_w_in"],) = _mm_tn(sv["n_mix"], dproj, S=N_CHIPS, name="l0_in_dw")
            dh, stk["mix_norm"][0] = _mm_nt(dproj, w["ab_w_in"], name="l0_dx", mode="norm", h=sv["h_mix"],
                                            gain=w["mix_norm"][0], dh=dh)
    for k, v in stk.items():
        if k == "ffn_w_gate_up":
            g[k] = jnp.stack(v, axis=1)
        else:
            g[k] = jnp.concatenate(v, axis=0) if v[0].shape[0] == 1 else jnp.stack(v, axis=0)
    return loss[0, 0], dh, g


ANY = pl.BlockSpec(memory_space=pl.ANY)
MESH = pl.DeviceIdType.MESH


def _place():
    x, y, c = lax.axis_index("x"), lax.axis_index("y"), lax.axis_index("c")
    return x, y, c, [(1 - x, y), (x, 1 - y), (1 - x, 1 - y)]


def _remote(send_sems, recv_sems):
    def copy(k, src, dst, to):
        return pltpu.make_async_remote_copy(src_ref=src, dst_ref=dst, send_sem=send_sems.at[k], recv_sem=recv_sems.at[k],
                                            device_id=to, device_id_type=MESH)
    return copy


def _halves(ref, n_rows):
    rh = n_rows // 2
    return lambda lead, hh: ref.at[(*lead, pl.ds(hh * rh, rh), slice(None))]


def _gather_weights(packs, spack):
    n = len(packs)

    def body(*refs):
        w_refs, s_ref, wf_refs, sf_ref = refs[:n], refs[n], refs[n + 1:2 * n + 1], refs[2 * n + 1]
        x, y, c, chips = _place()
        me, sib = 2 * x + y, (x, y, 1 - c)
        copy = _remote(*refs[-2:])
        src = [_halves(w_refs[g], packs[g].shape[0]) for g in range(n)]
        dst = [_halves(wf_refs[g], packs[g].shape[0]) for g in range(n)]
        sends = []
        for g in range(n):
            for j, (cx, cy) in enumerate(chips):
                sends.append(copy(3 * g + j, src[g]((), c), dst[g]((me,), c), (cx, cy, c)))
        for j, (cx, cy) in enumerate(chips):
            sends.append(copy(6 * n + j, s_ref, sf_ref.at[me], (cx, cy, c)))
        for cp in sends:
            cp.start()
        for g in range(n):
            for j, (cx, cy) in enumerate(chips):
                got = dst[g]((2 * cx + cy,), c)
                copy(3 * g + j, got, got, sib).wait_recv()
                fwd = copy(3 * n + 3 * g + j, got, got, sib)
                fwd.start()
                sends.append(fwd)
        for g in range(n):
            for j, (cx, cy) in enumerate(chips):
                got = dst[g]((2 * cx + cy,), 1 - c)
                copy(3 * n + 3 * g + j, got, got, sib).wait_recv()
        for j, (cx, cy) in enumerate(chips):
            copy(6 * n + j, s_ref, sf_ref.at[2 * cx + cy], sib).wait_recv()
        for cp in sends:
            cp.wait_send()

    ins = list(packs) + [spack]
    out_shape = [jax.ShapeDtypeStruct((N_CHIPS,) + a.shape, a.dtype) for a in ins]
    n_sems = 6 * n + 3
    outs = pl.pallas_call(body, name="gather_weights", out_shape=out_shape, in_specs=[ANY] * len(ins),
                          out_specs=[ANY] * len(ins),
                          scratch_shapes=[pltpu.SemaphoreType.DMA((n_sems,)), pltpu.SemaphoreType.DMA((n_sems,))])(*ins)
    chip = 2 * lax.axis_index("x") + lax.axis_index("y")
    outs = [lax.dynamic_update_index_in_dim(o, a, chip, 0) for o, a in zip(outs, ins)]
    return outs[:n], outs[n]


def _rs_pair_exchange(gpacks):
    n = len(gpacks)

    def body(*refs):
        g_refs, ra_refs = refs[:n], refs[n:2 * n]
        x, y, c, _ = _place()
        copy = _remote(*refs[-2:])
        cps = []
        for g in range(n):
            half = _halves(g_refs[g], gpacks[g].shape[1])
            cps += [copy(N_CHIPS * g + j, half((j,), 1 - c), ra_refs[g].at[j], (x, y, 1 - c)) for j in range(N_CHIPS)]
        for cp in cps:
            cp.start()
        for cp in cps:
            cp.wait()

    out_shape = [jax.ShapeDtypeStruct((N_CHIPS, a.shape[1] // 2, a.shape[2]), a.dtype) for a in gpacks]
    n_sems = N_CHIPS * n
    return pl.pallas_call(body, name="rs_pair_exchange", out_shape=out_shape, in_specs=[ANY] * n, out_specs=[ANY] * n,
                          scratch_shapes=[pltpu.SemaphoreType.DMA((n_sems,)), pltpu.SemaphoreType.DMA((n_sems,))])(*gpacks)


def _row_tile(rows, cap=512):
    return max(t for t in range(16, min(rows, cap) + 1, 16) if rows % t == 0)


def _rs_pair_add(place, gpack, ra, *, name):
    _, R, C = gpack.shape
    Rh = R // 2
    tr = _row_tile(Rh)
    nrb = Rh // tr

    def body(p_ref, g_ref, ra_ref, pair_ref, own_ref):
        s = g_ref[...].astype(F32) + ra_ref[...].astype(F32)
        pair_ref[...] = s.astype(BF16)

        @pl.when(pl.program_id(1) == p_ref[1])
        def _():
            own_ref[...] = s

    grid_spec = pltpu.PrefetchScalarGridSpec(
        num_scalar_prefetch=1, grid=(nrb, N_CHIPS),
        in_specs=[pl.BlockSpec((None, tr, C), lambda i, j, p: (j, p[0] * nrb + i, 0)),
                  pl.BlockSpec((None, tr, C), lambda i, j, p: (j, i, 0))],
        out_specs=[pl.BlockSpec((None, tr, C), lambda i, j, p: (j, i, 0)), pl.BlockSpec((tr, C), lambda i, j, p: (i, 0))])
    return pl.pallas_call(
        body, name=name, grid_spec=grid_spec,
        out_shape=[jax.ShapeDtypeStruct((N_CHIPS, Rh, C), BF16), jax.ShapeDtypeStruct((Rh, C), F32)],
        compiler_params=pltpu.CompilerParams(dimension_semantics=("arbitrary", "arbitrary"),
                                             vmem_limit_bytes=VMEM_LIMIT_V7X))(place, gpack, ra)


def _rs_chip_exchange(pairs, small):
    n = len(pairs)

    def body(*refs):
        p_refs, s_ref, rb_refs, rs_ref = refs[:n], refs[n], refs[n + 1:2 * n + 1], refs[2 * n + 1]
        local_sem = refs[-1]
        x, y, c, chips = _place()
        copy = _remote(*refs[-3:-1])
        dev = 4 * x + 2 * y + c
        mine = pltpu.make_async_copy(s_ref, rs_ref.at[dev], local_sem.at[0])
        mine.start()
        cps = []
        for g in range(n):
            cps += [copy(3 * g + j, p_refs[g].at[2 * cx + cy], rb_refs[g].at[j], (cx, cy, c)) for j, (cx, cy) in enumerate(chips)]
        peers = []
        for k in range(1, 8):
            px = 1 - x if k & 4 else x
            py = 1 - y if k & 2 else y
            pc = 1 - c if k & 1 else c
            peers.append((px, py, pc))
            cps.append(copy(3 * n - 1 + k, s_ref, rs_ref.at[dev], (px, py, pc)))
        for cp in cps:
            cp.start()
        for g in range(n):
            for j in range(3):
                copy(3 * g + j, p_refs[g].at[0], rb_refs[g].at[j], (x, y, c)).wait_recv()
        for k, (px, py, pc) in enumerate(peers, start=1):
            copy(3 * n - 1 + k, s_ref, rs_ref.at[4 * px + 2 * py + pc], (x, y, c)).wait_recv()
        for cp in cps:
            cp.wait_send()
        mine.wait()

    out_shape = [jax.ShapeDtypeStruct((3,) + p.shape[1:], p.dtype) for p in pairs]
    out_shape.append(jax.ShapeDtypeStruct((8,) + small.shape, small.dtype))
    n_sems = 3 * n + 7
    return pl.pallas_call(body, name="rs_chip_exchange", out_shape=out_shape, in_specs=[ANY] * (n + 1),
                          out_specs=[ANY] * (n + 1),
                          scratch_shapes=[pltpu.SemaphoreType.DMA((n_sems,)), pltpu.SemaphoreType.DMA((n_sems,)),
                                          pltpu.SemaphoreType.DMA((1,))])(*pairs, small)


def _rs_final_add(place, own, rb, *, name):
    Rh, C = own.shape
    tr = _row_tile(Rh)
    nrb = Rh // tr

    def body(p_ref, o_ref, rb_ref, f_ref):
        f_ref[...] = ((o_ref[...] + rb_ref[0].astype(F32)) + rb_ref[1].astype(F32)) + rb_ref[2].astype(F32)

    grid_spec = pltpu.PrefetchScalarGridSpec(
        num_scalar_prefetch=1, grid=(nrb,),
        in_specs=[pl.BlockSpec((tr, C), lambda i, p: (i, 0)), pl.BlockSpec((3, tr, C), lambda i, p: (0, i, 0))],
        out_specs=pl.BlockSpec((tr, C), lambda i, p: (p[0] * nrb + i, 0)))
    return pl.pallas_call(
        body, name=name, grid_spec=grid_spec, out_shape=jax.ShapeDtypeStruct((2 * Rh, C), F32),
        compiler_params=pltpu.CompilerParams(dimension_semantics=("arbitrary",), vmem_limit_bytes=VMEM_LIMIT_V7X))(place, own, rb)


def _sum_slots(rs):
    n, rows, C = rs.shape

    def body(r_ref, o_ref):
        acc = r_ref[0]
        for k in range(1, n):
            acc = acc + r_ref[k]
        o_ref[...] = acc

    return _call(body, name="small_grad_sum", grid=(1,), in_specs=[pl.BlockSpec((n, rows, C), lambda i: (0, 0, 0))],
                 out_specs=pl.BlockSpec((rows, C), lambda i: (0, 0)), out_shape=jax.ShapeDtypeStruct((rows, C), F32),
                 sem=("arbitrary",))(rs)


def _rs_sibling_share(gbufs):
    n = len(gbufs)

    def body(*refs):
        g_refs = refs[n:2 * n]
        x, y, c, _ = _place()
        copy = _remote(*refs[-2:])
        halves = [_halves(g_refs[g], gbufs[g].shape[0]) for g in range(n)]
        outs = [copy(g, halves[g]((), c), halves[g]((), c), (x, y, 1 - c)) for g in range(n)]
        for cp in outs:
            cp.start()
        for g in range(n):
            copy(g, halves[g]((), 1 - c), halves[g]((), 1 - c), (x, y, c)).wait_recv()
        for cp in outs:
            cp.wait_send()

    return pl.pallas_call(body, name="rs_sibling_share", out_shape=[jax.ShapeDtypeStruct(a.shape, a.dtype) for a in gbufs],
                          in_specs=[ANY] * n, out_specs=[ANY] * n, input_output_aliases={g: g for g in range(n)},
                          scratch_shapes=[pltpu.SemaphoreType.DMA((n,)), pltpu.SemaphoreType.DMA((n,))])(*gbufs)


def _adamw(w, g, m, v, *, name, g_row=0):
    rows, cols = w.shape
    tr = rows
    for cand in range(min(rows, 512), 7, -8):
        if rows % cand == 0 and g_row % cand == 0:
            tr = cand
            break
    spec = pl.BlockSpec((tr, cols), lambda i: (i, 0))
    g_spec = pl.BlockSpec((tr, cols), lambda i: (g_row // tr + i, 0))

    def body(w_ref, g_ref, m_ref, v_ref, d_ref, nm_ref, nv_ref):
        gg = g_ref[...]
        nm = ADAM_B1 * m_ref[...] + (1.0 - ADAM_B1) * gg
        nv = ADAM_B2 * v_ref[...] + (1.0 - ADAM_B2) * (gg * gg)
        m_hat = nm / (1.0 - ADAM_B1 ** ADAM_STEP)
        v_hat = nv / (1.0 - ADAM_B2 ** ADAM_STEP)
        d_ref[...] = -ADAM_LR * (m_hat / (jnp.sqrt(v_hat) + ADAM_EPS) + ADAM_WD * w_ref[...])
        nm_ref[...] = nm
        nv_ref[...] = nv

    return _call(body, name=name, grid=(rows // tr,), in_specs=[spec, g_spec, spec, spec], out_specs=[spec] * 3,
                 out_shape=[jax.ShapeDtypeStruct((rows, cols), F32)] * 3, sem=("parallel",))(w, g, m, v)


WEIGHT_NAMES = ("mix_norm", "ab_w_in", "lru_conv_w", "lru_conv_b", "lru_wa", "lru_ba", "lru_wx", "lru_bx", "lru_lambda",
                "ab_w_out", "c_w_qkv", "c_b_qkv", "c_sinks", "c_w_out", "c_b_out", "xa_norm", "xa_mem_norm", "xa_wq",
                "xa_wkv", "xa_wo", "ffn_norm", "ffn_w_gate_up", "ffn_w_down", "final_norm")
GROUPS = (("ab_w_out", "c_w_out", "xa_wkv", "ffn_w_down"), ("ab_w_in",), ("ffn_w_gate_up",), ("lru_wa", "lru_wx", "xa_wo"),
          ("xa_wq",), ("c_w_qkv",))
REPLICATED = ("mix_norm", "lru_conv_b", "lru_lambda", "c_sinks", "xa_norm", "xa_mem_norm", "ffn_norm", "final_norm")
SMALL_SHARDED = ("lru_conv_w", "lru_ba", "lru_bx", "c_b_qkv", "c_b_out")
LANES = 1024


def _rows(v):
    flat = v.reshape(-1)
    return jnp.pad(flat, (0, -flat.shape[0] % LANES)).reshape(-1, LANES)


def _pack_small(parts, total, *, name):
    def body(*refs):
        o_ref = refs[-1]
        o_ref[...] = jnp.zeros_like(o_ref)
        row = 0
        for p_ref in refs[:-1]:
            o_ref[row:row + p_ref.shape[0], :] = p_ref[...]
            row += p_ref.shape[0]

    return _call(body, name=name, grid=(1,), in_specs=[pl.BlockSpec(p.shape, lambda i: (0, 0)) for p in parts],
                 out_specs=pl.BlockSpec((total, LANES), lambda i: (0, 0)),
                 out_shape=jax.ShapeDtypeStruct((total, LANES), F32), sem=("arbitrary",))(*parts)


def _from_shards(name, t):
    minor = t.shape[-1]
    if name == "ab_w_in":
        return t
    if name in ("ab_w_out", "c_w_out"):
        return t.reshape(1, -1, minor)
    if name == "ffn_w_gate_up":
        return t.reshape(N_CHIPS, 2, -1, minor)
    if name in ("xa_wq", "xa_wkv", "ffn_w_down"):
        return t.reshape(N_CHIPS, 2, -1, minor).transpose(1, 0, 2, 3).reshape(2, -1, minor)
    if name in ("lru_wa", "lru_wx"):
        return t.reshape(N_CHIPS, LRU_HEADS, -1, minor).transpose(1, 0, 2, 3).reshape(LRU_HEADS, LRU_HEAD_DIM, minor)
    if name == "xa_wo":
        return t.reshape(N_CHIPS, 2, -1, minor).transpose(1, 0, 2, 3)
    assert name == "c_w_qkv"
    return t.transpose(1, 0, 2).reshape(1, D_MODEL, -1)


def _to_shards(name, g):
    minor = g.shape[-1]
    if name == "ab_w_in":
        return g
    if name in ("ab_w_out", "c_w_out"):
        return g.reshape(N_CHIPS, -1, minor)
    if name == "ffn_w_gate_up":
        return g.reshape(N_CHIPS, -1, minor)
    if name in ("xa_wq", "xa_wkv", "ffn_w_down"):
        return g.reshape(2, N_CHIPS, -1, minor).transpose(1, 0, 2, 3).reshape(N_CHIPS, -1, minor)
    if name in ("lru_wa", "lru_wx"):
        return g.reshape(LRU_HEADS, N_CHIPS, -1, minor).transpose(1, 0, 2, 3).reshape(N_CHIPS, -1, minor)
    if name == "xa_wo":
        return g.transpose(1, 0, 2, 3).reshape(N_CHIPS, -1, minor)
    assert name == "c_w_qkv"
    return g.reshape(D_MODEL, N_CHIPS, -1).transpose(1, 0, 2)


def kernel(x, mem, mix_norm, ab_w_in, lru_conv_w, lru_conv_b, lru_wa, lru_ba, lru_wx, lru_bx, lru_lambda, ab_w_out, c_w_qkv, c_b_qkv, c_sinks, c_w_out, c_b_out, xa_norm, xa_mem_norm, xa_wq, xa_wkv, xa_wo, ffn_norm, ffn_w_gate_up, ffn_w_down, final_norm, loss_target, m_mix_norm, m_ab_w_in, m_lru_conv_w, m_lru_conv_b, m_lru_wa, m_lru_ba, m_lru_wx, m_lru_bx, m_lru_lambda, m_ab_w_out, m_c_w_qkv, m_c_b_qkv, m_c_sinks, m_c_w_out, m_c_b_out, m_xa_norm, m_xa_mem_norm, m_xa_wq, m_xa_wkv, m_xa_wo, m_ffn_norm, m_ffn_w_gate_up, m_ffn_w_down, m_final_norm, v_mix_norm, v_ab_w_in, v_lru_conv_w, v_lru_conv_b, v_lru_wa, v_lru_ba, v_lru_wx, v_lru_bx, v_lru_lambda, v_ab_w_out, v_c_w_qkv, v_c_b_qkv, v_c_sinks, v_c_w_out, v_c_b_out, v_xa_norm, v_xa_mem_norm, v_xa_wq, v_xa_wkv, v_xa_wo, v_ffn_norm, v_ffn_w_gate_up, v_ffn_w_down, v_final_norm):
    given = dict(locals())
    wl = {n: given[n] for n in WEIGHT_NAMES}
    ml = {n: given["m_" + n] for n in WEIGHT_NAMES}
    vl = {n: given["v_" + n] for n in WEIGHT_NAMES}
    xi, yi, ci = lax.axis_index("x"), lax.axis_index("y"), lax.axis_index("c")
    chip = 2 * xi + yi

    def join(parts, axis):
        return parts[0] if len(parts) == 1 else jnp.concatenate(parts, axis=axis)

    local_rows = {n: wl[n].size // wl[n].shape[-1] for grp in GROUPS for n in grp}
    packs = [join([wl[n].astype(BF16).reshape(local_rows[n], wl[n].shape[-1]) for n in grp], 0) for grp in GROUPS]
    spack = _pack_small([_rows(wl[n]) for n in SMALL_SHARDED], 8, name="pack_small_weights")
    gathered, sfull = _gather_weights(packs, spack)
    w = {n: wl[n] for n in REPLICATED}
    w["c_sinks"] = wl["c_sinks"][0]
    for grp, full in zip(GROUPS, gathered):
        off = 0
        for n in grp:
            w[n] = _from_shards(n, full if len(grp) == 1 else full[:, off:off + local_rows[n]])
            off += local_rows[n]
    for r, n in enumerate(SMALL_SHARDED):
        loc = wl[n].shape[1:]
        t = sfull[:, r, :wl[n].size].reshape((N_CHIPS,) + loc)
        if n == "lru_conv_w":
            w[n] = t.transpose(1, 0, 2).reshape(CONV_WIDTH, -1)
        elif n in ("lru_ba", "lru_bx"):
            w[n] = t.transpose(1, 0, 2).reshape(1, -1)
        else:
            w[n] = t.reshape(1, -1)

    loss_part, grad_x, g = _device_step(x[0], mem[0], loss_target[0], w)

    gpacks = [join([_to_shards(n, g[n].astype(BF16)) for n in grp], 1) for grp in GROUPS]
    small_parts = [_rows(g[n]) for n in REPLICATED] + [_rows(jnp.broadcast_to(loss_part, (LANES,)))]
    small_parts += [_rows(g[n]) for n in SMALL_SHARDED]
    small = _pack_small(small_parts, 24, name="pack_small_grads")
    place = jnp.stack([ci, chip]).astype(jnp.int32)
    ras = _rs_pair_exchange(gpacks)
    sums = [_rs_pair_add(place, gp, ra, name=f"rs_pair_add_{i}") for i, (gp, ra) in enumerate(zip(gpacks, ras))]
    *rbs, rs = _rs_chip_exchange([pair for pair, _ in sums], small)
    gsums = _rs_sibling_share([_rs_final_add(place, own, rb, name=f"rs_final_add_{i}")
                               for i, ((_, own), rb) in enumerate(zip(sums, rbs))])
    ssum = _sum_slots(rs)

    grads, grad_rows = {}, {}
    for grp, gsum in zip(GROUPS, gsums):
        off = 0
        for n in grp:
            grad_rows[n] = (gsum, off)
            grads[n] = (gsum if len(grp) == 1 else gsum[off:off + local_rows[n]]).reshape(wl[n].shape)
            off += local_rows[n]
    row = 0
    for n in REPLICATED:
        k = _rows(g[n]).shape[0]
        grads[n] = ssum[row:row + k].reshape(-1)[:wl[n].size].reshape(wl[n].shape)
        row += k
    loss = ssum[row, 0]
    row += 1
    for n in SMALL_SHARDED:
        k = _rows(g[n]).shape[0]
        full = ssum[row:row + k].reshape(-1)[:g[n].size]
        row += k
        loc = wl[n].shape
        if n == "lru_conv_w":
            sh = full.reshape(CONV_WIDTH, N_CHIPS, -1)
        elif n in ("lru_ba", "lru_bx"):
            sh = full.reshape(LRU_HEADS, N_CHIPS, -1)
        else:
            sh = full.reshape(1, N_CHIPS, -1)
        grads[n] = lax.dynamic_index_in_dim(sh, chip, axis=1, keepdims=False).reshape(loc)

    delta, new_m, new_v = {}, {}, {}
    for n, (gsum, off) in grad_rows.items():
        shape2 = (local_rows[n], wl[n].shape[-1])
        d, nm, nv = _adamw(wl[n].reshape(shape2), gsum, ml[n].reshape(shape2), vl[n].reshape(shape2), g_row=off,
                           name="adamw_" + n)
        delta[n], new_m[n], new_v[n] = (t.reshape(wl[n].shape) for t in (d, nm, nv))
    smalls = REPLICATED + SMALL_SHARDED
    packs = [_pack_small([_rows(src[n]) for n in smalls], 24, name="pack_adamw_" + tag)
             for tag, src in (("w", wl), ("g", grads), ("m", ml), ("v", vl))]
    outs = _adamw(*packs, name="adamw_small")
    row = 0
    for n in smalls:
        k = _rows(wl[n]).shape[0]
        for dst, o in zip((delta, new_m, new_v), outs):
            dst[n] = o[row:row + k].reshape(-1)[:wl[n].size].reshape(wl[n].shape)
        row += k

    return (loss, grad_x[None], *[grads[n] for n in WEIGHT_NAMES], *[delta[n] for n in WEIGHT_NAMES],
            *[new_m[n] for n in WEIGHT_NAMES], *[new_v[n] for n in WEIGHT_NAMES])
```

```python
import jax
import jax.numpy as jnp
from jax import lax
from jax.experimental import pallas as pl
from jax.experimental.pallas import tpu as pltpu
from jax.experimental.pallas import tpu_sc as plsc

F32, BF16 = jnp.float32, jnp.bfloat16
D_MODEL = 1024
NORM_EPS = 1e-6
ROPE_THETA = 500000.0
HEAD_DIM = 64
ROT_DIM = 16
BLK = 128
LRU_HEADS, LRU_HEAD_DIM, CONV_WIDTH, LRU_C = 4, 256, 4, 8.0
DILATED_PATTERN = ((128, 1), (512, 4), (2048, 16))
B_HEADS, C_HEADS, C_KV_HEADS, C_WINDOW = 8, 16, 2, 128
XA_HEADS, XA_HEAD_DIM, N_MEM = 4, 128, 256
D_FF = 2816
NEG = -1e30
ADAM_LR, ADAM_B1, ADAM_B2, ADAM_EPS, ADAM_WD, ADAM_STEP = 0.001, 0.9, 0.999, 1e-08, 0.01, 10
N_CHIPS = 4
VMEM_LIMIT_V7X = 56 * 1024 * 1024

NN = (((1,), (0,)), ((), ()))
NT = (((1,), (1,)), ((), ()))
TN = (((0,), (0,)), ((), ()))


def _dot(a, b, dims=NN):
    return lax.dot_general(a, b, dims, preferred_element_type=F32)


def _sigmoid(x):
    return 1.0 / (1.0 + jnp.exp(-x))


def _call(body, *, name, grid, in_specs, out_specs, out_shape, scratch=(), sem=None):
    return pl.pallas_call(
        body, name=name, grid=grid, in_specs=in_specs, out_specs=out_specs, out_shape=out_shape,
        scratch_shapes=list(scratch),
        compiler_params=pltpu.CompilerParams(dimension_semantics=sem, vmem_limit_bytes=VMEM_LIMIT_V7X))


def _rope_tables(L):
    half = ROT_DIM // 2
    inv = ROPE_THETA ** (-jnp.arange(0, ROT_DIM, 2, dtype=F32) / ROT_DIM)
    ang = jnp.arange(L, dtype=F32)[:, None] * inv[None, :]
    cos, sin = jnp.cos(ang), jnp.sin(ang)
    rest = HEAD_DIM - ROT_DIM
    z8, zr, one = jnp.zeros((L, half), F32), jnp.zeros((L, rest), F32), jnp.ones((L, rest), F32)
    c = jnp.concatenate([cos, cos, one], axis=1)
    s1 = jnp.concatenate([-sin, z8, zr], axis=1)
    s2 = jnp.concatenate([z8, sin, zr], axis=1)
    return tuple(jnp.concatenate([t, t], axis=1) for t in (c, s1, s2))


def _rope_fwd(v, c, s1, s2):
    return v * c + pltpu.roll(v, 120, 1) * s1 + pltpu.roll(v, 8, 1) * s2


def _rope_bwd(dv, c, s1, s2):
    return dv * c + pltpu.roll(dv * s1, 8, 1) + pltpu.roll(dv * s2, 120, 1)


def _weight_spec(w, layer):
    if layer is None:
        return w.shape, pl.BlockSpec(w.shape, lambda i: (0, 0, 0))
    S, _, K, Ns = w.shape
    return (S, K, Ns), pl.BlockSpec((S, None, K, Ns), lambda i: (0, layer, 0, 0))


def _rowmm(a, w3, *, name, tm=256, gain=None, bias=None, res=None, swiglu=False, rope=None, layer=None):
    M, K = a.shape
    (S, _, Ns), w_spec = _weight_spec(w3, layer)
    N = S * Ns
    tm = min(tm, M)
    has_norm, has_bias, has_res, has_rope = gain is not None, bias is not None, res is not None, rope is not None
    row = lambda w: pl.BlockSpec((tm, w), lambda i: (i, 0))
    whole = lambda shape: pl.BlockSpec(shape, lambda i: (0,) * len(shape))
    ins, specs = [a], [row(K)]
    if has_norm:
        ins.append(gain.reshape(1, K)); specs.append(whole((1, K)))
    ins.append(w3); specs.append(w_spec)
    if has_bias:
        ins.append(bias.reshape(1, N)); specs.append(whole((1, N)))
    if has_res:
        ins.append(res); specs.append(row(N))
    if has_rope:
        ins += list(rope[2]); specs += [row(128)] * 3
    y_dtype = F32 if has_res else BF16
    out_shape, out_specs = [jax.ShapeDtypeStruct((M, N), y_dtype)], [row(N)]
    if has_norm:
        out_shape.append(jax.ShapeDtypeStruct((M, K), BF16)); out_specs.append(row(K))
    if swiglu:
        out_shape.append(jax.ShapeDtypeStruct((M, N // 2), BF16)); out_specs.append(row(N // 2))
    scratch = [pltpu.VMEM((tm, N), F32)] if has_rope else []

    def body(*refs):
        it = iter(refs)
        a_ref = next(it)
        g_ref = next(it) if has_norm else None
        w_ref = next(it)
        b_ref = next(it) if has_bias else None
        r_ref = next(it) if has_res else None
        tabs = [next(it) for _ in range(3)] if has_rope else None
        y_ref = next(it)
        n_ref = next(it) if has_norm else None
        act_ref = next(it) if swiglu else None
        ys_ref = next(it) if has_rope else None
        if has_norm:
            x = a_ref[...].astype(F32)
            ms = jnp.mean(x * x, axis=-1, keepdims=True)
            xb = (x * lax.rsqrt(ms + NORM_EPS) * g_ref[...]).astype(BF16)
            n_ref[...] = xb
        else:
            xb = a_ref[...].astype(BF16)
        if swiglu:
            for s in range(S // 2):
                g = _dot(xb, w_ref[s])
                u = _dot(xb, w_ref[s + S // 2])
                y_ref[:, s * Ns:(s + 1) * Ns] = g.astype(BF16)
                y_ref[:, N // 2 + s * Ns:N // 2 + (s + 1) * Ns] = u.astype(BF16)
                act_ref[:, s * Ns:(s + 1) * Ns] = (g * _sigmoid(g) * u).astype(BF16)
            return
        for s in range(S):
            sl = slice(s * Ns, (s + 1) * Ns)
            acc = _dot(xb, w_ref[s])
            if has_bias:
                acc = acc + b_ref[:, sl]
            if has_res:
                acc = acc + r_ref[:, sl]
            if has_rope:
                ys_ref[:, sl] = acc
            else:
                y_ref[:, sl] = acc.astype(y_dtype)
        if has_rope:
            c, s1, s2 = (t[...] for t in tabs)
            for cb in range(N // 128):
                cs = slice(cb * 128, (cb + 1) * 128)
                v = ys_ref[:, cs]
                if rope[0] <= cb * 128 < rope[1]:
                    v = _rope_fwd(v, c, s1, s2)
                y_ref[:, cs] = v.astype(BF16)

    return _call(body, name=name, grid=(M // tm,), in_specs=specs, out_specs=out_specs, out_shape=out_shape,
                 scratch=scratch, sem=("parallel",))(*ins)


def _mm_nt(dy, w3, *, name, mode, tm=256, kchunk=None, h=None, gain=None, dh=None, gu=None, layer=None):
    M, N = dy.shape
    (S, K, Ns), w_spec = _weight_spec(w3, layer)
    kchunk = kchunk or K
    tm = min(tm, M)
    row = lambda w: pl.BlockSpec((tm, w), lambda i: (i, 0))
    whole = lambda shape: pl.BlockSpec(shape, lambda i: (0,) * len(shape))
    ins, specs = [dy, w3], [row(N), w_spec]
    has_dh = dh is not None
    if mode == "norm":
        ins += [h, gain.reshape(1, K)]; specs += [row(K), whole((1, K))]
        if has_dh:
            ins.append(dh); specs.append(row(K))
        out_shape = [jax.ShapeDtypeStruct((M, K), F32), jax.ShapeDtypeStruct((1, K), F32)]
        out_specs = [row(K), whole((1, K))]
    elif mode == "swiglu":
        ins.append(gu); specs.append(row(2 * K))
        out_shape, out_specs = [jax.ShapeDtypeStruct((M, 2 * K), BF16)], [row(2 * K)]
    else:
        out_shape, out_specs = [jax.ShapeDtypeStruct((M, K), BF16)], [row(K)]

    def body(*refs):
        it = iter(refs)
        dy_ref, w_ref = next(it), next(it)
        if mode == "norm":
            h_ref, g_ref = next(it), next(it)
            dh_ref = next(it) if has_dh else None
            o_ref, dg_ref = next(it), next(it)
        elif mode == "swiglu":
            gu_ref, o_ref = next(it), next(it)
        else:
            o_ref = next(it)
        for kc in range(K // kchunk):
            ks = slice(kc * kchunk, (kc + 1) * kchunk)
            acc = None
            for s in range(S):
                t = _dot(dy_ref[:, s * Ns:(s + 1) * Ns].astype(BF16), w_ref[s, ks, :], NT)
                acc = t if acc is None else acc + t
            if mode == "plain":
                o_ref[:, ks] = acc.astype(BF16)
            elif mode == "swiglu":
                us = slice(K + kc * kchunk, K + (kc + 1) * kchunk)
                g = gu_ref[:, ks].astype(F32)
                u = gu_ref[:, us].astype(F32)
                sg = _sigmoid(g)
                o_ref[:, ks] = (acc * u * (sg * (1.0 + g * (1.0 - sg)))).astype(BF16)
                o_ref[:, us] = (acc * (g * sg)).astype(BF16)
            else:
                x = h_ref[...].astype(F32)
                r = lax.rsqrt(jnp.mean(x * x, axis=-1, keepdims=True) + NORM_EPS)
                xhat = x * r
                dxh = acc * g_ref[...]
                dx = r * (dxh - xhat * jnp.mean(dxh * xhat, axis=-1, keepdims=True))
                o_ref[...] = dx + dh_ref[...] if has_dh else dx

                @pl.when(pl.program_id(0) == 0)
                def _():
                    dg_ref[...] = jnp.zeros_like(dg_ref)

                dg_ref[...] += jnp.sum(acc * xhat, axis=0, keepdims=True)

    sem = ("arbitrary",) if mode == "norm" else ("parallel",)
    return _call(body, name=name, grid=(M // tm,), in_specs=specs, out_specs=out_specs, out_shape=out_shape, sem=sem)(*ins)


def _mm_tn(x, dy, *, S, name, tk=1024, kk=None, bias=False):
    M, K = x.shape
    N = dy.shape[1]
    Ns = N // S
    kk = kk or K
    tk = min(tk, M)
    nl = M // tk
    in_specs = [pl.BlockSpec((tk, kk), lambda s, kc, l: (l, kc)), pl.BlockSpec((tk, Ns), lambda s, kc, l: (l, s))]
    out_shape = [jax.ShapeDtypeStruct((S, K, Ns), BF16)]
    out_specs = [pl.BlockSpec((None, kk, Ns), lambda s, kc, l: (s, kc, 0))]
    if bias:
        out_shape.append(jax.ShapeDtypeStruct((1, N), F32))
        out_specs.append(pl.BlockSpec((1, Ns), lambda s, kc, l: (0, s)))

    def body(x_ref, dy_ref, o_ref, *rest):
        acc_ref = rest[-1]
        kc, l = pl.program_id(1), pl.program_id(2)

        @pl.when(l == 0)
        def _():
            acc_ref[...] = jnp.zeros_like(acc_ref)

        acc_ref[...] += _dot(x_ref[...].astype(BF16), dy_ref[...].astype(BF16), TN)
        if bias:
            b_ref = rest[0]

            @pl.when((kc == 0) & (l == 0))
            def _():
                b_ref[...] = jnp.zeros_like(b_ref)

            @pl.when(kc == 0)
            def _():
                b_ref[...] += jnp.sum(dy_ref[...].astype(F32), axis=0, keepdims=True)

        @pl.when(l == nl - 1)
        def _():
            o_ref[...] = acc_ref[...].astype(BF16)

    return _call(body, name=name, grid=(S, K // kk, nl), in_specs=in_specs, out_specs=out_specs, out_shape=out_shape,
                 scratch=[pltpu.VMEM((kk, Ns), F32)], sem=("arbitrary", "arbitrary", "arbitrary"))(x, dy)


def _band_bias(max_dist, has_prev):
    rows = lax.broadcasted_iota(jnp.int32, (BLK, 2 * BLK), 0)
    cols = lax.broadcasted_iota(jnp.int32, (BLK, 2 * BLK), 1)
    dist = rows - cols + BLK
    ok = (dist >= 0) & (dist <= max_dist) & ((cols >= BLK) | has_prev)
    return jnp.where(ok, 0.0, NEG)


Q_SCALE = HEAD_DIM ** -0.5


def _band_fwd(qa, ka, va, *, d, nq, nkv, qcol, kcol, vcol, max_dist, sinks=None, name):
    Lr = qa.shape[0]
    nb = Lr // BLK
    qw, kw, G = nq * HEAD_DIM, nkv * HEAD_DIM, nq // nkv
    cur = lambda colf, w: pl.BlockSpec((BLK, w), lambda r, i: (i, colf(r)))
    prv = lambda colf, w: pl.BlockSpec((BLK, w), lambda r, i: (jnp.maximum(i - 1, 0), colf(r)))
    out = pl.BlockSpec((BLK, qw), lambda r, i: (i, r))
    ins, specs = [qa, ka, ka, va, va], [cur(qcol, qw), cur(kcol, kw), prv(kcol, kw), cur(vcol, kw), prv(vcol, kw)]
    has_sinks = sinks is not None
    if has_sinks:
        ins.append(sinks); specs.append(pl.BlockSpec(memory_space=pltpu.SMEM))

    def body(*refs):
        q_ref, kc_ref, kp_ref, vc_ref, vp_ref = refs[:5]
        sk_ref = refs[5] if has_sinks else None
        o_ref, lse_ref = refs[-2], refs[-1]
        bias = _band_bias(max_dist, pl.program_id(1) > 0)
        k2 = jnp.concatenate([kp_ref[...], kc_ref[...]], axis=0)
        v2 = jnp.concatenate([vp_ref[...], vc_ref[...]], axis=0)
        for h in range(nq):
            hs = slice(h * HEAD_DIM, (h + 1) * HEAD_DIM)
            ks = slice((h // G) * HEAD_DIM, (h // G + 1) * HEAD_DIM)
            s = _dot(q_ref[:, hs] * jnp.asarray(Q_SCALE, BF16), k2[:, ks], NT) + bias
            m = jnp.max(s, axis=-1, keepdims=True)
            if has_sinks:
                m = jnp.maximum(m, sk_ref[h])
            p = jnp.exp(s - m)
            l = jnp.sum(p, axis=-1, keepdims=True)
            if has_sinks:
                l = l + jnp.exp(sk_ref[h] - m)
            o_ref[:, hs] = (_dot(p.astype(BF16), v2[:, ks]) / l).astype(BF16)
            lse_ref[:, hs] = jnp.broadcast_to(m + jnp.log(l), (BLK, HEAD_DIM))

    return _call(body, name=name, grid=(d, nb), in_specs=specs, out_specs=[out, out],
                 out_shape=[jax.ShapeDtypeStruct((Lr, d * qw), BF16), jax.ShapeDtypeStruct((Lr, d * qw), F32)],
                 sem=("parallel", "parallel"))(*ins)


def _band_bwd(qa, ka, va, doa, oa, lsea, *, d, nq, nkv, qcol, kcol, vcol, docol, max_dist, sinks=None, name):
    Lr = qa.shape[0]
    nb = Lr // BLK
    qw, kw, G = nq * HEAD_DIM, nkv * HEAD_DIM, nq // nkv
    transposed = G > 1
    last = lambda i: jnp.minimum(i, nb - 1)
    cur = lambda colf, w: pl.BlockSpec((BLK, w), lambda r, i: (last(i), colf(r)))
    prv = lambda colf, w: pl.BlockSpec((BLK, w), lambda r, i: (jnp.maximum(last(i) - 1, 0), colf(r)))
    own = lambda r: r
    ins = [qa, ka, ka, va, va, doa, oa, lsea]
    specs = [cur(qcol, qw), cur(kcol, kw), prv(kcol, kw), cur(vcol, kw), prv(vcol, kw), cur(docol, qw), cur(own, qw),
             cur(own, qw)]
    has_sinks = sinks is not None
    if has_sinks:
        ins.append(sinks); specs.append(pl.BlockSpec(memory_space=pltpu.SMEM))
    out_shape = [jax.ShapeDtypeStruct((Lr, d * qw), F32), jax.ShapeDtypeStruct((Lr, d * kw), F32),
                 jax.ShapeDtypeStruct((Lr, d * kw), F32)]
    behind = lambda r, i: (jnp.maximum(i - 1, 0), r)
    out_specs = [pl.BlockSpec((BLK, qw), lambda r, i: (last(i), r)), pl.BlockSpec((BLK, kw), behind),
                 pl.BlockSpec((BLK, kw), behind)]
    if has_sinks:
        out_shape.append(jax.ShapeDtypeStruct((8, 128), F32))
        out_specs.append(pl.BlockSpec((8, 128), lambda r, i: (0, 0)))

    def body(*refs):
        it = iter(refs)
        q_ref, kc_ref, kp_ref, vc_ref, vp_ref, do_ref, o_ref, ls_ref = (next(it) for _ in range(8))
        sk_ref = next(it) if has_sinks else None
        dq_ref, dk_ref, dv_ref = next(it), next(it), next(it)
        dsk_ref = next(it) if has_sinks else None
        dk_car, dv_car = next(it), next(it)
        r_id, i = pl.program_id(0), pl.program_id(1)

        @pl.when(i == 0)
        def _():
            dk_car[...] = jnp.zeros_like(dk_car)
            dv_car[...] = jnp.zeros_like(dv_car)

        if has_sinks:
            @pl.when((r_id == 0) & (i == 0))
            def _():
                dsk_ref[...] = jnp.zeros_like(dsk_ref)

        @pl.when(i == nb)
        def _():
            dk_ref[...] = dk_car[...]
            dv_ref[...] = dv_car[...]

        @pl.when(i < nb)
        def _():
            bias = _band_bias(max_dist, i > 0)
            k2 = jnp.concatenate([kp_ref[...], kc_ref[...]], axis=0)
            v2 = jnp.concatenate([vp_ref[...], vc_ref[...]], axis=0)
            if has_sinks:
                lane = lax.broadcasted_iota(jnp.int32, (8, 128), 1)
                dsk = jnp.zeros((8, 128), F32)
            for kv in range(nkv):
                ks = slice(kv * HEAD_DIM, (kv + 1) * HEAD_DIM)
                kh, vh = k2[:, ks], v2[:, ks]
                shape = (HEAD_DIM, 2 * BLK) if transposed else (2 * BLK, HEAD_DIM)
                dk, dv = jnp.zeros(shape, F32), jnp.zeros(shape, F32)
                for g in range(G):
                    h = kv * G + g
                    hs = slice(h * HEAD_DIM, (h + 1) * HEAD_DIM)
                    q = q_ref[:, hs] * jnp.asarray(Q_SCALE, BF16)
                    do = do_ref[:, hs]
                    lse = ls_ref[:, h * HEAD_DIM:h * HEAD_DIM + 1]
                    dl = jnp.sum(do.astype(F32) * o_ref[:, hs].astype(F32), axis=-1, keepdims=True)
                    p = jnp.exp(_dot(q, kh, NT) + bias - lse)
                    ds = (p * (_dot(do, vh, NT) - dl)).astype(BF16)
                    dq_ref[:, hs] = _dot(ds, kh) * Q_SCALE
                    if transposed:
                        dk = dk + _dot(q, ds, TN)
                        dv = dv + _dot(do, p.astype(BF16), TN)
                    else:
                        dk = dk + _dot(ds, q, TN)
                        dv = dv + _dot(p.astype(BF16), do, TN)
                    if has_sinks:
                        val = -jnp.sum(jnp.exp(sk_ref[h] - lse) * dl, axis=0, keepdims=True)
                        dsk = dsk + jnp.where(lane == h, val, 0.0)
                if transposed:
                    dk, dv = dk.T, dv.T
                dk_ref[:, ks] = dk_car[:, ks] + dk[:BLK]
                dv_ref[:, ks] = dv_car[:, ks] + dv[:BLK]
                dk_car[:, ks] = dk[BLK:]
                dv_car[:, ks] = dv[BLK:]
            if has_sinks:
                dsk_ref[...] += dsk

    return _call(body, name=name, grid=(d, nb + 1), in_specs=specs, out_specs=out_specs, out_shape=out_shape,
                 scratch=[pltpu.VMEM((BLK, kw), F32), pltpu.VMEM((BLK, kw), F32)], sem=("arbitrary", "arbitrary"))(*ins)


def _attn_grad_combine(branches, tabs, *, name, tm=256):
    L, qw = branches[0][0].shape
    kw = branches[0][1].shape[1]
    nbr = len(branches)
    row = lambda w: pl.BlockSpec((tm, w), lambda i: (i, 0))
    ins, specs = [], []
    for dq, dk, dv in branches:
        ins += [dq, dk, dv]; specs += [row(qw), row(kw), row(kw)]
    ins += list(tabs); specs += [row(128)] * 3

    def body(*refs):
        c, s1, s2 = (t[...] for t in refs[3 * nbr:3 * nbr + 3])
        o_ref = refs[-1]
        for part, (w, off, rot) in enumerate(((qw, 0, True), (kw, qw, True), (kw, qw + kw, False))):
            for cb in range(w // 128):
                cs = slice(cb * 128, (cb + 1) * 128)
                v = refs[part][:, cs]
                for b in range(1, nbr):
                    v = v + refs[3 * b + part][:, cs]
                if rot:
                    v = _rope_bwd(v, c, s1, s2)
                o_ref[:, off + cb * 128:off + (cb + 1) * 128] = v.astype(BF16)

    return _call(body, name=name, grid=(L // tm,), in_specs=specs, out_specs=row(qw + 2 * kw),
                 out_shape=jax.ShapeDtypeStruct((L, qw + 2 * kw), BF16), sem=("parallel",))(*ins)


def _xattn_fwd(q, kv, *, name, tq=512):
    L, W = q.shape
    scale = XA_HEAD_DIM ** -0.5
    row = pl.BlockSpec((tq, W), lambda i: (i, 0))
    kvs = pl.BlockSpec((N_MEM, 2 * W), lambda i: (0, 0))

    def body(q_ref, kv_ref, o_ref, lse_ref):
        for h in range(XA_HEADS):
            hs = slice(h * XA_HEAD_DIM, (h + 1) * XA_HEAD_DIM)
            vs = slice(W + h * XA_HEAD_DIM, W + (h + 1) * XA_HEAD_DIM)
            s = _dot(q_ref[:, hs], kv_ref[:, hs], NT) * scale
            m = jnp.max(s, axis=-1, keepdims=True)
            p = jnp.exp(s - m)
            l = jnp.sum(p, axis=-1, keepdims=True)
            o_ref[:, hs] = (_dot(p.astype(BF16), kv_ref[:, vs]) / l).astype(BF16)
            lse_ref[:, hs] = jnp.broadcast_to(m + jnp.log(l), (tq, XA_HEAD_DIM))

    return _call(body, name=name, grid=(L // tq,), in_specs=[row, kvs], out_specs=[row, row],
                 out_shape=[jax.ShapeDtypeStruct((L, W), BF16), jax.ShapeDtypeStruct((L, W), F32)], sem=("parallel",))(q, kv)


def _xattn_bwd(q, kv, o, lse, do, *, name, tq=512):
    L, W = q.shape
    scale = XA_HEAD_DIM ** -0.5
    row = pl.BlockSpec((tq, W), lambda i: (i, 0))
    kvs = pl.BlockSpec((N_MEM, 2 * W), lambda i: (0, 0))

    def body(q_ref, kv_ref, o_ref, lse_ref, do_ref, dq_ref, dkv_ref):
        @pl.when(pl.program_id(0) == 0)
        def _():
            dkv_ref[...] = jnp.zeros_like(dkv_ref)

        for h in range(XA_HEADS):
            hs = slice(h * XA_HEAD_DIM, (h + 1) * XA_HEAD_DIM)
            vs = slice(W + h * XA_HEAD_DIM, W + (h + 1) * XA_HEAD_DIM)
            qh, kh, vh, doh = q_ref[:, hs], kv_ref[:, hs], kv_ref[:, vs], do_ref[:, hs]
            p = jnp.exp(_dot(qh, kh, NT) * scale - lse_ref[:, h * XA_HEAD_DIM:h * XA_HEAD_DIM + 1])
            dl = jnp.sum(doh.astype(F32) * o_ref[:, hs].astype(F32), axis=-1, keepdims=True)
            ds = (p * (_dot(doh, vh, NT) - dl) * scale).astype(BF16)
            dq_ref[:, hs] = _dot(ds, kh).astype(BF16)
            dkv_ref[:, hs] += _dot(ds, qh, TN)
            dkv_ref[:, vs] += _dot(p.astype(BF16), doh, TN)

    return _call(body, name=name, grid=(L // tq,), in_specs=[row, kvs, row, row, row], out_specs=[row, kvs],
                 out_shape=[jax.ShapeDtypeStruct((L, W), BF16), jax.ShapeDtypeStruct((N_MEM, 2 * W), F32)],
                 sem=("arbitrary",))(q, kv, o, lse, do)


def _neg_expm1(z):
    series = -(z * (1.0 + z * (0.5 + z * (1.0 / 6.0 + z * (1.0 / 24.0 + z * (1.0 / 120.0))))))
    return jnp.where(z > -0.05, series, 1.0 - jnp.exp(z))


def _softplus(z):
    return jnp.maximum(z, 0.0) + jnp.log(1.0 + jnp.exp(-jnp.abs(z)))


def _gelu_parts(y):
    c = 0.7978845608028654
    t = jnp.tanh(c * (y + 0.044715 * y * y * y))
    gy = 0.5 * y * (1.0 + t)
    dgy = 0.5 * (1.0 + t) + 0.5 * y * (1.0 - t * t) * c * (1.0 + 3.0 * 0.044715 * y * y)
    return gy, dgy


def _lru_gates(xc, wa_ref, ba, wx_ref, bx, sp):
    rs, igs = [], []
    for hd in range(LRU_HEADS):
        sl = slice(hd * LRU_HEAD_DIM, (hd + 1) * LRU_HEAD_DIM)
        xh = xc[:, sl].astype(BF16)
        rs.append(_sigmoid(_dot(xh, wa_ref[hd]) + ba[:, sl]))
        igs.append(_sigmoid(_dot(xh, wx_ref[hd]) + bx[:, sl]))
    r, ig = jnp.concatenate(rs, axis=1), jnp.concatenate(igs, axis=1)
    la = -LRU_C * r * sp
    return r, ig, jnp.exp(la), _neg_expm1(2.0 * la)


def _conv_taps(x_ext, halo):
    n = x_ext.shape[0]
    return [x_ext[halo:] if k == CONV_WIDTH - 1 else pltpu.roll(x_ext, CONV_WIDTH - 1 - k, 0)[halo:]
            for k in range(CONV_WIDTH)]


def _lru_fwd(proj, cw, cb, wa, ba, wx, bx, lam, *, name, tc=512):
    L = proj.shape[0]
    W = LRU_HEADS * LRU_HEAD_DIM
    nb = L // tc
    whole = lambda shape: pl.BlockSpec(shape, lambda i: (0,) * len(shape))
    specs = [pl.BlockSpec((tc, W), lambda i: (i, 0)), pl.BlockSpec((tc, W), lambda i: (i, 1)),
             pl.BlockSpec((16, W), lambda i: (jnp.maximum(i * (tc // 16) - 1, 0), 0)),
             whole((CONV_WIDTH, W)), whole((1, W)), whole((LRU_HEADS, LRU_HEAD_DIM, LRU_HEAD_DIM)), whole((1, W)),
             whole((LRU_HEADS, LRU_HEAD_DIM, LRU_HEAD_DIM)), whole((1, W)), whole((1, W))]
    out_specs = [pl.BlockSpec((tc, W), lambda i: (i, 0))] * 2
    out_shape = [jax.ShapeDtypeStruct((L, W), BF16), jax.ShapeDtypeStruct((L, W), F32)]

    def body(x_ref, y_ref, xh_ref, cw_ref, cb_ref, wa_ref, ba_ref, wx_ref, bx_ref, lam_ref, rec_ref, hs_ref,
             hcar, a_scr, b_scr):
        i = pl.program_id(0)

        @pl.when(i == 0)
        def _():
            hcar[...] = jnp.zeros_like(hcar)

        halo = jnp.where(i > 0, xh_ref[...].astype(F32), 0.0)
        taps = _conv_taps(jnp.concatenate([halo, x_ref[...].astype(F32)], axis=0), 16)
        xc = cb_ref[...] + sum(cw_ref[k:k + 1, :] * taps[k] for k in range(CONV_WIDTH))
        _, ig, a, om = _lru_gates(xc, wa_ref, ba_ref[...], wx_ref, bx_ref[...], _softplus(-lam_ref[...]))
        b = jnp.sqrt(om) * (ig * xc)
        rowmod = lax.broadcasted_iota(jnp.int32, (tc, W), 0) & 7
        for s in (1, 2, 4):
            keep = rowmod >= s
            b = jnp.where(keep, a * pltpu.roll(b, s, 0) + b, b)
            a = jnp.where(keep, a * pltpu.roll(a, s, 0), a)
        a_scr[...] = a
        b_scr[...] = b

        def tile(j, hc):
            rows = pl.ds(pl.multiple_of(j * 8, 8), 8)
            ht = a_scr[rows, :] * hc + b_scr[rows, :]
            hs_ref[rows, :] = ht
            return jnp.broadcast_to(ht[7:8, :], (8, W))

        hcar[...] = lax.fori_loop(0, tc // 8, tile, hcar[...])
        gy, _ = _gelu_parts(y_ref[...].astype(F32))
        rec_ref[...] = (hs_ref[...] * gy).astype(BF16)

    return _call(body, name=name, grid=(nb,), in_specs=specs, out_specs=out_specs, out_shape=out_shape,
                 scratch=[pltpu.VMEM((8, W), F32), pltpu.VMEM((tc, W), F32), pltpu.VMEM((tc, W), F32)],
                 sem=("arbitrary",))(proj, proj, proj, cw, cb, wa, ba, wx, bx, lam)


def _lru_bwd(proj, hs, drec_src, cw, cb, wa, ba, wx, bx, lam, *, name, tc=256):
    L = proj.shape[0]
    W = LRU_HEADS * LRU_HEAD_DIM
    nb = L // tc
    tb = lambda i: nb - 1 - i
    whole = lambda shape: pl.BlockSpec(shape, lambda i: (0,) * len(shape))
    gate_w = (LRU_HEADS, LRU_HEAD_DIM, LRU_HEAD_DIM)
    specs = [pl.BlockSpec((tc, W), lambda i: (tb(i), 0)), pl.BlockSpec((tc, W), lambda i: (tb(i), 1)),
             pl.BlockSpec((16, W), lambda i: (jnp.maximum(tb(i) * (tc // 16) - 1, 0), 0)),
             pl.BlockSpec((tc, W), lambda i: (tb(i), 0)),
             pl.BlockSpec((8, W), lambda i: (jnp.maximum(tb(i) * (tc // 8) - 1, 0), 0)),
             pl.BlockSpec((tc, W), lambda i: (tb(i), 0)),
             whole((CONV_WIDTH, W)), whole((1, W)), whole(gate_w), whole((1, W)), whole(gate_w), whole((1, W)), whole((1, W))]
    out_specs = [pl.BlockSpec((tc, 2 * W), lambda i: (tb(i), 0)), whole((CONV_WIDTH, W)), whole((1, W)), whole(gate_w),
                 whole((1, W)), whole(gate_w), whole((1, W)), whole((1, W))]
    vec = jax.ShapeDtypeStruct((1, W), F32)
    out_shape = [jax.ShapeDtypeStruct((L, 2 * W), BF16), jax.ShapeDtypeStruct((CONV_WIDTH, W), F32), vec,
                 jax.ShapeDtypeStruct(gate_w, F32), vec, jax.ShapeDtypeStruct(gate_w, F32), vec, vec]

    def body(x_ref, y_ref, xh_ref, hs_ref, hh_ref, dr_ref, cw_ref, cb_ref, wa_ref, ba_ref, wx_ref, bx_ref, lam_ref,
             dxy_ref, dcw_ref, dcb_ref, dwa_ref, dba_ref, dwx_ref, dbx_ref, dlam_ref, gcar, dxc_car, a_scr, b_scr, g_scr):
        pid = pl.program_id(0)
        t = tb(pid)
        accs = (dcw_ref, dcb_ref, dwa_ref, dba_ref, dwx_ref, dbx_ref, dlam_ref)

        @pl.when(pid == 0)
        def _():
            gcar[...] = jnp.zeros_like(gcar)
            dxc_car[...] = jnp.zeros_like(dxc_car)
            for r in accs:
                r[...] = jnp.zeros_like(r)

        halo = jnp.where(t > 0, xh_ref[...].astype(F32), 0.0)
        taps = _conv_taps(jnp.concatenate([halo, x_ref[...].astype(F32)], axis=0), 16)
        xc = cb_ref[...] + sum(cw_ref[k:k + 1, :] * taps[k] for k in range(CONV_WIDTH))
        lam = lam_ref[...]
        sp = _softplus(-lam)
        r, ig, a, om = _lru_gates(xc, wa_ref, ba_ref[...], wx_ref, bx_ref[...], sp)
        sq = jnp.sqrt(om)
        hblk = hs_ref[...]
        hprev = pltpu.roll(jnp.concatenate([jnp.where(t > 0, hh_ref[...], 0.0), hblk], axis=0), 1, 0)[8:]
        gy, dgy = _gelu_parts(y_ref[...].astype(F32))
        drec = dr_ref[...].astype(F32)
        dxy_ref[:, W:] = (drec * hblk * dgy).astype(BF16)

        rowidx = lax.broadcasted_iota(jnp.int32, (tc, W), 0)
        rowmod = rowidx & 7
        ca = jnp.where(rowidx == tc - 1, 1.0, pltpu.roll(a, tc - 1, 0))
        cbv = drec * gy
        for s in (1, 2, 4):
            keep = rowmod < 8 - s
            cbv = jnp.where(keep, ca * pltpu.roll(cbv, tc - s, 0) + cbv, cbv)
            ca = jnp.where(keep, ca * pltpu.roll(ca, tc - s, 0), ca)
        a_scr[...] = ca
        b_scr[...] = cbv

        def tile(k, gc):
            j = tc // 8 - 1 - k
            rows = pl.ds(pl.multiple_of(j * 8, 8), 8)
            gt = a_scr[rows, :] * gc + b_scr[rows, :]
            g_scr[rows, :] = gt
            return jnp.broadcast_to(gt[0:1, :], (8, W))

        lax.fori_loop(0, tc // 8, tile, gcar[...])
        G = g_scr[...]
        gcar[...] = jnp.broadcast_to(a[0:1, :] * G[0:1, :], (8, W))

        da = G * hprev
        dsq = G * (ig * xc)
        di = G * (sq * xc)
        dxc = G * (sq * ig)
        dla = da * a - 2.0 * a * a * (dsq * 0.5 * lax.rsqrt(om))
        dlam_ref[...] += jnp.sum(dla * (-LRU_C * r), axis=0, keepdims=True) * (-_sigmoid(-lam))
        dpr = dla * (-LRU_C * sp) * r * (1.0 - r)
        dpi = di * ig * (1.0 - ig)
        dba_ref[...] += jnp.sum(dpr, axis=0, keepdims=True)
        dbx_ref[...] += jnp.sum(dpi, axis=0, keepdims=True)
        back = []
        for hd in range(LRU_HEADS):
            sl = slice(hd * LRU_HEAD_DIM, (hd + 1) * LRU_HEAD_DIM)
            xh, dprh, dpih = xc[:, sl].astype(BF16), dpr[:, sl].astype(BF16), dpi[:, sl].astype(BF16)
            back.append(_dot(dprh, wa_ref[hd], NT) + _dot(dpih, wx_ref[hd], NT))
            dwa_ref[hd] += _dot(xh, dprh, TN)
            dwx_ref[hd] += _dot(xh, dpih, TN)
        dxc = dxc + jnp.concatenate(back, axis=1)
        dcb_ref[...] += jnp.sum(dxc, axis=0, keepdims=True)
        for k in range(CONV_WIDTH):
            dcw_ref[k:k + 1, :] += jnp.sum(dxc * taps[k], axis=0, keepdims=True)
        ext = jnp.concatenate([dxc, dxc_car[...]], axis=0)
        dx = cw_ref[CONV_WIDTH - 1:CONV_WIDTH, :] * dxc
        for k in range(CONV_WIDTH - 1):
            dx = dx + cw_ref[k:k + 1, :] * pltpu.roll(ext, tc + 8 - (CONV_WIDTH - 1 - k), 0)[:tc]
        dxc_car[...] = dxc[0:8, :]
        dxy_ref[:, :W] = dx.astype(BF16)

    scratch = [pltpu.VMEM((8, W), F32), pltpu.VMEM((8, W), F32)] + [pltpu.VMEM((tc, W), F32)] * 3
    return _call(body, name=name, grid=(nb,), in_specs=specs, out_specs=out_specs, out_shape=out_shape, scratch=scratch,
                 sem=("arbitrary",))(proj, proj, proj, hs, hs, drec_src, cw, cb, wa, ba, wx, bx, lam)


def _final_loss(h, gain, target, *, name, tm=256):
    M, K = h.shape
    row = pl.BlockSpec((tm, K), lambda i: (i, 0))
    vec = pl.BlockSpec((1, K), lambda i: (0, 0))
    one = pl.BlockSpec((1, 128), lambda i: (0, 0))

    def body(h_ref, g_ref, t_ref, dh_ref, dg_ref, loss_ref):
        @pl.when(pl.program_id(0) == 0)
        def _():
            dg_ref[...] = jnp.zeros_like(dg_ref)
            loss_ref[...] = jnp.zeros_like(loss_ref)

        x = h_ref[...]
        r = lax.rsqrt(jnp.mean(x * x, axis=-1, keepdims=True) + NORM_EPS)
        xhat = x * r
        err = xhat * g_ref[...] - t_ref[...]
        loss_ref[...] += 0.5 / K * jnp.sum(err * err)
        dy = err * (1.0 / K)
        dg_ref[...] += jnp.sum(dy * xhat, axis=0, keepdims=True)
        dxh = dy * g_ref[...]
        dh_ref[...] = r * (dxh - xhat * jnp.mean(dxh * xhat, axis=-1, keepdims=True))

    return _call(body, name=name, grid=(M // tm,), in_specs=[row, vec, row], out_specs=[row, vec, one],
                 out_shape=[jax.ShapeDtypeStruct((M, K), F32), jax.ShapeDtypeStruct((1, K), F32),
                            jax.ShapeDtypeStruct((1, 128), F32)], sem=("arbitrary",))(h, gain.reshape(1, K), target)


def _dilated_merge(branches, *, name, tm=512):
    L, W = branches[0].shape
    nbr = len(branches) // 2
    row = pl.BlockSpec((tm, W), lambda i: (i, 0))

    def body(*refs):
        o_ref, lse_ref = refs[-2], refs[-1]
        lses = [refs[2 * b + 1][...] for b in range(nbr)]
        m = lses[0]
        for t in lses[1:]:
            m = jnp.maximum(m, t)
        ws = [jnp.exp(t - m) for t in lses]
        den = ws[0]
        for t in ws[1:]:
            den = den + t
        acc = ws[0] * refs[0][...].astype(F32)
        for b in range(1, nbr):
            acc = acc + ws[b] * refs[2 * b][...].astype(F32)
        o_ref[...] = (acc / den).astype(BF16)
        lse_ref[...] = m + jnp.log(den)

    return _call(body, name=name, grid=(L // tm,), in_specs=[row] * (2 * nbr), out_specs=[row, row],
                 out_shape=[jax.ShapeDtypeStruct((L, W), BF16), jax.ShapeDtypeStruct((L, W), F32)], sem=("parallel",))(*branches)


def _dilated_fwd(proj0):
    L = proj0.shape[0]
    qkv = proj0[:, 2 * D_MODEL:]
    W = B_HEADS * HEAD_DIM
    outs = []
    for window, d in DILATED_PATTERN:
        view = qkv.reshape(L // d, d * 3 * W)
        o, lse = _band_fwd(view, view, view, d=d, nq=B_HEADS, nkv=B_HEADS, qcol=lambda r: 3 * r, kcol=lambda r: 3 * r + 1,
                           vcol=lambda r: 3 * r + 2, max_dist=window // d, name=f"dilated_fwd_d{d}")
        outs += [o.reshape(L, W), lse.reshape(L, W)]
    return _dilated_merge(outs, name="dilated_merge")


def _dilated_bwd(proj0, att, lse, datt, tabs):
    L = proj0.shape[0]
    qkv = proj0[:, 2 * D_MODEL:]
    Wh = B_HEADS * HEAD_DIM
    branches = []
    for window, d in DILATED_PATTERN:
        view = qkv.reshape(L // d, d * 3 * Wh)
        v1 = lambda t: t.reshape(L // d, d * Wh)
        outs = _band_bwd(view, view, view, v1(datt), v1(att), v1(lse), d=d, nq=B_HEADS, nkv=B_HEADS,
                         qcol=lambda r: 3 * r, kcol=lambda r: 3 * r + 1, vcol=lambda r: 3 * r + 2, docol=lambda r: r,
                         max_dist=window // d, name=f"dilated_bwd_d{d}")
        branches.append([o.reshape(L, Wh) for o in outs])
    return _attn_grad_combine(branches, tabs, name="dilated_grad_combine")


def _device_step(x, mem, target, w):
    L = x.shape[0]
    tabs = _rope_tables(L)
    g = {}
    saved = []
    h = x
    for layer in range(2):
        sv = {"h_mix": h}
        if layer == 0:
            proj, n = _rowmm(h, w["ab_w_in"], name="l0_in_proj", gain=w["mix_norm"][0],
                             rope=(2 * D_MODEL, 2 * D_MODEL + 2 * B_HEADS * HEAD_DIM, tabs))
            rec, hs = _lru_fwd(proj, w["lru_conv_w"], w["lru_conv_b"], w["lru_wa"], w["lru_ba"], w["lru_wx"], w["lru_bx"],
                               w["lru_lambda"], name="lru_fwd")
            att, lse = _dilated_fwd(proj)
            mix = jnp.concatenate([rec, att], axis=1)
            (h,) = _rowmm(mix, w["ab_w_out"], name="l0_out_proj", res=h)
            sv.update(hs=hs)
        else:
            proj, n = _rowmm(h, w["c_w_qkv"], name="l1_qkv_proj", gain=w["mix_norm"][1], bias=w["c_b_qkv"],
                             rope=(0, (C_HEADS + C_KV_HEADS) * HEAD_DIM, tabs))
            mix, lse = _band_fwd(proj, proj, proj, d=1, nq=C_HEADS, nkv=C_KV_HEADS, qcol=lambda r: 0, kcol=lambda r: 8,
                                 vcol=lambda r: 9, max_dist=C_WINDOW - 1, sinks=w["c_sinks"], name="swa_fwd")
            (h,) = _rowmm(mix, w["c_w_out"], name="l1_out_proj", res=h, bias=w["c_b_out"])
        sv.update(proj=proj, n_mix=n, mix=mix, lse=lse, h_xa=h)
        xq, nx = _rowmm(h, w["xa_wq"][layer][None], name=f"xa_q_proj{layer}", gain=w["xa_norm"][layer])
        kv, nm = _rowmm(mem, w["xa_wkv"][layer][None], name=f"xa_kv_proj{layer}", gain=w["xa_mem_norm"][layer])
        xo, xlse = _xattn_fwd(xq, kv, name=f"xa_fwd{layer}")
        (h,) = _rowmm(xo, w["xa_wo"][layer], name=f"xa_out_proj{layer}", res=h)
        sv.update(xq=xq, nx=nx, kv=kv, nm=nm, xo=xo, xlse=xlse, h_ffn=h)
        gu, nf, act = _rowmm(h, w["ffn_w_gate_up"], layer=layer, name=f"ffn_in{layer}", gain=w["ffn_norm"][layer], swiglu=True)
        (h,) = _rowmm(act, w["ffn_w_down"][layer][None], name=f"ffn_out{layer}", res=h, tm=512)
        sv.update(gu=gu, nf=nf, act=act)
        saved.append(sv)

    dh, g["final_norm"], loss = _final_loss(h, w["final_norm"], target, name="final_loss")

    stk = {k: [None, None] for k in ("xa_norm", "xa_mem_norm", "ffn_norm", "mix_norm", "xa_wq", "xa_wkv", "xa_wo",
                                      "ffn_w_gate_up", "ffn_w_down")}
    for layer in (1, 0):
        sv = saved[layer]
        (stk["ffn_w_down"][layer],) = _mm_tn(sv["act"], dh, S=1, name=f"ffn_down_dw{layer}", kk=D_FF // 2)
        (dgu,) = _mm_nt(dh, w["ffn_w_down"][layer][None], name=f"ffn_dact{layer}", mode="swiglu", kchunk=D_FF // 2, gu=sv["gu"])
        (stk["ffn_w_gate_up"][layer],) = _mm_tn(sv["nf"], dgu, S=N_CHIPS, name=f"ffn_gu_dw{layer}")
        dh, stk["ffn_norm"][layer] = _mm_nt(dgu, w["ffn_w_gate_up"], layer=layer, name=f"ffn_dx{layer}", mode="norm",
                                            h=sv["h_ffn"], gain=w["ffn_norm"][layer], dh=dh)
        (stk["xa_wo"][layer],) = _mm_tn(sv["xo"], dh, S=N_CHIPS, name=f"xa_wo_dw{layer}")
        (dxo,) = _mm_nt(dh, w["xa_wo"][layer], name=f"xa_dxo{layer}", mode="plain")
        dxq, dkv = _xattn_bwd(sv["xq"], sv["kv"], sv["xo"], sv["xlse"], dxo, name=f"xa_bwd{layer}")
        (stk["xa_wq"][layer],) = _mm_tn(sv["nx"], dxq, S=1, name=f"xa_wq_dw{layer}")
        dh, stk["xa_norm"][layer] = _mm_nt(dxq, w["xa_wq"][layer][None], name=f"xa_dx{layer}", mode="norm", h=sv["h_xa"],
                                           gain=w["xa_norm"][layer], dh=dh)
        (stk["xa_wkv"][layer],) = _mm_tn(sv["nm"], dkv, S=1, name=f"xa_wkv_dw{layer}")
        _, stk["xa_mem_norm"][layer] = _mm_nt(dkv, w["xa_wkv"][layer][None], name=f"xa_dmem{layer}", mode="norm", h=mem,
                                              gain=w["xa_mem_norm"][layer])
        if layer == 1:
            g["c_w_out"], g["c_b_out"] = _mm_tn(sv["mix"], dh, S=1, name="l1_out_dw", bias=True)
            (dmix,) = _mm_nt(dh, w["c_w_out"], name="l1_dmix", mode="plain")
            dq, dk, dv, dsk = _band_bwd(sv["proj"], sv["proj"], sv["proj"], dmix, sv["mix"], sv["lse"], d=1, nq=C_HEADS,
                                        nkv=C_KV_HEADS, qcol=lambda r: 0, kcol=lambda r: 8, vcol=lambda r: 9,
                                        docol=lambda r: 0, max_dist=C_WINDOW - 1, sinks=w["c_sinks"], name="swa_bwd")
            g["c_sinks"] = dsk[0, :C_HEADS]
            dproj = _attn_grad_combine([(dq, dk, dv)], tabs, name="swa_grad_combine")
            g["c_w_qkv"], g["c_b_qkv"] = _mm_tn(sv["n_mix"], dproj, S=1, name="l1_qkv_dw", bias=True)
            dh, stk["mix_norm"][1] = _mm_nt(dproj, w["c_w_qkv"], name="l1_dx", mode="norm", h=sv["h_mix"],
                                            gain=w["mix_norm"][1], dh=dh)
        else:
            (g["ab_w_out"],) = _mm_tn(sv["mix"], dh, S=1, name="l0_out_dw", kk=768)
            (dmix,) = _mm_nt(dh, w["ab_w_out"], name="l0_dmix", mode="plain", kchunk=768)
            (dxy, g["lru_conv_w"], g["lru_conv_b"], g["lru_wa"], g["lru_ba"], g["lru_wx"], g["lru_bx"],
             g["lru_lambda"]) = _lru_bwd(sv["proj"], sv["hs"], dmix, w["lru_conv_w"], w["lru_conv_b"], w["lru_wa"],
                                         w["lru_ba"], w["lru_wx"], w["lru_bx"], w["lru_lambda"], name="lru_bwd")
            dqkv = _dilated_bwd(sv["proj"], sv["mix"][:, D_MODEL:], sv["lse"], dmix[:, D_MODEL:], tabs)
            dproj = jnp.concatenate([dxy, dqkv], axis=1)
            (g["ab_w_in"],) = _mm_tn(sv["n_mix"], dproj, S=N_CHIPS, name="l0_in_dw")
            dh, stk["mix_norm"][0] = _mm_nt(dproj, w["ab_w_in"], name="l0_dx", mode="norm", h=sv["h_mix"],
                                            gain=w["mix_norm"][0], dh=dh)
    for k, v in stk.items():
        if k == "ffn_w_gate_up":
            g[k] = jnp.stack(v, axis=1)
        else:
            g[k] = jnp.concatenate(v, axis=0) if v[0].shape[0] == 1 else jnp.stack(v, axis=0)
    return loss[0, 0], dh, g


ANY = pl.BlockSpec(memory_space=pl.ANY)
MESH = pl.DeviceIdType.MESH


def _place():
    x, y, c = lax.axis_index("x"), lax.axis_index("y"), lax.axis_index("c")
    return x, y, c, [(1 - x, y), (x, 1 - y), (1 - x, 1 - y)]


def _remote(send_sems, recv_sems):
    def copy(k, src, dst, to):
        return pltpu.make_async_remote_copy(src_ref=src, dst_ref=dst, send_sem=send_sems.at[k], recv_sem=recv_sems.at[k],
                                            device_id=to, device_id_type=MESH)
    return copy


def _halves(ref, n_rows):
    rh = n_rows // 2
    return lambda lead, hh: ref.at[(*lead, pl.ds(hh * rh, rh), slice(None))]


def _gather_weights(packs, spack):
    n = len(packs)

    def body(*refs):
        w_refs, s_ref, wf_refs, sf_ref = refs[:n], refs[n], refs[n + 1:2 * n + 1], refs[2 * n + 1]
        x, y, c, chips = _place()
        me, sib = 2 * x + y, (x, y, 1 - c)
        copy = _remote(*refs[-2:])
        src = [_halves(w_refs[g], packs[g].shape[0]) for g in range(n)]
        dst = [_halves(wf_refs[g], packs[g].shape[0]) for g in range(n)]
        sends = []
        for g in range(n):
            for j, (cx, cy) in enumerate(chips):
                sends.append(copy(3 * g + j, src[g]((), c), dst[g]((me,), c), (cx, cy, c)))
        for j, (cx, cy) in enumerate(chips):
            sends.append(copy(6 * n + j, s_ref, sf_ref.at[me], (cx, cy, c)))
        for cp in sends:
            cp.start()
        for g in range(n):
            for j, (cx, cy) in enumerate(chips):
                got = dst[g]((2 * cx + cy,), c)
                copy(3 * g + j, got, got, sib).wait_recv()
                fwd = copy(3 * n + 3 * g + j, got, got, sib)
                fwd.start()
                sends.append(fwd)
        for g in range(n):
            for j, (cx, cy) in enumerate(chips):
                got = dst[g]((2 * cx + cy,), 1 - c)
                copy(3 * n + 3 * g + j, got, got, sib).wait_recv()
        for j, (cx, cy) in enumerate(chips):
            copy(6 * n + j, s_ref, sf_ref.at[2 * cx + cy], sib).wait_recv()
        for cp in sends:
            cp.wait_send()

    ins = list(packs) + [spack]
    out_shape = [jax.ShapeDtypeStruct((N_CHIPS,) + a.shape, a.dtype) for a in ins]
    n_sems = 6 * n + 3
    outs = pl.pallas_call(body, name="gather_weights", out_shape=out_shape, in_specs=[ANY] * len(ins),
                          out_specs=[ANY] * len(ins),
                          scratch_shapes=[pltpu.SemaphoreType.DMA((n_sems,)), pltpu.SemaphoreType.DMA((n_sems,))])(*ins)
    chip = 2 * lax.axis_index("x") + lax.axis_index("y")
    outs = [lax.dynamic_update_index_in_dim(o, a, chip, 0) for o, a in zip(outs, ins)]
    return outs[:n], outs[n]


SEQUENCER_GATHER_ID = 1


def _gather_weights_behind(packs):
    n = len(packs)

    def body(*refs):
        w_refs, wf_refs = refs[:n], refs[n:2 * n]
        x, y, c, chips = _place()
        me, sib = 2 * x + y, (x, y, 1 - c)
        barrier = pltpu.get_barrier_semaphore()
        for peer in [(cx, cy, c) for cx, cy in chips] + [sib]:
            pl.semaphore_signal(barrier, inc=1, device_id=peer, device_id_type=MESH)
        pl.semaphore_wait(barrier, len(chips) + 1)
        copy = _remote(*refs[-2:])
        src = [_halves(w_refs[g], packs[g].shape[0]) for g in range(n)]
        dst = [_halves(wf_refs[g], packs[g].shape[0]) for g in range(n)]
        sends = []
        for g in range(n):
            for j, (cx, cy) in enumerate(chips):
                sends.append(copy(3 * g + j, src[g]((), c), dst[g]((me,), c), (cx, cy, c)))
        for cp in sends:
            cp.start()
        for g in range(n):
            for j, (cx, cy) in enumerate(chips):
                got = dst[g]((2 * cx + cy,), c)
                copy(3 * g + j, got, got, sib).wait_recv()
                fwd = copy(3 * n + 3 * g + j, got, got, sib)
                fwd.start()
                sends.append(fwd)
        for g in range(n):
            for j, (cx, cy) in enumerate(chips):
                got = dst[g]((2 * cx + cy,), 1 - c)
                copy(3 * n + 3 * g + j, got, got, sib).wait_recv()
        for cp in sends:
            cp.wait_send()

    out_type = [jax.ShapeDtypeStruct((N_CHIPS,) + a.shape, a.dtype) for a in packs]
    outs = pl.kernel(body, out_type=out_type, mesh=plsc.ScalarSubcoreMesh(axis_name="sequencer", num_cores=1),
                     name="gather_weights_behind",
                     scratch_types=[pltpu.SemaphoreType.DMA((6 * n,)), pltpu.SemaphoreType.DMA((6 * n,))],
                     compiler_params=pltpu.CompilerParams(collective_id=SEQUENCER_GATHER_ID))(*packs)
    chip = 2 * lax.axis_index("x") + lax.axis_index("y")
    return [lax.dynamic_update_index_in_dim(o, a, chip, 0) for o, a in zip(outs, packs)]


def _rs_pair_exchange(gpacks):
    n = len(gpacks)

    def body(*refs):
        g_refs, ra_refs = refs[:n], refs[n:2 * n]
        x, y, c, _ = _place()
        copy = _remote(*refs[-2:])
        cps = []
        for g in range(n):
            half = _halves(g_refs[g], gpacks[g].shape[1])
            cps += [copy(N_CHIPS * g + j, half((j,), 1 - c), ra_refs[g].at[j], (x, y, 1 - c)) for j in range(N_CHIPS)]
        for cp in cps:
            cp.start()
        for cp in cps:
            cp.wait()

    out_shape = [jax.ShapeDtypeStruct((N_CHIPS, a.shape[1] // 2, a.shape[2]), a.dtype) for a in gpacks]
    n_sems = N_CHIPS * n
    return pl.pallas_call(body, name="rs_pair_exchange", out_shape=out_shape, in_specs=[ANY] * n, out_specs=[ANY] * n,
                          scratch_shapes=[pltpu.SemaphoreType.DMA((n_sems,)), pltpu.SemaphoreType.DMA((n_sems,))])(*gpacks)


def _row_tile(rows, cap=512):
    return max(t for t in range(16, min(rows, cap) + 1, 16) if rows % t == 0)


def _rs_pair_add(place, gpack, ra, *, name):
    _, R, C = gpack.shape
    Rh = R // 2
    tr = _row_tile(Rh)
    nrb = Rh // tr

    def body(p_ref, g_ref, ra_ref, pair_ref, own_ref):
        s = g_ref[...].astype(F32) + ra_ref[...].astype(F32)
        pair_ref[...] = s.astype(BF16)

        @pl.when(pl.program_id(1) == p_ref[1])
        def _():
            own_ref[...] = s

    grid_spec = pltpu.PrefetchScalarGridSpec(
        num_scalar_prefetch=1, grid=(nrb, N_CHIPS),
        in_specs=[pl.BlockSpec((None, tr, C), lambda i, j, p: (j, p[0] * nrb + i, 0)),
                  pl.BlockSpec((None, tr, C), lambda i, j, p: (j, i, 0))],
        out_specs=[pl.BlockSpec((None, tr, C), lambda i, j, p: (j, i, 0)), pl.BlockSpec((tr, C), lambda i, j, p: (i, 0))])
    return pl.pallas_call(
        body, name=name, grid_spec=grid_spec,
        out_shape=[jax.ShapeDtypeStruct((N_CHIPS, Rh, C), BF16), jax.ShapeDtypeStruct((Rh, C), F32)],
        compiler_params=pltpu.CompilerParams(dimension_semantics=("arbitrary", "arbitrary"),
                                             vmem_limit_bytes=VMEM_LIMIT_V7X))(place, gpack, ra)


def _rs_chip_exchange(pairs, small):
    n = len(pairs)

    def body(*refs):
        p_refs, s_ref, rb_refs, rs_ref = refs[:n], refs[n], refs[n + 1:2 * n + 1], refs[2 * n + 1]
        local_sem = refs[-1]
        x, y, c, chips = _place()
        copy = _remote(*refs[-3:-1])
        dev = 4 * x + 2 * y + c
        mine = pltpu.make_async_copy(s_ref, rs_ref.at[dev], local_sem.at[0])
        mine.start()
        cps = []
        for g in range(n):
            cps += [copy(3 * g + j, p_refs[g].at[2 * cx + cy], rb_refs[g].at[j], (cx, cy, c)) for j, (cx, cy) in enumerate(chips)]
        peers = []
        for k in range(1, 8):
            px = 1 - x if k & 4 else x
            py = 1 - y if k & 2 else y
            pc = 1 - c if k & 1 else c
            peers.append((px, py, pc))
            cps.append(copy(3 * n - 1 + k, s_ref, rs_ref.at[dev], (px, py, pc)))
        for cp in cps:
            cp.start()
        for g in range(n):
            for j in range(3):
                copy(3 * g + j, p_refs[g].at[0], rb_refs[g].at[j], (x, y, c)).wait_recv()
        for k, (px, py, pc) in enumerate(peers, start=1):
            copy(3 * n - 1 + k, s_ref, rs_ref.at[4 * px + 2 * py + pc], (x, y, c)).wait_recv()
        for cp in cps:
            cp.wait_send()
        mine.wait()

    out_shape = [jax.ShapeDtypeStruct((3,) + p.shape[1:], p.dtype) for p in pairs]
    out_shape.append(jax.ShapeDtypeStruct((8,) + small.shape, small.dtype))
    n_sems = 3 * n + 7
    return pl.pallas_call(body, name="rs_chip_exchange", out_shape=out_shape, in_specs=[ANY] * (n + 1),
                          out_specs=[ANY] * (n + 1),
                          scratch_shapes=[pltpu.SemaphoreType.DMA((n_sems,)), pltpu.SemaphoreType.DMA((n_sems,)),
                                          pltpu.SemaphoreType.DMA((1,))])(*pairs, small)


def _rs_final_add(place, own, rb, *, name):
    Rh, C = own.shape
    tr = _row_tile(Rh)
    nrb = Rh // tr

    def body(p_ref, o_ref, rb_ref, f_ref):
        f_ref[...] = ((o_ref[...] + rb_ref[0].astype(F32)) + rb_ref[1].astype(F32)) + rb_ref[2].astype(F32)

    grid_spec = pltpu.PrefetchScalarGridSpec(
        num_scalar_prefetch=1, grid=(nrb,),
        in_specs=[pl.BlockSpec((tr, C), lambda i, p: (i, 0)), pl.BlockSpec((3, tr, C), lambda i, p: (0, i, 0))],
        out_specs=pl.BlockSpec((tr, C), lambda i, p: (p[0] * nrb + i, 0)))
    return pl.pallas_call(
        body, name=name, grid_spec=grid_spec, out_shape=jax.ShapeDtypeStruct((2 * Rh, C), F32),
        compiler_params=pltpu.CompilerParams(dimension_semantics=("arbitrary",), vmem_limit_bytes=VMEM_LIMIT_V7X))(place, own, rb)


def _sum_slots(rs):
    n, rows, C = rs.shape

    def body(r_ref, o_ref):
        acc = r_ref[0]
        for k in range(1, n):
            acc = acc + r_ref[k]
        o_ref[...] = acc

    return _call(body, name="small_grad_sum", grid=(1,), in_specs=[pl.BlockSpec((n, rows, C), lambda i: (0, 0, 0))],
                 out_specs=pl.BlockSpec((rows, C), lambda i: (0, 0)), out_shape=jax.ShapeDtypeStruct((rows, C), F32),
                 sem=("arbitrary",))(rs)


def _rs_sibling_share(gbufs):
    n = len(gbufs)

    def body(*refs):
        g_refs = refs[n:2 * n]
        x, y, c, _ = _place()
        copy = _remote(*refs[-2:])
        halves = [_halves(g_refs[g], gbufs[g].shape[0]) for g in range(n)]
        outs = [copy(g, halves[g]((), c), halves[g]((), c), (x, y, 1 - c)) for g in range(n)]
        for cp in outs:
            cp.start()
        for g in range(n):
            copy(g, halves[g]((), 1 - c), halves[g]((), 1 - c), (x, y, c)).wait_recv()
        for cp in outs:
            cp.wait_send()

    return pl.pallas_call(body, name="rs_sibling_share", out_shape=[jax.ShapeDtypeStruct(a.shape, a.dtype) for a in gbufs],
                          in_specs=[ANY] * n, out_specs=[ANY] * n, input_output_aliases={g: g for g in range(n)},
                          scratch_shapes=[pltpu.SemaphoreType.DMA((n,)), pltpu.SemaphoreType.DMA((n,))])(*gbufs)


def _adamw(w, g, m, v, *, name, g_row=0):
    rows, cols = w.shape
    tr = rows
    for cand in range(min(rows, 512), 7, -8):
        if rows % cand == 0 and g_row % cand == 0:
            tr = cand
            break
    spec = pl.BlockSpec((tr, cols), lambda i: (i, 0))
    g_spec = pl.BlockSpec((tr, cols), lambda i: (g_row // tr + i, 0))

    def body(w_ref, g_ref, m_ref, v_ref, d_ref, nm_ref, nv_ref):
        gg = g_ref[...]
        nm = ADAM_B1 * m_ref[...] + (1.0 - ADAM_B1) * gg
        nv = ADAM_B2 * v_ref[...] + (1.0 - ADAM_B2) * (gg * gg)
        m_hat = nm / (1.0 - ADAM_B1 ** ADAM_STEP)
        v_hat = nv / (1.0 - ADAM_B2 ** ADAM_STEP)
        d_ref[...] = -ADAM_LR * (m_hat / (jnp.sqrt(v_hat) + ADAM_EPS) + ADAM_WD * w_ref[...])
        nm_ref[...] = nm
        nv_ref[...] = nv

    return _call(body, name=name, grid=(rows // tr,), in_specs=[spec, g_spec, spec, spec], out_specs=[spec] * 3,
                 out_shape=[jax.ShapeDtypeStruct((rows, cols), F32)] * 3, sem=("parallel",))(w, g, m, v)


WEIGHT_NAMES = ("mix_norm", "ab_w_in", "lru_conv_w", "lru_conv_b", "lru_wa", "lru_ba", "lru_wx", "lru_bx", "lru_lambda",
                "ab_w_out", "c_w_qkv", "c_b_qkv", "c_sinks", "c_w_out", "c_b_out", "xa_norm", "xa_mem_norm", "xa_wq",
                "xa_wkv", "xa_wo", "ffn_norm", "ffn_w_gate_up", "ffn_w_down", "final_norm")
EARLY_GROUPS = (("ab_w_in",), ("ab_w_out",), ("lru_wa", "lru_wx"))
LATE_GROUPS = (("c_w_out", "xa_wkv", "ffn_w_down"), ("ffn_w_gate_up",), ("xa_wo",), ("xa_wq",), ("c_w_qkv",))
GROUPS = EARLY_GROUPS + LATE_GROUPS
REPLICATED = ("mix_norm", "lru_conv_b", "lru_lambda", "c_sinks", "xa_norm", "xa_mem_norm", "ffn_norm", "final_norm")
SMALL_SHARDED = ("lru_conv_w", "lru_ba", "lru_bx", "c_b_qkv", "c_b_out")
LANES = 1024


def _rows(v):
    flat = v.reshape(-1)
    return jnp.pad(flat, (0, -flat.shape[0] % LANES)).reshape(-1, LANES)


def _pack_small(parts, total, *, name):
    def body(*refs):
        o_ref = refs[-1]
        o_ref[...] = jnp.zeros_like(o_ref)
        row = 0
        for p_ref in refs[:-1]:
            o_ref[row:row + p_ref.shape[0], :] = p_ref[...]
            row += p_ref.shape[0]

    return _call(body, name=name, grid=(1,), in_specs=[pl.BlockSpec(p.shape, lambda i: (0, 0)) for p in parts],
                 out_specs=pl.BlockSpec((total, LANES), lambda i: (0, 0)),
                 out_shape=jax.ShapeDtypeStruct((total, LANES), F32), sem=("arbitrary",))(*parts)


def _from_shards(name, t):
    minor = t.shape[-1]
    if name == "ab_w_in":
        return t
    if name in ("ab_w_out", "c_w_out"):
        return t.reshape(1, -1, minor)
    if name == "ffn_w_gate_up":
        return t.reshape(N_CHIPS, 2, -1, minor)
    if name in ("xa_wq", "xa_wkv", "ffn_w_down"):
        return t.reshape(N_CHIPS, 2, -1, minor).transpose(1, 0, 2, 3).reshape(2, -1, minor)
    if name in ("lru_wa", "lru_wx"):
        return t.reshape(N_CHIPS, LRU_HEADS, -1, minor).transpose(1, 0, 2, 3).reshape(LRU_HEADS, LRU_HEAD_DIM, minor)
    if name == "xa_wo":
        return t.reshape(N_CHIPS, 2, -1, minor).transpose(1, 0, 2, 3)
    assert name == "c_w_qkv"
    return t.transpose(1, 0, 2).reshape(1, D_MODEL, -1)


def _to_shards(name, g):
    minor = g.shape[-1]
    if name == "ab_w_in":
        return g
    if name in ("ab_w_out", "c_w_out"):
        return g.reshape(N_CHIPS, -1, minor)
    if name == "ffn_w_gate_up":
        return g.reshape(N_CHIPS, -1, minor)
    if name in ("xa_wq", "xa_wkv", "ffn_w_down"):
        return g.reshape(2, N_CHIPS, -1, minor).transpose(1, 0, 2, 3).reshape(N_CHIPS, -1, minor)
    if name in ("lru_wa", "lru_wx"):
        return g.reshape(LRU_HEADS, N_CHIPS, -1, minor).transpose(1, 0, 2, 3).reshape(N_CHIPS, -1, minor)
    if name == "xa_wo":
        return g.transpose(1, 0, 2, 3).reshape(N_CHIPS, -1, minor)
    assert name == "c_w_qkv"
    return g.reshape(D_MODEL, N_CHIPS, -1).transpose(1, 0, 2)


def kernel(x, mem, mix_norm, ab_w_in, lru_conv_w, lru_conv_b, lru_wa, lru_ba, lru_wx, lru_bx, lru_lambda, ab_w_out, c_w_qkv, c_b_qkv, c_sinks, c_w_out, c_b_out, xa_norm, xa_mem_norm, xa_wq, xa_wkv, xa_wo, ffn_norm, ffn_w_gate_up, ffn_w_down, final_norm, loss_target, m_mix_norm, m_ab_w_in, m_lru_conv_w, m_lru_conv_b, m_lru_wa, m_lru_ba, m_lru_wx, m_lru_bx, m_lru_lambda, m_ab_w_out, m_c_w_qkv, m_c_b_qkv, m_c_sinks, m_c_w_out, m_c_b_out, m_xa_norm, m_xa_mem_norm, m_xa_wq, m_xa_wkv, m_xa_wo, m_ffn_norm, m_ffn_w_gate_up, m_ffn_w_down, m_final_norm, v_mix_norm, v_ab_w_in, v_lru_conv_w, v_lru_conv_b, v_lru_wa, v_lru_ba, v_lru_wx, v_lru_bx, v_lru_lambda, v_ab_w_out, v_c_w_qkv, v_c_b_qkv, v_c_sinks, v_c_w_out, v_c_b_out, v_xa_norm, v_xa_mem_norm, v_xa_wq, v_xa_wkv, v_xa_wo, v_ffn_norm, v_ffn_w_gate_up, v_ffn_w_down, v_final_norm):
    given = dict(locals())
    wl = {n: given[n] for n in WEIGHT_NAMES}
    ml = {n: given["m_" + n] for n in WEIGHT_NAMES}
    vl = {n: given["v_" + n] for n in WEIGHT_NAMES}
    xi, yi, ci = lax.axis_index("x"), lax.axis_index("y"), lax.axis_index("c")
    chip = 2 * xi + yi

    def join(parts, axis):
        return parts[0] if len(parts) == 1 else jnp.concatenate(parts, axis=axis)

    local_rows = {n: wl[n].size // wl[n].shape[-1] for grp in GROUPS for n in grp}
    packs = [join([wl[n].astype(BF16).reshape(local_rows[n], wl[n].shape[-1]) for n in grp], 0) for grp in GROUPS]
    spack = _pack_small([_rows(wl[n]) for n in SMALL_SHARDED], 8, name="pack_small_weights")
    n_early = len(EARLY_GROUPS)
    late = _gather_weights_behind(packs[n_early:])
    early, sfull = _gather_weights(packs[:n_early], spack)
    gathered = early + late
    w = {n: wl[n] for n in REPLICATED}
    w["c_sinks"] = wl["c_sinks"][0]
    for grp, full in zip(GROUPS, gathered):
        off = 0
        for n in grp:
            w[n] = _from_shards(n, full if len(grp) == 1 else full[:, off:off + local_rows[n]])
            off += local_rows[n]
    for r, n in enumerate(SMALL_SHARDED):
        loc = wl[n].shape[1:]
        t = sfull[:, r, :wl[n].size].reshape((N_CHIPS,) + loc)
        if n == "lru_conv_w":
            w[n] = t.transpose(1, 0, 2).reshape(CONV_WIDTH, -1)
        elif n in ("lru_ba", "lru_bx"):
            w[n] = t.transpose(1, 0, 2).reshape(1, -1)
        else:
            w[n] = t.reshape(1, -1)

    loss_part, grad_x, g = _device_step(x[0], mem[0], loss_target[0], w)

    gpacks = [join([_to_shards(n, g[n].astype(BF16)) for n in grp], 1) for grp in GROUPS]
    small_parts = [_rows(g[n]) for n in REPLICATED] + [_rows(jnp.broadcast_to(loss_part, (LANES,)))]
    small_parts += [_rows(g[n]) for n in SMALL_SHARDED]
    small = _pack_small(small_parts, 24, name="pack_small_grads")
    place = jnp.stack([ci, chip]).astype(jnp.int32)
    ras = _rs_pair_exchange(gpacks)
    sums = [_rs_pair_add(place, gp, ra, name=f"rs_pair_add_{i}") for i, (gp, ra) in enumerate(zip(gpacks, ras))]
    *rbs, rs = _rs_chip_exchange([pair for pair, _ in sums], small)
    gsums = _rs_sibling_share([_rs_final_add(place, own, rb, name=f"rs_final_add_{i}")
                               for i, ((_, own), rb) in enumerate(zip(sums, rbs))])
    ssum = _sum_slots(rs)

    grads, grad_rows = {}, {}
    for grp, gsum in zip(GROUPS, gsums):
        off = 0
        for n in grp:
            grad_rows[n] = (gsum, off)
            grads[n] = (gsum if len(grp) == 1 else gsum[off:off + local_rows[n]]).reshape(wl[n].shape)
            off += local_rows[n]
    row = 0
    for n in REPLICATED:
        k = _rows(g[n]).shape[0]
        grads[n] = ssum[row:row + k].reshape(-1)[:wl[n].size].reshape(wl[n].shape)
        row += k
    loss = ssum[row, 0]
    row += 1
    for n in SMALL_SHARDED:
        k = _rows(g[n]).shape[0]
        full = ssum[row:row + k].reshape(-1)[:g[n].size]
        row += k
        loc = wl[n].shape
        if n == "lru_conv_w":
            sh = full.reshape(CONV_WIDTH, N_CHIPS, -1)
        elif n in ("lru_ba", "lru_bx"):
            sh = full.reshape(LRU_HEADS, N_CHIPS, -1)
        else:
            sh = full.reshape(1, N_CHIPS, -1)
        grads[n] = lax.dynamic_index_in_dim(sh, chip, axis=1, keepdims=False).reshape(loc)

    delta, new_m, new_v = {}, {}, {}
    for n, (gsum, off) in grad_rows.items():
        shape2 = (local_rows[n], wl[n].shape[-1])
        d, nm, nv = _adamw(wl[n].reshape(shape2), gsum, ml[n].reshape(shape2), vl[n].reshape(shape2), g_row=off,
                           name="adamw_" + n)
        delta[n], new_m[n], new_v[n] = (t.reshape(wl[n].shape) for t in (d, nm, nv))
    smalls = REPLICATED + SMALL_SHARDED
    packs = [_pack_small([_rows(src[n]) for n in smalls], 24, name="pack_adamw_" + tag)
             for tag, src in (("w", wl), ("g", grads), ("m", ml), ("v", vl))]
    outs = _adamw(*packs, name="adamw_small")
    row = 0
    for n in smalls:
        k = _rows(wl[n]).shape[0]
        for dst, o in zip((delta, new_m, new_v), outs):
            dst[n] = o[row:row + k].reshape(-1)[:wl[n].size].reshape(wl[n].shape)
        row += k

    return (loss, grad_x[None], *[grads[n] for n in WEIGHT_NAMES], *[delta[n] for n in WEIGHT_NAMES],
            *[new_m[n] for n in WEIGHT_NAMES], *[new_v[n] for n in WEIGHT_NAMES])
```

```python
import jax
import jax.numpy as jnp
from jax import lax
from jax.experimental import pallas as pl
from jax.experimental.pallas import tpu as pltpu
from jax.experimental.pallas import tpu_sc as plsc

F32, BF16 = jnp.float32, jnp.bfloat16
D_MODEL = 1024
NORM_EPS = 1e-6
ROPE_THETA = 500000.0
HEAD_DIM = 64
ROT_DIM = 16
BLK = 128
LRU_HEADS, LRU_HEAD_DIM, CONV_WIDTH, LRU_C = 4, 256, 4, 8.0
DILATED_PATTERN = ((128, 1), (512, 4), (2048, 16))
B_HEADS, C_HEADS, C_KV_HEADS, C_WINDOW = 8, 16, 2, 128
XA_HEADS, XA_HEAD_DIM, N_MEM = 4, 128, 256
D_FF = 2816
NEG = -1e30
ADAM_LR, ADAM_B1, ADAM_B2, ADAM_EPS, ADAM_WD, ADAM_STEP = 0.001, 0.9, 0.999, 1e-08, 0.01, 10
N_CHIPS = 4
VMEM_LIMIT_V7X = 56 * 1024 * 1024

NN = (((1,), (0,)), ((), ()))
NT = (((1,), (1,)), ((), ()))
TN = (((0,), (0,)), ((), ()))


def _dot(a, b, dims=NN):
    return lax.dot_general(a, b, dims, preferred_element_type=F32)


def _sigmoid(x):
    return 1.0 / (1.0 + jnp.exp(-x))


def _call(body, *, name, grid, in_specs, out_specs, out_shape, scratch=(), sem=None):
    return pl.pallas_call(
        body, name=name, grid=grid, in_specs=in_specs, out_specs=out_specs, out_shape=out_shape,
        scratch_shapes=list(scratch),
        compiler_params=pltpu.CompilerParams(dimension_semantics=sem, vmem_limit_bytes=VMEM_LIMIT_V7X))


def _rope_tables(L):
    half = ROT_DIM // 2
    inv = ROPE_THETA ** (-jnp.arange(0, ROT_DIM, 2, dtype=F32) / ROT_DIM)
    ang = jnp.arange(L, dtype=F32)[:, None] * inv[None, :]
    cos, sin = jnp.cos(ang), jnp.sin(ang)
    rest = HEAD_DIM - ROT_DIM
    z8, zr, one = jnp.zeros((L, half), F32), jnp.zeros((L, rest), F32), jnp.ones((L, rest), F32)
    c = jnp.concatenate([cos, cos, one], axis=1)
    s1 = jnp.concatenate([-sin, z8, zr], axis=1)
    s2 = jnp.concatenate([z8, sin, zr], axis=1)
    return tuple(jnp.concatenate([t, t], axis=1) for t in (c, s1, s2))


def _rope_fwd(v, c, s1, s2):
    return v * c + pltpu.roll(v, 120, 1) * s1 + pltpu.roll(v, 8, 1) * s2


def _rope_bwd(dv, c, s1, s2):
    return dv * c + pltpu.roll(dv * s1, 8, 1) + pltpu.roll(dv * s2, 120, 1)


def _weight_spec(w, layer):
    if layer is None:
        return w.shape, pl.BlockSpec(w.shape, lambda i: (0, 0, 0))
    S, _, K, Ns = w.shape
    return (S, K, Ns), pl.BlockSpec((S, None, K, Ns), lambda i: (0, layer, 0, 0))


def _rowmm(a, w3, *, name, tm=256, gain=None, bias=None, res=None, swiglu=False, rope=None, layer=None):
    M, K = a.shape
    (S, _, Ns), w_spec = _weight_spec(w3, layer)
    N = S * Ns
    tm = min(tm, M)
    has_norm, has_bias, has_res, has_rope = gain is not None, bias is not None, res is not None, rope is not None
    row = lambda w: pl.BlockSpec((tm, w), lambda i: (i, 0))
    whole = lambda shape: pl.BlockSpec(shape, lambda i: (0,) * len(shape))
    ins, specs = [a], [row(K)]
    if has_norm:
        ins.append(gain.reshape(1, K)); specs.append(whole((1, K)))
    ins.append(w3); specs.append(w_spec)
    if has_bias:
        ins.append(bias.reshape(1, N)); specs.append(whole((1, N)))
    if has_res:
        ins.append(res); specs.append(row(N))
    if has_rope:
        ins += list(rope[2]); specs += [row(128)] * 3
    y_dtype = F32 if has_res else BF16
    out_shape, out_specs = [jax.ShapeDtypeStruct((M, N), y_dtype)], [row(N)]
    if has_norm:
        out_shape.append(jax.ShapeDtypeStruct((M, K), BF16)); out_specs.append(row(K))
    if swiglu:
        out_shape.append(jax.ShapeDtypeStruct((M, N // 2), BF16)); out_specs.append(row(N // 2))
    scratch = [pltpu.VMEM((tm, N), F32)] if has_rope else []

    def body(*refs):
        it = iter(refs)
        a_ref = next(it)
        g_ref = next(it) if has_norm else None
        w_ref = next(it)
        b_ref = next(it) if has_bias else None
        r_ref = next(it) if has_res else None
        tabs = [next(it) for _ in range(3)] if has_rope else None
        y_ref = next(it)
        n_ref = next(it) if has_norm else None
        act_ref = next(it) if swiglu else None
        ys_ref = next(it) if has_rope else None
        if has_norm:
            x = a_ref[...].astype(F32)
            ms = jnp.mean(x * x, axis=-1, keepdims=True)
            xb = (x * lax.rsqrt(ms + NORM_EPS) * g_ref[...]).astype(BF16)
            n_ref[...] = xb
        else:
            xb = a_ref[...].astype(BF16)
        if swiglu:
            for s in range(S // 2):
                g = _dot(xb, w_ref[s])
                u = _dot(xb, w_ref[s + S // 2])
                y_ref[:, s * Ns:(s + 1) * Ns] = g.astype(BF16)
                y_ref[:, N // 2 + s * Ns:N // 2 + (s + 1) * Ns] = u.astype(BF16)
                act_ref[:, s * Ns:(s + 1) * Ns] = (g * _sigmoid(g) * u).astype(BF16)
            return
        for s in range(S):
            sl = slice(s * Ns, (s + 1) * Ns)
            acc = _dot(xb, w_ref[s])
            if has_bias:
                acc = acc + b_ref[:, sl]
            if has_res:
                acc = acc + r_ref[:, sl]
            if has_rope:
                ys_ref[:, sl] = acc
            else:
                y_ref[:, sl] = acc.astype(y_dtype)
        if has_rope:
            c, s1, s2 = (t[...] for t in tabs)
            for cb in range(N // 128):
                cs = slice(cb * 128, (cb + 1) * 128)
                v = ys_ref[:, cs]
                if rope[0] <= cb * 128 < rope[1]:
                    v = _rope_fwd(v, c, s1, s2)
                y_ref[:, cs] = v.astype(BF16)

    return _call(body, name=name, grid=(M // tm,), in_specs=specs, out_specs=out_specs, out_shape=out_shape,
                 scratch=scratch, sem=("parallel",))(*ins)


def _mm_nt(dy, w3, *, name, mode, tm=256, kchunk=None, h=None, gain=None, dh=None, gu=None, layer=None):
    M, N = dy.shape
    (S, K, Ns), w_spec = _weight_spec(w3, layer)
    kchunk = kchunk or K
    tm = min(tm, M)
    row = lambda w: pl.BlockSpec((tm, w), lambda i: (i, 0))
    whole = lambda shape: pl.BlockSpec(shape, lambda i: (0,) * len(shape))
    ins, specs = [dy, w3], [row(N), w_spec]
    has_dh = dh is not None
    if mode == "norm":
        ins += [h, gain.reshape(1, K)]; specs += [row(K), whole((1, K))]
        if has_dh:
            ins.append(dh); specs.append(row(K))
        out_shape = [jax.ShapeDtypeStruct((M, K), F32), jax.ShapeDtypeStruct((1, K), F32)]
        out_specs = [row(K), whole((1, K))]
    elif mode == "swiglu":
        ins.append(gu); specs.append(row(2 * K))
        out_shape, out_specs = [jax.ShapeDtypeStruct((M, 2 * K), BF16)], [row(2 * K)]
    else:
        out_shape, out_specs = [jax.ShapeDtypeStruct((M, K), BF16)], [row(K)]

    def body(*refs):
        it = iter(refs)
        dy_ref, w_ref = next(it), next(it)
        if mode == "norm":
            h_ref, g_ref = next(it), next(it)
            dh_ref = next(it) if has_dh else None
            o_ref, dg_ref = next(it), next(it)
        elif mode == "swiglu":
            gu_ref, o_ref = next(it), next(it)
        else:
            o_ref = next(it)
        for kc in range(K // kchunk):
            ks = slice(kc * kchunk, (kc + 1) * kchunk)
            acc = None
            for s in range(S):
                t = _dot(dy_ref[:, s * Ns:(s + 1) * Ns].astype(BF16), w_ref[s, ks, :], NT)
                acc = t if acc is None else acc + t
            if mode == "plain":
                o_ref[:, ks] = acc.astype(BF16)
            elif mode == "swiglu":
                us = slice(K + kc * kchunk, K + (kc + 1) * kchunk)
                g = gu_ref[:, ks].astype(F32)
                u = gu_ref[:, us].astype(F32)
                sg = _sigmoid(g)
                o_ref[:, ks] = (acc * u * (sg * (1.0 + g * (1.0 - sg)))).astype(BF16)
                o_ref[:, us] = (acc * (g * sg)).astype(BF16)
            else:
                x = h_ref[...].astype(F32)
                r = lax.rsqrt(jnp.mean(x * x, axis=-1, keepdims=True) + NORM_EPS)
                xhat = x * r
                dxh = acc * g_ref[...]
                dx = r * (dxh - xhat * jnp.mean(dxh * xhat, axis=-1, keepdims=True))
                o_ref[...] = dx + dh_ref[...] if has_dh else dx

                @pl.when(pl.program_id(0) == 0)
                def _():
                    dg_ref[...] = jnp.zeros_like(dg_ref)

                dg_ref[...] += jnp.sum(acc * xhat, axis=0, keepdims=True)

    sem = ("arbitrary",) if mode == "norm" else ("parallel",)
    return _call(body, name=name, grid=(M // tm,), in_specs=specs, out_specs=out_specs, out_shape=out_shape, sem=sem)(*ins)


def _mm_tn(x, dy, *, S, name, tk=1024, kk=None, bias=False):
    M, K = x.shape
    N = dy.shape[1]
    Ns = N // S
    kk = kk or K
    tk = min(tk, M)
    nl = M // tk
    in_specs = [pl.BlockSpec((tk, kk), lambda s, kc, l: (l, kc)), pl.BlockSpec((tk, Ns), lambda s, kc, l: (l, s))]
    out_shape = [jax.ShapeDtypeStruct((S, K, Ns), BF16)]
    out_specs = [pl.BlockSpec((None, kk, Ns), lambda s, kc, l: (s, kc, 0))]
    if bias:
        out_shape.append(jax.ShapeDtypeStruct((1, N), F32))
        out_specs.append(pl.BlockSpec((1, Ns), lambda s, kc, l: (0, s)))

    def body(x_ref, dy_ref, o_ref, *rest):
        acc_ref = rest[-1]
        kc, l = pl.program_id(1), pl.program_id(2)

        @pl.when(l == 0)
        def _():
            acc_ref[...] = jnp.zeros_like(acc_ref)

        acc_ref[...] += _dot(x_ref[...].astype(BF16), dy_ref[...].astype(BF16), TN)
        if bias:
            b_ref = rest[0]

            @pl.when((kc == 0) & (l == 0))
            def _():
                b_ref[...] = jnp.zeros_like(b_ref)

            @pl.when(kc == 0)
            def _():
                b_ref[...] += jnp.sum(dy_ref[...].astype(F32), axis=0, keepdims=True)

        @pl.when(l == nl - 1)
        def _():
            o_ref[...] = acc_ref[...].astype(BF16)

    return _call(body, name=name, grid=(S, K // kk, nl), in_specs=in_specs, out_specs=out_specs, out_shape=out_shape,
                 scratch=[pltpu.VMEM((kk, Ns), F32)], sem=("arbitrary", "arbitrary", "arbitrary"))(x, dy)


def _band_bias(max_dist, has_prev):
    rows = lax.broadcasted_iota(jnp.int32, (BLK, 2 * BLK), 0)
    cols = lax.broadcasted_iota(jnp.int32, (BLK, 2 * BLK), 1)
    dist = rows - cols + BLK
    ok = (dist >= 0) & (dist <= max_dist) & ((cols >= BLK) | has_prev)
    return jnp.where(ok, 0.0, NEG)


Q_SCALE = HEAD_DIM ** -0.5


def _band_fwd(qa, ka, va, *, d, nq, nkv, qcol, kcol, vcol, max_dist, sinks=None, name):
    Lr = qa.shape[0]
    nb = Lr // BLK
    qw, kw, G = nq * HEAD_DIM, nkv * HEAD_DIM, nq // nkv
    cur = lambda colf, w: pl.BlockSpec((BLK, w), lambda r, i: (i, colf(r)))
    prv = lambda colf, w: pl.BlockSpec((BLK, w), lambda r, i: (jnp.maximum(i - 1, 0), colf(r)))
    out = pl.BlockSpec((BLK, qw), lambda r, i: (i, r))
    ins, specs = [qa, ka, ka, va, va], [cur(qcol, qw), cur(kcol, kw), prv(kcol, kw), cur(vcol, kw), prv(vcol, kw)]
    has_sinks = sinks is not None
    if has_sinks:
        ins.append(sinks); specs.append(pl.BlockSpec(memory_space=pltpu.SMEM))

    def body(*refs):
        q_ref, kc_ref, kp_ref, vc_ref, vp_ref = refs[:5]
        sk_ref = refs[5] if has_sinks else None
        o_ref, lse_ref = refs[-2], refs[-1]
        bias = _band_bias(max_dist, pl.program_id(1) > 0)
        k2 = jnp.concatenate([kp_ref[...], kc_ref[...]], axis=0)
        v2 = jnp.concatenate([vp_ref[...], vc_ref[...]], axis=0)
        for h in range(nq):
            hs = slice(h * HEAD_DIM, (h + 1) * HEAD_DIM)
            ks = slice((h // G) * HEAD_DIM, (h // G + 1) * HEAD_DIM)
            s = _dot(q_ref[:, hs] * jnp.asarray(Q_SCALE, BF16), k2[:, ks], NT) + bias
            m = jnp.max(s, axis=-1, keepdims=True)
            if has_sinks:
                m = jnp.maximum(m, sk_ref[h])
            p = jnp.exp(s - m)
            l = jnp.sum(p, axis=-1, keepdims=True)
            if has_sinks:
                l = l + jnp.exp(sk_ref[h] - m)
            o_ref[:, hs] = (_dot(p.astype(BF16), v2[:, ks]) / l).astype(BF16)
            lse_ref[:, hs] = jnp.broadcast_to(m + jnp.log(l), (BLK, HEAD_DIM))

    return _call(body, name=name, grid=(d, nb), in_specs=specs, out_specs=[out, out],
                 out_shape=[jax.ShapeDtypeStruct((Lr, d * qw), BF16), jax.ShapeDtypeStruct((Lr, d * qw), F32)],
                 sem=("parallel", "parallel"))(*ins)


def _band_bwd(qa, ka, va, doa, oa, lsea, *, d, nq, nkv, qcol, kcol, vcol, docol, max_dist, sinks=None, name):
    Lr = qa.shape[0]
    nb = Lr // BLK
    qw, kw, G = nq * HEAD_DIM, nkv * HEAD_DIM, nq // nkv
    transposed = G > 1
    last = lambda i: jnp.minimum(i, nb - 1)
    cur = lambda colf, w: pl.BlockSpec((BLK, w), lambda r, i: (last(i), colf(r)))
    prv = lambda colf, w: pl.BlockSpec((BLK, w), lambda r, i: (jnp.maximum(last(i) - 1, 0), colf(r)))
    own = lambda r: r
    ins = [qa, ka, ka, va, va, doa, oa, lsea]
    specs = [cur(qcol, qw), cur(kcol, kw), prv(kcol, kw), cur(vcol, kw), prv(vcol, kw), cur(docol, qw), cur(own, qw),
             cur(own, qw)]
    has_sinks = sinks is not None
    if has_sinks:
        ins.append(sinks); specs.append(pl.BlockSpec(memory_space=pltpu.SMEM))
    out_shape = [jax.ShapeDtypeStruct((Lr, d * qw), F32), jax.ShapeDtypeStruct((Lr, d * kw), F32),
                 jax.ShapeDtypeStruct((Lr, d * kw), F32)]
    behind = lambda r, i: (jnp.maximum(i - 1, 0), r)
    out_specs = [pl.BlockSpec((BLK, qw), lambda r, i: (last(i), r)), pl.BlockSpec((BLK, kw), behind),
                 pl.BlockSpec((BLK, kw), behind)]
    if has_sinks:
        out_shape.append(jax.ShapeDtypeStruct((8, 128), F32))
        out_specs.append(pl.BlockSpec((8, 128), lambda r, i: (0, 0)))

    def body(*refs):
        it = iter(refs)
        q_ref, kc_ref, kp_ref, vc_ref, vp_ref, do_ref, o_ref, ls_ref = (next(it) for _ in range(8))
        sk_ref = next(it) if has_sinks else None
        dq_ref, dk_ref, dv_ref = next(it), next(it), next(it)
        dsk_ref = next(it) if has_sinks else None
        dk_car, dv_car = next(it), next(it)
        r_id, i = pl.program_id(0), pl.program_id(1)

        @pl.when(i == 0)
        def _():
            dk_car[...] = jnp.zeros_like(dk_car)
            dv_car[...] = jnp.zeros_like(dv_car)

        if has_sinks:
            @pl.when((r_id == 0) & (i == 0))
            def _():
                dsk_ref[...] = jnp.zeros_like(dsk_ref)

        @pl.when(i == nb)
        def _():
            dk_ref[...] = dk_car[...]
            dv_ref[...] = dv_car[...]

        @pl.when(i < nb)
        def _():
            bias = _band_bias(max_dist, i > 0)
            k2 = jnp.concatenate([kp_ref[...], kc_ref[...]], axis=0)
            v2 = jnp.concatenate([vp_ref[...], vc_ref[...]], axis=0)
            if has_sinks:
                lane = lax.broadcasted_iota(jnp.int32, (8, 128), 1)
                dsk = jnp.zeros((8, 128), F32)
            for kv in range(nkv):
                ks = slice(kv * HEAD_DIM, (kv + 1) * HEAD_DIM)
                kh, vh = k2[:, ks], v2[:, ks]
                shape = (HEAD_DIM, 2 * BLK) if transposed else (2 * BLK, HEAD_DIM)
                dk, dv = jnp.zeros(shape, F32), jnp.zeros(shape, F32)
                for g in range(G):
                    h = kv * G + g
                    hs = slice(h * HEAD_DIM, (h + 1) * HEAD_DIM)
                    q = q_ref[:, hs] * jnp.asarray(Q_SCALE, BF16)
                    do = do_ref[:, hs]
                    lse = ls_ref[:, h * HEAD_DIM:h * HEAD_DIM + 1]
                    dl = jnp.sum(do.astype(F32) * o_ref[:, hs].astype(F32), axis=-1, keepdims=True)
                    p = jnp.exp(_dot(q, kh, NT) + bias - lse)
                    ds = (p * (_dot(do, vh, NT) - dl)).astype(BF16)
                    dq_ref[:, hs] = _dot(ds, kh) * Q_SCALE
                    if transposed:
                        dk = dk + _dot(q, ds, TN)
                        dv = dv + _dot(do, p.astype(BF16), TN)
                    else:
                        dk = dk + _dot(ds, q, TN)
                        dv = dv + _dot(p.astype(BF16), do, TN)
                    if has_sinks:
                        val = -jnp.sum(jnp.exp(sk_ref[h] - lse) * dl, axis=0, keepdims=True)
                        dsk = dsk + jnp.where(lane == h, val, 0.0)
                if transposed:
                    dk, dv = dk.T, dv.T
                dk_ref[:, ks] = dk_car[:, ks] + dk[:BLK]
                dv_ref[:, ks] = dv_car[:, ks] + dv[:BLK]
                dk_car[:, ks] = dk[BLK:]
                dv_car[:, ks] = dv[BLK:]
            if has_sinks:
                dsk_ref[...] += dsk

    return _call(body, name=name, grid=(d, nb + 1), in_specs=specs, out_specs=out_specs, out_shape=out_shape,
                 scratch=[pltpu.VMEM((BLK, kw), F32), pltpu.VMEM((BLK, kw), F32)], sem=("arbitrary", "arbitrary"))(*ins)


def _attn_grad_combine(branches, tabs, *, name, tm=256):
    L, qw = branches[0][0].shape
    kw = branches[0][1].shape[1]
    nbr = len(branches)
    row = lambda w: pl.BlockSpec((tm, w), lambda i: (i, 0))
    ins, specs = [], []
    for dq, dk, dv in branches:
        ins += [dq, dk, dv]; specs += [row(qw), row(kw), row(kw)]
    ins += list(tabs); specs += [row(128)] * 3

    def body(*refs):
        c, s1, s2 = (t[...] for t in refs[3 * nbr:3 * nbr + 3])
        o_ref = refs[-1]
        for part, (w, off, rot) in enumerate(((qw, 0, True), (kw, qw, True), (kw, qw + kw, False))):
            for cb in range(w // 128):
                cs = slice(cb * 128, (cb + 1) * 128)
                v = refs[part][:, cs]
                for b in range(1, nbr):
                    v = v + refs[3 * b + part][:, cs]
                if rot:
                    v = _rope_bwd(v, c, s1, s2)
                o_ref[:, off + cb * 128:off + (cb + 1) * 128] = v.astype(BF16)

    return _call(body, name=name, grid=(L // tm,), in_specs=specs, out_specs=row(qw + 2 * kw),
                 out_shape=jax.ShapeDtypeStruct((L, qw + 2 * kw), BF16), sem=("parallel",))(*ins)


def _xattn_fwd(q, kv, *, name, tq=512):
    L, W = q.shape
    scale = XA_HEAD_DIM ** -0.5
    row = pl.BlockSpec((tq, W), lambda i: (i, 0))
    kvs = pl.BlockSpec((N_MEM, 2 * W), lambda i: (0, 0))

    def body(q_ref, kv_ref, o_ref, lse_ref):
        for h in range(XA_HEADS):
            hs = slice(h * XA_HEAD_DIM, (h + 1) * XA_HEAD_DIM)
            vs = slice(W + h * XA_HEAD_DIM, W + (h + 1) * XA_HEAD_DIM)
            s = _dot(q_ref[:, hs], kv_ref[:, hs], NT) * scale
            m = jnp.max(s, axis=-1, keepdims=True)
            p = jnp.exp(s - m)
            l = jnp.sum(p, axis=-1, keepdims=True)
            o_ref[:, hs] = (_dot(p.astype(BF16), kv_ref[:, vs]) / l).astype(BF16)
            lse_ref[:, hs] = jnp.broadcast_to(m + jnp.log(l), (tq, XA_HEAD_DIM))

    return _call(body, name=name, grid=(L // tq,), in_specs=[row, kvs], out_specs=[row, row],
                 out_shape=[jax.ShapeDtypeStruct((L, W), BF16), jax.ShapeDtypeStruct((L, W), F32)], sem=("parallel",))(q, kv)


def _xattn_bwd(q, kv, o, lse, do, *, name, tq=512):
    L, W = q.shape
    scale = XA_HEAD_DIM ** -0.5
    row = pl.BlockSpec((tq, W), lambda i: (i, 0))
    kvs = pl.BlockSpec((N_MEM, 2 * W), lambda i: (0, 0))

    def body(q_ref, kv_ref, o_ref, lse_ref, do_ref, dq_ref, dkv_ref):
        @pl.when(pl.program_id(0) == 0)
        def _():
            dkv_ref[...] = jnp.zeros_like(dkv_ref)

        for h in range(XA_HEADS):
            hs = slice(h * XA_HEAD_DIM, (h + 1) * XA_HEAD_DIM)
            vs = slice(W + h * XA_HEAD_DIM, W + (h + 1) * XA_HEAD_DIM)
            qh, kh, vh, doh = q_ref[:, hs], kv_ref[:, hs], kv_ref[:, vs], do_ref[:, hs]
            p = jnp.exp(_dot(qh, kh, NT) * scale - lse_ref[:, h * XA_HEAD_DIM:h * XA_HEAD_DIM + 1])
            dl = jnp.sum(doh.astype(F32) * o_ref[:, hs].astype(F32), axis=-1, keepdims=True)
            ds = (p * (_dot(doh, vh, NT) - dl) * scale).astype(BF16)
            dq_ref[:, hs] = _dot(ds, kh).astype(BF16)
            dkv_ref[:, hs] += _dot(ds, qh, TN)
            dkv_ref[:, vs] += _dot(p.astype(BF16), doh, TN)

    return _call(body, name=name, grid=(L // tq,), in_specs=[row, kvs, row, row, row], out_specs=[row, kvs],
                 out_shape=[jax.ShapeDtypeStruct((L, W), BF16), jax.ShapeDtypeStruct((N_MEM, 2 * W), F32)],
                 sem=("arbitrary",))(q, kv, o, lse, do)


def _neg_expm1(z):
    series = -(z * (1.0 + z * (0.5 + z * (1.0 / 6.0 + z * (1.0 / 24.0 + z * (1.0 / 120.0))))))
    return jnp.where(z > -0.05, series, 1.0 - jnp.exp(z))


def _softplus(z):
    return jnp.maximum(z, 0.0) + jnp.log(1.0 + jnp.exp(-jnp.abs(z)))


def _gelu_parts(y):
    c = 0.7978845608028654
    t = jnp.tanh(c * (y + 0.044715 * y * y * y))
    gy = 0.5 * y * (1.0 + t)
    dgy = 0.5 * (1.0 + t) + 0.5 * y * (1.0 - t * t) * c * (1.0 + 3.0 * 0.044715 * y * y)
    return gy, dgy


def _lru_gates(xc, wa_ref, ba, wx_ref, bx, sp):
    rs, igs = [], []
    for hd in range(LRU_HEADS):
        sl = slice(hd * LRU_HEAD_DIM, (hd + 1) * LRU_HEAD_DIM)
        xh = xc[:, sl].astype(BF16)
        rs.append(_sigmoid(_dot(xh, wa_ref[hd]) + ba[:, sl]))
        igs.append(_sigmoid(_dot(xh, wx_ref[hd]) + bx[:, sl]))
    r, ig = jnp.concatenate(rs, axis=1), jnp.concatenate(igs, axis=1)
    la = -LRU_C * r * sp
    return r, ig, jnp.exp(la), _neg_expm1(2.0 * la)


def _conv_taps(x_ext, halo):
    n = x_ext.shape[0]
    return [x_ext[halo:] if k == CONV_WIDTH - 1 else pltpu.roll(x_ext, CONV_WIDTH - 1 - k, 0)[halo:]
            for k in range(CONV_WIDTH)]


def _lru_fwd(proj, cw, cb, wa, ba, wx, bx, lam, *, name, tc=512):
    L = proj.shape[0]
    W = LRU_HEADS * LRU_HEAD_DIM
    nb = L // tc
    whole = lambda shape: pl.BlockSpec(shape, lambda i: (0,) * len(shape))
    specs = [pl.BlockSpec((tc, W), lambda i: (i, 0)), pl.BlockSpec((tc, W), lambda i: (i, 1)),
             pl.BlockSpec((16, W), lambda i: (jnp.maximum(i * (tc // 16) - 1, 0), 0)),
             whole((CONV_WIDTH, W)), whole((1, W)), whole((LRU_HEADS, LRU_HEAD_DIM, LRU_HEAD_DIM)), whole((1, W)),
             whole((LRU_HEADS, LRU_HEAD_DIM, LRU_HEAD_DIM)), whole((1, W)), whole((1, W))]
    out_specs = [pl.BlockSpec((tc, W), lambda i: (i, 0))] * 2
    out_shape = [jax.ShapeDtypeStruct((L, W), BF16), jax.ShapeDtypeStruct((L, W), F32)]

    def body(x_ref, y_ref, xh_ref, cw_ref, cb_ref, wa_ref, ba_ref, wx_ref, bx_ref, lam_ref, rec_ref, hs_ref,
             hcar, a_scr, b_scr):
        i = pl.program_id(0)

        @pl.when(i == 0)
        def _():
            hcar[...] = jnp.zeros_like(hcar)

        halo = jnp.where(i > 0, xh_ref[...].astype(F32), 0.0)
        taps = _conv_taps(jnp.concatenate([halo, x_ref[...].astype(F32)], axis=0), 16)
        xc = cb_ref[...] + sum(cw_ref[k:k + 1, :] * taps[k] for k in range(CONV_WIDTH))
        _, ig, a, om = _lru_gates(xc, wa_ref, ba_ref[...], wx_ref, bx_ref[...], _softplus(-lam_ref[...]))
        b = jnp.sqrt(om) * (ig * xc)
        rowmod = lax.broadcasted_iota(jnp.int32, (tc, W), 0) & 7
        for s in (1, 2, 4):
            keep = rowmod >= s
            b = jnp.where(keep, a * pltpu.roll(b, s, 0) + b, b)
            a = jnp.where(keep, a * pltpu.roll(a, s, 0), a)
        a_scr[...] = a
        b_scr[...] = b

        def tile(j, hc):
            rows = pl.ds(pl.multiple_of(j * 8, 8), 8)
            ht = a_scr[rows, :] * hc + b_scr[rows, :]
            hs_ref[rows, :] = ht
            return jnp.broadcast_to(ht[7:8, :], (8, W))

        hcar[...] = lax.fori_loop(0, tc // 8, tile, hcar[...])
        gy, _ = _gelu_parts(y_ref[...].astype(F32))
        rec_ref[...] = (hs_ref[...] * gy).astype(BF16)

    return _call(body, name=name, grid=(nb,), in_specs=specs, out_specs=out_specs, out_shape=out_shape,
                 scratch=[pltpu.VMEM((8, W), F32), pltpu.VMEM((tc, W), F32), pltpu.VMEM((tc, W), F32)],
                 sem=("arbitrary",))(proj, proj, proj, cw, cb, wa, ba, wx, bx, lam)


def _lru_bwd(proj, hs, drec_src, cw, cb, wa, ba, wx, bx, lam, *, name, tc=256):
    L = proj.shape[0]
    W = LRU_HEADS * LRU_HEAD_DIM
    nb = L // tc
    tb = lambda i: nb - 1 - i
    whole = lambda shape: pl.BlockSpec(shape, lambda i: (0,) * len(shape))
    gate_w = (LRU_HEADS, LRU_HEAD_DIM, LRU_HEAD_DIM)
    specs = [pl.BlockSpec((tc, W), lambda i: (tb(i), 0)), pl.BlockSpec((tc, W), lambda i: (tb(i), 1)),
             pl.BlockSpec((16, W), lambda i: (jnp.maximum(tb(i) * (tc // 16) - 1, 0), 0)),
             pl.BlockSpec((tc, W), lambda i: (tb(i), 0)),
             pl.BlockSpec((8, W), lambda i: (jnp.maximum(tb(i) * (tc // 8) - 1, 0), 0)),
             pl.BlockSpec((tc, W), lambda i: (tb(i), 0)),
             whole((CONV_WIDTH, W)), whole((1, W)), whole(gate_w), whole((1, W)), whole(gate_w), whole((1, W)), whole((1, W))]
    out_specs = [pl.BlockSpec((tc, 2 * W), lambda i: (tb(i), 0)), whole((CONV_WIDTH, W)), whole((1, W)), whole(gate_w),
                 whole((1, W)), whole(gate_w), whole((1, W)), whole((1, W))]
    vec = jax.ShapeDtypeStruct((1, W), F32)
    out_shape = [jax.ShapeDtypeStruct((L, 2 * W), BF16), jax.ShapeDtypeStruct((CONV_WIDTH, W), F32), vec,
                 jax.ShapeDtypeStruct(gate_w, F32), vec, jax.ShapeDtypeStruct(gate_w, F32), vec, vec]

    def body(x_ref, y_ref, xh_ref, hs_ref, hh_ref, dr_ref, cw_ref, cb_ref, wa_ref, ba_ref, wx_ref, bx_ref, lam_ref,
             dxy_ref, dcw_ref, dcb_ref, dwa_ref, dba_ref, dwx_ref, dbx_ref, dlam_ref, gcar, dxc_car, a_scr, b_scr, g_scr):
        pid = pl.program_id(0)
        t = tb(pid)
        accs = (dcw_ref, dcb_ref, dwa_ref, dba_ref, dwx_ref, dbx_ref, dlam_ref)

        @pl.when(pid == 0)
        def _():
            gcar[...] = jnp.zeros_like(gcar)
            dxc_car[...] = jnp.zeros_like(dxc_car)
            for r in accs:
                r[...] = jnp.zeros_like(r)

        halo = jnp.where(t > 0, xh_ref[...].astype(F32), 0.0)
        taps = _conv_taps(jnp.concatenate([halo, x_ref[...].astype(F32)], axis=0), 16)
        xc = cb_ref[...] + sum(cw_ref[k:k + 1, :] * taps[k] for k in range(CONV_WIDTH))
        lam = lam_ref[...]
        sp = _softplus(-lam)
        r, ig, a, om = _lru_gates(xc, wa_ref, ba_ref[...], wx_ref, bx_ref[...], sp)
        sq = jnp.sqrt(om)
        hblk = hs_ref[...]
        hprev = pltpu.roll(jnp.concatenate([jnp.where(t > 0, hh_ref[...], 0.0), hblk], axis=0), 1, 0)[8:]
        gy, dgy = _gelu_parts(y_ref[...].astype(F32))
        drec = dr_ref[...].astype(F32)
        dxy_ref[:, W:] = (drec * hblk * dgy).astype(BF16)

        rowidx = lax.broadcasted_iota(jnp.int32, (tc, W), 0)
        rowmod = rowidx & 7
        ca = jnp.where(rowidx == tc - 1, 1.0, pltpu.roll(a, tc - 1, 0))
        cbv = drec * gy
        for s in (1, 2, 4):
            keep = rowmod < 8 - s
            cbv = jnp.where(keep, ca * pltpu.roll(cbv, tc - s, 0) + cbv, cbv)
            ca = jnp.where(keep, ca * pltpu.roll(ca, tc - s, 0), ca)
        a_scr[...] = ca
        b_scr[...] = cbv

        def tile(k, gc):
            j = tc // 8 - 1 - k
            rows = pl.ds(pl.multiple_of(j * 8, 8), 8)
            gt = a_scr[rows, :] * gc + b_scr[rows, :]
            g_scr[rows, :] = gt
            return jnp.broadcast_to(gt[0:1, :], (8, W))

        lax.fori_loop(0, tc // 8, tile, gcar[...])
        G = g_scr[...]
        gcar[...] = jnp.broadcast_to(a[0:1, :] * G[0:1, :], (8, W))

        da = G * hprev
        dsq = G * (ig * xc)
        di = G * (sq * xc)
        dxc = G * (sq * ig)
        dla = da * a - 2.0 * a * a * (dsq * 0.5 * lax.rsqrt(om))
        dlam_ref[...] += jnp.sum(dla * (-LRU_C * r), axis=0, keepdims=True) * (-_sigmoid(-lam))
        dpr = dla * (-LRU_C * sp) * r * (1.0 - r)
        dpi = di * ig * (1.0 - ig)
        dba_ref[...] += jnp.sum(dpr, axis=0, keepdims=True)
        dbx_ref[...] += jnp.sum(dpi, axis=0, keepdims=True)
        back = []
        for hd in range(LRU_HEADS):
            sl = slice(hd * LRU_HEAD_DIM, (hd + 1) * LRU_HEAD_DIM)
            xh, dprh, dpih = xc[:, sl].astype(BF16), dpr[:, sl].astype(BF16), dpi[:, sl].astype(BF16)
            back.append(_dot(dprh, wa_ref[hd], NT) + _dot(dpih, wx_ref[hd], NT))
            dwa_ref[hd] += _dot(xh, dprh, TN)
            dwx_ref[hd] += _dot(xh, dpih, TN)
        dxc = dxc + jnp.concatenate(back, axis=1)
        dcb_ref[...] += jnp.sum(dxc, axis=0, keepdims=True)
        for k in range(CONV_WIDTH):
            dcw_ref[k:k + 1, :] += jnp.sum(dxc * taps[k], axis=0, keepdims=True)
        ext = jnp.concatenate([dxc, dxc_car[...]], axis=0)
        dx = cw_ref[CONV_WIDTH - 1:CONV_WIDTH, :] * dxc
        for k in range(CONV_WIDTH - 1):
            dx = dx + cw_ref[k:k + 1, :] * pltpu.roll(ext, tc + 8 - (CONV_WIDTH - 1 - k), 0)[:tc]
        dxc_car[...] = dxc[0:8, :]
        dxy_ref[:, :W] = dx.astype(BF16)

    scratch = [pltpu.VMEM((8, W), F32), pltpu.VMEM((8, W), F32)] + [pltpu.VMEM((tc, W), F32)] * 3
    return _call(body, name=name, grid=(nb,), in_specs=specs, out_specs=out_specs, out_shape=out_shape, scratch=scratch,
                 sem=("arbitrary",))(proj, proj, proj, hs, hs, drec_src, cw, cb, wa, ba, wx, bx, lam)


def _final_loss(h, gain, target, *, name, tm=256):
    M, K = h.shape
    row = pl.BlockSpec((tm, K), lambda i: (i, 0))
    vec = pl.BlockSpec((1, K), lambda i: (0, 0))
    one = pl.BlockSpec((1, 128), lambda i: (0, 0))

    def body(h_ref, g_ref, t_ref, dh_ref, dg_ref, loss_ref):
        @pl.when(pl.program_id(0) == 0)
        def _():
            dg_ref[...] = jnp.zeros_like(dg_ref)
            loss_ref[...] = jnp.zeros_like(loss_ref)

        x = h_ref[...]
        r = lax.rsqrt(jnp.mean(x * x, axis=-1, keepdims=True) + NORM_EPS)
        xhat = x * r
        err = xhat * g_ref[...] - t_ref[...]
        loss_ref[...] += 0.5 / K * jnp.sum(err * err)
        dy = err * (1.0 / K)
        dg_ref[...] += jnp.sum(dy * xhat, axis=0, keepdims=True)
        dxh = dy * g_ref[...]
        dh_ref[...] = r * (dxh - xhat * jnp.mean(dxh * xhat, axis=-1, keepdims=True))

    return _call(body, name=name, grid=(M // tm,), in_specs=[row, vec, row], out_specs=[row, vec, one],
                 out_shape=[jax.ShapeDtypeStruct((M, K), F32), jax.ShapeDtypeStruct((1, K), F32),
                            jax.ShapeDtypeStruct((1, 128), F32)], sem=("arbitrary",))(h, gain.reshape(1, K), target)


def _dilated_merge(branches, *, name, tm=512):
    L, W = branches[0].shape
    nbr = len(branches) // 2
    row = pl.BlockSpec((tm, W), lambda i: (i, 0))

    def body(*refs):
        o_ref, lse_ref = refs[-2], refs[-1]
        lses = [refs[2 * b + 1][...] for b in range(nbr)]
        m = lses[0]
        for t in lses[1:]:
            m = jnp.maximum(m, t)
        ws = [jnp.exp(t - m) for t in lses]
        den = ws[0]
        for t in ws[1:]:
            den = den + t
        acc = ws[0] * refs[0][...].astype(F32)
        for b in range(1, nbr):
            acc = acc + ws[b] * refs[2 * b][...].astype(F32)
        o_ref[...] = (acc / den).astype(BF16)
        lse_ref[...] = m + jnp.log(den)

    return _call(body, name=name, grid=(L // tm,), in_specs=[row] * (2 * nbr), out_specs=[row, row],
                 out_shape=[jax.ShapeDtypeStruct((L, W), BF16), jax.ShapeDtypeStruct((L, W), F32)], sem=("parallel",))(*branches)


def _dilated_fwd(proj0):
    L = proj0.shape[0]
    qkv = proj0[:, 2 * D_MODEL:]
    W = B_HEADS * HEAD_DIM
    outs = []
    for window, d in DILATED_PATTERN:
        view = qkv.reshape(L // d, d * 3 * W)
        o, lse = _band_fwd(view, view, view, d=d, nq=B_HEADS, nkv=B_HEADS, qcol=lambda r: 3 * r, kcol=lambda r: 3 * r + 1,
                           vcol=lambda r: 3 * r + 2, max_dist=window // d, name=f"dilated_fwd_d{d}")
        outs += [o.reshape(L, W), lse.reshape(L, W)]
    return _dilated_merge(outs, name="dilated_merge")


def _dilated_bwd(proj0, att, lse, datt, tabs):
    L = proj0.shape[0]
    qkv = proj0[:, 2 * D_MODEL:]
    Wh = B_HEADS * HEAD_DIM
    branches = []
    for window, d in DILATED_PATTERN:
        view = qkv.reshape(L // d, d * 3 * Wh)
        v1 = lambda t: t.reshape(L // d, d * Wh)
        outs = _band_bwd(view, view, view, v1(datt), v1(att), v1(lse), d=d, nq=B_HEADS, nkv=B_HEADS,
                         qcol=lambda r: 3 * r, kcol=lambda r: 3 * r + 1, vcol=lambda r: 3 * r + 2, docol=lambda r: r,
                         max_dist=window // d, name=f"dilated_bwd_d{d}")
        branches.append([o.reshape(L, Wh) for o in outs])
    return _attn_grad_combine(branches, tabs, name="dilated_grad_combine")


def _device_step(x, mem, target, w, after_layer1=None):
    L = x.shape[0]
    tabs = _rope_tables(L)
    g = {}
    saved = []
    h = x
    for layer in range(2):
        sv = {"h_mix": h}
        if layer == 0:
            proj, n = _rowmm(h, w["ab_w_in"], name="l0_in_proj", gain=w["mix_norm"][0],
                             rope=(2 * D_MODEL, 2 * D_MODEL + 2 * B_HEADS * HEAD_DIM, tabs))
            rec, hs = _lru_fwd(proj, w["lru_conv_w"], w["lru_conv_b"], w["lru_wa"], w["lru_ba"], w["lru_wx"], w["lru_bx"],
                               w["lru_lambda"], name="lru_fwd")
            att, lse = _dilated_fwd(proj)
            mix = jnp.concatenate([rec, att], axis=1)
            (h,) = _rowmm(mix, w["ab_w_out"], name="l0_out_proj", res=h)
            sv.update(hs=hs)
        else:
            proj, n = _rowmm(h, w["c_w_qkv"], name="l1_qkv_proj", gain=w["mix_norm"][1], bias=w["c_b_qkv"],
                             rope=(0, (C_HEADS + C_KV_HEADS) * HEAD_DIM, tabs))
            mix, lse = _band_fwd(proj, proj, proj, d=1, nq=C_HEADS, nkv=C_KV_HEADS, qcol=lambda r: 0, kcol=lambda r: 8,
                                 vcol=lambda r: 9, max_dist=C_WINDOW - 1, sinks=w["c_sinks"], name="swa_fwd")
            (h,) = _rowmm(mix, w["c_w_out"], name="l1_out_proj", res=h, bias=w["c_b_out"])
        sv.update(proj=proj, n_mix=n, mix=mix, lse=lse, h_xa=h)
        xq, nx = _rowmm(h, w["xa_wq"][layer][None], name=f"xa_q_proj{layer}", gain=w["xa_norm"][layer])
        kv, nm = _rowmm(mem, w["xa_wkv"][layer][None], name=f"xa_kv_proj{layer}", gain=w["xa_mem_norm"][layer])
        xo, xlse = _xattn_fwd(xq, kv, name=f"xa_fwd{layer}")
        (h,) = _rowmm(xo, w["xa_wo"][layer], name=f"xa_out_proj{layer}", res=h)
        sv.update(xq=xq, nx=nx, kv=kv, nm=nm, xo=xo, xlse=xlse, h_ffn=h)
        gu, nf, act = _rowmm(h, w["ffn_w_gate_up"], layer=layer, name=f"ffn_in{layer}", gain=w["ffn_norm"][layer], swiglu=True)
        (h,) = _rowmm(act, w["ffn_w_down"][layer][None], name=f"ffn_out{layer}", res=h, tm=512)
        sv.update(gu=gu, nf=nf, act=act)
        saved.append(sv)

    dh, g["final_norm"], loss = _final_loss(h, w["final_norm"], target, name="final_loss")

    stk = {k: [None, None] for k in ("xa_norm", "xa_mem_norm", "ffn_norm", "mix_norm")}
    for layer in (1, 0):
        sv = saved[layer]
        (g["ffn_w_down", layer],) = _mm_tn(sv["act"], dh, S=1, name=f"ffn_down_dw{layer}", kk=D_FF // 2)
        (dgu,) = _mm_nt(dh, w["ffn_w_down"][layer][None], name=f"ffn_dact{layer}", mode="swiglu", kchunk=D_FF // 2, gu=sv["gu"])
        (g["ffn_w_gate_up", layer],) = _mm_tn(sv["nf"], dgu, S=N_CHIPS, name=f"ffn_gu_dw{layer}")
        dh, stk["ffn_norm"][layer] = _mm_nt(dgu, w["ffn_w_gate_up"], layer=layer, name=f"ffn_dx{layer}", mode="norm",
                                            h=sv["h_ffn"], gain=w["ffn_norm"][layer], dh=dh)
        (g["xa_wo", layer],) = _mm_tn(sv["xo"], dh, S=N_CHIPS, name=f"xa_wo_dw{layer}")
        (dxo,) = _mm_nt(dh, w["xa_wo"][layer], name=f"xa_dxo{layer}", mode="plain")
        dxq, dkv = _xattn_bwd(sv["xq"], sv["kv"], sv["xo"], sv["xlse"], dxo, name=f"xa_bwd{layer}")
        (g["xa_wq", layer],) = _mm_tn(sv["nx"], dxq, S=1, name=f"xa_wq_dw{layer}")
        dh, stk["xa_norm"][layer] = _mm_nt(dxq, w["xa_wq"][layer][None], name=f"xa_dx{layer}", mode="norm", h=sv["h_xa"],
                                           gain=w["xa_norm"][layer], dh=dh)
        (g["xa_wkv", layer],) = _mm_tn(sv["nm"], dkv, S=1, name=f"xa_wkv_dw{layer}")
        _, stk["xa_mem_norm"][layer] = _mm_nt(dkv, w["xa_wkv"][layer][None], name=f"xa_dmem{layer}", mode="norm", h=mem,
                                              gain=w["xa_mem_norm"][layer])
        if layer == 1:
            g["c_w_out"], g["c_b_out"] = _mm_tn(sv["mix"], dh, S=1, name="l1_out_dw", bias=True)
            (dmix,) = _mm_nt(dh, w["c_w_out"], name="l1_dmix", mode="plain")
            dq, dk, dv, dsk = _band_bwd(sv["proj"], sv["proj"], sv["proj"], dmix, sv["mix"], sv["lse"], d=1, nq=C_HEADS,
                                        nkv=C_KV_HEADS, qcol=lambda r: 0, kcol=lambda r: 8, vcol=lambda r: 9,
                                        docol=lambda r: 0, max_dist=C_WINDOW - 1, sinks=w["c_sinks"], name="swa_bwd")
            g["c_sinks"] = dsk[0, :C_HEADS]
            dproj = _attn_grad_combine([(dq, dk, dv)], tabs, name="swa_grad_combine")
            g["c_w_qkv"], g["c_b_qkv"] = _mm_tn(sv["n_mix"], dproj, S=1, name="l1_qkv_dw", bias=True)
            dh, stk["mix_norm"][1] = _mm_nt(dproj, w["c_w_qkv"], name="l1_dx", mode="norm", h=sv["h_mix"],
                                            gain=w["mix_norm"][1], dh=dh)
            if after_layer1 is not None:
                after_layer1(g)
        else:
            (g["ab_w_out"],) = _mm_tn(sv["mix"], dh, S=1, name="l0_out_dw", kk=768)
            (dmix,) = _mm_nt(dh, w["ab_w_out"], name="l0_dmix", mode="plain", kchunk=768)
            (dxy, g["lru_conv_w"], g["lru_conv_b"], g["lru_wa"], g["lru_ba"], g["lru_wx"], g["lru_bx"],
             g["lru_lambda"]) = _lru_bwd(sv["proj"], sv["hs"], dmix, w["lru_conv_w"], w["lru_conv_b"], w["lru_wa"],
                                         w["lru_ba"], w["lru_wx"], w["lru_bx"], w["lru_lambda"], name="lru_bwd")
            dqkv = _dilated_bwd(sv["proj"], sv["mix"][:, D_MODEL:], sv["lse"], dmix[:, D_MODEL:], tabs)
            dproj = jnp.concatenate([dxy, dqkv], axis=1)
            (g["ab_w_in"],) = _mm_tn(sv["n_mix"], dproj, S=N_CHIPS, name="l0_in_dw")
            dh, stk["mix_norm"][0] = _mm_nt(dproj, w["ab_w_in"], name="l0_dx", mode="norm", h=sv["h_mix"],
                                            gain=w["mix_norm"][0], dh=dh)
    for k, v in stk.items():
        g[k] = jnp.concatenate(v, axis=0)
    return loss[0, 0], dh, g


ANY = pl.BlockSpec(memory_space=pl.ANY)
MESH = pl.DeviceIdType.MESH


def _place():
    x, y, c = lax.axis_index("x"), lax.axis_index("y"), lax.axis_index("c")
    return x, y, c, [(1 - x, y), (x, 1 - y), (1 - x, 1 - y)]


def _remote(send_sems, recv_sems):
    def copy(k, src, dst, to):
        return pltpu.make_async_remote_copy(src_ref=src, dst_ref=dst, send_sem=send_sems.at[k], recv_sem=recv_sems.at[k],
                                            device_id=to, device_id_type=MESH)
    return copy


def _halves(ref, n_rows):
    rh = n_rows // 2
    return lambda lead, hh: ref.at[(*lead, pl.ds(hh * rh, rh), slice(None))]


def _gather_weights(packs, spack):
    n = len(packs)

    def body(*refs):
        w_refs, s_ref, wf_refs, sf_ref = refs[:n], refs[n], refs[n + 1:2 * n + 1], refs[2 * n + 1]
        x, y, c, chips = _place()
        me, sib = 2 * x + y, (x, y, 1 - c)
        copy = _remote(*refs[-2:])
        src = [_halves(w_refs[g], packs[g].shape[0]) for g in range(n)]
        dst = [_halves(wf_refs[g], packs[g].shape[0]) for g in range(n)]
        sends = []
        for g in range(n):
            for j, (cx, cy) in enumerate(chips):
                sends.append(copy(3 * g + j, src[g]((), c), dst[g]((me,), c), (cx, cy, c)))
        for j, (cx, cy) in enumerate(chips):
            sends.append(copy(6 * n + j, s_ref, sf_ref.at[me], (cx, cy, c)))
        for cp in sends:
            cp.start()
        for g in range(n):
            for j, (cx, cy) in enumerate(chips):
                got = dst[g]((2 * cx + cy,), c)
                copy(3 * g + j, got, got, sib).wait_recv()
                fwd = copy(3 * n + 3 * g + j, got, got, sib)
                fwd.start()
                sends.append(fwd)
        for g in range(n):
            for j, (cx, cy) in enumerate(chips):
                got = dst[g]((2 * cx + cy,), 1 - c)
                copy(3 * n + 3 * g + j, got, got, sib).wait_recv()
        for j, (cx, cy) in enumerate(chips):
            copy(6 * n + j, s_ref, sf_ref.at[2 * cx + cy], sib).wait_recv()
        for cp in sends:
            cp.wait_send()

    ins = list(packs) + [spack]
    out_shape = [jax.ShapeDtypeStruct((N_CHIPS,) + a.shape, a.dtype) for a in ins]
    n_sems = 6 * n + 3
    outs = pl.pallas_call(body, name="gather_weights", out_shape=out_shape, in_specs=[ANY] * len(ins),
                          out_specs=[ANY] * len(ins),
                          scratch_shapes=[pltpu.SemaphoreType.DMA((n_sems,)), pltpu.SemaphoreType.DMA((n_sems,))])(*ins)
    chip = 2 * lax.axis_index("x") + lax.axis_index("y")
    outs = [lax.dynamic_update_index_in_dim(o, a, chip, 0) for o, a in zip(outs, ins)]
    return outs[:n], outs[n]


SEQUENCER_GATHER_ID = 1


def _gather_weights_behind(packs):
    n = len(packs)

    def body(*refs):
        w_refs, wf_refs = refs[:n], refs[n:2 * n]
        x, y, c, chips = _place()
        me, sib = 2 * x + y, (x, y, 1 - c)
        barrier = pltpu.get_barrier_semaphore()
        for peer in [(cx, cy, c) for cx, cy in chips] + [sib]:
            pl.semaphore_signal(barrier, inc=1, device_id=peer, device_id_type=MESH)
        pl.semaphore_wait(barrier, len(chips) + 1)
        copy = _remote(*refs[-2:])
        src = [_halves(w_refs[g], packs[g].shape[0]) for g in range(n)]
        dst = [_halves(wf_refs[g], packs[g].shape[0]) for g in range(n)]
        sends = []
        for g in range(n):
            for j, (cx, cy) in enumerate(chips):
                sends.append(copy(3 * g + j, src[g]((), c), dst[g]((me,), c), (cx, cy, c)))
        for cp in sends:
            cp.start()
        for g in range(n):
            for j, (cx, cy) in enumerate(chips):
                got = dst[g]((2 * cx + cy,), c)
                copy(3 * g + j, got, got, sib).wait_recv()
                fwd = copy(3 * n + 3 * g + j, got, got, sib)
                fwd.start()
                sends.append(fwd)
        for g in range(n):
            for j, (cx, cy) in enumerate(chips):
                got = dst[g]((2 * cx + cy,), 1 - c)
                copy(3 * n + 3 * g + j, got, got, sib).wait_recv()
        for cp in sends:
            cp.wait_send()

    out_type = [jax.ShapeDtypeStruct((N_CHIPS,) + a.shape, a.dtype) for a in packs]
    outs = pl.kernel(body, out_type=out_type, mesh=plsc.ScalarSubcoreMesh(axis_name="sequencer", num_cores=1),
                     name="gather_weights_behind",
                     scratch_types=[pltpu.SemaphoreType.DMA((6 * n,)), pltpu.SemaphoreType.DMA((6 * n,))],
                     compiler_params=pltpu.CompilerParams(collective_id=SEQUENCER_GATHER_ID))(*packs)
    chip = 2 * lax.axis_index("x") + lax.axis_index("y")
    return [lax.dynamic_update_index_in_dim(o, a, chip, 0) for o, a in zip(outs, packs)]


def _rs_pair_exchange(gpacks, *, name):
    n = len(gpacks)

    def body(*refs):
        g_refs, ra_refs = refs[:n], refs[n:2 * n]
        x, y, c, _ = _place()
        copy = _remote(*refs[-2:])
        cps = []
        for g in range(n):
            half = _halves(g_refs[g], gpacks[g].shape[1])
            cps += [copy(N_CHIPS * g + j, half((j,), 1 - c), ra_refs[g].at[j], (x, y, 1 - c)) for j in range(N_CHIPS)]
        for cp in cps:
            cp.start()
        for cp in cps:
            cp.wait()

    out_shape = [jax.ShapeDtypeStruct((N_CHIPS, a.shape[1] // 2, a.shape[2]), a.dtype) for a in gpacks]
    n_sems = N_CHIPS * n
    return pl.pallas_call(body, name=name, out_shape=out_shape, in_specs=[ANY] * n, out_specs=[ANY] * n,
                          scratch_shapes=[pltpu.SemaphoreType.DMA((n_sems,)), pltpu.SemaphoreType.DMA((n_sems,))])(*gpacks)


def _row_tile(rows, cap=512):
    return max(t for t in range(16, min(rows, cap) + 1, 16) if rows % t == 0)


def _rs_pair_add(place, gpack, ra, *, name):
    _, R, C = gpack.shape
    Rh = R // 2
    tr = _row_tile(Rh)
    nrb = Rh // tr

    def body(p_ref, g_ref, ra_ref, pair_ref, own_ref):
        s = g_ref[...].astype(F32) + ra_ref[...].astype(F32)
        pair_ref[...] = s.astype(BF16)

        @pl.when(pl.program_id(1) == p_ref[1])
        def _():
            own_ref[...] = s

    grid_spec = pltpu.PrefetchScalarGridSpec(
        num_scalar_prefetch=1, grid=(nrb, N_CHIPS),
        in_specs=[pl.BlockSpec((None, tr, C), lambda i, j, p: (j, p[0] * nrb + i, 0)),
                  pl.BlockSpec((None, tr, C), lambda i, j, p: (j, i, 0))],
        out_specs=[pl.BlockSpec((None, tr, C), lambda i, j, p: (j, i, 0)), pl.BlockSpec((tr, C), lambda i, j, p: (i, 0))])
    return pl.pallas_call(
        body, name=name, grid_spec=grid_spec,
        out_shape=[jax.ShapeDtypeStruct((N_CHIPS, Rh, C), BF16), jax.ShapeDtypeStruct((Rh, C), F32)],
        compiler_params=pltpu.CompilerParams(dimension_semantics=("arbitrary", "arbitrary"),
                                             vmem_limit_bytes=VMEM_LIMIT_V7X))(place, gpack, ra)


def _rs_chip_exchange(pairs, small):
    n = len(pairs)

    def body(*refs):
        p_refs, s_ref, rb_refs, rs_ref = refs[:n], refs[n], refs[n + 1:2 * n + 1], refs[2 * n + 1]
        local_sem = refs[-1]
        x, y, c, chips = _place()
        copy = _remote(*refs[-3:-1])
        dev = 4 * x + 2 * y + c
        mine = pltpu.make_async_copy(s_ref, rs_ref.at[dev], local_sem.at[0])
        mine.start()
        cps = []
        for g in range(n):
            cps += [copy(3 * g + j, p_refs[g].at[2 * cx + cy], rb_refs[g].at[j], (cx, cy, c)) for j, (cx, cy) in enumerate(chips)]
        peers = []
        for k in range(1, 8):
            px = 1 - x if k & 4 else x
            py = 1 - y if k & 2 else y
            pc = 1 - c if k & 1 else c
            peers.append((px, py, pc))
            cps.append(copy(3 * n - 1 + k, s_ref, rs_ref.at[dev], (px, py, pc)))
        for cp in cps:
            cp.start()
        for g in range(n):
            for j in range(3):
                copy(3 * g + j, p_refs[g].at[0], rb_refs[g].at[j], (x, y, c)).wait_recv()
        for k, (px, py, pc) in enumerate(peers, start=1):
            copy(3 * n - 1 + k, s_ref, rs_ref.at[4 * px + 2 * py + pc], (x, y, c)).wait_recv()
        for cp in cps:
            cp.wait_send()
        mine.wait()

    out_shape = [jax.ShapeDtypeStruct((3,) + p.shape[1:], p.dtype) for p in pairs]
    out_shape.append(jax.ShapeDtypeStruct((8,) + small.shape, small.dtype))
    n_sems = 3 * n + 7
    return pl.pallas_call(body, name="rs_chip_exchange", out_shape=out_shape, in_specs=[ANY] * (n + 1),
                          out_specs=[ANY] * (n + 1),
                          scratch_shapes=[pltpu.SemaphoreType.DMA((n_sems,)), pltpu.SemaphoreType.DMA((n_sems,)),
                                          pltpu.SemaphoreType.DMA((1,))])(*pairs, small)


SEQUENCER_EXCHANGE_ID = 2


def _rs_chip_exchange_behind(pairs):
    n = len(pairs)

    def body(*refs):
        p_refs, rb_refs = refs[:n], refs[n:2 * n]
        x, y, c, chips = _place()
        barrier = pltpu.get_barrier_semaphore()
        for cx, cy in chips:
            pl.semaphore_signal(barrier, inc=1, device_id=(cx, cy, c), device_id_type=MESH)
        pl.semaphore_wait(barrier, len(chips))
        copy = _remote(*refs[-2:])
        cps = []
        for g in range(n):
            cps += [copy(3 * g + j, p_refs[g].at[2 * cx + cy], rb_refs[g].at[j], (cx, cy, c)) for j, (cx, cy) in enumerate(chips)]
        for cp in cps:
            cp.start()
        for g in range(n):
            for j in range(3):
                copy(3 * g + j, p_refs[g].at[0], rb_refs[g].at[j], (x, y, c)).wait_recv()
        for cp in cps:
            cp.wait_send()

    out_type = [jax.ShapeDtypeStruct((3,) + p.shape[1:], p.dtype) for p in pairs]
    return pl.kernel(body, out_type=out_type, mesh=plsc.ScalarSubcoreMesh(axis_name="sequencer", num_cores=1),
                     name="rs_chip_exchange_behind",
                     scratch_types=[pltpu.SemaphoreType.DMA((3 * n,)), pltpu.SemaphoreType.DMA((3 * n,))],
                     compiler_params=pltpu.CompilerParams(collective_id=SEQUENCER_EXCHANGE_ID))(*pairs)


def _rs_final_add(place, own, rb, *, name):
    Rh, C = own.shape
    tr = _row_tile(Rh)
    nrb = Rh // tr

    def body(p_ref, o_ref, rb_ref, f_ref):
        f_ref[...] = ((o_ref[...] + rb_ref[0].astype(F32)) + rb_ref[1].astype(F32)) + rb_ref[2].astype(F32)

    grid_spec = pltpu.PrefetchScalarGridSpec(
        num_scalar_prefetch=1, grid=(nrb,),
        in_specs=[pl.BlockSpec((tr, C), lambda i, p: (i, 0)), pl.BlockSpec((3, tr, C), lambda i, p: (0, i, 0))],
        out_specs=pl.BlockSpec((tr, C), lambda i, p: (p[0] * nrb + i, 0)))
    return pl.pallas_call(
        body, name=name, grid_spec=grid_spec, out_shape=jax.ShapeDtypeStruct((2 * Rh, C), F32),
        compiler_params=pltpu.CompilerParams(dimension_semantics=("arbitrary",), vmem_limit_bytes=VMEM_LIMIT_V7X))(place, own, rb)


def _sum_slots(rs):
    n, rows, C = rs.shape

    def body(r_ref, o_ref):
        acc = r_ref[0]
        for k in range(1, n):
            acc = acc + r_ref[k]
        o_ref[...] = acc

    return _call(body, name="small_grad_sum", grid=(1,), in_specs=[pl.BlockSpec((n, rows, C), lambda i: (0, 0, 0))],
                 out_specs=pl.BlockSpec((rows, C), lambda i: (0, 0)), out_shape=jax.ShapeDtypeStruct((rows, C), F32),
                 sem=("arbitrary",))(rs)


def _rs_sibling_share(gbufs):
    n = len(gbufs)

    def body(*refs):
        g_refs = refs[n:2 * n]
        x, y, c, _ = _place()
        copy = _remote(*refs[-2:])
        halves = [_halves(g_refs[g], gbufs[g].shape[0]) for g in range(n)]
        outs = [copy(g, halves[g]((), c), halves[g]((), c), (x, y, 1 - c)) for g in range(n)]
        for cp in outs:
            cp.start()
        for g in range(n):
            copy(g, halves[g]((), 1 - c), halves[g]((), 1 - c), (x, y, c)).wait_recv()
        for cp in outs:
            cp.wait_send()

    return pl.pallas_call(body, name="rs_sibling_share", out_shape=[jax.ShapeDtypeStruct(a.shape, a.dtype) for a in gbufs],
                          in_specs=[ANY] * n, out_specs=[ANY] * n, input_output_aliases={g: g for g in range(n)},
                          scratch_shapes=[pltpu.SemaphoreType.DMA((n,)), pltpu.SemaphoreType.DMA((n,))])(*gbufs)


def _adamw(w, g, m, v, *, name, g_row=0):
    rows, cols = w.shape
    tr = rows
    for cand in range(min(rows, 512), 7, -8):
        if rows % cand == 0 and g_row % cand == 0:
            tr = cand
            break
    spec = pl.BlockSpec((tr, cols), lambda i: (i, 0))
    g_spec = pl.BlockSpec((tr, cols), lambda i: (g_row // tr + i, 0))

    def body(w_ref, g_ref, m_ref, v_ref, d_ref, nm_ref, nv_ref):
        gg = g_ref[...]
        nm = ADAM_B1 * m_ref[...] + (1.0 - ADAM_B1) * gg
        nv = ADAM_B2 * v_ref[...] + (1.0 - ADAM_B2) * (gg * gg)
        m_hat = nm / (1.0 - ADAM_B1 ** ADAM_STEP)
        v_hat = nv / (1.0 - ADAM_B2 ** ADAM_STEP)
        d_ref[...] = -ADAM_LR * (m_hat / (jnp.sqrt(v_hat) + ADAM_EPS) + ADAM_WD * w_ref[...])
        nm_ref[...] = nm
        nv_ref[...] = nv

    return _call(body, name=name, grid=(rows // tr,), in_specs=[spec, g_spec, spec, spec], out_specs=[spec] * 3,
                 out_shape=[jax.ShapeDtypeStruct((rows, cols), F32)] * 3, sem=("parallel",))(w, g, m, v)


WEIGHT_NAMES = ("mix_norm", "ab_w_in", "lru_conv_w", "lru_conv_b", "lru_wa", "lru_ba", "lru_wx", "lru_bx", "lru_lambda",
                "ab_w_out", "c_w_qkv", "c_b_qkv", "c_sinks", "c_w_out", "c_b_out", "xa_norm", "xa_mem_norm", "xa_wq",
                "xa_wkv", "xa_wo", "ffn_norm", "ffn_w_gate_up", "ffn_w_down", "final_norm")
EARLY_GROUPS = (("ab_w_in",), ("ab_w_out",), ("lru_wa", "lru_wx"))
LATE_GROUPS = (("c_w_out", "xa_wkv", "ffn_w_down"), ("ffn_w_gate_up",), ("xa_wo",), ("xa_wq",), ("c_w_qkv",))
GROUPS = EARLY_GROUPS + LATE_GROUPS
REPLICATED = ("mix_norm", "lru_conv_b", "lru_lambda", "c_sinks", "xa_norm", "xa_mem_norm", "ffn_norm", "final_norm")
SMALL_SHARDED = ("lru_conv_w", "lru_ba", "lru_bx", "c_b_qkv", "c_b_out")
LANES = 1024


def _rows(v):
    flat = v.reshape(-1)
    return jnp.pad(flat, (0, -flat.shape[0] % LANES)).reshape(-1, LANES)


def _pack_small(parts, total, *, name):
    def body(*refs):
        o_ref = refs[-1]
        o_ref[...] = jnp.zeros_like(o_ref)
        row = 0
        for p_ref in refs[:-1]:
            o_ref[row:row + p_ref.shape[0], :] = p_ref[...]
            row += p_ref.shape[0]

    return _call(body, name=name, grid=(1,), in_specs=[pl.BlockSpec(p.shape, lambda i: (0, 0)) for p in parts],
                 out_specs=pl.BlockSpec((total, LANES), lambda i: (0, 0)),
                 out_shape=jax.ShapeDtypeStruct((total, LANES), F32), sem=("arbitrary",))(*parts)


def _from_shards(name, t):
    minor = t.shape[-1]
    if name == "ab_w_in":
        return t
    if name in ("ab_w_out", "c_w_out"):
        return t.reshape(1, -1, minor)
    if name == "ffn_w_gate_up":
        return t.reshape(N_CHIPS, 2, -1, minor)
    if name in ("xa_wq", "xa_wkv", "ffn_w_down"):
        return t.reshape(N_CHIPS, 2, -1, minor).transpose(1, 0, 2, 3).reshape(2, -1, minor)
    if name in ("lru_wa", "lru_wx"):
        return t.reshape(N_CHIPS, LRU_HEADS, -1, minor).transpose(1, 0, 2, 3).reshape(LRU_HEADS, LRU_HEAD_DIM, minor)
    if name == "xa_wo":
        return t.reshape(N_CHIPS, 2, -1, minor).transpose(1, 0, 2, 3)
    assert name == "c_w_qkv"
    return t.transpose(1, 0, 2).reshape(1, D_MODEL, -1)


def _piece_shards(name, g):
    minor = g.shape[-1]
    if name in ("ab_w_in", "ffn_w_gate_up", "xa_wo"):
        return g
    if name in ("ab_w_out", "c_w_out", "xa_wq", "xa_wkv", "ffn_w_down"):
        return g.reshape(N_CHIPS, -1, minor)
    if name in ("lru_wa", "lru_wx"):
        return g.reshape(LRU_HEADS, N_CHIPS, -1, minor).transpose(1, 0, 2, 3).reshape(N_CHIPS, -1, minor)
    assert name == "c_w_qkv"
    return g.reshape(D_MODEL, N_CHIPS, -1).transpose(1, 0, 2)


RS_LAYER1 = ((("c_w_out", None), ("xa_wkv", 1), ("ffn_w_down", 1)), (("ffn_w_gate_up", 1),), (("xa_wo", 1),),
             (("xa_wq", 1),), (("c_w_qkv", None),))
RS_LAYER0 = ((("ab_w_out", None), ("xa_wkv", 0), ("ffn_w_down", 0)), (("ab_w_in", None),), (("ffn_w_gate_up", 0),),
             (("lru_wa", None), ("lru_wx", None), ("xa_wo", 0)), (("xa_wq", 0),))


def kernel(x, mem, mix_norm, ab_w_in, lru_conv_w, lru_conv_b, lru_wa, lru_ba, lru_wx, lru_bx, lru_lambda, ab_w_out, c_w_qkv, c_b_qkv, c_sinks, c_w_out, c_b_out, xa_norm, xa_mem_norm, xa_wq, xa_wkv, xa_wo, ffn_norm, ffn_w_gate_up, ffn_w_down, final_norm, loss_target, m_mix_norm, m_ab_w_in, m_lru_conv_w, m_lru_conv_b, m_lru_wa, m_lru_ba, m_lru_wx, m_lru_bx, m_lru_lambda, m_ab_w_out, m_c_w_qkv, m_c_b_qkv, m_c_sinks, m_c_w_out, m_c_b_out, m_xa_norm, m_xa_mem_norm, m_xa_wq, m_xa_wkv, m_xa_wo, m_ffn_norm, m_ffn_w_gate_up, m_ffn_w_down, m_final_norm, v_mix_norm, v_ab_w_in, v_lru_conv_w, v_lru_conv_b, v_lru_wa, v_lru_ba, v_lru_wx, v_lru_bx, v_lru_lambda, v_ab_w_out, v_c_w_qkv, v_c_b_qkv, v_c_sinks, v_c_w_out, v_c_b_out, v_xa_norm, v_xa_mem_norm, v_xa_wq, v_xa_wkv, v_xa_wo, v_ffn_norm, v_ffn_w_gate_up, v_ffn_w_down, v_final_norm):
    given = dict(locals())
    wl = {n: given[n] for n in WEIGHT_NAMES}
    ml = {n: given["m_" + n] for n in WEIGHT_NAMES}
    vl = {n: given["v_" + n] for n in WEIGHT_NAMES}
    xi, yi, ci = lax.axis_index("x"), lax.axis_index("y"), lax.axis_index("c")
    chip = 2 * xi + yi

    def join(parts, axis):
        return parts[0] if len(parts) == 1 else jnp.concatenate(parts, axis=axis)

    local_rows = {n: wl[n].size // wl[n].shape[-1] for grp in GROUPS for n in grp}
    packs = [join([wl[n].astype(BF16).reshape(local_rows[n], wl[n].shape[-1]) for n in grp], 0) for grp in GROUPS]
    spack = _pack_small([_rows(wl[n]) for n in SMALL_SHARDED], 8, name="pack_small_weights")
    n_early = len(EARLY_GROUPS)
    early, sfull = _gather_weights(packs[:n_early], spack)
    early, sfull, late_packs = lax.optimization_barrier((early, sfull, packs[n_early:]))
    gathered = early + _gather_weights_behind(late_packs)
    w = {n: wl[n] for n in REPLICATED}
    w["c_sinks"] = wl["c_sinks"][0]
    for grp, full in zip(GROUPS, gathered):
        off = 0
        for n in grp:
            w[n] = _from_shards(n, full if len(grp) == 1 else full[:, off:off + local_rows[n]])
            off += local_rows[n]
    for r, n in enumerate(SMALL_SHARDED):
        loc = wl[n].shape[1:]
        t = sfull[:, r, :wl[n].size].reshape((N_CHIPS,) + loc)
        if n == "lru_conv_w":
            w[n] = t.transpose(1, 0, 2).reshape(CONV_WIDTH, -1)
        elif n in ("lru_ba", "lru_bx"):
            w[n] = t.transpose(1, 0, 2).reshape(1, -1)
        else:
            w[n] = t.reshape(1, -1)

    place = jnp.stack([ci, chip]).astype(jnp.int32)

    def pair_stage(spec, g, tag):
        piece = lambda n, l: (g[n] if l is None else g[n, l]).astype(BF16)
        gpacks = [join([_piece_shards(n, piece(n, l)) for n, l in grp], 1) for grp in spec]
        ras = _rs_pair_exchange(gpacks, name=f"rs_pair_exchange_{tag}")
        sums = [_rs_pair_add(place, gp, ra, name=f"rs_pair_add_{tag}_{i}") for i, (gp, ra) in enumerate(zip(gpacks, ras))]
        return [pair for pair, _ in sums], [own for _, own in sums]

    layer1 = {}

    def reduce_layer1(g):
        pairs, layer1["own"] = pair_stage(RS_LAYER1, g, "l1")
        layer1["rb"] = _rs_chip_exchange_behind(pairs)

    loss_part, grad_x, g = _device_step(x[0], mem[0], loss_target[0], w, after_layer1=reduce_layer1)

    small_parts = [_rows(g[n]) for n in REPLICATED] + [_rows(jnp.broadcast_to(loss_part, (LANES,)))]
    small_parts += [_rows(g[n]) for n in SMALL_SHARDED]
    small = _pack_small(small_parts, 24, name="pack_small_grads")
    pairs0, own0 = pair_stage(RS_LAYER0, g, "l0")
    *rb0, rs = _rs_chip_exchange(pairs0, small)
    gsums = _rs_sibling_share([_rs_final_add(place, own, rb, name=f"rs_final_add_{i}")
                               for i, (own, rb) in enumerate(zip(layer1["own"] + own0, list(layer1["rb"]) + rb0))])
    ssum = _sum_slots(rs)

    where = {}
    for grp, gsum in zip(RS_LAYER1 + RS_LAYER0, gsums):
        off = 0
        for n, l in grp:
            rows = local_rows[n] if l is None else local_rows[n] // 2
            where[n, l] = (gsum, off, rows, len(grp) == 1)
            off += rows
    take = lambda gsum, off, rows, whole: gsum if whole else gsum[off:off + rows]
    grads, grad_rows = {}, {}
    for grp in GROUPS:
        for n in grp:
            if (n, None) in where:
                grads[n] = take(*where[n, None]).reshape(wl[n].shape)
                grad_rows[n] = where[n, None][:2]
            else:
                grads[n] = jnp.stack([take(*where[n, l]).reshape(wl[n].shape[1:]) for l in range(2)])
                grad_rows[n] = (grads[n].reshape(local_rows[n], wl[n].shape[-1]), 0)
    row = 0
    for n in REPLICATED:
        k = _rows(g[n]).shape[0]
        grads[n] = ssum[row:row + k].reshape(-1)[:wl[n].size].reshape(wl[n].shape)
        row += k
    loss = ssum[row, 0]
    row += 1
    for n in SMALL_SHARDED:
        k = _rows(g[n]).shape[0]
        full = ssum[row:row + k].reshape(-1)[:g[n].size]
        row += k
        loc = wl[n].shape
        if n == "lru_conv_w":
            sh = full.reshape(CONV_WIDTH, N_CHIPS, -1)
        elif n in ("lru_ba", "lru_bx"):
            sh = full.reshape(LRU_HEADS, N_CHIPS, -1)
        else:
            sh = full.reshape(1, N_CHIPS, -1)
        grads[n] = lax.dynamic_index_in_dim(sh, chip, axis=1, keepdims=False).reshape(loc)

    delta, new_m, new_v = {}, {}, {}
    for n, (gsum, off) in grad_rows.items():
        shape2 = (local_rows[n], wl[n].shape[-1])
        d, nm, nv = _adamw(wl[n].reshape(shape2), gsum, ml[n].reshape(shape2), vl[n].reshape(shape2), g_row=off,
                           name="adamw_" + n)
        delta[n], new_m[n], new_v[n] = (t.reshape(wl[n].shape) for t in (d, nm, nv))
    smalls = REPLICATED + SMALL_SHARDED
    packs = [_pack_small([_rows(src[n]) for n in smalls], 24, name="pack_adamw_" + tag)
             for tag, src in (("w", wl), ("g", grads), ("m", ml), ("v", vl))]
    outs = _adamw(*packs, name="adamw_small")
    row = 0
    for n in smalls:
        k = _rows(wl[n]).shape[0]
        for dst, o in zip((delta, new_m, new_v), outs):
            dst[n] = o[row:row + k].reshape(-1)[:wl[n].size].reshape(wl[n].shape)
        row += k

    return (loss, grad_x[None], *[grads[n] for n in WEIGHT_NAMES], *[delta[n] for n in WEIGHT_NAMES],
            *[new_m[n] for n in WEIGHT_NAMES], *[new_v[n] for n in WEIGHT_NAMES])
```

```python
import jax
import jax.numpy as jnp
from jax import lax
from jax.experimental import pallas as pl
from jax.experimental.pallas import tpu as pltpu
from jax.experimental.pallas import tpu_sc as plsc

F32, BF16 = jnp.float32, jnp.bfloat16
D_MODEL = 1024
NORM_EPS = 1e-6
ROPE_THETA = 500000.0
HEAD_DIM = 64
ROT_DIM = 16
BLK = 128
LRU_HEADS, LRU_HEAD_DIM, CONV_WIDTH, LRU_C = 4, 256, 4, 8.0
DILATED_PATTERN = ((128, 1), (512, 4), (2048, 16))
B_HEADS, C_HEADS, C_KV_HEADS, C_WINDOW = 8, 16, 2, 128
XA_HEADS, XA_HEAD_DIM, N_MEM = 4, 128, 256
D_FF = 2816
NEG = -1e30
ADAM_LR, ADAM_B1, ADAM_B2, ADAM_EPS, ADAM_WD, ADAM_STEP = 0.001, 0.9, 0.999, 1e-08, 0.01, 10
N_CHIPS = 4
VMEM_LIMIT_V7X = 56 * 1024 * 1024

NN = (((1,), (0,)), ((), ()))
NT = (((1,), (1,)), ((), ()))
TN = (((0,), (0,)), ((), ()))


def _dot(a, b, dims=NN):
    return lax.dot_general(a, b, dims, preferred_element_type=F32)


def _sigmoid(x):
    return 1.0 / (1.0 + jnp.exp(-x))


def _call(body, *, name, grid, in_specs, out_specs, out_shape, scratch=(), sem=None):
    return pl.pallas_call(
        body, name=name, grid=grid, in_specs=in_specs, out_specs=out_specs, out_shape=out_shape,
        scratch_shapes=list(scratch),
        compiler_params=pltpu.CompilerParams(dimension_semantics=sem, vmem_limit_bytes=VMEM_LIMIT_V7X))


def _rope_tables(L):
    half = ROT_DIM // 2
    inv = ROPE_THETA ** (-jnp.arange(0, ROT_DIM, 2, dtype=F32) / ROT_DIM)
    ang = jnp.arange(L, dtype=F32)[:, None] * inv[None, :]
    cos, sin = jnp.cos(ang), jnp.sin(ang)
    rest = HEAD_DIM - ROT_DIM
    z8, zr, one = jnp.zeros((L, half), F32), jnp.zeros((L, rest), F32), jnp.ones((L, rest), F32)
    c = jnp.concatenate([cos, cos, one], axis=1)
    s1 = jnp.concatenate([-sin, z8, zr], axis=1)
    s2 = jnp.concatenate([z8, sin, zr], axis=1)
    return tuple(jnp.concatenate([t, t], axis=1) for t in (c, s1, s2))


def _rope_fwd(v, c, s1, s2):
    return v * c + pltpu.roll(v, 120, 1) * s1 + pltpu.roll(v, 8, 1) * s2


def _rope_bwd(dv, c, s1, s2):
    return dv * c + pltpu.roll(dv * s1, 8, 1) + pltpu.roll(dv * s2, 120, 1)


def _weight_spec(w, layer):
    if layer is None:
        return w.shape, pl.BlockSpec(w.shape, lambda i: (0, 0, 0))
    S, _, K, Ns = w.shape
    return (S, K, Ns), pl.BlockSpec((S, None, K, Ns), lambda i: (0, layer, 0, 0))


def _rowmm(a, w3, *, name, tm=256, gain=None, bias=None, res=None, swiglu=False, rope=None, layer=None):
    M, K = a.shape
    (S, _, Ns), w_spec = _weight_spec(w3, layer)
    N = S * Ns
    tm = min(tm, M)
    has_norm, has_bias, has_res, has_rope = gain is not None, bias is not None, res is not None, rope is not None
    row = lambda w: pl.BlockSpec((tm, w), lambda i: (i, 0))
    whole = lambda shape: pl.BlockSpec(shape, lambda i: (0,) * len(shape))
    ins, specs = [a], [row(K)]
    if has_norm:
        ins.append(gain.reshape(1, K)); specs.append(whole((1, K)))
    ins.append(w3); specs.append(w_spec)
    if has_bias:
        ins.append(bias.reshape(1, N)); specs.append(whole((1, N)))
    if has_res:
        ins.append(res); specs.append(row(N))
    if has_rope:
        ins += list(rope[2]); specs += [row(128)] * 3
    y_dtype = F32 if has_res else BF16
    out_shape, out_specs = [jax.ShapeDtypeStruct((M, N), y_dtype)], [row(N)]
    if has_norm:
        out_shape.append(jax.ShapeDtypeStruct((M, K), BF16)); out_specs.append(row(K))
    if swiglu:
        out_shape.append(jax.ShapeDtypeStruct((M, N // 2), BF16)); out_specs.append(row(N // 2))
    scratch = [pltpu.VMEM((tm, N), F32)] if has_rope else []

    def body(*refs):
        it = iter(refs)
        a_ref = next(it)
        g_ref = next(it) if has_norm else None
        w_ref = next(it)
        b_ref = next(it) if has_bias else None
        r_ref = next(it) if has_res else None
        tabs = [next(it) for _ in range(3)] if has_rope else None
        y_ref = next(it)
        n_ref = next(it) if has_norm else None
        act_ref = next(it) if swiglu else None
        ys_ref = next(it) if has_rope else None
        if has_norm:
            x = a_ref[...].astype(F32)
            ms = jnp.mean(x * x, axis=-1, keepdims=True)
            xb = (x * lax.rsqrt(ms + NORM_EPS) * g_ref[...]).astype(BF16)
            n_ref[...] = xb
        else:
            xb = a_ref[...].astype(BF16)
        if swiglu:
            for s in range(S // 2):
                g = _dot(xb, w_ref[s])
                u = _dot(xb, w_ref[s + S // 2])
                y_ref[:, s * Ns:(s + 1) * Ns] = g.astype(BF16)
                y_ref[:, N // 2 + s * Ns:N // 2 + (s + 1) * Ns] = u.astype(BF16)
                act_ref[:, s * Ns:(s + 1) * Ns] = (g * _sigmoid(g) * u).astype(BF16)
            return
        for s in range(S):
            sl = slice(s * Ns, (s + 1) * Ns)
            acc = _dot(xb, w_ref[s])
            if has_bias:
                acc = acc + b_ref[:, sl]
            if has_res:
                acc = acc + r_ref[:, sl]
            if has_rope:
                ys_ref[:, sl] = acc
            else:
                y_ref[:, sl] = acc.astype(y_dtype)
        if has_rope:
            c, s1, s2 = (t[...] for t in tabs)
            for cb in range(N // 128):
                cs = slice(cb * 128, (cb + 1) * 128)
                v = ys_ref[:, cs]
                if rope[0] <= cb * 128 < rope[1]:
                    v = _rope_fwd(v, c, s1, s2)
                y_ref[:, cs] = v.astype(BF16)

    return _call(body, name=name, grid=(M // tm,), in_specs=specs, out_specs=out_specs, out_shape=out_shape,
                 scratch=scratch, sem=("parallel",))(*ins)


def _mm_nt(dy, w3, *, name, mode, tm=256, kchunk=None, h=None, gain=None, dh=None, gu=None, layer=None):
    M, N = dy.shape
    (S, K, Ns), w_spec = _weight_spec(w3, layer)
    kchunk = kchunk or K
    tm = min(tm, M)
    row = lambda w: pl.BlockSpec((tm, w), lambda i: (i, 0))
    whole = lambda shape: pl.BlockSpec(shape, lambda i: (0,) * len(shape))
    ins, specs = [dy, w3], [row(N), w_spec]
    has_dh = dh is not None
    if mode == "norm":
        ins += [h, gain.reshape(1, K)]; specs += [row(K), whole((1, K))]
        if has_dh:
            ins.append(dh); specs.append(row(K))
        out_shape = [jax.ShapeDtypeStruct((M, K), F32), jax.ShapeDtypeStruct((1, K), F32)]
        out_specs = [row(K), whole((1, K))]
    elif mode == "swiglu":
        ins.append(gu); specs.append(row(2 * K))
        out_shape, out_specs = [jax.ShapeDtypeStruct((M, 2 * K), BF16)], [row(2 * K)]
    else:
        out_shape, out_specs = [jax.ShapeDtypeStruct((M, K), BF16)], [row(K)]

    def body(*refs):
        it = iter(refs)
        dy_ref, w_ref = next(it), next(it)
        if mode == "norm":
            h_ref, g_ref = next(it), next(it)
            dh_ref = next(it) if has_dh else None
            o_ref, dg_ref = next(it), next(it)
        elif mode == "swiglu":
            gu_ref, o_ref = next(it), next(it)
        else:
            o_ref = next(it)
        for kc in range(K // kchunk):
            ks = slice(kc * kchunk, (kc + 1) * kchunk)
            acc = None
            for s in range(S):
                t = _dot(dy_ref[:, s * Ns:(s + 1) * Ns].astype(BF16), w_ref[s, ks, :], NT)
                acc = t if acc is None else acc + t
            if mode == "plain":
                o_ref[:, ks] = acc.astype(BF16)
            elif mode == "swiglu":
                us = slice(K + kc * kchunk, K + (kc + 1) * kchunk)
                g = gu_ref[:, ks].astype(F32)
                u = gu_ref[:, us].astype(F32)
                sg = _sigmoid(g)
                o_ref[:, ks] = (acc * u * (sg * (1.0 + g * (1.0 - sg)))).astype(BF16)
                o_ref[:, us] = (acc * (g * sg)).astype(BF16)
            else:
                x = h_ref[...].astype(F32)
                r = lax.rsqrt(jnp.mean(x * x, axis=-1, keepdims=True) + NORM_EPS)
                xhat = x * r
                dxh = acc * g_ref[...]
                dx = r * (dxh - xhat * jnp.mean(dxh * xhat, axis=-1, keepdims=True))
                o_ref[...] = dx + dh_ref[...] if has_dh else dx

                @pl.when(pl.program_id(0) == 0)
                def _():
                    dg_ref[...] = jnp.zeros_like(dg_ref)

                dg_ref[...] += jnp.sum(acc * xhat, axis=0, keepdims=True)

    sem = ("arbitrary",) if mode == "norm" else ("parallel",)
    return _call(body, name=name, grid=(M // tm,), in_specs=specs, out_specs=out_specs, out_shape=out_shape, sem=sem)(*ins)


def _mm_tn(x, dy, *, S, name, tk=1024, kk=None, bias=False):
    M, K = x.shape
    N = dy.shape[1]
    Ns = N // S
    kk = kk or K
    tk = min(tk, M)
    nl = M // tk
    in_specs = [pl.BlockSpec((tk, kk), lambda s, kc, l: (l, kc)), pl.BlockSpec((tk, Ns), lambda s, kc, l: (l, s))]
    out_shape = [jax.ShapeDtypeStruct((S, K, Ns), BF16)]
    out_specs = [pl.BlockSpec((None, kk, Ns), lambda s, kc, l: (s, kc, 0))]
    if bias:
        out_shape.append(jax.ShapeDtypeStruct((1, N), F32))
        out_specs.append(pl.BlockSpec((1, Ns), lambda s, kc, l: (0, s)))

    def body(x_ref, dy_ref, o_ref, *rest):
        acc_ref = rest[-1]
        kc, l = pl.program_id(1), pl.program_id(2)

        @pl.when(l == 0)
        def _():
            acc_ref[...] = jnp.zeros_like(acc_ref)

        acc_ref[...] += _dot(x_ref[...].astype(BF16), dy_ref[...].astype(BF16), TN)
        if bias:
            b_ref = rest[0]

            @pl.when((kc == 0) & (l == 0))
            def _():
                b_ref[...] = jnp.zeros_like(b_ref)

            @pl.when(kc == 0)
            def _():
                b_ref[...] += jnp.sum(dy_ref[...].astype(F32), axis=0, keepdims=True)

        @pl.when(l == nl - 1)
        def _():
            o_ref[...] = acc_ref[...].astype(BF16)

    return _call(body, name=name, grid=(S, K // kk, nl), in_specs=in_specs, out_specs=out_specs, out_shape=out_shape,
                 scratch=[pltpu.VMEM((kk, Ns), F32)], sem=("arbitrary", "arbitrary", "arbitrary"))(x, dy)


def _band_bias(max_dist, has_prev):
    rows = lax.broadcasted_iota(jnp.int32, (BLK, 2 * BLK), 0)
    cols = lax.broadcasted_iota(jnp.int32, (BLK, 2 * BLK), 1)
    dist = rows - cols + BLK
    ok = (dist >= 0) & (dist <= max_dist) & ((cols >= BLK) | has_prev)
    return jnp.where(ok, 0.0, NEG)


Q_SCALE = HEAD_DIM ** -0.5


def _band_fwd(qa, ka, va, *, d, nq, nkv, qcol, kcol, vcol, max_dist, sinks=None, name):
    Lr = qa.shape[0]
    nb = Lr // BLK
    qw, kw, G = nq * HEAD_DIM, nkv * HEAD_DIM, nq // nkv
    cur = lambda colf, w: pl.BlockSpec((BLK, w), lambda r, i: (i, colf(r)))
    prv = lambda colf, w: pl.BlockSpec((BLK, w), lambda r, i: (jnp.maximum(i - 1, 0), colf(r)))
    out = pl.BlockSpec((BLK, qw), lambda r, i: (i, r))
    ins, specs = [qa, ka, ka, va, va], [cur(qcol, qw), cur(kcol, kw), prv(kcol, kw), cur(vcol, kw), prv(vcol, kw)]
    has_sinks = sinks is not None
    if has_sinks:
        ins.append(sinks); specs.append(pl.BlockSpec(memory_space=pltpu.SMEM))

    def body(*refs):
        q_ref, kc_ref, kp_ref, vc_ref, vp_ref = refs[:5]
        sk_ref = refs[5] if has_sinks else None
        o_ref, lse_ref = refs[-2], refs[-1]
        bias = _band_bias(max_dist, pl.program_id(1) > 0)
        k2 = jnp.concatenate([kp_ref[...], kc_ref[...]], axis=0)
        v2 = jnp.concatenate([vp_ref[...], vc_ref[...]], axis=0)
        for h in range(nq):
            hs = slice(h * HEAD_DIM, (h + 1) * HEAD_DIM)
            ks = slice((h // G) * HEAD_DIM, (h // G + 1) * HEAD_DIM)
            s = _dot(q_ref[:, hs] * jnp.asarray(Q_SCALE, BF16), k2[:, ks], NT) + bias
            m = jnp.max(s, axis=-1, keepdims=True)
            if has_sinks:
                m = jnp.maximum(m, sk_ref[h])
            p = jnp.exp(s - m)
            l = jnp.sum(p, axis=-1, keepdims=True)
            if has_sinks:
                l = l + jnp.exp(sk_ref[h] - m)
            o_ref[:, hs] = (_dot(p.astype(BF16), v2[:, ks]) / l).astype(BF16)
            lse_ref[:, hs] = jnp.broadcast_to(m + jnp.log(l), (BLK, HEAD_DIM))

    return _call(body, name=name, grid=(d, nb), in_specs=specs, out_specs=[out, out],
                 out_shape=[jax.ShapeDtypeStruct((Lr, d * qw), BF16), jax.ShapeDtypeStruct((Lr, d * qw), F32)],
                 sem=("parallel", "parallel"))(*ins)


def _band_bwd(qa, ka, va, doa, oa, lsea, *, d, nq, nkv, qcol, kcol, vcol, docol, max_dist, sinks=None, name):
    Lr = qa.shape[0]
    nb = Lr // BLK
    qw, kw, G = nq * HEAD_DIM, nkv * HEAD_DIM, nq // nkv
    transposed = G > 1
    last = lambda i: jnp.minimum(i, nb - 1)
    cur = lambda colf, w: pl.BlockSpec((BLK, w), lambda r, i: (last(i), colf(r)))
    prv = lambda colf, w: pl.BlockSpec((BLK, w), lambda r, i: (jnp.maximum(last(i) - 1, 0), colf(r)))
    own = lambda r: r
    ins = [qa, ka, ka, va, va, doa, oa, lsea]
    specs = [cur(qcol, qw), cur(kcol, kw), prv(kcol, kw), cur(vcol, kw), prv(vcol, kw), cur(docol, qw), cur(own, qw),
             cur(own, qw)]
    has_sinks = sinks is not None
    if has_sinks:
        ins.append(sinks); specs.append(pl.BlockSpec(memory_space=pltpu.SMEM))
    out_shape = [jax.ShapeDtypeStruct((Lr, d * qw), F32), jax.ShapeDtypeStruct((Lr, d * kw), F32),
                 jax.ShapeDtypeStruct((Lr, d * kw), F32)]
    behind = lambda r, i: (jnp.maximum(i - 1, 0), r)
    out_specs = [pl.BlockSpec((BLK, qw), lambda r, i: (last(i), r)), pl.BlockSpec((BLK, kw), behind),
                 pl.BlockSpec((BLK, kw), behind)]
    if has_sinks:
        out_shape.append(jax.ShapeDtypeStruct((8, 128), F32))
        out_specs.append(pl.BlockSpec((8, 128), lambda r, i: (0, 0)))

    def body(*refs):
        it = iter(refs)
        q_ref, kc_ref, kp_ref, vc_ref, vp_ref, do_ref, o_ref, ls_ref = (next(it) for _ in range(8))
        sk_ref = next(it) if has_sinks else None
        dq_ref, dk_ref, dv_ref = next(it), next(it), next(it)
        dsk_ref = next(it) if has_sinks else None
        dk_car, dv_car = next(it), next(it)
        r_id, i = pl.program_id(0), pl.program_id(1)

        @pl.when(i == 0)
        def _():
            dk_car[...] = jnp.zeros_like(dk_car)
            dv_car[...] = jnp.zeros_like(dv_car)

        if has_sinks:
            @pl.when((r_id == 0) & (i == 0))
            def _():
                dsk_ref[...] = jnp.zeros_like(dsk_ref)

        @pl.when(i == nb)
        def _():
            dk_ref[...] = dk_car[...]
            dv_ref[...] = dv_car[...]

        @pl.when(i < nb)
        def _():
            bias = _band_bias(max_dist, i > 0)
            k2 = jnp.concatenate([kp_ref[...], kc_ref[...]], axis=0)
            v2 = jnp.concatenate([vp_ref[...], vc_ref[...]], axis=0)
            if has_sinks:
                lane = lax.broadcasted_iota(jnp.int32, (8, 128), 1)
                dsk = jnp.zeros((8, 128), F32)
            for kv in range(nkv):
                ks = slice(kv * HEAD_DIM, (kv + 1) * HEAD_DIM)
                kh, vh = k2[:, ks], v2[:, ks]
                shape = (HEAD_DIM, 2 * BLK) if transposed else (2 * BLK, HEAD_DIM)
                dk, dv = jnp.zeros(shape, F32), jnp.zeros(shape, F32)
                for g in range(G):
                    h = kv * G + g
                    hs = slice(h * HEAD_DIM, (h + 1) * HEAD_DIM)
                    q = q_ref[:, hs] * jnp.asarray(Q_SCALE, BF16)
                    do = do_ref[:, hs]
                    lse = ls_ref[:, h * HEAD_DIM:h * HEAD_DIM + 1]
                    dl = jnp.sum(do.astype(F32) * o_ref[:, hs].astype(F32), axis=-1, keepdims=True)
                    p = jnp.exp(_dot(q, kh, NT) + bias - lse)
                    ds = (p * (_dot(do, vh, NT) - dl)).astype(BF16)
                    dq_ref[:, hs] = _dot(ds, kh) * Q_SCALE
                    if transposed:
                        dk = dk + _dot(q, ds, TN)
                        dv = dv + _dot(do, p.astype(BF16), TN)
                    else:
                        dk = dk + _dot(ds, q, TN)
                        dv = dv + _dot(p.astype(BF16), do, TN)
                    if has_sinks:
                        val = -jnp.sum(jnp.exp(sk_ref[h] - lse) * dl, axis=0, keepdims=True)
                        dsk = dsk + jnp.where(lane == h, val, 0.0)
                if transposed:
                    dk, dv = dk.T, dv.T
                dk_ref[:, ks] = dk_car[:, ks] + dk[:BLK]
                dv_ref[:, ks] = dv_car[:, ks] + dv[:BLK]
                dk_car[:, ks] = dk[BLK:]
                dv_car[:, ks] = dv[BLK:]
            if has_sinks:
                dsk_ref[...] += dsk

    return _call(body, name=name, grid=(d, nb + 1), in_specs=specs, out_specs=out_specs, out_shape=out_shape,
                 scratch=[pltpu.VMEM((BLK, kw), F32), pltpu.VMEM((BLK, kw), F32)], sem=("arbitrary", "arbitrary"))(*ins)


def _attn_grad_combine(branches, tabs, *, name, tm=256):
    L, qw = branches[0][0].shape
    kw = branches[0][1].shape[1]
    nbr = len(branches)
    row = lambda w: pl.BlockSpec((tm, w), lambda i: (i, 0))
    ins, specs = [], []
    for dq, dk, dv in branches:
        ins += [dq, dk, dv]; specs += [row(qw), row(kw), row(kw)]
    ins += list(tabs); specs += [row(128)] * 3

    def body(*refs):
        c, s1, s2 = (t[...] for t in refs[3 * nbr:3 * nbr + 3])
        o_ref = refs[-1]
        for part, (w, off, rot) in enumerate(((qw, 0, True), (kw, qw, True), (kw, qw + kw, False))):
            for cb in range(w // 128):
                cs = slice(cb * 128, (cb + 1) * 128)
                v = refs[part][:, cs]
                for b in range(1, nbr):
                    v = v + refs[3 * b + part][:, cs]
                if rot:
                    v = _rope_bwd(v, c, s1, s2)
                o_ref[:, off + cb * 128:off + (cb + 1) * 128] = v.astype(BF16)

    return _call(body, name=name, grid=(L // tm,), in_specs=specs, out_specs=row(qw + 2 * kw),
                 out_shape=jax.ShapeDtypeStruct((L, qw + 2 * kw), BF16), sem=("parallel",))(*ins)


def _xattn_fwd(q, kv, *, name, tq=512):
    L, W = q.shape
    scale = XA_HEAD_DIM ** -0.5
    row = pl.BlockSpec((tq, W), lambda i: (i, 0))
    kvs = pl.BlockSpec((N_MEM, 2 * W), lambda i: (0, 0))

    def body(q_ref, kv_ref, o_ref, lse_ref):
        for h in range(XA_HEADS):
            hs = slice(h * XA_HEAD_DIM, (h + 1) * XA_HEAD_DIM)
            vs = slice(W + h * XA_HEAD_DIM, W + (h + 1) * XA_HEAD_DIM)
            s = _dot(q_ref[:, hs], kv_ref[:, hs], NT) * scale
            m = jnp.max(s, axis=-1, keepdims=True)
            p = jnp.exp(s - m)
            l = jnp.sum(p, axis=-1, keepdims=True)
            o_ref[:, hs] = (_dot(p.astype(BF16), kv_ref[:, vs]) / l).astype(BF16)
            lse_ref[:, hs] = jnp.broadcast_to(m + jnp.log(l), (tq, XA_HEAD_DIM))

    return _call(body, name=name, grid=(L // tq,), in_specs=[row, kvs], out_specs=[row, row],
                 out_shape=[jax.ShapeDtypeStruct((L, W), BF16), jax.ShapeDtypeStruct((L, W), F32)], sem=("parallel",))(q, kv)


def _xattn_bwd(q, kv, o, lse, do, *, name, tq=512):
    L, W = q.shape
    scale = XA_HEAD_DIM ** -0.5
    row = pl.BlockSpec((tq, W), lambda i: (i, 0))
    kvs = pl.BlockSpec((N_MEM, 2 * W), lambda i: (0, 0))

    def body(q_ref, kv_ref, o_ref, lse_ref, do_ref, dq_ref, dkv_ref):
        @pl.when(pl.program_id(0) == 0)
        def _():
            dkv_ref[...] = jnp.zeros_like(dkv_ref)

        for h in range(XA_HEADS):
            hs = slice(h * XA_HEAD_DIM, (h + 1) * XA_HEAD_DIM)
            vs = slice(W + h * XA_HEAD_DIM, W + (h + 1) * XA_HEAD_DIM)
            qh, kh, vh, doh = q_ref[:, hs], kv_ref[:, hs], kv_ref[:, vs], do_ref[:, hs]
            p = jnp.exp(_dot(qh, kh, NT) * scale - lse_ref[:, h * XA_HEAD_DIM:h * XA_HEAD_DIM + 1])
            dl = jnp.sum(doh.astype(F32) * o_ref[:, hs].astype(F32), axis=-1, keepdims=True)
            ds = (p * (_dot(doh, vh, NT) - dl) * scale).astype(BF16)
            dq_ref[:, hs] = _dot(ds, kh).astype(BF16)
            dkv_ref[:, hs] += _dot(ds, qh, TN)
            dkv_ref[:, vs] += _dot(p.astype(BF16), doh, TN)

    return _call(body, name=name, grid=(L // tq,), in_specs=[row, kvs, row, row, row], out_specs=[row, kvs],
                 out_shape=[jax.ShapeDtypeStruct((L, W), BF16), jax.ShapeDtypeStruct((N_MEM, 2 * W), F32)],
                 sem=("arbitrary",))(q, kv, o, lse, do)


def _neg_expm1(z):
    series = -(z * (1.0 + z * (0.5 + z * (1.0 / 6.0 + z * (1.0 / 24.0 + z * (1.0 / 120.0))))))
    return jnp.where(z > -0.05, series, 1.0 - jnp.exp(z))


def _softplus(z):
    return jnp.maximum(z, 0.0) + jnp.log(1.0 + jnp.exp(-jnp.abs(z)))


def _gelu_parts(y):
    c = 0.7978845608028654
    t = jnp.tanh(c * (y + 0.044715 * y * y * y))
    gy = 0.5 * y * (1.0 + t)
    dgy = 0.5 * (1.0 + t) + 0.5 * y * (1.0 - t * t) * c * (1.0 + 3.0 * 0.044715 * y * y)
    return gy, dgy


def _lru_gates(xc, wa_ref, ba, wx_ref, bx, sp):
    rs, igs = [], []
    for hd in range(LRU_HEADS):
        sl = slice(hd * LRU_HEAD_DIM, (hd + 1) * LRU_HEAD_DIM)
        xh = xc[:, sl].astype(BF16)
        rs.append(_sigmoid(_dot(xh, wa_ref[hd]) + ba[:, sl]))
        igs.append(_sigmoid(_dot(xh, wx_ref[hd]) + bx[:, sl]))
    r, ig = jnp.concatenate(rs, axis=1), jnp.concatenate(igs, axis=1)
    la = -LRU_C * r * sp
    return r, ig, jnp.exp(la), _neg_expm1(2.0 * la)


def _conv_taps(x_ext, halo):
    n = x_ext.shape[0]
    return [x_ext[halo:] if k == CONV_WIDTH - 1 else pltpu.roll(x_ext, CONV_WIDTH - 1 - k, 0)[halo:]
            for k in range(CONV_WIDTH)]


def _lru_fwd(proj, cw, cb, wa, ba, wx, bx, lam, *, name, tc=512):
    L = proj.shape[0]
    W = LRU_HEADS * LRU_HEAD_DIM
    nb = L // tc
    whole = lambda shape: pl.BlockSpec(shape, lambda i: (0,) * len(shape))
    specs = [pl.BlockSpec((tc, W), lambda i: (i, 0)), pl.BlockSpec((tc, W), lambda i: (i, 1)),
             pl.BlockSpec((16, W), lambda i: (jnp.maximum(i * (tc // 16) - 1, 0), 0)),
             whole((CONV_WIDTH, W)), whole((1, W)), whole((LRU_HEADS, LRU_HEAD_DIM, LRU_HEAD_DIM)), whole((1, W)),
             whole((LRU_HEADS, LRU_HEAD_DIM, LRU_HEAD_DIM)), whole((1, W)), whole((1, W))]
    out_specs = [pl.BlockSpec((tc, W), lambda i: (i, 0))] * 2
    out_shape = [jax.ShapeDtypeStruct((L, W), BF16), jax.ShapeDtypeStruct((L, W), F32)]

    def body(x_ref, y_ref, xh_ref, cw_ref, cb_ref, wa_ref, ba_ref, wx_ref, bx_ref, lam_ref, rec_ref, hs_ref,
             hcar, a_scr, b_scr):
        i = pl.program_id(0)

        @pl.when(i == 0)
        def _():
            hcar[...] = jnp.zeros_like(hcar)

        halo = jnp.where(i > 0, xh_ref[...].astype(F32), 0.0)
        taps = _conv_taps(jnp.concatenate([halo, x_ref[...].astype(F32)], axis=0), 16)
        xc = cb_ref[...] + sum(cw_ref[k:k + 1, :] * taps[k] for k in range(CONV_WIDTH))
        _, ig, a, om = _lru_gates(xc, wa_ref, ba_ref[...], wx_ref, bx_ref[...], _softplus(-lam_ref[...]))
        b = jnp.sqrt(om) * (ig * xc)
        rowmod = lax.broadcasted_iota(jnp.int32, (tc, W), 0) & 7
        for s in (1, 2, 4):
            keep = rowmod >= s
            b = jnp.where(keep, a * pltpu.roll(b, s, 0) + b, b)
            a = jnp.where(keep, a * pltpu.roll(a, s, 0), a)
        a_scr[...] = a
        b_scr[...] = b

        def tile(j, hc):
            rows = pl.ds(pl.multiple_of(j * 8, 8), 8)
            ht = a_scr[rows, :] * hc + b_scr[rows, :]
            hs_ref[rows, :] = ht
            return jnp.broadcast_to(ht[7:8, :], (8, W))

        hcar[...] = lax.fori_loop(0, tc // 8, tile, hcar[...])
        gy, _ = _gelu_parts(y_ref[...].astype(F32))
        rec_ref[...] = (hs_ref[...] * gy).astype(BF16)

    return _call(body, name=name, grid=(nb,), in_specs=specs, out_specs=out_specs, out_shape=out_shape,
                 scratch=[pltpu.VMEM((8, W), F32), pltpu.VMEM((tc, W), F32), pltpu.VMEM((tc, W), F32)],
                 sem=("arbitrary",))(proj, proj, proj, cw, cb, wa, ba, wx, bx, lam)


def _lru_bwd(proj, hs, drec_src, cw, cb, wa, ba, wx, bx, lam, *, name, tc=256):
    L = proj.shape[0]
    W = LRU_HEADS * LRU_HEAD_DIM
    nb = L // tc
    tb = lambda i: nb - 1 - i
    whole = lambda shape: pl.BlockSpec(shape, lambda i: (0,) * len(shape))
    gate_w = (LRU_HEADS, LRU_HEAD_DIM, LRU_HEAD_DIM)
    specs = [pl.BlockSpec((tc, W), lambda i: (tb(i), 0)), pl.BlockSpec((tc, W), lambda i: (tb(i), 1)),
             pl.BlockSpec((16, W), lambda i: (jnp.maximum(tb(i) * (tc // 16) - 1, 0), 0)),
             pl.BlockSpec((tc, W), lambda i: (tb(i), 0)),
             pl.BlockSpec((8, W), lambda i: (jnp.maximum(tb(i) * (tc // 8) - 1, 0), 0)),
             pl.BlockSpec((tc, W), lambda i: (tb(i), 0)),
             whole((CONV_WIDTH, W)), whole((1, W)), whole(gate_w), whole((1, W)), whole(gate_w), whole((1, W)), whole((1, W))]
    out_specs = [pl.BlockSpec((tc, 2 * W), lambda i: (tb(i), 0)), whole((CONV_WIDTH, W)), whole((1, W)), whole(gate_w),
                 whole((1, W)), whole(gate_w), whole((1, W)), whole((1, W))]
    vec = jax.ShapeDtypeStruct((1, W), F32)
    out_shape = [jax.ShapeDtypeStruct((L, 2 * W), BF16), jax.ShapeDtypeStruct((CONV_WIDTH, W), F32), vec,
                 jax.ShapeDtypeStruct(gate_w, F32), vec, jax.ShapeDtypeStruct(gate_w, F32), vec, vec]

    def body(x_ref, y_ref, xh_ref, hs_ref, hh_ref, dr_ref, cw_ref, cb_ref, wa_ref, ba_ref, wx_ref, bx_ref, lam_ref,
             dxy_ref, dcw_ref, dcb_ref, dwa_ref, dba_ref, dwx_ref, dbx_ref, dlam_ref, gcar, dxc_car, a_scr, b_scr, g_scr):
        pid = pl.program_id(0)
        t = tb(pid)
        accs = (dcw_ref, dcb_ref, dwa_ref, dba_ref, dwx_ref, dbx_ref, dlam_ref)

        @pl.when(pid == 0)
        def _():
            gcar[...] = jnp.zeros_like(gcar)
            dxc_car[...] = jnp.zeros_like(dxc_car)
            for r in accs:
                r[...] = jnp.zeros_like(r)

        halo = jnp.where(t > 0, xh_ref[...].astype(F32), 0.0)
        taps = _conv_taps(jnp.concatenate([halo, x_ref[...].astype(F32)], axis=0), 16)
        xc = cb_ref[...] + sum(cw_ref[k:k + 1, :] * taps[k] for k in range(CONV_WIDTH))
        lam = lam_ref[...]
        sp = _softplus(-lam)
        r, ig, a, om = _lru_gates(xc, wa_ref, ba_ref[...], wx_ref, bx_ref[...], sp)
        sq = jnp.sqrt(om)
        hblk = hs_ref[...]
        hprev = pltpu.roll(jnp.concatenate([jnp.where(t > 0, hh_ref[...], 0.0), hblk], axis=0), 1, 0)[8:]
        gy, dgy = _gelu_parts(y_ref[...].astype(F32))
        drec = dr_ref[...].astype(F32)
        dxy_ref[:, W:] = (drec * hblk * dgy).astype(BF16)

        rowidx = lax.broadcasted_iota(jnp.int32, (tc, W), 0)
        rowmod = rowidx & 7
        ca = jnp.where(rowidx == tc - 1, 1.0, pltpu.roll(a, tc - 1, 0))
        cbv = drec * gy
        for s in (1, 2, 4):
            keep = rowmod < 8 - s
            cbv = jnp.where(keep, ca * pltpu.roll(cbv, tc - s, 0) + cbv, cbv)
            ca = jnp.where(keep, ca * pltpu.roll(ca, tc - s, 0), ca)
        a_scr[...] = ca
        b_scr[...] = cbv

        def tile(k, gc):
            j = tc // 8 - 1 - k
            rows = pl.ds(pl.multiple_of(j * 8, 8), 8)
            gt = a_scr[rows, :] * gc + b_scr[rows, :]
            g_scr[rows, :] = gt
            return jnp.broadcast_to(gt[0:1, :], (8, W))

        lax.fori_loop(0, tc // 8, tile, gcar[...])
        G = g_scr[...]
        gcar[...] = jnp.broadcast_to(a[0:1, :] * G[0:1, :], (8, W))

        da = G * hprev
        dsq = G * (ig * xc)
        di = G * (sq * xc)
        dxc = G * (sq * ig)
        dla = da * a - 2.0 * a * a * (dsq * 0.5 * lax.rsqrt(om))
        dlam_ref[...] += jnp.sum(dla * (-LRU_C * r), axis=0, keepdims=True) * (-_sigmoid(-lam))
        dpr = dla * (-LRU_C * sp) * r * (1.0 - r)
        dpi = di * ig * (1.0 - ig)
        dba_ref[...] += jnp.sum(dpr, axis=0, keepdims=True)
        dbx_ref[...] += jnp.sum(dpi, axis=0, keepdims=True)
        back = []
        for hd in range(LRU_HEADS):
            sl = slice(hd * LRU_HEAD_DIM, (hd + 1) * LRU_HEAD_DIM)
            xh, dprh, dpih = xc[:, sl].astype(BF16), dpr[:, sl].astype(BF16), dpi[:, sl].astype(BF16)
            back.append(_dot(dprh, wa_ref[hd], NT) + _dot(dpih, wx_ref[hd], NT))
            dwa_ref[hd] += _dot(xh, dprh, TN)
            dwx_ref[hd] += _dot(xh, dpih, TN)
        dxc = dxc + jnp.concatenate(back, axis=1)
        dcb_ref[...] += jnp.sum(dxc, axis=0, keepdims=True)
        for k in range(CONV_WIDTH):
            dcw_ref[k:k + 1, :] += jnp.sum(dxc * taps[k], axis=0, keepdims=True)
        ext = jnp.concatenate([dxc, dxc_car[...]], axis=0)
        dx = cw_ref[CONV_WIDTH - 1:CONV_WIDTH, :] * dxc
        for k in range(CONV_WIDTH - 1):
            dx = dx + cw_ref[k:k + 1, :] * pltpu.roll(ext, tc + 8 - (CONV_WIDTH - 1 - k), 0)[:tc]
        dxc_car[...] = dxc[0:8, :]
        dxy_ref[:, :W] = dx.astype(BF16)

    scratch = [pltpu.VMEM((8, W), F32), pltpu.VMEM((8, W), F32)] + [pltpu.VMEM((tc, W), F32)] * 3
    return _call(body, name=name, grid=(nb,), in_specs=specs, out_specs=out_specs, out_shape=out_shape, scratch=scratch,
                 sem=("arbitrary",))(proj, proj, proj, hs, hs, drec_src, cw, cb, wa, ba, wx, bx, lam)


def _final_loss(h, gain, target, *, name, tm=256):
    M, K = h.shape
    row = pl.BlockSpec((tm, K), lambda i: (i, 0))
    vec = pl.BlockSpec((1, K), lambda i: (0, 0))
    one = pl.BlockSpec((1, 128), lambda i: (0, 0))

    def body(h_ref, g_ref, t_ref, dh_ref, dg_ref, loss_ref):
        @pl.when(pl.program_id(0) == 0)
        def _():
            dg_ref[...] = jnp.zeros_like(dg_ref)
            loss_ref[...] = jnp.zeros_like(loss_ref)

        x = h_ref[...]
        r = lax.rsqrt(jnp.mean(x * x, axis=-1, keepdims=True) + NORM_EPS)
        xhat = x * r
        err = xhat * g_ref[...] - t_ref[...]
        loss_ref[...] += 0.5 / K * jnp.sum(err * err)
        dy = err * (1.0 / K)
        dg_ref[...] += jnp.sum(dy * xhat, axis=0, keepdims=True)
        dxh = dy * g_ref[...]
        dh_ref[...] = r * (dxh - xhat * jnp.mean(dxh * xhat, axis=-1, keepdims=True))

    return _call(body, name=name, grid=(M // tm,), in_specs=[row, vec, row], out_specs=[row, vec, one],
                 out_shape=[jax.ShapeDtypeStruct((M, K), F32), jax.ShapeDtypeStruct((1, K), F32),
                            jax.ShapeDtypeStruct((1, 128), F32)], sem=("arbitrary",))(h, gain.reshape(1, K), target)


def _dilated_merge(branches, *, name, tm=512):
    L, W = branches[0].shape
    nbr = len(branches) // 2
    row = pl.BlockSpec((tm, W), lambda i: (i, 0))

    def body(*refs):
        o_ref, lse_ref = refs[-2], refs[-1]
        lses = [refs[2 * b + 1][...] for b in range(nbr)]
        m = lses[0]
        for t in lses[1:]:
            m = jnp.maximum(m, t)
        ws = [jnp.exp(t - m) for t in lses]
        den = ws[0]
        for t in ws[1:]:
            den = den + t
        acc = ws[0] * refs[0][...].astype(F32)
        for b in range(1, nbr):
            acc = acc + ws[b] * refs[2 * b][...].astype(F32)
        o_ref[...] = (acc / den).astype(BF16)
        lse_ref[...] = m + jnp.log(den)

    return _call(body, name=name, grid=(L // tm,), in_specs=[row] * (2 * nbr), out_specs=[row, row],
                 out_shape=[jax.ShapeDtypeStruct((L, W), BF16), jax.ShapeDtypeStruct((L, W), F32)], sem=("parallel",))(*branches)


def _dilated_fwd(proj0):
    L = proj0.shape[0]
    qkv = proj0[:, 2 * D_MODEL:]
    W = B_HEADS * HEAD_DIM
    outs = []
    for window, d in DILATED_PATTERN:
        view = qkv.reshape(L // d, d * 3 * W)
        o, lse = _band_fwd(view, view, view, d=d, nq=B_HEADS, nkv=B_HEADS, qcol=lambda r: 3 * r, kcol=lambda r: 3 * r + 1,
                           vcol=lambda r: 3 * r + 2, max_dist=window // d, name=f"dilated_fwd_d{d}")
        outs += [o.reshape(L, W), lse.reshape(L, W)]
    return _dilated_merge(outs, name="dilated_merge")


def _dilated_bwd(proj0, att, lse, datt, tabs):
    L = proj0.shape[0]
    qkv = proj0[:, 2 * D_MODEL:]
    Wh = B_HEADS * HEAD_DIM
    branches = []
    for window, d in DILATED_PATTERN:
        view = qkv.reshape(L // d, d * 3 * Wh)
        v1 = lambda t: t.reshape(L // d, d * Wh)
        outs = _band_bwd(view, view, view, v1(datt), v1(att), v1(lse), d=d, nq=B_HEADS, nkv=B_HEADS,
                         qcol=lambda r: 3 * r, kcol=lambda r: 3 * r + 1, vcol=lambda r: 3 * r + 2, docol=lambda r: r,
                         max_dist=window // d, name=f"dilated_bwd_d{d}")
        branches.append([o.reshape(L, Wh) for o in outs])
    return _attn_grad_combine(branches, tabs, name="dilated_grad_combine")


def _device_step(x, mem, target, w, on_grads=None):
    L = x.shape[0]
    tabs = _rope_tables(L)
    g = {}
    saved = []
    h = x
    for layer in range(2):
        sv = {"h_mix": h}
        if layer == 0:
            proj, n = _rowmm(h, w["ab_w_in"], name="l0_in_proj", gain=w["mix_norm"][0],
                             rope=(2 * D_MODEL, 2 * D_MODEL + 2 * B_HEADS * HEAD_DIM, tabs))
            rec, hs = _lru_fwd(proj, w["lru_conv_w"], w["lru_conv_b"], w["lru_wa"], w["lru_ba"], w["lru_wx"], w["lru_bx"],
                               w["lru_lambda"], name="lru_fwd")
            att, lse = _dilated_fwd(proj)
            mix = jnp.concatenate([rec, att], axis=1)
            (h,) = _rowmm(mix, w["ab_w_out"], name="l0_out_proj", res=h)
            sv.update(hs=hs)
        else:
            proj, n = _rowmm(h, w["c_w_qkv"], name="l1_qkv_proj", gain=w["mix_norm"][1], bias=w["c_b_qkv"],
                             rope=(0, (C_HEADS + C_KV_HEADS) * HEAD_DIM, tabs))
            mix, lse = _band_fwd(proj, proj, proj, d=1, nq=C_HEADS, nkv=C_KV_HEADS, qcol=lambda r: 0, kcol=lambda r: 8,
                                 vcol=lambda r: 9, max_dist=C_WINDOW - 1, sinks=w["c_sinks"], name="swa_fwd")
            (h,) = _rowmm(mix, w["c_w_out"], name="l1_out_proj", res=h, bias=w["c_b_out"])
        sv.update(proj=proj, n_mix=n, mix=mix, lse=lse, h_xa=h)
        xq, nx = _rowmm(h, w["xa_wq"][layer][None], name=f"xa_q_proj{layer}", gain=w["xa_norm"][layer])
        kv, nm = _rowmm(mem, w["xa_wkv"][layer][None], name=f"xa_kv_proj{layer}", gain=w["xa_mem_norm"][layer])
        xo, xlse = _xattn_fwd(xq, kv, name=f"xa_fwd{layer}")
        (h,) = _rowmm(xo, w["xa_wo"][layer], name=f"xa_out_proj{layer}", res=h)
        sv.update(xq=xq, nx=nx, kv=kv, nm=nm, xo=xo, xlse=xlse, h_ffn=h)
        gu, nf, act = _rowmm(h, w["ffn_w_gate_up"], layer=layer, name=f"ffn_in{layer}", gain=w["ffn_norm"][layer], swiglu=True)
        (h,) = _rowmm(act, w["ffn_w_down"][layer][None], name=f"ffn_out{layer}", res=h, tm=512)
        sv.update(gu=gu, nf=nf, act=act)
        saved.append(sv)

    dh, g["final_norm"], loss = _final_loss(h, w["final_norm"], target, name="final_loss")

    stk = {k: [None, None] for k in ("xa_norm", "xa_mem_norm", "ffn_norm", "mix_norm")}
    for layer in (1, 0):
        sv = saved[layer]
        (g["ffn_w_down", layer],) = _mm_tn(sv["act"], dh, S=1, name=f"ffn_down_dw{layer}", kk=D_FF // 2)
        (dgu,) = _mm_nt(dh, w["ffn_w_down"][layer][None], name=f"ffn_dact{layer}", mode="swiglu", kchunk=D_FF // 2, gu=sv["gu"])
        (g["ffn_w_gate_up", layer],) = _mm_tn(sv["nf"], dgu, S=N_CHIPS, name=f"ffn_gu_dw{layer}")
        dh, stk["ffn_norm"][layer] = _mm_nt(dgu, w["ffn_w_gate_up"], layer=layer, name=f"ffn_dx{layer}", mode="norm",
                                            h=sv["h_ffn"], gain=w["ffn_norm"][layer], dh=dh)
        (g["xa_wo", layer],) = _mm_tn(sv["xo"], dh, S=N_CHIPS, name=f"xa_wo_dw{layer}")
        (dxo,) = _mm_nt(dh, w["xa_wo"][layer], name=f"xa_dxo{layer}", mode="plain")
        dxq, dkv = _xattn_bwd(sv["xq"], sv["kv"], sv["xo"], sv["xlse"], dxo, name=f"xa_bwd{layer}")
        (g["xa_wq", layer],) = _mm_tn(sv["nx"], dxq, S=1, name=f"xa_wq_dw{layer}")
        dh, stk["xa_norm"][layer] = _mm_nt(dxq, w["xa_wq"][layer][None], name=f"xa_dx{layer}", mode="norm", h=sv["h_xa"],
                                           gain=w["xa_norm"][layer], dh=dh)
        (g["xa_wkv", layer],) = _mm_tn(sv["nm"], dkv, S=1, name=f"xa_wkv_dw{layer}")
        _, stk["xa_mem_norm"][layer] = _mm_nt(dkv, w["xa_wkv"][layer][None], name=f"xa_dmem{layer}", mode="norm", h=mem,
                                              gain=w["xa_mem_norm"][layer])
        if layer == 1:
            g["c_w_out"], g["c_b_out"] = _mm_tn(sv["mix"], dh, S=1, name="l1_out_dw", bias=True)
            (dmix,) = _mm_nt(dh, w["c_w_out"], name="l1_dmix", mode="plain")
            dq, dk, dv, dsk = _band_bwd(sv["proj"], sv["proj"], sv["proj"], dmix, sv["mix"], sv["lse"], d=1, nq=C_HEADS,
                                        nkv=C_KV_HEADS, qcol=lambda r: 0, kcol=lambda r: 8, vcol=lambda r: 9,
                                        docol=lambda r: 0, max_dist=C_WINDOW - 1, sinks=w["c_sinks"], name="swa_bwd")
            g["c_sinks"] = dsk[0, :C_HEADS]
            dproj = _attn_grad_combine([(dq, dk, dv)], tabs, name="swa_grad_combine")
            g["c_w_qkv"], g["c_b_qkv"] = _mm_tn(sv["n_mix"], dproj, S=1, name="l1_qkv_dw", bias=True)
            dh, stk["mix_norm"][1] = _mm_nt(dproj, w["c_w_qkv"], name="l1_dx", mode="norm", h=sv["h_mix"],
                                            gain=w["mix_norm"][1], dh=dh)
            if on_grads is not None:
                on_grads("layer1", g)
        else:
            if on_grads is not None:
                on_grads("layer0_ffn_xa", g)
            (g["ab_w_out"],) = _mm_tn(sv["mix"], dh, S=1, name="l0_out_dw", kk=768)
            (dmix,) = _mm_nt(dh, w["ab_w_out"], name="l0_dmix", mode="plain", kchunk=768)
            (dxy, g["lru_conv_w"], g["lru_conv_b"], g["lru_wa"], g["lru_ba"], g["lru_wx"], g["lru_bx"],
             g["lru_lambda"]) = _lru_bwd(sv["proj"], sv["hs"], dmix, w["lru_conv_w"], w["lru_conv_b"], w["lru_wa"],
                                         w["lru_ba"], w["lru_wx"], w["lru_bx"], w["lru_lambda"], name="lru_bwd")
            dqkv = _dilated_bwd(sv["proj"], sv["mix"][:, D_MODEL:], sv["lse"], dmix[:, D_MODEL:], tabs)
            dproj = jnp.concatenate([dxy, dqkv], axis=1)
            (g["ab_w_in"],) = _mm_tn(sv["n_mix"], dproj, S=N_CHIPS, name="l0_in_dw")
            dh, stk["mix_norm"][0] = _mm_nt(dproj, w["ab_w_in"], name="l0_dx", mode="norm", h=sv["h_mix"],
                                            gain=w["mix_norm"][0], dh=dh)
    for k, v in stk.items():
        g[k] = jnp.concatenate(v, axis=0)
    return loss[0, 0], dh, g


ANY = pl.BlockSpec(memory_space=pl.ANY)
MESH = pl.DeviceIdType.MESH


def _place():
    x, y, c = lax.axis_index("x"), lax.axis_index("y"), lax.axis_index("c")
    return x, y, c, [(1 - x, y), (x, 1 - y), (1 - x, 1 - y)]


def _remote(send_sems, recv_sems):
    def copy(k, src, dst, to):
        return pltpu.make_async_remote_copy(src_ref=src, dst_ref=dst, send_sem=send_sems.at[k], recv_sem=recv_sems.at[k],
                                            device_id=to, device_id_type=MESH)
    return copy


def _halves(ref, n_rows):
    rh = n_rows // 2
    return lambda lead, hh: ref.at[(*lead, pl.ds(hh * rh, rh), slice(None))]


def _gather_weights(packs, spack):
    n = len(packs)

    def body(*refs):
        w_refs, s_ref, wf_refs, sf_ref = refs[:n], refs[n], refs[n + 1:2 * n + 1], refs[2 * n + 1]
        x, y, c, chips = _place()
        me, sib = 2 * x + y, (x, y, 1 - c)
        copy = _remote(*refs[-2:])
        src = [_halves(w_refs[g], packs[g].shape[0]) for g in range(n)]
        dst = [_halves(wf_refs[g], packs[g].shape[0]) for g in range(n)]
        sends = []
        for g in range(n):
            for j, (cx, cy) in enumerate(chips):
                sends.append(copy(3 * g + j, src[g]((), c), dst[g]((me,), c), (cx, cy, c)))
        for j, (cx, cy) in enumerate(chips):
            sends.append(copy(6 * n + j, s_ref, sf_ref.at[me], (cx, cy, c)))
        for cp in sends:
            cp.start()
        for g in range(n):
            for j, (cx, cy) in enumerate(chips):
                got = dst[g]((2 * cx + cy,), c)
                copy(3 * g + j, got, got, sib).wait_recv()
                fwd = copy(3 * n + 3 * g + j, got, got, sib)
                fwd.start()
                sends.append(fwd)
        for g in range(n):
            for j, (cx, cy) in enumerate(chips):
                got = dst[g]((2 * cx + cy,), 1 - c)
                copy(3 * n + 3 * g + j, got, got, sib).wait_recv()
        for j, (cx, cy) in enumerate(chips):
            copy(6 * n + j, s_ref, sf_ref.at[2 * cx + cy], sib).wait_recv()
        for cp in sends:
            cp.wait_send()

    ins = list(packs) + [spack]
    out_shape = [jax.ShapeDtypeStruct((N_CHIPS,) + a.shape, a.dtype) for a in ins]
    n_sems = 6 * n + 3
    outs = pl.pallas_call(body, name="gather_weights", out_shape=out_shape, in_specs=[ANY] * len(ins),
                          out_specs=[ANY] * len(ins),
                          scratch_shapes=[pltpu.SemaphoreType.DMA((n_sems,)), pltpu.SemaphoreType.DMA((n_sems,))])(*ins)
    chip = 2 * lax.axis_index("x") + lax.axis_index("y")
    outs = [lax.dynamic_update_index_in_dim(o, a, chip, 0) for o, a in zip(outs, ins)]
    return outs[:n], outs[n]


SEQUENCER_GATHER_ID = 1


def _gather_weights_behind(packs):
    n = len(packs)

    def body(*refs):
        w_refs, wf_refs = refs[:n], refs[n:2 * n]
        x, y, c, chips = _place()
        me, sib = 2 * x + y, (x, y, 1 - c)
        barrier = pltpu.get_barrier_semaphore()
        for peer in [(cx, cy, c) for cx, cy in chips] + [sib]:
            pl.semaphore_signal(barrier, inc=1, device_id=peer, device_id_type=MESH)
        pl.semaphore_wait(barrier, len(chips) + 1)
        copy = _remote(*refs[-2:])
        src = [_halves(w_refs[g], packs[g].shape[0]) for g in range(n)]
        dst = [_halves(wf_refs[g], packs[g].shape[0]) for g in range(n)]
        sends = []
        for g in range(n):
            for j, (cx, cy) in enumerate(chips):
                sends.append(copy(3 * g + j, src[g]((), c), dst[g]((me,), c), (cx, cy, c)))
        for cp in sends:
            cp.start()
        for g in range(n):
            for j, (cx, cy) in enumerate(chips):
                got = dst[g]((2 * cx + cy,), c)
                copy(3 * g + j, got, got, sib).wait_recv()
                fwd = copy(3 * n + 3 * g + j, got, got, sib)
                fwd.start()
                sends.append(fwd)
        for g in range(n):
            for j, (cx, cy) in enumerate(chips):
                got = dst[g]((2 * cx + cy,), 1 - c)
                copy(3 * n + 3 * g + j, got, got, sib).wait_recv()
        for cp in sends:
            cp.wait_send()

    out_type = [jax.ShapeDtypeStruct((N_CHIPS,) + a.shape, a.dtype) for a in packs]
    outs = pl.kernel(body, out_type=out_type, mesh=plsc.ScalarSubcoreMesh(axis_name="sequencer", num_cores=1),
                     name="gather_weights_behind",
                     scratch_types=[pltpu.SemaphoreType.DMA((6 * n,)), pltpu.SemaphoreType.DMA((6 * n,))],
                     compiler_params=pltpu.CompilerParams(collective_id=SEQUENCER_GATHER_ID))(*packs)
    chip = 2 * lax.axis_index("x") + lax.axis_index("y")
    return [lax.dynamic_update_index_in_dim(o, a, chip, 0) for o, a in zip(outs, packs)]


def _rs_pair_exchange(gpacks, *, name):
    n = len(gpacks)

    def body(*refs):
        g_refs, ra_refs = refs[:n], refs[n:2 * n]
        x, y, c, _ = _place()
        copy = _remote(*refs[-2:])
        cps = []
        for g in range(n):
            half = _halves(g_refs[g], gpacks[g].shape[1])
            cps += [copy(N_CHIPS * g + j, half((j,), 1 - c), ra_refs[g].at[j], (x, y, 1 - c)) for j in range(N_CHIPS)]
        for cp in cps:
            cp.start()
        for cp in cps:
            cp.wait()

    out_shape = [jax.ShapeDtypeStruct((N_CHIPS, a.shape[1] // 2, a.shape[2]), a.dtype) for a in gpacks]
    n_sems = N_CHIPS * n
    return pl.pallas_call(body, name=name, out_shape=out_shape, in_specs=[ANY] * n, out_specs=[ANY] * n,
                          scratch_shapes=[pltpu.SemaphoreType.DMA((n_sems,)), pltpu.SemaphoreType.DMA((n_sems,))])(*gpacks)


def _row_tile(rows, cap=512):
    return max(t for t in range(16, min(rows, cap) + 1, 16) if rows % t == 0)


def _rs_pair_add(place, gpack, ra, *, name):
    _, R, C = gpack.shape
    Rh = R // 2
    tr = _row_tile(Rh)
    nrb = Rh // tr

    def body(p_ref, g_ref, ra_ref, pair_ref, own_ref):
        s = g_ref[...].astype(F32) + ra_ref[...].astype(F32)
        pair_ref[...] = s.astype(BF16)

        @pl.when(pl.program_id(1) == p_ref[1])
        def _():
            own_ref[...] = s

    grid_spec = pltpu.PrefetchScalarGridSpec(
        num_scalar_prefetch=1, grid=(nrb, N_CHIPS),
        in_specs=[pl.BlockSpec((None, tr, C), lambda i, j, p: (j, p[0] * nrb + i, 0)),
                  pl.BlockSpec((None, tr, C), lambda i, j, p: (j, i, 0))],
        out_specs=[pl.BlockSpec((None, tr, C), lambda i, j, p: (j, i, 0)), pl.BlockSpec((tr, C), lambda i, j, p: (i, 0))])
    return pl.pallas_call(
        body, name=name, grid_spec=grid_spec,
        out_shape=[jax.ShapeDtypeStruct((N_CHIPS, Rh, C), BF16), jax.ShapeDtypeStruct((Rh, C), F32)],
        compiler_params=pltpu.CompilerParams(dimension_semantics=("arbitrary", "arbitrary"),
                                             vmem_limit_bytes=VMEM_LIMIT_V7X))(place, gpack, ra)


def _rs_chip_exchange(pairs, small):
    n = len(pairs)

    def body(*refs):
        p_refs, s_ref, rb_refs, rs_ref = refs[:n], refs[n], refs[n + 1:2 * n + 1], refs[2 * n + 1]
        local_sem = refs[-1]
        x, y, c, chips = _place()
        copy = _remote(*refs[-3:-1])
        dev = 4 * x + 2 * y + c
        mine = pltpu.make_async_copy(s_ref, rs_ref.at[dev], local_sem.at[0])
        mine.start()
        cps = []
        for g in range(n):
            cps += [copy(3 * g + j, p_refs[g].at[2 * cx + cy], rb_refs[g].at[j], (cx, cy, c)) for j, (cx, cy) in enumerate(chips)]
        peers = []
        for k in range(1, 8):
            px = 1 - x if k & 4 else x
            py = 1 - y if k & 2 else y
            pc = 1 - c if k & 1 else c
            peers.append((px, py, pc))
            cps.append(copy(3 * n - 1 + k, s_ref, rs_ref.at[dev], (px, py, pc)))
        for cp in cps:
            cp.start()
        for g in range(n):
            for j in range(3):
                copy(3 * g + j, p_refs[g].at[0], rb_refs[g].at[j], (x, y, c)).wait_recv()
        for k, (px, py, pc) in enumerate(peers, start=1):
            copy(3 * n - 1 + k, s_ref, rs_ref.at[4 * px + 2 * py + pc], (x, y, c)).wait_recv()
        for cp in cps:
            cp.wait_send()
        mine.wait()

    out_shape = [jax.ShapeDtypeStruct((3,) + p.shape[1:], p.dtype) for p in pairs]
    out_shape.append(jax.ShapeDtypeStruct((8,) + small.shape, small.dtype))
    n_sems = 3 * n + 7
    return pl.pallas_call(body, name="rs_chip_exchange", out_shape=out_shape, in_specs=[ANY] * (n + 1),
                          out_specs=[ANY] * (n + 1),
                          scratch_shapes=[pltpu.SemaphoreType.DMA((n_sems,)), pltpu.SemaphoreType.DMA((n_sems,)),
                                          pltpu.SemaphoreType.DMA((1,))])(*pairs, small)


SEQUENCER_EXCHANGE_IDS = {"l1": 2, "l0a": 3}


def _rs_chip_exchange_behind(pairs, *, tag):
    n = len(pairs)

    def body(*refs):
        p_refs, rb_refs = refs[:n], refs[n:2 * n]
        x, y, c, chips = _place()
        barrier = pltpu.get_barrier_semaphore()
        for cx, cy in chips:
            pl.semaphore_signal(barrier, inc=1, device_id=(cx, cy, c), device_id_type=MESH)
        pl.semaphore_wait(barrier, len(chips))
        copy = _remote(*refs[-2:])
        cps = []
        for g in range(n):
            cps += [copy(3 * g + j, p_refs[g].at[2 * cx + cy], rb_refs[g].at[j], (cx, cy, c)) for j, (cx, cy) in enumerate(chips)]
        for cp in cps:
            cp.start()
        for g in range(n):
            for j in range(3):
                copy(3 * g + j, p_refs[g].at[0], rb_refs[g].at[j], (x, y, c)).wait_recv()
        for cp in cps:
            cp.wait_send()

    out_type = [jax.ShapeDtypeStruct((3,) + p.shape[1:], p.dtype) for p in pairs]
    return pl.kernel(body, out_type=out_type, mesh=plsc.ScalarSubcoreMesh(axis_name="sequencer", num_cores=1),
                     name="rs_chip_exchange_behind_" + tag,
                     scratch_types=[pltpu.SemaphoreType.DMA((3 * n,)), pltpu.SemaphoreType.DMA((3 * n,))],
                     compiler_params=pltpu.CompilerParams(collective_id=SEQUENCER_EXCHANGE_IDS[tag]))(*pairs)


def _rs_final_add(place, own, rb, *, name):
    Rh, C = own.shape
    tr = _row_tile(Rh)
    nrb = Rh // tr

    def body(p_ref, o_ref, rb_ref, f_ref):
        f_ref[...] = ((o_ref[...] + rb_ref[0].astype(F32)) + rb_ref[1].astype(F32)) + rb_ref[2].astype(F32)

    grid_spec = pltpu.PrefetchScalarGridSpec(
        num_scalar_prefetch=1, grid=(nrb,),
        in_specs=[pl.BlockSpec((tr, C), lambda i, p: (i, 0)), pl.BlockSpec((3, tr, C), lambda i, p: (0, i, 0))],
        out_specs=pl.BlockSpec((tr, C), lambda i, p: (p[0] * nrb + i, 0)))
    return pl.pallas_call(
        body, name=name, grid_spec=grid_spec, out_shape=jax.ShapeDtypeStruct((2 * Rh, C), F32),
        compiler_params=pltpu.CompilerParams(dimension_semantics=("arbitrary",), vmem_limit_bytes=VMEM_LIMIT_V7X))(place, own, rb)


def _sum_slots(rs):
    n, rows, C = rs.shape

    def body(r_ref, o_ref):
        acc = r_ref[0]
        for k in range(1, n):
            acc = acc + r_ref[k]
        o_ref[...] = acc

    return _call(body, name="small_grad_sum", grid=(1,), in_specs=[pl.BlockSpec((n, rows, C), lambda i: (0, 0, 0))],
                 out_specs=pl.BlockSpec((rows, C), lambda i: (0, 0)), out_shape=jax.ShapeDtypeStruct((rows, C), F32),
                 sem=("arbitrary",))(rs)


def _rs_sibling_share(gbufs):
    n = len(gbufs)

    def body(*refs):
        g_refs = refs[n:2 * n]
        x, y, c, _ = _place()
        copy = _remote(*refs[-2:])
        halves = [_halves(g_refs[g], gbufs[g].shape[0]) for g in range(n)]
        outs = [copy(g, halves[g]((), c), halves[g]((), c), (x, y, 1 - c)) for g in range(n)]
        for cp in outs:
            cp.start()
        for g in range(n):
            copy(g, halves[g]((), 1 - c), halves[g]((), 1 - c), (x, y, c)).wait_recv()
        for cp in outs:
            cp.wait_send()

    return pl.pallas_call(body, name="rs_sibling_share", out_shape=[jax.ShapeDtypeStruct(a.shape, a.dtype) for a in gbufs],
                          in_specs=[ANY] * n, out_specs=[ANY] * n, input_output_aliases={g: g for g in range(n)},
                          scratch_shapes=[pltpu.SemaphoreType.DMA((n,)), pltpu.SemaphoreType.DMA((n,))])(*gbufs)


def _adamw(w, g, m, v, *, name, g_row=0):
    rows, cols = w.shape
    tr = rows
    for cand in range(min(rows, 512), 7, -8):
        if rows % cand == 0 and g_row % cand == 0:
            tr = cand
            break
    spec = pl.BlockSpec((tr, cols), lambda i: (i, 0))
    g_spec = pl.BlockSpec((tr, cols), lambda i: (g_row // tr + i, 0))

    def body(w_ref, g_ref, m_ref, v_ref, d_ref, nm_ref, nv_ref):
        gg = g_ref[...]
        nm = ADAM_B1 * m_ref[...] + (1.0 - ADAM_B1) * gg
        nv = ADAM_B2 * v_ref[...] + (1.0 - ADAM_B2) * (gg * gg)
        m_hat = nm / (1.0 - ADAM_B1 ** ADAM_STEP)
        v_hat = nv / (1.0 - ADAM_B2 ** ADAM_STEP)
        d_ref[...] = -ADAM_LR * (m_hat / (jnp.sqrt(v_hat) + ADAM_EPS) + ADAM_WD * w_ref[...])
        nm_ref[...] = nm
        nv_ref[...] = nv

    return _call(body, name=name, grid=(rows // tr,), in_specs=[spec, g_spec, spec, spec], out_specs=[spec] * 3,
                 out_shape=[jax.ShapeDtypeStruct((rows, cols), F32)] * 3, sem=("parallel",))(w, g, m, v)


WEIGHT_NAMES = ("mix_norm", "ab_w_in", "lru_conv_w", "lru_conv_b", "lru_wa", "lru_ba", "lru_wx", "lru_bx", "lru_lambda",
                "ab_w_out", "c_w_qkv", "c_b_qkv", "c_sinks", "c_w_out", "c_b_out", "xa_norm", "xa_mem_norm", "xa_wq",
                "xa_wkv", "xa_wo", "ffn_norm", "ffn_w_gate_up", "ffn_w_down", "final_norm")
EARLY_GROUPS = (("ab_w_in",), ("ab_w_out",), ("lru_wa", "lru_wx"))
LATE_GROUPS = (("c_w_out", "xa_wkv", "ffn_w_down"), ("ffn_w_gate_up",), ("xa_wo",), ("xa_wq",), ("c_w_qkv",))
GROUPS = EARLY_GROUPS + LATE_GROUPS
REPLICATED = ("mix_norm", "lru_conv_b", "lru_lambda", "c_sinks", "xa_norm", "xa_mem_norm", "ffn_norm", "final_norm")
SMALL_SHARDED = ("lru_conv_w", "lru_ba", "lru_bx", "c_b_qkv", "c_b_out")
LANES = 1024


def _rows(v):
    flat = v.reshape(-1)
    return jnp.pad(flat, (0, -flat.shape[0] % LANES)).reshape(-1, LANES)


def _pack_small(parts, total, *, name):
    def body(*refs):
        o_ref = refs[-1]
        o_ref[...] = jnp.zeros_like(o_ref)
        row = 0
        for p_ref in refs[:-1]:
            o_ref[row:row + p_ref.shape[0], :] = p_ref[...]
            row += p_ref.shape[0]

    return _call(body, name=name, grid=(1,), in_specs=[pl.BlockSpec(p.shape, lambda i: (0, 0)) for p in parts],
                 out_specs=pl.BlockSpec((total, LANES), lambda i: (0, 0)),
                 out_shape=jax.ShapeDtypeStruct((total, LANES), F32), sem=("arbitrary",))(*parts)


def _from_shards(name, t):
    minor = t.shape[-1]
    if name == "ab_w_in":
        return t
    if name in ("ab_w_out", "c_w_out"):
        return t.reshape(1, -1, minor)
    if name == "ffn_w_gate_up":
        return t.reshape(N_CHIPS, 2, -1, minor)
    if name in ("xa_wq", "xa_wkv", "ffn_w_down"):
        return t.reshape(N_CHIPS, 2, -1, minor).transpose(1, 0, 2, 3).reshape(2, -1, minor)
    if name in ("lru_wa", "lru_wx"):
        return t.reshape(N_CHIPS, LRU_HEADS, -1, minor).transpose(1, 0, 2, 3).reshape(LRU_HEADS, LRU_HEAD_DIM, minor)
    if name == "xa_wo":
        return t.reshape(N_CHIPS, 2, -1, minor).transpose(1, 0, 2, 3)
    assert name == "c_w_qkv"
    return t.transpose(1, 0, 2).reshape(1, D_MODEL, -1)


def _piece_shards(name, g):
    minor = g.shape[-1]
    if name in ("ab_w_in", "ffn_w_gate_up", "xa_wo"):
        return g
    if name in ("ab_w_out", "c_w_out", "xa_wq", "xa_wkv", "ffn_w_down"):
        return g.reshape(N_CHIPS, -1, minor)
    if name in ("lru_wa", "lru_wx"):
        return g.reshape(LRU_HEADS, N_CHIPS, -1, minor).transpose(1, 0, 2, 3).reshape(N_CHIPS, -1, minor)
    assert name == "c_w_qkv"
    return g.reshape(D_MODEL, N_CHIPS, -1).transpose(1, 0, 2)


RS_SETS = {
    "l1": ((("c_w_out", None), ("xa_wkv", 1), ("ffn_w_down", 1)), (("ffn_w_gate_up", 1),), (("xa_wo", 1),),
           (("xa_wq", 1),), (("c_w_qkv", None),)),
    "l0a": ((("xa_wkv", 0), ("ffn_w_down", 0)), (("ffn_w_gate_up", 0),), (("xa_wo", 0),), (("xa_wq", 0),)),
    "l0b": ((("ab_w_out", None),), (("ab_w_in", None),), (("lru_wa", None), ("lru_wx", None))),
}
RS_STAGE = {"layer1": "l1", "layer0_ffn_xa": "l0a"}


def kernel(x, mem, mix_norm, ab_w_in, lru_conv_w, lru_conv_b, lru_wa, lru_ba, lru_wx, lru_bx, lru_lambda, ab_w_out, c_w_qkv, c_b_qkv, c_sinks, c_w_out, c_b_out, xa_norm, xa_mem_norm, xa_wq, xa_wkv, xa_wo, ffn_norm, ffn_w_gate_up, ffn_w_down, final_norm, loss_target, m_mix_norm, m_ab_w_in, m_lru_conv_w, m_lru_conv_b, m_lru_wa, m_lru_ba, m_lru_wx, m_lru_bx, m_lru_lambda, m_ab_w_out, m_c_w_qkv, m_c_b_qkv, m_c_sinks, m_c_w_out, m_c_b_out, m_xa_norm, m_xa_mem_norm, m_xa_wq, m_xa_wkv, m_xa_wo, m_ffn_norm, m_ffn_w_gate_up, m_ffn_w_down, m_final_norm, v_mix_norm, v_ab_w_in, v_lru_conv_w, v_lru_conv_b, v_lru_wa, v_lru_ba, v_lru_wx, v_lru_bx, v_lru_lambda, v_ab_w_out, v_c_w_qkv, v_c_b_qkv, v_c_sinks, v_c_w_out, v_c_b_out, v_xa_norm, v_xa_mem_norm, v_xa_wq, v_xa_wkv, v_xa_wo, v_ffn_norm, v_ffn_w_gate_up, v_ffn_w_down, v_final_norm):
    given = dict(locals())
    wl = {n: given[n] for n in WEIGHT_NAMES}
    ml = {n: given["m_" + n] for n in WEIGHT_NAMES}
    vl = {n: given["v_" + n] for n in WEIGHT_NAMES}
    xi, yi, ci = lax.axis_index("x"), lax.axis_index("y"), lax.axis_index("c")
    chip = 2 * xi + yi

    def join(parts, axis):
        return parts[0] if len(parts) == 1 else jnp.concatenate(parts, axis=axis)

    local_rows = {n: wl[n].size // wl[n].shape[-1] for grp in GROUPS for n in grp}
    packs = [join([wl[n].astype(BF16).reshape(local_rows[n], wl[n].shape[-1]) for n in grp], 0) for grp in GROUPS]
    spack = _pack_small([_rows(wl[n]) for n in SMALL_SHARDED], 8, name="pack_small_weights")
    n_early = len(EARLY_GROUPS)
    early, sfull = _gather_weights(packs[:n_early], spack)
    early, sfull, late_packs = lax.optimization_barrier((early, sfull, packs[n_early:]))
    gathered = early + _gather_weights_behind(late_packs)
    w = {n: wl[n] for n in REPLICATED}
    w["c_sinks"] = wl["c_sinks"][0]
    for grp, full in zip(GROUPS, gathered):
        off = 0
        for n in grp:
            w[n] = _from_shards(n, full if len(grp) == 1 else full[:, off:off + local_rows[n]])
            off += local_rows[n]
    for r, n in enumerate(SMALL_SHARDED):
        loc = wl[n].shape[1:]
        t = sfull[:, r, :wl[n].size].reshape((N_CHIPS,) + loc)
        if n == "lru_conv_w":
            w[n] = t.transpose(1, 0, 2).reshape(CONV_WIDTH, -1)
        elif n in ("lru_ba", "lru_bx"):
            w[n] = t.transpose(1, 0, 2).reshape(1, -1)
        else:
            w[n] = t.reshape(1, -1)

    place = jnp.stack([ci, chip]).astype(jnp.int32)

    def pair_stage(spec, g, tag):
        piece = lambda n, l: (g[n] if l is None else g[n, l]).astype(BF16)
        gpacks = [join([_piece_shards(n, piece(n, l)) for n, l in grp], 1) for grp in spec]
        ras = _rs_pair_exchange(gpacks, name=f"rs_pair_exchange_{tag}")
        sums = [_rs_pair_add(place, gp, ra, name=f"rs_pair_add_{tag}_{i}") for i, (gp, ra) in enumerate(zip(gpacks, ras))]
        return [pair for pair, _ in sums], [own for _, own in sums]

    owns, rbs = [], []

    def reduce_behind(stage, g):
        tag = RS_STAGE[stage]
        pairs, own = pair_stage(RS_SETS[tag], g, tag)
        owns.extend(own)
        rbs.extend(_rs_chip_exchange_behind(pairs, tag=tag))

    loss_part, grad_x, g = _device_step(x[0], mem[0], loss_target[0], w, on_grads=reduce_behind)

    small_parts = [_rows(g[n]) for n in REPLICATED] + [_rows(jnp.broadcast_to(loss_part, (LANES,)))]
    small_parts += [_rows(g[n]) for n in SMALL_SHARDED]
    small = _pack_small(small_parts, 24, name="pack_small_grads")
    pairs, own = pair_stage(RS_SETS["l0b"], g, "l0b")
    *rb, rs = _rs_chip_exchange(pairs, small)
    gsums = _rs_sibling_share([_rs_final_add(place, o, r, name=f"rs_final_add_{i}")
                               for i, (o, r) in enumerate(zip(owns + own, rbs + rb))])
    ssum = _sum_slots(rs)

    where = {}
    for grp, gsum in zip(RS_SETS["l1"] + RS_SETS["l0a"] + RS_SETS["l0b"], gsums):
        off = 0
        for n, l in grp:
            rows = local_rows[n] if l is None else local_rows[n] // 2
            where[n, l] = (gsum, off, rows, len(grp) == 1)
            off += rows
    take = lambda gsum, off, rows, whole: gsum if whole else gsum[off:off + rows]
    grads, grad_rows = {}, {}
    for grp in GROUPS:
        for n in grp:
            if (n, None) in where:
                grads[n] = take(*where[n, None]).reshape(wl[n].shape)
                grad_rows[n] = where[n, None][:2]
            else:
                grads[n] = jnp.stack([take(*where[n, l]).reshape(wl[n].shape[1:]) for l in range(2)])
                grad_rows[n] = (grads[n].reshape(local_rows[n], wl[n].shape[-1]), 0)
    row = 0
    for n in REPLICATED:
        k = _rows(g[n]).shape[0]
        grads[n] = ssum[row:row + k].reshape(-1)[:wl[n].size].reshape(wl[n].shape)
        row += k
    loss = ssum[row, 0]
    row += 1
    for n in SMALL_SHARDED:
        k = _rows(g[n]).shape[0]
        full = ssum[row:row + k].reshape(-1)[:g[n].size]
        row += k
        loc = wl[n].shape
        if n == "lru_conv_w":
            sh = full.reshape(CONV_WIDTH, N_CHIPS, -1)
        elif n in ("lru_ba", "lru_bx"):
            sh = full.reshape(LRU_HEADS, N_CHIPS, -1)
        else:
            sh = full.reshape(1, N_CHIPS, -1)
        grads[n] = lax.dynamic_index_in_dim(sh, chip, axis=1, keepdims=False).reshape(loc)

    delta, new_m, new_v = {}, {}, {}
    for n, (gsum, off) in grad_rows.items():
        shape2 = (local_rows[n], wl[n].shape[-1])
        d, nm, nv = _adamw(wl[n].reshape(shape2), gsum, ml[n].reshape(shape2), vl[n].reshape(shape2), g_row=off,
                           name="adamw_" + n)
        delta[n], new_m[n], new_v[n] = (t.reshape(wl[n].shape) for t in (d, nm, nv))
    smalls = REPLICATED + SMALL_SHARDED
    packs = [_pack_small([_rows(src[n]) for n in smalls], 24, name="pack_adamw_" + tag)
             for tag, src in (("w", wl), ("g", grads), ("m", ml), ("v", vl))]
    outs = _adamw(*packs, name="adamw_small")
    row = 0
    for n in smalls:
        k = _rows(wl[n]).shape[0]
        for dst, o in zip((delta, new_m, new_v), outs):
            dst[n] = o[row:row + k].reshape(-1)[:wl[n].size].reshape(wl[n].shape)
        row += k

    return (loss, grad_x[None], *[grads[n] for n in WEIGHT_NAMES], *[delta[n] for n in WEIGHT_NAMES],
            *[new_m[n] for n in WEIGHT_NAMES], *[new_v[n] for n in WEIGHT_NAMES])
```

```python
import jax
import jax.numpy as jnp
from jax import lax
from jax.experimental import pallas as pl
from jax.experimental.pallas import tpu as pltpu
from jax.experimental.pallas import tpu_sc as plsc

F32, BF16 = jnp.float32, jnp.bfloat16
D_MODEL = 1024
NORM_EPS = 1e-6
ROPE_THETA = 500000.0
HEAD_DIM = 64
ROT_DIM = 16
BLK = 128
LRU_HEADS, LRU_HEAD_DIM, CONV_WIDTH, LRU_C = 4, 256, 4, 8.0
DILATED_PATTERN = ((128, 1), (512, 4), (2048, 16))
B_HEADS, C_HEADS, C_KV_HEADS, C_WINDOW = 8, 16, 2, 128
XA_HEADS, XA_HEAD_DIM, N_MEM = 4, 128, 256
D_FF = 2816
NEG = -1e30
ADAM_LR, ADAM_B1, ADAM_B2, ADAM_EPS, ADAM_WD, ADAM_STEP = 0.001, 0.9, 0.999, 1e-08, 0.01, 10
N_CHIPS = 4
VMEM_LIMIT_V7X = 56 * 1024 * 1024

NN = (((1,), (0,)), ((), ()))
NT = (((1,), (1,)), ((), ()))
TN = (((0,), (0,)), ((), ()))


def _dot(a, b, dims=NN):
    return lax.dot_general(a, b, dims, preferred_element_type=F32)


def _sigmoid(x):
    return 1.0 / (1.0 + jnp.exp(-x))


def _call(body, *, name, grid, in_specs, out_specs, out_shape, scratch=(), sem=None):
    return pl.pallas_call(
        body, name=name, grid=grid, in_specs=in_specs, out_specs=out_specs, out_shape=out_shape,
        scratch_shapes=list(scratch),
        compiler_params=pltpu.CompilerParams(dimension_semantics=sem, vmem_limit_bytes=VMEM_LIMIT_V7X))


def _rope_tables(L):
    half = ROT_DIM // 2
    inv = ROPE_THETA ** (-jnp.arange(0, ROT_DIM, 2, dtype=F32) / ROT_DIM)
    ang = jnp.arange(L, dtype=F32)[:, None] * inv[None, :]
    cos, sin = jnp.cos(ang), jnp.sin(ang)
    rest = HEAD_DIM - ROT_DIM
    z8, zr, one = jnp.zeros((L, half), F32), jnp.zeros((L, rest), F32), jnp.ones((L, rest), F32)
    c = jnp.concatenate([cos, cos, one], axis=1)
    s1 = jnp.concatenate([-sin, z8, zr], axis=1)
    s2 = jnp.concatenate([z8, sin, zr], axis=1)
    return tuple(jnp.concatenate([t, t], axis=1) for t in (c, s1, s2))


def _rope_fwd(v, c, s1, s2):
    return v * c + pltpu.roll(v, 120, 1) * s1 + pltpu.roll(v, 8, 1) * s2


def _rope_bwd(dv, c, s1, s2):
    return dv * c + pltpu.roll(dv * s1, 8, 1) + pltpu.roll(dv * s2, 120, 1)


def _weight_spec(w, layer):
    once = pl.Buffered(1)
    if layer is None:
        return w.shape, pl.BlockSpec(w.shape, lambda i: (0, 0, 0), pipeline_mode=once)
    S, _, K, Ns = w.shape
    return (S, K, Ns), pl.BlockSpec((S, None, K, Ns), lambda i: (0, layer, 0, 0), pipeline_mode=once)


def _rowmm(a, w3, *, name, tm=512, gain=None, bias=None, res=None, swiglu=False, rope=None, layer=None):
    M, K = a.shape
    (S, _, Ns), w_spec = _weight_spec(w3, layer)
    N = S * Ns
    tm = min(tm, M)
    has_norm, has_bias, has_res, has_rope = gain is not None, bias is not None, res is not None, rope is not None
    row = lambda w: pl.BlockSpec((tm, w), lambda i: (i, 0))
    whole = lambda shape: pl.BlockSpec(shape, lambda i: (0,) * len(shape))
    ins, specs = [a], [row(K)]
    if has_norm:
        ins.append(gain.reshape(1, K)); specs.append(whole((1, K)))
    ins.append(w3); specs.append(w_spec)
    if has_bias:
        ins.append(bias.reshape(1, N)); specs.append(whole((1, N)))
    if has_res:
        ins.append(res); specs.append(row(N))
    if has_rope:
        ins += list(rope[2]); specs += [row(128)] * 3
    y_dtype = F32 if has_res else BF16
    out_shape, out_specs = [jax.ShapeDtypeStruct((M, N), y_dtype)], [row(N)]
    if has_norm:
        out_shape.append(jax.ShapeDtypeStruct((M, K), BF16)); out_specs.append(row(K))
    if swiglu:
        out_shape.append(jax.ShapeDtypeStruct((M, N // 2), BF16)); out_specs.append(row(N // 2))
    scratch = [pltpu.VMEM((tm, N), F32)] if has_rope else []

    def body(*refs):
        it = iter(refs)
        a_ref = next(it)
        g_ref = next(it) if has_norm else None
        w_ref = next(it)
        b_ref = next(it) if has_bias else None
        r_ref = next(it) if has_res else None
        tabs = [next(it) for _ in range(3)] if has_rope else None
        y_ref = next(it)
        n_ref = next(it) if has_norm else None
        act_ref = next(it) if swiglu else None
        ys_ref = next(it) if has_rope else None
        if has_norm:
            x = a_ref[...].astype(F32)
            ms = jnp.mean(x * x, axis=-1, keepdims=True)
            xb = (x * lax.rsqrt(ms + NORM_EPS) * g_ref[...]).astype(BF16)
            n_ref[...] = xb
        else:
            xb = a_ref[...].astype(BF16)
        if swiglu:
            for s in range(S // 2):
                g = _dot(xb, w_ref[s])
                u = _dot(xb, w_ref[s + S // 2])
                y_ref[:, s * Ns:(s + 1) * Ns] = g.astype(BF16)
                y_ref[:, N // 2 + s * Ns:N // 2 + (s + 1) * Ns] = u.astype(BF16)
                act_ref[:, s * Ns:(s + 1) * Ns] = (g * _sigmoid(g) * u).astype(BF16)
            return
        for s in range(S):
            sl = slice(s * Ns, (s + 1) * Ns)
            acc = _dot(xb, w_ref[s])
            if has_bias:
                acc = acc + b_ref[:, sl]
            if has_res:
                acc = acc + r_ref[:, sl]
            if has_rope:
                ys_ref[:, sl] = acc
            else:
                y_ref[:, sl] = acc.astype(y_dtype)
        if has_rope:
            c, s1, s2 = (t[...] for t in tabs)
            for cb in range(N // 128):
                cs = slice(cb * 128, (cb + 1) * 128)
                v = ys_ref[:, cs]
                if rope[0] <= cb * 128 < rope[1]:
                    v = _rope_fwd(v, c, s1, s2)
                y_ref[:, cs] = v.astype(BF16)

    return _call(body, name=name, grid=(M // tm,), in_specs=specs, out_specs=out_specs, out_shape=out_shape,
                 scratch=scratch, sem=("parallel",))(*ins)


def _mm_nt(dy, w3, *, name, mode, tm=512, kchunk=None, h=None, gain=None, dh=None, gu=None, layer=None):
    M, N = dy.shape
    (S, K, Ns), w_spec = _weight_spec(w3, layer)
    kchunk = kchunk or K
    tm = min(tm, M)
    row = lambda w: pl.BlockSpec((tm, w), lambda i: (i, 0))
    whole = lambda shape: pl.BlockSpec(shape, lambda i: (0,) * len(shape))
    ins, specs = [dy, w3], [row(N), w_spec]
    has_dh = dh is not None
    if mode == "norm":
        ins += [h, gain.reshape(1, K)]; specs += [row(K), whole((1, K))]
        if has_dh:
            ins.append(dh); specs.append(row(K))
        out_shape = [jax.ShapeDtypeStruct((M, K), F32), jax.ShapeDtypeStruct((1, K), F32)]
        out_specs = [row(K), whole((1, K))]
    elif mode == "swiglu":
        ins.append(gu); specs.append(row(2 * K))
        out_shape, out_specs = [jax.ShapeDtypeStruct((M, 2 * K), BF16)], [row(2 * K)]
    else:
        out_shape, out_specs = [jax.ShapeDtypeStruct((M, K), BF16)], [row(K)]

    def body(*refs):
        it = iter(refs)
        dy_ref, w_ref = next(it), next(it)
        if mode == "norm":
            h_ref, g_ref = next(it), next(it)
            dh_ref = next(it) if has_dh else None
            o_ref, dg_ref = next(it), next(it)
        elif mode == "swiglu":
            gu_ref, o_ref = next(it), next(it)
        else:
            o_ref = next(it)
        for kc in range(K // kchunk):
            ks = slice(kc * kchunk, (kc + 1) * kchunk)
            acc = None
            for s in range(S):
                t = _dot(dy_ref[:, s * Ns:(s + 1) * Ns].astype(BF16), w_ref[s, ks, :], NT)
                acc = t if acc is None else acc + t
            if mode == "plain":
                o_ref[:, ks] = acc.astype(BF16)
            elif mode == "swiglu":
                us = slice(K + kc * kchunk, K + (kc + 1) * kchunk)
                g = gu_ref[:, ks].astype(F32)
                u = gu_ref[:, us].astype(F32)
                sg = _sigmoid(g)
                o_ref[:, ks] = (acc * u * (sg * (1.0 + g * (1.0 - sg)))).astype(BF16)
                o_ref[:, us] = (acc * (g * sg)).astype(BF16)
            else:
                x = h_ref[...].astype(F32)
                r = lax.rsqrt(jnp.mean(x * x, axis=-1, keepdims=True) + NORM_EPS)
                xhat = x * r
                dxh = acc * g_ref[...]
                dx = r * (dxh - xhat * jnp.mean(dxh * xhat, axis=-1, keepdims=True))
                o_ref[...] = dx + dh_ref[...] if has_dh else dx

                @pl.when(pl.program_id(0) == 0)
                def _():
                    dg_ref[...] = jnp.zeros_like(dg_ref)

                dg_ref[...] += jnp.sum(acc * xhat, axis=0, keepdims=True)

    sem = ("arbitrary",) if mode == "norm" else ("parallel",)
    return _call(body, name=name, grid=(M // tm,), in_specs=specs, out_specs=out_specs, out_shape=out_shape, sem=sem)(*ins)


def _mm_tn(x, dy, *, S, name, tk=1024, kk=None, bias=False):
    M, K = x.shape
    N = dy.shape[1]
    Ns = N // S
    kk = kk or K
    tk = min(tk, M)
    nl = M // tk
    in_specs = [pl.BlockSpec((tk, kk), lambda s, kc, l: (l, kc)), pl.BlockSpec((tk, Ns), lambda s, kc, l: (l, s))]
    out_shape = [jax.ShapeDtypeStruct((S, K, Ns), BF16)]
    out_specs = [pl.BlockSpec((None, kk, Ns), lambda s, kc, l: (s, kc, 0))]
    if bias:
        out_shape.append(jax.ShapeDtypeStruct((1, N), F32))
        out_specs.append(pl.BlockSpec((1, Ns), lambda s, kc, l: (0, s)))

    def body(x_ref, dy_ref, o_ref, *rest):
        acc_ref = rest[-1]
        kc, l = pl.program_id(1), pl.program_id(2)

        @pl.when(l == 0)
        def _():
            acc_ref[...] = jnp.zeros_like(acc_ref)

        acc_ref[...] += _dot(x_ref[...].astype(BF16), dy_ref[...].astype(BF16), TN)
        if bias:
            b_ref = rest[0]

            @pl.when((kc == 0) & (l == 0))
            def _():
                b_ref[...] = jnp.zeros_like(b_ref)

            @pl.when(kc == 0)
            def _():
                b_ref[...] += jnp.sum(dy_ref[...].astype(F32), axis=0, keepdims=True)

        @pl.when(l == nl - 1)
        def _():
            o_ref[...] = acc_ref[...].astype(BF16)

    return _call(body, name=name, grid=(S, K // kk, nl), in_specs=in_specs, out_specs=out_specs, out_shape=out_shape,
                 scratch=[pltpu.VMEM((kk, Ns), F32)], sem=("arbitrary", "arbitrary", "arbitrary"))(x, dy)


def _band_bias(max_dist, has_prev):
    rows = lax.broadcasted_iota(jnp.int32, (BLK, 2 * BLK), 0)
    cols = lax.broadcasted_iota(jnp.int32, (BLK, 2 * BLK), 1)
    dist = rows - cols + BLK
    ok = (dist >= 0) & (dist <= max_dist) & ((cols >= BLK) | has_prev)
    return jnp.where(ok, 0.0, NEG)


Q_SCALE = HEAD_DIM ** -0.5


def _band_fwd(qa, ka, va, *, d, nq, nkv, qcol, kcol, vcol, max_dist, sinks=None, name):
    Lr = qa.shape[0]
    nb = Lr // BLK
    qw, kw, G = nq * HEAD_DIM, nkv * HEAD_DIM, nq // nkv
    cur = lambda colf, w: pl.BlockSpec((BLK, w), lambda r, i: (i, colf(r)))
    prv = lambda colf, w: pl.BlockSpec((BLK, w), lambda r, i: (jnp.maximum(i - 1, 0), colf(r)))
    out = pl.BlockSpec((BLK, qw), lambda r, i: (i, r))
    ins, specs = [qa, ka, ka, va, va], [cur(qcol, qw), cur(kcol, kw), prv(kcol, kw), cur(vcol, kw), prv(vcol, kw)]
    has_sinks = sinks is not None
    if has_sinks:
        ins.append(sinks); specs.append(pl.BlockSpec(memory_space=pltpu.SMEM))

    def body(*refs):
        q_ref, kc_ref, kp_ref, vc_ref, vp_ref = refs[:5]
        sk_ref = refs[5] if has_sinks else None
        o_ref, lse_ref = refs[-2], refs[-1]
        bias = _band_bias(max_dist, pl.program_id(1) > 0)
        k2 = jnp.concatenate([kp_ref[...], kc_ref[...]], axis=0)
        v2 = jnp.concatenate([vp_ref[...], vc_ref[...]], axis=0)
        for h in range(nq):
            hs = slice(h * HEAD_DIM, (h + 1) * HEAD_DIM)
            ks = slice((h // G) * HEAD_DIM, (h // G + 1) * HEAD_DIM)
            s = _dot(q_ref[:, hs] * jnp.asarray(Q_SCALE, BF16), k2[:, ks], NT) + bias
            m = jnp.max(s, axis=-1, keepdims=True)
            if has_sinks:
                m = jnp.maximum(m, sk_ref[h])
            p = jnp.exp(s - m)
            l = jnp.sum(p, axis=-1, keepdims=True)
            if has_sinks:
                l = l + jnp.exp(sk_ref[h] - m)
            o_ref[:, hs] = (_dot(p.astype(BF16), v2[:, ks]) / l).astype(BF16)
            lse_ref[:, hs] = jnp.broadcast_to(m + jnp.log(l), (BLK, HEAD_DIM))

    return _call(body, name=name, grid=(d, nb), in_specs=specs, out_specs=[out, out],
                 out_shape=[jax.ShapeDtypeStruct((Lr, d * qw), BF16), jax.ShapeDtypeStruct((Lr, d * qw), F32)],
                 sem=("parallel", "parallel"))(*ins)


def _band_bwd(qa, ka, va, doa, oa, lsea, *, d, nq, nkv, qcol, kcol, vcol, docol, max_dist, sinks=None, name):
    Lr = qa.shape[0]
    nb = Lr // BLK
    qw, kw, G = nq * HEAD_DIM, nkv * HEAD_DIM, nq // nkv
    transposed = G > 1
    last = lambda i: jnp.minimum(i, nb - 1)
    cur = lambda colf, w: pl.BlockSpec((BLK, w), lambda r, i: (last(i), colf(r)))
    prv = lambda colf, w: pl.BlockSpec((BLK, w), lambda r, i: (jnp.maximum(last(i) - 1, 0), colf(r)))
    own = lambda r: r
    ins = [qa, ka, ka, va, va, doa, oa, lsea]
    specs = [cur(qcol, qw), cur(kcol, kw), prv(kcol, kw), cur(vcol, kw), prv(vcol, kw), cur(docol, qw), cur(own, qw),
             cur(own, qw)]
    has_sinks = sinks is not None
    if has_sinks:
        ins.append(sinks); specs.append(pl.BlockSpec(memory_space=pltpu.SMEM))
    out_shape = [jax.ShapeDtypeStruct((Lr, d * qw), F32), jax.ShapeDtypeStruct((Lr, d * kw), F32),
                 jax.ShapeDtypeStruct((Lr, d * kw), F32)]
    behind = lambda r, i: (jnp.maximum(i - 1, 0), r)
    out_specs = [pl.BlockSpec((BLK, qw), lambda r, i: (last(i), r)), pl.BlockSpec((BLK, kw), behind),
                 pl.BlockSpec((BLK, kw), behind)]
    if has_sinks:
        out_shape.append(jax.ShapeDtypeStruct((8, 128), F32))
        out_specs.append(pl.BlockSpec((8, 128), lambda r, i: (0, 0)))

    def body(*refs):
        it = iter(refs)
        q_ref, kc_ref, kp_ref, vc_ref, vp_ref, do_ref, o_ref, ls_ref = (next(it) for _ in range(8))
        sk_ref = next(it) if has_sinks else None
        dq_ref, dk_ref, dv_ref = next(it), next(it), next(it)
        dsk_ref = next(it) if has_sinks else None
        dk_car, dv_car = next(it), next(it)
        r_id, i = pl.program_id(0), pl.program_id(1)

        @pl.when(i == 0)
        def _():
            dk_car[...] = jnp.zeros_like(dk_car)
            dv_car[...] = jnp.zeros_like(dv_car)

        if has_sinks:
            @pl.when((r_id == 0) & (i == 0))
            def _():
                dsk_ref[...] = jnp.zeros_like(dsk_ref)

        @pl.when(i == nb)
        def _():
            dk_ref[...] = dk_car[...]
            dv_ref[...] = dv_car[...]

        @pl.when(i < nb)
        def _():
            bias = _band_bias(max_dist, i > 0)
            k2 = jnp.concatenate([kp_ref[...], kc_ref[...]], axis=0)
            v2 = jnp.concatenate([vp_ref[...], vc_ref[...]], axis=0)
            if has_sinks:
                lane = lax.broadcasted_iota(jnp.int32, (8, 128), 1)
                dsk = jnp.zeros((8, 128), F32)
            for kv in range(nkv):
                ks = slice(kv * HEAD_DIM, (kv + 1) * HEAD_DIM)
                kh, vh = k2[:, ks], v2[:, ks]
                shape = (HEAD_DIM, 2 * BLK) if transposed else (2 * BLK, HEAD_DIM)
                dk, dv = jnp.zeros(shape, F32), jnp.zeros(shape, F32)
                for g in range(G):
                    h = kv * G + g
                    hs = slice(h * HEAD_DIM, (h + 1) * HEAD_DIM)
                    q = q_ref[:, hs] * jnp.asarray(Q_SCALE, BF16)
                    do = do_ref[:, hs]
                    lse = ls_ref[:, h * HEAD_DIM:h * HEAD_DIM + 1]
                    dl = jnp.sum(do.astype(F32) * o_ref[:, hs].astype(F32), axis=-1, keepdims=True)
                    p = jnp.exp(_dot(q, kh, NT) + bias - lse)
                    ds = (p * (_dot(do, vh, NT) - dl)).astype(BF16)
                    dq_ref[:, hs] = _dot(ds, kh) * Q_SCALE
                    if transposed:
                        dk = dk + _dot(q, ds, TN)
                        dv = dv + _dot(do, p.astype(BF16), TN)
                    else:
                        dk = dk + _dot(ds, q, TN)
                        dv = dv + _dot(p.astype(BF16), do, TN)
                    if has_sinks:
                        val = -jnp.sum(jnp.exp(sk_ref[h] - lse) * dl, axis=0, keepdims=True)
                        dsk = dsk + jnp.where(lane == h, val, 0.0)
                if transposed:
                    dk, dv = dk.T, dv.T
                dk_ref[:, ks] = dk_car[:, ks] + dk[:BLK]
                dv_ref[:, ks] = dv_car[:, ks] + dv[:BLK]
                dk_car[:, ks] = dk[BLK:]
                dv_car[:, ks] = dv[BLK:]
            if has_sinks:
                dsk_ref[...] += dsk

    return _call(body, name=name, grid=(d, nb + 1), in_specs=specs, out_specs=out_specs, out_shape=out_shape,
                 scratch=[pltpu.VMEM((BLK, kw), F32), pltpu.VMEM((BLK, kw), F32)], sem=("arbitrary", "arbitrary"))(*ins)


def _attn_grad_combine(branches, tabs, *, name, tm=256):
    L, qw = branches[0][0].shape
    kw = branches[0][1].shape[1]
    nbr = len(branches)
    row = lambda w: pl.BlockSpec((tm, w), lambda i: (i, 0))
    ins, specs = [], []
    for dq, dk, dv in branches:
        ins += [dq, dk, dv]; specs += [row(qw), row(kw), row(kw)]
    ins += list(tabs); specs += [row(128)] * 3

    def body(*refs):
        c, s1, s2 = (t[...] for t in refs[3 * nbr:3 * nbr + 3])
        o_ref = refs[-1]
        for part, (w, off, rot) in enumerate(((qw, 0, True), (kw, qw, True), (kw, qw + kw, False))):
            for cb in range(w // 128):
                cs = slice(cb * 128, (cb + 1) * 128)
                v = refs[part][:, cs]
                for b in range(1, nbr):
                    v = v + refs[3 * b + part][:, cs]
                if rot:
                    v = _rope_bwd(v, c, s1, s2)
                o_ref[:, off + cb * 128:off + (cb + 1) * 128] = v.astype(BF16)

    return _call(body, name=name, grid=(L // tm,), in_specs=specs, out_specs=row(qw + 2 * kw),
                 out_shape=jax.ShapeDtypeStruct((L, qw + 2 * kw), BF16), sem=("parallel",))(*ins)


def _xattn_fwd(q, kv, *, name, tq=512):
    L, W = q.shape
    scale = XA_HEAD_DIM ** -0.5
    row = pl.BlockSpec((tq, W), lambda i: (i, 0))
    kvs = pl.BlockSpec((N_MEM, 2 * W), lambda i: (0, 0))

    def body(q_ref, kv_ref, o_ref, lse_ref):
        for h in range(XA_HEADS):
            hs = slice(h * XA_HEAD_DIM, (h + 1) * XA_HEAD_DIM)
            vs = slice(W + h * XA_HEAD_DIM, W + (h + 1) * XA_HEAD_DIM)
            s = _dot(q_ref[:, hs], kv_ref[:, hs], NT) * scale
            m = jnp.max(s, axis=-1, keepdims=True)
            p = jnp.exp(s - m)
            l = jnp.sum(p, axis=-1, keepdims=True)
            o_ref[:, hs] = (_dot(p.astype(BF16), kv_ref[:, vs]) / l).astype(BF16)
            lse_ref[:, hs] = jnp.broadcast_to(m + jnp.log(l), (tq, XA_HEAD_DIM))

    return _call(body, name=name, grid=(L // tq,), in_specs=[row, kvs], out_specs=[row, row],
                 out_shape=[jax.ShapeDtypeStruct((L, W), BF16), jax.ShapeDtypeStruct((L, W), F32)], sem=("parallel",))(q, kv)


def _xattn_bwd(q, kv, o, lse, do, *, name, tq=512):
    L, W = q.shape
    scale = XA_HEAD_DIM ** -0.5
    row = pl.BlockSpec((tq, W), lambda i: (i, 0))
    kvs = pl.BlockSpec((N_MEM, 2 * W), lambda i: (0, 0))

    def body(q_ref, kv_ref, o_ref, lse_ref, do_ref, dq_ref, dkv_ref):
        @pl.when(pl.program_id(0) == 0)
        def _():
            dkv_ref[...] = jnp.zeros_like(dkv_ref)

        for h in range(XA_HEADS):
            hs = slice(h * XA_HEAD_DIM, (h + 1) * XA_HEAD_DIM)
            vs = slice(W + h * XA_HEAD_DIM, W + (h + 1) * XA_HEAD_DIM)
            qh, kh, vh, doh = q_ref[:, hs], kv_ref[:, hs], kv_ref[:, vs], do_ref[:, hs]
            p = jnp.exp(_dot(qh, kh, NT) * scale - lse_ref[:, h * XA_HEAD_DIM:h * XA_HEAD_DIM + 1])
            dl = jnp.sum(doh.astype(F32) * o_ref[:, hs].astype(F32), axis=-1, keepdims=True)
            ds = (p * (_dot(doh, vh, NT) - dl) * scale).astype(BF16)
            dq_ref[:, hs] = _dot(ds, kh).astype(BF16)
            dkv_ref[:, hs] += _dot(ds, qh, TN)
            dkv_ref[:, vs] += _dot(p.astype(BF16), doh, TN)

    return _call(body, name=name, grid=(L // tq,), in_specs=[row, kvs, row, row, row], out_specs=[row, kvs],
                 out_shape=[jax.ShapeDtypeStruct((L, W), BF16), jax.ShapeDtypeStruct((N_MEM, 2 * W), F32)],
                 sem=("arbitrary",))(q, kv, o, lse, do)


def _neg_expm1(z):
    series = -(z * (1.0 + z * (0.5 + z * (1.0 / 6.0 + z * (1.0 / 24.0 + z * (1.0 / 120.0))))))
    return jnp.where(z > -0.05, series, 1.0 - jnp.exp(z))


def _softplus(z):
    return jnp.maximum(z, 0.0) + jnp.log(1.0 + jnp.exp(-jnp.abs(z)))


def _gelu_parts(y):
    c = 0.7978845608028654
    t = jnp.tanh(c * (y + 0.044715 * y * y * y))
    gy = 0.5 * y * (1.0 + t)
    dgy = 0.5 * (1.0 + t) + 0.5 * y * (1.0 - t * t) * c * (1.0 + 3.0 * 0.044715 * y * y)
    return gy, dgy


def _lru_gates(xc, wa_ref, ba, wx_ref, bx, sp):
    rs, igs = [], []
    for hd in range(LRU_HEADS):
        sl = slice(hd * LRU_HEAD_DIM, (hd + 1) * LRU_HEAD_DIM)
        xh = xc[:, sl].astype(BF16)
        rs.append(_sigmoid(_dot(xh, wa_ref[hd]) + ba[:, sl]))
        igs.append(_sigmoid(_dot(xh, wx_ref[hd]) + bx[:, sl]))
    r, ig = jnp.concatenate(rs, axis=1), jnp.concatenate(igs, axis=1)
    la = -LRU_C * r * sp
    return r, ig, jnp.exp(la), _neg_expm1(2.0 * la)


def _conv_taps(x_ext, halo):
    n = x_ext.shape[0]
    return [x_ext[halo:] if k == CONV_WIDTH - 1 else pltpu.roll(x_ext, CONV_WIDTH - 1 - k, 0)[halo:]
            for k in range(CONV_WIDTH)]


def _lru_fwd(proj, cw, cb, wa, ba, wx, bx, lam, *, name, tc=512):
    L = proj.shape[0]
    W = LRU_HEADS * LRU_HEAD_DIM
    nb = L // tc
    whole = lambda shape: pl.BlockSpec(shape, lambda i: (0,) * len(shape))
    specs = [pl.BlockSpec((tc, W), lambda i: (i, 0)), pl.BlockSpec((tc, W), lambda i: (i, 1)),
             pl.BlockSpec((16, W), lambda i: (jnp.maximum(i * (tc // 16) - 1, 0), 0)),
             whole((CONV_WIDTH, W)), whole((1, W)), whole((LRU_HEADS, LRU_HEAD_DIM, LRU_HEAD_DIM)), whole((1, W)),
             whole((LRU_HEADS, LRU_HEAD_DIM, LRU_HEAD_DIM)), whole((1, W)), whole((1, W))]
    out_specs = [pl.BlockSpec((tc, W), lambda i: (i, 0))] * 2
    out_shape = [jax.ShapeDtypeStruct((L, W), BF16), jax.ShapeDtypeStruct((L, W), F32)]

    def body(x_ref, y_ref, xh_ref, cw_ref, cb_ref, wa_ref, ba_ref, wx_ref, bx_ref, lam_ref, rec_ref, hs_ref,
             hcar, a_scr, b_scr):
        i = pl.program_id(0)

        @pl.when(i == 0)
        def _():
            hcar[...] = jnp.zeros_like(hcar)

        halo = jnp.where(i > 0, xh_ref[...].astype(F32), 0.0)
        taps = _conv_taps(jnp.concatenate([halo, x_ref[...].astype(F32)], axis=0), 16)
        xc = cb_ref[...] + sum(cw_ref[k:k + 1, :] * taps[k] for k in range(CONV_WIDTH))
        _, ig, a, om = _lru_gates(xc, wa_ref, ba_ref[...], wx_ref, bx_ref[...], _softplus(-lam_ref[...]))
        b = jnp.sqrt(om) * (ig * xc)
        rowmod = lax.broadcasted_iota(jnp.int32, (tc, W), 0) & 7
        for s in (1, 2, 4):
            keep = rowmod >= s
            b = jnp.where(keep, a * pltpu.roll(b, s, 0) + b, b)
            a = jnp.where(keep, a * pltpu.roll(a, s, 0), a)
        a_scr[...] = a
        b_scr[...] = b

        def tile(j, hc):
            rows = pl.ds(pl.multiple_of(j * 8, 8), 8)
            ht = a_scr[rows, :] * hc + b_scr[rows, :]
            hs_ref[rows, :] = ht
            return jnp.broadcast_to(ht[7:8, :], (8, W))

        hcar[...] = lax.fori_loop(0, tc // 8, tile, hcar[...])
        gy, _ = _gelu_parts(y_ref[...].astype(F32))
        rec_ref[...] = (hs_ref[...] * gy).astype(BF16)

    return _call(body, name=name, grid=(nb,), in_specs=specs, out_specs=out_specs, out_shape=out_shape,
                 scratch=[pltpu.VMEM((8, W), F32), pltpu.VMEM((tc, W), F32), pltpu.VMEM((tc, W), F32)],
                 sem=("arbitrary",))(proj, proj, proj, cw, cb, wa, ba, wx, bx, lam)


def _lru_bwd(proj, hs, drec_src, cw, cb, wa, ba, wx, bx, lam, *, name, tc=256):
    L = proj.shape[0]
    W = LRU_HEADS * LRU_HEAD_DIM
    nb = L // tc
    tb = lambda i: nb - 1 - i
    whole = lambda shape: pl.BlockSpec(shape, lambda i: (0,) * len(shape))
    gate_w = (LRU_HEADS, LRU_HEAD_DIM, LRU_HEAD_DIM)
    specs = [pl.BlockSpec((tc, W), lambda i: (tb(i), 0)), pl.BlockSpec((tc, W), lambda i: (tb(i), 1)),
             pl.BlockSpec((16, W), lambda i: (jnp.maximum(tb(i) * (tc // 16) - 1, 0), 0)),
             pl.BlockSpec((tc, W), lambda i: (tb(i), 0)),
             pl.BlockSpec((8, W), lambda i: (jnp.maximum(tb(i) * (tc // 8) - 1, 0), 0)),
             pl.BlockSpec((tc, W), lambda i: (tb(i), 0)),
             whole((CONV_WIDTH, W)), whole((1, W)), whole(gate_w), whole((1, W)), whole(gate_w), whole((1, W)), whole((1, W))]
    out_specs = [pl.BlockSpec((tc, 2 * W), lambda i: (tb(i), 0)), whole((CONV_WIDTH, W)), whole((1, W)), whole(gate_w),
                 whole((1, W)), whole(gate_w), whole((1, W)), whole((1, W))]
    vec = jax.ShapeDtypeStruct((1, W), F32)
    out_shape = [jax.ShapeDtypeStruct((L, 2 * W), BF16), jax.ShapeDtypeStruct((CONV_WIDTH, W), F32), vec,
                 jax.ShapeDtypeStruct(gate_w, F32), vec, jax.ShapeDtypeStruct(gate_w, F32), vec, vec]

    def body(x_ref, y_ref, xh_ref, hs_ref, hh_ref, dr_ref, cw_ref, cb_ref, wa_ref, ba_ref, wx_ref, bx_ref, lam_ref,
             dxy_ref, dcw_ref, dcb_ref, dwa_ref, dba_ref, dwx_ref, dbx_ref, dlam_ref, gcar, dxc_car, a_scr, b_scr, g_scr):
        pid = pl.program_id(0)
        t = tb(pid)
        accs = (dcw_ref, dcb_ref, dwa_ref, dba_ref, dwx_ref, dbx_ref, dlam_ref)

        @pl.when(pid == 0)
        def _():
            gcar[...] = jnp.zeros_like(gcar)
            dxc_car[...] = jnp.zeros_like(dxc_car)
            for r in accs:
                r[...] = jnp.zeros_like(r)

        halo = jnp.where(t > 0, xh_ref[...].astype(F32), 0.0)
        taps = _conv_taps(jnp.concatenate([halo, x_ref[...].astype(F32)], axis=0), 16)
        xc = cb_ref[...] + sum(cw_ref[k:k + 1, :] * taps[k] for k in range(CONV_WIDTH))
        lam = lam_ref[...]
        sp = _softplus(-lam)
        r, ig, a, om = _lru_gates(xc, wa_ref, ba_ref[...], wx_ref, bx_ref[...], sp)
        sq = jnp.sqrt(om)
        hblk = hs_ref[...]
        hprev = pltpu.roll(jnp.concatenate([jnp.where(t > 0, hh_ref[...], 0.0), hblk], axis=0), 1, 0)[8:]
        gy, dgy = _gelu_parts(y_ref[...].astype(F32))
        drec = dr_ref[...].astype(F32)
        dxy_ref[:, W:] = (drec * hblk * dgy).astype(BF16)

        rowidx = lax.broadcasted_iota(jnp.int32, (tc, W), 0)
        rowmod = rowidx & 7
        ca = jnp.where(rowidx == tc - 1, 1.0, pltpu.roll(a, tc - 1, 0))
        cbv = drec * gy
        for s in (1, 2, 4):
            keep = rowmod < 8 - s
            cbv = jnp.where(keep, ca * pltpu.roll(cbv, tc - s, 0) + cbv, cbv)
            ca = jnp.where(keep, ca * pltpu.roll(ca, tc - s, 0), ca)
        a_scr[...] = ca
        b_scr[...] = cbv

        def tile(k, gc):
            j = tc // 8 - 1 - k
            rows = pl.ds(pl.multiple_of(j * 8, 8), 8)
            gt = a_scr[rows, :] * gc + b_scr[rows, :]
            g_scr[rows, :] = gt
            return jnp.broadcast_to(gt[0:1, :], (8, W))

        lax.fori_loop(0, tc // 8, tile, gcar[...])
        G = g_scr[...]
        gcar[...] = jnp.broadcast_to(a[0:1, :] * G[0:1, :], (8, W))

        da = G * hprev
        dsq = G * (ig * xc)
        di = G * (sq * xc)
        dxc = G * (sq * ig)
        dla = da * a - 2.0 * a * a * (dsq * 0.5 * lax.rsqrt(om))
        dlam_ref[...] += jnp.sum(dla * (-LRU_C * r), axis=0, keepdims=True) * (-_sigmoid(-lam))
        dpr = dla * (-LRU_C * sp) * r * (1.0 - r)
        dpi = di * ig * (1.0 - ig)
        dba_ref[...] += jnp.sum(dpr, axis=0, keepdims=True)
        dbx_ref[...] += jnp.sum(dpi, axis=0, keepdims=True)
        back = []
        for hd in range(LRU_HEADS):
            sl = slice(hd * LRU_HEAD_DIM, (hd + 1) * LRU_HEAD_DIM)
            xh, dprh, dpih = xc[:, sl].astype(BF16), dpr[:, sl].astype(BF16), dpi[:, sl].astype(BF16)
            back.append(_dot(dprh, wa_ref[hd], NT) + _dot(dpih, wx_ref[hd], NT))
            dwa_ref[hd] += _dot(xh, dprh, TN)
            dwx_ref[hd] += _dot(xh, dpih, TN)
        dxc = dxc + jnp.concatenate(back, axis=1)
        dcb_ref[...] += jnp.sum(dxc, axis=0, keepdims=True)
        for k in range(CONV_WIDTH):
            dcw_ref[k:k + 1, :] += jnp.sum(dxc * taps[k], axis=0, keepdims=True)
        ext = jnp.concatenate([dxc, dxc_car[...]], axis=0)
        dx = cw_ref[CONV_WIDTH - 1:CONV_WIDTH, :] * dxc
        for k in range(CONV_WIDTH - 1):
            dx = dx + cw_ref[k:k + 1, :] * pltpu.roll(ext, tc + 8 - (CONV_WIDTH - 1 - k), 0)[:tc]
        dxc_car[...] = dxc[0:8, :]
        dxy_ref[:, :W] = dx.astype(BF16)

    scratch = [pltpu.VMEM((8, W), F32), pltpu.VMEM((8, W), F32)] + [pltpu.VMEM((tc, W), F32)] * 3
    return _call(body, name=name, grid=(nb,), in_specs=specs, out_specs=out_specs, out_shape=out_shape, scratch=scratch,
                 sem=("arbitrary",))(proj, proj, proj, hs, hs, drec_src, cw, cb, wa, ba, wx, bx, lam)


def _final_loss(h, gain, target, *, name, tm=256):
    M, K = h.shape
    row = pl.BlockSpec((tm, K), lambda i: (i, 0))
    vec = pl.BlockSpec((1, K), lambda i: (0, 0))
    one = pl.BlockSpec((1, 128), lambda i: (0, 0))

    def body(h_ref, g_ref, t_ref, dh_ref, dg_ref, loss_ref):
        @pl.when(pl.program_id(0) == 0)
        def _():
            dg_ref[...] = jnp.zeros_like(dg_ref)
            loss_ref[...] = jnp.zeros_like(loss_ref)

        x = h_ref[...]
        r = lax.rsqrt(jnp.mean(x * x, axis=-1, keepdims=True) + NORM_EPS)
        xhat = x * r
        err = xhat * g_ref[...] - t_ref[...]
        loss_ref[...] += 0.5 / K * jnp.sum(err * err)
        dy = err * (1.0 / K)
        dg_ref[...] += jnp.sum(dy * xhat, axis=0, keepdims=True)
        dxh = dy * g_ref[...]
        dh_ref[...] = r * (dxh - xhat * jnp.mean(dxh * xhat, axis=-1, keepdims=True))

    return _call(body, name=name, grid=(M // tm,), in_specs=[row, vec, row], out_specs=[row, vec, one],
                 out_shape=[jax.ShapeDtypeStruct((M, K), F32), jax.ShapeDtypeStruct((1, K), F32),
                            jax.ShapeDtypeStruct((1, 128), F32)], sem=("arbitrary",))(h, gain.reshape(1, K), target)


def _dilated_merge(branches, *, name, tm=512):
    L, W = branches[0].shape
    nbr = len(branches) // 2
    row = pl.BlockSpec((tm, W), lambda i: (i, 0))

    def body(*refs):
        o_ref, lse_ref = refs[-2], refs[-1]
        lses = [refs[2 * b + 1][...] for b in range(nbr)]
        m = lses[0]
        for t in lses[1:]:
            m = jnp.maximum(m, t)
        ws = [jnp.exp(t - m) for t in lses]
        den = ws[0]
        for t in ws[1:]:
            den = den + t
        acc = ws[0] * refs[0][...].astype(F32)
        for b in range(1, nbr):
            acc = acc + ws[b] * refs[2 * b][...].astype(F32)
        o_ref[...] = (acc / den).astype(BF16)
        lse_ref[...] = m + jnp.log(den)

    return _call(body, name=name, grid=(L // tm,), in_specs=[row] * (2 * nbr), out_specs=[row, row],
                 out_shape=[jax.ShapeDtypeStruct((L, W), BF16), jax.ShapeDtypeStruct((L, W), F32)], sem=("parallel",))(*branches)


def _dilated_fwd(proj0):
    L = proj0.shape[0]
    qkv = proj0[:, 2 * D_MODEL:]
    W = B_HEADS * HEAD_DIM
    outs = []
    for window, d in DILATED_PATTERN:
        view = qkv.reshape(L // d, d * 3 * W)
        o, lse = _band_fwd(view, view, view, d=d, nq=B_HEADS, nkv=B_HEADS, qcol=lambda r: 3 * r, kcol=lambda r: 3 * r + 1,
                           vcol=lambda r: 3 * r + 2, max_dist=window // d, name=f"dilated_fwd_d{d}")
        outs += [o.reshape(L, W), lse.reshape(L, W)]
    return _dilated_merge(outs, name="dilated_merge")


def _dilated_bwd(proj0, att, lse, datt, tabs):
    L = proj0.shape[0]
    qkv = proj0[:, 2 * D_MODEL:]
    Wh = B_HEADS * HEAD_DIM
    branches = []
    for window, d in DILATED_PATTERN:
        view = qkv.reshape(L // d, d * 3 * Wh)
        v1 = lambda t: t.reshape(L // d, d * Wh)
        outs = _band_bwd(view, view, view, v1(datt), v1(att), v1(lse), d=d, nq=B_HEADS, nkv=B_HEADS,
                         qcol=lambda r: 3 * r, kcol=lambda r: 3 * r + 1, vcol=lambda r: 3 * r + 2, docol=lambda r: r,
                         max_dist=window // d, name=f"dilated_bwd_d{d}")
        branches.append([o.reshape(L, Wh) for o in outs])
    return _attn_grad_combine(branches, tabs, name="dilated_grad_combine")


def _device_step(x, mem, target, w, on_grads=None):
    L = x.shape[0]
    tabs = _rope_tables(L)
    g = {}
    saved = []
    h = x
    for layer in range(2):
        sv = {"h_mix": h}
        if layer == 0:
            proj, n = _rowmm(h, w["ab_w_in"], name="l0_in_proj", gain=w["mix_norm"][0],
                             rope=(2 * D_MODEL, 2 * D_MODEL + 2 * B_HEADS * HEAD_DIM, tabs))
            rec, hs = _lru_fwd(proj, w["lru_conv_w"], w["lru_conv_b"], w["lru_wa"], w["lru_ba"], w["lru_wx"], w["lru_bx"],
                               w["lru_lambda"], name="lru_fwd")
            att, lse = _dilated_fwd(proj)
            mix = jnp.concatenate([rec, att], axis=1)
            (h,) = _rowmm(mix, w["ab_w_out"], name="l0_out_proj", res=h)
            sv.update(hs=hs)
        else:
            proj, n = _rowmm(h, w["c_w_qkv"], name="l1_qkv_proj", gain=w["mix_norm"][1], bias=w["c_b_qkv"],
                             rope=(0, (C_HEADS + C_KV_HEADS) * HEAD_DIM, tabs))
            mix, lse = _band_fwd(proj, proj, proj, d=1, nq=C_HEADS, nkv=C_KV_HEADS, qcol=lambda r: 0, kcol=lambda r: 8,
                                 vcol=lambda r: 9, max_dist=C_WINDOW - 1, sinks=w["c_sinks"], name="swa_fwd")
            (h,) = _rowmm(mix, w["c_w_out"], name="l1_out_proj", res=h, bias=w["c_b_out"])
        sv.update(proj=proj, n_mix=n, mix=mix, lse=lse, h_xa=h)
        xq, nx = _rowmm(h, w["xa_wq"][layer][None], name=f"xa_q_proj{layer}", gain=w["xa_norm"][layer])
        kv, nm = _rowmm(mem, w["xa_wkv"][layer][None], name=f"xa_kv_proj{layer}", gain=w["xa_mem_norm"][layer])
        xo, xlse = _xattn_fwd(xq, kv, name=f"xa_fwd{layer}")
        (h,) = _rowmm(xo, w["xa_wo"][layer], name=f"xa_out_proj{layer}", res=h)
        sv.update(xq=xq, nx=nx, kv=kv, nm=nm, xo=xo, xlse=xlse, h_ffn=h)
        gu, nf, act = _rowmm(h, w["ffn_w_gate_up"], layer=layer, name=f"ffn_in{layer}", gain=w["ffn_norm"][layer], swiglu=True)
        (h,) = _rowmm(act, w["ffn_w_down"][layer][None], name=f"ffn_out{layer}", res=h, tm=512)
        sv.update(gu=gu, nf=nf, act=act)
        saved.append(sv)

    dh, g["final_norm"], loss = _final_loss(h, w["final_norm"], target, name="final_loss")

    stk = {k: [None, None] for k in ("xa_norm", "xa_mem_norm", "ffn_norm", "mix_norm")}
    for layer in (1, 0):
        sv = saved[layer]
        (g["ffn_w_down", layer],) = _mm_tn(sv["act"], dh, S=1, name=f"ffn_down_dw{layer}", kk=D_FF // 2)
        (dgu,) = _mm_nt(dh, w["ffn_w_down"][layer][None], name=f"ffn_dact{layer}", mode="swiglu", kchunk=D_FF // 2, gu=sv["gu"])
        (g["ffn_w_gate_up", layer],) = _mm_tn(sv["nf"], dgu, S=N_CHIPS, name=f"ffn_gu_dw{layer}")
        dh, stk["ffn_norm"][layer] = _mm_nt(dgu, w["ffn_w_gate_up"], layer=layer, name=f"ffn_dx{layer}", mode="norm",
                                            h=sv["h_ffn"], gain=w["ffn_norm"][layer], dh=dh)
        (g["xa_wo", layer],) = _mm_tn(sv["xo"], dh, S=N_CHIPS, name=f"xa_wo_dw{layer}")
        (dxo,) = _mm_nt(dh, w["xa_wo"][layer], name=f"xa_dxo{layer}", mode="plain")
        dxq, dkv = _xattn_bwd(sv["xq"], sv["kv"], sv["xo"], sv["xlse"], dxo, name=f"xa_bwd{layer}")
        (g["xa_wq", layer],) = _mm_tn(sv["nx"], dxq, S=1, name=f"xa_wq_dw{layer}")
        dh, stk["xa_norm"][layer] = _mm_nt(dxq, w["xa_wq"][layer][None], name=f"xa_dx{layer}", mode="norm", h=sv["h_xa"],
                                           gain=w["xa_norm"][layer], dh=dh)
        (g["xa_wkv", layer],) = _mm_tn(sv["nm"], dkv, S=1, name=f"xa_wkv_dw{layer}")
        _, stk["xa_mem_norm"][layer] = _mm_nt(dkv, w["xa_wkv"][layer][None], name=f"xa_dmem{layer}", mode="norm", h=mem,
                                              gain=w["xa_mem_norm"][layer])
        if layer == 1:
            g["c_w_out"], g["c_b_out"] = _mm_tn(sv["mix"], dh, S=1, name="l1_out_dw", bias=True)
            (dmix,) = _mm_nt(dh, w["c_w_out"], name="l1_dmix", mode="plain")
            dq, dk, dv, dsk = _band_bwd(sv["proj"], sv["proj"], sv["proj"], dmix, sv["mix"], sv["lse"], d=1, nq=C_HEADS,
                                        nkv=C_KV_HEADS, qcol=lambda r: 0, kcol=lambda r: 8, vcol=lambda r: 9,
                                        docol=lambda r: 0, max_dist=C_WINDOW - 1, sinks=w["c_sinks"], name="swa_bwd")
            g["c_sinks"] = dsk[0, :C_HEADS]
            dproj = _attn_grad_combine([(dq, dk, dv)], tabs, name="swa_grad_combine")
            g["c_w_qkv"], g["c_b_qkv"] = _mm_tn(sv["n_mix"], dproj, S=1, name="l1_qkv_dw", bias=True)
            dh, stk["mix_norm"][1] = _mm_nt(dproj, w["c_w_qkv"], name="l1_dx", mode="norm", h=sv["h_mix"],
                                            gain=w["mix_norm"][1], dh=dh)
            if on_grads is not None:
                on_grads("layer1", g)
        else:
            if on_grads is not None:
                on_grads("layer0_ffn_xa", g)
            (g["ab_w_out"],) = _mm_tn(sv["mix"], dh, S=1, name="l0_out_dw", kk=768)
            (dmix,) = _mm_nt(dh, w["ab_w_out"], name="l0_dmix", mode="plain", kchunk=768)
            (dxy, g["lru_conv_w"], g["lru_conv_b"], g["lru_wa"], g["lru_ba"], g["lru_wx"], g["lru_bx"],
             g["lru_lambda"]) = _lru_bwd(sv["proj"], sv["hs"], dmix, w["lru_conv_w"], w["lru_conv_b"], w["lru_wa"],
                                         w["lru_ba"], w["lru_wx"], w["lru_bx"], w["lru_lambda"], name="lru_bwd")
            dqkv = _dilated_bwd(sv["proj"], sv["mix"][:, D_MODEL:], sv["lse"], dmix[:, D_MODEL:], tabs)
            dproj = jnp.concatenate([dxy, dqkv], axis=1)
            (g["ab_w_in"],) = _mm_tn(sv["n_mix"], dproj, S=N_CHIPS, name="l0_in_dw")
            dh, stk["mix_norm"][0] = _mm_nt(dproj, w["ab_w_in"], name="l0_dx", mode="norm", h=sv["h_mix"],
                                            gain=w["mix_norm"][0], dh=dh)
    for k, v in stk.items():
        g[k] = jnp.concatenate(v, axis=0)
    return loss[0, 0], dh, g


ANY = pl.BlockSpec(memory_space=pl.ANY)
MESH = pl.DeviceIdType.MESH


def _place():
    x, y, c = lax.axis_index("x"), lax.axis_index("y"), lax.axis_index("c")
    return x, y, c, [(1 - x, y), (x, 1 - y), (1 - x, 1 - y)]


def _remote(send_sems, recv_sems):
    def copy(k, src, dst, to):
        return pltpu.make_async_remote_copy(src_ref=src, dst_ref=dst, send_sem=send_sems.at[k], recv_sem=recv_sems.at[k],
                                            device_id=to, device_id_type=MESH)
    return copy


def _halves(ref, n_rows):
    rh = n_rows // 2
    return lambda lead, hh: ref.at[(*lead, pl.ds(hh * rh, rh), slice(None))]


def _gather_weights(packs, spack):
    n = len(packs)

    def body(*refs):
        w_refs, s_ref, wf_refs, sf_ref = refs[:n], refs[n], refs[n + 1:2 * n + 1], refs[2 * n + 1]
        x, y, c, chips = _place()
        me, sib = 2 * x + y, (x, y, 1 - c)
        copy = _remote(*refs[-2:])
        src = [_halves(w_refs[g], packs[g].shape[0]) for g in range(n)]
        dst = [_halves(wf_refs[g], packs[g].shape[0]) for g in range(n)]
        sends = []
        for g in range(n):
            for j, (cx, cy) in enumerate(chips):
                sends.append(copy(3 * g + j, src[g]((), c), dst[g]((me,), c), (cx, cy, c)))
        for j, (cx, cy) in enumerate(chips):
            sends.append(copy(6 * n + j, s_ref, sf_ref.at[me], (cx, cy, c)))
        for cp in sends:
            cp.start()
        for g in range(n):
            for j, (cx, cy) in enumerate(chips):
                got = dst[g]((2 * cx + cy,), c)
                copy(3 * g + j, got, got, sib).wait_recv()
                fwd = copy(3 * n + 3 * g + j, got, got, sib)
                fwd.start()
                sends.append(fwd)
        for g in range(n):
            for j, (cx, cy) in enumerate(chips):
                got = dst[g]((2 * cx + cy,), 1 - c)
                copy(3 * n + 3 * g + j, got, got, sib).wait_recv()
        for j, (cx, cy) in enumerate(chips):
            copy(6 * n + j, s_ref, sf_ref.at[2 * cx + cy], sib).wait_recv()
        for cp in sends:
            cp.wait_send()

    ins = list(packs) + [spack]
    out_shape = [jax.ShapeDtypeStruct((N_CHIPS,) + a.shape, a.dtype) for a in ins]
    n_sems = 6 * n + 3
    outs = pl.pallas_call(body, name="gather_weights", out_shape=out_shape, in_specs=[ANY] * len(ins),
                          out_specs=[ANY] * len(ins),
                          scratch_shapes=[pltpu.SemaphoreType.DMA((n_sems,)), pltpu.SemaphoreType.DMA((n_sems,))])(*ins)
    chip = 2 * lax.axis_index("x") + lax.axis_index("y")
    outs = [lax.dynamic_update_index_in_dim(o, a, chip, 0) for o, a in zip(outs, ins)]
    return outs[:n], outs[n]


SEQUENCER_GATHER_ID = 1


def _gather_weights_behind(packs):
    n = len(packs)

    def body(*refs):
        w_refs, wf_refs = refs[:n], refs[n:2 * n]
        x, y, c, chips = _place()
        me, sib = 2 * x + y, (x, y, 1 - c)
        barrier = pltpu.get_barrier_semaphore()
        for peer in [(cx, cy, c) for cx, cy in chips] + [sib]:
            pl.semaphore_signal(barrier, inc=1, device_id=peer, device_id_type=MESH)
        pl.semaphore_wait(barrier, len(chips) + 1)
        copy = _remote(*refs[-2:])
        src = [_halves(w_refs[g], packs[g].shape[0]) for g in range(n)]
        dst = [_halves(wf_refs[g], packs[g].shape[0]) for g in range(n)]
        sends = []
        for g in range(n):
            for j, (cx, cy) in enumerate(chips):
                sends.append(copy(3 * g + j, src[g]((), c), dst[g]((me,), c), (cx, cy, c)))
        for cp in sends:
            cp.start()
        for g in range(n):
            for j, (cx, cy) in enumerate(chips):
                got = dst[g]((2 * cx + cy,), c)
                copy(3 * g + j, got, got, sib).wait_recv()
                fwd = copy(3 * n + 3 * g + j, got, got, sib)
                fwd.start()
                sends.append(fwd)
        for g in range(n):
            for j, (cx, cy) in enumerate(chips):
                got = dst[g]((2 * cx + cy,), 1 - c)
                copy(3 * n + 3 * g + j, got, got, sib).wait_recv()
        for cp in sends:
            cp.wait_send()

    out_type = [jax.ShapeDtypeStruct((N_CHIPS,) + a.shape, a.dtype) for a in packs]
    outs = pl.kernel(body, out_type=out_type, mesh=plsc.ScalarSubcoreMesh(axis_name="sequencer", num_cores=1),
                     name="gather_weights_behind",
                     scratch_types=[pltpu.SemaphoreType.DMA((6 * n,)), pltpu.SemaphoreType.DMA((6 * n,))],
                     compiler_params=pltpu.CompilerParams(collective_id=SEQUENCER_GATHER_ID))(*packs)
    chip = 2 * lax.axis_index("x") + lax.axis_index("y")
    return [lax.dynamic_update_index_in_dim(o, a, chip, 0) for o, a in zip(outs, packs)]


def _rs_pair_exchange(gpacks, *, name):
    n = len(gpacks)

    def body(*refs):
        g_refs, ra_refs = refs[:n], refs[n:2 * n]
        x, y, c, _ = _place()
        copy = _remote(*refs[-2:])
        cps = []
        for g in range(n):
            half = _halves(g_refs[g], gpacks[g].shape[1])
            cps += [copy(N_CHIPS * g + j, half((j,), 1 - c), ra_refs[g].at[j], (x, y, 1 - c)) for j in range(N_CHIPS)]
        for cp in cps:
            cp.start()
        for cp in cps:
            cp.wait()

    out_shape = [jax.ShapeDtypeStruct((N_CHIPS, a.shape[1] // 2, a.shape[2]), a.dtype) for a in gpacks]
    n_sems = N_CHIPS * n
    return pl.pallas_call(body, name=name, out_shape=out_shape, in_specs=[ANY] * n, out_specs=[ANY] * n,
                          scratch_shapes=[pltpu.SemaphoreType.DMA((n_sems,)), pltpu.SemaphoreType.DMA((n_sems,))])(*gpacks)


def _row_tile(rows, cap=512):
    return max(t for t in range(16, min(rows, cap) + 1, 16) if rows % t == 0)


def _rs_pair_add(place, gpack, ra, *, name):
    _, R, C = gpack.shape
    Rh = R // 2
    tr = _row_tile(Rh)
    nrb = Rh // tr

    def body(p_ref, g_ref, ra_ref, pair_ref, own_ref):
        s = g_ref[...].astype(F32) + ra_ref[...].astype(F32)
        pair_ref[...] = s.astype(BF16)

        @pl.when(pl.program_id(1) == p_ref[1])
        def _():
            own_ref[...] = s

    grid_spec = pltpu.PrefetchScalarGridSpec(
        num_scalar_prefetch=1, grid=(nrb, N_CHIPS),
        in_specs=[pl.BlockSpec((None, tr, C), lambda i, j, p: (j, p[0] * nrb + i, 0)),
                  pl.BlockSpec((None, tr, C), lambda i, j, p: (j, i, 0))],
        out_specs=[pl.BlockSpec((None, tr, C), lambda i, j, p: (j, i, 0)), pl.BlockSpec((tr, C), lambda i, j, p: (i, 0))])
    return pl.pallas_call(
        body, name=name, grid_spec=grid_spec,
        out_shape=[jax.ShapeDtypeStruct((N_CHIPS, Rh, C), BF16), jax.ShapeDtypeStruct((Rh, C), F32)],
        compiler_params=pltpu.CompilerParams(dimension_semantics=("arbitrary", "arbitrary"),
                                             vmem_limit_bytes=VMEM_LIMIT_V7X))(place, gpack, ra)


SEQUENCER_EXCHANGE_IDS = {"l1": 2, "l0a": 3, "l0b": 4}


def _rs_chip_exchange_behind(pairs, *, tag, small=None):
    n = len(pairs)
    has_small = small is not None

    def body(*refs):
        p_refs = refs[:n]
        s_ref = refs[n] if has_small else None
        rb_refs = refs[n + has_small:2 * n + has_small]
        rs_ref = refs[2 * n + 1] if has_small else None
        x, y, c, chips = _place()
        peers = [(1 - x if k & 4 else x, 1 - y if k & 2 else y, 1 - c if k & 1 else c) for k in range(1, 8)]
        shake = peers if has_small else [(cx, cy, c) for cx, cy in chips]
        barrier = pltpu.get_barrier_semaphore()
        for peer in shake:
            pl.semaphore_signal(barrier, inc=1, device_id=peer, device_id_type=MESH)
        pl.semaphore_wait(barrier, len(shake))
        copy = _remote(*refs[-2:])
        cps = []
        for g in range(n):
            cps += [copy(3 * g + j, p_refs[g].at[2 * cx + cy], rb_refs[g].at[j], (cx, cy, c)) for j, (cx, cy) in enumerate(chips)]
        if has_small:
            dev = 4 * x + 2 * y + c
            cps += [copy(3 * n + k, s_ref, rs_ref.at[dev], peer) for k, peer in enumerate(peers)]
        for cp in cps:
            cp.start()
        for g in range(n):
            for j in range(3):
                copy(3 * g + j, p_refs[g].at[0], rb_refs[g].at[j], (x, y, c)).wait_recv()
        if has_small:
            for k, (px, py, pc) in enumerate(peers):
                copy(3 * n + k, s_ref, rs_ref.at[4 * px + 2 * py + pc], (x, y, c)).wait_recv()
        for cp in cps:
            cp.wait_send()

    ins = list(pairs) + ([small] if has_small else [])
    out_type = [jax.ShapeDtypeStruct((3,) + p.shape[1:], p.dtype) for p in pairs]
    if has_small:
        out_type.append(jax.ShapeDtypeStruct((8,) + small.shape, small.dtype))
    n_sems = 3 * n + 7 * has_small
    outs = pl.kernel(body, out_type=out_type, mesh=plsc.ScalarSubcoreMesh(axis_name="sequencer", num_cores=1),
                     name="rs_chip_exchange_behind_" + tag,
                     scratch_types=[pltpu.SemaphoreType.DMA((n_sems,)), pltpu.SemaphoreType.DMA((n_sems,))],
                     compiler_params=pltpu.CompilerParams(collective_id=SEQUENCER_EXCHANGE_IDS[tag]))(*ins)
    if has_small:
        dev = 4 * lax.axis_index("x") + 2 * lax.axis_index("y") + lax.axis_index("c")
        outs = list(outs[:n]) + [lax.dynamic_update_index_in_dim(outs[n], small, dev, 0)]
    return outs


def _rs_final_add(place, own, rb, *, name):
    Rh, C = own.shape
    tr = _row_tile(Rh)
    nrb = Rh // tr

    def body(p_ref, o_ref, rb_ref, f_ref):
        f_ref[...] = ((o_ref[...] + rb_ref[0].astype(F32)) + rb_ref[1].astype(F32)) + rb_ref[2].astype(F32)

    grid_spec = pltpu.PrefetchScalarGridSpec(
        num_scalar_prefetch=1, grid=(nrb,),
        in_specs=[pl.BlockSpec((tr, C), lambda i, p: (i, 0)), pl.BlockSpec((3, tr, C), lambda i, p: (0, i, 0))],
        out_specs=pl.BlockSpec((tr, C), lambda i, p: (p[0] * nrb + i, 0)))
    return pl.pallas_call(
        body, name=name, grid_spec=grid_spec, out_shape=jax.ShapeDtypeStruct((2 * Rh, C), F32),
        compiler_params=pltpu.CompilerParams(dimension_semantics=("arbitrary",), vmem_limit_bytes=VMEM_LIMIT_V7X))(place, own, rb)


def _sum_slots(rs):
    n, rows, C = rs.shape

    def body(r_ref, o_ref):
        acc = r_ref[0]
        for k in range(1, n):
            acc = acc + r_ref[k]
        o_ref[...] = acc

    return _call(body, name="small_grad_sum", grid=(1,), in_specs=[pl.BlockSpec((n, rows, C), lambda i: (0, 0, 0))],
                 out_specs=pl.BlockSpec((rows, C), lambda i: (0, 0)), out_shape=jax.ShapeDtypeStruct((rows, C), F32),
                 sem=("arbitrary",))(rs)


def _rs_sibling_share(gbufs, *, name):
    n = len(gbufs)

    def body(*refs):
        g_refs = refs[n:2 * n]
        x, y, c, _ = _place()
        copy = _remote(*refs[-2:])
        halves = [_halves(g_refs[g], gbufs[g].shape[0]) for g in range(n)]
        outs = [copy(g, halves[g]((), c), halves[g]((), c), (x, y, 1 - c)) for g in range(n)]
        for cp in outs:
            cp.start()
        for g in range(n):
            copy(g, halves[g]((), 1 - c), halves[g]((), 1 - c), (x, y, c)).wait_recv()
        for cp in outs:
            cp.wait_send()

    return pl.pallas_call(body, name=name, out_shape=[jax.ShapeDtypeStruct(a.shape, a.dtype) for a in gbufs],
                          in_specs=[ANY] * n, out_specs=[ANY] * n, input_output_aliases={g: g for g in range(n)},
                          scratch_shapes=[pltpu.SemaphoreType.DMA((n,)), pltpu.SemaphoreType.DMA((n,))])(*gbufs)


def _adamw(w, g, m, v, *, name, g_row=0):
    rows, cols = w.shape
    tr = rows
    for cand in range(min(rows, 512), 7, -8):
        if rows % cand == 0 and g_row % cand == 0:
            tr = cand
            break
    spec = pl.BlockSpec((tr, cols), lambda i: (i, 0))
    g_spec = pl.BlockSpec((tr, cols), lambda i: (g_row // tr + i, 0))

    def body(w_ref, g_ref, m_ref, v_ref, d_ref, nm_ref, nv_ref):
        gg = g_ref[...]
        nm = ADAM_B1 * m_ref[...] + (1.0 - ADAM_B1) * gg
        nv = ADAM_B2 * v_ref[...] + (1.0 - ADAM_B2) * (gg * gg)
        m_hat = nm / (1.0 - ADAM_B1 ** ADAM_STEP)
        v_hat = nv / (1.0 - ADAM_B2 ** ADAM_STEP)
        d_ref[...] = -ADAM_LR * (m_hat / (jnp.sqrt(v_hat) + ADAM_EPS) + ADAM_WD * w_ref[...])
        nm_ref[...] = nm
        nv_ref[...] = nv

    return _call(body, name=name, grid=(rows // tr,), in_specs=[spec, g_spec, spec, spec], out_specs=[spec] * 3,
                 out_shape=[jax.ShapeDtypeStruct((rows, cols), F32)] * 3, sem=("parallel",))(w, g, m, v)


WEIGHT_NAMES = ("mix_norm", "ab_w_in", "lru_conv_w", "lru_conv_b", "lru_wa", "lru_ba", "lru_wx", "lru_bx", "lru_lambda",
                "ab_w_out", "c_w_qkv", "c_b_qkv", "c_sinks", "c_w_out", "c_b_out", "xa_norm", "xa_mem_norm", "xa_wq",
                "xa_wkv", "xa_wo", "ffn_norm", "ffn_w_gate_up", "ffn_w_down", "final_norm")
EARLY_GROUPS = (("ab_w_in",), ("ab_w_out",), ("lru_wa", "lru_wx"))
LATE_GROUPS = (("c_w_out", "xa_wkv", "ffn_w_down"), ("ffn_w_gate_up",), ("xa_wo",), ("xa_wq",), ("c_w_qkv",))
GROUPS = EARLY_GROUPS + LATE_GROUPS
REPLICATED = ("mix_norm", "lru_conv_b", "lru_lambda", "c_sinks", "xa_norm", "xa_mem_norm", "ffn_norm", "final_norm")
SMALL_SHARDED = ("lru_conv_w", "lru_ba", "lru_bx", "c_b_qkv", "c_b_out")
LANES = 1024


def _rows(v):
    flat = v.reshape(-1)
    return jnp.pad(flat, (0, -flat.shape[0] % LANES)).reshape(-1, LANES)


def _pack_small(parts, total, *, name):
    def body(*refs):
        o_ref = refs[-1]
        o_ref[...] = jnp.zeros_like(o_ref)
        row = 0
        for p_ref in refs[:-1]:
            o_ref[row:row + p_ref.shape[0], :] = p_ref[...]
            row += p_ref.shape[0]

    return _call(body, name=name, grid=(1,), in_specs=[pl.BlockSpec(p.shape, lambda i: (0, 0)) for p in parts],
                 out_specs=pl.BlockSpec((total, LANES), lambda i: (0, 0)),
                 out_shape=jax.ShapeDtypeStruct((total, LANES), F32), sem=("arbitrary",))(*parts)


def _from_shards(name, t):
    minor = t.shape[-1]
    if name == "ab_w_in":
        return t
    if name in ("ab_w_out", "c_w_out"):
        return t.reshape(1, -1, minor)
    if name == "ffn_w_gate_up":
        return t.reshape(N_CHIPS, 2, -1, minor)
    if name in ("xa_wq", "xa_wkv", "ffn_w_down"):
        return t.reshape(N_CHIPS, 2, -1, minor).transpose(1, 0, 2, 3).reshape(2, -1, minor)
    if name in ("lru_wa", "lru_wx"):
        return t.reshape(N_CHIPS, LRU_HEADS, -1, minor).transpose(1, 0, 2, 3).reshape(LRU_HEADS, LRU_HEAD_DIM, minor)
    if name == "xa_wo":
        return t.reshape(N_CHIPS, 2, -1, minor).transpose(1, 0, 2, 3)
    assert name == "c_w_qkv"
    return t.transpose(1, 0, 2).reshape(1, D_MODEL, -1)


def _piece_shards(name, g):
    minor = g.shape[-1]
    if name in ("ab_w_in", "ffn_w_gate_up", "xa_wo"):
        return g
    if name in ("ab_w_out", "c_w_out", "xa_wq", "xa_wkv", "ffn_w_down"):
        return g.reshape(N_CHIPS, -1, minor)
    if name in ("lru_wa", "lru_wx"):
        return g.reshape(LRU_HEADS, N_CHIPS, -1, minor).transpose(1, 0, 2, 3).reshape(N_CHIPS, -1, minor)
    assert name == "c_w_qkv"
    return g.reshape(D_MODEL, N_CHIPS, -1).transpose(1, 0, 2)


RS_SETS = {
    "l1": ((("c_w_out", None), ("xa_wkv", 1), ("ffn_w_down", 1)), (("ffn_w_gate_up", 1),), (("xa_wo", 1),),
           (("xa_wq", 1),), (("c_w_qkv", None),)),
    "l0a": ((("xa_wkv", 0), ("ffn_w_down", 0)), (("ffn_w_gate_up", 0),), (("xa_wo", 0),), (("xa_wq", 0),)),
    "l0b": ((("ab_w_out", None),), (("ab_w_in", None),), (("lru_wa", None), ("lru_wx", None))),
}
RS_STAGE = {"layer1": "l1", "layer0_ffn_xa": "l0a"}


def kernel(x, mem, mix_norm, ab_w_in, lru_conv_w, lru_conv_b, lru_wa, lru_ba, lru_wx, lru_bx, lru_lambda, ab_w_out, c_w_qkv, c_b_qkv, c_sinks, c_w_out, c_b_out, xa_norm, xa_mem_norm, xa_wq, xa_wkv, xa_wo, ffn_norm, ffn_w_gate_up, ffn_w_down, final_norm, loss_target, m_mix_norm, m_ab_w_in, m_lru_conv_w, m_lru_conv_b, m_lru_wa, m_lru_ba, m_lru_wx, m_lru_bx, m_lru_lambda, m_ab_w_out, m_c_w_qkv, m_c_b_qkv, m_c_sinks, m_c_w_out, m_c_b_out, m_xa_norm, m_xa_mem_norm, m_xa_wq, m_xa_wkv, m_xa_wo, m_ffn_norm, m_ffn_w_gate_up, m_ffn_w_down, m_final_norm, v_mix_norm, v_ab_w_in, v_lru_conv_w, v_lru_conv_b, v_lru_wa, v_lru_ba, v_lru_wx, v_lru_bx, v_lru_lambda, v_ab_w_out, v_c_w_qkv, v_c_b_qkv, v_c_sinks, v_c_w_out, v_c_b_out, v_xa_norm, v_xa_mem_norm, v_xa_wq, v_xa_wkv, v_xa_wo, v_ffn_norm, v_ffn_w_gate_up, v_ffn_w_down, v_final_norm):
    given = dict(locals())
    wl = {n: given[n] for n in WEIGHT_NAMES}
    ml = {n: given["m_" + n] for n in WEIGHT_NAMES}
    vl = {n: given["v_" + n] for n in WEIGHT_NAMES}
    xi, yi, ci = lax.axis_index("x"), lax.axis_index("y"), lax.axis_index("c")
    chip = 2 * xi + yi

    def join(parts, axis):
        return parts[0] if len(parts) == 1 else jnp.concatenate(parts, axis=axis)

    local_rows = {n: wl[n].size // wl[n].shape[-1] for grp in GROUPS for n in grp}
    packs = [join([wl[n].astype(BF16).reshape(local_rows[n], wl[n].shape[-1]) for n in grp], 0) for grp in GROUPS]
    spack = _pack_small([_rows(wl[n]) for n in SMALL_SHARDED], 8, name="pack_small_weights")
    n_early = len(EARLY_GROUPS)
    early, sfull = _gather_weights(packs[:n_early], spack)
    early, sfull, late_packs = lax.optimization_barrier((early, sfull, packs[n_early:]))
    gathered = early + _gather_weights_behind(late_packs)
    w = {n: wl[n] for n in REPLICATED}
    w["c_sinks"] = wl["c_sinks"][0]
    for grp, full in zip(GROUPS, gathered):
        off = 0
        for n in grp:
            w[n] = _from_shards(n, full if len(grp) == 1 else full[:, off:off + local_rows[n]])
            off += local_rows[n]
    for r, n in enumerate(SMALL_SHARDED):
        loc = wl[n].shape[1:]
        t = sfull[:, r, :wl[n].size].reshape((N_CHIPS,) + loc)
        if n == "lru_conv_w":
            w[n] = t.transpose(1, 0, 2).reshape(CONV_WIDTH, -1)
        elif n in ("lru_ba", "lru_bx"):
            w[n] = t.transpose(1, 0, 2).reshape(1, -1)
        else:
            w[n] = t.reshape(1, -1)

    place = jnp.stack([ci, chip]).astype(jnp.int32)

    def pair_stage(spec, g, tag):
        piece = lambda n, l: (g[n] if l is None else g[n, l]).astype(BF16)
        gpacks = [join([_piece_shards(n, piece(n, l)) for n, l in grp], 1) for grp in spec]
        ras = _rs_pair_exchange(gpacks, name=f"rs_pair_exchange_{tag}")
        sums = [_rs_pair_add(place, gp, ra, name=f"rs_pair_add_{tag}_{i}") for i, (gp, ra) in enumerate(zip(gpacks, ras))]
        return [pair for pair, _ in sums], [own for _, own in sums]

    owns, rbs = [], []

    def reduce_behind(stage, g):
        tag = RS_STAGE[stage]
        pairs, own = pair_stage(RS_SETS[tag], g, tag)
        owns.extend(own)
        rbs.extend(_rs_chip_exchange_behind(pairs, tag=tag))

    loss_part, grad_x, g = _device_step(x[0], mem[0], loss_target[0], w, on_grads=reduce_behind)

    small_parts = [_rows(g[n]) for n in REPLICATED] + [_rows(jnp.broadcast_to(loss_part, (LANES,)))]
    small_parts += [_rows(g[n]) for n in SMALL_SHARDED]
    small = _pack_small(small_parts, 24, name="pack_small_grads")
    pairs, own = pair_stage(RS_SETS["l0b"], g, "l0b")
    *rb, rs = _rs_chip_exchange_behind(pairs, tag="l0b", small=small)

    def finish(own_sums, received, first, name):
        return _rs_sibling_share([_rs_final_add(place, o, r, name=f"rs_final_add_{first + i}")
                                  for i, (o, r) in enumerate(zip(own_sums, received))], name=name)

    gsums = finish(owns, rbs, 0, "rs_sibling_share_behind") + finish(own, rb, len(owns), "rs_sibling_share_last")
    ssum = _sum_slots(rs)

    where = {}
    for grp, gsum in zip(RS_SETS["l1"] + RS_SETS["l0a"] + RS_SETS["l0b"], gsums):
        off = 0
        for n, l in grp:
            rows = local_rows[n] if l is None else local_rows[n] // 2
            where[n, l] = (gsum, off, rows, len(grp) == 1)
            off += rows
    take = lambda gsum, off, rows, whole: gsum if whole else gsum[off:off + rows]
    grads, grad_rows = {}, {}
    for grp in LATE_GROUPS + EARLY_GROUPS:
        for n in grp:
            if (n, None) in where:
                grads[n] = take(*where[n, None]).reshape(wl[n].shape)
                grad_rows[n] = where[n, None][:2]
            else:
                grads[n] = jnp.stack([take(*where[n, l]).reshape(wl[n].shape[1:]) for l in range(2)])
                grad_rows[n] = (grads[n].reshape(local_rows[n], wl[n].shape[-1]), 0)
    row = 0
    for n in REPLICATED:
        k = _rows(g[n]).shape[0]
        grads[n] = ssum[row:row + k].reshape(-1)[:wl[n].size].reshape(wl[n].shape)
        row += k
    loss = ssum[row, 0]
    row += 1
    for n in SMALL_SHARDED:
        k = _rows(g[n]).shape[0]
        full = ssum[row:row + k].reshape(-1)[:g[n].size]
        row += k
        loc = wl[n].shape
        if n == "lru_conv_w":
            sh = full.reshape(CONV_WIDTH, N_CHIPS, -1)
        elif n in ("lru_ba", "lru_bx"):
            sh = full.reshape(LRU_HEADS, N_CHIPS, -1)
        else:
            sh = full.reshape(1, N_CHIPS, -1)
        grads[n] = lax.dynamic_index_in_dim(sh, chip, axis=1, keepdims=False).reshape(loc)

    delta, new_m, new_v = {}, {}, {}
    for n, (gsum, off) in grad_rows.items():
        shape2 = (local_rows[n], wl[n].shape[-1])
        d, nm, nv = _adamw(wl[n].reshape(shape2), gsum, ml[n].reshape(shape2), vl[n].reshape(shape2), g_row=off,
                           name="adamw_" + n)
        delta[n], new_m[n], new_v[n] = (t.reshape(wl[n].shape) for t in (d, nm, nv))
    smalls = REPLICATED + SMALL_SHARDED
    packs = [_pack_small([_rows(src[n]) for n in smalls], 24, name="pack_adamw_" + tag)
             for tag, src in (("w", wl), ("g", grads), ("m", ml), ("v", vl))]
    outs = _adamw(*packs, name="adamw_small")
    row = 0
    for n in smalls:
        k = _rows(wl[n]).shape[0]
        for dst, o in zip((delta, new_m, new_v), outs):
            dst[n] = o[row:row + k].reshape(-1)[:wl[n].size].reshape(wl[n].shape)
        row += k

    return (loss, grad_x[None], *[grads[n] for n in WEIGHT_NAMES], *[delta[n] for n in WEIGHT_NAMES],
            *[new_m[n] for n in WEIGHT_NAMES], *[new_v[n] for n in WEIGHT_NAMES])
```

```python
import jax
import jax.numpy as jnp
from jax import lax
from jax.experimental import pallas as pl
from jax.experimental.pallas import tpu as pltpu
from jax.experimental.pallas import tpu_sc as plsc

F32, BF16 = jnp.float32, jnp.bfloat16
D_MODEL = 1024
NORM_EPS = 1e-6
ROPE_THETA = 500000.0
HEAD_DIM = 64
ROT_DIM = 16
BLK = 128
LRU_HEADS, LRU_HEAD_DIM, CONV_WIDTH, LRU_C = 4, 256, 4, 8.0
DILATED_PATTERN = ((128, 1), (512, 4), (2048, 16))
B_HEADS, C_HEADS, C_KV_HEADS, C_WINDOW = 8, 16, 2, 128
XA_HEADS, XA_HEAD_DIM, N_MEM = 4, 128, 256
D_FF = 2816
NEG = -1e30
ADAM_LR, ADAM_B1, ADAM_B2, ADAM_EPS, ADAM_WD, ADAM_STEP = 0.001, 0.9, 0.999, 1e-08, 0.01, 10
N_CHIPS = 4
VMEM_LIMIT_V7X = 56 * 1024 * 1024

NN = (((1,), (0,)), ((), ()))
NT = (((1,), (1,)), ((), ()))
TN = (((0,), (0,)), ((), ()))


def _dot(a, b, dims=NN):
    return lax.dot_general(a, b, dims, preferred_element_type=F32)


def _sigmoid(x):
    return 1.0 / (1.0 + jnp.exp(-x))


def _call(body, *, name, grid, in_specs, out_specs, out_shape, scratch=(), sem=None):
    return pl.pallas_call(
        body, name=name, grid=grid, in_specs=in_specs, out_specs=out_specs, out_shape=out_shape,
        scratch_shapes=list(scratch),
        compiler_params=pltpu.CompilerParams(dimension_semantics=sem, vmem_limit_bytes=VMEM_LIMIT_V7X))


def _rope_tables(L):
    half = ROT_DIM // 2
    inv = ROPE_THETA ** (-jnp.arange(0, ROT_DIM, 2, dtype=F32) / ROT_DIM)
    ang = jnp.arange(L, dtype=F32)[:, None] * inv[None, :]
    cos, sin = jnp.cos(ang), jnp.sin(ang)
    rest = HEAD_DIM - ROT_DIM
    z8, zr, one = jnp.zeros((L, half), F32), jnp.zeros((L, rest), F32), jnp.ones((L, rest), F32)
    c = jnp.concatenate([cos, cos, one], axis=1)
    s1 = jnp.concatenate([-sin, z8, zr], axis=1)
    s2 = jnp.concatenate([z8, sin, zr], axis=1)
    return tuple(jnp.concatenate([t, t], axis=1) for t in (c, s1, s2))


def _rope_fwd(v, c, s1, s2):
    return v * c + pltpu.roll(v, 120, 1) * s1 + pltpu.roll(v, 8, 1) * s2


def _rope_bwd(dv, c, s1, s2):
    return dv * c + pltpu.roll(dv * s1, 8, 1) + pltpu.roll(dv * s2, 120, 1)


def _weight_spec(w, layer):
    once = pl.Buffered(1)
    if layer is None:
        return w.shape, pl.BlockSpec(w.shape, lambda i: (0, 0, 0), pipeline_mode=once)
    S, _, K, Ns = w.shape
    return (S, K, Ns), pl.BlockSpec((S, None, K, Ns), lambda i: (0, layer, 0, 0), pipeline_mode=once)


def _rowmm(a, w3, *, name, tm=512, gain=None, bias=None, res=None, swiglu=False, rope=None, layer=None):
    M, K = a.shape
    (S, _, Ns), w_spec = _weight_spec(w3, layer)
    N = S * Ns
    tm = min(tm, M)
    has_norm, has_bias, has_res, has_rope = gain is not None, bias is not None, res is not None, rope is not None
    row = lambda w: pl.BlockSpec((tm, w), lambda i: (i, 0))
    whole = lambda shape: pl.BlockSpec(shape, lambda i: (0,) * len(shape))
    ins, specs = [a], [row(K)]
    if has_norm:
        ins.append(gain.reshape(1, K)); specs.append(whole((1, K)))
    ins.append(w3); specs.append(w_spec)
    if has_bias:
        ins.append(bias.reshape(1, N)); specs.append(whole((1, N)))
    if has_res:
        ins.append(res); specs.append(row(N))
    if has_rope:
        ins += list(rope[2]); specs += [row(128)] * 3
    y_dtype = F32 if has_res else BF16
    out_shape, out_specs = [jax.ShapeDtypeStruct((M, N), y_dtype)], [row(N)]
    if has_norm:
        out_shape.append(jax.ShapeDtypeStruct((M, K), BF16)); out_specs.append(row(K))
    if swiglu:
        out_shape.append(jax.ShapeDtypeStruct((M, N // 2), BF16)); out_specs.append(row(N // 2))
    scratch = [pltpu.VMEM((tm, N), F32)] if has_rope else []

    def body(*refs):
        it = iter(refs)
        a_ref = next(it)
        g_ref = next(it) if has_norm else None
        w_ref = next(it)
        b_ref = next(it) if has_bias else None
        r_ref = next(it) if has_res else None
        tabs = [next(it) for _ in range(3)] if has_rope else None
        y_ref = next(it)
        n_ref = next(it) if has_norm else None
        act_ref = next(it) if swiglu else None
        ys_ref = next(it) if has_rope else None
        if has_norm:
            x = a_ref[...].astype(F32)
            ms = jnp.mean(x * x, axis=-1, keepdims=True)
            xb = (x * lax.rsqrt(ms + NORM_EPS) * g_ref[...]).astype(BF16)
            n_ref[...] = xb
        else:
            xb = a_ref[...].astype(BF16)
        if swiglu:
            for s in range(S // 2):
                g = _dot(xb, w_ref[s])
                u = _dot(xb, w_ref[s + S // 2])
                y_ref[:, s * Ns:(s + 1) * Ns] = g.astype(BF16)
                y_ref[:, N // 2 + s * Ns:N // 2 + (s + 1) * Ns] = u.astype(BF16)
                act_ref[:, s * Ns:(s + 1) * Ns] = (g * _sigmoid(g) * u).astype(BF16)
            return
        for s in range(S):
            sl = slice(s * Ns, (s + 1) * Ns)
            acc = _dot(xb, w_ref[s])
            if has_bias:
                acc = acc + b_ref[:, sl]
            if has_res:
                acc = acc + r_ref[:, sl]
            if has_rope:
                ys_ref[:, sl] = acc
            else:
                y_ref[:, sl] = acc.astype(y_dtype)
        if has_rope:
            c, s1, s2 = (t[...] for t in tabs)
            for cb in range(N // 128):
                cs = slice(cb * 128, (cb + 1) * 128)
                v = ys_ref[:, cs]
                if rope[0] <= cb * 128 < rope[1]:
                    v = _rope_fwd(v, c, s1, s2)
                y_ref[:, cs] = v.astype(BF16)

    return _call(body, name=name, grid=(M // tm,), in_specs=specs, out_specs=out_specs, out_shape=out_shape,
                 scratch=scratch, sem=("parallel",))(*ins)


def _mm_nt(dy, w3, *, name, mode, tm=512, kchunk=None, h=None, gain=None, dh=None, gu=None, layer=None):
    M, N = dy.shape
    (S, K, Ns), w_spec = _weight_spec(w3, layer)
    kchunk = kchunk or K
    tm = min(tm, M)
    row = lambda w: pl.BlockSpec((tm, w), lambda i: (i, 0))
    whole = lambda shape: pl.BlockSpec(shape, lambda i: (0,) * len(shape))
    ins, specs = [dy, w3], [row(N), w_spec]
    has_dh = dh is not None
    if mode == "norm":
        ins += [h, gain.reshape(1, K)]; specs += [row(K), whole((1, K))]
        if has_dh:
            ins.append(dh); specs.append(row(K))
        out_shape = [jax.ShapeDtypeStruct((M, K), F32), jax.ShapeDtypeStruct((1, K), F32)]
        out_specs = [row(K), whole((1, K))]
    elif mode == "swiglu":
        ins.append(gu); specs.append(row(2 * K))
        out_shape, out_specs = [jax.ShapeDtypeStruct((M, 2 * K), BF16)], [row(2 * K)]
    else:
        out_shape, out_specs = [jax.ShapeDtypeStruct((M, K), BF16)], [row(K)]

    def body(*refs):
        it = iter(refs)
        dy_ref, w_ref = next(it), next(it)
        if mode == "norm":
            h_ref, g_ref = next(it), next(it)
            dh_ref = next(it) if has_dh else None
            o_ref, dg_ref = next(it), next(it)
        elif mode == "swiglu":
            gu_ref, o_ref = next(it), next(it)
        else:
            o_ref = next(it)
        for kc in range(K // kchunk):
            ks = slice(kc * kchunk, (kc + 1) * kchunk)
            acc = None
            for s in range(S):
                t = _dot(dy_ref[:, s * Ns:(s + 1) * Ns].astype(BF16), w_ref[s, ks, :], NT)
                acc = t if acc is None else acc + t
            if mode == "plain":
                o_ref[:, ks] = acc.astype(BF16)
            elif mode == "swiglu":
                us = slice(K + kc * kchunk, K + (kc + 1) * kchunk)
                g = gu_ref[:, ks].astype(F32)
                u = gu_ref[:, us].astype(F32)
                sg = _sigmoid(g)
                o_ref[:, ks] = (acc * u * (sg * (1.0 + g * (1.0 - sg)))).astype(BF16)
                o_ref[:, us] = (acc * (g * sg)).astype(BF16)
            else:
                x = h_ref[...].astype(F32)
                r = lax.rsqrt(jnp.mean(x * x, axis=-1, keepdims=True) + NORM_EPS)
                xhat = x * r
                dxh = acc * g_ref[...]
                dx = r * (dxh - xhat * jnp.mean(dxh * xhat, axis=-1, keepdims=True))
                o_ref[...] = dx + dh_ref[...] if has_dh else dx

                @pl.when(pl.program_id(0) == 0)
                def _():
                    dg_ref[...] = jnp.zeros_like(dg_ref)

                dg_ref[...] += jnp.sum(acc * xhat, axis=0, keepdims=True)

    sem = ("arbitrary",) if mode == "norm" else ("parallel",)
    return _call(body, name=name, grid=(M // tm,), in_specs=specs, out_specs=out_specs, out_shape=out_shape, sem=sem)(*ins)


def _mm_tn(x, dy, *, S, name, tk=1024, kk=None, bias=False):
    M, K = x.shape
    N = dy.shape[1]
    Ns = N // S
    kk = kk or K
    tk = min(tk, M)
    nl = M // tk
    in_specs = [pl.BlockSpec((tk, kk), lambda s, kc, l: (l, kc)), pl.BlockSpec((tk, Ns), lambda s, kc, l: (l, s))]
    out_shape = [jax.ShapeDtypeStruct((S, K, Ns), BF16)]
    out_specs = [pl.BlockSpec((None, kk, Ns), lambda s, kc, l: (s, kc, 0))]
    if bias:
        out_shape.append(jax.ShapeDtypeStruct((1, N), F32))
        out_specs.append(pl.BlockSpec((1, Ns), lambda s, kc, l: (0, s)))

    def body(x_ref, dy_ref, o_ref, *rest):
        acc_ref = rest[-1]
        kc, l = pl.program_id(1), pl.program_id(2)

        @pl.when(l == 0)
        def _():
            acc_ref[...] = jnp.zeros_like(acc_ref)

        acc_ref[...] += _dot(x_ref[...].astype(BF16), dy_ref[...].astype(BF16), TN)
        if bias:
            b_ref = rest[0]

            @pl.when((kc == 0) & (l == 0))
            def _():
                b_ref[...] = jnp.zeros_like(b_ref)

            @pl.when(kc == 0)
            def _():
                b_ref[...] += jnp.sum(dy_ref[...].astype(F32), axis=0, keepdims=True)

        @pl.when(l == nl - 1)
        def _():
            o_ref[...] = acc_ref[...].astype(BF16)

    return _call(body, name=name, grid=(S, K // kk, nl), in_specs=in_specs, out_specs=out_specs, out_shape=out_shape,
                 scratch=[pltpu.VMEM((kk, Ns), F32)], sem=("arbitrary", "arbitrary", "arbitrary"))(x, dy)


def _band_bias(max_dist, has_prev):
    rows = lax.broadcasted_iota(jnp.int32, (BLK, 2 * BLK), 0)
    cols = lax.broadcasted_iota(jnp.int32, (BLK, 2 * BLK), 1)
    dist = rows - cols + BLK
    ok = (dist >= 0) & (dist <= max_dist) & ((cols >= BLK) | has_prev)
    return jnp.where(ok, 0.0, NEG)


Q_SCALE = HEAD_DIM ** -0.5


def _band_fwd(qa, ka, va, *, d, nq, nkv, qcol, kcol, vcol, max_dist, sinks=None, name):
    Lr = qa.shape[0]
    nb = Lr // BLK
    qw, kw, G = nq * HEAD_DIM, nkv * HEAD_DIM, nq // nkv
    cur = lambda colf, w: pl.BlockSpec((BLK, w), lambda r, i: (i, colf(r)))
    prv = lambda colf, w: pl.BlockSpec((BLK, w), lambda r, i: (jnp.maximum(i - 1, 0), colf(r)))
    out = pl.BlockSpec((BLK, qw), lambda r, i: (i, r))
    ins, specs = [qa, ka, ka, va, va], [cur(qcol, qw), cur(kcol, kw), prv(kcol, kw), cur(vcol, kw), prv(vcol, kw)]
    has_sinks = sinks is not None
    if has_sinks:
        ins.append(sinks); specs.append(pl.BlockSpec(memory_space=pltpu.SMEM))

    def body(*refs):
        q_ref, kc_ref, kp_ref, vc_ref, vp_ref = refs[:5]
        sk_ref = refs[5] if has_sinks else None
        o_ref, lse_ref = refs[-2], refs[-1]
        bias = _band_bias(max_dist, pl.program_id(1) > 0)
        k2 = jnp.concatenate([kp_ref[...], kc_ref[...]], axis=0)
        v2 = jnp.concatenate([vp_ref[...], vc_ref[...]], axis=0)
        for h in range(nq):
            hs = slice(h * HEAD_DIM, (h + 1) * HEAD_DIM)
            ks = slice((h // G) * HEAD_DIM, (h // G + 1) * HEAD_DIM)
            s = _dot(q_ref[:, hs] * jnp.asarray(Q_SCALE, BF16), k2[:, ks], NT) + bias
            m = jnp.max(s, axis=-1, keepdims=True)
            if has_sinks:
                m = jnp.maximum(m, sk_ref[h])
            p = jnp.exp(s - m)
            l = jnp.sum(p, axis=-1, keepdims=True)
            if has_sinks:
                l = l + jnp.exp(sk_ref[h] - m)
            o_ref[:, hs] = (_dot(p.astype(BF16), v2[:, ks]) / l).astype(BF16)
            lse_ref[:, hs] = jnp.broadcast_to(m + jnp.log(l), (BLK, HEAD_DIM))

    return _call(body, name=name, grid=(d, nb), in_specs=specs, out_specs=[out, out],
                 out_shape=[jax.ShapeDtypeStruct((Lr, d * qw), BF16), jax.ShapeDtypeStruct((Lr, d * qw), F32)],
                 sem=("parallel", "parallel"))(*ins)


def _band_bwd(qa, ka, va, doa, oa, lsea, *, d, nq, nkv, qcol, kcol, vcol, docol, max_dist, sinks=None, name):
    Lr = qa.shape[0]
    nb = Lr // BLK
    qw, kw, G = nq * HEAD_DIM, nkv * HEAD_DIM, nq // nkv
    transposed = G > 1
    last = lambda i: jnp.minimum(i, nb - 1)
    cur = lambda colf, w: pl.BlockSpec((BLK, w), lambda r, i: (last(i), colf(r)))
    prv = lambda colf, w: pl.BlockSpec((BLK, w), lambda r, i: (jnp.maximum(last(i) - 1, 0), colf(r)))
    own = lambda r: r
    ins = [qa, ka, ka, va, va, doa, oa, lsea]
    specs = [cur(qcol, qw), cur(kcol, kw), prv(kcol, kw), cur(vcol, kw), prv(vcol, kw), cur(docol, qw), cur(own, qw),
             cur(own, qw)]
    has_sinks = sinks is not None
    if has_sinks:
        ins.append(sinks); specs.append(pl.BlockSpec(memory_space=pltpu.SMEM))
    out_shape = [jax.ShapeDtypeStruct((Lr, d * qw), F32), jax.ShapeDtypeStruct((Lr, d * kw), F32),
                 jax.ShapeDtypeStruct((Lr, d * kw), F32)]
    behind = lambda r, i: (jnp.maximum(i - 1, 0), r)
    out_specs = [pl.BlockSpec((BLK, qw), lambda r, i: (last(i), r)), pl.BlockSpec((BLK, kw), behind),
                 pl.BlockSpec((BLK, kw), behind)]
    if has_sinks:
        out_shape.append(jax.ShapeDtypeStruct((8, 128), F32))
        out_specs.append(pl.BlockSpec((8, 128), lambda r, i: (0, 0)))

    def body(*refs):
        it = iter(refs)
        q_ref, kc_ref, kp_ref, vc_ref, vp_ref, do_ref, o_ref, ls_ref = (next(it) for _ in range(8))
        sk_ref = next(it) if has_sinks else None
        dq_ref, dk_ref, dv_ref = next(it), next(it), next(it)
        dsk_ref = next(it) if has_sinks else None
        dk_car, dv_car = next(it), next(it)
        r_id, i = pl.program_id(0), pl.program_id(1)

        @pl.when(i == 0)
        def _():
            dk_car[...] = jnp.zeros_like(dk_car)
            dv_car[...] = jnp.zeros_like(dv_car)

        if has_sinks:
            @pl.when((r_id == 0) & (i == 0))
            def _():
                dsk_ref[...] = jnp.zeros_like(dsk_ref)

        @pl.when(i == nb)
        def _():
            dk_ref[...] = dk_car[...]
            dv_ref[...] = dv_car[...]

        @pl.when(i < nb)
        def _():
            bias = _band_bias(max_dist, i > 0)
            k2 = jnp.concatenate([kp_ref[...], kc_ref[...]], axis=0)
            v2 = jnp.concatenate([vp_ref[...], vc_ref[...]], axis=0)
            if has_sinks:
                lane = lax.broadcasted_iota(jnp.int32, (8, 128), 1)
                dsk = jnp.zeros((8, 128), F32)
            for kv in range(nkv):
                ks = slice(kv * HEAD_DIM, (kv + 1) * HEAD_DIM)
                kh, vh = k2[:, ks], v2[:, ks]
                shape = (HEAD_DIM, 2 * BLK) if transposed else (2 * BLK, HEAD_DIM)
                dk, dv = jnp.zeros(shape, F32), jnp.zeros(shape, F32)
                for g in range(G):
                    h = kv * G + g
                    hs = slice(h * HEAD_DIM, (h + 1) * HEAD_DIM)
                    q = q_ref[:, hs] * jnp.asarray(Q_SCALE, BF16)
                    do = do_ref[:, hs]
                    lse = ls_ref[:, h * HEAD_DIM:h * HEAD_DIM + 1]
                    dl = jnp.sum(do.astype(F32) * o_ref[:, hs].astype(F32), axis=-1, keepdims=True)
                    p = jnp.exp(_dot(q, kh, NT) + bias - lse)
                    ds = (p * (_dot(do, vh, NT) - dl)).astype(BF16)
                    dq_ref[:, hs] = _dot(ds, kh) * Q_SCALE
                    if transposed:
                        dk = dk + _dot(q, ds, TN)
                        dv = dv + _dot(do, p.astype(BF16), TN)
                    else:
                        dk = dk + _dot(ds, q, TN)
                        dv = dv + _dot(p.astype(BF16), do, TN)
                    if has_sinks:
                        val = -jnp.sum(jnp.exp(sk_ref[h] - lse) * dl, axis=0, keepdims=True)
                        dsk = dsk + jnp.where(lane == h, val, 0.0)
                if transposed:
                    dk, dv = dk.T, dv.T
                dk_ref[:, ks] = dk_car[:, ks] + dk[:BLK]
                dv_ref[:, ks] = dv_car[:, ks] + dv[:BLK]
                dk_car[:, ks] = dk[BLK:]
                dv_car[:, ks] = dv[BLK:]
            if has_sinks:
                dsk_ref[...] += dsk

    return _call(body, name=name, grid=(d, nb + 1), in_specs=specs, out_specs=out_specs, out_shape=out_shape,
                 scratch=[pltpu.VMEM((BLK, kw), F32), pltpu.VMEM((BLK, kw), F32)], sem=("arbitrary", "arbitrary"))(*ins)


def _attn_grad_combine(branches, tabs, *, name, tm=256):
    L, qw = branches[0][0].shape
    kw = branches[0][1].shape[1]
    nbr = len(branches)
    row = lambda w: pl.BlockSpec((tm, w), lambda i: (i, 0))
    ins, specs = [], []
    for dq, dk, dv in branches:
        ins += [dq, dk, dv]; specs += [row(qw), row(kw), row(kw)]
    ins += list(tabs); specs += [row(128)] * 3

    def body(*refs):
        c, s1, s2 = (t[...] for t in refs[3 * nbr:3 * nbr + 3])
        o_ref = refs[-1]
        for part, (w, off, rot) in enumerate(((qw, 0, True), (kw, qw, True), (kw, qw + kw, False))):
            for cb in range(w // 128):
                cs = slice(cb * 128, (cb + 1) * 128)
                v = refs[part][:, cs]
                for b in range(1, nbr):
                    v = v + refs[3 * b + part][:, cs]
                if rot:
                    v = _rope_bwd(v, c, s1, s2)
                o_ref[:, off + cb * 128:off + (cb + 1) * 128] = v.astype(BF16)

    return _call(body, name=name, grid=(L // tm,), in_specs=specs, out_specs=row(qw + 2 * kw),
                 out_shape=jax.ShapeDtypeStruct((L, qw + 2 * kw), BF16), sem=("parallel",))(*ins)


def _xattn_fwd(q, kv, *, name, tq=512):
    L, W = q.shape
    scale = XA_HEAD_DIM ** -0.5
    row = pl.BlockSpec((tq, W), lambda i: (i, 0))
    kvs = pl.BlockSpec((N_MEM, 2 * W), lambda i: (0, 0))

    def body(q_ref, kv_ref, o_ref, lse_ref):
        for h in range(XA_HEADS):
            hs = slice(h * XA_HEAD_DIM, (h + 1) * XA_HEAD_DIM)
            vs = slice(W + h * XA_HEAD_DIM, W + (h + 1) * XA_HEAD_DIM)
            s = _dot(q_ref[:, hs], kv_ref[:, hs], NT) * scale
            m = jnp.max(s, axis=-1, keepdims=True)
            p = jnp.exp(s - m)
            l = jnp.sum(p, axis=-1, keepdims=True)
            o_ref[:, hs] = (_dot(p.astype(BF16), kv_ref[:, vs]) / l).astype(BF16)
            lse_ref[:, hs] = jnp.broadcast_to(m + jnp.log(l), (tq, XA_HEAD_DIM))

    return _call(body, name=name, grid=(L // tq,), in_specs=[row, kvs], out_specs=[row, row],
                 out_shape=[jax.ShapeDtypeStruct((L, W), BF16), jax.ShapeDtypeStruct((L, W), F32)], sem=("parallel",))(q, kv)


def _xattn_bwd(q, kv, o, lse, do, *, name, tq=512):
    L, W = q.shape
    scale = XA_HEAD_DIM ** -0.5
    row = pl.BlockSpec((tq, W), lambda i: (i, 0))
    kvs = pl.BlockSpec((N_MEM, 2 * W), lambda i: (0, 0))

    def body(q_ref, kv_ref, o_ref, lse_ref, do_ref, dq_ref, dkv_ref):
        @pl.when(pl.program_id(0) == 0)
        def _():
            dkv_ref[...] = jnp.zeros_like(dkv_ref)

        for h in range(XA_HEADS):
            hs = slice(h * XA_HEAD_DIM, (h + 1) * XA_HEAD_DIM)
            vs = slice(W + h * XA_HEAD_DIM, W + (h + 1) * XA_HEAD_DIM)
            qh, kh, vh, doh = q_ref[:, hs], kv_ref[:, hs], kv_ref[:, vs], do_ref[:, hs]
            p = jnp.exp(_dot(qh, kh, NT) * scale - lse_ref[:, h * XA_HEAD_DIM:h * XA_HEAD_DIM + 1])
            dl = jnp.sum(doh.astype(F32) * o_ref[:, hs].astype(F32), axis=-1, keepdims=True)
            ds = (p * (_dot(doh, vh, NT) - dl) * scale).astype(BF16)
            dq_ref[:, hs] = _dot(ds, kh).astype(BF16)
            dkv_ref[:, hs] += _dot(ds, qh, TN)
            dkv_ref[:, vs] += _dot(p.astype(BF16), doh, TN)

    return _call(body, name=name, grid=(L // tq,), in_specs=[row, kvs, row, row, row], out_specs=[row, kvs],
                 out_shape=[jax.ShapeDtypeStruct((L, W), BF16), jax.ShapeDtypeStruct((N_MEM, 2 * W), F32)],
                 sem=("arbitrary",))(q, kv, o, lse, do)


def _neg_expm1(z):
    series = -(z * (1.0 + z * (0.5 + z * (1.0 / 6.0 + z * (1.0 / 24.0 + z * (1.0 / 120.0))))))
    return jnp.where(z > -0.05, series, 1.0 - jnp.exp(z))


def _softplus(z):
    return jnp.maximum(z, 0.0) + jnp.log(1.0 + jnp.exp(-jnp.abs(z)))


def _gelu_parts(y):
    c = 0.7978845608028654
    t = jnp.tanh(c * (y + 0.044715 * y * y * y))
    gy = 0.5 * y * (1.0 + t)
    dgy = 0.5 * (1.0 + t) + 0.5 * y * (1.0 - t * t) * c * (1.0 + 3.0 * 0.044715 * y * y)
    return gy, dgy


def _lru_gates(xc, wa_ref, ba, wx_ref, bx, sp):
    rs, igs = [], []
    for hd in range(LRU_HEADS):
        sl = slice(hd * LRU_HEAD_DIM, (hd + 1) * LRU_HEAD_DIM)
        xh = xc[:, sl].astype(BF16)
        rs.append(_sigmoid(_dot(xh, wa_ref[hd]) + ba[:, sl]))
        igs.append(_sigmoid(_dot(xh, wx_ref[hd]) + bx[:, sl]))
    r, ig = jnp.concatenate(rs, axis=1), jnp.concatenate(igs, axis=1)
    la = -LRU_C * r * sp
    return r, ig, jnp.exp(la), _neg_expm1(2.0 * la)


def _conv_taps(x_ext, halo):
    n = x_ext.shape[0]
    return [x_ext[halo:] if k == CONV_WIDTH - 1 else pltpu.roll(x_ext, CONV_WIDTH - 1 - k, 0)[halo:]
            for k in range(CONV_WIDTH)]


def _lru_fwd(proj, cw, cb, wa, ba, wx, bx, lam, *, name, tc=512):
    L = proj.shape[0]
    W = LRU_HEADS * LRU_HEAD_DIM
    nb = L // tc
    whole = lambda shape: pl.BlockSpec(shape, lambda i: (0,) * len(shape))
    specs = [pl.BlockSpec((tc, W), lambda i: (i, 0)), pl.BlockSpec((tc, W), lambda i: (i, 1)),
             pl.BlockSpec((16, W), lambda i: (jnp.maximum(i * (tc // 16) - 1, 0), 0)),
             whole((CONV_WIDTH, W)), whole((1, W)), whole((LRU_HEADS, LRU_HEAD_DIM, LRU_HEAD_DIM)), whole((1, W)),
             whole((LRU_HEADS, LRU_HEAD_DIM, LRU_HEAD_DIM)), whole((1, W)), whole((1, W))]
    out_specs = [pl.BlockSpec((tc, W), lambda i: (i, 0))] * 2
    out_shape = [jax.ShapeDtypeStruct((L, W), BF16), jax.ShapeDtypeStruct((L, W), F32)]

    def body(x_ref, y_ref, xh_ref, cw_ref, cb_ref, wa_ref, ba_ref, wx_ref, bx_ref, lam_ref, rec_ref, hs_ref,
             hcar, a_scr, b_scr):
        i = pl.program_id(0)

        @pl.when(i == 0)
        def _():
            hcar[...] = jnp.zeros_like(hcar)

        halo = jnp.where(i > 0, xh_ref[...].astype(F32), 0.0)
        taps = _conv_taps(jnp.concatenate([halo, x_ref[...].astype(F32)], axis=0), 16)
        xc = cb_ref[...] + sum(cw_ref[k:k + 1, :] * taps[k] for k in range(CONV_WIDTH))
        _, ig, a, om = _lru_gates(xc, wa_ref, ba_ref[...], wx_ref, bx_ref[...], _softplus(-lam_ref[...]))
        b = jnp.sqrt(om) * (ig * xc)
        rowmod = lax.broadcasted_iota(jnp.int32, (tc, W), 0) & 7
        for s in (1, 2, 4):
            keep = rowmod >= s
            b = jnp.where(keep, a * pltpu.roll(b, s, 0) + b, b)
            a = jnp.where(keep, a * pltpu.roll(a, s, 0), a)
        a_scr[...] = a
        b_scr[...] = b

        def tile(j, hc):
            rows = pl.ds(pl.multiple_of(j * 8, 8), 8)
            ht = a_scr[rows, :] * hc + b_scr[rows, :]
            hs_ref[rows, :] = ht
            return jnp.broadcast_to(ht[7:8, :], (8, W))

        hcar[...] = lax.fori_loop(0, tc // 8, tile, hcar[...])
        gy, _ = _gelu_parts(y_ref[...].astype(F32))
        rec_ref[...] = (hs_ref[...] * gy).astype(BF16)

    return _call(body, name=name, grid=(nb,), in_specs=specs, out_specs=out_specs, out_shape=out_shape,
                 scratch=[pltpu.VMEM((8, W), F32), pltpu.VMEM((tc, W), F32), pltpu.VMEM((tc, W), F32)],
                 sem=("arbitrary",))(proj, proj, proj, cw, cb, wa, ba, wx, bx, lam)


def _lru_bwd(proj, hs, drec_src, cw, cb, wa, ba, wx, bx, lam, *, name, tc=256):
    L = proj.shape[0]
    W = LRU_HEADS * LRU_HEAD_DIM
    nb = L // tc
    tb = lambda i: nb - 1 - i
    whole = lambda shape: pl.BlockSpec(shape, lambda i: (0,) * len(shape))
    gate_w = (LRU_HEADS, LRU_HEAD_DIM, LRU_HEAD_DIM)
    specs = [pl.BlockSpec((tc, W), lambda i: (tb(i), 0)), pl.BlockSpec((tc, W), lambda i: (tb(i), 1)),
             pl.BlockSpec((16, W), lambda i: (jnp.maximum(tb(i) * (tc // 16) - 1, 0), 0)),
             pl.BlockSpec((tc, W), lambda i: (tb(i), 0)),
             pl.BlockSpec((8, W), lambda i: (jnp.maximum(tb(i) * (tc // 8) - 1, 0), 0)),
             pl.BlockSpec((tc, W), lambda i: (tb(i), 0)),
             whole((CONV_WIDTH, W)), whole((1, W)), whole(gate_w), whole((1, W)), whole(gate_w), whole((1, W)), whole((1, W))]
    out_specs = [pl.BlockSpec((tc, 2 * W), lambda i: (tb(i), 0)), whole((CONV_WIDTH, W)), whole((1, W)), whole(gate_w),
                 whole((1, W)), whole(gate_w), whole((1, W)), whole((1, W))]
    vec = jax.ShapeDtypeStruct((1, W), F32)
    out_shape = [jax.ShapeDtypeStruct((L, 2 * W), BF16), jax.ShapeDtypeStruct((CONV_WIDTH, W), F32), vec,
                 jax.ShapeDtypeStruct(gate_w, F32), vec, jax.ShapeDtypeStruct(gate_w, F32), vec, vec]

    def body(x_ref, y_ref, xh_ref, hs_ref, hh_ref, dr_ref, cw_ref, cb_ref, wa_ref, ba_ref, wx_ref, bx_ref, lam_ref,
             dxy_ref, dcw_ref, dcb_ref, dwa_ref, dba_ref, dwx_ref, dbx_ref, dlam_ref, gcar, dxc_car, a_scr, b_scr, g_scr):
        pid = pl.program_id(0)
        t = tb(pid)
        accs = (dcw_ref, dcb_ref, dwa_ref, dba_ref, dwx_ref, dbx_ref, dlam_ref)

        @pl.when(pid == 0)
        def _():
            gcar[...] = jnp.zeros_like(gcar)
            dxc_car[...] = jnp.zeros_like(dxc_car)
            for r in accs:
                r[...] = jnp.zeros_like(r)

        halo = jnp.where(t > 0, xh_ref[...].astype(F32), 0.0)
        taps = _conv_taps(jnp.concatenate([halo, x_ref[...].astype(F32)], axis=0), 16)
        xc = cb_ref[...] + sum(cw_ref[k:k + 1, :] * taps[k] for k in range(CONV_WIDTH))
        lam = lam_ref[...]
        sp = _softplus(-lam)
        r, ig, a, om = _lru_gates(xc, wa_ref, ba_ref[...], wx_ref, bx_ref[...], sp)
        sq = jnp.sqrt(om)
        hblk = hs_ref[...]
        hprev = pltpu.roll(jnp.concatenate([jnp.where(t > 0, hh_ref[...], 0.0), hblk], axis=0), 1, 0)[8:]
        gy, dgy = _gelu_parts(y_ref[...].astype(F32))
        drec = dr_ref[...].astype(F32)
        dxy_ref[:, W:] = (drec * hblk * dgy).astype(BF16)

        rowidx = lax.broadcasted_iota(jnp.int32, (tc, W), 0)
        rowmod = rowidx & 7
        ca = jnp.where(rowidx == tc - 1, 1.0, pltpu.roll(a, tc - 1, 0))
        cbv = drec * gy
        for s in (1, 2, 4):
            keep = rowmod < 8 - s
            cbv = jnp.where(keep, ca * pltpu.roll(cbv, tc - s, 0) + cbv, cbv)
            ca = jnp.where(keep, ca * pltpu.roll(ca, tc - s, 0), ca)
        a_scr[...] = ca
        b_scr[...] = cbv

        def tile(k, gc):
            j = tc // 8 - 1 - k
            rows = pl.ds(pl.multiple_of(j * 8, 8), 8)
            gt = a_scr[rows, :] * gc + b_scr[rows, :]
            g_scr[rows, :] = gt
            return jnp.broadcast_to(gt[0:1, :], (8, W))

        lax.fori_loop(0, tc // 8, tile, gcar[...])
        G = g_scr[...]
        gcar[...] = jnp.broadcast_to(a[0:1, :] * G[0:1, :], (8, W))

        da = G * hprev
        dsq = G * (ig * xc)
        di = G * (sq * xc)
        dxc = G * (sq * ig)
        dla = da * a - 2.0 * a * a * (dsq * 0.5 * lax.rsqrt(om))
        dlam_ref[...] += jnp.sum(dla * (-LRU_C * r), axis=0, keepdims=True) * (-_sigmoid(-lam))
        dpr = dla * (-LRU_C * sp) * r * (1.0 - r)
        dpi = di * ig * (1.0 - ig)
        dba_ref[...] += jnp.sum(dpr, axis=0, keepdims=True)
        dbx_ref[...] += jnp.sum(dpi, axis=0, keepdims=True)
        back = []
        for hd in range(LRU_HEADS):
            sl = slice(hd * LRU_HEAD_DIM, (hd + 1) * LRU_HEAD_DIM)
            xh, dprh, dpih = xc[:, sl].astype(BF16), dpr[:, sl].astype(BF16), dpi[:, sl].astype(BF16)
            back.append(_dot(dprh, wa_ref[hd], NT) + _dot(dpih, wx_ref[hd], NT))
            dwa_ref[hd] += _dot(xh, dprh, TN)
            dwx_ref[hd] += _dot(xh, dpih, TN)
        dxc = dxc + jnp.concatenate(back, axis=1)
        dcb_ref[...] += jnp.sum(dxc, axis=0, keepdims=True)
        for k in range(CONV_WIDTH):
            dcw_ref[k:k + 1, :] += jnp.sum(dxc * taps[k], axis=0, keepdims=True)
        ext = jnp.concatenate([dxc, dxc_car[...]], axis=0)
        dx = cw_ref[CONV_WIDTH - 1:CONV_WIDTH, :] * dxc
        for k in range(CONV_WIDTH - 1):
            dx = dx + cw_ref[k:k + 1, :] * pltpu.roll(ext, tc + 8 - (CONV_WIDTH - 1 - k), 0)[:tc]
        dxc_car[...] = dxc[0:8, :]
        dxy_ref[:, :W] = dx.astype(BF16)

    scratch = [pltpu.VMEM((8, W), F32), pltpu.VMEM((8, W), F32)] + [pltpu.VMEM((tc, W), F32)] * 3
    return _call(body, name=name, grid=(nb,), in_specs=specs, out_specs=out_specs, out_shape=out_shape, scratch=scratch,
                 sem=("arbitrary",))(proj, proj, proj, hs, hs, drec_src, cw, cb, wa, ba, wx, bx, lam)


def _final_loss(h, gain, target, *, name, tm=256):
    M, K = h.shape
    row = pl.BlockSpec((tm, K), lambda i: (i, 0))
    vec = pl.BlockSpec((1, K), lambda i: (0, 0))
    one = pl.BlockSpec((1, 128), lambda i: (0, 0))

    def body(h_ref, g_ref, t_ref, dh_ref, dg_ref, loss_ref):
        @pl.when(pl.program_id(0) == 0)
        def _():
            dg_ref[...] = jnp.zeros_like(dg_ref)
            loss_ref[...] = jnp.zeros_like(loss_ref)

        x = h_ref[...]
        r = lax.rsqrt(jnp.mean(x * x, axis=-1, keepdims=True) + NORM_EPS)
        xhat = x * r
        err = xhat * g_ref[...] - t_ref[...]
        loss_ref[...] += 0.5 / K * jnp.sum(err * err)
        dy = err * (1.0 / K)
        dg_ref[...] += jnp.sum(dy * xhat, axis=0, keepdims=True)
        dxh = dy * g_ref[...]
        dh_ref[...] = r * (dxh - xhat * jnp.mean(dxh * xhat, axis=-1, keepdims=True))

    return _call(body, name=name, grid=(M // tm,), in_specs=[row, vec, row], out_specs=[row, vec, one],
                 out_shape=[jax.ShapeDtypeStruct((M, K), F32), jax.ShapeDtypeStruct((1, K), F32),
                            jax.ShapeDtypeStruct((1, 128), F32)], sem=("arbitrary",))(h, gain.reshape(1, K), target)


def _dilated_merge(branches, *, name, tm=512):
    L, W = branches[0].shape
    nbr = len(branches) // 2
    row = pl.BlockSpec((tm, W), lambda i: (i, 0))

    def body(*refs):
        o_ref, lse_ref = refs[-2], refs[-1]
        lses = [refs[2 * b + 1][...] for b in range(nbr)]
        m = lses[0]
        for t in lses[1:]:
            m = jnp.maximum(m, t)
        ws = [jnp.exp(t - m) for t in lses]
        den = ws[0]
        for t in ws[1:]:
            den = den + t
        acc = ws[0] * refs[0][...].astype(F32)
        for b in range(1, nbr):
            acc = acc + ws[b] * refs[2 * b][...].astype(F32)
        o_ref[...] = (acc / den).astype(BF16)
        lse_ref[...] = m + jnp.log(den)

    return _call(body, name=name, grid=(L // tm,), in_specs=[row] * (2 * nbr), out_specs=[row, row],
                 out_shape=[jax.ShapeDtypeStruct((L, W), BF16), jax.ShapeDtypeStruct((L, W), F32)], sem=("parallel",))(*branches)


def _dilated_fwd(proj0):
    L = proj0.shape[0]
    qkv = proj0[:, 2 * D_MODEL:]
    W = B_HEADS * HEAD_DIM
    outs = []
    for window, d in DILATED_PATTERN:
        view = qkv.reshape(L // d, d * 3 * W)
        o, lse = _band_fwd(view, view, view, d=d, nq=B_HEADS, nkv=B_HEADS, qcol=lambda r: 3 * r, kcol=lambda r: 3 * r + 1,
                           vcol=lambda r: 3 * r + 2, max_dist=window // d, name=f"dilated_fwd_d{d}")
        outs += [o.reshape(L, W), lse.reshape(L, W)]
    return _dilated_merge(outs, name="dilated_merge")


def _dilated_bwd(proj0, att, lse, datt, tabs):
    L = proj0.shape[0]
    qkv = proj0[:, 2 * D_MODEL:]
    Wh = B_HEADS * HEAD_DIM
    branches = []
    for window, d in DILATED_PATTERN:
        view = qkv.reshape(L // d, d * 3 * Wh)
        v1 = lambda t: t.reshape(L // d, d * Wh)
        outs = _band_bwd(view, view, view, v1(datt), v1(att), v1(lse), d=d, nq=B_HEADS, nkv=B_HEADS,
                         qcol=lambda r: 3 * r, kcol=lambda r: 3 * r + 1, vcol=lambda r: 3 * r + 2, docol=lambda r: r,
                         max_dist=window // d, name=f"dilated_bwd_d{d}")
        branches.append([o.reshape(L, Wh) for o in outs])
    return _attn_grad_combine(branches, tabs, name="dilated_grad_combine")


def _device_step(x, mem, target, w, on_grads=None):
    L = x.shape[0]
    tabs = _rope_tables(L)
    g = {}
    saved = []
    h = x
    for layer in range(2):
        sv = {"h_mix": h}
        if layer == 0:
            proj, n = _rowmm(h, w["ab_w_in"], name="l0_in_proj", gain=w["mix_norm"][0],
                             rope=(2 * D_MODEL, 2 * D_MODEL + 2 * B_HEADS * HEAD_DIM, tabs))
            rec, hs = _lru_fwd(proj, w["lru_conv_w"], w["lru_conv_b"], w["lru_wa"], w["lru_ba"], w["lru_wx"], w["lru_bx"],
                               w["lru_lambda"], name="lru_fwd")
            att, lse = _dilated_fwd(proj)
            mix = jnp.concatenate([rec, att], axis=1)
            (h,) = _rowmm(mix, w["ab_w_out"], name="l0_out_proj", res=h)
            sv.update(hs=hs)
        else:
            proj, n = _rowmm(h, w["c_w_qkv"], name="l1_qkv_proj", gain=w["mix_norm"][1], bias=w["c_b_qkv"],
                             rope=(0, (C_HEADS + C_KV_HEADS) * HEAD_DIM, tabs))
            mix, lse = _band_fwd(proj, proj, proj, d=1, nq=C_HEADS, nkv=C_KV_HEADS, qcol=lambda r: 0, kcol=lambda r: 8,
                                 vcol=lambda r: 9, max_dist=C_WINDOW - 1, sinks=w["c_sinks"], name="swa_fwd")
            (h,) = _rowmm(mix, w["c_w_out"], name="l1_out_proj", res=h, bias=w["c_b_out"])
        sv.update(proj=proj, n_mix=n, mix=mix, lse=lse, h_xa=h)
        xq, nx = _rowmm(h, w["xa_wq"][layer][None], name=f"xa_q_proj{layer}", gain=w["xa_norm"][layer])
        kv, nm = _rowmm(mem, w["xa_wkv"][layer][None], name=f"xa_kv_proj{layer}", gain=w["xa_mem_norm"][layer])
        xo, xlse = _xattn_fwd(xq, kv, name=f"xa_fwd{layer}")
        (h,) = _rowmm(xo, w["xa_wo"][layer], name=f"xa_out_proj{layer}", res=h)
        sv.update(xq=xq, nx=nx, kv=kv, nm=nm, xo=xo, xlse=xlse, h_ffn=h)
        gu, nf, act = _rowmm(h, w["ffn_w_gate_up"], layer=layer, name=f"ffn_in{layer}", gain=w["ffn_norm"][layer], swiglu=True)
        (h,) = _rowmm(act, w["ffn_w_down"][layer][None], name=f"ffn_out{layer}", res=h, tm=512)
        sv.update(gu=gu, nf=nf, act=act)
        saved.append(sv)

    dh, g["final_norm"], loss = _final_loss(h, w["final_norm"], target, name="final_loss")

    stk = {k: [None, None] for k in ("xa_norm", "xa_mem_norm", "ffn_norm", "mix_norm")}
    for layer in (1, 0):
        sv = saved[layer]
        (g["ffn_w_down", layer],) = _mm_tn(sv["act"], dh, S=1, name=f"ffn_down_dw{layer}", kk=D_FF // 2)
        (dgu,) = _mm_nt(dh, w["ffn_w_down"][layer][None], name=f"ffn_dact{layer}", mode="swiglu", kchunk=D_FF // 2, gu=sv["gu"])
        (g["ffn_w_gate_up", layer],) = _mm_tn(sv["nf"], dgu, S=N_CHIPS, name=f"ffn_gu_dw{layer}")
        dh, stk["ffn_norm"][layer] = _mm_nt(dgu, w["ffn_w_gate_up"], layer=layer, name=f"ffn_dx{layer}", mode="norm",
                                            h=sv["h_ffn"], gain=w["ffn_norm"][layer], dh=dh)
        (g["xa_wo", layer],) = _mm_tn(sv["xo"], dh, S=N_CHIPS, name=f"xa_wo_dw{layer}")
        (dxo,) = _mm_nt(dh, w["xa_wo"][layer], name=f"xa_dxo{layer}", mode="plain")
        dxq, dkv = _xattn_bwd(sv["xq"], sv["kv"], sv["xo"], sv["xlse"], dxo, name=f"xa_bwd{layer}")
        (g["xa_wq", layer],) = _mm_tn(sv["nx"], dxq, S=1, name=f"xa_wq_dw{layer}")
        dh, stk["xa_norm"][layer] = _mm_nt(dxq, w["xa_wq"][layer][None], name=f"xa_dx{layer}", mode="norm", h=sv["h_xa"],
                                           gain=w["xa_norm"][layer], dh=dh)
        (g["xa_wkv", layer],) = _mm_tn(sv["nm"], dkv, S=1, name=f"xa_wkv_dw{layer}")
        _, stk["xa_mem_norm"][layer] = _mm_nt(dkv, w["xa_wkv"][layer][None], name=f"xa_dmem{layer}", mode="norm", h=mem,
                                              gain=w["xa_mem_norm"][layer])
        if layer == 1:
            g["c_w_out"], g["c_b_out"] = _mm_tn(sv["mix"], dh, S=1, name="l1_out_dw", bias=True)
            (dmix,) = _mm_nt(dh, w["c_w_out"], name="l1_dmix", mode="plain")
            dq, dk, dv, dsk = _band_bwd(sv["proj"], sv["proj"], sv["proj"], dmix, sv["mix"], sv["lse"], d=1, nq=C_HEADS,
                                        nkv=C_KV_HEADS, qcol=lambda r: 0, kcol=lambda r: 8, vcol=lambda r: 9,
                                        docol=lambda r: 0, max_dist=C_WINDOW - 1, sinks=w["c_sinks"], name="swa_bwd")
            g["c_sinks"] = dsk[0, :C_HEADS]
            dproj = _attn_grad_combine([(dq, dk, dv)], tabs, name="swa_grad_combine")
            g["c_w_qkv"], g["c_b_qkv"] = _mm_tn(sv["n_mix"], dproj, S=1, name="l1_qkv_dw", bias=True)
            dh, stk["mix_norm"][1] = _mm_nt(dproj, w["c_w_qkv"], name="l1_dx", mode="norm", h=sv["h_mix"],
                                            gain=w["mix_norm"][1], dh=dh)
            if on_grads is not None:
                on_grads("layer1", g)
        else:
            if on_grads is not None:
                on_grads("layer0_ffn_xa", g)
            (g["ab_w_out"],) = _mm_tn(sv["mix"], dh, S=1, name="l0_out_dw", kk=768)
            (dmix,) = _mm_nt(dh, w["ab_w_out"], name="l0_dmix", mode="plain", kchunk=768)
            (dxy, g["lru_conv_w"], g["lru_conv_b"], g["lru_wa"], g["lru_ba"], g["lru_wx"], g["lru_bx"],
             g["lru_lambda"]) = _lru_bwd(sv["proj"], sv["hs"], dmix, w["lru_conv_w"], w["lru_conv_b"], w["lru_wa"],
                                         w["lru_ba"], w["lru_wx"], w["lru_bx"], w["lru_lambda"], name="lru_bwd")
            dqkv = _dilated_bwd(sv["proj"], sv["mix"][:, D_MODEL:], sv["lse"], dmix[:, D_MODEL:], tabs)
            dproj = jnp.concatenate([dxy, dqkv], axis=1)
            (g["ab_w_in"],) = _mm_tn(sv["n_mix"], dproj, S=N_CHIPS, name="l0_in_dw")
            dh, stk["mix_norm"][0] = _mm_nt(dproj, w["ab_w_in"], name="l0_dx", mode="norm", h=sv["h_mix"],
                                            gain=w["mix_norm"][0], dh=dh)
    for k, v in stk.items():
        g[k] = jnp.concatenate(v, axis=0)
    return loss[0, 0], dh, g


ANY = pl.BlockSpec(memory_space=pl.ANY)
MESH = pl.DeviceIdType.MESH


def _place():
    x, y, c = lax.axis_index("x"), lax.axis_index("y"), lax.axis_index("c")
    return x, y, c, [(1 - x, y), (x, 1 - y), (1 - x, 1 - y)]


def _remote(send_sems, recv_sems):
    def copy(k, src, dst, to):
        return pltpu.make_async_remote_copy(src_ref=src, dst_ref=dst, send_sem=send_sems.at[k], recv_sem=recv_sems.at[k],
                                            device_id=to, device_id_type=MESH)
    return copy


def _halves(ref, n_rows):
    rh = n_rows // 2
    return lambda lead, hh: ref.at[(*lead, pl.ds(hh * rh, rh), slice(None))]


def _gather_weights(packs, spack):
    n = len(packs)

    def body(*refs):
        w_refs, s_ref, wf_refs, sf_ref = refs[:n], refs[n], refs[n + 1:2 * n + 1], refs[2 * n + 1]
        x, y, c, chips = _place()
        me, sib = 2 * x + y, (x, y, 1 - c)
        copy = _remote(*refs[-2:])
        src = [_halves(w_refs[g], packs[g].shape[0]) for g in range(n)]
        dst = [_halves(wf_refs[g], packs[g].shape[0]) for g in range(n)]
        sends = []
        for g in range(n):
            for j, (cx, cy) in enumerate(chips):
                sends.append(copy(3 * g + j, src[g]((), c), dst[g]((me,), c), (cx, cy, c)))
        for j, (cx, cy) in enumerate(chips):
            sends.append(copy(6 * n + j, s_ref, sf_ref.at[me], (cx, cy, c)))
        for cp in sends:
            cp.start()
        for g in range(n):
            for j, (cx, cy) in enumerate(chips):
                got = dst[g]((2 * cx + cy,), c)
                copy(3 * g + j, got, got, sib).wait_recv()
                fwd = copy(3 * n + 3 * g + j, got, got, sib)
                fwd.start()
                sends.append(fwd)
        for g in range(n):
            for j, (cx, cy) in enumerate(chips):
                got = dst[g]((2 * cx + cy,), 1 - c)
                copy(3 * n + 3 * g + j, got, got, sib).wait_recv()
        for j, (cx, cy) in enumerate(chips):
            copy(6 * n + j, s_ref, sf_ref.at[2 * cx + cy], sib).wait_recv()
        for cp in sends:
            cp.wait_send()

    ins = list(packs) + [spack]
    out_shape = [jax.ShapeDtypeStruct((N_CHIPS,) + a.shape, a.dtype) for a in ins]
    n_sems = 6 * n + 3
    outs = pl.pallas_call(body, name="gather_weights", out_shape=out_shape, in_specs=[ANY] * len(ins),
                          out_specs=[ANY] * len(ins),
                          scratch_shapes=[pltpu.SemaphoreType.DMA((n_sems,)), pltpu.SemaphoreType.DMA((n_sems,))])(*ins)
    chip = 2 * lax.axis_index("x") + lax.axis_index("y")
    outs = [lax.dynamic_update_index_in_dim(o, a, chip, 0) for o, a in zip(outs, ins)]
    return outs[:n], outs[n]


SEQUENCER_GATHER_IDS = {"mid": 1, "late": 5}


def _gather_weights_behind(packs, *, tag):
    n = len(packs)

    def body(*refs):
        w_refs, wf_refs = refs[:n], refs[n:2 * n]
        x, y, c, chips = _place()
        me, sib = 2 * x + y, (x, y, 1 - c)
        barrier = pltpu.get_barrier_semaphore()
        for peer in [(cx, cy, c) for cx, cy in chips] + [sib]:
            pl.semaphore_signal(barrier, inc=1, device_id=peer, device_id_type=MESH)
        pl.semaphore_wait(barrier, len(chips) + 1)
        copy = _remote(*refs[-2:])
        src = [_halves(w_refs[g], packs[g].shape[0]) for g in range(n)]
        dst = [_halves(wf_refs[g], packs[g].shape[0]) for g in range(n)]
        sends = []
        for g in range(n):
            for j, (cx, cy) in enumerate(chips):
                sends.append(copy(3 * g + j, src[g]((), c), dst[g]((me,), c), (cx, cy, c)))
        for cp in sends:
            cp.start()
        for g in range(n):
            for j, (cx, cy) in enumerate(chips):
                got = dst[g]((2 * cx + cy,), c)
                copy(3 * g + j, got, got, sib).wait_recv()
                fwd = copy(3 * n + 3 * g + j, got, got, sib)
                fwd.start()
                sends.append(fwd)
        for g in range(n):
            for j, (cx, cy) in enumerate(chips):
                got = dst[g]((2 * cx + cy,), 1 - c)
                copy(3 * n + 3 * g + j, got, got, sib).wait_recv()
        for cp in sends:
            cp.wait_send()

    out_type = [jax.ShapeDtypeStruct((N_CHIPS,) + a.shape, a.dtype) for a in packs]
    outs = pl.kernel(body, out_type=out_type, mesh=plsc.ScalarSubcoreMesh(axis_name="sequencer", num_cores=1),
                     name="gather_weights_behind_" + tag,
                     scratch_types=[pltpu.SemaphoreType.DMA((6 * n,)), pltpu.SemaphoreType.DMA((6 * n,))],
                     compiler_params=pltpu.CompilerParams(collective_id=SEQUENCER_GATHER_IDS[tag]))(*packs)
    chip = 2 * lax.axis_index("x") + lax.axis_index("y")
    return [lax.dynamic_update_index_in_dim(o, a, chip, 0) for o, a in zip(outs, packs)]


def _rs_pair_exchange(gpacks, *, name):
    n = len(gpacks)

    def body(*refs):
        g_refs, ra_refs = refs[:n], refs[n:2 * n]
        x, y, c, _ = _place()
        copy = _remote(*refs[-2:])
        cps = []
        for g in range(n):
            half = _halves(g_refs[g], gpacks[g].shape[1])
            cps += [copy(N_CHIPS * g + j, half((j,), 1 - c), ra_refs[g].at[j], (x, y, 1 - c)) for j in range(N_CHIPS)]
        for cp in cps:
            cp.start()
        for cp in cps:
            cp.wait()

    out_shape = [jax.ShapeDtypeStruct((N_CHIPS, a.shape[1] // 2, a.shape[2]), a.dtype) for a in gpacks]
    n_sems = N_CHIPS * n
    return pl.pallas_call(body, name=name, out_shape=out_shape, in_specs=[ANY] * n, out_specs=[ANY] * n,
                          scratch_shapes=[pltpu.SemaphoreType.DMA((n_sems,)), pltpu.SemaphoreType.DMA((n_sems,))])(*gpacks)


def _row_tile(rows, cap=512):
    return max(t for t in range(16, min(rows, cap) + 1, 16) if rows % t == 0)


def _rs_pair_add(place, gpack, ra, *, name):
    _, R, C = gpack.shape
    Rh = R // 2
    tr = _row_tile(Rh)
    nrb = Rh // tr

    def body(p_ref, g_ref, ra_ref, pair_ref, own_ref):
        s = g_ref[...].astype(F32) + ra_ref[...].astype(F32)
        pair_ref[...] = s.astype(BF16)

        @pl.when(pl.program_id(1) == p_ref[1])
        def _():
            own_ref[...] = s

    grid_spec = pltpu.PrefetchScalarGridSpec(
        num_scalar_prefetch=1, grid=(nrb, N_CHIPS),
        in_specs=[pl.BlockSpec((None, tr, C), lambda i, j, p: (j, p[0] * nrb + i, 0)),
                  pl.BlockSpec((None, tr, C), lambda i, j, p: (j, i, 0))],
        out_specs=[pl.BlockSpec((None, tr, C), lambda i, j, p: (j, i, 0)), pl.BlockSpec((tr, C), lambda i, j, p: (i, 0))])
    return pl.pallas_call(
        body, name=name, grid_spec=grid_spec,
        out_shape=[jax.ShapeDtypeStruct((N_CHIPS, Rh, C), BF16), jax.ShapeDtypeStruct((Rh, C), F32)],
        compiler_params=pltpu.CompilerParams(dimension_semantics=("arbitrary", "arbitrary"),
                                             vmem_limit_bytes=VMEM_LIMIT_V7X))(place, gpack, ra)


SEQUENCER_EXCHANGE_IDS = {"l1": 2, "l0a": 3, "l0b": 4}


def _rs_chip_exchange_behind(pairs, *, tag, small=None):
    n = len(pairs)
    has_small = small is not None

    def body(*refs):
        p_refs = refs[:n]
        s_ref = refs[n] if has_small else None
        rb_refs = refs[n + has_small:2 * n + has_small]
        rs_ref = refs[2 * n + 1] if has_small else None
        x, y, c, chips = _place()
        peers = [(1 - x if k & 4 else x, 1 - y if k & 2 else y, 1 - c if k & 1 else c) for k in range(1, 8)]
        shake = peers if has_small else [(cx, cy, c) for cx, cy in chips]
        barrier = pltpu.get_barrier_semaphore()
        for peer in shake:
            pl.semaphore_signal(barrier, inc=1, device_id=peer, device_id_type=MESH)
        pl.semaphore_wait(barrier, len(shake))
        copy = _remote(*refs[-2:])
        cps = []
        for g in range(n):
            cps += [copy(3 * g + j, p_refs[g].at[2 * cx + cy], rb_refs[g].at[j], (cx, cy, c)) for j, (cx, cy) in enumerate(chips)]
        if has_small:
            dev = 4 * x + 2 * y + c
            cps += [copy(3 * n + k, s_ref, rs_ref.at[dev], peer) for k, peer in enumerate(peers)]
        for cp in cps:
            cp.start()
        for g in range(n):
            for j in range(3):
                copy(3 * g + j, p_refs[g].at[0], rb_refs[g].at[j], (x, y, c)).wait_recv()
        if has_small:
            for k, (px, py, pc) in enumerate(peers):
                copy(3 * n + k, s_ref, rs_ref.at[4 * px + 2 * py + pc], (x, y, c)).wait_recv()
        for cp in cps:
            cp.wait_send()

    ins = list(pairs) + ([small] if has_small else [])
    out_type = [jax.ShapeDtypeStruct((3,) + p.shape[1:], p.dtype) for p in pairs]
    if has_small:
        out_type.append(jax.ShapeDtypeStruct((8,) + small.shape, small.dtype))
    n_sems = 3 * n + 7 * has_small
    outs = pl.kernel(body, out_type=out_type, mesh=plsc.ScalarSubcoreMesh(axis_name="sequencer", num_cores=1),
                     name="rs_chip_exchange_behind_" + tag,
                     scratch_types=[pltpu.SemaphoreType.DMA((n_sems,)), pltpu.SemaphoreType.DMA((n_sems,))],
                     compiler_params=pltpu.CompilerParams(collective_id=SEQUENCER_EXCHANGE_IDS[tag]))(*ins)
    if has_small:
        dev = 4 * lax.axis_index("x") + 2 * lax.axis_index("y") + lax.axis_index("c")
        outs = list(outs[:n]) + [lax.dynamic_update_index_in_dim(outs[n], small, dev, 0)]
    return outs


def _rs_final_add(place, own, rb, *, name):
    Rh, C = own.shape
    tr = _row_tile(Rh)
    nrb = Rh // tr

    def body(p_ref, o_ref, rb_ref, f_ref):
        f_ref[...] = ((o_ref[...] + rb_ref[0].astype(F32)) + rb_ref[1].astype(F32)) + rb_ref[2].astype(F32)

    grid_spec = pltpu.PrefetchScalarGridSpec(
        num_scalar_prefetch=1, grid=(nrb,),
        in_specs=[pl.BlockSpec((tr, C), lambda i, p: (i, 0)), pl.BlockSpec((3, tr, C), lambda i, p: (0, i, 0))],
        out_specs=pl.BlockSpec((tr, C), lambda i, p: (p[0] * nrb + i, 0)))
    return pl.pallas_call(
        body, name=name, grid_spec=grid_spec, out_shape=jax.ShapeDtypeStruct((2 * Rh, C), F32),
        compiler_params=pltpu.CompilerParams(dimension_semantics=("arbitrary",), vmem_limit_bytes=VMEM_LIMIT_V7X))(place, own, rb)


def _sum_slots(rs):
    n, rows, C = rs.shape

    def body(r_ref, o_ref):
        acc = r_ref[0]
        for k in range(1, n):
            acc = acc + r_ref[k]
        o_ref[...] = acc

    return _call(body, name="small_grad_sum", grid=(1,), in_specs=[pl.BlockSpec((n, rows, C), lambda i: (0, 0, 0))],
                 out_specs=pl.BlockSpec((rows, C), lambda i: (0, 0)), out_shape=jax.ShapeDtypeStruct((rows, C), F32),
                 sem=("arbitrary",))(rs)


def _rs_sibling_share(gbufs, *, name):
    n = len(gbufs)

    def body(*refs):
        g_refs = refs[n:2 * n]
        x, y, c, _ = _place()
        copy = _remote(*refs[-2:])
        halves = [_halves(g_refs[g], gbufs[g].shape[0]) for g in range(n)]
        outs = [copy(g, halves[g]((), c), halves[g]((), c), (x, y, 1 - c)) for g in range(n)]
        for cp in outs:
            cp.start()
        for g in range(n):
            copy(g, halves[g]((), 1 - c), halves[g]((), 1 - c), (x, y, c)).wait_recv()
        for cp in outs:
            cp.wait_send()

    return pl.pallas_call(body, name=name, out_shape=[jax.ShapeDtypeStruct(a.shape, a.dtype) for a in gbufs],
                          in_specs=[ANY] * n, out_specs=[ANY] * n, input_output_aliases={g: g for g in range(n)},
                          scratch_shapes=[pltpu.SemaphoreType.DMA((n,)), pltpu.SemaphoreType.DMA((n,))])(*gbufs)


def _adamw(w, g, m, v, *, name, g_row=0):
    rows, cols = w.shape
    tr = rows
    for cand in range(min(rows, 512), 7, -8):
        if rows % cand == 0 and g_row % cand == 0:
            tr = cand
            break
    spec = pl.BlockSpec((tr, cols), lambda i: (i, 0))
    g_spec = pl.BlockSpec((tr, cols), lambda i: (g_row // tr + i, 0))

    def body(w_ref, g_ref, m_ref, v_ref, d_ref, nm_ref, nv_ref):
        gg = g_ref[...]
        nm = ADAM_B1 * m_ref[...] + (1.0 - ADAM_B1) * gg
        nv = ADAM_B2 * v_ref[...] + (1.0 - ADAM_B2) * (gg * gg)
        m_hat = nm / (1.0 - ADAM_B1 ** ADAM_STEP)
        v_hat = nv / (1.0 - ADAM_B2 ** ADAM_STEP)
        d_ref[...] = -ADAM_LR * (m_hat / (jnp.sqrt(v_hat) + ADAM_EPS) + ADAM_WD * w_ref[...])
        nm_ref[...] = nm
        nv_ref[...] = nv

    return _call(body, name=name, grid=(rows // tr,), in_specs=[spec, g_spec, spec, spec], out_specs=[spec] * 3,
                 out_shape=[jax.ShapeDtypeStruct((rows, cols), F32)] * 3, sem=("parallel",))(w, g, m, v)


WEIGHT_NAMES = ("mix_norm", "ab_w_in", "lru_conv_w", "lru_conv_b", "lru_wa", "lru_ba", "lru_wx", "lru_bx", "lru_lambda",
                "ab_w_out", "c_w_qkv", "c_b_qkv", "c_sinks", "c_w_out", "c_b_out", "xa_norm", "xa_mem_norm", "xa_wq",
                "xa_wkv", "xa_wo", "ffn_norm", "ffn_w_gate_up", "ffn_w_down", "final_norm")
EARLY_GROUPS = (("ab_w_in",),)
MID_GROUPS = (("ab_w_out",), ("lru_wa", "lru_wx"))
LATE_GROUPS = (("c_w_out", "xa_wkv", "ffn_w_down"), ("ffn_w_gate_up",), ("xa_wo",), ("xa_wq",), ("c_w_qkv",))
GROUPS = EARLY_GROUPS + MID_GROUPS + LATE_GROUPS
REPLICATED = ("mix_norm", "lru_conv_b", "lru_lambda", "c_sinks", "xa_norm", "xa_mem_norm", "ffn_norm", "final_norm")
SMALL_SHARDED = ("lru_conv_w", "lru_ba", "lru_bx", "c_b_qkv", "c_b_out")
LANES = 1024


def _rows(v):
    flat = v.reshape(-1)
    return jnp.pad(flat, (0, -flat.shape[0] % LANES)).reshape(-1, LANES)


def _pack_small(parts, total, *, name):
    def body(*refs):
        o_ref = refs[-1]
        o_ref[...] = jnp.zeros_like(o_ref)
        row = 0
        for p_ref in refs[:-1]:
            o_ref[row:row + p_ref.shape[0], :] = p_ref[...]
            row += p_ref.shape[0]

    return _call(body, name=name, grid=(1,), in_specs=[pl.BlockSpec(p.shape, lambda i: (0, 0)) for p in parts],
                 out_specs=pl.BlockSpec((total, LANES), lambda i: (0, 0)),
                 out_shape=jax.ShapeDtypeStruct((total, LANES), F32), sem=("arbitrary",))(*parts)


def _from_shards(name, t):
    minor = t.shape[-1]
    if name == "ab_w_in":
        return t
    if name in ("ab_w_out", "c_w_out"):
        return t.reshape(1, -1, minor)
    if name == "ffn_w_gate_up":
        return t.reshape(N_CHIPS, 2, -1, minor)
    if name in ("xa_wq", "xa_wkv", "ffn_w_down"):
        return t.reshape(N_CHIPS, 2, -1, minor).transpose(1, 0, 2, 3).reshape(2, -1, minor)
    if name in ("lru_wa", "lru_wx"):
        return t.reshape(N_CHIPS, LRU_HEADS, -1, minor).transpose(1, 0, 2, 3).reshape(LRU_HEADS, LRU_HEAD_DIM, minor)
    if name == "xa_wo":
        return t.reshape(N_CHIPS, 2, -1, minor).transpose(1, 0, 2, 3)
    assert name == "c_w_qkv"
    return t.transpose(1, 0, 2).reshape(1, D_MODEL, -1)


def _piece_shards(name, g):
    minor = g.shape[-1]
    if name in ("ab_w_in", "ffn_w_gate_up", "xa_wo"):
        return g
    if name in ("ab_w_out", "c_w_out", "xa_wq", "xa_wkv", "ffn_w_down"):
        return g.reshape(N_CHIPS, -1, minor)
    if name in ("lru_wa", "lru_wx"):
        return g.reshape(LRU_HEADS, N_CHIPS, -1, minor).transpose(1, 0, 2, 3).reshape(N_CHIPS, -1, minor)
    assert name == "c_w_qkv"
    return g.reshape(D_MODEL, N_CHIPS, -1).transpose(1, 0, 2)


RS_SETS = {
    "l1": ((("c_w_out", None), ("xa_wkv", 1), ("ffn_w_down", 1)), (("ffn_w_gate_up", 1),), (("xa_wo", 1),),
           (("xa_wq", 1),), (("c_w_qkv", None),)),
    "l0a": ((("xa_wkv", 0), ("ffn_w_down", 0)), (("ffn_w_gate_up", 0),), (("xa_wo", 0),), (("xa_wq", 0),)),
    "l0b": ((("ab_w_out", None),), (("ab_w_in", None),), (("lru_wa", None), ("lru_wx", None))),
}
RS_STAGE = {"layer1": "l1", "layer0_ffn_xa": "l0a"}


def kernel(x, mem, mix_norm, ab_w_in, lru_conv_w, lru_conv_b, lru_wa, lru_ba, lru_wx, lru_bx, lru_lambda, ab_w_out, c_w_qkv, c_b_qkv, c_sinks, c_w_out, c_b_out, xa_norm, xa_mem_norm, xa_wq, xa_wkv, xa_wo, ffn_norm, ffn_w_gate_up, ffn_w_down, final_norm, loss_target, m_mix_norm, m_ab_w_in, m_lru_conv_w, m_lru_conv_b, m_lru_wa, m_lru_ba, m_lru_wx, m_lru_bx, m_lru_lambda, m_ab_w_out, m_c_w_qkv, m_c_b_qkv, m_c_sinks, m_c_w_out, m_c_b_out, m_xa_norm, m_xa_mem_norm, m_xa_wq, m_xa_wkv, m_xa_wo, m_ffn_norm, m_ffn_w_gate_up, m_ffn_w_down, m_final_norm, v_mix_norm, v_ab_w_in, v_lru_conv_w, v_lru_conv_b, v_lru_wa, v_lru_ba, v_lru_wx, v_lru_bx, v_lru_lambda, v_ab_w_out, v_c_w_qkv, v_c_b_qkv, v_c_sinks, v_c_w_out, v_c_b_out, v_xa_norm, v_xa_mem_norm, v_xa_wq, v_xa_wkv, v_xa_wo, v_ffn_norm, v_ffn_w_gate_up, v_ffn_w_down, v_final_norm):
    given = dict(locals())
    wl = {n: given[n] for n in WEIGHT_NAMES}
    ml = {n: given["m_" + n] for n in WEIGHT_NAMES}
    vl = {n: given["v_" + n] for n in WEIGHT_NAMES}
    xi, yi, ci = lax.axis_index("x"), lax.axis_index("y"), lax.axis_index("c")
    chip = 2 * xi + yi

    def join(parts, axis):
        return parts[0] if len(parts) == 1 else jnp.concatenate(parts, axis=axis)

    local_rows = {n: wl[n].size // wl[n].shape[-1] for grp in GROUPS for n in grp}
    packs = [join([wl[n].astype(BF16).reshape(local_rows[n], wl[n].shape[-1]) for n in grp], 0) for grp in GROUPS]
    spack = _pack_small([_rows(wl[n]) for n in SMALL_SHARDED], 8, name="pack_small_weights")
    n_early, n_mid = len(EARLY_GROUPS), len(EARLY_GROUPS) + len(MID_GROUPS)
    early, sfull = _gather_weights(packs[:n_early], spack)
    early, sfull, mid_packs = lax.optimization_barrier((early, sfull, packs[n_early:n_mid]))
    mid = _gather_weights_behind(mid_packs, tag="mid")
    mid, late_packs = lax.optimization_barrier((mid, packs[n_mid:]))
    gathered = early + mid + _gather_weights_behind(late_packs, tag="late")
    w = {n: wl[n] for n in REPLICATED}
    w["c_sinks"] = wl["c_sinks"][0]
    for grp, full in zip(GROUPS, gathered):
        off = 0
        for n in grp:
            w[n] = _from_shards(n, full if len(grp) == 1 else full[:, off:off + local_rows[n]])
            off += local_rows[n]
    for r, n in enumerate(SMALL_SHARDED):
        loc = wl[n].shape[1:]
        t = sfull[:, r, :wl[n].size].reshape((N_CHIPS,) + loc)
        if n == "lru_conv_w":
            w[n] = t.transpose(1, 0, 2).reshape(CONV_WIDTH, -1)
        elif n in ("lru_ba", "lru_bx"):
            w[n] = t.transpose(1, 0, 2).reshape(1, -1)
        else:
            w[n] = t.reshape(1, -1)

    place = jnp.stack([ci, chip]).astype(jnp.int32)

    def pair_stage(spec, g, tag):
        piece = lambda n, l: (g[n] if l is None else g[n, l]).astype(BF16)
        gpacks = [join([_piece_shards(n, piece(n, l)) for n, l in grp], 1) for grp in spec]
        ras = _rs_pair_exchange(gpacks, name=f"rs_pair_exchange_{tag}")
        sums = [_rs_pair_add(place, gp, ra, name=f"rs_pair_add_{tag}_{i}") for i, (gp, ra) in enumerate(zip(gpacks, ras))]
        return [pair for pair, _ in sums], [own for _, own in sums]

    owns, rbs = [], []

    def reduce_behind(stage, g):
        tag = RS_STAGE[stage]
        pairs, own = pair_stage(RS_SETS[tag], g, tag)
        owns.extend(own)
        rbs.extend(_rs_chip_exchange_behind(pairs, tag=tag))

    loss_part, grad_x, g = _device_step(x[0], mem[0], loss_target[0], w, on_grads=reduce_behind)
    owns, rbs, grad_x = lax.optimization_barrier((owns, rbs, grad_x))

    small_parts = [_rows(g[n]) for n in REPLICATED] + [_rows(jnp.broadcast_to(loss_part, (LANES,)))]
    small_parts += [_rows(g[n]) for n in SMALL_SHARDED]
    small = _pack_small(small_parts, 24, name="pack_small_grads")
    pairs, own = pair_stage(RS_SETS["l0b"], g, "l0b")
    *rb, rs = _rs_chip_exchange_behind(pairs, tag="l0b", small=small)

    def finish(own_sums, received, first, name):
        return _rs_sibling_share([_rs_final_add(place, o, r, name=f"rs_final_add_{first + i}")
                                  for i, (o, r) in enumerate(zip(own_sums, received))], name=name)

    gsums = finish(owns, rbs, 0, "rs_sibling_share_behind") + finish(own, rb, len(owns), "rs_sibling_share_last")
    ssum = _sum_slots(rs)

    where = {}
    for grp, gsum in zip(RS_SETS["l1"] + RS_SETS["l0a"] + RS_SETS["l0b"], gsums):
        off = 0
        for n, l in grp:
            rows = local_rows[n] if l is None else local_rows[n] // 2
            where[n, l] = (gsum, off, rows, len(grp) == 1)
            off += rows
    take = lambda gsum, off, rows, whole: gsum if whole else gsum[off:off + rows]
    grads, grad_rows = {}, {}
    for grp in LATE_GROUPS + EARLY_GROUPS + MID_GROUPS:
        for n in grp:
            if (n, None) in where:
                grads[n] = take(*where[n, None]).reshape(wl[n].shape)
                grad_rows[n] = where[n, None][:2]
            else:
                grads[n] = jnp.stack([take(*where[n, l]).reshape(wl[n].shape[1:]) for l in range(2)])
                grad_rows[n] = (grads[n].reshape(local_rows[n], wl[n].shape[-1]), 0)
    row = 0
    for n in REPLICATED:
        k = _rows(g[n]).shape[0]
        grads[n] = ssum[row:row + k].reshape(-1)[:wl[n].size].reshape(wl[n].shape)
        row += k
    loss = ssum[row, 0]
    row += 1
    for n in SMALL_SHARDED:
        k = _rows(g[n]).shape[0]
        full = ssum[row:row + k].reshape(-1)[:g[n].size]
        row += k
        loc = wl[n].shape
        if n == "lru_conv_w":
            sh = full.reshape(CONV_WIDTH, N_CHIPS, -1)
        elif n in ("lru_ba", "lru_bx"):
            sh = full.reshape(LRU_HEADS, N_CHIPS, -1)
        else:
            sh = full.reshape(1, N_CHIPS, -1)
        grads[n] = lax.dynamic_index_in_dim(sh, chip, axis=1, keepdims=False).reshape(loc)

    delta, new_m, new_v = {}, {}, {}
    for n, (gsum, off) in grad_rows.items():
        shape2 = (local_rows[n], wl[n].shape[-1])
        d, nm, nv = _adamw(wl[n].reshape(shape2), gsum, ml[n].reshape(shape2), vl[n].reshape(shape2), g_row=off,
                           name="adamw_" + n)
        delta[n], new_m[n], new_v[n] = (t.reshape(wl[n].shape) for t in (d, nm, nv))
    smalls = REPLICATED + SMALL_SHARDED
    packs = [_pack_small([_rows(src[n]) for n in smalls], 24, name="pack_adamw_" + tag)
             for tag, src in (("w", wl), ("g", grads), ("m", ml), ("v", vl))]
    outs = _adamw(*packs, name="adamw_small")
    row = 0
    for n in smalls:
        k = _rows(wl[n]).shape[0]
        for dst, o in zip((delta, new_m, new_v), outs):
            dst[n] = o[row:row + k].reshape(-1)[:wl[n].size].reshape(wl[n].shape)
        row += k

    return (loss, grad_x[None], *[grads[n] for n in WEIGHT_NAMES], *[delta[n] for n in WEIGHT_NAMES],
            *[new_m[n] for n in WEIGHT_NAMES], *[new_v[n] for n in WEIGHT_NAMES])
```

```python
import jax
import jax.numpy as jnp
from jax import lax
from jax.experimental import pallas as pl
from jax.experimental.pallas import tpu as pltpu
from jax.experimental.pallas import tpu_sc as plsc

F32, BF16 = jnp.float32, jnp.bfloat16
D_MODEL = 1024
NORM_EPS = 1e-6
ROPE_THETA = 500000.0
HEAD_DIM = 64
ROT_DIM = 16
BLK = 128
LRU_HEADS, LRU_HEAD_DIM, CONV_WIDTH, LRU_C = 4, 256, 4, 8.0
DILATED_PATTERN = ((128, 1), (512, 4), (2048, 16))
B_HEADS, C_HEADS, C_KV_HEADS, C_WINDOW = 8, 16, 2, 128
XA_HEADS, XA_HEAD_DIM, N_MEM = 4, 128, 256
D_FF = 2816
NEG = -1e30
ADAM_LR, ADAM_B1, ADAM_B2, ADAM_EPS, ADAM_WD, ADAM_STEP = 0.001, 0.9, 0.999, 1e-08, 0.01, 10
N_CHIPS = 4
VMEM_LIMIT_V7X = 56 * 1024 * 1024

NN = (((1,), (0,)), ((), ()))
NT = (((1,), (1,)), ((), ()))
TN = (((0,), (0,)), ((), ()))


def _dot(a, b, dims=NN):
    return lax.dot_general(a, b, dims, preferred_element_type=F32)


def _sigmoid(x):
    return 1.0 / (1.0 + jnp.exp(-x))


def _call(body, *, name, grid, in_specs, out_specs, out_shape, scratch=(), sem=None):
    return pl.pallas_call(
        body, name=name, grid=grid, in_specs=in_specs, out_specs=out_specs, out_shape=out_shape,
        scratch_shapes=list(scratch),
        compiler_params=pltpu.CompilerParams(dimension_semantics=sem, vmem_limit_bytes=VMEM_LIMIT_V7X))


def _rope_tables(L):
    half = ROT_DIM // 2
    inv = ROPE_THETA ** (-jnp.arange(0, ROT_DIM, 2, dtype=F32) / ROT_DIM)
    ang = jnp.arange(L, dtype=F32)[:, None] * inv[None, :]
    cos, sin = jnp.cos(ang), jnp.sin(ang)
    rest = HEAD_DIM - ROT_DIM
    z8, zr, one = jnp.zeros((L, half), F32), jnp.zeros((L, rest), F32), jnp.ones((L, rest), F32)
    c = jnp.concatenate([cos, cos, one], axis=1)
    s1 = jnp.concatenate([-sin, z8, zr], axis=1)
    s2 = jnp.concatenate([z8, sin, zr], axis=1)
    return tuple(jnp.concatenate([t, t], axis=1) for t in (c, s1, s2))


def _rope_fwd(v, c, s1, s2):
    return v * c + pltpu.roll(v, 120, 1) * s1 + pltpu.roll(v, 8, 1) * s2


def _rope_bwd(dv, c, s1, s2):
    return dv * c + pltpu.roll(dv * s1, 8, 1) + pltpu.roll(dv * s2, 120, 1)


def _weight_spec(w, layer):
    once = pl.Buffered(1)
    if layer is None:
        return w.shape, pl.BlockSpec(w.shape, lambda i: (0, 0, 0), pipeline_mode=once)
    S, _, K, Ns = w.shape
    return (S, K, Ns), pl.BlockSpec((S, None, K, Ns), lambda i: (0, layer, 0, 0), pipeline_mode=once)


def _rowmm(a, w3, *, name, tm=512, gain=None, bias=None, res=None, swiglu=False, rope=None, layer=None):
    M, K = a.shape
    (S, _, Ns), w_spec = _weight_spec(w3, layer)
    N = S * Ns
    tm = min(tm, M)
    has_norm, has_bias, has_res, has_rope = gain is not None, bias is not None, res is not None, rope is not None
    row = lambda w: pl.BlockSpec((tm, w), lambda i: (i, 0))
    whole = lambda shape: pl.BlockSpec(shape, lambda i: (0,) * len(shape))
    ins, specs = [a], [row(K)]
    if has_norm:
        ins.append(gain.reshape(1, K)); specs.append(whole((1, K)))
    ins.append(w3); specs.append(w_spec)
    if has_bias:
        ins.append(bias.reshape(1, N)); specs.append(whole((1, N)))
    if has_res:
        ins.append(res); specs.append(row(N))
    if has_rope:
        ins += list(rope[2]); specs += [row(128)] * 3
    y_dtype = F32 if has_res else BF16
    out_shape, out_specs = [jax.ShapeDtypeStruct((M, N), y_dtype)], [row(N)]
    if has_norm:
        out_shape.append(jax.ShapeDtypeStruct((M, K), BF16)); out_specs.append(row(K))
    if swiglu:
        out_shape.append(jax.ShapeDtypeStruct((M, N // 2), BF16)); out_specs.append(row(N // 2))
    scratch = [pltpu.VMEM((tm, N), F32)] if has_rope else []

    def body(*refs):
        it = iter(refs)
        a_ref = next(it)
        g_ref = next(it) if has_norm else None
        w_ref = next(it)
        b_ref = next(it) if has_bias else None
        r_ref = next(it) if has_res else None
        tabs = [next(it) for _ in range(3)] if has_rope else None
        y_ref = next(it)
        n_ref = next(it) if has_norm else None
        act_ref = next(it) if swiglu else None
        ys_ref = next(it) if has_rope else None
        if has_norm:
            x = a_ref[...].astype(F32)
            ms = jnp.mean(x * x, axis=-1, keepdims=True)
            xb = (x * lax.rsqrt(ms + NORM_EPS) * g_ref[...]).astype(BF16)
            n_ref[...] = xb
        else:
            xb = a_ref[...].astype(BF16)
        if swiglu:
            for s in range(S // 2):
                g = _dot(xb, w_ref[s])
                u = _dot(xb, w_ref[s + S // 2])
                y_ref[:, s * Ns:(s + 1) * Ns] = g.astype(BF16)
                y_ref[:, N // 2 + s * Ns:N // 2 + (s + 1) * Ns] = u.astype(BF16)
                act_ref[:, s * Ns:(s + 1) * Ns] = (g * _sigmoid(g) * u).astype(BF16)
            return
        for s in range(S):
            sl = slice(s * Ns, (s + 1) * Ns)
            acc = _dot(xb, w_ref[s])
            if has_bias:
                acc = acc + b_ref[:, sl]
            if has_res:
                acc = acc + r_ref[:, sl]
            if has_rope:
                ys_ref[:, sl] = acc
            else:
                y_ref[:, sl] = acc.astype(y_dtype)
        if has_rope:
            c, s1, s2 = (t[...] for t in tabs)
            for cb in range(N // 128):
                cs = slice(cb * 128, (cb + 1) * 128)
                v = ys_ref[:, cs]
                if rope[0] <= cb * 128 < rope[1]:
                    v = _rope_fwd(v, c, s1, s2)
                y_ref[:, cs] = v.astype(BF16)

    return _call(body, name=name, grid=(M // tm,), in_specs=specs, out_specs=out_specs, out_shape=out_shape,
                 scratch=scratch, sem=("parallel",))(*ins)


def _mm_nt(dy, w3, *, name, mode, tm=512, kchunk=None, h=None, gain=None, dh=None, gu=None, layer=None):
    M, N = dy.shape
    (S, K, Ns), w_spec = _weight_spec(w3, layer)
    kchunk = kchunk or K
    tm = min(tm, M)
    row = lambda w: pl.BlockSpec((tm, w), lambda i: (i, 0))
    whole = lambda shape: pl.BlockSpec(shape, lambda i: (0,) * len(shape))
    ins, specs = [dy, w3], [row(N), w_spec]
    has_dh = dh is not None
    if mode == "norm":
        ins += [h, gain.reshape(1, K)]; specs += [row(K), whole((1, K))]
        if has_dh:
            ins.append(dh); specs.append(row(K))
        out_shape = [jax.ShapeDtypeStruct((M, K), F32), jax.ShapeDtypeStruct((1, K), F32)]
        out_specs = [row(K), whole((1, K))]
    elif mode == "swiglu":
        ins.append(gu); specs.append(row(2 * K))
        out_shape, out_specs = [jax.ShapeDtypeStruct((M, 2 * K), BF16)], [row(2 * K)]
    else:
        out_shape, out_specs = [jax.ShapeDtypeStruct((M, K), BF16)], [row(K)]

    def body(*refs):
        it = iter(refs)
        dy_ref, w_ref = next(it), next(it)
        if mode == "norm":
            h_ref, g_ref = next(it), next(it)
            dh_ref = next(it) if has_dh else None
            o_ref, dg_ref = next(it), next(it)
        elif mode == "swiglu":
            gu_ref, o_ref = next(it), next(it)
        else:
            o_ref = next(it)
        for kc in range(K // kchunk):
            ks = slice(kc * kchunk, (kc + 1) * kchunk)
            acc = None
            for s in range(S):
                t = _dot(dy_ref[:, s * Ns:(s + 1) * Ns].astype(BF16), w_ref[s, ks, :], NT)
                acc = t if acc is None else acc + t
            if mode == "plain":
                o_ref[:, ks] = acc.astype(BF16)
            elif mode == "swiglu":
                us = slice(K + kc * kchunk, K + (kc + 1) * kchunk)
                g = gu_ref[:, ks].astype(F32)
                u = gu_ref[:, us].astype(F32)
                sg = _sigmoid(g)
                o_ref[:, ks] = (acc * u * (sg * (1.0 + g * (1.0 - sg)))).astype(BF16)
                o_ref[:, us] = (acc * (g * sg)).astype(BF16)
            else:
                x = h_ref[...].astype(F32)
                r = lax.rsqrt(jnp.mean(x * x, axis=-1, keepdims=True) + NORM_EPS)
                xhat = x * r
                dxh = acc * g_ref[...]
                dx = r * (dxh - xhat * jnp.mean(dxh * xhat, axis=-1, keepdims=True))
                o_ref[...] = dx + dh_ref[...] if has_dh else dx

                @pl.when(pl.program_id(0) == 0)
                def _():
                    dg_ref[...] = jnp.zeros_like(dg_ref)

                dg_ref[...] += jnp.sum(acc * xhat, axis=0, keepdims=True)

    sem = ("arbitrary",) if mode == "norm" else ("parallel",)
    return _call(body, name=name, grid=(M // tm,), in_specs=specs, out_specs=out_specs, out_shape=out_shape, sem=sem)(*ins)


def _mm_tn(x, dy, *, S, name, tk=1024, kk=None, bias=False):
    M, K = x.shape
    N = dy.shape[1]
    Ns = N // S
    kk = kk or K
    tk = min(tk, M)
    nl = M // tk
    in_specs = [pl.BlockSpec((tk, kk), lambda s, kc, l: (l, kc)), pl.BlockSpec((tk, Ns), lambda s, kc, l: (l, s))]
    out_shape = [jax.ShapeDtypeStruct((S, K, Ns), BF16)]
    out_specs = [pl.BlockSpec((None, kk, Ns), lambda s, kc, l: (s, kc, 0))]
    if bias:
        out_shape.append(jax.ShapeDtypeStruct((1, N), F32))
        out_specs.append(pl.BlockSpec((1, Ns), lambda s, kc, l: (0, s)))

    def body(x_ref, dy_ref, o_ref, *rest):
        acc_ref = rest[-1]
        kc, l = pl.program_id(1), pl.program_id(2)

        @pl.when(l == 0)
        def _():
            acc_ref[...] = jnp.zeros_like(acc_ref)

        acc_ref[...] += _dot(x_ref[...].astype(BF16), dy_ref[...].astype(BF16), TN)
        if bias:
            b_ref = rest[0]

            @pl.when((kc == 0) & (l == 0))
            def _():
                b_ref[...] = jnp.zeros_like(b_ref)

            @pl.when(kc == 0)
            def _():
                b_ref[...] += jnp.sum(dy_ref[...].astype(F32), axis=0, keepdims=True)

        @pl.when(l == nl - 1)
        def _():
            o_ref[...] = acc_ref[...].astype(BF16)

    return _call(body, name=name, grid=(S, K // kk, nl), in_specs=in_specs, out_specs=out_specs, out_shape=out_shape,
                 scratch=[pltpu.VMEM((kk, Ns), F32)], sem=("arbitrary", "arbitrary", "arbitrary"))(x, dy)


def _band_bias(max_dist, has_prev):
    rows = lax.broadcasted_iota(jnp.int32, (BLK, 2 * BLK), 0)
    cols = lax.broadcasted_iota(jnp.int32, (BLK, 2 * BLK), 1)
    dist = rows - cols + BLK
    ok = (dist >= 0) & (dist <= max_dist) & ((cols >= BLK) | has_prev)
    return jnp.where(ok, 0.0, NEG)


Q_SCALE = HEAD_DIM ** -0.5


def _band_fwd(qa, ka, va, *, d, nq, nkv, qcol, kcol, vcol, max_dist, sinks=None, name):
    Lr = qa.shape[0]
    nb = Lr // BLK
    qw, kw, G = nq * HEAD_DIM, nkv * HEAD_DIM, nq // nkv
    cur = lambda colf, w: pl.BlockSpec((BLK, w), lambda r, i: (i, colf(r)))
    prv = lambda colf, w: pl.BlockSpec((BLK, w), lambda r, i: (jnp.maximum(i - 1, 0), colf(r)))
    out = pl.BlockSpec((BLK, qw), lambda r, i: (i, r))
    ins, specs = [qa, ka, ka, va, va], [cur(qcol, qw), cur(kcol, kw), prv(kcol, kw), cur(vcol, kw), prv(vcol, kw)]
    has_sinks = sinks is not None
    if has_sinks:
        ins.append(sinks); specs.append(pl.BlockSpec(memory_space=pltpu.SMEM))

    def body(*refs):
        q_ref, kc_ref, kp_ref, vc_ref, vp_ref = refs[:5]
        sk_ref = refs[5] if has_sinks else None
        o_ref, lse_ref = refs[-2], refs[-1]
        bias = _band_bias(max_dist, pl.program_id(1) > 0)
        k2 = jnp.concatenate([kp_ref[...], kc_ref[...]], axis=0)
        v2 = jnp.concatenate([vp_ref[...], vc_ref[...]], axis=0)
        for h in range(nq):
            hs = slice(h * HEAD_DIM, (h + 1) * HEAD_DIM)
            ks = slice((h // G) * HEAD_DIM, (h // G + 1) * HEAD_DIM)
            s = _dot(q_ref[:, hs] * jnp.asarray(Q_SCALE, BF16), k2[:, ks], NT) + bias
            m = jnp.max(s, axis=-1, keepdims=True)
            if has_sinks:
                m = jnp.maximum(m, sk_ref[h])
            p = jnp.exp(s - m)
            l = jnp.sum(p, axis=-1, keepdims=True)
            if has_sinks:
                l = l + jnp.exp(sk_ref[h] - m)
            o_ref[:, hs] = (_dot(p.astype(BF16), v2[:, ks]) / l).astype(BF16)
            lse_ref[:, hs] = jnp.broadcast_to(m + jnp.log(l), (BLK, HEAD_DIM))

    return _call(body, name=name, grid=(d, nb), in_specs=specs, out_specs=[out, out],
                 out_shape=[jax.ShapeDtypeStruct((Lr, d * qw), BF16), jax.ShapeDtypeStruct((Lr, d * qw), F32)],
                 sem=("parallel", "parallel"))(*ins)


def _band_bwd(qa, ka, va, doa, oa, lsea, *, d, nq, nkv, qcol, kcol, vcol, docol, max_dist, sinks=None, name):
    Lr = qa.shape[0]
    nb = Lr // BLK
    qw, kw, G = nq * HEAD_DIM, nkv * HEAD_DIM, nq // nkv
    transposed = G > 1
    last = lambda i: jnp.minimum(i, nb - 1)
    cur = lambda colf, w: pl.BlockSpec((BLK, w), lambda r, i: (last(i), colf(r)))
    prv = lambda colf, w: pl.BlockSpec((BLK, w), lambda r, i: (jnp.maximum(last(i) - 1, 0), colf(r)))
    own = lambda r: r
    ins = [qa, ka, ka, va, va, doa, oa, lsea]
    specs = [cur(qcol, qw), cur(kcol, kw), prv(kcol, kw), cur(vcol, kw), prv(vcol, kw), cur(docol, qw), cur(own, qw),
             cur(own, qw)]
    has_sinks = sinks is not None
    if has_sinks:
        ins.append(sinks); specs.append(pl.BlockSpec(memory_space=pltpu.SMEM))
    out_shape = [jax.ShapeDtypeStruct((Lr, d * qw), F32), jax.ShapeDtypeStruct((Lr, d * kw), F32),
                 jax.ShapeDtypeStruct((Lr, d * kw), F32)]
    behind = lambda r, i: (jnp.maximum(i - 1, 0), r)
    out_specs = [pl.BlockSpec((BLK, qw), lambda r, i: (last(i), r)), pl.BlockSpec((BLK, kw), behind),
                 pl.BlockSpec((BLK, kw), behind)]
    if has_sinks:
        out_shape.append(jax.ShapeDtypeStruct((8, 128), F32))
        out_specs.append(pl.BlockSpec((8, 128), lambda r, i: (0, 0)))

    def body(*refs):
        it = iter(refs)
        q_ref, kc_ref, kp_ref, vc_ref, vp_ref, do_ref, o_ref, ls_ref = (next(it) for _ in range(8))
        sk_ref = next(it) if has_sinks else None
        dq_ref, dk_ref, dv_ref = next(it), next(it), next(it)
        dsk_ref = next(it) if has_sinks else None
        dk_car, dv_car = next(it), next(it)
        r_id, i = pl.program_id(0), pl.program_id(1)

        @pl.when(i == 0)
        def _():
            dk_car[...] = jnp.zeros_like(dk_car)
            dv_car[...] = jnp.zeros_like(dv_car)

        if has_sinks:
            @pl.when((r_id == 0) & (i == 0))
            def _():
                dsk_ref[...] = jnp.zeros_like(dsk_ref)

        @pl.when(i == nb)
        def _():
            dk_ref[...] = dk_car[...]
            dv_ref[...] = dv_car[...]

        @pl.when(i < nb)
        def _():
            bias = _band_bias(max_dist, i > 0)
            k2 = jnp.concatenate([kp_ref[...], kc_ref[...]], axis=0)
            v2 = jnp.concatenate([vp_ref[...], vc_ref[...]], axis=0)
            if has_sinks:
                lane = lax.broadcasted_iota(jnp.int32, (8, 128), 1)
                dsk = jnp.zeros((8, 128), F32)
            for kv in range(nkv):
                ks = slice(kv * HEAD_DIM, (kv + 1) * HEAD_DIM)
                kh, vh = k2[:, ks], v2[:, ks]
                shape = (HEAD_DIM, 2 * BLK) if transposed else (2 * BLK, HEAD_DIM)
                dk, dv = jnp.zeros(shape, F32), jnp.zeros(shape, F32)
                for g in range(G):
                    h = kv * G + g
                    hs = slice(h * HEAD_DIM, (h + 1) * HEAD_DIM)
                    q = q_ref[:, hs] * jnp.asarray(Q_SCALE, BF16)
                    do = do_ref[:, hs]
                    lse = ls_ref[:, h * HEAD_DIM:h * HEAD_DIM + 1]
                    dl = jnp.sum(do.astype(F32) * o_ref[:, hs].astype(F32), axis=-1, keepdims=True)
                    p = jnp.exp(_dot(q, kh, NT) + bias - lse)
                    ds = (p * (_dot(do, vh, NT) - dl)).astype(BF16)
                    dq_ref[:, hs] = _dot(ds, kh) * Q_SCALE
                    if transposed:
                        dk = dk + _dot(q, ds, TN)
                        dv = dv + _dot(do, p.astype(BF16), TN)
                    else:
                        dk = dk + _dot(ds, q, TN)
                        dv = dv + _dot(p.astype(BF16), do, TN)
                    if has_sinks:
                        val = -jnp.sum(jnp.exp(sk_ref[h] - lse) * dl, axis=0, keepdims=True)
                        dsk = dsk + jnp.where(lane == h, val, 0.0)
                if transposed:
                    dk, dv = dk.T, dv.T
                dk_ref[:, ks] = dk_car[:, ks] + dk[:BLK]
                dv_ref[:, ks] = dv_car[:, ks] + dv[:BLK]
                dk_car[:, ks] = dk[BLK:]
                dv_car[:, ks] = dv[BLK:]
            if has_sinks:
                dsk_ref[...] += dsk

    return _call(body, name=name, grid=(d, nb + 1), in_specs=specs, out_specs=out_specs, out_shape=out_shape,
                 scratch=[pltpu.VMEM((BLK, kw), F32), pltpu.VMEM((BLK, kw), F32)], sem=("arbitrary", "arbitrary"))(*ins)


def _attn_grad_combine(branches, tabs, *, name, tm=256):
    L, qw = branches[0][0].shape
    kw = branches[0][1].shape[1]
    nbr = len(branches)
    row = lambda w: pl.BlockSpec((tm, w), lambda i: (i, 0))
    ins, specs = [], []
    for dq, dk, dv in branches:
        ins += [dq, dk, dv]; specs += [row(qw), row(kw), row(kw)]
    ins += list(tabs); specs += [row(128)] * 3

    def body(*refs):
        c, s1, s2 = (t[...] for t in refs[3 * nbr:3 * nbr + 3])
        o_ref = refs[-1]
        for part, (w, off, rot) in enumerate(((qw, 0, True), (kw, qw, True), (kw, qw + kw, False))):
            for cb in range(w // 128):
                cs = slice(cb * 128, (cb + 1) * 128)
                v = refs[part][:, cs]
                for b in range(1, nbr):
                    v = v + refs[3 * b + part][:, cs]
                if rot:
                    v = _rope_bwd(v, c, s1, s2)
                o_ref[:, off + cb * 128:off + (cb + 1) * 128] = v.astype(BF16)

    return _call(body, name=name, grid=(L // tm,), in_specs=specs, out_specs=row(qw + 2 * kw),
                 out_shape=jax.ShapeDtypeStruct((L, qw + 2 * kw), BF16), sem=("parallel",))(*ins)


def _xattn_fwd(q, kv, *, name, tq=512):
    L, W = q.shape
    scale = XA_HEAD_DIM ** -0.5
    row = pl.BlockSpec((tq, W), lambda i: (i, 0))
    kvs = pl.BlockSpec((N_MEM, 2 * W), lambda i: (0, 0))

    def body(q_ref, kv_ref, o_ref, lse_ref):
        for h in range(XA_HEADS):
            hs = slice(h * XA_HEAD_DIM, (h + 1) * XA_HEAD_DIM)
            vs = slice(W + h * XA_HEAD_DIM, W + (h + 1) * XA_HEAD_DIM)
            s = _dot(q_ref[:, hs], kv_ref[:, hs], NT) * scale
            m = jnp.max(s, axis=-1, keepdims=True)
            p = jnp.exp(s - m)
            l = jnp.sum(p, axis=-1, keepdims=True)
            o_ref[:, hs] = (_dot(p.astype(BF16), kv_ref[:, vs]) / l).astype(BF16)
            lse_ref[:, hs] = jnp.broadcast_to(m + jnp.log(l), (tq, XA_HEAD_DIM))

    return _call(body, name=name, grid=(L // tq,), in_specs=[row, kvs], out_specs=[row, row],
                 out_shape=[jax.ShapeDtypeStruct((L, W), BF16), jax.ShapeDtypeStruct((L, W), F32)], sem=("parallel",))(q, kv)


def _xattn_bwd(q, kv, o, lse, do, *, name, tq=512):
    L, W = q.shape
    scale = XA_HEAD_DIM ** -0.5
    row = pl.BlockSpec((tq, W), lambda i: (i, 0))
    kvs = pl.BlockSpec((N_MEM, 2 * W), lambda i: (0, 0))

    def body(q_ref, kv_ref, o_ref, lse_ref, do_ref, dq_ref, dkv_ref):
        @pl.when(pl.program_id(0) == 0)
        def _():
            dkv_ref[...] = jnp.zeros_like(dkv_ref)

        for h in range(XA_HEADS):
            hs = slice(h * XA_HEAD_DIM, (h + 1) * XA_HEAD_DIM)
            vs = slice(W + h * XA_HEAD_DIM, W + (h + 1) * XA_HEAD_DIM)
            qh, kh, vh, doh = q_ref[:, hs], kv_ref[:, hs], kv_ref[:, vs], do_ref[:, hs]
            p = jnp.exp(_dot(qh, kh, NT) * scale - lse_ref[:, h * XA_HEAD_DIM:h * XA_HEAD_DIM + 1])
            dl = jnp.sum(doh.astype(F32) * o_ref[:, hs].astype(F32), axis=-1, keepdims=True)
            ds = (p * (_dot(doh, vh, NT) - dl) * scale).astype(BF16)
            dq_ref[:, hs] = _dot(ds, kh).astype(BF16)
            dkv_ref[:, hs] += _dot(ds, qh, TN)
            dkv_ref[:, vs] += _dot(p.astype(BF16), doh, TN)

    return _call(body, name=name, grid=(L // tq,), in_specs=[row, kvs, row, row, row], out_specs=[row, kvs],
                 out_shape=[jax.ShapeDtypeStruct((L, W), BF16), jax.ShapeDtypeStruct((N_MEM, 2 * W), F32)],
                 sem=("arbitrary",))(q, kv, o, lse, do)


def _neg_expm1(z):
    series = -(z * (1.0 + z * (0.5 + z * (1.0 / 6.0 + z * (1.0 / 24.0 + z * (1.0 / 120.0))))))
    return jnp.where(z > -0.05, series, 1.0 - jnp.exp(z))


def _softplus(z):
    return jnp.maximum(z, 0.0) + jnp.log(1.0 + jnp.exp(-jnp.abs(z)))


def _gelu_parts(y):
    c = 0.7978845608028654
    t = jnp.tanh(c * (y + 0.044715 * y * y * y))
    gy = 0.5 * y * (1.0 + t)
    dgy = 0.5 * (1.0 + t) + 0.5 * y * (1.0 - t * t) * c * (1.0 + 3.0 * 0.044715 * y * y)
    return gy, dgy


def _lru_gates(xc, wa_ref, ba, wx_ref, bx, sp):
    rs, igs = [], []
    for hd in range(LRU_HEADS):
        sl = slice(hd * LRU_HEAD_DIM, (hd + 1) * LRU_HEAD_DIM)
        xh = xc[:, sl].astype(BF16)
        rs.append(_sigmoid(_dot(xh, wa_ref[hd]) + ba[:, sl]))
        igs.append(_sigmoid(_dot(xh, wx_ref[hd]) + bx[:, sl]))
    r, ig = jnp.concatenate(rs, axis=1), jnp.concatenate(igs, axis=1)
    la = -LRU_C * r * sp
    return r, ig, jnp.exp(la), _neg_expm1(2.0 * la)


def _conv_taps(x_ext, halo):
    n = x_ext.shape[0]
    return [x_ext[halo:] if k == CONV_WIDTH - 1 else pltpu.roll(x_ext, CONV_WIDTH - 1 - k, 0)[halo:]
            for k in range(CONV_WIDTH)]


def _lru_fwd(proj, cw, cb, wa, ba, wx, bx, lam, *, name, tc=512):
    L = proj.shape[0]
    W = LRU_HEADS * LRU_HEAD_DIM
    nb = L // tc
    whole = lambda shape: pl.BlockSpec(shape, lambda i: (0,) * len(shape))
    specs = [pl.BlockSpec((tc, W), lambda i: (i, 0)), pl.BlockSpec((tc, W), lambda i: (i, 1)),
             pl.BlockSpec((16, W), lambda i: (jnp.maximum(i * (tc // 16) - 1, 0), 0)),
             whole((CONV_WIDTH, W)), whole((1, W)), whole((LRU_HEADS, LRU_HEAD_DIM, LRU_HEAD_DIM)), whole((1, W)),
             whole((LRU_HEADS, LRU_HEAD_DIM, LRU_HEAD_DIM)), whole((1, W)), whole((1, W))]
    out_specs = [pl.BlockSpec((tc, W), lambda i: (i, 0))] * 2
    out_shape = [jax.ShapeDtypeStruct((L, W), BF16), jax.ShapeDtypeStruct((L, W), F32)]

    def body(x_ref, y_ref, xh_ref, cw_ref, cb_ref, wa_ref, ba_ref, wx_ref, bx_ref, lam_ref, rec_ref, hs_ref,
             hcar, a_scr, b_scr):
        i = pl.program_id(0)

        @pl.when(i == 0)
        def _():
            hcar[...] = jnp.zeros_like(hcar)

        halo = jnp.where(i > 0, xh_ref[...].astype(F32), 0.0)
        taps = _conv_taps(jnp.concatenate([halo, x_ref[...].astype(F32)], axis=0), 16)
        xc = cb_ref[...] + sum(cw_ref[k:k + 1, :] * taps[k] for k in range(CONV_WIDTH))
        _, ig, a, om = _lru_gates(xc, wa_ref, ba_ref[...], wx_ref, bx_ref[...], _softplus(-lam_ref[...]))
        b = jnp.sqrt(om) * (ig * xc)
        rowmod = lax.broadcasted_iota(jnp.int32, (tc, W), 0) & 7
        for s in (1, 2, 4):
            keep = rowmod >= s
            b = jnp.where(keep, a * pltpu.roll(b, s, 0) + b, b)
            a = jnp.where(keep, a * pltpu.roll(a, s, 0), a)
        a_scr[...] = a
        b_scr[...] = b

        def tile(j, hc):
            rows = pl.ds(pl.multiple_of(j * 8, 8), 8)
            ht = a_scr[rows, :] * hc + b_scr[rows, :]
            hs_ref[rows, :] = ht
            return jnp.broadcast_to(ht[7:8, :], (8, W))

        hcar[...] = lax.fori_loop(0, tc // 8, tile, hcar[...])
        gy, _ = _gelu_parts(y_ref[...].astype(F32))
        rec_ref[...] = (hs_ref[...] * gy).astype(BF16)

    return _call(body, name=name, grid=(nb,), in_specs=specs, out_specs=out_specs, out_shape=out_shape,
                 scratch=[pltpu.VMEM((8, W), F32), pltpu.VMEM((tc, W), F32), pltpu.VMEM((tc, W), F32)],
                 sem=("arbitrary",))(proj, proj, proj, cw, cb, wa, ba, wx, bx, lam)


def _lru_bwd(proj, hs, drec_src, cw, cb, wa, ba, wx, bx, lam, *, name, tc=256):
    L = proj.shape[0]
    W = LRU_HEADS * LRU_HEAD_DIM
    nb = L // tc
    tb = lambda i: nb - 1 - i
    whole = lambda shape: pl.BlockSpec(shape, lambda i: (0,) * len(shape))
    gate_w = (LRU_HEADS, LRU_HEAD_DIM, LRU_HEAD_DIM)
    specs = [pl.BlockSpec((tc, W), lambda i: (tb(i), 0)), pl.BlockSpec((tc, W), lambda i: (tb(i), 1)),
             pl.BlockSpec((16, W), lambda i: (jnp.maximum(tb(i) * (tc // 16) - 1, 0), 0)),
             pl.BlockSpec((tc, W), lambda i: (tb(i), 0)),
             pl.BlockSpec((8, W), lambda i: (jnp.maximum(tb(i) * (tc // 8) - 1, 0), 0)),
             pl.BlockSpec((tc, W), lambda i: (tb(i), 0)),
             whole((CONV_WIDTH, W)), whole((1, W)), whole(gate_w), whole((1, W)), whole(gate_w), whole((1, W)), whole((1, W))]
    out_specs = [pl.BlockSpec((tc, 2 * W), lambda i: (tb(i), 0)), whole((CONV_WIDTH, W)), whole((1, W)), whole(gate_w),
                 whole((1, W)), whole(gate_w), whole((1, W)), whole((1, W))]
    vec = jax.ShapeDtypeStruct((1, W), F32)
    out_shape = [jax.ShapeDtypeStruct((L, 2 * W), BF16), jax.ShapeDtypeStruct((CONV_WIDTH, W), F32), vec,
                 jax.ShapeDtypeStruct(gate_w, F32), vec, jax.ShapeDtypeStruct(gate_w, F32), vec, vec]

    def body(x_ref, y_ref, xh_ref, hs_ref, hh_ref, dr_ref, cw_ref, cb_ref, wa_ref, ba_ref, wx_ref, bx_ref, lam_ref,
             dxy_ref, dcw_ref, dcb_ref, dwa_ref, dba_ref, dwx_ref, dbx_ref, dlam_ref, gcar, dxc_car, a_scr, b_scr, g_scr):
        pid = pl.program_id(0)
        t = tb(pid)
        accs = (dcw_ref, dcb_ref, dwa_ref, dba_ref, dwx_ref, dbx_ref, dlam_ref)

        @pl.when(pid == 0)
        def _():
            gcar[...] = jnp.zeros_like(gcar)
            dxc_car[...] = jnp.zeros_like(dxc_car)
            for r in accs:
                r[...] = jnp.zeros_like(r)

        halo = jnp.where(t > 0, xh_ref[...].astype(F32), 0.0)
        taps = _conv_taps(jnp.concatenate([halo, x_ref[...].astype(F32)], axis=0), 16)
        xc = cb_ref[...] + sum(cw_ref[k:k + 1, :] * taps[k] for k in range(CONV_WIDTH))
        lam = lam_ref[...]
        sp = _softplus(-lam)
        r, ig, a, om = _lru_gates(xc, wa_ref, ba_ref[...], wx_ref, bx_ref[...], sp)
        sq = jnp.sqrt(om)
        hblk = hs_ref[...]
        hprev = pltpu.roll(jnp.concatenate([jnp.where(t > 0, hh_ref[...], 0.0), hblk], axis=0), 1, 0)[8:]
        gy, dgy = _gelu_parts(y_ref[...].astype(F32))
        drec = dr_ref[...].astype(F32)
        dxy_ref[:, W:] = (drec * hblk * dgy).astype(BF16)

        rowidx = lax.broadcasted_iota(jnp.int32, (tc, W), 0)
        rowmod = rowidx & 7
        ca = jnp.where(rowidx == tc - 1, 1.0, pltpu.roll(a, tc - 1, 0))
        cbv = drec * gy
        for s in (1, 2, 4):
            keep = rowmod < 8 - s
            cbv = jnp.where(keep, ca * pltpu.roll(cbv, tc - s, 0) + cbv, cbv)
            ca = jnp.where(keep, ca * pltpu.roll(ca, tc - s, 0), ca)
        a_scr[...] = ca
        b_scr[...] = cbv

        def tile(k, gc):
            j = tc // 8 - 1 - k
            rows = pl.ds(pl.multiple_of(j * 8, 8), 8)
            gt = a_scr[rows, :] * gc + b_scr[rows, :]
            g_scr[rows, :] = gt
            return jnp.broadcast_to(gt[0:1, :], (8, W))

        lax.fori_loop(0, tc // 8, tile, gcar[...])
        G = g_scr[...]
        gcar[...] = jnp.broadcast_to(a[0:1, :] * G[0:1, :], (8, W))

        da = G * hprev
        dsq = G * (ig * xc)
        di = G * (sq * xc)
        dxc = G * (sq * ig)
        dla = da * a - 2.0 * a * a * (dsq * 0.5 * lax.rsqrt(om))
        dlam_ref[...] += jnp.sum(dla * (-LRU_C * r), axis=0, keepdims=True) * (-_sigmoid(-lam))
        dpr = dla * (-LRU_C * sp) * r * (1.0 - r)
        dpi = di * ig * (1.0 - ig)
        dba_ref[...] += jnp.sum(dpr, axis=0, keepdims=True)
        dbx_ref[...] += jnp.sum(dpi, axis=0, keepdims=True)
        back = []
        for hd in range(LRU_HEADS):
            sl = slice(hd * LRU_HEAD_DIM, (hd + 1) * LRU_HEAD_DIM)
            xh, dprh, dpih = xc[:, sl].astype(BF16), dpr[:, sl].astype(BF16), dpi[:, sl].astype(BF16)
            back.append(_dot(dprh, wa_ref[hd], NT) + _dot(dpih, wx_ref[hd], NT))
            dwa_ref[hd] += _dot(xh, dprh, TN)
            dwx_ref[hd] += _dot(xh, dpih, TN)
        dxc = dxc + jnp.concatenate(back, axis=1)
        dcb_ref[...] += jnp.sum(dxc, axis=0, keepdims=True)
        for k in range(CONV_WIDTH):
            dcw_ref[k:k + 1, :] += jnp.sum(dxc * taps[k], axis=0, keepdims=True)
        ext = jnp.concatenate([dxc, dxc_car[...]], axis=0)
        dx = cw_ref[CONV_WIDTH - 1:CONV_WIDTH, :] * dxc
        for k in range(CONV_WIDTH - 1):
            dx = dx + cw_ref[k:k + 1, :] * pltpu.roll(ext, tc + 8 - (CONV_WIDTH - 1 - k), 0)[:tc]
        dxc_car[...] = dxc[0:8, :]
        dxy_ref[:, :W] = dx.astype(BF16)

    scratch = [pltpu.VMEM((8, W), F32), pltpu.VMEM((8, W), F32)] + [pltpu.VMEM((tc, W), F32)] * 3
    return _call(body, name=name, grid=(nb,), in_specs=specs, out_specs=out_specs, out_shape=out_shape, scratch=scratch,
                 sem=("arbitrary",))(proj, proj, proj, hs, hs, drec_src, cw, cb, wa, ba, wx, bx, lam)


def _final_loss(h, gain, target, *, name, tm=256):
    M, K = h.shape
    row = pl.BlockSpec((tm, K), lambda i: (i, 0))
    vec = pl.BlockSpec((1, K), lambda i: (0, 0))
    one = pl.BlockSpec((1, 128), lambda i: (0, 0))

    def body(h_ref, g_ref, t_ref, dh_ref, dg_ref, loss_ref):
        @pl.when(pl.program_id(0) == 0)
        def _():
            dg_ref[...] = jnp.zeros_like(dg_ref)
            loss_ref[...] = jnp.zeros_like(loss_ref)

        x = h_ref[...]
        r = lax.rsqrt(jnp.mean(x * x, axis=-1, keepdims=True) + NORM_EPS)
        xhat = x * r
        err = xhat * g_ref[...] - t_ref[...]
        loss_ref[...] += 0.5 / K * jnp.sum(err * err)
        dy = err * (1.0 / K)
        dg_ref[...] += jnp.sum(dy * xhat, axis=0, keepdims=True)
        dxh = dy * g_ref[...]
        dh_ref[...] = r * (dxh - xhat * jnp.mean(dxh * xhat, axis=-1, keepdims=True))

    return _call(body, name=name, grid=(M // tm,), in_specs=[row, vec, row], out_specs=[row, vec, one],
                 out_shape=[jax.ShapeDtypeStruct((M, K), F32), jax.ShapeDtypeStruct((1, K), F32),
                            jax.ShapeDtypeStruct((1, 128), F32)], sem=("arbitrary",))(h, gain.reshape(1, K), target)


def _dilated_merge(branches, *, name, tm=512):
    L, W = branches[0].shape
    nbr = len(branches) // 2
    row = pl.BlockSpec((tm, W), lambda i: (i, 0))

    def body(*refs):
        o_ref, lse_ref = refs[-2], refs[-1]
        lses = [refs[2 * b + 1][...] for b in range(nbr)]
        m = lses[0]
        for t in lses[1:]:
            m = jnp.maximum(m, t)
        ws = [jnp.exp(t - m) for t in lses]
        den = ws[0]
        for t in ws[1:]:
            den = den + t
        acc = ws[0] * refs[0][...].astype(F32)
        for b in range(1, nbr):
            acc = acc + ws[b] * refs[2 * b][...].astype(F32)
        o_ref[...] = (acc / den).astype(BF16)
        lse_ref[...] = m + jnp.log(den)

    return _call(body, name=name, grid=(L // tm,), in_specs=[row] * (2 * nbr), out_specs=[row, row],
                 out_shape=[jax.ShapeDtypeStruct((L, W), BF16), jax.ShapeDtypeStruct((L, W), F32)], sem=("parallel",))(*branches)


def _dilated_fwd(proj0):
    L = proj0.shape[0]
    qkv = proj0[:, 2 * D_MODEL:]
    W = B_HEADS * HEAD_DIM
    outs = []
    for window, d in DILATED_PATTERN:
        view = qkv.reshape(L // d, d * 3 * W)
        o, lse = _band_fwd(view, view, view, d=d, nq=B_HEADS, nkv=B_HEADS, qcol=lambda r: 3 * r, kcol=lambda r: 3 * r + 1,
                           vcol=lambda r: 3 * r + 2, max_dist=window // d, name=f"dilated_fwd_d{d}")
        outs += [o.reshape(L, W), lse.reshape(L, W)]
    return _dilated_merge(outs, name="dilated_merge")


def _dilated_bwd(proj0, att, lse, datt, tabs):
    L = proj0.shape[0]
    qkv = proj0[:, 2 * D_MODEL:]
    Wh = B_HEADS * HEAD_DIM
    branches = []
    for window, d in DILATED_PATTERN:
        view = qkv.reshape(L // d, d * 3 * Wh)
        v1 = lambda t: t.reshape(L // d, d * Wh)
        outs = _band_bwd(view, view, view, v1(datt), v1(att), v1(lse), d=d, nq=B_HEADS, nkv=B_HEADS,
                         qcol=lambda r: 3 * r, kcol=lambda r: 3 * r + 1, vcol=lambda r: 3 * r + 2, docol=lambda r: r,
                         max_dist=window // d, name=f"dilated_bwd_d{d}")
        branches.append([o.reshape(L, Wh) for o in outs])
    return _attn_grad_combine(branches, tabs, name="dilated_grad_combine")


def _device_step(x, mem, target, w, on_grads=None):
    L = x.shape[0]
    tabs = _rope_tables(L)
    g = {}
    saved = []
    h = x
    for layer in range(2):
        sv = {"h_mix": h}
        if layer == 0:
            proj, n = _rowmm(h, w["ab_w_in"], name="l0_in_proj", gain=w["mix_norm"][0],
                             rope=(2 * D_MODEL, 2 * D_MODEL + 2 * B_HEADS * HEAD_DIM, tabs))
            rec, hs = _lru_fwd(proj, w["lru_conv_w"], w["lru_conv_b"], w["lru_wa"], w["lru_ba"], w["lru_wx"], w["lru_bx"],
                               w["lru_lambda"], name="lru_fwd")
            att, lse = _dilated_fwd(proj)
            mix = jnp.concatenate([rec, att], axis=1)
            (h,) = _rowmm(mix, w["ab_w_out"], name="l0_out_proj", res=h)
            sv.update(hs=hs)
        else:
            proj, n = _rowmm(h, w["c_w_qkv"], name="l1_qkv_proj", gain=w["mix_norm"][1], bias=w["c_b_qkv"],
                             rope=(0, (C_HEADS + C_KV_HEADS) * HEAD_DIM, tabs))
            mix, lse = _band_fwd(proj, proj, proj, d=1, nq=C_HEADS, nkv=C_KV_HEADS, qcol=lambda r: 0, kcol=lambda r: 8,
                                 vcol=lambda r: 9, max_dist=C_WINDOW - 1, sinks=w["c_sinks"], name="swa_fwd")
            (h,) = _rowmm(mix, w["c_w_out"], name="l1_out_proj", res=h, bias=w["c_b_out"])
        sv.update(proj=proj, n_mix=n, mix=mix, lse=lse, h_xa=h)
        xq, nx = _rowmm(h, w["xa_wq"][layer][None], name=f"xa_q_proj{layer}", gain=w["xa_norm"][layer])
        kv, nm = _rowmm(mem, w["xa_wkv"][layer][None], name=f"xa_kv_proj{layer}", gain=w["xa_mem_norm"][layer])
        xo, xlse = _xattn_fwd(xq, kv, name=f"xa_fwd{layer}")
        (h,) = _rowmm(xo, w["xa_wo"][layer], name=f"xa_out_proj{layer}", res=h)
        sv.update(xq=xq, nx=nx, kv=kv, nm=nm, xo=xo, xlse=xlse, h_ffn=h)
        gu, nf, act = _rowmm(h, w["ffn_w_gate_up"], layer=layer, name=f"ffn_in{layer}", gain=w["ffn_norm"][layer], swiglu=True)
        (h,) = _rowmm(act, w["ffn_w_down"][layer][None], name=f"ffn_out{layer}", res=h, tm=512)
        sv.update(gu=gu, nf=nf, act=act)
        saved.append(sv)

    dh, g["final_norm"], loss = _final_loss(h, w["final_norm"], target, name="final_loss")

    stk = {k: [None, None] for k in ("xa_norm", "xa_mem_norm", "ffn_norm", "mix_norm")}
    for layer in (1, 0):
        sv = saved[layer]
        (g["ffn_w_down", layer],) = _mm_tn(sv["act"], dh, S=1, name=f"ffn_down_dw{layer}", kk=D_FF // 2)
        (dgu,) = _mm_nt(dh, w["ffn_w_down"][layer][None], name=f"ffn_dact{layer}", mode="swiglu", kchunk=D_FF // 2, gu=sv["gu"])
        (g["ffn_w_gate_up", layer],) = _mm_tn(sv["nf"], dgu, S=N_CHIPS, name=f"ffn_gu_dw{layer}")
        dh, stk["ffn_norm"][layer] = _mm_nt(dgu, w["ffn_w_gate_up"], layer=layer, name=f"ffn_dx{layer}", mode="norm",
                                            h=sv["h_ffn"], gain=w["ffn_norm"][layer], dh=dh)
        (g["xa_wo", layer],) = _mm_tn(sv["xo"], dh, S=N_CHIPS, name=f"xa_wo_dw{layer}")
        (dxo,) = _mm_nt(dh, w["xa_wo"][layer], name=f"xa_dxo{layer}", mode="plain")
        dxq, dkv = _xattn_bwd(sv["xq"], sv["kv"], sv["xo"], sv["xlse"], dxo, name=f"xa_bwd{layer}")
        (g["xa_wq", layer],) = _mm_tn(sv["nx"], dxq, S=1, name=f"xa_wq_dw{layer}")
        dh, stk["xa_norm"][layer] = _mm_nt(dxq, w["xa_wq"][layer][None], name=f"xa_dx{layer}", mode="norm", h=sv["h_xa"],
                                           gain=w["xa_norm"][layer], dh=dh)
        (g["xa_wkv", layer],) = _mm_tn(sv["nm"], dkv, S=1, name=f"xa_wkv_dw{layer}")
        _, stk["xa_mem_norm"][layer] = _mm_nt(dkv, w["xa_wkv"][layer][None], name=f"xa_dmem{layer}", mode="norm", h=mem,
                                              gain=w["xa_mem_norm"][layer])
        if layer == 1:
            g["c_w_out"], g["c_b_out"] = _mm_tn(sv["mix"], dh, S=1, name="l1_out_dw", bias=True)
            (dmix,) = _mm_nt(dh, w["c_w_out"], name="l1_dmix", mode="plain")
            dq, dk, dv, dsk = _band_bwd(sv["proj"], sv["proj"], sv["proj"], dmix, sv["mix"], sv["lse"], d=1, nq=C_HEADS,
                                        nkv=C_KV_HEADS, qcol=lambda r: 0, kcol=lambda r: 8, vcol=lambda r: 9,
                                        docol=lambda r: 0, max_dist=C_WINDOW - 1, sinks=w["c_sinks"], name="swa_bwd")
            g["c_sinks"] = dsk[0, :C_HEADS]
            dproj = _attn_grad_combine([(dq, dk, dv)], tabs, name="swa_grad_combine")
            g["c_w_qkv"], g["c_b_qkv"] = _mm_tn(sv["n_mix"], dproj, S=1, name="l1_qkv_dw", bias=True)
            dh, stk["mix_norm"][1] = _mm_nt(dproj, w["c_w_qkv"], name="l1_dx", mode="norm", h=sv["h_mix"],
                                            gain=w["mix_norm"][1], dh=dh)
            if on_grads is not None:
                dh = on_grads("layer1", g, dh)
        else:
            if on_grads is not None:
                dh = on_grads("layer0_ffn_xa", g, dh)
            (g["ab_w_out"],) = _mm_tn(sv["mix"], dh, S=1, name="l0_out_dw", kk=768)
            (dmix,) = _mm_nt(dh, w["ab_w_out"], name="l0_dmix", mode="plain", kchunk=768)
            (dxy, g["lru_conv_w"], g["lru_conv_b"], g["lru_wa"], g["lru_ba"], g["lru_wx"], g["lru_bx"],
             g["lru_lambda"]) = _lru_bwd(sv["proj"], sv["hs"], dmix, w["lru_conv_w"], w["lru_conv_b"], w["lru_wa"],
                                         w["lru_ba"], w["lru_wx"], w["lru_bx"], w["lru_lambda"], name="lru_bwd")
            dqkv = _dilated_bwd(sv["proj"], sv["mix"][:, D_MODEL:], sv["lse"], dmix[:, D_MODEL:], tabs)
            dproj = jnp.concatenate([dxy, dqkv], axis=1)
            (g["ab_w_in"],) = _mm_tn(sv["n_mix"], dproj, S=N_CHIPS, name="l0_in_dw")
            dh, stk["mix_norm"][0] = _mm_nt(dproj, w["ab_w_in"], name="l0_dx", mode="norm", h=sv["h_mix"],
                                            gain=w["mix_norm"][0], dh=dh)
    for k, v in stk.items():
        g[k] = jnp.concatenate(v, axis=0)
    return loss[0, 0], dh, g


ANY = pl.BlockSpec(memory_space=pl.ANY)
MESH = pl.DeviceIdType.MESH


def _place():
    x, y, c = lax.axis_index("x"), lax.axis_index("y"), lax.axis_index("c")
    return x, y, c, [(1 - x, y), (x, 1 - y), (1 - x, 1 - y)]


def _remote(send_sems, recv_sems):
    def copy(k, src, dst, to):
        return pltpu.make_async_remote_copy(src_ref=src, dst_ref=dst, send_sem=send_sems.at[k], recv_sem=recv_sems.at[k],
                                            device_id=to, device_id_type=MESH)
    return copy


def _halves(ref, n_rows):
    rh = n_rows // 2
    return lambda lead, hh: ref.at[(*lead, pl.ds(hh * rh, rh), slice(None))]


def _gather_weights(packs, spack):
    n = len(packs)

    def body(*refs):
        w_refs, s_ref, wf_refs, sf_ref = refs[:n], refs[n], refs[n + 1:2 * n + 1], refs[2 * n + 1]
        x, y, c, chips = _place()
        me, sib = 2 * x + y, (x, y, 1 - c)
        copy = _remote(*refs[-2:])
        src = [_halves(w_refs[g], packs[g].shape[0]) for g in range(n)]
        dst = [_halves(wf_refs[g], packs[g].shape[0]) for g in range(n)]
        sends = []
        for g in range(n):
            for j, (cx, cy) in enumerate(chips):
                sends.append(copy(3 * g + j, src[g]((), c), dst[g]((me,), c), (cx, cy, c)))
        for j, (cx, cy) in enumerate(chips):
            sends.append(copy(6 * n + j, s_ref, sf_ref.at[me], (cx, cy, c)))
        for cp in sends:
            cp.start()
        for g in range(n):
            for j, (cx, cy) in enumerate(chips):
                got = dst[g]((2 * cx + cy,), c)
                copy(3 * g + j, got, got, sib).wait_recv()
                fwd = copy(3 * n + 3 * g + j, got, got, sib)
                fwd.start()
                sends.append(fwd)
        for g in range(n):
            for j, (cx, cy) in enumerate(chips):
                got = dst[g]((2 * cx + cy,), 1 - c)
                copy(3 * n + 3 * g + j, got, got, sib).wait_recv()
        for j, (cx, cy) in enumerate(chips):
            copy(6 * n + j, s_ref, sf_ref.at[2 * cx + cy], sib).wait_recv()
        for cp in sends:
            cp.wait_send()

    ins = list(packs) + [spack]
    out_shape = [jax.ShapeDtypeStruct((N_CHIPS,) + a.shape, a.dtype) for a in ins]
    n_sems = 6 * n + 3
    outs = pl.pallas_call(body, name="gather_weights", out_shape=out_shape, in_specs=[ANY] * len(ins),
                          out_specs=[ANY] * len(ins),
                          scratch_shapes=[pltpu.SemaphoreType.DMA((n_sems,)), pltpu.SemaphoreType.DMA((n_sems,))])(*ins)
    chip = 2 * lax.axis_index("x") + lax.axis_index("y")
    outs = [lax.dynamic_update_index_in_dim(o, a, chip, 0) for o, a in zip(outs, ins)]
    return outs[:n], outs[n]


SEQUENCER_GATHER_IDS = {"mid": 1, "late": 5}


def _gather_weights_behind(packs, *, tag):
    n = len(packs)

    def body(*refs):
        w_refs, wf_refs = refs[:n], refs[n:2 * n]
        x, y, c, chips = _place()
        me, sib = 2 * x + y, (x, y, 1 - c)
        barrier = pltpu.get_barrier_semaphore()
        for peer in [(cx, cy, c) for cx, cy in chips] + [sib]:
            pl.semaphore_signal(barrier, inc=1, device_id=peer, device_id_type=MESH)
        pl.semaphore_wait(barrier, len(chips) + 1)
        copy = _remote(*refs[-2:])
        src = [_halves(w_refs[g], packs[g].shape[0]) for g in range(n)]
        dst = [_halves(wf_refs[g], packs[g].shape[0]) for g in range(n)]
        sends = []
        for g in range(n):
            for j, (cx, cy) in enumerate(chips):
                sends.append(copy(3 * g + j, src[g]((), c), dst[g]((me,), c), (cx, cy, c)))
        for cp in sends:
            cp.start()
        for g in range(n):
            for j, (cx, cy) in enumerate(chips):
                got = dst[g]((2 * cx + cy,), c)
                copy(3 * g + j, got, got, sib).wait_recv()
                fwd = copy(3 * n + 3 * g + j, got, got, sib)
                fwd.start()
                sends.append(fwd)
        for g in range(n):
            for j, (cx, cy) in enumerate(chips):
                got = dst[g]((2 * cx + cy,), 1 - c)
                copy(3 * n + 3 * g + j, got, got, sib).wait_recv()
        for cp in sends:
            cp.wait_send()

    out_type = [jax.ShapeDtypeStruct((N_CHIPS,) + a.shape, a.dtype) for a in packs]
    outs = pl.kernel(body, out_type=out_type, mesh=plsc.ScalarSubcoreMesh(axis_name="sequencer", num_cores=1),
                     name="gather_weights_behind_" + tag,
                     scratch_types=[pltpu.SemaphoreType.DMA((6 * n,)), pltpu.SemaphoreType.DMA((6 * n,))],
                     compiler_params=pltpu.CompilerParams(collective_id=SEQUENCER_GATHER_IDS[tag]))(*packs)
    chip = 2 * lax.axis_index("x") + lax.axis_index("y")
    return [lax.dynamic_update_index_in_dim(o, a, chip, 0) for o, a in zip(outs, packs)]


def _rs_pair_exchange(gpacks, *, name):
    n = len(gpacks)

    def body(*refs):
        g_refs, ra_refs = refs[:n], refs[n:2 * n]
        x, y, c, _ = _place()
        copy = _remote(*refs[-2:])
        cps = []
        for g in range(n):
            half = _halves(g_refs[g], gpacks[g].shape[1])
            cps += [copy(N_CHIPS * g + j, half((j,), 1 - c), ra_refs[g].at[j], (x, y, 1 - c)) for j in range(N_CHIPS)]
        for cp in cps:
            cp.start()
        for cp in cps:
            cp.wait()

    out_shape = [jax.ShapeDtypeStruct((N_CHIPS, a.shape[1] // 2, a.shape[2]), a.dtype) for a in gpacks]
    n_sems = N_CHIPS * n
    return pl.pallas_call(body, name=name, out_shape=out_shape, in_specs=[ANY] * n, out_specs=[ANY] * n,
                          scratch_shapes=[pltpu.SemaphoreType.DMA((n_sems,)), pltpu.SemaphoreType.DMA((n_sems,))])(*gpacks)


def _row_tile(rows, cap=512):
    return max(t for t in range(16, min(rows, cap) + 1, 16) if rows % t == 0)


def _rs_pair_add(place, gpack, ra, *, name):
    _, R, C = gpack.shape
    Rh = R // 2
    tr = _row_tile(Rh)
    nrb = Rh // tr

    def body(p_ref, g_ref, ra_ref, pair_ref, own_ref):
        s = g_ref[...].astype(F32) + ra_ref[...].astype(F32)
        pair_ref[...] = s.astype(BF16)

        @pl.when(pl.program_id(1) == p_ref[1])
        def _():
            own_ref[...] = s

    grid_spec = pltpu.PrefetchScalarGridSpec(
        num_scalar_prefetch=1, grid=(nrb, N_CHIPS),
        in_specs=[pl.BlockSpec((None, tr, C), lambda i, j, p: (j, p[0] * nrb + i, 0)),
                  pl.BlockSpec((None, tr, C), lambda i, j, p: (j, i, 0))],
        out_specs=[pl.BlockSpec((None, tr, C), lambda i, j, p: (j, i, 0)), pl.BlockSpec((tr, C), lambda i, j, p: (i, 0))])
    return pl.pallas_call(
        body, name=name, grid_spec=grid_spec,
        out_shape=[jax.ShapeDtypeStruct((N_CHIPS, Rh, C), BF16), jax.ShapeDtypeStruct((Rh, C), F32)],
        compiler_params=pltpu.CompilerParams(dimension_semantics=("arbitrary", "arbitrary"),
                                             vmem_limit_bytes=VMEM_LIMIT_V7X))(place, gpack, ra)


SEQUENCER_EXCHANGE_IDS = {"l1": 2, "l0a": 3, "l0b": 4}


def _rs_chip_exchange_behind(pairs, *, tag, small=None):
    n = len(pairs)
    has_small = small is not None

    def body(*refs):
        p_refs = refs[:n]
        s_ref = refs[n] if has_small else None
        rb_refs = refs[n + has_small:2 * n + has_small]
        rs_ref = refs[2 * n + 1] if has_small else None
        x, y, c, chips = _place()
        peers = [(1 - x if k & 4 else x, 1 - y if k & 2 else y, 1 - c if k & 1 else c) for k in range(1, 8)]
        shake = peers if has_small else [(cx, cy, c) for cx, cy in chips]
        barrier = pltpu.get_barrier_semaphore()
        for peer in shake:
            pl.semaphore_signal(barrier, inc=1, device_id=peer, device_id_type=MESH)
        pl.semaphore_wait(barrier, len(shake))
        copy = _remote(*refs[-2:])
        cps = []
        for g in range(n):
            cps += [copy(3 * g + j, p_refs[g].at[2 * cx + cy], rb_refs[g].at[j], (cx, cy, c)) for j, (cx, cy) in enumerate(chips)]
        if has_small:
            dev = 4 * x + 2 * y + c
            cps += [copy(3 * n + k, s_ref, rs_ref.at[dev], peer) for k, peer in enumerate(peers)]
        for cp in cps:
            cp.start()
        for g in range(n):
            for j in range(3):
                copy(3 * g + j, p_refs[g].at[0], rb_refs[g].at[j], (x, y, c)).wait_recv()
        if has_small:
            for k, (px, py, pc) in enumerate(peers):
                copy(3 * n + k, s_ref, rs_ref.at[4 * px + 2 * py + pc], (x, y, c)).wait_recv()
        for cp in cps:
            cp.wait_send()

    ins = list(pairs) + ([small] if has_small else [])
    out_type = [jax.ShapeDtypeStruct((3,) + p.shape[1:], p.dtype) for p in pairs]
    if has_small:
        out_type.append(jax.ShapeDtypeStruct((8,) + small.shape, small.dtype))
    n_sems = 3 * n + 7 * has_small
    outs = pl.kernel(body, out_type=out_type, mesh=plsc.ScalarSubcoreMesh(axis_name="sequencer", num_cores=1),
                     name="rs_chip_exchange_behind_" + tag,
                     scratch_types=[pltpu.SemaphoreType.DMA((n_sems,)), pltpu.SemaphoreType.DMA((n_sems,))],
                     compiler_params=pltpu.CompilerParams(collective_id=SEQUENCER_EXCHANGE_IDS[tag]))(*ins)
    if has_small:
        dev = 4 * lax.axis_index("x") + 2 * lax.axis_index("y") + lax.axis_index("c")
        outs = list(outs[:n]) + [lax.dynamic_update_index_in_dim(outs[n], small, dev, 0)]
    return outs


def _rs_final_add(place, own, rb, *, name):
    Rh, C = own.shape
    tr = _row_tile(Rh)
    nrb = Rh // tr

    def body(p_ref, o_ref, rb_ref, f_ref):
        f_ref[...] = ((o_ref[...] + rb_ref[0].astype(F32)) + rb_ref[1].astype(F32)) + rb_ref[2].astype(F32)

    grid_spec = pltpu.PrefetchScalarGridSpec(
        num_scalar_prefetch=1, grid=(nrb,),
        in_specs=[pl.BlockSpec((tr, C), lambda i, p: (i, 0)), pl.BlockSpec((3, tr, C), lambda i, p: (0, i, 0))],
        out_specs=pl.BlockSpec((tr, C), lambda i, p: (p[0] * nrb + i, 0)))
    return pl.pallas_call(
        body, name=name, grid_spec=grid_spec, out_shape=jax.ShapeDtypeStruct((2 * Rh, C), F32),
        compiler_params=pltpu.CompilerParams(dimension_semantics=("arbitrary",), vmem_limit_bytes=VMEM_LIMIT_V7X))(place, own, rb)


def _sum_slots(rs):
    n, rows, C = rs.shape

    def body(r_ref, o_ref):
        acc = r_ref[0]
        for k in range(1, n):
            acc = acc + r_ref[k]
        o_ref[...] = acc

    return _call(body, name="small_grad_sum", grid=(1,), in_specs=[pl.BlockSpec((n, rows, C), lambda i: (0, 0, 0))],
                 out_specs=pl.BlockSpec((rows, C), lambda i: (0, 0)), out_shape=jax.ShapeDtypeStruct((rows, C), F32),
                 sem=("arbitrary",))(rs)


def _rs_sibling_share(gbufs, *, name):
    n = len(gbufs)

    def body(*refs):
        g_refs = refs[n:2 * n]
        x, y, c, _ = _place()
        copy = _remote(*refs[-2:])
        halves = [_halves(g_refs[g], gbufs[g].shape[0]) for g in range(n)]
        outs = [copy(g, halves[g]((), c), halves[g]((), c), (x, y, 1 - c)) for g in range(n)]
        for cp in outs:
            cp.start()
        for g in range(n):
            copy(g, halves[g]((), 1 - c), halves[g]((), 1 - c), (x, y, c)).wait_recv()
        for cp in outs:
            cp.wait_send()

    return pl.pallas_call(body, name=name, out_shape=[jax.ShapeDtypeStruct(a.shape, a.dtype) for a in gbufs],
                          in_specs=[ANY] * n, out_specs=[ANY] * n, input_output_aliases={g: g for g in range(n)},
                          scratch_shapes=[pltpu.SemaphoreType.DMA((n,)), pltpu.SemaphoreType.DMA((n,))])(*gbufs)


def _adamw(w, g, m, v, *, name, g_row=0):
    rows, cols = w.shape
    tr = rows
    for cand in range(min(rows, 512), 7, -8):
        if rows % cand == 0 and g_row % cand == 0:
            tr = cand
            break
    spec = pl.BlockSpec((tr, cols), lambda i: (i, 0))
    g_spec = pl.BlockSpec((tr, cols), lambda i: (g_row // tr + i, 0))

    def body(w_ref, g_ref, m_ref, v_ref, d_ref, nm_ref, nv_ref):
        gg = g_ref[...]
        nm = ADAM_B1 * m_ref[...] + (1.0 - ADAM_B1) * gg
        nv = ADAM_B2 * v_ref[...] + (1.0 - ADAM_B2) * (gg * gg)
        m_hat = nm / (1.0 - ADAM_B1 ** ADAM_STEP)
        v_hat = nv / (1.0 - ADAM_B2 ** ADAM_STEP)
        d_ref[...] = -ADAM_LR * (m_hat / (jnp.sqrt(v_hat) + ADAM_EPS) + ADAM_WD * w_ref[...])
        nm_ref[...] = nm
        nv_ref[...] = nv

    return _call(body, name=name, grid=(rows // tr,), in_specs=[spec, g_spec, spec, spec], out_specs=[spec] * 3,
                 out_shape=[jax.ShapeDtypeStruct((rows, cols), F32)] * 3, sem=("parallel",))(w, g, m, v)


WEIGHT_NAMES = ("mix_norm", "ab_w_in", "lru_conv_w", "lru_conv_b", "lru_wa", "lru_ba", "lru_wx", "lru_bx", "lru_lambda",
                "ab_w_out", "c_w_qkv", "c_b_qkv", "c_sinks", "c_w_out", "c_b_out", "xa_norm", "xa_mem_norm", "xa_wq",
                "xa_wkv", "xa_wo", "ffn_norm", "ffn_w_gate_up", "ffn_w_down", "final_norm")
EARLY_GROUPS = (("ab_w_in",),)
MID_GROUPS = (("ab_w_out",), ("lru_wa", "lru_wx"))
LATE_GROUPS = (("c_w_out", "xa_wkv", "ffn_w_down"), ("ffn_w_gate_up",), ("xa_wo",), ("xa_wq",), ("c_w_qkv",))
GROUPS = EARLY_GROUPS + MID_GROUPS + LATE_GROUPS
REPLICATED = ("mix_norm", "lru_conv_b", "lru_lambda", "c_sinks", "xa_norm", "xa_mem_norm", "ffn_norm", "final_norm")
SMALL_SHARDED = ("lru_conv_w", "lru_ba", "lru_bx", "c_b_qkv", "c_b_out")
LANES = 1024


def _rows(v):
    flat = v.reshape(-1)
    return jnp.pad(flat, (0, -flat.shape[0] % LANES)).reshape(-1, LANES)


def _pack_small(parts, total, *, name):
    def body(*refs):
        o_ref = refs[-1]
        o_ref[...] = jnp.zeros_like(o_ref)
        row = 0
        for p_ref in refs[:-1]:
            o_ref[row:row + p_ref.shape[0], :] = p_ref[...]
            row += p_ref.shape[0]

    return _call(body, name=name, grid=(1,), in_specs=[pl.BlockSpec(p.shape, lambda i: (0, 0)) for p in parts],
                 out_specs=pl.BlockSpec((total, LANES), lambda i: (0, 0)),
                 out_shape=jax.ShapeDtypeStruct((total, LANES), F32), sem=("arbitrary",))(*parts)


def _from_shards(name, t):
    minor = t.shape[-1]
    if name == "ab_w_in":
        return t
    if name in ("ab_w_out", "c_w_out"):
        return t.reshape(1, -1, minor)
    if name == "ffn_w_gate_up":
        return t.reshape(N_CHIPS, 2, -1, minor)
    if name in ("xa_wq", "xa_wkv", "ffn_w_down"):
        return t.reshape(N_CHIPS, 2, -1, minor).transpose(1, 0, 2, 3).reshape(2, -1, minor)
    if name in ("lru_wa", "lru_wx"):
        return t.reshape(N_CHIPS, LRU_HEADS, -1, minor).transpose(1, 0, 2, 3).reshape(LRU_HEADS, LRU_HEAD_DIM, minor)
    if name == "xa_wo":
        return t.reshape(N_CHIPS, 2, -1, minor).transpose(1, 0, 2, 3)
    assert name == "c_w_qkv"
    return t.transpose(1, 0, 2).reshape(1, D_MODEL, -1)


def _piece_shards(name, g):
    minor = g.shape[-1]
    if name in ("ab_w_in", "ffn_w_gate_up", "xa_wo"):
        return g
    if name in ("ab_w_out", "c_w_out", "xa_wq", "xa_wkv", "ffn_w_down"):
        return g.reshape(N_CHIPS, -1, minor)
    if name in ("lru_wa", "lru_wx"):
        return g.reshape(LRU_HEADS, N_CHIPS, -1, minor).transpose(1, 0, 2, 3).reshape(N_CHIPS, -1, minor)
    assert name == "c_w_qkv"
    return g.reshape(D_MODEL, N_CHIPS, -1).transpose(1, 0, 2)


RS_SETS = {
    "l1": ((("c_w_out", None), ("xa_wkv", 1), ("ffn_w_down", 1)), (("ffn_w_gate_up", 1),), (("xa_wo", 1),),
           (("xa_wq", 1),), (("c_w_qkv", None),)),
    "l0a": ((("xa_wkv", 0), ("ffn_w_down", 0)), (("ffn_w_gate_up", 0),), (("xa_wo", 0),), (("xa_wq", 0),)),
    "l0b": ((("ab_w_out", None),), (("ab_w_in", None),), (("lru_wa", None), ("lru_wx", None))),
}
RS_STAGE = {"layer1": "l1", "layer0_ffn_xa": "l0a"}


def kernel(x, mem, mix_norm, ab_w_in, lru_conv_w, lru_conv_b, lru_wa, lru_ba, lru_wx, lru_bx, lru_lambda, ab_w_out, c_w_qkv, c_b_qkv, c_sinks, c_w_out, c_b_out, xa_norm, xa_mem_norm, xa_wq, xa_wkv, xa_wo, ffn_norm, ffn_w_gate_up, ffn_w_down, final_norm, loss_target, m_mix_norm, m_ab_w_in, m_lru_conv_w, m_lru_conv_b, m_lru_wa, m_lru_ba, m_lru_wx, m_lru_bx, m_lru_lambda, m_ab_w_out, m_c_w_qkv, m_c_b_qkv, m_c_sinks, m_c_w_out, m_c_b_out, m_xa_norm, m_xa_mem_norm, m_xa_wq, m_xa_wkv, m_xa_wo, m_ffn_norm, m_ffn_w_gate_up, m_ffn_w_down, m_final_norm, v_mix_norm, v_ab_w_in, v_lru_conv_w, v_lru_conv_b, v_lru_wa, v_lru_ba, v_lru_wx, v_lru_bx, v_lru_lambda, v_ab_w_out, v_c_w_qkv, v_c_b_qkv, v_c_sinks, v_c_w_out, v_c_b_out, v_xa_norm, v_xa_mem_norm, v_xa_wq, v_xa_wkv, v_xa_wo, v_ffn_norm, v_ffn_w_gate_up, v_ffn_w_down, v_final_norm):
    given = dict(locals())
    wl = {n: given[n] for n in WEIGHT_NAMES}
    ml = {n: given["m_" + n] for n in WEIGHT_NAMES}
    vl = {n: given["v_" + n] for n in WEIGHT_NAMES}
    xi, yi, ci = lax.axis_index("x"), lax.axis_index("y"), lax.axis_index("c")
    chip = 2 * xi + yi

    def join(parts, axis):
        return parts[0] if len(parts) == 1 else jnp.concatenate(parts, axis=axis)

    local_rows = {n: wl[n].size // wl[n].shape[-1] for grp in GROUPS for n in grp}
    packs = [join([wl[n].astype(BF16).reshape(local_rows[n], wl[n].shape[-1]) for n in grp], 0) for grp in GROUPS]
    spack = _pack_small([_rows(wl[n]) for n in SMALL_SHARDED], 8, name="pack_small_weights")
    n_early, n_mid = len(EARLY_GROUPS), len(EARLY_GROUPS) + len(MID_GROUPS)
    early, sfull = _gather_weights(packs[:n_early], spack)
    early, sfull, mid_packs = lax.optimization_barrier((early, sfull, packs[n_early:n_mid]))
    mid = _gather_weights_behind(mid_packs, tag="mid")
    mid, late_packs = lax.optimization_barrier((mid, packs[n_mid:]))
    gathered = early + mid + _gather_weights_behind(late_packs, tag="late")
    w = {n: wl[n] for n in REPLICATED}
    w["c_sinks"] = wl["c_sinks"][0]
    for grp, full in zip(GROUPS, gathered):
        off = 0
        for n in grp:
            w[n] = _from_shards(n, full if len(grp) == 1 else full[:, off:off + local_rows[n]])
            off += local_rows[n]
    for r, n in enumerate(SMALL_SHARDED):
        loc = wl[n].shape[1:]
        t = sfull[:, r, :wl[n].size].reshape((N_CHIPS,) + loc)
        if n == "lru_conv_w":
            w[n] = t.transpose(1, 0, 2).reshape(CONV_WIDTH, -1)
        elif n in ("lru_ba", "lru_bx"):
            w[n] = t.transpose(1, 0, 2).reshape(1, -1)
        else:
            w[n] = t.reshape(1, -1)

    place = jnp.stack([ci, chip]).astype(jnp.int32)

    def pair_stage(spec, g, tag):
        piece = lambda n, l: (g[n] if l is None else g[n, l]).astype(BF16)
        gpacks = [join([_piece_shards(n, piece(n, l)) for n, l in grp], 1) for grp in spec]
        ras = _rs_pair_exchange(gpacks, name=f"rs_pair_exchange_{tag}")
        sums = [_rs_pair_add(place, gp, ra, name=f"rs_pair_add_{tag}_{i}") for i, (gp, ra) in enumerate(zip(gpacks, ras))]
        return [pair for pair, _ in sums], [own for _, own in sums]

    owns, rbs = [], []

    def reduce_behind(stage, g, dh):
        tag = RS_STAGE[stage]
        pairs, own = pair_stage(RS_SETS[tag], g, tag)
        pairs, dh = lax.optimization_barrier((pairs, dh))
        owns.extend(own)
        rbs.extend(_rs_chip_exchange_behind(pairs, tag=tag))
        return dh

    loss_part, grad_x, g = _device_step(x[0], mem[0], loss_target[0], w, on_grads=reduce_behind)
    owns, rbs, grad_x = lax.optimization_barrier((owns, rbs, grad_x))

    small_parts = [_rows(g[n]) for n in REPLICATED] + [_rows(jnp.broadcast_to(loss_part, (LANES,)))]
    small_parts += [_rows(g[n]) for n in SMALL_SHARDED]
    small = _pack_small(small_parts, 24, name="pack_small_grads")
    pairs, own = pair_stage(RS_SETS["l0b"], g, "l0b")
    *rb, rs = _rs_chip_exchange_behind(pairs, tag="l0b", small=small)

    def finish(own_sums, received, first, name):
        return _rs_sibling_share([_rs_final_add(place, o, r, name=f"rs_final_add_{first + i}")
                                  for i, (o, r) in enumerate(zip(own_sums, received))], name=name)

    gsums = finish(owns, rbs, 0, "rs_sibling_share_behind") + finish(own, rb, len(owns), "rs_sibling_share_last")
    ssum = _sum_slots(rs)

    where = {}
    for grp, gsum in zip(RS_SETS["l1"] + RS_SETS["l0a"] + RS_SETS["l0b"], gsums):
        off = 0
        for n, l in grp:
            rows = local_rows[n] if l is None else local_rows[n] // 2
            where[n, l] = (gsum, off, rows, len(grp) == 1)
            off += rows
    take = lambda gsum, off, rows, whole: gsum if whole else gsum[off:off + rows]
    grads, grad_rows = {}, {}
    for grp in LATE_GROUPS + EARLY_GROUPS + MID_GROUPS:
        for n in grp:
            if (n, None) in where:
                grads[n] = take(*where[n, None]).reshape(wl[n].shape)
                grad_rows[n] = where[n, None][:2]
            else:
                grads[n] = jnp.stack([take(*where[n, l]).reshape(wl[n].shape[1:]) for l in range(2)])
                grad_rows[n] = (grads[n].reshape(local_rows[n], wl[n].shape[-1]), 0)
    row = 0
    for n in REPLICATED:
        k = _rows(g[n]).shape[0]
        grads[n] = ssum[row:row + k].reshape(-1)[:wl[n].size].reshape(wl[n].shape)
        row += k
    loss = ssum[row, 0]
    row += 1
    for n in SMALL_SHARDED:
        k = _rows(g[n]).shape[0]
        full = ssum[row:row + k].reshape(-1)[:g[n].size]
        row += k
        loc = wl[n].shape
        if n == "lru_conv_w":
            sh = full.reshape(CONV_WIDTH, N_CHIPS, -1)
        elif n in ("lru_ba", "lru_bx"):
            sh = full.reshape(LRU_HEADS, N_CHIPS, -1)
        else:
            sh = full.reshape(1, N_CHIPS, -1)
        grads[n] = lax.dynamic_index_in_dim(sh, chip, axis=1, keepdims=False).reshape(loc)

    delta, new_m, new_v = {}, {}, {}
    for n, (gsum, off) in grad_rows.items():
        shape2 = (local_rows[n], wl[n].shape[-1])
        d, nm, nv = _adamw(wl[n].reshape(shape2), gsum, ml[n].reshape(shape2), vl[n].reshape(shape2), g_row=off,
                           name="adamw_" + n)
        delta[n], new_m[n], new_v[n] = (t.reshape(wl[n].shape) for t in (d, nm, nv))
    smalls = REPLICATED + SMALL_SHARDED
    packs = [_pack_small([_rows(src[n]) for n in smalls], 24, name="pack_adamw_" + tag)
             for tag, src in (("w", wl), ("g", grads), ("m", ml), ("v", vl))]
    outs = _adamw(*packs, name="adamw_small")
    row = 0
    for n in smalls:
        k = _rows(wl[n]).shape[0]
        for dst, o in zip((delta, new_m, new_v), outs):
            dst[n] = o[row:row + k].reshape(-1)[:wl[n].size].reshape(wl[n].shape)
        row += k

    return (loss, grad_x[None], *[grads[n] for n in WEIGHT_NAMES], *[delta[n] for n in WEIGHT_NAMES],
            *[new_m[n] for n in WEIGHT_NAMES], *[new_v[n] for n in WEIGHT_NAMES])
```

```python
import jax
import jax.numpy as jnp
from jax import lax
from jax.experimental import pallas as pl
from jax.experimental.pallas import tpu as pltpu
from jax.experimental.pallas import tpu_sc as plsc

F32, BF16 = jnp.float32, jnp.bfloat16
D_MODEL = 1024
NORM_EPS = 1e-6
ROPE_THETA = 500000.0
HEAD_DIM = 64
ROT_DIM = 16
BLK = 128
LRU_HEADS, LRU_HEAD_DIM, CONV_WIDTH, LRU_C = 4, 256, 4, 8.0
DILATED_PATTERN = ((128, 1), (512, 4), (2048, 16))
B_HEADS, C_HEADS, C_KV_HEADS, C_WINDOW = 8, 16, 2, 128
XA_HEADS, XA_HEAD_DIM, N_MEM = 4, 128, 256
D_FF = 2816
NEG = -1e30
ADAM_LR, ADAM_B1, ADAM_B2, ADAM_EPS, ADAM_WD, ADAM_STEP = 0.001, 0.9, 0.999, 1e-08, 0.01, 10
N_CHIPS = 4
VMEM_LIMIT_V7X = 56 * 1024 * 1024

NN = (((1,), (0,)), ((), ()))
NT = (((1,), (1,)), ((), ()))
TN = (((0,), (0,)), ((), ()))


def _dot(a, b, dims=NN):
    return lax.dot_general(a, b, dims, preferred_element_type=F32)


def _sigmoid(x):
    return 1.0 / (1.0 + jnp.exp(-x))


def _call(body, *, name, grid, in_specs, out_specs, out_shape, scratch=(), sem=None):
    return pl.pallas_call(
        body, name=name, grid=grid, in_specs=in_specs, out_specs=out_specs, out_shape=out_shape,
        scratch_shapes=list(scratch),
        compiler_params=pltpu.CompilerParams(dimension_semantics=sem, vmem_limit_bytes=VMEM_LIMIT_V7X))


def _rope_tables(L):
    half = ROT_DIM // 2
    inv = ROPE_THETA ** (-jnp.arange(0, ROT_DIM, 2, dtype=F32) / ROT_DIM)
    ang = jnp.arange(L, dtype=F32)[:, None] * inv[None, :]
    cos, sin = jnp.cos(ang), jnp.sin(ang)
    rest = HEAD_DIM - ROT_DIM
    z8, zr, one = jnp.zeros((L, half), F32), jnp.zeros((L, rest), F32), jnp.ones((L, rest), F32)
    c = jnp.concatenate([cos, cos, one], axis=1)
    s1 = jnp.concatenate([-sin, z8, zr], axis=1)
    s2 = jnp.concatenate([z8, sin, zr], axis=1)
    return tuple(jnp.concatenate([t, t], axis=1) for t in (c, s1, s2))


def _rope_fwd(v, c, s1, s2):
    return v * c + pltpu.roll(v, 120, 1) * s1 + pltpu.roll(v, 8, 1) * s2


def _rope_bwd(dv, c, s1, s2):
    return dv * c + pltpu.roll(dv * s1, 8, 1) + pltpu.roll(dv * s2, 120, 1)


def _weight_spec(w, layer):
    once = pl.Buffered(1)
    if layer is None:
        return w.shape, pl.BlockSpec(w.shape, lambda i: (0, 0, 0), pipeline_mode=once)
    S, _, K, Ns = w.shape
    return (S, K, Ns), pl.BlockSpec((S, None, K, Ns), lambda i: (0, layer, 0, 0), pipeline_mode=once)


def _rowmm(a, w3, *, name, tm=512, gain=None, bias=None, res=None, swiglu=False, rope=None, layer=None):
    M, K = a.shape
    (S, _, Ns), w_spec = _weight_spec(w3, layer)
    N = S * Ns
    tm = min(tm, M)
    has_norm, has_bias, has_res, has_rope = gain is not None, bias is not None, res is not None, rope is not None
    row = lambda w: pl.BlockSpec((tm, w), lambda i: (i, 0))
    whole = lambda shape: pl.BlockSpec(shape, lambda i: (0,) * len(shape))
    ins, specs = [a], [row(K)]
    if has_norm:
        ins.append(gain.reshape(1, K)); specs.append(whole((1, K)))
    ins.append(w3); specs.append(w_spec)
    if has_bias:
        ins.append(bias.reshape(1, N)); specs.append(whole((1, N)))
    if has_res:
        ins.append(res); specs.append(row(N))
    if has_rope:
        ins += list(rope[2]); specs += [row(128)] * 3
    y_dtype = F32 if has_res else BF16
    out_shape, out_specs = [jax.ShapeDtypeStruct((M, N), y_dtype)], [row(N)]
    if has_norm:
        out_shape.append(jax.ShapeDtypeStruct((M, K), BF16)); out_specs.append(row(K))
    if swiglu:
        out_shape.append(jax.ShapeDtypeStruct((M, N // 2), BF16)); out_specs.append(row(N // 2))
    scratch = [pltpu.VMEM((tm, N), F32)] if has_rope else []

    def body(*refs):
        it = iter(refs)
        a_ref = next(it)
        g_ref = next(it) if has_norm else None
        w_ref = next(it)
        b_ref = next(it) if has_bias else None
        r_ref = next(it) if has_res else None
        tabs = [next(it) for _ in range(3)] if has_rope else None
        y_ref = next(it)
        n_ref = next(it) if has_norm else None
        act_ref = next(it) if swiglu else None
        ys_ref = next(it) if has_rope else None
        if has_norm:
            x = a_ref[...].astype(F32)
            ms = jnp.mean(x * x, axis=-1, keepdims=True)
            xb = (x * lax.rsqrt(ms + NORM_EPS) * g_ref[...]).astype(BF16)
            n_ref[...] = xb
        else:
            xb = a_ref[...].astype(BF16)
        if swiglu:
            for s in range(S // 2):
                g = _dot(xb, w_ref[s])
                u = _dot(xb, w_ref[s + S // 2])
                y_ref[:, s * Ns:(s + 1) * Ns] = g.astype(BF16)
                y_ref[:, N // 2 + s * Ns:N // 2 + (s + 1) * Ns] = u.astype(BF16)
                act_ref[:, s * Ns:(s + 1) * Ns] = (g * _sigmoid(g) * u).astype(BF16)
            return
        for s in range(S):
            sl = slice(s * Ns, (s + 1) * Ns)
            acc = _dot(xb, w_ref[s])
            if has_bias:
                acc = acc + b_ref[:, sl]
            if has_res:
                acc = acc + r_ref[:, sl]
            if has_rope:
                ys_ref[:, sl] = acc
            else:
                y_ref[:, sl] = acc.astype(y_dtype)
        if has_rope:
            c, s1, s2 = (t[...] for t in tabs)
            for cb in range(N // 128):
                cs = slice(cb * 128, (cb + 1) * 128)
                v = ys_ref[:, cs]
                if rope[0] <= cb * 128 < rope[1]:
                    v = _rope_fwd(v, c, s1, s2)
                y_ref[:, cs] = v.astype(BF16)

    return _call(body, name=name, grid=(M // tm,), in_specs=specs, out_specs=out_specs, out_shape=out_shape,
                 scratch=scratch, sem=("parallel",))(*ins)


def _mm_nt(dy, w3, *, name, mode, tm=512, kchunk=None, h=None, gain=None, dh=None, gu=None, layer=None):
    M, N = dy.shape
    (S, K, Ns), w_spec = _weight_spec(w3, layer)
    kchunk = kchunk or K
    tm = min(tm, M)
    row = lambda w: pl.BlockSpec((tm, w), lambda i: (i, 0))
    whole = lambda shape: pl.BlockSpec(shape, lambda i: (0,) * len(shape))
    ins, specs = [dy, w3], [row(N), w_spec]
    has_dh = dh is not None
    if mode == "norm":
        ins += [h, gain.reshape(1, K)]; specs += [row(K), whole((1, K))]
        if has_dh:
            ins.append(dh); specs.append(row(K))
        out_shape = [jax.ShapeDtypeStruct((M, K), F32), jax.ShapeDtypeStruct((1, K), F32)]
        out_specs = [row(K), whole((1, K))]
    elif mode == "swiglu":
        ins.append(gu); specs.append(row(2 * K))
        out_shape, out_specs = [jax.ShapeDtypeStruct((M, 2 * K), BF16)], [row(2 * K)]
    else:
        out_shape, out_specs = [jax.ShapeDtypeStruct((M, K), BF16)], [row(K)]

    def body(*refs):
        it = iter(refs)
        dy_ref, w_ref = next(it), next(it)
        if mode == "norm":
            h_ref, g_ref = next(it), next(it)
            dh_ref = next(it) if has_dh else None
            o_ref, dg_ref = next(it), next(it)
        elif mode == "swiglu":
            gu_ref, o_ref = next(it), next(it)
        else:
            o_ref = next(it)
        for kc in range(K // kchunk):
            ks = slice(kc * kchunk, (kc + 1) * kchunk)
            acc = None
            for s in range(S):
                t = _dot(dy_ref[:, s * Ns:(s + 1) * Ns].astype(BF16), w_ref[s, ks, :], NT)
                acc = t if acc is None else acc + t
            if mode == "plain":
                o_ref[:, ks] = acc.astype(BF16)
            elif mode == "swiglu":
                us = slice(K + kc * kchunk, K + (kc + 1) * kchunk)
                g = gu_ref[:, ks].astype(F32)
                u = gu_ref[:, us].astype(F32)
                sg = _sigmoid(g)
                o_ref[:, ks] = (acc * u * (sg * (1.0 + g * (1.0 - sg)))).astype(BF16)
                o_ref[:, us] = (acc * (g * sg)).astype(BF16)
            else:
                x = h_ref[...].astype(F32)
                r = lax.rsqrt(jnp.mean(x * x, axis=-1, keepdims=True) + NORM_EPS)
                xhat = x * r
                dxh = acc * g_ref[...]
                dx = r * (dxh - xhat * jnp.mean(dxh * xhat, axis=-1, keepdims=True))
                o_ref[...] = dx + dh_ref[...] if has_dh else dx

                @pl.when(pl.program_id(0) == 0)
                def _():
                    dg_ref[...] = jnp.zeros_like(dg_ref)

                dg_ref[...] += jnp.sum(acc * xhat, axis=0, keepdims=True)

    sem = ("arbitrary",) if mode == "norm" else ("parallel",)
    return _call(body, name=name, grid=(M // tm,), in_specs=specs, out_specs=out_specs, out_shape=out_shape, sem=sem)(*ins)


def _mm_tn(x, dy, *, S, name, tk=1024, kk=None, bias=False):
    M, K = x.shape
    N = dy.shape[1]
    Ns = N // S
    kk = kk or K
    tk = min(tk, M)
    nl = M // tk
    in_specs = [pl.BlockSpec((tk, kk), lambda s, kc, l: (l, kc)), pl.BlockSpec((tk, Ns), lambda s, kc, l: (l, s))]
    out_shape = [jax.ShapeDtypeStruct((S, K, Ns), BF16)]
    out_specs = [pl.BlockSpec((None, kk, Ns), lambda s, kc, l: (s, kc, 0))]
    if bias:
        out_shape.append(jax.ShapeDtypeStruct((1, N), F32))
        out_specs.append(pl.BlockSpec((1, Ns), lambda s, kc, l: (0, s)))

    def body(x_ref, dy_ref, o_ref, *rest):
        acc_ref = rest[-1]
        kc, l = pl.program_id(1), pl.program_id(2)

        @pl.when(l == 0)
        def _():
            acc_ref[...] = jnp.zeros_like(acc_ref)

        acc_ref[...] += _dot(x_ref[...].astype(BF16), dy_ref[...].astype(BF16), TN)
        if bias:
            b_ref = rest[0]

            @pl.when((kc == 0) & (l == 0))
            def _():
                b_ref[...] = jnp.zeros_like(b_ref)

            @pl.when(kc == 0)
            def _():
                b_ref[...] += jnp.sum(dy_ref[...].astype(F32), axis=0, keepdims=True)

        @pl.when(l == nl - 1)
        def _():
            o_ref[...] = acc_ref[...].astype(BF16)

    return _call(body, name=name, grid=(S, K // kk, nl), in_specs=in_specs, out_specs=out_specs, out_shape=out_shape,
                 scratch=[pltpu.VMEM((kk, Ns), F32)], sem=("arbitrary", "arbitrary", "arbitrary"))(x, dy)


def _band_bias(max_dist, has_prev):
    rows = lax.broadcasted_iota(jnp.int32, (BLK, 2 * BLK), 0)
    cols = lax.broadcasted_iota(jnp.int32, (BLK, 2 * BLK), 1)
    dist = rows - cols + BLK
    ok = (dist >= 0) & (dist <= max_dist) & ((cols >= BLK) | has_prev)
    return jnp.where(ok, 0.0, NEG)


Q_SCALE = HEAD_DIM ** -0.5


def _band_fwd(qa, ka, va, *, d, nq, nkv, qcol, kcol, vcol, max_dist, sinks=None, name):
    Lr = qa.shape[0]
    nb = Lr // BLK
    qw, kw, G = nq * HEAD_DIM, nkv * HEAD_DIM, nq // nkv
    cur = lambda colf, w: pl.BlockSpec((BLK, w), lambda r, i: (i, colf(r)))
    prv = lambda colf, w: pl.BlockSpec((BLK, w), lambda r, i: (jnp.maximum(i - 1, 0), colf(r)))
    out = pl.BlockSpec((BLK, qw), lambda r, i: (i, r))
    ins, specs = [qa, ka, ka, va, va], [cur(qcol, qw), cur(kcol, kw), prv(kcol, kw), cur(vcol, kw), prv(vcol, kw)]
    has_sinks = sinks is not None
    if has_sinks:
        ins.append(sinks); specs.append(pl.BlockSpec(memory_space=pltpu.SMEM))

    def body(*refs):
        q_ref, kc_ref, kp_ref, vc_ref, vp_ref = refs[:5]
        sk_ref = refs[5] if has_sinks else None
        o_ref, lse_ref = refs[-2], refs[-1]
        bias = _band_bias(max_dist, pl.program_id(1) > 0)
        k2 = jnp.concatenate([kp_ref[...], kc_ref[...]], axis=0)
        v2 = jnp.concatenate([vp_ref[...], vc_ref[...]], axis=0)
        for h in range(nq):
            hs = slice(h * HEAD_DIM, (h + 1) * HEAD_DIM)
            ks = slice((h // G) * HEAD_DIM, (h // G + 1) * HEAD_DIM)
            s = _dot(q_ref[:, hs] * jnp.asarray(Q_SCALE, BF16), k2[:, ks], NT) + bias
            m = jnp.max(s, axis=-1, keepdims=True)
            if has_sinks:
                m = jnp.maximum(m, sk_ref[h])
            p = jnp.exp(s - m)
            l = jnp.sum(p, axis=-1, keepdims=True)
            if has_sinks:
                l = l + jnp.exp(sk_ref[h] - m)
            o_ref[:, hs] = (_dot(p.astype(BF16), v2[:, ks]) / l).astype(BF16)
            lse_ref[:, hs] = jnp.broadcast_to(m + jnp.log(l), (BLK, HEAD_DIM))

    return _call(body, name=name, grid=(d, nb), in_specs=specs, out_specs=[out, out],
                 out_shape=[jax.ShapeDtypeStruct((Lr, d * qw), BF16), jax.ShapeDtypeStruct((Lr, d * qw), F32)],
                 sem=("parallel", "parallel"))(*ins)


def _band_bwd(qa, ka, va, doa, oa, lsea, *, d, nq, nkv, qcol, kcol, vcol, docol, max_dist, sinks=None, name):
    Lr = qa.shape[0]
    nb = Lr // BLK
    qw, kw, G = nq * HEAD_DIM, nkv * HEAD_DIM, nq // nkv
    transposed = G > 1
    last = lambda i: jnp.minimum(i, nb - 1)
    cur = lambda colf, w: pl.BlockSpec((BLK, w), lambda r, i: (last(i), colf(r)))
    prv = lambda colf, w: pl.BlockSpec((BLK, w), lambda r, i: (jnp.maximum(last(i) - 1, 0), colf(r)))
    own = lambda r: r
    ins = [qa, ka, ka, va, va, doa, oa, lsea]
    specs = [cur(qcol, qw), cur(kcol, kw), prv(kcol, kw), cur(vcol, kw), prv(vcol, kw), cur(docol, qw), cur(own, qw),
             cur(own, qw)]
    has_sinks = sinks is not None
    if has_sinks:
        ins.append(sinks); specs.append(pl.BlockSpec(memory_space=pltpu.SMEM))
    out_shape = [jax.ShapeDtypeStruct((Lr, d * qw), F32), jax.ShapeDtypeStruct((Lr, d * kw), F32),
                 jax.ShapeDtypeStruct((Lr, d * kw), F32)]
    behind = lambda r, i: (jnp.maximum(i - 1, 0), r)
    out_specs = [pl.BlockSpec((BLK, qw), lambda r, i: (last(i), r)), pl.BlockSpec((BLK, kw), behind),
                 pl.BlockSpec((BLK, kw), behind)]
    if has_sinks:
        out_shape.append(jax.ShapeDtypeStruct((8, 128), F32))
        out_specs.append(pl.BlockSpec((8, 128), lambda r, i: (0, 0)))

    def body(*refs):
        it = iter(refs)
        q_ref, kc_ref, kp_ref, vc_ref, vp_ref, do_ref, o_ref, ls_ref = (next(it) for _ in range(8))
        sk_ref = next(it) if has_sinks else None
        dq_ref, dk_ref, dv_ref = next(it), next(it), next(it)
        dsk_ref = next(it) if has_sinks else None
        dk_car, dv_car = next(it), next(it)
        r_id, i = pl.program_id(0), pl.program_id(1)

        @pl.when(i == 0)
        def _():
            dk_car[...] = jnp.zeros_like(dk_car)
            dv_car[...] = jnp.zeros_like(dv_car)

        if has_sinks:
            @pl.when((r_id == 0) & (i == 0))
            def _():
                dsk_ref[...] = jnp.zeros_like(dsk_ref)

        @pl.when(i == nb)
        def _():
            dk_ref[...] = dk_car[...]
            dv_ref[...] = dv_car[...]

        @pl.when(i < nb)
        def _():
            bias = _band_bias(max_dist, i > 0)
            k2 = jnp.concatenate([kp_ref[...], kc_ref[...]], axis=0)
            v2 = jnp.concatenate([vp_ref[...], vc_ref[...]], axis=0)
            if has_sinks:
                lane = lax.broadcasted_iota(jnp.int32, (8, 128), 1)
                dsk = jnp.zeros((8, 128), F32)
            for kv in range(nkv):
                ks = slice(kv * HEAD_DIM, (kv + 1) * HEAD_DIM)
                kh, vh = k2[:, ks], v2[:, ks]
                shape = (HEAD_DIM, 2 * BLK) if transposed else (2 * BLK, HEAD_DIM)
                dk, dv = jnp.zeros(shape, F32), jnp.zeros(shape, F32)
                for g in range(G):
                    h = kv * G + g
                    hs = slice(h * HEAD_DIM, (h + 1) * HEAD_DIM)
                    q = q_ref[:, hs] * jnp.asarray(Q_SCALE, BF16)
                    do = do_ref[:, hs]
                    lse = ls_ref[:, h * HEAD_DIM:h * HEAD_DIM + 1]
                    dl = jnp.sum(do.astype(F32) * o_ref[:, hs].astype(F32), axis=-1, keepdims=True)
                    p = jnp.exp(_dot(q, kh, NT) + bias - lse)
                    ds = (p * (_dot(do, vh, NT) - dl)).astype(BF16)
                    dq_ref[:, hs] = _dot(ds, kh) * Q_SCALE
                    if transposed:
                        dk = dk + _dot(q, ds, TN)
                        dv = dv + _dot(do, p.astype(BF16), TN)
                    else:
                        dk = dk + _dot(ds, q, TN)
                        dv = dv + _dot(p.astype(BF16), do, TN)
                    if has_sinks:
                        val = -jnp.sum(jnp.exp(sk_ref[h] - lse) * dl, axis=0, keepdims=True)
                        dsk = dsk + jnp.where(lane == h, val, 0.0)
                if transposed:
                    dk, dv = dk.T, dv.T
                dk_ref[:, ks] = dk_car[:, ks] + dk[:BLK]
                dv_ref[:, ks] = dv_car[:, ks] + dv[:BLK]
                dk_car[:, ks] = dk[BLK:]
                dv_car[:, ks] = dv[BLK:]
            if has_sinks:
                dsk_ref[...] += dsk

    return _call(body, name=name, grid=(d, nb + 1), in_specs=specs, out_specs=out_specs, out_shape=out_shape,
                 scratch=[pltpu.VMEM((BLK, kw), F32), pltpu.VMEM((BLK, kw), F32)], sem=("arbitrary", "arbitrary"))(*ins)


def _attn_grad_combine(branches, tabs, *, name, tm=256):
    L, qw = branches[0][0].shape
    kw = branches[0][1].shape[1]
    nbr = len(branches)
    row = lambda w: pl.BlockSpec((tm, w), lambda i: (i, 0))
    ins, specs = [], []
    for dq, dk, dv in branches:
        ins += [dq, dk, dv]; specs += [row(qw), row(kw), row(kw)]
    ins += list(tabs); specs += [row(128)] * 3

    def body(*refs):
        c, s1, s2 = (t[...] for t in refs[3 * nbr:3 * nbr + 3])
        o_ref = refs[-1]
        for part, (w, off, rot) in enumerate(((qw, 0, True), (kw, qw, True), (kw, qw + kw, False))):
            for cb in range(w // 128):
                cs = slice(cb * 128, (cb + 1) * 128)
                v = refs[part][:, cs]
                for b in range(1, nbr):
                    v = v + refs[3 * b + part][:, cs]
                if rot:
                    v = _rope_bwd(v, c, s1, s2)
                o_ref[:, off + cb * 128:off + (cb + 1) * 128] = v.astype(BF16)

    return _call(body, name=name, grid=(L // tm,), in_specs=specs, out_specs=row(qw + 2 * kw),
                 out_shape=jax.ShapeDtypeStruct((L, qw + 2 * kw), BF16), sem=("parallel",))(*ins)


def _xattn_fwd(q, kv, *, name, tq=512):
    L, W = q.shape
    scale = XA_HEAD_DIM ** -0.5
    row = pl.BlockSpec((tq, W), lambda i: (i, 0))
    kvs = pl.BlockSpec((N_MEM, 2 * W), lambda i: (0, 0))

    def body(q_ref, kv_ref, o_ref, lse_ref):
        for h in range(XA_HEADS):
            hs = slice(h * XA_HEAD_DIM, (h + 1) * XA_HEAD_DIM)
            vs = slice(W + h * XA_HEAD_DIM, W + (h + 1) * XA_HEAD_DIM)
            s = _dot(q_ref[:, hs], kv_ref[:, hs], NT) * scale
            m = jnp.max(s, axis=-1, keepdims=True)
            p = jnp.exp(s - m)
            l = jnp.sum(p, axis=-1, keepdims=True)
            o_ref[:, hs] = (_dot(p.astype(BF16), kv_ref[:, vs]) / l).astype(BF16)
            lse_ref[:, hs] = jnp.broadcast_to(m + jnp.log(l), (tq, XA_HEAD_DIM))

    return _call(body, name=name, grid=(L // tq,), in_specs=[row, kvs], out_specs=[row, row],
                 out_shape=[jax.ShapeDtypeStruct((L, W), BF16), jax.ShapeDtypeStruct((L, W), F32)], sem=("parallel",))(q, kv)


def _xattn_bwd(q, kv, o, lse, do, *, name, tq=512):
    L, W = q.shape
    scale = XA_HEAD_DIM ** -0.5
    row = pl.BlockSpec((tq, W), lambda i: (i, 0))
    kvs = pl.BlockSpec((N_MEM, 2 * W), lambda i: (0, 0))

    def body(q_ref, kv_ref, o_ref, lse_ref, do_ref, dq_ref, dkv_ref):
        @pl.when(pl.program_id(0) == 0)
        def _():
            dkv_ref[...] = jnp.zeros_like(dkv_ref)

        for h in range(XA_HEADS):
            hs = slice(h * XA_HEAD_DIM, (h + 1) * XA_HEAD_DIM)
            vs = slice(W + h * XA_HEAD_DIM, W + (h + 1) * XA_HEAD_DIM)
            qh, kh, vh, doh = q_ref[:, hs], kv_ref[:, hs], kv_ref[:, vs], do_ref[:, hs]
            p = jnp.exp(_dot(qh, kh, NT) * scale - lse_ref[:, h * XA_HEAD_DIM:h * XA_HEAD_DIM + 1])
            dl = jnp.sum(doh.astype(F32) * o_ref[:, hs].astype(F32), axis=-1, keepdims=True)
            ds = (p * (_dot(doh, vh, NT) - dl) * scale).astype(BF16)
            dq_ref[:, hs] = _dot(ds, kh).astype(BF16)
            dkv_ref[:, hs] += _dot(ds, qh, TN)
            dkv_ref[:, vs] += _dot(p.astype(BF16), doh, TN)

    return _call(body, name=name, grid=(L // tq,), in_specs=[row, kvs, row, row, row], out_specs=[row, kvs],
                 out_shape=[jax.ShapeDtypeStruct((L, W), BF16), jax.ShapeDtypeStruct((N_MEM, 2 * W), F32)],
                 sem=("arbitrary",))(q, kv, o, lse, do)


def _neg_expm1(z):
    series = -(z * (1.0 + z * (0.5 + z * (1.0 / 6.0 + z * (1.0 / 24.0 + z * (1.0 / 120.0))))))
    return jnp.where(z > -0.05, series, 1.0 - jnp.exp(z))


def _softplus(z):
    return jnp.maximum(z, 0.0) + jnp.log(1.0 + jnp.exp(-jnp.abs(z)))


def _gelu_parts(y):
    c = 0.7978845608028654
    t = jnp.tanh(c * (y + 0.044715 * y * y * y))
    gy = 0.5 * y * (1.0 + t)
    dgy = 0.5 * (1.0 + t) + 0.5 * y * (1.0 - t * t) * c * (1.0 + 3.0 * 0.044715 * y * y)
    return gy, dgy


def _lru_gates(xc, wa_ref, ba, wx_ref, bx, sp):
    rs, igs = [], []
    for hd in range(LRU_HEADS):
        sl = slice(hd * LRU_HEAD_DIM, (hd + 1) * LRU_HEAD_DIM)
        xh = xc[:, sl].astype(BF16)
        rs.append(_sigmoid(_dot(xh, wa_ref[hd]) + ba[:, sl]))
        igs.append(_sigmoid(_dot(xh, wx_ref[hd]) + bx[:, sl]))
    r, ig = jnp.concatenate(rs, axis=1), jnp.concatenate(igs, axis=1)
    la = -LRU_C * r * sp
    return r, ig, jnp.exp(la), _neg_expm1(2.0 * la)


def _conv_taps(x_ext, halo):
    n = x_ext.shape[0]
    return [x_ext[halo:] if k == CONV_WIDTH - 1 else pltpu.roll(x_ext, CONV_WIDTH - 1 - k, 0)[halo:]
            for k in range(CONV_WIDTH)]


def _lru_fwd(proj, cw, cb, wa, ba, wx, bx, lam, *, name, tc=512):
    L = proj.shape[0]
    W = LRU_HEADS * LRU_HEAD_DIM
    nb = L // tc
    whole = lambda shape: pl.BlockSpec(shape, lambda i: (0,) * len(shape))
    specs = [pl.BlockSpec((tc, W), lambda i: (i, 0)), pl.BlockSpec((tc, W), lambda i: (i, 1)),
             pl.BlockSpec((16, W), lambda i: (jnp.maximum(i * (tc // 16) - 1, 0), 0)),
             whole((CONV_WIDTH, W)), whole((1, W)), whole((LRU_HEADS, LRU_HEAD_DIM, LRU_HEAD_DIM)), whole((1, W)),
             whole((LRU_HEADS, LRU_HEAD_DIM, LRU_HEAD_DIM)), whole((1, W)), whole((1, W))]
    out_specs = [pl.BlockSpec((tc, W), lambda i: (i, 0))] * 2
    out_shape = [jax.ShapeDtypeStruct((L, W), BF16), jax.ShapeDtypeStruct((L, W), F32)]

    def body(x_ref, y_ref, xh_ref, cw_ref, cb_ref, wa_ref, ba_ref, wx_ref, bx_ref, lam_ref, rec_ref, hs_ref,
             hcar, a_scr, b_scr):
        i = pl.program_id(0)

        @pl.when(i == 0)
        def _():
            hcar[...] = jnp.zeros_like(hcar)

        halo = jnp.where(i > 0, xh_ref[...].astype(F32), 0.0)
        taps = _conv_taps(jnp.concatenate([halo, x_ref[...].astype(F32)], axis=0), 16)
        xc = cb_ref[...] + sum(cw_ref[k:k + 1, :] * taps[k] for k in range(CONV_WIDTH))
        _, ig, a, om = _lru_gates(xc, wa_ref, ba_ref[...], wx_ref, bx_ref[...], _softplus(-lam_ref[...]))
        b = jnp.sqrt(om) * (ig * xc)
        rowmod = lax.broadcasted_iota(jnp.int32, (tc, W), 0) & 7
        for s in (1, 2, 4):
            keep = rowmod >= s
            b = jnp.where(keep, a * pltpu.roll(b, s, 0) + b, b)
            a = jnp.where(keep, a * pltpu.roll(a, s, 0), a)
        a_scr[...] = a
        b_scr[...] = b

        def tile(j, hc):
            rows = pl.ds(pl.multiple_of(j * 8, 8), 8)
            ht = a_scr[rows, :] * hc + b_scr[rows, :]
            hs_ref[rows, :] = ht
            return jnp.broadcast_to(ht[7:8, :], (8, W))

        hcar[...] = lax.fori_loop(0, tc // 8, tile, hcar[...])
        gy, _ = _gelu_parts(y_ref[...].astype(F32))
        rec_ref[...] = (hs_ref[...] * gy).astype(BF16)

    return _call(body, name=name, grid=(nb,), in_specs=specs, out_specs=out_specs, out_shape=out_shape,
                 scratch=[pltpu.VMEM((8, W), F32), pltpu.VMEM((tc, W), F32), pltpu.VMEM((tc, W), F32)],
                 sem=("arbitrary",))(proj, proj, proj, cw, cb, wa, ba, wx, bx, lam)


def _lru_bwd(proj, hs, drec_src, cw, cb, wa, ba, wx, bx, lam, *, name, tc=256):
    L = proj.shape[0]
    W = LRU_HEADS * LRU_HEAD_DIM
    nb = L // tc
    tb = lambda i: nb - 1 - i
    whole = lambda shape: pl.BlockSpec(shape, lambda i: (0,) * len(shape))
    gate_w = (LRU_HEADS, LRU_HEAD_DIM, LRU_HEAD_DIM)
    specs = [pl.BlockSpec((tc, W), lambda i: (tb(i), 0)), pl.BlockSpec((tc, W), lambda i: (tb(i), 1)),
             pl.BlockSpec((16, W), lambda i: (jnp.maximum(tb(i) * (tc // 16) - 1, 0), 0)),
             pl.BlockSpec((tc, W), lambda i: (tb(i), 0)),
             pl.BlockSpec((8, W), lambda i: (jnp.maximum(tb(i) * (tc // 8) - 1, 0), 0)),
             pl.BlockSpec((tc, W), lambda i: (tb(i), 0)),
             whole((CONV_WIDTH, W)), whole((1, W)), whole(gate_w), whole((1, W)), whole(gate_w), whole((1, W)), whole((1, W))]
    out_specs = [pl.BlockSpec((tc, 2 * W), lambda i: (tb(i), 0)), whole((CONV_WIDTH, W)), whole((1, W)), whole(gate_w),
                 whole((1, W)), whole(gate_w), whole((1, W)), whole((1, W))]
    vec = jax.ShapeDtypeStruct((1, W), F32)
    out_shape = [jax.ShapeDtypeStruct((L, 2 * W), BF16), jax.ShapeDtypeStruct((CONV_WIDTH, W), F32), vec,
                 jax.ShapeDtypeStruct(gate_w, F32), vec, jax.ShapeDtypeStruct(gate_w, F32), vec, vec]

    def body(x_ref, y_ref, xh_ref, hs_ref, hh_ref, dr_ref, cw_ref, cb_ref, wa_ref, ba_ref, wx_ref, bx_ref, lam_ref,
             dxy_ref, dcw_ref, dcb_ref, dwa_ref, dba_ref, dwx_ref, dbx_ref, dlam_ref, gcar, dxc_car, a_scr, b_scr, g_scr):
        pid = pl.program_id(0)
        t = tb(pid)
        accs = (dcw_ref, dcb_ref, dwa_ref, dba_ref, dwx_ref, dbx_ref, dlam_ref)

        @pl.when(pid == 0)
        def _():
            gcar[...] = jnp.zeros_like(gcar)
            dxc_car[...] = jnp.zeros_like(dxc_car)
            for r in accs:
                r[...] = jnp.zeros_like(r)

        halo = jnp.where(t > 0, xh_ref[...].astype(F32), 0.0)
        taps = _conv_taps(jnp.concatenate([halo, x_ref[...].astype(F32)], axis=0), 16)
        xc = cb_ref[...] + sum(cw_ref[k:k + 1, :] * taps[k] for k in range(CONV_WIDTH))
        lam = lam_ref[...]
        sp = _softplus(-lam)
        r, ig, a, om = _lru_gates(xc, wa_ref, ba_ref[...], wx_ref, bx_ref[...], sp)
        sq = jnp.sqrt(om)
        hblk = hs_ref[...]
        hprev = pltpu.roll(jnp.concatenate([jnp.where(t > 0, hh_ref[...], 0.0), hblk], axis=0), 1, 0)[8:]
        gy, dgy = _gelu_parts(y_ref[...].astype(F32))
        drec = dr_ref[...].astype(F32)
        dxy_ref[:, W:] = (drec * hblk * dgy).astype(BF16)

        rowidx = lax.broadcasted_iota(jnp.int32, (tc, W), 0)
        rowmod = rowidx & 7
        ca = jnp.where(rowidx == tc - 1, 1.0, pltpu.roll(a, tc - 1, 0))
        cbv = drec * gy
        for s in (1, 2, 4):
            keep = rowmod < 8 - s
            cbv = jnp.where(keep, ca * pltpu.roll(cbv, tc - s, 0) + cbv, cbv)
            ca = jnp.where(keep, ca * pltpu.roll(ca, tc - s, 0), ca)
        a_scr[...] = ca
        b_scr[...] = cbv

        def tile(k, gc):
            j = tc // 8 - 1 - k
            rows = pl.ds(pl.multiple_of(j * 8, 8), 8)
            gt = a_scr[rows, :] * gc + b_scr[rows, :]
            g_scr[rows, :] = gt
            return jnp.broadcast_to(gt[0:1, :], (8, W))

        lax.fori_loop(0, tc // 8, tile, gcar[...])
        G = g_scr[...]
        gcar[...] = jnp.broadcast_to(a[0:1, :] * G[0:1, :], (8, W))

        da = G * hprev
        dsq = G * (ig * xc)
        di = G * (sq * xc)
        dxc = G * (sq * ig)
        dla = da * a - 2.0 * a * a * (dsq * 0.5 * lax.rsqrt(om))
        dlam_ref[...] += jnp.sum(dla * (-LRU_C * r), axis=0, keepdims=True) * (-_sigmoid(-lam))
        dpr = dla * (-LRU_C * sp) * r * (1.0 - r)
        dpi = di * ig * (1.0 - ig)
        dba_ref[...] += jnp.sum(dpr, axis=0, keepdims=True)
        dbx_ref[...] += jnp.sum(dpi, axis=0, keepdims=True)
        back = []
        for hd in range(LRU_HEADS):
            sl = slice(hd * LRU_HEAD_DIM, (hd + 1) * LRU_HEAD_DIM)
            xh, dprh, dpih = xc[:, sl].astype(BF16), dpr[:, sl].astype(BF16), dpi[:, sl].astype(BF16)
            back.append(_dot(dprh, wa_ref[hd], NT) + _dot(dpih, wx_ref[hd], NT))
            dwa_ref[hd] += _dot(xh, dprh, TN)
            dwx_ref[hd] += _dot(xh, dpih, TN)
        dxc = dxc + jnp.concatenate(back, axis=1)
        dcb_ref[...] += jnp.sum(dxc, axis=0, keepdims=True)
        for k in range(CONV_WIDTH):
            dcw_ref[k:k + 1, :] += jnp.sum(dxc * taps[k], axis=0, keepdims=True)
        ext = jnp.concatenate([dxc, dxc_car[...]], axis=0)
        dx = cw_ref[CONV_WIDTH - 1:CONV_WIDTH, :] * dxc
        for k in range(CONV_WIDTH - 1):
            dx = dx + cw_ref[k:k + 1, :] * pltpu.roll(ext, tc + 8 - (CONV_WIDTH - 1 - k), 0)[:tc]
        dxc_car[...] = dxc[0:8, :]
        dxy_ref[:, :W] = dx.astype(BF16)

    scratch = [pltpu.VMEM((8, W), F32), pltpu.VMEM((8, W), F32)] + [pltpu.VMEM((tc, W), F32)] * 3
    return _call(body, name=name, grid=(nb,), in_specs=specs, out_specs=out_specs, out_shape=out_shape, scratch=scratch,
                 sem=("arbitrary",))(proj, proj, proj, hs, hs, drec_src, cw, cb, wa, ba, wx, bx, lam)


def _final_loss(h, gain, target, *, name, tm=256):
    M, K = h.shape
    row = pl.BlockSpec((tm, K), lambda i: (i, 0))
    vec = pl.BlockSpec((1, K), lambda i: (0, 0))
    one = pl.BlockSpec((1, 128), lambda i: (0, 0))

    def body(h_ref, g_ref, t_ref, dh_ref, dg_ref, loss_ref):
        @pl.when(pl.program_id(0) == 0)
        def _():
            dg_ref[...] = jnp.zeros_like(dg_ref)
            loss_ref[...] = jnp.zeros_like(loss_ref)

        x = h_ref[...]
        r = lax.rsqrt(jnp.mean(x * x, axis=-1, keepdims=True) + NORM_EPS)
        xhat = x * r
        err = xhat * g_ref[...] - t_ref[...]
        loss_ref[...] += 0.5 / K * jnp.sum(err * err)
        dy = err * (1.0 / K)
        dg_ref[...] += jnp.sum(dy * xhat, axis=0, keepdims=True)
        dxh = dy * g_ref[...]
        dh_ref[...] = r * (dxh - xhat * jnp.mean(dxh * xhat, axis=-1, keepdims=True))

    return _call(body, name=name, grid=(M // tm,), in_specs=[row, vec, row], out_specs=[row, vec, one],
                 out_shape=[jax.ShapeDtypeStruct((M, K), F32), jax.ShapeDtypeStruct((1, K), F32),
                            jax.ShapeDtypeStruct((1, 128), F32)], sem=("arbitrary",))(h, gain.reshape(1, K), target)


def _dilated_merge(branches, *, name, tm=512):
    L, W = branches[0].shape
    nbr = len(branches) // 2
    row = pl.BlockSpec((tm, W), lambda i: (i, 0))

    def body(*refs):
        o_ref, lse_ref = refs[-2], refs[-1]
        lses = [refs[2 * b + 1][...] for b in range(nbr)]
        m = lses[0]
        for t in lses[1:]:
            m = jnp.maximum(m, t)
        ws = [jnp.exp(t - m) for t in lses]
        den = ws[0]
        for t in ws[1:]:
            den = den + t
        acc = ws[0] * refs[0][...].astype(F32)
        for b in range(1, nbr):
            acc = acc + ws[b] * refs[2 * b][...].astype(F32)
        o_ref[...] = (acc / den).astype(BF16)
        lse_ref[...] = m + jnp.log(den)

    return _call(body, name=name, grid=(L // tm,), in_specs=[row] * (2 * nbr), out_specs=[row, row],
                 out_shape=[jax.ShapeDtypeStruct((L, W), BF16), jax.ShapeDtypeStruct((L, W), F32)], sem=("parallel",))(*branches)


def _dilated_fwd(proj0):
    L = proj0.shape[0]
    qkv = proj0[:, 2 * D_MODEL:]
    W = B_HEADS * HEAD_DIM
    outs = []
    for window, d in DILATED_PATTERN:
        view = qkv.reshape(L // d, d * 3 * W)
        o, lse = _band_fwd(view, view, view, d=d, nq=B_HEADS, nkv=B_HEADS, qcol=lambda r: 3 * r, kcol=lambda r: 3 * r + 1,
                           vcol=lambda r: 3 * r + 2, max_dist=window // d, name=f"dilated_fwd_d{d}")
        outs += [o.reshape(L, W), lse.reshape(L, W)]
    return _dilated_merge(outs, name="dilated_merge")


def _dilated_bwd(proj0, att, lse, datt, tabs):
    L = proj0.shape[0]
    qkv = proj0[:, 2 * D_MODEL:]
    Wh = B_HEADS * HEAD_DIM
    branches = []
    for window, d in DILATED_PATTERN:
        view = qkv.reshape(L // d, d * 3 * Wh)
        v1 = lambda t: t.reshape(L // d, d * Wh)
        outs = _band_bwd(view, view, view, v1(datt), v1(att), v1(lse), d=d, nq=B_HEADS, nkv=B_HEADS,
                         qcol=lambda r: 3 * r, kcol=lambda r: 3 * r + 1, vcol=lambda r: 3 * r + 2, docol=lambda r: r,
                         max_dist=window // d, name=f"dilated_bwd_d{d}")
        branches.append([o.reshape(L, Wh) for o in outs])
    return _attn_grad_combine(branches, tabs, name="dilated_grad_combine")


def _device_step(x, mem, target, w, on_grads=None):
    L = x.shape[0]
    tabs = _rope_tables(L)
    g = {}
    saved = []
    h = x
    for layer in range(2):
        sv = {"h_mix": h}
        if layer == 0:
            proj, n = _rowmm(h, w["ab_w_in"], name="l0_in_proj", gain=w["mix_norm"][0],
                             rope=(2 * D_MODEL, 2 * D_MODEL + 2 * B_HEADS * HEAD_DIM, tabs))
            rec, hs = _lru_fwd(proj, w["lru_conv_w"], w["lru_conv_b"], w["lru_wa"], w["lru_ba"], w["lru_wx"], w["lru_bx"],
                               w["lru_lambda"], name="lru_fwd")
            att, lse = _dilated_fwd(proj)
            mix = jnp.concatenate([rec, att], axis=1)
            (h,) = _rowmm(mix, w["ab_w_out"], name="l0_out_proj", res=h)
            sv.update(hs=hs)
        else:
            proj, n = _rowmm(h, w["c_w_qkv"], name="l1_qkv_proj", gain=w["mix_norm"][1], bias=w["c_b_qkv"],
                             rope=(0, (C_HEADS + C_KV_HEADS) * HEAD_DIM, tabs))
            mix, lse = _band_fwd(proj, proj, proj, d=1, nq=C_HEADS, nkv=C_KV_HEADS, qcol=lambda r: 0, kcol=lambda r: 8,
                                 vcol=lambda r: 9, max_dist=C_WINDOW - 1, sinks=w["c_sinks"], name="swa_fwd")
            (h,) = _rowmm(mix, w["c_w_out"], name="l1_out_proj", res=h, bias=w["c_b_out"])
        sv.update(proj=proj, n_mix=n, mix=mix, lse=lse, h_xa=h)
        xq, nx = _rowmm(h, w["xa_wq"][layer][None], name=f"xa_q_proj{layer}", gain=w["xa_norm"][layer])
        kv, nm = _rowmm(mem, w["xa_wkv"][layer][None], name=f"xa_kv_proj{layer}", gain=w["xa_mem_norm"][layer])
        xo, xlse = _xattn_fwd(xq, kv, name=f"xa_fwd{layer}")
        (h,) = _rowmm(xo, w["xa_wo"][layer], name=f"xa_out_proj{layer}", res=h)
        sv.update(xq=xq, nx=nx, kv=kv, nm=nm, xo=xo, xlse=xlse, h_ffn=h)
        gu, nf, act = _rowmm(h, w["ffn_w_gate_up"], layer=layer, name=f"ffn_in{layer}", gain=w["ffn_norm"][layer], swiglu=True)
        (h,) = _rowmm(act, w["ffn_w_down"][layer][None], name=f"ffn_out{layer}", res=h, tm=512)
        sv.update(gu=gu, nf=nf, act=act)
        saved.append(sv)

    dh, g["final_norm"], loss = _final_loss(h, w["final_norm"], target, name="final_loss")

    stk = {k: [None, None] for k in ("xa_norm", "xa_mem_norm", "ffn_norm", "mix_norm")}
    for layer in (1, 0):
        sv = saved[layer]
        (g["ffn_w_down", layer],) = _mm_tn(sv["act"], dh, S=1, name=f"ffn_down_dw{layer}", kk=D_FF // 2)
        (dgu,) = _mm_nt(dh, w["ffn_w_down"][layer][None], name=f"ffn_dact{layer}", mode="swiglu", kchunk=D_FF // 2, gu=sv["gu"])
        (g["ffn_w_gate_up", layer],) = _mm_tn(sv["nf"], dgu, S=N_CHIPS, name=f"ffn_gu_dw{layer}")
        dh, stk["ffn_norm"][layer] = _mm_nt(dgu, w["ffn_w_gate_up"], layer=layer, name=f"ffn_dx{layer}", mode="norm",
                                            h=sv["h_ffn"], gain=w["ffn_norm"][layer], dh=dh)
        (g["xa_wo", layer],) = _mm_tn(sv["xo"], dh, S=N_CHIPS, name=f"xa_wo_dw{layer}")
        (dxo,) = _mm_nt(dh, w["xa_wo"][layer], name=f"xa_dxo{layer}", mode="plain")
        dxq, dkv = _xattn_bwd(sv["xq"], sv["kv"], sv["xo"], sv["xlse"], dxo, name=f"xa_bwd{layer}")
        (g["xa_wq", layer],) = _mm_tn(sv["nx"], dxq, S=1, name=f"xa_wq_dw{layer}")
        dh, stk["xa_norm"][layer] = _mm_nt(dxq, w["xa_wq"][layer][None], name=f"xa_dx{layer}", mode="norm", h=sv["h_xa"],
                                           gain=w["xa_norm"][layer], dh=dh)
        (g["xa_wkv", layer],) = _mm_tn(sv["nm"], dkv, S=1, name=f"xa_wkv_dw{layer}")
        _, stk["xa_mem_norm"][layer] = _mm_nt(dkv, w["xa_wkv"][layer][None], name=f"xa_dmem{layer}", mode="norm", h=mem,
                                              gain=w["xa_mem_norm"][layer])
        if layer == 1:
            g["c_w_out"], g["c_b_out"] = _mm_tn(sv["mix"], dh, S=1, name="l1_out_dw", bias=True)
            (dmix,) = _mm_nt(dh, w["c_w_out"], name="l1_dmix", mode="plain")
            dq, dk, dv, dsk = _band_bwd(sv["proj"], sv["proj"], sv["proj"], dmix, sv["mix"], sv["lse"], d=1, nq=C_HEADS,
                                        nkv=C_KV_HEADS, qcol=lambda r: 0, kcol=lambda r: 8, vcol=lambda r: 9,
                                        docol=lambda r: 0, max_dist=C_WINDOW - 1, sinks=w["c_sinks"], name="swa_bwd")
            g["c_sinks"] = dsk[0, :C_HEADS]
            dproj = _attn_grad_combine([(dq, dk, dv)], tabs, name="swa_grad_combine")
            g["c_w_qkv"], g["c_b_qkv"] = _mm_tn(sv["n_mix"], dproj, S=1, name="l1_qkv_dw", bias=True)
            dh, stk["mix_norm"][1] = _mm_nt(dproj, w["c_w_qkv"], name="l1_dx", mode="norm", h=sv["h_mix"],
                                            gain=w["mix_norm"][1], dh=dh)
            if on_grads is not None:
                dh = on_grads("layer1", g, dh)
        else:
            if on_grads is not None:
                dh = on_grads("layer0_ffn_xa", g, dh)
            (g["ab_w_out"],) = _mm_tn(sv["mix"], dh, S=1, name="l0_out_dw", kk=768)
            (dmix,) = _mm_nt(dh, w["ab_w_out"], name="l0_dmix", mode="plain", kchunk=768)
            (dxy, g["lru_conv_w"], g["lru_conv_b"], g["lru_wa"], g["lru_ba"], g["lru_wx"], g["lru_bx"],
             g["lru_lambda"]) = _lru_bwd(sv["proj"], sv["hs"], dmix, w["lru_conv_w"], w["lru_conv_b"], w["lru_wa"],
                                         w["lru_ba"], w["lru_wx"], w["lru_bx"], w["lru_lambda"], name="lru_bwd")
            dqkv = _dilated_bwd(sv["proj"], sv["mix"][:, D_MODEL:], sv["lse"], dmix[:, D_MODEL:], tabs)
            dproj = jnp.concatenate([dxy, dqkv], axis=1)
            (g["ab_w_in"],) = _mm_tn(sv["n_mix"], dproj, S=N_CHIPS, name="l0_in_dw")
            dh, stk["mix_norm"][0] = _mm_nt(dproj, w["ab_w_in"], name="l0_dx", mode="norm", h=sv["h_mix"],
                                            gain=w["mix_norm"][0], dh=dh)
    for k, v in stk.items():
        g[k] = jnp.concatenate(v, axis=0)
    return loss[0, 0], dh, g


ANY = pl.BlockSpec(memory_space=pl.ANY)
MESH = pl.DeviceIdType.MESH


def _place():
    x, y, c = lax.axis_index("x"), lax.axis_index("y"), lax.axis_index("c")
    return x, y, c, [(1 - x, y), (x, 1 - y), (1 - x, 1 - y)]


def _remote(send_sems, recv_sems):
    def copy(k, src, dst, to):
        return pltpu.make_async_remote_copy(src_ref=src, dst_ref=dst, send_sem=send_sems.at[k], recv_sem=recv_sems.at[k],
                                            device_id=to, device_id_type=MESH)
    return copy


def _halves(ref, n_rows):
    rh = n_rows // 2
    return lambda lead, hh: ref.at[(*lead, pl.ds(hh * rh, rh), slice(None))]


def _gather_weights(packs, spack):
    n = len(packs)

    def body(*refs):
        w_refs, s_ref, wf_refs, sf_ref = refs[:n], refs[n], refs[n + 1:2 * n + 1], refs[2 * n + 1]
        x, y, c, chips = _place()
        me, sib = 2 * x + y, (x, y, 1 - c)
        copy = _remote(*refs[-2:])
        src = [_halves(w_refs[g], packs[g].shape[0]) for g in range(n)]
        dst = [_halves(wf_refs[g], packs[g].shape[0]) for g in range(n)]
        sends = []
        for g in range(n):
            for j, (cx, cy) in enumerate(chips):
                sends.append(copy(3 * g + j, src[g]((), c), dst[g]((me,), c), (cx, cy, c)))
        for j, (cx, cy) in enumerate(chips):
            sends.append(copy(6 * n + j, s_ref, sf_ref.at[me], (cx, cy, c)))
        for cp in sends:
            cp.start()
        for g in range(n):
            for j, (cx, cy) in enumerate(chips):
                got = dst[g]((2 * cx + cy,), c)
                copy(3 * g + j, got, got, sib).wait_recv()
                fwd = copy(3 * n + 3 * g + j, got, got, sib)
                fwd.start()
                sends.append(fwd)
        for g in range(n):
            for j, (cx, cy) in enumerate(chips):
                got = dst[g]((2 * cx + cy,), 1 - c)
                copy(3 * n + 3 * g + j, got, got, sib).wait_recv()
        for j, (cx, cy) in enumerate(chips):
            copy(6 * n + j, s_ref, sf_ref.at[2 * cx + cy], sib).wait_recv()
        for cp in sends:
            cp.wait_send()

    ins = list(packs) + [spack]
    out_shape = [jax.ShapeDtypeStruct((N_CHIPS,) + a.shape, a.dtype) for a in ins]
    n_sems = 6 * n + 3
    outs = pl.pallas_call(body, name="gather_weights", out_shape=out_shape, in_specs=[ANY] * len(ins),
                          out_specs=[ANY] * len(ins),
                          scratch_shapes=[pltpu.SemaphoreType.DMA((n_sems,)), pltpu.SemaphoreType.DMA((n_sems,))])(*ins)
    chip = 2 * lax.axis_index("x") + lax.axis_index("y")
    outs = [lax.dynamic_update_index_in_dim(o, a, chip, 0) for o, a in zip(outs, ins)]
    return outs[:n], outs[n]


SEQUENCER_GATHER_IDS = {"mid": 1, "late": 5}


def _gather_weights_behind(packs, *, tag):
    n = len(packs)

    def body(*refs):
        w_refs, wf_refs = refs[:n], refs[n:2 * n]
        x, y, c, chips = _place()
        me, sib = 2 * x + y, (x, y, 1 - c)
        barrier = pltpu.get_barrier_semaphore()
        for peer in [(cx, cy, c) for cx, cy in chips] + [sib]:
            pl.semaphore_signal(barrier, inc=1, device_id=peer, device_id_type=MESH)
        pl.semaphore_wait(barrier, len(chips) + 1)
        copy = _remote(*refs[-2:])
        src = [_halves(w_refs[g], packs[g].shape[0]) for g in range(n)]
        dst = [_halves(wf_refs[g], packs[g].shape[0]) for g in range(n)]
        sends = []
        for g in range(n):
            for j, (cx, cy) in enumerate(chips):
                sends.append(copy(3 * g + j, src[g]((), c), dst[g]((me,), c), (cx, cy, c)))
        for cp in sends:
            cp.start()
        for g in range(n):
            for j, (cx, cy) in enumerate(chips):
                got = dst[g]((2 * cx + cy,), c)
                copy(3 * g + j, got, got, sib).wait_recv()
                fwd = copy(3 * n + 3 * g + j, got, got, sib)
                fwd.start()
                sends.append(fwd)
        for g in range(n):
            for j, (cx, cy) in enumerate(chips):
                got = dst[g]((2 * cx + cy,), 1 - c)
                copy(3 * n + 3 * g + j, got, got, sib).wait_recv()
        for cp in sends:
            cp.wait_send()

    out_type = [jax.ShapeDtypeStruct((N_CHIPS,) + a.shape, a.dtype) for a in packs]
    outs = pl.kernel(body, out_type=out_type, mesh=plsc.ScalarSubcoreMesh(axis_name="sequencer", num_cores=1),
                     name="gather_weights_behind_" + tag,
                     scratch_types=[pltpu.SemaphoreType.DMA((6 * n,)), pltpu.SemaphoreType.DMA((6 * n,))],
                     compiler_params=pltpu.CompilerParams(collective_id=SEQUENCER_GATHER_IDS[tag]))(*packs)
    chip = 2 * lax.axis_index("x") + lax.axis_index("y")
    return [lax.dynamic_update_index_in_dim(o, a, chip, 0) for o, a in zip(outs, packs)]


def _rs_pair_exchange(gpacks, *, name):
    n = len(gpacks)

    def body(*refs):
        g_refs, ra_refs = refs[:n], refs[n:2 * n]
        x, y, c, _ = _place()
        copy = _remote(*refs[-2:])
        cps = []
        for g in range(n):
            half = _halves(g_refs[g], gpacks[g].shape[1])
            cps += [copy(N_CHIPS * g + j, half((j,), 1 - c), ra_refs[g].at[j], (x, y, 1 - c)) for j in range(N_CHIPS)]
        for cp in cps:
            cp.start()
        for cp in cps:
            cp.wait()

    out_shape = [jax.ShapeDtypeStruct((N_CHIPS, a.shape[1] // 2, a.shape[2]), a.dtype) for a in gpacks]
    n_sems = N_CHIPS * n
    return pl.pallas_call(body, name=name, out_shape=out_shape, in_specs=[ANY] * n, out_specs=[ANY] * n,
                          scratch_shapes=[pltpu.SemaphoreType.DMA((n_sems,)), pltpu.SemaphoreType.DMA((n_sems,))])(*gpacks)


def _row_tile(rows, cap=512):
    return max(t for t in range(16, min(rows, cap) + 1, 16) if rows % t == 0)


def _rs_pair_add(place, gpack, ra, *, name):
    _, R, C = gpack.shape
    Rh = R // 2
    tr = _row_tile(Rh)
    nrb = Rh // tr

    def body(p_ref, g_ref, ra_ref, pair_ref, own_ref):
        s = g_ref[...].astype(F32) + ra_ref[...].astype(F32)
        pair_ref[...] = s.astype(BF16)

        @pl.when(pl.program_id(1) == p_ref[1])
        def _():
            own_ref[...] = s

    grid_spec = pltpu.PrefetchScalarGridSpec(
        num_scalar_prefetch=1, grid=(nrb, N_CHIPS),
        in_specs=[pl.BlockSpec((None, tr, C), lambda i, j, p: (j, p[0] * nrb + i, 0)),
                  pl.BlockSpec((None, tr, C), lambda i, j, p: (j, i, 0))],
        out_specs=[pl.BlockSpec((None, tr, C), lambda i, j, p: (j, i, 0)), pl.BlockSpec((tr, C), lambda i, j, p: (i, 0))])
    return pl.pallas_call(
        body, name=name, grid_spec=grid_spec,
        out_shape=[jax.ShapeDtypeStruct((N_CHIPS, Rh, C), BF16), jax.ShapeDtypeStruct((Rh, C), F32)],
        compiler_params=pltpu.CompilerParams(dimension_semantics=("arbitrary", "arbitrary"),
                                             vmem_limit_bytes=VMEM_LIMIT_V7X))(place, gpack, ra)


SEQUENCER_EXCHANGE_IDS = {"l1": 2, "l0a": 3, "l0b": 4}


def _rs_chip_exchange_behind(pairs, *, tag, small=None):
    n = len(pairs)
    has_small = small is not None

    def body(*refs):
        p_refs = refs[:n]
        s_ref = refs[n] if has_small else None
        rb_refs = refs[n + has_small:2 * n + has_small]
        rs_ref = refs[2 * n + 1] if has_small else None
        x, y, c, chips = _place()
        peers = [(1 - x if k & 4 else x, 1 - y if k & 2 else y, 1 - c if k & 1 else c) for k in range(1, 8)]
        shake = peers if has_small else [(cx, cy, c) for cx, cy in chips]
        barrier = pltpu.get_barrier_semaphore()
        for peer in shake:
            pl.semaphore_signal(barrier, inc=1, device_id=peer, device_id_type=MESH)
        pl.semaphore_wait(barrier, len(shake))
        copy = _remote(*refs[-2:])
        cps = []
        for g in range(n):
            cps += [copy(3 * g + j, p_refs[g].at[2 * cx + cy], rb_refs[g].at[j], (cx, cy, c)) for j, (cx, cy) in enumerate(chips)]
        if has_small:
            dev = 4 * x + 2 * y + c
            cps += [copy(3 * n + k, s_ref, rs_ref.at[dev], peer) for k, peer in enumerate(peers)]
        for cp in cps:
            cp.start()
        for g in range(n):
            for j in range(3):
                copy(3 * g + j, p_refs[g].at[0], rb_refs[g].at[j], (x, y, c)).wait_recv()
        if has_small:
            for k, (px, py, pc) in enumerate(peers):
                copy(3 * n + k, s_ref, rs_ref.at[4 * px + 2 * py + pc], (x, y, c)).wait_recv()
        for cp in cps:
            cp.wait_send()

    ins = list(pairs) + ([small] if has_small else [])
    out_type = [jax.ShapeDtypeStruct((3,) + p.shape[1:], p.dtype) for p in pairs]
    if has_small:
        out_type.append(jax.ShapeDtypeStruct((8,) + small.shape, small.dtype))
    n_sems = 3 * n + 7 * has_small
    outs = pl.kernel(body, out_type=out_type, mesh=plsc.ScalarSubcoreMesh(axis_name="sequencer", num_cores=1),
                     name="rs_chip_exchange_behind_" + tag,
                     scratch_types=[pltpu.SemaphoreType.DMA((n_sems,)), pltpu.SemaphoreType.DMA((n_sems,))],
                     compiler_params=pltpu.CompilerParams(collective_id=SEQUENCER_EXCHANGE_IDS[tag]))(*ins)
    if has_small:
        dev = 4 * lax.axis_index("x") + 2 * lax.axis_index("y") + lax.axis_index("c")
        outs = list(outs[:n]) + [lax.dynamic_update_index_in_dim(outs[n], small, dev, 0)]
    return outs


def _rs_final_add(place, own, rb, *, name):
    Rh, C = own.shape
    tr = _row_tile(Rh)
    nrb = Rh // tr

    def body(p_ref, o_ref, rb_ref, f_ref):
        f_ref[...] = ((o_ref[...] + rb_ref[0].astype(F32)) + rb_ref[1].astype(F32)) + rb_ref[2].astype(F32)

    grid_spec = pltpu.PrefetchScalarGridSpec(
        num_scalar_prefetch=1, grid=(nrb,),
        in_specs=[pl.BlockSpec((tr, C), lambda i, p: (i, 0)), pl.BlockSpec((3, tr, C), lambda i, p: (0, i, 0))],
        out_specs=pl.BlockSpec((tr, C), lambda i, p: (p[0] * nrb + i, 0)))
    return pl.pallas_call(
        body, name=name, grid_spec=grid_spec, out_shape=jax.ShapeDtypeStruct((2 * Rh, C), F32),
        compiler_params=pltpu.CompilerParams(dimension_semantics=("arbitrary",), vmem_limit_bytes=VMEM_LIMIT_V7X))(place, own, rb)


def _sum_slots(rs):
    n, rows, C = rs.shape

    def body(r_ref, o_ref):
        acc = r_ref[0]
        for k in range(1, n):
            acc = acc + r_ref[k]
        o_ref[...] = acc

    return _call(body, name="small_grad_sum", grid=(1,), in_specs=[pl.BlockSpec((n, rows, C), lambda i: (0, 0, 0))],
                 out_specs=pl.BlockSpec((rows, C), lambda i: (0, 0)), out_shape=jax.ShapeDtypeStruct((rows, C), F32),
                 sem=("arbitrary",))(rs)


def _rs_sibling_share(gbufs, *, name):
    n = len(gbufs)

    def body(*refs):
        g_refs = refs[n:2 * n]
        x, y, c, _ = _place()
        copy = _remote(*refs[-2:])
        halves = [_halves(g_refs[g], gbufs[g].shape[0]) for g in range(n)]
        outs = [copy(g, halves[g]((), c), halves[g]((), c), (x, y, 1 - c)) for g in range(n)]
        for cp in outs:
            cp.start()
        for g in range(n):
            copy(g, halves[g]((), 1 - c), halves[g]((), 1 - c), (x, y, c)).wait_recv()
        for cp in outs:
            cp.wait_send()

    return pl.pallas_call(body, name=name, out_shape=[jax.ShapeDtypeStruct(a.shape, a.dtype) for a in gbufs],
                          in_specs=[ANY] * n, out_specs=[ANY] * n, input_output_aliases={g: g for g in range(n)},
                          scratch_shapes=[pltpu.SemaphoreType.DMA((n,)), pltpu.SemaphoreType.DMA((n,))])(*gbufs)


def _adamw(w, g, m, v, *, name, g_row=0):
    rows, cols = w.shape
    tr = rows
    for cand in range(min(rows, 512), 7, -8):
        if rows % cand == 0 and g_row % cand == 0:
            tr = cand
            break
    spec = pl.BlockSpec((tr, cols), lambda i: (i, 0))
    g_spec = pl.BlockSpec((tr, cols), lambda i: (g_row // tr + i, 0))

    def body(w_ref, g_ref, m_ref, v_ref, d_ref, nm_ref, nv_ref):
        gg = g_ref[...]
        nm = ADAM_B1 * m_ref[...] + (1.0 - ADAM_B1) * gg
        nv = ADAM_B2 * v_ref[...] + (1.0 - ADAM_B2) * (gg * gg)
        m_hat = nm / (1.0 - ADAM_B1 ** ADAM_STEP)
        v_hat = nv / (1.0 - ADAM_B2 ** ADAM_STEP)
        d_ref[...] = -ADAM_LR * (m_hat / (jnp.sqrt(v_hat) + ADAM_EPS) + ADAM_WD * w_ref[...])
        nm_ref[...] = nm
        nv_ref[...] = nv

    return _call(body, name=name, grid=(rows // tr,), in_specs=[spec, g_spec, spec, spec], out_specs=[spec] * 3,
                 out_shape=[jax.ShapeDtypeStruct((rows, cols), F32)] * 3, sem=("parallel",))(w, g, m, v)


WEIGHT_NAMES = ("mix_norm", "ab_w_in", "lru_conv_w", "lru_conv_b", "lru_wa", "lru_ba", "lru_wx", "lru_bx", "lru_lambda",
                "ab_w_out", "c_w_qkv", "c_b_qkv", "c_sinks", "c_w_out", "c_b_out", "xa_norm", "xa_mem_norm", "xa_wq",
                "xa_wkv", "xa_wo", "ffn_norm", "ffn_w_gate_up", "ffn_w_down", "final_norm")
EARLY_GROUPS = (("ab_w_in",),)
MID_GROUPS = (("ab_w_out",), ("lru_wa", "lru_wx"))
LATE_GROUPS = (("c_w_out", "xa_wkv", "ffn_w_down"), ("ffn_w_gate_up",), ("xa_wo",), ("xa_wq",), ("c_w_qkv",))
GROUPS = EARLY_GROUPS + MID_GROUPS + LATE_GROUPS
REPLICATED = ("mix_norm", "lru_conv_b", "lru_lambda", "c_sinks", "xa_norm", "xa_mem_norm", "ffn_norm", "final_norm")
SMALL_SHARDED = ("lru_conv_w", "lru_ba", "lru_bx", "c_b_qkv", "c_b_out")
LANES = 1024


def _rows(v):
    flat = v.reshape(-1)
    return jnp.pad(flat, (0, -flat.shape[0] % LANES)).reshape(-1, LANES)


def _pack_small(parts, total, *, name):
    def body(*refs):
        o_ref = refs[-1]
        o_ref[...] = jnp.zeros_like(o_ref)
        row = 0
        for p_ref in refs[:-1]:
            o_ref[row:row + p_ref.shape[0], :] = p_ref[...]
            row += p_ref.shape[0]

    return _call(body, name=name, grid=(1,), in_specs=[pl.BlockSpec(p.shape, lambda i: (0, 0)) for p in parts],
                 out_specs=pl.BlockSpec((total, LANES), lambda i: (0, 0)),
                 out_shape=jax.ShapeDtypeStruct((total, LANES), F32), sem=("arbitrary",))(*parts)


def _from_shards(name, t):
    minor = t.shape[-1]
    if name == "ab_w_in":
        return t
    if name in ("ab_w_out", "c_w_out"):
        return t.reshape(1, -1, minor)
    if name == "ffn_w_gate_up":
        return t.reshape(N_CHIPS, 2, -1, minor)
    if name in ("xa_wq", "xa_wkv", "ffn_w_down"):
        return t.reshape(N_CHIPS, 2, -1, minor).transpose(1, 0, 2, 3).reshape(2, -1, minor)
    if name in ("lru_wa", "lru_wx"):
        return t.reshape(N_CHIPS, LRU_HEADS, -1, minor).transpose(1, 0, 2, 3).reshape(LRU_HEADS, LRU_HEAD_DIM, minor)
    if name == "xa_wo":
        return t.reshape(N_CHIPS, 2, -1, minor).transpose(1, 0, 2, 3)
    assert name == "c_w_qkv"
    return t.transpose(1, 0, 2).reshape(1, D_MODEL, -1)


def _piece_shards(name, g):
    minor = g.shape[-1]
    if name in ("ab_w_in", "ffn_w_gate_up", "xa_wo"):
        return g
    if name in ("ab_w_out", "c_w_out", "xa_wq", "xa_wkv", "ffn_w_down"):
        return g.reshape(N_CHIPS, -1, minor)
    if name in ("lru_wa", "lru_wx"):
        return g.reshape(LRU_HEADS, N_CHIPS, -1, minor).transpose(1, 0, 2, 3).reshape(N_CHIPS, -1, minor)
    assert name == "c_w_qkv"
    return g.reshape(D_MODEL, N_CHIPS, -1).transpose(1, 0, 2)


RS_SETS = {
    "l1": ((("c_w_out", None), ("xa_wkv", 1), ("ffn_w_down", 1)), (("ffn_w_gate_up", 1),), (("xa_wo", 1),),
           (("xa_wq", 1),), (("c_w_qkv", None),)),
    "l0a": ((("xa_wkv", 0), ("ffn_w_down", 0)), (("ffn_w_gate_up", 0),), (("xa_wo", 0),), (("xa_wq", 0),)),
    "l0b": ((("ab_w_out", None),), (("ab_w_in", None),), (("lru_wa", None), ("lru_wx", None))),
}
RS_STAGE = {"layer1": "l1", "layer0_ffn_xa": "l0a"}


def kernel(x, mem, mix_norm, ab_w_in, lru_conv_w, lru_conv_b, lru_wa, lru_ba, lru_wx, lru_bx, lru_lambda, ab_w_out, c_w_qkv, c_b_qkv, c_sinks, c_w_out, c_b_out, xa_norm, xa_mem_norm, xa_wq, xa_wkv, xa_wo, ffn_norm, ffn_w_gate_up, ffn_w_down, final_norm, loss_target, m_mix_norm, m_ab_w_in, m_lru_conv_w, m_lru_conv_b, m_lru_wa, m_lru_ba, m_lru_wx, m_lru_bx, m_lru_lambda, m_ab_w_out, m_c_w_qkv, m_c_b_qkv, m_c_sinks, m_c_w_out, m_c_b_out, m_xa_norm, m_xa_mem_norm, m_xa_wq, m_xa_wkv, m_xa_wo, m_ffn_norm, m_ffn_w_gate_up, m_ffn_w_down, m_final_norm, v_mix_norm, v_ab_w_in, v_lru_conv_w, v_lru_conv_b, v_lru_wa, v_lru_ba, v_lru_wx, v_lru_bx, v_lru_lambda, v_ab_w_out, v_c_w_qkv, v_c_b_qkv, v_c_sinks, v_c_w_out, v_c_b_out, v_xa_norm, v_xa_mem_norm, v_xa_wq, v_xa_wkv, v_xa_wo, v_ffn_norm, v_ffn_w_gate_up, v_ffn_w_down, v_final_norm):
    given = dict(locals())
    wl = {n: given[n] for n in WEIGHT_NAMES}
    ml = {n: given["m_" + n] for n in WEIGHT_NAMES}
    vl = {n: given["v_" + n] for n in WEIGHT_NAMES}
    xi, yi, ci = lax.axis_index("x"), lax.axis_index("y"), lax.axis_index("c")
    chip = 2 * xi + yi

    def join(parts, axis):
        return parts[0] if len(parts) == 1 else jnp.concatenate(parts, axis=axis)

    local_rows = {n: wl[n].size // wl[n].shape[-1] for grp in GROUPS for n in grp}
    packs = [join([wl[n].astype(BF16).reshape(local_rows[n], wl[n].shape[-1]) for n in grp], 0) for grp in GROUPS]
    spack = _pack_small([_rows(wl[n]) for n in SMALL_SHARDED], 8, name="pack_small_weights")
    n_early, n_mid = len(EARLY_GROUPS), len(EARLY_GROUPS) + len(MID_GROUPS)
    early, sfull = _gather_weights(packs[:n_early], spack)
    early, sfull, mid_packs = lax.optimization_barrier((early, sfull, packs[n_early:n_mid]))
    mid = _gather_weights_behind(mid_packs, tag="mid")
    mid, late_packs = lax.optimization_barrier((mid, packs[n_mid:]))
    gathered = early + mid + _gather_weights_behind(late_packs, tag="late")
    w = {n: wl[n] for n in REPLICATED}
    w["c_sinks"] = wl["c_sinks"][0]
    for grp, full in zip(GROUPS, gathered):
        off = 0
        for n in grp:
            w[n] = _from_shards(n, full if len(grp) == 1 else full[:, off:off + local_rows[n]])
            off += local_rows[n]
    for r, n in enumerate(SMALL_SHARDED):
        loc = wl[n].shape[1:]
        t = sfull[:, r, :wl[n].size].reshape((N_CHIPS,) + loc)
        if n == "lru_conv_w":
            w[n] = t.transpose(1, 0, 2).reshape(CONV_WIDTH, -1)
        elif n in ("lru_ba", "lru_bx"):
            w[n] = t.transpose(1, 0, 2).reshape(1, -1)
        else:
            w[n] = t.reshape(1, -1)

    place = jnp.stack([ci, chip]).astype(jnp.int32)

    def pair_stage(spec, g, tag):
        piece = lambda n, l: (g[n] if l is None else g[n, l]).astype(BF16)
        gpacks = [join([_piece_shards(n, piece(n, l)) for n, l in grp], 1) for grp in spec]
        ras = _rs_pair_exchange(gpacks, name=f"rs_pair_exchange_{tag}")
        sums = [_rs_pair_add(place, gp, ra, name=f"rs_pair_add_{tag}_{i}") for i, (gp, ra) in enumerate(zip(gpacks, ras))]
        return [pair for pair, _ in sums], [own for _, own in sums]

    owns, rbs = [], []

    def reduce_behind(stage, g, dh):
        tag = RS_STAGE[stage]
        pairs, own = pair_stage(RS_SETS[tag], g, tag)
        own, dh = lax.optimization_barrier((own, dh))
        owns.extend(own)
        rbs.extend(_rs_chip_exchange_behind(pairs, tag=tag))
        return dh

    loss_part, grad_x, g = _device_step(x[0], mem[0], loss_target[0], w, on_grads=reduce_behind)
    owns, rbs, grad_x = lax.optimization_barrier((owns, rbs, grad_x))

    small_parts = [_rows(g[n]) for n in REPLICATED] + [_rows(jnp.broadcast_to(loss_part, (LANES,)))]
    small_parts += [_rows(g[n]) for n in SMALL_SHARDED]
    small = _pack_small(small_parts, 24, name="pack_small_grads")
    pairs, own = pair_stage(RS_SETS["l0b"], g, "l0b")
    *rb, rs = _rs_chip_exchange_behind(pairs, tag="l0b", small=small)

    def finish(own_sums, received, first, name):
        return _rs_sibling_share([_rs_final_add(place, o, r, name=f"rs_final_add_{first + i}")
                                  for i, (o, r) in enumerate(zip(own_sums, received))], name=name)

    gsums = finish(owns, rbs, 0, "rs_sibling_share_behind") + finish(own, rb, len(owns), "rs_sibling_share_last")
    ssum = _sum_slots(rs)

    where = {}
    for grp, gsum in zip(RS_SETS["l1"] + RS_SETS["l0a"] + RS_SETS["l0b"], gsums):
        off = 0
        for n, l in grp:
            rows = local_rows[n] if l is None else local_rows[n] // 2
            where[n, l] = (gsum, off, rows, len(grp) == 1)
            off += rows
    take = lambda gsum, off, rows, whole: gsum if whole else gsum[off:off + rows]
    grads, grad_rows = {}, {}
    for grp in LATE_GROUPS + EARLY_GROUPS + MID_GROUPS:
        for n in grp:
            if (n, None) in where:
                grads[n] = take(*where[n, None]).reshape(wl[n].shape)
                grad_rows[n] = where[n, None][:2]
            else:
                grads[n] = jnp.stack([take(*where[n, l]).reshape(wl[n].shape[1:]) for l in range(2)])
                grad_rows[n] = (grads[n].reshape(local_rows[n], wl[n].shape[-1]), 0)
    row = 0
    for n in REPLICATED:
        k = _rows(g[n]).shape[0]
        grads[n] = ssum[row:row + k].reshape(-1)[:wl[n].size].reshape(wl[n].shape)
        row += k
    loss = ssum[row, 0]
    row += 1
    for n in SMALL_SHARDED:
        k = _rows(g[n]).shape[0]
        full = ssum[row:row + k].reshape(-1)[:g[n].size]
        row += k
        loc = wl[n].shape
        if n == "lru_conv_w":
            sh = full.reshape(CONV_WIDTH, N_CHIPS, -1)
        elif n in ("lru_ba", "lru_bx"):
            sh = full.reshape(LRU_HEADS, N_CHIPS, -1)
        else:
            sh = full.reshape(1, N_CHIPS, -1)
        grads[n] = lax.dynamic_index_in_dim(sh, chip, axis=1, keepdims=False).reshape(loc)

    delta, new_m, new_v = {}, {}, {}
    for n, (gsum, off) in grad_rows.items():
        shape2 = (local_rows[n], wl[n].shape[-1])
        d, nm, nv = _adamw(wl[n].reshape(shape2), gsum, ml[n].reshape(shape2), vl[n].reshape(shape2), g_row=off,
                           name="adamw_" + n)
        delta[n], new_m[n], new_v[n] = (t.reshape(wl[n].shape) for t in (d, nm, nv))
    smalls = REPLICATED + SMALL_SHARDED
    packs = [_pack_small([_rows(src[n]) for n in smalls], 24, name="pack_adamw_" + tag)
             for tag, src in (("w", wl), ("g", grads), ("m", ml), ("v", vl))]
    outs = _adamw(*packs, name="adamw_small")
    row = 0
    for n in smalls:
        k = _rows(wl[n]).shape[0]
        for dst, o in zip((delta, new_m, new_v), outs):
            dst[n] = o[row:row + k].reshape(-1)[:wl[n].size].reshape(wl[n].shape)
        row += k

    return (loss, grad_x[None], *[grads[n] for n in WEIGHT_NAMES], *[delta[n] for n in WEIGHT_NAMES],
            *[new_m[n] for n in WEIGHT_NAMES], *[new_v[n] for n in WEIGHT_NAMES])
```

```python
import jax
import jax.numpy as jnp
from jax import lax
from jax.experimental import pallas as pl
from jax.experimental.pallas import tpu as pltpu
from jax.experimental.pallas import tpu_sc as plsc

F32, BF16 = jnp.float32, jnp.bfloat16
D_MODEL = 1024
NORM_EPS = 1e-6
ROPE_THETA = 500000.0
HEAD_DIM = 64
ROT_DIM = 16
BLK = 128
LRU_HEADS, LRU_HEAD_DIM, CONV_WIDTH, LRU_C = 4, 256, 4, 8.0
DILATED_PATTERN = ((128, 1), (512, 4), (2048, 16))
B_HEADS, C_HEADS, C_KV_HEADS, C_WINDOW = 8, 16, 2, 128
XA_HEADS, XA_HEAD_DIM, N_MEM = 4, 128, 256
D_FF = 2816
NEG = -1e30
ADAM_LR, ADAM_B1, ADAM_B2, ADAM_EPS, ADAM_WD, ADAM_STEP = 0.001, 0.9, 0.999, 1e-08, 0.01, 10
N_CHIPS = 4
VMEM_LIMIT_V7X = 56 * 1024 * 1024

NN = (((1,), (0,)), ((), ()))
NT = (((1,), (1,)), ((), ()))
TN = (((0,), (0,)), ((), ()))


def _dot(a, b, dims=NN):
    return lax.dot_general(a, b, dims, preferred_element_type=F32)


def _sigmoid(x):
    return 1.0 / (1.0 + jnp.exp(-x))


def _call(body, *, name, grid, in_specs, out_specs, out_shape, scratch=(), sem=None):
    return pl.pallas_call(
        body, name=name, grid=grid, in_specs=in_specs, out_specs=out_specs, out_shape=out_shape,
        scratch_shapes=list(scratch),
        compiler_params=pltpu.CompilerParams(dimension_semantics=sem, vmem_limit_bytes=VMEM_LIMIT_V7X))


def _rope_tables(L):
    half = ROT_DIM // 2
    inv = ROPE_THETA ** (-jnp.arange(0, ROT_DIM, 2, dtype=F32) / ROT_DIM)
    ang = jnp.arange(L, dtype=F32)[:, None] * inv[None, :]
    cos, sin = jnp.cos(ang), jnp.sin(ang)
    rest = HEAD_DIM - ROT_DIM
    z8, zr, one = jnp.zeros((L, half), F32), jnp.zeros((L, rest), F32), jnp.ones((L, rest), F32)
    c = jnp.concatenate([cos, cos, one], axis=1)
    s1 = jnp.concatenate([-sin, z8, zr], axis=1)
    s2 = jnp.concatenate([z8, sin, zr], axis=1)
    return tuple(jnp.concatenate([t, t], axis=1) for t in (c, s1, s2))


def _rope_fwd(v, c, s1, s2):
    return v * c + pltpu.roll(v, 120, 1) * s1 + pltpu.roll(v, 8, 1) * s2


def _rope_bwd(dv, c, s1, s2):
    return dv * c + pltpu.roll(dv * s1, 8, 1) + pltpu.roll(dv * s2, 120, 1)


def _weight_spec(w, layer):
    once = pl.Buffered(1)
    if layer is None:
        return w.shape, pl.BlockSpec(w.shape, lambda i: (0, 0, 0), pipeline_mode=once)
    S, _, K, Ns = w.shape
    return (S, K, Ns), pl.BlockSpec((S, None, K, Ns), lambda i: (0, layer, 0, 0), pipeline_mode=once)


def _rowmm(a, w3, *, name, tm=512, gain=None, bias=None, res=None, swiglu=False, rope=None, layer=None):
    M, K = a.shape
    (S, _, Ns), w_spec = _weight_spec(w3, layer)
    N = S * Ns
    tm = min(tm, M)
    has_norm, has_bias, has_res, has_rope = gain is not None, bias is not None, res is not None, rope is not None
    row = lambda w: pl.BlockSpec((tm, w), lambda i: (i, 0))
    whole = lambda shape: pl.BlockSpec(shape, lambda i: (0,) * len(shape))
    ins, specs = [a], [row(K)]
    if has_norm:
        ins.append(gain.reshape(1, K)); specs.append(whole((1, K)))
    ins.append(w3); specs.append(w_spec)
    if has_bias:
        ins.append(bias.reshape(1, N)); specs.append(whole((1, N)))
    if has_res:
        ins.append(res); specs.append(row(N))
    if has_rope:
        ins += list(rope[2]); specs += [row(128)] * 3
    y_dtype = F32 if has_res else BF16
    out_shape, out_specs = [jax.ShapeDtypeStruct((M, N), y_dtype)], [row(N)]
    if has_norm:
        out_shape.append(jax.ShapeDtypeStruct((M, K), BF16)); out_specs.append(row(K))
    if swiglu:
        out_shape.append(jax.ShapeDtypeStruct((M, N // 2), BF16)); out_specs.append(row(N // 2))
    scratch = [pltpu.VMEM((tm, N), F32)] if has_rope else []

    def body(*refs):
        it = iter(refs)
        a_ref = next(it)
        g_ref = next(it) if has_norm else None
        w_ref = next(it)
        b_ref = next(it) if has_bias else None
        r_ref = next(it) if has_res else None
        tabs = [next(it) for _ in range(3)] if has_rope else None
        y_ref = next(it)
        n_ref = next(it) if has_norm else None
        act_ref = next(it) if swiglu else None
        ys_ref = next(it) if has_rope else None
        if has_norm:
            x = a_ref[...].astype(F32)
            ms = jnp.mean(x * x, axis=-1, keepdims=True)
            xb = (x * lax.rsqrt(ms + NORM_EPS) * g_ref[...]).astype(BF16)
            n_ref[...] = xb
        else:
            xb = a_ref[...].astype(BF16)
        if swiglu:
            for s in range(S // 2):
                g = _dot(xb, w_ref[s])
                u = _dot(xb, w_ref[s + S // 2])
                y_ref[:, s * Ns:(s + 1) * Ns] = g.astype(BF16)
                y_ref[:, N // 2 + s * Ns:N // 2 + (s + 1) * Ns] = u.astype(BF16)
                act_ref[:, s * Ns:(s + 1) * Ns] = (g * _sigmoid(g) * u).astype(BF16)
            return
        for s in range(S):
            sl = slice(s * Ns, (s + 1) * Ns)
            acc = _dot(xb, w_ref[s])
            if has_bias:
                acc = acc + b_ref[:, sl]
            if has_res:
                acc = acc + r_ref[:, sl]
            if has_rope:
                ys_ref[:, sl] = acc
            else:
                y_ref[:, sl] = acc.astype(y_dtype)
        if has_rope:
            c, s1, s2 = (t[...] for t in tabs)
            for cb in range(N // 128):
                cs = slice(cb * 128, (cb + 1) * 128)
                v = ys_ref[:, cs]
                if rope[0] <= cb * 128 < rope[1]:
                    v = _rope_fwd(v, c, s1, s2)
                y_ref[:, cs] = v.astype(BF16)

    return _call(body, name=name, grid=(M // tm,), in_specs=specs, out_specs=out_specs, out_shape=out_shape,
                 scratch=scratch, sem=("parallel",))(*ins)


def _mm_nt(dy, w3, *, name, mode, tm=512, kchunk=None, h=None, gain=None, dh=None, gu=None, layer=None, after=None):
    M, N = dy.shape
    (S, K, Ns), w_spec = _weight_spec(w3, layer)
    kchunk = kchunk or K
    tm = min(tm, M)
    row = lambda w: pl.BlockSpec((tm, w), lambda i: (i, 0))
    whole = lambda shape: pl.BlockSpec(shape, lambda i: (0,) * len(shape))
    ins, specs = [dy, w3], [row(N), w_spec]
    after = list(after or ())
    ins = after + ins
    specs = [pl.BlockSpec((8, a.shape[1]), lambda i: (0, 0)) for a in after] + specs
    has_dh = dh is not None
    if mode == "norm":
        ins += [h, gain.reshape(1, K)]; specs += [row(K), whole((1, K))]
        if has_dh:
            ins.append(dh); specs.append(row(K))
        out_shape = [jax.ShapeDtypeStruct((M, K), F32), jax.ShapeDtypeStruct((1, K), F32)]
        out_specs = [row(K), whole((1, K))]
    elif mode == "swiglu":
        ins.append(gu); specs.append(row(2 * K))
        out_shape, out_specs = [jax.ShapeDtypeStruct((M, 2 * K), BF16)], [row(2 * K)]
    else:
        out_shape, out_specs = [jax.ShapeDtypeStruct((M, K), BF16)], [row(K)]

    def body(*refs):
        it = iter(refs[len(after):])
        dy_ref, w_ref = next(it), next(it)
        if mode == "norm":
            h_ref, g_ref = next(it), next(it)
            dh_ref = next(it) if has_dh else None
            o_ref, dg_ref = next(it), next(it)
        elif mode == "swiglu":
            gu_ref, o_ref = next(it), next(it)
        else:
            o_ref = next(it)
        for kc in range(K // kchunk):
            ks = slice(kc * kchunk, (kc + 1) * kchunk)
            acc = None
            for s in range(S):
                t = _dot(dy_ref[:, s * Ns:(s + 1) * Ns].astype(BF16), w_ref[s, ks, :], NT)
                acc = t if acc is None else acc + t
            if mode == "plain":
                o_ref[:, ks] = acc.astype(BF16)
            elif mode == "swiglu":
                us = slice(K + kc * kchunk, K + (kc + 1) * kchunk)
                g = gu_ref[:, ks].astype(F32)
                u = gu_ref[:, us].astype(F32)
                sg = _sigmoid(g)
                o_ref[:, ks] = (acc * u * (sg * (1.0 + g * (1.0 - sg)))).astype(BF16)
                o_ref[:, us] = (acc * (g * sg)).astype(BF16)
            else:
                x = h_ref[...].astype(F32)
                r = lax.rsqrt(jnp.mean(x * x, axis=-1, keepdims=True) + NORM_EPS)
                xhat = x * r
                dxh = acc * g_ref[...]
                dx = r * (dxh - xhat * jnp.mean(dxh * xhat, axis=-1, keepdims=True))
                o_ref[...] = dx + dh_ref[...] if has_dh else dx

                @pl.when(pl.program_id(0) == 0)
                def _():
                    dg_ref[...] = jnp.zeros_like(dg_ref)

                dg_ref[...] += jnp.sum(acc * xhat, axis=0, keepdims=True)

    sem = ("arbitrary",) if mode == "norm" else ("parallel",)
    return _call(body, name=name, grid=(M // tm,), in_specs=specs, out_specs=out_specs, out_shape=out_shape, sem=sem)(*ins)


def _mm_tn(x, dy, *, S, name, tk=1024, kk=None, bias=False):
    M, K = x.shape
    N = dy.shape[1]
    Ns = N // S
    kk = kk or K
    tk = min(tk, M)
    nl = M // tk
    in_specs = [pl.BlockSpec((tk, kk), lambda s, kc, l: (l, kc)), pl.BlockSpec((tk, Ns), lambda s, kc, l: (l, s))]
    out_shape = [jax.ShapeDtypeStruct((S, K, Ns), BF16)]
    out_specs = [pl.BlockSpec((None, kk, Ns), lambda s, kc, l: (s, kc, 0))]
    if bias:
        out_shape.append(jax.ShapeDtypeStruct((1, N), F32))
        out_specs.append(pl.BlockSpec((1, Ns), lambda s, kc, l: (0, s)))

    def body(x_ref, dy_ref, o_ref, *rest):
        acc_ref = rest[-1]
        kc, l = pl.program_id(1), pl.program_id(2)

        @pl.when(l == 0)
        def _():
            acc_ref[...] = jnp.zeros_like(acc_ref)

        acc_ref[...] += _dot(x_ref[...].astype(BF16), dy_ref[...].astype(BF16), TN)
        if bias:
            b_ref = rest[0]

            @pl.when((kc == 0) & (l == 0))
            def _():
                b_ref[...] = jnp.zeros_like(b_ref)

            @pl.when(kc == 0)
            def _():
                b_ref[...] += jnp.sum(dy_ref[...].astype(F32), axis=0, keepdims=True)

        @pl.when(l == nl - 1)
        def _():
            o_ref[...] = acc_ref[...].astype(BF16)

    return _call(body, name=name, grid=(S, K // kk, nl), in_specs=in_specs, out_specs=out_specs, out_shape=out_shape,
                 scratch=[pltpu.VMEM((kk, Ns), F32)], sem=("arbitrary", "arbitrary", "arbitrary"))(x, dy)


def _band_bias(max_dist, has_prev):
    rows = lax.broadcasted_iota(jnp.int32, (BLK, 2 * BLK), 0)
    cols = lax.broadcasted_iota(jnp.int32, (BLK, 2 * BLK), 1)
    dist = rows - cols + BLK
    ok = (dist >= 0) & (dist <= max_dist) & ((cols >= BLK) | has_prev)
    return jnp.where(ok, 0.0, NEG)


Q_SCALE = HEAD_DIM ** -0.5


def _band_fwd(qa, ka, va, *, d, nq, nkv, qcol, kcol, vcol, max_dist, sinks=None, name):
    Lr = qa.shape[0]
    nb = Lr // BLK
    qw, kw, G = nq * HEAD_DIM, nkv * HEAD_DIM, nq // nkv
    cur = lambda colf, w: pl.BlockSpec((BLK, w), lambda r, i: (i, colf(r)))
    prv = lambda colf, w: pl.BlockSpec((BLK, w), lambda r, i: (jnp.maximum(i - 1, 0), colf(r)))
    out = pl.BlockSpec((BLK, qw), lambda r, i: (i, r))
    ins, specs = [qa, ka, ka, va, va], [cur(qcol, qw), cur(kcol, kw), prv(kcol, kw), cur(vcol, kw), prv(vcol, kw)]
    has_sinks = sinks is not None
    if has_sinks:
        ins.append(sinks); specs.append(pl.BlockSpec(memory_space=pltpu.SMEM))

    def body(*refs):
        q_ref, kc_ref, kp_ref, vc_ref, vp_ref = refs[:5]
        sk_ref = refs[5] if has_sinks else None
        o_ref, lse_ref = refs[-2], refs[-1]
        bias = _band_bias(max_dist, pl.program_id(1) > 0)
        k2 = jnp.concatenate([kp_ref[...], kc_ref[...]], axis=0)
        v2 = jnp.concatenate([vp_ref[...], vc_ref[...]], axis=0)
        for h in range(nq):
            hs = slice(h * HEAD_DIM, (h + 1) * HEAD_DIM)
            ks = slice((h // G) * HEAD_DIM, (h // G + 1) * HEAD_DIM)
            s = _dot(q_ref[:, hs] * jnp.asarray(Q_SCALE, BF16), k2[:, ks], NT) + bias
            m = jnp.max(s, axis=-1, keepdims=True)
            if has_sinks:
                m = jnp.maximum(m, sk_ref[h])
            p = jnp.exp(s - m)
            l = jnp.sum(p, axis=-1, keepdims=True)
            if has_sinks:
                l = l + jnp.exp(sk_ref[h] - m)
            o_ref[:, hs] = (_dot(p.astype(BF16), v2[:, ks]) / l).astype(BF16)
            lse_ref[:, hs] = jnp.broadcast_to(m + jnp.log(l), (BLK, HEAD_DIM))

    return _call(body, name=name, grid=(d, nb), in_specs=specs, out_specs=[out, out],
                 out_shape=[jax.ShapeDtypeStruct((Lr, d * qw), BF16), jax.ShapeDtypeStruct((Lr, d * qw), F32)],
                 sem=("parallel", "parallel"))(*ins)


def _band_bwd(qa, ka, va, doa, oa, lsea, *, d, nq, nkv, qcol, kcol, vcol, docol, max_dist, sinks=None, name):
    Lr = qa.shape[0]
    nb = Lr // BLK
    qw, kw, G = nq * HEAD_DIM, nkv * HEAD_DIM, nq // nkv
    transposed = G > 1
    last = lambda i: jnp.minimum(i, nb - 1)
    cur = lambda colf, w: pl.BlockSpec((BLK, w), lambda r, i: (last(i), colf(r)))
    prv = lambda colf, w: pl.BlockSpec((BLK, w), lambda r, i: (jnp.maximum(last(i) - 1, 0), colf(r)))
    own = lambda r: r
    ins = [qa, ka, ka, va, va, doa, oa, lsea]
    specs = [cur(qcol, qw), cur(kcol, kw), prv(kcol, kw), cur(vcol, kw), prv(vcol, kw), cur(docol, qw), cur(own, qw),
             cur(own, qw)]
    has_sinks = sinks is not None
    if has_sinks:
        ins.append(sinks); specs.append(pl.BlockSpec(memory_space=pltpu.SMEM))
    out_shape = [jax.ShapeDtypeStruct((Lr, d * qw), F32), jax.ShapeDtypeStruct((Lr, d * kw), F32),
                 jax.ShapeDtypeStruct((Lr, d * kw), F32)]
    behind = lambda r, i: (jnp.maximum(i - 1, 0), r)
    out_specs = [pl.BlockSpec((BLK, qw), lambda r, i: (last(i), r)), pl.BlockSpec((BLK, kw), behind),
                 pl.BlockSpec((BLK, kw), behind)]
    if has_sinks:
        out_shape.append(jax.ShapeDtypeStruct((8, 128), F32))
        out_specs.append(pl.BlockSpec((8, 128), lambda r, i: (0, 0)))

    def body(*refs):
        it = iter(refs)
        q_ref, kc_ref, kp_ref, vc_ref, vp_ref, do_ref, o_ref, ls_ref = (next(it) for _ in range(8))
        sk_ref = next(it) if has_sinks else None
        dq_ref, dk_ref, dv_ref = next(it), next(it), next(it)
        dsk_ref = next(it) if has_sinks else None
        dk_car, dv_car = next(it), next(it)
        r_id, i = pl.program_id(0), pl.program_id(1)

        @pl.when(i == 0)
        def _():
            dk_car[...] = jnp.zeros_like(dk_car)
            dv_car[...] = jnp.zeros_like(dv_car)

        if has_sinks:
            @pl.when((r_id == 0) & (i == 0))
            def _():
                dsk_ref[...] = jnp.zeros_like(dsk_ref)

        @pl.when(i == nb)
        def _():
            dk_ref[...] = dk_car[...]
            dv_ref[...] = dv_car[...]

        @pl.when(i < nb)
        def _():
            bias = _band_bias(max_dist, i > 0)
            k2 = jnp.concatenate([kp_ref[...], kc_ref[...]], axis=0)
            v2 = jnp.concatenate([vp_ref[...], vc_ref[...]], axis=0)
            if has_sinks:
                lane = lax.broadcasted_iota(jnp.int32, (8, 128), 1)
                dsk = jnp.zeros((8, 128), F32)
            for kv in range(nkv):
                ks = slice(kv * HEAD_DIM, (kv + 1) * HEAD_DIM)
                kh, vh = k2[:, ks], v2[:, ks]
                shape = (HEAD_DIM, 2 * BLK) if transposed else (2 * BLK, HEAD_DIM)
                dk, dv = jnp.zeros(shape, F32), jnp.zeros(shape, F32)
                for g in range(G):
                    h = kv * G + g
                    hs = slice(h * HEAD_DIM, (h + 1) * HEAD_DIM)
                    q = q_ref[:, hs] * jnp.asarray(Q_SCALE, BF16)
                    do = do_ref[:, hs]
                    lse = ls_ref[:, h * HEAD_DIM:h * HEAD_DIM + 1]
                    dl = jnp.sum(do.astype(F32) * o_ref[:, hs].astype(F32), axis=-1, keepdims=True)
                    p = jnp.exp(_dot(q, kh, NT) + bias - lse)
                    ds = (p * (_dot(do, vh, NT) - dl)).astype(BF16)
                    dq_ref[:, hs] = _dot(ds, kh) * Q_SCALE
                    if transposed:
                        dk = dk + _dot(q, ds, TN)
                        dv = dv + _dot(do, p.astype(BF16), TN)
                    else:
                        dk = dk + _dot(ds, q, TN)
                        dv = dv + _dot(p.astype(BF16), do, TN)
                    if has_sinks:
                        val = -jnp.sum(jnp.exp(sk_ref[h] - lse) * dl, axis=0, keepdims=True)
                        dsk = dsk + jnp.where(lane == h, val, 0.0)
                if transposed:
                    dk, dv = dk.T, dv.T
                dk_ref[:, ks] = dk_car[:, ks] + dk[:BLK]
                dv_ref[:, ks] = dv_car[:, ks] + dv[:BLK]
                dk_car[:, ks] = dk[BLK:]
                dv_car[:, ks] = dv[BLK:]
            if has_sinks:
                dsk_ref[...] += dsk

    return _call(body, name=name, grid=(d, nb + 1), in_specs=specs, out_specs=out_specs, out_shape=out_shape,
                 scratch=[pltpu.VMEM((BLK, kw), F32), pltpu.VMEM((BLK, kw), F32)], sem=("arbitrary", "arbitrary"))(*ins)


def _attn_grad_combine(branches, tabs, *, name, tm=256):
    L, qw = branches[0][0].shape
    kw = branches[0][1].shape[1]
    nbr = len(branches)
    row = lambda w: pl.BlockSpec((tm, w), lambda i: (i, 0))
    ins, specs = [], []
    for dq, dk, dv in branches:
        ins += [dq, dk, dv]; specs += [row(qw), row(kw), row(kw)]
    ins += list(tabs); specs += [row(128)] * 3

    def body(*refs):
        c, s1, s2 = (t[...] for t in refs[3 * nbr:3 * nbr + 3])
        o_ref = refs[-1]
        for part, (w, off, rot) in enumerate(((qw, 0, True), (kw, qw, True), (kw, qw + kw, False))):
            for cb in range(w // 128):
                cs = slice(cb * 128, (cb + 1) * 128)
                v = refs[part][:, cs]
                for b in range(1, nbr):
                    v = v + refs[3 * b + part][:, cs]
                if rot:
                    v = _rope_bwd(v, c, s1, s2)
                o_ref[:, off + cb * 128:off + (cb + 1) * 128] = v.astype(BF16)

    return _call(body, name=name, grid=(L // tm,), in_specs=specs, out_specs=row(qw + 2 * kw),
                 out_shape=jax.ShapeDtypeStruct((L, qw + 2 * kw), BF16), sem=("parallel",))(*ins)


def _xattn_fwd(q, kv, *, name, tq=512):
    L, W = q.shape
    scale = XA_HEAD_DIM ** -0.5
    row = pl.BlockSpec((tq, W), lambda i: (i, 0))
    kvs = pl.BlockSpec((N_MEM, 2 * W), lambda i: (0, 0))

    def body(q_ref, kv_ref, o_ref, lse_ref):
        for h in range(XA_HEADS):
            hs = slice(h * XA_HEAD_DIM, (h + 1) * XA_HEAD_DIM)
            vs = slice(W + h * XA_HEAD_DIM, W + (h + 1) * XA_HEAD_DIM)
            s = _dot(q_ref[:, hs], kv_ref[:, hs], NT) * scale
            m = jnp.max(s, axis=-1, keepdims=True)
            p = jnp.exp(s - m)
            l = jnp.sum(p, axis=-1, keepdims=True)
            o_ref[:, hs] = (_dot(p.astype(BF16), kv_ref[:, vs]) / l).astype(BF16)
            lse_ref[:, hs] = jnp.broadcast_to(m + jnp.log(l), (tq, XA_HEAD_DIM))

    return _call(body, name=name, grid=(L // tq,), in_specs=[row, kvs], out_specs=[row, row],
                 out_shape=[jax.ShapeDtypeStruct((L, W), BF16), jax.ShapeDtypeStruct((L, W), F32)], sem=("parallel",))(q, kv)


def _xattn_bwd(q, kv, o, lse, do, *, name, tq=512):
    L, W = q.shape
    scale = XA_HEAD_DIM ** -0.5
    row = pl.BlockSpec((tq, W), lambda i: (i, 0))
    kvs = pl.BlockSpec((N_MEM, 2 * W), lambda i: (0, 0))

    def body(q_ref, kv_ref, o_ref, lse_ref, do_ref, dq_ref, dkv_ref):
        @pl.when(pl.program_id(0) == 0)
        def _():
            dkv_ref[...] = jnp.zeros_like(dkv_ref)

        for h in range(XA_HEADS):
            hs = slice(h * XA_HEAD_DIM, (h + 1) * XA_HEAD_DIM)
            vs = slice(W + h * XA_HEAD_DIM, W + (h + 1) * XA_HEAD_DIM)
            qh, kh, vh, doh = q_ref[:, hs], kv_ref[:, hs], kv_ref[:, vs], do_ref[:, hs]
            p = jnp.exp(_dot(qh, kh, NT) * scale - lse_ref[:, h * XA_HEAD_DIM:h * XA_HEAD_DIM + 1])
            dl = jnp.sum(doh.astype(F32) * o_ref[:, hs].astype(F32), axis=-1, keepdims=True)
            ds = (p * (_dot(doh, vh, NT) - dl) * scale).astype(BF16)
            dq_ref[:, hs] = _dot(ds, kh).astype(BF16)
            dkv_ref[:, hs] += _dot(ds, qh, TN)
            dkv_ref[:, vs] += _dot(p.astype(BF16), doh, TN)

    return _call(body, name=name, grid=(L // tq,), in_specs=[row, kvs, row, row, row], out_specs=[row, kvs],
                 out_shape=[jax.ShapeDtypeStruct((L, W), BF16), jax.ShapeDtypeStruct((N_MEM, 2 * W), F32)],
                 sem=("arbitrary",))(q, kv, o, lse, do)


def _neg_expm1(z):
    series = -(z * (1.0 + z * (0.5 + z * (1.0 / 6.0 + z * (1.0 / 24.0 + z * (1.0 / 120.0))))))
    return jnp.where(z > -0.05, series, 1.0 - jnp.exp(z))


def _softplus(z):
    return jnp.maximum(z, 0.0) + jnp.log(1.0 + jnp.exp(-jnp.abs(z)))


def _gelu_parts(y):
    c = 0.7978845608028654
    t = jnp.tanh(c * (y + 0.044715 * y * y * y))
    gy = 0.5 * y * (1.0 + t)
    dgy = 0.5 * (1.0 + t) + 0.5 * y * (1.0 - t * t) * c * (1.0 + 3.0 * 0.044715 * y * y)
    return gy, dgy


def _lru_gates(xc, wa_ref, ba, wx_ref, bx, sp):
    rs, igs = [], []
    for hd in range(LRU_HEADS):
        sl = slice(hd * LRU_HEAD_DIM, (hd + 1) * LRU_HEAD_DIM)
        xh = xc[:, sl].astype(BF16)
        rs.append(_sigmoid(_dot(xh, wa_ref[hd]) + ba[:, sl]))
        igs.append(_sigmoid(_dot(xh, wx_ref[hd]) + bx[:, sl]))
    r, ig = jnp.concatenate(rs, axis=1), jnp.concatenate(igs, axis=1)
    la = -LRU_C * r * sp
    return r, ig, jnp.exp(la), _neg_expm1(2.0 * la)


def _conv_taps(x_ext, halo):
    n = x_ext.shape[0]
    return [x_ext[halo:] if k == CONV_WIDTH - 1 else pltpu.roll(x_ext, CONV_WIDTH - 1 - k, 0)[halo:]
            for k in range(CONV_WIDTH)]


def _lru_fwd(proj, cw, cb, wa, ba, wx, bx, lam, *, name, tc=512):
    L = proj.shape[0]
    W = LRU_HEADS * LRU_HEAD_DIM
    nb = L // tc
    whole = lambda shape: pl.BlockSpec(shape, lambda i: (0,) * len(shape))
    specs = [pl.BlockSpec((tc, W), lambda i: (i, 0)), pl.BlockSpec((tc, W), lambda i: (i, 1)),
             pl.BlockSpec((16, W), lambda i: (jnp.maximum(i * (tc // 16) - 1, 0), 0)),
             whole((CONV_WIDTH, W)), whole((1, W)), whole((LRU_HEADS, LRU_HEAD_DIM, LRU_HEAD_DIM)), whole((1, W)),
             whole((LRU_HEADS, LRU_HEAD_DIM, LRU_HEAD_DIM)), whole((1, W)), whole((1, W))]
    out_specs = [pl.BlockSpec((tc, W), lambda i: (i, 0))] * 2
    out_shape = [jax.ShapeDtypeStruct((L, W), BF16), jax.ShapeDtypeStruct((L, W), F32)]

    def body(x_ref, y_ref, xh_ref, cw_ref, cb_ref, wa_ref, ba_ref, wx_ref, bx_ref, lam_ref, rec_ref, hs_ref,
             hcar, a_scr, b_scr):
        i = pl.program_id(0)

        @pl.when(i == 0)
        def _():
            hcar[...] = jnp.zeros_like(hcar)

        halo = jnp.where(i > 0, xh_ref[...].astype(F32), 0.0)
        taps = _conv_taps(jnp.concatenate([halo, x_ref[...].astype(F32)], axis=0), 16)
        xc = cb_ref[...] + sum(cw_ref[k:k + 1, :] * taps[k] for k in range(CONV_WIDTH))
        _, ig, a, om = _lru_gates(xc, wa_ref, ba_ref[...], wx_ref, bx_ref[...], _softplus(-lam_ref[...]))
        b = jnp.sqrt(om) * (ig * xc)
        rowmod = lax.broadcasted_iota(jnp.int32, (tc, W), 0) & 7
        for s in (1, 2, 4):
            keep = rowmod >= s
            b = jnp.where(keep, a * pltpu.roll(b, s, 0) + b, b)
            a = jnp.where(keep, a * pltpu.roll(a, s, 0), a)
        a_scr[...] = a
        b_scr[...] = b

        def tile(j, hc):
            rows = pl.ds(pl.multiple_of(j * 8, 8), 8)
            ht = a_scr[rows, :] * hc + b_scr[rows, :]
            hs_ref[rows, :] = ht
            return jnp.broadcast_to(ht[7:8, :], (8, W))

        hcar[...] = lax.fori_loop(0, tc // 8, tile, hcar[...])
        gy, _ = _gelu_parts(y_ref[...].astype(F32))
        rec_ref[...] = (hs_ref[...] * gy).astype(BF16)

    return _call(body, name=name, grid=(nb,), in_specs=specs, out_specs=out_specs, out_shape=out_shape,
                 scratch=[pltpu.VMEM((8, W), F32), pltpu.VMEM((tc, W), F32), pltpu.VMEM((tc, W), F32)],
                 sem=("arbitrary",))(proj, proj, proj, cw, cb, wa, ba, wx, bx, lam)


def _lru_bwd(proj, hs, drec_src, cw, cb, wa, ba, wx, bx, lam, *, name, tc=256):
    L = proj.shape[0]
    W = LRU_HEADS * LRU_HEAD_DIM
    nb = L // tc
    tb = lambda i: nb - 1 - i
    whole = lambda shape: pl.BlockSpec(shape, lambda i: (0,) * len(shape))
    gate_w = (LRU_HEADS, LRU_HEAD_DIM, LRU_HEAD_DIM)
    specs = [pl.BlockSpec((tc, W), lambda i: (tb(i), 0)), pl.BlockSpec((tc, W), lambda i: (tb(i), 1)),
             pl.BlockSpec((16, W), lambda i: (jnp.maximum(tb(i) * (tc // 16) - 1, 0), 0)),
             pl.BlockSpec((tc, W), lambda i: (tb(i), 0)),
             pl.BlockSpec((8, W), lambda i: (jnp.maximum(tb(i) * (tc // 8) - 1, 0), 0)),
             pl.BlockSpec((tc, W), lambda i: (tb(i), 0)),
             whole((CONV_WIDTH, W)), whole((1, W)), whole(gate_w), whole((1, W)), whole(gate_w), whole((1, W)), whole((1, W))]
    out_specs = [pl.BlockSpec((tc, 2 * W), lambda i: (tb(i), 0)), whole((CONV_WIDTH, W)), whole((1, W)), whole(gate_w),
                 whole((1, W)), whole(gate_w), whole((1, W)), whole((1, W))]
    vec = jax.ShapeDtypeStruct((1, W), F32)
    out_shape = [jax.ShapeDtypeStruct((L, 2 * W), BF16), jax.ShapeDtypeStruct((CONV_WIDTH, W), F32), vec,
                 jax.ShapeDtypeStruct(gate_w, F32), vec, jax.ShapeDtypeStruct(gate_w, F32), vec, vec]

    def body(x_ref, y_ref, xh_ref, hs_ref, hh_ref, dr_ref, cw_ref, cb_ref, wa_ref, ba_ref, wx_ref, bx_ref, lam_ref,
             dxy_ref, dcw_ref, dcb_ref, dwa_ref, dba_ref, dwx_ref, dbx_ref, dlam_ref, gcar, dxc_car, a_scr, b_scr, g_scr):
        pid = pl.program_id(0)
        t = tb(pid)
        accs = (dcw_ref, dcb_ref, dwa_ref, dba_ref, dwx_ref, dbx_ref, dlam_ref)

        @pl.when(pid == 0)
        def _():
            gcar[...] = jnp.zeros_like(gcar)
            dxc_car[...] = jnp.zeros_like(dxc_car)
            for r in accs:
                r[...] = jnp.zeros_like(r)

        halo = jnp.where(t > 0, xh_ref[...].astype(F32), 0.0)
        taps = _conv_taps(jnp.concatenate([halo, x_ref[...].astype(F32)], axis=0), 16)
        xc = cb_ref[...] + sum(cw_ref[k:k + 1, :] * taps[k] for k in range(CONV_WIDTH))
        lam = lam_ref[...]
        sp = _softplus(-lam)
        r, ig, a, om = _lru_gates(xc, wa_ref, ba_ref[...], wx_ref, bx_ref[...], sp)
        sq = jnp.sqrt(om)
        hblk = hs_ref[...]
        hprev = pltpu.roll(jnp.concatenate([jnp.where(t > 0, hh_ref[...], 0.0), hblk], axis=0), 1, 0)[8:]
        gy, dgy = _gelu_parts(y_ref[...].astype(F32))
        drec = dr_ref[...].astype(F32)
        dxy_ref[:, W:] = (drec * hblk * dgy).astype(BF16)

        rowidx = lax.broadcasted_iota(jnp.int32, (tc, W), 0)
        rowmod = rowidx & 7
        ca = jnp.where(rowidx == tc - 1, 1.0, pltpu.roll(a, tc - 1, 0))
        cbv = drec * gy
        for s in (1, 2, 4):
            keep = rowmod < 8 - s
            cbv = jnp.where(keep, ca * pltpu.roll(cbv, tc - s, 0) + cbv, cbv)
            ca = jnp.where(keep, ca * pltpu.roll(ca, tc - s, 0), ca)
        a_scr[...] = ca
        b_scr[...] = cbv

        def tile(k, gc):
            j = tc // 8 - 1 - k
            rows = pl.ds(pl.multiple_of(j * 8, 8), 8)
            gt = a_scr[rows, :] * gc + b_scr[rows, :]
            g_scr[rows, :] = gt
            return jnp.broadcast_to(gt[0:1, :], (8, W))

        lax.fori_loop(0, tc // 8, tile, gcar[...])
        G = g_scr[...]
        gcar[...] = jnp.broadcast_to(a[0:1, :] * G[0:1, :], (8, W))

        da = G * hprev
        dsq = G * (ig * xc)
        di = G * (sq * xc)
        dxc = G * (sq * ig)
        dla = da * a - 2.0 * a * a * (dsq * 0.5 * lax.rsqrt(om))
        dlam_ref[...] += jnp.sum(dla * (-LRU_C * r), axis=0, keepdims=True) * (-_sigmoid(-lam))
        dpr = dla * (-LRU_C * sp) * r * (1.0 - r)
        dpi = di * ig * (1.0 - ig)
        dba_ref[...] += jnp.sum(dpr, axis=0, keepdims=True)
        dbx_ref[...] += jnp.sum(dpi, axis=0, keepdims=True)
        back = []
        for hd in range(LRU_HEADS):
            sl = slice(hd * LRU_HEAD_DIM, (hd + 1) * LRU_HEAD_DIM)
            xh, dprh, dpih = xc[:, sl].astype(BF16), dpr[:, sl].astype(BF16), dpi[:, sl].astype(BF16)
            back.append(_dot(dprh, wa_ref[hd], NT) + _dot(dpih, wx_ref[hd], NT))
            dwa_ref[hd] += _dot(xh, dprh, TN)
            dwx_ref[hd] += _dot(xh, dpih, TN)
        dxc = dxc + jnp.concatenate(back, axis=1)
        dcb_ref[...] += jnp.sum(dxc, axis=0, keepdims=True)
        for k in range(CONV_WIDTH):
            dcw_ref[k:k + 1, :] += jnp.sum(dxc * taps[k], axis=0, keepdims=True)
        ext = jnp.concatenate([dxc, dxc_car[...]], axis=0)
        dx = cw_ref[CONV_WIDTH - 1:CONV_WIDTH, :] * dxc
        for k in range(CONV_WIDTH - 1):
            dx = dx + cw_ref[k:k + 1, :] * pltpu.roll(ext, tc + 8 - (CONV_WIDTH - 1 - k), 0)[:tc]
        dxc_car[...] = dxc[0:8, :]
        dxy_ref[:, :W] = dx.astype(BF16)

    scratch = [pltpu.VMEM((8, W), F32), pltpu.VMEM((8, W), F32)] + [pltpu.VMEM((tc, W), F32)] * 3
    return _call(body, name=name, grid=(nb,), in_specs=specs, out_specs=out_specs, out_shape=out_shape, scratch=scratch,
                 sem=("arbitrary",))(proj, proj, proj, hs, hs, drec_src, cw, cb, wa, ba, wx, bx, lam)


def _final_loss(h, gain, target, *, name, tm=256):
    M, K = h.shape
    row = pl.BlockSpec((tm, K), lambda i: (i, 0))
    vec = pl.BlockSpec((1, K), lambda i: (0, 0))
    one = pl.BlockSpec((1, 128), lambda i: (0, 0))

    def body(h_ref, g_ref, t_ref, dh_ref, dg_ref, loss_ref):
        @pl.when(pl.program_id(0) == 0)
        def _():
            dg_ref[...] = jnp.zeros_like(dg_ref)
            loss_ref[...] = jnp.zeros_like(loss_ref)

        x = h_ref[...]
        r = lax.rsqrt(jnp.mean(x * x, axis=-1, keepdims=True) + NORM_EPS)
        xhat = x * r
        err = xhat * g_ref[...] - t_ref[...]
        loss_ref[...] += 0.5 / K * jnp.sum(err * err)
        dy = err * (1.0 / K)
        dg_ref[...] += jnp.sum(dy * xhat, axis=0, keepdims=True)
        dxh = dy * g_ref[...]
        dh_ref[...] = r * (dxh - xhat * jnp.mean(dxh * xhat, axis=-1, keepdims=True))

    return _call(body, name=name, grid=(M // tm,), in_specs=[row, vec, row], out_specs=[row, vec, one],
                 out_shape=[jax.ShapeDtypeStruct((M, K), F32), jax.ShapeDtypeStruct((1, K), F32),
                            jax.ShapeDtypeStruct((1, 128), F32)], sem=("arbitrary",))(h, gain.reshape(1, K), target)


def _dilated_merge(branches, *, name, tm=512):
    L, W = branches[0].shape
    nbr = len(branches) // 2
    row = pl.BlockSpec((tm, W), lambda i: (i, 0))

    def body(*refs):
        o_ref, lse_ref = refs[-2], refs[-1]
        lses = [refs[2 * b + 1][...] for b in range(nbr)]
        m = lses[0]
        for t in lses[1:]:
            m = jnp.maximum(m, t)
        ws = [jnp.exp(t - m) for t in lses]
        den = ws[0]
        for t in ws[1:]:
            den = den + t
        acc = ws[0] * refs[0][...].astype(F32)
        for b in range(1, nbr):
            acc = acc + ws[b] * refs[2 * b][...].astype(F32)
        o_ref[...] = (acc / den).astype(BF16)
        lse_ref[...] = m + jnp.log(den)

    return _call(body, name=name, grid=(L // tm,), in_specs=[row] * (2 * nbr), out_specs=[row, row],
                 out_shape=[jax.ShapeDtypeStruct((L, W), BF16), jax.ShapeDtypeStruct((L, W), F32)], sem=("parallel",))(*branches)


def _dilated_fwd(proj0):
    L = proj0.shape[0]
    qkv = proj0[:, 2 * D_MODEL:]
    W = B_HEADS * HEAD_DIM
    outs = []
    for window, d in DILATED_PATTERN:
        view = qkv.reshape(L // d, d * 3 * W)
        o, lse = _band_fwd(view, view, view, d=d, nq=B_HEADS, nkv=B_HEADS, qcol=lambda r: 3 * r, kcol=lambda r: 3 * r + 1,
                           vcol=lambda r: 3 * r + 2, max_dist=window // d, name=f"dilated_fwd_d{d}")
        outs += [o.reshape(L, W), lse.reshape(L, W)]
    return _dilated_merge(outs, name="dilated_merge")


def _dilated_bwd(proj0, att, lse, datt, tabs):
    L = proj0.shape[0]
    qkv = proj0[:, 2 * D_MODEL:]
    Wh = B_HEADS * HEAD_DIM
    branches = []
    for window, d in DILATED_PATTERN:
        view = qkv.reshape(L // d, d * 3 * Wh)
        v1 = lambda t: t.reshape(L // d, d * Wh)
        outs = _band_bwd(view, view, view, v1(datt), v1(att), v1(lse), d=d, nq=B_HEADS, nkv=B_HEADS,
                         qcol=lambda r: 3 * r, kcol=lambda r: 3 * r + 1, vcol=lambda r: 3 * r + 2, docol=lambda r: r,
                         max_dist=window // d, name=f"dilated_bwd_d{d}")
        branches.append([o.reshape(L, Wh) for o in outs])
    return _attn_grad_combine(branches, tabs, name="dilated_grad_combine")


def _device_step(x, mem, target, w, on_grads=None):
    L = x.shape[0]
    tabs = _rope_tables(L)
    g = {}
    saved = []
    h = x
    for layer in range(2):
        sv = {"h_mix": h}
        if layer == 0:
            proj, n = _rowmm(h, w["ab_w_in"], name="l0_in_proj", gain=w["mix_norm"][0],
                             rope=(2 * D_MODEL, 2 * D_MODEL + 2 * B_HEADS * HEAD_DIM, tabs))
            rec, hs = _lru_fwd(proj, w["lru_conv_w"], w["lru_conv_b"], w["lru_wa"], w["lru_ba"], w["lru_wx"], w["lru_bx"],
                               w["lru_lambda"], name="lru_fwd")
            att, lse = _dilated_fwd(proj)
            mix = jnp.concatenate([rec, att], axis=1)
            (h,) = _rowmm(mix, w["ab_w_out"], name="l0_out_proj", res=h)
            sv.update(hs=hs)
        else:
            proj, n = _rowmm(h, w["c_w_qkv"], name="l1_qkv_proj", gain=w["mix_norm"][1], bias=w["c_b_qkv"],
                             rope=(0, (C_HEADS + C_KV_HEADS) * HEAD_DIM, tabs))
            mix, lse = _band_fwd(proj, proj, proj, d=1, nq=C_HEADS, nkv=C_KV_HEADS, qcol=lambda r: 0, kcol=lambda r: 8,
                                 vcol=lambda r: 9, max_dist=C_WINDOW - 1, sinks=w["c_sinks"], name="swa_fwd")
            (h,) = _rowmm(mix, w["c_w_out"], name="l1_out_proj", res=h, bias=w["c_b_out"])
        sv.update(proj=proj, n_mix=n, mix=mix, lse=lse, h_xa=h)
        xq, nx = _rowmm(h, w["xa_wq"][layer][None], name=f"xa_q_proj{layer}", gain=w["xa_norm"][layer])
        kv, nm = _rowmm(mem, w["xa_wkv"][layer][None], name=f"xa_kv_proj{layer}", gain=w["xa_mem_norm"][layer])
        xo, xlse = _xattn_fwd(xq, kv, name=f"xa_fwd{layer}")
        (h,) = _rowmm(xo, w["xa_wo"][layer], name=f"xa_out_proj{layer}", res=h)
        sv.update(xq=xq, nx=nx, kv=kv, nm=nm, xo=xo, xlse=xlse, h_ffn=h)
        gu, nf, act = _rowmm(h, w["ffn_w_gate_up"], layer=layer, name=f"ffn_in{layer}", gain=w["ffn_norm"][layer], swiglu=True)
        (h,) = _rowmm(act, w["ffn_w_down"][layer][None], name=f"ffn_out{layer}", res=h, tm=512)
        sv.update(gu=gu, nf=nf, act=act)
        saved.append(sv)

    dh, g["final_norm"], loss = _final_loss(h, w["final_norm"], target, name="final_loss")

    stk = {k: [None, None] for k in ("xa_norm", "xa_mem_norm", "ffn_norm", "mix_norm")}
    after = None
    for layer in (1, 0):
        sv = saved[layer]
        (g["ffn_w_down", layer],) = _mm_tn(sv["act"], dh, S=1, name=f"ffn_down_dw{layer}", kk=D_FF // 2)
        (dgu,) = _mm_nt(dh, w["ffn_w_down"][layer][None], name=f"ffn_dact{layer}", mode="swiglu", kchunk=D_FF // 2, gu=sv["gu"],
                        after=after)
        (g["ffn_w_gate_up", layer],) = _mm_tn(sv["nf"], dgu, S=N_CHIPS, name=f"ffn_gu_dw{layer}")
        dh, stk["ffn_norm"][layer] = _mm_nt(dgu, w["ffn_w_gate_up"], layer=layer, name=f"ffn_dx{layer}", mode="norm",
                                            h=sv["h_ffn"], gain=w["ffn_norm"][layer], dh=dh)
        (g["xa_wo", layer],) = _mm_tn(sv["xo"], dh, S=N_CHIPS, name=f"xa_wo_dw{layer}")
        (dxo,) = _mm_nt(dh, w["xa_wo"][layer], name=f"xa_dxo{layer}", mode="plain")
        dxq, dkv = _xattn_bwd(sv["xq"], sv["kv"], sv["xo"], sv["xlse"], dxo, name=f"xa_bwd{layer}")
        (g["xa_wq", layer],) = _mm_tn(sv["nx"], dxq, S=1, name=f"xa_wq_dw{layer}")
        dh, stk["xa_norm"][layer] = _mm_nt(dxq, w["xa_wq"][layer][None], name=f"xa_dx{layer}", mode="norm", h=sv["h_xa"],
                                           gain=w["xa_norm"][layer], dh=dh)
        (g["xa_wkv", layer],) = _mm_tn(sv["nm"], dkv, S=1, name=f"xa_wkv_dw{layer}")
        _, stk["xa_mem_norm"][layer] = _mm_nt(dkv, w["xa_wkv"][layer][None], name=f"xa_dmem{layer}", mode="norm", h=mem,
                                              gain=w["xa_mem_norm"][layer])
        if layer == 1:
            g["c_w_out"], g["c_b_out"] = _mm_tn(sv["mix"], dh, S=1, name="l1_out_dw", bias=True)
            (dmix,) = _mm_nt(dh, w["c_w_out"], name="l1_dmix", mode="plain")
            dq, dk, dv, dsk = _band_bwd(sv["proj"], sv["proj"], sv["proj"], dmix, sv["mix"], sv["lse"], d=1, nq=C_HEADS,
                                        nkv=C_KV_HEADS, qcol=lambda r: 0, kcol=lambda r: 8, vcol=lambda r: 9,
                                        docol=lambda r: 0, max_dist=C_WINDOW - 1, sinks=w["c_sinks"], name="swa_bwd")
            g["c_sinks"] = dsk[0, :C_HEADS]
            dproj = _attn_grad_combine([(dq, dk, dv)], tabs, name="swa_grad_combine")
            g["c_w_qkv"], g["c_b_qkv"] = _mm_tn(sv["n_mix"], dproj, S=1, name="l1_qkv_dw", bias=True)
            dh, stk["mix_norm"][1] = _mm_nt(dproj, w["c_w_qkv"], name="l1_dx", mode="norm", h=sv["h_mix"],
                                            gain=w["mix_norm"][1], dh=dh)
            if on_grads is not None:
                after = on_grads("layer1", g)
        else:
            if on_grads is not None:
                after = on_grads("layer0_ffn_xa", g)
            (g["ab_w_out"],) = _mm_tn(sv["mix"], dh, S=1, name="l0_out_dw", kk=768)
            (dmix,) = _mm_nt(dh, w["ab_w_out"], name="l0_dmix", mode="plain", kchunk=768, after=after)
            (dxy, g["lru_conv_w"], g["lru_conv_b"], g["lru_wa"], g["lru_ba"], g["lru_wx"], g["lru_bx"],
             g["lru_lambda"]) = _lru_bwd(sv["proj"], sv["hs"], dmix, w["lru_conv_w"], w["lru_conv_b"], w["lru_wa"],
                                         w["lru_ba"], w["lru_wx"], w["lru_bx"], w["lru_lambda"], name="lru_bwd")
            dqkv = _dilated_bwd(sv["proj"], sv["mix"][:, D_MODEL:], sv["lse"], dmix[:, D_MODEL:], tabs)
            dproj = jnp.concatenate([dxy, dqkv], axis=1)
            (g["ab_w_in"],) = _mm_tn(sv["n_mix"], dproj, S=N_CHIPS, name="l0_in_dw")
            dh, stk["mix_norm"][0] = _mm_nt(dproj, w["ab_w_in"], name="l0_dx", mode="norm", h=sv["h_mix"],
                                            gain=w["mix_norm"][0], dh=dh)
    for k, v in stk.items():
        g[k] = jnp.concatenate(v, axis=0)
    return loss[0, 0], dh, g


ANY = pl.BlockSpec(memory_space=pl.ANY)
MESH = pl.DeviceIdType.MESH


def _place():
    x, y, c = lax.axis_index("x"), lax.axis_index("y"), lax.axis_index("c")
    return x, y, c, [(1 - x, y), (x, 1 - y), (1 - x, 1 - y)]


def _remote(send_sems, recv_sems):
    def copy(k, src, dst, to):
        return pltpu.make_async_remote_copy(src_ref=src, dst_ref=dst, send_sem=send_sems.at[k], recv_sem=recv_sems.at[k],
                                            device_id=to, device_id_type=MESH)
    return copy


def _halves(ref, n_rows):
    rh = n_rows // 2
    return lambda lead, hh: ref.at[(*lead, pl.ds(hh * rh, rh), slice(None))]


def _gather_weights(packs, spack):
    n = len(packs)

    def body(*refs):
        w_refs, s_ref, wf_refs, sf_ref = refs[:n], refs[n], refs[n + 1:2 * n + 1], refs[2 * n + 1]
        x, y, c, chips = _place()
        me, sib = 2 * x + y, (x, y, 1 - c)
        copy = _remote(*refs[-2:])
        src = [_halves(w_refs[g], packs[g].shape[0]) for g in range(n)]
        dst = [_halves(wf_refs[g], packs[g].shape[0]) for g in range(n)]
        sends = []
        for g in range(n):
            for j, (cx, cy) in enumerate(chips):
                sends.append(copy(3 * g + j, src[g]((), c), dst[g]((me,), c), (cx, cy, c)))
        for j, (cx, cy) in enumerate(chips):
            sends.append(copy(6 * n + j, s_ref, sf_ref.at[me], (cx, cy, c)))
        for cp in sends:
            cp.start()
        for g in range(n):
            for j, (cx, cy) in enumerate(chips):
                got = dst[g]((2 * cx + cy,), c)
                copy(3 * g + j, got, got, sib).wait_recv()
                fwd = copy(3 * n + 3 * g + j, got, got, sib)
                fwd.start()
                sends.append(fwd)
        for g in range(n):
            for j, (cx, cy) in enumerate(chips):
                got = dst[g]((2 * cx + cy,), 1 - c)
                copy(3 * n + 3 * g + j, got, got, sib).wait_recv()
        for j, (cx, cy) in enumerate(chips):
            copy(6 * n + j, s_ref, sf_ref.at[2 * cx + cy], sib).wait_recv()
        for cp in sends:
            cp.wait_send()

    ins = list(packs) + [spack]
    out_shape = [jax.ShapeDtypeStruct((N_CHIPS,) + a.shape, a.dtype) for a in ins]
    n_sems = 6 * n + 3
    outs = pl.pallas_call(body, name="gather_weights", out_shape=out_shape, in_specs=[ANY] * len(ins),
                          out_specs=[ANY] * len(ins),
                          scratch_shapes=[pltpu.SemaphoreType.DMA((n_sems,)), pltpu.SemaphoreType.DMA((n_sems,))])(*ins)
    chip = 2 * lax.axis_index("x") + lax.axis_index("y")
    outs = [lax.dynamic_update_index_in_dim(o, a, chip, 0) for o, a in zip(outs, ins)]
    return outs[:n], outs[n]


SEQUENCER_GATHER_IDS = {"mid": 1, "late": 5}


def _gather_weights_behind(packs, *, tag):
    n = len(packs)

    def body(*refs):
        w_refs, wf_refs = refs[:n], refs[n:2 * n]
        x, y, c, chips = _place()
        me, sib = 2 * x + y, (x, y, 1 - c)
        barrier = pltpu.get_barrier_semaphore()
        for peer in [(cx, cy, c) for cx, cy in chips] + [sib]:
            pl.semaphore_signal(barrier, inc=1, device_id=peer, device_id_type=MESH)
        pl.semaphore_wait(barrier, len(chips) + 1)
        copy = _remote(*refs[-2:])
        src = [_halves(w_refs[g], packs[g].shape[0]) for g in range(n)]
        dst = [_halves(wf_refs[g], packs[g].shape[0]) for g in range(n)]
        sends = []
        for g in range(n):
            for j, (cx, cy) in enumerate(chips):
                sends.append(copy(3 * g + j, src[g]((), c), dst[g]((me,), c), (cx, cy, c)))
        for cp in sends:
            cp.start()
        for g in range(n):
            for j, (cx, cy) in enumerate(chips):
                got = dst[g]((2 * cx + cy,), c)
                copy(3 * g + j, got, got, sib).wait_recv()
                fwd = copy(3 * n + 3 * g + j, got, got, sib)
                fwd.start()
                sends.append(fwd)
        for g in range(n):
            for j, (cx, cy) in enumerate(chips):
                got = dst[g]((2 * cx + cy,), 1 - c)
                copy(3 * n + 3 * g + j, got, got, sib).wait_recv()
        for cp in sends:
            cp.wait_send()

    out_type = [jax.ShapeDtypeStruct((N_CHIPS,) + a.shape, a.dtype) for a in packs]
    outs = pl.kernel(body, out_type=out_type, mesh=plsc.ScalarSubcoreMesh(axis_name="sequencer", num_cores=1),
                     name="gather_weights_behind_" + tag,
                     scratch_types=[pltpu.SemaphoreType.DMA((6 * n,)), pltpu.SemaphoreType.DMA((6 * n,))],
                     compiler_params=pltpu.CompilerParams(collective_id=SEQUENCER_GATHER_IDS[tag]))(*packs)
    chip = 2 * lax.axis_index("x") + lax.axis_index("y")
    return [lax.dynamic_update_index_in_dim(o, a, chip, 0) for o, a in zip(outs, packs)]


def _rs_pair_exchange(gpacks, *, name):
    n = len(gpacks)

    def body(*refs):
        g_refs, ra_refs = refs[:n], refs[n:2 * n]
        x, y, c, _ = _place()
        copy = _remote(*refs[-2:])
        cps = []
        for g in range(n):
            half = _halves(g_refs[g], gpacks[g].shape[1])
            cps += [copy(N_CHIPS * g + j, half((j,), 1 - c), ra_refs[g].at[j], (x, y, 1 - c)) for j in range(N_CHIPS)]
        for cp in cps:
            cp.start()
        for cp in cps:
            cp.wait()

    out_shape = [jax.ShapeDtypeStruct((N_CHIPS, a.shape[1] // 2, a.shape[2]), a.dtype) for a in gpacks]
    n_sems = N_CHIPS * n
    return pl.pallas_call(body, name=name, out_shape=out_shape, in_specs=[ANY] * n, out_specs=[ANY] * n,
                          scratch_shapes=[pltpu.SemaphoreType.DMA((n_sems,)), pltpu.SemaphoreType.DMA((n_sems,))])(*gpacks)


def _row_tile(rows, cap=512):
    return max(t for t in range(16, min(rows, cap) + 1, 16) if rows % t == 0)


def _rs_pair_add(place, gpack, ra, *, name):
    _, R, C = gpack.shape
    Rh = R // 2
    tr = _row_tile(Rh)
    nrb = Rh // tr

    def body(p_ref, g_ref, ra_ref, pair_ref, own_ref):
        s = g_ref[...].astype(F32) + ra_ref[...].astype(F32)
        pair_ref[...] = s.astype(BF16)

        @pl.when(pl.program_id(1) == p_ref[1])
        def _():
            own_ref[...] = s

    grid_spec = pltpu.PrefetchScalarGridSpec(
        num_scalar_prefetch=1, grid=(nrb, N_CHIPS),
        in_specs=[pl.BlockSpec((None, tr, C), lambda i, j, p: (j, p[0] * nrb + i, 0)),
                  pl.BlockSpec((None, tr, C), lambda i, j, p: (j, i, 0))],
        out_specs=[pl.BlockSpec((None, tr, C), lambda i, j, p: (j, i, 0)), pl.BlockSpec((tr, C), lambda i, j, p: (i, 0))])
    return pl.pallas_call(
        body, name=name, grid_spec=grid_spec,
        out_shape=[jax.ShapeDtypeStruct((N_CHIPS, Rh, C), BF16), jax.ShapeDtypeStruct((Rh, C), F32)],
        compiler_params=pltpu.CompilerParams(dimension_semantics=("arbitrary", "arbitrary"),
                                             vmem_limit_bytes=VMEM_LIMIT_V7X))(place, gpack, ra)


SEQUENCER_EXCHANGE_IDS = {"l1": 2, "l0a": 3, "l0b": 4}


def _rs_chip_exchange_behind(pairs, *, tag, small=None):
    n = len(pairs)
    has_small = small is not None

    def body(*refs):
        p_refs = refs[:n]
        s_ref = refs[n] if has_small else None
        rb_refs = refs[n + has_small:2 * n + has_small]
        rs_ref = refs[2 * n + 1] if has_small else None
        x, y, c, chips = _place()
        peers = [(1 - x if k & 4 else x, 1 - y if k & 2 else y, 1 - c if k & 1 else c) for k in range(1, 8)]
        shake = peers if has_small else [(cx, cy, c) for cx, cy in chips]
        barrier = pltpu.get_barrier_semaphore()
        for peer in shake:
            pl.semaphore_signal(barrier, inc=1, device_id=peer, device_id_type=MESH)
        pl.semaphore_wait(barrier, len(shake))
        copy = _remote(*refs[-2:])
        cps = []
        for g in range(n):
            cps += [copy(3 * g + j, p_refs[g].at[2 * cx + cy], rb_refs[g].at[j], (cx, cy, c)) for j, (cx, cy) in enumerate(chips)]
        if has_small:
            dev = 4 * x + 2 * y + c
            cps += [copy(3 * n + k, s_ref, rs_ref.at[dev], peer) for k, peer in enumerate(peers)]
        for cp in cps:
            cp.start()
        for g in range(n):
            for j in range(3):
                copy(3 * g + j, p_refs[g].at[0], rb_refs[g].at[j], (x, y, c)).wait_recv()
        if has_small:
            for k, (px, py, pc) in enumerate(peers):
                copy(3 * n + k, s_ref, rs_ref.at[4 * px + 2 * py + pc], (x, y, c)).wait_recv()
        for cp in cps:
            cp.wait_send()

    ins = list(pairs) + ([small] if has_small else [])
    out_type = [jax.ShapeDtypeStruct((3,) + p.shape[1:], p.dtype) for p in pairs]
    if has_small:
        out_type.append(jax.ShapeDtypeStruct((8,) + small.shape, small.dtype))
    n_sems = 3 * n + 7 * has_small
    outs = pl.kernel(body, out_type=out_type, mesh=plsc.ScalarSubcoreMesh(axis_name="sequencer", num_cores=1),
                     name="rs_chip_exchange_behind_" + tag,
                     scratch_types=[pltpu.SemaphoreType.DMA((n_sems,)), pltpu.SemaphoreType.DMA((n_sems,))],
                     compiler_params=pltpu.CompilerParams(collective_id=SEQUENCER_EXCHANGE_IDS[tag]))(*ins)
    if has_small:
        dev = 4 * lax.axis_index("x") + 2 * lax.axis_index("y") + lax.axis_index("c")
        outs = list(outs[:n]) + [lax.dynamic_update_index_in_dim(outs[n], small, dev, 0)]
    return outs


def _rs_final_add(place, own, rb, *, name):
    Rh, C = own.shape
    tr = _row_tile(Rh)
    nrb = Rh // tr

    def body(p_ref, o_ref, rb_ref, f_ref):
        f_ref[...] = ((o_ref[...] + rb_ref[0].astype(F32)) + rb_ref[1].astype(F32)) + rb_ref[2].astype(F32)

    grid_spec = pltpu.PrefetchScalarGridSpec(
        num_scalar_prefetch=1, grid=(nrb,),
        in_specs=[pl.BlockSpec((tr, C), lambda i, p: (i, 0)), pl.BlockSpec((3, tr, C), lambda i, p: (0, i, 0))],
        out_specs=pl.BlockSpec((tr, C), lambda i, p: (p[0] * nrb + i, 0)))
    return pl.pallas_call(
        body, name=name, grid_spec=grid_spec, out_shape=jax.ShapeDtypeStruct((2 * Rh, C), F32),
        compiler_params=pltpu.CompilerParams(dimension_semantics=("arbitrary",), vmem_limit_bytes=VMEM_LIMIT_V7X))(place, own, rb)


def _sum_slots(rs):
    n, rows, C = rs.shape

    def body(r_ref, o_ref):
        acc = r_ref[0]
        for k in range(1, n):
            acc = acc + r_ref[k]
        o_ref[...] = acc

    return _call(body, name="small_grad_sum", grid=(1,), in_specs=[pl.BlockSpec((n, rows, C), lambda i: (0, 0, 0))],
                 out_specs=pl.BlockSpec((rows, C), lambda i: (0, 0)), out_shape=jax.ShapeDtypeStruct((rows, C), F32),
                 sem=("arbitrary",))(rs)


def _rs_sibling_share(gbufs, *, name):
    n = len(gbufs)

    def body(*refs):
        g_refs = refs[n:2 * n]
        x, y, c, _ = _place()
        copy = _remote(*refs[-2:])
        halves = [_halves(g_refs[g], gbufs[g].shape[0]) for g in range(n)]
        outs = [copy(g, halves[g]((), c), halves[g]((), c), (x, y, 1 - c)) for g in range(n)]
        for cp in outs:
            cp.start()
        for g in range(n):
            copy(g, halves[g]((), 1 - c), halves[g]((), 1 - c), (x, y, c)).wait_recv()
        for cp in outs:
            cp.wait_send()

    return pl.pallas_call(body, name=name, out_shape=[jax.ShapeDtypeStruct(a.shape, a.dtype) for a in gbufs],
                          in_specs=[ANY] * n, out_specs=[ANY] * n, input_output_aliases={g: g for g in range(n)},
                          scratch_shapes=[pltpu.SemaphoreType.DMA((n,)), pltpu.SemaphoreType.DMA((n,))])(*gbufs)


def _adamw(w, g, m, v, *, name, g_row=0):
    rows, cols = w.shape
    tr = rows
    for cand in range(min(rows, 512), 7, -8):
        if rows % cand == 0 and g_row % cand == 0:
            tr = cand
            break
    spec = pl.BlockSpec((tr, cols), lambda i: (i, 0))
    g_spec = pl.BlockSpec((tr, cols), lambda i: (g_row // tr + i, 0))

    def body(w_ref, g_ref, m_ref, v_ref, d_ref, nm_ref, nv_ref):
        gg = g_ref[...]
        nm = ADAM_B1 * m_ref[...] + (1.0 - ADAM_B1) * gg
        nv = ADAM_B2 * v_ref[...] + (1.0 - ADAM_B2) * (gg * gg)
        m_hat = nm / (1.0 - ADAM_B1 ** ADAM_STEP)
        v_hat = nv / (1.0 - ADAM_B2 ** ADAM_STEP)
        d_ref[...] = -ADAM_LR * (m_hat / (jnp.sqrt(v_hat) + ADAM_EPS) + ADAM_WD * w_ref[...])
        nm_ref[...] = nm
        nv_ref[...] = nv

    return _call(body, name=name, grid=(rows // tr,), in_specs=[spec, g_spec, spec, spec], out_specs=[spec] * 3,
                 out_shape=[jax.ShapeDtypeStruct((rows, cols), F32)] * 3, sem=("parallel",))(w, g, m, v)


WEIGHT_NAMES = ("mix_norm", "ab_w_in", "lru_conv_w", "lru_conv_b", "lru_wa", "lru_ba", "lru_wx", "lru_bx", "lru_lambda",
                "ab_w_out", "c_w_qkv", "c_b_qkv", "c_sinks", "c_w_out", "c_b_out", "xa_norm", "xa_mem_norm", "xa_wq",
                "xa_wkv", "xa_wo", "ffn_norm", "ffn_w_gate_up", "ffn_w_down", "final_norm")
EARLY_GROUPS = (("ab_w_in",),)
MID_GROUPS = (("ab_w_out",), ("lru_wa", "lru_wx"))
LATE_GROUPS = (("c_w_out", "xa_wkv", "ffn_w_down"), ("ffn_w_gate_up",), ("xa_wo",), ("xa_wq",), ("c_w_qkv",))
GROUPS = EARLY_GROUPS + MID_GROUPS + LATE_GROUPS
REPLICATED = ("mix_norm", "lru_conv_b", "lru_lambda", "c_sinks", "xa_norm", "xa_mem_norm", "ffn_norm", "final_norm")
SMALL_SHARDED = ("lru_conv_w", "lru_ba", "lru_bx", "c_b_qkv", "c_b_out")
LANES = 1024


def _rows(v):
    flat = v.reshape(-1)
    return jnp.pad(flat, (0, -flat.shape[0] % LANES)).reshape(-1, LANES)


def _pack_small(parts, total, *, name):
    def body(*refs):
        o_ref = refs[-1]
        o_ref[...] = jnp.zeros_like(o_ref)
        row = 0
        for p_ref in refs[:-1]:
            o_ref[row:row + p_ref.shape[0], :] = p_ref[...]
            row += p_ref.shape[0]

    return _call(body, name=name, grid=(1,), in_specs=[pl.BlockSpec(p.shape, lambda i: (0, 0)) for p in parts],
                 out_specs=pl.BlockSpec((total, LANES), lambda i: (0, 0)),
                 out_shape=jax.ShapeDtypeStruct((total, LANES), F32), sem=("arbitrary",))(*parts)


def _from_shards(name, t):
    minor = t.shape[-1]
    if name == "ab_w_in":
        return t
    if name in ("ab_w_out", "c_w_out"):
        return t.reshape(1, -1, minor)
    if name == "ffn_w_gate_up":
        return t.reshape(N_CHIPS, 2, -1, minor)
    if name in ("xa_wq", "xa_wkv", "ffn_w_down"):
        return t.reshape(N_CHIPS, 2, -1, minor).transpose(1, 0, 2, 3).reshape(2, -1, minor)
    if name in ("lru_wa", "lru_wx"):
        return t.reshape(N_CHIPS, LRU_HEADS, -1, minor).transpose(1, 0, 2, 3).reshape(LRU_HEADS, LRU_HEAD_DIM, minor)
    if name == "xa_wo":
        return t.reshape(N_CHIPS, 2, -1, minor).transpose(1, 0, 2, 3)
    assert name == "c_w_qkv"
    return t.transpose(1, 0, 2).reshape(1, D_MODEL, -1)


def _piece_shards(name, g):
    minor = g.shape[-1]
    if name in ("ab_w_in", "ffn_w_gate_up", "xa_wo"):
        return g
    if name in ("ab_w_out", "c_w_out", "xa_wq", "xa_wkv", "ffn_w_down"):
        return g.reshape(N_CHIPS, -1, minor)
    if name in ("lru_wa", "lru_wx"):
        return g.reshape(LRU_HEADS, N_CHIPS, -1, minor).transpose(1, 0, 2, 3).reshape(N_CHIPS, -1, minor)
    assert name == "c_w_qkv"
    return g.reshape(D_MODEL, N_CHIPS, -1).transpose(1, 0, 2)


RS_SETS = {
    "l1": ((("c_w_out", None), ("xa_wkv", 1), ("ffn_w_down", 1)), (("ffn_w_gate_up", 1),), (("xa_wo", 1),),
           (("xa_wq", 1),), (("c_w_qkv", None),)),
    "l0a": ((("xa_wkv", 0), ("ffn_w_down", 0)), (("ffn_w_gate_up", 0),), (("xa_wo", 0),), (("xa_wq", 0),)),
    "l0b": ((("ab_w_out", None),), (("ab_w_in", None),), (("lru_wa", None), ("lru_wx", None))),
}
RS_STAGE = {"layer1": "l1", "layer0_ffn_xa": "l0a"}


def kernel(x, mem, mix_norm, ab_w_in, lru_conv_w, lru_conv_b, lru_wa, lru_ba, lru_wx, lru_bx, lru_lambda, ab_w_out, c_w_qkv, c_b_qkv, c_sinks, c_w_out, c_b_out, xa_norm, xa_mem_norm, xa_wq, xa_wkv, xa_wo, ffn_norm, ffn_w_gate_up, ffn_w_down, final_norm, loss_target, m_mix_norm, m_ab_w_in, m_lru_conv_w, m_lru_conv_b, m_lru_wa, m_lru_ba, m_lru_wx, m_lru_bx, m_lru_lambda, m_ab_w_out, m_c_w_qkv, m_c_b_qkv, m_c_sinks, m_c_w_out, m_c_b_out, m_xa_norm, m_xa_mem_norm, m_xa_wq, m_xa_wkv, m_xa_wo, m_ffn_norm, m_ffn_w_gate_up, m_ffn_w_down, m_final_norm, v_mix_norm, v_ab_w_in, v_lru_conv_w, v_lru_conv_b, v_lru_wa, v_lru_ba, v_lru_wx, v_lru_bx, v_lru_lambda, v_ab_w_out, v_c_w_qkv, v_c_b_qkv, v_c_sinks, v_c_w_out, v_c_b_out, v_xa_norm, v_xa_mem_norm, v_xa_wq, v_xa_wkv, v_xa_wo, v_ffn_norm, v_ffn_w_gate_up, v_ffn_w_down, v_final_norm):
    given = dict(locals())
    wl = {n: given[n] for n in WEIGHT_NAMES}
    ml = {n: given["m_" + n] for n in WEIGHT_NAMES}
    vl = {n: given["v_" + n] for n in WEIGHT_NAMES}
    xi, yi, ci = lax.axis_index("x"), lax.axis_index("y"), lax.axis_index("c")
    chip = 2 * xi + yi

    def join(parts, axis):
        return parts[0] if len(parts) == 1 else jnp.concatenate(parts, axis=axis)

    local_rows = {n: wl[n].size // wl[n].shape[-1] for grp in GROUPS for n in grp}
    packs = [join([wl[n].astype(BF16).reshape(local_rows[n], wl[n].shape[-1]) for n in grp], 0) for grp in GROUPS]
    spack = _pack_small([_rows(wl[n]) for n in SMALL_SHARDED], 8, name="pack_small_weights")
    n_early, n_mid = len(EARLY_GROUPS), len(EARLY_GROUPS) + len(MID_GROUPS)
    early, sfull = _gather_weights(packs[:n_early], spack)
    early, sfull, mid_packs = lax.optimization_barrier((early, sfull, packs[n_early:n_mid]))
    mid = _gather_weights_behind(mid_packs, tag="mid")
    mid, late_packs = lax.optimization_barrier((mid, packs[n_mid:]))
    gathered = early + mid + _gather_weights_behind(late_packs, tag="late")
    w = {n: wl[n] for n in REPLICATED}
    w["c_sinks"] = wl["c_sinks"][0]
    for grp, full in zip(GROUPS, gathered):
        off = 0
        for n in grp:
            w[n] = _from_shards(n, full if len(grp) == 1 else full[:, off:off + local_rows[n]])
            off += local_rows[n]
    for r, n in enumerate(SMALL_SHARDED):
        loc = wl[n].shape[1:]
        t = sfull[:, r, :wl[n].size].reshape((N_CHIPS,) + loc)
        if n == "lru_conv_w":
            w[n] = t.transpose(1, 0, 2).reshape(CONV_WIDTH, -1)
        elif n in ("lru_ba", "lru_bx"):
            w[n] = t.transpose(1, 0, 2).reshape(1, -1)
        else:
            w[n] = t.reshape(1, -1)

    place = jnp.stack([ci, chip]).astype(jnp.int32)

    def pair_stage(spec, g, tag):
        piece = lambda n, l: (g[n] if l is None else g[n, l]).astype(BF16)
        gpacks = [join([_piece_shards(n, piece(n, l)) for n, l in grp], 1) for grp in spec]
        ras = _rs_pair_exchange(gpacks, name=f"rs_pair_exchange_{tag}")
        sums = [_rs_pair_add(place, gp, ra, name=f"rs_pair_add_{tag}_{i}") for i, (gp, ra) in enumerate(zip(gpacks, ras))]
        return [pair for pair, _ in sums], [own for _, own in sums]

    owns, rbs = [], []

    def reduce_behind(stage, g):
        tag = RS_STAGE[stage]
        pairs, own = pair_stage(RS_SETS[tag], g, tag)
        owns.extend(own)
        rbs.extend(_rs_chip_exchange_behind(pairs, tag=tag))
        return own

    loss_part, grad_x, g = _device_step(x[0], mem[0], loss_target[0], w, on_grads=reduce_behind)

    small_parts = [_rows(g[n]) for n in REPLICATED] + [_rows(jnp.broadcast_to(loss_part, (LANES,)))]
    small_parts += [_rows(g[n]) for n in SMALL_SHARDED]
    small = _pack_small(small_parts, 24, name="pack_small_grads")
    pairs, own = pair_stage(RS_SETS["l0b"], g, "l0b")
    *rb, rs = _rs_chip_exchange_behind(pairs, tag="l0b", small=small)

    def finish(own_sums, received, first, name):
        return _rs_sibling_share([_rs_final_add(place, o, r, name=f"rs_final_add_{first + i}")
                                  for i, (o, r) in enumerate(zip(own_sums, received))], name=name)

    gsums = finish(owns, rbs, 0, "rs_sibling_share_behind") + finish(own, rb, len(owns), "rs_sibling_share_last")
    ssum = _sum_slots(rs)

    where = {}
    for grp, gsum in zip(RS_SETS["l1"] + RS_SETS["l0a"] + RS_SETS["l0b"], gsums):
        off = 0
        for n, l in grp:
            rows = local_rows[n] if l is None else local_rows[n] // 2
            where[n, l] = (gsum, off, rows, len(grp) == 1)
            off += rows
    take = lambda gsum, off, rows, whole: gsum if whole else gsum[off:off + rows]
    grads, grad_rows = {}, {}
    for grp in LATE_GROUPS + EARLY_GROUPS + MID_GROUPS:
        for n in grp:
            if (n, None) in where:
                grads[n] = take(*where[n, None]).reshape(wl[n].shape)
                grad_rows[n] = where[n, None][:2]
            else:
                grads[n] = jnp.stack([take(*where[n, l]).reshape(wl[n].shape[1:]) for l in range(2)])
                grad_rows[n] = (grads[n].reshape(local_rows[n], wl[n].shape[-1]), 0)
    row = 0
    for n in REPLICATED:
        k = _rows(g[n]).shape[0]
        grads[n] = ssum[row:row + k].reshape(-1)[:wl[n].size].reshape(wl[n].shape)
        row += k
    loss = ssum[row, 0]
    row += 1
    for n in SMALL_SHARDED:
        k = _rows(g[n]).shape[0]
        full = ssum[row:row + k].reshape(-1)[:g[n].size]
        row += k
        loc = wl[n].shape
        if n == "lru_conv_w":
            sh = full.reshape(CONV_WIDTH, N_CHIPS, -1)
        elif n in ("lru_ba", "lru_bx"):
            sh = full.reshape(LRU_HEADS, N_CHIPS, -1)
        else:
            sh = full.reshape(1, N_CHIPS, -1)
        grads[n] = lax.dynamic_index_in_dim(sh, chip, axis=1, keepdims=False).reshape(loc)

    delta, new_m, new_v = {}, {}, {}
    for n, (gsum, off) in grad_rows.items():
        shape2 = (local_rows[n], wl[n].shape[-1])
        d, nm, nv = _adamw(wl[n].reshape(shape2), gsum, ml[n].reshape(shape2), vl[n].reshape(shape2), g_row=off,
                           name="adamw_" + n)
        delta[n], new_m[n], new_v[n] = (t.reshape(wl[n].shape) for t in (d, nm, nv))
    smalls = REPLICATED + SMALL_SHARDED
    packs = [_pack_small([_rows(src[n]) for n in smalls], 24, name="pack_adamw_" + tag)
             for tag, src in (("w", wl), ("g", grads), ("m", ml), ("v", vl))]
    outs = _adamw(*packs, name="adamw_small")
    row = 0
    for n in smalls:
        k = _rows(wl[n]).shape[0]
        for dst, o in zip((delta, new_m, new_v), outs):
            dst[n] = o[row:row + k].reshape(-1)[:wl[n].size].reshape(wl[n].shape)
        row += k

    return (loss, grad_x[None], *[grads[n] for n in WEIGHT_NAMES], *[delta[n] for n in WEIGHT_NAMES],
            *[new_m[n] for n in WEIGHT_NAMES], *[new_v[n] for n in WEIGHT_NAMES])
```

```python
import jax
import jax.numpy as jnp
from jax import lax
from jax.experimental import pallas as pl
from jax.experimental.pallas import tpu as pltpu
from jax.experimental.pallas import tpu_sc as plsc

F32, BF16 = jnp.float32, jnp.bfloat16
D_MODEL = 1024
NORM_EPS = 1e-6
ROPE_THETA = 500000.0
HEAD_DIM = 64
ROT_DIM = 16
BLK = 128
LRU_HEADS, LRU_HEAD_DIM, CONV_WIDTH, LRU_C = 4, 256, 4, 8.0
DILATED_PATTERN = ((128, 1), (512, 4), (2048, 16))
B_HEADS, C_HEADS, C_KV_HEADS, C_WINDOW = 8, 16, 2, 128
XA_HEADS, XA_HEAD_DIM, N_MEM = 4, 128, 256
D_FF = 2816
NEG = -1e30
ADAM_LR, ADAM_B1, ADAM_B2, ADAM_EPS, ADAM_WD, ADAM_STEP = 0.001, 0.9, 0.999, 1e-08, 0.01, 10
N_CHIPS = 4
VMEM_LIMIT_V7X = 56 * 1024 * 1024

NN = (((1,), (0,)), ((), ()))
NT = (((1,), (1,)), ((), ()))
TN = (((0,), (0,)), ((), ()))


def _dot(a, b, dims=NN):
    return lax.dot_general(a, b, dims, preferred_element_type=F32)


def _sigmoid(x):
    return 1.0 / (1.0 + jnp.exp(-x))


def _call(body, *, name, grid, in_specs, out_specs, out_shape, scratch=(), sem=None):
    return pl.pallas_call(
        body, name=name, grid=grid, in_specs=in_specs, out_specs=out_specs, out_shape=out_shape,
        scratch_shapes=list(scratch),
        compiler_params=pltpu.CompilerParams(dimension_semantics=sem, vmem_limit_bytes=VMEM_LIMIT_V7X))


def _rope_tables(L):
    half = ROT_DIM // 2
    inv = ROPE_THETA ** (-jnp.arange(0, ROT_DIM, 2, dtype=F32) / ROT_DIM)
    ang = jnp.arange(L, dtype=F32)[:, None] * inv[None, :]
    cos, sin = jnp.cos(ang), jnp.sin(ang)
    rest = HEAD_DIM - ROT_DIM
    z8, zr, one = jnp.zeros((L, half), F32), jnp.zeros((L, rest), F32), jnp.ones((L, rest), F32)
    c = jnp.concatenate([cos, cos, one], axis=1)
    s1 = jnp.concatenate([-sin, z8, zr], axis=1)
    s2 = jnp.concatenate([z8, sin, zr], axis=1)
    return tuple(jnp.concatenate([t, t], axis=1) for t in (c, s1, s2))


def _rope_fwd(v, c, s1, s2):
    return v * c + pltpu.roll(v, 120, 1) * s1 + pltpu.roll(v, 8, 1) * s2


def _rope_bwd(dv, c, s1, s2):
    return dv * c + pltpu.roll(dv * s1, 8, 1) + pltpu.roll(dv * s2, 120, 1)


def _weight_spec(w, layer):
    once = pl.Buffered(1)
    if layer is None:
        return w.shape, pl.BlockSpec(w.shape, lambda i: (0, 0, 0), pipeline_mode=once)
    S, _, K, Ns = w.shape
    return (S, K, Ns), pl.BlockSpec((S, None, K, Ns), lambda i: (0, layer, 0, 0), pipeline_mode=once)


def _rowmm(a, w3, *, name, tm=512, gain=None, bias=None, res=None, swiglu=False, rope=None, layer=None):
    M, K = a.shape
    (S, _, Ns), w_spec = _weight_spec(w3, layer)
    N = S * Ns
    tm = min(tm, M)
    has_norm, has_bias, has_res, has_rope = gain is not None, bias is not None, res is not None, rope is not None
    row = lambda w: pl.BlockSpec((tm, w), lambda i: (i, 0))
    whole = lambda shape: pl.BlockSpec(shape, lambda i: (0,) * len(shape))
    ins, specs = [a], [row(K)]
    if has_norm:
        ins.append(gain.reshape(1, K)); specs.append(whole((1, K)))
    ins.append(w3); specs.append(w_spec)
    if has_bias:
        ins.append(bias.reshape(1, N)); specs.append(whole((1, N)))
    if has_res:
        ins.append(res); specs.append(row(N))
    if has_rope:
        ins += list(rope[2]); specs += [row(128)] * 3
    y_dtype = F32 if has_res else BF16
    out_shape, out_specs = [jax.ShapeDtypeStruct((M, N), y_dtype)], [row(N)]
    if has_norm:
        out_shape.append(jax.ShapeDtypeStruct((M, K), BF16)); out_specs.append(row(K))
    if swiglu:
        out_shape.append(jax.ShapeDtypeStruct((M, N // 2), BF16)); out_specs.append(row(N // 2))
    scratch = [pltpu.VMEM((tm, N), F32)] if has_rope else []

    def body(*refs):
        it = iter(refs)
        a_ref = next(it)
        g_ref = next(it) if has_norm else None
        w_ref = next(it)
        b_ref = next(it) if has_bias else None
        r_ref = next(it) if has_res else None
        tabs = [next(it) for _ in range(3)] if has_rope else None
        y_ref = next(it)
        n_ref = next(it) if has_norm else None
        act_ref = next(it) if swiglu else None
        ys_ref = next(it) if has_rope else None
        if has_norm:
            x = a_ref[...].astype(F32)
            ms = jnp.mean(x * x, axis=-1, keepdims=True)
            xb = (x * lax.rsqrt(ms + NORM_EPS) * g_ref[...]).astype(BF16)
            n_ref[...] = xb
        else:
            xb = a_ref[...].astype(BF16)
        if swiglu:
            for s in range(S // 2):
                g = _dot(xb, w_ref[s])
                u = _dot(xb, w_ref[s + S // 2])
                y_ref[:, s * Ns:(s + 1) * Ns] = g.astype(BF16)
                y_ref[:, N // 2 + s * Ns:N // 2 + (s + 1) * Ns] = u.astype(BF16)
                act_ref[:, s * Ns:(s + 1) * Ns] = (g * _sigmoid(g) * u).astype(BF16)
            return
        for s in range(S):
            sl = slice(s * Ns, (s + 1) * Ns)
            acc = _dot(xb, w_ref[s])
            if has_bias:
                acc = acc + b_ref[:, sl]
            if has_res:
                acc = acc + r_ref[:, sl]
            if has_rope:
                ys_ref[:, sl] = acc
            else:
                y_ref[:, sl] = acc.astype(y_dtype)
        if has_rope:
            c, s1, s2 = (t[...] for t in tabs)
            for cb in range(N // 128):
                cs = slice(cb * 128, (cb + 1) * 128)
                v = ys_ref[:, cs]
                if rope[0] <= cb * 128 < rope[1]:
                    v = _rope_fwd(v, c, s1, s2)
                y_ref[:, cs] = v.astype(BF16)

    return _call(body, name=name, grid=(M // tm,), in_specs=specs, out_specs=out_specs, out_shape=out_shape,
                 scratch=scratch, sem=("parallel",))(*ins)


def _mm_nt(dy, w3, *, name, mode, tm=512, kchunk=None, h=None, gain=None, dh=None, gu=None, layer=None, after=None):
    M, N = dy.shape
    (S, K, Ns), w_spec = _weight_spec(w3, layer)
    kchunk = kchunk or K
    tm = min(tm, M)
    row = lambda w: pl.BlockSpec((tm, w), lambda i: (i, 0))
    whole = lambda shape: pl.BlockSpec(shape, lambda i: (0,) * len(shape))
    ins, specs = [dy, w3], [row(N), w_spec]
    after = list(after or ())
    ins = after + ins
    specs = [pl.BlockSpec((8, a.shape[1]), lambda i: (0, 0)) for a in after] + specs
    has_dh = dh is not None
    if mode == "norm":
        ins += [h, gain.reshape(1, K)]; specs += [row(K), whole((1, K))]
        if has_dh:
            ins.append(dh); specs.append(row(K))
        out_shape = [jax.ShapeDtypeStruct((M, K), F32), jax.ShapeDtypeStruct((1, K), F32)]
        out_specs = [row(K), whole((1, K))]
    elif mode == "swiglu":
        ins.append(gu); specs.append(row(2 * K))
        out_shape, out_specs = [jax.ShapeDtypeStruct((M, 2 * K), BF16)], [row(2 * K)]
    else:
        out_shape, out_specs = [jax.ShapeDtypeStruct((M, K), BF16)], [row(K)]

    def body(*refs):
        it = iter(refs[len(after):])
        dy_ref, w_ref = next(it), next(it)
        if mode == "norm":
            h_ref, g_ref = next(it), next(it)
            dh_ref = next(it) if has_dh else None
            o_ref, dg_ref = next(it), next(it)
        elif mode == "swiglu":
            gu_ref, o_ref = next(it), next(it)
        else:
            o_ref = next(it)
        for kc in range(K // kchunk):
            ks = slice(kc * kchunk, (kc + 1) * kchunk)
            acc = None
            for s in range(S):
                t = _dot(dy_ref[:, s * Ns:(s + 1) * Ns].astype(BF16), w_ref[s, ks, :], NT)
                acc = t if acc is None else acc + t
            if mode == "plain":
                o_ref[:, ks] = acc.astype(BF16)
            elif mode == "swiglu":
                us = slice(K + kc * kchunk, K + (kc + 1) * kchunk)
                g = gu_ref[:, ks].astype(F32)
                u = gu_ref[:, us].astype(F32)
                sg = _sigmoid(g)
                o_ref[:, ks] = (acc * u * (sg * (1.0 + g * (1.0 - sg)))).astype(BF16)
                o_ref[:, us] = (acc * (g * sg)).astype(BF16)
            else:
                x = h_ref[...].astype(F32)
                r = lax.rsqrt(jnp.mean(x * x, axis=-1, keepdims=True) + NORM_EPS)
                xhat = x * r
                dxh = acc * g_ref[...]
                dx = r * (dxh - xhat * jnp.mean(dxh * xhat, axis=-1, keepdims=True))
                o_ref[...] = dx + dh_ref[...] if has_dh else dx

                @pl.when(pl.program_id(0) == 0)
                def _():
                    dg_ref[...] = jnp.zeros_like(dg_ref)

                dg_ref[...] += jnp.sum(acc * xhat, axis=0, keepdims=True)

    sem = ("arbitrary",) if mode == "norm" else ("parallel",)
    return _call(body, name=name, grid=(M // tm,), in_specs=specs, out_specs=out_specs, out_shape=out_shape, sem=sem)(*ins)


def _mm_tn(x, dy, *, S, name, tk=1024, kk=None, bias=False):
    M, K = x.shape
    N = dy.shape[1]
    Ns = N // S
    kk = kk or K
    tk = min(tk, M)
    nl = M // tk
    in_specs = [pl.BlockSpec((tk, kk), lambda s, kc, l: (l, kc)), pl.BlockSpec((tk, Ns), lambda s, kc, l: (l, s))]
    out_shape = [jax.ShapeDtypeStruct((S, K, Ns), BF16)]
    out_specs = [pl.BlockSpec((None, kk, Ns), lambda s, kc, l: (s, kc, 0))]
    if bias:
        out_shape.append(jax.ShapeDtypeStruct((1, N), F32))
        out_specs.append(pl.BlockSpec((1, Ns), lambda s, kc, l: (0, s)))

    def body(x_ref, dy_ref, o_ref, *rest):
        acc_ref = rest[-1]
        kc, l = pl.program_id(1), pl.program_id(2)

        @pl.when(l == 0)
        def _():
            acc_ref[...] = jnp.zeros_like(acc_ref)

        acc_ref[...] += _dot(x_ref[...].astype(BF16), dy_ref[...].astype(BF16), TN)
        if bias:
            b_ref = rest[0]

            @pl.when((kc == 0) & (l == 0))
            def _():
                b_ref[...] = jnp.zeros_like(b_ref)

            @pl.when(kc == 0)
            def _():
                b_ref[...] += jnp.sum(dy_ref[...].astype(F32), axis=0, keepdims=True)

        @pl.when(l == nl - 1)
        def _():
            o_ref[...] = acc_ref[...].astype(BF16)

    return _call(body, name=name, grid=(S, K // kk, nl), in_specs=in_specs, out_specs=out_specs, out_shape=out_shape,
                 scratch=[pltpu.VMEM((kk, Ns), F32)], sem=("arbitrary", "arbitrary", "arbitrary"))(x, dy)


def _band_bias(max_dist, has_prev):
    rows = lax.broadcasted_iota(jnp.int32, (BLK, 2 * BLK), 0)
    cols = lax.broadcasted_iota(jnp.int32, (BLK, 2 * BLK), 1)
    dist = rows - cols + BLK
    ok = (dist >= 0) & (dist <= max_dist) & ((cols >= BLK) | has_prev)
    return jnp.where(ok, 0.0, NEG)


Q_SCALE = HEAD_DIM ** -0.5


def _band_fwd(qa, ka, va, *, d, nq, nkv, qcol, kcol, vcol, max_dist, sinks=None, name):
    Lr = qa.shape[0]
    nb = Lr // BLK
    qw, kw, G = nq * HEAD_DIM, nkv * HEAD_DIM, nq // nkv
    cur = lambda colf, w: pl.BlockSpec((BLK, w), lambda r, i: (i, colf(r)))
    prv = lambda colf, w: pl.BlockSpec((BLK, w), lambda r, i: (jnp.maximum(i - 1, 0), colf(r)))
    out = pl.BlockSpec((BLK, qw), lambda r, i: (i, r))
    ins, specs = [qa, ka, ka, va, va], [cur(qcol, qw), cur(kcol, kw), prv(kcol, kw), cur(vcol, kw), prv(vcol, kw)]
    has_sinks = sinks is not None
    if has_sinks:
        ins.append(sinks); specs.append(pl.BlockSpec(memory_space=pltpu.SMEM))

    def body(*refs):
        q_ref, kc_ref, kp_ref, vc_ref, vp_ref = refs[:5]
        sk_ref = refs[5] if has_sinks else None
        o_ref, lse_ref = refs[-2], refs[-1]
        bias = _band_bias(max_dist, pl.program_id(1) > 0)
        k2 = jnp.concatenate([kp_ref[...], kc_ref[...]], axis=0)
        v2 = jnp.concatenate([vp_ref[...], vc_ref[...]], axis=0)
        for h in range(nq):
            hs = slice(h * HEAD_DIM, (h + 1) * HEAD_DIM)
            ks = slice((h // G) * HEAD_DIM, (h // G + 1) * HEAD_DIM)
            s = _dot(q_ref[:, hs] * jnp.asarray(Q_SCALE, BF16), k2[:, ks], NT) + bias
            m = jnp.max(s, axis=-1, keepdims=True)
            if has_sinks:
                m = jnp.maximum(m, sk_ref[h])
            p = jnp.exp(s - m)
            l = jnp.sum(p, axis=-1, keepdims=True)
            if has_sinks:
                l = l + jnp.exp(sk_ref[h] - m)
            o_ref[:, hs] = (_dot(p.astype(BF16), v2[:, ks]) / l).astype(BF16)
            lse_ref[:, hs] = jnp.broadcast_to(m + jnp.log(l), (BLK, HEAD_DIM))

    return _call(body, name=name, grid=(d, nb), in_specs=specs, out_specs=[out, out],
                 out_shape=[jax.ShapeDtypeStruct((Lr, d * qw), BF16), jax.ShapeDtypeStruct((Lr, d * qw), F32)],
                 sem=("parallel", "parallel"))(*ins)


def _band_bwd(qa, ka, va, doa, oa, lsea, *, d, nq, nkv, qcol, kcol, vcol, docol, max_dist, sinks=None, name):
    Lr = qa.shape[0]
    nb = Lr // BLK
    qw, kw, G = nq * HEAD_DIM, nkv * HEAD_DIM, nq // nkv
    transposed = G > 1
    last = lambda i: jnp.minimum(i, nb - 1)
    cur = lambda colf, w: pl.BlockSpec((BLK, w), lambda r, i: (last(i), colf(r)))
    prv = lambda colf, w: pl.BlockSpec((BLK, w), lambda r, i: (jnp.maximum(last(i) - 1, 0), colf(r)))
    own = lambda r: r
    ins = [qa, ka, ka, va, va, doa, oa, lsea]
    specs = [cur(qcol, qw), cur(kcol, kw), prv(kcol, kw), cur(vcol, kw), prv(vcol, kw), cur(docol, qw), cur(own, qw),
             cur(own, qw)]
    has_sinks = sinks is not None
    if has_sinks:
        ins.append(sinks); specs.append(pl.BlockSpec(memory_space=pltpu.SMEM))
    out_shape = [jax.ShapeDtypeStruct((Lr, d * qw), F32), jax.ShapeDtypeStruct((Lr, d * kw), F32),
                 jax.ShapeDtypeStruct((Lr, d * kw), F32)]
    behind = lambda r, i: (jnp.maximum(i - 1, 0), r)
    out_specs = [pl.BlockSpec((BLK, qw), lambda r, i: (last(i), r)), pl.BlockSpec((BLK, kw), behind),
                 pl.BlockSpec((BLK, kw), behind)]
    if has_sinks:
        out_shape.append(jax.ShapeDtypeStruct((8, 128), F32))
        out_specs.append(pl.BlockSpec((8, 128), lambda r, i: (0, 0)))

    def body(*refs):
        it = iter(refs)
        q_ref, kc_ref, kp_ref, vc_ref, vp_ref, do_ref, o_ref, ls_ref = (next(it) for _ in range(8))
        sk_ref = next(it) if has_sinks else None
        dq_ref, dk_ref, dv_ref = next(it), next(it), next(it)
        dsk_ref = next(it) if has_sinks else None
        dk_car, dv_car = next(it), next(it)
        r_id, i = pl.program_id(0), pl.program_id(1)

        @pl.when(i == 0)
        def _():
            dk_car[...] = jnp.zeros_like(dk_car)
            dv_car[...] = jnp.zeros_like(dv_car)

        if has_sinks:
            @pl.when((r_id == 0) & (i == 0))
            def _():
                dsk_ref[...] = jnp.zeros_like(dsk_ref)

        @pl.when(i == nb)
        def _():
            dk_ref[...] = dk_car[...]
            dv_ref[...] = dv_car[...]

        @pl.when(i < nb)
        def _():
            bias = _band_bias(max_dist, i > 0)
            k2 = jnp.concatenate([kp_ref[...], kc_ref[...]], axis=0)
            v2 = jnp.concatenate([vp_ref[...], vc_ref[...]], axis=0)
            if has_sinks:
                lane = lax.broadcasted_iota(jnp.int32, (8, 128), 1)
                dsk = jnp.zeros((8, 128), F32)
            for kv in range(nkv):
                ks = slice(kv * HEAD_DIM, (kv + 1) * HEAD_DIM)
                kh, vh = k2[:, ks], v2[:, ks]
                shape = (HEAD_DIM, 2 * BLK) if transposed else (2 * BLK, HEAD_DIM)
                dk, dv = jnp.zeros(shape, F32), jnp.zeros(shape, F32)
                for g in range(G):
                    h = kv * G + g
                    hs = slice(h * HEAD_DIM, (h + 1) * HEAD_DIM)
                    q = q_ref[:, hs] * jnp.asarray(Q_SCALE, BF16)
                    do = do_ref[:, hs]
                    lse = ls_ref[:, h * HEAD_DIM:h * HEAD_DIM + 1]
                    dl = jnp.sum(do.astype(F32) * o_ref[:, hs].astype(F32), axis=-1, keepdims=True)
                    p = jnp.exp(_dot(q, kh, NT) + bias - lse)
                    ds = (p * (_dot(do, vh, NT) - dl)).astype(BF16)
                    dq_ref[:, hs] = _dot(ds, kh) * Q_SCALE
                    if transposed:
                        dk = dk + _dot(q, ds, TN)
                        dv = dv + _dot(do, p.astype(BF16), TN)
                    else:
                        dk = dk + _dot(ds, q, TN)
                        dv = dv + _dot(p.astype(BF16), do, TN)
                    if has_sinks:
                        val = -jnp.sum(jnp.exp(sk_ref[h] - lse) * dl, axis=0, keepdims=True)
                        dsk = dsk + jnp.where(lane == h, val, 0.0)
                if transposed:
                    dk, dv = dk.T, dv.T
                dk_ref[:, ks] = dk_car[:, ks] + dk[:BLK]
                dv_ref[:, ks] = dv_car[:, ks] + dv[:BLK]
                dk_car[:, ks] = dk[BLK:]
                dv_car[:, ks] = dv[BLK:]
            if has_sinks:
                dsk_ref[...] += dsk

    return _call(body, name=name, grid=(d, nb + 1), in_specs=specs, out_specs=out_specs, out_shape=out_shape,
                 scratch=[pltpu.VMEM((BLK, kw), F32), pltpu.VMEM((BLK, kw), F32)], sem=("arbitrary", "arbitrary"))(*ins)


def _attn_grad_combine(branches, tabs, *, name, tm=256):
    L, qw = branches[0][0].shape
    kw = branches[0][1].shape[1]
    nbr = len(branches)
    row = lambda w: pl.BlockSpec((tm, w), lambda i: (i, 0))
    ins, specs = [], []
    for dq, dk, dv in branches:
        ins += [dq, dk, dv]; specs += [row(qw), row(kw), row(kw)]
    ins += list(tabs); specs += [row(128)] * 3

    def body(*refs):
        c, s1, s2 = (t[...] for t in refs[3 * nbr:3 * nbr + 3])
        o_ref = refs[-1]
        for part, (w, off, rot) in enumerate(((qw, 0, True), (kw, qw, True), (kw, qw + kw, False))):
            for cb in range(w // 128):
                cs = slice(cb * 128, (cb + 1) * 128)
                v = refs[part][:, cs]
                for b in range(1, nbr):
                    v = v + refs[3 * b + part][:, cs]
                if rot:
                    v = _rope_bwd(v, c, s1, s2)
                o_ref[:, off + cb * 128:off + (cb + 1) * 128] = v.astype(BF16)

    return _call(body, name=name, grid=(L // tm,), in_specs=specs, out_specs=row(qw + 2 * kw),
                 out_shape=jax.ShapeDtypeStruct((L, qw + 2 * kw), BF16), sem=("parallel",))(*ins)


def _xattn_fwd(q, kv, *, name, tq=512):
    L, W = q.shape
    scale = XA_HEAD_DIM ** -0.5
    row = pl.BlockSpec((tq, W), lambda i: (i, 0))
    kvs = pl.BlockSpec((N_MEM, 2 * W), lambda i: (0, 0))

    def body(q_ref, kv_ref, o_ref, lse_ref):
        for h in range(XA_HEADS):
            hs = slice(h * XA_HEAD_DIM, (h + 1) * XA_HEAD_DIM)
            vs = slice(W + h * XA_HEAD_DIM, W + (h + 1) * XA_HEAD_DIM)
            s = _dot(q_ref[:, hs], kv_ref[:, hs], NT) * scale
            m = jnp.max(s, axis=-1, keepdims=True)
            p = jnp.exp(s - m)
            l = jnp.sum(p, axis=-1, keepdims=True)
            o_ref[:, hs] = (_dot(p.astype(BF16), kv_ref[:, vs]) / l).astype(BF16)
            lse_ref[:, hs] = jnp.broadcast_to(m + jnp.log(l), (tq, XA_HEAD_DIM))

    return _call(body, name=name, grid=(L // tq,), in_specs=[row, kvs], out_specs=[row, row],
                 out_shape=[jax.ShapeDtypeStruct((L, W), BF16), jax.ShapeDtypeStruct((L, W), F32)], sem=("parallel",))(q, kv)


def _xattn_bwd(q, kv, o, lse, do, *, name, tq=512):
    L, W = q.shape
    scale = XA_HEAD_DIM ** -0.5
    row = pl.BlockSpec((tq, W), lambda i: (i, 0))
    kvs = pl.BlockSpec((N_MEM, 2 * W), lambda i: (0, 0))

    def body(q_ref, kv_ref, o_ref, lse_ref, do_ref, dq_ref, dkv_ref):
        @pl.when(pl.program_id(0) == 0)
        def _():
            dkv_ref[...] = jnp.zeros_like(dkv_ref)

        for h in range(XA_HEADS):
            hs = slice(h * XA_HEAD_DIM, (h + 1) * XA_HEAD_DIM)
            vs = slice(W + h * XA_HEAD_DIM, W + (h + 1) * XA_HEAD_DIM)
            qh, kh, vh, doh = q_ref[:, hs], kv_ref[:, hs], kv_ref[:, vs], do_ref[:, hs]
            p = jnp.exp(_dot(qh, kh, NT) * scale - lse_ref[:, h * XA_HEAD_DIM:h * XA_HEAD_DIM + 1])
            dl = jnp.sum(doh.astype(F32) * o_ref[:, hs].astype(F32), axis=-1, keepdims=True)
            ds = (p * (_dot(doh, vh, NT) - dl) * scale).astype(BF16)
            dq_ref[:, hs] = _dot(ds, kh).astype(BF16)
            dkv_ref[:, hs] += _dot(ds, qh, TN)
            dkv_ref[:, vs] += _dot(p.astype(BF16), doh, TN)

    return _call(body, name=name, grid=(L // tq,), in_specs=[row, kvs, row, row, row], out_specs=[row, kvs],
                 out_shape=[jax.ShapeDtypeStruct((L, W), BF16), jax.ShapeDtypeStruct((N_MEM, 2 * W), F32)],
                 sem=("arbitrary",))(q, kv, o, lse, do)


def _neg_expm1(z):
    series = -(z * (1.0 + z * (0.5 + z * (1.0 / 6.0 + z * (1.0 / 24.0 + z * (1.0 / 120.0))))))
    return jnp.where(z > -0.05, series, 1.0 - jnp.exp(z))


def _softplus(z):
    return jnp.maximum(z, 0.0) + jnp.log(1.0 + jnp.exp(-jnp.abs(z)))


def _gelu_parts(y):
    c = 0.7978845608028654
    t = jnp.tanh(c * (y + 0.044715 * y * y * y))
    gy = 0.5 * y * (1.0 + t)
    dgy = 0.5 * (1.0 + t) + 0.5 * y * (1.0 - t * t) * c * (1.0 + 3.0 * 0.044715 * y * y)
    return gy, dgy


def _lru_gates(xc, wa_ref, ba, wx_ref, bx, sp):
    rs, igs = [], []
    for hd in range(LRU_HEADS):
        sl = slice(hd * LRU_HEAD_DIM, (hd + 1) * LRU_HEAD_DIM)
        xh = xc[:, sl].astype(BF16)
        rs.append(_sigmoid(_dot(xh, wa_ref[hd]) + ba[:, sl]))
        igs.append(_sigmoid(_dot(xh, wx_ref[hd]) + bx[:, sl]))
    r, ig = jnp.concatenate(rs, axis=1), jnp.concatenate(igs, axis=1)
    la = -LRU_C * r * sp
    return r, ig, jnp.exp(la), _neg_expm1(2.0 * la)


def _conv_taps(x_ext, halo):
    n = x_ext.shape[0]
    return [x_ext[halo:] if k == CONV_WIDTH - 1 else pltpu.roll(x_ext, CONV_WIDTH - 1 - k, 0)[halo:]
            for k in range(CONV_WIDTH)]


def _lru_fwd(proj, cw, cb, wa, ba, wx, bx, lam, *, name, tc=512):
    L = proj.shape[0]
    W = LRU_HEADS * LRU_HEAD_DIM
    nb = L // tc
    whole = lambda shape: pl.BlockSpec(shape, lambda i: (0,) * len(shape))
    specs = [pl.BlockSpec((tc, W), lambda i: (i, 0)), pl.BlockSpec((tc, W), lambda i: (i, 1)),
             pl.BlockSpec((16, W), lambda i: (jnp.maximum(i * (tc // 16) - 1, 0), 0)),
             whole((CONV_WIDTH, W)), whole((1, W)), whole((LRU_HEADS, LRU_HEAD_DIM, LRU_HEAD_DIM)), whole((1, W)),
             whole((LRU_HEADS, LRU_HEAD_DIM, LRU_HEAD_DIM)), whole((1, W)), whole((1, W))]
    out_specs = [pl.BlockSpec((tc, W), lambda i: (i, 0))] * 2
    out_shape = [jax.ShapeDtypeStruct((L, W), BF16), jax.ShapeDtypeStruct((L, W), F32)]

    def body(x_ref, y_ref, xh_ref, cw_ref, cb_ref, wa_ref, ba_ref, wx_ref, bx_ref, lam_ref, rec_ref, hs_ref,
             hcar, a_scr, b_scr):
        i = pl.program_id(0)

        @pl.when(i == 0)
        def _():
            hcar[...] = jnp.zeros_like(hcar)

        halo = jnp.where(i > 0, xh_ref[...].astype(F32), 0.0)
        taps = _conv_taps(jnp.concatenate([halo, x_ref[...].astype(F32)], axis=0), 16)
        xc = cb_ref[...] + sum(cw_ref[k:k + 1, :] * taps[k] for k in range(CONV_WIDTH))
        _, ig, a, om = _lru_gates(xc, wa_ref, ba_ref[...], wx_ref, bx_ref[...], _softplus(-lam_ref[...]))
        b = jnp.sqrt(om) * (ig * xc)
        rowmod = lax.broadcasted_iota(jnp.int32, (tc, W), 0) & 7
        for s in (1, 2, 4):
            keep = rowmod >= s
            b = jnp.where(keep, a * pltpu.roll(b, s, 0) + b, b)
            a = jnp.where(keep, a * pltpu.roll(a, s, 0), a)
        a_scr[...] = a
        b_scr[...] = b

        def tile(j, hc):
            rows = pl.ds(pl.multiple_of(j * 8, 8), 8)
            ht = a_scr[rows, :] * hc + b_scr[rows, :]
            hs_ref[rows, :] = ht
            return jnp.broadcast_to(ht[7:8, :], (8, W))

        hcar[...] = lax.fori_loop(0, tc // 8, tile, hcar[...])
        gy, _ = _gelu_parts(y_ref[...].astype(F32))
        rec_ref[...] = (hs_ref[...] * gy).astype(BF16)

    return _call(body, name=name, grid=(nb,), in_specs=specs, out_specs=out_specs, out_shape=out_shape,
                 scratch=[pltpu.VMEM((8, W), F32), pltpu.VMEM((tc, W), F32), pltpu.VMEM((tc, W), F32)],
                 sem=("arbitrary",))(proj, proj, proj, cw, cb, wa, ba, wx, bx, lam)


def _lru_bwd(proj, hs, drec_src, cw, cb, wa, ba, wx, bx, lam, *, name, tc=256):
    L = proj.shape[0]
    W = LRU_HEADS * LRU_HEAD_DIM
    nb = L // tc
    tb = lambda i: nb - 1 - i
    whole = lambda shape: pl.BlockSpec(shape, lambda i: (0,) * len(shape))
    gate_w = (LRU_HEADS, LRU_HEAD_DIM, LRU_HEAD_DIM)
    specs = [pl.BlockSpec((tc, W), lambda i: (tb(i), 0)), pl.BlockSpec((tc, W), lambda i: (tb(i), 1)),
             pl.BlockSpec((16, W), lambda i: (jnp.maximum(tb(i) * (tc // 16) - 1, 0), 0)),
             pl.BlockSpec((tc, W), lambda i: (tb(i), 0)),
             pl.BlockSpec((8, W), lambda i: (jnp.maximum(tb(i) * (tc // 8) - 1, 0), 0)),
             pl.BlockSpec((tc, W), lambda i: (tb(i), 0)),
             whole((CONV_WIDTH, W)), whole((1, W)), whole(gate_w), whole((1, W)), whole(gate_w), whole((1, W)), whole((1, W))]
    out_specs = [pl.BlockSpec((tc, 2 * W), lambda i: (tb(i), 0)), whole((CONV_WIDTH, W)), whole((1, W)), whole(gate_w),
                 whole((1, W)), whole(gate_w), whole((1, W)), whole((1, W))]
    vec = jax.ShapeDtypeStruct((1, W), F32)
    out_shape = [jax.ShapeDtypeStruct((L, 2 * W), BF16), jax.ShapeDtypeStruct((CONV_WIDTH, W), F32), vec,
                 jax.ShapeDtypeStruct(gate_w, F32), vec, jax.ShapeDtypeStruct(gate_w, F32), vec, vec]

    def body(x_ref, y_ref, xh_ref, hs_ref, hh_ref, dr_ref, cw_ref, cb_ref, wa_ref, ba_ref, wx_ref, bx_ref, lam_ref,
             dxy_ref, dcw_ref, dcb_ref, dwa_ref, dba_ref, dwx_ref, dbx_ref, dlam_ref, gcar, dxc_car, a_scr, b_scr, g_scr):
        pid = pl.program_id(0)
        t = tb(pid)
        accs = (dcw_ref, dcb_ref, dwa_ref, dba_ref, dwx_ref, dbx_ref, dlam_ref)

        @pl.when(pid == 0)
        def _():
            gcar[...] = jnp.zeros_like(gcar)
            dxc_car[...] = jnp.zeros_like(dxc_car)
            for r in accs:
                r[...] = jnp.zeros_like(r)

        halo = jnp.where(t > 0, xh_ref[...].astype(F32), 0.0)
        taps = _conv_taps(jnp.concatenate([halo, x_ref[...].astype(F32)], axis=0), 16)
        xc = cb_ref[...] + sum(cw_ref[k:k + 1, :] * taps[k] for k in range(CONV_WIDTH))
        lam = lam_ref[...]
        sp = _softplus(-lam)
        r, ig, a, om = _lru_gates(xc, wa_ref, ba_ref[...], wx_ref, bx_ref[...], sp)
        sq = jnp.sqrt(om)
        hblk = hs_ref[...]
        hprev = pltpu.roll(jnp.concatenate([jnp.where(t > 0, hh_ref[...], 0.0), hblk], axis=0), 1, 0)[8:]
        gy, dgy = _gelu_parts(y_ref[...].astype(F32))
        drec = dr_ref[...].astype(F32)
        dxy_ref[:, W:] = (drec * hblk * dgy).astype(BF16)

        rowidx = lax.broadcasted_iota(jnp.int32, (tc, W), 0)
        rowmod = rowidx & 7
        ca = jnp.where(rowidx == tc - 1, 1.0, pltpu.roll(a, tc - 1, 0))
        cbv = drec * gy
        for s in (1, 2, 4):
            keep = rowmod < 8 - s
            cbv = jnp.where(keep, ca * pltpu.roll(cbv, tc - s, 0) + cbv, cbv)
            ca = jnp.where(keep, ca * pltpu.roll(ca, tc - s, 0), ca)
        a_scr[...] = ca
        b_scr[...] = cbv

        def tile(k, gc):
            j = tc // 8 - 1 - k
            rows = pl.ds(pl.multiple_of(j * 8, 8), 8)
            gt = a_scr[rows, :] * gc + b_scr[rows, :]
            g_scr[rows, :] = gt
            return jnp.broadcast_to(gt[0:1, :], (8, W))

        lax.fori_loop(0, tc // 8, tile, gcar[...])
        G = g_scr[...]
        gcar[...] = jnp.broadcast_to(a[0:1, :] * G[0:1, :], (8, W))

        da = G * hprev
        dsq = G * (ig * xc)
        di = G * (sq * xc)
        dxc = G * (sq * ig)
        dla = da * a - 2.0 * a * a * (dsq * 0.5 * lax.rsqrt(om))
        dlam_ref[...] += jnp.sum(dla * (-LRU_C * r), axis=0, keepdims=True) * (-_sigmoid(-lam))
        dpr = dla * (-LRU_C * sp) * r * (1.0 - r)
        dpi = di * ig * (1.0 - ig)
        dba_ref[...] += jnp.sum(dpr, axis=0, keepdims=True)
        dbx_ref[...] += jnp.sum(dpi, axis=0, keepdims=True)
        back = []
        for hd in range(LRU_HEADS):
            sl = slice(hd * LRU_HEAD_DIM, (hd + 1) * LRU_HEAD_DIM)
            xh, dprh, dpih = xc[:, sl].astype(BF16), dpr[:, sl].astype(BF16), dpi[:, sl].astype(BF16)
            back.append(_dot(dprh, wa_ref[hd], NT) + _dot(dpih, wx_ref[hd], NT))
            dwa_ref[hd] += _dot(xh, dprh, TN)
            dwx_ref[hd] += _dot(xh, dpih, TN)
        dxc = dxc + jnp.concatenate(back, axis=1)
        dcb_ref[...] += jnp.sum(dxc, axis=0, keepdims=True)
        for k in range(CONV_WIDTH):
            dcw_ref[k:k + 1, :] += jnp.sum(dxc * taps[k], axis=0, keepdims=True)
        ext = jnp.concatenate([dxc, dxc_car[...]], axis=0)
        dx = cw_ref[CONV_WIDTH - 1:CONV_WIDTH, :] * dxc
        for k in range(CONV_WIDTH - 1):
            dx = dx + cw_ref[k:k + 1, :] * pltpu.roll(ext, tc + 8 - (CONV_WIDTH - 1 - k), 0)[:tc]
        dxc_car[...] = dxc[0:8, :]
        dxy_ref[:, :W] = dx.astype(BF16)

    scratch = [pltpu.VMEM((8, W), F32), pltpu.VMEM((8, W), F32)] + [pltpu.VMEM((tc, W), F32)] * 3
    return _call(body, name=name, grid=(nb,), in_specs=specs, out_specs=out_specs, out_shape=out_shape, scratch=scratch,
                 sem=("arbitrary",))(proj, proj, proj, hs, hs, drec_src, cw, cb, wa, ba, wx, bx, lam)


def _final_loss(h, gain, target, *, name, tm=256):
    M, K = h.shape
    row = pl.BlockSpec((tm, K), lambda i: (i, 0))
    vec = pl.BlockSpec((1, K), lambda i: (0, 0))
    one = pl.BlockSpec((1, 128), lambda i: (0, 0))

    def body(h_ref, g_ref, t_ref, dh_ref, dg_ref, loss_ref):
        @pl.when(pl.program_id(0) == 0)
        def _():
            dg_ref[...] = jnp.zeros_like(dg_ref)
            loss_ref[...] = jnp.zeros_like(loss_ref)

        x = h_ref[...]
        r = lax.rsqrt(jnp.mean(x * x, axis=-1, keepdims=True) + NORM_EPS)
        xhat = x * r
        err = xhat * g_ref[...] - t_ref[...]
        loss_ref[...] += 0.5 / K * jnp.sum(err * err)
        dy = err * (1.0 / K)
        dg_ref[...] += jnp.sum(dy * xhat, axis=0, keepdims=True)
        dxh = dy * g_ref[...]
        dh_ref[...] = r * (dxh - xhat * jnp.mean(dxh * xhat, axis=-1, keepdims=True))

    return _call(body, name=name, grid=(M // tm,), in_specs=[row, vec, row], out_specs=[row, vec, one],
                 out_shape=[jax.ShapeDtypeStruct((M, K), F32), jax.ShapeDtypeStruct((1, K), F32),
                            jax.ShapeDtypeStruct((1, 128), F32)], sem=("arbitrary",))(h, gain.reshape(1, K), target)


def _dilated_merge(branches, *, name, tm=512):
    L, W = branches[0].shape
    nbr = len(branches) // 2
    row = pl.BlockSpec((tm, W), lambda i: (i, 0))

    def body(*refs):
        o_ref, lse_ref = refs[-2], refs[-1]
        lses = [refs[2 * b + 1][...] for b in range(nbr)]
        m = lses[0]
        for t in lses[1:]:
            m = jnp.maximum(m, t)
        ws = [jnp.exp(t - m) for t in lses]
        den = ws[0]
        for t in ws[1:]:
            den = den + t
        acc = ws[0] * refs[0][...].astype(F32)
        for b in range(1, nbr):
            acc = acc + ws[b] * refs[2 * b][...].astype(F32)
        o_ref[...] = (acc / den).astype(BF16)
        lse_ref[...] = m + jnp.log(den)

    return _call(body, name=name, grid=(L // tm,), in_specs=[row] * (2 * nbr), out_specs=[row, row],
                 out_shape=[jax.ShapeDtypeStruct((L, W), BF16), jax.ShapeDtypeStruct((L, W), F32)], sem=("parallel",))(*branches)


def _dilated_fwd(proj0):
    L = proj0.shape[0]
    qkv = proj0[:, 2 * D_MODEL:]
    W = B_HEADS * HEAD_DIM
    outs = []
    for window, d in DILATED_PATTERN:
        view = qkv.reshape(L // d, d * 3 * W)
        o, lse = _band_fwd(view, view, view, d=d, nq=B_HEADS, nkv=B_HEADS, qcol=lambda r: 3 * r, kcol=lambda r: 3 * r + 1,
                           vcol=lambda r: 3 * r + 2, max_dist=window // d, name=f"dilated_fwd_d{d}")
        outs += [o.reshape(L, W), lse.reshape(L, W)]
    return _dilated_merge(outs, name="dilated_merge")


def _dilated_bwd(proj0, att, lse, datt, tabs):
    L = proj0.shape[0]
    qkv = proj0[:, 2 * D_MODEL:]
    Wh = B_HEADS * HEAD_DIM
    branches = []
    for window, d in DILATED_PATTERN:
        view = qkv.reshape(L // d, d * 3 * Wh)
        v1 = lambda t: t.reshape(L // d, d * Wh)
        outs = _band_bwd(view, view, view, v1(datt), v1(att), v1(lse), d=d, nq=B_HEADS, nkv=B_HEADS,
                         qcol=lambda r: 3 * r, kcol=lambda r: 3 * r + 1, vcol=lambda r: 3 * r + 2, docol=lambda r: r,
                         max_dist=window // d, name=f"dilated_bwd_d{d}")
        branches.append([o.reshape(L, Wh) for o in outs])
    return _attn_grad_combine(branches, tabs, name="dilated_grad_combine")


def _device_step(x, mem, target, w, on_grads=None):
    L = x.shape[0]
    tabs = _rope_tables(L)
    g = {}
    saved = []
    h = x
    for layer in range(2):
        sv = {"h_mix": h}
        if layer == 0:
            proj, n = _rowmm(h, w["ab_w_in"], name="l0_in_proj", gain=w["mix_norm"][0],
                             rope=(2 * D_MODEL, 2 * D_MODEL + 2 * B_HEADS * HEAD_DIM, tabs))
            rec, hs = _lru_fwd(proj, w["lru_conv_w"], w["lru_conv_b"], w["lru_wa"], w["lru_ba"], w["lru_wx"], w["lru_bx"],
                               w["lru_lambda"], name="lru_fwd")
            att, lse = _dilated_fwd(proj)
            mix = jnp.concatenate([rec, att], axis=1)
            (h,) = _rowmm(mix, w["ab_w_out"], name="l0_out_proj", res=h)
            sv.update(hs=hs)
        else:
            proj, n = _rowmm(h, w["c_w_qkv"], name="l1_qkv_proj", gain=w["mix_norm"][1], bias=w["c_b_qkv"],
                             rope=(0, (C_HEADS + C_KV_HEADS) * HEAD_DIM, tabs))
            mix, lse = _band_fwd(proj, proj, proj, d=1, nq=C_HEADS, nkv=C_KV_HEADS, qcol=lambda r: 0, kcol=lambda r: 8,
                                 vcol=lambda r: 9, max_dist=C_WINDOW - 1, sinks=w["c_sinks"], name="swa_fwd")
            (h,) = _rowmm(mix, w["c_w_out"], name="l1_out_proj", res=h, bias=w["c_b_out"])
        sv.update(proj=proj, n_mix=n, mix=mix, lse=lse, h_xa=h)
        xq, nx = _rowmm(h, w["xa_wq"][layer][None], name=f"xa_q_proj{layer}", gain=w["xa_norm"][layer])
        kv, nm = _rowmm(mem, w["xa_wkv"][layer][None], name=f"xa_kv_proj{layer}", gain=w["xa_mem_norm"][layer])
        xo, xlse = _xattn_fwd(xq, kv, name=f"xa_fwd{layer}")
        (h,) = _rowmm(xo, w["xa_wo"][layer], name=f"xa_out_proj{layer}", res=h)
        sv.update(xq=xq, nx=nx, kv=kv, nm=nm, xo=xo, xlse=xlse, h_ffn=h)
        gu, nf, act = _rowmm(h, w["ffn_w_gate_up"], layer=layer, name=f"ffn_in{layer}", gain=w["ffn_norm"][layer], swiglu=True)
        (h,) = _rowmm(act, w["ffn_w_down"][layer][None], name=f"ffn_out{layer}", res=h, tm=512)
        sv.update(gu=gu, nf=nf, act=act)
        saved.append(sv)

    dh, g["final_norm"], loss = _final_loss(h, w["final_norm"], target, name="final_loss")

    stk = {k: [None, None] for k in ("xa_norm", "xa_mem_norm", "ffn_norm", "mix_norm")}
    after = None
    for layer in (1, 0):
        sv = saved[layer]
        (g["ffn_w_down", layer],) = _mm_tn(sv["act"], dh, S=1, name=f"ffn_down_dw{layer}", kk=D_FF // 2)
        (dgu,) = _mm_nt(dh, w["ffn_w_down"][layer][None], name=f"ffn_dact{layer}", mode="swiglu", kchunk=D_FF // 2, gu=sv["gu"],
                        after=after)
        (g["ffn_w_gate_up", layer],) = _mm_tn(sv["nf"], dgu, S=N_CHIPS, name=f"ffn_gu_dw{layer}")
        dh, stk["ffn_norm"][layer] = _mm_nt(dgu, w["ffn_w_gate_up"], layer=layer, name=f"ffn_dx{layer}", mode="norm",
                                            h=sv["h_ffn"], gain=w["ffn_norm"][layer], dh=dh)
        (g["xa_wo", layer],) = _mm_tn(sv["xo"], dh, S=N_CHIPS, name=f"xa_wo_dw{layer}")
        (dxo,) = _mm_nt(dh, w["xa_wo"][layer], name=f"xa_dxo{layer}", mode="plain")
        dxq, dkv = _xattn_bwd(sv["xq"], sv["kv"], sv["xo"], sv["xlse"], dxo, name=f"xa_bwd{layer}")
        (g["xa_wq", layer],) = _mm_tn(sv["nx"], dxq, S=1, name=f"xa_wq_dw{layer}")
        dh, stk["xa_norm"][layer] = _mm_nt(dxq, w["xa_wq"][layer][None], name=f"xa_dx{layer}", mode="norm", h=sv["h_xa"],
                                           gain=w["xa_norm"][layer], dh=dh)
        (g["xa_wkv", layer],) = _mm_tn(sv["nm"], dkv, S=1, name=f"xa_wkv_dw{layer}")
        _, stk["xa_mem_norm"][layer] = _mm_nt(dkv, w["xa_wkv"][layer][None], name=f"xa_dmem{layer}", mode="norm", h=mem,
                                              gain=w["xa_mem_norm"][layer])
        if layer == 1:
            g["c_w_out"], g["c_b_out"] = _mm_tn(sv["mix"], dh, S=1, name="l1_out_dw", bias=True)
            (dmix,) = _mm_nt(dh, w["c_w_out"], name="l1_dmix", mode="plain")
            dq, dk, dv, dsk = _band_bwd(sv["proj"], sv["proj"], sv["proj"], dmix, sv["mix"], sv["lse"], d=1, nq=C_HEADS,
                                        nkv=C_KV_HEADS, qcol=lambda r: 0, kcol=lambda r: 8, vcol=lambda r: 9,
                                        docol=lambda r: 0, max_dist=C_WINDOW - 1, sinks=w["c_sinks"], name="swa_bwd")
            g["c_sinks"] = dsk[0, :C_HEADS]
            dproj = _attn_grad_combine([(dq, dk, dv)], tabs, name="swa_grad_combine")
            g["c_w_qkv"], g["c_b_qkv"] = _mm_tn(sv["n_mix"], dproj, S=1, name="l1_qkv_dw", bias=True)
            dh, stk["mix_norm"][1] = _mm_nt(dproj, w["c_w_qkv"], name="l1_dx", mode="norm", h=sv["h_mix"],
                                            gain=w["mix_norm"][1], dh=dh)
            if on_grads is not None:
                after = on_grads("layer1", g)
        else:
            if on_grads is not None:
                after = on_grads("layer0_ffn_xa", g)
            (g["ab_w_out"],) = _mm_tn(sv["mix"], dh, S=1, name="l0_out_dw", kk=768)
            (dmix,) = _mm_nt(dh, w["ab_w_out"], name="l0_dmix", mode="plain", kchunk=768, after=after)
            (dxy, g["lru_conv_w"], g["lru_conv_b"], g["lru_wa"], g["lru_ba"], g["lru_wx"], g["lru_bx"],
             g["lru_lambda"]) = _lru_bwd(sv["proj"], sv["hs"], dmix, w["lru_conv_w"], w["lru_conv_b"], w["lru_wa"],
                                         w["lru_ba"], w["lru_wx"], w["lru_bx"], w["lru_lambda"], name="lru_bwd")
            dqkv = _dilated_bwd(sv["proj"], sv["mix"][:, D_MODEL:], sv["lse"], dmix[:, D_MODEL:], tabs)
            dproj = jnp.concatenate([dxy, dqkv], axis=1)
            (g["ab_w_in"],) = _mm_tn(sv["n_mix"], dproj, S=N_CHIPS, name="l0_in_dw")
            dh, stk["mix_norm"][0] = _mm_nt(dproj, w["ab_w_in"], name="l0_dx", mode="norm", h=sv["h_mix"],
                                            gain=w["mix_norm"][0], dh=dh)
    for k, v in stk.items():
        g[k] = jnp.concatenate(v, axis=0)
    return loss[0, 0], dh, g


ANY = pl.BlockSpec(memory_space=pl.ANY)
MESH = pl.DeviceIdType.MESH


def _place():
    x, y, c = lax.axis_index("x"), lax.axis_index("y"), lax.axis_index("c")
    return x, y, c, [(1 - x, y), (x, 1 - y), (1 - x, 1 - y)]


def _remote(send_sems, recv_sems):
    def copy(k, src, dst, to):
        return pltpu.make_async_remote_copy(src_ref=src, dst_ref=dst, send_sem=send_sems.at[k], recv_sem=recv_sems.at[k],
                                            device_id=to, device_id_type=MESH)
    return copy


def _halves(ref, n_rows):
    rh = n_rows // 2
    return lambda lead, hh: ref.at[(*lead, pl.ds(hh * rh, rh), slice(None))]


def _gather_weights(packs, spack):
    n = len(packs)

    def body(*refs):
        w_refs, s_ref, wf_refs, sf_ref = refs[:n], refs[n], refs[n + 1:2 * n + 1], refs[2 * n + 1]
        x, y, c, chips = _place()
        me, sib = 2 * x + y, (x, y, 1 - c)
        copy = _remote(*refs[-2:])
        src = [_halves(w_refs[g], packs[g].shape[0]) for g in range(n)]
        dst = [_halves(wf_refs[g], packs[g].shape[0]) for g in range(n)]
        sends = []
        for g in range(n):
            for j, (cx, cy) in enumerate(chips):
                sends.append(copy(3 * g + j, src[g]((), c), dst[g]((me,), c), (cx, cy, c)))
        for j, (cx, cy) in enumerate(chips):
            sends.append(copy(6 * n + j, s_ref, sf_ref.at[me], (cx, cy, c)))
        for cp in sends:
            cp.start()
        for g in range(n):
            for j, (cx, cy) in enumerate(chips):
                got = dst[g]((2 * cx + cy,), c)
                copy(3 * g + j, got, got, sib).wait_recv()
                fwd = copy(3 * n + 3 * g + j, got, got, sib)
                fwd.start()
                sends.append(fwd)
        for g in range(n):
            for j, (cx, cy) in enumerate(chips):
                got = dst[g]((2 * cx + cy,), 1 - c)
                copy(3 * n + 3 * g + j, got, got, sib).wait_recv()
        for j, (cx, cy) in enumerate(chips):
            copy(6 * n + j, s_ref, sf_ref.at[2 * cx + cy], sib).wait_recv()
        for cp in sends:
            cp.wait_send()

    ins = list(packs) + [spack]
    out_shape = [jax.ShapeDtypeStruct((N_CHIPS,) + a.shape, a.dtype) for a in ins]
    n_sems = 6 * n + 3
    outs = pl.pallas_call(body, name="gather_weights", out_shape=out_shape, in_specs=[ANY] * len(ins),
                          out_specs=[ANY] * len(ins),
                          scratch_shapes=[pltpu.SemaphoreType.DMA((n_sems,)), pltpu.SemaphoreType.DMA((n_sems,))])(*ins)
    chip = 2 * lax.axis_index("x") + lax.axis_index("y")
    outs = [lax.dynamic_update_index_in_dim(o, a, chip, 0) for o, a in zip(outs, ins)]
    return outs[:n], outs[n]


SEQUENCER_GATHER_IDS = {"mid": 1, "late": 5}


def _gather_weights_behind(packs, *, tag):
    n = len(packs)

    def body(*refs):
        w_refs, wf_refs = refs[:n], refs[n:2 * n]
        x, y, c, chips = _place()
        me, sib = 2 * x + y, (x, y, 1 - c)
        barrier = pltpu.get_barrier_semaphore()
        for peer in [(cx, cy, c) for cx, cy in chips] + [sib]:
            pl.semaphore_signal(barrier, inc=1, device_id=peer, device_id_type=MESH)
        pl.semaphore_wait(barrier, len(chips) + 1)
        copy = _remote(*refs[-2:])
        src = [_halves(w_refs[g], packs[g].shape[0]) for g in range(n)]
        dst = [_halves(wf_refs[g], packs[g].shape[0]) for g in range(n)]
        sends = []
        for g in range(n):
            for j, (cx, cy) in enumerate(chips):
                sends.append(copy(3 * g + j, src[g]((), c), dst[g]((me,), c), (cx, cy, c)))
        for cp in sends:
            cp.start()
        for g in range(n):
            for j, (cx, cy) in enumerate(chips):
                got = dst[g]((2 * cx + cy,), c)
                copy(3 * g + j, got, got, sib).wait_recv()
                fwd = copy(3 * n + 3 * g + j, got, got, sib)
                fwd.start()
                sends.append(fwd)
        for g in range(n):
            for j, (cx, cy) in enumerate(chips):
                got = dst[g]((2 * cx + cy,), 1 - c)
                copy(3 * n + 3 * g + j, got, got, sib).wait_recv()
        for cp in sends:
            cp.wait_send()

    out_type = [jax.ShapeDtypeStruct((N_CHIPS,) + a.shape, a.dtype) for a in packs]
    outs = pl.kernel(body, out_type=out_type, mesh=plsc.ScalarSubcoreMesh(axis_name="sequencer", num_cores=1),
                     name="gather_weights_behind_" + tag,
                     scratch_types=[pltpu.SemaphoreType.DMA((6 * n,)), pltpu.SemaphoreType.DMA((6 * n,))],
                     compiler_params=pltpu.CompilerParams(collective_id=SEQUENCER_GATHER_IDS[tag]))(*packs)
    chip = 2 * lax.axis_index("x") + lax.axis_index("y")
    return [lax.dynamic_update_index_in_dim(o, a, chip, 0) for o, a in zip(outs, packs)]


def _rs_pair_exchange(gpacks, *, name):
    n = len(gpacks)

    def body(*refs):
        g_refs, ra_refs = refs[:n], refs[n:2 * n]
        x, y, c, _ = _place()
        copy = _remote(*refs[-2:])
        cps = []
        for g in range(n):
            half = _halves(g_refs[g], gpacks[g].shape[1])
            cps += [copy(N_CHIPS * g + j, half((j,), 1 - c), ra_refs[g].at[j], (x, y, 1 - c)) for j in range(N_CHIPS)]
        for cp in cps:
            cp.start()
        for cp in cps:
            cp.wait()

    out_shape = [jax.ShapeDtypeStruct((N_CHIPS, a.shape[1] // 2, a.shape[2]), a.dtype) for a in gpacks]
    n_sems = N_CHIPS * n
    return pl.pallas_call(body, name=name, out_shape=out_shape, in_specs=[ANY] * n, out_specs=[ANY] * n,
                          scratch_shapes=[pltpu.SemaphoreType.DMA((n_sems,)), pltpu.SemaphoreType.DMA((n_sems,))])(*gpacks)


def _row_tile(rows, cap=512):
    return max(t for t in range(16, min(rows, cap) + 1, 16) if rows % t == 0)


def _rs_pair_add(place, gpack, ra, *, name):
    _, R, C = gpack.shape
    Rh = R // 2
    tr = _row_tile(Rh)
    nrb = Rh // tr

    def body(p_ref, g_ref, ra_ref, pair_ref, own_ref):
        s = g_ref[...].astype(F32) + ra_ref[...].astype(F32)
        pair_ref[...] = s.astype(BF16)

        @pl.when(pl.program_id(1) == p_ref[1])
        def _():
            own_ref[...] = s

    grid_spec = pltpu.PrefetchScalarGridSpec(
        num_scalar_prefetch=1, grid=(nrb, N_CHIPS),
        in_specs=[pl.BlockSpec((None, tr, C), lambda i, j, p: (j, p[0] * nrb + i, 0)),
                  pl.BlockSpec((None, tr, C), lambda i, j, p: (j, i, 0))],
        out_specs=[pl.BlockSpec((None, tr, C), lambda i, j, p: (j, i, 0)), pl.BlockSpec((tr, C), lambda i, j, p: (i, 0))])
    return pl.pallas_call(
        body, name=name, grid_spec=grid_spec,
        out_shape=[jax.ShapeDtypeStruct((N_CHIPS, Rh, C), BF16), jax.ShapeDtypeStruct((Rh, C), F32)],
        compiler_params=pltpu.CompilerParams(dimension_semantics=("arbitrary", "arbitrary"),
                                             vmem_limit_bytes=VMEM_LIMIT_V7X))(place, gpack, ra)


SEQUENCER_EXCHANGE_IDS = {"l1": 2, "l0a": 3, "l0b": 4}


def _rs_chip_exchange_behind(pairs, *, tag, small=None):
    n = len(pairs)
    has_small = small is not None

    def body(*refs):
        p_refs = refs[:n]
        s_ref = refs[n] if has_small else None
        rb_refs = refs[n + has_small:2 * n + has_small]
        rs_ref = refs[2 * n + 1] if has_small else None
        x, y, c, chips = _place()
        peers = [(1 - x if k & 4 else x, 1 - y if k & 2 else y, 1 - c if k & 1 else c) for k in range(1, 8)]
        shake = peers if has_small else [(cx, cy, c) for cx, cy in chips]
        barrier = pltpu.get_barrier_semaphore()
        for peer in shake:
            pl.semaphore_signal(barrier, inc=1, device_id=peer, device_id_type=MESH)
        pl.semaphore_wait(barrier, len(shake))
        copy = _remote(*refs[-2:])
        cps = []
        for g in range(n):
            cps += [copy(3 * g + j, p_refs[g].at[2 * cx + cy], rb_refs[g].at[j], (cx, cy, c)) for j, (cx, cy) in enumerate(chips)]
        if has_small:
            dev = 4 * x + 2 * y + c
            cps += [copy(3 * n + k, s_ref, rs_ref.at[dev], peer) for k, peer in enumerate(peers)]
        for cp in cps:
            cp.start()
        for g in range(n):
            for j in range(3):
                copy(3 * g + j, p_refs[g].at[0], rb_refs[g].at[j], (x, y, c)).wait_recv()
        if has_small:
            for k, (px, py, pc) in enumerate(peers):
                copy(3 * n + k, s_ref, rs_ref.at[4 * px + 2 * py + pc], (x, y, c)).wait_recv()
        for cp in cps:
            cp.wait_send()

    ins = list(pairs) + ([small] if has_small else [])
    out_type = [jax.ShapeDtypeStruct((3,) + p.shape[1:], p.dtype) for p in pairs]
    if has_small:
        out_type.append(jax.ShapeDtypeStruct((8,) + small.shape, small.dtype))
    n_sems = 3 * n + 7 * has_small
    outs = pl.kernel(body, out_type=out_type, mesh=plsc.ScalarSubcoreMesh(axis_name="sequencer", num_cores=1),
                     name="rs_chip_exchange_behind_" + tag,
                     scratch_types=[pltpu.SemaphoreType.DMA((n_sems,)), pltpu.SemaphoreType.DMA((n_sems,))],
                     compiler_params=pltpu.CompilerParams(collective_id=SEQUENCER_EXCHANGE_IDS[tag]))(*ins)
    if has_small:
        dev = 4 * lax.axis_index("x") + 2 * lax.axis_index("y") + lax.axis_index("c")
        outs = list(outs[:n]) + [lax.dynamic_update_index_in_dim(outs[n], small, dev, 0)]
    return outs


def _rs_final_add(place, own, rb, *, name):
    Rh, C = own.shape
    tr = _row_tile(Rh)
    nrb = Rh // tr

    def body(p_ref, o_ref, rb_ref, f_ref):
        f_ref[...] = ((o_ref[...] + rb_ref[0].astype(F32)) + rb_ref[1].astype(F32)) + rb_ref[2].astype(F32)

    grid_spec = pltpu.PrefetchScalarGridSpec(
        num_scalar_prefetch=1, grid=(nrb,),
        in_specs=[pl.BlockSpec((tr, C), lambda i, p: (i, 0)), pl.BlockSpec((3, tr, C), lambda i, p: (0, i, 0))],
        out_specs=pl.BlockSpec((tr, C), lambda i, p: (p[0] * nrb + i, 0)))
    return pl.pallas_call(
        body, name=name, grid_spec=grid_spec, out_shape=jax.ShapeDtypeStruct((2 * Rh, C), F32),
        compiler_params=pltpu.CompilerParams(dimension_semantics=("arbitrary",), vmem_limit_bytes=VMEM_LIMIT_V7X))(place, own, rb)


def _sum_slots(rs):
    n, rows, C = rs.shape

    def body(r_ref, o_ref):
        acc = r_ref[0]
        for k in range(1, n):
            acc = acc + r_ref[k]
        o_ref[...] = acc

    return _call(body, name="small_grad_sum", grid=(1,), in_specs=[pl.BlockSpec((n, rows, C), lambda i: (0, 0, 0))],
                 out_specs=pl.BlockSpec((rows, C), lambda i: (0, 0)), out_shape=jax.ShapeDtypeStruct((rows, C), F32),
                 sem=("arbitrary",))(rs)


def _rs_sibling_share(gbufs, *, name):
    n = len(gbufs)

    def body(*refs):
        g_refs = refs[n:2 * n]
        x, y, c, _ = _place()
        copy = _remote(*refs[-2:])
        halves = [_halves(g_refs[g], gbufs[g].shape[0]) for g in range(n)]
        outs = [copy(g, halves[g]((), c), halves[g]((), c), (x, y, 1 - c)) for g in range(n)]
        for cp in outs:
            cp.start()
        for g in range(n):
            copy(g, halves[g]((), 1 - c), halves[g]((), 1 - c), (x, y, c)).wait_recv()
        for cp in outs:
            cp.wait_send()

    return pl.pallas_call(body, name=name, out_shape=[jax.ShapeDtypeStruct(a.shape, a.dtype) for a in gbufs],
                          in_specs=[ANY] * n, out_specs=[ANY] * n, input_output_aliases={g: g for g in range(n)},
                          scratch_shapes=[pltpu.SemaphoreType.DMA((n,)), pltpu.SemaphoreType.DMA((n,))])(*gbufs)


def _adamw(w, g, m, v, *, name, g_row=0):
    rows, cols = w.shape
    tr = rows
    for cand in range(min(rows, 512), 7, -8):
        if rows % cand == 0 and g_row % cand == 0:
            tr = cand
            break
    spec = pl.BlockSpec((tr, cols), lambda i: (i, 0))
    g_spec = pl.BlockSpec((tr, cols), lambda i: (g_row // tr + i, 0))

    def body(w_ref, g_ref, m_ref, v_ref, d_ref, nm_ref, nv_ref):
        gg = g_ref[...]
        nm = ADAM_B1 * m_ref[...] + (1.0 - ADAM_B1) * gg
        nv = ADAM_B2 * v_ref[...] + (1.0 - ADAM_B2) * (gg * gg)
        m_hat = nm / (1.0 - ADAM_B1 ** ADAM_STEP)
        v_hat = nv / (1.0 - ADAM_B2 ** ADAM_STEP)
        d_ref[...] = -ADAM_LR * (m_hat / (jnp.sqrt(v_hat) + ADAM_EPS) + ADAM_WD * w_ref[...])
        nm_ref[...] = nm
        nv_ref[...] = nv

    return _call(body, name=name, grid=(rows // tr,), in_specs=[spec, g_spec, spec, spec], out_specs=[spec] * 3,
                 out_shape=[jax.ShapeDtypeStruct((rows, cols), F32)] * 3, sem=("parallel",))(w, g, m, v)


WEIGHT_NAMES = ("mix_norm", "ab_w_in", "lru_conv_w", "lru_conv_b", "lru_wa", "lru_ba", "lru_wx", "lru_bx", "lru_lambda",
                "ab_w_out", "c_w_qkv", "c_b_qkv", "c_sinks", "c_w_out", "c_b_out", "xa_norm", "xa_mem_norm", "xa_wq",
                "xa_wkv", "xa_wo", "ffn_norm", "ffn_w_gate_up", "ffn_w_down", "final_norm")
EARLY_GROUPS = (("ab_w_in",),)
MID_GROUPS = (("ab_w_out",), ("lru_wa", "lru_wx"))
LATE_GROUPS = (("c_w_out", "xa_wkv", "ffn_w_down"), ("ffn_w_gate_up",), ("xa_wo",), ("xa_wq",), ("c_w_qkv",))
GROUPS = EARLY_GROUPS + MID_GROUPS + LATE_GROUPS
REPLICATED = ("mix_norm", "lru_conv_b", "lru_lambda", "c_sinks", "xa_norm", "xa_mem_norm", "ffn_norm", "final_norm")
SMALL_SHARDED = ("lru_conv_w", "lru_ba", "lru_bx", "c_b_qkv", "c_b_out")
LANES = 1024


def _rows(v):
    flat = v.reshape(-1)
    return jnp.pad(flat, (0, -flat.shape[0] % LANES)).reshape(-1, LANES)


def _pack_small(parts, total, *, name):
    def body(*refs):
        o_ref = refs[-1]
        o_ref[...] = jnp.zeros_like(o_ref)
        row = 0
        for p_ref in refs[:-1]:
            o_ref[row:row + p_ref.shape[0], :] = p_ref[...]
            row += p_ref.shape[0]

    return _call(body, name=name, grid=(1,), in_specs=[pl.BlockSpec(p.shape, lambda i: (0, 0)) for p in parts],
                 out_specs=pl.BlockSpec((total, LANES), lambda i: (0, 0)),
                 out_shape=jax.ShapeDtypeStruct((total, LANES), F32), sem=("arbitrary",))(*parts)


def _from_shards(name, t):
    minor = t.shape[-1]
    if name == "ab_w_in":
        return t
    if name in ("ab_w_out", "c_w_out"):
        return t.reshape(1, -1, minor)
    if name == "ffn_w_gate_up":
        return t.reshape(N_CHIPS, 2, -1, minor)
    if name in ("xa_wq", "xa_wkv", "ffn_w_down"):
        return t.reshape(N_CHIPS, 2, -1, minor).transpose(1, 0, 2, 3).reshape(2, -1, minor)
    if name in ("lru_wa", "lru_wx"):
        return t.reshape(N_CHIPS, LRU_HEADS, -1, minor).transpose(1, 0, 2, 3).reshape(LRU_HEADS, LRU_HEAD_DIM, minor)
    if name == "xa_wo":
        return t.reshape(N_CHIPS, 2, -1, minor).transpose(1, 0, 2, 3)
    assert name == "c_w_qkv"
    return t.transpose(1, 0, 2).reshape(1, D_MODEL, -1)


def _piece_shards(name, g):
    minor = g.shape[-1]
    if name in ("ab_w_in", "ffn_w_gate_up", "xa_wo"):
        return g
    if name in ("ab_w_out", "c_w_out", "xa_wq", "xa_wkv", "ffn_w_down"):
        return g.reshape(N_CHIPS, -1, minor)
    if name in ("lru_wa", "lru_wx"):
        return g.reshape(LRU_HEADS, N_CHIPS, -1, minor).transpose(1, 0, 2, 3).reshape(N_CHIPS, -1, minor)
    assert name == "c_w_qkv"
    return g.reshape(D_MODEL, N_CHIPS, -1).transpose(1, 0, 2)


RS_SETS = {
    "l1": ((("c_w_out", None), ("xa_wkv", 1), ("ffn_w_down", 1)), (("ffn_w_gate_up", 1),), (("xa_wo", 1),),
           (("xa_wq", 1),), (("c_w_qkv", None),)),
    "l0a": ((("xa_wkv", 0), ("ffn_w_down", 0)), (("ffn_w_gate_up", 0),), (("xa_wo", 0),), (("xa_wq", 0),)),
    "l0b": ((("ab_w_out", None),), (("ab_w_in", None),), (("lru_wa", None), ("lru_wx", None))),
}
RS_STAGE = {"layer1": "l1", "layer0_ffn_xa": "l0a"}


def kernel(x, mem, mix_norm, ab_w_in, lru_conv_w, lru_conv_b, lru_wa, lru_ba, lru_wx, lru_bx, lru_lambda, ab_w_out, c_w_qkv, c_b_qkv, c_sinks, c_w_out, c_b_out, xa_norm, xa_mem_norm, xa_wq, xa_wkv, xa_wo, ffn_norm, ffn_w_gate_up, ffn_w_down, final_norm, loss_target, m_mix_norm, m_ab_w_in, m_lru_conv_w, m_lru_conv_b, m_lru_wa, m_lru_ba, m_lru_wx, m_lru_bx, m_lru_lambda, m_ab_w_out, m_c_w_qkv, m_c_b_qkv, m_c_sinks, m_c_w_out, m_c_b_out, m_xa_norm, m_xa_mem_norm, m_xa_wq, m_xa_wkv, m_xa_wo, m_ffn_norm, m_ffn_w_gate_up, m_ffn_w_down, m_final_norm, v_mix_norm, v_ab_w_in, v_lru_conv_w, v_lru_conv_b, v_lru_wa, v_lru_ba, v_lru_wx, v_lru_bx, v_lru_lambda, v_ab_w_out, v_c_w_qkv, v_c_b_qkv, v_c_sinks, v_c_w_out, v_c_b_out, v_xa_norm, v_xa_mem_norm, v_xa_wq, v_xa_wkv, v_xa_wo, v_ffn_norm, v_ffn_w_gate_up, v_ffn_w_down, v_final_norm):
    given = dict(locals())
    wl = {n: given[n] for n in WEIGHT_NAMES}
    ml = {n: given["m_" + n] for n in WEIGHT_NAMES}
    vl = {n: given["v_" + n] for n in WEIGHT_NAMES}
    xi, yi, ci = lax.axis_index("x"), lax.axis_index("y"), lax.axis_index("c")
    chip = 2 * xi + yi

    def join(parts, axis):
        return parts[0] if len(parts) == 1 else jnp.concatenate(parts, axis=axis)

    local_rows = {n: wl[n].size // wl[n].shape[-1] for grp in GROUPS for n in grp}
    packs = [join([wl[n].astype(BF16).reshape(local_rows[n], wl[n].shape[-1]) for n in grp], 0) for grp in GROUPS]
    spack = _pack_small([_rows(wl[n]) for n in SMALL_SHARDED], 8, name="pack_small_weights")
    n_early, n_mid = len(EARLY_GROUPS), len(EARLY_GROUPS) + len(MID_GROUPS)
    early, sfull = _gather_weights(packs[:n_early], spack)
    early, sfull, mid_packs = lax.optimization_barrier((early, sfull, packs[n_early:n_mid]))
    mid = _gather_weights_behind(mid_packs, tag="mid")
    mid, late_packs = lax.optimization_barrier((mid, packs[n_mid:]))
    gathered = early + mid + _gather_weights_behind(late_packs, tag="late")
    w = {n: wl[n] for n in REPLICATED}
    w["c_sinks"] = wl["c_sinks"][0]
    for grp, full in zip(GROUPS, gathered):
        off = 0
        for n in grp:
            w[n] = _from_shards(n, full if len(grp) == 1 else full[:, off:off + local_rows[n]])
            off += local_rows[n]
    for r, n in enumerate(SMALL_SHARDED):
        loc = wl[n].shape[1:]
        t = sfull[:, r, :wl[n].size].reshape((N_CHIPS,) + loc)
        if n == "lru_conv_w":
            w[n] = t.transpose(1, 0, 2).reshape(CONV_WIDTH, -1)
        elif n in ("lru_ba", "lru_bx"):
            w[n] = t.transpose(1, 0, 2).reshape(1, -1)
        else:
            w[n] = t.reshape(1, -1)

    place = jnp.stack([ci, chip]).astype(jnp.int32)

    def pair_stage(spec, g, tag):
        piece = lambda n, l: (g[n] if l is None else g[n, l]).astype(BF16)
        gpacks = [join([_piece_shards(n, piece(n, l)) for n, l in grp], 1) for grp in spec]
        ras = _rs_pair_exchange(gpacks, name=f"rs_pair_exchange_{tag}")
        sums = [_rs_pair_add(place, gp, ra, name=f"rs_pair_add_{tag}_{i}") for i, (gp, ra) in enumerate(zip(gpacks, ras))]
        return [pair for pair, _ in sums], [own for _, own in sums]

    reduced, in_flight = [], []

    def take_up():
        done = [_rs_final_add(place, o, r, name=f"rs_final_add_{len(reduced) + i}") for i, (o, r) in enumerate(in_flight)]
        reduced.extend(done)
        in_flight.clear()
        return done

    def reduce_behind(stage, g):
        done = take_up()
        tag = RS_STAGE[stage]
        pairs, own = pair_stage(RS_SETS[tag], g, tag)
        in_flight.extend(zip(own, _rs_chip_exchange_behind(pairs, tag=tag)))
        return own + done

    loss_part, grad_x, g = _device_step(x[0], mem[0], loss_target[0], w, on_grads=reduce_behind)

    small_parts = [_rows(g[n]) for n in REPLICATED] + [_rows(jnp.broadcast_to(loss_part, (LANES,)))]
    small_parts += [_rows(g[n]) for n in SMALL_SHARDED]
    small = _pack_small(small_parts, 24, name="pack_small_grads")
    take_up()
    pairs, own = pair_stage(RS_SETS["l0b"], g, "l0b")
    *rb, rs = _rs_chip_exchange_behind(pairs, tag="l0b", small=small)
    gsums = list(_rs_sibling_share(list(reduced), name="rs_sibling_share_behind"))
    in_flight.extend(zip(own, rb))
    gsums += list(_rs_sibling_share(take_up(), name="rs_sibling_share_last"))
    ssum = _sum_slots(rs)

    where = {}
    for grp, gsum in zip(RS_SETS["l1"] + RS_SETS["l0a"] + RS_SETS["l0b"], gsums):
        off = 0
        for n, l in grp:
            rows = local_rows[n] if l is None else local_rows[n] // 2
            where[n, l] = (gsum, off, rows, len(grp) == 1)
            off += rows
    take = lambda gsum, off, rows, whole: gsum if whole else gsum[off:off + rows]
    grads, grad_rows = {}, {}
    for grp in LATE_GROUPS + EARLY_GROUPS + MID_GROUPS:
        for n in grp:
            if (n, None) in where:
                grads[n] = take(*where[n, None]).reshape(wl[n].shape)
                grad_rows[n] = where[n, None][:2]
            else:
                grads[n] = jnp.stack([take(*where[n, l]).reshape(wl[n].shape[1:]) for l in range(2)])
                grad_rows[n] = (grads[n].reshape(local_rows[n], wl[n].shape[-1]), 0)
    row = 0
    for n in REPLICATED:
        k = _rows(g[n]).shape[0]
        grads[n] = ssum[row:row + k].reshape(-1)[:wl[n].size].reshape(wl[n].shape)
        row += k
    loss = ssum[row, 0]
    row += 1
    for n in SMALL_SHARDED:
        k = _rows(g[n]).shape[0]
        full = ssum[row:row + k].reshape(-1)[:g[n].size]
        row += k
        loc = wl[n].shape
        if n == "lru_conv_w":
            sh = full.reshape(CONV_WIDTH, N_CHIPS, -1)
        elif n in ("lru_ba", "lru_bx"):
            sh = full.reshape(LRU_HEADS, N_CHIPS, -1)
        else:
            sh = full.reshape(1, N_CHIPS, -1)
        grads[n] = lax.dynamic_index_in_dim(sh, chip, axis=1, keepdims=False).reshape(loc)

    delta, new_m, new_v = {}, {}, {}
    for n, (gsum, off) in grad_rows.items():
        shape2 = (local_rows[n], wl[n].shape[-1])
        d, nm, nv = _adamw(wl[n].reshape(shape2), gsum, ml[n].reshape(shape2), vl[n].reshape(shape2), g_row=off,
                           name="adamw_" + n)
        delta[n], new_m[n], new_v[n] = (t.reshape(wl[n].shape) for t in (d, nm, nv))
    smalls = REPLICATED + SMALL_SHARDED
    packs = [_pack_small([_rows(src[n]) for n in smalls], 24, name="pack_adamw_" + tag)
             for tag, src in (("w", wl), ("g", grads), ("m", ml), ("v", vl))]
    outs = _adamw(*packs, name="adamw_small")
    row = 0
    for n in smalls:
        k = _rows(wl[n]).shape[0]
        for dst, o in zip((delta, new_m, new_v), outs):
            dst[n] = o[row:row + k].reshape(-1)[:wl[n].size].reshape(wl[n].shape)
        row += k

    return (loss, grad_x[None], *[grads[n] for n in WEIGHT_NAMES], *[delta[n] for n in WEIGHT_NAMES],
            *[new_m[n] for n in WEIGHT_NAMES], *[new_v[n] for n in WEIGHT_NAMES])
```

```python
import jax
import jax.numpy as jnp
from jax import lax
from jax.experimental import pallas as pl
from jax.experimental.pallas import tpu as pltpu
from jax.experimental.pallas import tpu_sc as plsc

F32, BF16 = jnp.float32, jnp.bfloat16
D_MODEL = 1024
NORM_EPS = 1e-6
ROPE_THETA = 500000.0
HEAD_DIM = 64
ROT_DIM = 16
BLK = 128
LRU_HEADS, LRU_HEAD_DIM, CONV_WIDTH, LRU_C = 4, 256, 4, 8.0
DILATED_PATTERN = ((128, 1), (512, 4), (2048, 16))
B_HEADS, C_HEADS, C_KV_HEADS, C_WINDOW = 8, 16, 2, 128
XA_HEADS, XA_HEAD_DIM, N_MEM = 4, 128, 256
D_FF = 2816
NEG = -1e30
ADAM_LR, ADAM_B1, ADAM_B2, ADAM_EPS, ADAM_WD, ADAM_STEP = 0.001, 0.9, 0.999, 1e-08, 0.01, 10
N_CHIPS = 4
VMEM_LIMIT_V7X = 56 * 1024 * 1024

NN = (((1,), (0,)), ((), ()))
NT = (((1,), (1,)), ((), ()))
TN = (((0,), (0,)), ((), ()))


def _dot(a, b, dims=NN):
    return lax.dot_general(a, b, dims, preferred_element_type=F32)


def _sigmoid(x):
    return 1.0 / (1.0 + jnp.exp(-x))


def _call(body, *, name, grid, in_specs, out_specs, out_shape, scratch=(), sem=None):
    return pl.pallas_call(
        body, name=name, grid=grid, in_specs=in_specs, out_specs=out_specs, out_shape=out_shape,
        scratch_shapes=list(scratch),
        compiler_params=pltpu.CompilerParams(dimension_semantics=sem, vmem_limit_bytes=VMEM_LIMIT_V7X))


def _rope_tables(L):
    half = ROT_DIM // 2
    inv = ROPE_THETA ** (-jnp.arange(0, ROT_DIM, 2, dtype=F32) / ROT_DIM)
    j = jnp.arange(2 * HEAD_DIM) % HEAD_DIM
    ang = jnp.arange(L, dtype=F32)[:, None] * inv[j % half][None, :]
    cos, sin = jnp.cos(ang), jnp.sin(ang)
    c = jnp.where(j < ROT_DIM, cos, 1.0)
    s1 = jnp.where(j < half, -sin, 0.0)
    s2 = jnp.where((j >= half) & (j < ROT_DIM), sin, 0.0)
    return c, s1, s2


def _rope_fwd(v, c, s1, s2):
    return v * c + pltpu.roll(v, 120, 1) * s1 + pltpu.roll(v, 8, 1) * s2


def _rope_bwd(dv, c, s1, s2):
    return dv * c + pltpu.roll(dv * s1, 8, 1) + pltpu.roll(dv * s2, 120, 1)


def _weight_spec(w, layer):
    once = pl.Buffered(1)
    if layer is None:
        return w.shape, pl.BlockSpec(w.shape, lambda i: (0, 0, 0), pipeline_mode=once)
    S, _, K, Ns = w.shape
    return (S, K, Ns), pl.BlockSpec((S, None, K, Ns), lambda i: (0, layer, 0, 0), pipeline_mode=once)


def _rowmm(a, w3, *, name, tm=512, gain=None, bias=None, res=None, swiglu=False, rope=None, layer=None):
    M, K = a.shape
    (S, _, Ns), w_spec = _weight_spec(w3, layer)
    N = S * Ns
    tm = min(tm, M)
    has_norm, has_bias, has_res, has_rope = gain is not None, bias is not None, res is not None, rope is not None
    row = lambda w: pl.BlockSpec((tm, w), lambda i: (i, 0))
    whole = lambda shape: pl.BlockSpec(shape, lambda i: (0,) * len(shape))
    ins, specs = [a], [row(K)]
    if has_norm:
        ins.append(gain.reshape(1, K)); specs.append(whole((1, K)))
    ins.append(w3); specs.append(w_spec)
    if has_bias:
        ins.append(bias.reshape(1, N)); specs.append(whole((1, N)))
    if has_res:
        ins.append(res); specs.append(row(N))
    if has_rope:
        ins += list(rope[2]); specs += [row(128)] * 3
    y_dtype = F32 if has_res else BF16
    out_shape, out_specs = [jax.ShapeDtypeStruct((M, N), y_dtype)], [row(N)]
    if has_norm:
        out_shape.append(jax.ShapeDtypeStruct((M, K), BF16)); out_specs.append(row(K))
    if swiglu:
        out_shape.append(jax.ShapeDtypeStruct((M, N // 2), BF16)); out_specs.append(row(N // 2))
    scratch = [pltpu.VMEM((tm, N), F32)] if has_rope else []

    def body(*refs):
        it = iter(refs)
        a_ref = next(it)
        g_ref = next(it) if has_norm else None
        w_ref = next(it)
        b_ref = next(it) if has_bias else None
        r_ref = next(it) if has_res else None
        tabs = [next(it) for _ in range(3)] if has_rope else None
        y_ref = next(it)
        n_ref = next(it) if has_norm else None
        act_ref = next(it) if swiglu else None
        ys_ref = next(it) if has_rope else None
        if has_norm:
            x = a_ref[...].astype(F32)
            ms = jnp.mean(x * x, axis=-1, keepdims=True)
            xb = (x * lax.rsqrt(ms + NORM_EPS) * g_ref[...]).astype(BF16)
            n_ref[...] = xb
        else:
            xb = a_ref[...].astype(BF16)
        if swiglu:
            for s in range(S // 2):
                g = _dot(xb, w_ref[s])
                u = _dot(xb, w_ref[s + S // 2])
                y_ref[:, s * Ns:(s + 1) * Ns] = g.astype(BF16)
                y_ref[:, N // 2 + s * Ns:N // 2 + (s + 1) * Ns] = u.astype(BF16)
                act_ref[:, s * Ns:(s + 1) * Ns] = (g * _sigmoid(g) * u).astype(BF16)
            return
        for s in range(S):
            sl = slice(s * Ns, (s + 1) * Ns)
            acc = _dot(xb, w_ref[s])
            if has_bias:
                acc = acc + b_ref[:, sl]
            if has_res:
                acc = acc + r_ref[:, sl]
            if has_rope:
                ys_ref[:, sl] = acc
            else:
                y_ref[:, sl] = acc.astype(y_dtype)
        if has_rope:
            c, s1, s2 = (t[...] for t in tabs)
            for cb in range(N // 128):
                cs = slice(cb * 128, (cb + 1) * 128)
                v = ys_ref[:, cs]
                if rope[0] <= cb * 128 < rope[1]:
                    v = _rope_fwd(v, c, s1, s2)
                y_ref[:, cs] = v.astype(BF16)

    return _call(body, name=name, grid=(M // tm,), in_specs=specs, out_specs=out_specs, out_shape=out_shape,
                 scratch=scratch, sem=("parallel",))(*ins)


def _mm_nt(dy, w3, *, name, mode, tm=512, kchunk=None, h=None, gain=None, dh=None, gu=None, layer=None, after=None):
    M, N = dy.shape
    (S, K, Ns), w_spec = _weight_spec(w3, layer)
    kchunk = kchunk or K
    tm = min(tm, M)
    row = lambda w: pl.BlockSpec((tm, w), lambda i: (i, 0))
    whole = lambda shape: pl.BlockSpec(shape, lambda i: (0,) * len(shape))
    ins, specs = [dy, w3], [row(N), w_spec]
    after = list(after or ())
    ins = after + ins
    specs = [pl.BlockSpec((8, a.shape[1]), lambda i: (0, 0)) for a in after] + specs
    has_dh = dh is not None
    if mode == "norm":
        ins += [h, gain.reshape(1, K)]; specs += [row(K), whole((1, K))]
        if has_dh:
            ins.append(dh); specs.append(row(K))
        out_shape = [jax.ShapeDtypeStruct((M, K), F32), jax.ShapeDtypeStruct((1, K), F32)]
        out_specs = [row(K), whole((1, K))]
    elif mode == "swiglu":
        ins.append(gu); specs.append(row(2 * K))
        out_shape, out_specs = [jax.ShapeDtypeStruct((M, 2 * K), BF16)], [row(2 * K)]
    else:
        out_shape, out_specs = [jax.ShapeDtypeStruct((M, K), BF16)], [row(K)]

    def body(*refs):
        it = iter(refs[len(after):])
        dy_ref, w_ref = next(it), next(it)
        if mode == "norm":
            h_ref, g_ref = next(it), next(it)
            dh_ref = next(it) if has_dh else None
            o_ref, dg_ref = next(it), next(it)
        elif mode == "swiglu":
            gu_ref, o_ref = next(it), next(it)
        else:
            o_ref = next(it)
        for kc in range(K // kchunk):
            ks = slice(kc * kchunk, (kc + 1) * kchunk)
            acc = None
            for s in range(S):
                t = _dot(dy_ref[:, s * Ns:(s + 1) * Ns].astype(BF16), w_ref[s, ks, :], NT)
                acc = t if acc is None else acc + t
            if mode == "plain":
                o_ref[:, ks] = acc.astype(BF16)
            elif mode == "swiglu":
                us = slice(K + kc * kchunk, K + (kc + 1) * kchunk)
                g = gu_ref[:, ks].astype(F32)
                u = gu_ref[:, us].astype(F32)
                sg = _sigmoid(g)
                o_ref[:, ks] = (acc * u * (sg * (1.0 + g * (1.0 - sg)))).astype(BF16)
                o_ref[:, us] = (acc * (g * sg)).astype(BF16)
            else:
                x = h_ref[...].astype(F32)
                r = lax.rsqrt(jnp.mean(x * x, axis=-1, keepdims=True) + NORM_EPS)
                xhat = x * r
                dxh = acc * g_ref[...]
                dx = r * (dxh - xhat * jnp.mean(dxh * xhat, axis=-1, keepdims=True))
                o_ref[...] = dx + dh_ref[...] if has_dh else dx

                @pl.when(pl.program_id(0) == 0)
                def _():
                    dg_ref[...] = jnp.zeros_like(dg_ref)

                dg_ref[...] += jnp.sum(acc * xhat, axis=0, keepdims=True)

    sem = ("arbitrary",) if mode == "norm" else ("parallel",)
    return _call(body, name=name, grid=(M // tm,), in_specs=specs, out_specs=out_specs, out_shape=out_shape, sem=sem)(*ins)


def _mm_tn(x, dy, *, S, name, tk=1024, kk=None, bias=False):
    M, K = x.shape
    N = dy.shape[1]
    Ns = N // S
    kk = kk or K
    tk = min(tk, M)
    nl = M // tk
    in_specs = [pl.BlockSpec((tk, kk), lambda s, kc, l: (l, kc)), pl.BlockSpec((tk, Ns), lambda s, kc, l: (l, s))]
    out_shape = [jax.ShapeDtypeStruct((S, K, Ns), BF16)]
    out_specs = [pl.BlockSpec((None, kk, Ns), lambda s, kc, l: (s, kc, 0))]
    if bias:
        out_shape.append(jax.ShapeDtypeStruct((1, N), F32))
        out_specs.append(pl.BlockSpec((1, Ns), lambda s, kc, l: (0, s)))

    def body(x_ref, dy_ref, o_ref, *rest):
        acc_ref = rest[-1]
        kc, l = pl.program_id(1), pl.program_id(2)

        @pl.when(l == 0)
        def _():
            acc_ref[...] = jnp.zeros_like(acc_ref)

        acc_ref[...] += _dot(x_ref[...].astype(BF16), dy_ref[...].astype(BF16), TN)
        if bias:
            b_ref = rest[0]

            @pl.when((kc == 0) & (l == 0))
            def _():
                b_ref[...] = jnp.zeros_like(b_ref)

            @pl.when(kc == 0)
            def _():
                b_ref[...] += jnp.sum(dy_ref[...].astype(F32), axis=0, keepdims=True)

        @pl.when(l == nl - 1)
        def _():
            o_ref[...] = acc_ref[...].astype(BF16)

    return _call(body, name=name, grid=(S, K // kk, nl), in_specs=in_specs, out_specs=out_specs, out_shape=out_shape,
                 scratch=[pltpu.VMEM((kk, Ns), F32)], sem=("arbitrary", "arbitrary", "arbitrary"))(x, dy)


def _band_bias(max_dist, has_prev):
    rows = lax.broadcasted_iota(jnp.int32, (BLK, 2 * BLK), 0)
    cols = lax.broadcasted_iota(jnp.int32, (BLK, 2 * BLK), 1)
    dist = rows - cols + BLK
    ok = (dist >= 0) & (dist <= max_dist) & ((cols >= BLK) | has_prev)
    return jnp.where(ok, 0.0, NEG)


Q_SCALE = HEAD_DIM ** -0.5


def _band_fwd(qa, ka, va, *, d, nq, nkv, qcol, kcol, vcol, max_dist, sinks=None, name):
    Lr = qa.shape[0]
    nb = Lr // BLK
    qw, kw, G = nq * HEAD_DIM, nkv * HEAD_DIM, nq // nkv
    cur = lambda colf, w: pl.BlockSpec((BLK, w), lambda r, i: (i, colf(r)))
    prv = lambda colf, w: pl.BlockSpec((BLK, w), lambda r, i: (jnp.maximum(i - 1, 0), colf(r)))
    out = pl.BlockSpec((BLK, qw), lambda r, i: (i, r))
    ins, specs = [qa, ka, ka, va, va], [cur(qcol, qw), cur(kcol, kw), prv(kcol, kw), cur(vcol, kw), prv(vcol, kw)]
    has_sinks = sinks is not None
    if has_sinks:
        ins.append(sinks); specs.append(pl.BlockSpec(memory_space=pltpu.SMEM))

    def body(*refs):
        q_ref, kc_ref, kp_ref, vc_ref, vp_ref = refs[:5]
        sk_ref = refs[5] if has_sinks else None
        o_ref, lse_ref = refs[-2], refs[-1]
        bias = _band_bias(max_dist, pl.program_id(1) > 0)
        k2 = jnp.concatenate([kp_ref[...], kc_ref[...]], axis=0)
        v2 = jnp.concatenate([vp_ref[...], vc_ref[...]], axis=0)
        for h in range(nq):
            hs = slice(h * HEAD_DIM, (h + 1) * HEAD_DIM)
            ks = slice((h // G) * HEAD_DIM, (h // G + 1) * HEAD_DIM)
            s = _dot(q_ref[:, hs] * jnp.asarray(Q_SCALE, BF16), k2[:, ks], NT) + bias
            m = jnp.max(s, axis=-1, keepdims=True)
            if has_sinks:
                m = jnp.maximum(m, sk_ref[h])
            p = jnp.exp(s - m)
            l = jnp.sum(p, axis=-1, keepdims=True)
            if has_sinks:
                l = l + jnp.exp(sk_ref[h] - m)
            o_ref[:, hs] = (_dot(p.astype(BF16), v2[:, ks]) / l).astype(BF16)
            lse_ref[:, hs] = jnp.broadcast_to(m + jnp.log(l), (BLK, HEAD_DIM))

    return _call(body, name=name, grid=(d, nb), in_specs=specs, out_specs=[out, out],
                 out_shape=[jax.ShapeDtypeStruct((Lr, d * qw), BF16), jax.ShapeDtypeStruct((Lr, d * qw), F32)],
                 sem=("parallel", "parallel"))(*ins)


def _band_bwd(qa, ka, va, doa, oa, lsea, *, d, nq, nkv, qcol, kcol, vcol, docol, max_dist, sinks=None, name):
    Lr = qa.shape[0]
    nb = Lr // BLK
    qw, kw, G = nq * HEAD_DIM, nkv * HEAD_DIM, nq // nkv
    transposed = G > 1
    last = lambda i: jnp.minimum(i, nb - 1)
    cur = lambda colf, w: pl.BlockSpec((BLK, w), lambda r, i: (last(i), colf(r)))
    prv = lambda colf, w: pl.BlockSpec((BLK, w), lambda r, i: (jnp.maximum(last(i) - 1, 0), colf(r)))
    own = lambda r: r
    ins = [qa, ka, ka, va, va, doa, oa, lsea]
    specs = [cur(qcol, qw), cur(kcol, kw), prv(kcol, kw), cur(vcol, kw), prv(vcol, kw), cur(docol, qw), cur(own, qw),
             cur(own, qw)]
    has_sinks = sinks is not None
    if has_sinks:
        ins.append(sinks); specs.append(pl.BlockSpec(memory_space=pltpu.SMEM))
    out_shape = [jax.ShapeDtypeStruct((Lr, d * qw), BF16), jax.ShapeDtypeStruct((Lr, d * kw), BF16),
                 jax.ShapeDtypeStruct((Lr, d * kw), BF16)]
    behind = lambda r, i: (jnp.maximum(i - 1, 0), r)
    out_specs = [pl.BlockSpec((BLK, qw), lambda r, i: (last(i), r)), pl.BlockSpec((BLK, kw), behind),
                 pl.BlockSpec((BLK, kw), behind)]
    if has_sinks:
        out_shape.append(jax.ShapeDtypeStruct((8, 128), F32))
        out_specs.append(pl.BlockSpec((8, 128), lambda r, i: (0, 0)))

    def body(*refs):
        it = iter(refs)
        q_ref, kc_ref, kp_ref, vc_ref, vp_ref, do_ref, o_ref, ls_ref = (next(it) for _ in range(8))
        sk_ref = next(it) if has_sinks else None
        dq_ref, dk_ref, dv_ref = next(it), next(it), next(it)
        dsk_ref = next(it) if has_sinks else None
        dk_car, dv_car = next(it), next(it)
        r_id, i = pl.program_id(0), pl.program_id(1)

        @pl.when(i == 0)
        def _():
            dk_car[...] = jnp.zeros_like(dk_car)
            dv_car[...] = jnp.zeros_like(dv_car)

        if has_sinks:
            @pl.when((r_id == 0) & (i == 0))
            def _():
                dsk_ref[...] = jnp.zeros_like(dsk_ref)

        @pl.when(i == nb)
        def _():
            dk_ref[...] = dk_car[...].astype(BF16)
            dv_ref[...] = dv_car[...].astype(BF16)

        @pl.when(i < nb)
        def _():
            bias = _band_bias(max_dist, i > 0)
            k2 = jnp.concatenate([kp_ref[...], kc_ref[...]], axis=0)
            v2 = jnp.concatenate([vp_ref[...], vc_ref[...]], axis=0)
            if has_sinks:
                lane = lax.broadcasted_iota(jnp.int32, (8, 128), 1)
                dsk = jnp.zeros((8, 128), F32)
            for kv in range(nkv):
                ks = slice(kv * HEAD_DIM, (kv + 1) * HEAD_DIM)
                kh, vh = k2[:, ks], v2[:, ks]
                shape = (HEAD_DIM, 2 * BLK) if transposed else (2 * BLK, HEAD_DIM)
                dk, dv = jnp.zeros(shape, F32), jnp.zeros(shape, F32)
                for g in range(G):
                    h = kv * G + g
                    hs = slice(h * HEAD_DIM, (h + 1) * HEAD_DIM)
                    q = q_ref[:, hs] * jnp.asarray(Q_SCALE, BF16)
                    do = do_ref[:, hs]
                    lse = ls_ref[:, h * HEAD_DIM:h * HEAD_DIM + 1]
                    dl = jnp.sum(do.astype(F32) * o_ref[:, hs].astype(F32), axis=-1, keepdims=True)
                    p = jnp.exp(_dot(q, kh, NT) + bias - lse)
                    ds = (p * (_dot(do, vh, NT) - dl)).astype(BF16)
                    dq_ref[:, hs] = (_dot(ds, kh) * Q_SCALE).astype(BF16)
                    if transposed:
                        dk = dk + _dot(q, ds, TN)
                        dv = dv + _dot(do, p.astype(BF16), TN)
                    else:
                        dk = dk + _dot(ds, q, TN)
                        dv = dv + _dot(p.astype(BF16), do, TN)
                    if has_sinks:
                        val = -jnp.sum(jnp.exp(sk_ref[h] - lse) * dl, axis=0, keepdims=True)
                        dsk = dsk + jnp.where(lane == h, val, 0.0)
                if transposed:
                    dk, dv = dk.T, dv.T
                dk_ref[:, ks] = (dk_car[:, ks] + dk[:BLK]).astype(BF16)
                dv_ref[:, ks] = (dv_car[:, ks] + dv[:BLK]).astype(BF16)
                dk_car[:, ks] = dk[BLK:]
                dv_car[:, ks] = dv[BLK:]
            if has_sinks:
                dsk_ref[...] += dsk

    return _call(body, name=name, grid=(d, nb + 1), in_specs=specs, out_specs=out_specs, out_shape=out_shape,
                 scratch=[pltpu.VMEM((BLK, kw), F32), pltpu.VMEM((BLK, kw), F32)], sem=("arbitrary", "arbitrary"))(*ins)


def _attn_grad_combine(branches, tabs, *, name, tm=256):
    L, qw = branches[0][0].shape
    kw = branches[0][1].shape[1]
    nbr = len(branches)
    row = lambda w: pl.BlockSpec((tm, w), lambda i: (i, 0))
    ins, specs = [], []
    for dq, dk, dv in branches:
        ins += [dq, dk, dv]; specs += [row(qw), row(kw), row(kw)]
    ins += list(tabs); specs += [row(128)] * 3

    def body(*refs):
        c, s1, s2 = (t[...] for t in refs[3 * nbr:3 * nbr + 3])
        o_ref = refs[-1]
        for part, (w, off, rot) in enumerate(((qw, 0, True), (kw, qw, True), (kw, qw + kw, False))):
            for cb in range(w // 128):
                cs = slice(cb * 128, (cb + 1) * 128)
                v = refs[part][:, cs].astype(F32)
                for b in range(1, nbr):
                    v = v + refs[3 * b + part][:, cs].astype(F32)
                if rot:
                    v = _rope_bwd(v, c, s1, s2)
                o_ref[:, off + cb * 128:off + (cb + 1) * 128] = v.astype(BF16)

    return _call(body, name=name, grid=(L // tm,), in_specs=specs, out_specs=row(qw + 2 * kw),
                 out_shape=jax.ShapeDtypeStruct((L, qw + 2 * kw), BF16), sem=("parallel",))(*ins)


def _xattn_fwd(q, kv, *, name, tq=512):
    L, W = q.shape
    scale = XA_HEAD_DIM ** -0.5
    row = pl.BlockSpec((tq, W), lambda i: (i, 0))
    kvs = pl.BlockSpec((N_MEM, 2 * W), lambda i: (0, 0))

    def body(q_ref, kv_ref, o_ref, lse_ref):
        for h in range(XA_HEADS):
            hs = slice(h * XA_HEAD_DIM, (h + 1) * XA_HEAD_DIM)
            vs = slice(W + h * XA_HEAD_DIM, W + (h + 1) * XA_HEAD_DIM)
            s = _dot(q_ref[:, hs], kv_ref[:, hs], NT) * scale
            m = jnp.max(s, axis=-1, keepdims=True)
            p = jnp.exp(s - m)
            l = jnp.sum(p, axis=-1, keepdims=True)
            o_ref[:, hs] = (_dot(p.astype(BF16), kv_ref[:, vs]) / l).astype(BF16)
            lse_ref[:, hs] = jnp.broadcast_to(m + jnp.log(l), (tq, XA_HEAD_DIM))

    return _call(body, name=name, grid=(L // tq,), in_specs=[row, kvs], out_specs=[row, row],
                 out_shape=[jax.ShapeDtypeStruct((L, W), BF16), jax.ShapeDtypeStruct((L, W), F32)], sem=("parallel",))(q, kv)


def _xattn_bwd(q, kv, o, lse, do, *, name, tq=512):
    L, W = q.shape
    scale = XA_HEAD_DIM ** -0.5
    row = pl.BlockSpec((tq, W), lambda i: (i, 0))
    kvs = pl.BlockSpec((N_MEM, 2 * W), lambda i: (0, 0))

    def body(q_ref, kv_ref, o_ref, lse_ref, do_ref, dq_ref, dkv_ref):
        @pl.when(pl.program_id(0) == 0)
        def _():
            dkv_ref[...] = jnp.zeros_like(dkv_ref)

        for h in range(XA_HEADS):
            hs = slice(h * XA_HEAD_DIM, (h + 1) * XA_HEAD_DIM)
            vs = slice(W + h * XA_HEAD_DIM, W + (h + 1) * XA_HEAD_DIM)
            qh, kh, vh, doh = q_ref[:, hs], kv_ref[:, hs], kv_ref[:, vs], do_ref[:, hs]
            p = jnp.exp(_dot(qh, kh, NT) * scale - lse_ref[:, h * XA_HEAD_DIM:h * XA_HEAD_DIM + 1])
            dl = jnp.sum(doh.astype(F32) * o_ref[:, hs].astype(F32), axis=-1, keepdims=True)
            ds = (p * (_dot(doh, vh, NT) - dl) * scale).astype(BF16)
            dq_ref[:, hs] = _dot(ds, kh).astype(BF16)
            dkv_ref[:, hs] += _dot(ds, qh, TN)
            dkv_ref[:, vs] += _dot(p.astype(BF16), doh, TN)

    return _call(body, name=name, grid=(L // tq,), in_specs=[row, kvs, row, row, row], out_specs=[row, kvs],
                 out_shape=[jax.ShapeDtypeStruct((L, W), BF16), jax.ShapeDtypeStruct((N_MEM, 2 * W), F32)],
                 sem=("arbitrary",))(q, kv, o, lse, do)


def _neg_expm1(z):
    series = -(z * (1.0 + z * (0.5 + z * (1.0 / 6.0 + z * (1.0 / 24.0 + z * (1.0 / 120.0))))))
    return jnp.where(z > -0.05, series, 1.0 - jnp.exp(z))


def _softplus(z):
    return jnp.maximum(z, 0.0) + jnp.log(1.0 + jnp.exp(-jnp.abs(z)))


def _gelu_parts(y):
    c = 0.7978845608028654
    t = jnp.tanh(c * (y + 0.044715 * y * y * y))
    gy = 0.5 * y * (1.0 + t)
    dgy = 0.5 * (1.0 + t) + 0.5 * y * (1.0 - t * t) * c * (1.0 + 3.0 * 0.044715 * y * y)
    return gy, dgy


def _lru_gates(xc, wa_ref, ba, wx_ref, bx, sp):
    rs, igs = [], []
    for hd in range(LRU_HEADS):
        sl = slice(hd * LRU_HEAD_DIM, (hd + 1) * LRU_HEAD_DIM)
        xh = xc[:, sl].astype(BF16)
        rs.append(_sigmoid(_dot(xh, wa_ref[hd]) + ba[:, sl]))
        igs.append(_sigmoid(_dot(xh, wx_ref[hd]) + bx[:, sl]))
    r, ig = jnp.concatenate(rs, axis=1), jnp.concatenate(igs, axis=1)
    la = -LRU_C * r * sp
    return r, ig, jnp.exp(la), _neg_expm1(2.0 * la)


def _conv_taps(x_ext, halo):
    n = x_ext.shape[0]
    return [x_ext[halo:] if k == CONV_WIDTH - 1 else pltpu.roll(x_ext, CONV_WIDTH - 1 - k, 0)[halo:]
            for k in range(CONV_WIDTH)]


def _lru_fwd(proj, cw, cb, wa, ba, wx, bx, lam, *, name, tc=512):
    L = proj.shape[0]
    W = LRU_HEADS * LRU_HEAD_DIM
    nb = L // tc
    whole = lambda shape: pl.BlockSpec(shape, lambda i: (0,) * len(shape))
    specs = [pl.BlockSpec((tc, W), lambda i: (i, 0)), pl.BlockSpec((tc, W), lambda i: (i, 1)),
             pl.BlockSpec((16, W), lambda i: (jnp.maximum(i * (tc // 16) - 1, 0), 0)),
             whole((CONV_WIDTH, W)), whole((1, W)), whole((LRU_HEADS, LRU_HEAD_DIM, LRU_HEAD_DIM)), whole((1, W)),
             whole((LRU_HEADS, LRU_HEAD_DIM, LRU_HEAD_DIM)), whole((1, W)), whole((1, W))]
    out_specs = [pl.BlockSpec((tc, W), lambda i: (i, 0))] * 2
    out_shape = [jax.ShapeDtypeStruct((L, W), BF16), jax.ShapeDtypeStruct((L, W), F32)]

    def body(x_ref, y_ref, xh_ref, cw_ref, cb_ref, wa_ref, ba_ref, wx_ref, bx_ref, lam_ref, rec_ref, hs_ref,
             hcar, a_scr, b_scr):
        i = pl.program_id(0)

        @pl.when(i == 0)
        def _():
            hcar[...] = jnp.zeros_like(hcar)

        halo = jnp.where(i > 0, xh_ref[...].astype(F32), 0.0)
        taps = _conv_taps(jnp.concatenate([halo, x_ref[...].astype(F32)], axis=0), 16)
        xc = cb_ref[...] + sum(cw_ref[k:k + 1, :] * taps[k] for k in range(CONV_WIDTH))
        _, ig, a, om = _lru_gates(xc, wa_ref, ba_ref[...], wx_ref, bx_ref[...], _softplus(-lam_ref[...]))
        b = jnp.sqrt(om) * (ig * xc)
        rowmod = lax.broadcasted_iota(jnp.int32, (tc, W), 0) & 7
        for s in (1, 2, 4):
            keep = rowmod >= s
            b = jnp.where(keep, a * pltpu.roll(b, s, 0) + b, b)
            a = jnp.where(keep, a * pltpu.roll(a, s, 0), a)
        a_scr[...] = a
        b_scr[...] = b

        def tile(j, hc):
            rows = pl.ds(pl.multiple_of(j * 8, 8), 8)
            ht = a_scr[rows, :] * hc + b_scr[rows, :]
            hs_ref[rows, :] = ht
            return jnp.broadcast_to(ht[7:8, :], (8, W))

        hcar[...] = lax.fori_loop(0, tc // 8, tile, hcar[...])
        gy, _ = _gelu_parts(y_ref[...].astype(F32))
        rec_ref[...] = (hs_ref[...] * gy).astype(BF16)

    return _call(body, name=name, grid=(nb,), in_specs=specs, out_specs=out_specs, out_shape=out_shape,
                 scratch=[pltpu.VMEM((8, W), F32), pltpu.VMEM((tc, W), F32), pltpu.VMEM((tc, W), F32)],
                 sem=("arbitrary",))(proj, proj, proj, cw, cb, wa, ba, wx, bx, lam)


def _lru_bwd(proj, hs, drec_src, cw, cb, wa, ba, wx, bx, lam, *, name, tc=256):
    L = proj.shape[0]
    W = LRU_HEADS * LRU_HEAD_DIM
    nb = L // tc
    tb = lambda i: nb - 1 - i
    whole = lambda shape: pl.BlockSpec(shape, lambda i: (0,) * len(shape))
    gate_w = (LRU_HEADS, LRU_HEAD_DIM, LRU_HEAD_DIM)
    specs = [pl.BlockSpec((tc, W), lambda i: (tb(i), 0)), pl.BlockSpec((tc, W), lambda i: (tb(i), 1)),
             pl.BlockSpec((16, W), lambda i: (jnp.maximum(tb(i) * (tc // 16) - 1, 0), 0)),
             pl.BlockSpec((tc, W), lambda i: (tb(i), 0)),
             pl.BlockSpec((8, W), lambda i: (jnp.maximum(tb(i) * (tc // 8) - 1, 0), 0)),
             pl.BlockSpec((tc, W), lambda i: (tb(i), 0)),
             whole((CONV_WIDTH, W)), whole((1, W)), whole(gate_w), whole((1, W)), whole(gate_w), whole((1, W)), whole((1, W))]
    out_specs = [pl.BlockSpec((tc, 2 * W), lambda i: (tb(i), 0)), whole((CONV_WIDTH, W)), whole((1, W)), whole(gate_w),
                 whole((1, W)), whole(gate_w), whole((1, W)), whole((1, W))]
    vec = jax.ShapeDtypeStruct((1, W), F32)
    out_shape = [jax.ShapeDtypeStruct((L, 2 * W), BF16), jax.ShapeDtypeStruct((CONV_WIDTH, W), F32), vec,
                 jax.ShapeDtypeStruct(gate_w, F32), vec, jax.ShapeDtypeStruct(gate_w, F32), vec, vec]

    def body(x_ref, y_ref, xh_ref, hs_ref, hh_ref, dr_ref, cw_ref, cb_ref, wa_ref, ba_ref, wx_ref, bx_ref, lam_ref,
             dxy_ref, dcw_ref, dcb_ref, dwa_ref, dba_ref, dwx_ref, dbx_ref, dlam_ref, gcar, dxc_car, a_scr, b_scr, g_scr):
        pid = pl.program_id(0)
        t = tb(pid)
        accs = (dcw_ref, dcb_ref, dwa_ref, dba_ref, dwx_ref, dbx_ref, dlam_ref)

        @pl.when(pid == 0)
        def _():
            gcar[...] = jnp.zeros_like(gcar)
            dxc_car[...] = jnp.zeros_like(dxc_car)
            for r in accs:
                r[...] = jnp.zeros_like(r)

        halo = jnp.where(t > 0, xh_ref[...].astype(F32), 0.0)
        taps = _conv_taps(jnp.concatenate([halo, x_ref[...].astype(F32)], axis=0), 16)
        xc = cb_ref[...] + sum(cw_ref[k:k + 1, :] * taps[k] for k in range(CONV_WIDTH))
        lam = lam_ref[...]
        sp = _softplus(-lam)
        r, ig, a, om = _lru_gates(xc, wa_ref, ba_ref[...], wx_ref, bx_ref[...], sp)
        sq = jnp.sqrt(om)
        hblk = hs_ref[...]
        hprev = pltpu.roll(jnp.concatenate([jnp.where(t > 0, hh_ref[...], 0.0), hblk], axis=0), 1, 0)[8:]
        gy, dgy = _gelu_parts(y_ref[...].astype(F32))
        drec = dr_ref[...].astype(F32)
        dxy_ref[:, W:] = (drec * hblk * dgy).astype(BF16)

        rowidx = lax.broadcasted_iota(jnp.int32, (tc, W), 0)
        rowmod = rowidx & 7
        ca = jnp.where(rowidx == tc - 1, 1.0, pltpu.roll(a, tc - 1, 0))
        cbv = drec * gy
        for s in (1, 2, 4):
            keep = rowmod < 8 - s
            cbv = jnp.where(keep, ca * pltpu.roll(cbv, tc - s, 0) + cbv, cbv)
            ca = jnp.where(keep, ca * pltpu.roll(ca, tc - s, 0), ca)
        a_scr[...] = ca
        b_scr[...] = cbv

        def tile(k, gc):
            j = tc // 8 - 1 - k
            rows = pl.ds(pl.multiple_of(j * 8, 8), 8)
            gt = a_scr[rows, :] * gc + b_scr[rows, :]
            g_scr[rows, :] = gt
            return jnp.broadcast_to(gt[0:1, :], (8, W))

        lax.fori_loop(0, tc // 8, tile, gcar[...])
        G = g_scr[...]
        gcar[...] = jnp.broadcast_to(a[0:1, :] * G[0:1, :], (8, W))

        da = G * hprev
        dsq = G * (ig * xc)
        di = G * (sq * xc)
        dxc = G * (sq * ig)
        dla = da * a - 2.0 * a * a * (dsq * 0.5 * lax.rsqrt(om))
        dlam_ref[...] += jnp.sum(dla * (-LRU_C * r), axis=0, keepdims=True) * (-_sigmoid(-lam))
        dpr = dla * (-LRU_C * sp) * r * (1.0 - r)
        dpi = di * ig * (1.0 - ig)
        dba_ref[...] += jnp.sum(dpr, axis=0, keepdims=True)
        dbx_ref[...] += jnp.sum(dpi, axis=0, keepdims=True)
        back = []
        for hd in range(LRU_HEADS):
            sl = slice(hd * LRU_HEAD_DIM, (hd + 1) * LRU_HEAD_DIM)
            xh, dprh, dpih = xc[:, sl].astype(BF16), dpr[:, sl].astype(BF16), dpi[:, sl].astype(BF16)
            back.append(_dot(dprh, wa_ref[hd], NT) + _dot(dpih, wx_ref[hd], NT))
            dwa_ref[hd] += _dot(xh, dprh, TN)
            dwx_ref[hd] += _dot(xh, dpih, TN)
        dxc = dxc + jnp.concatenate(back, axis=1)
        dcb_ref[...] += jnp.sum(dxc, axis=0, keepdims=True)
        for k in range(CONV_WIDTH):
            dcw_ref[k:k + 1, :] += jnp.sum(dxc * taps[k], axis=0, keepdims=True)
        ext = jnp.concatenate([dxc, dxc_car[...]], axis=0)
        dx = cw_ref[CONV_WIDTH - 1:CONV_WIDTH, :] * dxc
        for k in range(CONV_WIDTH - 1):
            dx = dx + cw_ref[k:k + 1, :] * pltpu.roll(ext, tc + 8 - (CONV_WIDTH - 1 - k), 0)[:tc]
        dxc_car[...] = dxc[0:8, :]
        dxy_ref[:, :W] = dx.astype(BF16)

    scratch = [pltpu.VMEM((8, W), F32), pltpu.VMEM((8, W), F32)] + [pltpu.VMEM((tc, W), F32)] * 3
    return _call(body, name=name, grid=(nb,), in_specs=specs, out_specs=out_specs, out_shape=out_shape, scratch=scratch,
                 sem=("arbitrary",))(proj, proj, proj, hs, hs, drec_src, cw, cb, wa, ba, wx, bx, lam)


def _final_loss(h, gain, target, *, name, tm=256):
    M, K = h.shape
    row = pl.BlockSpec((tm, K), lambda i: (i, 0))
    vec = pl.BlockSpec((1, K), lambda i: (0, 0))
    one = pl.BlockSpec((1, 128), lambda i: (0, 0))

    def body(h_ref, g_ref, t_ref, dh_ref, dg_ref, loss_ref):
        @pl.when(pl.program_id(0) == 0)
        def _():
            dg_ref[...] = jnp.zeros_like(dg_ref)
            loss_ref[...] = jnp.zeros_like(loss_ref)

        x = h_ref[...]
        r = lax.rsqrt(jnp.mean(x * x, axis=-1, keepdims=True) + NORM_EPS)
        xhat = x * r
        err = xhat * g_ref[...] - t_ref[...]
        loss_ref[...] += 0.5 / K * jnp.sum(err * err)
        dy = err * (1.0 / K)
        dg_ref[...] += jnp.sum(dy * xhat, axis=0, keepdims=True)
        dxh = dy * g_ref[...]
        dh_ref[...] = r * (dxh - xhat * jnp.mean(dxh * xhat, axis=-1, keepdims=True))

    return _call(body, name=name, grid=(M // tm,), in_specs=[row, vec, row], out_specs=[row, vec, one],
                 out_shape=[jax.ShapeDtypeStruct((M, K), F32), jax.ShapeDtypeStruct((1, K), F32),
                            jax.ShapeDtypeStruct((1, 128), F32)], sem=("arbitrary",))(h, gain.reshape(1, K), target)


def _dilated_merge(branches, *, name, tm=512):
    L, W = branches[0].shape
    nbr = len(branches) // 2
    row = pl.BlockSpec((tm, W), lambda i: (i, 0))

    def body(*refs):
        o_ref, lse_ref = refs[-2], refs[-1]
        lses = [refs[2 * b + 1][...] for b in range(nbr)]
        m = lses[0]
        for t in lses[1:]:
            m = jnp.maximum(m, t)
        ws = [jnp.exp(t - m) for t in lses]
        den = ws[0]
        for t in ws[1:]:
            den = den + t
        acc = ws[0] * refs[0][...].astype(F32)
        for b in range(1, nbr):
            acc = acc + ws[b] * refs[2 * b][...].astype(F32)
        o_ref[...] = (acc / den).astype(BF16)
        lse_ref[...] = m + jnp.log(den)

    return _call(body, name=name, grid=(L // tm,), in_specs=[row] * (2 * nbr), out_specs=[row, row],
                 out_shape=[jax.ShapeDtypeStruct((L, W), BF16), jax.ShapeDtypeStruct((L, W), F32)], sem=("parallel",))(*branches)


def _dilated_fwd(proj0):
    L = proj0.shape[0]
    qkv = proj0[:, 2 * D_MODEL:]
    W = B_HEADS * HEAD_DIM
    outs = []
    for window, d in DILATED_PATTERN:
        view = qkv.reshape(L // d, d * 3 * W)
        o, lse = _band_fwd(view, view, view, d=d, nq=B_HEADS, nkv=B_HEADS, qcol=lambda r: 3 * r, kcol=lambda r: 3 * r + 1,
                           vcol=lambda r: 3 * r + 2, max_dist=window // d, name=f"dilated_fwd_d{d}")
        outs += [o.reshape(L, W), lse.reshape(L, W)]
    return _dilated_merge(outs, name="dilated_merge")


def _dilated_bwd(proj0, att, lse, datt, tabs):
    L = proj0.shape[0]
    qkv = proj0[:, 2 * D_MODEL:]
    Wh = B_HEADS * HEAD_DIM
    branches = []
    for window, d in DILATED_PATTERN:
        view = qkv.reshape(L // d, d * 3 * Wh)
        v1 = lambda t: t.reshape(L // d, d * Wh)
        outs = _band_bwd(view, view, view, v1(datt), v1(att), v1(lse), d=d, nq=B_HEADS, nkv=B_HEADS,
                         qcol=lambda r: 3 * r, kcol=lambda r: 3 * r + 1, vcol=lambda r: 3 * r + 2, docol=lambda r: r,
                         max_dist=window // d, name=f"dilated_bwd_d{d}")
        branches.append([o.reshape(L, Wh) for o in outs])
    return _attn_grad_combine(branches, tabs, name="dilated_grad_combine")


def _device_step(x, mem, target, w, on_grads=None):
    L = x.shape[0]
    tabs = _rope_tables(L)
    g = {}
    saved = []
    h = x
    for layer in range(2):
        sv = {"h_mix": h}
        if layer == 0:
            proj, n = _rowmm(h, w["ab_w_in"], name="l0_in_proj", gain=w["mix_norm"][0],
                             rope=(2 * D_MODEL, 2 * D_MODEL + 2 * B_HEADS * HEAD_DIM, tabs))
            rec, hs = _lru_fwd(proj, w["lru_conv_w"], w["lru_conv_b"], w["lru_wa"], w["lru_ba"], w["lru_wx"], w["lru_bx"],
                               w["lru_lambda"], name="lru_fwd")
            att, lse = _dilated_fwd(proj)
            mix = jnp.concatenate([rec, att], axis=1)
            (h,) = _rowmm(mix, w["ab_w_out"], name="l0_out_proj", res=h)
            sv.update(hs=hs)
        else:
            proj, n = _rowmm(h, w["c_w_qkv"], name="l1_qkv_proj", gain=w["mix_norm"][1], bias=w["c_b_qkv"],
                             rope=(0, (C_HEADS + C_KV_HEADS) * HEAD_DIM, tabs))
            mix, lse = _band_fwd(proj, proj, proj, d=1, nq=C_HEADS, nkv=C_KV_HEADS, qcol=lambda r: 0, kcol=lambda r: 8,
                                 vcol=lambda r: 9, max_dist=C_WINDOW - 1, sinks=w["c_sinks"], name="swa_fwd")
            (h,) = _rowmm(mix, w["c_w_out"], name="l1_out_proj", res=h, bias=w["c_b_out"])
        sv.update(proj=proj, n_mix=n, mix=mix, lse=lse, h_xa=h)
        xq, nx = _rowmm(h, w["xa_wq"][layer][None], name=f"xa_q_proj{layer}", gain=w["xa_norm"][layer])
        kv, nm = _rowmm(mem, w["xa_wkv"][layer][None], name=f"xa_kv_proj{layer}", gain=w["xa_mem_norm"][layer])
        xo, xlse = _xattn_fwd(xq, kv, name=f"xa_fwd{layer}")
        (h,) = _rowmm(xo, w["xa_wo"][layer], name=f"xa_out_proj{layer}", res=h)
        sv.update(xq=xq, nx=nx, kv=kv, nm=nm, xo=xo, xlse=xlse, h_ffn=h)
        gu, nf, act = _rowmm(h, w["ffn_w_gate_up"], layer=layer, name=f"ffn_in{layer}", gain=w["ffn_norm"][layer], swiglu=True)
        (h,) = _rowmm(act, w["ffn_w_down"][layer][None], name=f"ffn_out{layer}", res=h, tm=512)
        sv.update(gu=gu, nf=nf, act=act)
        saved.append(sv)

    dh, g["final_norm"], loss = _final_loss(h, w["final_norm"], target, name="final_loss")

    stk = {k: [None, None] for k in ("xa_norm", "xa_mem_norm", "ffn_norm", "mix_norm")}
    after = None
    for layer in (1, 0):
        sv = saved[layer]
        (g["ffn_w_down", layer],) = _mm_tn(sv["act"], dh, S=1, name=f"ffn_down_dw{layer}", kk=D_FF // 2)
        (dgu,) = _mm_nt(dh, w["ffn_w_down"][layer][None], name=f"ffn_dact{layer}", mode="swiglu", kchunk=D_FF // 2, gu=sv["gu"],
                        after=after)
        (g["ffn_w_gate_up", layer],) = _mm_tn(sv["nf"], dgu, S=N_CHIPS, name=f"ffn_gu_dw{layer}")
        dh, stk["ffn_norm"][layer] = _mm_nt(dgu, w["ffn_w_gate_up"], layer=layer, name=f"ffn_dx{layer}", mode="norm",
                                            h=sv["h_ffn"], gain=w["ffn_norm"][layer], dh=dh)
        (g["xa_wo", layer],) = _mm_tn(sv["xo"], dh, S=N_CHIPS, name=f"xa_wo_dw{layer}")
        (dxo,) = _mm_nt(dh, w["xa_wo"][layer], name=f"xa_dxo{layer}", mode="plain")
        dxq, dkv = _xattn_bwd(sv["xq"], sv["kv"], sv["xo"], sv["xlse"], dxo, name=f"xa_bwd{layer}")
        (g["xa_wq", layer],) = _mm_tn(sv["nx"], dxq, S=1, name=f"xa_wq_dw{layer}")
        dh, stk["xa_norm"][layer] = _mm_nt(dxq, w["xa_wq"][layer][None], name=f"xa_dx{layer}", mode="norm", h=sv["h_xa"],
                                           gain=w["xa_norm"][layer], dh=dh)
        (g["xa_wkv", layer],) = _mm_tn(sv["nm"], dkv, S=1, name=f"xa_wkv_dw{layer}")
        _, stk["xa_mem_norm"][layer] = _mm_nt(dkv, w["xa_wkv"][layer][None], name=f"xa_dmem{layer}", mode="norm", h=mem,
                                              gain=w["xa_mem_norm"][layer])
        if layer == 1:
            g["c_w_out"], g["c_b_out"] = _mm_tn(sv["mix"], dh, S=1, name="l1_out_dw", bias=True)
            (dmix,) = _mm_nt(dh, w["c_w_out"], name="l1_dmix", mode="plain")
            dq, dk, dv, dsk = _band_bwd(sv["proj"], sv["proj"], sv["proj"], dmix, sv["mix"], sv["lse"], d=1, nq=C_HEADS,
                                        nkv=C_KV_HEADS, qcol=lambda r: 0, kcol=lambda r: 8, vcol=lambda r: 9,
                                        docol=lambda r: 0, max_dist=C_WINDOW - 1, sinks=w["c_sinks"], name="swa_bwd")
            g["c_sinks"] = dsk[0, :C_HEADS]
            dproj = _attn_grad_combine([(dq, dk, dv)], tabs, name="swa_grad_combine")
            g["c_w_qkv"], g["c_b_qkv"] = _mm_tn(sv["n_mix"], dproj, S=1, name="l1_qkv_dw", bias=True)
            dh, stk["mix_norm"][1] = _mm_nt(dproj, w["c_w_qkv"], name="l1_dx", mode="norm", h=sv["h_mix"],
                                            gain=w["mix_norm"][1], dh=dh)
            if on_grads is not None:
                after = on_grads("layer1", g)
        else:
            if on_grads is not None:
                after = on_grads("layer0_ffn_xa", g)
            (g["ab_w_out"],) = _mm_tn(sv["mix"], dh, S=1, name="l0_out_dw", kk=768)
            (dmix,) = _mm_nt(dh, w["ab_w_out"], name="l0_dmix", mode="plain", kchunk=768, after=after)
            (dxy, g["lru_conv_w"], g["lru_conv_b"], g["lru_wa"], g["lru_ba"], g["lru_wx"], g["lru_bx"],
             g["lru_lambda"]) = _lru_bwd(sv["proj"], sv["hs"], dmix, w["lru_conv_w"], w["lru_conv_b"], w["lru_wa"],
                                         w["lru_ba"], w["lru_wx"], w["lru_bx"], w["lru_lambda"], name="lru_bwd")
            dqkv = _dilated_bwd(sv["proj"], sv["mix"][:, D_MODEL:], sv["lse"], dmix[:, D_MODEL:], tabs)
            dproj = jnp.concatenate([dxy, dqkv], axis=1)
            (g["ab_w_in"],) = _mm_tn(sv["n_mix"], dproj, S=N_CHIPS, name="l0_in_dw")
            dh, stk["mix_norm"][0] = _mm_nt(dproj, w["ab_w_in"], name="l0_dx", mode="norm", h=sv["h_mix"],
                                            gain=w["mix_norm"][0], dh=dh)
    for k, v in stk.items():
        g[k] = jnp.concatenate(v, axis=0)
    return loss[0, 0], dh, g


ANY = pl.BlockSpec(memory_space=pl.ANY)
MESH = pl.DeviceIdType.MESH


def _place():
    x, y, c = lax.axis_index("x"), lax.axis_index("y"), lax.axis_index("c")
    return x, y, c, [(1 - x, y), (x, 1 - y), (1 - x, 1 - y)]


def _remote(send_sems, recv_sems):
    def copy(k, src, dst, to):
        return pltpu.make_async_remote_copy(src_ref=src, dst_ref=dst, send_sem=send_sems.at[k], recv_sem=recv_sems.at[k],
                                            device_id=to, device_id_type=MESH)
    return copy


def _halves(ref, n_rows):
    rh = n_rows // 2
    return lambda lead, hh: ref.at[(*lead, pl.ds(hh * rh, rh), slice(None))]


def _gather_weights(packs, spack):
    n = len(packs)

    def body(*refs):
        w_refs, s_ref, wf_refs, sf_ref = refs[:n], refs[n], refs[n + 1:2 * n + 1], refs[2 * n + 1]
        x, y, c, chips = _place()
        me, sib = 2 * x + y, (x, y, 1 - c)
        copy = _remote(*refs[-2:])
        src = [_halves(w_refs[g], packs[g].shape[0]) for g in range(n)]
        dst = [_halves(wf_refs[g], packs[g].shape[0]) for g in range(n)]
        sends = []
        for g in range(n):
            for j, (cx, cy) in enumerate(chips):
                sends.append(copy(3 * g + j, src[g]((), c), dst[g]((me,), c), (cx, cy, c)))
        for j, (cx, cy) in enumerate(chips):
            sends.append(copy(6 * n + j, s_ref, sf_ref.at[me], (cx, cy, c)))
        for cp in sends:
            cp.start()
        for g in range(n):
            for j, (cx, cy) in enumerate(chips):
                got = dst[g]((2 * cx + cy,), c)
                copy(3 * g + j, got, got, sib).wait_recv()
                fwd = copy(3 * n + 3 * g + j, got, got, sib)
                fwd.start()
                sends.append(fwd)
        for g in range(n):
            for j, (cx, cy) in enumerate(chips):
                got = dst[g]((2 * cx + cy,), 1 - c)
                copy(3 * n + 3 * g + j, got, got, sib).wait_recv()
        for j, (cx, cy) in enumerate(chips):
            copy(6 * n + j, s_ref, sf_ref.at[2 * cx + cy], sib).wait_recv()
        for cp in sends:
            cp.wait_send()

    ins = list(packs) + [spack]
    out_shape = [jax.ShapeDtypeStruct((N_CHIPS,) + a.shape, a.dtype) for a in ins]
    n_sems = 6 * n + 3
    outs = pl.pallas_call(body, name="gather_weights", out_shape=out_shape, in_specs=[ANY] * len(ins),
                          out_specs=[ANY] * len(ins),
                          scratch_shapes=[pltpu.SemaphoreType.DMA((n_sems,)), pltpu.SemaphoreType.DMA((n_sems,))])(*ins)
    chip = 2 * lax.axis_index("x") + lax.axis_index("y")
    outs = [lax.dynamic_update_index_in_dim(o, a, chip, 0) for o, a in zip(outs, ins)]
    return outs[:n], outs[n]


SEQUENCER_GATHER_IDS = {"mid": 1, "late": 5}


def _gather_weights_behind(packs, *, tag):
    n = len(packs)

    def body(*refs):
        w_refs, wf_refs = refs[:n], refs[n:2 * n]
        x, y, c, chips = _place()
        me, sib = 2 * x + y, (x, y, 1 - c)
        barrier = pltpu.get_barrier_semaphore()
        for peer in [(cx, cy, c) for cx, cy in chips] + [sib]:
            pl.semaphore_signal(barrier, inc=1, device_id=peer, device_id_type=MESH)
        pl.semaphore_wait(barrier, len(chips) + 1)
        copy = _remote(*refs[-2:])
        src = [_halves(w_refs[g], packs[g].shape[0]) for g in range(n)]
        dst = [_halves(wf_refs[g], packs[g].shape[0]) for g in range(n)]
        sends = []
        for g in range(n):
            for j, (cx, cy) in enumerate(chips):
                sends.append(copy(3 * g + j, src[g]((), c), dst[g]((me,), c), (cx, cy, c)))
        for cp in sends:
            cp.start()
        for g in range(n):
            for j, (cx, cy) in enumerate(chips):
                got = dst[g]((2 * cx + cy,), c)
                copy(3 * g + j, got, got, sib).wait_recv()
                fwd = copy(3 * n + 3 * g + j, got, got, sib)
                fwd.start()
                sends.append(fwd)
        for g in range(n):
            for j, (cx, cy) in enumerate(chips):
                got = dst[g]((2 * cx + cy,), 1 - c)
                copy(3 * n + 3 * g + j, got, got, sib).wait_recv()
        for cp in sends:
            cp.wait_send()

    out_type = [jax.ShapeDtypeStruct((N_CHIPS,) + a.shape, a.dtype) for a in packs]
    outs = pl.kernel(body, out_type=out_type, mesh=plsc.ScalarSubcoreMesh(axis_name="sequencer", num_cores=1),
                     name="gather_weights_behind_" + tag,
                     scratch_types=[pltpu.SemaphoreType.DMA((6 * n,)), pltpu.SemaphoreType.DMA((6 * n,))],
                     compiler_params=pltpu.CompilerParams(collective_id=SEQUENCER_GATHER_IDS[tag]))(*packs)
    chip = 2 * lax.axis_index("x") + lax.axis_index("y")
    return [lax.dynamic_update_index_in_dim(o, a, chip, 0) for o, a in zip(outs, packs)]


def _rs_pair_exchange(gpacks, *, name):
    n = len(gpacks)

    def body(*refs):
        g_refs, ra_refs = refs[:n], refs[n:2 * n]
        x, y, c, _ = _place()
        copy = _remote(*refs[-2:])
        cps = []
        for g in range(n):
            half = _halves(g_refs[g], gpacks[g].shape[1])
            cps += [copy(N_CHIPS * g + j, half((j,), 1 - c), ra_refs[g].at[j], (x, y, 1 - c)) for j in range(N_CHIPS)]
        for cp in cps:
            cp.start()
        for cp in cps:
            cp.wait()

    out_shape = [jax.ShapeDtypeStruct((N_CHIPS, a.shape[1] // 2, a.shape[2]), a.dtype) for a in gpacks]
    n_sems = N_CHIPS * n
    return pl.pallas_call(body, name=name, out_shape=out_shape, in_specs=[ANY] * n, out_specs=[ANY] * n,
                          scratch_shapes=[pltpu.SemaphoreType.DMA((n_sems,)), pltpu.SemaphoreType.DMA((n_sems,))])(*gpacks)


def _row_tile(rows, cap=512):
    return max(t for t in range(16, min(rows, cap) + 1, 16) if rows % t == 0)


def _rs_pair_add(place, gpack, ra, *, name):
    _, R, C = gpack.shape
    Rh = R // 2
    tr = _row_tile(Rh)
    nrb = Rh // tr

    def body(p_ref, g_ref, ra_ref, pair_ref, own_ref):
        s = g_ref[...].astype(F32) + ra_ref[...].astype(F32)
        pair_ref[...] = s.astype(BF16)

        @pl.when(pl.program_id(1) == p_ref[1])
        def _():
            own_ref[...] = s

    grid_spec = pltpu.PrefetchScalarGridSpec(
        num_scalar_prefetch=1, grid=(nrb, N_CHIPS),
        in_specs=[pl.BlockSpec((None, tr, C), lambda i, j, p: (j, p[0] * nrb + i, 0)),
                  pl.BlockSpec((None, tr, C), lambda i, j, p: (j, i, 0))],
        out_specs=[pl.BlockSpec((None, tr, C), lambda i, j, p: (j, i, 0)), pl.BlockSpec((tr, C), lambda i, j, p: (i, 0))])
    return pl.pallas_call(
        body, name=name, grid_spec=grid_spec,
        out_shape=[jax.ShapeDtypeStruct((N_CHIPS, Rh, C), BF16), jax.ShapeDtypeStruct((Rh, C), F32)],
        compiler_params=pltpu.CompilerParams(dimension_semantics=("arbitrary", "arbitrary"),
                                             vmem_limit_bytes=VMEM_LIMIT_V7X))(place, gpack, ra)


SEQUENCER_EXCHANGE_IDS = {"l1": 2, "l0a": 3, "l0b": 4}


def _rs_chip_exchange_behind(pairs, *, tag, small=None):
    n = len(pairs)
    has_small = small is not None

    def body(*refs):
        p_refs = refs[:n]
        s_ref = refs[n] if has_small else None
        rb_refs = refs[n + has_small:2 * n + has_small]
        rs_ref = refs[2 * n + 1] if has_small else None
        x, y, c, chips = _place()
        peers = [(1 - x if k & 4 else x, 1 - y if k & 2 else y, 1 - c if k & 1 else c) for k in range(1, 8)]
        shake = peers if has_small else [(cx, cy, c) for cx, cy in chips]
        barrier = pltpu.get_barrier_semaphore()
        for peer in shake:
            pl.semaphore_signal(barrier, inc=1, device_id=peer, device_id_type=MESH)
        pl.semaphore_wait(barrier, len(shake))
        copy = _remote(*refs[-2:])
        cps = []
        for g in range(n):
            cps += [copy(3 * g + j, p_refs[g].at[2 * cx + cy], rb_refs[g].at[j], (cx, cy, c)) for j, (cx, cy) in enumerate(chips)]
        if has_small:
            dev = 4 * x + 2 * y + c
            cps += [copy(3 * n + k, s_ref, rs_ref.at[dev], peer) for k, peer in enumerate(peers)]
        for cp in cps:
            cp.start()
        for g in range(n):
            for j in range(3):
                copy(3 * g + j, p_refs[g].at[0], rb_refs[g].at[j], (x, y, c)).wait_recv()
        if has_small:
            for k, (px, py, pc) in enumerate(peers):
                copy(3 * n + k, s_ref, rs_ref.at[4 * px + 2 * py + pc], (x, y, c)).wait_recv()
        for cp in cps:
            cp.wait_send()

    ins = list(pairs) + ([small] if has_small else [])
    out_type = [jax.ShapeDtypeStruct((3,) + p.shape[1:], p.dtype) for p in pairs]
    if has_small:
        out_type.append(jax.ShapeDtypeStruct((8,) + small.shape, small.dtype))
    n_sems = 3 * n + 7 * has_small
    outs = pl.kernel(body, out_type=out_type, mesh=plsc.ScalarSubcoreMesh(axis_name="sequencer", num_cores=1),
                     name="rs_chip_exchange_behind_" + tag,
                     scratch_types=[pltpu.SemaphoreType.DMA((n_sems,)), pltpu.SemaphoreType.DMA((n_sems,))],
                     compiler_params=pltpu.CompilerParams(collective_id=SEQUENCER_EXCHANGE_IDS[tag]))(*ins)
    if has_small:
        dev = 4 * lax.axis_index("x") + 2 * lax.axis_index("y") + lax.axis_index("c")
        outs = list(outs[:n]) + [lax.dynamic_update_index_in_dim(outs[n], small, dev, 0)]
    return outs


def _rs_final_add(place, own, rb, *, name):
    Rh, C = own.shape
    tr = _row_tile(Rh)
    nrb = Rh // tr

    def body(p_ref, o_ref, rb_ref, f_ref):
        f_ref[...] = ((o_ref[...] + rb_ref[0].astype(F32)) + rb_ref[1].astype(F32)) + rb_ref[2].astype(F32)

    grid_spec = pltpu.PrefetchScalarGridSpec(
        num_scalar_prefetch=1, grid=(nrb,),
        in_specs=[pl.BlockSpec((tr, C), lambda i, p: (i, 0)), pl.BlockSpec((3, tr, C), lambda i, p: (0, i, 0))],
        out_specs=pl.BlockSpec((tr, C), lambda i, p: (p[0] * nrb + i, 0)))
    return pl.pallas_call(
        body, name=name, grid_spec=grid_spec, out_shape=jax.ShapeDtypeStruct((2 * Rh, C), F32),
        compiler_params=pltpu.CompilerParams(dimension_semantics=("arbitrary",), vmem_limit_bytes=VMEM_LIMIT_V7X))(place, own, rb)


def _sum_slots(rs):
    n, rows, C = rs.shape

    def body(r_ref, o_ref):
        acc = r_ref[0]
        for k in range(1, n):
            acc = acc + r_ref[k]
        o_ref[...] = acc

    return _call(body, name="small_grad_sum", grid=(1,), in_specs=[pl.BlockSpec((n, rows, C), lambda i: (0, 0, 0))],
                 out_specs=pl.BlockSpec((rows, C), lambda i: (0, 0)), out_shape=jax.ShapeDtypeStruct((rows, C), F32),
                 sem=("arbitrary",))(rs)


def _rs_sibling_share(gbufs, *, name):
    n = len(gbufs)

    def body(*refs):
        g_refs = refs[n:2 * n]
        x, y, c, _ = _place()
        copy = _remote(*refs[-2:])
        halves = [_halves(g_refs[g], gbufs[g].shape[0]) for g in range(n)]
        outs = [copy(g, halves[g]((), c), halves[g]((), c), (x, y, 1 - c)) for g in range(n)]
        for cp in outs:
            cp.start()
        for g in range(n):
            copy(g, halves[g]((), 1 - c), halves[g]((), 1 - c), (x, y, c)).wait_recv()
        for cp in outs:
            cp.wait_send()

    return pl.pallas_call(body, name=name, out_shape=[jax.ShapeDtypeStruct(a.shape, a.dtype) for a in gbufs],
                          in_specs=[ANY] * n, out_specs=[ANY] * n, input_output_aliases={g: g for g in range(n)},
                          scratch_shapes=[pltpu.SemaphoreType.DMA((n,)), pltpu.SemaphoreType.DMA((n,))])(*gbufs)


def _adamw(w, g, m, v, *, name, g_row=0):
    rows, cols = w.shape
    tr = rows
    for cand in range(min(rows, 512), 7, -8):
        if rows % cand == 0 and g_row % cand == 0:
            tr = cand
            break
    spec = pl.BlockSpec((tr, cols), lambda i: (i, 0))
    g_spec = pl.BlockSpec((tr, cols), lambda i: (g_row // tr + i, 0))

    def body(w_ref, g_ref, m_ref, v_ref, d_ref, nm_ref, nv_ref):
        gg = g_ref[...]
        nm = ADAM_B1 * m_ref[...] + (1.0 - ADAM_B1) * gg
        nv = ADAM_B2 * v_ref[...] + (1.0 - ADAM_B2) * (gg * gg)
        m_hat = nm / (1.0 - ADAM_B1 ** ADAM_STEP)
        v_hat = nv / (1.0 - ADAM_B2 ** ADAM_STEP)
        d_ref[...] = -ADAM_LR * (m_hat / (jnp.sqrt(v_hat) + ADAM_EPS) + ADAM_WD * w_ref[...])
        nm_ref[...] = nm
        nv_ref[...] = nv

    return _call(body, name=name, grid=(rows // tr,), in_specs=[spec, g_spec, spec, spec], out_specs=[spec] * 3,
                 out_shape=[jax.ShapeDtypeStruct((rows, cols), F32)] * 3, sem=("parallel",))(w, g, m, v)


WEIGHT_NAMES = ("mix_norm", "ab_w_in", "lru_conv_w", "lru_conv_b", "lru_wa", "lru_ba", "lru_wx", "lru_bx", "lru_lambda",
                "ab_w_out", "c_w_qkv", "c_b_qkv", "c_sinks", "c_w_out", "c_b_out", "xa_norm", "xa_mem_norm", "xa_wq",
                "xa_wkv", "xa_wo", "ffn_norm", "ffn_w_gate_up", "ffn_w_down", "final_norm")
EARLY_GROUPS = (("ab_w_in",),)
MID_GROUPS = (("ab_w_out",), ("lru_wa", "lru_wx"))
LATE_GROUPS = (("c_w_out", "xa_wkv", "ffn_w_down"), ("ffn_w_gate_up",), ("xa_wo",), ("xa_wq",), ("c_w_qkv",))
GROUPS = EARLY_GROUPS + MID_GROUPS + LATE_GROUPS
REPLICATED = ("mix_norm", "lru_conv_b", "lru_lambda", "c_sinks", "xa_norm", "xa_mem_norm", "ffn_norm", "final_norm")
SMALL_SHARDED = ("lru_conv_w", "lru_ba", "lru_bx", "c_b_qkv", "c_b_out")
LANES = 1024


def _rows(v):
    flat = v.reshape(-1)
    return jnp.pad(flat, (0, -flat.shape[0] % LANES)).reshape(-1, LANES)


def _pack_small(parts, total, *, name):
    def body(*refs):
        o_ref = refs[-1]
        o_ref[...] = jnp.zeros_like(o_ref)
        row = 0
        for p_ref in refs[:-1]:
            o_ref[row:row + p_ref.shape[0], :] = p_ref[...]
            row += p_ref.shape[0]

    return _call(body, name=name, grid=(1,), in_specs=[pl.BlockSpec(p.shape, lambda i: (0, 0)) for p in parts],
                 out_specs=pl.BlockSpec((total, LANES), lambda i: (0, 0)),
                 out_shape=jax.ShapeDtypeStruct((total, LANES), F32), sem=("arbitrary",))(*parts)


def _from_shards(name, t):
    minor = t.shape[-1]
    if name == "ab_w_in":
        return t
    if name in ("ab_w_out", "c_w_out"):
        return t.reshape(1, -1, minor)
    if name == "ffn_w_gate_up":
        return t.reshape(N_CHIPS, 2, -1, minor)
    if name in ("xa_wq", "xa_wkv", "ffn_w_down"):
        return t.reshape(N_CHIPS, 2, -1, minor).transpose(1, 0, 2, 3).reshape(2, -1, minor)
    if name in ("lru_wa", "lru_wx"):
        return t.reshape(N_CHIPS, LRU_HEADS, -1, minor).transpose(1, 0, 2, 3).reshape(LRU_HEADS, LRU_HEAD_DIM, minor)
    if name == "xa_wo":
        return t.reshape(N_CHIPS, 2, -1, minor).transpose(1, 0, 2, 3)
    assert name == "c_w_qkv"
    return t.transpose(1, 0, 2).reshape(1, D_MODEL, -1)


def _piece_shards(name, g):
    minor = g.shape[-1]
    if name in ("ab_w_in", "ffn_w_gate_up", "xa_wo"):
        return g
    if name in ("ab_w_out", "c_w_out", "xa_wq", "xa_wkv", "ffn_w_down"):
        return g.reshape(N_CHIPS, -1, minor)
    if name in ("lru_wa", "lru_wx"):
        return g.reshape(LRU_HEADS, N_CHIPS, -1, minor).transpose(1, 0, 2, 3).reshape(N_CHIPS, -1, minor)
    assert name == "c_w_qkv"
    return g.reshape(D_MODEL, N_CHIPS, -1).transpose(1, 0, 2)


RS_SETS = {
    "l1": ((("c_w_out", None), ("xa_wkv", 1), ("ffn_w_down", 1)), (("ffn_w_gate_up", 1),), (("xa_wo", 1),),
           (("xa_wq", 1),), (("c_w_qkv", None),)),
    "l0a": ((("xa_wkv", 0), ("ffn_w_down", 0)), (("ffn_w_gate_up", 0),), (("xa_wo", 0),), (("xa_wq", 0),)),
    "l0b": ((("ab_w_out", None),), (("ab_w_in", None),), (("lru_wa", None), ("lru_wx", None))),
}
RS_STAGE = {"layer1": "l1", "layer0_ffn_xa": "l0a"}


def kernel(x, mem, mix_norm, ab_w_in, lru_conv_w, lru_conv_b, lru_wa, lru_ba, lru_wx, lru_bx, lru_lambda, ab_w_out, c_w_qkv, c_b_qkv, c_sinks, c_w_out, c_b_out, xa_norm, xa_mem_norm, xa_wq, xa_wkv, xa_wo, ffn_norm, ffn_w_gate_up, ffn_w_down, final_norm, loss_target, m_mix_norm, m_ab_w_in, m_lru_conv_w, m_lru_conv_b, m_lru_wa, m_lru_ba, m_lru_wx, m_lru_bx, m_lru_lambda, m_ab_w_out, m_c_w_qkv, m_c_b_qkv, m_c_sinks, m_c_w_out, m_c_b_out, m_xa_norm, m_xa_mem_norm, m_xa_wq, m_xa_wkv, m_xa_wo, m_ffn_norm, m_ffn_w_gate_up, m_ffn_w_down, m_final_norm, v_mix_norm, v_ab_w_in, v_lru_conv_w, v_lru_conv_b, v_lru_wa, v_lru_ba, v_lru_wx, v_lru_bx, v_lru_lambda, v_ab_w_out, v_c_w_qkv, v_c_b_qkv, v_c_sinks, v_c_w_out, v_c_b_out, v_xa_norm, v_xa_mem_norm, v_xa_wq, v_xa_wkv, v_xa_wo, v_ffn_norm, v_ffn_w_gate_up, v_ffn_w_down, v_final_norm):
    given = dict(locals())
    wl = {n: given[n] for n in WEIGHT_NAMES}
    ml = {n: given["m_" + n] for n in WEIGHT_NAMES}
    vl = {n: given["v_" + n] for n in WEIGHT_NAMES}
    xi, yi, ci = lax.axis_index("x"), lax.axis_index("y"), lax.axis_index("c")
    chip = 2 * xi + yi

    def join(parts, axis):
        return parts[0] if len(parts) == 1 else jnp.concatenate(parts, axis=axis)

    local_rows = {n: wl[n].size // wl[n].shape[-1] for grp in GROUPS for n in grp}
    packs = [join([wl[n].astype(BF16).reshape(local_rows[n], wl[n].shape[-1]) for n in grp], 0) for grp in GROUPS]
    spack = _pack_small([_rows(wl[n]) for n in SMALL_SHARDED], 8, name="pack_small_weights")
    n_early, n_mid = len(EARLY_GROUPS), len(EARLY_GROUPS) + len(MID_GROUPS)
    early, sfull = _gather_weights(packs[:n_early], spack)
    early, sfull, mid_packs = lax.optimization_barrier((early, sfull, packs[n_early:n_mid]))
    mid = _gather_weights_behind(mid_packs, tag="mid")
    mid, late_packs = lax.optimization_barrier((mid, packs[n_mid:]))
    gathered = early + mid + _gather_weights_behind(late_packs, tag="late")
    w = {n: wl[n] for n in REPLICATED}
    w["c_sinks"] = wl["c_sinks"][0]
    for grp, full in zip(GROUPS, gathered):
        off = 0
        for n in grp:
            w[n] = _from_shards(n, full if len(grp) == 1 else full[:, off:off + local_rows[n]])
            off += local_rows[n]
    for r, n in enumerate(SMALL_SHARDED):
        loc = wl[n].shape[1:]
        t = sfull[:, r, :wl[n].size].reshape((N_CHIPS,) + loc)
        if n == "lru_conv_w":
            w[n] = t.transpose(1, 0, 2).reshape(CONV_WIDTH, -1)
        elif n in ("lru_ba", "lru_bx"):
            w[n] = t.transpose(1, 0, 2).reshape(1, -1)
        else:
            w[n] = t.reshape(1, -1)

    place = jnp.stack([ci, chip]).astype(jnp.int32)

    def pair_stage(spec, g, tag):
        piece = lambda n, l: (g[n] if l is None else g[n, l]).astype(BF16)
        gpacks = [join([_piece_shards(n, piece(n, l)) for n, l in grp], 1) for grp in spec]
        ras = _rs_pair_exchange(gpacks, name=f"rs_pair_exchange_{tag}")
        sums = [_rs_pair_add(place, gp, ra, name=f"rs_pair_add_{tag}_{i}") for i, (gp, ra) in enumerate(zip(gpacks, ras))]
        return [pair for pair, _ in sums], [own for _, own in sums]

    reduced, in_flight = [], []

    def take_up():
        done = [_rs_final_add(place, o, r, name=f"rs_final_add_{len(reduced) + i}") for i, (o, r) in enumerate(in_flight)]
        reduced.extend(done)
        in_flight.clear()
        return done

    def reduce_behind(stage, g):
        done = take_up()
        tag = RS_STAGE[stage]
        pairs, own = pair_stage(RS_SETS[tag], g, tag)
        in_flight.extend(zip(own, _rs_chip_exchange_behind(pairs, tag=tag)))
        return own + done

    loss_part, grad_x, g = _device_step(x[0], mem[0], loss_target[0], w, on_grads=reduce_behind)

    small_parts = [_rows(g[n]) for n in REPLICATED] + [_rows(jnp.broadcast_to(loss_part, (LANES,)))]
    small_parts += [_rows(g[n]) for n in SMALL_SHARDED]
    small = _pack_small(small_parts, 24, name="pack_small_grads")
    take_up()
    pairs, own = pair_stage(RS_SETS["l0b"], g, "l0b")
    *rb, rs = _rs_chip_exchange_behind(pairs, tag="l0b", small=small)
    gsums = list(_rs_sibling_share(list(reduced), name="rs_sibling_share_behind"))
    in_flight.extend(zip(own, rb))
    gsums += list(_rs_sibling_share(take_up(), name="rs_sibling_share_last"))
    ssum = _sum_slots(rs)

    where = {}
    for grp, gsum in zip(RS_SETS["l1"] + RS_SETS["l0a"] + RS_SETS["l0b"], gsums):
        off = 0
        for n, l in grp:
            rows = local_rows[n] if l is None else local_rows[n] // 2
            where[n, l] = (gsum, off, rows, len(grp) == 1)
            off += rows
    take = lambda gsum, off, rows, whole: gsum if whole else gsum[off:off + rows]
    grads, grad_rows = {}, {}
    for grp in LATE_GROUPS + EARLY_GROUPS + MID_GROUPS:
        for n in grp:
            if (n, None) in where:
                grads[n] = take(*where[n, None]).reshape(wl[n].shape)
                grad_rows[n] = where[n, None][:2]
            else:
                grads[n] = jnp.stack([take(*where[n, l]).reshape(wl[n].shape[1:]) for l in range(2)])
                grad_rows[n] = (grads[n].reshape(local_rows[n], wl[n].shape[-1]), 0)
    row = 0
    for n in REPLICATED:
        k = _rows(g[n]).shape[0]
        grads[n] = ssum[row:row + k].reshape(-1)[:wl[n].size].reshape(wl[n].shape)
        row += k
    loss = ssum[row, 0]
    row += 1
    for n in SMALL_SHARDED:
        k = _rows(g[n]).shape[0]
        full = ssum[row:row + k].reshape(-1)[:g[n].size]
        row += k
        loc = wl[n].shape
        if n == "lru_conv_w":
            sh = full.reshape(CONV_WIDTH, N_CHIPS, -1)
        elif n in ("lru_ba", "lru_bx"):
            sh = full.reshape(LRU_HEADS, N_CHIPS, -1)
        else:
            sh = full.reshape(1, N_CHIPS, -1)
        grads[n] = lax.dynamic_index_in_dim(sh, chip, axis=1, keepdims=False).reshape(loc)

    delta, new_m, new_v = {}, {}, {}
    for n, (gsum, off) in grad_rows.items():
        shape2 = (local_rows[n], wl[n].shape[-1])
        d, nm, nv = _adamw(wl[n].reshape(shape2), gsum, ml[n].reshape(shape2), vl[n].reshape(shape2), g_row=off,
                           name="adamw_" + n)
        delta[n], new_m[n], new_v[n] = (t.reshape(wl[n].shape) for t in (d, nm, nv))
    smalls = REPLICATED + SMALL_SHARDED
    packs = [_pack_small([_rows(src[n]) for n in smalls], 24, name="pack_adamw_" + tag)
             for tag, src in (("w", wl), ("g", grads), ("m", ml), ("v", vl))]
    outs = _adamw(*packs, name="adamw_small")
    row = 0
    for n in smalls:
        k = _rows(wl[n]).shape[0]
        for dst, o in zip((delta, new_m, new_v), outs):
            dst[n] = o[row:row + k].reshape(-1)[:wl[n].size].reshape(wl[n].shape)
        row += k

    return (loss, grad_x[None], *[grads[n] for n in WEIGHT_NAMES], *[delta[n] for n in WEIGHT_NAMES],
            *[new_m[n] for n in WEIGHT_NAMES], *[new_v[n] for n in WEIGHT_NAMES])
```

```python
import jax
import jax.numpy as jnp
from jax import lax
from jax.experimental import pallas as pl
from jax.experimental.pallas import tpu as pltpu
from jax.experimental.pallas import tpu_sc as plsc

F32, BF16 = jnp.float32, jnp.bfloat16
D_MODEL = 1024
NORM_EPS = 1e-6
ROPE_THETA = 500000.0
HEAD_DIM = 64
ROT_DIM = 16
BLK = 128
LRU_HEADS, LRU_HEAD_DIM, CONV_WIDTH, LRU_C = 4, 256, 4, 8.0
DILATED_PATTERN = ((128, 1), (512, 4), (2048, 16))
B_HEADS, C_HEADS, C_KV_HEADS, C_WINDOW = 8, 16, 2, 128
XA_HEADS, XA_HEAD_DIM, N_MEM = 4, 128, 256
D_FF = 2816
NEG = -1e30
ADAM_LR, ADAM_B1, ADAM_B2, ADAM_EPS, ADAM_WD, ADAM_STEP = 0.001, 0.9, 0.999, 1e-08, 0.01, 10
N_CHIPS = 4
VMEM_LIMIT_V7X = 56 * 1024 * 1024

NN = (((1,), (0,)), ((), ()))
NT = (((1,), (1,)), ((), ()))
TN = (((0,), (0,)), ((), ()))


def _dot(a, b, dims=NN):
    return lax.dot_general(a, b, dims, preferred_element_type=F32)


def _sigmoid(x):
    return 0.5 * jnp.tanh(0.5 * x) + 0.5


def _call(body, *, name, grid, in_specs, out_specs, out_shape, scratch=(), sem=None):
    return pl.pallas_call(
        body, name=name, grid=grid, in_specs=in_specs, out_specs=out_specs, out_shape=out_shape,
        scratch_shapes=list(scratch),
        compiler_params=pltpu.CompilerParams(dimension_semantics=sem, vmem_limit_bytes=VMEM_LIMIT_V7X))


def _rope_tables(L):
    half = ROT_DIM // 2
    inv = ROPE_THETA ** (-jnp.arange(0, ROT_DIM, 2, dtype=F32) / ROT_DIM)
    j = jnp.arange(2 * HEAD_DIM) % HEAD_DIM
    ang = jnp.arange(L, dtype=F32)[:, None] * inv[j % half][None, :]
    cos, sin = jnp.cos(ang), jnp.sin(ang)
    c = jnp.where(j < ROT_DIM, cos, 1.0)
    s1 = jnp.where(j < half, -sin, 0.0)
    s2 = jnp.where((j >= half) & (j < ROT_DIM), sin, 0.0)
    return c, s1, s2


def _rope_fwd(v, c, s1, s2):
    return v * c + pltpu.roll(v, 120, 1) * s1 + pltpu.roll(v, 8, 1) * s2


def _rope_bwd(dv, c, s1, s2):
    return dv * c + pltpu.roll(dv * s1, 8, 1) + pltpu.roll(dv * s2, 120, 1)


def _weight_spec(w, layer):
    once = pl.Buffered(1)
    if layer is None:
        return w.shape, pl.BlockSpec(w.shape, lambda i: (0, 0, 0), pipeline_mode=once)
    S, _, K, Ns = w.shape
    return (S, K, Ns), pl.BlockSpec((S, None, K, Ns), lambda i: (0, layer, 0, 0), pipeline_mode=once)


def _rowmm(a, w3, *, name, tm=512, gain=None, bias=None, res=None, swiglu=False, rope=None, layer=None):
    M, K = a.shape
    (S, _, Ns), w_spec = _weight_spec(w3, layer)
    N = S * Ns
    tm = min(tm, M)
    has_norm, has_bias, has_res, has_rope = gain is not None, bias is not None, res is not None, rope is not None
    row = lambda w: pl.BlockSpec((tm, w), lambda i: (i, 0))
    whole = lambda shape: pl.BlockSpec(shape, lambda i: (0,) * len(shape))
    ins, specs = [a], [row(K)]
    if has_norm:
        ins.append(gain.reshape(1, K)); specs.append(whole((1, K)))
    ins.append(w3); specs.append(w_spec)
    if has_bias:
        ins.append(bias.reshape(1, N)); specs.append(whole((1, N)))
    if has_res:
        ins.append(res); specs.append(row(N))
    if has_rope:
        ins += list(rope[2]); specs += [row(128)] * 3
    y_dtype = F32 if has_res else BF16
    out_shape, out_specs = [jax.ShapeDtypeStruct((M, N), y_dtype)], [row(N)]
    if has_norm:
        out_shape.append(jax.ShapeDtypeStruct((M, K), BF16)); out_specs.append(row(K))
    if swiglu:
        out_shape.append(jax.ShapeDtypeStruct((M, N // 2), BF16)); out_specs.append(row(N // 2))
    scratch = [pltpu.VMEM((tm, N), F32)] if has_rope else []

    def body(*refs):
        it = iter(refs)
        a_ref = next(it)
        g_ref = next(it) if has_norm else None
        w_ref = next(it)
        b_ref = next(it) if has_bias else None
        r_ref = next(it) if has_res else None
        tabs = [next(it) for _ in range(3)] if has_rope else None
        y_ref = next(it)
        n_ref = next(it) if has_norm else None
        act_ref = next(it) if swiglu else None
        ys_ref = next(it) if has_rope else None
        if has_norm:
            x = a_ref[...].astype(F32)
            ms = jnp.mean(x * x, axis=-1, keepdims=True)
            xb = (x * lax.rsqrt(ms + NORM_EPS) * g_ref[...]).astype(BF16)
            n_ref[...] = xb
        else:
            xb = a_ref[...].astype(BF16)
        if swiglu:
            for s in range(S // 2):
                g = _dot(xb, w_ref[s])
                u = _dot(xb, w_ref[s + S // 2])
                y_ref[:, s * Ns:(s + 1) * Ns] = g.astype(BF16)
                y_ref[:, N // 2 + s * Ns:N // 2 + (s + 1) * Ns] = u.astype(BF16)
                act_ref[:, s * Ns:(s + 1) * Ns] = (g * _sigmoid(g) * u).astype(BF16)
            return
        for s in range(S):
            sl = slice(s * Ns, (s + 1) * Ns)
            acc = _dot(xb, w_ref[s])
            if has_bias:
                acc = acc + b_ref[:, sl]
            if has_res:
                acc = acc + r_ref[:, sl]
            if has_rope:
                ys_ref[:, sl] = acc
            else:
                y_ref[:, sl] = acc.astype(y_dtype)
        if has_rope:
            c, s1, s2 = (t[...] for t in tabs)
            for cb in range(N // 128):
                cs = slice(cb * 128, (cb + 1) * 128)
                v = ys_ref[:, cs]
                if rope[0] <= cb * 128 < rope[1]:
                    v = _rope_fwd(v, c, s1, s2)
                y_ref[:, cs] = v.astype(BF16)

    return _call(body, name=name, grid=(M // tm,), in_specs=specs, out_specs=out_specs, out_shape=out_shape,
                 scratch=scratch, sem=("parallel",))(*ins)


def _mm_nt(dy, w3, *, name, mode, tm=512, kchunk=None, h=None, gain=None, dh=None, gu=None, layer=None, after=None):
    M, N = dy.shape
    (S, K, Ns), w_spec = _weight_spec(w3, layer)
    kchunk = kchunk or K
    tm = min(tm, M)
    row = lambda w: pl.BlockSpec((tm, w), lambda i: (i, 0))
    whole = lambda shape: pl.BlockSpec(shape, lambda i: (0,) * len(shape))
    ins, specs = [dy, w3], [row(N), w_spec]
    after = list(after or ())
    ins = after + ins
    specs = [pl.BlockSpec((8, a.shape[1]), lambda i: (0, 0)) for a in after] + specs
    has_dh = dh is not None
    if mode == "norm":
        ins += [h, gain.reshape(1, K)]; specs += [row(K), whole((1, K))]
        if has_dh:
            ins.append(dh); specs.append(row(K))
        out_shape = [jax.ShapeDtypeStruct((M, K), F32), jax.ShapeDtypeStruct((1, K), F32)]
        out_specs = [row(K), whole((1, K))]
    elif mode == "swiglu":
        ins.append(gu); specs.append(row(2 * K))
        out_shape, out_specs = [jax.ShapeDtypeStruct((M, 2 * K), BF16)], [row(2 * K)]
    else:
        out_shape, out_specs = [jax.ShapeDtypeStruct((M, K), BF16)], [row(K)]

    def body(*refs):
        it = iter(refs[len(after):])
        dy_ref, w_ref = next(it), next(it)
        if mode == "norm":
            h_ref, g_ref = next(it), next(it)
            dh_ref = next(it) if has_dh else None
            o_ref, dg_ref = next(it), next(it)
        elif mode == "swiglu":
            gu_ref, o_ref = next(it), next(it)
        else:
            o_ref = next(it)
        for kc in range(K // kchunk):
            ks = slice(kc * kchunk, (kc + 1) * kchunk)
            acc = None
            for s in range(S):
                t = _dot(dy_ref[:, s * Ns:(s + 1) * Ns].astype(BF16), w_ref[s, ks, :], NT)
                acc = t if acc is None else acc + t
            if mode == "plain":
                o_ref[:, ks] = acc.astype(BF16)
            elif mode == "swiglu":
                us = slice(K + kc * kchunk, K + (kc + 1) * kchunk)
                g = gu_ref[:, ks].astype(F32)
                u = gu_ref[:, us].astype(F32)
                sg = _sigmoid(g)
                o_ref[:, ks] = (acc * u * (sg * (1.0 + g * (1.0 - sg)))).astype(BF16)
                o_ref[:, us] = (acc * (g * sg)).astype(BF16)
            else:
                x = h_ref[...].astype(F32)
                r = lax.rsqrt(jnp.mean(x * x, axis=-1, keepdims=True) + NORM_EPS)
                xhat = x * r
                dxh = acc * g_ref[...]
                dx = r * (dxh - xhat * jnp.mean(dxh * xhat, axis=-1, keepdims=True))
                o_ref[...] = dx + dh_ref[...] if has_dh else dx

                @pl.when(pl.program_id(0) == 0)
                def _():
                    dg_ref[...] = jnp.zeros_like(dg_ref)

                dg_ref[...] += jnp.sum(acc * xhat, axis=0, keepdims=True)

    sem = ("arbitrary",) if mode == "norm" else ("parallel",)
    return _call(body, name=name, grid=(M // tm,), in_specs=specs, out_specs=out_specs, out_shape=out_shape, sem=sem)(*ins)


def _mm_tn(x, dy, *, S, name, tk=1024, kk=None, bias=False):
    M, K = x.shape
    N = dy.shape[1]
    Ns = N // S
    kk = kk or K
    tk = min(tk, M)
    nl = M // tk
    in_specs = [pl.BlockSpec((tk, kk), lambda s, kc, l: (l, kc)), pl.BlockSpec((tk, Ns), lambda s, kc, l: (l, s))]
    out_shape = [jax.ShapeDtypeStruct((S, K, Ns), BF16)]
    out_specs = [pl.BlockSpec((None, kk, Ns), lambda s, kc, l: (s, kc, 0))]
    if bias:
        out_shape.append(jax.ShapeDtypeStruct((1, N), F32))
        out_specs.append(pl.BlockSpec((1, Ns), lambda s, kc, l: (0, s)))

    def body(x_ref, dy_ref, o_ref, *rest):
        acc_ref = rest[-1]
        kc, l = pl.program_id(1), pl.program_id(2)

        @pl.when(l == 0)
        def _():
            acc_ref[...] = jnp.zeros_like(acc_ref)

        acc_ref[...] += _dot(x_ref[...].astype(BF16), dy_ref[...].astype(BF16), TN)
        if bias:
            b_ref = rest[0]

            @pl.when((kc == 0) & (l == 0))
            def _():
                b_ref[...] = jnp.zeros_like(b_ref)

            @pl.when(kc == 0)
            def _():
                b_ref[...] += jnp.sum(dy_ref[...].astype(F32), axis=0, keepdims=True)

        @pl.when(l == nl - 1)
        def _():
            o_ref[...] = acc_ref[...].astype(BF16)

    return _call(body, name=name, grid=(S, K // kk, nl), in_specs=in_specs, out_specs=out_specs, out_shape=out_shape,
                 scratch=[pltpu.VMEM((kk, Ns), F32)], sem=("arbitrary", "arbitrary", "arbitrary"))(x, dy)


def _band_bias(max_dist, has_prev):
    rows = lax.broadcasted_iota(jnp.int32, (BLK, 2 * BLK), 0)
    cols = lax.broadcasted_iota(jnp.int32, (BLK, 2 * BLK), 1)
    dist = rows - cols + BLK
    ok = (dist >= 0) & (dist <= max_dist) & ((cols >= BLK) | has_prev)
    return jnp.where(ok, 0.0, NEG)


Q_SCALE = HEAD_DIM ** -0.5


def _band_fwd(qa, ka, va, *, d, nq, nkv, qcol, kcol, vcol, max_dist, sinks=None, name):
    Lr = qa.shape[0]
    nb = Lr // BLK
    qw, kw, G = nq * HEAD_DIM, nkv * HEAD_DIM, nq // nkv
    cur = lambda colf, w: pl.BlockSpec((BLK, w), lambda r, i: (i, colf(r)))
    prv = lambda colf, w: pl.BlockSpec((BLK, w), lambda r, i: (jnp.maximum(i - 1, 0), colf(r)))
    out = pl.BlockSpec((BLK, qw), lambda r, i: (i, r))
    ins, specs = [qa, ka, ka, va, va], [cur(qcol, qw), cur(kcol, kw), prv(kcol, kw), cur(vcol, kw), prv(vcol, kw)]
    has_sinks = sinks is not None
    if has_sinks:
        ins.append(sinks); specs.append(pl.BlockSpec(memory_space=pltpu.SMEM))

    def body(*refs):
        q_ref, kc_ref, kp_ref, vc_ref, vp_ref = refs[:5]
        sk_ref = refs[5] if has_sinks else None
        o_ref, lse_ref = refs[-2], refs[-1]
        bias = _band_bias(max_dist, pl.program_id(1) > 0)
        k2 = jnp.concatenate([kp_ref[...], kc_ref[...]], axis=0)
        v2 = jnp.concatenate([vp_ref[...], vc_ref[...]], axis=0)
        for h in range(nq):
            hs = slice(h * HEAD_DIM, (h + 1) * HEAD_DIM)
            ks = slice((h // G) * HEAD_DIM, (h // G + 1) * HEAD_DIM)
            s = _dot(q_ref[:, hs] * jnp.asarray(Q_SCALE, BF16), k2[:, ks], NT) + bias
            m = jnp.max(s, axis=-1, keepdims=True)
            if has_sinks:
                m = jnp.maximum(m, sk_ref[h])
            p = jnp.exp(s - m)
            l = jnp.sum(p, axis=-1, keepdims=True)
            if has_sinks:
                l = l + jnp.exp(sk_ref[h] - m)
            o_ref[:, hs] = (_dot(p.astype(BF16), v2[:, ks]) / l).astype(BF16)
            lse_ref[:, hs] = jnp.broadcast_to(m + jnp.log(l), (BLK, HEAD_DIM))

    return _call(body, name=name, grid=(d, nb), in_specs=specs, out_specs=[out, out],
                 out_shape=[jax.ShapeDtypeStruct((Lr, d * qw), BF16), jax.ShapeDtypeStruct((Lr, d * qw), F32)],
                 sem=("parallel", "parallel"))(*ins)


def _band_bwd(qa, ka, va, doa, oa, lsea, *, d, nq, nkv, qcol, kcol, vcol, docol, max_dist, sinks=None, name):
    Lr = qa.shape[0]
    nb = Lr // BLK
    qw, kw, G = nq * HEAD_DIM, nkv * HEAD_DIM, nq // nkv
    transposed = G > 1
    last = lambda i: jnp.minimum(i, nb - 1)
    cur = lambda colf, w: pl.BlockSpec((BLK, w), lambda r, i: (last(i), colf(r)))
    prv = lambda colf, w: pl.BlockSpec((BLK, w), lambda r, i: (jnp.maximum(last(i) - 1, 0), colf(r)))
    own = lambda r: r
    ins = [qa, ka, ka, va, va, doa, oa, lsea]
    specs = [cur(qcol, qw), cur(kcol, kw), prv(kcol, kw), cur(vcol, kw), prv(vcol, kw), cur(docol, qw), cur(own, qw),
             cur(own, qw)]
    has_sinks = sinks is not None
    if has_sinks:
        ins.append(sinks); specs.append(pl.BlockSpec(memory_space=pltpu.SMEM))
    out_shape = [jax.ShapeDtypeStruct((Lr, d * qw), BF16), jax.ShapeDtypeStruct((Lr, d * kw), BF16),
                 jax.ShapeDtypeStruct((Lr, d * kw), BF16)]
    behind = lambda r, i: (jnp.maximum(i - 1, 0), r)
    out_specs = [pl.BlockSpec((BLK, qw), lambda r, i: (last(i), r)), pl.BlockSpec((BLK, kw), behind),
                 pl.BlockSpec((BLK, kw), behind)]
    if has_sinks:
        out_shape.append(jax.ShapeDtypeStruct((8, 128), F32))
        out_specs.append(pl.BlockSpec((8, 128), lambda r, i: (0, 0)))

    def body(*refs):
        it = iter(refs)
        q_ref, kc_ref, kp_ref, vc_ref, vp_ref, do_ref, o_ref, ls_ref = (next(it) for _ in range(8))
        sk_ref = next(it) if has_sinks else None
        dq_ref, dk_ref, dv_ref = next(it), next(it), next(it)
        dsk_ref = next(it) if has_sinks else None
        dk_car, dv_car = next(it), next(it)
        r_id, i = pl.program_id(0), pl.program_id(1)

        @pl.when(i == 0)
        def _():
            dk_car[...] = jnp.zeros_like(dk_car)
            dv_car[...] = jnp.zeros_like(dv_car)

        if has_sinks:
            @pl.when((r_id == 0) & (i == 0))
            def _():
                dsk_ref[...] = jnp.zeros_like(dsk_ref)

        @pl.when(i == nb)
        def _():
            dk_ref[...] = dk_car[...].astype(BF16)
            dv_ref[...] = dv_car[...].astype(BF16)

        @pl.when(i < nb)
        def _():
            bias = _band_bias(max_dist, i > 0)
            k2 = jnp.concatenate([kp_ref[...], kc_ref[...]], axis=0)
            v2 = jnp.concatenate([vp_ref[...], vc_ref[...]], axis=0)
            if has_sinks:
                lane = lax.broadcasted_iota(jnp.int32, (8, 128), 1)
                dsk = jnp.zeros((8, 128), F32)
            for kv in range(nkv):
                ks = slice(kv * HEAD_DIM, (kv + 1) * HEAD_DIM)
                kh, vh = k2[:, ks], v2[:, ks]
                shape = (HEAD_DIM, 2 * BLK) if transposed else (2 * BLK, HEAD_DIM)
                dk, dv = jnp.zeros(shape, F32), jnp.zeros(shape, F32)
                for g in range(G):
                    h = kv * G + g
                    hs = slice(h * HEAD_DIM, (h + 1) * HEAD_DIM)
                    q = q_ref[:, hs] * jnp.asarray(Q_SCALE, BF16)
                    do = do_ref[:, hs]
                    lse = ls_ref[:, h * HEAD_DIM:h * HEAD_DIM + 1]
                    dl = jnp.sum(do.astype(F32) * o_ref[:, hs].astype(F32), axis=-1, keepdims=True)
                    p = jnp.exp(_dot(q, kh, NT) + bias - lse)
                    ds = (p * (_dot(do, vh, NT) - dl)).astype(BF16)
                    dq_ref[:, hs] = (_dot(ds, kh) * Q_SCALE).astype(BF16)
                    if transposed:
                        dk = dk + _dot(q, ds, TN)
                        dv = dv + _dot(do, p.astype(BF16), TN)
                    else:
                        dk = dk + _dot(ds, q, TN)
                        dv = dv + _dot(p.astype(BF16), do, TN)
                    if has_sinks:
                        val = -jnp.sum(jnp.exp(sk_ref[h] - lse) * dl, axis=0, keepdims=True)
                        dsk = dsk + jnp.where(lane == h, val, 0.0)
                if transposed:
                    dk, dv = dk.T, dv.T
                dk_ref[:, ks] = (dk_car[:, ks] + dk[:BLK]).astype(BF16)
                dv_ref[:, ks] = (dv_car[:, ks] + dv[:BLK]).astype(BF16)
                dk_car[:, ks] = dk[BLK:]
                dv_car[:, ks] = dv[BLK:]
            if has_sinks:
                dsk_ref[...] += dsk

    return _call(body, name=name, grid=(d, nb + 1), in_specs=specs, out_specs=out_specs, out_shape=out_shape,
                 scratch=[pltpu.VMEM((BLK, kw), F32), pltpu.VMEM((BLK, kw), F32)], sem=("arbitrary", "arbitrary"))(*ins)


def _attn_grad_combine(branches, tabs, *, name, tm=256):
    L, qw = branches[0][0].shape
    kw = branches[0][1].shape[1]
    nbr = len(branches)
    row = lambda w: pl.BlockSpec((tm, w), lambda i: (i, 0))
    ins, specs = [], []
    for dq, dk, dv in branches:
        ins += [dq, dk, dv]; specs += [row(qw), row(kw), row(kw)]
    ins += list(tabs); specs += [row(128)] * 3

    def body(*refs):
        c, s1, s2 = (t[...] for t in refs[3 * nbr:3 * nbr + 3])
        o_ref = refs[-1]
        for part, (w, off, rot) in enumerate(((qw, 0, True), (kw, qw, True), (kw, qw + kw, False))):
            for cb in range(w // 128):
                cs = slice(cb * 128, (cb + 1) * 128)
                v = refs[part][:, cs].astype(F32)
                for b in range(1, nbr):
                    v = v + refs[3 * b + part][:, cs].astype(F32)
                if rot:
                    v = _rope_bwd(v, c, s1, s2)
                o_ref[:, off + cb * 128:off + (cb + 1) * 128] = v.astype(BF16)

    return _call(body, name=name, grid=(L // tm,), in_specs=specs, out_specs=row(qw + 2 * kw),
                 out_shape=jax.ShapeDtypeStruct((L, qw + 2 * kw), BF16), sem=("parallel",))(*ins)


def _xattn_fwd(q, kv, *, name, tq=512):
    L, W = q.shape
    scale = XA_HEAD_DIM ** -0.5
    row = pl.BlockSpec((tq, W), lambda i: (i, 0))
    kvs = pl.BlockSpec((N_MEM, 2 * W), lambda i: (0, 0))

    def body(q_ref, kv_ref, o_ref, lse_ref):
        for h in range(XA_HEADS):
            hs = slice(h * XA_HEAD_DIM, (h + 1) * XA_HEAD_DIM)
            vs = slice(W + h * XA_HEAD_DIM, W + (h + 1) * XA_HEAD_DIM)
            s = _dot(q_ref[:, hs], kv_ref[:, hs], NT) * scale
            m = jnp.max(s, axis=-1, keepdims=True)
            p = jnp.exp(s - m)
            l = jnp.sum(p, axis=-1, keepdims=True)
            o_ref[:, hs] = (_dot(p.astype(BF16), kv_ref[:, vs]) / l).astype(BF16)
            lse_ref[:, hs] = jnp.broadcast_to(m + jnp.log(l), (tq, XA_HEAD_DIM))

    return _call(body, name=name, grid=(L // tq,), in_specs=[row, kvs], out_specs=[row, row],
                 out_shape=[jax.ShapeDtypeStruct((L, W), BF16), jax.ShapeDtypeStruct((L, W), F32)], sem=("parallel",))(q, kv)


def _xattn_bwd(q, kv, o, lse, do, *, name, tq=512):
    L, W = q.shape
    scale = XA_HEAD_DIM ** -0.5
    row = pl.BlockSpec((tq, W), lambda i: (i, 0))
    kvs = pl.BlockSpec((N_MEM, 2 * W), lambda i: (0, 0))

    def body(q_ref, kv_ref, o_ref, lse_ref, do_ref, dq_ref, dkv_ref):
        @pl.when(pl.program_id(0) == 0)
        def _():
            dkv_ref[...] = jnp.zeros_like(dkv_ref)

        for h in range(XA_HEADS):
            hs = slice(h * XA_HEAD_DIM, (h + 1) * XA_HEAD_DIM)
            vs = slice(W + h * XA_HEAD_DIM, W + (h + 1) * XA_HEAD_DIM)
            qh, kh, vh, doh = q_ref[:, hs], kv_ref[:, hs], kv_ref[:, vs], do_ref[:, hs]
            p = jnp.exp(_dot(qh, kh, NT) * scale - lse_ref[:, h * XA_HEAD_DIM:h * XA_HEAD_DIM + 1])
            dl = jnp.sum(doh.astype(F32) * o_ref[:, hs].astype(F32), axis=-1, keepdims=True)
            ds = (p * (_dot(doh, vh, NT) - dl) * scale).astype(BF16)
            dq_ref[:, hs] = _dot(ds, kh).astype(BF16)
            dkv_ref[:, hs] += _dot(ds, qh, TN)
            dkv_ref[:, vs] += _dot(p.astype(BF16), doh, TN)

    return _call(body, name=name, grid=(L // tq,), in_specs=[row, kvs, row, row, row], out_specs=[row, kvs],
                 out_shape=[jax.ShapeDtypeStruct((L, W), BF16), jax.ShapeDtypeStruct((N_MEM, 2 * W), F32)],
                 sem=("arbitrary",))(q, kv, o, lse, do)


def _neg_expm1(z):
    series = -(z * (1.0 + z * (0.5 + z * (1.0 / 6.0 + z * (1.0 / 24.0 + z * (1.0 / 120.0))))))
    return jnp.where(z > -0.05, series, 1.0 - jnp.exp(z))


def _softplus(z):
    return jnp.maximum(z, 0.0) + jnp.log(1.0 + jnp.exp(-jnp.abs(z)))


def _gelu_parts(y):
    c = 0.7978845608028654
    t = jnp.tanh(c * (y + 0.044715 * y * y * y))
    gy = 0.5 * y * (1.0 + t)
    dgy = 0.5 * (1.0 + t) + 0.5 * y * (1.0 - t * t) * c * (1.0 + 3.0 * 0.044715 * y * y)
    return gy, dgy


def _lru_gates(xc, wa_ref, ba, wx_ref, bx, sp):
    rs, igs = [], []
    for hd in range(LRU_HEADS):
        sl = slice(hd * LRU_HEAD_DIM, (hd + 1) * LRU_HEAD_DIM)
        xh = xc[:, sl].astype(BF16)
        rs.append(_sigmoid(_dot(xh, wa_ref[hd]) + ba[:, sl]))
        igs.append(_sigmoid(_dot(xh, wx_ref[hd]) + bx[:, sl]))
    r, ig = jnp.concatenate(rs, axis=1), jnp.concatenate(igs, axis=1)
    la = -LRU_C * r * sp
    return r, ig, jnp.exp(la), _neg_expm1(2.0 * la)


def _conv_taps(x_ext, halo):
    n = x_ext.shape[0]
    return [x_ext[halo:] if k == CONV_WIDTH - 1 else pltpu.roll(x_ext, CONV_WIDTH - 1 - k, 0)[halo:]
            for k in range(CONV_WIDTH)]


def _lru_fwd(proj, cw, cb, wa, ba, wx, bx, lam, *, name, tc=512):
    L = proj.shape[0]
    W = LRU_HEADS * LRU_HEAD_DIM
    nb = L // tc
    whole = lambda shape: pl.BlockSpec(shape, lambda i: (0,) * len(shape))
    specs = [pl.BlockSpec((tc, W), lambda i: (i, 0)), pl.BlockSpec((tc, W), lambda i: (i, 1)),
             pl.BlockSpec((16, W), lambda i: (jnp.maximum(i * (tc // 16) - 1, 0), 0)),
             whole((CONV_WIDTH, W)), whole((1, W)), whole((LRU_HEADS, LRU_HEAD_DIM, LRU_HEAD_DIM)), whole((1, W)),
             whole((LRU_HEADS, LRU_HEAD_DIM, LRU_HEAD_DIM)), whole((1, W)), whole((1, W))]
    out_specs = [pl.BlockSpec((tc, W), lambda i: (i, 0))] * 2
    out_shape = [jax.ShapeDtypeStruct((L, W), BF16), jax.ShapeDtypeStruct((L, W), F32)]

    def body(x_ref, y_ref, xh_ref, cw_ref, cb_ref, wa_ref, ba_ref, wx_ref, bx_ref, lam_ref, rec_ref, hs_ref,
             hcar, a_scr, b_scr):
        i = pl.program_id(0)

        @pl.when(i == 0)
        def _():
            hcar[...] = jnp.zeros_like(hcar)

        halo = jnp.where(i > 0, xh_ref[...].astype(F32), 0.0)
        taps = _conv_taps(jnp.concatenate([halo, x_ref[...].astype(F32)], axis=0), 16)
        xc = cb_ref[...] + sum(cw_ref[k:k + 1, :] * taps[k] for k in range(CONV_WIDTH))
        _, ig, a, om = _lru_gates(xc, wa_ref, ba_ref[...], wx_ref, bx_ref[...], _softplus(-lam_ref[...]))
        b = jnp.sqrt(om) * (ig * xc)
        rowmod = lax.broadcasted_iota(jnp.int32, (tc, W), 0) & 7
        for s in (1, 2, 4):
            keep = rowmod >= s
            b = jnp.where(keep, a * pltpu.roll(b, s, 0) + b, b)
            a = jnp.where(keep, a * pltpu.roll(a, s, 0), a)
        a_scr[...] = a
        b_scr[...] = b

        def tile(j, hc):
            rows = pl.ds(pl.multiple_of(j * 8, 8), 8)
            ht = a_scr[rows, :] * hc + b_scr[rows, :]
            hs_ref[rows, :] = ht
            return jnp.broadcast_to(ht[7:8, :], (8, W))

        hcar[...] = lax.fori_loop(0, tc // 8, tile, hcar[...])
        gy, _ = _gelu_parts(y_ref[...].astype(F32))
        rec_ref[...] = (hs_ref[...] * gy).astype(BF16)

    return _call(body, name=name, grid=(nb,), in_specs=specs, out_specs=out_specs, out_shape=out_shape,
                 scratch=[pltpu.VMEM((8, W), F32), pltpu.VMEM((tc, W), F32), pltpu.VMEM((tc, W), F32)],
                 sem=("arbitrary",))(proj, proj, proj, cw, cb, wa, ba, wx, bx, lam)


def _lru_bwd(proj, hs, drec_src, cw, cb, wa, ba, wx, bx, lam, *, name, tc=256):
    L = proj.shape[0]
    W = LRU_HEADS * LRU_HEAD_DIM
    nb = L // tc
    tb = lambda i: nb - 1 - i
    whole = lambda shape: pl.BlockSpec(shape, lambda i: (0,) * len(shape))
    gate_w = (LRU_HEADS, LRU_HEAD_DIM, LRU_HEAD_DIM)
    specs = [pl.BlockSpec((tc, W), lambda i: (tb(i), 0)), pl.BlockSpec((tc, W), lambda i: (tb(i), 1)),
             pl.BlockSpec((16, W), lambda i: (jnp.maximum(tb(i) * (tc // 16) - 1, 0), 0)),
             pl.BlockSpec((tc, W), lambda i: (tb(i), 0)),
             pl.BlockSpec((8, W), lambda i: (jnp.maximum(tb(i) * (tc // 8) - 1, 0), 0)),
             pl.BlockSpec((tc, W), lambda i: (tb(i), 0)),
             whole((CONV_WIDTH, W)), whole((1, W)), whole(gate_w), whole((1, W)), whole(gate_w), whole((1, W)), whole((1, W))]
    out_specs = [pl.BlockSpec((tc, 2 * W), lambda i: (tb(i), 0)), whole((CONV_WIDTH, W)), whole((1, W)), whole(gate_w),
                 whole((1, W)), whole(gate_w), whole((1, W)), whole((1, W))]
    vec = jax.ShapeDtypeStruct((1, W), F32)
    out_shape = [jax.ShapeDtypeStruct((L, 2 * W), BF16), jax.ShapeDtypeStruct((CONV_WIDTH, W), F32), vec,
                 jax.ShapeDtypeStruct(gate_w, F32), vec, jax.ShapeDtypeStruct(gate_w, F32), vec, vec]

    def body(x_ref, y_ref, xh_ref, hs_ref, hh_ref, dr_ref, cw_ref, cb_ref, wa_ref, ba_ref, wx_ref, bx_ref, lam_ref,
             dxy_ref, dcw_ref, dcb_ref, dwa_ref, dba_ref, dwx_ref, dbx_ref, dlam_ref, gcar, dxc_car, a_scr, b_scr, g_scr):
        pid = pl.program_id(0)
        t = tb(pid)
        accs = (dcw_ref, dcb_ref, dwa_ref, dba_ref, dwx_ref, dbx_ref, dlam_ref)

        @pl.when(pid == 0)
        def _():
            gcar[...] = jnp.zeros_like(gcar)
            dxc_car[...] = jnp.zeros_like(dxc_car)
            for r in accs:
                r[...] = jnp.zeros_like(r)

        halo = jnp.where(t > 0, xh_ref[...].astype(F32), 0.0)
        taps = _conv_taps(jnp.concatenate([halo, x_ref[...].astype(F32)], axis=0), 16)
        xc = cb_ref[...] + sum(cw_ref[k:k + 1, :] * taps[k] for k in range(CONV_WIDTH))
        lam = lam_ref[...]
        sp = _softplus(-lam)
        r, ig, a, om = _lru_gates(xc, wa_ref, ba_ref[...], wx_ref, bx_ref[...], sp)
        sq = jnp.sqrt(om)
        hblk = hs_ref[...]
        hprev = pltpu.roll(jnp.concatenate([jnp.where(t > 0, hh_ref[...], 0.0), hblk], axis=0), 1, 0)[8:]
        gy, dgy = _gelu_parts(y_ref[...].astype(F32))
        drec = dr_ref[...].astype(F32)
        dxy_ref[:, W:] = (drec * hblk * dgy).astype(BF16)

        rowidx = lax.broadcasted_iota(jnp.int32, (tc, W), 0)
        rowmod = rowidx & 7
        ca = jnp.where(rowidx == tc - 1, 1.0, pltpu.roll(a, tc - 1, 0))
        cbv = drec * gy
        for s in (1, 2, 4):
            keep = rowmod < 8 - s
            cbv = jnp.where(keep, ca * pltpu.roll(cbv, tc - s, 0) + cbv, cbv)
            ca = jnp.where(keep, ca * pltpu.roll(ca, tc - s, 0), ca)
        a_scr[...] = ca
        b_scr[...] = cbv

        def tile(k, gc):
            j = tc // 8 - 1 - k
            rows = pl.ds(pl.multiple_of(j * 8, 8), 8)
            gt = a_scr[rows, :] * gc + b_scr[rows, :]
            g_scr[rows, :] = gt
            return jnp.broadcast_to(gt[0:1, :], (8, W))

        lax.fori_loop(0, tc // 8, tile, gcar[...])
        G = g_scr[...]
        gcar[...] = jnp.broadcast_to(a[0:1, :] * G[0:1, :], (8, W))

        da = G * hprev
        dsq = G * (ig * xc)
        di = G * (sq * xc)
        dxc = G * (sq * ig)
        dla = da * a - 2.0 * a * a * (dsq * 0.5 * lax.rsqrt(om))
        dlam_ref[...] += jnp.sum(dla * (-LRU_C * r), axis=0, keepdims=True) * (-_sigmoid(-lam))
        dpr = dla * (-LRU_C * sp) * r * (1.0 - r)
        dpi = di * ig * (1.0 - ig)
        dba_ref[...] += jnp.sum(dpr, axis=0, keepdims=True)
        dbx_ref[...] += jnp.sum(dpi, axis=0, keepdims=True)
        back = []
        for hd in range(LRU_HEADS):
            sl = slice(hd * LRU_HEAD_DIM, (hd + 1) * LRU_HEAD_DIM)
            xh, dprh, dpih = xc[:, sl].astype(BF16), dpr[:, sl].astype(BF16), dpi[:, sl].astype(BF16)
            back.append(_dot(dprh, wa_ref[hd], NT) + _dot(dpih, wx_ref[hd], NT))
            dwa_ref[hd] += _dot(xh, dprh, TN)
            dwx_ref[hd] += _dot(xh, dpih, TN)
        dxc = dxc + jnp.concatenate(back, axis=1)
        dcb_ref[...] += jnp.sum(dxc, axis=0, keepdims=True)
        for k in range(CONV_WIDTH):
            dcw_ref[k:k + 1, :] += jnp.sum(dxc * taps[k], axis=0, keepdims=True)
        ext = jnp.concatenate([dxc, dxc_car[...]], axis=0)
        dx = cw_ref[CONV_WIDTH - 1:CONV_WIDTH, :] * dxc
        for k in range(CONV_WIDTH - 1):
            dx = dx + cw_ref[k:k + 1, :] * pltpu.roll(ext, tc + 8 - (CONV_WIDTH - 1 - k), 0)[:tc]
        dxc_car[...] = dxc[0:8, :]
        dxy_ref[:, :W] = dx.astype(BF16)

    scratch = [pltpu.VMEM((8, W), F32), pltpu.VMEM((8, W), F32)] + [pltpu.VMEM((tc, W), F32)] * 3
    return _call(body, name=name, grid=(nb,), in_specs=specs, out_specs=out_specs, out_shape=out_shape, scratch=scratch,
                 sem=("arbitrary",))(proj, proj, proj, hs, hs, drec_src, cw, cb, wa, ba, wx, bx, lam)


def _final_loss(h, gain, target, *, name, tm=256):
    M, K = h.shape
    row = pl.BlockSpec((tm, K), lambda i: (i, 0))
    vec = pl.BlockSpec((1, K), lambda i: (0, 0))
    one = pl.BlockSpec((1, 128), lambda i: (0, 0))

    def body(h_ref, g_ref, t_ref, dh_ref, dg_ref, loss_ref):
        @pl.when(pl.program_id(0) == 0)
        def _():
            dg_ref[...] = jnp.zeros_like(dg_ref)
            loss_ref[...] = jnp.zeros_like(loss_ref)

        x = h_ref[...]
        r = lax.rsqrt(jnp.mean(x * x, axis=-1, keepdims=True) + NORM_EPS)
        xhat = x * r
        err = xhat * g_ref[...] - t_ref[...]
        loss_ref[...] += 0.5 / K * jnp.sum(err * err)
        dy = err * (1.0 / K)
        dg_ref[...] += jnp.sum(dy * xhat, axis=0, keepdims=True)
        dxh = dy * g_ref[...]
        dh_ref[...] = r * (dxh - xhat * jnp.mean(dxh * xhat, axis=-1, keepdims=True))

    return _call(body, name=name, grid=(M // tm,), in_specs=[row, vec, row], out_specs=[row, vec, one],
                 out_shape=[jax.ShapeDtypeStruct((M, K), F32), jax.ShapeDtypeStruct((1, K), F32),
                            jax.ShapeDtypeStruct((1, 128), F32)], sem=("arbitrary",))(h, gain.reshape(1, K), target)


def _dilated_merge(branches, *, name, tm=512):
    L, W = branches[0].shape
    nbr = len(branches) // 2
    row = pl.BlockSpec((tm, W), lambda i: (i, 0))

    def body(*refs):
        o_ref, lse_ref = refs[-2], refs[-1]
        lses = [refs[2 * b + 1][...] for b in range(nbr)]
        m = lses[0]
        for t in lses[1:]:
            m = jnp.maximum(m, t)
        ws = [jnp.exp(t - m) for t in lses]
        den = ws[0]
        for t in ws[1:]:
            den = den + t
        acc = ws[0] * refs[0][...].astype(F32)
        for b in range(1, nbr):
            acc = acc + ws[b] * refs[2 * b][...].astype(F32)
        o_ref[...] = (acc / den).astype(BF16)
        lse_ref[...] = m + jnp.log(den)

    return _call(body, name=name, grid=(L // tm,), in_specs=[row] * (2 * nbr), out_specs=[row, row],
                 out_shape=[jax.ShapeDtypeStruct((L, W), BF16), jax.ShapeDtypeStruct((L, W), F32)], sem=("parallel",))(*branches)


def _dilated_fwd(proj0):
    L = proj0.shape[0]
    qkv = proj0[:, 2 * D_MODEL:]
    W = B_HEADS * HEAD_DIM
    outs = []
    for window, d in DILATED_PATTERN:
        view = qkv.reshape(L // d, d * 3 * W)
        o, lse = _band_fwd(view, view, view, d=d, nq=B_HEADS, nkv=B_HEADS, qcol=lambda r: 3 * r, kcol=lambda r: 3 * r + 1,
                           vcol=lambda r: 3 * r + 2, max_dist=window // d, name=f"dilated_fwd_d{d}")
        outs += [o.reshape(L, W), lse.reshape(L, W)]
    return _dilated_merge(outs, name="dilated_merge")


def _dilated_bwd(proj0, att, lse, datt, tabs):
    L = proj0.shape[0]
    qkv = proj0[:, 2 * D_MODEL:]
    Wh = B_HEADS * HEAD_DIM
    branches = []
    for window, d in DILATED_PATTERN:
        view = qkv.reshape(L // d, d * 3 * Wh)
        v1 = lambda t: t.reshape(L // d, d * Wh)
        outs = _band_bwd(view, view, view, v1(datt), v1(att), v1(lse), d=d, nq=B_HEADS, nkv=B_HEADS,
                         qcol=lambda r: 3 * r, kcol=lambda r: 3 * r + 1, vcol=lambda r: 3 * r + 2, docol=lambda r: r,
                         max_dist=window // d, name=f"dilated_bwd_d{d}")
        branches.append([o.reshape(L, Wh) for o in outs])
    return _attn_grad_combine(branches, tabs, name="dilated_grad_combine")


def _device_step(x, mem, target, w, on_grads=None):
    L = x.shape[0]
    tabs = _rope_tables(L)
    g = {}
    saved = []
    h = x
    for layer in range(2):
        sv = {"h_mix": h}
        if layer == 0:
            proj, n = _rowmm(h, w["ab_w_in"], name="l0_in_proj", gain=w["mix_norm"][0],
                             rope=(2 * D_MODEL, 2 * D_MODEL + 2 * B_HEADS * HEAD_DIM, tabs))
            rec, hs = _lru_fwd(proj, w["lru_conv_w"], w["lru_conv_b"], w["lru_wa"], w["lru_ba"], w["lru_wx"], w["lru_bx"],
                               w["lru_lambda"], name="lru_fwd")
            att, lse = _dilated_fwd(proj)
            mix = jnp.concatenate([rec, att], axis=1)
            (h,) = _rowmm(mix, w["ab_w_out"], name="l0_out_proj", res=h)
            sv.update(hs=hs)
        else:
            proj, n = _rowmm(h, w["c_w_qkv"], name="l1_qkv_proj", gain=w["mix_norm"][1], bias=w["c_b_qkv"],
                             rope=(0, (C_HEADS + C_KV_HEADS) * HEAD_DIM, tabs))
            mix, lse = _band_fwd(proj, proj, proj, d=1, nq=C_HEADS, nkv=C_KV_HEADS, qcol=lambda r: 0, kcol=lambda r: 8,
                                 vcol=lambda r: 9, max_dist=C_WINDOW - 1, sinks=w["c_sinks"], name="swa_fwd")
            (h,) = _rowmm(mix, w["c_w_out"], name="l1_out_proj", res=h, bias=w["c_b_out"])
        sv.update(proj=proj, n_mix=n, mix=mix, lse=lse, h_xa=h)
        xq, nx = _rowmm(h, w["xa_wq"][layer][None], name=f"xa_q_proj{layer}", gain=w["xa_norm"][layer])
        kv, nm = _rowmm(mem, w["xa_wkv"][layer][None], name=f"xa_kv_proj{layer}", gain=w["xa_mem_norm"][layer])
        xo, xlse = _xattn_fwd(xq, kv, name=f"xa_fwd{layer}")
        (h,) = _rowmm(xo, w["xa_wo"][layer], name=f"xa_out_proj{layer}", res=h)
        sv.update(xq=xq, nx=nx, kv=kv, nm=nm, xo=xo, xlse=xlse, h_ffn=h)
        gu, nf, act = _rowmm(h, w["ffn_w_gate_up"], layer=layer, name=f"ffn_in{layer}", gain=w["ffn_norm"][layer], swiglu=True)
        (h,) = _rowmm(act, w["ffn_w_down"][layer][None], name=f"ffn_out{layer}", res=h, tm=512)
        sv.update(gu=gu, nf=nf, act=act)
        saved.append(sv)

    dh, g["final_norm"], loss = _final_loss(h, w["final_norm"], target, name="final_loss")

    stk = {k: [None, None] for k in ("xa_norm", "xa_mem_norm", "ffn_norm", "mix_norm")}
    after = None
    for layer in (1, 0):
        sv = saved[layer]
        (g["ffn_w_down", layer],) = _mm_tn(sv["act"], dh, S=1, name=f"ffn_down_dw{layer}", kk=D_FF // 2)
        (dgu,) = _mm_nt(dh, w["ffn_w_down"][layer][None], name=f"ffn_dact{layer}", mode="swiglu", kchunk=D_FF // 2, gu=sv["gu"],
                        after=after)
        (g["ffn_w_gate_up", layer],) = _mm_tn(sv["nf"], dgu, S=N_CHIPS, name=f"ffn_gu_dw{layer}")
        dh, stk["ffn_norm"][layer] = _mm_nt(dgu, w["ffn_w_gate_up"], layer=layer, name=f"ffn_dx{layer}", mode="norm",
                                            h=sv["h_ffn"], gain=w["ffn_norm"][layer], dh=dh)
        (g["xa_wo", layer],) = _mm_tn(sv["xo"], dh, S=N_CHIPS, name=f"xa_wo_dw{layer}")
        (dxo,) = _mm_nt(dh, w["xa_wo"][layer], name=f"xa_dxo{layer}", mode="plain")
        dxq, dkv = _xattn_bwd(sv["xq"], sv["kv"], sv["xo"], sv["xlse"], dxo, name=f"xa_bwd{layer}")
        (g["xa_wq", layer],) = _mm_tn(sv["nx"], dxq, S=1, name=f"xa_wq_dw{layer}")
        dh, stk["xa_norm"][layer] = _mm_nt(dxq, w["xa_wq"][layer][None], name=f"xa_dx{layer}", mode="norm", h=sv["h_xa"],
                                           gain=w["xa_norm"][layer], dh=dh)
        (g["xa_wkv", layer],) = _mm_tn(sv["nm"], dkv, S=1, name=f"xa_wkv_dw{layer}")
        _, stk["xa_mem_norm"][layer] = _mm_nt(dkv, w["xa_wkv"][layer][None], name=f"xa_dmem{layer}", mode="norm", h=mem,
                                              gain=w["xa_mem_norm"][layer])
        if layer == 1:
            g["c_w_out"], g["c_b_out"] = _mm_tn(sv["mix"], dh, S=1, name="l1_out_dw", bias=True)
            (dmix,) = _mm_nt(dh, w["c_w_out"], name="l1_dmix", mode="plain")
            dq, dk, dv, dsk = _band_bwd(sv["proj"], sv["proj"], sv["proj"], dmix, sv["mix"], sv["lse"], d=1, nq=C_HEADS,
                                        nkv=C_KV_HEADS, qcol=lambda r: 0, kcol=lambda r: 8, vcol=lambda r: 9,
                                        docol=lambda r: 0, max_dist=C_WINDOW - 1, sinks=w["c_sinks"], name="swa_bwd")
            g["c_sinks"] = dsk[0, :C_HEADS]
            dproj = _attn_grad_combine([(dq, dk, dv)], tabs, name="swa_grad_combine")
            g["c_w_qkv"], g["c_b_qkv"] = _mm_tn(sv["n_mix"], dproj, S=1, name="l1_qkv_dw", bias=True)
            dh, stk["mix_norm"][1] = _mm_nt(dproj, w["c_w_qkv"], name="l1_dx", mode="norm", h=sv["h_mix"],
                                            gain=w["mix_norm"][1], dh=dh)
            if on_grads is not None:
                after = on_grads("layer1", g)
        else:
            if on_grads is not None:
                after = on_grads("layer0_ffn_xa", g)
            (g["ab_w_out"],) = _mm_tn(sv["mix"], dh, S=1, name="l0_out_dw", kk=768)
            (dmix,) = _mm_nt(dh, w["ab_w_out"], name="l0_dmix", mode="plain", kchunk=768, after=after)
            (dxy, g["lru_conv_w"], g["lru_conv_b"], g["lru_wa"], g["lru_ba"], g["lru_wx"], g["lru_bx"],
             g["lru_lambda"]) = _lru_bwd(sv["proj"], sv["hs"], dmix, w["lru_conv_w"], w["lru_conv_b"], w["lru_wa"],
                                         w["lru_ba"], w["lru_wx"], w["lru_bx"], w["lru_lambda"], name="lru_bwd")
            dqkv = _dilated_bwd(sv["proj"], sv["mix"][:, D_MODEL:], sv["lse"], dmix[:, D_MODEL:], tabs)
            dproj = jnp.concatenate([dxy, dqkv], axis=1)
            (g["ab_w_in"],) = _mm_tn(sv["n_mix"], dproj, S=N_CHIPS, name="l0_in_dw")
            dh, stk["mix_norm"][0] = _mm_nt(dproj, w["ab_w_in"], name="l0_dx", mode="norm", h=sv["h_mix"],
                                            gain=w["mix_norm"][0], dh=dh)
    for k, v in stk.items():
        g[k] = jnp.concatenate(v, axis=0)
    return loss[0, 0], dh, g


ANY = pl.BlockSpec(memory_space=pl.ANY)
MESH = pl.DeviceIdType.MESH


def _place():
    x, y, c = lax.axis_index("x"), lax.axis_index("y"), lax.axis_index("c")
    return x, y, c, [(1 - x, y), (x, 1 - y), (1 - x, 1 - y)]


def _remote(send_sems, recv_sems):
    def copy(k, src, dst, to):
        return pltpu.make_async_remote_copy(src_ref=src, dst_ref=dst, send_sem=send_sems.at[k], recv_sem=recv_sems.at[k],
                                            device_id=to, device_id_type=MESH)
    return copy


def _halves(ref, n_rows):
    rh = n_rows // 2
    return lambda lead, hh: ref.at[(*lead, pl.ds(hh * rh, rh), slice(None))]


def _gather_weights(packs, spack):
    n = len(packs)

    def body(*refs):
        w_refs, s_ref, wf_refs, sf_ref = refs[:n], refs[n], refs[n + 1:2 * n + 1], refs[2 * n + 1]
        x, y, c, chips = _place()
        me, sib = 2 * x + y, (x, y, 1 - c)
        copy = _remote(*refs[-2:])
        src = [_halves(w_refs[g], packs[g].shape[0]) for g in range(n)]
        dst = [_halves(wf_refs[g], packs[g].shape[0]) for g in range(n)]
        sends = []
        for g in range(n):
            for j, (cx, cy) in enumerate(chips):
                sends.append(copy(3 * g + j, src[g]((), c), dst[g]((me,), c), (cx, cy, c)))
        for j, (cx, cy) in enumerate(chips):
            sends.append(copy(6 * n + j, s_ref, sf_ref.at[me], (cx, cy, c)))
        for cp in sends:
            cp.start()
        for g in range(n):
            for j, (cx, cy) in enumerate(chips):
                got = dst[g]((2 * cx + cy,), c)
                copy(3 * g + j, got, got, sib).wait_recv()
                fwd = copy(3 * n + 3 * g + j, got, got, sib)
                fwd.start()
                sends.append(fwd)
        for g in range(n):
            for j, (cx, cy) in enumerate(chips):
                got = dst[g]((2 * cx + cy,), 1 - c)
                copy(3 * n + 3 * g + j, got, got, sib).wait_recv()
        for j, (cx, cy) in enumerate(chips):
            copy(6 * n + j, s_ref, sf_ref.at[2 * cx + cy], sib).wait_recv()
        for cp in sends:
            cp.wait_send()

    ins = list(packs) + [spack]
    out_shape = [jax.ShapeDtypeStruct((N_CHIPS,) + a.shape, a.dtype) for a in ins]
    n_sems = 6 * n + 3
    outs = pl.pallas_call(body, name="gather_weights", out_shape=out_shape, in_specs=[ANY] * len(ins),
                          out_specs=[ANY] * len(ins),
                          scratch_shapes=[pltpu.SemaphoreType.DMA((n_sems,)), pltpu.SemaphoreType.DMA((n_sems,))])(*ins)
    chip = 2 * lax.axis_index("x") + lax.axis_index("y")
    outs = [lax.dynamic_update_index_in_dim(o, a, chip, 0) for o, a in zip(outs, ins)]
    return outs[:n], outs[n]


SEQUENCER_GATHER_IDS = {"mid": 1, "late": 5}


def _gather_weights_behind(packs, *, tag):
    n = len(packs)

    def body(*refs):
        w_refs, wf_refs = refs[:n], refs[n:2 * n]
        x, y, c, chips = _place()
        me, sib = 2 * x + y, (x, y, 1 - c)
        barrier = pltpu.get_barrier_semaphore()
        for peer in [(cx, cy, c) for cx, cy in chips] + [sib]:
            pl.semaphore_signal(barrier, inc=1, device_id=peer, device_id_type=MESH)
        pl.semaphore_wait(barrier, len(chips) + 1)
        copy = _remote(*refs[-2:])
        src = [_halves(w_refs[g], packs[g].shape[0]) for g in range(n)]
        dst = [_halves(wf_refs[g], packs[g].shape[0]) for g in range(n)]
        sends = []
        for g in range(n):
            for j, (cx, cy) in enumerate(chips):
                sends.append(copy(3 * g + j, src[g]((), c), dst[g]((me,), c), (cx, cy, c)))
        for cp in sends:
            cp.start()
        for g in range(n):
            for j, (cx, cy) in enumerate(chips):
                got = dst[g]((2 * cx + cy,), c)
                copy(3 * g + j, got, got, sib).wait_recv()
                fwd = copy(3 * n + 3 * g + j, got, got, sib)
                fwd.start()
                sends.append(fwd)
        for g in range(n):
            for j, (cx, cy) in enumerate(chips):
                got = dst[g]((2 * cx + cy,), 1 - c)
                copy(3 * n + 3 * g + j, got, got, sib).wait_recv()
        for cp in sends:
            cp.wait_send()

    out_type = [jax.ShapeDtypeStruct((N_CHIPS,) + a.shape, a.dtype) for a in packs]
    outs = pl.kernel(body, out_type=out_type, mesh=plsc.ScalarSubcoreMesh(axis_name="sequencer", num_cores=1),
                     name="gather_weights_behind_" + tag,
                     scratch_types=[pltpu.SemaphoreType.DMA((6 * n,)), pltpu.SemaphoreType.DMA((6 * n,))],
                     compiler_params=pltpu.CompilerParams(collective_id=SEQUENCER_GATHER_IDS[tag]))(*packs)
    chip = 2 * lax.axis_index("x") + lax.axis_index("y")
    return [lax.dynamic_update_index_in_dim(o, a, chip, 0) for o, a in zip(outs, packs)]


def _rs_pair_exchange(gpacks, *, name):
    n = len(gpacks)

    def body(*refs):
        g_refs, ra_refs = refs[:n], refs[n:2 * n]
        x, y, c, _ = _place()
        copy = _remote(*refs[-2:])
        cps = []
        for g in range(n):
            half = _halves(g_refs[g], gpacks[g].shape[1])
            cps += [copy(N_CHIPS * g + j, half((j,), 1 - c), ra_refs[g].at[j], (x, y, 1 - c)) for j in range(N_CHIPS)]
        for cp in cps:
            cp.start()
        for cp in cps:
            cp.wait()

    out_shape = [jax.ShapeDtypeStruct((N_CHIPS, a.shape[1] // 2, a.shape[2]), a.dtype) for a in gpacks]
    n_sems = N_CHIPS * n
    return pl.pallas_call(body, name=name, out_shape=out_shape, in_specs=[ANY] * n, out_specs=[ANY] * n,
                          scratch_shapes=[pltpu.SemaphoreType.DMA((n_sems,)), pltpu.SemaphoreType.DMA((n_sems,))])(*gpacks)


def _row_tile(rows, cap=512):
    return max(t for t in range(16, min(rows, cap) + 1, 16) if rows % t == 0)


def _rs_pair_add(place, gpack, ra, *, name):
    _, R, C = gpack.shape
    Rh = R // 2
    tr = _row_tile(Rh)
    nrb = Rh // tr

    def body(p_ref, g_ref, ra_ref, pair_ref, own_ref):
        s = g_ref[...].astype(F32) + ra_ref[...].astype(F32)
        pair_ref[...] = s.astype(BF16)

        @pl.when(pl.program_id(1) == p_ref[1])
        def _():
            own_ref[...] = s

    grid_spec = pltpu.PrefetchScalarGridSpec(
        num_scalar_prefetch=1, grid=(nrb, N_CHIPS),
        in_specs=[pl.BlockSpec((None, tr, C), lambda i, j, p: (j, p[0] * nrb + i, 0)),
                  pl.BlockSpec((None, tr, C), lambda i, j, p: (j, i, 0))],
        out_specs=[pl.BlockSpec((None, tr, C), lambda i, j, p: (j, i, 0)), pl.BlockSpec((tr, C), lambda i, j, p: (i, 0))])
    return pl.pallas_call(
        body, name=name, grid_spec=grid_spec,
        out_shape=[jax.ShapeDtypeStruct((N_CHIPS, Rh, C), BF16), jax.ShapeDtypeStruct((Rh, C), F32)],
        compiler_params=pltpu.CompilerParams(dimension_semantics=("arbitrary", "arbitrary"),
                                             vmem_limit_bytes=VMEM_LIMIT_V7X))(place, gpack, ra)


SEQUENCER_EXCHANGE_IDS = {"l1": 2, "l0a": 3, "l0b": 4}


def _rs_chip_exchange_behind(pairs, *, tag, small=None):
    n = len(pairs)
    has_small = small is not None

    def body(*refs):
        p_refs = refs[:n]
        s_ref = refs[n] if has_small else None
        rb_refs = refs[n + has_small:2 * n + has_small]
        rs_ref = refs[2 * n + 1] if has_small else None
        x, y, c, chips = _place()
        peers = [(1 - x if k & 4 else x, 1 - y if k & 2 else y, 1 - c if k & 1 else c) for k in range(1, 8)]
        shake = peers if has_small else [(cx, cy, c) for cx, cy in chips]
        barrier = pltpu.get_barrier_semaphore()
        for peer in shake:
            pl.semaphore_signal(barrier, inc=1, device_id=peer, device_id_type=MESH)
        pl.semaphore_wait(barrier, len(shake))
        copy = _remote(*refs[-2:])
        cps = []
        for g in range(n):
            cps += [copy(3 * g + j, p_refs[g].at[2 * cx + cy], rb_refs[g].at[j], (cx, cy, c)) for j, (cx, cy) in enumerate(chips)]
        if has_small:
            dev = 4 * x + 2 * y + c
            cps += [copy(3 * n + k, s_ref, rs_ref.at[dev], peer) for k, peer in enumerate(peers)]
        for cp in cps:
            cp.start()
        for g in range(n):
            for j in range(3):
                copy(3 * g + j, p_refs[g].at[0], rb_refs[g].at[j], (x, y, c)).wait_recv()
        if has_small:
            for k, (px, py, pc) in enumerate(peers):
                copy(3 * n + k, s_ref, rs_ref.at[4 * px + 2 * py + pc], (x, y, c)).wait_recv()
        for cp in cps:
            cp.wait_send()

    ins = list(pairs) + ([small] if has_small else [])
    out_type = [jax.ShapeDtypeStruct((3,) + p.shape[1:], p.dtype) for p in pairs]
    if has_small:
        out_type.append(jax.ShapeDtypeStruct((8,) + small.shape, small.dtype))
    n_sems = 3 * n + 7 * has_small
    outs = pl.kernel(body, out_type=out_type, mesh=plsc.ScalarSubcoreMesh(axis_name="sequencer", num_cores=1),
                     name="rs_chip_exchange_behind_" + tag,
                     scratch_types=[pltpu.SemaphoreType.DMA((n_sems,)), pltpu.SemaphoreType.DMA((n_sems,))],
                     compiler_params=pltpu.CompilerParams(collective_id=SEQUENCER_EXCHANGE_IDS[tag]))(*ins)
    if has_small:
        dev = 4 * lax.axis_index("x") + 2 * lax.axis_index("y") + lax.axis_index("c")
        outs = list(outs[:n]) + [lax.dynamic_update_index_in_dim(outs[n], small, dev, 0)]
    return outs


def _rs_final_add(place, own, rb, *, name):
    Rh, C = own.shape
    tr = _row_tile(Rh)
    nrb = Rh // tr

    def body(p_ref, o_ref, rb_ref, f_ref):
        f_ref[...] = ((o_ref[...] + rb_ref[0].astype(F32)) + rb_ref[1].astype(F32)) + rb_ref[2].astype(F32)

    grid_spec = pltpu.PrefetchScalarGridSpec(
        num_scalar_prefetch=1, grid=(nrb,),
        in_specs=[pl.BlockSpec((tr, C), lambda i, p: (i, 0)), pl.BlockSpec((3, tr, C), lambda i, p: (0, i, 0))],
        out_specs=pl.BlockSpec((tr, C), lambda i, p: (p[0] * nrb + i, 0)))
    return pl.pallas_call(
        body, name=name, grid_spec=grid_spec, out_shape=jax.ShapeDtypeStruct((2 * Rh, C), F32),
        compiler_params=pltpu.CompilerParams(dimension_semantics=("arbitrary",), vmem_limit_bytes=VMEM_LIMIT_V7X))(place, own, rb)


def _sum_slots(rs):
    n, rows, C = rs.shape

    def body(r_ref, o_ref):
        acc = r_ref[0]
        for k in range(1, n):
            acc = acc + r_ref[k]
        o_ref[...] = acc

    return _call(body, name="small_grad_sum", grid=(1,), in_specs=[pl.BlockSpec((n, rows, C), lambda i: (0, 0, 0))],
                 out_specs=pl.BlockSpec((rows, C), lambda i: (0, 0)), out_shape=jax.ShapeDtypeStruct((rows, C), F32),
                 sem=("arbitrary",))(rs)


def _rs_sibling_share(gbufs, *, name):
    n = len(gbufs)

    def body(*refs):
        g_refs = refs[n:2 * n]
        x, y, c, _ = _place()
        copy = _remote(*refs[-2:])
        halves = [_halves(g_refs[g], gbufs[g].shape[0]) for g in range(n)]
        outs = [copy(g, halves[g]((), c), halves[g]((), c), (x, y, 1 - c)) for g in range(n)]
        for cp in outs:
            cp.start()
        for g in range(n):
            copy(g, halves[g]((), 1 - c), halves[g]((), 1 - c), (x, y, c)).wait_recv()
        for cp in outs:
            cp.wait_send()

    return pl.pallas_call(body, name=name, out_shape=[jax.ShapeDtypeStruct(a.shape, a.dtype) for a in gbufs],
                          in_specs=[ANY] * n, out_specs=[ANY] * n, input_output_aliases={g: g for g in range(n)},
                          scratch_shapes=[pltpu.SemaphoreType.DMA((n,)), pltpu.SemaphoreType.DMA((n,))])(*gbufs)


def _adamw(w, g, m, v, *, name, g_row=0):
    rows, cols = w.shape
    tr = rows
    for cand in range(min(rows, 512), 7, -8):
        if rows % cand == 0 and g_row % cand == 0:
            tr = cand
            break
    spec = pl.BlockSpec((tr, cols), lambda i: (i, 0))
    g_spec = pl.BlockSpec((tr, cols), lambda i: (g_row // tr + i, 0))

    def body(w_ref, g_ref, m_ref, v_ref, d_ref, nm_ref, nv_ref):
        gg = g_ref[...]
        nm = ADAM_B1 * m_ref[...] + (1.0 - ADAM_B1) * gg
        nv = ADAM_B2 * v_ref[...] + (1.0 - ADAM_B2) * (gg * gg)
        m_hat = nm / (1.0 - ADAM_B1 ** ADAM_STEP)
        v_hat = nv / (1.0 - ADAM_B2 ** ADAM_STEP)
        d_ref[...] = -ADAM_LR * (m_hat / (jnp.sqrt(v_hat) + ADAM_EPS) + ADAM_WD * w_ref[...])
        nm_ref[...] = nm
        nv_ref[...] = nv

    return _call(body, name=name, grid=(rows // tr,), in_specs=[spec, g_spec, spec, spec], out_specs=[spec] * 3,
                 out_shape=[jax.ShapeDtypeStruct((rows, cols), F32)] * 3, sem=("parallel",))(w, g, m, v)


WEIGHT_NAMES = ("mix_norm", "ab_w_in", "lru_conv_w", "lru_conv_b", "lru_wa", "lru_ba", "lru_wx", "lru_bx", "lru_lambda",
                "ab_w_out", "c_w_qkv", "c_b_qkv", "c_sinks", "c_w_out", "c_b_out", "xa_norm", "xa_mem_norm", "xa_wq",
                "xa_wkv", "xa_wo", "ffn_norm", "ffn_w_gate_up", "ffn_w_down", "final_norm")
EARLY_GROUPS = (("ab_w_in",),)
MID_GROUPS = (("ab_w_out",), ("lru_wa", "lru_wx"))
LATE_GROUPS = (("c_w_out", "xa_wkv", "ffn_w_down"), ("ffn_w_gate_up",), ("xa_wo",), ("xa_wq",), ("c_w_qkv",))
GROUPS = EARLY_GROUPS + MID_GROUPS + LATE_GROUPS
REPLICATED = ("mix_norm", "lru_conv_b", "lru_lambda", "c_sinks", "xa_norm", "xa_mem_norm", "ffn_norm", "final_norm")
SMALL_SHARDED = ("lru_conv_w", "lru_ba", "lru_bx", "c_b_qkv", "c_b_out")
LANES = 1024


def _rows(v):
    flat = v.reshape(-1)
    return jnp.pad(flat, (0, -flat.shape[0] % LANES)).reshape(-1, LANES)


def _pack_small(parts, total, *, name):
    def body(*refs):
        o_ref = refs[-1]
        o_ref[...] = jnp.zeros_like(o_ref)
        row = 0
        for p_ref in refs[:-1]:
            o_ref[row:row + p_ref.shape[0], :] = p_ref[...]
            row += p_ref.shape[0]

    return _call(body, name=name, grid=(1,), in_specs=[pl.BlockSpec(p.shape, lambda i: (0, 0)) for p in parts],
                 out_specs=pl.BlockSpec((total, LANES), lambda i: (0, 0)),
                 out_shape=jax.ShapeDtypeStruct((total, LANES), F32), sem=("arbitrary",))(*parts)


def _from_shards(name, t):
    minor = t.shape[-1]
    if name == "ab_w_in":
        return t
    if name in ("ab_w_out", "c_w_out"):
        return t.reshape(1, -1, minor)
    if name == "ffn_w_gate_up":
        return t.reshape(N_CHIPS, 2, -1, minor)
    if name in ("xa_wq", "xa_wkv", "ffn_w_down"):
        return t.reshape(N_CHIPS, 2, -1, minor).transpose(1, 0, 2, 3).reshape(2, -1, minor)
    if name in ("lru_wa", "lru_wx"):
        return t.reshape(N_CHIPS, LRU_HEADS, -1, minor).transpose(1, 0, 2, 3).reshape(LRU_HEADS, LRU_HEAD_DIM, minor)
    if name == "xa_wo":
        return t.reshape(N_CHIPS, 2, -1, minor).transpose(1, 0, 2, 3)
    assert name == "c_w_qkv"
    return t.transpose(1, 0, 2).reshape(1, D_MODEL, -1)


def _piece_shards(name, g):
    minor = g.shape[-1]
    if name in ("ab_w_in", "ffn_w_gate_up", "xa_wo"):
        return g
    if name in ("ab_w_out", "c_w_out", "xa_wq", "xa_wkv", "ffn_w_down"):
        return g.reshape(N_CHIPS, -1, minor)
    if name in ("lru_wa", "lru_wx"):
        return g.reshape(LRU_HEADS, N_CHIPS, -1, minor).transpose(1, 0, 2, 3).reshape(N_CHIPS, -1, minor)
    assert name == "c_w_qkv"
    return g.reshape(D_MODEL, N_CHIPS, -1).transpose(1, 0, 2)


RS_SETS = {
    "l1": ((("c_w_out", None), ("xa_wkv", 1), ("ffn_w_down", 1)), (("ffn_w_gate_up", 1),), (("xa_wo", 1),),
           (("xa_wq", 1),), (("c_w_qkv", None),)),
    "l0a": ((("xa_wkv", 0), ("ffn_w_down", 0)), (("ffn_w_gate_up", 0),), (("xa_wo", 0),), (("xa_wq", 0),)),
    "l0b": ((("ab_w_out", None),), (("ab_w_in", None),), (("lru_wa", None), ("lru_wx", None))),
}
RS_STAGE = {"layer1": "l1", "layer0_ffn_xa": "l0a"}


def kernel(x, mem, mix_norm, ab_w_in, lru_conv_w, lru_conv_b, lru_wa, lru_ba, lru_wx, lru_bx, lru_lambda, ab_w_out, c_w_qkv, c_b_qkv, c_sinks, c_w_out, c_b_out, xa_norm, xa_mem_norm, xa_wq, xa_wkv, xa_wo, ffn_norm, ffn_w_gate_up, ffn_w_down, final_norm, loss_target, m_mix_norm, m_ab_w_in, m_lru_conv_w, m_lru_conv_b, m_lru_wa, m_lru_ba, m_lru_wx, m_lru_bx, m_lru_lambda, m_ab_w_out, m_c_w_qkv, m_c_b_qkv, m_c_sinks, m_c_w_out, m_c_b_out, m_xa_norm, m_xa_mem_norm, m_xa_wq, m_xa_wkv, m_xa_wo, m_ffn_norm, m_ffn_w_gate_up, m_ffn_w_down, m_final_norm, v_mix_norm, v_ab_w_in, v_lru_conv_w, v_lru_conv_b, v_lru_wa, v_lru_ba, v_lru_wx, v_lru_bx, v_lru_lambda, v_ab_w_out, v_c_w_qkv, v_c_b_qkv, v_c_sinks, v_c_w_out, v_c_b_out, v_xa_norm, v_xa_mem_norm, v_xa_wq, v_xa_wkv, v_xa_wo, v_ffn_norm, v_ffn_w_gate_up, v_ffn_w_down, v_final_norm):
    given = dict(locals())
    wl = {n: given[n] for n in WEIGHT_NAMES}
    ml = {n: given["m_" + n] for n in WEIGHT_NAMES}
    vl = {n: given["v_" + n] for n in WEIGHT_NAMES}
    xi, yi, ci = lax.axis_index("x"), lax.axis_index("y"), lax.axis_index("c")
    chip = 2 * xi + yi

    def join(parts, axis):
        return parts[0] if len(parts) == 1 else jnp.concatenate(parts, axis=axis)

    local_rows = {n: wl[n].size // wl[n].shape[-1] for grp in GROUPS for n in grp}
    packs = [join([wl[n].astype(BF16).reshape(local_rows[n], wl[n].shape[-1]) for n in grp], 0) for grp in GROUPS]
    spack = _pack_small([_rows(wl[n]) for n in SMALL_SHARDED], 8, name="pack_small_weights")
    n_early, n_mid = len(EARLY_GROUPS), len(EARLY_GROUPS) + len(MID_GROUPS)
    early, sfull = _gather_weights(packs[:n_early], spack)
    early, sfull, mid_packs = lax.optimization_barrier((early, sfull, packs[n_early:n_mid]))
    mid = _gather_weights_behind(mid_packs, tag="mid")
    mid, late_packs = lax.optimization_barrier((mid, packs[n_mid:]))
    gathered = early + mid + _gather_weights_behind(late_packs, tag="late")
    w = {n: wl[n] for n in REPLICATED}
    w["c_sinks"] = wl["c_sinks"][0]
    for grp, full in zip(GROUPS, gathered):
        off = 0
        for n in grp:
            w[n] = _from_shards(n, full if len(grp) == 1 else full[:, off:off + local_rows[n]])
            off += local_rows[n]
    for r, n in enumerate(SMALL_SHARDED):
        loc = wl[n].shape[1:]
        t = sfull[:, r, :wl[n].size].reshape((N_CHIPS,) + loc)
        if n == "lru_conv_w":
            w[n] = t.transpose(1, 0, 2).reshape(CONV_WIDTH, -1)
        elif n in ("lru_ba", "lru_bx"):
            w[n] = t.transpose(1, 0, 2).reshape(1, -1)
        else:
            w[n] = t.reshape(1, -1)

    place = jnp.stack([ci, chip]).astype(jnp.int32)

    def pair_stage(spec, g, tag):
        piece = lambda n, l: (g[n] if l is None else g[n, l]).astype(BF16)
        gpacks = [join([_piece_shards(n, piece(n, l)) for n, l in grp], 1) for grp in spec]
        ras = _rs_pair_exchange(gpacks, name=f"rs_pair_exchange_{tag}")
        sums = [_rs_pair_add(place, gp, ra, name=f"rs_pair_add_{tag}_{i}") for i, (gp, ra) in enumerate(zip(gpacks, ras))]
        return [pair for pair, _ in sums], [own for _, own in sums]

    reduced, in_flight = [], []

    def take_up():
        done = [_rs_final_add(place, o, r, name=f"rs_final_add_{len(reduced) + i}") for i, (o, r) in enumerate(in_flight)]
        reduced.extend(done)
        in_flight.clear()
        return done

    def reduce_behind(stage, g):
        done = take_up()
        tag = RS_STAGE[stage]
        pairs, own = pair_stage(RS_SETS[tag], g, tag)
        in_flight.extend(zip(own, _rs_chip_exchange_behind(pairs, tag=tag)))
        return own + done

    loss_part, grad_x, g = _device_step(x[0], mem[0], loss_target[0], w, on_grads=reduce_behind)

    small_parts = [_rows(g[n]) for n in REPLICATED] + [_rows(jnp.broadcast_to(loss_part, (LANES,)))]
    small_parts += [_rows(g[n]) for n in SMALL_SHARDED]
    small = _pack_small(small_parts, 24, name="pack_small_grads")
    take_up()
    pairs, own = pair_stage(RS_SETS["l0b"], g, "l0b")
    *rb, rs = _rs_chip_exchange_behind(pairs, tag="l0b", small=small)
    gsums = list(_rs_sibling_share(list(reduced), name="rs_sibling_share_behind"))
    in_flight.extend(zip(own, rb))
    gsums += list(_rs_sibling_share(take_up(), name="rs_sibling_share_last"))
    ssum = _sum_slots(rs)

    where = {}
    for grp, gsum in zip(RS_SETS["l1"] + RS_SETS["l0a"] + RS_SETS["l0b"], gsums):
        off = 0
        for n, l in grp:
            rows = local_rows[n] if l is None else local_rows[n] // 2
            where[n, l] = (gsum, off, rows, len(grp) == 1)
            off += rows
    take = lambda gsum, off, rows, whole: gsum if whole else gsum[off:off + rows]
    grads, grad_rows = {}, {}
    for grp in LATE_GROUPS + EARLY_GROUPS + MID_GROUPS:
        for n in grp:
            if (n, None) in where:
                grads[n] = take(*where[n, None]).reshape(wl[n].shape)
                grad_rows[n] = where[n, None][:2]
            else:
                grads[n] = jnp.stack([take(*where[n, l]).reshape(wl[n].shape[1:]) for l in range(2)])
                grad_rows[n] = (grads[n].reshape(local_rows[n], wl[n].shape[-1]), 0)
    row = 0
    for n in REPLICATED:
        k = _rows(g[n]).shape[0]
        grads[n] = ssum[row:row + k].reshape(-1)[:wl[n].size].reshape(wl[n].shape)
        row += k
    loss = ssum[row, 0]
    row += 1
    for n in SMALL_SHARDED:
        k = _rows(g[n]).shape[0]
        full = ssum[row:row + k].reshape(-1)[:g[n].size]
        row += k
        loc = wl[n].shape
        if n == "lru_conv_w":
            sh = full.reshape(CONV_WIDTH, N_CHIPS, -1)
        elif n in ("lru_ba", "lru_bx"):
            sh = full.reshape(LRU_HEADS, N_CHIPS, -1)
        else:
            sh = full.reshape(1, N_CHIPS, -1)
        grads[n] = lax.dynamic_index_in_dim(sh, chip, axis=1, keepdims=False).reshape(loc)

    delta, new_m, new_v = {}, {}, {}
    for n, (gsum, off) in grad_rows.items():
        shape2 = (local_rows[n], wl[n].shape[-1])
        d, nm, nv = _adamw(wl[n].reshape(shape2), gsum, ml[n].reshape(shape2), vl[n].reshape(shape2), g_row=off,
                           name="adamw_" + n)
        delta[n], new_m[n], new_v[n] = (t.reshape(wl[n].shape) for t in (d, nm, nv))
    smalls = REPLICATED + SMALL_SHARDED
    packs = [_pack_small([_rows(src[n]) for n in smalls], 24, name="pack_adamw_" + tag)
             for tag, src in (("w", wl), ("g", grads), ("m", ml), ("v", vl))]
    outs = _adamw(*packs, name="adamw_small")
    row = 0
    for n in smalls:
        k = _rows(wl[n]).shape[0]
        for dst, o in zip((delta, new_m, new_v), outs):
            dst[n] = o[row:row + k].reshape(-1)[:wl[n].size].reshape(wl[n].shape)
        row += k

    return (loss, grad_x[None], *[grads[n] for n in WEIGHT_NAMES], *[delta[n] for n in WEIGHT_NAMES],
            *[new_m[n] for n in WEIGHT_NAMES], *[new_v[n] for n in WEIGHT_NAMES])
```

```python
import jax
import jax.numpy as jnp
from jax import lax
from jax.experimental import pallas as pl
from jax.experimental.pallas import tpu as pltpu
from jax.experimental.pallas import tpu_sc as plsc

F32, BF16 = jnp.float32, jnp.bfloat16
D_MODEL = 1024
NORM_EPS = 1e-6
ROPE_THETA = 500000.0
HEAD_DIM = 64
ROT_DIM = 16
BLK = 128
LRU_HEADS, LRU_HEAD_DIM, CONV_WIDTH, LRU_C = 4, 256, 4, 8.0
DILATED_PATTERN = ((128, 1), (512, 4), (2048, 16))
B_HEADS, C_HEADS, C_KV_HEADS, C_WINDOW = 8, 16, 2, 128
XA_HEADS, XA_HEAD_DIM, N_MEM = 4, 128, 256
D_FF = 2816
NEG = -1e30
ADAM_LR, ADAM_B1, ADAM_B2, ADAM_EPS, ADAM_WD, ADAM_STEP = 0.001, 0.9, 0.999, 1e-08, 0.01, 10
N_CHIPS = 4
VMEM_LIMIT_V7X = 56 * 1024 * 1024

NN = (((1,), (0,)), ((), ()))
NT = (((1,), (1,)), ((), ()))
TN = (((0,), (0,)), ((), ()))


def _dot(a, b, dims=NN):
    return lax.dot_general(a, b, dims, preferred_element_type=F32)


def _sigmoid(x):
    return 0.5 * jnp.tanh(0.5 * x) + 0.5


def _call(body, *, name, grid, in_specs, out_specs, out_shape, scratch=(), sem=None):
    return pl.pallas_call(
        body, name=name, grid=grid, in_specs=in_specs, out_specs=out_specs, out_shape=out_shape,
        scratch_shapes=list(scratch),
        compiler_params=pltpu.CompilerParams(dimension_semantics=sem, vmem_limit_bytes=VMEM_LIMIT_V7X))


def _rope_tables(L):
    half = ROT_DIM // 2
    inv = ROPE_THETA ** (-jnp.arange(0, ROT_DIM, 2, dtype=F32) / ROT_DIM)
    j = jnp.arange(2 * HEAD_DIM) % HEAD_DIM
    ang = jnp.arange(L, dtype=F32)[:, None] * inv[j % half][None, :]
    cos, sin = jnp.cos(ang), jnp.sin(ang)
    c = jnp.where(j < ROT_DIM, cos, 1.0)
    s1 = jnp.where(j < half, -sin, 0.0)
    s2 = jnp.where((j >= half) & (j < ROT_DIM), sin, 0.0)
    return c, s1, s2


def _rope_fwd(v, c, s1, s2):
    return v * c + pltpu.roll(v, 120, 1) * s1 + pltpu.roll(v, 8, 1) * s2


def _rope_bwd(dv, c, s1, s2):
    return dv * c + pltpu.roll(dv * s1, 8, 1) + pltpu.roll(dv * s2, 120, 1)


def _weight_spec(w, layer):
    once = pl.Buffered(1)
    if layer is None:
        return w.shape, pl.BlockSpec(w.shape, lambda i: (0, 0, 0), pipeline_mode=once)
    S, _, K, Ns = w.shape
    return (S, K, Ns), pl.BlockSpec((S, None, K, Ns), lambda i: (0, layer, 0, 0), pipeline_mode=once)


def _rowmm(a, w3, *, name, tm=512, gain=None, bias=None, res=None, swiglu=False, rope=None, layer=None):
    M, K = a.shape
    (S, _, Ns), w_spec = _weight_spec(w3, layer)
    N = S * Ns
    tm = min(tm, M)
    has_norm, has_bias, has_res, has_rope = gain is not None, bias is not None, res is not None, rope is not None
    row = lambda w: pl.BlockSpec((tm, w), lambda i: (i, 0))
    whole = lambda shape: pl.BlockSpec(shape, lambda i: (0,) * len(shape))
    ins, specs = [a], [row(K)]
    if has_norm:
        ins.append(gain.reshape(1, K)); specs.append(whole((1, K)))
    ins.append(w3); specs.append(w_spec)
    if has_bias:
        ins.append(bias.reshape(1, N)); specs.append(whole((1, N)))
    if has_res:
        ins.append(res); specs.append(row(N))
    if has_rope:
        ins += list(rope[2]); specs += [row(128)] * 3
    y_dtype = F32 if has_res else BF16
    out_shape, out_specs = [jax.ShapeDtypeStruct((M, N), y_dtype)], [row(N)]
    if has_norm:
        out_shape.append(jax.ShapeDtypeStruct((M, K), BF16)); out_specs.append(row(K))
    if swiglu:
        out_shape.append(jax.ShapeDtypeStruct((M, N // 2), BF16)); out_specs.append(row(N // 2))
    scratch = [pltpu.VMEM((tm, N), F32)] if has_rope else []

    def body(*refs):
        it = iter(refs)
        a_ref = next(it)
        g_ref = next(it) if has_norm else None
        w_ref = next(it)
        b_ref = next(it) if has_bias else None
        r_ref = next(it) if has_res else None
        tabs = [next(it) for _ in range(3)] if has_rope else None
        y_ref = next(it)
        n_ref = next(it) if has_norm else None
        act_ref = next(it) if swiglu else None
        ys_ref = next(it) if has_rope else None
        if has_norm:
            x = a_ref[...].astype(F32)
            ms = jnp.mean(x * x, axis=-1, keepdims=True)
            xb = (x * lax.rsqrt(ms + NORM_EPS) * g_ref[...]).astype(BF16)
            n_ref[...] = xb
        else:
            xb = a_ref[...].astype(BF16)
        if swiglu:
            for s in range(S // 2):
                g = _dot(xb, w_ref[s])
                u = _dot(xb, w_ref[s + S // 2])
                y_ref[:, s * Ns:(s + 1) * Ns] = g.astype(BF16)
                y_ref[:, N // 2 + s * Ns:N // 2 + (s + 1) * Ns] = u.astype(BF16)
                act_ref[:, s * Ns:(s + 1) * Ns] = (g * _sigmoid(g) * u).astype(BF16)
            return
        for s in range(S):
            sl = slice(s * Ns, (s + 1) * Ns)
            acc = _dot(xb, w_ref[s])
            if has_bias:
                acc = acc + b_ref[:, sl]
            if has_res:
                acc = acc + r_ref[:, sl]
            if has_rope:
                ys_ref[:, sl] = acc
            else:
                y_ref[:, sl] = acc.astype(y_dtype)
        if has_rope:
            c, s1, s2 = (t[...] for t in tabs)
            for cb in range(N // 128):
                cs = slice(cb * 128, (cb + 1) * 128)
                v = ys_ref[:, cs]
                if rope[0] <= cb * 128 < rope[1]:
                    v = _rope_fwd(v, c, s1, s2)
                y_ref[:, cs] = v.astype(BF16)

    return _call(body, name=name, grid=(M // tm,), in_specs=specs, out_specs=out_specs, out_shape=out_shape,
                 scratch=scratch, sem=("parallel",))(*ins)


def _mm_nt(dy, w3, *, name, mode, tm=512, kchunk=None, h=None, gain=None, dh=None, gu=None, layer=None, after=None):
    M, N = dy.shape
    (S, K, Ns), w_spec = _weight_spec(w3, layer)
    kchunk = kchunk or K
    tm = min(tm, M)
    row = lambda w: pl.BlockSpec((tm, w), lambda i: (i, 0))
    whole = lambda shape: pl.BlockSpec(shape, lambda i: (0,) * len(shape))
    ins, specs = [dy, w3], [row(N), w_spec]
    after = list(after or ())
    ins = after + ins
    specs = [pl.BlockSpec((8, a.shape[1]), lambda i: (0, 0)) for a in after] + specs
    has_dh = dh is not None
    if mode == "norm":
        ins += [h, gain.reshape(1, K)]; specs += [row(K), whole((1, K))]
        if has_dh:
            ins.append(dh); specs.append(row(K))
        out_shape = [jax.ShapeDtypeStruct((M, K), F32), jax.ShapeDtypeStruct((1, K), F32)]
        out_specs = [row(K), whole((1, K))]
    elif mode == "swiglu":
        ins.append(gu); specs.append(row(2 * K))
        out_shape, out_specs = [jax.ShapeDtypeStruct((M, 2 * K), BF16)], [row(2 * K)]
    else:
        out_shape, out_specs = [jax.ShapeDtypeStruct((M, K), BF16)], [row(K)]

    def body(*refs):
        it = iter(refs[len(after):])
        dy_ref, w_ref = next(it), next(it)
        if mode == "norm":
            h_ref, g_ref = next(it), next(it)
            dh_ref = next(it) if has_dh else None
            o_ref, dg_ref = next(it), next(it)
        elif mode == "swiglu":
            gu_ref, o_ref = next(it), next(it)
        else:
            o_ref = next(it)
        for kc in range(K // kchunk):
            ks = slice(kc * kchunk, (kc + 1) * kchunk)
            acc = None
            for s in range(S):
                t = _dot(dy_ref[:, s * Ns:(s + 1) * Ns].astype(BF16), w_ref[s, ks, :], NT)
                acc = t if acc is None else acc + t
            if mode == "plain":
                o_ref[:, ks] = acc.astype(BF16)
            elif mode == "swiglu":
                us = slice(K + kc * kchunk, K + (kc + 1) * kchunk)
                g = gu_ref[:, ks].astype(F32)
                u = gu_ref[:, us].astype(F32)
                sg = _sigmoid(g)
                o_ref[:, ks] = (acc * u * (sg * (1.0 + g * (1.0 - sg)))).astype(BF16)
                o_ref[:, us] = (acc * (g * sg)).astype(BF16)
            else:
                x = h_ref[...].astype(F32)
                r = lax.rsqrt(jnp.mean(x * x, axis=-1, keepdims=True) + NORM_EPS)
                xhat = x * r
                dxh = acc * g_ref[...]
                dx = r * (dxh - xhat * jnp.mean(dxh * xhat, axis=-1, keepdims=True))
                o_ref[...] = dx + dh_ref[...] if has_dh else dx

                @pl.when(pl.program_id(0) == 0)
                def _():
                    dg_ref[...] = jnp.zeros_like(dg_ref)

                dg_ref[...] += jnp.sum(acc * xhat, axis=0, keepdims=True)

    sem = ("arbitrary",) if mode == "norm" else ("parallel",)
    return _call(body, name=name, grid=(M // tm,), in_specs=specs, out_specs=out_specs, out_shape=out_shape, sem=sem)(*ins)


def _mm_tn(x, dy, *, S, name, tk=2048, kk=None, bias=False):
    M, K = x.shape
    N = dy.shape[1]
    Ns = N // S
    kk = kk or K
    tk = min(tk, M)
    nl = M // tk
    in_specs = [pl.BlockSpec((tk, kk), lambda s, kc, l: (l, kc)), pl.BlockSpec((tk, Ns), lambda s, kc, l: (l, s))]
    out_shape = [jax.ShapeDtypeStruct((S, K, Ns), BF16)]
    out_specs = [pl.BlockSpec((None, kk, Ns), lambda s, kc, l: (s, kc, 0))]
    if bias:
        out_shape.append(jax.ShapeDtypeStruct((1, N), F32))
        out_specs.append(pl.BlockSpec((1, Ns), lambda s, kc, l: (0, s)))

    def body(x_ref, dy_ref, o_ref, *rest):
        acc_ref = rest[-1]
        kc, l = pl.program_id(1), pl.program_id(2)

        @pl.when(l == 0)
        def _():
            acc_ref[...] = jnp.zeros_like(acc_ref)

        acc_ref[...] += _dot(x_ref[...].astype(BF16), dy_ref[...].astype(BF16), TN)
        if bias:
            b_ref = rest[0]

            @pl.when((kc == 0) & (l == 0))
            def _():
                b_ref[...] = jnp.zeros_like(b_ref)

            @pl.when(kc == 0)
            def _():
                b_ref[...] += jnp.sum(dy_ref[...].astype(F32), axis=0, keepdims=True)

        @pl.when(l == nl - 1)
        def _():
            o_ref[...] = acc_ref[...].astype(BF16)

    return _call(body, name=name, grid=(S, K // kk, nl), in_specs=in_specs, out_specs=out_specs, out_shape=out_shape,
                 scratch=[pltpu.VMEM((kk, Ns), F32)], sem=("arbitrary", "arbitrary", "arbitrary"))(x, dy)


def _band_bias(max_dist, has_prev):
    rows = lax.broadcasted_iota(jnp.int32, (BLK, 2 * BLK), 0)
    cols = lax.broadcasted_iota(jnp.int32, (BLK, 2 * BLK), 1)
    dist = rows - cols + BLK
    ok = (dist >= 0) & (dist <= max_dist) & ((cols >= BLK) | has_prev)
    return jnp.where(ok, 0.0, NEG)


Q_SCALE = HEAD_DIM ** -0.5


def _band_fwd(qa, ka, va, *, d, nq, nkv, qcol, kcol, vcol, max_dist, sinks=None, name):
    Lr = qa.shape[0]
    nb = Lr // BLK
    qw, kw, G = nq * HEAD_DIM, nkv * HEAD_DIM, nq // nkv
    cur = lambda colf, w: pl.BlockSpec((BLK, w), lambda r, i: (i, colf(r)))
    prv = lambda colf, w: pl.BlockSpec((BLK, w), lambda r, i: (jnp.maximum(i - 1, 0), colf(r)))
    out = pl.BlockSpec((BLK, qw), lambda r, i: (i, r))
    ins, specs = [qa, ka, ka, va, va], [cur(qcol, qw), cur(kcol, kw), prv(kcol, kw), cur(vcol, kw), prv(vcol, kw)]
    has_sinks = sinks is not None
    if has_sinks:
        ins.append(sinks); specs.append(pl.BlockSpec(memory_space=pltpu.SMEM))

    def body(*refs):
        q_ref, kc_ref, kp_ref, vc_ref, vp_ref = refs[:5]
        sk_ref = refs[5] if has_sinks else None
        o_ref, lse_ref = refs[-2], refs[-1]
        bias = _band_bias(max_dist, pl.program_id(1) > 0)
        k2 = jnp.concatenate([kp_ref[...], kc_ref[...]], axis=0)
        v2 = jnp.concatenate([vp_ref[...], vc_ref[...]], axis=0)
        for h in range(nq):
            hs = slice(h * HEAD_DIM, (h + 1) * HEAD_DIM)
            ks = slice((h // G) * HEAD_DIM, (h // G + 1) * HEAD_DIM)
            s = _dot(q_ref[:, hs] * jnp.asarray(Q_SCALE, BF16), k2[:, ks], NT) + bias
            m = jnp.max(s, axis=-1, keepdims=True)
            if has_sinks:
                m = jnp.maximum(m, sk_ref[h])
            p = jnp.exp(s - m)
            l = jnp.sum(p, axis=-1, keepdims=True)
            if has_sinks:
                l = l + jnp.exp(sk_ref[h] - m)
            o_ref[:, hs] = (_dot(p.astype(BF16), v2[:, ks]) / l).astype(BF16)
            lse_ref[:, hs] = jnp.broadcast_to(m + jnp.log(l), (BLK, HEAD_DIM))

    return _call(body, name=name, grid=(d, nb), in_specs=specs, out_specs=[out, out],
                 out_shape=[jax.ShapeDtypeStruct((Lr, d * qw), BF16), jax.ShapeDtypeStruct((Lr, d * qw), F32)],
                 sem=("parallel", "parallel"))(*ins)


def _band_bwd(qa, ka, va, doa, oa, lsea, *, d, nq, nkv, qcol, kcol, vcol, docol, max_dist, sinks=None, name):
    Lr = qa.shape[0]
    nb = Lr // BLK
    qw, kw, G = nq * HEAD_DIM, nkv * HEAD_DIM, nq // nkv
    transposed = G > 1
    last = lambda i: jnp.minimum(i, nb - 1)
    cur = lambda colf, w: pl.BlockSpec((BLK, w), lambda r, i: (last(i), colf(r)))
    prv = lambda colf, w: pl.BlockSpec((BLK, w), lambda r, i: (jnp.maximum(last(i) - 1, 0), colf(r)))
    own = lambda r: r
    ins = [qa, ka, ka, va, va, doa, oa, lsea]
    specs = [cur(qcol, qw), cur(kcol, kw), prv(kcol, kw), cur(vcol, kw), prv(vcol, kw), cur(docol, qw), cur(own, qw),
             cur(own, qw)]
    has_sinks = sinks is not None
    if has_sinks:
        ins.append(sinks); specs.append(pl.BlockSpec(memory_space=pltpu.SMEM))
    out_shape = [jax.ShapeDtypeStruct((Lr, d * qw), BF16), jax.ShapeDtypeStruct((Lr, d * kw), BF16),
                 jax.ShapeDtypeStruct((Lr, d * kw), BF16)]
    behind = lambda r, i: (jnp.maximum(i - 1, 0), r)
    out_specs = [pl.BlockSpec((BLK, qw), lambda r, i: (last(i), r)), pl.BlockSpec((BLK, kw), behind),
                 pl.BlockSpec((BLK, kw), behind)]
    if has_sinks:
        out_shape.append(jax.ShapeDtypeStruct((8, 128), F32))
        out_specs.append(pl.BlockSpec((8, 128), lambda r, i: (0, 0)))

    def body(*refs):
        it = iter(refs)
        q_ref, kc_ref, kp_ref, vc_ref, vp_ref, do_ref, o_ref, ls_ref = (next(it) for _ in range(8))
        sk_ref = next(it) if has_sinks else None
        dq_ref, dk_ref, dv_ref = next(it), next(it), next(it)
        dsk_ref = next(it) if has_sinks else None
        dk_car, dv_car = next(it), next(it)
        r_id, i = pl.program_id(0), pl.program_id(1)

        @pl.when(i == 0)
        def _():
            dk_car[...] = jnp.zeros_like(dk_car)
            dv_car[...] = jnp.zeros_like(dv_car)

        if has_sinks:
            @pl.when((r_id == 0) & (i == 0))
            def _():
                dsk_ref[...] = jnp.zeros_like(dsk_ref)

        @pl.when(i == nb)
        def _():
            dk_ref[...] = dk_car[...].astype(BF16)
            dv_ref[...] = dv_car[...].astype(BF16)

        @pl.when(i < nb)
        def _():
            bias = _band_bias(max_dist, i > 0)
            k2 = jnp.concatenate([kp_ref[...], kc_ref[...]], axis=0)
            v2 = jnp.concatenate([vp_ref[...], vc_ref[...]], axis=0)
            if has_sinks:
                lane = lax.broadcasted_iota(jnp.int32, (8, 128), 1)
                dsk = jnp.zeros((8, 128), F32)
            for kv in range(nkv):
                ks = slice(kv * HEAD_DIM, (kv + 1) * HEAD_DIM)
                kh, vh = k2[:, ks], v2[:, ks]
                shape = (HEAD_DIM, 2 * BLK) if transposed else (2 * BLK, HEAD_DIM)
                dk, dv = jnp.zeros(shape, F32), jnp.zeros(shape, F32)
                for g in range(G):
                    h = kv * G + g
                    hs = slice(h * HEAD_DIM, (h + 1) * HEAD_DIM)
                    q = q_ref[:, hs] * jnp.asarray(Q_SCALE, BF16)
                    do = do_ref[:, hs]
                    lse = ls_ref[:, h * HEAD_DIM:h * HEAD_DIM + 1]
                    dl = jnp.sum(do.astype(F32) * o_ref[:, hs].astype(F32), axis=-1, keepdims=True)
                    p = jnp.exp(_dot(q, kh, NT) + bias - lse)
                    ds = (p * (_dot(do, vh, NT) - dl)).astype(BF16)
                    dq_ref[:, hs] = (_dot(ds, kh) * Q_SCALE).astype(BF16)
                    if transposed:
                        dk = dk + _dot(q, ds, TN)
                        dv = dv + _dot(do, p.astype(BF16), TN)
                    else:
                        dk = dk + _dot(ds, q, TN)
                        dv = dv + _dot(p.astype(BF16), do, TN)
                    if has_sinks:
                        val = -jnp.sum(jnp.exp(sk_ref[h] - lse) * dl, axis=0, keepdims=True)
                        dsk = dsk + jnp.where(lane == h, val, 0.0)
                if transposed:
                    dk, dv = dk.T, dv.T
                dk_ref[:, ks] = (dk_car[:, ks] + dk[:BLK]).astype(BF16)
                dv_ref[:, ks] = (dv_car[:, ks] + dv[:BLK]).astype(BF16)
                dk_car[:, ks] = dk[BLK:]
                dv_car[:, ks] = dv[BLK:]
            if has_sinks:
                dsk_ref[...] += dsk

    return _call(body, name=name, grid=(d, nb + 1), in_specs=specs, out_specs=out_specs, out_shape=out_shape,
                 scratch=[pltpu.VMEM((BLK, kw), F32), pltpu.VMEM((BLK, kw), F32)], sem=("arbitrary", "arbitrary"))(*ins)


def _attn_grad_combine(branches, tabs, *, name, tm=256):
    L, qw = branches[0][0].shape
    kw = branches[0][1].shape[1]
    nbr = len(branches)
    row = lambda w: pl.BlockSpec((tm, w), lambda i: (i, 0))
    ins, specs = [], []
    for dq, dk, dv in branches:
        ins += [dq, dk, dv]; specs += [row(qw), row(kw), row(kw)]
    ins += list(tabs); specs += [row(128)] * 3

    def body(*refs):
        c, s1, s2 = (t[...] for t in refs[3 * nbr:3 * nbr + 3])
        o_ref = refs[-1]
        for part, (w, off, rot) in enumerate(((qw, 0, True), (kw, qw, True), (kw, qw + kw, False))):
            for cb in range(w // 128):
                cs = slice(cb * 128, (cb + 1) * 128)
                v = refs[part][:, cs].astype(F32)
                for b in range(1, nbr):
                    v = v + refs[3 * b + part][:, cs].astype(F32)
                if rot:
                    v = _rope_bwd(v, c, s1, s2)
                o_ref[:, off + cb * 128:off + (cb + 1) * 128] = v.astype(BF16)

    return _call(body, name=name, grid=(L // tm,), in_specs=specs, out_specs=row(qw + 2 * kw),
                 out_shape=jax.ShapeDtypeStruct((L, qw + 2 * kw), BF16), sem=("parallel",))(*ins)


def _xattn_fwd(q, kv, *, name, tq=512):
    L, W = q.shape
    scale = XA_HEAD_DIM ** -0.5
    row = pl.BlockSpec((tq, W), lambda i: (i, 0))
    kvs = pl.BlockSpec((N_MEM, 2 * W), lambda i: (0, 0))

    def body(q_ref, kv_ref, o_ref, lse_ref):
        for h in range(XA_HEADS):
            hs = slice(h * XA_HEAD_DIM, (h + 1) * XA_HEAD_DIM)
            vs = slice(W + h * XA_HEAD_DIM, W + (h + 1) * XA_HEAD_DIM)
            s = _dot(q_ref[:, hs], kv_ref[:, hs], NT) * scale
            m = jnp.max(s, axis=-1, keepdims=True)
            p = jnp.exp(s - m)
            l = jnp.sum(p, axis=-1, keepdims=True)
            o_ref[:, hs] = (_dot(p.astype(BF16), kv_ref[:, vs]) / l).astype(BF16)
            lse_ref[:, hs] = jnp.broadcast_to(m + jnp.log(l), (tq, XA_HEAD_DIM))

    return _call(body, name=name, grid=(L // tq,), in_specs=[row, kvs], out_specs=[row, row],
                 out_shape=[jax.ShapeDtypeStruct((L, W), BF16), jax.ShapeDtypeStruct((L, W), F32)], sem=("parallel",))(q, kv)


def _xattn_bwd(q, kv, o, lse, do, *, name, tq=512):
    L, W = q.shape
    scale = XA_HEAD_DIM ** -0.5
    row = pl.BlockSpec((tq, W), lambda i: (i, 0))
    kvs = pl.BlockSpec((N_MEM, 2 * W), lambda i: (0, 0))

    def body(q_ref, kv_ref, o_ref, lse_ref, do_ref, dq_ref, dkv_ref):
        @pl.when(pl.program_id(0) == 0)
        def _():
            dkv_ref[...] = jnp.zeros_like(dkv_ref)

        for h in range(XA_HEADS):
            hs = slice(h * XA_HEAD_DIM, (h + 1) * XA_HEAD_DIM)
            vs = slice(W + h * XA_HEAD_DIM, W + (h + 1) * XA_HEAD_DIM)
            qh, kh, vh, doh = q_ref[:, hs], kv_ref[:, hs], kv_ref[:, vs], do_ref[:, hs]
            p = jnp.exp(_dot(qh, kh, NT) * scale - lse_ref[:, h * XA_HEAD_DIM:h * XA_HEAD_DIM + 1])
            dl = jnp.sum(doh.astype(F32) * o_ref[:, hs].astype(F32), axis=-1, keepdims=True)
            ds = (p * (_dot(doh, vh, NT) - dl) * scale).astype(BF16)
            dq_ref[:, hs] = _dot(ds, kh).astype(BF16)
            dkv_ref[:, hs] += _dot(ds, qh, TN)
            dkv_ref[:, vs] += _dot(p.astype(BF16), doh, TN)

    return _call(body, name=name, grid=(L // tq,), in_specs=[row, kvs, row, row, row], out_specs=[row, kvs],
                 out_shape=[jax.ShapeDtypeStruct((L, W), BF16), jax.ShapeDtypeStruct((N_MEM, 2 * W), F32)],
                 sem=("arbitrary",))(q, kv, o, lse, do)


def _neg_expm1(z):
    series = -(z * (1.0 + z * (0.5 + z * (1.0 / 6.0 + z * (1.0 / 24.0 + z * (1.0 / 120.0))))))
    return jnp.where(z > -0.05, series, 1.0 - jnp.exp(z))


def _softplus(z):
    return jnp.maximum(z, 0.0) + jnp.log(1.0 + jnp.exp(-jnp.abs(z)))


def _gelu_parts(y):
    c = 0.7978845608028654
    t = jnp.tanh(c * (y + 0.044715 * y * y * y))
    gy = 0.5 * y * (1.0 + t)
    dgy = 0.5 * (1.0 + t) + 0.5 * y * (1.0 - t * t) * c * (1.0 + 3.0 * 0.044715 * y * y)
    return gy, dgy


def _lru_gates(xc, wa_ref, ba, wx_ref, bx, sp):
    rs, igs = [], []
    for hd in range(LRU_HEADS):
        sl = slice(hd * LRU_HEAD_DIM, (hd + 1) * LRU_HEAD_DIM)
        xh = xc[:, sl].astype(BF16)
        rs.append(_sigmoid(_dot(xh, wa_ref[hd]) + ba[:, sl]))
        igs.append(_sigmoid(_dot(xh, wx_ref[hd]) + bx[:, sl]))
    r, ig = jnp.concatenate(rs, axis=1), jnp.concatenate(igs, axis=1)
    la = -LRU_C * r * sp
    return r, ig, jnp.exp(la), _neg_expm1(2.0 * la)


def _conv_taps(x_ext, halo):
    n = x_ext.shape[0]
    return [x_ext[halo:] if k == CONV_WIDTH - 1 else pltpu.roll(x_ext, CONV_WIDTH - 1 - k, 0)[halo:]
            for k in range(CONV_WIDTH)]


def _lru_fwd(proj, cw, cb, wa, ba, wx, bx, lam, *, name, tc=512):
    L = proj.shape[0]
    W = LRU_HEADS * LRU_HEAD_DIM
    nb = L // tc
    whole = lambda shape: pl.BlockSpec(shape, lambda i: (0,) * len(shape))
    specs = [pl.BlockSpec((tc, W), lambda i: (i, 0)), pl.BlockSpec((tc, W), lambda i: (i, 1)),
             pl.BlockSpec((16, W), lambda i: (jnp.maximum(i * (tc // 16) - 1, 0), 0)),
             whole((CONV_WIDTH, W)), whole((1, W)), whole((LRU_HEADS, LRU_HEAD_DIM, LRU_HEAD_DIM)), whole((1, W)),
             whole((LRU_HEADS, LRU_HEAD_DIM, LRU_HEAD_DIM)), whole((1, W)), whole((1, W))]
    out_specs = [pl.BlockSpec((tc, W), lambda i: (i, 0))] * 2
    out_shape = [jax.ShapeDtypeStruct((L, W), BF16), jax.ShapeDtypeStruct((L, W), F32)]

    def body(x_ref, y_ref, xh_ref, cw_ref, cb_ref, wa_ref, ba_ref, wx_ref, bx_ref, lam_ref, rec_ref, hs_ref,
             hcar, a_scr, b_scr):
        i = pl.program_id(0)

        @pl.when(i == 0)
        def _():
            hcar[...] = jnp.zeros_like(hcar)

        halo = jnp.where(i > 0, xh_ref[...].astype(F32), 0.0)
        taps = _conv_taps(jnp.concatenate([halo, x_ref[...].astype(F32)], axis=0), 16)
        xc = cb_ref[...] + sum(cw_ref[k:k + 1, :] * taps[k] for k in range(CONV_WIDTH))
        _, ig, a, om = _lru_gates(xc, wa_ref, ba_ref[...], wx_ref, bx_ref[...], _softplus(-lam_ref[...]))
        b = jnp.sqrt(om) * (ig * xc)
        rowmod = lax.broadcasted_iota(jnp.int32, (tc, W), 0) & 7
        for s in (1, 2, 4):
            keep = rowmod >= s
            b = jnp.where(keep, a * pltpu.roll(b, s, 0) + b, b)
            a = jnp.where(keep, a * pltpu.roll(a, s, 0), a)
        a_scr[...] = a
        b_scr[...] = b

        def tile(j, hc):
            rows = pl.ds(pl.multiple_of(j * 8, 8), 8)
            ht = a_scr[rows, :] * hc + b_scr[rows, :]
            hs_ref[rows, :] = ht
            return jnp.broadcast_to(ht[7:8, :], (8, W))

        hcar[...] = lax.fori_loop(0, tc // 8, tile, hcar[...])
        gy, _ = _gelu_parts(y_ref[...].astype(F32))
        rec_ref[...] = (hs_ref[...] * gy).astype(BF16)

    return _call(body, name=name, grid=(nb,), in_specs=specs, out_specs=out_specs, out_shape=out_shape,
                 scratch=[pltpu.VMEM((8, W), F32), pltpu.VMEM((tc, W), F32), pltpu.VMEM((tc, W), F32)],
                 sem=("arbitrary",))(proj, proj, proj, cw, cb, wa, ba, wx, bx, lam)


def _lru_bwd(proj, hs, drec_src, cw, cb, wa, ba, wx, bx, lam, *, name, tc=256):
    L = proj.shape[0]
    W = LRU_HEADS * LRU_HEAD_DIM
    nb = L // tc
    tb = lambda i: nb - 1 - i
    whole = lambda shape: pl.BlockSpec(shape, lambda i: (0,) * len(shape))
    gate_w = (LRU_HEADS, LRU_HEAD_DIM, LRU_HEAD_DIM)
    specs = [pl.BlockSpec((tc, W), lambda i: (tb(i), 0)), pl.BlockSpec((tc, W), lambda i: (tb(i), 1)),
             pl.BlockSpec((16, W), lambda i: (jnp.maximum(tb(i) * (tc // 16) - 1, 0), 0)),
             pl.BlockSpec((tc, W), lambda i: (tb(i), 0)),
             pl.BlockSpec((8, W), lambda i: (jnp.maximum(tb(i) * (tc // 8) - 1, 0), 0)),
             pl.BlockSpec((tc, W), lambda i: (tb(i), 0)),
             whole((CONV_WIDTH, W)), whole((1, W)), whole(gate_w), whole((1, W)), whole(gate_w), whole((1, W)), whole((1, W))]
    out_specs = [pl.BlockSpec((tc, 2 * W), lambda i: (tb(i), 0)), whole((CONV_WIDTH, W)), whole((1, W)), whole(gate_w),
                 whole((1, W)), whole(gate_w), whole((1, W)), whole((1, W))]
    vec = jax.ShapeDtypeStruct((1, W), F32)
    out_shape = [jax.ShapeDtypeStruct((L, 2 * W), BF16), jax.ShapeDtypeStruct((CONV_WIDTH, W), F32), vec,
                 jax.ShapeDtypeStruct(gate_w, F32), vec, jax.ShapeDtypeStruct(gate_w, F32), vec, vec]

    def body(x_ref, y_ref, xh_ref, hs_ref, hh_ref, dr_ref, cw_ref, cb_ref, wa_ref, ba_ref, wx_ref, bx_ref, lam_ref,
             dxy_ref, dcw_ref, dcb_ref, dwa_ref, dba_ref, dwx_ref, dbx_ref, dlam_ref, gcar, dxc_car, a_scr, b_scr, g_scr):
        pid = pl.program_id(0)
        t = tb(pid)
        accs = (dcw_ref, dcb_ref, dwa_ref, dba_ref, dwx_ref, dbx_ref, dlam_ref)

        @pl.when(pid == 0)
        def _():
            gcar[...] = jnp.zeros_like(gcar)
            dxc_car[...] = jnp.zeros_like(dxc_car)
            for r in accs:
                r[...] = jnp.zeros_like(r)

        halo = jnp.where(t > 0, xh_ref[...].astype(F32), 0.0)
        taps = _conv_taps(jnp.concatenate([halo, x_ref[...].astype(F32)], axis=0), 16)
        xc = cb_ref[...] + sum(cw_ref[k:k + 1, :] * taps[k] for k in range(CONV_WIDTH))
        lam = lam_ref[...]
        sp = _softplus(-lam)
        r, ig, a, om = _lru_gates(xc, wa_ref, ba_ref[...], wx_ref, bx_ref[...], sp)
        sq = jnp.sqrt(om)
        hblk = hs_ref[...]
        hprev = pltpu.roll(jnp.concatenate([jnp.where(t > 0, hh_ref[...], 0.0), hblk], axis=0), 1, 0)[8:]
        gy, dgy = _gelu_parts(y_ref[...].astype(F32))
        drec = dr_ref[...].astype(F32)
        dxy_ref[:, W:] = (drec * hblk * dgy).astype(BF16)

        rowidx = lax.broadcasted_iota(jnp.int32, (tc, W), 0)
        rowmod = rowidx & 7
        ca = jnp.where(rowidx == tc - 1, 1.0, pltpu.roll(a, tc - 1, 0))
        cbv = drec * gy
        for s in (1, 2, 4):
            keep = rowmod < 8 - s
            cbv = jnp.where(keep, ca * pltpu.roll(cbv, tc - s, 0) + cbv, cbv)
            ca = jnp.where(keep, ca * pltpu.roll(ca, tc - s, 0), ca)
        a_scr[...] = ca
        b_scr[...] = cbv

        def tile(k, gc):
            j = tc // 8 - 1 - k
            rows = pl.ds(pl.multiple_of(j * 8, 8), 8)
            gt = a_scr[rows, :] * gc + b_scr[rows, :]
            g_scr[rows, :] = gt
            return jnp.broadcast_to(gt[0:1, :], (8, W))

        lax.fori_loop(0, tc // 8, tile, gcar[...])
        G = g_scr[...]
        gcar[...] = jnp.broadcast_to(a[0:1, :] * G[0:1, :], (8, W))

        da = G * hprev
        dsq = G * (ig * xc)
        di = G * (sq * xc)
        dxc = G * (sq * ig)
        dla = da * a - 2.0 * a * a * (dsq * 0.5 * lax.rsqrt(om))
        dlam_ref[...] += jnp.sum(dla * (-LRU_C * r), axis=0, keepdims=True) * (-_sigmoid(-lam))
        dpr = dla * (-LRU_C * sp) * r * (1.0 - r)
        dpi = di * ig * (1.0 - ig)
        dba_ref[...] += jnp.sum(dpr, axis=0, keepdims=True)
        dbx_ref[...] += jnp.sum(dpi, axis=0, keepdims=True)
        back = []
        for hd in range(LRU_HEADS):
            sl = slice(hd * LRU_HEAD_DIM, (hd + 1) * LRU_HEAD_DIM)
            xh, dprh, dpih = xc[:, sl].astype(BF16), dpr[:, sl].astype(BF16), dpi[:, sl].astype(BF16)
            back.append(_dot(dprh, wa_ref[hd], NT) + _dot(dpih, wx_ref[hd], NT))
            dwa_ref[hd] += _dot(xh, dprh, TN)
            dwx_ref[hd] += _dot(xh, dpih, TN)
        dxc = dxc + jnp.concatenate(back, axis=1)
        dcb_ref[...] += jnp.sum(dxc, axis=0, keepdims=True)
        for k in range(CONV_WIDTH):
            dcw_ref[k:k + 1, :] += jnp.sum(dxc * taps[k], axis=0, keepdims=True)
        ext = jnp.concatenate([dxc, dxc_car[...]], axis=0)
        dx = cw_ref[CONV_WIDTH - 1:CONV_WIDTH, :] * dxc
        for k in range(CONV_WIDTH - 1):
            dx = dx + cw_ref[k:k + 1, :] * pltpu.roll(ext, tc + 8 - (CONV_WIDTH - 1 - k), 0)[:tc]
        dxc_car[...] = dxc[0:8, :]
        dxy_ref[:, :W] = dx.astype(BF16)

    scratch = [pltpu.VMEM((8, W), F32), pltpu.VMEM((8, W), F32)] + [pltpu.VMEM((tc, W), F32)] * 3
    return _call(body, name=name, grid=(nb,), in_specs=specs, out_specs=out_specs, out_shape=out_shape, scratch=scratch,
                 sem=("arbitrary",))(proj, proj, proj, hs, hs, drec_src, cw, cb, wa, ba, wx, bx, lam)


def _final_loss(h, gain, target, *, name, tm=256):
    M, K = h.shape
    row = pl.BlockSpec((tm, K), lambda i: (i, 0))
    vec = pl.BlockSpec((1, K), lambda i: (0, 0))
    one = pl.BlockSpec((1, 128), lambda i: (0, 0))

    def body(h_ref, g_ref, t_ref, dh_ref, dg_ref, loss_ref):
        @pl.when(pl.program_id(0) == 0)
        def _():
            dg_ref[...] = jnp.zeros_like(dg_ref)
            loss_ref[...] = jnp.zeros_like(loss_ref)

        x = h_ref[...]
        r = lax.rsqrt(jnp.mean(x * x, axis=-1, keepdims=True) + NORM_EPS)
        xhat = x * r
        err = xhat * g_ref[...] - t_ref[...]
        loss_ref[...] += 0.5 / K * jnp.sum(err * err)
        dy = err * (1.0 / K)
        dg_ref[...] += jnp.sum(dy * xhat, axis=0, keepdims=True)
        dxh = dy * g_ref[...]
        dh_ref[...] = r * (dxh - xhat * jnp.mean(dxh * xhat, axis=-1, keepdims=True))

    return _call(body, name=name, grid=(M // tm,), in_specs=[row, vec, row], out_specs=[row, vec, one],
                 out_shape=[jax.ShapeDtypeStruct((M, K), F32), jax.ShapeDtypeStruct((1, K), F32),
                            jax.ShapeDtypeStruct((1, 128), F32)], sem=("arbitrary",))(h, gain.reshape(1, K), target)


def _dilated_merge(branches, *, name, tm=512):
    L, W = branches[0].shape
    nbr = len(branches) // 2
    row = pl.BlockSpec((tm, W), lambda i: (i, 0))

    def body(*refs):
        o_ref, lse_ref = refs[-2], refs[-1]
        lses = [refs[2 * b + 1][...] for b in range(nbr)]
        m = lses[0]
        for t in lses[1:]:
            m = jnp.maximum(m, t)
        ws = [jnp.exp(t - m) for t in lses]
        den = ws[0]
        for t in ws[1:]:
            den = den + t
        acc = ws[0] * refs[0][...].astype(F32)
        for b in range(1, nbr):
            acc = acc + ws[b] * refs[2 * b][...].astype(F32)
        o_ref[...] = (acc / den).astype(BF16)
        lse_ref[...] = m + jnp.log(den)

    return _call(body, name=name, grid=(L // tm,), in_specs=[row] * (2 * nbr), out_specs=[row, row],
                 out_shape=[jax.ShapeDtypeStruct((L, W), BF16), jax.ShapeDtypeStruct((L, W), F32)], sem=("parallel",))(*branches)


def _dilated_fwd(proj0):
    L = proj0.shape[0]
    qkv = proj0[:, 2 * D_MODEL:]
    W = B_HEADS * HEAD_DIM
    outs = []
    for window, d in DILATED_PATTERN:
        view = qkv.reshape(L // d, d * 3 * W)
        o, lse = _band_fwd(view, view, view, d=d, nq=B_HEADS, nkv=B_HEADS, qcol=lambda r: 3 * r, kcol=lambda r: 3 * r + 1,
                           vcol=lambda r: 3 * r + 2, max_dist=window // d, name=f"dilated_fwd_d{d}")
        outs += [o.reshape(L, W), lse.reshape(L, W)]
    return _dilated_merge(outs, name="dilated_merge")


def _dilated_bwd(proj0, att, lse, datt, tabs):
    L = proj0.shape[0]
    qkv = proj0[:, 2 * D_MODEL:]
    Wh = B_HEADS * HEAD_DIM
    branches = []
    for window, d in DILATED_PATTERN:
        view = qkv.reshape(L // d, d * 3 * Wh)
        v1 = lambda t: t.reshape(L // d, d * Wh)
        outs = _band_bwd(view, view, view, v1(datt), v1(att), v1(lse), d=d, nq=B_HEADS, nkv=B_HEADS,
                         qcol=lambda r: 3 * r, kcol=lambda r: 3 * r + 1, vcol=lambda r: 3 * r + 2, docol=lambda r: r,
                         max_dist=window // d, name=f"dilated_bwd_d{d}")
        branches.append([o.reshape(L, Wh) for o in outs])
    return _attn_grad_combine(branches, tabs, name="dilated_grad_combine")


def _device_step(x, mem, target, w, on_grads=None):
    L = x.shape[0]
    tabs = _rope_tables(L)
    g = {}
    saved = []
    h = x
    for layer in range(2):
        sv = {"h_mix": h}
        if layer == 0:
            proj, n = _rowmm(h, w["ab_w_in"], name="l0_in_proj", gain=w["mix_norm"][0],
                             rope=(2 * D_MODEL, 2 * D_MODEL + 2 * B_HEADS * HEAD_DIM, tabs))
            rec, hs = _lru_fwd(proj, w["lru_conv_w"], w["lru_conv_b"], w["lru_wa"], w["lru_ba"], w["lru_wx"], w["lru_bx"],
                               w["lru_lambda"], name="lru_fwd")
            att, lse = _dilated_fwd(proj)
            mix = jnp.concatenate([rec, att], axis=1)
            (h,) = _rowmm(mix, w["ab_w_out"], name="l0_out_proj", res=h)
            sv.update(hs=hs)
        else:
            proj, n = _rowmm(h, w["c_w_qkv"], name="l1_qkv_proj", gain=w["mix_norm"][1], bias=w["c_b_qkv"],
                             rope=(0, (C_HEADS + C_KV_HEADS) * HEAD_DIM, tabs))
            mix, lse = _band_fwd(proj, proj, proj, d=1, nq=C_HEADS, nkv=C_KV_HEADS, qcol=lambda r: 0, kcol=lambda r: 8,
                                 vcol=lambda r: 9, max_dist=C_WINDOW - 1, sinks=w["c_sinks"], name="swa_fwd")
            (h,) = _rowmm(mix, w["c_w_out"], name="l1_out_proj", res=h, bias=w["c_b_out"])
        sv.update(proj=proj, n_mix=n, mix=mix, lse=lse, h_xa=h)
        xq, nx = _rowmm(h, w["xa_wq"][layer][None], name=f"xa_q_proj{layer}", gain=w["xa_norm"][layer])
        kv, nm = _rowmm(mem, w["xa_wkv"][layer][None], name=f"xa_kv_proj{layer}", gain=w["xa_mem_norm"][layer])
        xo, xlse = _xattn_fwd(xq, kv, name=f"xa_fwd{layer}")
        (h,) = _rowmm(xo, w["xa_wo"][layer], name=f"xa_out_proj{layer}", res=h)
        sv.update(xq=xq, nx=nx, kv=kv, nm=nm, xo=xo, xlse=xlse, h_ffn=h)
        gu, nf, act = _rowmm(h, w["ffn_w_gate_up"], layer=layer, name=f"ffn_in{layer}", gain=w["ffn_norm"][layer], swiglu=True)
        (h,) = _rowmm(act, w["ffn_w_down"][layer][None], name=f"ffn_out{layer}", res=h, tm=512)
        sv.update(gu=gu, nf=nf, act=act)
        saved.append(sv)

    dh, g["final_norm"], loss = _final_loss(h, w["final_norm"], target, name="final_loss")

    stk = {k: [None, None] for k in ("xa_norm", "xa_mem_norm", "ffn_norm", "mix_norm")}
    after = None
    for layer in (1, 0):
        sv = saved[layer]
        (g["ffn_w_down", layer],) = _mm_tn(sv["act"], dh, S=1, name=f"ffn_down_dw{layer}", kk=D_FF // 2)
        (dgu,) = _mm_nt(dh, w["ffn_w_down"][layer][None], name=f"ffn_dact{layer}", mode="swiglu", kchunk=D_FF // 2, gu=sv["gu"],
                        after=after)
        (g["ffn_w_gate_up", layer],) = _mm_tn(sv["nf"], dgu, S=N_CHIPS, name=f"ffn_gu_dw{layer}")
        dh, stk["ffn_norm"][layer] = _mm_nt(dgu, w["ffn_w_gate_up"], layer=layer, name=f"ffn_dx{layer}", mode="norm",
                                            h=sv["h_ffn"], gain=w["ffn_norm"][layer], dh=dh)
        (g["xa_wo", layer],) = _mm_tn(sv["xo"], dh, S=N_CHIPS, name=f"xa_wo_dw{layer}")
        (dxo,) = _mm_nt(dh, w["xa_wo"][layer], name=f"xa_dxo{layer}", mode="plain")
        dxq, dkv = _xattn_bwd(sv["xq"], sv["kv"], sv["xo"], sv["xlse"], dxo, name=f"xa_bwd{layer}")
        (g["xa_wq", layer],) = _mm_tn(sv["nx"], dxq, S=1, name=f"xa_wq_dw{layer}")
        dh, stk["xa_norm"][layer] = _mm_nt(dxq, w["xa_wq"][layer][None], name=f"xa_dx{layer}", mode="norm", h=sv["h_xa"],
                                           gain=w["xa_norm"][layer], dh=dh)
        (g["xa_wkv", layer],) = _mm_tn(sv["nm"], dkv, S=1, name=f"xa_wkv_dw{layer}")
        _, stk["xa_mem_norm"][layer] = _mm_nt(dkv, w["xa_wkv"][layer][None], name=f"xa_dmem{layer}", mode="norm", h=mem,
                                              gain=w["xa_mem_norm"][layer])
        if layer == 1:
            g["c_w_out"], g["c_b_out"] = _mm_tn(sv["mix"], dh, S=1, name="l1_out_dw", bias=True)
            (dmix,) = _mm_nt(dh, w["c_w_out"], name="l1_dmix", mode="plain")
            dq, dk, dv, dsk = _band_bwd(sv["proj"], sv["proj"], sv["proj"], dmix, sv["mix"], sv["lse"], d=1, nq=C_HEADS,
                                        nkv=C_KV_HEADS, qcol=lambda r: 0, kcol=lambda r: 8, vcol=lambda r: 9,
                                        docol=lambda r: 0, max_dist=C_WINDOW - 1, sinks=w["c_sinks"], name="swa_bwd")
            g["c_sinks"] = dsk[0, :C_HEADS]
            dproj = _attn_grad_combine([(dq, dk, dv)], tabs, name="swa_grad_combine")
            g["c_w_qkv"], g["c_b_qkv"] = _mm_tn(sv["n_mix"], dproj, S=1, name="l1_qkv_dw", bias=True)
            dh, stk["mix_norm"][1] = _mm_nt(dproj, w["c_w_qkv"], name="l1_dx", mode="norm", h=sv["h_mix"],
                                            gain=w["mix_norm"][1], dh=dh)
            if on_grads is not None:
                after = on_grads("layer1", g)
        else:
            if on_grads is not None:
                after = on_grads("layer0_ffn_xa", g)
            (g["ab_w_out"],) = _mm_tn(sv["mix"], dh, S=1, name="l0_out_dw", kk=768)
            (dmix,) = _mm_nt(dh, w["ab_w_out"], name="l0_dmix", mode="plain", kchunk=768, after=after)
            (dxy, g["lru_conv_w"], g["lru_conv_b"], g["lru_wa"], g["lru_ba"], g["lru_wx"], g["lru_bx"],
             g["lru_lambda"]) = _lru_bwd(sv["proj"], sv["hs"], dmix, w["lru_conv_w"], w["lru_conv_b"], w["lru_wa"],
                                         w["lru_ba"], w["lru_wx"], w["lru_bx"], w["lru_lambda"], name="lru_bwd")
            dqkv = _dilated_bwd(sv["proj"], sv["mix"][:, D_MODEL:], sv["lse"], dmix[:, D_MODEL:], tabs)
            dproj = jnp.concatenate([dxy, dqkv], axis=1)
            (g["ab_w_in"],) = _mm_tn(sv["n_mix"], dproj, S=N_CHIPS, name="l0_in_dw")
            dh, stk["mix_norm"][0] = _mm_nt(dproj, w["ab_w_in"], name="l0_dx", mode="norm", h=sv["h_mix"],
                                            gain=w["mix_norm"][0], dh=dh)
    for k, v in stk.items():
        g[k] = jnp.concatenate(v, axis=0)
    return loss[0, 0], dh, g


ANY = pl.BlockSpec(memory_space=pl.ANY)
MESH = pl.DeviceIdType.MESH


def _place():
    x, y, c = lax.axis_index("x"), lax.axis_index("y"), lax.axis_index("c")
    return x, y, c, [(1 - x, y), (x, 1 - y), (1 - x, 1 - y)]


def _remote(send_sems, recv_sems):
    def copy(k, src, dst, to):
        return pltpu.make_async_remote_copy(src_ref=src, dst_ref=dst, send_sem=send_sems.at[k], recv_sem=recv_sems.at[k],
                                            device_id=to, device_id_type=MESH)
    return copy


def _halves(ref, n_rows):
    rh = n_rows // 2
    return lambda lead, hh: ref.at[(*lead, pl.ds(hh * rh, rh), slice(None))]


def _gather_weights(packs, spack):
    n = len(packs)

    def body(*refs):
        w_refs, s_ref, wf_refs, sf_ref = refs[:n], refs[n], refs[n + 1:2 * n + 1], refs[2 * n + 1]
        x, y, c, chips = _place()
        me, sib = 2 * x + y, (x, y, 1 - c)
        copy = _remote(*refs[-2:])
        src = [_halves(w_refs[g], packs[g].shape[0]) for g in range(n)]
        dst = [_halves(wf_refs[g], packs[g].shape[0]) for g in range(n)]
        sends = []
        for g in range(n):
            for j, (cx, cy) in enumerate(chips):
                sends.append(copy(3 * g + j, src[g]((), c), dst[g]((me,), c), (cx, cy, c)))
        for j, (cx, cy) in enumerate(chips):
            sends.append(copy(6 * n + j, s_ref, sf_ref.at[me], (cx, cy, c)))
        for cp in sends:
            cp.start()
        for g in range(n):
            for j, (cx, cy) in enumerate(chips):
                got = dst[g]((2 * cx + cy,), c)
                copy(3 * g + j, got, got, sib).wait_recv()
                fwd = copy(3 * n + 3 * g + j, got, got, sib)
                fwd.start()
                sends.append(fwd)
        for g in range(n):
            for j, (cx, cy) in enumerate(chips):
                got = dst[g]((2 * cx + cy,), 1 - c)
                copy(3 * n + 3 * g + j, got, got, sib).wait_recv()
        for j, (cx, cy) in enumerate(chips):
            copy(6 * n + j, s_ref, sf_ref.at[2 * cx + cy], sib).wait_recv()
        for cp in sends:
            cp.wait_send()

    ins = list(packs) + [spack]
    out_shape = [jax.ShapeDtypeStruct((N_CHIPS,) + a.shape, a.dtype) for a in ins]
    n_sems = 6 * n + 3
    outs = pl.pallas_call(body, name="gather_weights", out_shape=out_shape, in_specs=[ANY] * len(ins),
                          out_specs=[ANY] * len(ins),
                          scratch_shapes=[pltpu.SemaphoreType.DMA((n_sems,)), pltpu.SemaphoreType.DMA((n_sems,))])(*ins)
    chip = 2 * lax.axis_index("x") + lax.axis_index("y")
    outs = [lax.dynamic_update_index_in_dim(o, a, chip, 0) for o, a in zip(outs, ins)]
    return outs[:n], outs[n]


SEQUENCER_GATHER_IDS = {"mid": 1, "late": 5}


def _gather_weights_behind(packs, *, tag):
    n = len(packs)

    def body(*refs):
        w_refs, wf_refs = refs[:n], refs[n:2 * n]
        x, y, c, chips = _place()
        me, sib = 2 * x + y, (x, y, 1 - c)
        barrier = pltpu.get_barrier_semaphore()
        for peer in [(cx, cy, c) for cx, cy in chips] + [sib]:
            pl.semaphore_signal(barrier, inc=1, device_id=peer, device_id_type=MESH)
        pl.semaphore_wait(barrier, len(chips) + 1)
        copy = _remote(*refs[-2:])
        src = [_halves(w_refs[g], packs[g].shape[0]) for g in range(n)]
        dst = [_halves(wf_refs[g], packs[g].shape[0]) for g in range(n)]
        sends = []
        for g in range(n):
            for j, (cx, cy) in enumerate(chips):
                sends.append(copy(3 * g + j, src[g]((), c), dst[g]((me,), c), (cx, cy, c)))
        for cp in sends:
            cp.start()
        for g in range(n):
            for j, (cx, cy) in enumerate(chips):
                got = dst[g]((2 * cx + cy,), c)
                copy(3 * g + j, got, got, sib).wait_recv()
                fwd = copy(3 * n + 3 * g + j, got, got, sib)
                fwd.start()
                sends.append(fwd)
        for g in range(n):
            for j, (cx, cy) in enumerate(chips):
                got = dst[g]((2 * cx + cy,), 1 - c)
                copy(3 * n + 3 * g + j, got, got, sib).wait_recv()
        for cp in sends:
            cp.wait_send()

    out_type = [jax.ShapeDtypeStruct((N_CHIPS,) + a.shape, a.dtype) for a in packs]
    outs = pl.kernel(body, out_type=out_type, mesh=plsc.ScalarSubcoreMesh(axis_name="sequencer", num_cores=1),
                     name="gather_weights_behind_" + tag,
                     scratch_types=[pltpu.SemaphoreType.DMA((6 * n,)), pltpu.SemaphoreType.DMA((6 * n,))],
                     compiler_params=pltpu.CompilerParams(collective_id=SEQUENCER_GATHER_IDS[tag]))(*packs)
    chip = 2 * lax.axis_index("x") + lax.axis_index("y")
    return [lax.dynamic_update_index_in_dim(o, a, chip, 0) for o, a in zip(outs, packs)]


def _rs_pair_exchange(gpacks, *, name):
    n = len(gpacks)

    def body(*refs):
        g_refs, ra_refs = refs[:n], refs[n:2 * n]
        x, y, c, _ = _place()
        copy = _remote(*refs[-2:])
        cps = []
        for g in range(n):
            half = _halves(g_refs[g], gpacks[g].shape[1])
            cps += [copy(N_CHIPS * g + j, half((j,), 1 - c), ra_refs[g].at[j], (x, y, 1 - c)) for j in range(N_CHIPS)]
        for cp in cps:
            cp.start()
        for cp in cps:
            cp.wait()

    out_shape = [jax.ShapeDtypeStruct((N_CHIPS, a.shape[1] // 2, a.shape[2]), a.dtype) for a in gpacks]
    n_sems = N_CHIPS * n
    return pl.pallas_call(body, name=name, out_shape=out_shape, in_specs=[ANY] * n, out_specs=[ANY] * n,
                          scratch_shapes=[pltpu.SemaphoreType.DMA((n_sems,)), pltpu.SemaphoreType.DMA((n_sems,))])(*gpacks)


def _row_tile(rows, cap=512):
    return max(t for t in range(16, min(rows, cap) + 1, 16) if rows % t == 0)


def _rs_pair_add(place, gpack, ra, *, name):
    _, R, C = gpack.shape
    Rh = R // 2
    tr = _row_tile(Rh)
    nrb = Rh // tr

    def body(p_ref, g_ref, ra_ref, pair_ref, own_ref):
        s = g_ref[...].astype(F32) + ra_ref[...].astype(F32)
        pair_ref[...] = s.astype(BF16)

        @pl.when(pl.program_id(1) == p_ref[1])
        def _():
            own_ref[...] = s

    grid_spec = pltpu.PrefetchScalarGridSpec(
        num_scalar_prefetch=1, grid=(nrb, N_CHIPS),
        in_specs=[pl.BlockSpec((None, tr, C), lambda i, j, p: (j, p[0] * nrb + i, 0)),
                  pl.BlockSpec((None, tr, C), lambda i, j, p: (j, i, 0))],
        out_specs=[pl.BlockSpec((None, tr, C), lambda i, j, p: (j, i, 0)), pl.BlockSpec((tr, C), lambda i, j, p: (i, 0))])
    return pl.pallas_call(
        body, name=name, grid_spec=grid_spec,
        out_shape=[jax.ShapeDtypeStruct((N_CHIPS, Rh, C), BF16), jax.ShapeDtypeStruct((Rh, C), F32)],
        compiler_params=pltpu.CompilerParams(dimension_semantics=("arbitrary", "arbitrary"),
                                             vmem_limit_bytes=VMEM_LIMIT_V7X))(place, gpack, ra)


SEQUENCER_EXCHANGE_IDS = {"l1": 2, "l0a": 3, "l0b": 4}


def _rs_chip_exchange_behind(pairs, *, tag, small=None):
    n = len(pairs)
    has_small = small is not None

    def body(*refs):
        p_refs = refs[:n]
        s_ref = refs[n] if has_small else None
        rb_refs = refs[n + has_small:2 * n + has_small]
        rs_ref = refs[2 * n + 1] if has_small else None
        x, y, c, chips = _place()
        peers = [(1 - x if k & 4 else x, 1 - y if k & 2 else y, 1 - c if k & 1 else c) for k in range(1, 8)]
        shake = peers if has_small else [(cx, cy, c) for cx, cy in chips]
        barrier = pltpu.get_barrier_semaphore()
        for peer in shake:
            pl.semaphore_signal(barrier, inc=1, device_id=peer, device_id_type=MESH)
        pl.semaphore_wait(barrier, len(shake))
        copy = _remote(*refs[-2:])
        cps = []
        for g in range(n):
            cps += [copy(3 * g + j, p_refs[g].at[2 * cx + cy], rb_refs[g].at[j], (cx, cy, c)) for j, (cx, cy) in enumerate(chips)]
        if has_small:
            dev = 4 * x + 2 * y + c
            cps += [copy(3 * n + k, s_ref, rs_ref.at[dev], peer) for k, peer in enumerate(peers)]
        for cp in cps:
            cp.start()
        for g in range(n):
            for j in range(3):
                copy(3 * g + j, p_refs[g].at[0], rb_refs[g].at[j], (x, y, c)).wait_recv()
        if has_small:
            for k, (px, py, pc) in enumerate(peers):
                copy(3 * n + k, s_ref, rs_ref.at[4 * px + 2 * py + pc], (x, y, c)).wait_recv()
        for cp in cps:
            cp.wait_send()

    ins = list(pairs) + ([small] if has_small else [])
    out_type = [jax.ShapeDtypeStruct((3,) + p.shape[1:], p.dtype) for p in pairs]
    if has_small:
        out_type.append(jax.ShapeDtypeStruct((8,) + small.shape, small.dtype))
    n_sems = 3 * n + 7 * has_small
    outs = pl.kernel(body, out_type=out_type, mesh=plsc.ScalarSubcoreMesh(axis_name="sequencer", num_cores=1),
                     name="rs_chip_exchange_behind_" + tag,
                     scratch_types=[pltpu.SemaphoreType.DMA((n_sems,)), pltpu.SemaphoreType.DMA((n_sems,))],
                     compiler_params=pltpu.CompilerParams(collective_id=SEQUENCER_EXCHANGE_IDS[tag]))(*ins)
    if has_small:
        dev = 4 * lax.axis_index("x") + 2 * lax.axis_index("y") + lax.axis_index("c")
        outs = list(outs[:n]) + [lax.dynamic_update_index_in_dim(outs[n], small, dev, 0)]
    return outs


def _rs_final_add(place, own, rb, *, name):
    Rh, C = own.shape
    tr = _row_tile(Rh)
    nrb = Rh // tr

    def body(p_ref, o_ref, rb_ref, f_ref):
        f_ref[...] = ((o_ref[...] + rb_ref[0].astype(F32)) + rb_ref[1].astype(F32)) + rb_ref[2].astype(F32)

    grid_spec = pltpu.PrefetchScalarGridSpec(
        num_scalar_prefetch=1, grid=(nrb,),
        in_specs=[pl.BlockSpec((tr, C), lambda i, p: (i, 0)), pl.BlockSpec((3, tr, C), lambda i, p: (0, i, 0))],
        out_specs=pl.BlockSpec((tr, C), lambda i, p: (p[0] * nrb + i, 0)))
    return pl.pallas_call(
        body, name=name, grid_spec=grid_spec, out_shape=jax.ShapeDtypeStruct((2 * Rh, C), F32),
        compiler_params=pltpu.CompilerParams(dimension_semantics=("arbitrary",), vmem_limit_bytes=VMEM_LIMIT_V7X))(place, own, rb)


def _sum_slots(rs):
    n, rows, C = rs.shape

    def body(r_ref, o_ref):
        acc = r_ref[0]
        for k in range(1, n):
            acc = acc + r_ref[k]
        o_ref[...] = acc

    return _call(body, name="small_grad_sum", grid=(1,), in_specs=[pl.BlockSpec((n, rows, C), lambda i: (0, 0, 0))],
                 out_specs=pl.BlockSpec((rows, C), lambda i: (0, 0)), out_shape=jax.ShapeDtypeStruct((rows, C), F32),
                 sem=("arbitrary",))(rs)


def _rs_sibling_share(gbufs, *, name):
    n = len(gbufs)

    def body(*refs):
        g_refs = refs[n:2 * n]
        x, y, c, _ = _place()
        copy = _remote(*refs[-2:])
        halves = [_halves(g_refs[g], gbufs[g].shape[0]) for g in range(n)]
        outs = [copy(g, halves[g]((), c), halves[g]((), c), (x, y, 1 - c)) for g in range(n)]
        for cp in outs:
            cp.start()
        for g in range(n):
            copy(g, halves[g]((), 1 - c), halves[g]((), 1 - c), (x, y, c)).wait_recv()
        for cp in outs:
            cp.wait_send()

    return pl.pallas_call(body, name=name, out_shape=[jax.ShapeDtypeStruct(a.shape, a.dtype) for a in gbufs],
                          in_specs=[ANY] * n, out_specs=[ANY] * n, input_output_aliases={g: g for g in range(n)},
                          scratch_shapes=[pltpu.SemaphoreType.DMA((n,)), pltpu.SemaphoreType.DMA((n,))])(*gbufs)


def _adamw(w, g, m, v, *, name, g_row=0):
    rows, cols = w.shape
    tr = rows
    for cand in range(min(rows, 512), 7, -8):
        if rows % cand == 0 and g_row % cand == 0:
            tr = cand
            break
    spec = pl.BlockSpec((tr, cols), lambda i: (i, 0))
    g_spec = pl.BlockSpec((tr, cols), lambda i: (g_row // tr + i, 0))

    def body(w_ref, g_ref, m_ref, v_ref, d_ref, nm_ref, nv_ref):
        gg = g_ref[...]
        nm = ADAM_B1 * m_ref[...] + (1.0 - ADAM_B1) * gg
        nv = ADAM_B2 * v_ref[...] + (1.0 - ADAM_B2) * (gg * gg)
        m_hat = nm / (1.0 - ADAM_B1 ** ADAM_STEP)
        v_hat = nv / (1.0 - ADAM_B2 ** ADAM_STEP)
        d_ref[...] = -ADAM_LR * (m_hat / (jnp.sqrt(v_hat) + ADAM_EPS) + ADAM_WD * w_ref[...])
        nm_ref[...] = nm
        nv_ref[...] = nv

    return _call(body, name=name, grid=(rows // tr,), in_specs=[spec, g_spec, spec, spec], out_specs=[spec] * 3,
                 out_shape=[jax.ShapeDtypeStruct((rows, cols), F32)] * 3, sem=("parallel",))(w, g, m, v)


WEIGHT_NAMES = ("mix_norm", "ab_w_in", "lru_conv_w", "lru_conv_b", "lru_wa", "lru_ba", "lru_wx", "lru_bx", "lru_lambda",
                "ab_w_out", "c_w_qkv", "c_b_qkv", "c_sinks", "c_w_out", "c_b_out", "xa_norm", "xa_mem_norm", "xa_wq",
                "xa_wkv", "xa_wo", "ffn_norm", "ffn_w_gate_up", "ffn_w_down", "final_norm")
EARLY_GROUPS = (("ab_w_in",),)
MID_GROUPS = (("ab_w_out",), ("lru_wa", "lru_wx"))
LATE_GROUPS = (("c_w_out", "xa_wkv", "ffn_w_down"), ("ffn_w_gate_up",), ("xa_wo",), ("xa_wq",), ("c_w_qkv",))
GROUPS = EARLY_GROUPS + MID_GROUPS + LATE_GROUPS
REPLICATED = ("mix_norm", "lru_conv_b", "lru_lambda", "c_sinks", "xa_norm", "xa_mem_norm", "ffn_norm", "final_norm")
SMALL_SHARDED = ("lru_conv_w", "lru_ba", "lru_bx", "c_b_qkv", "c_b_out")
LANES = 1024


def _rows(v):
    flat = v.reshape(-1)
    return jnp.pad(flat, (0, -flat.shape[0] % LANES)).reshape(-1, LANES)


def _pack_small(parts, total, *, name):
    def body(*refs):
        o_ref = refs[-1]
        o_ref[...] = jnp.zeros_like(o_ref)
        row = 0
        for p_ref in refs[:-1]:
            o_ref[row:row + p_ref.shape[0], :] = p_ref[...]
            row += p_ref.shape[0]

    return _call(body, name=name, grid=(1,), in_specs=[pl.BlockSpec(p.shape, lambda i: (0, 0)) for p in parts],
                 out_specs=pl.BlockSpec((total, LANES), lambda i: (0, 0)),
                 out_shape=jax.ShapeDtypeStruct((total, LANES), F32), sem=("arbitrary",))(*parts)


def _from_shards(name, t):
    minor = t.shape[-1]
    if name == "ab_w_in":
        return t
    if name in ("ab_w_out", "c_w_out"):
        return t.reshape(1, -1, minor)
    if name == "ffn_w_gate_up":
        return t.reshape(N_CHIPS, 2, -1, minor)
    if name in ("xa_wq", "xa_wkv", "ffn_w_down"):
        return t.reshape(N_CHIPS, 2, -1, minor).transpose(1, 0, 2, 3).reshape(2, -1, minor)
    if name in ("lru_wa", "lru_wx"):
        return t.reshape(N_CHIPS, LRU_HEADS, -1, minor).transpose(1, 0, 2, 3).reshape(LRU_HEADS, LRU_HEAD_DIM, minor)
    if name == "xa_wo":
        return t.reshape(N_CHIPS, 2, -1, minor).transpose(1, 0, 2, 3)
    assert name == "c_w_qkv"
    return t.transpose(1, 0, 2).reshape(1, D_MODEL, -1)


def _piece_shards(name, g):
    minor = g.shape[-1]
    if name in ("ab_w_in", "ffn_w_gate_up", "xa_wo"):
        return g
    if name in ("ab_w_out", "c_w_out", "xa_wq", "xa_wkv", "ffn_w_down"):
        return g.reshape(N_CHIPS, -1, minor)
    if name in ("lru_wa", "lru_wx"):
        return g.reshape(LRU_HEADS, N_CHIPS, -1, minor).transpose(1, 0, 2, 3).reshape(N_CHIPS, -1, minor)
    assert name == "c_w_qkv"
    return g.reshape(D_MODEL, N_CHIPS, -1).transpose(1, 0, 2)


RS_SETS = {
    "l1": ((("c_w_out", None), ("xa_wkv", 1), ("ffn_w_down", 1)), (("ffn_w_gate_up", 1),), (("xa_wo", 1),),
           (("xa_wq", 1),), (("c_w_qkv", None),)),
    "l0a": ((("xa_wkv", 0), ("ffn_w_down", 0)), (("ffn_w_gate_up", 0),), (("xa_wo", 0),), (("xa_wq", 0),)),
    "l0b": ((("ab_w_out", None),), (("ab_w_in", None),), (("lru_wa", None), ("lru_wx", None))),
}
RS_STAGE = {"layer1": "l1", "layer0_ffn_xa": "l0a"}


def kernel(x, mem, mix_norm, ab_w_in, lru_conv_w, lru_conv_b, lru_wa, lru_ba, lru_wx, lru_bx, lru_lambda, ab_w_out, c_w_qkv, c_b_qkv, c_sinks, c_w_out, c_b_out, xa_norm, xa_mem_norm, xa_wq, xa_wkv, xa_wo, ffn_norm, ffn_w_gate_up, ffn_w_down, final_norm, loss_target, m_mix_norm, m_ab_w_in, m_lru_conv_w, m_lru_conv_b, m_lru_wa, m_lru_ba, m_lru_wx, m_lru_bx, m_lru_lambda, m_ab_w_out, m_c_w_qkv, m_c_b_qkv, m_c_sinks, m_c_w_out, m_c_b_out, m_xa_norm, m_xa_mem_norm, m_xa_wq, m_xa_wkv, m_xa_wo, m_ffn_norm, m_ffn_w_gate_up, m_ffn_w_down, m_final_norm, v_mix_norm, v_ab_w_in, v_lru_conv_w, v_lru_conv_b, v_lru_wa, v_lru_ba, v_lru_wx, v_lru_bx, v_lru_lambda, v_ab_w_out, v_c_w_qkv, v_c_b_qkv, v_c_sinks, v_c_w_out, v_c_b_out, v_xa_norm, v_xa_mem_norm, v_xa_wq, v_xa_wkv, v_xa_wo, v_ffn_norm, v_ffn_w_gate_up, v_ffn_w_down, v_final_norm):
    given = dict(locals())
    wl = {n: given[n] for n in WEIGHT_NAMES}
    ml = {n: given["m_" + n] for n in WEIGHT_NAMES}
    vl = {n: given["v_" + n] for n in WEIGHT_NAMES}
    xi, yi, ci = lax.axis_index("x"), lax.axis_index("y"), lax.axis_index("c")
    chip = 2 * xi + yi

    def join(parts, axis):
        return parts[0] if len(parts) == 1 else jnp.concatenate(parts, axis=axis)

    local_rows = {n: wl[n].size // wl[n].shape[-1] for grp in GROUPS for n in grp}
    packs = [join([wl[n].astype(BF16).reshape(local_rows[n], wl[n].shape[-1]) for n in grp], 0) for grp in GROUPS]
    spack = _pack_small([_rows(wl[n]) for n in SMALL_SHARDED], 8, name="pack_small_weights")
    n_early, n_mid = len(EARLY_GROUPS), len(EARLY_GROUPS) + len(MID_GROUPS)
    early, sfull = _gather_weights(packs[:n_early], spack)
    early, sfull, mid_packs = lax.optimization_barrier((early, sfull, packs[n_early:n_mid]))
    mid = _gather_weights_behind(mid_packs, tag="mid")
    mid, late_packs = lax.optimization_barrier((mid, packs[n_mid:]))
    gathered = early + mid + _gather_weights_behind(late_packs, tag="late")
    w = {n: wl[n] for n in REPLICATED}
    w["c_sinks"] = wl["c_sinks"][0]
    for grp, full in zip(GROUPS, gathered):
        off = 0
        for n in grp:
            w[n] = _from_shards(n, full if len(grp) == 1 else full[:, off:off + local_rows[n]])
            off += local_rows[n]
    for r, n in enumerate(SMALL_SHARDED):
        loc = wl[n].shape[1:]
        t = sfull[:, r, :wl[n].size].reshape((N_CHIPS,) + loc)
        if n == "lru_conv_w":
            w[n] = t.transpose(1, 0, 2).reshape(CONV_WIDTH, -1)
        elif n in ("lru_ba", "lru_bx"):
            w[n] = t.transpose(1, 0, 2).reshape(1, -1)
        else:
            w[n] = t.reshape(1, -1)

    place = jnp.stack([ci, chip]).astype(jnp.int32)

    def pair_stage(spec, g, tag):
        piece = lambda n, l: (g[n] if l is None else g[n, l]).astype(BF16)
        gpacks = [join([_piece_shards(n, piece(n, l)) for n, l in grp], 1) for grp in spec]
        ras = _rs_pair_exchange(gpacks, name=f"rs_pair_exchange_{tag}")
        sums = [_rs_pair_add(place, gp, ra, name=f"rs_pair_add_{tag}_{i}") for i, (gp, ra) in enumerate(zip(gpacks, ras))]
        return [pair for pair, _ in sums], [own for _, own in sums]

    reduced, in_flight = [], []

    def take_up():
        done = [_rs_final_add(place, o, r, name=f"rs_final_add_{len(reduced) + i}") for i, (o, r) in enumerate(in_flight)]
        reduced.extend(done)
        in_flight.clear()
        return done

    def reduce_behind(stage, g):
        done = take_up()
        tag = RS_STAGE[stage]
        pairs, own = pair_stage(RS_SETS[tag], g, tag)
        in_flight.extend(zip(own, _rs_chip_exchange_behind(pairs, tag=tag)))
        return own + done

    loss_part, grad_x, g = _device_step(x[0], mem[0], loss_target[0], w, on_grads=reduce_behind)

    small_parts = [_rows(g[n]) for n in REPLICATED] + [_rows(jnp.broadcast_to(loss_part, (LANES,)))]
    small_parts += [_rows(g[n]) for n in SMALL_SHARDED]
    small = _pack_small(small_parts, 24, name="pack_small_grads")
    take_up()
    pairs, own = pair_stage(RS_SETS["l0b"], g, "l0b")
    *rb, rs = _rs_chip_exchange_behind(pairs, tag="l0b", small=small)
    gsums = list(_rs_sibling_share(list(reduced), name="rs_sibling_share_behind"))
    in_flight.extend(zip(own, rb))
    gsums += list(_rs_sibling_share(take_up(), name="rs_sibling_share_last"))
    ssum = _sum_slots(rs)

    where = {}
    for grp, gsum in zip(RS_SETS["l1"] + RS_SETS["l0a"] + RS_SETS["l0b"], gsums):
        off = 0
        for n, l in grp:
            rows = local_rows[n] if l is None else local_rows[n] // 2
            where[n, l] = (gsum, off, rows, len(grp) == 1)
            off += rows
    take = lambda gsum, off, rows, whole: gsum if whole else gsum[off:off + rows]
    grads, grad_rows = {}, {}
    for grp in LATE_GROUPS + EARLY_GROUPS + MID_GROUPS:
        for n in grp:
            if (n, None) in where:
                grads[n] = take(*where[n, None]).reshape(wl[n].shape)
                grad_rows[n] = where[n, None][:2]
            else:
                grads[n] = jnp.stack([take(*where[n, l]).reshape(wl[n].shape[1:]) for l in range(2)])
                grad_rows[n] = (grads[n].reshape(local_rows[n], wl[n].shape[-1]), 0)
    row = 0
    for n in REPLICATED:
        k = _rows(g[n]).shape[0]
        grads[n] = ssum[row:row + k].reshape(-1)[:wl[n].size].reshape(wl[n].shape)
        row += k
    loss = ssum[row, 0]
    row += 1
    for n in SMALL_SHARDED:
        k = _rows(g[n]).shape[0]
        full = ssum[row:row + k].reshape(-1)[:g[n].size]
        row += k
        loc = wl[n].shape
        if n == "lru_conv_w":
            sh = full.reshape(CONV_WIDTH, N_CHIPS, -1)
        elif n in ("lru_ba", "lru_bx"):
            sh = full.reshape(LRU_HEADS, N_CHIPS, -1)
        else:
            sh = full.reshape(1, N_CHIPS, -1)
        grads[n] = lax.dynamic_index_in_dim(sh, chip, axis=1, keepdims=False).reshape(loc)

    delta, new_m, new_v = {}, {}, {}
    for n, (gsum, off) in grad_rows.items():
        shape2 = (local_rows[n], wl[n].shape[-1])
        d, nm, nv = _adamw(wl[n].reshape(shape2), gsum, ml[n].reshape(shape2), vl[n].reshape(shape2), g_row=off,
                           name="adamw_" + n)
        delta[n], new_m[n], new_v[n] = (t.reshape(wl[n].shape) for t in (d, nm, nv))
    smalls = REPLICATED + SMALL_SHARDED
    packs = [_pack_small([_rows(src[n]) for n in smalls], 24, name="pack_adamw_" + tag)
             for tag, src in (("w", wl), ("g", grads), ("m", ml), ("v", vl))]
    outs = _adamw(*packs, name="adamw_small")
    row = 0
    for n in smalls:
        k = _rows(wl[n]).shape[0]
        for dst, o in zip((delta, new_m, new_v), outs):
            dst[n] = o[row:row + k].reshape(-1)[:wl[n].size].reshape(wl[n].shape)
        row += k

    return (loss, grad_x[None], *[grads[n] for n in WEIGHT_NAMES], *[delta[n] for n in WEIGHT_NAMES],
            *[new_m[n] for n in WEIGHT_NAMES], *[new_v[n] for n in WEIGHT_NAMES])
```

```python
import jax
import jax.numpy as jnp
from jax import lax
from jax.experimental import pallas as pl
from jax.experimental.pallas import tpu as pltpu
from jax.experimental.pallas import tpu_sc as plsc

F32, BF16 = jnp.float32, jnp.bfloat16
D_MODEL = 1024
NORM_EPS = 1e-6
ROPE_THETA = 500000.0
HEAD_DIM = 64
ROT_DIM = 16
BLK = 128
LRU_HEADS, LRU_HEAD_DIM, CONV_WIDTH, LRU_C = 4, 256, 4, 8.0
DILATED_PATTERN = ((128, 1), (512, 4), (2048, 16))
B_HEADS, C_HEADS, C_KV_HEADS, C_WINDOW = 8, 16, 2, 128
XA_HEADS, XA_HEAD_DIM, N_MEM = 4, 128, 256
D_FF = 2816
NEG = -1e30
ADAM_LR, ADAM_B1, ADAM_B2, ADAM_EPS, ADAM_WD, ADAM_STEP = 0.001, 0.9, 0.999, 1e-08, 0.01, 10
N_CHIPS = 4
VMEM_LIMIT_V7X = 56 * 1024 * 1024

NN = (((1,), (0,)), ((), ()))
NT = (((1,), (1,)), ((), ()))
TN = (((0,), (0,)), ((), ()))


def _dot(a, b, dims=NN):
    return lax.dot_general(a, b, dims, preferred_element_type=F32)


def _sigmoid(x):
    return 0.5 * jnp.tanh(0.5 * x) + 0.5


def _call(body, *, name, grid, in_specs, out_specs, out_shape, scratch=(), sem=None):
    return pl.pallas_call(
        body, name=name, grid=grid, in_specs=in_specs, out_specs=out_specs, out_shape=out_shape,
        scratch_shapes=list(scratch),
        compiler_params=pltpu.CompilerParams(dimension_semantics=sem, vmem_limit_bytes=VMEM_LIMIT_V7X))


def _rope_tables(L):
    half = ROT_DIM // 2
    inv = ROPE_THETA ** (-jnp.arange(0, ROT_DIM, 2, dtype=F32) / ROT_DIM)
    j = jnp.arange(2 * HEAD_DIM) % HEAD_DIM
    ang = jnp.arange(L, dtype=F32)[:, None] * inv[j % half][None, :]
    cos, sin = jnp.cos(ang), jnp.sin(ang)
    c = jnp.where(j < ROT_DIM, cos, 1.0)
    s1 = jnp.where(j < half, -sin, 0.0)
    s2 = jnp.where((j >= half) & (j < ROT_DIM), sin, 0.0)
    return c, s1, s2


def _rope_fwd(v, c, s1, s2):
    return v * c + pltpu.roll(v, 120, 1) * s1 + pltpu.roll(v, 8, 1) * s2


def _rope_bwd(dv, c, s1, s2):
    return dv * c + pltpu.roll(dv * s1, 8, 1) + pltpu.roll(dv * s2, 120, 1)


def _weight_spec(w, layer):
    once = pl.Buffered(1)
    if layer is None:
        return w.shape, pl.BlockSpec(w.shape, lambda i: (0, 0, 0), pipeline_mode=once)
    S, _, K, Ns = w.shape
    return (S, K, Ns), pl.BlockSpec((S, None, K, Ns), lambda i: (0, layer, 0, 0), pipeline_mode=once)


def _rowmm(a, w3, *, name, tm=512, gain=None, bias=None, res=None, swiglu=False, rope=None, layer=None):
    M, K = a.shape
    (S, _, Ns), w_spec = _weight_spec(w3, layer)
    N = S * Ns
    tm = min(tm, M)
    has_norm, has_bias, has_res, has_rope = gain is not None, bias is not None, res is not None, rope is not None
    row = lambda w: pl.BlockSpec((tm, w), lambda i: (i, 0))
    whole = lambda shape: pl.BlockSpec(shape, lambda i: (0,) * len(shape))
    ins, specs = [a], [row(K)]
    if has_norm:
        ins.append(gain.reshape(1, K)); specs.append(whole((1, K)))
    ins.append(w3); specs.append(w_spec)
    if has_bias:
        ins.append(bias.reshape(1, N)); specs.append(whole((1, N)))
    if has_res:
        ins.append(res); specs.append(row(N))
    if has_rope:
        ins += list(rope[2]); specs += [row(128)] * 3
    y_dtype = F32 if has_res else BF16
    out_shape, out_specs = [jax.ShapeDtypeStruct((M, N), y_dtype)], [row(N)]
    if has_norm:
        out_shape.append(jax.ShapeDtypeStruct((M, K), BF16)); out_specs.append(row(K))
    if swiglu:
        out_shape.append(jax.ShapeDtypeStruct((M, N // 2), BF16)); out_specs.append(row(N // 2))
    scratch = [pltpu.VMEM((tm, N), F32)] if has_rope else []

    def body(*refs):
        it = iter(refs)
        a_ref = next(it)
        g_ref = next(it) if has_norm else None
        w_ref = next(it)
        b_ref = next(it) if has_bias else None
        r_ref = next(it) if has_res else None
        tabs = [next(it) for _ in range(3)] if has_rope else None
        y_ref = next(it)
        n_ref = next(it) if has_norm else None
        act_ref = next(it) if swiglu else None
        ys_ref = next(it) if has_rope else None
        if has_norm:
            x = a_ref[...].astype(F32)
            ms = jnp.mean(x * x, axis=-1, keepdims=True)
            xb = (x * lax.rsqrt(ms + NORM_EPS) * g_ref[...]).astype(BF16)
            n_ref[...] = xb
        else:
            xb = a_ref[...].astype(BF16)
        if swiglu:
            for s in range(S // 2):
                g = _dot(xb, w_ref[s])
                u = _dot(xb, w_ref[s + S // 2])
                y_ref[:, s * Ns:(s + 1) * Ns] = g.astype(BF16)
                y_ref[:, N // 2 + s * Ns:N // 2 + (s + 1) * Ns] = u.astype(BF16)
                act_ref[:, s * Ns:(s + 1) * Ns] = (g * _sigmoid(g) * u).astype(BF16)
            return
        for s in range(S):
            sl = slice(s * Ns, (s + 1) * Ns)
            acc = _dot(xb, w_ref[s])
            if has_bias:
                acc = acc + b_ref[:, sl]
            if has_res:
                acc = acc + r_ref[:, sl]
            if has_rope:
                ys_ref[:, sl] = acc
            else:
                y_ref[:, sl] = acc.astype(y_dtype)
        if has_rope:
            c, s1, s2 = (t[...] for t in tabs)
            for cb in range(N // 128):
                cs = slice(cb * 128, (cb + 1) * 128)
                v = ys_ref[:, cs]
                if rope[0] <= cb * 128 < rope[1]:
                    v = _rope_fwd(v, c, s1, s2)
                y_ref[:, cs] = v.astype(BF16)

    return _call(body, name=name, grid=(M // tm,), in_specs=specs, out_specs=out_specs, out_shape=out_shape,
                 scratch=scratch, sem=("parallel",))(*ins)


def _mm_nt(dy, w3, *, name, mode, tm=512, kchunk=None, h=None, gain=None, dh=None, gu=None, layer=None, after=None):
    M, N = dy.shape
    (S, K, Ns), w_spec = _weight_spec(w3, layer)
    kchunk = kchunk or K
    tm = min(tm, M)
    row = lambda w: pl.BlockSpec((tm, w), lambda i: (i, 0))
    whole = lambda shape: pl.BlockSpec(shape, lambda i: (0,) * len(shape))
    ins, specs = [dy, w3], [row(N), w_spec]
    after = list(after or ())
    ins = after + ins
    specs = [pl.BlockSpec((8, a.shape[1]), lambda i: (0, 0)) for a in after] + specs
    has_dh = dh is not None
    if mode == "norm":
        ins += [h, gain.reshape(1, K)]; specs += [row(K), whole((1, K))]
        if has_dh:
            ins.append(dh); specs.append(row(K))
        out_shape = [jax.ShapeDtypeStruct((M, K), F32), jax.ShapeDtypeStruct((1, K), F32)]
        out_specs = [row(K), whole((1, K))]
    elif mode == "swiglu":
        ins.append(gu); specs.append(row(2 * K))
        out_shape, out_specs = [jax.ShapeDtypeStruct((M, 2 * K), BF16)], [row(2 * K)]
    else:
        out_shape, out_specs = [jax.ShapeDtypeStruct((M, K), BF16)], [row(K)]

    def body(*refs):
        it = iter(refs[len(after):])
        dy_ref, w_ref = next(it), next(it)
        if mode == "norm":
            h_ref, g_ref = next(it), next(it)
            dh_ref = next(it) if has_dh else None
            o_ref, dg_ref = next(it), next(it)
        elif mode == "swiglu":
            gu_ref, o_ref = next(it), next(it)
        else:
            o_ref = next(it)
        for kc in range(K // kchunk):
            ks = slice(kc * kchunk, (kc + 1) * kchunk)
            acc = None
            for s in range(S):
                t = _dot(dy_ref[:, s * Ns:(s + 1) * Ns].astype(BF16), w_ref[s, ks, :], NT)
                acc = t if acc is None else acc + t
            if mode == "plain":
                o_ref[:, ks] = acc.astype(BF16)
            elif mode == "swiglu":
                us = slice(K + kc * kchunk, K + (kc + 1) * kchunk)
                g = gu_ref[:, ks].astype(F32)
                u = gu_ref[:, us].astype(F32)
                sg = _sigmoid(g)
                o_ref[:, ks] = (acc * u * (sg * (1.0 + g * (1.0 - sg)))).astype(BF16)
                o_ref[:, us] = (acc * (g * sg)).astype(BF16)
            else:
                x = h_ref[...].astype(F32)
                r = lax.rsqrt(jnp.mean(x * x, axis=-1, keepdims=True) + NORM_EPS)
                xhat = x * r
                dxh = acc * g_ref[...]
                dx = r * (dxh - xhat * jnp.mean(dxh * xhat, axis=-1, keepdims=True))
                o_ref[...] = dx + dh_ref[...] if has_dh else dx

                @pl.when(pl.program_id(0) == 0)
                def _():
                    dg_ref[...] = jnp.zeros_like(dg_ref)

                dg_ref[...] += jnp.sum(acc * xhat, axis=0, keepdims=True)

    sem = ("arbitrary",) if mode == "norm" else ("parallel",)
    return _call(body, name=name, grid=(M // tm,), in_specs=specs, out_specs=out_specs, out_shape=out_shape, sem=sem)(*ins)


def _mm_tn(x, dy, *, S, name, tk=2048, kk=None, bias=False):
    M, K = x.shape
    N = dy.shape[1]
    Ns = N // S
    kk = kk or K
    tk = min(tk, M)
    nl = M // tk
    in_specs = [pl.BlockSpec((tk, kk), lambda s, kc, l: (l, kc)), pl.BlockSpec((tk, Ns), lambda s, kc, l: (l, s))]
    out_shape = [jax.ShapeDtypeStruct((S, K, Ns), BF16)]
    out_specs = [pl.BlockSpec((None, kk, Ns), lambda s, kc, l: (s, kc, 0))]
    if bias:
        out_shape.append(jax.ShapeDtypeStruct((1, N), F32))
        out_specs.append(pl.BlockSpec((1, Ns), lambda s, kc, l: (0, s)))

    def body(x_ref, dy_ref, o_ref, *rest):
        acc_ref = rest[-1]
        kc, l = pl.program_id(1), pl.program_id(2)

        @pl.when(l == 0)
        def _():
            acc_ref[...] = jnp.zeros_like(acc_ref)

        acc_ref[...] += _dot(x_ref[...].astype(BF16), dy_ref[...].astype(BF16), TN)
        if bias:
            b_ref = rest[0]

            @pl.when((kc == 0) & (l == 0))
            def _():
                b_ref[...] = jnp.zeros_like(b_ref)

            @pl.when(kc == 0)
            def _():
                b_ref[...] += jnp.sum(dy_ref[...].astype(F32), axis=0, keepdims=True)

        @pl.when(l == nl - 1)
        def _():
            o_ref[...] = acc_ref[...].astype(BF16)

    return _call(body, name=name, grid=(S, K // kk, nl), in_specs=in_specs, out_specs=out_specs, out_shape=out_shape,
                 scratch=[pltpu.VMEM((kk, Ns), F32)], sem=("arbitrary", "arbitrary", "arbitrary"))(x, dy)


def _band_bias(max_dist, has_prev):
    rows = lax.broadcasted_iota(jnp.int32, (BLK, 2 * BLK), 0)
    cols = lax.broadcasted_iota(jnp.int32, (BLK, 2 * BLK), 1)
    dist = rows - cols + BLK
    ok = (dist >= 0) & (dist <= max_dist) & ((cols >= BLK) | has_prev)
    return jnp.where(ok, 0.0, NEG)


Q_SCALE = HEAD_DIM ** -0.5


def _band_fwd(qa, ka, va, *, d, nq, nkv, qcol, kcol, vcol, max_dist, sinks=None, name):
    Lr = qa.shape[0]
    nb = Lr // BLK
    qw, kw, G = nq * HEAD_DIM, nkv * HEAD_DIM, nq // nkv
    cur = lambda colf, w: pl.BlockSpec((BLK, w), lambda r, i: (i, colf(r)))
    prv = lambda colf, w: pl.BlockSpec((BLK, w), lambda r, i: (jnp.maximum(i - 1, 0), colf(r)))
    out = pl.BlockSpec((BLK, qw), lambda r, i: (i, r))
    ins, specs = [qa, ka, ka, va, va], [cur(qcol, qw), cur(kcol, kw), prv(kcol, kw), cur(vcol, kw), prv(vcol, kw)]
    has_sinks = sinks is not None
    if has_sinks:
        ins.append(sinks); specs.append(pl.BlockSpec(memory_space=pltpu.SMEM))

    def body(*refs):
        q_ref, kc_ref, kp_ref, vc_ref, vp_ref = refs[:5]
        sk_ref = refs[5] if has_sinks else None
        o_ref, lse_ref = refs[-2], refs[-1]
        bias = _band_bias(max_dist, pl.program_id(1) > 0)
        k2 = jnp.concatenate([kp_ref[...], kc_ref[...]], axis=0)
        v2 = jnp.concatenate([vp_ref[...], vc_ref[...]], axis=0)
        for h in range(nq):
            hs = slice(h * HEAD_DIM, (h + 1) * HEAD_DIM)
            ks = slice((h // G) * HEAD_DIM, (h // G + 1) * HEAD_DIM)
            s = _dot(q_ref[:, hs] * jnp.asarray(Q_SCALE, BF16), k2[:, ks], NT) + bias
            m = jnp.max(s, axis=-1, keepdims=True)
            if has_sinks:
                m = jnp.maximum(m, sk_ref[h])
            p = jnp.exp(s - m)
            l = jnp.sum(p, axis=-1, keepdims=True)
            if has_sinks:
                l = l + jnp.exp(sk_ref[h] - m)
            o_ref[:, hs] = (_dot(p.astype(BF16), v2[:, ks]) / l).astype(BF16)
            lse_ref[:, hs] = jnp.broadcast_to(m + jnp.log(l), (BLK, HEAD_DIM))

    return _call(body, name=name, grid=(d, nb), in_specs=specs, out_specs=[out, out],
                 out_shape=[jax.ShapeDtypeStruct((Lr, d * qw), BF16), jax.ShapeDtypeStruct((Lr, d * qw), F32)],
                 sem=("parallel", "parallel"))(*ins)


def _band_bwd(qa, ka, va, doa, oa, lsea, *, d, nq, nkv, qcol, kcol, vcol, docol, max_dist, sinks=None, name):
    Lr = qa.shape[0]
    nb = Lr // BLK
    qw, kw, G = nq * HEAD_DIM, nkv * HEAD_DIM, nq // nkv
    transposed = G > 1
    last = lambda i: jnp.minimum(i, nb - 1)
    cur = lambda colf, w: pl.BlockSpec((BLK, w), lambda r, i: (last(i), colf(r)))
    prv = lambda colf, w: pl.BlockSpec((BLK, w), lambda r, i: (jnp.maximum(last(i) - 1, 0), colf(r)))
    own = lambda r: r
    ins = [qa, ka, ka, va, va, doa, oa, lsea]
    specs = [cur(qcol, qw), cur(kcol, kw), prv(kcol, kw), cur(vcol, kw), prv(vcol, kw), cur(docol, qw), cur(own, qw),
             cur(own, qw)]
    has_sinks = sinks is not None
    if has_sinks:
        ins.append(sinks); specs.append(pl.BlockSpec(memory_space=pltpu.SMEM))
    out_shape = [jax.ShapeDtypeStruct((Lr, d * qw), BF16), jax.ShapeDtypeStruct((Lr, d * kw), BF16),
                 jax.ShapeDtypeStruct((Lr, d * kw), BF16)]
    behind = lambda r, i: (jnp.maximum(i - 1, 0), r)
    out_specs = [pl.BlockSpec((BLK, qw), lambda r, i: (last(i), r)), pl.BlockSpec((BLK, kw), behind),
                 pl.BlockSpec((BLK, kw), behind)]
    if has_sinks:
        out_shape.append(jax.ShapeDtypeStruct((8, 128), F32))
        out_specs.append(pl.BlockSpec((8, 128), lambda r, i: (0, 0)))

    def body(*refs):
        it = iter(refs)
        q_ref, kc_ref, kp_ref, vc_ref, vp_ref, do_ref, o_ref, ls_ref = (next(it) for _ in range(8))
        sk_ref = next(it) if has_sinks else None
        dq_ref, dk_ref, dv_ref = next(it), next(it), next(it)
        dsk_ref = next(it) if has_sinks else None
        dk_car, dv_car = next(it), next(it)
        r_id, i = pl.program_id(0), pl.program_id(1)

        @pl.when(i == 0)
        def _():
            dk_car[...] = jnp.zeros_like(dk_car)
            dv_car[...] = jnp.zeros_like(dv_car)

        if has_sinks:
            @pl.when((r_id == 0) & (i == 0))
            def _():
                dsk_ref[...] = jnp.zeros_like(dsk_ref)

        @pl.when(i == nb)
        def _():
            dk_ref[...] = dk_car[...].astype(BF16)
            dv_ref[...] = dv_car[...].astype(BF16)

        @pl.when(i < nb)
        def _():
            bias = _band_bias(max_dist, i > 0)
            k2 = jnp.concatenate([kp_ref[...], kc_ref[...]], axis=0)
            v2 = jnp.concatenate([vp_ref[...], vc_ref[...]], axis=0)
            if has_sinks:
                lane = lax.broadcasted_iota(jnp.int32, (8, 128), 1)
                dsk = jnp.zeros((8, 128), F32)
            for kv in range(nkv):
                ks = slice(kv * HEAD_DIM, (kv + 1) * HEAD_DIM)
                kh, vh = k2[:, ks], v2[:, ks]
                shape = (HEAD_DIM, 2 * BLK) if transposed else (2 * BLK, HEAD_DIM)
                dk, dv = jnp.zeros(shape, F32), jnp.zeros(shape, F32)
                for g in range(G):
                    h = kv * G + g
                    hs = slice(h * HEAD_DIM, (h + 1) * HEAD_DIM)
                    q = q_ref[:, hs] * jnp.asarray(Q_SCALE, BF16)
                    do = do_ref[:, hs]
                    lse = ls_ref[:, h * HEAD_DIM:h * HEAD_DIM + 1]
                    dl = jnp.sum(do.astype(F32) * o_ref[:, hs].astype(F32), axis=-1, keepdims=True)
                    p = jnp.exp(_dot(q, kh, NT) + bias - lse)
                    ds = (p * (_dot(do, vh, NT) - dl)).astype(BF16)
                    dq_ref[:, hs] = (_dot(ds, kh) * Q_SCALE).astype(BF16)
                    if transposed:
                        dk = dk + _dot(q, ds, TN)
                        dv = dv + _dot(do, p.astype(BF16), TN)
                    else:
                        dk = dk + _dot(ds, q, TN)
                        dv = dv + _dot(p.astype(BF16), do, TN)
                    if has_sinks:
                        val = -jnp.sum(jnp.exp(sk_ref[h] - lse) * dl, axis=0, keepdims=True)
                        dsk = dsk + jnp.where(lane == h, val, 0.0)
                if transposed:
                    dk, dv = dk.T, dv.T
                dk_ref[:, ks] = (dk_car[:, ks] + dk[:BLK]).astype(BF16)
                dv_ref[:, ks] = (dv_car[:, ks] + dv[:BLK]).astype(BF16)
                dk_car[:, ks] = dk[BLK:]
                dv_car[:, ks] = dv[BLK:]
            if has_sinks:
                dsk_ref[...] += dsk

    return _call(body, name=name, grid=(d, nb + 1), in_specs=specs, out_specs=out_specs, out_shape=out_shape,
                 scratch=[pltpu.VMEM((BLK, kw), F32), pltpu.VMEM((BLK, kw), F32)], sem=("arbitrary", "arbitrary"))(*ins)


def _attn_grad_combine(branches, tabs, *, name, tm=512):
    L, qw = branches[0][0].shape
    kw = branches[0][1].shape[1]
    nbr = len(branches)
    row = lambda w: pl.BlockSpec((tm, w), lambda i: (i, 0))
    ins, specs = [], []
    for dq, dk, dv in branches:
        ins += [dq, dk, dv]; specs += [row(qw), row(kw), row(kw)]
    ins += list(tabs); specs += [row(128)] * 3

    def body(*refs):
        c, s1, s2 = (t[...] for t in refs[3 * nbr:3 * nbr + 3])
        o_ref = refs[-1]
        for part, (w, off, rot) in enumerate(((qw, 0, True), (kw, qw, True), (kw, qw + kw, False))):
            for cb in range(w // 128):
                cs = slice(cb * 128, (cb + 1) * 128)
                v = refs[part][:, cs].astype(F32)
                for b in range(1, nbr):
                    v = v + refs[3 * b + part][:, cs].astype(F32)
                if rot:
                    v = _rope_bwd(v, c, s1, s2)
                o_ref[:, off + cb * 128:off + (cb + 1) * 128] = v.astype(BF16)

    return _call(body, name=name, grid=(L // tm,), in_specs=specs, out_specs=row(qw + 2 * kw),
                 out_shape=jax.ShapeDtypeStruct((L, qw + 2 * kw), BF16), sem=("parallel",))(*ins)


def _xattn_fwd(q, kv, *, name, tq=1024):
    L, W = q.shape
    scale = XA_HEAD_DIM ** -0.5
    row = pl.BlockSpec((tq, W), lambda i: (i, 0))
    kvs = pl.BlockSpec((N_MEM, 2 * W), lambda i: (0, 0))

    def body(q_ref, kv_ref, o_ref, lse_ref):
        for h in range(XA_HEADS):
            hs = slice(h * XA_HEAD_DIM, (h + 1) * XA_HEAD_DIM)
            vs = slice(W + h * XA_HEAD_DIM, W + (h + 1) * XA_HEAD_DIM)
            s = _dot(q_ref[:, hs], kv_ref[:, hs], NT) * scale
            m = jnp.max(s, axis=-1, keepdims=True)
            p = jnp.exp(s - m)
            l = jnp.sum(p, axis=-1, keepdims=True)
            o_ref[:, hs] = (_dot(p.astype(BF16), kv_ref[:, vs]) / l).astype(BF16)
            lse_ref[:, hs] = jnp.broadcast_to(m + jnp.log(l), (tq, XA_HEAD_DIM))

    return _call(body, name=name, grid=(L // tq,), in_specs=[row, kvs], out_specs=[row, row],
                 out_shape=[jax.ShapeDtypeStruct((L, W), BF16), jax.ShapeDtypeStruct((L, W), F32)], sem=("parallel",))(q, kv)


def _xattn_bwd(q, kv, o, lse, do, *, name, tq=1024):
    L, W = q.shape
    scale = XA_HEAD_DIM ** -0.5
    row = pl.BlockSpec((tq, W), lambda i: (i, 0))
    kvs = pl.BlockSpec((N_MEM, 2 * W), lambda i: (0, 0))

    def body(q_ref, kv_ref, o_ref, lse_ref, do_ref, dq_ref, dkv_ref):
        @pl.when(pl.program_id(0) == 0)
        def _():
            dkv_ref[...] = jnp.zeros_like(dkv_ref)

        for h in range(XA_HEADS):
            hs = slice(h * XA_HEAD_DIM, (h + 1) * XA_HEAD_DIM)
            vs = slice(W + h * XA_HEAD_DIM, W + (h + 1) * XA_HEAD_DIM)
            qh, kh, vh, doh = q_ref[:, hs], kv_ref[:, hs], kv_ref[:, vs], do_ref[:, hs]
            p = jnp.exp(_dot(qh, kh, NT) * scale - lse_ref[:, h * XA_HEAD_DIM:h * XA_HEAD_DIM + 1])
            dl = jnp.sum(doh.astype(F32) * o_ref[:, hs].astype(F32), axis=-1, keepdims=True)
            ds = (p * (_dot(doh, vh, NT) - dl) * scale).astype(BF16)
            dq_ref[:, hs] = _dot(ds, kh).astype(BF16)
            dkv_ref[:, hs] += _dot(ds, qh, TN)
            dkv_ref[:, vs] += _dot(p.astype(BF16), doh, TN)

    return _call(body, name=name, grid=(L // tq,), in_specs=[row, kvs, row, row, row], out_specs=[row, kvs],
                 out_shape=[jax.ShapeDtypeStruct((L, W), BF16), jax.ShapeDtypeStruct((N_MEM, 2 * W), F32)],
                 sem=("arbitrary",))(q, kv, o, lse, do)


def _neg_expm1(z):
    series = -(z * (1.0 + z * (0.5 + z * (1.0 / 6.0 + z * (1.0 / 24.0 + z * (1.0 / 120.0))))))
    return jnp.where(z > -0.05, series, 1.0 - jnp.exp(z))


def _softplus(z):
    return jnp.maximum(z, 0.0) + jnp.log(1.0 + jnp.exp(-jnp.abs(z)))


def _gelu_parts(y):
    c = 0.7978845608028654
    t = jnp.tanh(c * (y + 0.044715 * y * y * y))
    gy = 0.5 * y * (1.0 + t)
    dgy = 0.5 * (1.0 + t) + 0.5 * y * (1.0 - t * t) * c * (1.0 + 3.0 * 0.044715 * y * y)
    return gy, dgy


def _lru_gates(xc, wa_ref, ba, wx_ref, bx, sp):
    rs, igs = [], []
    for hd in range(LRU_HEADS):
        sl = slice(hd * LRU_HEAD_DIM, (hd + 1) * LRU_HEAD_DIM)
        xh = xc[:, sl].astype(BF16)
        rs.append(_sigmoid(_dot(xh, wa_ref[hd]) + ba[:, sl]))
        igs.append(_sigmoid(_dot(xh, wx_ref[hd]) + bx[:, sl]))
    r, ig = jnp.concatenate(rs, axis=1), jnp.concatenate(igs, axis=1)
    la = -LRU_C * r * sp
    return r, ig, jnp.exp(la), _neg_expm1(2.0 * la)


def _conv_taps(x_ext, halo):
    n = x_ext.shape[0]
    return [x_ext[halo:] if k == CONV_WIDTH - 1 else pltpu.roll(x_ext, CONV_WIDTH - 1 - k, 0)[halo:]
            for k in range(CONV_WIDTH)]


def _lru_fwd(proj, cw, cb, wa, ba, wx, bx, lam, *, name, tc=512):
    L = proj.shape[0]
    W = LRU_HEADS * LRU_HEAD_DIM
    nb = L // tc
    whole = lambda shape: pl.BlockSpec(shape, lambda i: (0,) * len(shape))
    specs = [pl.BlockSpec((tc, W), lambda i: (i, 0)), pl.BlockSpec((tc, W), lambda i: (i, 1)),
             pl.BlockSpec((16, W), lambda i: (jnp.maximum(i * (tc // 16) - 1, 0), 0)),
             whole((CONV_WIDTH, W)), whole((1, W)), whole((LRU_HEADS, LRU_HEAD_DIM, LRU_HEAD_DIM)), whole((1, W)),
             whole((LRU_HEADS, LRU_HEAD_DIM, LRU_HEAD_DIM)), whole((1, W)), whole((1, W))]
    out_specs = [pl.BlockSpec((tc, W), lambda i: (i, 0))] * 2
    out_shape = [jax.ShapeDtypeStruct((L, W), BF16), jax.ShapeDtypeStruct((L, W), F32)]

    def body(x_ref, y_ref, xh_ref, cw_ref, cb_ref, wa_ref, ba_ref, wx_ref, bx_ref, lam_ref, rec_ref, hs_ref,
             hcar, a_scr, b_scr):
        i = pl.program_id(0)

        @pl.when(i == 0)
        def _():
            hcar[...] = jnp.zeros_like(hcar)

        halo = jnp.where(i > 0, xh_ref[...].astype(F32), 0.0)
        taps = _conv_taps(jnp.concatenate([halo, x_ref[...].astype(F32)], axis=0), 16)
        xc = cb_ref[...] + sum(cw_ref[k:k + 1, :] * taps[k] for k in range(CONV_WIDTH))
        _, ig, a, om = _lru_gates(xc, wa_ref, ba_ref[...], wx_ref, bx_ref[...], _softplus(-lam_ref[...]))
        b = jnp.sqrt(om) * (ig * xc)
        rowmod = lax.broadcasted_iota(jnp.int32, (tc, W), 0) & 7
        for s in (1, 2, 4):
            keep = rowmod >= s
            b = jnp.where(keep, a * pltpu.roll(b, s, 0) + b, b)
            a = jnp.where(keep, a * pltpu.roll(a, s, 0), a)
        a_scr[...] = a
        b_scr[...] = b

        def tile(j, hc):
            rows = pl.ds(pl.multiple_of(j * 8, 8), 8)
            ht = a_scr[rows, :] * hc + b_scr[rows, :]
            hs_ref[rows, :] = ht
            return jnp.broadcast_to(ht[7:8, :], (8, W))

        hcar[...] = lax.fori_loop(0, tc // 8, tile, hcar[...])
        gy, _ = _gelu_parts(y_ref[...].astype(F32))
        rec_ref[...] = (hs_ref[...] * gy).astype(BF16)

    return _call(body, name=name, grid=(nb,), in_specs=specs, out_specs=out_specs, out_shape=out_shape,
                 scratch=[pltpu.VMEM((8, W), F32), pltpu.VMEM((tc, W), F32), pltpu.VMEM((tc, W), F32)],
                 sem=("arbitrary",))(proj, proj, proj, cw, cb, wa, ba, wx, bx, lam)


def _lru_bwd(proj, hs, drec_src, cw, cb, wa, ba, wx, bx, lam, *, name, tc=256):
    L = proj.shape[0]
    W = LRU_HEADS * LRU_HEAD_DIM
    nb = L // tc
    tb = lambda i: nb - 1 - i
    whole = lambda shape: pl.BlockSpec(shape, lambda i: (0,) * len(shape))
    gate_w = (LRU_HEADS, LRU_HEAD_DIM, LRU_HEAD_DIM)
    specs = [pl.BlockSpec((tc, W), lambda i: (tb(i), 0)), pl.BlockSpec((tc, W), lambda i: (tb(i), 1)),
             pl.BlockSpec((16, W), lambda i: (jnp.maximum(tb(i) * (tc // 16) - 1, 0), 0)),
             pl.BlockSpec((tc, W), lambda i: (tb(i), 0)),
             pl.BlockSpec((8, W), lambda i: (jnp.maximum(tb(i) * (tc // 8) - 1, 0), 0)),
             pl.BlockSpec((tc, W), lambda i: (tb(i), 0)),
             whole((CONV_WIDTH, W)), whole((1, W)), whole(gate_w), whole((1, W)), whole(gate_w), whole((1, W)), whole((1, W))]
    out_specs = [pl.BlockSpec((tc, 2 * W), lambda i: (tb(i), 0)), whole((CONV_WIDTH, W)), whole((1, W)), whole(gate_w),
                 whole((1, W)), whole(gate_w), whole((1, W)), whole((1, W))]
    vec = jax.ShapeDtypeStruct((1, W), F32)
    out_shape = [jax.ShapeDtypeStruct((L, 2 * W), BF16), jax.ShapeDtypeStruct((CONV_WIDTH, W), F32), vec,
                 jax.ShapeDtypeStruct(gate_w, F32), vec, jax.ShapeDtypeStruct(gate_w, F32), vec, vec]

    def body(x_ref, y_ref, xh_ref, hs_ref, hh_ref, dr_ref, cw_ref, cb_ref, wa_ref, ba_ref, wx_ref, bx_ref, lam_ref,
             dxy_ref, dcw_ref, dcb_ref, dwa_ref, dba_ref, dwx_ref, dbx_ref, dlam_ref, gcar, dxc_car, a_scr, b_scr, g_scr):
        pid = pl.program_id(0)
        t = tb(pid)
        accs = (dcw_ref, dcb_ref, dwa_ref, dba_ref, dwx_ref, dbx_ref, dlam_ref)

        @pl.when(pid == 0)
        def _():
            gcar[...] = jnp.zeros_like(gcar)
            dxc_car[...] = jnp.zeros_like(dxc_car)
            for r in accs:
                r[...] = jnp.zeros_like(r)

        halo = jnp.where(t > 0, xh_ref[...].astype(F32), 0.0)
        taps = _conv_taps(jnp.concatenate([halo, x_ref[...].astype(F32)], axis=0), 16)
        xc = cb_ref[...] + sum(cw_ref[k:k + 1, :] * taps[k] for k in range(CONV_WIDTH))
        lam = lam_ref[...]
        sp = _softplus(-lam)
        r, ig, a, om = _lru_gates(xc, wa_ref, ba_ref[...], wx_ref, bx_ref[...], sp)
        sq = jnp.sqrt(om)
        hblk = hs_ref[...]
        hprev = pltpu.roll(jnp.concatenate([jnp.where(t > 0, hh_ref[...], 0.0), hblk], axis=0), 1, 0)[8:]
        gy, dgy = _gelu_parts(y_ref[...].astype(F32))
        drec = dr_ref[...].astype(F32)
        dxy_ref[:, W:] = (drec * hblk * dgy).astype(BF16)

        rowidx = lax.broadcasted_iota(jnp.int32, (tc, W), 0)
        rowmod = rowidx & 7
        ca = jnp.where(rowidx == tc - 1, 1.0, pltpu.roll(a, tc - 1, 0))
        cbv = drec * gy
        for s in (1, 2, 4):
            keep = rowmod < 8 - s
            cbv = jnp.where(keep, ca * pltpu.roll(cbv, tc - s, 0) + cbv, cbv)
            ca = jnp.where(keep, ca * pltpu.roll(ca, tc - s, 0), ca)
        a_scr[...] = ca
        b_scr[...] = cbv

        def tile(k, gc):
            j = tc // 8 - 1 - k
            rows = pl.ds(pl.multiple_of(j * 8, 8), 8)
            gt = a_scr[rows, :] * gc + b_scr[rows, :]
            g_scr[rows, :] = gt
            return jnp.broadcast_to(gt[0:1, :], (8, W))

        lax.fori_loop(0, tc // 8, tile, gcar[...])
        G = g_scr[...]
        gcar[...] = jnp.broadcast_to(a[0:1, :] * G[0:1, :], (8, W))

        da = G * hprev
        dsq = G * (ig * xc)
        di = G * (sq * xc)
        dxc = G * (sq * ig)
        dla = da * a - 2.0 * a * a * (dsq * 0.5 * lax.rsqrt(om))
        dlam_ref[...] += jnp.sum(dla * (-LRU_C * r), axis=0, keepdims=True) * (-_sigmoid(-lam))
        dpr = dla * (-LRU_C * sp) * r * (1.0 - r)
        dpi = di * ig * (1.0 - ig)
        dba_ref[...] += jnp.sum(dpr, axis=0, keepdims=True)
        dbx_ref[...] += jnp.sum(dpi, axis=0, keepdims=True)
        back = []
        for hd in range(LRU_HEADS):
            sl = slice(hd * LRU_HEAD_DIM, (hd + 1) * LRU_HEAD_DIM)
            xh, dprh, dpih = xc[:, sl].astype(BF16), dpr[:, sl].astype(BF16), dpi[:, sl].astype(BF16)
            back.append(_dot(dprh, wa_ref[hd], NT) + _dot(dpih, wx_ref[hd], NT))
            dwa_ref[hd] += _dot(xh, dprh, TN)
            dwx_ref[hd] += _dot(xh, dpih, TN)
        dxc = dxc + jnp.concatenate(back, axis=1)
        dcb_ref[...] += jnp.sum(dxc, axis=0, keepdims=True)
        for k in range(CONV_WIDTH):
            dcw_ref[k:k + 1, :] += jnp.sum(dxc * taps[k], axis=0, keepdims=True)
        ext = jnp.concatenate([dxc, dxc_car[...]], axis=0)
        dx = cw_ref[CONV_WIDTH - 1:CONV_WIDTH, :] * dxc
        for k in range(CONV_WIDTH - 1):
            dx = dx + cw_ref[k:k + 1, :] * pltpu.roll(ext, tc + 8 - (CONV_WIDTH - 1 - k), 0)[:tc]
        dxc_car[...] = dxc[0:8, :]
        dxy_ref[:, :W] = dx.astype(BF16)

    scratch = [pltpu.VMEM((8, W), F32), pltpu.VMEM((8, W), F32)] + [pltpu.VMEM((tc, W), F32)] * 3
    return _call(body, name=name, grid=(nb,), in_specs=specs, out_specs=out_specs, out_shape=out_shape, scratch=scratch,
                 sem=("arbitrary",))(proj, proj, proj, hs, hs, drec_src, cw, cb, wa, ba, wx, bx, lam)


def _final_loss(h, gain, target, *, name, tm=512):
    M, K = h.shape
    row = pl.BlockSpec((tm, K), lambda i: (i, 0))
    vec = pl.BlockSpec((1, K), lambda i: (0, 0))
    one = pl.BlockSpec((1, 128), lambda i: (0, 0))

    def body(h_ref, g_ref, t_ref, dh_ref, dg_ref, loss_ref):
        @pl.when(pl.program_id(0) == 0)
        def _():
            dg_ref[...] = jnp.zeros_like(dg_ref)
            loss_ref[...] = jnp.zeros_like(loss_ref)

        x = h_ref[...]
        r = lax.rsqrt(jnp.mean(x * x, axis=-1, keepdims=True) + NORM_EPS)
        xhat = x * r
        err = xhat * g_ref[...] - t_ref[...]
        loss_ref[...] += 0.5 / K * jnp.sum(err * err)
        dy = err * (1.0 / K)
        dg_ref[...] += jnp.sum(dy * xhat, axis=0, keepdims=True)
        dxh = dy * g_ref[...]
        dh_ref[...] = r * (dxh - xhat * jnp.mean(dxh * xhat, axis=-1, keepdims=True))

    return _call(body, name=name, grid=(M // tm,), in_specs=[row, vec, row], out_specs=[row, vec, one],
                 out_shape=[jax.ShapeDtypeStruct((M, K), F32), jax.ShapeDtypeStruct((1, K), F32),
                            jax.ShapeDtypeStruct((1, 128), F32)], sem=("arbitrary",))(h, gain.reshape(1, K), target)


def _dilated_merge(branches, *, name, tm=512):
    L, W = branches[0].shape
    nbr = len(branches) // 2
    row = pl.BlockSpec((tm, W), lambda i: (i, 0))

    def body(*refs):
        o_ref, lse_ref = refs[-2], refs[-1]
        lses = [refs[2 * b + 1][...] for b in range(nbr)]
        m = lses[0]
        for t in lses[1:]:
            m = jnp.maximum(m, t)
        ws = [jnp.exp(t - m) for t in lses]
        den = ws[0]
        for t in ws[1:]:
            den = den + t
        acc = ws[0] * refs[0][...].astype(F32)
        for b in range(1, nbr):
            acc = acc + ws[b] * refs[2 * b][...].astype(F32)
        o_ref[...] = (acc / den).astype(BF16)
        lse_ref[...] = m + jnp.log(den)

    return _call(body, name=name, grid=(L // tm,), in_specs=[row] * (2 * nbr), out_specs=[row, row],
                 out_shape=[jax.ShapeDtypeStruct((L, W), BF16), jax.ShapeDtypeStruct((L, W), F32)], sem=("parallel",))(*branches)


def _dilated_fwd(proj0):
    L = proj0.shape[0]
    qkv = proj0[:, 2 * D_MODEL:]
    W = B_HEADS * HEAD_DIM
    outs = []
    for window, d in DILATED_PATTERN:
        view = qkv.reshape(L // d, d * 3 * W)
        o, lse = _band_fwd(view, view, view, d=d, nq=B_HEADS, nkv=B_HEADS, qcol=lambda r: 3 * r, kcol=lambda r: 3 * r + 1,
                           vcol=lambda r: 3 * r + 2, max_dist=window // d, name=f"dilated_fwd_d{d}")
        outs += [o.reshape(L, W), lse.reshape(L, W)]
    return _dilated_merge(outs, name="dilated_merge")


def _dilated_bwd(proj0, att, lse, datt, tabs):
    L = proj0.shape[0]
    qkv = proj0[:, 2 * D_MODEL:]
    Wh = B_HEADS * HEAD_DIM
    branches = []
    for window, d in DILATED_PATTERN:
        view = qkv.reshape(L // d, d * 3 * Wh)
        v1 = lambda t: t.reshape(L // d, d * Wh)
        outs = _band_bwd(view, view, view, v1(datt), v1(att), v1(lse), d=d, nq=B_HEADS, nkv=B_HEADS,
                         qcol=lambda r: 3 * r, kcol=lambda r: 3 * r + 1, vcol=lambda r: 3 * r + 2, docol=lambda r: r,
                         max_dist=window // d, name=f"dilated_bwd_d{d}")
        branches.append([o.reshape(L, Wh) for o in outs])
    return _attn_grad_combine(branches, tabs, name="dilated_grad_combine")


def _device_step(x, mem, target, w, on_grads=None):
    L = x.shape[0]
    tabs = _rope_tables(L)
    g = {}
    saved = []
    h = x
    for layer in range(2):
        sv = {"h_mix": h}
        if layer == 0:
            proj, n = _rowmm(h, w["ab_w_in"], name="l0_in_proj", gain=w["mix_norm"][0],
                             rope=(2 * D_MODEL, 2 * D_MODEL + 2 * B_HEADS * HEAD_DIM, tabs))
            rec, hs = _lru_fwd(proj, w["lru_conv_w"], w["lru_conv_b"], w["lru_wa"], w["lru_ba"], w["lru_wx"], w["lru_bx"],
                               w["lru_lambda"], name="lru_fwd")
            att, lse = _dilated_fwd(proj)
            mix = jnp.concatenate([rec, att], axis=1)
            (h,) = _rowmm(mix, w["ab_w_out"], name="l0_out_proj", res=h)
            sv.update(hs=hs)
        else:
            proj, n = _rowmm(h, w["c_w_qkv"], name="l1_qkv_proj", gain=w["mix_norm"][1], bias=w["c_b_qkv"],
                             rope=(0, (C_HEADS + C_KV_HEADS) * HEAD_DIM, tabs))
            mix, lse = _band_fwd(proj, proj, proj, d=1, nq=C_HEADS, nkv=C_KV_HEADS, qcol=lambda r: 0, kcol=lambda r: 8,
                                 vcol=lambda r: 9, max_dist=C_WINDOW - 1, sinks=w["c_sinks"], name="swa_fwd")
            (h,) = _rowmm(mix, w["c_w_out"], name="l1_out_proj", res=h, bias=w["c_b_out"])
        sv.update(proj=proj, n_mix=n, mix=mix, lse=lse, h_xa=h)
        xq, nx = _rowmm(h, w["xa_wq"][layer][None], name=f"xa_q_proj{layer}", gain=w["xa_norm"][layer])
        kv, nm = _rowmm(mem, w["xa_wkv"][layer][None], name=f"xa_kv_proj{layer}", gain=w["xa_mem_norm"][layer])
        xo, xlse = _xattn_fwd(xq, kv, name=f"xa_fwd{layer}")
        (h,) = _rowmm(xo, w["xa_wo"][layer], name=f"xa_out_proj{layer}", res=h)
        sv.update(xq=xq, nx=nx, kv=kv, nm=nm, xo=xo, xlse=xlse, h_ffn=h)
        gu, nf, act = _rowmm(h, w["ffn_w_gate_up"], layer=layer, name=f"ffn_in{layer}", gain=w["ffn_norm"][layer], swiglu=True)
        (h,) = _rowmm(act, w["ffn_w_down"][layer][None], name=f"ffn_out{layer}", res=h, tm=512)
        sv.update(gu=gu, nf=nf, act=act)
        saved.append(sv)

    dh, g["final_norm"], loss = _final_loss(h, w["final_norm"], target, name="final_loss")

    stk = {k: [None, None] for k in ("xa_norm", "xa_mem_norm", "ffn_norm", "mix_norm")}
    after = None
    for layer in (1, 0):
        sv = saved[layer]
        (g["ffn_w_down", layer],) = _mm_tn(sv["act"], dh, S=1, name=f"ffn_down_dw{layer}", kk=D_FF // 2)
        (dgu,) = _mm_nt(dh, w["ffn_w_down"][layer][None], name=f"ffn_dact{layer}", mode="swiglu", kchunk=D_FF // 2, gu=sv["gu"],
                        after=after)
        (g["ffn_w_gate_up", layer],) = _mm_tn(sv["nf"], dgu, S=N_CHIPS, name=f"ffn_gu_dw{layer}")
        dh, stk["ffn_norm"][layer] = _mm_nt(dgu, w["ffn_w_gate_up"], layer=layer, name=f"ffn_dx{layer}", mode="norm",
                                            h=sv["h_ffn"], gain=w["ffn_norm"][layer], dh=dh)
        (g["xa_wo", layer],) = _mm_tn(sv["xo"], dh, S=N_CHIPS, name=f"xa_wo_dw{layer}")
        (dxo,) = _mm_nt(dh, w["xa_wo"][layer], name=f"xa_dxo{layer}", mode="plain")
        dxq, dkv = _xattn_bwd(sv["xq"], sv["kv"], sv["xo"], sv["xlse"], dxo, name=f"xa_bwd{layer}")
        (g["xa_wq", layer],) = _mm_tn(sv["nx"], dxq, S=1, name=f"xa_wq_dw{layer}")
        dh, stk["xa_norm"][layer] = _mm_nt(dxq, w["xa_wq"][layer][None], name=f"xa_dx{layer}", mode="norm", h=sv["h_xa"],
                                           gain=w["xa_norm"][layer], dh=dh)
        (g["xa_wkv", layer],) = _mm_tn(sv["nm"], dkv, S=1, name=f"xa_wkv_dw{layer}")
        _, stk["xa_mem_norm"][layer] = _mm_nt(dkv, w["xa_wkv"][layer][None], name=f"xa_dmem{layer}", mode="norm", h=mem,
                                              gain=w["xa_mem_norm"][layer])
        if layer == 1:
            g["c_w_out"], g["c_b_out"] = _mm_tn(sv["mix"], dh, S=1, name="l1_out_dw", bias=True)
            (dmix,) = _mm_nt(dh, w["c_w_out"], name="l1_dmix", mode="plain")
            dq, dk, dv, dsk = _band_bwd(sv["proj"], sv["proj"], sv["proj"], dmix, sv["mix"], sv["lse"], d=1, nq=C_HEADS,
                                        nkv=C_KV_HEADS, qcol=lambda r: 0, kcol=lambda r: 8, vcol=lambda r: 9,
                                        docol=lambda r: 0, max_dist=C_WINDOW - 1, sinks=w["c_sinks"], name="swa_bwd")
            g["c_sinks"] = dsk[0, :C_HEADS]
            dproj = _attn_grad_combine([(dq, dk, dv)], tabs, name="swa_grad_combine")
            g["c_w_qkv"], g["c_b_qkv"] = _mm_tn(sv["n_mix"], dproj, S=1, name="l1_qkv_dw", bias=True)
            dh, stk["mix_norm"][1] = _mm_nt(dproj, w["c_w_qkv"], name="l1_dx", mode="norm", h=sv["h_mix"],
                                            gain=w["mix_norm"][1], dh=dh)
            if on_grads is not None:
                after = on_grads("layer1", g)
        else:
            if on_grads is not None:
                after = on_grads("layer0_ffn_xa", g)
            (g["ab_w_out"],) = _mm_tn(sv["mix"], dh, S=1, name="l0_out_dw", kk=768)
            (dmix,) = _mm_nt(dh, w["ab_w_out"], name="l0_dmix", mode="plain", kchunk=768, after=after)
            (dxy, g["lru_conv_w"], g["lru_conv_b"], g["lru_wa"], g["lru_ba"], g["lru_wx"], g["lru_bx"],
             g["lru_lambda"]) = _lru_bwd(sv["proj"], sv["hs"], dmix, w["lru_conv_w"], w["lru_conv_b"], w["lru_wa"],
                                         w["lru_ba"], w["lru_wx"], w["lru_bx"], w["lru_lambda"], name="lru_bwd")
            dqkv = _dilated_bwd(sv["proj"], sv["mix"][:, D_MODEL:], sv["lse"], dmix[:, D_MODEL:], tabs)
            dproj = jnp.concatenate([dxy, dqkv], axis=1)
            (g["ab_w_in"],) = _mm_tn(sv["n_mix"], dproj, S=N_CHIPS, name="l0_in_dw")
            dh, stk["mix_norm"][0] = _mm_nt(dproj, w["ab_w_in"], name="l0_dx", mode="norm", h=sv["h_mix"],
                                            gain=w["mix_norm"][0], dh=dh)
    for k, v in stk.items():
        g[k] = jnp.concatenate(v, axis=0)
    return loss[0, 0], dh, g


ANY = pl.BlockSpec(memory_space=pl.ANY)
MESH = pl.DeviceIdType.MESH


def _place():
    x, y, c = lax.axis_index("x"), lax.axis_index("y"), lax.axis_index("c")
    return x, y, c, [(1 - x, y), (x, 1 - y), (1 - x, 1 - y)]


def _remote(send_sems, recv_sems):
    def copy(k, src, dst, to):
        return pltpu.make_async_remote_copy(src_ref=src, dst_ref=dst, send_sem=send_sems.at[k], recv_sem=recv_sems.at[k],
                                            device_id=to, device_id_type=MESH)
    return copy


def _halves(ref, n_rows):
    rh = n_rows // 2
    return lambda lead, hh: ref.at[(*lead, pl.ds(hh * rh, rh), slice(None))]


def _gather_weights(packs, spack):
    n = len(packs)

    def body(*refs):
        w_refs, s_ref, wf_refs, sf_ref = refs[:n], refs[n], refs[n + 1:2 * n + 1], refs[2 * n + 1]
        x, y, c, chips = _place()
        me, sib = 2 * x + y, (x, y, 1 - c)
        copy = _remote(*refs[-2:])
        src = [_halves(w_refs[g], packs[g].shape[0]) for g in range(n)]
        dst = [_halves(wf_refs[g], packs[g].shape[0]) for g in range(n)]
        sends = []
        for g in range(n):
            for j, (cx, cy) in enumerate(chips):
                sends.append(copy(3 * g + j, src[g]((), c), dst[g]((me,), c), (cx, cy, c)))
        for j, (cx, cy) in enumerate(chips):
            sends.append(copy(6 * n + j, s_ref, sf_ref.at[me], (cx, cy, c)))
        for cp in sends:
            cp.start()
        for g in range(n):
            for j, (cx, cy) in enumerate(chips):
                got = dst[g]((2 * cx + cy,), c)
                copy(3 * g + j, got, got, sib).wait_recv()
                fwd = copy(3 * n + 3 * g + j, got, got, sib)
                fwd.start()
                sends.append(fwd)
        for g in range(n):
            for j, (cx, cy) in enumerate(chips):
                got = dst[g]((2 * cx + cy,), 1 - c)
                copy(3 * n + 3 * g + j, got, got, sib).wait_recv()
        for j, (cx, cy) in enumerate(chips):
            copy(6 * n + j, s_ref, sf_ref.at[2 * cx + cy], sib).wait_recv()
        for cp in sends:
            cp.wait_send()

    ins = list(packs) + [spack]
    out_shape = [jax.ShapeDtypeStruct((N_CHIPS,) + a.shape, a.dtype) for a in ins]
    n_sems = 6 * n + 3
    outs = pl.pallas_call(body, name="gather_weights", out_shape=out_shape, in_specs=[ANY] * len(ins),
                          out_specs=[ANY] * len(ins),
                          scratch_shapes=[pltpu.SemaphoreType.DMA((n_sems,)), pltpu.SemaphoreType.DMA((n_sems,))])(*ins)
    chip = 2 * lax.axis_index("x") + lax.axis_index("y")
    outs = [lax.dynamic_update_index_in_dim(o, a, chip, 0) for o, a in zip(outs, ins)]
    return outs[:n], outs[n]


SEQUENCER_GATHER_IDS = {"mid": 1, "late": 5}


def _gather_weights_behind(packs, *, tag):
    n = len(packs)

    def body(*refs):
        w_refs, wf_refs = refs[:n], refs[n:2 * n]
        x, y, c, chips = _place()
        me, sib = 2 * x + y, (x, y, 1 - c)
        barrier = pltpu.get_barrier_semaphore()
        for peer in [(cx, cy, c) for cx, cy in chips] + [sib]:
            pl.semaphore_signal(barrier, inc=1, device_id=peer, device_id_type=MESH)
        pl.semaphore_wait(barrier, len(chips) + 1)
        copy = _remote(*refs[-2:])
        src = [_halves(w_refs[g], packs[g].shape[0]) for g in range(n)]
        dst = [_halves(wf_refs[g], packs[g].shape[0]) for g in range(n)]
        sends = []
        for g in range(n):
            for j, (cx, cy) in enumerate(chips):
                sends.append(copy(3 * g + j, src[g]((), c), dst[g]((me,), c), (cx, cy, c)))
        for cp in sends:
            cp.start()
        for g in range(n):
            for j, (cx, cy) in enumerate(chips):
                got = dst[g]((2 * cx + cy,), c)
                copy(3 * g + j, got, got, sib).wait_recv()
                fwd = copy(3 * n + 3 * g + j, got, got, sib)
                fwd.start()
                sends.append(fwd)
        for g in range(n):
            for j, (cx, cy) in enumerate(chips):
                got = dst[g]((2 * cx + cy,), 1 - c)
                copy(3 * n + 3 * g + j, got, got, sib).wait_recv()
        for cp in sends:
            cp.wait_send()

    out_type = [jax.ShapeDtypeStruct((N_CHIPS,) + a.shape, a.dtype) for a in packs]
    outs = pl.kernel(body, out_type=out_type, mesh=plsc.ScalarSubcoreMesh(axis_name="sequencer", num_cores=1),
                     name="gather_weights_behind_" + tag,
                     scratch_types=[pltpu.SemaphoreType.DMA((6 * n,)), pltpu.SemaphoreType.DMA((6 * n,))],
                     compiler_params=pltpu.CompilerParams(collective_id=SEQUENCER_GATHER_IDS[tag]))(*packs)
    chip = 2 * lax.axis_index("x") + lax.axis_index("y")
    return [lax.dynamic_update_index_in_dim(o, a, chip, 0) for o, a in zip(outs, packs)]


def _rs_pair_exchange(gpacks, *, name):
    n = len(gpacks)

    def body(*refs):
        g_refs, ra_refs = refs[:n], refs[n:2 * n]
        x, y, c, _ = _place()
        copy = _remote(*refs[-2:])
        cps = []
        for g in range(n):
            half = _halves(g_refs[g], gpacks[g].shape[1])
            cps += [copy(N_CHIPS * g + j, half((j,), 1 - c), ra_refs[g].at[j], (x, y, 1 - c)) for j in range(N_CHIPS)]
        for cp in cps:
            cp.start()
        for cp in cps:
            cp.wait()

    out_shape = [jax.ShapeDtypeStruct((N_CHIPS, a.shape[1] // 2, a.shape[2]), a.dtype) for a in gpacks]
    n_sems = N_CHIPS * n
    return pl.pallas_call(body, name=name, out_shape=out_shape, in_specs=[ANY] * n, out_specs=[ANY] * n,
                          scratch_shapes=[pltpu.SemaphoreType.DMA((n_sems,)), pltpu.SemaphoreType.DMA((n_sems,))])(*gpacks)


def _row_tile(rows, cap=512):
    return max(t for t in range(16, min(rows, cap) + 1, 16) if rows % t == 0)


def _rs_pair_add(place, gpack, ra, *, name):
    _, R, C = gpack.shape
    Rh = R // 2
    tr = _row_tile(Rh)
    nrb = Rh // tr

    def body(p_ref, g_ref, ra_ref, pair_ref, own_ref):
        s = g_ref[...].astype(F32) + ra_ref[...].astype(F32)
        pair_ref[...] = s.astype(BF16)

        @pl.when(pl.program_id(1) == p_ref[1])
        def _():
            own_ref[...] = s

    grid_spec = pltpu.PrefetchScalarGridSpec(
        num_scalar_prefetch=1, grid=(nrb, N_CHIPS),
        in_specs=[pl.BlockSpec((None, tr, C), lambda i, j, p: (j, p[0] * nrb + i, 0)),
                  pl.BlockSpec((None, tr, C), lambda i, j, p: (j, i, 0))],
        out_specs=[pl.BlockSpec((None, tr, C), lambda i, j, p: (j, i, 0)), pl.BlockSpec((tr, C), lambda i, j, p: (i, 0))])
    return pl.pallas_call(
        body, name=name, grid_spec=grid_spec,
        out_shape=[jax.ShapeDtypeStruct((N_CHIPS, Rh, C), BF16), jax.ShapeDtypeStruct((Rh, C), F32)],
        compiler_params=pltpu.CompilerParams(dimension_semantics=("arbitrary", "arbitrary"),
                                             vmem_limit_bytes=VMEM_LIMIT_V7X))(place, gpack, ra)


SEQUENCER_EXCHANGE_IDS = {"l1": 2, "l0a": 3, "l0b": 4}


def _rs_chip_exchange_behind(pairs, *, tag, small=None):
    n = len(pairs)
    has_small = small is not None

    def body(*refs):
        p_refs = refs[:n]
        s_ref = refs[n] if has_small else None
        rb_refs = refs[n + has_small:2 * n + has_small]
        rs_ref = refs[2 * n + 1] if has_small else None
        x, y, c, chips = _place()
        peers = [(1 - x if k & 4 else x, 1 - y if k & 2 else y, 1 - c if k & 1 else c) for k in range(1, 8)]
        shake = peers if has_small else [(cx, cy, c) for cx, cy in chips]
        barrier = pltpu.get_barrier_semaphore()
        for peer in shake:
            pl.semaphore_signal(barrier, inc=1, device_id=peer, device_id_type=MESH)
        pl.semaphore_wait(barrier, len(shake))
        copy = _remote(*refs[-2:])
        cps = []
        for g in range(n):
            cps += [copy(3 * g + j, p_refs[g].at[2 * cx + cy], rb_refs[g].at[j], (cx, cy, c)) for j, (cx, cy) in enumerate(chips)]
        if has_small:
            dev = 4 * x + 2 * y + c
            cps += [copy(3 * n + k, s_ref, rs_ref.at[dev], peer) for k, peer in enumerate(peers)]
        for cp in cps:
            cp.start()
        for g in range(n):
            for j in range(3):
                copy(3 * g + j, p_refs[g].at[0], rb_refs[g].at[j], (x, y, c)).wait_recv()
        if has_small:
            for k, (px, py, pc) in enumerate(peers):
                copy(3 * n + k, s_ref, rs_ref.at[4 * px + 2 * py + pc], (x, y, c)).wait_recv()
        for cp in cps:
            cp.wait_send()

    ins = list(pairs) + ([small] if has_small else [])
    out_type = [jax.ShapeDtypeStruct((3,) + p.shape[1:], p.dtype) for p in pairs]
    if has_small:
        out_type.append(jax.ShapeDtypeStruct((8,) + small.shape, small.dtype))
    n_sems = 3 * n + 7 * has_small
    outs = pl.kernel(body, out_type=out_type, mesh=plsc.ScalarSubcoreMesh(axis_name="sequencer", num_cores=1),
                     name="rs_chip_exchange_behind_" + tag,
                     scratch_types=[pltpu.SemaphoreType.DMA((n_sems,)), pltpu.SemaphoreType.DMA((n_sems,))],
                     compiler_params=pltpu.CompilerParams(collective_id=SEQUENCER_EXCHANGE_IDS[tag]))(*ins)
    if has_small:
        dev = 4 * lax.axis_index("x") + 2 * lax.axis_index("y") + lax.axis_index("c")
        outs = list(outs[:n]) + [lax.dynamic_update_index_in_dim(outs[n], small, dev, 0)]
    return outs


def _rs_final_add(place, own, rb, *, name):
    Rh, C = own.shape
    tr = _row_tile(Rh)
    nrb = Rh // tr

    def body(p_ref, o_ref, rb_ref, f_ref):
        f_ref[...] = ((o_ref[...] + rb_ref[0].astype(F32)) + rb_ref[1].astype(F32)) + rb_ref[2].astype(F32)

    grid_spec = pltpu.PrefetchScalarGridSpec(
        num_scalar_prefetch=1, grid=(nrb,),
        in_specs=[pl.BlockSpec((tr, C), lambda i, p: (i, 0)), pl.BlockSpec((3, tr, C), lambda i, p: (0, i, 0))],
        out_specs=pl.BlockSpec((tr, C), lambda i, p: (p[0] * nrb + i, 0)))
    return pl.pallas_call(
        body, name=name, grid_spec=grid_spec, out_shape=jax.ShapeDtypeStruct((2 * Rh, C), F32),
        compiler_params=pltpu.CompilerParams(dimension_semantics=("arbitrary",), vmem_limit_bytes=VMEM_LIMIT_V7X))(place, own, rb)


def _sum_slots(rs):
    n, rows, C = rs.shape

    def body(r_ref, o_ref):
        acc = r_ref[0]
        for k in range(1, n):
            acc = acc + r_ref[k]
        o_ref[...] = acc

    return _call(body, name="small_grad_sum", grid=(1,), in_specs=[pl.BlockSpec((n, rows, C), lambda i: (0, 0, 0))],
                 out_specs=pl.BlockSpec((rows, C), lambda i: (0, 0)), out_shape=jax.ShapeDtypeStruct((rows, C), F32),
                 sem=("arbitrary",))(rs)


def _rs_sibling_share(gbufs, *, name):
    n = len(gbufs)

    def body(*refs):
        g_refs = refs[n:2 * n]
        x, y, c, _ = _place()
        copy = _remote(*refs[-2:])
        halves = [_halves(g_refs[g], gbufs[g].shape[0]) for g in range(n)]
        outs = [copy(g, halves[g]((), c), halves[g]((), c), (x, y, 1 - c)) for g in range(n)]
        for cp in outs:
            cp.start()
        for g in range(n):
            copy(g, halves[g]((), 1 - c), halves[g]((), 1 - c), (x, y, c)).wait_recv()
        for cp in outs:
            cp.wait_send()

    return pl.pallas_call(body, name=name, out_shape=[jax.ShapeDtypeStruct(a.shape, a.dtype) for a in gbufs],
                          in_specs=[ANY] * n, out_specs=[ANY] * n, input_output_aliases={g: g for g in range(n)},
                          scratch_shapes=[pltpu.SemaphoreType.DMA((n,)), pltpu.SemaphoreType.DMA((n,))])(*gbufs)


def _adamw(w, g, m, v, *, name, g_row=0):
    rows, cols = w.shape
    tr = rows
    for cand in range(min(rows, 512), 7, -8):
        if rows % cand == 0 and g_row % cand == 0:
            tr = cand
            break
    spec = pl.BlockSpec((tr, cols), lambda i: (i, 0))
    g_spec = pl.BlockSpec((tr, cols), lambda i: (g_row // tr + i, 0))

    def body(w_ref, g_ref, m_ref, v_ref, d_ref, nm_ref, nv_ref):
        gg = g_ref[...]
        nm = ADAM_B1 * m_ref[...] + (1.0 - ADAM_B1) * gg
        nv = ADAM_B2 * v_ref[...] + (1.0 - ADAM_B2) * (gg * gg)
        m_hat = nm / (1.0 - ADAM_B1 ** ADAM_STEP)
        v_hat = nv / (1.0 - ADAM_B2 ** ADAM_STEP)
        d_ref[...] = -ADAM_LR * (m_hat / (jnp.sqrt(v_hat) + ADAM_EPS) + ADAM_WD * w_ref[...])
        nm_ref[...] = nm
        nv_ref[...] = nv

    return _call(body, name=name, grid=(rows // tr,), in_specs=[spec, g_spec, spec, spec], out_specs=[spec] * 3,
                 out_shape=[jax.ShapeDtypeStruct((rows, cols), F32)] * 3, sem=("parallel",))(w, g, m, v)


WEIGHT_NAMES = ("mix_norm", "ab_w_in", "lru_conv_w", "lru_conv_b", "lru_wa", "lru_ba", "lru_wx", "lru_bx", "lru_lambda",
                "ab_w_out", "c_w_qkv", "c_b_qkv", "c_sinks", "c_w_out", "c_b_out", "xa_norm", "xa_mem_norm", "xa_wq",
                "xa_wkv", "xa_wo", "ffn_norm", "ffn_w_gate_up", "ffn_w_down", "final_norm")
EARLY_GROUPS = (("ab_w_in",),)
MID_GROUPS = (("ab_w_out",), ("lru_wa", "lru_wx"))
LATE_GROUPS = (("c_w_out", "xa_wkv", "ffn_w_down"), ("ffn_w_gate_up",), ("xa_wo",), ("xa_wq",), ("c_w_qkv",))
GROUPS = EARLY_GROUPS + MID_GROUPS + LATE_GROUPS
REPLICATED = ("mix_norm", "lru_conv_b", "lru_lambda", "c_sinks", "xa_norm", "xa_mem_norm", "ffn_norm", "final_norm")
SMALL_SHARDED = ("lru_conv_w", "lru_ba", "lru_bx", "c_b_qkv", "c_b_out")
LANES = 1024


def _rows(v):
    flat = v.reshape(-1)
    return jnp.pad(flat, (0, -flat.shape[0] % LANES)).reshape(-1, LANES)


def _pack_small(parts, total, *, name):
    def body(*refs):
        o_ref = refs[-1]
        o_ref[...] = jnp.zeros_like(o_ref)
        row = 0
        for p_ref in refs[:-1]:
            o_ref[row:row + p_ref.shape[0], :] = p_ref[...]
            row += p_ref.shape[0]

    return _call(body, name=name, grid=(1,), in_specs=[pl.BlockSpec(p.shape, lambda i: (0, 0)) for p in parts],
                 out_specs=pl.BlockSpec((total, LANES), lambda i: (0, 0)),
                 out_shape=jax.ShapeDtypeStruct((total, LANES), F32), sem=("arbitrary",))(*parts)


def _from_shards(name, t):
    minor = t.shape[-1]
    if name == "ab_w_in":
        return t
    if name in ("ab_w_out", "c_w_out"):
        return t.reshape(1, -1, minor)
    if name == "ffn_w_gate_up":
        return t.reshape(N_CHIPS, 2, -1, minor)
    if name in ("xa_wq", "xa_wkv", "ffn_w_down"):
        return t.reshape(N_CHIPS, 2, -1, minor).transpose(1, 0, 2, 3).reshape(2, -1, minor)
    if name in ("lru_wa", "lru_wx"):
        return t.reshape(N_CHIPS, LRU_HEADS, -1, minor).transpose(1, 0, 2, 3).reshape(LRU_HEADS, LRU_HEAD_DIM, minor)
    if name == "xa_wo":
        return t.reshape(N_CHIPS, 2, -1, minor).transpose(1, 0, 2, 3)
    assert name == "c_w_qkv"
    return t.transpose(1, 0, 2).reshape(1, D_MODEL, -1)


def _piece_shards(name, g):
    minor = g.shape[-1]
    if name in ("ab_w_in", "ffn_w_gate_up", "xa_wo"):
        return g
    if name in ("ab_w_out", "c_w_out", "xa_wq", "xa_wkv", "ffn_w_down"):
        return g.reshape(N_CHIPS, -1, minor)
    if name in ("lru_wa", "lru_wx"):
        return g.reshape(LRU_HEADS, N_CHIPS, -1, minor).transpose(1, 0, 2, 3).reshape(N_CHIPS, -1, minor)
    assert name == "c_w_qkv"
    return g.reshape(D_MODEL, N_CHIPS, -1).transpose(1, 0, 2)


RS_SETS = {
    "l1": ((("c_w_out", None), ("xa_wkv", 1), ("ffn_w_down", 1)), (("ffn_w_gate_up", 1),), (("xa_wo", 1),),
           (("xa_wq", 1),), (("c_w_qkv", None),)),
    "l0a": ((("xa_wkv", 0), ("ffn_w_down", 0)), (("ffn_w_gate_up", 0),), (("xa_wo", 0),), (("xa_wq", 0),)),
    "l0b": ((("ab_w_out", None),), (("ab_w_in", None),), (("lru_wa", None), ("lru_wx", None))),
}
RS_STAGE = {"layer1": "l1", "layer0_ffn_xa": "l0a"}


def kernel(x, mem, mix_norm, ab_w_in, lru_conv_w, lru_conv_b, lru_wa, lru_ba, lru_wx, lru_bx, lru_lambda, ab_w_out, c_w_qkv, c_b_qkv, c_sinks, c_w_out, c_b_out, xa_norm, xa_mem_norm, xa_wq, xa_wkv, xa_wo, ffn_norm, ffn_w_gate_up, ffn_w_down, final_norm, loss_target, m_mix_norm, m_ab_w_in, m_lru_conv_w, m_lru_conv_b, m_lru_wa, m_lru_ba, m_lru_wx, m_lru_bx, m_lru_lambda, m_ab_w_out, m_c_w_qkv, m_c_b_qkv, m_c_sinks, m_c_w_out, m_c_b_out, m_xa_norm, m_xa_mem_norm, m_xa_wq, m_xa_wkv, m_xa_wo, m_ffn_norm, m_ffn_w_gate_up, m_ffn_w_down, m_final_norm, v_mix_norm, v_ab_w_in, v_lru_conv_w, v_lru_conv_b, v_lru_wa, v_lru_ba, v_lru_wx, v_lru_bx, v_lru_lambda, v_ab_w_out, v_c_w_qkv, v_c_b_qkv, v_c_sinks, v_c_w_out, v_c_b_out, v_xa_norm, v_xa_mem_norm, v_xa_wq, v_xa_wkv, v_xa_wo, v_ffn_norm, v_ffn_w_gate_up, v_ffn_w_down, v_final_norm):
    given = dict(locals())
    wl = {n: given[n] for n in WEIGHT_NAMES}
    ml = {n: given["m_" + n] for n in WEIGHT_NAMES}
    vl = {n: given["v_" + n] for n in WEIGHT_NAMES}
    xi, yi, ci = lax.axis_index("x"), lax.axis_index("y"), lax.axis_index("c")
    chip = 2 * xi + yi

    def join(parts, axis):
        return parts[0] if len(parts) == 1 else jnp.concatenate(parts, axis=axis)

    local_rows = {n: wl[n].size // wl[n].shape[-1] for grp in GROUPS for n in grp}
    packs = [join([wl[n].astype(BF16).reshape(local_rows[n], wl[n].shape[-1]) for n in grp], 0) for grp in GROUPS]
    spack = _pack_small([_rows(wl[n]) for n in SMALL_SHARDED], 8, name="pack_small_weights")
    n_early, n_mid = len(EARLY_GROUPS), len(EARLY_GROUPS) + len(MID_GROUPS)
    early, sfull = _gather_weights(packs[:n_early], spack)
    early, sfull, mid_packs = lax.optimization_barrier((early, sfull, packs[n_early:n_mid]))
    mid = _gather_weights_behind(mid_packs, tag="mid")
    mid, late_packs = lax.optimization_barrier((mid, packs[n_mid:]))
    gathered = early + mid + _gather_weights_behind(late_packs, tag="late")
    w = {n: wl[n] for n in REPLICATED}
    w["c_sinks"] = wl["c_sinks"][0]
    for grp, full in zip(GROUPS, gathered):
        off = 0
        for n in grp:
            w[n] = _from_shards(n, full if len(grp) == 1 else full[:, off:off + local_rows[n]])
            off += local_rows[n]
    for r, n in enumerate(SMALL_SHARDED):
        loc = wl[n].shape[1:]
        t = sfull[:, r, :wl[n].size].reshape((N_CHIPS,) + loc)
        if n == "lru_conv_w":
            w[n] = t.transpose(1, 0, 2).reshape(CONV_WIDTH, -1)
        elif n in ("lru_ba", "lru_bx"):
            w[n] = t.transpose(1, 0, 2).reshape(1, -1)
        else:
            w[n] = t.reshape(1, -1)

    place = jnp.stack([ci, chip]).astype(jnp.int32)

    def pair_stage(spec, g, tag):
        piece = lambda n, l: (g[n] if l is None else g[n, l]).astype(BF16)
        gpacks = [join([_piece_shards(n, piece(n, l)) for n, l in grp], 1) for grp in spec]
        ras = _rs_pair_exchange(gpacks, name=f"rs_pair_exchange_{tag}")
        sums = [_rs_pair_add(place, gp, ra, name=f"rs_pair_add_{tag}_{i}") for i, (gp, ra) in enumerate(zip(gpacks, ras))]
        return [pair for pair, _ in sums], [own for _, own in sums]

    reduced, in_flight = [], []

    def take_up():
        done = [_rs_final_add(place, o, r, name=f"rs_final_add_{len(reduced) + i}") for i, (o, r) in enumerate(in_flight)]
        reduced.extend(done)
        in_flight.clear()
        return done

    def reduce_behind(stage, g):
        done = take_up()
        tag = RS_STAGE[stage]
        pairs, own = pair_stage(RS_SETS[tag], g, tag)
        in_flight.extend(zip(own, _rs_chip_exchange_behind(pairs, tag=tag)))
        return own + done

    loss_part, grad_x, g = _device_step(x[0], mem[0], loss_target[0], w, on_grads=reduce_behind)

    small_parts = [_rows(g[n]) for n in REPLICATED] + [_rows(jnp.broadcast_to(loss_part, (LANES,)))]
    small_parts += [_rows(g[n]) for n in SMALL_SHARDED]
    small = _pack_small(small_parts, 24, name="pack_small_grads")
    take_up()
    pairs, own = pair_stage(RS_SETS["l0b"], g, "l0b")
    *rb, rs = _rs_chip_exchange_behind(pairs, tag="l0b", small=small)
    gsums = list(_rs_sibling_share(list(reduced), name="rs_sibling_share_behind"))
    in_flight.extend(zip(own, rb))
    gsums += list(_rs_sibling_share(take_up(), name="rs_sibling_share_last"))
    ssum = _sum_slots(rs)

    where = {}
    for grp, gsum in zip(RS_SETS["l1"] + RS_SETS["l0a"] + RS_SETS["l0b"], gsums):
        off = 0
        for n, l in grp:
            rows = local_rows[n] if l is None else local_rows[n] // 2
            where[n, l] = (gsum, off, rows, len(grp) == 1)
            off += rows
    take = lambda gsum, off, rows, whole: gsum if whole else gsum[off:off + rows]
    grads, grad_rows = {}, {}
    for grp in LATE_GROUPS + EARLY_GROUPS + MID_GROUPS:
        for n in grp:
            if (n, None) in where:
                grads[n] = take(*where[n, None]).reshape(wl[n].shape)
                grad_rows[n] = where[n, None][:2]
            else:
                grads[n] = jnp.stack([take(*where[n, l]).reshape(wl[n].shape[1:]) for l in range(2)])
                grad_rows[n] = (grads[n].reshape(local_rows[n], wl[n].shape[-1]), 0)
    row = 0
    for n in REPLICATED:
        k = _rows(g[n]).shape[0]
        grads[n] = ssum[row:row + k].reshape(-1)[:wl[n].size].reshape(wl[n].shape)
        row += k
    loss = ssum[row, 0]
    row += 1
    for n in SMALL_SHARDED:
        k = _rows(g[n]).shape[0]
        full = ssum[row:row + k].reshape(-1)[:g[n].size]
        row += k
        loc = wl[n].shape
        if n == "lru_conv_w":
            sh = full.reshape(CONV_WIDTH, N_CHIPS, -1)
        elif n in ("lru_ba", "lru_bx"):
            sh = full.reshape(LRU_HEADS, N_CHIPS, -1)
        else:
            sh = full.reshape(1, N_CHIPS, -1)
        grads[n] = lax.dynamic_index_in_dim(sh, chip, axis=1, keepdims=False).reshape(loc)

    delta, new_m, new_v = {}, {}, {}
    for n, (gsum, off) in grad_rows.items():
        shape2 = (local_rows[n], wl[n].shape[-1])
        d, nm, nv = _adamw(wl[n].reshape(shape2), gsum, ml[n].reshape(shape2), vl[n].reshape(shape2), g_row=off,
                           name="adamw_" + n)
        delta[n], new_m[n], new_v[n] = (t.reshape(wl[n].shape) for t in (d, nm, nv))
    smalls = REPLICATED + SMALL_SHARDED
    packs = [_pack_small([_rows(src[n]) for n in smalls], 24, name="pack_adamw_" + tag)
             for tag, src in (("w", wl), ("g", grads), ("m", ml), ("v", vl))]
    outs = _adamw(*packs, name="adamw_small")
    row = 0
    for n in smalls:
        k = _rows(wl[n]).shape[0]
        for dst, o in zip((delta, new_m, new_v), outs):
            dst[n] = o[row:row + k].reshape(-1)[:wl[n].size].reshape(wl[n].shape)
        row += k

    return (loss, grad_x[None], *[grads[n] for n in WEIGHT_NAMES], *[delta[n] for n in WEIGHT_NAMES],
            *[new_m[n] for n in WEIGHT_NAMES], *[new_v[n] for n in WEIGHT_NAMES])
```

```python
import jax
import jax.numpy as jnp
from jax import lax
from jax.experimental import pallas as pl
from jax.experimental.pallas import tpu as pltpu
from jax.experimental.pallas import tpu_sc as plsc

F32, BF16 = jnp.float32, jnp.bfloat16
D_MODEL = 1024
NORM_EPS = 1e-6
ROPE_THETA = 500000.0
HEAD_DIM = 64
ROT_DIM = 16
BLK = 128
LRU_HEADS, LRU_HEAD_DIM, CONV_WIDTH, LRU_C = 4, 256, 4, 8.0
DILATED_PATTERN = ((128, 1), (512, 4), (2048, 16))
B_HEADS, C_HEADS, C_KV_HEADS, C_WINDOW = 8, 16, 2, 128
XA_HEADS, XA_HEAD_DIM, N_MEM = 4, 128, 256
D_FF = 2816
NEG = -1e30
ADAM_LR, ADAM_B1, ADAM_B2, ADAM_EPS, ADAM_WD, ADAM_STEP = 0.001, 0.9, 0.999, 1e-08, 0.01, 10
N_CHIPS = 4
VMEM_LIMIT_V7X = 56 * 1024 * 1024

NN = (((1,), (0,)), ((), ()))
NT = (((1,), (1,)), ((), ()))
TN = (((0,), (0,)), ((), ()))


def _dot(a, b, dims=NN):
    return lax.dot_general(a, b, dims, preferred_element_type=F32)


def _sigmoid(x):
    return 0.5 * jnp.tanh(0.5 * x) + 0.5


def _call(body, *, name, grid, in_specs, out_specs, out_shape, scratch=(), sem=None):
    return pl.pallas_call(
        body, name=name, grid=grid, in_specs=in_specs, out_specs=out_specs, out_shape=out_shape,
        scratch_shapes=list(scratch),
        compiler_params=pltpu.CompilerParams(dimension_semantics=sem, vmem_limit_bytes=VMEM_LIMIT_V7X))


def _rope_tables(L):
    half = ROT_DIM // 2
    inv = ROPE_THETA ** (-jnp.arange(0, ROT_DIM, 2, dtype=F32) / ROT_DIM)
    j = jnp.arange(2 * HEAD_DIM) % HEAD_DIM
    ang = jnp.arange(L, dtype=F32)[:, None] * inv[j % half][None, :]
    cos, sin = jnp.cos(ang), jnp.sin(ang)
    c = jnp.where(j < ROT_DIM, cos, 1.0)
    s1 = jnp.where(j < half, -sin, 0.0)
    s2 = jnp.where((j >= half) & (j < ROT_DIM), sin, 0.0)
    return c, s1, s2


def _rope_fwd(v, c, s1, s2):
    return v * c + pltpu.roll(v, 120, 1) * s1 + pltpu.roll(v, 8, 1) * s2


def _rope_bwd(dv, c, s1, s2):
    return dv * c + pltpu.roll(dv * s1, 8, 1) + pltpu.roll(dv * s2, 120, 1)


def _weight_spec(w, layer):
    once = pl.Buffered(1)
    if layer is None:
        return w.shape, pl.BlockSpec(w.shape, lambda i: (0, 0, 0), pipeline_mode=once)
    S, _, K, Ns = w.shape
    return (S, K, Ns), pl.BlockSpec((S, None, K, Ns), lambda i: (0, layer, 0, 0), pipeline_mode=once)


def _rowmm(a, w3, *, name, tm=512, gain=None, bias=None, res=None, swiglu=False, rope=None, layer=None):
    M, K = a.shape
    (S, _, Ns), w_spec = _weight_spec(w3, layer)
    N = S * Ns
    tm = min(tm, M)
    has_norm, has_bias, has_res, has_rope = gain is not None, bias is not None, res is not None, rope is not None
    row = lambda w: pl.BlockSpec((tm, w), lambda i: (i, 0))
    whole = lambda shape: pl.BlockSpec(shape, lambda i: (0,) * len(shape))
    ins, specs = [a], [row(K)]
    if has_norm:
        ins.append(gain.reshape(1, K)); specs.append(whole((1, K)))
    ins.append(w3); specs.append(w_spec)
    if has_bias:
        ins.append(bias.reshape(1, N)); specs.append(whole((1, N)))
    if has_res:
        ins.append(res); specs.append(row(N))
    if has_rope:
        ins += list(rope[2]); specs += [row(128)] * 3
    y_dtype = F32 if has_res else BF16
    out_shape, out_specs = [jax.ShapeDtypeStruct((M, N), y_dtype)], [row(N)]
    if has_norm:
        out_shape.append(jax.ShapeDtypeStruct((M, K), BF16)); out_specs.append(row(K))
    if swiglu:
        out_shape.append(jax.ShapeDtypeStruct((M, N // 2), BF16)); out_specs.append(row(N // 2))
    scratch = [pltpu.VMEM((tm, N), F32)] if has_rope else []

    def body(*refs):
        it = iter(refs)
        a_ref = next(it)
        g_ref = next(it) if has_norm else None
        w_ref = next(it)
        b_ref = next(it) if has_bias else None
        r_ref = next(it) if has_res else None
        tabs = [next(it) for _ in range(3)] if has_rope else None
        y_ref = next(it)
        n_ref = next(it) if has_norm else None
        act_ref = next(it) if swiglu else None
        ys_ref = next(it) if has_rope else None
        if has_norm:
            x = a_ref[...].astype(F32)
            ms = jnp.mean(x * x, axis=-1, keepdims=True)
            xb = (x * lax.rsqrt(ms + NORM_EPS) * g_ref[...]).astype(BF16)
            n_ref[...] = xb
        else:
            xb = a_ref[...].astype(BF16)
        if swiglu:
            for s in range(S // 2):
                g = _dot(xb, w_ref[s])
                u = _dot(xb, w_ref[s + S // 2])
                y_ref[:, s * Ns:(s + 1) * Ns] = g.astype(BF16)
                y_ref[:, N // 2 + s * Ns:N // 2 + (s + 1) * Ns] = u.astype(BF16)
                act_ref[:, s * Ns:(s + 1) * Ns] = (g * _sigmoid(g) * u).astype(BF16)
            return
        for s in range(S):
            sl = slice(s * Ns, (s + 1) * Ns)
            acc = _dot(xb, w_ref[s])
            if has_bias:
                acc = acc + b_ref[:, sl]
            if has_res:
                acc = acc + r_ref[:, sl]
            if has_rope:
                ys_ref[:, sl] = acc
            else:
                y_ref[:, sl] = acc.astype(y_dtype)
        if has_rope:
            c, s1, s2 = (t[...] for t in tabs)
            for cb in range(N // 128):
                cs = slice(cb * 128, (cb + 1) * 128)
                v = ys_ref[:, cs]
                if rope[0] <= cb * 128 < rope[1]:
                    v = _rope_fwd(v, c, s1, s2)
                y_ref[:, cs] = v.astype(BF16)

    return _call(body, name=name, grid=(M // tm,), in_specs=specs, out_specs=out_specs, out_shape=out_shape,
                 scratch=scratch, sem=("parallel",))(*ins)


def _mm_nt(dy, w3, *, name, mode, tm=512, kchunk=None, h=None, gain=None, dh=None, gu=None, layer=None, after=None):
    M, N = dy.shape
    (S, K, Ns), w_spec = _weight_spec(w3, layer)
    kchunk = kchunk or K
    tm = min(tm, M)
    row = lambda w: pl.BlockSpec((tm, w), lambda i: (i, 0))
    whole = lambda shape: pl.BlockSpec(shape, lambda i: (0,) * len(shape))
    ins, specs = [dy, w3], [row(N), w_spec]
    after = list(after or ())
    ins = after + ins
    specs = [pl.BlockSpec((8, a.shape[1]), lambda i: (0, 0)) for a in after] + specs
    has_dh = dh is not None
    if mode == "norm":
        ins += [h, gain.reshape(1, K)]; specs += [row(K), whole((1, K))]
        if has_dh:
            ins.append(dh); specs.append(row(K))
        out_shape = [jax.ShapeDtypeStruct((M, K), F32), jax.ShapeDtypeStruct((1, K), F32)]
        out_specs = [row(K), whole((1, K))]
    elif mode == "swiglu":
        ins.append(gu); specs.append(row(2 * K))
        out_shape, out_specs = [jax.ShapeDtypeStruct((M, 2 * K), BF16)], [row(2 * K)]
    else:
        out_shape, out_specs = [jax.ShapeDtypeStruct((M, K), BF16)], [row(K)]

    def body(*refs):
        it = iter(refs[len(after):])
        dy_ref, w_ref = next(it), next(it)
        if mode == "norm":
            h_ref, g_ref = next(it), next(it)
            dh_ref = next(it) if has_dh else None
            o_ref, dg_ref = next(it), next(it)
        elif mode == "swiglu":
            gu_ref, o_ref = next(it), next(it)
        else:
            o_ref = next(it)
        for kc in range(K // kchunk):
            ks = slice(kc * kchunk, (kc + 1) * kchunk)
            acc = None
            for s in range(S):
                t = _dot(dy_ref[:, s * Ns:(s + 1) * Ns].astype(BF16), w_ref[s, ks, :], NT)
                acc = t if acc is None else acc + t
            if mode == "plain":
                o_ref[:, ks] = acc.astype(BF16)
            elif mode == "swiglu":
                us = slice(K + kc * kchunk, K + (kc + 1) * kchunk)
                g = gu_ref[:, ks].astype(F32)
                u = gu_ref[:, us].astype(F32)
                sg = _sigmoid(g)
                o_ref[:, ks] = (acc * u * (sg * (1.0 + g * (1.0 - sg)))).astype(BF16)
                o_ref[:, us] = (acc * (g * sg)).astype(BF16)
            else:
                x = h_ref[...].astype(F32)
                r = lax.rsqrt(jnp.mean(x * x, axis=-1, keepdims=True) + NORM_EPS)
                xhat = x * r
                dxh = acc * g_ref[...]
                dx = r * (dxh - xhat * jnp.mean(dxh * xhat, axis=-1, keepdims=True))
                o_ref[...] = dx + dh_ref[...] if has_dh else dx

                @pl.when(pl.program_id(0) == 0)
                def _():
                    dg_ref[...] = jnp.zeros_like(dg_ref)

                dg_ref[...] += jnp.sum(acc * xhat, axis=0, keepdims=True)

    sem = ("arbitrary",) if mode == "norm" else ("parallel",)
    return _call(body, name=name, grid=(M // tm,), in_specs=specs, out_specs=out_specs, out_shape=out_shape, sem=sem)(*ins)


def _mm_tn(x, dy, *, S, name, tk=2048, kk=None, bias=False):
    M, K = x.shape
    N = dy.shape[1]
    Ns = N // S
    kk = kk or K
    tk = min(tk, M)
    nl = M // tk
    in_specs = [pl.BlockSpec((tk, kk), lambda s, kc, l: (l, kc)), pl.BlockSpec((tk, Ns), lambda s, kc, l: (l, s))]
    out_shape = [jax.ShapeDtypeStruct((S, K, Ns), BF16)]
    out_specs = [pl.BlockSpec((None, kk, Ns), lambda s, kc, l: (s, kc, 0))]
    if bias:
        out_shape.append(jax.ShapeDtypeStruct((1, N), F32))
        out_specs.append(pl.BlockSpec((1, Ns), lambda s, kc, l: (0, s)))

    def body(x_ref, dy_ref, o_ref, *rest):
        acc_ref = rest[-1]
        kc, l = pl.program_id(1), pl.program_id(2)

        @pl.when(l == 0)
        def _():
            acc_ref[...] = jnp.zeros_like(acc_ref)

        acc_ref[...] += _dot(x_ref[...].astype(BF16), dy_ref[...].astype(BF16), TN)
        if bias:
            b_ref = rest[0]

            @pl.when((kc == 0) & (l == 0))
            def _():
                b_ref[...] = jnp.zeros_like(b_ref)

            @pl.when(kc == 0)
            def _():
                b_ref[...] += jnp.sum(dy_ref[...].astype(F32), axis=0, keepdims=True)

        @pl.when(l == nl - 1)
        def _():
            o_ref[...] = acc_ref[...].astype(BF16)

    return _call(body, name=name, grid=(S, K // kk, nl), in_specs=in_specs, out_specs=out_specs, out_shape=out_shape,
                 scratch=[pltpu.VMEM((kk, Ns), F32)], sem=("arbitrary", "arbitrary", "arbitrary"))(x, dy)


def _band_bias(max_dist, has_prev):
    rows = lax.broadcasted_iota(jnp.int32, (BLK, 2 * BLK), 0)
    cols = lax.broadcasted_iota(jnp.int32, (BLK, 2 * BLK), 1)
    dist = rows - cols + BLK
    ok = (dist >= 0) & (dist <= max_dist) & ((cols >= BLK) | has_prev)
    return jnp.where(ok, 0.0, NEG)


Q_SCALE = HEAD_DIM ** -0.5


def _band_fwd(qa, ka, va, *, d, nq, nkv, qcol, kcol, vcol, max_dist, sinks=None, name):
    Lr = qa.shape[0]
    nb = Lr // BLK
    qw, kw, G = nq * HEAD_DIM, nkv * HEAD_DIM, nq // nkv
    cur = lambda colf, w: pl.BlockSpec((BLK, w), lambda r, i: (i, colf(r)))
    prv = lambda colf, w: pl.BlockSpec((BLK, w), lambda r, i: (jnp.maximum(i - 1, 0), colf(r)))
    out = pl.BlockSpec((BLK, qw), lambda r, i: (i, r))
    ins, specs = [qa, ka, ka, va, va], [cur(qcol, qw), cur(kcol, kw), prv(kcol, kw), cur(vcol, kw), prv(vcol, kw)]
    has_sinks = sinks is not None
    if has_sinks:
        ins.append(sinks); specs.append(pl.BlockSpec(memory_space=pltpu.SMEM))

    def body(*refs):
        q_ref, kc_ref, kp_ref, vc_ref, vp_ref = refs[:5]
        sk_ref = refs[5] if has_sinks else None
        o_ref, lse_ref = refs[-2], refs[-1]
        bias = _band_bias(max_dist, pl.program_id(1) > 0)
        k2 = jnp.concatenate([kp_ref[...], kc_ref[...]], axis=0)
        v2 = jnp.concatenate([vp_ref[...], vc_ref[...]], axis=0)
        for h in range(nq):
            hs = slice(h * HEAD_DIM, (h + 1) * HEAD_DIM)
            ks = slice((h // G) * HEAD_DIM, (h // G + 1) * HEAD_DIM)
            s = _dot(q_ref[:, hs] * jnp.asarray(Q_SCALE, BF16), k2[:, ks], NT) + bias
            m = jnp.max(s, axis=-1, keepdims=True)
            if has_sinks:
                m = jnp.maximum(m, sk_ref[h])
            p = jnp.exp(s - m)
            l = jnp.sum(p, axis=-1, keepdims=True)
            if has_sinks:
                l = l + jnp.exp(sk_ref[h] - m)
            o_ref[:, hs] = (_dot(p.astype(BF16), v2[:, ks]) / l).astype(BF16)
            lse_ref[:, hs] = jnp.broadcast_to(m + jnp.log(l), (BLK, HEAD_DIM))

    return _call(body, name=name, grid=(d, nb), in_specs=specs, out_specs=[out, out],
                 out_shape=[jax.ShapeDtypeStruct((Lr, d * qw), BF16), jax.ShapeDtypeStruct((Lr, d * qw), F32)],
                 sem=("parallel", "parallel"))(*ins)


def _band_bwd(qa, ka, va, doa, oa, lsea, *, d, nq, nkv, qcol, kcol, vcol, docol, max_dist, sinks=None, name):
    Lr = qa.shape[0]
    nb = Lr // BLK
    qw, kw, G = nq * HEAD_DIM, nkv * HEAD_DIM, nq // nkv
    transposed = G > 1
    last = lambda i: jnp.minimum(i, nb - 1)
    cur = lambda colf, w: pl.BlockSpec((BLK, w), lambda r, i: (last(i), colf(r)))
    prv = lambda colf, w: pl.BlockSpec((BLK, w), lambda r, i: (jnp.maximum(last(i) - 1, 0), colf(r)))
    own = lambda r: r
    ins = [qa, ka, ka, va, va, doa, oa, lsea]
    specs = [cur(qcol, qw), cur(kcol, kw), prv(kcol, kw), cur(vcol, kw), prv(vcol, kw), cur(docol, qw), cur(own, qw),
             cur(own, qw)]
    has_sinks = sinks is not None
    if has_sinks:
        ins.append(sinks); specs.append(pl.BlockSpec(memory_space=pltpu.SMEM))
    out_shape = [jax.ShapeDtypeStruct((Lr, d * qw), BF16), jax.ShapeDtypeStruct((Lr, d * kw), BF16),
                 jax.ShapeDtypeStruct((Lr, d * kw), BF16)]
    behind = lambda r, i: (jnp.maximum(i - 1, 0), r)
    out_specs = [pl.BlockSpec((BLK, qw), lambda r, i: (last(i), r)), pl.BlockSpec((BLK, kw), behind),
                 pl.BlockSpec((BLK, kw), behind)]
    if has_sinks:
        out_shape.append(jax.ShapeDtypeStruct((8, 128), F32))
        out_specs.append(pl.BlockSpec((8, 128), lambda r, i: (0, 0)))

    def body(*refs):
        it = iter(refs)
        q_ref, kc_ref, kp_ref, vc_ref, vp_ref, do_ref, o_ref, ls_ref = (next(it) for _ in range(8))
        sk_ref = next(it) if has_sinks else None
        dq_ref, dk_ref, dv_ref = next(it), next(it), next(it)
        dsk_ref = next(it) if has_sinks else None
        dk_car, dv_car = next(it), next(it)
        r_id, i = pl.program_id(0), pl.program_id(1)

        @pl.when(i == 0)
        def _():
            dk_car[...] = jnp.zeros_like(dk_car)
            dv_car[...] = jnp.zeros_like(dv_car)

        if has_sinks:
            @pl.when((r_id == 0) & (i == 0))
            def _():
                dsk_ref[...] = jnp.zeros_like(dsk_ref)

        @pl.when(i == nb)
        def _():
            dk_ref[...] = dk_car[...].astype(BF16)
            dv_ref[...] = dv_car[...].astype(BF16)

        @pl.when(i < nb)
        def _():
            bias = _band_bias(max_dist, i > 0)
            k2 = jnp.concatenate([kp_ref[...], kc_ref[...]], axis=0)
            v2 = jnp.concatenate([vp_ref[...], vc_ref[...]], axis=0)
            if has_sinks:
                lane = lax.broadcasted_iota(jnp.int32, (8, 128), 1)
                dsk = jnp.zeros((8, 128), F32)
            for kv in range(nkv):
                ks = slice(kv * HEAD_DIM, (kv + 1) * HEAD_DIM)
                kh, vh = k2[:, ks], v2[:, ks]
                shape = (HEAD_DIM, 2 * BLK) if transposed else (2 * BLK, HEAD_DIM)
                dk, dv = jnp.zeros(shape, F32), jnp.zeros(shape, F32)
                for g in range(G):
                    h = kv * G + g
                    hs = slice(h * HEAD_DIM, (h + 1) * HEAD_DIM)
                    q = q_ref[:, hs] * jnp.asarray(Q_SCALE, BF16)
                    do = do_ref[:, hs]
                    lse = ls_ref[:, h * HEAD_DIM:h * HEAD_DIM + 1]
                    dl = jnp.sum(do.astype(F32) * o_ref[:, hs].astype(F32), axis=-1, keepdims=True)
                    p = jnp.exp(_dot(q, kh, NT) + bias - lse)
                    ds = (p * (_dot(do, vh, NT) - dl)).astype(BF16)
                    dq_ref[:, hs] = (_dot(ds, kh) * Q_SCALE).astype(BF16)
                    if transposed:
                        dk = dk + _dot(q, ds, TN)
                        dv = dv + _dot(do, p.astype(BF16), TN)
                    else:
                        dk = dk + _dot(ds, q, TN)
                        dv = dv + _dot(p.astype(BF16), do, TN)
                    if has_sinks:
                        val = -jnp.sum(jnp.exp(sk_ref[h] - lse) * dl, axis=0, keepdims=True)
                        dsk = dsk + jnp.where(lane == h, val, 0.0)
                if transposed:
                    dk, dv = dk.T, dv.T
                dk_ref[:, ks] = (dk_car[:, ks] + dk[:BLK]).astype(BF16)
                dv_ref[:, ks] = (dv_car[:, ks] + dv[:BLK]).astype(BF16)
                dk_car[:, ks] = dk[BLK:]
                dv_car[:, ks] = dv[BLK:]
            if has_sinks:
                dsk_ref[...] += dsk

    return _call(body, name=name, grid=(d, nb + 1), in_specs=specs, out_specs=out_specs, out_shape=out_shape,
                 scratch=[pltpu.VMEM((BLK, kw), F32), pltpu.VMEM((BLK, kw), F32)], sem=("arbitrary", "arbitrary"))(*ins)


def _attn_grad_combine(branches, tabs, *, name, tm=512):
    L, qw = branches[0][0].shape
    kw = branches[0][1].shape[1]
    nbr = len(branches)
    row = lambda w: pl.BlockSpec((tm, w), lambda i: (i, 0))
    ins, specs = [], []
    for dq, dk, dv in branches:
        ins += [dq, dk, dv]; specs += [row(qw), row(kw), row(kw)]
    ins += list(tabs); specs += [row(128)] * 3

    def body(*refs):
        c, s1, s2 = (t[...] for t in refs[3 * nbr:3 * nbr + 3])
        o_ref = refs[-1]
        for part, (w, off, rot) in enumerate(((qw, 0, True), (kw, qw, True), (kw, qw + kw, False))):
            for cb in range(w // 128):
                cs = slice(cb * 128, (cb + 1) * 128)
                v = refs[part][:, cs].astype(F32)
                for b in range(1, nbr):
                    v = v + refs[3 * b + part][:, cs].astype(F32)
                if rot:
                    v = _rope_bwd(v, c, s1, s2)
                o_ref[:, off + cb * 128:off + (cb + 1) * 128] = v.astype(BF16)

    return _call(body, name=name, grid=(L // tm,), in_specs=specs, out_specs=row(qw + 2 * kw),
                 out_shape=jax.ShapeDtypeStruct((L, qw + 2 * kw), BF16), sem=("parallel",))(*ins)


def _xattn_fwd(q, kv, *, name, tq=1024):
    L, W = q.shape
    scale = XA_HEAD_DIM ** -0.5
    row = pl.BlockSpec((tq, W), lambda i: (i, 0))
    kvs = pl.BlockSpec((N_MEM, 2 * W), lambda i: (0, 0))

    def body(q_ref, kv_ref, o_ref, lse_ref):
        for h in range(XA_HEADS):
            hs = slice(h * XA_HEAD_DIM, (h + 1) * XA_HEAD_DIM)
            vs = slice(W + h * XA_HEAD_DIM, W + (h + 1) * XA_HEAD_DIM)
            s = _dot(q_ref[:, hs], kv_ref[:, hs], NT) * scale
            m = jnp.max(s, axis=-1, keepdims=True)
            p = jnp.exp(s - m)
            l = jnp.sum(p, axis=-1, keepdims=True)
            o_ref[:, hs] = (_dot(p.astype(BF16), kv_ref[:, vs]) / l).astype(BF16)
            lse_ref[:, hs] = jnp.broadcast_to(m + jnp.log(l), (tq, XA_HEAD_DIM))

    return _call(body, name=name, grid=(L // tq,), in_specs=[row, kvs], out_specs=[row, row],
                 out_shape=[jax.ShapeDtypeStruct((L, W), BF16), jax.ShapeDtypeStruct((L, W), F32)], sem=("parallel",))(q, kv)


def _xattn_bwd(q, kv, o, lse, do, *, name, tq=1024):
    L, W = q.shape
    scale = XA_HEAD_DIM ** -0.5
    row = pl.BlockSpec((tq, W), lambda i: (i, 0))
    kvs = pl.BlockSpec((N_MEM, 2 * W), lambda i: (0, 0))

    def body(q_ref, kv_ref, o_ref, lse_ref, do_ref, dq_ref, dkv_ref):
        @pl.when(pl.program_id(0) == 0)
        def _():
            dkv_ref[...] = jnp.zeros_like(dkv_ref)

        for h in range(XA_HEADS):
            hs = slice(h * XA_HEAD_DIM, (h + 1) * XA_HEAD_DIM)
            vs = slice(W + h * XA_HEAD_DIM, W + (h + 1) * XA_HEAD_DIM)
            qh, kh, vh, doh = q_ref[:, hs], kv_ref[:, hs], kv_ref[:, vs], do_ref[:, hs]
            p = jnp.exp(_dot(qh, kh, NT) * scale - lse_ref[:, h * XA_HEAD_DIM:h * XA_HEAD_DIM + 1])
            dl = jnp.sum(doh.astype(F32) * o_ref[:, hs].astype(F32), axis=-1, keepdims=True)
            ds = (p * (_dot(doh, vh, NT) - dl) * scale).astype(BF16)
            dq_ref[:, hs] = _dot(ds, kh).astype(BF16)
            dkv_ref[:, hs] += _dot(ds, qh, TN)
            dkv_ref[:, vs] += _dot(p.astype(BF16), doh, TN)

    return _call(body, name=name, grid=(L // tq,), in_specs=[row, kvs, row, row, row], out_specs=[row, kvs],
                 out_shape=[jax.ShapeDtypeStruct((L, W), BF16), jax.ShapeDtypeStruct((N_MEM, 2 * W), F32)],
                 sem=("arbitrary",))(q, kv, o, lse, do)


def _neg_expm1(z):
    series = -(z * (1.0 + z * (0.5 + z * (1.0 / 6.0 + z * (1.0 / 24.0 + z * (1.0 / 120.0))))))
    return jnp.where(z > -0.05, series, 1.0 - jnp.exp(z))


def _softplus(z):
    return jnp.maximum(z, 0.0) + jnp.log(1.0 + jnp.exp(-jnp.abs(z)))


def _gelu_parts(y):
    c = 0.7978845608028654
    t = jnp.tanh(c * (y + 0.044715 * y * y * y))
    gy = 0.5 * y * (1.0 + t)
    dgy = 0.5 * (1.0 + t) + 0.5 * y * (1.0 - t * t) * c * (1.0 + 3.0 * 0.044715 * y * y)
    return gy, dgy


def _lru_gates(xc, wa_ref, ba, wx_ref, bx, sp):
    rs, igs = [], []
    for hd in range(LRU_HEADS):
        sl = slice(hd * LRU_HEAD_DIM, (hd + 1) * LRU_HEAD_DIM)
        xh = xc[:, sl].astype(BF16)
        rs.append(_sigmoid(_dot(xh, wa_ref[hd]) + ba[:, sl]))
        igs.append(_sigmoid(_dot(xh, wx_ref[hd]) + bx[:, sl]))
    r, ig = jnp.concatenate(rs, axis=1), jnp.concatenate(igs, axis=1)
    la = -LRU_C * r * sp
    return r, ig, jnp.exp(la), _neg_expm1(2.0 * la)


def _conv_taps(x_ext, halo):
    n = x_ext.shape[0]
    return [x_ext[halo:] if k == CONV_WIDTH - 1 else pltpu.roll(x_ext, CONV_WIDTH - 1 - k, 0)[halo:]
            for k in range(CONV_WIDTH)]


def _lru_fwd(proj, cw, cb, wa, ba, wx, bx, lam, *, name, tc=512):
    L = proj.shape[0]
    W = LRU_HEADS * LRU_HEAD_DIM
    nb = L // tc
    whole = lambda shape: pl.BlockSpec(shape, lambda i: (0,) * len(shape))
    specs = [pl.BlockSpec((tc, W), lambda i: (i, 0)), pl.BlockSpec((tc, W), lambda i: (i, 1)),
             pl.BlockSpec((16, W), lambda i: (jnp.maximum(i * (tc // 16) - 1, 0), 0)),
             whole((CONV_WIDTH, W)), whole((1, W)), whole((LRU_HEADS, LRU_HEAD_DIM, LRU_HEAD_DIM)), whole((1, W)),
             whole((LRU_HEADS, LRU_HEAD_DIM, LRU_HEAD_DIM)), whole((1, W)), whole((1, W))]
    out_specs = [pl.BlockSpec((tc, W), lambda i: (i, 0))] * 2
    out_shape = [jax.ShapeDtypeStruct((L, W), BF16), jax.ShapeDtypeStruct((L, W), F32)]

    def body(x_ref, y_ref, xh_ref, cw_ref, cb_ref, wa_ref, ba_ref, wx_ref, bx_ref, lam_ref, rec_ref, hs_ref,
             hcar, a_scr, b_scr):
        i = pl.program_id(0)

        @pl.when(i == 0)
        def _():
            hcar[...] = jnp.zeros_like(hcar)

        halo = jnp.where(i > 0, xh_ref[...].astype(F32), 0.0)
        taps = _conv_taps(jnp.concatenate([halo, x_ref[...].astype(F32)], axis=0), 16)
        xc = cb_ref[...] + sum(cw_ref[k:k + 1, :] * taps[k] for k in range(CONV_WIDTH))
        _, ig, a, om = _lru_gates(xc, wa_ref, ba_ref[...], wx_ref, bx_ref[...], _softplus(-lam_ref[...]))
        b = jnp.sqrt(om) * (ig * xc)
        rowmod = lax.broadcasted_iota(jnp.int32, (tc, W), 0) & 7
        for s in (1, 2, 4):
            keep = rowmod >= s
            b = jnp.where(keep, a * pltpu.roll(b, s, 0) + b, b)
            a = jnp.where(keep, a * pltpu.roll(a, s, 0), a)
        a_scr[...] = a
        b_scr[...] = b

        def tile(j, hc):
            rows = pl.ds(pl.multiple_of(j * 8, 8), 8)
            ht = a_scr[rows, :] * hc + b_scr[rows, :]
            hs_ref[rows, :] = ht
            return jnp.broadcast_to(ht[7:8, :], (8, W))

        hcar[...] = lax.fori_loop(0, tc // 8, tile, hcar[...])
        gy, _ = _gelu_parts(y_ref[...].astype(F32))
        rec_ref[...] = (hs_ref[...] * gy).astype(BF16)

    return _call(body, name=name, grid=(nb,), in_specs=specs, out_specs=out_specs, out_shape=out_shape,
                 scratch=[pltpu.VMEM((8, W), F32), pltpu.VMEM((tc, W), F32), pltpu.VMEM((tc, W), F32)],
                 sem=("arbitrary",))(proj, proj, proj, cw, cb, wa, ba, wx, bx, lam)


def _lru_bwd(proj, hs, drec_src, cw, cb, wa, ba, wx, bx, lam, *, name, tc=256):
    L = proj.shape[0]
    W = LRU_HEADS * LRU_HEAD_DIM
    nb = L // tc
    tb = lambda i: nb - 1 - i
    whole = lambda shape: pl.BlockSpec(shape, lambda i: (0,) * len(shape))
    gate_w = (LRU_HEADS, LRU_HEAD_DIM, LRU_HEAD_DIM)
    specs = [pl.BlockSpec((tc, W), lambda i: (tb(i), 0)), pl.BlockSpec((tc, W), lambda i: (tb(i), 1)),
             pl.BlockSpec((16, W), lambda i: (jnp.maximum(tb(i) * (tc // 16) - 1, 0), 0)),
             pl.BlockSpec((tc, W), lambda i: (tb(i), 0)),
             pl.BlockSpec((8, W), lambda i: (jnp.maximum(tb(i) * (tc // 8) - 1, 0), 0)),
             pl.BlockSpec((tc, W), lambda i: (tb(i), 0)),
             whole((CONV_WIDTH, W)), whole((1, W)), whole(gate_w), whole((1, W)), whole(gate_w), whole((1, W)), whole((1, W))]
    out_specs = [pl.BlockSpec((tc, 2 * W), lambda i: (tb(i), 0)), whole((CONV_WIDTH, W)), whole((1, W)), whole(gate_w),
                 whole((1, W)), whole(gate_w), whole((1, W)), whole((1, W))]
    vec = jax.ShapeDtypeStruct((1, W), F32)
    out_shape = [jax.ShapeDtypeStruct((L, 2 * W), BF16), jax.ShapeDtypeStruct((CONV_WIDTH, W), F32), vec,
                 jax.ShapeDtypeStruct(gate_w, F32), vec, jax.ShapeDtypeStruct(gate_w, F32), vec, vec]

    def body(x_ref, y_ref, xh_ref, hs_ref, hh_ref, dr_ref, cw_ref, cb_ref, wa_ref, ba_ref, wx_ref, bx_ref, lam_ref,
             dxy_ref, dcw_ref, dcb_ref, dwa_ref, dba_ref, dwx_ref, dbx_ref, dlam_ref, gcar, dxc_car, a_scr, b_scr, g_scr):
        pid = pl.program_id(0)
        t = tb(pid)
        accs = (dcw_ref, dcb_ref, dwa_ref, dba_ref, dwx_ref, dbx_ref, dlam_ref)

        @pl.when(pid == 0)
        def _():
            gcar[...] = jnp.zeros_like(gcar)
            dxc_car[...] = jnp.zeros_like(dxc_car)
            for r in accs:
                r[...] = jnp.zeros_like(r)

        halo = jnp.where(t > 0, xh_ref[...].astype(F32), 0.0)
        taps = _conv_taps(jnp.concatenate([halo, x_ref[...].astype(F32)], axis=0), 16)
        xc = cb_ref[...] + sum(cw_ref[k:k + 1, :] * taps[k] for k in range(CONV_WIDTH))
        lam = lam_ref[...]
        sp = _softplus(-lam)
        r, ig, a, om = _lru_gates(xc, wa_ref, ba_ref[...], wx_ref, bx_ref[...], sp)
        sq = jnp.sqrt(om)
        hblk = hs_ref[...]
        hprev = pltpu.roll(jnp.concatenate([jnp.where(t > 0, hh_ref[...], 0.0), hblk], axis=0), 1, 0)[8:]
        gy, dgy = _gelu_parts(y_ref[...].astype(F32))
        drec = dr_ref[...].astype(F32)
        dxy_ref[:, W:] = (drec * hblk * dgy).astype(BF16)

        rowidx = lax.broadcasted_iota(jnp.int32, (tc, W), 0)
        rowmod = rowidx & 7
        ca = jnp.where(rowidx == tc - 1, 1.0, pltpu.roll(a, tc - 1, 0))
        cbv = drec * gy
        for s in (1, 2, 4):
            keep = rowmod < 8 - s
            cbv = jnp.where(keep, ca * pltpu.roll(cbv, tc - s, 0) + cbv, cbv)
            ca = jnp.where(keep, ca * pltpu.roll(ca, tc - s, 0), ca)
        a_scr[...] = ca
        b_scr[...] = cbv

        def tile(k, gc):
            j = tc // 8 - 1 - k
            rows = pl.ds(pl.multiple_of(j * 8, 8), 8)
            gt = a_scr[rows, :] * gc + b_scr[rows, :]
            g_scr[rows, :] = gt
            return jnp.broadcast_to(gt[0:1, :], (8, W))

        lax.fori_loop(0, tc // 8, tile, gcar[...])
        G = g_scr[...]
        gcar[...] = jnp.broadcast_to(a[0:1, :] * G[0:1, :], (8, W))

        da = G * hprev
        dsq = G * (ig * xc)
        di = G * (sq * xc)
        dxc = G * (sq * ig)
        dla = da * a - 2.0 * a * a * (dsq * 0.5 * lax.rsqrt(om))
        dlam_ref[...] += jnp.sum(dla * (-LRU_C * r), axis=0, keepdims=True) * (-_sigmoid(-lam))
        dpr = dla * (-LRU_C * sp) * r * (1.0 - r)
        dpi = di * ig * (1.0 - ig)
        dba_ref[...] += jnp.sum(dpr, axis=0, keepdims=True)
        dbx_ref[...] += jnp.sum(dpi, axis=0, keepdims=True)
        back = []
        for hd in range(LRU_HEADS):
            sl = slice(hd * LRU_HEAD_DIM, (hd + 1) * LRU_HEAD_DIM)
            xh, dprh, dpih = xc[:, sl].astype(BF16), dpr[:, sl].astype(BF16), dpi[:, sl].astype(BF16)
            back.append(_dot(dprh, wa_ref[hd], NT) + _dot(dpih, wx_ref[hd], NT))
            dwa_ref[hd] += _dot(xh, dprh, TN)
            dwx_ref[hd] += _dot(xh, dpih, TN)
        dxc = dxc + jnp.concatenate(back, axis=1)
        dcb_ref[...] += jnp.sum(dxc, axis=0, keepdims=True)
        for k in range(CONV_WIDTH):
            dcw_ref[k:k + 1, :] += jnp.sum(dxc * taps[k], axis=0, keepdims=True)
        ext = jnp.concatenate([dxc, dxc_car[...]], axis=0)
        dx = cw_ref[CONV_WIDTH - 1:CONV_WIDTH, :] * dxc
        for k in range(CONV_WIDTH - 1):
            dx = dx + cw_ref[k:k + 1, :] * pltpu.roll(ext, tc + 8 - (CONV_WIDTH - 1 - k), 0)[:tc]
        dxc_car[...] = dxc[0:8, :]
        dxy_ref[:, :W] = dx.astype(BF16)

    scratch = [pltpu.VMEM((8, W), F32), pltpu.VMEM((8, W), F32)] + [pltpu.VMEM((tc, W), F32)] * 3
    return _call(body, name=name, grid=(nb,), in_specs=specs, out_specs=out_specs, out_shape=out_shape, scratch=scratch,
                 sem=("arbitrary",))(proj, proj, proj, hs, hs, drec_src, cw, cb, wa, ba, wx, bx, lam)


def _final_loss(h, gain, target, *, name, tm=512):
    M, K = h.shape
    row = pl.BlockSpec((tm, K), lambda i: (i, 0))
    vec = pl.BlockSpec((1, K), lambda i: (0, 0))
    one = pl.BlockSpec((1, 128), lambda i: (0, 0))

    def body(h_ref, g_ref, t_ref, dh_ref, dg_ref, loss_ref):
        @pl.when(pl.program_id(0) == 0)
        def _():
            dg_ref[...] = jnp.zeros_like(dg_ref)
            loss_ref[...] = jnp.zeros_like(loss_ref)

        x = h_ref[...]
        r = lax.rsqrt(jnp.mean(x * x, axis=-1, keepdims=True) + NORM_EPS)
        xhat = x * r
        err = xhat * g_ref[...] - t_ref[...]
        loss_ref[...] += 0.5 / K * jnp.sum(err * err)
        dy = err * (1.0 / K)
        dg_ref[...] += jnp.sum(dy * xhat, axis=0, keepdims=True)
        dxh = dy * g_ref[...]
        dh_ref[...] = r * (dxh - xhat * jnp.mean(dxh * xhat, axis=-1, keepdims=True))

    return _call(body, name=name, grid=(M // tm,), in_specs=[row, vec, row], out_specs=[row, vec, one],
                 out_shape=[jax.ShapeDtypeStruct((M, K), F32), jax.ShapeDtypeStruct((1, K), F32),
                            jax.ShapeDtypeStruct((1, 128), F32)], sem=("arbitrary",))(h, gain.reshape(1, K), target)


def _dilated_merge(branches, *, name, tm=1024):
    L, W = branches[0].shape
    nbr = len(branches) // 2
    row = pl.BlockSpec((tm, W), lambda i: (i, 0))

    def body(*refs):
        o_ref, lse_ref = refs[-2], refs[-1]
        lses = [refs[2 * b + 1][...] for b in range(nbr)]
        m = lses[0]
        for t in lses[1:]:
            m = jnp.maximum(m, t)
        ws = [jnp.exp(t - m) for t in lses]
        den = ws[0]
        for t in ws[1:]:
            den = den + t
        acc = ws[0] * refs[0][...].astype(F32)
        for b in range(1, nbr):
            acc = acc + ws[b] * refs[2 * b][...].astype(F32)
        o_ref[...] = (acc / den).astype(BF16)
        lse_ref[...] = m + jnp.log(den)

    return _call(body, name=name, grid=(L // tm,), in_specs=[row] * (2 * nbr), out_specs=[row, row],
                 out_shape=[jax.ShapeDtypeStruct((L, W), BF16), jax.ShapeDtypeStruct((L, W), F32)], sem=("parallel",))(*branches)


def _dilated_fwd(proj0):
    L = proj0.shape[0]
    qkv = proj0[:, 2 * D_MODEL:]
    W = B_HEADS * HEAD_DIM
    outs = []
    for window, d in DILATED_PATTERN:
        view = qkv.reshape(L // d, d * 3 * W)
        o, lse = _band_fwd(view, view, view, d=d, nq=B_HEADS, nkv=B_HEADS, qcol=lambda r: 3 * r, kcol=lambda r: 3 * r + 1,
                           vcol=lambda r: 3 * r + 2, max_dist=window // d, name=f"dilated_fwd_d{d}")
        outs += [o.reshape(L, W), lse.reshape(L, W)]
    return _dilated_merge(outs, name="dilated_merge")


def _dilated_bwd(proj0, att, lse, datt, tabs):
    L = proj0.shape[0]
    qkv = proj0[:, 2 * D_MODEL:]
    Wh = B_HEADS * HEAD_DIM
    branches = []
    for window, d in DILATED_PATTERN:
        view = qkv.reshape(L // d, d * 3 * Wh)
        v1 = lambda t: t.reshape(L // d, d * Wh)
        outs = _band_bwd(view, view, view, v1(datt), v1(att), v1(lse), d=d, nq=B_HEADS, nkv=B_HEADS,
                         qcol=lambda r: 3 * r, kcol=lambda r: 3 * r + 1, vcol=lambda r: 3 * r + 2, docol=lambda r: r,
                         max_dist=window // d, name=f"dilated_bwd_d{d}")
        branches.append([o.reshape(L, Wh) for o in outs])
    return _attn_grad_combine(branches, tabs, name="dilated_grad_combine")


def _device_step(x, mem, target, w, on_grads=None):
    L = x.shape[0]
    tabs = _rope_tables(L)
    g = {}
    saved = []
    h = x
    for layer in range(2):
        sv = {"h_mix": h}
        if layer == 0:
            proj, n = _rowmm(h, w["ab_w_in"], name="l0_in_proj", gain=w["mix_norm"][0],
                             rope=(2 * D_MODEL, 2 * D_MODEL + 2 * B_HEADS * HEAD_DIM, tabs))
            rec, hs = _lru_fwd(proj, w["lru_conv_w"], w["lru_conv_b"], w["lru_wa"], w["lru_ba"], w["lru_wx"], w["lru_bx"],
                               w["lru_lambda"], name="lru_fwd")
            att, lse = _dilated_fwd(proj)
            mix = jnp.concatenate([rec, att], axis=1)
            (h,) = _rowmm(mix, w["ab_w_out"], name="l0_out_proj", res=h)
            sv.update(hs=hs)
        else:
            proj, n = _rowmm(h, w["c_w_qkv"], name="l1_qkv_proj", gain=w["mix_norm"][1], bias=w["c_b_qkv"],
                             rope=(0, (C_HEADS + C_KV_HEADS) * HEAD_DIM, tabs))
            mix, lse = _band_fwd(proj, proj, proj, d=1, nq=C_HEADS, nkv=C_KV_HEADS, qcol=lambda r: 0, kcol=lambda r: 8,
                                 vcol=lambda r: 9, max_dist=C_WINDOW - 1, sinks=w["c_sinks"], name="swa_fwd")
            (h,) = _rowmm(mix, w["c_w_out"], name="l1_out_proj", res=h, bias=w["c_b_out"])
        sv.update(proj=proj, n_mix=n, mix=mix, lse=lse, h_xa=h)
        xq, nx = _rowmm(h, w["xa_wq"][layer][None], name=f"xa_q_proj{layer}", gain=w["xa_norm"][layer])
        kv, nm = _rowmm(mem, w["xa_wkv"][layer][None], name=f"xa_kv_proj{layer}", gain=w["xa_mem_norm"][layer])
        xo, xlse = _xattn_fwd(xq, kv, name=f"xa_fwd{layer}")
        (h,) = _rowmm(xo, w["xa_wo"][layer], name=f"xa_out_proj{layer}", res=h)
        sv.update(xq=xq, nx=nx, kv=kv, nm=nm, xo=xo, xlse=xlse, h_ffn=h)
        gu, nf, act = _rowmm(h, w["ffn_w_gate_up"], layer=layer, name=f"ffn_in{layer}", gain=w["ffn_norm"][layer], swiglu=True)
        (h,) = _rowmm(act, w["ffn_w_down"][layer][None], name=f"ffn_out{layer}", res=h, tm=512)
        sv.update(gu=gu, nf=nf, act=act)
        saved.append(sv)

    dh, g["final_norm"], loss = _final_loss(h, w["final_norm"], target, name="final_loss")

    stk = {k: [None, None] for k in ("xa_norm", "xa_mem_norm", "ffn_norm", "mix_norm")}
    after = None
    for layer in (1, 0):
        sv = saved[layer]
        (g["ffn_w_down", layer],) = _mm_tn(sv["act"], dh, S=1, name=f"ffn_down_dw{layer}", kk=D_FF // 2)
        (dgu,) = _mm_nt(dh, w["ffn_w_down"][layer][None], name=f"ffn_dact{layer}", mode="swiglu", kchunk=D_FF // 2, gu=sv["gu"],
                        after=after)
        (g["ffn_w_gate_up", layer],) = _mm_tn(sv["nf"], dgu, S=N_CHIPS, name=f"ffn_gu_dw{layer}")
        dh, stk["ffn_norm"][layer] = _mm_nt(dgu, w["ffn_w_gate_up"], layer=layer, name=f"ffn_dx{layer}", mode="norm",
                                            h=sv["h_ffn"], gain=w["ffn_norm"][layer], dh=dh)
        (g["xa_wo", layer],) = _mm_tn(sv["xo"], dh, S=N_CHIPS, name=f"xa_wo_dw{layer}")
        (dxo,) = _mm_nt(dh, w["xa_wo"][layer], name=f"xa_dxo{layer}", mode="plain")
        dxq, dkv = _xattn_bwd(sv["xq"], sv["kv"], sv["xo"], sv["xlse"], dxo, name=f"xa_bwd{layer}")
        (g["xa_wq", layer],) = _mm_tn(sv["nx"], dxq, S=1, name=f"xa_wq_dw{layer}")
        dh, stk["xa_norm"][layer] = _mm_nt(dxq, w["xa_wq"][layer][None], name=f"xa_dx{layer}", mode="norm", h=sv["h_xa"],
                                           gain=w["xa_norm"][layer], dh=dh)
        (g["xa_wkv", layer],) = _mm_tn(sv["nm"], dkv, S=1, name=f"xa_wkv_dw{layer}")
        _, stk["xa_mem_norm"][layer] = _mm_nt(dkv, w["xa_wkv"][layer][None], name=f"xa_dmem{layer}", mode="norm", h=mem,
                                              gain=w["xa_mem_norm"][layer])
        if layer == 1:
            g["c_w_out"], g["c_b_out"] = _mm_tn(sv["mix"], dh, S=1, name="l1_out_dw", bias=True)
            (dmix,) = _mm_nt(dh, w["c_w_out"], name="l1_dmix", mode="plain")
            dq, dk, dv, dsk = _band_bwd(sv["proj"], sv["proj"], sv["proj"], dmix, sv["mix"], sv["lse"], d=1, nq=C_HEADS,
                                        nkv=C_KV_HEADS, qcol=lambda r: 0, kcol=lambda r: 8, vcol=lambda r: 9,
                                        docol=lambda r: 0, max_dist=C_WINDOW - 1, sinks=w["c_sinks"], name="swa_bwd")
            g["c_sinks"] = dsk[0, :C_HEADS]
            dproj = _attn_grad_combine([(dq, dk, dv)], tabs, name="swa_grad_combine")
            g["c_w_qkv"], g["c_b_qkv"] = _mm_tn(sv["n_mix"], dproj, S=1, name="l1_qkv_dw", bias=True)
            dh, stk["mix_norm"][1] = _mm_nt(dproj, w["c_w_qkv"], name="l1_dx", mode="norm", h=sv["h_mix"],
                                            gain=w["mix_norm"][1], dh=dh)
            if on_grads is not None:
                after = on_grads("layer1", g)
        else:
            if on_grads is not None:
                after = on_grads("layer0_ffn_xa", g)
            (g["ab_w_out"],) = _mm_tn(sv["mix"], dh, S=1, name="l0_out_dw", kk=768)
            (dmix,) = _mm_nt(dh, w["ab_w_out"], name="l0_dmix", mode="plain", kchunk=768, after=after)
            (dxy, g["lru_conv_w"], g["lru_conv_b"], g["lru_wa"], g["lru_ba"], g["lru_wx"], g["lru_bx"],
             g["lru_lambda"]) = _lru_bwd(sv["proj"], sv["hs"], dmix, w["lru_conv_w"], w["lru_conv_b"], w["lru_wa"],
                                         w["lru_ba"], w["lru_wx"], w["lru_bx"], w["lru_lambda"], name="lru_bwd")
            dqkv = _dilated_bwd(sv["proj"], sv["mix"][:, D_MODEL:], sv["lse"], dmix[:, D_MODEL:], tabs)
            dproj = jnp.concatenate([dxy, dqkv], axis=1)
            (g["ab_w_in"],) = _mm_tn(sv["n_mix"], dproj, S=N_CHIPS, name="l0_in_dw")
            dh, stk["mix_norm"][0] = _mm_nt(dproj, w["ab_w_in"], name="l0_dx", mode="norm", h=sv["h_mix"],
                                            gain=w["mix_norm"][0], dh=dh)
    for k, v in stk.items():
        g[k] = jnp.concatenate(v, axis=0)
    return loss[0, 0], dh, g


ANY = pl.BlockSpec(memory_space=pl.ANY)
MESH = pl.DeviceIdType.MESH


def _place():
    x, y, c = lax.axis_index("x"), lax.axis_index("y"), lax.axis_index("c")
    return x, y, c, [(1 - x, y), (x, 1 - y), (1 - x, 1 - y)]


def _remote(send_sems, recv_sems):
    def copy(k, src, dst, to):
        return pltpu.make_async_remote_copy(src_ref=src, dst_ref=dst, send_sem=send_sems.at[k], recv_sem=recv_sems.at[k],
                                            device_id=to, device_id_type=MESH)
    return copy


def _halves(ref, n_rows):
    rh = n_rows // 2
    return lambda lead, hh: ref.at[(*lead, pl.ds(hh * rh, rh), slice(None))]


def _gather_weights(packs, spack):
    n = len(packs)

    def body(*refs):
        w_refs, s_ref, wf_refs, sf_ref = refs[:n], refs[n], refs[n + 1:2 * n + 1], refs[2 * n + 1]
        x, y, c, chips = _place()
        me, sib = 2 * x + y, (x, y, 1 - c)
        copy = _remote(*refs[-2:])
        src = [_halves(w_refs[g], packs[g].shape[0]) for g in range(n)]
        dst = [_halves(wf_refs[g], packs[g].shape[0]) for g in range(n)]
        sends = []
        for g in range(n):
            for j, (cx, cy) in enumerate(chips):
                sends.append(copy(3 * g + j, src[g]((), c), dst[g]((me,), c), (cx, cy, c)))
        for j, (cx, cy) in enumerate(chips):
            sends.append(copy(6 * n + j, s_ref, sf_ref.at[me], (cx, cy, c)))
        for cp in sends:
            cp.start()
        for g in range(n):
            for j, (cx, cy) in enumerate(chips):
                got = dst[g]((2 * cx + cy,), c)
                copy(3 * g + j, got, got, sib).wait_recv()
                fwd = copy(3 * n + 3 * g + j, got, got, sib)
                fwd.start()
                sends.append(fwd)
        for g in range(n):
            for j, (cx, cy) in enumerate(chips):
                got = dst[g]((2 * cx + cy,), 1 - c)
                copy(3 * n + 3 * g + j, got, got, sib).wait_recv()
        for j, (cx, cy) in enumerate(chips):
            copy(6 * n + j, s_ref, sf_ref.at[2 * cx + cy], sib).wait_recv()
        for cp in sends:
            cp.wait_send()

    ins = list(packs) + [spack]
    out_shape = [jax.ShapeDtypeStruct((N_CHIPS,) + a.shape, a.dtype) for a in ins]
    n_sems = 6 * n + 3
    outs = pl.pallas_call(body, name="gather_weights", out_shape=out_shape, in_specs=[ANY] * len(ins),
                          out_specs=[ANY] * len(ins),
                          scratch_shapes=[pltpu.SemaphoreType.DMA((n_sems,)), pltpu.SemaphoreType.DMA((n_sems,))])(*ins)
    chip = 2 * lax.axis_index("x") + lax.axis_index("y")
    outs = [lax.dynamic_update_index_in_dim(o, a, chip, 0) for o, a in zip(outs, ins)]
    return outs[:n], outs[n]


SEQUENCER_GATHER_IDS = {"mid": 1, "late": 5}


def _gather_weights_behind(packs, *, tag):
    n = len(packs)

    def body(*refs):
        w_refs, wf_refs = refs[:n], refs[n:2 * n]
        x, y, c, chips = _place()
        me, sib = 2 * x + y, (x, y, 1 - c)
        barrier = pltpu.get_barrier_semaphore()
        for peer in [(cx, cy, c) for cx, cy in chips] + [sib]:
            pl.semaphore_signal(barrier, inc=1, device_id=peer, device_id_type=MESH)
        pl.semaphore_wait(barrier, len(chips) + 1)
        copy = _remote(*refs[-2:])
        src = [_halves(w_refs[g], packs[g].shape[0]) for g in range(n)]
        dst = [_halves(wf_refs[g], packs[g].shape[0]) for g in range(n)]
        sends = []
        for g in range(n):
            for j, (cx, cy) in enumerate(chips):
                sends.append(copy(3 * g + j, src[g]((), c), dst[g]((me,), c), (cx, cy, c)))
        for cp in sends:
            cp.start()
        for g in range(n):
            for j, (cx, cy) in enumerate(chips):
                got = dst[g]((2 * cx + cy,), c)
                copy(3 * g + j, got, got, sib).wait_recv()
                fwd = copy(3 * n + 3 * g + j, got, got, sib)
                fwd.start()
                sends.append(fwd)
        for g in range(n):
            for j, (cx, cy) in enumerate(chips):
                got = dst[g]((2 * cx + cy,), 1 - c)
                copy(3 * n + 3 * g + j, got, got, sib).wait_recv()
        for cp in sends:
            cp.wait_send()

    out_type = [jax.ShapeDtypeStruct((N_CHIPS,) + a.shape, a.dtype) for a in packs]
    outs = pl.kernel(body, out_type=out_type, mesh=plsc.ScalarSubcoreMesh(axis_name="sequencer", num_cores=1),
                     name="gather_weights_behind_" + tag,
                     scratch_types=[pltpu.SemaphoreType.DMA((6 * n,)), pltpu.SemaphoreType.DMA((6 * n,))],
                     compiler_params=pltpu.CompilerParams(collective_id=SEQUENCER_GATHER_IDS[tag]))(*packs)
    chip = 2 * lax.axis_index("x") + lax.axis_index("y")
    return [lax.dynamic_update_index_in_dim(o, a, chip, 0) for o, a in zip(outs, packs)]


def _rs_pair_exchange(gpacks, *, name):
    n = len(gpacks)

    def body(*refs):
        g_refs, ra_refs = refs[:n], refs[n:2 * n]
        x, y, c, _ = _place()
        copy = _remote(*refs[-2:])
        cps = []
        for g in range(n):
            half = _halves(g_refs[g], gpacks[g].shape[1])
            cps += [copy(N_CHIPS * g + j, half((j,), 1 - c), ra_refs[g].at[j], (x, y, 1 - c)) for j in range(N_CHIPS)]
        for cp in cps:
            cp.start()
        for cp in cps:
            cp.wait()

    out_shape = [jax.ShapeDtypeStruct((N_CHIPS, a.shape[1] // 2, a.shape[2]), a.dtype) for a in gpacks]
    n_sems = N_CHIPS * n
    return pl.pallas_call(body, name=name, out_shape=out_shape, in_specs=[ANY] * n, out_specs=[ANY] * n,
                          scratch_shapes=[pltpu.SemaphoreType.DMA((n_sems,)), pltpu.SemaphoreType.DMA((n_sems,))])(*gpacks)


def _row_tile(rows, cap=1024):
    return max(t for t in range(16, min(rows, cap) + 1, 16) if rows % t == 0)


def _rs_pair_add(place, gpack, ra, *, name):
    _, R, C = gpack.shape
    Rh = R // 2
    tr = _row_tile(Rh)
    nrb = Rh // tr

    def body(p_ref, g_ref, ra_ref, pair_ref, own_ref):
        s = g_ref[...].astype(F32) + ra_ref[...].astype(F32)
        pair_ref[...] = s.astype(BF16)

        @pl.when(pl.program_id(1) == p_ref[1])
        def _():
            own_ref[...] = s

    grid_spec = pltpu.PrefetchScalarGridSpec(
        num_scalar_prefetch=1, grid=(nrb, N_CHIPS),
        in_specs=[pl.BlockSpec((None, tr, C), lambda i, j, p: (j, p[0] * nrb + i, 0)),
                  pl.BlockSpec((None, tr, C), lambda i, j, p: (j, i, 0))],
        out_specs=[pl.BlockSpec((None, tr, C), lambda i, j, p: (j, i, 0)), pl.BlockSpec((tr, C), lambda i, j, p: (i, 0))])
    return pl.pallas_call(
        body, name=name, grid_spec=grid_spec,
        out_shape=[jax.ShapeDtypeStruct((N_CHIPS, Rh, C), BF16), jax.ShapeDtypeStruct((Rh, C), F32)],
        compiler_params=pltpu.CompilerParams(dimension_semantics=("arbitrary", "arbitrary"),
                                             vmem_limit_bytes=VMEM_LIMIT_V7X))(place, gpack, ra)


SEQUENCER_EXCHANGE_IDS = {"l1": 2, "l0a": 3, "l0b": 4}


def _rs_chip_exchange_behind(pairs, *, tag, small=None):
    n = len(pairs)
    has_small = small is not None

    def body(*refs):
        p_refs = refs[:n]
        s_ref = refs[n] if has_small else None
        rb_refs = refs[n + has_small:2 * n + has_small]
        rs_ref = refs[2 * n + 1] if has_small else None
        x, y, c, chips = _place()
        peers = [(1 - x if k & 4 else x, 1 - y if k & 2 else y, 1 - c if k & 1 else c) for k in range(1, 8)]
        shake = peers if has_small else [(cx, cy, c) for cx, cy in chips]
        barrier = pltpu.get_barrier_semaphore()
        for peer in shake:
            pl.semaphore_signal(barrier, inc=1, device_id=peer, device_id_type=MESH)
        pl.semaphore_wait(barrier, len(shake))
        copy = _remote(*refs[-2:])
        cps = []
        for g in range(n):
            cps += [copy(3 * g + j, p_refs[g].at[2 * cx + cy], rb_refs[g].at[j], (cx, cy, c)) for j, (cx, cy) in enumerate(chips)]
        if has_small:
            dev = 4 * x + 2 * y + c
            cps += [copy(3 * n + k, s_ref, rs_ref.at[dev], peer) for k, peer in enumerate(peers)]
        for cp in cps:
            cp.start()
        for g in range(n):
            for j in range(3):
                copy(3 * g + j, p_refs[g].at[0], rb_refs[g].at[j], (x, y, c)).wait_recv()
        if has_small:
            for k, (px, py, pc) in enumerate(peers):
                copy(3 * n + k, s_ref, rs_ref.at[4 * px + 2 * py + pc], (x, y, c)).wait_recv()
        for cp in cps:
            cp.wait_send()

    ins = list(pairs) + ([small] if has_small else [])
    out_type = [jax.ShapeDtypeStruct((3,) + p.shape[1:], p.dtype) for p in pairs]
    if has_small:
        out_type.append(jax.ShapeDtypeStruct((8,) + small.shape, small.dtype))
    n_sems = 3 * n + 7 * has_small
    outs = pl.kernel(body, out_type=out_type, mesh=plsc.ScalarSubcoreMesh(axis_name="sequencer", num_cores=1),
                     name="rs_chip_exchange_behind_" + tag,
                     scratch_types=[pltpu.SemaphoreType.DMA((n_sems,)), pltpu.SemaphoreType.DMA((n_sems,))],
                     compiler_params=pltpu.CompilerParams(collective_id=SEQUENCER_EXCHANGE_IDS[tag]))(*ins)
    if has_small:
        dev = 4 * lax.axis_index("x") + 2 * lax.axis_index("y") + lax.axis_index("c")
        outs = list(outs[:n]) + [lax.dynamic_update_index_in_dim(outs[n], small, dev, 0)]
    return outs


def _rs_final_add(place, own, rb, *, name):
    Rh, C = own.shape
    tr = _row_tile(Rh)
    nrb = Rh // tr

    def body(p_ref, o_ref, rb_ref, f_ref):
        f_ref[...] = ((o_ref[...] + rb_ref[0].astype(F32)) + rb_ref[1].astype(F32)) + rb_ref[2].astype(F32)

    grid_spec = pltpu.PrefetchScalarGridSpec(
        num_scalar_prefetch=1, grid=(nrb,),
        in_specs=[pl.BlockSpec((tr, C), lambda i, p: (i, 0)), pl.BlockSpec((3, tr, C), lambda i, p: (0, i, 0))],
        out_specs=pl.BlockSpec((tr, C), lambda i, p: (p[0] * nrb + i, 0)))
    return pl.pallas_call(
        body, name=name, grid_spec=grid_spec, out_shape=jax.ShapeDtypeStruct((2 * Rh, C), F32),
        compiler_params=pltpu.CompilerParams(dimension_semantics=("arbitrary",), vmem_limit_bytes=VMEM_LIMIT_V7X))(place, own, rb)


def _sum_slots(rs):
    n, rows, C = rs.shape

    def body(r_ref, o_ref):
        acc = r_ref[0]
        for k in range(1, n):
            acc = acc + r_ref[k]
        o_ref[...] = acc

    return _call(body, name="small_grad_sum", grid=(1,), in_specs=[pl.BlockSpec((n, rows, C), lambda i: (0, 0, 0))],
                 out_specs=pl.BlockSpec((rows, C), lambda i: (0, 0)), out_shape=jax.ShapeDtypeStruct((rows, C), F32),
                 sem=("arbitrary",))(rs)


def _rs_sibling_share(gbufs, *, name):
    n = len(gbufs)

    def body(*refs):
        g_refs = refs[n:2 * n]
        x, y, c, _ = _place()
        copy = _remote(*refs[-2:])
        halves = [_halves(g_refs[g], gbufs[g].shape[0]) for g in range(n)]
        outs = [copy(g, halves[g]((), c), halves[g]((), c), (x, y, 1 - c)) for g in range(n)]
        for cp in outs:
            cp.start()
        for g in range(n):
            copy(g, halves[g]((), 1 - c), halves[g]((), 1 - c), (x, y, c)).wait_recv()
        for cp in outs:
            cp.wait_send()

    return pl.pallas_call(body, name=name, out_shape=[jax.ShapeDtypeStruct(a.shape, a.dtype) for a in gbufs],
                          in_specs=[ANY] * n, out_specs=[ANY] * n, input_output_aliases={g: g for g in range(n)},
                          scratch_shapes=[pltpu.SemaphoreType.DMA((n,)), pltpu.SemaphoreType.DMA((n,))])(*gbufs)


def _adamw(w, g, m, v, *, name, g_row=0):
    rows, cols = w.shape
    tr = rows
    for cand in range(min(rows, 512), 7, -8):
        if rows % cand == 0 and g_row % cand == 0:
            tr = cand
            break
    spec = pl.BlockSpec((tr, cols), lambda i: (i, 0))
    g_spec = pl.BlockSpec((tr, cols), lambda i: (g_row // tr + i, 0))

    def body(w_ref, g_ref, m_ref, v_ref, d_ref, nm_ref, nv_ref):
        gg = g_ref[...]
        nm = ADAM_B1 * m_ref[...] + (1.0 - ADAM_B1) * gg
        nv = ADAM_B2 * v_ref[...] + (1.0 - ADAM_B2) * (gg * gg)
        m_hat = nm / (1.0 - ADAM_B1 ** ADAM_STEP)
        v_hat = nv / (1.0 - ADAM_B2 ** ADAM_STEP)
        d_ref[...] = -ADAM_LR * (m_hat / (jnp.sqrt(v_hat) + ADAM_EPS) + ADAM_WD * w_ref[...])
        nm_ref[...] = nm
        nv_ref[...] = nv

    return _call(body, name=name, grid=(rows // tr,), in_specs=[spec, g_spec, spec, spec], out_specs=[spec] * 3,
                 out_shape=[jax.ShapeDtypeStruct((rows, cols), F32)] * 3, sem=("parallel",))(w, g, m, v)


WEIGHT_NAMES = ("mix_norm", "ab_w_in", "lru_conv_w", "lru_conv_b", "lru_wa", "lru_ba", "lru_wx", "lru_bx", "lru_lambda",
                "ab_w_out", "c_w_qkv", "c_b_qkv", "c_sinks", "c_w_out", "c_b_out", "xa_norm", "xa_mem_norm", "xa_wq",
                "xa_wkv", "xa_wo", "ffn_norm", "ffn_w_gate_up", "ffn_w_down", "final_norm")
EARLY_GROUPS = (("ab_w_in",),)
MID_GROUPS = (("ab_w_out",), ("lru_wa", "lru_wx"))
LATE_GROUPS = (("c_w_out", "xa_wkv", "ffn_w_down"), ("ffn_w_gate_up",), ("xa_wo",), ("xa_wq",), ("c_w_qkv",))
GROUPS = EARLY_GROUPS + MID_GROUPS + LATE_GROUPS
REPLICATED = ("mix_norm", "lru_conv_b", "lru_lambda", "c_sinks", "xa_norm", "xa_mem_norm", "ffn_norm", "final_norm")
SMALL_SHARDED = ("lru_conv_w", "lru_ba", "lru_bx", "c_b_qkv", "c_b_out")
LANES = 1024


def _rows(v):
    flat = v.reshape(-1)
    return jnp.pad(flat, (0, -flat.shape[0] % LANES)).reshape(-1, LANES)


def _pack_small(parts, total, *, name):
    def body(*refs):
        o_ref = refs[-1]
        o_ref[...] = jnp.zeros_like(o_ref)
        row = 0
        for p_ref in refs[:-1]:
            o_ref[row:row + p_ref.shape[0], :] = p_ref[...]
            row += p_ref.shape[0]

    return _call(body, name=name, grid=(1,), in_specs=[pl.BlockSpec(p.shape, lambda i: (0, 0)) for p in parts],
                 out_specs=pl.BlockSpec((total, LANES), lambda i: (0, 0)),
                 out_shape=jax.ShapeDtypeStruct((total, LANES), F32), sem=("arbitrary",))(*parts)


def _from_shards(name, t):
    minor = t.shape[-1]
    if name == "ab_w_in":
        return t
    if name in ("ab_w_out", "c_w_out"):
        return t.reshape(1, -1, minor)
    if name == "ffn_w_gate_up":
        return t.reshape(N_CHIPS, 2, -1, minor)
    if name in ("xa_wq", "xa_wkv", "ffn_w_down"):
        return t.reshape(N_CHIPS, 2, -1, minor).transpose(1, 0, 2, 3).reshape(2, -1, minor)
    if name in ("lru_wa", "lru_wx"):
        return t.reshape(N_CHIPS, LRU_HEADS, -1, minor).transpose(1, 0, 2, 3).reshape(LRU_HEADS, LRU_HEAD_DIM, minor)
    if name == "xa_wo":
        return t.reshape(N_CHIPS, 2, -1, minor).transpose(1, 0, 2, 3)
    assert name == "c_w_qkv"
    return t.transpose(1, 0, 2).reshape(1, D_MODEL, -1)


def _piece_shards(name, g):
    minor = g.shape[-1]
    if name in ("ab_w_in", "ffn_w_gate_up", "xa_wo"):
        return g
    if name in ("ab_w_out", "c_w_out", "xa_wq", "xa_wkv", "ffn_w_down"):
        return g.reshape(N_CHIPS, -1, minor)
    if name in ("lru_wa", "lru_wx"):
        return g.reshape(LRU_HEADS, N_CHIPS, -1, minor).transpose(1, 0, 2, 3).reshape(N_CHIPS, -1, minor)
    assert name == "c_w_qkv"
    return g.reshape(D_MODEL, N_CHIPS, -1).transpose(1, 0, 2)


RS_SETS = {
    "l1": ((("c_w_out", None), ("xa_wkv", 1), ("ffn_w_down", 1)), (("ffn_w_gate_up", 1),), (("xa_wo", 1),),
           (("xa_wq", 1),), (("c_w_qkv", None),)),
    "l0a": ((("xa_wkv", 0), ("ffn_w_down", 0)), (("ffn_w_gate_up", 0),), (("xa_wo", 0),), (("xa_wq", 0),)),
    "l0b": ((("ab_w_out", None),), (("ab_w_in", None),), (("lru_wa", None), ("lru_wx", None))),
}
RS_STAGE = {"layer1": "l1", "layer0_ffn_xa": "l0a"}


def kernel(x, mem, mix_norm, ab_w_in, lru_conv_w, lru_conv_b, lru_wa, lru_ba, lru_wx, lru_bx, lru_lambda, ab_w_out, c_w_qkv, c_b_qkv, c_sinks, c_w_out, c_b_out, xa_norm, xa_mem_norm, xa_wq, xa_wkv, xa_wo, ffn_norm, ffn_w_gate_up, ffn_w_down, final_norm, loss_target, m_mix_norm, m_ab_w_in, m_lru_conv_w, m_lru_conv_b, m_lru_wa, m_lru_ba, m_lru_wx, m_lru_bx, m_lru_lambda, m_ab_w_out, m_c_w_qkv, m_c_b_qkv, m_c_sinks, m_c_w_out, m_c_b_out, m_xa_norm, m_xa_mem_norm, m_xa_wq, m_xa_wkv, m_xa_wo, m_ffn_norm, m_ffn_w_gate_up, m_ffn_w_down, m_final_norm, v_mix_norm, v_ab_w_in, v_lru_conv_w, v_lru_conv_b, v_lru_wa, v_lru_ba, v_lru_wx, v_lru_bx, v_lru_lambda, v_ab_w_out, v_c_w_qkv, v_c_b_qkv, v_c_sinks, v_c_w_out, v_c_b_out, v_xa_norm, v_xa_mem_norm, v_xa_wq, v_xa_wkv, v_xa_wo, v_ffn_norm, v_ffn_w_gate_up, v_ffn_w_down, v_final_norm):
    given = dict(locals())
    wl = {n: given[n] for n in WEIGHT_NAMES}
    ml = {n: given["m_" + n] for n in WEIGHT_NAMES}
    vl = {n: given["v_" + n] for n in WEIGHT_NAMES}
    xi, yi, ci = lax.axis_index("x"), lax.axis_index("y"), lax.axis_index("c")
    chip = 2 * xi + yi

    def join(parts, axis):
        return parts[0] if len(parts) == 1 else jnp.concatenate(parts, axis=axis)

    local_rows = {n: wl[n].size // wl[n].shape[-1] for grp in GROUPS for n in grp}
    packs = [join([wl[n].astype(BF16).reshape(local_rows[n], wl[n].shape[-1]) for n in grp], 0) for grp in GROUPS]
    spack = _pack_small([_rows(wl[n]) for n in SMALL_SHARDED], 8, name="pack_small_weights")
    n_early, n_mid = len(EARLY_GROUPS), len(EARLY_GROUPS) + len(MID_GROUPS)
    early, sfull = _gather_weights(packs[:n_early], spack)
    early, sfull, mid_packs = lax.optimization_barrier((early, sfull, packs[n_early:n_mid]))
    mid = _gather_weights_behind(mid_packs, tag="mid")
    mid, late_packs = lax.optimization_barrier((mid, packs[n_mid:]))
    gathered = early + mid + _gather_weights_behind(late_packs, tag="late")
    w = {n: wl[n] for n in REPLICATED}
    w["c_sinks"] = wl["c_sinks"][0]
    for grp, full in zip(GROUPS, gathered):
        off = 0
        for n in grp:
            w[n] = _from_shards(n, full if len(grp) == 1 else full[:, off:off + local_rows[n]])
            off += local_rows[n]
    for r, n in enumerate(SMALL_SHARDED):
        loc = wl[n].shape[1:]
        t = sfull[:, r, :wl[n].size].reshape((N_CHIPS,) + loc)
        if n == "lru_conv_w":
            w[n] = t.transpose(1, 0, 2).reshape(CONV_WIDTH, -1)
        elif n in ("lru_ba", "lru_bx"):
            w[n] = t.transpose(1, 0, 2).reshape(1, -1)
        else:
            w[n] = t.reshape(1, -1)

    place = jnp.stack([ci, chip]).astype(jnp.int32)

    def pair_stage(spec, g, tag):
        piece = lambda n, l: (g[n] if l is None else g[n, l]).astype(BF16)
        gpacks = [join([_piece_shards(n, piece(n, l)) for n, l in grp], 1) for grp in spec]
        ras = _rs_pair_exchange(gpacks, name=f"rs_pair_exchange_{tag}")
        sums = [_rs_pair_add(place, gp, ra, name=f"rs_pair_add_{tag}_{i}") for i, (gp, ra) in enumerate(zip(gpacks, ras))]
        return [pair for pair, _ in sums], [own for _, own in sums]

    reduced, in_flight = [], []

    def take_up():
        done = [_rs_final_add(place, o, r, name=f"rs_final_add_{len(reduced) + i}") for i, (o, r) in enumerate(in_flight)]
        reduced.extend(done)
        in_flight.clear()
        return done

    def reduce_behind(stage, g):
        done = take_up()
        tag = RS_STAGE[stage]
        pairs, own = pair_stage(RS_SETS[tag], g, tag)
        in_flight.extend(zip(own, _rs_chip_exchange_behind(pairs, tag=tag)))
        return own + done

    loss_part, grad_x, g = _device_step(x[0], mem[0], loss_target[0], w, on_grads=reduce_behind)

    small_parts = [_rows(g[n]) for n in REPLICATED] + [_rows(jnp.broadcast_to(loss_part, (LANES,)))]
    small_parts += [_rows(g[n]) for n in SMALL_SHARDED]
    small = _pack_small(small_parts, 24, name="pack_small_grads")
    take_up()
    pairs, own = pair_stage(RS_SETS["l0b"], g, "l0b")
    *rb, rs = _rs_chip_exchange_behind(pairs, tag="l0b", small=small)
    gsums = list(_rs_sibling_share(list(reduced), name="rs_sibling_share_behind"))
    in_flight.extend(zip(own, rb))
    gsums += list(_rs_sibling_share(take_up(), name="rs_sibling_share_last"))
    ssum = _sum_slots(rs)

    where = {}
    for grp, gsum in zip(RS_SETS["l1"] + RS_SETS["l0a"] + RS_SETS["l0b"], gsums):
        off = 0
        for n, l in grp:
            rows = local_rows[n] if l is None else local_rows[n] // 2
            where[n, l] = (gsum, off, rows, len(grp) == 1)
            off += rows
    take = lambda gsum, off, rows, whole: gsum if whole else gsum[off:off + rows]
    grads, grad_rows = {}, {}
    for grp in LATE_GROUPS + EARLY_GROUPS + MID_GROUPS:
        for n in grp:
            if (n, None) in where:
                grads[n] = take(*where[n, None]).reshape(wl[n].shape)
                grad_rows[n] = where[n, None][:2]
            else:
                grads[n] = jnp.stack([take(*where[n, l]).reshape(wl[n].shape[1:]) for l in range(2)])
                grad_rows[n] = (grads[n].reshape(local_rows[n], wl[n].shape[-1]), 0)
    row = 0
    for n in REPLICATED:
        k = _rows(g[n]).shape[0]
        grads[n] = ssum[row:row + k].reshape(-1)[:wl[n].size].reshape(wl[n].shape)
        row += k
    loss = ssum[row, 0]
    row += 1
    for n in SMALL_SHARDED:
        k = _rows(g[n]).shape[0]
        full = ssum[row:row + k].reshape(-1)[:g[n].size]
        row += k
        loc = wl[n].shape
        if n == "lru_conv_w":
            sh = full.reshape(CONV_WIDTH, N_CHIPS, -1)
        elif n in ("lru_ba", "lru_bx"):
            sh = full.reshape(LRU_HEADS, N_CHIPS, -1)
        else:
            sh = full.reshape(1, N_CHIPS, -1)
        grads[n] = lax.dynamic_index_in_dim(sh, chip, axis=1, keepdims=False).reshape(loc)

    delta, new_m, new_v = {}, {}, {}
    for n, (gsum, off) in grad_rows.items():
        shape2 = (local_rows[n], wl[n].shape[-1])
        d, nm, nv = _adamw(wl[n].reshape(shape2), gsum, ml[n].reshape(shape2), vl[n].reshape(shape2), g_row=off,
                           name="adamw_" + n)
        delta[n], new_m[n], new_v[n] = (t.reshape(wl[n].shape) for t in (d, nm, nv))
    smalls = REPLICATED + SMALL_SHARDED
    packs = [_pack_small([_rows(src[n]) for n in smalls], 24, name="pack_adamw_" + tag)
             for tag, src in (("w", wl), ("g", grads), ("m", ml), ("v", vl))]
    outs = _adamw(*packs, name="adamw_small")
    row = 0
    for n in smalls:
        k = _rows(wl[n]).shape[0]
        for dst, o in zip((delta, new_m, new_v), outs):
            dst[n] = o[row:row + k].reshape(-1)[:wl[n].size].reshape(wl[n].shape)
        row += k

    return (loss, grad_x[None], *[grads[n] for n in WEIGHT_NAMES], *[delta[n] for n in WEIGHT_NAMES],
            *[new_m[n] for n in WEIGHT_NAMES], *[new_v[n] for n in WEIGHT_NAMES])
```

```python
import jax
import jax.numpy as jnp
from jax import lax
from jax.experimental import pallas as pl
from jax.experimental.pallas import tpu as pltpu
from jax.experimental.pallas import tpu_sc as plsc

F32, BF16 = jnp.float32, jnp.bfloat16
D_MODEL = 1024
NORM_EPS = 1e-6
ROPE_THETA = 500000.0
HEAD_DIM = 64
ROT_DIM = 16
BLK = 128
LRU_HEADS, LRU_HEAD_DIM, CONV_WIDTH, LRU_C = 4, 256, 4, 8.0
DILATED_PATTERN = ((128, 1), (512, 4), (2048, 16))
B_HEADS, C_HEADS, C_KV_HEADS, C_WINDOW = 8, 16, 2, 128
XA_HEADS, XA_HEAD_DIM, N_MEM = 4, 128, 256
D_FF = 2816
NEG = -1e30
ADAM_LR, ADAM_B1, ADAM_B2, ADAM_EPS, ADAM_WD, ADAM_STEP = 0.001, 0.9, 0.999, 1e-08, 0.01, 10
N_CHIPS = 4
VMEM_LIMIT_V7X = 56 * 1024 * 1024

NN = (((1,), (0,)), ((), ()))
NT = (((1,), (1,)), ((), ()))
TN = (((0,), (0,)), ((), ()))


def _dot(a, b, dims=NN):
    return lax.dot_general(a, b, dims, preferred_element_type=F32)


def _sigmoid(x):
    return 0.5 * jnp.tanh(0.5 * x) + 0.5


def _call(body, *, name, grid, in_specs, out_specs, out_shape, scratch=(), sem=None):
    return pl.pallas_call(
        body, name=name, grid=grid, in_specs=in_specs, out_specs=out_specs, out_shape=out_shape,
        scratch_shapes=list(scratch),
        compiler_params=pltpu.CompilerParams(dimension_semantics=sem, vmem_limit_bytes=VMEM_LIMIT_V7X))


def _rope_tables(L):
    half = ROT_DIM // 2
    inv = ROPE_THETA ** (-jnp.arange(0, ROT_DIM, 2, dtype=F32) / ROT_DIM)
    j = jnp.arange(2 * HEAD_DIM) % HEAD_DIM
    ang = jnp.arange(L, dtype=F32)[:, None] * inv[j % half][None, :]
    cos, sin = jnp.cos(ang), jnp.sin(ang)
    c = jnp.where(j < ROT_DIM, cos, 1.0)
    s1 = jnp.where(j < half, -sin, 0.0)
    s2 = jnp.where((j >= half) & (j < ROT_DIM), sin, 0.0)
    return c, s1, s2


def _rope_fwd(v, c, s1, s2):
    return v * c + pltpu.roll(v, 120, 1) * s1 + pltpu.roll(v, 8, 1) * s2


def _rope_bwd(dv, c, s1, s2):
    return dv * c + pltpu.roll(dv * s1, 8, 1) + pltpu.roll(dv * s2, 120, 1)


def _weight_spec(w, layer):
    once = pl.Buffered(1)
    if layer is None:
        return w.shape, pl.BlockSpec(w.shape, lambda i: (0, 0, 0), pipeline_mode=once)
    S, _, K, Ns = w.shape
    return (S, K, Ns), pl.BlockSpec((S, None, K, Ns), lambda i: (0, layer, 0, 0), pipeline_mode=once)


def _rowmm(a, w3, *, name, tm=512, gain=None, bias=None, res=None, swiglu=False, rope=None, layer=None):
    M, K = a.shape
    (S, _, Ns), w_spec = _weight_spec(w3, layer)
    N = S * Ns
    tm = min(tm, M)
    has_norm, has_bias, has_res, has_rope = gain is not None, bias is not None, res is not None, rope is not None
    row = lambda w: pl.BlockSpec((tm, w), lambda i: (i, 0))
    whole = lambda shape: pl.BlockSpec(shape, lambda i: (0,) * len(shape))
    ins, specs = [a], [row(K)]
    if has_norm:
        ins.append(gain.reshape(1, K)); specs.append(whole((1, K)))
    ins.append(w3); specs.append(w_spec)
    if has_bias:
        ins.append(bias.reshape(1, N)); specs.append(whole((1, N)))
    if has_res:
        ins.append(res); specs.append(row(N))
    if has_rope:
        ins += list(rope[2]); specs += [row(128)] * 3
    y_dtype = F32 if has_res else BF16
    out_shape, out_specs = [jax.ShapeDtypeStruct((M, N), y_dtype)], [row(N)]
    if has_norm:
        out_shape.append(jax.ShapeDtypeStruct((M, K), BF16)); out_specs.append(row(K))
    if swiglu:
        out_shape.append(jax.ShapeDtypeStruct((M, N // 2), BF16)); out_specs.append(row(N // 2))
    scratch = [pltpu.VMEM((tm, N), F32)] if has_rope else []

    def body(*refs):
        it = iter(refs)
        a_ref = next(it)
        g_ref = next(it) if has_norm else None
        w_ref = next(it)
        b_ref = next(it) if has_bias else None
        r_ref = next(it) if has_res else None
        tabs = [next(it) for _ in range(3)] if has_rope else None
        y_ref = next(it)
        n_ref = next(it) if has_norm else None
        act_ref = next(it) if swiglu else None
        ys_ref = next(it) if has_rope else None
        if has_norm:
            x = a_ref[...].astype(F32)
            ms = jnp.mean(x * x, axis=-1, keepdims=True)
            xb = (x * lax.rsqrt(ms + NORM_EPS) * g_ref[...]).astype(BF16)
            n_ref[...] = xb
        else:
            xb = a_ref[...].astype(BF16)
        if swiglu:
            for s in range(S // 2):
                g = _dot(xb, w_ref[s])
                u = _dot(xb, w_ref[s + S // 2])
                y_ref[:, s * Ns:(s + 1) * Ns] = g.astype(BF16)
                y_ref[:, N // 2 + s * Ns:N // 2 + (s + 1) * Ns] = u.astype(BF16)
                act_ref[:, s * Ns:(s + 1) * Ns] = (g * _sigmoid(g) * u).astype(BF16)
            return
        for s in range(S):
            sl = slice(s * Ns, (s + 1) * Ns)
            acc = _dot(xb, w_ref[s])
            if has_bias:
                acc = acc + b_ref[:, sl]
            if has_res:
                acc = acc + r_ref[:, sl]
            if has_rope:
                ys_ref[:, sl] = acc
            else:
                y_ref[:, sl] = acc.astype(y_dtype)
        if has_rope:
            c, s1, s2 = (t[...] for t in tabs)
            for cb in range(N // 128):
                cs = slice(cb * 128, (cb + 1) * 128)
                v = ys_ref[:, cs]
                if rope[0] <= cb * 128 < rope[1]:
                    v = _rope_fwd(v, c, s1, s2)
                y_ref[:, cs] = v.astype(BF16)

    return _call(body, name=name, grid=(M // tm,), in_specs=specs, out_specs=out_specs, out_shape=out_shape,
                 scratch=scratch, sem=("parallel",))(*ins)


def _mm_nt(dy, w3, *, name, mode, tm=512, kchunk=None, h=None, gain=None, dh=None, gu=None, layer=None, after=None):
    M, N = dy.shape
    (S, K, Ns), w_spec = _weight_spec(w3, layer)
    kchunk = kchunk or K
    tm = min(tm, M)
    row = lambda w: pl.BlockSpec((tm, w), lambda i: (i, 0))
    whole = lambda shape: pl.BlockSpec(shape, lambda i: (0,) * len(shape))
    ins, specs = [dy, w3], [row(N), w_spec]
    after = list(after or ())
    ins = after + ins
    specs = [pl.BlockSpec((8, a.shape[1]), lambda i: (0, 0)) for a in after] + specs
    has_dh = dh is not None
    if mode == "norm":
        ins += [h, gain.reshape(1, K)]; specs += [row(K), whole((1, K))]
        if has_dh:
            ins.append(dh); specs.append(row(K))
        out_shape = [jax.ShapeDtypeStruct((M, K), F32), jax.ShapeDtypeStruct((1, K), F32)]
        out_specs = [row(K), whole((1, K))]
    elif mode == "swiglu":
        ins.append(gu); specs.append(row(2 * K))
        out_shape, out_specs = [jax.ShapeDtypeStruct((M, 2 * K), BF16)], [row(2 * K)]
    else:
        out_shape, out_specs = [jax.ShapeDtypeStruct((M, K), BF16)], [row(K)]

    def body(*refs):
        it = iter(refs[len(after):])
        dy_ref, w_ref = next(it), next(it)
        if mode == "norm":
            h_ref, g_ref = next(it), next(it)
            dh_ref = next(it) if has_dh else None
            o_ref, dg_ref = next(it), next(it)
        elif mode == "swiglu":
            gu_ref, o_ref = next(it), next(it)
        else:
            o_ref = next(it)
        for kc in range(K // kchunk):
            ks = slice(kc * kchunk, (kc + 1) * kchunk)
            acc = None
            for s in range(S):
                t = _dot(dy_ref[:, s * Ns:(s + 1) * Ns].astype(BF16), w_ref[s, ks, :], NT)
                acc = t if acc is None else acc + t
            if mode == "plain":
                o_ref[:, ks] = acc.astype(BF16)
            elif mode == "swiglu":
                us = slice(K + kc * kchunk, K + (kc + 1) * kchunk)
                g = gu_ref[:, ks].astype(F32)
                u = gu_ref[:, us].astype(F32)
                sg = _sigmoid(g)
                o_ref[:, ks] = (acc * u * (sg * (1.0 + g * (1.0 - sg)))).astype(BF16)
                o_ref[:, us] = (acc * (g * sg)).astype(BF16)
            else:
                x = h_ref[...].astype(F32)
                r = lax.rsqrt(jnp.mean(x * x, axis=-1, keepdims=True) + NORM_EPS)
                xhat = x * r
                dxh = acc * g_ref[...]
                dx = r * (dxh - xhat * jnp.mean(dxh * xhat, axis=-1, keepdims=True))
                o_ref[...] = dx + dh_ref[...] if has_dh else dx

                @pl.when(pl.program_id(0) == 0)
                def _():
                    dg_ref[...] = jnp.zeros_like(dg_ref)

                dg_ref[...] += jnp.sum(acc * xhat, axis=0, keepdims=True)

    sem = ("arbitrary",) if mode == "norm" else ("parallel",)
    return _call(body, name=name, grid=(M // tm,), in_specs=specs, out_specs=out_specs, out_shape=out_shape, sem=sem)(*ins)


def _mm_tn(x, dy, *, S, name, tk=2048, kk=None, bias=False):
    M, K = x.shape
    N = dy.shape[1]
    Ns = N // S
    kk = kk or K
    tk = min(tk, M)
    nl = M // tk
    in_specs = [pl.BlockSpec((tk, kk), lambda s, kc, l: (l, kc)), pl.BlockSpec((tk, Ns), lambda s, kc, l: (l, s))]
    out_shape = [jax.ShapeDtypeStruct((S, K, Ns), BF16)]
    out_specs = [pl.BlockSpec((None, kk, Ns), lambda s, kc, l: (s, kc, 0))]
    if bias:
        out_shape.append(jax.ShapeDtypeStruct((1, N), F32))
        out_specs.append(pl.BlockSpec((1, Ns), lambda s, kc, l: (0, s)))

    def body(x_ref, dy_ref, o_ref, *rest):
        acc_ref = rest[-1]
        kc, l = pl.program_id(1), pl.program_id(2)

        @pl.when(l == 0)
        def _():
            acc_ref[...] = jnp.zeros_like(acc_ref)

        acc_ref[...] += _dot(x_ref[...].astype(BF16), dy_ref[...].astype(BF16), TN)
        if bias:
            b_ref = rest[0]

            @pl.when((kc == 0) & (l == 0))
            def _():
                b_ref[...] = jnp.zeros_like(b_ref)

            @pl.when(kc == 0)
            def _():
                b_ref[...] += jnp.sum(dy_ref[...].astype(F32), axis=0, keepdims=True)

        @pl.when(l == nl - 1)
        def _():
            o_ref[...] = acc_ref[...].astype(BF16)

    return _call(body, name=name, grid=(S, K // kk, nl), in_specs=in_specs, out_specs=out_specs, out_shape=out_shape,
                 scratch=[pltpu.VMEM((kk, Ns), F32)], sem=("arbitrary", "arbitrary", "arbitrary"))(x, dy)


def _band_bias(max_dist, has_prev):
    rows = lax.broadcasted_iota(jnp.int32, (BLK, 2 * BLK), 0)
    cols = lax.broadcasted_iota(jnp.int32, (BLK, 2 * BLK), 1)
    dist = rows - cols + BLK
    ok = (dist >= 0) & (dist <= max_dist) & ((cols >= BLK) | has_prev)
    return jnp.where(ok, 0.0, NEG)


Q_SCALE = HEAD_DIM ** -0.5


def _band_fwd(qa, ka, va, *, d, nq, nkv, qcol, kcol, vcol, max_dist, sinks=None, name):
    Lr = qa.shape[0]
    nb = Lr // BLK
    qw, kw, G = nq * HEAD_DIM, nkv * HEAD_DIM, nq // nkv
    cur = lambda colf, w: pl.BlockSpec((BLK, w), lambda r, i: (i, colf(r)))
    prv = lambda colf, w: pl.BlockSpec((BLK, w), lambda r, i: (jnp.maximum(i - 1, 0), colf(r)))
    out = pl.BlockSpec((BLK, qw), lambda r, i: (i, r))
    ins, specs = [qa, ka, ka, va, va], [cur(qcol, qw), cur(kcol, kw), prv(kcol, kw), cur(vcol, kw), prv(vcol, kw)]
    has_sinks = sinks is not None
    if has_sinks:
        ins.append(sinks); specs.append(pl.BlockSpec(memory_space=pltpu.SMEM))

    def body(*refs):
        q_ref, kc_ref, kp_ref, vc_ref, vp_ref = refs[:5]
        sk_ref = refs[5] if has_sinks else None
        o_ref, lse_ref = refs[-2], refs[-1]
        bias = _band_bias(max_dist, pl.program_id(1) > 0)
        k2 = jnp.concatenate([kp_ref[...], kc_ref[...]], axis=0)
        v2 = jnp.concatenate([vp_ref[...], vc_ref[...]], axis=0)
        for h in range(nq):
            hs = slice(h * HEAD_DIM, (h + 1) * HEAD_DIM)
            ks = slice((h // G) * HEAD_DIM, (h // G + 1) * HEAD_DIM)
            s = _dot(q_ref[:, hs] * jnp.asarray(Q_SCALE, BF16), k2[:, ks], NT) + bias
            m = jnp.max(s, axis=-1, keepdims=True)
            if has_sinks:
                m = jnp.maximum(m, sk_ref[h])
            p = jnp.exp(s - m)
            l = jnp.sum(p, axis=-1, keepdims=True)
            if has_sinks:
                l = l + jnp.exp(sk_ref[h] - m)
            o_ref[:, hs] = (_dot(p.astype(BF16), v2[:, ks]) / l).astype(BF16)
            lse_ref[:, hs] = jnp.broadcast_to(m + jnp.log(l), (BLK, HEAD_DIM))

    return _call(body, name=name, grid=(d, nb), in_specs=specs, out_specs=[out, out],
                 out_shape=[jax.ShapeDtypeStruct((Lr, d * qw), BF16), jax.ShapeDtypeStruct((Lr, d * qw), F32)],
                 sem=("parallel", "parallel"))(*ins)


def _band_bwd(qa, ka, va, doa, oa, lsea, *, d, nq, nkv, qcol, kcol, vcol, docol, max_dist, sinks=None, name):
    Lr = qa.shape[0]
    nb = Lr // BLK
    qw, kw, G = nq * HEAD_DIM, nkv * HEAD_DIM, nq // nkv
    transposed = G > 1
    last = lambda i: jnp.minimum(i, nb - 1)
    cur = lambda colf, w: pl.BlockSpec((BLK, w), lambda r, i: (last(i), colf(r)))
    prv = lambda colf, w: pl.BlockSpec((BLK, w), lambda r, i: (jnp.maximum(last(i) - 1, 0), colf(r)))
    own = lambda r: r
    ins = [qa, ka, ka, va, va, doa, oa, lsea]
    specs = [cur(qcol, qw), cur(kcol, kw), prv(kcol, kw), cur(vcol, kw), prv(vcol, kw), cur(docol, qw), cur(own, qw),
             cur(own, qw)]
    has_sinks = sinks is not None
    if has_sinks:
        ins.append(sinks); specs.append(pl.BlockSpec(memory_space=pltpu.SMEM))
    out_shape = [jax.ShapeDtypeStruct((Lr, d * qw), BF16), jax.ShapeDtypeStruct((Lr, d * kw), BF16),
                 jax.ShapeDtypeStruct((Lr, d * kw), BF16)]
    behind = lambda r, i: (jnp.maximum(i - 1, 0), r)
    out_specs = [pl.BlockSpec((BLK, qw), lambda r, i: (last(i), r)), pl.BlockSpec((BLK, kw), behind),
                 pl.BlockSpec((BLK, kw), behind)]
    if has_sinks:
        out_shape.append(jax.ShapeDtypeStruct((8, 128), F32))
        out_specs.append(pl.BlockSpec((8, 128), lambda r, i: (0, 0)))

    def body(*refs):
        it = iter(refs)
        q_ref, kc_ref, kp_ref, vc_ref, vp_ref, do_ref, o_ref, ls_ref = (next(it) for _ in range(8))
        sk_ref = next(it) if has_sinks else None
        dq_ref, dk_ref, dv_ref = next(it), next(it), next(it)
        dsk_ref = next(it) if has_sinks else None
        dk_car, dv_car = next(it), next(it)
        r_id, i = pl.program_id(0), pl.program_id(1)

        @pl.when(i == 0)
        def _():
            dk_car[...] = jnp.zeros_like(dk_car)
            dv_car[...] = jnp.zeros_like(dv_car)

        if has_sinks:
            @pl.when((r_id == 0) & (i == 0))
            def _():
                dsk_ref[...] = jnp.zeros_like(dsk_ref)

        @pl.when(i == nb)
        def _():
            dk_ref[...] = dk_car[...].astype(BF16)
            dv_ref[...] = dv_car[...].astype(BF16)

        @pl.when(i < nb)
        def _():
            bias = jnp.tile(_band_bias(max_dist, i > 0), (G, 1))
            k2 = jnp.concatenate([kp_ref[...], kc_ref[...]], axis=0)
            v2 = jnp.concatenate([vp_ref[...], vc_ref[...]], axis=0)
            if has_sinks:
                lane = lax.broadcasted_iota(jnp.int32, (8, 128), 1)
                dsk = jnp.zeros((8, 128), F32)
            for kv in range(nkv):
                heads = [slice((kv * G + g) * HEAD_DIM, (kv * G + g + 1) * HEAD_DIM) for g in range(G)]
                stack = lambda ref: jnp.concatenate([ref[:, hs] for hs in heads], axis=0)
                ks = slice(kv * HEAD_DIM, (kv + 1) * HEAD_DIM)
                kh, vh = k2[:, ks], v2[:, ks]
                q = stack(q_ref) * jnp.asarray(Q_SCALE, BF16)
                do = stack(do_ref)
                lse = jnp.concatenate([ls_ref[:, hs.start:hs.start + 1] for hs in heads], axis=0)
                dl = jnp.sum(do.astype(F32) * stack(o_ref).astype(F32), axis=-1, keepdims=True)
                p = jnp.exp(_dot(q, kh, NT) + bias - lse)
                ds = (p * (_dot(do, vh, NT) - dl)).astype(BF16)
                dq = (_dot(ds, kh) * Q_SCALE).astype(BF16)
                for g, hs in enumerate(heads):
                    dq_ref[:, hs] = dq[g * BLK:(g + 1) * BLK]
                if transposed:
                    dk, dv = _dot(q, ds, TN).T, _dot(do, p.astype(BF16), TN).T
                else:
                    dk, dv = _dot(ds, q, TN), _dot(p.astype(BF16), do, TN)
                if has_sinks:
                    sk = jnp.concatenate([jnp.full((BLK, 1), sk_ref[kv * G + g], F32) for g in range(G)], axis=0)
                    lost = jnp.exp(sk - lse) * dl
                    for g in range(G):
                        val = -jnp.sum(lost[g * BLK:(g + 1) * BLK], axis=0, keepdims=True)
                        dsk = dsk + jnp.where(lane == kv * G + g, val, 0.0)
                dk_ref[:, ks] = (dk_car[:, ks] + dk[:BLK]).astype(BF16)
                dv_ref[:, ks] = (dv_car[:, ks] + dv[:BLK]).astype(BF16)
                dk_car[:, ks] = dk[BLK:]
                dv_car[:, ks] = dv[BLK:]
            if has_sinks:
                dsk_ref[...] += dsk

    return _call(body, name=name, grid=(d, nb + 1), in_specs=specs, out_specs=out_specs, out_shape=out_shape,
                 scratch=[pltpu.VMEM((BLK, kw), F32), pltpu.VMEM((BLK, kw), F32)], sem=("arbitrary", "arbitrary"))(*ins)


def _attn_grad_combine(branches, tabs, *, name, tm=512):
    L, qw = branches[0][0].shape
    kw = branches[0][1].shape[1]
    nbr = len(branches)
    row = lambda w: pl.BlockSpec((tm, w), lambda i: (i, 0))
    ins, specs = [], []
    for dq, dk, dv in branches:
        ins += [dq, dk, dv]; specs += [row(qw), row(kw), row(kw)]
    ins += list(tabs); specs += [row(128)] * 3

    def body(*refs):
        c, s1, s2 = (t[...] for t in refs[3 * nbr:3 * nbr + 3])
        o_ref = refs[-1]
        for part, (w, off, rot) in enumerate(((qw, 0, True), (kw, qw, True), (kw, qw + kw, False))):
            for cb in range(w // 128):
                cs = slice(cb * 128, (cb + 1) * 128)
                v = refs[part][:, cs].astype(F32)
                for b in range(1, nbr):
                    v = v + refs[3 * b + part][:, cs].astype(F32)
                if rot:
                    v = _rope_bwd(v, c, s1, s2)
                o_ref[:, off + cb * 128:off + (cb + 1) * 128] = v.astype(BF16)

    return _call(body, name=name, grid=(L // tm,), in_specs=specs, out_specs=row(qw + 2 * kw),
                 out_shape=jax.ShapeDtypeStruct((L, qw + 2 * kw), BF16), sem=("parallel",))(*ins)


def _xattn_fwd(q, kv, *, name, tq=1024):
    L, W = q.shape
    scale = XA_HEAD_DIM ** -0.5
    row = pl.BlockSpec((tq, W), lambda i: (i, 0))
    kvs = pl.BlockSpec((N_MEM, 2 * W), lambda i: (0, 0))

    def body(q_ref, kv_ref, o_ref, lse_ref):
        for h in range(XA_HEADS):
            hs = slice(h * XA_HEAD_DIM, (h + 1) * XA_HEAD_DIM)
            vs = slice(W + h * XA_HEAD_DIM, W + (h + 1) * XA_HEAD_DIM)
            s = _dot(q_ref[:, hs], kv_ref[:, hs], NT) * scale
            m = jnp.max(s, axis=-1, keepdims=True)
            p = jnp.exp(s - m)
            l = jnp.sum(p, axis=-1, keepdims=True)
            o_ref[:, hs] = (_dot(p.astype(BF16), kv_ref[:, vs]) / l).astype(BF16)
            lse_ref[:, hs] = jnp.broadcast_to(m + jnp.log(l), (tq, XA_HEAD_DIM))

    return _call(body, name=name, grid=(L // tq,), in_specs=[row, kvs], out_specs=[row, row],
                 out_shape=[jax.ShapeDtypeStruct((L, W), BF16), jax.ShapeDtypeStruct((L, W), F32)], sem=("parallel",))(q, kv)


def _xattn_bwd(q, kv, o, lse, do, *, name, tq=1024):
    L, W = q.shape
    scale = XA_HEAD_DIM ** -0.5
    row = pl.BlockSpec((tq, W), lambda i: (i, 0))
    kvs = pl.BlockSpec((N_MEM, 2 * W), lambda i: (0, 0))

    def body(q_ref, kv_ref, o_ref, lse_ref, do_ref, dq_ref, dkv_ref):
        @pl.when(pl.program_id(0) == 0)
        def _():
            dkv_ref[...] = jnp.zeros_like(dkv_ref)

        for h in range(XA_HEADS):
            hs = slice(h * XA_HEAD_DIM, (h + 1) * XA_HEAD_DIM)
            vs = slice(W + h * XA_HEAD_DIM, W + (h + 1) * XA_HEAD_DIM)
            qh, kh, vh, doh = q_ref[:, hs], kv_ref[:, hs], kv_ref[:, vs], do_ref[:, hs]
            p = jnp.exp(_dot(qh, kh, NT) * scale - lse_ref[:, h * XA_HEAD_DIM:h * XA_HEAD_DIM + 1])
            dl = jnp.sum(doh.astype(F32) * o_ref[:, hs].astype(F32), axis=-1, keepdims=True)
            ds = (p * (_dot(doh, vh, NT) - dl) * scale).astype(BF16)
            dq_ref[:, hs] = _dot(ds, kh).astype(BF16)
            dkv_ref[:, hs] += _dot(ds, qh, TN)
            dkv_ref[:, vs] += _dot(p.astype(BF16), doh, TN)

    return _call(body, name=name, grid=(L // tq,), in_specs=[row, kvs, row, row, row], out_specs=[row, kvs],
                 out_shape=[jax.ShapeDtypeStruct((L, W), BF16), jax.ShapeDtypeStruct((N_MEM, 2 * W), F32)],
                 sem=("arbitrary",))(q, kv, o, lse, do)


def _neg_expm1(z):
    series = -(z * (1.0 + z * (0.5 + z * (1.0 / 6.0 + z * (1.0 / 24.0 + z * (1.0 / 120.0))))))
    return jnp.where(z > -0.05, series, 1.0 - jnp.exp(z))


def _softplus(z):
    return jnp.maximum(z, 0.0) + jnp.log(1.0 + jnp.exp(-jnp.abs(z)))


def _gelu_parts(y):
    c = 0.7978845608028654
    t = jnp.tanh(c * (y + 0.044715 * y * y * y))
    gy = 0.5 * y * (1.0 + t)
    dgy = 0.5 * (1.0 + t) + 0.5 * y * (1.0 - t * t) * c * (1.0 + 3.0 * 0.044715 * y * y)
    return gy, dgy


def _lru_gates(xc, wa_ref, ba, wx_ref, bx, sp):
    rs, igs = [], []
    for hd in range(LRU_HEADS):
        sl = slice(hd * LRU_HEAD_DIM, (hd + 1) * LRU_HEAD_DIM)
        xh = xc[:, sl].astype(BF16)
        rs.append(_sigmoid(_dot(xh, wa_ref[hd]) + ba[:, sl]))
        igs.append(_sigmoid(_dot(xh, wx_ref[hd]) + bx[:, sl]))
    r, ig = jnp.concatenate(rs, axis=1), jnp.concatenate(igs, axis=1)
    la = -LRU_C * r * sp
    return r, ig, jnp.exp(la), _neg_expm1(2.0 * la)


def _conv_taps(x_ext, halo):
    n = x_ext.shape[0]
    return [x_ext[halo:] if k == CONV_WIDTH - 1 else pltpu.roll(x_ext, CONV_WIDTH - 1 - k, 0)[halo:]
            for k in range(CONV_WIDTH)]


def _lru_fwd(proj, cw, cb, wa, ba, wx, bx, lam, *, name, tc=512):
    L = proj.shape[0]
    W = LRU_HEADS * LRU_HEAD_DIM
    nb = L // tc
    whole = lambda shape: pl.BlockSpec(shape, lambda i: (0,) * len(shape))
    specs = [pl.BlockSpec((tc, W), lambda i: (i, 0)), pl.BlockSpec((tc, W), lambda i: (i, 1)),
             pl.BlockSpec((16, W), lambda i: (jnp.maximum(i * (tc // 16) - 1, 0), 0)),
             whole((CONV_WIDTH, W)), whole((1, W)), whole((LRU_HEADS, LRU_HEAD_DIM, LRU_HEAD_DIM)), whole((1, W)),
             whole((LRU_HEADS, LRU_HEAD_DIM, LRU_HEAD_DIM)), whole((1, W)), whole((1, W))]
    out_specs = [pl.BlockSpec((tc, W), lambda i: (i, 0))] * 2
    out_shape = [jax.ShapeDtypeStruct((L, W), BF16), jax.ShapeDtypeStruct((L, W), F32)]

    def body(x_ref, y_ref, xh_ref, cw_ref, cb_ref, wa_ref, ba_ref, wx_ref, bx_ref, lam_ref, rec_ref, hs_ref,
             hcar, a_scr, b_scr):
        i = pl.program_id(0)

        @pl.when(i == 0)
        def _():
            hcar[...] = jnp.zeros_like(hcar)

        halo = jnp.where(i > 0, xh_ref[...].astype(F32), 0.0)
        taps = _conv_taps(jnp.concatenate([halo, x_ref[...].astype(F32)], axis=0), 16)
        xc = cb_ref[...] + sum(cw_ref[k:k + 1, :] * taps[k] for k in range(CONV_WIDTH))
        _, ig, a, om = _lru_gates(xc, wa_ref, ba_ref[...], wx_ref, bx_ref[...], _softplus(-lam_ref[...]))
        b = jnp.sqrt(om) * (ig * xc)
        rowmod = lax.broadcasted_iota(jnp.int32, (tc, W), 0) & 7
        for s in (1, 2, 4):
            keep = rowmod >= s
            b = jnp.where(keep, a * pltpu.roll(b, s, 0) + b, b)
            a = jnp.where(keep, a * pltpu.roll(a, s, 0), a)
        a_scr[...] = a
        b_scr[...] = b

        def tile(j, hc):
            rows = pl.ds(pl.multiple_of(j * 8, 8), 8)
            ht = a_scr[rows, :] * hc + b_scr[rows, :]
            hs_ref[rows, :] = ht
            return jnp.broadcast_to(ht[7:8, :], (8, W))

        hcar[...] = lax.fori_loop(0, tc // 8, tile, hcar[...])
        gy, _ = _gelu_parts(y_ref[...].astype(F32))
        rec_ref[...] = (hs_ref[...] * gy).astype(BF16)

    return _call(body, name=name, grid=(nb,), in_specs=specs, out_specs=out_specs, out_shape=out_shape,
                 scratch=[pltpu.VMEM((8, W), F32), pltpu.VMEM((tc, W), F32), pltpu.VMEM((tc, W), F32)],
                 sem=("arbitrary",))(proj, proj, proj, cw, cb, wa, ba, wx, bx, lam)


def _lru_bwd(proj, hs, drec_src, cw, cb, wa, ba, wx, bx, lam, *, name, tc=256):
    L = proj.shape[0]
    W = LRU_HEADS * LRU_HEAD_DIM
    nb = L // tc
    tb = lambda i: nb - 1 - i
    whole = lambda shape: pl.BlockSpec(shape, lambda i: (0,) * len(shape))
    gate_w = (LRU_HEADS, LRU_HEAD_DIM, LRU_HEAD_DIM)
    specs = [pl.BlockSpec((tc, W), lambda i: (tb(i), 0)), pl.BlockSpec((tc, W), lambda i: (tb(i), 1)),
             pl.BlockSpec((16, W), lambda i: (jnp.maximum(tb(i) * (tc // 16) - 1, 0), 0)),
             pl.BlockSpec((tc, W), lambda i: (tb(i), 0)),
             pl.BlockSpec((8, W), lambda i: (jnp.maximum(tb(i) * (tc // 8) - 1, 0), 0)),
             pl.BlockSpec((tc, W), lambda i: (tb(i), 0)),
             whole((CONV_WIDTH, W)), whole((1, W)), whole(gate_w), whole((1, W)), whole(gate_w), whole((1, W)), whole((1, W))]
    out_specs = [pl.BlockSpec((tc, 2 * W), lambda i: (tb(i), 0)), whole((CONV_WIDTH, W)), whole((1, W)), whole(gate_w),
                 whole((1, W)), whole(gate_w), whole((1, W)), whole((1, W))]
    vec = jax.ShapeDtypeStruct((1, W), F32)
    out_shape = [jax.ShapeDtypeStruct((L, 2 * W), BF16), jax.ShapeDtypeStruct((CONV_WIDTH, W), F32), vec,
                 jax.ShapeDtypeStruct(gate_w, F32), vec, jax.ShapeDtypeStruct(gate_w, F32), vec, vec]

    def body(x_ref, y_ref, xh_ref, hs_ref, hh_ref, dr_ref, cw_ref, cb_ref, wa_ref, ba_ref, wx_ref, bx_ref, lam_ref,
             dxy_ref, dcw_ref, dcb_ref, dwa_ref, dba_ref, dwx_ref, dbx_ref, dlam_ref, gcar, dxc_car, a_scr, b_scr, g_scr):
        pid = pl.program_id(0)
        t = tb(pid)
        accs = (dcw_ref, dcb_ref, dwa_ref, dba_ref, dwx_ref, dbx_ref, dlam_ref)

        @pl.when(pid == 0)
        def _():
            gcar[...] = jnp.zeros_like(gcar)
            dxc_car[...] = jnp.zeros_like(dxc_car)
            for r in accs:
                r[...] = jnp.zeros_like(r)

        halo = jnp.where(t > 0, xh_ref[...].astype(F32), 0.0)
        taps = _conv_taps(jnp.concatenate([halo, x_ref[...].astype(F32)], axis=0), 16)
        xc = cb_ref[...] + sum(cw_ref[k:k + 1, :] * taps[k] for k in range(CONV_WIDTH))
        lam = lam_ref[...]
        sp = _softplus(-lam)
        r, ig, a, om = _lru_gates(xc, wa_ref, ba_ref[...], wx_ref, bx_ref[...], sp)
        sq = jnp.sqrt(om)
        hblk = hs_ref[...]
        hprev = pltpu.roll(jnp.concatenate([jnp.where(t > 0, hh_ref[...], 0.0), hblk], axis=0), 1, 0)[8:]
        gy, dgy = _gelu_parts(y_ref[...].astype(F32))
        drec = dr_ref[...].astype(F32)
        dxy_ref[:, W:] = (drec * hblk * dgy).astype(BF16)

        rowidx = lax.broadcasted_iota(jnp.int32, (tc, W), 0)
        rowmod = rowidx & 7
        ca = jnp.where(rowidx == tc - 1, 1.0, pltpu.roll(a, tc - 1, 0))
        cbv = drec * gy
        for s in (1, 2, 4):
            keep = rowmod < 8 - s
            cbv = jnp.where(keep, ca * pltpu.roll(cbv, tc - s, 0) + cbv, cbv)
            ca = jnp.where(keep, ca * pltpu.roll(ca, tc - s, 0), ca)
        a_scr[...] = ca
        b_scr[...] = cbv

        def tile(k, gc):
            j = tc // 8 - 1 - k
            rows = pl.ds(pl.multiple_of(j * 8, 8), 8)
            gt = a_scr[rows, :] * gc + b_scr[rows, :]
            g_scr[rows, :] = gt
            return jnp.broadcast_to(gt[0:1, :], (8, W))

        lax.fori_loop(0, tc // 8, tile, gcar[...])
        G = g_scr[...]
        gcar[...] = jnp.broadcast_to(a[0:1, :] * G[0:1, :], (8, W))

        da = G * hprev
        dsq = G * (ig * xc)
        di = G * (sq * xc)
        dxc = G * (sq * ig)
        dla = da * a - 2.0 * a * a * (dsq * 0.5 * lax.rsqrt(om))
        dlam_ref[...] += jnp.sum(dla * (-LRU_C * r), axis=0, keepdims=True) * (-_sigmoid(-lam))
        dpr = dla * (-LRU_C * sp) * r * (1.0 - r)
        dpi = di * ig * (1.0 - ig)
        dba_ref[...] += jnp.sum(dpr, axis=0, keepdims=True)
        dbx_ref[...] += jnp.sum(dpi, axis=0, keepdims=True)
        back = []
        for hd in range(LRU_HEADS):
            sl = slice(hd * LRU_HEAD_DIM, (hd + 1) * LRU_HEAD_DIM)
            xh, dprh, dpih = xc[:, sl].astype(BF16), dpr[:, sl].astype(BF16), dpi[:, sl].astype(BF16)
            back.append(_dot(dprh, wa_ref[hd], NT) + _dot(dpih, wx_ref[hd], NT))
            dwa_ref[hd] += _dot(xh, dprh, TN)
            dwx_ref[hd] += _dot(xh, dpih, TN)
        dxc = dxc + jnp.concatenate(back, axis=1)
        dcb_ref[...] += jnp.sum(dxc, axis=0, keepdims=True)
        for k in range(CONV_WIDTH):
            dcw_ref[k:k + 1, :] += jnp.sum(dxc * taps[k], axis=0, keepdims=True)
        ext = jnp.concatenate([dxc, dxc_car[...]], axis=0)
        dx = cw_ref[CONV_WIDTH - 1:CONV_WIDTH, :] * dxc
        for k in range(CONV_WIDTH - 1):
            dx = dx + cw_ref[k:k + 1, :] * pltpu.roll(ext, tc + 8 - (CONV_WIDTH - 1 - k), 0)[:tc]
        dxc_car[...] = dxc[0:8, :]
        dxy_ref[:, :W] = dx.astype(BF16)

    scratch = [pltpu.VMEM((8, W), F32), pltpu.VMEM((8, W), F32)] + [pltpu.VMEM((tc, W), F32)] * 3
    return _call(body, name=name, grid=(nb,), in_specs=specs, out_specs=out_specs, out_shape=out_shape, scratch=scratch,
                 sem=("arbitrary",))(proj, proj, proj, hs, hs, drec_src, cw, cb, wa, ba, wx, bx, lam)


def _final_loss(h, gain, target, *, name, tm=512):
    M, K = h.shape
    row = pl.BlockSpec((tm, K), lambda i: (i, 0))
    vec = pl.BlockSpec((1, K), lambda i: (0, 0))
    one = pl.BlockSpec((1, 128), lambda i: (0, 0))

    def body(h_ref, g_ref, t_ref, dh_ref, dg_ref, loss_ref):
        @pl.when(pl.program_id(0) == 0)
        def _():
            dg_ref[...] = jnp.zeros_like(dg_ref)
            loss_ref[...] = jnp.zeros_like(loss_ref)

        x = h_ref[...]
        r = lax.rsqrt(jnp.mean(x * x, axis=-1, keepdims=True) + NORM_EPS)
        xhat = x * r
        err = xhat * g_ref[...] - t_ref[...]
        loss_ref[...] += 0.5 / K * jnp.sum(err * err)
        dy = err * (1.0 / K)
        dg_ref[...] += jnp.sum(dy * xhat, axis=0, keepdims=True)
        dxh = dy * g_ref[...]
        dh_ref[...] = r * (dxh - xhat * jnp.mean(dxh * xhat, axis=-1, keepdims=True))

    return _call(body, name=name, grid=(M // tm,), in_specs=[row, vec, row], out_specs=[row, vec, one],
                 out_shape=[jax.ShapeDtypeStruct((M, K), F32), jax.ShapeDtypeStruct((1, K), F32),
                            jax.ShapeDtypeStruct((1, 128), F32)], sem=("arbitrary",))(h, gain.reshape(1, K), target)


def _dilated_merge(branches, *, name, tm=512):
    L, W = branches[0].shape
    nbr = len(branches) // 2
    row = pl.BlockSpec((tm, W), lambda i: (i, 0))

    def body(*refs):
        o_ref, lse_ref = refs[-2], refs[-1]
        lses = [refs[2 * b + 1][...] for b in range(nbr)]
        m = lses[0]
        for t in lses[1:]:
            m = jnp.maximum(m, t)
        ws = [jnp.exp(t - m) for t in lses]
        den = ws[0]
        for t in ws[1:]:
            den = den + t
        acc = ws[0] * refs[0][...].astype(F32)
        for b in range(1, nbr):
            acc = acc + ws[b] * refs[2 * b][...].astype(F32)
        o_ref[...] = (acc / den).astype(BF16)
        lse_ref[...] = m + jnp.log(den)

    return _call(body, name=name, grid=(L // tm,), in_specs=[row] * (2 * nbr), out_specs=[row, row],
                 out_shape=[jax.ShapeDtypeStruct((L, W), BF16), jax.ShapeDtypeStruct((L, W), F32)], sem=("parallel",))(*branches)


def _dilated_fwd(proj0):
    L = proj0.shape[0]
    qkv = proj0[:, 2 * D_MODEL:]
    W = B_HEADS * HEAD_DIM
    outs = []
    for window, d in DILATED_PATTERN:
        view = qkv.reshape(L // d, d * 3 * W)
        o, lse = _band_fwd(view, view, view, d=d, nq=B_HEADS, nkv=B_HEADS, qcol=lambda r: 3 * r, kcol=lambda r: 3 * r + 1,
                           vcol=lambda r: 3 * r + 2, max_dist=window // d, name=f"dilated_fwd_d{d}")
        outs += [o.reshape(L, W), lse.reshape(L, W)]
    return _dilated_merge(outs, name="dilated_merge")


def _dilated_bwd(proj0, att, lse, datt, tabs):
    L = proj0.shape[0]
    qkv = proj0[:, 2 * D_MODEL:]
    Wh = B_HEADS * HEAD_DIM
    branches = []
    for window, d in DILATED_PATTERN:
        view = qkv.reshape(L // d, d * 3 * Wh)
        v1 = lambda t: t.reshape(L // d, d * Wh)
        outs = _band_bwd(view, view, view, v1(datt), v1(att), v1(lse), d=d, nq=B_HEADS, nkv=B_HEADS,
                         qcol=lambda r: 3 * r, kcol=lambda r: 3 * r + 1, vcol=lambda r: 3 * r + 2, docol=lambda r: r,
                         max_dist=window // d, name=f"dilated_bwd_d{d}")
        branches.append([o.reshape(L, Wh) for o in outs])
    return _attn_grad_combine(branches, tabs, name="dilated_grad_combine")


def _device_step(x, mem, target, w, on_grads=None):
    L = x.shape[0]
    tabs = _rope_tables(L)
    g = {}
    saved = []
    h = x
    for layer in range(2):
        sv = {"h_mix": h}
        if layer == 0:
            proj, n = _rowmm(h, w["ab_w_in"], name="l0_in_proj", gain=w["mix_norm"][0],
                             rope=(2 * D_MODEL, 2 * D_MODEL + 2 * B_HEADS * HEAD_DIM, tabs))
            rec, hs = _lru_fwd(proj, w["lru_conv_w"], w["lru_conv_b"], w["lru_wa"], w["lru_ba"], w["lru_wx"], w["lru_bx"],
                               w["lru_lambda"], name="lru_fwd")
            att, lse = _dilated_fwd(proj)
            mix = jnp.concatenate([rec, att], axis=1)
            (h,) = _rowmm(mix, w["ab_w_out"], name="l0_out_proj", res=h)
            sv.update(hs=hs)
        else:
            proj, n = _rowmm(h, w["c_w_qkv"], name="l1_qkv_proj", gain=w["mix_norm"][1], bias=w["c_b_qkv"],
                             rope=(0, (C_HEADS + C_KV_HEADS) * HEAD_DIM, tabs))
            mix, lse = _band_fwd(proj, proj, proj, d=1, nq=C_HEADS, nkv=C_KV_HEADS, qcol=lambda r: 0, kcol=lambda r: 8,
                                 vcol=lambda r: 9, max_dist=C_WINDOW - 1, sinks=w["c_sinks"], name="swa_fwd")
            (h,) = _rowmm(mix, w["c_w_out"], name="l1_out_proj", res=h, bias=w["c_b_out"])
        sv.update(proj=proj, n_mix=n, mix=mix, lse=lse, h_xa=h)
        xq, nx = _rowmm(h, w["xa_wq"][layer][None], name=f"xa_q_proj{layer}", gain=w["xa_norm"][layer])
        kv, nm = _rowmm(mem, w["xa_wkv"][layer][None], name=f"xa_kv_proj{layer}", gain=w["xa_mem_norm"][layer])
        xo, xlse = _xattn_fwd(xq, kv, name=f"xa_fwd{layer}")
        (h,) = _rowmm(xo, w["xa_wo"][layer], name=f"xa_out_proj{layer}", res=h)
        sv.update(xq=xq, nx=nx, kv=kv, nm=nm, xo=xo, xlse=xlse, h_ffn=h)
        gu, nf, act = _rowmm(h, w["ffn_w_gate_up"], layer=layer, name=f"ffn_in{layer}", gain=w["ffn_norm"][layer], swiglu=True)
        (h,) = _rowmm(act, w["ffn_w_down"][layer][None], name=f"ffn_out{layer}", res=h, tm=512)
        sv.update(gu=gu, nf=nf, act=act)
        saved.append(sv)

    dh, g["final_norm"], loss = _final_loss(h, w["final_norm"], target, name="final_loss")

    stk = {k: [None, None] for k in ("xa_norm", "xa_mem_norm", "ffn_norm", "mix_norm")}
    after = None
    for layer in (1, 0):
        sv = saved[layer]
        (g["ffn_w_down", layer],) = _mm_tn(sv["act"], dh, S=1, name=f"ffn_down_dw{layer}", kk=D_FF // 2)
        (dgu,) = _mm_nt(dh, w["ffn_w_down"][layer][None], name=f"ffn_dact{layer}", mode="swiglu", kchunk=D_FF // 2, gu=sv["gu"],
                        after=after)
        (g["ffn_w_gate_up", layer],) = _mm_tn(sv["nf"], dgu, S=N_CHIPS, name=f"ffn_gu_dw{layer}")
        dh, stk["ffn_norm"][layer] = _mm_nt(dgu, w["ffn_w_gate_up"], layer=layer, name=f"ffn_dx{layer}", mode="norm",
                                            h=sv["h_ffn"], gain=w["ffn_norm"][layer], dh=dh)
        (g["xa_wo", layer],) = _mm_tn(sv["xo"], dh, S=N_CHIPS, name=f"xa_wo_dw{layer}")
        (dxo,) = _mm_nt(dh, w["xa_wo"][layer], name=f"xa_dxo{layer}", mode="plain")
        dxq, dkv = _xattn_bwd(sv["xq"], sv["kv"], sv["xo"], sv["xlse"], dxo, name=f"xa_bwd{layer}")
        (g["xa_wq", layer],) = _mm_tn(sv["nx"], dxq, S=1, name=f"xa_wq_dw{layer}")
        dh, stk["xa_norm"][layer] = _mm_nt(dxq, w["xa_wq"][layer][None], name=f"xa_dx{layer}", mode="norm", h=sv["h_xa"],
                                           gain=w["xa_norm"][layer], dh=dh)
        (g["xa_wkv", layer],) = _mm_tn(sv["nm"], dkv, S=1, name=f"xa_wkv_dw{layer}")
        _, stk["xa_mem_norm"][layer] = _mm_nt(dkv, w["xa_wkv"][layer][None], name=f"xa_dmem{layer}", mode="norm", h=mem,
                                              gain=w["xa_mem_norm"][layer])
        if layer == 1:
            g["c_w_out"], g["c_b_out"] = _mm_tn(sv["mix"], dh, S=1, name="l1_out_dw", bias=True)
            (dmix,) = _mm_nt(dh, w["c_w_out"], name="l1_dmix", mode="plain")
            dq, dk, dv, dsk = _band_bwd(sv["proj"], sv["proj"], sv["proj"], dmix, sv["mix"], sv["lse"], d=1, nq=C_HEADS,
                                        nkv=C_KV_HEADS, qcol=lambda r: 0, kcol=lambda r: 8, vcol=lambda r: 9,
                                        docol=lambda r: 0, max_dist=C_WINDOW - 1, sinks=w["c_sinks"], name="swa_bwd")
            g["c_sinks"] = dsk[0, :C_HEADS]
            dproj = _attn_grad_combine([(dq, dk, dv)], tabs, name="swa_grad_combine")
            g["c_w_qkv"], g["c_b_qkv"] = _mm_tn(sv["n_mix"], dproj, S=1, name="l1_qkv_dw", bias=True)
            dh, stk["mix_norm"][1] = _mm_nt(dproj, w["c_w_qkv"], name="l1_dx", mode="norm", h=sv["h_mix"],
                                            gain=w["mix_norm"][1], dh=dh)
            if on_grads is not None:
                after = on_grads("layer1", g)
        else:
            if on_grads is not None:
                after = on_grads("layer0_ffn_xa", g)
            (g["ab_w_out"],) = _mm_tn(sv["mix"], dh, S=1, name="l0_out_dw", kk=768)
            (dmix,) = _mm_nt(dh, w["ab_w_out"], name="l0_dmix", mode="plain", kchunk=768, after=after)
            (dxy, g["lru_conv_w"], g["lru_conv_b"], g["lru_wa"], g["lru_ba"], g["lru_wx"], g["lru_bx"],
             g["lru_lambda"]) = _lru_bwd(sv["proj"], sv["hs"], dmix, w["lru_conv_w"], w["lru_conv_b"], w["lru_wa"],
                                         w["lru_ba"], w["lru_wx"], w["lru_bx"], w["lru_lambda"], name="lru_bwd")
            dqkv = _dilated_bwd(sv["proj"], sv["mix"][:, D_MODEL:], sv["lse"], dmix[:, D_MODEL:], tabs)
            dproj = jnp.concatenate([dxy, dqkv], axis=1)
            (g["ab_w_in"],) = _mm_tn(sv["n_mix"], dproj, S=N_CHIPS, name="l0_in_dw")
            dh, stk["mix_norm"][0] = _mm_nt(dproj, w["ab_w_in"], name="l0_dx", mode="norm", h=sv["h_mix"],
                                            gain=w["mix_norm"][0], dh=dh)
    for k, v in stk.items():
        g[k] = jnp.concatenate(v, axis=0)
    return loss[0, 0], dh, g


ANY = pl.BlockSpec(memory_space=pl.ANY)
MESH = pl.DeviceIdType.MESH


def _place():
    x, y, c = lax.axis_index("x"), lax.axis_index("y"), lax.axis_index("c")
    return x, y, c, [(1 - x, y), (x, 1 - y), (1 - x, 1 - y)]


def _remote(send_sems, recv_sems):
    def copy(k, src, dst, to):
        return pltpu.make_async_remote_copy(src_ref=src, dst_ref=dst, send_sem=send_sems.at[k], recv_sem=recv_sems.at[k],
                                            device_id=to, device_id_type=MESH)
    return copy


def _halves(ref, n_rows):
    rh = n_rows // 2
    return lambda lead, hh: ref.at[(*lead, pl.ds(hh * rh, rh), slice(None))]


def _gather_weights(packs, spack):
    n = len(packs)

    def body(*refs):
        w_refs, s_ref, wf_refs, sf_ref = refs[:n], refs[n], refs[n + 1:2 * n + 1], refs[2 * n + 1]
        x, y, c, chips = _place()
        me, sib = 2 * x + y, (x, y, 1 - c)
        copy = _remote(*refs[-2:])
        src = [_halves(w_refs[g], packs[g].shape[0]) for g in range(n)]
        dst = [_halves(wf_refs[g], packs[g].shape[0]) for g in range(n)]
        sends = []
        for g in range(n):
            for j, (cx, cy) in enumerate(chips):
                sends.append(copy(3 * g + j, src[g]((), c), dst[g]((me,), c), (cx, cy, c)))
        for j, (cx, cy) in enumerate(chips):
            sends.append(copy(6 * n + j, s_ref, sf_ref.at[me], (cx, cy, c)))
        for cp in sends:
            cp.start()
        for g in range(n):
            for j, (cx, cy) in enumerate(chips):
                got = dst[g]((2 * cx + cy,), c)
                copy(3 * g + j, got, got, sib).wait_recv()
                fwd = copy(3 * n + 3 * g + j, got, got, sib)
                fwd.start()
                sends.append(fwd)
        for g in range(n):
            for j, (cx, cy) in enumerate(chips):
                got = dst[g]((2 * cx + cy,), 1 - c)
                copy(3 * n + 3 * g + j, got, got, sib).wait_recv()
        for j, (cx, cy) in enumerate(chips):
            copy(6 * n + j, s_ref, sf_ref.at[2 * cx + cy], sib).wait_recv()
        for cp in sends:
            cp.wait_send()

    ins = list(packs) + [spack]
    out_shape = [jax.ShapeDtypeStruct((N_CHIPS,) + a.shape, a.dtype) for a in ins]
    n_sems = 6 * n + 3
    outs = pl.pallas_call(body, name="gather_weights", out_shape=out_shape, in_specs=[ANY] * len(ins),
                          out_specs=[ANY] * len(ins),
                          scratch_shapes=[pltpu.SemaphoreType.DMA((n_sems,)), pltpu.SemaphoreType.DMA((n_sems,))])(*ins)
    chip = 2 * lax.axis_index("x") + lax.axis_index("y")
    outs = [lax.dynamic_update_index_in_dim(o, a, chip, 0) for o, a in zip(outs, ins)]
    return outs[:n], outs[n]


SEQUENCER_GATHER_IDS = {"mid": 1, "late": 5}


def _gather_weights_behind(packs, *, tag):
    n = len(packs)

    def body(*refs):
        w_refs, wf_refs = refs[:n], refs[n:2 * n]
        x, y, c, chips = _place()
        me, sib = 2 * x + y, (x, y, 1 - c)
        barrier = pltpu.get_barrier_semaphore()
        for peer in [(cx, cy, c) for cx, cy in chips] + [sib]:
            pl.semaphore_signal(barrier, inc=1, device_id=peer, device_id_type=MESH)
        pl.semaphore_wait(barrier, len(chips) + 1)
        copy = _remote(*refs[-2:])
        src = [_halves(w_refs[g], packs[g].shape[0]) for g in range(n)]
        dst = [_halves(wf_refs[g], packs[g].shape[0]) for g in range(n)]
        sends = []
        for g in range(n):
            for j, (cx, cy) in enumerate(chips):
                sends.append(copy(3 * g + j, src[g]((), c), dst[g]((me,), c), (cx, cy, c)))
        for cp in sends:
            cp.start()
        for g in range(n):
            for j, (cx, cy) in enumerate(chips):
                got = dst[g]((2 * cx + cy,), c)
                copy(3 * g + j, got, got, sib).wait_recv()
                fwd = copy(3 * n + 3 * g + j, got, got, sib)
                fwd.start()
                sends.append(fwd)
        for g in range(n):
            for j, (cx, cy) in enumerate(chips):
                got = dst[g]((2 * cx + cy,), 1 - c)
                copy(3 * n + 3 * g + j, got, got, sib).wait_recv()
        for cp in sends:
            cp.wait_send()

    out_type = [jax.ShapeDtypeStruct((N_CHIPS,) + a.shape, a.dtype) for a in packs]
    outs = pl.kernel(body, out_type=out_type, mesh=plsc.ScalarSubcoreMesh(axis_name="sequencer", num_cores=1),
                     name="gather_weights_behind_" + tag,
                     scratch_types=[pltpu.SemaphoreType.DMA((6 * n,)), pltpu.SemaphoreType.DMA((6 * n,))],
                     compiler_params=pltpu.CompilerParams(collective_id=SEQUENCER_GATHER_IDS[tag]))(*packs)
    chip = 2 * lax.axis_index("x") + lax.axis_index("y")
    return [lax.dynamic_update_index_in_dim(o, a, chip, 0) for o, a in zip(outs, packs)]


def _rs_pair_exchange(gpacks, *, name):
    n = len(gpacks)

    def body(*refs):
        g_refs, ra_refs = refs[:n], refs[n:2 * n]
        x, y, c, _ = _place()
        copy = _remote(*refs[-2:])
        cps = []
        for g in range(n):
            half = _halves(g_refs[g], gpacks[g].shape[1])
            cps += [copy(N_CHIPS * g + j, half((j,), 1 - c), ra_refs[g].at[j], (x, y, 1 - c)) for j in range(N_CHIPS)]
        for cp in cps:
            cp.start()
        for cp in cps:
            cp.wait()

    out_shape = [jax.ShapeDtypeStruct((N_CHIPS, a.shape[1] // 2, a.shape[2]), a.dtype) for a in gpacks]
    n_sems = N_CHIPS * n
    return pl.pallas_call(body, name=name, out_shape=out_shape, in_specs=[ANY] * n, out_specs=[ANY] * n,
                          scratch_shapes=[pltpu.SemaphoreType.DMA((n_sems,)), pltpu.SemaphoreType.DMA((n_sems,))])(*gpacks)


def _row_tile(rows, cap=512):
    return max(t for t in range(16, min(rows, cap) + 1, 16) if rows % t == 0)


def _rs_pair_add(place, gpack, ra, *, name):
    _, R, C = gpack.shape
    Rh = R // 2
    tr = _row_tile(Rh)
    nrb = Rh // tr

    def body(p_ref, g_ref, ra_ref, pair_ref, own_ref):
        s = g_ref[...].astype(F32) + ra_ref[...].astype(F32)
        pair_ref[...] = s.astype(BF16)

        @pl.when(pl.program_id(1) == p_ref[1])
        def _():
            own_ref[...] = s

    grid_spec = pltpu.PrefetchScalarGridSpec(
        num_scalar_prefetch=1, grid=(nrb, N_CHIPS),
        in_specs=[pl.BlockSpec((None, tr, C), lambda i, j, p: (j, p[0] * nrb + i, 0)),
                  pl.BlockSpec((None, tr, C), lambda i, j, p: (j, i, 0))],
        out_specs=[pl.BlockSpec((None, tr, C), lambda i, j, p: (j, i, 0)), pl.BlockSpec((tr, C), lambda i, j, p: (i, 0))])
    return pl.pallas_call(
        body, name=name, grid_spec=grid_spec,
        out_shape=[jax.ShapeDtypeStruct((N_CHIPS, Rh, C), BF16), jax.ShapeDtypeStruct((Rh, C), F32)],
        compiler_params=pltpu.CompilerParams(dimension_semantics=("arbitrary", "arbitrary"),
                                             vmem_limit_bytes=VMEM_LIMIT_V7X))(place, gpack, ra)


SEQUENCER_EXCHANGE_IDS = {"l1": 2, "l0a": 3, "l0b": 4}


def _rs_chip_exchange_behind(pairs, *, tag, small=None):
    n = len(pairs)
    has_small = small is not None

    def body(*refs):
        p_refs = refs[:n]
        s_ref = refs[n] if has_small else None
        rb_refs = refs[n + has_small:2 * n + has_small]
        rs_ref = refs[2 * n + 1] if has_small else None
        x, y, c, chips = _place()
        peers = [(1 - x if k & 4 else x, 1 - y if k & 2 else y, 1 - c if k & 1 else c) for k in range(1, 8)]
        shake = peers if has_small else [(cx, cy, c) for cx, cy in chips]
        barrier = pltpu.get_barrier_semaphore()
        for peer in shake:
            pl.semaphore_signal(barrier, inc=1, device_id=peer, device_id_type=MESH)
        pl.semaphore_wait(barrier, len(shake))
        copy = _remote(*refs[-2:])
        cps = []
        for g in range(n):
            cps += [copy(3 * g + j, p_refs[g].at[2 * cx + cy], rb_refs[g].at[j], (cx, cy, c)) for j, (cx, cy) in enumerate(chips)]
        if has_small:
            dev = 4 * x + 2 * y + c
            cps += [copy(3 * n + k, s_ref, rs_ref.at[dev], peer) for k, peer in enumerate(peers)]
        for cp in cps:
            cp.start()
        for g in range(n):
            for j in range(3):
                copy(3 * g + j, p_refs[g].at[0], rb_refs[g].at[j], (x, y, c)).wait_recv()
        if has_small:
            for k, (px, py, pc) in enumerate(peers):
                copy(3 * n + k, s_ref, rs_ref.at[4 * px + 2 * py + pc], (x, y, c)).wait_recv()
        for cp in cps:
            cp.wait_send()

    ins = list(pairs) + ([small] if has_small else [])
    out_type = [jax.ShapeDtypeStruct((3,) + p.shape[1:], p.dtype) for p in pairs]
    if has_small:
        out_type.append(jax.ShapeDtypeStruct((8,) + small.shape, small.dtype))
    n_sems = 3 * n + 7 * has_small
    outs = pl.kernel(body, out_type=out_type, mesh=plsc.ScalarSubcoreMesh(axis_name="sequencer", num_cores=1),
                     name="rs_chip_exchange_behind_" + tag,
                     scratch_types=[pltpu.SemaphoreType.DMA((n_sems,)), pltpu.SemaphoreType.DMA((n_sems,))],
                     compiler_params=pltpu.CompilerParams(collective_id=SEQUENCER_EXCHANGE_IDS[tag]))(*ins)
    if has_small:
        dev = 4 * lax.axis_index("x") + 2 * lax.axis_index("y") + lax.axis_index("c")
        outs = list(outs[:n]) + [lax.dynamic_update_index_in_dim(outs[n], small, dev, 0)]
    return outs


def _rs_final_add(place, own, rb, *, name):
    Rh, C = own.shape
    tr = _row_tile(Rh)
    nrb = Rh // tr

    def body(p_ref, o_ref, rb_ref, f_ref):
        f_ref[...] = ((o_ref[...] + rb_ref[0].astype(F32)) + rb_ref[1].astype(F32)) + rb_ref[2].astype(F32)

    grid_spec = pltpu.PrefetchScalarGridSpec(
        num_scalar_prefetch=1, grid=(nrb,),
        in_specs=[pl.BlockSpec((tr, C), lambda i, p: (i, 0)), pl.BlockSpec((3, tr, C), lambda i, p: (0, i, 0))],
        out_specs=pl.BlockSpec((tr, C), lambda i, p: (p[0] * nrb + i, 0)))
    return pl.pallas_call(
        body, name=name, grid_spec=grid_spec, out_shape=jax.ShapeDtypeStruct((2 * Rh, C), F32),
        compiler_params=pltpu.CompilerParams(dimension_semantics=("arbitrary",), vmem_limit_bytes=VMEM_LIMIT_V7X))(place, own, rb)


def _sum_slots(rs):
    n, rows, C = rs.shape

    def body(r_ref, o_ref):
        acc = r_ref[0]
        for k in range(1, n):
            acc = acc + r_ref[k]
        o_ref[...] = acc

    return _call(body, name="small_grad_sum", grid=(1,), in_specs=[pl.BlockSpec((n, rows, C), lambda i: (0, 0, 0))],
                 out_specs=pl.BlockSpec((rows, C), lambda i: (0, 0)), out_shape=jax.ShapeDtypeStruct((rows, C), F32),
                 sem=("arbitrary",))(rs)


def _rs_sibling_share(gbufs, *, name):
    n = len(gbufs)

    def body(*refs):
        g_refs = refs[n:2 * n]
        x, y, c, _ = _place()
        copy = _remote(*refs[-2:])
        halves = [_halves(g_refs[g], gbufs[g].shape[0]) for g in range(n)]
        outs = [copy(g, halves[g]((), c), halves[g]((), c), (x, y, 1 - c)) for g in range(n)]
        for cp in outs:
            cp.start()
        for g in range(n):
            copy(g, halves[g]((), 1 - c), halves[g]((), 1 - c), (x, y, c)).wait_recv()
        for cp in outs:
            cp.wait_send()

    return pl.pallas_call(body, name=name, out_shape=[jax.ShapeDtypeStruct(a.shape, a.dtype) for a in gbufs],
                          in_specs=[ANY] * n, out_specs=[ANY] * n, input_output_aliases={g: g for g in range(n)},
                          scratch_shapes=[pltpu.SemaphoreType.DMA((n,)), pltpu.SemaphoreType.DMA((n,))])(*gbufs)


def _adamw(w, g, m, v, *, name, g_row=0):
    rows, cols = w.shape
    tr = rows
    for cand in range(min(rows, 512), 7, -8):
        if rows % cand == 0 and g_row % cand == 0:
            tr = cand
            break
    spec = pl.BlockSpec((tr, cols), lambda i: (i, 0))
    g_spec = pl.BlockSpec((tr, cols), lambda i: (g_row // tr + i, 0))

    def body(w_ref, g_ref, m_ref, v_ref, d_ref, nm_ref, nv_ref):
        gg = g_ref[...]
        nm = ADAM_B1 * m_ref[...] + (1.0 - ADAM_B1) * gg
        nv = ADAM_B2 * v_ref[...] + (1.0 - ADAM_B2) * (gg * gg)
        m_hat = nm / (1.0 - ADAM_B1 ** ADAM_STEP)
        v_hat = nv / (1.0 - ADAM_B2 ** ADAM_STEP)
        d_ref[...] = -ADAM_LR * (m_hat / (jnp.sqrt(v_hat) + ADAM_EPS) + ADAM_WD * w_ref[...])
        nm_ref[...] = nm
        nv_ref[...] = nv

    return _call(body, name=name, grid=(rows // tr,), in_specs=[spec, g_spec, spec, spec], out_specs=[spec] * 3,
                 out_shape=[jax.ShapeDtypeStruct((rows, cols), F32)] * 3, sem=("parallel",))(w, g, m, v)


WEIGHT_NAMES = ("mix_norm", "ab_w_in", "lru_conv_w", "lru_conv_b", "lru_wa", "lru_ba", "lru_wx", "lru_bx", "lru_lambda",
                "ab_w_out", "c_w_qkv", "c_b_qkv", "c_sinks", "c_w_out", "c_b_out", "xa_norm", "xa_mem_norm", "xa_wq",
                "xa_wkv", "xa_wo", "ffn_norm", "ffn_w_gate_up", "ffn_w_down", "final_norm")
EARLY_GROUPS = (("ab_w_in",),)
MID_GROUPS = (("ab_w_out",), ("lru_wa", "lru_wx"))
LATE_GROUPS = (("c_w_out", "xa_wkv", "ffn_w_down"), ("ffn_w_gate_up",), ("xa_wo",), ("xa_wq",), ("c_w_qkv",))
GROUPS = EARLY_GROUPS + MID_GROUPS + LATE_GROUPS
REPLICATED = ("mix_norm", "lru_conv_b", "lru_lambda", "c_sinks", "xa_norm", "xa_mem_norm", "ffn_norm", "final_norm")
SMALL_SHARDED = ("lru_conv_w", "lru_ba", "lru_bx", "c_b_qkv", "c_b_out")
LANES = 1024


def _rows(v):
    flat = v.reshape(-1)
    return jnp.pad(flat, (0, -flat.shape[0] % LANES)).reshape(-1, LANES)


def _pack_small(parts, total, *, name):
    def body(*refs):
        o_ref = refs[-1]
        o_ref[...] = jnp.zeros_like(o_ref)
        row = 0
        for p_ref in refs[:-1]:
            o_ref[row:row + p_ref.shape[0], :] = p_ref[...]
            row += p_ref.shape[0]

    return _call(body, name=name, grid=(1,), in_specs=[pl.BlockSpec(p.shape, lambda i: (0, 0)) for p in parts],
                 out_specs=pl.BlockSpec((total, LANES), lambda i: (0, 0)),
                 out_shape=jax.ShapeDtypeStruct((total, LANES), F32), sem=("arbitrary",))(*parts)


def _from_shards(name, t):
    minor = t.shape[-1]
    if name == "ab_w_in":
        return t
    if name in ("ab_w_out", "c_w_out"):
        return t.reshape(1, -1, minor)
    if name == "ffn_w_gate_up":
        return t.reshape(N_CHIPS, 2, -1, minor)
    if name in ("xa_wq", "xa_wkv", "ffn_w_down"):
        return t.reshape(N_CHIPS, 2, -1, minor).transpose(1, 0, 2, 3).reshape(2, -1, minor)
    if name in ("lru_wa", "lru_wx"):
        return t.reshape(N_CHIPS, LRU_HEADS, -1, minor).transpose(1, 0, 2, 3).reshape(LRU_HEADS, LRU_HEAD_DIM, minor)
    if name == "xa_wo":
        return t.reshape(N_CHIPS, 2, -1, minor).transpose(1, 0, 2, 3)
    assert name == "c_w_qkv"
    return t.transpose(1, 0, 2).reshape(1, D_MODEL, -1)


def _piece_shards(name, g):
    minor = g.shape[-1]
    if name in ("ab_w_in", "ffn_w_gate_up", "xa_wo"):
        return g
    if name in ("ab_w_out", "c_w_out", "xa_wq", "xa_wkv", "ffn_w_down"):
        return g.reshape(N_CHIPS, -1, minor)
    if name in ("lru_wa", "lru_wx"):
        return g.reshape(LRU_HEADS, N_CHIPS, -1, minor).transpose(1, 0, 2, 3).reshape(N_CHIPS, -1, minor)
    assert name == "c_w_qkv"
    return g.reshape(D_MODEL, N_CHIPS, -1).transpose(1, 0, 2)


RS_SETS = {
    "l1": ((("c_w_out", None), ("xa_wkv", 1), ("ffn_w_down", 1)), (("ffn_w_gate_up", 1),), (("xa_wo", 1),),
           (("xa_wq", 1),), (("c_w_qkv", None),)),
    "l0a": ((("xa_wkv", 0), ("ffn_w_down", 0)), (("ffn_w_gate_up", 0),), (("xa_wo", 0),), (("xa_wq", 0),)),
    "l0b": ((("ab_w_out", None),), (("ab_w_in", None),), (("lru_wa", None), ("lru_wx", None))),
}
RS_STAGE = {"layer1": "l1", "layer0_ffn_xa": "l0a"}


def kernel(x, mem, mix_norm, ab_w_in, lru_conv_w, lru_conv_b, lru_wa, lru_ba, lru_wx, lru_bx, lru_lambda, ab_w_out, c_w_qkv, c_b_qkv, c_sinks, c_w_out, c_b_out, xa_norm, xa_mem_norm, xa_wq, xa_wkv, xa_wo, ffn_norm, ffn_w_gate_up, ffn_w_down, final_norm, loss_target, m_mix_norm, m_ab_w_in, m_lru_conv_w, m_lru_conv_b, m_lru_wa, m_lru_ba, m_lru_wx, m_lru_bx, m_lru_lambda, m_ab_w_out, m_c_w_qkv, m_c_b_qkv, m_c_sinks, m_c_w_out, m_c_b_out, m_xa_norm, m_xa_mem_norm, m_xa_wq, m_xa_wkv, m_xa_wo, m_ffn_norm, m_ffn_w_gate_up, m_ffn_w_down, m_final_norm, v_mix_norm, v_ab_w_in, v_lru_conv_w, v_lru_conv_b, v_lru_wa, v_lru_ba, v_lru_wx, v_lru_bx, v_lru_lambda, v_ab_w_out, v_c_w_qkv, v_c_b_qkv, v_c_sinks, v_c_w_out, v_c_b_out, v_xa_norm, v_xa_mem_norm, v_xa_wq, v_xa_wkv, v_xa_wo, v_ffn_norm, v_ffn_w_gate_up, v_ffn_w_down, v_final_norm):
    given = dict(locals())
    wl = {n: given[n] for n in WEIGHT_NAMES}
    ml = {n: given["m_" + n] for n in WEIGHT_NAMES}
    vl = {n: given["v_" + n] for n in WEIGHT_NAMES}
    xi, yi, ci = lax.axis_index("x"), lax.axis_index("y"), lax.axis_index("c")
    chip = 2 * xi + yi

    def join(parts, axis):
        return parts[0] if len(parts) == 1 else jnp.concatenate(parts, axis=axis)

    local_rows = {n: wl[n].size // wl[n].shape[-1] for grp in GROUPS for n in grp}
    packs = [join([wl[n].astype(BF16).reshape(local_rows[n], wl[n].shape[-1]) for n in grp], 0) for grp in GROUPS]
    spack = _pack_small([_rows(wl[n]) for n in SMALL_SHARDED], 8, name="pack_small_weights")
    n_early, n_mid = len(EARLY_GROUPS), len(EARLY_GROUPS) + len(MID_GROUPS)
    early, sfull = _gather_weights(packs[:n_early], spack)
    early, sfull, mid_packs = lax.optimization_barrier((early, sfull, packs[n_early:n_mid]))
    mid = _gather_weights_behind(mid_packs, tag="mid")
    mid, late_packs = lax.optimization_barrier((mid, packs[n_mid:]))
    gathered = early + mid + _gather_weights_behind(late_packs, tag="late")
    w = {n: wl[n] for n in REPLICATED}
    w["c_sinks"] = wl["c_sinks"][0]
    for grp, full in zip(GROUPS, gathered):
        off = 0
        for n in grp:
            w[n] = _from_shards(n, full if len(grp) == 1 else full[:, off:off + local_rows[n]])
            off += local_rows[n]
    for r, n in enumerate(SMALL_SHARDED):
        loc = wl[n].shape[1:]
        t = sfull[:, r, :wl[n].size].reshape((N_CHIPS,) + loc)
        if n == "lru_conv_w":
            w[n] = t.transpose(1, 0, 2).reshape(CONV_WIDTH, -1)
        elif n in ("lru_ba", "lru_bx"):
            w[n] = t.transpose(1, 0, 2).reshape(1, -1)
        else:
            w[n] = t.reshape(1, -1)

    place = jnp.stack([ci, chip]).astype(jnp.int32)

    def pair_stage(spec, g, tag):
        piece = lambda n, l: (g[n] if l is None else g[n, l]).astype(BF16)
        gpacks = [join([_piece_shards(n, piece(n, l)) for n, l in grp], 1) for grp in spec]
        ras = _rs_pair_exchange(gpacks, name=f"rs_pair_exchange_{tag}")
        sums = [_rs_pair_add(place, gp, ra, name=f"rs_pair_add_{tag}_{i}") for i, (gp, ra) in enumerate(zip(gpacks, ras))]
        return [pair for pair, _ in sums], [own for _, own in sums]

    reduced, in_flight = [], []

    def take_up():
        done = [_rs_final_add(place, o, r, name=f"rs_final_add_{len(reduced) + i}") for i, (o, r) in enumerate(in_flight)]
        reduced.extend(done)
        in_flight.clear()
        return done

    def reduce_behind(stage, g):
        done = take_up()
        tag = RS_STAGE[stage]
        pairs, own = pair_stage(RS_SETS[tag], g, tag)
        in_flight.extend(zip(own, _rs_chip_exchange_behind(pairs, tag=tag)))
        return own + done

    loss_part, grad_x, g = _device_step(x[0], mem[0], loss_target[0], w, on_grads=reduce_behind)

    small_parts = [_rows(g[n]) for n in REPLICATED] + [_rows(jnp.broadcast_to(loss_part, (LANES,)))]
    small_parts += [_rows(g[n]) for n in SMALL_SHARDED]
    small = _pack_small(small_parts, 24, name="pack_small_grads")
    take_up()
    pairs, own = pair_stage(RS_SETS["l0b"], g, "l0b")
    *rb, rs = _rs_chip_exchange_behind(pairs, tag="l0b", small=small)
    gsums = list(_rs_sibling_share(list(reduced), name="rs_sibling_share_behind"))
    in_flight.extend(zip(own, rb))
    gsums += list(_rs_sibling_share(take_up(), name="rs_sibling_share_last"))
    ssum = _sum_slots(rs)

    where = {}
    for grp, gsum in zip(RS_SETS["l1"] + RS_SETS["l0a"] + RS_SETS["l0b"], gsums):
        off = 0
        for n, l in grp:
            rows = local_rows[n] if l is None else local_rows[n] // 2
            where[n, l] = (gsum, off, rows, len(grp) == 1)
            off += rows
    take = lambda gsum, off, rows, whole: gsum if whole else gsum[off:off + rows]
    grads, grad_rows = {}, {}
    for grp in LATE_GROUPS + EARLY_GROUPS + MID_GROUPS:
        for n in grp:
            if (n, None) in where:
                grads[n] = take(*where[n, None]).reshape(wl[n].shape)
                grad_rows[n] = where[n, None][:2]
            else:
                grads[n] = jnp.stack([take(*where[n, l]).reshape(wl[n].shape[1:]) for l in range(2)])
                grad_rows[n] = (grads[n].reshape(local_rows[n], wl[n].shape[-1]), 0)
    row = 0
    for n in REPLICATED:
        k = _rows(g[n]).shape[0]
        grads[n] = ssum[row:row + k].reshape(-1)[:wl[n].size].reshape(wl[n].shape)
        row += k
    loss = ssum[row, 0]
    row += 1
    for n in SMALL_SHARDED:
        k = _rows(g[n]).shape[0]
        full = ssum[row:row + k].reshape(-1)[:g[n].size]
        row += k
        loc = wl[n].shape
        if n == "lru_conv_w":
            sh = full.reshape(CONV_WIDTH, N_CHIPS, -1)
        elif n in ("lru_ba", "lru_bx"):
            sh = full.reshape(LRU_HEADS, N_CHIPS, -1)
        else:
            sh = full.reshape(1, N_CHIPS, -1)
        grads[n] = lax.dynamic_index_in_dim(sh, chip, axis=1, keepdims=False).reshape(loc)

    delta, new_m, new_v = {}, {}, {}
    for n, (gsum, off) in grad_rows.items():
        shape2 = (local_rows[n], wl[n].shape[-1])
        d, nm, nv = _adamw(wl[n].reshape(shape2), gsum, ml[n].reshape(shape2), vl[n].reshape(shape2), g_row=off,
                           name="adamw_" + n)
        delta[n], new_m[n], new_v[n] = (t.reshape(wl[n].shape) for t in (d, nm, nv))
    smalls = REPLICATED + SMALL_SHARDED
    packs = [_pack_small([_rows(src[n]) for n in smalls], 24, name="pack_adamw_" + tag)
             for tag, src in (("w", wl), ("g", grads), ("m", ml), ("v", vl))]
    outs = _adamw(*packs, name="adamw_small")
    row = 0
    for n in smalls:
        k = _rows(wl[n]).shape[0]
        for dst, o in zip((delta, new_m, new_v), outs):
            dst[n] = o[row:row + k].reshape(-1)[:wl[n].size].reshape(wl[n].shape)
        row += k

    return (loss, grad_x[None], *[grads[n] for n in WEIGHT_NAMES], *[delta[n] for n in WEIGHT_NAMES],
            *[new_m[n] for n in WEIGHT_NAMES], *[new_v[n] for n in WEIGHT_NAMES])
```

```python
import jax
import jax.numpy as jnp
from jax import lax
from jax.experimental import pallas as pl
from jax.experimental.pallas import tpu as pltpu
from jax.experimental.pallas import tpu_sc as plsc

F32, BF16 = jnp.float32, jnp.bfloat16
D_MODEL = 1024
NORM_EPS = 1e-6
ROPE_THETA = 500000.0
HEAD_DIM = 64
ROT_DIM = 16
BLK = 128
LRU_HEADS, LRU_HEAD_DIM, CONV_WIDTH, LRU_C = 4, 256, 4, 8.0
DILATED_PATTERN = ((128, 1), (512, 4), (2048, 16))
B_HEADS, C_HEADS, C_KV_HEADS, C_WINDOW = 8, 16, 2, 128
XA_HEADS, XA_HEAD_DIM, N_MEM = 4, 128, 256
D_FF = 2816
NEG = -1e30
ADAM_LR, ADAM_B1, ADAM_B2, ADAM_EPS, ADAM_WD, ADAM_STEP = 0.001, 0.9, 0.999, 1e-08, 0.01, 10
N_CHIPS = 4
VMEM_LIMIT_V7X = 56 * 1024 * 1024

NN = (((1,), (0,)), ((), ()))
NT = (((1,), (1,)), ((), ()))
TN = (((0,), (0,)), ((), ()))


def _dot(a, b, dims=NN):
    return lax.dot_general(a, b, dims, preferred_element_type=F32)


def _sigmoid(x):
    return 0.5 * jnp.tanh(0.5 * x) + 0.5


def _call(body, *, name, grid, in_specs, out_specs, out_shape, scratch=(), sem=None):
    return pl.pallas_call(
        body, name=name, grid=grid, in_specs=in_specs, out_specs=out_specs, out_shape=out_shape,
        scratch_shapes=list(scratch),
        compiler_params=pltpu.CompilerParams(dimension_semantics=sem, vmem_limit_bytes=VMEM_LIMIT_V7X))


def _rope_tables(L):
    half = ROT_DIM // 2
    inv = ROPE_THETA ** (-jnp.arange(0, ROT_DIM, 2, dtype=F32) / ROT_DIM)
    j = jnp.arange(2 * HEAD_DIM) % HEAD_DIM
    ang = jnp.arange(L, dtype=F32)[:, None] * inv[j % half][None, :]
    cos, sin = jnp.cos(ang), jnp.sin(ang)
    c = jnp.where(j < ROT_DIM, cos, 1.0)
    s1 = jnp.where(j < half, -sin, 0.0)
    s2 = jnp.where((j >= half) & (j < ROT_DIM), sin, 0.0)
    return c, s1, s2


def _rope_fwd(v, c, s1, s2):
    return v * c + pltpu.roll(v, 120, 1) * s1 + pltpu.roll(v, 8, 1) * s2


def _rope_bwd(dv, c, s1, s2):
    return dv * c + pltpu.roll(dv * s1, 8, 1) + pltpu.roll(dv * s2, 120, 1)


def _weight_spec(w, layer):
    once = pl.Buffered(1)
    if layer is None:
        return w.shape, pl.BlockSpec(w.shape, lambda i: (0, 0, 0), pipeline_mode=once)
    S, _, K, Ns = w.shape
    return (S, K, Ns), pl.BlockSpec((S, None, K, Ns), lambda i: (0, layer, 0, 0), pipeline_mode=once)


def _rowmm(a, w3, *, name, tm=512, gain=None, bias=None, res=None, swiglu=False, rope=None, layer=None):
    M, K = a.shape
    (S, _, Ns), w_spec = _weight_spec(w3, layer)
    N = S * Ns
    tm = min(tm, M)
    has_norm, has_bias, has_res, has_rope = gain is not None, bias is not None, res is not None, rope is not None
    row = lambda w: pl.BlockSpec((tm, w), lambda i: (i, 0))
    whole = lambda shape: pl.BlockSpec(shape, lambda i: (0,) * len(shape))
    ins, specs = [a], [row(K)]
    if has_norm:
        ins.append(gain.reshape(1, K)); specs.append(whole((1, K)))
    ins.append(w3); specs.append(w_spec)
    if has_bias:
        ins.append(bias.reshape(1, N)); specs.append(whole((1, N)))
    if has_res:
        ins.append(res); specs.append(row(N))
    if has_rope:
        ins += list(rope[2]); specs += [row(128)] * 3
    y_dtype = F32 if has_res else BF16
    out_shape, out_specs = [jax.ShapeDtypeStruct((M, N), y_dtype)], [row(N)]
    if has_norm:
        out_shape.append(jax.ShapeDtypeStruct((M, K), BF16)); out_specs.append(row(K))
    if swiglu:
        out_shape.append(jax.ShapeDtypeStruct((M, N // 2), BF16)); out_specs.append(row(N // 2))
    scratch = [pltpu.VMEM((tm, N), F32)] if has_rope else []

    def body(*refs):
        it = iter(refs)
        a_ref = next(it)
        g_ref = next(it) if has_norm else None
        w_ref = next(it)
        b_ref = next(it) if has_bias else None
        r_ref = next(it) if has_res else None
        tabs = [next(it) for _ in range(3)] if has_rope else None
        y_ref = next(it)
        n_ref = next(it) if has_norm else None
        act_ref = next(it) if swiglu else None
        ys_ref = next(it) if has_rope else None
        if has_norm:
            x = a_ref[...].astype(F32)
            ms = jnp.mean(x * x, axis=-1, keepdims=True)
            xb = (x * lax.rsqrt(ms + NORM_EPS) * g_ref[...]).astype(BF16)
            n_ref[...] = xb
        else:
            xb = a_ref[...].astype(BF16)
        if swiglu:
            for s in range(S // 2):
                g = _dot(xb, w_ref[s])
                u = _dot(xb, w_ref[s + S // 2])
                y_ref[:, s * Ns:(s + 1) * Ns] = g.astype(BF16)
                y_ref[:, N // 2 + s * Ns:N // 2 + (s + 1) * Ns] = u.astype(BF16)
                act_ref[:, s * Ns:(s + 1) * Ns] = (g * _sigmoid(g) * u).astype(BF16)
            return
        for s in range(S):
            sl = slice(s * Ns, (s + 1) * Ns)
            acc = _dot(xb, w_ref[s])
            if has_bias:
                acc = acc + b_ref[:, sl]
            if has_res:
                acc = acc + r_ref[:, sl]
            if has_rope:
                ys_ref[:, sl] = acc
            else:
                y_ref[:, sl] = acc.astype(y_dtype)
        if has_rope:
            c, s1, s2 = (t[...] for t in tabs)
            for cb in range(N // 128):
                cs = slice(cb * 128, (cb + 1) * 128)
                v = ys_ref[:, cs]
                if rope[0] <= cb * 128 < rope[1]:
                    v = _rope_fwd(v, c, s1, s2)
                y_ref[:, cs] = v.astype(BF16)

    return _call(body, name=name, grid=(M // tm,), in_specs=specs, out_specs=out_specs, out_shape=out_shape,
                 scratch=scratch, sem=("parallel",))(*ins)


def _mm_nt(dy, w3, *, name, mode, tm=512, kchunk=None, h=None, gain=None, dh=None, gu=None, layer=None, after=None):
    M, N = dy.shape
    (S, K, Ns), w_spec = _weight_spec(w3, layer)
    kchunk = kchunk or K
    tm = min(tm, M)
    row = lambda w: pl.BlockSpec((tm, w), lambda i: (i, 0))
    whole = lambda shape: pl.BlockSpec(shape, lambda i: (0,) * len(shape))
    ins, specs = [dy, w3], [row(N), w_spec]
    after = list(after or ())
    ins = after + ins
    specs = [pl.BlockSpec((8, a.shape[1]), lambda i: (0, 0)) for a in after] + specs
    has_dh = dh is not None
    if mode == "norm":
        ins += [h, gain.reshape(1, K)]; specs += [row(K), whole((1, K))]
        if has_dh:
            ins.append(dh); specs.append(row(K))
        out_shape = [jax.ShapeDtypeStruct((M, K), F32), jax.ShapeDtypeStruct((1, K), F32)]
        out_specs = [row(K), whole((1, K))]
    elif mode == "swiglu":
        ins.append(gu); specs.append(row(2 * K))
        out_shape, out_specs = [jax.ShapeDtypeStruct((M, 2 * K), BF16)], [row(2 * K)]
    else:
        out_shape, out_specs = [jax.ShapeDtypeStruct((M, K), BF16)], [row(K)]

    def body(*refs):
        it = iter(refs[len(after):])
        dy_ref, w_ref = next(it), next(it)
        if mode == "norm":
            h_ref, g_ref = next(it), next(it)
            dh_ref = next(it) if has_dh else None
            o_ref, dg_ref = next(it), next(it)
        elif mode == "swiglu":
            gu_ref, o_ref = next(it), next(it)
        else:
            o_ref = next(it)
        for kc in range(K // kchunk):
            ks = slice(kc * kchunk, (kc + 1) * kchunk)
            acc = None
            for s in range(S):
                t = _dot(dy_ref[:, s * Ns:(s + 1) * Ns].astype(BF16), w_ref[s, ks, :], NT)
                acc = t if acc is None else acc + t
            if mode == "plain":
                o_ref[:, ks] = acc.astype(BF16)
            elif mode == "swiglu":
                us = slice(K + kc * kchunk, K + (kc + 1) * kchunk)
                g = gu_ref[:, ks].astype(F32)
                u = gu_ref[:, us].astype(F32)
                sg = _sigmoid(g)
                o_ref[:, ks] = (acc * u * (sg * (1.0 + g * (1.0 - sg)))).astype(BF16)
                o_ref[:, us] = (acc * (g * sg)).astype(BF16)
            else:
                x = h_ref[...].astype(F32)
                r = lax.rsqrt(jnp.mean(x * x, axis=-1, keepdims=True) + NORM_EPS)
                xhat = x * r
                dxh = acc * g_ref[...]
                dx = r * (dxh - xhat * jnp.mean(dxh * xhat, axis=-1, keepdims=True))
                o_ref[...] = dx + dh_ref[...] if has_dh else dx

                @pl.when(pl.program_id(0) == 0)
                def _():
                    dg_ref[...] = jnp.zeros_like(dg_ref)

                dg_ref[...] += jnp.sum(acc * xhat, axis=0, keepdims=True)

    sem = ("arbitrary",) if mode == "norm" else ("parallel",)
    return _call(body, name=name, grid=(M // tm,), in_specs=specs, out_specs=out_specs, out_shape=out_shape, sem=sem)(*ins)


def _mm_tn(x, dy, *, S, name, tk=2048, kk=None, bias=False):
    M, K = x.shape
    N = dy.shape[1]
    Ns = N // S
    kk = kk or K
    tk = min(tk, M)
    nl = M // tk
    in_specs = [pl.BlockSpec((tk, kk), lambda s, kc, l: (l, kc)), pl.BlockSpec((tk, Ns), lambda s, kc, l: (l, s))]
    out_shape = [jax.ShapeDtypeStruct((S, K, Ns), BF16)]
    out_specs = [pl.BlockSpec((None, kk, Ns), lambda s, kc, l: (s, kc, 0))]
    if bias:
        out_shape.append(jax.ShapeDtypeStruct((1, N), F32))
        out_specs.append(pl.BlockSpec((1, Ns), lambda s, kc, l: (0, s)))

    def body(x_ref, dy_ref, o_ref, *rest):
        acc_ref = rest[-1]
        kc, l = pl.program_id(1), pl.program_id(2)

        @pl.when(l == 0)
        def _():
            acc_ref[...] = jnp.zeros_like(acc_ref)

        acc_ref[...] += _dot(x_ref[...].astype(BF16), dy_ref[...].astype(BF16), TN)
        if bias:
            b_ref = rest[0]

            @pl.when((kc == 0) & (l == 0))
            def _():
                b_ref[...] = jnp.zeros_like(b_ref)

            @pl.when(kc == 0)
            def _():
                b_ref[...] += jnp.sum(dy_ref[...].astype(F32), axis=0, keepdims=True)

        @pl.when(l == nl - 1)
        def _():
            o_ref[...] = acc_ref[...].astype(BF16)

    return _call(body, name=name, grid=(S, K // kk, nl), in_specs=in_specs, out_specs=out_specs, out_shape=out_shape,
                 scratch=[pltpu.VMEM((kk, Ns), F32)], sem=("arbitrary", "arbitrary", "arbitrary"))(x, dy)


def _band_bias(max_dist, has_prev):
    rows = lax.broadcasted_iota(jnp.int32, (BLK, 2 * BLK), 0)
    cols = lax.broadcasted_iota(jnp.int32, (BLK, 2 * BLK), 1)
    dist = rows - cols + BLK
    ok = (dist >= 0) & (dist <= max_dist) & ((cols >= BLK) | has_prev)
    return jnp.where(ok, 0.0, NEG)


Q_SCALE = HEAD_DIM ** -0.5
BNT = (((2,), (2,)), ((0,), (0,)))
BNN = (((2,), (1,)), ((0,), (0,)))
BTN = (((1,), (1,)), ((0,), (0,)))


def _bdot(a, b, dims):
    return lax.dot_general(a, b, dims, preferred_element_type=F32)


def _split_heads(x, n):
    return jnp.stack([x[:, h * HEAD_DIM:(h + 1) * HEAD_DIM] for h in range(n)], axis=0)


def _band_fwd(qa, ka, va, *, d, nq, nkv, qcol, kcol, vcol, max_dist, sinks=None, name):
    Lr = qa.shape[0]
    nb = Lr // BLK
    qw, kw, G = nq * HEAD_DIM, nkv * HEAD_DIM, nq // nkv
    cur = lambda colf, w: pl.BlockSpec((BLK, w), lambda r, i: (i, colf(r)))
    prv = lambda colf, w: pl.BlockSpec((BLK, w), lambda r, i: (jnp.maximum(i - 1, 0), colf(r)))
    out = pl.BlockSpec((BLK, qw), lambda r, i: (i, r))
    ins, specs = [qa, ka, ka, va, va], [cur(qcol, qw), cur(kcol, kw), prv(kcol, kw), cur(vcol, kw), prv(vcol, kw)]
    has_sinks = sinks is not None
    if has_sinks:
        ins.append(sinks); specs.append(pl.BlockSpec(memory_space=pltpu.SMEM))

    def body(*refs):
        q_ref, kc_ref, kp_ref, vc_ref, vp_ref = refs[:5]
        sk_ref = refs[5] if has_sinks else None
        o_ref, lse_ref = refs[-2], refs[-1]
        bias = _band_bias(max_dist, pl.program_id(1) > 0)
        k2 = jnp.concatenate([kp_ref[...], kc_ref[...]], axis=0)
        v2 = jnp.concatenate([vp_ref[...], vc_ref[...]], axis=0)
        if G == 1 and not has_sinks:
            q3 = _split_heads(q_ref[...], nq) * jnp.asarray(Q_SCALE, BF16)
            s = _bdot(q3, _split_heads(k2, nq), BNT) + bias
            m = jnp.max(s, axis=-1, keepdims=True)
            p = jnp.exp(s - m)
            l = jnp.sum(p, axis=-1, keepdims=True)
            o = (_bdot(p.astype(BF16), _split_heads(v2, nq), BNN) / l).astype(BF16)
            lse = m + jnp.log(l)
            for h in range(nq):
                hs = slice(h * HEAD_DIM, (h + 1) * HEAD_DIM)
                o_ref[:, hs] = o[h]
                lse_ref[:, hs] = jnp.broadcast_to(lse[h], (BLK, HEAD_DIM))
            return
        for h in range(nq):
            hs = slice(h * HEAD_DIM, (h + 1) * HEAD_DIM)
            ks = slice((h // G) * HEAD_DIM, (h // G + 1) * HEAD_DIM)
            s = _dot(q_ref[:, hs] * jnp.asarray(Q_SCALE, BF16), k2[:, ks], NT) + bias
            m = jnp.max(s, axis=-1, keepdims=True)
            if has_sinks:
                m = jnp.maximum(m, sk_ref[h])
            p = jnp.exp(s - m)
            l = jnp.sum(p, axis=-1, keepdims=True)
            if has_sinks:
                l = l + jnp.exp(sk_ref[h] - m)
            o_ref[:, hs] = (_dot(p.astype(BF16), v2[:, ks]) / l).astype(BF16)
            lse_ref[:, hs] = jnp.broadcast_to(m + jnp.log(l), (BLK, HEAD_DIM))

    return _call(body, name=name, grid=(d, nb), in_specs=specs, out_specs=[out, out],
                 out_shape=[jax.ShapeDtypeStruct((Lr, d * qw), BF16), jax.ShapeDtypeStruct((Lr, d * qw), F32)],
                 sem=("parallel", "parallel"))(*ins)


def _band_bwd(qa, ka, va, doa, oa, lsea, *, d, nq, nkv, qcol, kcol, vcol, docol, max_dist, sinks=None, name):
    Lr = qa.shape[0]
    nb = Lr // BLK
    qw, kw, G = nq * HEAD_DIM, nkv * HEAD_DIM, nq // nkv
    transposed = G > 1
    last = lambda i: jnp.minimum(i, nb - 1)
    cur = lambda colf, w: pl.BlockSpec((BLK, w), lambda r, i: (last(i), colf(r)))
    prv = lambda colf, w: pl.BlockSpec((BLK, w), lambda r, i: (jnp.maximum(last(i) - 1, 0), colf(r)))
    own = lambda r: r
    ins = [qa, ka, ka, va, va, doa, oa, lsea]
    specs = [cur(qcol, qw), cur(kcol, kw), prv(kcol, kw), cur(vcol, kw), prv(vcol, kw), cur(docol, qw), cur(own, qw),
             cur(own, qw)]
    has_sinks = sinks is not None
    if has_sinks:
        ins.append(sinks); specs.append(pl.BlockSpec(memory_space=pltpu.SMEM))
    out_shape = [jax.ShapeDtypeStruct((Lr, d * qw), BF16), jax.ShapeDtypeStruct((Lr, d * kw), BF16),
                 jax.ShapeDtypeStruct((Lr, d * kw), BF16)]
    behind = lambda r, i: (jnp.maximum(i - 1, 0), r)
    out_specs = [pl.BlockSpec((BLK, qw), lambda r, i: (last(i), r)), pl.BlockSpec((BLK, kw), behind),
                 pl.BlockSpec((BLK, kw), behind)]
    if has_sinks:
        out_shape.append(jax.ShapeDtypeStruct((8, 128), F32))
        out_specs.append(pl.BlockSpec((8, 128), lambda r, i: (0, 0)))

    def body(*refs):
        it = iter(refs)
        q_ref, kc_ref, kp_ref, vc_ref, vp_ref, do_ref, o_ref, ls_ref = (next(it) for _ in range(8))
        sk_ref = next(it) if has_sinks else None
        dq_ref, dk_ref, dv_ref = next(it), next(it), next(it)
        dsk_ref = next(it) if has_sinks else None
        dk_car, dv_car = next(it), next(it)
        r_id, i = pl.program_id(0), pl.program_id(1)

        @pl.when(i == 0)
        def _():
            dk_car[...] = jnp.zeros_like(dk_car)
            dv_car[...] = jnp.zeros_like(dv_car)

        if has_sinks:
            @pl.when((r_id == 0) & (i == 0))
            def _():
                dsk_ref[...] = jnp.zeros_like(dsk_ref)

        @pl.when(i == nb)
        def _():
            dk_ref[...] = dk_car[...].astype(BF16)
            dv_ref[...] = dv_car[...].astype(BF16)

        @pl.when(i < nb)
        def _():
            bias = jnp.tile(_band_bias(max_dist, i > 0), (G, 1))
            k2 = jnp.concatenate([kp_ref[...], kc_ref[...]], axis=0)
            v2 = jnp.concatenate([vp_ref[...], vc_ref[...]], axis=0)
            if G == 1 and not has_sinks:
                scale = jnp.asarray(Q_SCALE, BF16)
                q3, do3 = _split_heads(q_ref[...], nq) * scale, _split_heads(do_ref[...], nq)
                k3, v3 = _split_heads(k2, nq), _split_heads(v2, nq)
                lse = jnp.stack([ls_ref[:, h * HEAD_DIM:h * HEAD_DIM + 1] for h in range(nq)], axis=0)
                dl = jnp.sum(do3.astype(F32) * _split_heads(o_ref[...], nq).astype(F32), axis=-1, keepdims=True)
                p = jnp.exp(_bdot(q3, k3, BNT) + bias - lse)
                ds = (p * (_bdot(do3, v3, BNT) - dl)).astype(BF16)
                dq = (_bdot(ds, k3, BNN) * Q_SCALE).astype(BF16)
                dk, dv = _bdot(ds, q3, BTN), _bdot(p.astype(BF16), do3, BTN)
                for h in range(nq):
                    hs = slice(h * HEAD_DIM, (h + 1) * HEAD_DIM)
                    dq_ref[:, hs] = dq[h]
                    dk_ref[:, hs] = (dk_car[:, hs] + dk[h, :BLK]).astype(BF16)
                    dv_ref[:, hs] = (dv_car[:, hs] + dv[h, :BLK]).astype(BF16)
                    dk_car[:, hs] = dk[h, BLK:]
                    dv_car[:, hs] = dv[h, BLK:]
                return
            if has_sinks:
                lane = lax.broadcasted_iota(jnp.int32, (8, 128), 1)
                dsk = jnp.zeros((8, 128), F32)
            for kv in range(nkv):
                heads = [slice((kv * G + g) * HEAD_DIM, (kv * G + g + 1) * HEAD_DIM) for g in range(G)]
                stack = lambda ref: jnp.concatenate([ref[:, hs] for hs in heads], axis=0)
                ks = slice(kv * HEAD_DIM, (kv + 1) * HEAD_DIM)
                kh, vh = k2[:, ks], v2[:, ks]
                q = stack(q_ref) * jnp.asarray(Q_SCALE, BF16)
                do = stack(do_ref)
                lse = jnp.concatenate([ls_ref[:, hs.start:hs.start + 1] for hs in heads], axis=0)
                dl = jnp.sum(do.astype(F32) * stack(o_ref).astype(F32), axis=-1, keepdims=True)
                p = jnp.exp(_dot(q, kh, NT) + bias - lse)
                ds = (p * (_dot(do, vh, NT) - dl)).astype(BF16)
                dq = (_dot(ds, kh) * Q_SCALE).astype(BF16)
                for g, hs in enumerate(heads):
                    dq_ref[:, hs] = dq[g * BLK:(g + 1) * BLK]
                if transposed:
                    dk, dv = _dot(q, ds, TN).T, _dot(do, p.astype(BF16), TN).T
                else:
                    dk, dv = _dot(ds, q, TN), _dot(p.astype(BF16), do, TN)
                if has_sinks:
                    sk = jnp.concatenate([jnp.full((BLK, 1), sk_ref[kv * G + g], F32) for g in range(G)], axis=0)
                    lost = jnp.exp(sk - lse) * dl
                    for g in range(G):
                        val = -jnp.sum(lost[g * BLK:(g + 1) * BLK], axis=0, keepdims=True)
                        dsk = dsk + jnp.where(lane == kv * G + g, val, 0.0)
                dk_ref[:, ks] = (dk_car[:, ks] + dk[:BLK]).astype(BF16)
                dv_ref[:, ks] = (dv_car[:, ks] + dv[:BLK]).astype(BF16)
                dk_car[:, ks] = dk[BLK:]
                dv_car[:, ks] = dv[BLK:]
            if has_sinks:
                dsk_ref[...] += dsk

    return _call(body, name=name, grid=(d, nb + 1), in_specs=specs, out_specs=out_specs, out_shape=out_shape,
                 scratch=[pltpu.VMEM((BLK, kw), F32), pltpu.VMEM((BLK, kw), F32)], sem=("arbitrary", "arbitrary"))(*ins)


def _attn_grad_combine(branches, tabs, *, name, tm=512):
    L, qw = branches[0][0].shape
    kw = branches[0][1].shape[1]
    nbr = len(branches)
    row = lambda w: pl.BlockSpec((tm, w), lambda i: (i, 0))
    ins, specs = [], []
    for dq, dk, dv in branches:
        ins += [dq, dk, dv]; specs += [row(qw), row(kw), row(kw)]
    ins += list(tabs); specs += [row(128)] * 3

    def body(*refs):
        c, s1, s2 = (t[...] for t in refs[3 * nbr:3 * nbr + 3])
        o_ref = refs[-1]
        for part, (w, off, rot) in enumerate(((qw, 0, True), (kw, qw, True), (kw, qw + kw, False))):
            for cb in range(w // 128):
                cs = slice(cb * 128, (cb + 1) * 128)
                v = refs[part][:, cs].astype(F32)
                for b in range(1, nbr):
                    v = v + refs[3 * b + part][:, cs].astype(F32)
                if rot:
                    v = _rope_bwd(v, c, s1, s2)
                o_ref[:, off + cb * 128:off + (cb + 1) * 128] = v.astype(BF16)

    return _call(body, name=name, grid=(L // tm,), in_specs=specs, out_specs=row(qw + 2 * kw),
                 out_shape=jax.ShapeDtypeStruct((L, qw + 2 * kw), BF16), sem=("parallel",))(*ins)


def _xattn_fwd(q, kv, *, name, tq=1024):
    L, W = q.shape
    scale = XA_HEAD_DIM ** -0.5
    row = pl.BlockSpec((tq, W), lambda i: (i, 0))
    kvs = pl.BlockSpec((N_MEM, 2 * W), lambda i: (0, 0))

    def body(q_ref, kv_ref, o_ref, lse_ref):
        for h in range(XA_HEADS):
            hs = slice(h * XA_HEAD_DIM, (h + 1) * XA_HEAD_DIM)
            vs = slice(W + h * XA_HEAD_DIM, W + (h + 1) * XA_HEAD_DIM)
            s = _dot(q_ref[:, hs], kv_ref[:, hs], NT) * scale
            m = jnp.max(s, axis=-1, keepdims=True)
            p = jnp.exp(s - m)
            l = jnp.sum(p, axis=-1, keepdims=True)
            o_ref[:, hs] = (_dot(p.astype(BF16), kv_ref[:, vs]) / l).astype(BF16)
            lse_ref[:, hs] = jnp.broadcast_to(m + jnp.log(l), (tq, XA_HEAD_DIM))

    return _call(body, name=name, grid=(L // tq,), in_specs=[row, kvs], out_specs=[row, row],
                 out_shape=[jax.ShapeDtypeStruct((L, W), BF16), jax.ShapeDtypeStruct((L, W), F32)], sem=("parallel",))(q, kv)


def _xattn_bwd(q, kv, o, lse, do, *, name, tq=1024):
    L, W = q.shape
    scale = XA_HEAD_DIM ** -0.5
    row = pl.BlockSpec((tq, W), lambda i: (i, 0))
    kvs = pl.BlockSpec((N_MEM, 2 * W), lambda i: (0, 0))

    def body(q_ref, kv_ref, o_ref, lse_ref, do_ref, dq_ref, dkv_ref):
        @pl.when(pl.program_id(0) == 0)
        def _():
            dkv_ref[...] = jnp.zeros_like(dkv_ref)

        for h in range(XA_HEADS):
            hs = slice(h * XA_HEAD_DIM, (h + 1) * XA_HEAD_DIM)
            vs = slice(W + h * XA_HEAD_DIM, W + (h + 1) * XA_HEAD_DIM)
            qh, kh, vh, doh = q_ref[:, hs], kv_ref[:, hs], kv_ref[:, vs], do_ref[:, hs]
            p = jnp.exp(_dot(qh, kh, NT) * scale - lse_ref[:, h * XA_HEAD_DIM:h * XA_HEAD_DIM + 1])
            dl = jnp.sum(doh.astype(F32) * o_ref[:, hs].astype(F32), axis=-1, keepdims=True)
            ds = (p * (_dot(doh, vh, NT) - dl) * scale).astype(BF16)
            dq_ref[:, hs] = _dot(ds, kh).astype(BF16)
            dkv_ref[:, hs] += _dot(ds, qh, TN)
            dkv_ref[:, vs] += _dot(p.astype(BF16), doh, TN)

    return _call(body, name=name, grid=(L // tq,), in_specs=[row, kvs, row, row, row], out_specs=[row, kvs],
                 out_shape=[jax.ShapeDtypeStruct((L, W), BF16), jax.ShapeDtypeStruct((N_MEM, 2 * W), F32)],
                 sem=("arbitrary",))(q, kv, o, lse, do)


def _neg_expm1(z):
    series = -(z * (1.0 + z * (0.5 + z * (1.0 / 6.0 + z * (1.0 / 24.0 + z * (1.0 / 120.0))))))
    return jnp.where(z > -0.05, series, 1.0 - jnp.exp(z))


def _softplus(z):
    return jnp.maximum(z, 0.0) + jnp.log(1.0 + jnp.exp(-jnp.abs(z)))


def _gelu_parts(y):
    c = 0.7978845608028654
    t = jnp.tanh(c * (y + 0.044715 * y * y * y))
    gy = 0.5 * y * (1.0 + t)
    dgy = 0.5 * (1.0 + t) + 0.5 * y * (1.0 - t * t) * c * (1.0 + 3.0 * 0.044715 * y * y)
    return gy, dgy


def _lru_gates(xc, wa_ref, ba, wx_ref, bx, sp):
    rs, igs = [], []
    for hd in range(LRU_HEADS):
        sl = slice(hd * LRU_HEAD_DIM, (hd + 1) * LRU_HEAD_DIM)
        xh = xc[:, sl].astype(BF16)
        rs.append(_sigmoid(_dot(xh, wa_ref[hd]) + ba[:, sl]))
        igs.append(_sigmoid(_dot(xh, wx_ref[hd]) + bx[:, sl]))
    r, ig = jnp.concatenate(rs, axis=1), jnp.concatenate(igs, axis=1)
    la = -LRU_C * r * sp
    return r, ig, jnp.exp(la), _neg_expm1(2.0 * la)


def _conv_taps(x_ext, halo):
    n = x_ext.shape[0]
    return [x_ext[halo:] if k == CONV_WIDTH - 1 else pltpu.roll(x_ext, CONV_WIDTH - 1 - k, 0)[halo:]
            for k in range(CONV_WIDTH)]


def _lru_fwd(proj, cw, cb, wa, ba, wx, bx, lam, *, name, tc=512):
    L = proj.shape[0]
    W = LRU_HEADS * LRU_HEAD_DIM
    nb = L // tc
    whole = lambda shape: pl.BlockSpec(shape, lambda i: (0,) * len(shape))
    specs = [pl.BlockSpec((tc, W), lambda i: (i, 0)), pl.BlockSpec((tc, W), lambda i: (i, 1)),
             pl.BlockSpec((16, W), lambda i: (jnp.maximum(i * (tc // 16) - 1, 0), 0)),
             whole((CONV_WIDTH, W)), whole((1, W)), whole((LRU_HEADS, LRU_HEAD_DIM, LRU_HEAD_DIM)), whole((1, W)),
             whole((LRU_HEADS, LRU_HEAD_DIM, LRU_HEAD_DIM)), whole((1, W)), whole((1, W))]
    out_specs = [pl.BlockSpec((tc, W), lambda i: (i, 0))] * 2
    out_shape = [jax.ShapeDtypeStruct((L, W), BF16), jax.ShapeDtypeStruct((L, W), F32)]

    def body(x_ref, y_ref, xh_ref, cw_ref, cb_ref, wa_ref, ba_ref, wx_ref, bx_ref, lam_ref, rec_ref, hs_ref,
             hcar, a_scr, b_scr):
        i = pl.program_id(0)

        @pl.when(i == 0)
        def _():
            hcar[...] = jnp.zeros_like(hcar)

        halo = jnp.where(i > 0, xh_ref[...].astype(F32), 0.0)
        taps = _conv_taps(jnp.concatenate([halo, x_ref[...].astype(F32)], axis=0), 16)
        xc = cb_ref[...] + sum(cw_ref[k:k + 1, :] * taps[k] for k in range(CONV_WIDTH))
        _, ig, a, om = _lru_gates(xc, wa_ref, ba_ref[...], wx_ref, bx_ref[...], _softplus(-lam_ref[...]))
        b = jnp.sqrt(om) * (ig * xc)
        rowmod = lax.broadcasted_iota(jnp.int32, (tc, W), 0) & 7
        for s in (1, 2, 4):
            keep = rowmod >= s
            b = jnp.where(keep, a * pltpu.roll(b, s, 0) + b, b)
            a = jnp.where(keep, a * pltpu.roll(a, s, 0), a)
        a_scr[...] = a
        b_scr[...] = b

        def tile(j, hc):
            rows = pl.ds(pl.multiple_of(j * 8, 8), 8)
            ht = a_scr[rows, :] * hc + b_scr[rows, :]
            hs_ref[rows, :] = ht
            return jnp.broadcast_to(ht[7:8, :], (8, W))

        hcar[...] = lax.fori_loop(0, tc // 8, tile, hcar[...])
        gy, _ = _gelu_parts(y_ref[...].astype(F32))
        rec_ref[...] = (hs_ref[...] * gy).astype(BF16)

    return _call(body, name=name, grid=(nb,), in_specs=specs, out_specs=out_specs, out_shape=out_shape,
                 scratch=[pltpu.VMEM((8, W), F32), pltpu.VMEM((tc, W), F32), pltpu.VMEM((tc, W), F32)],
                 sem=("arbitrary",))(proj, proj, proj, cw, cb, wa, ba, wx, bx, lam)


def _lru_bwd(proj, hs, drec_src, cw, cb, wa, ba, wx, bx, lam, *, name, tc=256):
    L = proj.shape[0]
    W = LRU_HEADS * LRU_HEAD_DIM
    nb = L // tc
    tb = lambda i: nb - 1 - i
    whole = lambda shape: pl.BlockSpec(shape, lambda i: (0,) * len(shape))
    gate_w = (LRU_HEADS, LRU_HEAD_DIM, LRU_HEAD_DIM)
    specs = [pl.BlockSpec((tc, W), lambda i: (tb(i), 0)), pl.BlockSpec((tc, W), lambda i: (tb(i), 1)),
             pl.BlockSpec((16, W), lambda i: (jnp.maximum(tb(i) * (tc // 16) - 1, 0), 0)),
             pl.BlockSpec((tc, W), lambda i: (tb(i), 0)),
             pl.BlockSpec((8, W), lambda i: (jnp.maximum(tb(i) * (tc // 8) - 1, 0), 0)),
             pl.BlockSpec((tc, W), lambda i: (tb(i), 0)),
             whole((CONV_WIDTH, W)), whole((1, W)), whole(gate_w), whole((1, W)), whole(gate_w), whole((1, W)), whole((1, W))]
    out_specs = [pl.BlockSpec((tc, 2 * W), lambda i: (tb(i), 0)), whole((CONV_WIDTH, W)), whole((1, W)), whole(gate_w),
                 whole((1, W)), whole(gate_w), whole((1, W)), whole((1, W))]
    vec = jax.ShapeDtypeStruct((1, W), F32)
    out_shape = [jax.ShapeDtypeStruct((L, 2 * W), BF16), jax.ShapeDtypeStruct((CONV_WIDTH, W), F32), vec,
                 jax.ShapeDtypeStruct(gate_w, F32), vec, jax.ShapeDtypeStruct(gate_w, F32), vec, vec]

    def body(x_ref, y_ref, xh_ref, hs_ref, hh_ref, dr_ref, cw_ref, cb_ref, wa_ref, ba_ref, wx_ref, bx_ref, lam_ref,
             dxy_ref, dcw_ref, dcb_ref, dwa_ref, dba_ref, dwx_ref, dbx_ref, dlam_ref, gcar, dxc_car, a_scr, b_scr, g_scr):
        pid = pl.program_id(0)
        t = tb(pid)
        accs = (dcw_ref, dcb_ref, dwa_ref, dba_ref, dwx_ref, dbx_ref, dlam_ref)

        @pl.when(pid == 0)
        def _():
            gcar[...] = jnp.zeros_like(gcar)
            dxc_car[...] = jnp.zeros_like(dxc_car)
            for r in accs:
                r[...] = jnp.zeros_like(r)

        halo = jnp.where(t > 0, xh_ref[...].astype(F32), 0.0)
        taps = _conv_taps(jnp.concatenate([halo, x_ref[...].astype(F32)], axis=0), 16)
        xc = cb_ref[...] + sum(cw_ref[k:k + 1, :] * taps[k] for k in range(CONV_WIDTH))
        lam = lam_ref[...]
        sp = _softplus(-lam)
        r, ig, a, om = _lru_gates(xc, wa_ref, ba_ref[...], wx_ref, bx_ref[...], sp)
        sq = jnp.sqrt(om)
        hblk = hs_ref[...]
        hprev = pltpu.roll(jnp.concatenate([jnp.where(t > 0, hh_ref[...], 0.0), hblk], axis=0), 1, 0)[8:]
        gy, dgy = _gelu_parts(y_ref[...].astype(F32))
        drec = dr_ref[...].astype(F32)
        dxy_ref[:, W:] = (drec * hblk * dgy).astype(BF16)

        rowidx = lax.broadcasted_iota(jnp.int32, (tc, W), 0)
        rowmod = rowidx & 7
        ca = jnp.where(rowidx == tc - 1, 1.0, pltpu.roll(a, tc - 1, 0))
        cbv = drec * gy
        for s in (1, 2, 4):
            keep = rowmod < 8 - s
            cbv = jnp.where(keep, ca * pltpu.roll(cbv, tc - s, 0) + cbv, cbv)
            ca = jnp.where(keep, ca * pltpu.roll(ca, tc - s, 0), ca)
        a_scr[...] = ca
        b_scr[...] = cbv

        def tile(k, gc):
            j = tc // 8 - 1 - k
            rows = pl.ds(pl.multiple_of(j * 8, 8), 8)
            gt = a_scr[rows, :] * gc + b_scr[rows, :]
            g_scr[rows, :] = gt
            return jnp.broadcast_to(gt[0:1, :], (8, W))

        lax.fori_loop(0, tc // 8, tile, gcar[...])
        G = g_scr[...]
        gcar[...] = jnp.broadcast_to(a[0:1, :] * G[0:1, :], (8, W))

        da = G * hprev
        dsq = G * (ig * xc)
        di = G * (sq * xc)
        dxc = G * (sq * ig)
        dla = da * a - 2.0 * a * a * (dsq * 0.5 * lax.rsqrt(om))
        dlam_ref[...] += jnp.sum(dla * (-LRU_C * r), axis=0, keepdims=True) * (-_sigmoid(-lam))
        dpr = dla * (-LRU_C * sp) * r * (1.0 - r)
        dpi = di * ig * (1.0 - ig)
        dba_ref[...] += jnp.sum(dpr, axis=0, keepdims=True)
        dbx_ref[...] += jnp.sum(dpi, axis=0, keepdims=True)
        back = []
        for hd in range(LRU_HEADS):
            sl = slice(hd * LRU_HEAD_DIM, (hd + 1) * LRU_HEAD_DIM)
            xh, dprh, dpih = xc[:, sl].astype(BF16), dpr[:, sl].astype(BF16), dpi[:, sl].astype(BF16)
            back.append(_dot(dprh, wa_ref[hd], NT) + _dot(dpih, wx_ref[hd], NT))
            dwa_ref[hd] += _dot(xh, dprh, TN)
            dwx_ref[hd] += _dot(xh, dpih, TN)
        dxc = dxc + jnp.concatenate(back, axis=1)
        dcb_ref[...] += jnp.sum(dxc, axis=0, keepdims=True)
        for k in range(CONV_WIDTH):
            dcw_ref[k:k + 1, :] += jnp.sum(dxc * taps[k], axis=0, keepdims=True)
        ext = jnp.concatenate([dxc, dxc_car[...]], axis=0)
        dx = cw_ref[CONV_WIDTH - 1:CONV_WIDTH, :] * dxc
        for k in range(CONV_WIDTH - 1):
            dx = dx + cw_ref[k:k + 1, :] * pltpu.roll(ext, tc + 8 - (CONV_WIDTH - 1 - k), 0)[:tc]
        dxc_car[...] = dxc[0:8, :]
        dxy_ref[:, :W] = dx.astype(BF16)

    scratch = [pltpu.VMEM((8, W), F32), pltpu.VMEM((8, W), F32)] + [pltpu.VMEM((tc, W), F32)] * 3
    return _call(body, name=name, grid=(nb,), in_specs=specs, out_specs=out_specs, out_shape=out_shape, scratch=scratch,
                 sem=("arbitrary",))(proj, proj, proj, hs, hs, drec_src, cw, cb, wa, ba, wx, bx, lam)


def _final_loss(h, gain, target, *, name, tm=512):
    M, K = h.shape
    row = pl.BlockSpec((tm, K), lambda i: (i, 0))
    vec = pl.BlockSpec((1, K), lambda i: (0, 0))
    one = pl.BlockSpec((1, 128), lambda i: (0, 0))

    def body(h_ref, g_ref, t_ref, dh_ref, dg_ref, loss_ref):
        @pl.when(pl.program_id(0) == 0)
        def _():
            dg_ref[...] = jnp.zeros_like(dg_ref)
            loss_ref[...] = jnp.zeros_like(loss_ref)

        x = h_ref[...]
        r = lax.rsqrt(jnp.mean(x * x, axis=-1, keepdims=True) + NORM_EPS)
        xhat = x * r
        err = xhat * g_ref[...] - t_ref[...]
        loss_ref[...] += 0.5 / K * jnp.sum(err * err)
        dy = err * (1.0 / K)
        dg_ref[...] += jnp.sum(dy * xhat, axis=0, keepdims=True)
        dxh = dy * g_ref[...]
        dh_ref[...] = r * (dxh - xhat * jnp.mean(dxh * xhat, axis=-1, keepdims=True))

    return _call(body, name=name, grid=(M // tm,), in_specs=[row, vec, row], out_specs=[row, vec, one],
                 out_shape=[jax.ShapeDtypeStruct((M, K), F32), jax.ShapeDtypeStruct((1, K), F32),
                            jax.ShapeDtypeStruct((1, 128), F32)], sem=("arbitrary",))(h, gain.reshape(1, K), target)


def _dilated_merge(branches, *, name, tm=512):
    L, W = branches[0].shape
    nbr = len(branches) // 2
    row = pl.BlockSpec((tm, W), lambda i: (i, 0))

    def body(*refs):
        o_ref, lse_ref = refs[-2], refs[-1]
        lses = [refs[2 * b + 1][...] for b in range(nbr)]
        m = lses[0]
        for t in lses[1:]:
            m = jnp.maximum(m, t)
        ws = [jnp.exp(t - m) for t in lses]
        den = ws[0]
        for t in ws[1:]:
            den = den + t
        acc = ws[0] * refs[0][...].astype(F32)
        for b in range(1, nbr):
            acc = acc + ws[b] * refs[2 * b][...].astype(F32)
        o_ref[...] = (acc / den).astype(BF16)
        lse_ref[...] = m + jnp.log(den)

    return _call(body, name=name, grid=(L // tm,), in_specs=[row] * (2 * nbr), out_specs=[row, row],
                 out_shape=[jax.ShapeDtypeStruct((L, W), BF16), jax.ShapeDtypeStruct((L, W), F32)], sem=("parallel",))(*branches)


def _dilated_fwd(proj0):
    L = proj0.shape[0]
    qkv = proj0[:, 2 * D_MODEL:]
    W = B_HEADS * HEAD_DIM
    outs = []
    for window, d in DILATED_PATTERN:
        view = qkv.reshape(L // d, d * 3 * W)
        o, lse = _band_fwd(view, view, view, d=d, nq=B_HEADS, nkv=B_HEADS, qcol=lambda r: 3 * r, kcol=lambda r: 3 * r + 1,
                           vcol=lambda r: 3 * r + 2, max_dist=window // d, name=f"dilated_fwd_d{d}")
        outs += [o.reshape(L, W), lse.reshape(L, W)]
    return _dilated_merge(outs, name="dilated_merge")


def _dilated_bwd(proj0, att, lse, datt, tabs):
    L = proj0.shape[0]
    qkv = proj0[:, 2 * D_MODEL:]
    Wh = B_HEADS * HEAD_DIM
    branches = []
    for window, d in DILATED_PATTERN:
        view = qkv.reshape(L // d, d * 3 * Wh)
        v1 = lambda t: t.reshape(L // d, d * Wh)
        outs = _band_bwd(view, view, view, v1(datt), v1(att), v1(lse), d=d, nq=B_HEADS, nkv=B_HEADS,
                         qcol=lambda r: 3 * r, kcol=lambda r: 3 * r + 1, vcol=lambda r: 3 * r + 2, docol=lambda r: r,
                         max_dist=window // d, name=f"dilated_bwd_d{d}")
        branches.append([o.reshape(L, Wh) for o in outs])
    return _attn_grad_combine(branches, tabs, name="dilated_grad_combine")


def _device_step(x, mem, target, w, on_grads=None):
    L = x.shape[0]
    tabs = _rope_tables(L)
    g = {}
    saved = []
    h = x
    for layer in range(2):
        sv = {"h_mix": h}
        if layer == 0:
            proj, n = _rowmm(h, w["ab_w_in"], name="l0_in_proj", gain=w["mix_norm"][0],
                             rope=(2 * D_MODEL, 2 * D_MODEL + 2 * B_HEADS * HEAD_DIM, tabs))
            rec, hs = _lru_fwd(proj, w["lru_conv_w"], w["lru_conv_b"], w["lru_wa"], w["lru_ba"], w["lru_wx"], w["lru_bx"],
                               w["lru_lambda"], name="lru_fwd")
            att, lse = _dilated_fwd(proj)
            mix = jnp.concatenate([rec, att], axis=1)
            (h,) = _rowmm(mix, w["ab_w_out"], name="l0_out_proj", res=h)
            sv.update(hs=hs)
        else:
            proj, n = _rowmm(h, w["c_w_qkv"], name="l1_qkv_proj", gain=w["mix_norm"][1], bias=w["c_b_qkv"],
                             rope=(0, (C_HEADS + C_KV_HEADS) * HEAD_DIM, tabs))
            mix, lse = _band_fwd(proj, proj, proj, d=1, nq=C_HEADS, nkv=C_KV_HEADS, qcol=lambda r: 0, kcol=lambda r: 8,
                                 vcol=lambda r: 9, max_dist=C_WINDOW - 1, sinks=w["c_sinks"], name="swa_fwd")
            (h,) = _rowmm(mix, w["c_w_out"], name="l1_out_proj", res=h, bias=w["c_b_out"])
        sv.update(proj=proj, n_mix=n, mix=mix, lse=lse, h_xa=h)
        xq, nx = _rowmm(h, w["xa_wq"][layer][None], name=f"xa_q_proj{layer}", gain=w["xa_norm"][layer])
        kv, nm = _rowmm(mem, w["xa_wkv"][layer][None], name=f"xa_kv_proj{layer}", gain=w["xa_mem_norm"][layer])
        xo, xlse = _xattn_fwd(xq, kv, name=f"xa_fwd{layer}")
        (h,) = _rowmm(xo, w["xa_wo"][layer], name=f"xa_out_proj{layer}", res=h)
        sv.update(xq=xq, nx=nx, kv=kv, nm=nm, xo=xo, xlse=xlse, h_ffn=h)
        gu, nf, act = _rowmm(h, w["ffn_w_gate_up"], layer=layer, name=f"ffn_in{layer}", gain=w["ffn_norm"][layer], swiglu=True)
        (h,) = _rowmm(act, w["ffn_w_down"][layer][None], name=f"ffn_out{layer}", res=h, tm=512)
        sv.update(gu=gu, nf=nf, act=act)
        saved.append(sv)

    dh, g["final_norm"], loss = _final_loss(h, w["final_norm"], target, name="final_loss")

    stk = {k: [None, None] for k in ("xa_norm", "xa_mem_norm", "ffn_norm", "mix_norm")}
    after = None
    for layer in (1, 0):
        sv = saved[layer]
        (g["ffn_w_down", layer],) = _mm_tn(sv["act"], dh, S=1, name=f"ffn_down_dw{layer}", kk=D_FF // 2)
        (dgu,) = _mm_nt(dh, w["ffn_w_down"][layer][None], name=f"ffn_dact{layer}", mode="swiglu", kchunk=D_FF // 2, gu=sv["gu"],
                        after=after)
        (g["ffn_w_gate_up", layer],) = _mm_tn(sv["nf"], dgu, S=N_CHIPS, name=f"ffn_gu_dw{layer}")
        dh, stk["ffn_norm"][layer] = _mm_nt(dgu, w["ffn_w_gate_up"], layer=layer, name=f"ffn_dx{layer}", mode="norm",
                                            h=sv["h_ffn"], gain=w["ffn_norm"][layer], dh=dh)
        (g["xa_wo", layer],) = _mm_tn(sv["xo"], dh, S=N_CHIPS, name=f"xa_wo_dw{layer}")
        (dxo,) = _mm_nt(dh, w["xa_wo"][layer], name=f"xa_dxo{layer}", mode="plain")
        dxq, dkv = _xattn_bwd(sv["xq"], sv["kv"], sv["xo"], sv["xlse"], dxo, name=f"xa_bwd{layer}")
        (g["xa_wq", layer],) = _mm_tn(sv["nx"], dxq, S=1, name=f"xa_wq_dw{layer}")
        dh, stk["xa_norm"][layer] = _mm_nt(dxq, w["xa_wq"][layer][None], name=f"xa_dx{layer}", mode="norm", h=sv["h_xa"],
                                           gain=w["xa_norm"][layer], dh=dh)
        (g["xa_wkv", layer],) = _mm_tn(sv["nm"], dkv, S=1, name=f"xa_wkv_dw{layer}")
        _, stk["xa_mem_norm"][layer] = _mm_nt(dkv, w["xa_wkv"][layer][None], name=f"xa_dmem{layer}", mode="norm", h=mem,
                                              gain=w["xa_mem_norm"][layer])
        if layer == 1:
            g["c_w_out"], g["c_b_out"] = _mm_tn(sv["mix"], dh, S=1, name="l1_out_dw", bias=True)
            (dmix,) = _mm_nt(dh, w["c_w_out"], name="l1_dmix", mode="plain")
            dq, dk, dv, dsk = _band_bwd(sv["proj"], sv["proj"], sv["proj"], dmix, sv["mix"], sv["lse"], d=1, nq=C_HEADS,
                                        nkv=C_KV_HEADS, qcol=lambda r: 0, kcol=lambda r: 8, vcol=lambda r: 9,
                                        docol=lambda r: 0, max_dist=C_WINDOW - 1, sinks=w["c_sinks"], name="swa_bwd")
            g["c_sinks"] = dsk[0, :C_HEADS]
            dproj = _attn_grad_combine([(dq, dk, dv)], tabs, name="swa_grad_combine")
            g["c_w_qkv"], g["c_b_qkv"] = _mm_tn(sv["n_mix"], dproj, S=1, name="l1_qkv_dw", bias=True)
            dh, stk["mix_norm"][1] = _mm_nt(dproj, w["c_w_qkv"], name="l1_dx", mode="norm", h=sv["h_mix"],
                                            gain=w["mix_norm"][1], dh=dh)
            if on_grads is not None:
                after = on_grads("layer1", g)
        else:
            if on_grads is not None:
                after = on_grads("layer0_ffn_xa", g)
            (g["ab_w_out"],) = _mm_tn(sv["mix"], dh, S=1, name="l0_out_dw", kk=768)
            (dmix,) = _mm_nt(dh, w["ab_w_out"], name="l0_dmix", mode="plain", kchunk=768, after=after)
            (dxy, g["lru_conv_w"], g["lru_conv_b"], g["lru_wa"], g["lru_ba"], g["lru_wx"], g["lru_bx"],
             g["lru_lambda"]) = _lru_bwd(sv["proj"], sv["hs"], dmix, w["lru_conv_w"], w["lru_conv_b"], w["lru_wa"],
                                         w["lru_ba"], w["lru_wx"], w["lru_bx"], w["lru_lambda"], name="lru_bwd")
            dqkv = _dilated_bwd(sv["proj"], sv["mix"][:, D_MODEL:], sv["lse"], dmix[:, D_MODEL:], tabs)
            dproj = jnp.concatenate([dxy, dqkv], axis=1)
            (g["ab_w_in"],) = _mm_tn(sv["n_mix"], dproj, S=N_CHIPS, name="l0_in_dw")
            dh, stk["mix_norm"][0] = _mm_nt(dproj, w["ab_w_in"], name="l0_dx", mode="norm", h=sv["h_mix"],
                                            gain=w["mix_norm"][0], dh=dh)
    for k, v in stk.items():
        g[k] = jnp.concatenate(v, axis=0)
    return loss[0, 0], dh, g


ANY = pl.BlockSpec(memory_space=pl.ANY)
MESH = pl.DeviceIdType.MESH


def _place():
    x, y, c = lax.axis_index("x"), lax.axis_index("y"), lax.axis_index("c")
    return x, y, c, [(1 - x, y), (x, 1 - y), (1 - x, 1 - y)]


def _remote(send_sems, recv_sems):
    def copy(k, src, dst, to):
        return pltpu.make_async_remote_copy(src_ref=src, dst_ref=dst, send_sem=send_sems.at[k], recv_sem=recv_sems.at[k],
                                            device_id=to, device_id_type=MESH)
    return copy


def _halves(ref, n_rows):
    rh = n_rows // 2
    return lambda lead, hh: ref.at[(*lead, pl.ds(hh * rh, rh), slice(None))]


def _gather_weights(packs, spack):
    n = len(packs)

    def body(*refs):
        w_refs, s_ref, wf_refs, sf_ref = refs[:n], refs[n], refs[n + 1:2 * n + 1], refs[2 * n + 1]
        x, y, c, chips = _place()
        me, sib = 2 * x + y, (x, y, 1 - c)
        copy = _remote(*refs[-2:])
        src = [_halves(w_refs[g], packs[g].shape[0]) for g in range(n)]
        dst = [_halves(wf_refs[g], packs[g].shape[0]) for g in range(n)]
        sends = []
        for g in range(n):
            for j, (cx, cy) in enumerate(chips):
                sends.append(copy(3 * g + j, src[g]((), c), dst[g]((me,), c), (cx, cy, c)))
        for j, (cx, cy) in enumerate(chips):
            sends.append(copy(6 * n + j, s_ref, sf_ref.at[me], (cx, cy, c)))
        for cp in sends:
            cp.start()
        for g in range(n):
            for j, (cx, cy) in enumerate(chips):
                got = dst[g]((2 * cx + cy,), c)
                copy(3 * g + j, got, got, sib).wait_recv()
                fwd = copy(3 * n + 3 * g + j, got, got, sib)
                fwd.start()
                sends.append(fwd)
        for g in range(n):
            for j, (cx, cy) in enumerate(chips):
                got = dst[g]((2 * cx + cy,), 1 - c)
                copy(3 * n + 3 * g + j, got, got, sib).wait_recv()
        for j, (cx, cy) in enumerate(chips):
            copy(6 * n + j, s_ref, sf_ref.at[2 * cx + cy], sib).wait_recv()
        for cp in sends:
            cp.wait_send()

    ins = list(packs) + [spack]
    out_shape = [jax.ShapeDtypeStruct((N_CHIPS,) + a.shape, a.dtype) for a in ins]
    n_sems = 6 * n + 3
    outs = pl.pallas_call(body, name="gather_weights", out_shape=out_shape, in_specs=[ANY] * len(ins),
                          out_specs=[ANY] * len(ins),
                          scratch_shapes=[pltpu.SemaphoreType.DMA((n_sems,)), pltpu.SemaphoreType.DMA((n_sems,))])(*ins)
    chip = 2 * lax.axis_index("x") + lax.axis_index("y")
    outs = [lax.dynamic_update_index_in_dim(o, a, chip, 0) for o, a in zip(outs, ins)]
    return outs[:n], outs[n]


SEQUENCER_GATHER_IDS = {"mid": 1, "late": 5}


def _gather_weights_behind(packs, *, tag):
    n = len(packs)

    def body(*refs):
        w_refs, wf_refs = refs[:n], refs[n:2 * n]
        x, y, c, chips = _place()
        me, sib = 2 * x + y, (x, y, 1 - c)
        barrier = pltpu.get_barrier_semaphore()
        for peer in [(cx, cy, c) for cx, cy in chips] + [sib]:
            pl.semaphore_signal(barrier, inc=1, device_id=peer, device_id_type=MESH)
        pl.semaphore_wait(barrier, len(chips) + 1)
        copy = _remote(*refs[-2:])
        src = [_halves(w_refs[g], packs[g].shape[0]) for g in range(n)]
        dst = [_halves(wf_refs[g], packs[g].shape[0]) for g in range(n)]
        sends = []
        for g in range(n):
            for j, (cx, cy) in enumerate(chips):
                sends.append(copy(3 * g + j, src[g]((), c), dst[g]((me,), c), (cx, cy, c)))
        for cp in sends:
            cp.start()
        for g in range(n):
            for j, (cx, cy) in enumerate(chips):
                got = dst[g]((2 * cx + cy,), c)
                copy(3 * g + j, got, got, sib).wait_recv()
                fwd = copy(3 * n + 3 * g + j, got, got, sib)
                fwd.start()
                sends.append(fwd)
        for g in range(n):
            for j, (cx, cy) in enumerate(chips):
                got = dst[g]((2 * cx + cy,), 1 - c)
                copy(3 * n + 3 * g + j, got, got, sib).wait_recv()
        for cp in sends:
            cp.wait_send()

    out_type = [jax.ShapeDtypeStruct((N_CHIPS,) + a.shape, a.dtype) for a in packs]
    outs = pl.kernel(body, out_type=out_type, mesh=plsc.ScalarSubcoreMesh(axis_name="sequencer", num_cores=1),
                     name="gather_weights_behind_" + tag,
                     scratch_types=[pltpu.SemaphoreType.DMA((6 * n,)), pltpu.SemaphoreType.DMA((6 * n,))],
                     compiler_params=pltpu.CompilerParams(collective_id=SEQUENCER_GATHER_IDS[tag]))(*packs)
    chip = 2 * lax.axis_index("x") + lax.axis_index("y")
    return [lax.dynamic_update_index_in_dim(o, a, chip, 0) for o, a in zip(outs, packs)]


def _rs_pair_exchange(gpacks, *, name):
    n = len(gpacks)

    def body(*refs):
        g_refs, ra_refs = refs[:n], refs[n:2 * n]
        x, y, c, _ = _place()
        copy = _remote(*refs[-2:])
        cps = []
        for g in range(n):
            half = _halves(g_refs[g], gpacks[g].shape[1])
            cps += [copy(N_CHIPS * g + j, half((j,), 1 - c), ra_refs[g].at[j], (x, y, 1 - c)) for j in range(N_CHIPS)]
        for cp in cps:
            cp.start()
        for cp in cps:
            cp.wait()

    out_shape = [jax.ShapeDtypeStruct((N_CHIPS, a.shape[1] // 2, a.shape[2]), a.dtype) for a in gpacks]
    n_sems = N_CHIPS * n
    return pl.pallas_call(body, name=name, out_shape=out_shape, in_specs=[ANY] * n, out_specs=[ANY] * n,
                          scratch_shapes=[pltpu.SemaphoreType.DMA((n_sems,)), pltpu.SemaphoreType.DMA((n_sems,))])(*gpacks)


def _row_tile(rows, cap=512):
    return max(t for t in range(16, min(rows, cap) + 1, 16) if rows % t == 0)


def _rs_pair_add(place, gpack, ra, *, name):
    _, R, C = gpack.shape
    Rh = R // 2
    tr = _row_tile(Rh)
    nrb = Rh // tr

    def body(p_ref, g_ref, ra_ref, pair_ref, own_ref):
        s = g_ref[...].astype(F32) + ra_ref[...].astype(F32)
        pair_ref[...] = s.astype(BF16)

        @pl.when(pl.program_id(1) == p_ref[1])
        def _():
            own_ref[...] = s

    grid_spec = pltpu.PrefetchScalarGridSpec(
        num_scalar_prefetch=1, grid=(nrb, N_CHIPS),
        in_specs=[pl.BlockSpec((None, tr, C), lambda i, j, p: (j, p[0] * nrb + i, 0)),
                  pl.BlockSpec((None, tr, C), lambda i, j, p: (j, i, 0))],
        out_specs=[pl.BlockSpec((None, tr, C), lambda i, j, p: (j, i, 0)), pl.BlockSpec((tr, C), lambda i, j, p: (i, 0))])
    return pl.pallas_call(
        body, name=name, grid_spec=grid_spec,
        out_shape=[jax.ShapeDtypeStruct((N_CHIPS, Rh, C), BF16), jax.ShapeDtypeStruct((Rh, C), F32)],
        compiler_params=pltpu.CompilerParams(dimension_semantics=("arbitrary", "arbitrary"),
                                             vmem_limit_bytes=VMEM_LIMIT_V7X))(place, gpack, ra)


SEQUENCER_EXCHANGE_IDS = {"l1": 2, "l0a": 3, "l0b": 4}


def _rs_chip_exchange_behind(pairs, *, tag, small=None):
    n = len(pairs)
    has_small = small is not None

    def body(*refs):
        p_refs = refs[:n]
        s_ref = refs[n] if has_small else None
        rb_refs = refs[n + has_small:2 * n + has_small]
        rs_ref = refs[2 * n + 1] if has_small else None
        x, y, c, chips = _place()
        peers = [(1 - x if k & 4 else x, 1 - y if k & 2 else y, 1 - c if k & 1 else c) for k in range(1, 8)]
        shake = peers if has_small else [(cx, cy, c) for cx, cy in chips]
        barrier = pltpu.get_barrier_semaphore()
        for peer in shake:
            pl.semaphore_signal(barrier, inc=1, device_id=peer, device_id_type=MESH)
        pl.semaphore_wait(barrier, len(shake))
        copy = _remote(*refs[-2:])
        cps = []
        for g in range(n):
            cps += [copy(3 * g + j, p_refs[g].at[2 * cx + cy], rb_refs[g].at[j], (cx, cy, c)) for j, (cx, cy) in enumerate(chips)]
        if has_small:
            dev = 4 * x + 2 * y + c
            cps += [copy(3 * n + k, s_ref, rs_ref.at[dev], peer) for k, peer in enumerate(peers)]
        for cp in cps:
            cp.start()
        for g in range(n):
            for j in range(3):
                copy(3 * g + j, p_refs[g].at[0], rb_refs[g].at[j], (x, y, c)).wait_recv()
        if has_small:
            for k, (px, py, pc) in enumerate(peers):
                copy(3 * n + k, s_ref, rs_ref.at[4 * px + 2 * py + pc], (x, y, c)).wait_recv()
        for cp in cps:
            cp.wait_send()

    ins = list(pairs) + ([small] if has_small else [])
    out_type = [jax.ShapeDtypeStruct((3,) + p.shape[1:], p.dtype) for p in pairs]
    if has_small:
        out_type.append(jax.ShapeDtypeStruct((8,) + small.shape, small.dtype))
    n_sems = 3 * n + 7 * has_small
    outs = pl.kernel(body, out_type=out_type, mesh=plsc.ScalarSubcoreMesh(axis_name="sequencer", num_cores=1),
                     name="rs_chip_exchange_behind_" + tag,
                     scratch_types=[pltpu.SemaphoreType.DMA((n_sems,)), pltpu.SemaphoreType.DMA((n_sems,))],
                     compiler_params=pltpu.CompilerParams(collective_id=SEQUENCER_EXCHANGE_IDS[tag]))(*ins)
    if has_small:
        dev = 4 * lax.axis_index("x") + 2 * lax.axis_index("y") + lax.axis_index("c")
        outs = list(outs[:n]) + [lax.dynamic_update_index_in_dim(outs[n], small, dev, 0)]
    return outs


def _rs_final_add(place, own, rb, *, name):
    Rh, C = own.shape
    tr = _row_tile(Rh)
    nrb = Rh // tr

    def body(p_ref, o_ref, rb_ref, f_ref):
        f_ref[...] = ((o_ref[...] + rb_ref[0].astype(F32)) + rb_ref[1].astype(F32)) + rb_ref[2].astype(F32)

    grid_spec = pltpu.PrefetchScalarGridSpec(
        num_scalar_prefetch=1, grid=(nrb,),
        in_specs=[pl.BlockSpec((tr, C), lambda i, p: (i, 0)), pl.BlockSpec((3, tr, C), lambda i, p: (0, i, 0))],
        out_specs=pl.BlockSpec((tr, C), lambda i, p: (p[0] * nrb + i, 0)))
    return pl.pallas_call(
        body, name=name, grid_spec=grid_spec, out_shape=jax.ShapeDtypeStruct((2 * Rh, C), F32),
        compiler_params=pltpu.CompilerParams(dimension_semantics=("arbitrary",), vmem_limit_bytes=VMEM_LIMIT_V7X))(place, own, rb)


def _sum_slots(rs):
    n, rows, C = rs.shape

    def body(r_ref, o_ref):
        acc = r_ref[0]
        for k in range(1, n):
            acc = acc + r_ref[k]
        o_ref[...] = acc

    return _call(body, name="small_grad_sum", grid=(1,), in_specs=[pl.BlockSpec((n, rows, C), lambda i: (0, 0, 0))],
                 out_specs=pl.BlockSpec((rows, C), lambda i: (0, 0)), out_shape=jax.ShapeDtypeStruct((rows, C), F32),
                 sem=("arbitrary",))(rs)


def _rs_sibling_share(gbufs, *, name):
    n = len(gbufs)

    def body(*refs):
        g_refs = refs[n:2 * n]
        x, y, c, _ = _place()
        copy = _remote(*refs[-2:])
        halves = [_halves(g_refs[g], gbufs[g].shape[0]) for g in range(n)]
        outs = [copy(g, halves[g]((), c), halves[g]((), c), (x, y, 1 - c)) for g in range(n)]
        for cp in outs:
            cp.start()
        for g in range(n):
            copy(g, halves[g]((), 1 - c), halves[g]((), 1 - c), (x, y, c)).wait_recv()
        for cp in outs:
            cp.wait_send()

    return pl.pallas_call(body, name=name, out_shape=[jax.ShapeDtypeStruct(a.shape, a.dtype) for a in gbufs],
                          in_specs=[ANY] * n, out_specs=[ANY] * n, input_output_aliases={g: g for g in range(n)},
                          scratch_shapes=[pltpu.SemaphoreType.DMA((n,)), pltpu.SemaphoreType.DMA((n,))])(*gbufs)


def _adamw(w, g, m, v, *, name, g_row=0):
    rows, cols = w.shape
    tr = rows
    for cand in range(min(rows, 512), 7, -8):
        if rows % cand == 0 and g_row % cand == 0:
            tr = cand
            break
    spec = pl.BlockSpec((tr, cols), lambda i: (i, 0))
    g_spec = pl.BlockSpec((tr, cols), lambda i: (g_row // tr + i, 0))

    def body(w_ref, g_ref, m_ref, v_ref, d_ref, nm_ref, nv_ref):
        gg = g_ref[...]
        nm = ADAM_B1 * m_ref[...] + (1.0 - ADAM_B1) * gg
        nv = ADAM_B2 * v_ref[...] + (1.0 - ADAM_B2) * (gg * gg)
        m_hat = nm / (1.0 - ADAM_B1 ** ADAM_STEP)
        v_hat = nv / (1.0 - ADAM_B2 ** ADAM_STEP)
        d_ref[...] = -ADAM_LR * (m_hat / (jnp.sqrt(v_hat) + ADAM_EPS) + ADAM_WD * w_ref[...])
        nm_ref[...] = nm
        nv_ref[...] = nv

    return _call(body, name=name, grid=(rows // tr,), in_specs=[spec, g_spec, spec, spec], out_specs=[spec] * 3,
                 out_shape=[jax.ShapeDtypeStruct((rows, cols), F32)] * 3, sem=("parallel",))(w, g, m, v)


WEIGHT_NAMES = ("mix_norm", "ab_w_in", "lru_conv_w", "lru_conv_b", "lru_wa", "lru_ba", "lru_wx", "lru_bx", "lru_lambda",
                "ab_w_out", "c_w_qkv", "c_b_qkv", "c_sinks", "c_w_out", "c_b_out", "xa_norm", "xa_mem_norm", "xa_wq",
                "xa_wkv", "xa_wo", "ffn_norm", "ffn_w_gate_up", "ffn_w_down", "final_norm")
EARLY_GROUPS = (("ab_w_in",),)
MID_GROUPS = (("ab_w_out",), ("lru_wa", "lru_wx"))
LATE_GROUPS = (("c_w_out", "xa_wkv", "ffn_w_down"), ("ffn_w_gate_up",), ("xa_wo",), ("xa_wq",), ("c_w_qkv",))
GROUPS = EARLY_GROUPS + MID_GROUPS + LATE_GROUPS
REPLICATED = ("mix_norm", "lru_conv_b", "lru_lambda", "c_sinks", "xa_norm", "xa_mem_norm", "ffn_norm", "final_norm")
SMALL_SHARDED = ("lru_conv_w", "lru_ba", "lru_bx", "c_b_qkv", "c_b_out")
LANES = 1024


def _rows(v):
    flat = v.reshape(-1)
    return jnp.pad(flat, (0, -flat.shape[0] % LANES)).reshape(-1, LANES)


def _pack_small(parts, total, *, name):
    def body(*refs):
        o_ref = refs[-1]
        o_ref[...] = jnp.zeros_like(o_ref)
        row = 0
        for p_ref in refs[:-1]:
            o_ref[row:row + p_ref.shape[0], :] = p_ref[...]
            row += p_ref.shape[0]

    return _call(body, name=name, grid=(1,), in_specs=[pl.BlockSpec(p.shape, lambda i: (0, 0)) for p in parts],
                 out_specs=pl.BlockSpec((total, LANES), lambda i: (0, 0)),
                 out_shape=jax.ShapeDtypeStruct((total, LANES), F32), sem=("arbitrary",))(*parts)


def _from_shards(name, t):
    minor = t.shape[-1]
    if name == "ab_w_in":
        return t
    if name in ("ab_w_out", "c_w_out"):
        return t.reshape(1, -1, minor)
    if name == "ffn_w_gate_up":
        return t.reshape(N_CHIPS, 2, -1, minor)
    if name in ("xa_wq", "xa_wkv", "ffn_w_down"):
        return t.reshape(N_CHIPS, 2, -1, minor).transpose(1, 0, 2, 3).reshape(2, -1, minor)
    if name in ("lru_wa", "lru_wx"):
        return t.reshape(N_CHIPS, LRU_HEADS, -1, minor).transpose(1, 0, 2, 3).reshape(LRU_HEADS, LRU_HEAD_DIM, minor)
    if name == "xa_wo":
        return t.reshape(N_CHIPS, 2, -1, minor).transpose(1, 0, 2, 3)
    assert name == "c_w_qkv"
    return t.transpose(1, 0, 2).reshape(1, D_MODEL, -1)


def _piece_shards(name, g):
    minor = g.shape[-1]
    if name in ("ab_w_in", "ffn_w_gate_up", "xa_wo"):
        return g
    if name in ("ab_w_out", "c_w_out", "xa_wq", "xa_wkv", "ffn_w_down"):
        return g.reshape(N_CHIPS, -1, minor)
    if name in ("lru_wa", "lru_wx"):
        return g.reshape(LRU_HEADS, N_CHIPS, -1, minor).transpose(1, 0, 2, 3).reshape(N_CHIPS, -1, minor)
    assert name == "c_w_qkv"
    return g.reshape(D_MODEL, N_CHIPS, -1).transpose(1, 0, 2)


RS_SETS = {
    "l1": ((("c_w_out", None), ("xa_wkv", 1), ("ffn_w_down", 1)), (("ffn_w_gate_up", 1),), (("xa_wo", 1),),
           (("xa_wq", 1),), (("c_w_qkv", None),)),
    "l0a": ((("xa_wkv", 0), ("ffn_w_down", 0)), (("ffn_w_gate_up", 0),), (("xa_wo", 0),), (("xa_wq", 0),)),
    "l0b": ((("ab_w_out", None),), (("ab_w_in", None),), (("lru_wa", None), ("lru_wx", None))),
}
RS_STAGE = {"layer1": "l1", "layer0_ffn_xa": "l0a"}


def kernel(x, mem, mix_norm, ab_w_in, lru_conv_w, lru_conv_b, lru_wa, lru_ba, lru_wx, lru_bx, lru_lambda, ab_w_out, c_w_qkv, c_b_qkv, c_sinks, c_w_out, c_b_out, xa_norm, xa_mem_norm, xa_wq, xa_wkv, xa_wo, ffn_norm, ffn_w_gate_up, ffn_w_down, final_norm, loss_target, m_mix_norm, m_ab_w_in, m_lru_conv_w, m_lru_conv_b, m_lru_wa, m_lru_ba, m_lru_wx, m_lru_bx, m_lru_lambda, m_ab_w_out, m_c_w_qkv, m_c_b_qkv, m_c_sinks, m_c_w_out, m_c_b_out, m_xa_norm, m_xa_mem_norm, m_xa_wq, m_xa_wkv, m_xa_wo, m_ffn_norm, m_ffn_w_gate_up, m_ffn_w_down, m_final_norm, v_mix_norm, v_ab_w_in, v_lru_conv_w, v_lru_conv_b, v_lru_wa, v_lru_ba, v_lru_wx, v_lru_bx, v_lru_lambda, v_ab_w_out, v_c_w_qkv, v_c_b_qkv, v_c_sinks, v_c_w_out, v_c_b_out, v_xa_norm, v_xa_mem_norm, v_xa_wq, v_xa_wkv, v_xa_wo, v_ffn_norm, v_ffn_w_gate_up, v_ffn_w_down, v_final_norm):
    given = dict(locals())
    wl = {n: given[n] for n in WEIGHT_NAMES}
    ml = {n: given["m_" + n] for n in WEIGHT_NAMES}
    vl = {n: given["v_" + n] for n in WEIGHT_NAMES}
    xi, yi, ci = lax.axis_index("x"), lax.axis_index("y"), lax.axis_index("c")
    chip = 2 * xi + yi

    def join(parts, axis):
        return parts[0] if len(parts) == 1 else jnp.concatenate(parts, axis=axis)

    local_rows = {n: wl[n].size // wl[n].shape[-1] for grp in GROUPS for n in grp}
    packs = [join([wl[n].astype(BF16).reshape(local_rows[n], wl[n].shape[-1]) for n in grp], 0) for grp in GROUPS]
    spack = _pack_small([_rows(wl[n]) for n in SMALL_SHARDED], 8, name="pack_small_weights")
    n_early, n_mid = len(EARLY_GROUPS), len(EARLY_GROUPS) + len(MID_GROUPS)
    early, sfull = _gather_weights(packs[:n_early], spack)
    early, sfull, mid_packs = lax.optimization_barrier((early, sfull, packs[n_early:n_mid]))
    mid = _gather_weights_behind(mid_packs, tag="mid")
    mid, late_packs = lax.optimization_barrier((mid, packs[n_mid:]))
    gathered = early + mid + _gather_weights_behind(late_packs, tag="late")
    w = {n: wl[n] for n in REPLICATED}
    w["c_sinks"] = wl["c_sinks"][0]
    for grp, full in zip(GROUPS, gathered):
        off = 0
        for n in grp:
            w[n] = _from_shards(n, full if len(grp) == 1 else full[:, off:off + local_rows[n]])
            off += local_rows[n]
    for r, n in enumerate(SMALL_SHARDED):
        loc = wl[n].shape[1:]
        t = sfull[:, r, :wl[n].size].reshape((N_CHIPS,) + loc)
        if n == "lru_conv_w":
            w[n] = t.transpose(1, 0, 2).reshape(CONV_WIDTH, -1)
        elif n in ("lru_ba", "lru_bx"):
            w[n] = t.transpose(1, 0, 2).reshape(1, -1)
        else:
            w[n] = t.reshape(1, -1)

    place = jnp.stack([ci, chip]).astype(jnp.int32)

    def pair_stage(spec, g, tag):
        piece = lambda n, l: (g[n] if l is None else g[n, l]).astype(BF16)
        gpacks = [join([_piece_shards(n, piece(n, l)) for n, l in grp], 1) for grp in spec]
        ras = _rs_pair_exchange(gpacks, name=f"rs_pair_exchange_{tag}")
        sums = [_rs_pair_add(place, gp, ra, name=f"rs_pair_add_{tag}_{i}") for i, (gp, ra) in enumerate(zip(gpacks, ras))]
        return [pair for pair, _ in sums], [own for _, own in sums]

    reduced, in_flight = [], []

    def take_up():
        done = [_rs_final_add(place, o, r, name=f"rs_final_add_{len(reduced) + i}") for i, (o, r) in enumerate(in_flight)]
        reduced.extend(done)
        in_flight.clear()
        return done

    def reduce_behind(stage, g):
        done = take_up()
        tag = RS_STAGE[stage]
        pairs, own = pair_stage(RS_SETS[tag], g, tag)
        in_flight.extend(zip(own, _rs_chip_exchange_behind(pairs, tag=tag)))
        return own + done

    loss_part, grad_x, g = _device_step(x[0], mem[0], loss_target[0], w, on_grads=reduce_behind)

    small_parts = [_rows(g[n]) for n in REPLICATED] + [_rows(jnp.broadcast_to(loss_part, (LANES,)))]
    small_parts += [_rows(g[n]) for n in SMALL_SHARDED]
    small = _pack_small(small_parts, 24, name="pack_small_grads")
    take_up()
    pairs, own = pair_stage(RS_SETS["l0b"], g, "l0b")
    *rb, rs = _rs_chip_exchange_behind(pairs, tag="l0b", small=small)
    gsums = list(_rs_sibling_share(list(reduced), name="rs_sibling_share_behind"))
    in_flight.extend(zip(own, rb))
    gsums += list(_rs_sibling_share(take_up(), name="rs_sibling_share_last"))
    ssum = _sum_slots(rs)

    where = {}
    for grp, gsum in zip(RS_SETS["l1"] + RS_SETS["l0a"] + RS_SETS["l0b"], gsums):
        off = 0
        for n, l in grp:
            rows = local_rows[n] if l is None else local_rows[n] // 2
            where[n, l] = (gsum, off, rows, len(grp) == 1)
            off += rows
    take = lambda gsum, off, rows, whole: gsum if whole else gsum[off:off + rows]
    grads, grad_rows = {}, {}
    for grp in LATE_GROUPS + EARLY_GROUPS + MID_GROUPS:
        for n in grp:
            if (n, None) in where:
                grads[n] = take(*where[n, None]).reshape(wl[n].shape)
                grad_rows[n] = where[n, None][:2]
            else:
                grads[n] = jnp.stack([take(*where[n, l]).reshape(wl[n].shape[1:]) for l in range(2)])
                grad_rows[n] = (grads[n].reshape(local_rows[n], wl[n].shape[-1]), 0)
    row = 0
    for n in REPLICATED:
        k = _rows(g[n]).shape[0]
        grads[n] = ssum[row:row + k].reshape(-1)[:wl[n].size].reshape(wl[n].shape)
        row += k
    loss = ssum[row, 0]
    row += 1
    for n in SMALL_SHARDED:
        k = _rows(g[n]).shape[0]
        full = ssum[row:row + k].reshape(-1)[:g[n].size]
        row += k
        loc = wl[n].shape
        if n == "lru_conv_w":
            sh = full.reshape(CONV_WIDTH, N_CHIPS, -1)
        elif n in ("lru_ba", "lru_bx"):
            sh = full.reshape(LRU_HEADS, N_CHIPS, -1)
        else:
            sh = full.reshape(1, N_CHIPS, -1)
        grads[n] = lax.dynamic_index_in_dim(sh, chip, axis=1, keepdims=False).reshape(loc)

    delta, new_m, new_v = {}, {}, {}
    for n, (gsum, off) in grad_rows.items():
        shape2 = (local_rows[n], wl[n].shape[-1])
        d, nm, nv = _adamw(wl[n].reshape(shape2), gsum, ml[n].reshape(shape2), vl[n].reshape(shape2), g_row=off,
                           name="adamw_" + n)
        delta[n], new_m[n], new_v[n] = (t.reshape(wl[n].shape) for t in (d, nm, nv))
    smalls = REPLICATED + SMALL_SHARDED
    packs = [_pack_small([_rows(src[n]) for n in smalls], 24, name="pack_adamw_" + tag)
             for tag, src in (("w", wl), ("g", grads), ("m", ml), ("v", vl))]
    outs = _adamw(*packs, name="adamw_small")
    row = 0
    for n in smalls:
        k = _rows(wl[n]).shape[0]
        for dst, o in zip((delta, new_m, new_v), outs):
            dst[n] = o[row:row + k].reshape(-1)[:wl[n].size].reshape(wl[n].shape)
        row += k

    return (loss, grad_x[None], *[grads[n] for n in WEIGHT_NAMES], *[delta[n] for n in WEIGHT_NAMES],
            *[new_m[n] for n in WEIGHT_NAMES], *[new_v[n] for n in WEIGHT_NAMES])
```

```python
import jax
import jax.numpy as jnp
from jax import lax
from jax.experimental import pallas as pl
from jax.experimental.pallas import tpu as pltpu
from jax.experimental.pallas import tpu_sc as plsc

F32, BF16 = jnp.float32, jnp.bfloat16
D_MODEL = 1024
NORM_EPS = 1e-6
ROPE_THETA = 500000.0
HEAD_DIM = 64
ROT_DIM = 16
BLK = 128
LRU_HEADS, LRU_HEAD_DIM, CONV_WIDTH, LRU_C = 4, 256, 4, 8.0
DILATED_PATTERN = ((128, 1), (512, 4), (2048, 16))
B_HEADS, C_HEADS, C_KV_HEADS, C_WINDOW = 8, 16, 2, 128
XA_HEADS, XA_HEAD_DIM, N_MEM = 4, 128, 256
D_FF = 2816
NEG = -1e30
ADAM_LR, ADAM_B1, ADAM_B2, ADAM_EPS, ADAM_WD, ADAM_STEP = 0.001, 0.9, 0.999, 1e-08, 0.01, 10
N_CHIPS = 4
VMEM_LIMIT_V7X = 56 * 1024 * 1024

NN = (((1,), (0,)), ((), ()))
NT = (((1,), (1,)), ((), ()))
TN = (((0,), (0,)), ((), ()))


def _dot(a, b, dims=NN):
    return lax.dot_general(a, b, dims, preferred_element_type=F32)


def _sigmoid(x):
    return 0.5 * jnp.tanh(0.5 * x) + 0.5


def _call(body, *, name, grid, in_specs, out_specs, out_shape, scratch=(), sem=None):
    return pl.pallas_call(
        body, name=name, grid=grid, in_specs=in_specs, out_specs=out_specs, out_shape=out_shape,
        scratch_shapes=list(scratch),
        compiler_params=pltpu.CompilerParams(dimension_semantics=sem, vmem_limit_bytes=VMEM_LIMIT_V7X))


def _rope_tables(L):
    half = ROT_DIM // 2
    inv = ROPE_THETA ** (-jnp.arange(0, ROT_DIM, 2, dtype=F32) / ROT_DIM)
    j = jnp.arange(2 * HEAD_DIM) % HEAD_DIM
    ang = jnp.arange(L, dtype=F32)[:, None] * inv[j % half][None, :]
    cos, sin = jnp.cos(ang), jnp.sin(ang)
    c = jnp.where(j < ROT_DIM, cos, 1.0)
    s1 = jnp.where(j < half, -sin, 0.0)
    s2 = jnp.where((j >= half) & (j < ROT_DIM), sin, 0.0)
    return c, s1, s2


def _rope_fwd(v, c, s1, s2):
    return v * c + pltpu.roll(v, 120, 1) * s1 + pltpu.roll(v, 8, 1) * s2


def _rope_bwd(dv, c, s1, s2):
    return dv * c + pltpu.roll(dv * s1, 8, 1) + pltpu.roll(dv * s2, 120, 1)


def _weight_spec(w, layer):
    once = pl.Buffered(1)
    if layer is None:
        return w.shape, pl.BlockSpec(w.shape, lambda i: (0, 0, 0), pipeline_mode=once)
    S, _, K, Ns = w.shape
    return (S, K, Ns), pl.BlockSpec((S, None, K, Ns), lambda i: (0, layer, 0, 0), pipeline_mode=once)


def _rowmm(a, w3, *, name, tm=512, gain=None, bias=None, res=None, swiglu=False, rope=None, layer=None):
    M, K = a.shape
    (S, _, Ns), w_spec = _weight_spec(w3, layer)
    N = S * Ns
    tm = min(tm, M)
    has_norm, has_bias, has_res, has_rope = gain is not None, bias is not None, res is not None, rope is not None
    row = lambda w: pl.BlockSpec((tm, w), lambda i: (i, 0))
    whole = lambda shape: pl.BlockSpec(shape, lambda i: (0,) * len(shape))
    ins, specs = [a], [row(K)]
    if has_norm:
        ins.append(gain.reshape(1, K)); specs.append(whole((1, K)))
    ins.append(w3); specs.append(w_spec)
    if has_bias:
        ins.append(bias.reshape(1, N)); specs.append(whole((1, N)))
    if has_res:
        ins.append(res); specs.append(row(N))
    if has_rope:
        ins += list(rope[2]); specs += [row(128)] * 3
    y_dtype = F32 if has_res else BF16
    out_shape, out_specs = [jax.ShapeDtypeStruct((M, N), y_dtype)], [row(N)]
    if has_norm:
        out_shape.append(jax.ShapeDtypeStruct((M, K), BF16)); out_specs.append(row(K))
    if swiglu:
        out_shape.append(jax.ShapeDtypeStruct((M, N // 2), BF16)); out_specs.append(row(N // 2))
    scratch = [pltpu.VMEM((tm, N), F32)] if has_rope else []

    def body(*refs):
        it = iter(refs)
        a_ref = next(it)
        g_ref = next(it) if has_norm else None
        w_ref = next(it)
        b_ref = next(it) if has_bias else None
        r_ref = next(it) if has_res else None
        tabs = [next(it) for _ in range(3)] if has_rope else None
        y_ref = next(it)
        n_ref = next(it) if has_norm else None
        act_ref = next(it) if swiglu else None
        ys_ref = next(it) if has_rope else None
        if has_norm:
            x = a_ref[...].astype(F32)
            ms = jnp.mean(x * x, axis=-1, keepdims=True)
            xb = (x * lax.rsqrt(ms + NORM_EPS) * g_ref[...]).astype(BF16)
            n_ref[...] = xb
        else:
            xb = a_ref[...].astype(BF16)
        if swiglu:
            for s in range(S // 2):
                g = _dot(xb, w_ref[s])
                u = _dot(xb, w_ref[s + S // 2])
                y_ref[:, s * Ns:(s + 1) * Ns] = g.astype(BF16)
                y_ref[:, N // 2 + s * Ns:N // 2 + (s + 1) * Ns] = u.astype(BF16)
                act_ref[:, s * Ns:(s + 1) * Ns] = (g * _sigmoid(g) * u).astype(BF16)
            return
        for s in range(S):
            sl = slice(s * Ns, (s + 1) * Ns)
            acc = _dot(xb, w_ref[s])
            if has_bias:
                acc = acc + b_ref[:, sl]
            if has_res:
                acc = acc + r_ref[:, sl]
            if has_rope:
                ys_ref[:, sl] = acc
            else:
                y_ref[:, sl] = acc.astype(y_dtype)
        if has_rope:
            c, s1, s2 = (t[...] for t in tabs)
            for cb in range(N // 128):
                cs = slice(cb * 128, (cb + 1) * 128)
                v = ys_ref[:, cs]
                if rope[0] <= cb * 128 < rope[1]:
                    v = _rope_fwd(v, c, s1, s2)
                y_ref[:, cs] = v.astype(BF16)

    return _call(body, name=name, grid=(M // tm,), in_specs=specs, out_specs=out_specs, out_shape=out_shape,
                 scratch=scratch, sem=("parallel",))(*ins)


def _mm_nt(dy, w3, *, name, mode, tm=512, kchunk=None, h=None, gain=None, dh=None, gu=None, layer=None, after=None):
    M, N = dy.shape
    (S, K, Ns), w_spec = _weight_spec(w3, layer)
    kchunk = kchunk or K
    tm = min(tm, M)
    row = lambda w: pl.BlockSpec((tm, w), lambda i: (i, 0))
    whole = lambda shape: pl.BlockSpec(shape, lambda i: (0,) * len(shape))
    ins, specs = [dy, w3], [row(N), w_spec]
    after = list(after or ())
    ins = after + ins
    specs = [pl.BlockSpec((8, a.shape[1]), lambda i: (0, 0)) for a in after] + specs
    has_dh = dh is not None
    if mode == "norm":
        ins += [h, gain.reshape(1, K)]; specs += [row(K), whole((1, K))]
        if has_dh:
            ins.append(dh); specs.append(row(K))
        out_shape = [jax.ShapeDtypeStruct((M, K), F32), jax.ShapeDtypeStruct((1, K), F32)]
        out_specs = [row(K), whole((1, K))]
    elif mode == "swiglu":
        ins.append(gu); specs.append(row(2 * K))
        out_shape, out_specs = [jax.ShapeDtypeStruct((M, 2 * K), BF16)], [row(2 * K)]
    else:
        out_shape, out_specs = [jax.ShapeDtypeStruct((M, K), BF16)], [row(K)]

    def body(*refs):
        it = iter(refs[len(after):])
        dy_ref, w_ref = next(it), next(it)
        if mode == "norm":
            h_ref, g_ref = next(it), next(it)
            dh_ref = next(it) if has_dh else None
            o_ref, dg_ref = next(it), next(it)
        elif mode == "swiglu":
            gu_ref, o_ref = next(it), next(it)
        else:
            o_ref = next(it)
        for kc in range(K // kchunk):
            ks = slice(kc * kchunk, (kc + 1) * kchunk)
            acc = None
            for s in range(S):
                t = _dot(dy_ref[:, s * Ns:(s + 1) * Ns].astype(BF16), w_ref[s, ks, :], NT)
                acc = t if acc is None else acc + t
            if mode == "plain":
                o_ref[:, ks] = acc.astype(BF16)
            elif mode == "swiglu":
                us = slice(K + kc * kchunk, K + (kc + 1) * kchunk)
                g = gu_ref[:, ks].astype(F32)
                u = gu_ref[:, us].astype(F32)
                sg = _sigmoid(g)
                o_ref[:, ks] = (acc * u * (sg * (1.0 + g * (1.0 - sg)))).astype(BF16)
                o_ref[:, us] = (acc * (g * sg)).astype(BF16)
            else:
                x = h_ref[...].astype(F32)
                r = lax.rsqrt(jnp.mean(x * x, axis=-1, keepdims=True) + NORM_EPS)
                xhat = x * r
                dxh = acc * g_ref[...]
                dx = r * (dxh - xhat * jnp.mean(dxh * xhat, axis=-1, keepdims=True))
                o_ref[...] = dx + dh_ref[...] if has_dh else dx

                @pl.when(pl.program_id(0) == 0)
                def _():
                    dg_ref[...] = jnp.zeros_like(dg_ref)

                dg_ref[...] += jnp.sum(acc * xhat, axis=0, keepdims=True)

    sem = ("arbitrary",) if mode == "norm" else ("parallel",)
    return _call(body, name=name, grid=(M // tm,), in_specs=specs, out_specs=out_specs, out_shape=out_shape, sem=sem)(*ins)


def _mm_tn(x, dy, *, S, name, tk=2048, kk=None, bias=False):
    M, K = x.shape
    N = dy.shape[1]
    Ns = N // S
    kk = kk or K
    tk = min(tk, M)
    nl = M // tk
    in_specs = [pl.BlockSpec((tk, kk), lambda s, kc, l: (l, kc)), pl.BlockSpec((tk, Ns), lambda s, kc, l: (l, s))]
    out_shape = [jax.ShapeDtypeStruct((S, K, Ns), BF16)]
    out_specs = [pl.BlockSpec((None, kk, Ns), lambda s, kc, l: (s, kc, 0))]
    if bias:
        out_shape.append(jax.ShapeDtypeStruct((1, N), F32))
        out_specs.append(pl.BlockSpec((1, Ns), lambda s, kc, l: (0, s)))

    def body(x_ref, dy_ref, o_ref, *rest):
        acc_ref = rest[-1]
        kc, l = pl.program_id(1), pl.program_id(2)

        @pl.when(l == 0)
        def _():
            acc_ref[...] = jnp.zeros_like(acc_ref)

        acc_ref[...] += _dot(x_ref[...].astype(BF16), dy_ref[...].astype(BF16), TN)
        if bias:
            b_ref = rest[0]

            @pl.when((kc == 0) & (l == 0))
            def _():
                b_ref[...] = jnp.zeros_like(b_ref)

            @pl.when(kc == 0)
            def _():
                b_ref[...] += jnp.sum(dy_ref[...].astype(F32), axis=0, keepdims=True)

        @pl.when(l == nl - 1)
        def _():
            o_ref[...] = acc_ref[...].astype(BF16)

    return _call(body, name=name, grid=(S, K // kk, nl), in_specs=in_specs, out_specs=out_specs, out_shape=out_shape,
                 scratch=[pltpu.VMEM((kk, Ns), F32)], sem=("arbitrary", "arbitrary", "arbitrary"))(x, dy)


def _band_bias(max_dist, has_prev):
    rows = lax.broadcasted_iota(jnp.int32, (BLK, 2 * BLK), 0)
    cols = lax.broadcasted_iota(jnp.int32, (BLK, 2 * BLK), 1)
    dist = rows - cols + BLK
    ok = (dist >= 0) & (dist <= max_dist) & ((cols >= BLK) | has_prev)
    return jnp.where(ok, 0.0, NEG)


Q_SCALE = HEAD_DIM ** -0.5
BNT = (((2,), (2,)), ((0,), (0,)))
BNN = (((2,), (1,)), ((0,), (0,)))
BTN = (((1,), (1,)), ((0,), (0,)))


def _bdot(a, b, dims):
    return lax.dot_general(a, b, dims, preferred_element_type=F32)


def _split_heads(x, n, width=HEAD_DIM, group=1):
    return jnp.stack([x[:, (h // group) * width:(h // group + 1) * width] for h in range(n)], axis=0)


def _band_fwd(qa, ka, va, *, d, nq, nkv, qcol, kcol, vcol, max_dist, sinks=None, name):
    Lr = qa.shape[0]
    nb = Lr // BLK
    qw, kw, G = nq * HEAD_DIM, nkv * HEAD_DIM, nq // nkv
    cur = lambda colf, w: pl.BlockSpec((BLK, w), lambda r, i: (i, colf(r)))
    prv = lambda colf, w: pl.BlockSpec((BLK, w), lambda r, i: (jnp.maximum(i - 1, 0), colf(r)))
    out = pl.BlockSpec((BLK, qw), lambda r, i: (i, r))
    ins, specs = [qa, ka, ka, va, va], [cur(qcol, qw), cur(kcol, kw), prv(kcol, kw), cur(vcol, kw), prv(vcol, kw)]
    has_sinks = sinks is not None
    if has_sinks:
        ins.append(sinks); specs.append(pl.BlockSpec(memory_space=pltpu.SMEM))

    def body(*refs):
        q_ref, kc_ref, kp_ref, vc_ref, vp_ref = refs[:5]
        sk_ref = refs[5] if has_sinks else None
        o_ref, lse_ref = refs[-2], refs[-1]
        bias = _band_bias(max_dist, pl.program_id(1) > 0)
        k2 = jnp.concatenate([kp_ref[...], kc_ref[...]], axis=0)
        v2 = jnp.concatenate([vp_ref[...], vc_ref[...]], axis=0)
        if G == 1 and not has_sinks:
            q3 = _split_heads(q_ref[...], nq) * jnp.asarray(Q_SCALE, BF16)
            s = _bdot(q3, _split_heads(k2, nq), BNT) + bias
            m = jnp.max(s, axis=-1, keepdims=True)
            p = jnp.exp(s - m)
            l = jnp.sum(p, axis=-1, keepdims=True)
            o = (_bdot(p.astype(BF16), _split_heads(v2, nq), BNN) / l).astype(BF16)
            lse = m + jnp.log(l)
            for h in range(nq):
                hs = slice(h * HEAD_DIM, (h + 1) * HEAD_DIM)
                o_ref[:, hs] = o[h]
                lse_ref[:, hs] = jnp.broadcast_to(lse[h], (BLK, HEAD_DIM))
            return
        for h in range(nq):
            hs = slice(h * HEAD_DIM, (h + 1) * HEAD_DIM)
            ks = slice((h // G) * HEAD_DIM, (h // G + 1) * HEAD_DIM)
            s = _dot(q_ref[:, hs] * jnp.asarray(Q_SCALE, BF16), k2[:, ks], NT) + bias
            m = jnp.max(s, axis=-1, keepdims=True)
            if has_sinks:
                m = jnp.maximum(m, sk_ref[h])
            p = jnp.exp(s - m)
            l = jnp.sum(p, axis=-1, keepdims=True)
            if has_sinks:
                l = l + jnp.exp(sk_ref[h] - m)
            o_ref[:, hs] = (_dot(p.astype(BF16), v2[:, ks]) / l).astype(BF16)
            lse_ref[:, hs] = jnp.broadcast_to(m + jnp.log(l), (BLK, HEAD_DIM))

    return _call(body, name=name, grid=(d, nb), in_specs=specs, out_specs=[out, out],
                 out_shape=[jax.ShapeDtypeStruct((Lr, d * qw), BF16), jax.ShapeDtypeStruct((Lr, d * qw), F32)],
                 sem=("parallel", "parallel"))(*ins)


def _band_bwd(qa, ka, va, doa, oa, lsea, *, d, nq, nkv, qcol, kcol, vcol, docol, max_dist, sinks=None, name):
    Lr = qa.shape[0]
    nb = Lr // BLK
    qw, kw, G = nq * HEAD_DIM, nkv * HEAD_DIM, nq // nkv
    transposed = G > 1
    last = lambda i: jnp.minimum(i, nb - 1)
    cur = lambda colf, w: pl.BlockSpec((BLK, w), lambda r, i: (last(i), colf(r)))
    prv = lambda colf, w: pl.BlockSpec((BLK, w), lambda r, i: (jnp.maximum(last(i) - 1, 0), colf(r)))
    own = lambda r: r
    ins = [qa, ka, ka, va, va, doa, oa, lsea]
    specs = [cur(qcol, qw), cur(kcol, kw), prv(kcol, kw), cur(vcol, kw), prv(vcol, kw), cur(docol, qw), cur(own, qw),
             cur(own, qw)]
    has_sinks = sinks is not None
    if has_sinks:
        ins.append(sinks); specs.append(pl.BlockSpec(memory_space=pltpu.SMEM))
    out_shape = [jax.ShapeDtypeStruct((Lr, d * qw), BF16), jax.ShapeDtypeStruct((Lr, d * kw), BF16),
                 jax.ShapeDtypeStruct((Lr, d * kw), BF16)]
    behind = lambda r, i: (jnp.maximum(i - 1, 0), r)
    out_specs = [pl.BlockSpec((BLK, qw), lambda r, i: (last(i), r)), pl.BlockSpec((BLK, kw), behind),
                 pl.BlockSpec((BLK, kw), behind)]
    if has_sinks:
        out_shape.append(jax.ShapeDtypeStruct((8, 128), F32))
        out_specs.append(pl.BlockSpec((8, 128), lambda r, i: (0, 0)))

    def body(*refs):
        it = iter(refs)
        q_ref, kc_ref, kp_ref, vc_ref, vp_ref, do_ref, o_ref, ls_ref = (next(it) for _ in range(8))
        sk_ref = next(it) if has_sinks else None
        dq_ref, dk_ref, dv_ref = next(it), next(it), next(it)
        dsk_ref = next(it) if has_sinks else None
        dk_car, dv_car = next(it), next(it)
        r_id, i = pl.program_id(0), pl.program_id(1)

        @pl.when(i == 0)
        def _():
            dk_car[...] = jnp.zeros_like(dk_car)
            dv_car[...] = jnp.zeros_like(dv_car)

        if has_sinks:
            @pl.when((r_id == 0) & (i == 0))
            def _():
                dsk_ref[...] = jnp.zeros_like(dsk_ref)

        @pl.when(i == nb)
        def _():
            dk_ref[...] = dk_car[...].astype(BF16)
            dv_ref[...] = dv_car[...].astype(BF16)

        @pl.when(i < nb)
        def _():
            bias = jnp.tile(_band_bias(max_dist, i > 0), (G, 1))
            k2 = jnp.concatenate([kp_ref[...], kc_ref[...]], axis=0)
            v2 = jnp.concatenate([vp_ref[...], vc_ref[...]], axis=0)
            if G == 1 and not has_sinks:
                scale = jnp.asarray(Q_SCALE, BF16)
                q3, do3 = _split_heads(q_ref[...], nq) * scale, _split_heads(do_ref[...], nq)
                k3, v3 = _split_heads(k2, nq), _split_heads(v2, nq)
                lse = jnp.stack([ls_ref[:, h * HEAD_DIM:h * HEAD_DIM + 1] for h in range(nq)], axis=0)
                dl = jnp.sum(do3.astype(F32) * _split_heads(o_ref[...], nq).astype(F32), axis=-1, keepdims=True)
                p = jnp.exp(_bdot(q3, k3, BNT) + bias - lse)
                ds = (p * (_bdot(do3, v3, BNT) - dl)).astype(BF16)
                dq = (_bdot(ds, k3, BNN) * Q_SCALE).astype(BF16)
                dk, dv = _bdot(ds, q3, BTN), _bdot(p.astype(BF16), do3, BTN)
                for h in range(nq):
                    hs = slice(h * HEAD_DIM, (h + 1) * HEAD_DIM)
                    dq_ref[:, hs] = dq[h]
                    dk_ref[:, hs] = (dk_car[:, hs] + dk[h, :BLK]).astype(BF16)
                    dv_ref[:, hs] = (dv_car[:, hs] + dv[h, :BLK]).astype(BF16)
                    dk_car[:, hs] = dk[h, BLK:]
                    dv_car[:, hs] = dv[h, BLK:]
                return
            if has_sinks:
                lane = lax.broadcasted_iota(jnp.int32, (8, 128), 1)
                dsk = jnp.zeros((8, 128), F32)
            for kv in range(nkv):
                heads = [slice((kv * G + g) * HEAD_DIM, (kv * G + g + 1) * HEAD_DIM) for g in range(G)]
                stack = lambda ref: jnp.concatenate([ref[:, hs] for hs in heads], axis=0)
                ks = slice(kv * HEAD_DIM, (kv + 1) * HEAD_DIM)
                kh, vh = k2[:, ks], v2[:, ks]
                q = stack(q_ref) * jnp.asarray(Q_SCALE, BF16)
                do = stack(do_ref)
                lse = jnp.concatenate([ls_ref[:, hs.start:hs.start + 1] for hs in heads], axis=0)
                dl = jnp.sum(do.astype(F32) * stack(o_ref).astype(F32), axis=-1, keepdims=True)
                p = jnp.exp(_dot(q, kh, NT) + bias - lse)
                ds = (p * (_dot(do, vh, NT) - dl)).astype(BF16)
                dq = (_dot(ds, kh) * Q_SCALE).astype(BF16)
                for g, hs in enumerate(heads):
                    dq_ref[:, hs] = dq[g * BLK:(g + 1) * BLK]
                if transposed:
                    dk, dv = _dot(q, ds, TN).T, _dot(do, p.astype(BF16), TN).T
                else:
                    dk, dv = _dot(ds, q, TN), _dot(p.astype(BF16), do, TN)
                if has_sinks:
                    sk = jnp.concatenate([jnp.full((BLK, 1), sk_ref[kv * G + g], F32) for g in range(G)], axis=0)
                    lost = jnp.exp(sk - lse) * dl
                    for g in range(G):
                        val = -jnp.sum(lost[g * BLK:(g + 1) * BLK], axis=0, keepdims=True)
                        dsk = dsk + jnp.where(lane == kv * G + g, val, 0.0)
                dk_ref[:, ks] = (dk_car[:, ks] + dk[:BLK]).astype(BF16)
                dv_ref[:, ks] = (dv_car[:, ks] + dv[:BLK]).astype(BF16)
                dk_car[:, ks] = dk[BLK:]
                dv_car[:, ks] = dv[BLK:]
            if has_sinks:
                dsk_ref[...] += dsk

    return _call(body, name=name, grid=(d, nb + 1), in_specs=specs, out_specs=out_specs, out_shape=out_shape,
                 scratch=[pltpu.VMEM((BLK, kw), F32), pltpu.VMEM((BLK, kw), F32)], sem=("arbitrary", "arbitrary"))(*ins)


def _attn_grad_combine(branches, tabs, *, name, tm=512):
    L, qw = branches[0][0].shape
    kw = branches[0][1].shape[1]
    nbr = len(branches)
    row = lambda w: pl.BlockSpec((tm, w), lambda i: (i, 0))
    ins, specs = [], []
    for dq, dk, dv in branches:
        ins += [dq, dk, dv]; specs += [row(qw), row(kw), row(kw)]
    ins += list(tabs); specs += [row(128)] * 3

    def body(*refs):
        c, s1, s2 = (t[...] for t in refs[3 * nbr:3 * nbr + 3])
        o_ref = refs[-1]
        for part, (w, off, rot) in enumerate(((qw, 0, True), (kw, qw, True), (kw, qw + kw, False))):
            for cb in range(w // 128):
                cs = slice(cb * 128, (cb + 1) * 128)
                v = refs[part][:, cs].astype(F32)
                for b in range(1, nbr):
                    v = v + refs[3 * b + part][:, cs].astype(F32)
                if rot:
                    v = _rope_bwd(v, c, s1, s2)
                o_ref[:, off + cb * 128:off + (cb + 1) * 128] = v.astype(BF16)

    return _call(body, name=name, grid=(L // tm,), in_specs=specs, out_specs=row(qw + 2 * kw),
                 out_shape=jax.ShapeDtypeStruct((L, qw + 2 * kw), BF16), sem=("parallel",))(*ins)


def _xattn_fwd(q, kv, *, name, tq=1024):
    L, W = q.shape
    scale = XA_HEAD_DIM ** -0.5
    row = pl.BlockSpec((tq, W), lambda i: (i, 0))
    kvs = pl.BlockSpec((N_MEM, 2 * W), lambda i: (0, 0))

    def body(q_ref, kv_ref, o_ref, lse_ref):
        kv = kv_ref[...]
        k3, v3 = _split_heads(kv[:, :W], XA_HEADS, XA_HEAD_DIM), _split_heads(kv[:, W:], XA_HEADS, XA_HEAD_DIM)
        s = _bdot(_split_heads(q_ref[...], XA_HEADS, XA_HEAD_DIM), k3, BNT) * scale
        m = jnp.max(s, axis=-1, keepdims=True)
        p = jnp.exp(s - m)
        l = jnp.sum(p, axis=-1, keepdims=True)
        o = (_bdot(p.astype(BF16), v3, BNN) / l).astype(BF16)
        lse = m + jnp.log(l)
        for h in range(XA_HEADS):
            hs = slice(h * XA_HEAD_DIM, (h + 1) * XA_HEAD_DIM)
            o_ref[:, hs] = o[h]
            lse_ref[:, hs] = jnp.broadcast_to(lse[h], (tq, XA_HEAD_DIM))

    return _call(body, name=name, grid=(L // tq,), in_specs=[row, kvs], out_specs=[row, row],
                 out_shape=[jax.ShapeDtypeStruct((L, W), BF16), jax.ShapeDtypeStruct((L, W), F32)], sem=("parallel",))(q, kv)


def _xattn_bwd(q, kv, o, lse, do, *, name, tq=1024):
    L, W = q.shape
    scale = XA_HEAD_DIM ** -0.5
    row = pl.BlockSpec((tq, W), lambda i: (i, 0))
    kvs = pl.BlockSpec((N_MEM, 2 * W), lambda i: (0, 0))

    def body(q_ref, kv_ref, o_ref, lse_ref, do_ref, dq_ref, dkv_ref):
        @pl.when(pl.program_id(0) == 0)
        def _():
            dkv_ref[...] = jnp.zeros_like(dkv_ref)

        split = lambda x: _split_heads(x, XA_HEADS, XA_HEAD_DIM)
        kv = kv_ref[...]
        q3, k3, v3, do3 = split(q_ref[...]), split(kv[:, :W]), split(kv[:, W:]), split(do_ref[...])
        lse = jnp.stack([lse_ref[:, h * XA_HEAD_DIM:h * XA_HEAD_DIM + 1] for h in range(XA_HEADS)], axis=0)
        p = jnp.exp(_bdot(q3, k3, BNT) * scale - lse)
        dl = jnp.sum(do3.astype(F32) * split(o_ref[...]).astype(F32), axis=-1, keepdims=True)
        ds = (p * (_bdot(do3, v3, BNT) - dl) * scale).astype(BF16)
        dq = _bdot(ds, k3, BNN).astype(BF16)
        dk, dv = _bdot(ds, q3, BTN), _bdot(p.astype(BF16), do3, BTN)
        for h in range(XA_HEADS):
            hs = slice(h * XA_HEAD_DIM, (h + 1) * XA_HEAD_DIM)
            vs = slice(W + h * XA_HEAD_DIM, W + (h + 1) * XA_HEAD_DIM)
            dq_ref[:, hs] = dq[h]
            dkv_ref[:, hs] += dk[h]
            dkv_ref[:, vs] += dv[h]

    return _call(body, name=name, grid=(L // tq,), in_specs=[row, kvs, row, row, row], out_specs=[row, kvs],
                 out_shape=[jax.ShapeDtypeStruct((L, W), BF16), jax.ShapeDtypeStruct((N_MEM, 2 * W), F32)],
                 sem=("arbitrary",))(q, kv, o, lse, do)


def _neg_expm1(z):
    series = -(z * (1.0 + z * (0.5 + z * (1.0 / 6.0 + z * (1.0 / 24.0 + z * (1.0 / 120.0))))))
    return jnp.where(z > -0.05, series, 1.0 - jnp.exp(z))


def _softplus(z):
    return jnp.maximum(z, 0.0) + jnp.log(1.0 + jnp.exp(-jnp.abs(z)))


def _gelu_parts(y):
    c = 0.7978845608028654
    t = jnp.tanh(c * (y + 0.044715 * y * y * y))
    gy = 0.5 * y * (1.0 + t)
    dgy = 0.5 * (1.0 + t) + 0.5 * y * (1.0 - t * t) * c * (1.0 + 3.0 * 0.044715 * y * y)
    return gy, dgy


def _lru_gates(xc, wa_ref, ba, wx_ref, bx, sp):
    rs, igs = [], []
    for hd in range(LRU_HEADS):
        sl = slice(hd * LRU_HEAD_DIM, (hd + 1) * LRU_HEAD_DIM)
        xh = xc[:, sl].astype(BF16)
        rs.append(_sigmoid(_dot(xh, wa_ref[hd]) + ba[:, sl]))
        igs.append(_sigmoid(_dot(xh, wx_ref[hd]) + bx[:, sl]))
    r, ig = jnp.concatenate(rs, axis=1), jnp.concatenate(igs, axis=1)
    la = -LRU_C * r * sp
    return r, ig, jnp.exp(la), _neg_expm1(2.0 * la)


def _conv_taps(x_ext, halo):
    n = x_ext.shape[0]
    return [x_ext[halo:] if k == CONV_WIDTH - 1 else pltpu.roll(x_ext, CONV_WIDTH - 1 - k, 0)[halo:]
            for k in range(CONV_WIDTH)]


def _lru_fwd(proj, cw, cb, wa, ba, wx, bx, lam, *, name, tc=512):
    L = proj.shape[0]
    W = LRU_HEADS * LRU_HEAD_DIM
    nb = L // tc
    whole = lambda shape: pl.BlockSpec(shape, lambda i: (0,) * len(shape))
    specs = [pl.BlockSpec((tc, W), lambda i: (i, 0)), pl.BlockSpec((tc, W), lambda i: (i, 1)),
             pl.BlockSpec((16, W), lambda i: (jnp.maximum(i * (tc // 16) - 1, 0), 0)),
             whole((CONV_WIDTH, W)), whole((1, W)), whole((LRU_HEADS, LRU_HEAD_DIM, LRU_HEAD_DIM)), whole((1, W)),
             whole((LRU_HEADS, LRU_HEAD_DIM, LRU_HEAD_DIM)), whole((1, W)), whole((1, W))]
    out_specs = [pl.BlockSpec((tc, W), lambda i: (i, 0))] * 2
    out_shape = [jax.ShapeDtypeStruct((L, W), BF16), jax.ShapeDtypeStruct((L, W), F32)]

    def body(x_ref, y_ref, xh_ref, cw_ref, cb_ref, wa_ref, ba_ref, wx_ref, bx_ref, lam_ref, rec_ref, hs_ref,
             hcar, a_scr, b_scr):
        i = pl.program_id(0)

        @pl.when(i == 0)
        def _():
            hcar[...] = jnp.zeros_like(hcar)

        halo = jnp.where(i > 0, xh_ref[...].astype(F32), 0.0)
        taps = _conv_taps(jnp.concatenate([halo, x_ref[...].astype(F32)], axis=0), 16)
        xc = cb_ref[...] + sum(cw_ref[k:k + 1, :] * taps[k] for k in range(CONV_WIDTH))
        _, ig, a, om = _lru_gates(xc, wa_ref, ba_ref[...], wx_ref, bx_ref[...], _softplus(-lam_ref[...]))
        b = jnp.sqrt(om) * (ig * xc)
        rowmod = lax.broadcasted_iota(jnp.int32, (tc, W), 0) & 7
        for s in (1, 2, 4):
            keep = rowmod >= s
            b = jnp.where(keep, a * pltpu.roll(b, s, 0) + b, b)
            a = jnp.where(keep, a * pltpu.roll(a, s, 0), a)
        a_scr[...] = a
        b_scr[...] = b

        def tile(j, hc):
            rows = pl.ds(pl.multiple_of(j * 8, 8), 8)
            ht = a_scr[rows, :] * hc + b_scr[rows, :]
            hs_ref[rows, :] = ht
            return jnp.broadcast_to(ht[7:8, :], (8, W))

        hcar[...] = lax.fori_loop(0, tc // 8, tile, hcar[...])
        gy, _ = _gelu_parts(y_ref[...].astype(F32))
        rec_ref[...] = (hs_ref[...] * gy).astype(BF16)

    return _call(body, name=name, grid=(nb,), in_specs=specs, out_specs=out_specs, out_shape=out_shape,
                 scratch=[pltpu.VMEM((8, W), F32), pltpu.VMEM((tc, W), F32), pltpu.VMEM((tc, W), F32)],
                 sem=("arbitrary",))(proj, proj, proj, cw, cb, wa, ba, wx, bx, lam)


def _lru_bwd(proj, hs, drec_src, cw, cb, wa, ba, wx, bx, lam, *, name, tc=256):
    L = proj.shape[0]
    W = LRU_HEADS * LRU_HEAD_DIM
    nb = L // tc
    tb = lambda i: nb - 1 - i
    whole = lambda shape: pl.BlockSpec(shape, lambda i: (0,) * len(shape))
    gate_w = (LRU_HEADS, LRU_HEAD_DIM, LRU_HEAD_DIM)
    specs = [pl.BlockSpec((tc, W), lambda i: (tb(i), 0)), pl.BlockSpec((tc, W), lambda i: (tb(i), 1)),
             pl.BlockSpec((16, W), lambda i: (jnp.maximum(tb(i) * (tc // 16) - 1, 0), 0)),
             pl.BlockSpec((tc, W), lambda i: (tb(i), 0)),
             pl.BlockSpec((8, W), lambda i: (jnp.maximum(tb(i) * (tc // 8) - 1, 0), 0)),
             pl.BlockSpec((tc, W), lambda i: (tb(i), 0)),
             whole((CONV_WIDTH, W)), whole((1, W)), whole(gate_w), whole((1, W)), whole(gate_w), whole((1, W)), whole((1, W))]
    out_specs = [pl.BlockSpec((tc, 2 * W), lambda i: (tb(i), 0)), whole((CONV_WIDTH, W)), whole((1, W)), whole(gate_w),
                 whole((1, W)), whole(gate_w), whole((1, W)), whole((1, W))]
    vec = jax.ShapeDtypeStruct((1, W), F32)
    out_shape = [jax.ShapeDtypeStruct((L, 2 * W), BF16), jax.ShapeDtypeStruct((CONV_WIDTH, W), F32), vec,
                 jax.ShapeDtypeStruct(gate_w, F32), vec, jax.ShapeDtypeStruct(gate_w, F32), vec, vec]

    def body(x_ref, y_ref, xh_ref, hs_ref, hh_ref, dr_ref, cw_ref, cb_ref, wa_ref, ba_ref, wx_ref, bx_ref, lam_ref,
             dxy_ref, dcw_ref, dcb_ref, dwa_ref, dba_ref, dwx_ref, dbx_ref, dlam_ref, gcar, dxc_car, a_scr, b_scr, g_scr):
        pid = pl.program_id(0)
        t = tb(pid)
        accs = (dcw_ref, dcb_ref, dwa_ref, dba_ref, dwx_ref, dbx_ref, dlam_ref)

        @pl.when(pid == 0)
        def _():
            gcar[...] = jnp.zeros_like(gcar)
            dxc_car[...] = jnp.zeros_like(dxc_car)
            for r in accs:
                r[...] = jnp.zeros_like(r)

        halo = jnp.where(t > 0, xh_ref[...].astype(F32), 0.0)
        taps = _conv_taps(jnp.concatenate([halo, x_ref[...].astype(F32)], axis=0), 16)
        xc = cb_ref[...] + sum(cw_ref[k:k + 1, :] * taps[k] for k in range(CONV_WIDTH))
        lam = lam_ref[...]
        sp = _softplus(-lam)
        r, ig, a, om = _lru_gates(xc, wa_ref, ba_ref[...], wx_ref, bx_ref[...], sp)
        sq = jnp.sqrt(om)
        hblk = hs_ref[...]
        hprev = pltpu.roll(jnp.concatenate([jnp.where(t > 0, hh_ref[...], 0.0), hblk], axis=0), 1, 0)[8:]
        gy, dgy = _gelu_parts(y_ref[...].astype(F32))
        drec = dr_ref[...].astype(F32)
        dxy_ref[:, W:] = (drec * hblk * dgy).astype(BF16)

        rowidx = lax.broadcasted_iota(jnp.int32, (tc, W), 0)
        rowmod = rowidx & 7
        ca = jnp.where(rowidx == tc - 1, 1.0, pltpu.roll(a, tc - 1, 0))
        cbv = drec * gy
        for s in (1, 2, 4):
            keep = rowmod < 8 - s
            cbv = jnp.where(keep, ca * pltpu.roll(cbv, tc - s, 0) + cbv, cbv)
            ca = jnp.where(keep, ca * pltpu.roll(ca, tc - s, 0), ca)
        a_scr[...] = ca
        b_scr[...] = cbv

        def tile(k, gc):
            j = tc // 8 - 1 - k
            rows = pl.ds(pl.multiple_of(j * 8, 8), 8)
            gt = a_scr[rows, :] * gc + b_scr[rows, :]
            g_scr[rows, :] = gt
            return jnp.broadcast_to(gt[0:1, :], (8, W))

        lax.fori_loop(0, tc // 8, tile, gcar[...])
        G = g_scr[...]
        gcar[...] = jnp.broadcast_to(a[0:1, :] * G[0:1, :], (8, W))

        da = G * hprev
        dsq = G * (ig * xc)
        di = G * (sq * xc)
        dxc = G * (sq * ig)
        dla = da * a - 2.0 * a * a * (dsq * 0.5 * lax.rsqrt(om))
        dlam_ref[...] += jnp.sum(dla * (-LRU_C * r), axis=0, keepdims=True) * (-_sigmoid(-lam))
        dpr = dla * (-LRU_C * sp) * r * (1.0 - r)
        dpi = di * ig * (1.0 - ig)
        dba_ref[...] += jnp.sum(dpr, axis=0, keepdims=True)
        dbx_ref[...] += jnp.sum(dpi, axis=0, keepdims=True)
        back = []
        for hd in range(LRU_HEADS):
            sl = slice(hd * LRU_HEAD_DIM, (hd + 1) * LRU_HEAD_DIM)
            xh, dprh, dpih = xc[:, sl].astype(BF16), dpr[:, sl].astype(BF16), dpi[:, sl].astype(BF16)
            back.append(_dot(dprh, wa_ref[hd], NT) + _dot(dpih, wx_ref[hd], NT))
            dwa_ref[hd] += _dot(xh, dprh, TN)
            dwx_ref[hd] += _dot(xh, dpih, TN)
        dxc = dxc + jnp.concatenate(back, axis=1)
        dcb_ref[...] += jnp.sum(dxc, axis=0, keepdims=True)
        for k in range(CONV_WIDTH):
            dcw_ref[k:k + 1, :] += jnp.sum(dxc * taps[k], axis=0, keepdims=True)
        ext = jnp.concatenate([dxc, dxc_car[...]], axis=0)
        dx = cw_ref[CONV_WIDTH - 1:CONV_WIDTH, :] * dxc
        for k in range(CONV_WIDTH - 1):
            dx = dx + cw_ref[k:k + 1, :] * pltpu.roll(ext, tc + 8 - (CONV_WIDTH - 1 - k), 0)[:tc]
        dxc_car[...] = dxc[0:8, :]
        dxy_ref[:, :W] = dx.astype(BF16)

    scratch = [pltpu.VMEM((8, W), F32), pltpu.VMEM((8, W), F32)] + [pltpu.VMEM((tc, W), F32)] * 3
    return _call(body, name=name, grid=(nb,), in_specs=specs, out_specs=out_specs, out_shape=out_shape, scratch=scratch,
                 sem=("arbitrary",))(proj, proj, proj, hs, hs, drec_src, cw, cb, wa, ba, wx, bx, lam)


def _final_loss(h, gain, target, *, name, tm=512):
    M, K = h.shape
    row = pl.BlockSpec((tm, K), lambda i: (i, 0))
    vec = pl.BlockSpec((1, K), lambda i: (0, 0))
    one = pl.BlockSpec((1, 128), lambda i: (0, 0))

    def body(h_ref, g_ref, t_ref, dh_ref, dg_ref, loss_ref):
        @pl.when(pl.program_id(0) == 0)
        def _():
            dg_ref[...] = jnp.zeros_like(dg_ref)
            loss_ref[...] = jnp.zeros_like(loss_ref)

        x = h_ref[...]
        r = lax.rsqrt(jnp.mean(x * x, axis=-1, keepdims=True) + NORM_EPS)
        xhat = x * r
        err = xhat * g_ref[...] - t_ref[...]
        loss_ref[...] += 0.5 / K * jnp.sum(err * err)
        dy = err * (1.0 / K)
        dg_ref[...] += jnp.sum(dy * xhat, axis=0, keepdims=True)
        dxh = dy * g_ref[...]
        dh_ref[...] = r * (dxh - xhat * jnp.mean(dxh * xhat, axis=-1, keepdims=True))

    return _call(body, name=name, grid=(M // tm,), in_specs=[row, vec, row], out_specs=[row, vec, one],
                 out_shape=[jax.ShapeDtypeStruct((M, K), F32), jax.ShapeDtypeStruct((1, K), F32),
                            jax.ShapeDtypeStruct((1, 128), F32)], sem=("arbitrary",))(h, gain.reshape(1, K), target)


def _dilated_merge(branches, *, name, tm=512):
    L, W = branches[0].shape
    nbr = len(branches) // 2
    row = pl.BlockSpec((tm, W), lambda i: (i, 0))

    def body(*refs):
        o_ref, lse_ref = refs[-2], refs[-1]
        lses = [refs[2 * b + 1][...] for b in range(nbr)]
        m = lses[0]
        for t in lses[1:]:
            m = jnp.maximum(m, t)
        ws = [jnp.exp(t - m) for t in lses]
        den = ws[0]
        for t in ws[1:]:
            den = den + t
        acc = ws[0] * refs[0][...].astype(F32)
        for b in range(1, nbr):
            acc = acc + ws[b] * refs[2 * b][...].astype(F32)
        o_ref[...] = (acc / den).astype(BF16)
        lse_ref[...] = m + jnp.log(den)

    return _call(body, name=name, grid=(L // tm,), in_specs=[row] * (2 * nbr), out_specs=[row, row],
                 out_shape=[jax.ShapeDtypeStruct((L, W), BF16), jax.ShapeDtypeStruct((L, W), F32)], sem=("parallel",))(*branches)


def _dilated_fwd(proj0):
    L = proj0.shape[0]
    qkv = proj0[:, 2 * D_MODEL:]
    W = B_HEADS * HEAD_DIM
    outs = []
    for window, d in DILATED_PATTERN:
        view = qkv.reshape(L // d, d * 3 * W)
        o, lse = _band_fwd(view, view, view, d=d, nq=B_HEADS, nkv=B_HEADS, qcol=lambda r: 3 * r, kcol=lambda r: 3 * r + 1,
                           vcol=lambda r: 3 * r + 2, max_dist=window // d, name=f"dilated_fwd_d{d}")
        outs += [o.reshape(L, W), lse.reshape(L, W)]
    return _dilated_merge(outs, name="dilated_merge")


def _dilated_bwd(proj0, att, lse, datt, tabs):
    L = proj0.shape[0]
    qkv = proj0[:, 2 * D_MODEL:]
    Wh = B_HEADS * HEAD_DIM
    branches = []
    for window, d in DILATED_PATTERN:
        view = qkv.reshape(L // d, d * 3 * Wh)
        v1 = lambda t: t.reshape(L // d, d * Wh)
        outs = _band_bwd(view, view, view, v1(datt), v1(att), v1(lse), d=d, nq=B_HEADS, nkv=B_HEADS,
                         qcol=lambda r: 3 * r, kcol=lambda r: 3 * r + 1, vcol=lambda r: 3 * r + 2, docol=lambda r: r,
                         max_dist=window // d, name=f"dilated_bwd_d{d}")
        branches.append([o.reshape(L, Wh) for o in outs])
    return _attn_grad_combine(branches, tabs, name="dilated_grad_combine")


def _device_step(x, mem, target, w, on_grads=None):
    L = x.shape[0]
    tabs = _rope_tables(L)
    g = {}
    saved = []
    h = x
    for layer in range(2):
        sv = {"h_mix": h}
        if layer == 0:
            proj, n = _rowmm(h, w["ab_w_in"], name="l0_in_proj", gain=w["mix_norm"][0],
                             rope=(2 * D_MODEL, 2 * D_MODEL + 2 * B_HEADS * HEAD_DIM, tabs))
            rec, hs = _lru_fwd(proj, w["lru_conv_w"], w["lru_conv_b"], w["lru_wa"], w["lru_ba"], w["lru_wx"], w["lru_bx"],
                               w["lru_lambda"], name="lru_fwd")
            att, lse = _dilated_fwd(proj)
            mix = jnp.concatenate([rec, att], axis=1)
            (h,) = _rowmm(mix, w["ab_w_out"], name="l0_out_proj", res=h)
            sv.update(hs=hs)
        else:
            proj, n = _rowmm(h, w["c_w_qkv"], name="l1_qkv_proj", gain=w["mix_norm"][1], bias=w["c_b_qkv"],
                             rope=(0, (C_HEADS + C_KV_HEADS) * HEAD_DIM, tabs))
            mix, lse = _band_fwd(proj, proj, proj, d=1, nq=C_HEADS, nkv=C_KV_HEADS, qcol=lambda r: 0, kcol=lambda r: 8,
                                 vcol=lambda r: 9, max_dist=C_WINDOW - 1, sinks=w["c_sinks"], name="swa_fwd")
            (h,) = _rowmm(mix, w["c_w_out"], name="l1_out_proj", res=h, bias=w["c_b_out"])
        sv.update(proj=proj, n_mix=n, mix=mix, lse=lse, h_xa=h)
        xq, nx = _rowmm(h, w["xa_wq"][layer][None], name=f"xa_q_proj{layer}", gain=w["xa_norm"][layer])
        kv, nm = _rowmm(mem, w["xa_wkv"][layer][None], name=f"xa_kv_proj{layer}", gain=w["xa_mem_norm"][layer])
        xo, xlse = _xattn_fwd(xq, kv, name=f"xa_fwd{layer}")
        (h,) = _rowmm(xo, w["xa_wo"][layer], name=f"xa_out_proj{layer}", res=h)
        sv.update(xq=xq, nx=nx, kv=kv, nm=nm, xo=xo, xlse=xlse, h_ffn=h)
        gu, nf, act = _rowmm(h, w["ffn_w_gate_up"], layer=layer, name=f"ffn_in{layer}", gain=w["ffn_norm"][layer], swiglu=True)
        (h,) = _rowmm(act, w["ffn_w_down"][layer][None], name=f"ffn_out{layer}", res=h, tm=512)
        sv.update(gu=gu, nf=nf, act=act)
        saved.append(sv)

    dh, g["final_norm"], loss = _final_loss(h, w["final_norm"], target, name="final_loss")

    stk = {k: [None, None] for k in ("xa_norm", "xa_mem_norm", "ffn_norm", "mix_norm")}
    after = None
    for layer in (1, 0):
        sv = saved[layer]
        (g["ffn_w_down", layer],) = _mm_tn(sv["act"], dh, S=1, name=f"ffn_down_dw{layer}", kk=D_FF // 2)
        (dgu,) = _mm_nt(dh, w["ffn_w_down"][layer][None], name=f"ffn_dact{layer}", mode="swiglu", kchunk=D_FF // 2, gu=sv["gu"],
                        after=after)
        (g["ffn_w_gate_up", layer],) = _mm_tn(sv["nf"], dgu, S=N_CHIPS, name=f"ffn_gu_dw{layer}")
        dh, stk["ffn_norm"][layer] = _mm_nt(dgu, w["ffn_w_gate_up"], layer=layer, name=f"ffn_dx{layer}", mode="norm",
                                            h=sv["h_ffn"], gain=w["ffn_norm"][layer], dh=dh)
        (g["xa_wo", layer],) = _mm_tn(sv["xo"], dh, S=N_CHIPS, name=f"xa_wo_dw{layer}")
        (dxo,) = _mm_nt(dh, w["xa_wo"][layer], name=f"xa_dxo{layer}", mode="plain")
        dxq, dkv = _xattn_bwd(sv["xq"], sv["kv"], sv["xo"], sv["xlse"], dxo, name=f"xa_bwd{layer}")
        (g["xa_wq", layer],) = _mm_tn(sv["nx"], dxq, S=1, name=f"xa_wq_dw{layer}")
        dh, stk["xa_norm"][layer] = _mm_nt(dxq, w["xa_wq"][layer][None], name=f"xa_dx{layer}", mode="norm", h=sv["h_xa"],
                                           gain=w["xa_norm"][layer], dh=dh)
        (g["xa_wkv", layer],) = _mm_tn(sv["nm"], dkv, S=1, name=f"xa_wkv_dw{layer}")
        _, stk["xa_mem_norm"][layer] = _mm_nt(dkv, w["xa_wkv"][layer][None], name=f"xa_dmem{layer}", mode="norm", h=mem,
                                              gain=w["xa_mem_norm"][layer])
        if layer == 1:
            g["c_w_out"], g["c_b_out"] = _mm_tn(sv["mix"], dh, S=1, name="l1_out_dw", bias=True)
            (dmix,) = _mm_nt(dh, w["c_w_out"], name="l1_dmix", mode="plain")
            dq, dk, dv, dsk = _band_bwd(sv["proj"], sv["proj"], sv["proj"], dmix, sv["mix"], sv["lse"], d=1, nq=C_HEADS,
                                        nkv=C_KV_HEADS, qcol=lambda r: 0, kcol=lambda r: 8, vcol=lambda r: 9,
                                        docol=lambda r: 0, max_dist=C_WINDOW - 1, sinks=w["c_sinks"], name="swa_bwd")
            g["c_sinks"] = dsk[0, :C_HEADS]
            dproj = _attn_grad_combine([(dq, dk, dv)], tabs, name="swa_grad_combine")
            g["c_w_qkv"], g["c_b_qkv"] = _mm_tn(sv["n_mix"], dproj, S=1, name="l1_qkv_dw", bias=True)
            dh, stk["mix_norm"][1] = _mm_nt(dproj, w["c_w_qkv"], name="l1_dx", mode="norm", h=sv["h_mix"],
                                            gain=w["mix_norm"][1], dh=dh)
            if on_grads is not None:
                after = on_grads("layer1", g)
        else:
            if on_grads is not None:
                after = on_grads("layer0_ffn_xa", g)
            (g["ab_w_out"],) = _mm_tn(sv["mix"], dh, S=1, name="l0_out_dw", kk=768)
            (dmix,) = _mm_nt(dh, w["ab_w_out"], name="l0_dmix", mode="plain", kchunk=768, after=after)
            (dxy, g["lru_conv_w"], g["lru_conv_b"], g["lru_wa"], g["lru_ba"], g["lru_wx"], g["lru_bx"],
             g["lru_lambda"]) = _lru_bwd(sv["proj"], sv["hs"], dmix, w["lru_conv_w"], w["lru_conv_b"], w["lru_wa"],
                                         w["lru_ba"], w["lru_wx"], w["lru_bx"], w["lru_lambda"], name="lru_bwd")
            dqkv = _dilated_bwd(sv["proj"], sv["mix"][:, D_MODEL:], sv["lse"], dmix[:, D_MODEL:], tabs)
            dproj = jnp.concatenate([dxy, dqkv], axis=1)
            (g["ab_w_in"],) = _mm_tn(sv["n_mix"], dproj, S=N_CHIPS, name="l0_in_dw")
            dh, stk["mix_norm"][0] = _mm_nt(dproj, w["ab_w_in"], name="l0_dx", mode="norm", h=sv["h_mix"],
                                            gain=w["mix_norm"][0], dh=dh)
    for k, v in stk.items():
        g[k] = jnp.concatenate(v, axis=0)
    return loss[0, 0], dh, g


ANY = pl.BlockSpec(memory_space=pl.ANY)
MESH = pl.DeviceIdType.MESH


def _place():
    x, y, c = lax.axis_index("x"), lax.axis_index("y"), lax.axis_index("c")
    return x, y, c, [(1 - x, y), (x, 1 - y), (1 - x, 1 - y)]


def _remote(send_sems, recv_sems):
    def copy(k, src, dst, to):
        return pltpu.make_async_remote_copy(src_ref=src, dst_ref=dst, send_sem=send_sems.at[k], recv_sem=recv_sems.at[k],
                                            device_id=to, device_id_type=MESH)
    return copy


def _halves(ref, n_rows):
    rh = n_rows // 2
    return lambda lead, hh: ref.at[(*lead, pl.ds(hh * rh, rh), slice(None))]


def _gather_weights(packs, spack):
    n = len(packs)

    def body(*refs):
        w_refs, s_ref, wf_refs, sf_ref = refs[:n], refs[n], refs[n + 1:2 * n + 1], refs[2 * n + 1]
        x, y, c, chips = _place()
        me, sib = 2 * x + y, (x, y, 1 - c)
        copy = _remote(*refs[-2:])
        src = [_halves(w_refs[g], packs[g].shape[0]) for g in range(n)]
        dst = [_halves(wf_refs[g], packs[g].shape[0]) for g in range(n)]
        sends = []
        for g in range(n):
            for j, (cx, cy) in enumerate(chips):
                sends.append(copy(3 * g + j, src[g]((), c), dst[g]((me,), c), (cx, cy, c)))
        for j, (cx, cy) in enumerate(chips):
            sends.append(copy(6 * n + j, s_ref, sf_ref.at[me], (cx, cy, c)))
        for cp in sends:
            cp.start()
        for g in range(n):
            for j, (cx, cy) in enumerate(chips):
                got = dst[g]((2 * cx + cy,), c)
                copy(3 * g + j, got, got, sib).wait_recv()
                fwd = copy(3 * n + 3 * g + j, got, got, sib)
                fwd.start()
                sends.append(fwd)
        for g in range(n):
            for j, (cx, cy) in enumerate(chips):
                got = dst[g]((2 * cx + cy,), 1 - c)
                copy(3 * n + 3 * g + j, got, got, sib).wait_recv()
        for j, (cx, cy) in enumerate(chips):
            copy(6 * n + j, s_ref, sf_ref.at[2 * cx + cy], sib).wait_recv()
        for cp in sends:
            cp.wait_send()

    ins = list(packs) + [spack]
    out_shape = [jax.ShapeDtypeStruct((N_CHIPS,) + a.shape, a.dtype) for a in ins]
    n_sems = 6 * n + 3
    outs = pl.pallas_call(body, name="gather_weights", out_shape=out_shape, in_specs=[ANY] * len(ins),
                          out_specs=[ANY] * len(ins),
                          scratch_shapes=[pltpu.SemaphoreType.DMA((n_sems,)), pltpu.SemaphoreType.DMA((n_sems,))])(*ins)
    chip = 2 * lax.axis_index("x") + lax.axis_index("y")
    outs = [lax.dynamic_update_index_in_dim(o, a, chip, 0) for o, a in zip(outs, ins)]
    return outs[:n], outs[n]


SEQUENCER_GATHER_IDS = {"mid": 1, "late": 5}


def _gather_weights_behind(packs, *, tag):
    n = len(packs)

    def body(*refs):
        w_refs, wf_refs = refs[:n], refs[n:2 * n]
        x, y, c, chips = _place()
        me, sib = 2 * x + y, (x, y, 1 - c)
        barrier = pltpu.get_barrier_semaphore()
        for peer in [(cx, cy, c) for cx, cy in chips] + [sib]:
            pl.semaphore_signal(barrier, inc=1, device_id=peer, device_id_type=MESH)
        pl.semaphore_wait(barrier, len(chips) + 1)
        copy = _remote(*refs[-2:])
        src = [_halves(w_refs[g], packs[g].shape[0]) for g in range(n)]
        dst = [_halves(wf_refs[g], packs[g].shape[0]) for g in range(n)]
        sends = []
        for g in range(n):
            for j, (cx, cy) in enumerate(chips):
                sends.append(copy(3 * g + j, src[g]((), c), dst[g]((me,), c), (cx, cy, c)))
        for cp in sends:
            cp.start()
        for g in range(n):
            for j, (cx, cy) in enumerate(chips):
                got = dst[g]((2 * cx + cy,), c)
                copy(3 * g + j, got, got, sib).wait_recv()
                fwd = copy(3 * n + 3 * g + j, got, got, sib)
                fwd.start()
                sends.append(fwd)
        for g in range(n):
            for j, (cx, cy) in enumerate(chips):
                got = dst[g]((2 * cx + cy,), 1 - c)
                copy(3 * n + 3 * g + j, got, got, sib).wait_recv()
        for cp in sends:
            cp.wait_send()

    out_type = [jax.ShapeDtypeStruct((N_CHIPS,) + a.shape, a.dtype) for a in packs]
    outs = pl.kernel(body, out_type=out_type, mesh=plsc.ScalarSubcoreMesh(axis_name="sequencer", num_cores=1),
                     name="gather_weights_behind_" + tag,
                     scratch_types=[pltpu.SemaphoreType.DMA((6 * n,)), pltpu.SemaphoreType.DMA((6 * n,))],
                     compiler_params=pltpu.CompilerParams(collective_id=SEQUENCER_GATHER_IDS[tag]))(*packs)
    chip = 2 * lax.axis_index("x") + lax.axis_index("y")
    return [lax.dynamic_update_index_in_dim(o, a, chip, 0) for o, a in zip(outs, packs)]


def _rs_pair_exchange(gpacks, *, name):
    n = len(gpacks)

    def body(*refs):
        g_refs, ra_refs = refs[:n], refs[n:2 * n]
        x, y, c, _ = _place()
        copy = _remote(*refs[-2:])
        cps = []
        for g in range(n):
            half = _halves(g_refs[g], gpacks[g].shape[1])
            cps += [copy(N_CHIPS * g + j, half((j,), 1 - c), ra_refs[g].at[j], (x, y, 1 - c)) for j in range(N_CHIPS)]
        for cp in cps:
            cp.start()
        for cp in cps:
            cp.wait()

    out_shape = [jax.ShapeDtypeStruct((N_CHIPS, a.shape[1] // 2, a.shape[2]), a.dtype) for a in gpacks]
    n_sems = N_CHIPS * n
    return pl.pallas_call(body, name=name, out_shape=out_shape, in_specs=[ANY] * n, out_specs=[ANY] * n,
                          scratch_shapes=[pltpu.SemaphoreType.DMA((n_sems,)), pltpu.SemaphoreType.DMA((n_sems,))])(*gpacks)


def _row_tile(rows, cap=512):
    return max(t for t in range(16, min(rows, cap) + 1, 16) if rows % t == 0)


def _rs_pair_add(place, gpack, ra, *, name):
    _, R, C = gpack.shape
    Rh = R // 2
    tr = _row_tile(Rh)
    nrb = Rh // tr

    def body(p_ref, g_ref, ra_ref, pair_ref, own_ref):
        s = g_ref[...].astype(F32) + ra_ref[...].astype(F32)
        pair_ref[...] = s.astype(BF16)

        @pl.when(pl.program_id(1) == p_ref[1])
        def _():
            own_ref[...] = s

    grid_spec = pltpu.PrefetchScalarGridSpec(
        num_scalar_prefetch=1, grid=(nrb, N_CHIPS),
        in_specs=[pl.BlockSpec((None, tr, C), lambda i, j, p: (j, p[0] * nrb + i, 0)),
                  pl.BlockSpec((None, tr, C), lambda i, j, p: (j, i, 0))],
        out_specs=[pl.BlockSpec((None, tr, C), lambda i, j, p: (j, i, 0)), pl.BlockSpec((tr, C), lambda i, j, p: (i, 0))])
    return pl.pallas_call(
        body, name=name, grid_spec=grid_spec,
        out_shape=[jax.ShapeDtypeStruct((N_CHIPS, Rh, C), BF16), jax.ShapeDtypeStruct((Rh, C), F32)],
        compiler_params=pltpu.CompilerParams(dimension_semantics=("arbitrary", "arbitrary"),
                                             vmem_limit_bytes=VMEM_LIMIT_V7X))(place, gpack, ra)


SEQUENCER_EXCHANGE_IDS = {"l1": 2, "l0a": 3, "l0b": 4}


def _rs_chip_exchange_behind(pairs, *, tag, small=None):
    n = len(pairs)
    has_small = small is not None

    def body(*refs):
        p_refs = refs[:n]
        s_ref = refs[n] if has_small else None
        rb_refs = refs[n + has_small:2 * n + has_small]
        rs_ref = refs[2 * n + 1] if has_small else None
        x, y, c, chips = _place()
        peers = [(1 - x if k & 4 else x, 1 - y if k & 2 else y, 1 - c if k & 1 else c) for k in range(1, 8)]
        shake = peers if has_small else [(cx, cy, c) for cx, cy in chips]
        barrier = pltpu.get_barrier_semaphore()
        for peer in shake:
            pl.semaphore_signal(barrier, inc=1, device_id=peer, device_id_type=MESH)
        pl.semaphore_wait(barrier, len(shake))
        copy = _remote(*refs[-2:])
        cps = []
        for g in range(n):
            cps += [copy(3 * g + j, p_refs[g].at[2 * cx + cy], rb_refs[g].at[j], (cx, cy, c)) for j, (cx, cy) in enumerate(chips)]
        if has_small:
            dev = 4 * x + 2 * y + c
            cps += [copy(3 * n + k, s_ref, rs_ref.at[dev], peer) for k, peer in enumerate(peers)]
        for cp in cps:
            cp.start()
        for g in range(n):
            for j in range(3):
                copy(3 * g + j, p_refs[g].at[0], rb_refs[g].at[j], (x, y, c)).wait_recv()
        if has_small:
            for k, (px, py, pc) in enumerate(peers):
                copy(3 * n + k, s_ref, rs_ref.at[4 * px + 2 * py + pc], (x, y, c)).wait_recv()
        for cp in cps:
            cp.wait_send()

    ins = list(pairs) + ([small] if has_small else [])
    out_type = [jax.ShapeDtypeStruct((3,) + p.shape[1:], p.dtype) for p in pairs]
    if has_small:
        out_type.append(jax.ShapeDtypeStruct((8,) + small.shape, small.dtype))
    n_sems = 3 * n + 7 * has_small
    outs = pl.kernel(body, out_type=out_type, mesh=plsc.ScalarSubcoreMesh(axis_name="sequencer", num_cores=1),
                     name="rs_chip_exchange_behind_" + tag,
                     scratch_types=[pltpu.SemaphoreType.DMA((n_sems,)), pltpu.SemaphoreType.DMA((n_sems,))],
                     compiler_params=pltpu.CompilerParams(collective_id=SEQUENCER_EXCHANGE_IDS[tag]))(*ins)
    if has_small:
        dev = 4 * lax.axis_index("x") + 2 * lax.axis_index("y") + lax.axis_index("c")
        outs = list(outs[:n]) + [lax.dynamic_update_index_in_dim(outs[n], small, dev, 0)]
    return outs


def _rs_final_add(place, own, rb, *, name):
    Rh, C = own.shape
    tr = _row_tile(Rh)
    nrb = Rh // tr

    def body(p_ref, o_ref, rb_ref, f_ref):
        f_ref[...] = ((o_ref[...] + rb_ref[0].astype(F32)) + rb_ref[1].astype(F32)) + rb_ref[2].astype(F32)

    grid_spec = pltpu.PrefetchScalarGridSpec(
        num_scalar_prefetch=1, grid=(nrb,),
        in_specs=[pl.BlockSpec((tr, C), lambda i, p: (i, 0)), pl.BlockSpec((3, tr, C), lambda i, p: (0, i, 0))],
        out_specs=pl.BlockSpec((tr, C), lambda i, p: (p[0] * nrb + i, 0)))
    return pl.pallas_call(
        body, name=name, grid_spec=grid_spec, out_shape=jax.ShapeDtypeStruct((2 * Rh, C), F32),
        compiler_params=pltpu.CompilerParams(dimension_semantics=("arbitrary",), vmem_limit_bytes=VMEM_LIMIT_V7X))(place, own, rb)


def _sum_slots(rs):
    n, rows, C = rs.shape

    def body(r_ref, o_ref):
        acc = r_ref[0]
        for k in range(1, n):
            acc = acc + r_ref[k]
        o_ref[...] = acc

    return _call(body, name="small_grad_sum", grid=(1,), in_specs=[pl.BlockSpec((n, rows, C), lambda i: (0, 0, 0))],
                 out_specs=pl.BlockSpec((rows, C), lambda i: (0, 0)), out_shape=jax.ShapeDtypeStruct((rows, C), F32),
                 sem=("arbitrary",))(rs)


def _rs_sibling_share(gbufs, *, name):
    n = len(gbufs)

    def body(*refs):
        g_refs = refs[n:2 * n]
        x, y, c, _ = _place()
        copy = _remote(*refs[-2:])
        halves = [_halves(g_refs[g], gbufs[g].shape[0]) for g in range(n)]
        outs = [copy(g, halves[g]((), c), halves[g]((), c), (x, y, 1 - c)) for g in range(n)]
        for cp in outs:
            cp.start()
        for g in range(n):
            copy(g, halves[g]((), 1 - c), halves[g]((), 1 - c), (x, y, c)).wait_recv()
        for cp in outs:
            cp.wait_send()

    return pl.pallas_call(body, name=name, out_shape=[jax.ShapeDtypeStruct(a.shape, a.dtype) for a in gbufs],
                          in_specs=[ANY] * n, out_specs=[ANY] * n, input_output_aliases={g: g for g in range(n)},
                          scratch_shapes=[pltpu.SemaphoreType.DMA((n,)), pltpu.SemaphoreType.DMA((n,))])(*gbufs)


def _adamw(w, g, m, v, *, name, g_row=0):
    rows, cols = w.shape
    tr = rows
    for cand in range(min(rows, 512), 7, -8):
        if rows % cand == 0 and g_row % cand == 0:
            tr = cand
            break
    spec = pl.BlockSpec((tr, cols), lambda i: (i, 0))
    g_spec = pl.BlockSpec((tr, cols), lambda i: (g_row // tr + i, 0))

    def body(w_ref, g_ref, m_ref, v_ref, d_ref, nm_ref, nv_ref):
        gg = g_ref[...]
        nm = ADAM_B1 * m_ref[...] + (1.0 - ADAM_B1) * gg
        nv = ADAM_B2 * v_ref[...] + (1.0 - ADAM_B2) * (gg * gg)
        m_hat = nm / (1.0 - ADAM_B1 ** ADAM_STEP)
        v_hat = nv / (1.0 - ADAM_B2 ** ADAM_STEP)
        d_ref[...] = -ADAM_LR * (m_hat / (jnp.sqrt(v_hat) + ADAM_EPS) + ADAM_WD * w_ref[...])
        nm_ref[...] = nm
        nv_ref[...] = nv

    return _call(body, name=name, grid=(rows // tr,), in_specs=[spec, g_spec, spec, spec], out_specs=[spec] * 3,
                 out_shape=[jax.ShapeDtypeStruct((rows, cols), F32)] * 3, sem=("parallel",))(w, g, m, v)


WEIGHT_NAMES = ("mix_norm", "ab_w_in", "lru_conv_w", "lru_conv_b", "lru_wa", "lru_ba", "lru_wx", "lru_bx", "lru_lambda",
                "ab_w_out", "c_w_qkv", "c_b_qkv", "c_sinks", "c_w_out", "c_b_out", "xa_norm", "xa_mem_norm", "xa_wq",
                "xa_wkv", "xa_wo", "ffn_norm", "ffn_w_gate_up", "ffn_w_down", "final_norm")
EARLY_GROUPS = (("ab_w_in",),)
MID_GROUPS = (("ab_w_out",), ("lru_wa", "lru_wx"))
LATE_GROUPS = (("c_w_out", "xa_wkv", "ffn_w_down"), ("ffn_w_gate_up",), ("xa_wo",), ("xa_wq",), ("c_w_qkv",))
GROUPS = EARLY_GROUPS + MID_GROUPS + LATE_GROUPS
REPLICATED = ("mix_norm", "lru_conv_b", "lru_lambda", "c_sinks", "xa_norm", "xa_mem_norm", "ffn_norm", "final_norm")
SMALL_SHARDED = ("lru_conv_w", "lru_ba", "lru_bx", "c_b_qkv", "c_b_out")
LANES = 1024


def _rows(v):
    flat = v.reshape(-1)
    return jnp.pad(flat, (0, -flat.shape[0] % LANES)).reshape(-1, LANES)


def _pack_small(parts, total, *, name):
    def body(*refs):
        o_ref = refs[-1]
        o_ref[...] = jnp.zeros_like(o_ref)
        row = 0
        for p_ref in refs[:-1]:
            o_ref[row:row + p_ref.shape[0], :] = p_ref[...]
            row += p_ref.shape[0]

    return _call(body, name=name, grid=(1,), in_specs=[pl.BlockSpec(p.shape, lambda i: (0, 0)) for p in parts],
                 out_specs=pl.BlockSpec((total, LANES), lambda i: (0, 0)),
                 out_shape=jax.ShapeDtypeStruct((total, LANES), F32), sem=("arbitrary",))(*parts)


def _from_shards(name, t):
    minor = t.shape[-1]
    if name == "ab_w_in":
        return t
    if name in ("ab_w_out", "c_w_out"):
        return t.reshape(1, -1, minor)
    if name == "ffn_w_gate_up":
        return t.reshape(N_CHIPS, 2, -1, minor)
    if name in ("xa_wq", "xa_wkv", "ffn_w_down"):
        return t.reshape(N_CHIPS, 2, -1, minor).transpose(1, 0, 2, 3).reshape(2, -1, minor)
    if name in ("lru_wa", "lru_wx"):
        return t.reshape(N_CHIPS, LRU_HEADS, -1, minor).transpose(1, 0, 2, 3).reshape(LRU_HEADS, LRU_HEAD_DIM, minor)
    if name == "xa_wo":
        return t.reshape(N_CHIPS, 2, -1, minor).transpose(1, 0, 2, 3)
    assert name == "c_w_qkv"
    return t.transpose(1, 0, 2).reshape(1, D_MODEL, -1)


def _piece_shards(name, g):
    minor = g.shape[-1]
    if name in ("ab_w_in", "ffn_w_gate_up", "xa_wo"):
        return g
    if name in ("ab_w_out", "c_w_out", "xa_wq", "xa_wkv", "ffn_w_down"):
        return g.reshape(N_CHIPS, -1, minor)
    if name in ("lru_wa", "lru_wx"):
        return g.reshape(LRU_HEADS, N_CHIPS, -1, minor).transpose(1, 0, 2, 3).reshape(N_CHIPS, -1, minor)
    assert name == "c_w_qkv"
    return g.reshape(D_MODEL, N_CHIPS, -1).transpose(1, 0, 2)


RS_SETS = {
    "l1": ((("c_w_out", None), ("xa_wkv", 1), ("ffn_w_down", 1)), (("ffn_w_gate_up", 1),), (("xa_wo", 1),),
           (("xa_wq", 1),), (("c_w_qkv", None),)),
    "l0a": ((("xa_wkv", 0), ("ffn_w_down", 0)), (("ffn_w_gate_up", 0),), (("xa_wo", 0),), (("xa_wq", 0),)),
    "l0b": ((("ab_w_out", None),), (("ab_w_in", None),), (("lru_wa", None), ("lru_wx", None))),
}
RS_STAGE = {"layer1": "l1", "layer0_ffn_xa": "l0a"}


def kernel(x, mem, mix_norm, ab_w_in, lru_conv_w, lru_conv_b, lru_wa, lru_ba, lru_wx, lru_bx, lru_lambda, ab_w_out, c_w_qkv, c_b_qkv, c_sinks, c_w_out, c_b_out, xa_norm, xa_mem_norm, xa_wq, xa_wkv, xa_wo, ffn_norm, ffn_w_gate_up, ffn_w_down, final_norm, loss_target, m_mix_norm, m_ab_w_in, m_lru_conv_w, m_lru_conv_b, m_lru_wa, m_lru_ba, m_lru_wx, m_lru_bx, m_lru_lambda, m_ab_w_out, m_c_w_qkv, m_c_b_qkv, m_c_sinks, m_c_w_out, m_c_b_out, m_xa_norm, m_xa_mem_norm, m_xa_wq, m_xa_wkv, m_xa_wo, m_ffn_norm, m_ffn_w_gate_up, m_ffn_w_down, m_final_norm, v_mix_norm, v_ab_w_in, v_lru_conv_w, v_lru_conv_b, v_lru_wa, v_lru_ba, v_lru_wx, v_lru_bx, v_lru_lambda, v_ab_w_out, v_c_w_qkv, v_c_b_qkv, v_c_sinks, v_c_w_out, v_c_b_out, v_xa_norm, v_xa_mem_norm, v_xa_wq, v_xa_wkv, v_xa_wo, v_ffn_norm, v_ffn_w_gate_up, v_ffn_w_down, v_final_norm):
    given = dict(locals())
    wl = {n: given[n] for n in WEIGHT_NAMES}
    ml = {n: given["m_" + n] for n in WEIGHT_NAMES}
    vl = {n: given["v_" + n] for n in WEIGHT_NAMES}
    xi, yi, ci = lax.axis_index("x"), lax.axis_index("y"), lax.axis_index("c")
    chip = 2 * xi + yi

    def join(parts, axis):
        return parts[0] if len(parts) == 1 else jnp.concatenate(parts, axis=axis)

    local_rows = {n: wl[n].size // wl[n].shape[-1] for grp in GROUPS for n in grp}
    packs = [join([wl[n].astype(BF16).reshape(local_rows[n], wl[n].shape[-1]) for n in grp], 0) for grp in GROUPS]
    spack = _pack_small([_rows(wl[n]) for n in SMALL_SHARDED], 8, name="pack_small_weights")
    n_early, n_mid = len(EARLY_GROUPS), len(EARLY_GROUPS) + len(MID_GROUPS)
    early, sfull = _gather_weights(packs[:n_early], spack)
    early, sfull, mid_packs = lax.optimization_barrier((early, sfull, packs[n_early:n_mid]))
    mid = _gather_weights_behind(mid_packs, tag="mid")
    mid, late_packs = lax.optimization_barrier((mid, packs[n_mid:]))
    gathered = early + mid + _gather_weights_behind(late_packs, tag="late")
    w = {n: wl[n] for n in REPLICATED}
    w["c_sinks"] = wl["c_sinks"][0]
    for grp, full in zip(GROUPS, gathered):
        off = 0
        for n in grp:
            w[n] = _from_shards(n, full if len(grp) == 1 else full[:, off:off + local_rows[n]])
            off += local_rows[n]
    for r, n in enumerate(SMALL_SHARDED):
        loc = wl[n].shape[1:]
        t = sfull[:, r, :wl[n].size].reshape((N_CHIPS,) + loc)
        if n == "lru_conv_w":
            w[n] = t.transpose(1, 0, 2).reshape(CONV_WIDTH, -1)
        elif n in ("lru_ba", "lru_bx"):
            w[n] = t.transpose(1, 0, 2).reshape(1, -1)
        else:
            w[n] = t.reshape(1, -1)

    place = jnp.stack([ci, chip]).astype(jnp.int32)

    def pair_stage(spec, g, tag):
        piece = lambda n, l: (g[n] if l is None else g[n, l]).astype(BF16)
        gpacks = [join([_piece_shards(n, piece(n, l)) for n, l in grp], 1) for grp in spec]
        ras = _rs_pair_exchange(gpacks, name=f"rs_pair_exchange_{tag}")
        sums = [_rs_pair_add(place, gp, ra, name=f"rs_pair_add_{tag}_{i}") for i, (gp, ra) in enumerate(zip(gpacks, ras))]
        return [pair for pair, _ in sums], [own for _, own in sums]

    reduced, in_flight = [], []

    def take_up():
        done = [_rs_final_add(place, o, r, name=f"rs_final_add_{len(reduced) + i}") for i, (o, r) in enumerate(in_flight)]
        reduced.extend(done)
        in_flight.clear()
        return done

    def reduce_behind(stage, g):
        done = take_up()
        tag = RS_STAGE[stage]
        pairs, own = pair_stage(RS_SETS[tag], g, tag)
        in_flight.extend(zip(own, _rs_chip_exchange_behind(pairs, tag=tag)))
        return own + done

    loss_part, grad_x, g = _device_step(x[0], mem[0], loss_target[0], w, on_grads=reduce_behind)

    small_parts = [_rows(g[n]) for n in REPLICATED] + [_rows(jnp.broadcast_to(loss_part, (LANES,)))]
    small_parts += [_rows(g[n]) for n in SMALL_SHARDED]
    small = _pack_small(small_parts, 24, name="pack_small_grads")
    take_up()
    pairs, own = pair_stage(RS_SETS["l0b"], g, "l0b")
    *rb, rs = _rs_chip_exchange_behind(pairs, tag="l0b", small=small)
    gsums = list(_rs_sibling_share(list(reduced), name="rs_sibling_share_behind"))
    in_flight.extend(zip(own, rb))
    gsums += list(_rs_sibling_share(take_up(), name="rs_sibling_share_last"))
    ssum = _sum_slots(rs)

    where = {}
    for grp, gsum in zip(RS_SETS["l1"] + RS_SETS["l0a"] + RS_SETS["l0b"], gsums):
        off = 0
        for n, l in grp:
            rows = local_rows[n] if l is None else local_rows[n] // 2
            where[n, l] = (gsum, off, rows, len(grp) == 1)
            off += rows
    take = lambda gsum, off, rows, whole: gsum if whole else gsum[off:off + rows]
    grads, grad_rows = {}, {}
    for grp in LATE_GROUPS + EARLY_GROUPS + MID_GROUPS:
        for n in grp:
            if (n, None) in where:
                grads[n] = take(*where[n, None]).reshape(wl[n].shape)
                grad_rows[n] = where[n, None][:2]
            else:
                grads[n] = jnp.stack([take(*where[n, l]).reshape(wl[n].shape[1:]) for l in range(2)])
                grad_rows[n] = (grads[n].reshape(local_rows[n], wl[n].shape[-1]), 0)
    row = 0
    for n in REPLICATED:
        k = _rows(g[n]).shape[0]
        grads[n] = ssum[row:row + k].reshape(-1)[:wl[n].size].reshape(wl[n].shape)
        row += k
    loss = ssum[row, 0]
    row += 1
    for n in SMALL_SHARDED:
        k = _rows(g[n]).shape[0]
        full = ssum[row:row + k].reshape(-1)[:g[n].size]
        row += k
        loc = wl[n].shape
        if n == "lru_conv_w":
            sh = full.reshape(CONV_WIDTH, N_CHIPS, -1)
        elif n in ("lru_ba", "lru_bx"):
            sh = full.reshape(LRU_HEADS, N_CHIPS, -1)
        else:
            sh = full.reshape(1, N_CHIPS, -1)
        grads[n] = lax.dynamic_index_in_dim(sh, chip, axis=1, keepdims=False).reshape(loc)

    delta, new_m, new_v = {}, {}, {}
    for n, (gsum, off) in grad_rows.items():
        shape2 = (local_rows[n], wl[n].shape[-1])
        d, nm, nv = _adamw(wl[n].reshape(shape2), gsum, ml[n].reshape(shape2), vl[n].reshape(shape2), g_row=off,
                           name="adamw_" + n)
        delta[n], new_m[n], new_v[n] = (t.reshape(wl[n].shape) for t in (d, nm, nv))
    smalls = REPLICATED + SMALL_SHARDED
    packs = [_pack_small([_rows(src[n]) for n in smalls], 24, name="pack_adamw_" + tag)
             for tag, src in (("w", wl), ("g", grads), ("m", ml), ("v", vl))]
    outs = _adamw(*packs, name="adamw_small")
    row = 0
    for n in smalls:
        k = _rows(wl[n]).shape[0]
        for dst, o in zip((delta, new_m, new_v), outs):
            dst[n] = o[row:row + k].reshape(-1)[:wl[n].size].reshape(wl[n].shape)
        row += k

    return (loss, grad_x[None], *[grads[n] for n in WEIGHT_NAMES], *[delta[n] for n in WEIGHT_NAMES],
            *[new_m[n] for n in WEIGHT_NAMES], *[new_v[n] for n in WEIGHT_NAMES])
```

```python
import jax
import jax.numpy as jnp
from jax import lax
from jax.experimental import pallas as pl
from jax.experimental.pallas import tpu as pltpu
from jax.experimental.pallas import tpu_sc as plsc

F32, BF16 = jnp.float32, jnp.bfloat16
D_MODEL = 1024
NORM_EPS = 1e-6
ROPE_THETA = 500000.0
HEAD_DIM = 64
ROT_DIM = 16
BLK = 128
LRU_HEADS, LRU_HEAD_DIM, CONV_WIDTH, LRU_C = 4, 256, 4, 8.0
DILATED_PATTERN = ((128, 1), (512, 4), (2048, 16))
B_HEADS, C_HEADS, C_KV_HEADS, C_WINDOW = 8, 16, 2, 128
XA_HEADS, XA_HEAD_DIM, N_MEM = 4, 128, 256
D_FF = 2816
NEG = -1e30
ADAM_LR, ADAM_B1, ADAM_B2, ADAM_EPS, ADAM_WD, ADAM_STEP = 0.001, 0.9, 0.999, 1e-08, 0.01, 10
N_CHIPS = 4
VMEM_LIMIT_V7X = 56 * 1024 * 1024

NN = (((1,), (0,)), ((), ()))
NT = (((1,), (1,)), ((), ()))
TN = (((0,), (0,)), ((), ()))


def _dot(a, b, dims=NN):
    return lax.dot_general(a, b, dims, preferred_element_type=F32)


def _sigmoid(x):
    return 0.5 * jnp.tanh(0.5 * x) + 0.5


def _call(body, *, name, grid, in_specs, out_specs, out_shape, scratch=(), sem=None):
    return pl.pallas_call(
        body, name=name, grid=grid, in_specs=in_specs, out_specs=out_specs, out_shape=out_shape,
        scratch_shapes=list(scratch),
        compiler_params=pltpu.CompilerParams(dimension_semantics=sem, vmem_limit_bytes=VMEM_LIMIT_V7X))


def _rope_tables(L):
    half = ROT_DIM // 2
    inv = ROPE_THETA ** (-jnp.arange(0, ROT_DIM, 2, dtype=F32) / ROT_DIM)
    j = jnp.arange(2 * HEAD_DIM) % HEAD_DIM
    ang = jnp.arange(L, dtype=F32)[:, None] * inv[j % half][None, :]
    cos, sin = jnp.cos(ang), jnp.sin(ang)
    c = jnp.where(j < ROT_DIM, cos, 1.0)
    s1 = jnp.where(j < half, -sin, 0.0)
    s2 = jnp.where((j >= half) & (j < ROT_DIM), sin, 0.0)
    return c, s1, s2


def _rope_fwd(v, c, s1, s2):
    return v * c + pltpu.roll(v, 120, 1) * s1 + pltpu.roll(v, 8, 1) * s2


def _rope_bwd(dv, c, s1, s2):
    return dv * c + pltpu.roll(dv * s1, 8, 1) + pltpu.roll(dv * s2, 120, 1)


def _weight_spec(w, layer):
    once = pl.Buffered(1)
    if layer is None:
        return w.shape, pl.BlockSpec(w.shape, lambda i: (0, 0, 0), pipeline_mode=once)
    S, _, K, Ns = w.shape
    return (S, K, Ns), pl.BlockSpec((S, None, K, Ns), lambda i: (0, layer, 0, 0), pipeline_mode=once)


def _rowmm(a, w3, *, name, tm=512, gain=None, bias=None, res=None, swiglu=False, rope=None, layer=None):
    M, K = a.shape
    (S, _, Ns), w_spec = _weight_spec(w3, layer)
    N = S * Ns
    tm = min(tm, M)
    has_norm, has_bias, has_res, has_rope = gain is not None, bias is not None, res is not None, rope is not None
    row = lambda w: pl.BlockSpec((tm, w), lambda i: (i, 0))
    whole = lambda shape: pl.BlockSpec(shape, lambda i: (0,) * len(shape))
    ins, specs = [a], [row(K)]
    if has_norm:
        ins.append(gain.reshape(1, K)); specs.append(whole((1, K)))
    ins.append(w3); specs.append(w_spec)
    if has_bias:
        ins.append(bias.reshape(1, N)); specs.append(whole((1, N)))
    if has_res:
        ins.append(res); specs.append(row(N))
    if has_rope:
        ins += list(rope[2]); specs += [row(128)] * 3
    y_dtype = F32 if has_res else BF16
    out_shape, out_specs = [jax.ShapeDtypeStruct((M, N), y_dtype)], [row(N)]
    if has_norm:
        out_shape.append(jax.ShapeDtypeStruct((M, K), BF16)); out_specs.append(row(K))
    if swiglu:
        out_shape.append(jax.ShapeDtypeStruct((M, N // 2), BF16)); out_specs.append(row(N // 2))
    scratch = [pltpu.VMEM((tm, N), F32)] if has_rope else []

    def body(*refs):
        it = iter(refs)
        a_ref = next(it)
        g_ref = next(it) if has_norm else None
        w_ref = next(it)
        b_ref = next(it) if has_bias else None
        r_ref = next(it) if has_res else None
        tabs = [next(it) for _ in range(3)] if has_rope else None
        y_ref = next(it)
        n_ref = next(it) if has_norm else None
        act_ref = next(it) if swiglu else None
        ys_ref = next(it) if has_rope else None
        if has_norm:
            x = a_ref[...].astype(F32)
            ms = jnp.mean(x * x, axis=-1, keepdims=True)
            xb = (x * lax.rsqrt(ms + NORM_EPS) * g_ref[...]).astype(BF16)
            n_ref[...] = xb
        else:
            xb = a_ref[...].astype(BF16)
        if swiglu:
            for s in range(S // 2):
                g = _dot(xb, w_ref[s])
                u = _dot(xb, w_ref[s + S // 2])
                y_ref[:, s * Ns:(s + 1) * Ns] = g.astype(BF16)
                y_ref[:, N // 2 + s * Ns:N // 2 + (s + 1) * Ns] = u.astype(BF16)
                act_ref[:, s * Ns:(s + 1) * Ns] = (g * _sigmoid(g) * u).astype(BF16)
            return
        for s in range(S):
            sl = slice(s * Ns, (s + 1) * Ns)
            acc = _dot(xb, w_ref[s])
            if has_bias:
                acc = acc + b_ref[:, sl]
            if has_res:
                acc = acc + r_ref[:, sl]
            if has_rope:
                ys_ref[:, sl] = acc
            else:
                y_ref[:, sl] = acc.astype(y_dtype)
        if has_rope:
            c, s1, s2 = (t[...] for t in tabs)
            for cb in range(N // 128):
                cs = slice(cb * 128, (cb + 1) * 128)
                v = ys_ref[:, cs]
                if rope[0] <= cb * 128 < rope[1]:
                    v = _rope_fwd(v, c, s1, s2)
                y_ref[:, cs] = v.astype(BF16)

    return _call(body, name=name, grid=(M // tm,), in_specs=specs, out_specs=out_specs, out_shape=out_shape,
                 scratch=scratch, sem=("parallel",))(*ins)


def _mm_nt(dy, w3, *, name, mode, tm=512, kchunk=None, h=None, gain=None, dh=None, gu=None, layer=None, after=None):
    M, N = dy.shape
    (S, K, Ns), w_spec = _weight_spec(w3, layer)
    kchunk = kchunk or K
    tm = min(tm, M)
    row = lambda w: pl.BlockSpec((tm, w), lambda i: (i, 0))
    whole = lambda shape: pl.BlockSpec(shape, lambda i: (0,) * len(shape))
    ins, specs = [dy, w3], [row(N), w_spec]
    after = list(after or ())
    ins = after + ins
    specs = [pl.BlockSpec((8, a.shape[1]), lambda i: (0, 0)) for a in after] + specs
    has_dh = dh is not None
    if mode == "norm":
        ins += [h, gain.reshape(1, K)]; specs += [row(K), whole((1, K))]
        if has_dh:
            ins.append(dh); specs.append(row(K))
        out_shape = [jax.ShapeDtypeStruct((M, K), F32), jax.ShapeDtypeStruct((1, K), F32)]
        out_specs = [row(K), whole((1, K))]
    elif mode == "swiglu":
        ins.append(gu); specs.append(row(2 * K))
        out_shape, out_specs = [jax.ShapeDtypeStruct((M, 2 * K), BF16)], [row(2 * K)]
    else:
        out_shape, out_specs = [jax.ShapeDtypeStruct((M, K), BF16)], [row(K)]

    def body(*refs):
        it = iter(refs[len(after):])
        dy_ref, w_ref = next(it), next(it)
        if mode == "norm":
            h_ref, g_ref = next(it), next(it)
            dh_ref = next(it) if has_dh else None
            o_ref, dg_ref = next(it), next(it)
        elif mode == "swiglu":
            gu_ref, o_ref = next(it), next(it)
        else:
            o_ref = next(it)
        for kc in range(K // kchunk):
            ks = slice(kc * kchunk, (kc + 1) * kchunk)
            acc = None
            for s in range(S):
                t = _dot(dy_ref[:, s * Ns:(s + 1) * Ns].astype(BF16), w_ref[s, ks, :], NT)
                acc = t if acc is None else acc + t
            if mode == "plain":
                o_ref[:, ks] = acc.astype(BF16)
            elif mode == "swiglu":
                us = slice(K + kc * kchunk, K + (kc + 1) * kchunk)
                g = gu_ref[:, ks].astype(F32)
                u = gu_ref[:, us].astype(F32)
                sg = _sigmoid(g)
                o_ref[:, ks] = (acc * u * (sg * (1.0 + g * (1.0 - sg)))).astype(BF16)
                o_ref[:, us] = (acc * (g * sg)).astype(BF16)
            else:
                x = h_ref[...].astype(F32)
                r = lax.rsqrt(jnp.mean(x * x, axis=-1, keepdims=True) + NORM_EPS)
                xhat = x * r
                dxh = acc * g_ref[...]
                dx = r * (dxh - xhat * jnp.mean(dxh * xhat, axis=-1, keepdims=True))
                o_ref[...] = dx + dh_ref[...] if has_dh else dx

                @pl.when(pl.program_id(0) == 0)
                def _():
                    dg_ref[...] = jnp.zeros_like(dg_ref)

                dg_ref[...] += jnp.sum(acc * xhat, axis=0, keepdims=True)

    sem = ("arbitrary",) if mode == "norm" else ("parallel",)
    return _call(body, name=name, grid=(M // tm,), in_specs=specs, out_specs=out_specs, out_shape=out_shape, sem=sem)(*ins)


def _mm_tn(x, dy, *, S, name, tk=2048, kk=None, bias=False):
    M, K = x.shape
    N = dy.shape[1]
    Ns = N // S
    kk = kk or K
    tk = min(tk, M)
    nl = M // tk
    in_specs = [pl.BlockSpec((tk, kk), lambda s, kc, l: (l, kc)), pl.BlockSpec((tk, Ns), lambda s, kc, l: (l, s))]
    out_shape = [jax.ShapeDtypeStruct((S, K, Ns), BF16)]
    out_specs = [pl.BlockSpec((None, kk, Ns), lambda s, kc, l: (s, kc, 0))]
    if bias:
        out_shape.append(jax.ShapeDtypeStruct((1, N), F32))
        out_specs.append(pl.BlockSpec((1, Ns), lambda s, kc, l: (0, s)))

    def body(x_ref, dy_ref, o_ref, *rest):
        acc_ref = rest[-1]
        kc, l = pl.program_id(1), pl.program_id(2)

        @pl.when(l == 0)
        def _():
            acc_ref[...] = jnp.zeros_like(acc_ref)

        acc_ref[...] += _dot(x_ref[...].astype(BF16), dy_ref[...].astype(BF16), TN)
        if bias:
            b_ref = rest[0]

            @pl.when((kc == 0) & (l == 0))
            def _():
                b_ref[...] = jnp.zeros_like(b_ref)

            @pl.when(kc == 0)
            def _():
                b_ref[...] += jnp.sum(dy_ref[...].astype(F32), axis=0, keepdims=True)

        @pl.when(l == nl - 1)
        def _():
            o_ref[...] = acc_ref[...].astype(BF16)

    return _call(body, name=name, grid=(S, K // kk, nl), in_specs=in_specs, out_specs=out_specs, out_shape=out_shape,
                 scratch=[pltpu.VMEM((kk, Ns), F32)], sem=("arbitrary", "arbitrary", "arbitrary"))(x, dy)


def _band_bias(max_dist, has_prev):
    rows = lax.broadcasted_iota(jnp.int32, (BLK, 2 * BLK), 0)
    cols = lax.broadcasted_iota(jnp.int32, (BLK, 2 * BLK), 1)
    dist = rows - cols + BLK
    ok = (dist >= 0) & (dist <= max_dist) & ((cols >= BLK) | has_prev)
    return jnp.where(ok, 0.0, NEG)


Q_SCALE = HEAD_DIM ** -0.5
BNT = (((2,), (2,)), ((0,), (0,)))
BNN = (((2,), (1,)), ((0,), (0,)))
BTN = (((1,), (1,)), ((0,), (0,)))


def _bdot(a, b, dims):
    return lax.dot_general(a, b, dims, preferred_element_type=F32)


def _split_heads(x, n, width=HEAD_DIM, group=1):
    return jnp.stack([x[:, (h // group) * width:(h // group + 1) * width] for h in range(n)], axis=0)


def _qkv_views(cur_ref, prv_ref, w):
    part = lambda ref, j: ref.at[:, pl.ds(j * w, w)]
    return part(cur_ref, 0), part(cur_ref, 1), part(prv_ref, 1), part(cur_ref, 2), part(prv_ref, 2)


def _band_fwd(qa, ka, va, *, d, nq, nkv, qcol, kcol, vcol, max_dist, sinks=None, fused=False, name):
    Lr = qa.shape[0]
    nb = Lr // BLK
    qw, kw, G = nq * HEAD_DIM, nkv * HEAD_DIM, nq // nkv
    cur = lambda colf, w: pl.BlockSpec((BLK, w), lambda r, i: (i, colf(r)))
    prv = lambda colf, w: pl.BlockSpec((BLK, w), lambda r, i: (jnp.maximum(i - 1, 0), colf(r)))
    out = pl.BlockSpec((BLK, qw), lambda r, i: (i, r))
    if fused:
        ins, specs = [qa, qa], [cur(lambda r: r, 3 * qw), prv(lambda r: r, 3 * qw)]
    else:
        ins, specs = [qa, ka, ka, va, va], [cur(qcol, qw), cur(kcol, kw), prv(kcol, kw), cur(vcol, kw), prv(vcol, kw)]
    n_in = len(ins)
    has_sinks = sinks is not None
    if has_sinks:
        ins.append(sinks); specs.append(pl.BlockSpec(memory_space=pltpu.SMEM))

    def body(*refs):
        q_ref, kc_ref, kp_ref, vc_ref, vp_ref = _qkv_views(refs[0], refs[1], qw) if fused else refs[:5]
        sk_ref = refs[n_in] if has_sinks else None
        o_ref, lse_ref = refs[-2], refs[-1]
        bias = _band_bias(max_dist, pl.program_id(1) > 0)
        k2 = jnp.concatenate([kp_ref[...], kc_ref[...]], axis=0)
        v2 = jnp.concatenate([vp_ref[...], vc_ref[...]], axis=0)
        if G == 1 and not has_sinks:
            q3 = _split_heads(q_ref[...], nq) * jnp.asarray(Q_SCALE, BF16)
            s = _bdot(q3, _split_heads(k2, nq), BNT) + bias
            m = jnp.max(s, axis=-1, keepdims=True)
            p = jnp.exp(s - m)
            l = jnp.sum(p, axis=-1, keepdims=True)
            o = (_bdot(p.astype(BF16), _split_heads(v2, nq), BNN) / l).astype(BF16)
            lse = m + jnp.log(l)
            for h in range(nq):
                hs = slice(h * HEAD_DIM, (h + 1) * HEAD_DIM)
                o_ref[:, hs] = o[h]
                lse_ref[:, hs] = jnp.broadcast_to(lse[h], (BLK, HEAD_DIM))
            return
        for h in range(nq):
            hs = slice(h * HEAD_DIM, (h + 1) * HEAD_DIM)
            ks = slice((h // G) * HEAD_DIM, (h // G + 1) * HEAD_DIM)
            s = _dot(q_ref[:, hs] * jnp.asarray(Q_SCALE, BF16), k2[:, ks], NT) + bias
            m = jnp.max(s, axis=-1, keepdims=True)
            if has_sinks:
                m = jnp.maximum(m, sk_ref[h])
            p = jnp.exp(s - m)
            l = jnp.sum(p, axis=-1, keepdims=True)
            if has_sinks:
                l = l + jnp.exp(sk_ref[h] - m)
            o_ref[:, hs] = (_dot(p.astype(BF16), v2[:, ks]) / l).astype(BF16)
            lse_ref[:, hs] = jnp.broadcast_to(m + jnp.log(l), (BLK, HEAD_DIM))

    return _call(body, name=name, grid=(d, nb), in_specs=specs, out_specs=[out, out],
                 out_shape=[jax.ShapeDtypeStruct((Lr, d * qw), BF16), jax.ShapeDtypeStruct((Lr, d * qw), F32)],
                 sem=("parallel", "parallel"))(*ins)


def _band_bwd(qa, ka, va, doa, oa, lsea, *, d, nq, nkv, qcol, kcol, vcol, docol, max_dist, sinks=None, fused=False, name):
    Lr = qa.shape[0]
    nb = Lr // BLK
    qw, kw, G = nq * HEAD_DIM, nkv * HEAD_DIM, nq // nkv
    transposed = G > 1
    last = lambda i: jnp.minimum(i, nb - 1)
    cur = lambda colf, w: pl.BlockSpec((BLK, w), lambda r, i: (last(i), colf(r)))
    prv = lambda colf, w: pl.BlockSpec((BLK, w), lambda r, i: (jnp.maximum(last(i) - 1, 0), colf(r)))
    own = lambda r: r
    if fused:
        ins, specs = [qa, qa], [cur(own, 3 * qw), prv(own, 3 * qw)]
    else:
        ins = [qa, ka, ka, va, va]
        specs = [cur(qcol, qw), cur(kcol, kw), prv(kcol, kw), cur(vcol, kw), prv(vcol, kw)]
    ins += [doa, oa, lsea]
    specs += [cur(docol, qw), cur(own, qw), cur(own, qw)]
    has_sinks = sinks is not None
    if has_sinks:
        ins.append(sinks); specs.append(pl.BlockSpec(memory_space=pltpu.SMEM))
    out_shape = [jax.ShapeDtypeStruct((Lr, d * qw), BF16), jax.ShapeDtypeStruct((Lr, d * kw), BF16),
                 jax.ShapeDtypeStruct((Lr, d * kw), BF16)]
    behind = lambda r, i: (jnp.maximum(i - 1, 0), r)
    out_specs = [pl.BlockSpec((BLK, qw), lambda r, i: (last(i), r)), pl.BlockSpec((BLK, kw), behind),
                 pl.BlockSpec((BLK, kw), behind)]
    if has_sinks:
        out_shape.append(jax.ShapeDtypeStruct((8, 128), F32))
        out_specs.append(pl.BlockSpec((8, 128), lambda r, i: (0, 0)))

    def body(*refs):
        it = iter(refs)
        if fused:
            q_ref, kc_ref, kp_ref, vc_ref, vp_ref = _qkv_views(next(it), next(it), qw)
        else:
            q_ref, kc_ref, kp_ref, vc_ref, vp_ref = (next(it) for _ in range(5))
        do_ref, o_ref, ls_ref = next(it), next(it), next(it)
        sk_ref = next(it) if has_sinks else None
        dq_ref, dk_ref, dv_ref = next(it), next(it), next(it)
        dsk_ref = next(it) if has_sinks else None
        dk_car, dv_car = next(it), next(it)
        r_id, i = pl.program_id(0), pl.program_id(1)

        @pl.when(i == 0)
        def _():
            dk_car[...] = jnp.zeros_like(dk_car)
            dv_car[...] = jnp.zeros_like(dv_car)

        if has_sinks:
            @pl.when((r_id == 0) & (i == 0))
            def _():
                dsk_ref[...] = jnp.zeros_like(dsk_ref)

        @pl.when(i == nb)
        def _():
            dk_ref[...] = dk_car[...].astype(BF16)
            dv_ref[...] = dv_car[...].astype(BF16)

        @pl.when(i < nb)
        def _():
            bias = jnp.tile(_band_bias(max_dist, i > 0), (G, 1))
            k2 = jnp.concatenate([kp_ref[...], kc_ref[...]], axis=0)
            v2 = jnp.concatenate([vp_ref[...], vc_ref[...]], axis=0)
            if G == 1 and not has_sinks:
                scale = jnp.asarray(Q_SCALE, BF16)
                q3, do3 = _split_heads(q_ref[...], nq) * scale, _split_heads(do_ref[...], nq)
                k3, v3 = _split_heads(k2, nq), _split_heads(v2, nq)
                lse = jnp.stack([ls_ref[:, h * HEAD_DIM:h * HEAD_DIM + 1] for h in range(nq)], axis=0)
                dl = jnp.sum(do3.astype(F32) * _split_heads(o_ref[...], nq).astype(F32), axis=-1, keepdims=True)
                p = jnp.exp(_bdot(q3, k3, BNT) + bias - lse)
                ds = (p * (_bdot(do3, v3, BNT) - dl)).astype(BF16)
                dq = (_bdot(ds, k3, BNN) * Q_SCALE).astype(BF16)
                dk, dv = _bdot(ds, q3, BTN), _bdot(p.astype(BF16), do3, BTN)
                for h in range(nq):
                    hs = slice(h * HEAD_DIM, (h + 1) * HEAD_DIM)
                    dq_ref[:, hs] = dq[h]
                    dk_ref[:, hs] = (dk_car[:, hs] + dk[h, :BLK]).astype(BF16)
                    dv_ref[:, hs] = (dv_car[:, hs] + dv[h, :BLK]).astype(BF16)
                    dk_car[:, hs] = dk[h, BLK:]
                    dv_car[:, hs] = dv[h, BLK:]
                return
            if has_sinks:
                lane = lax.broadcasted_iota(jnp.int32, (8, 128), 1)
                dsk = jnp.zeros((8, 128), F32)
            for kv in range(nkv):
                heads = [slice((kv * G + g) * HEAD_DIM, (kv * G + g + 1) * HEAD_DIM) for g in range(G)]
                stack = lambda ref: jnp.concatenate([ref[:, hs] for hs in heads], axis=0)
                ks = slice(kv * HEAD_DIM, (kv + 1) * HEAD_DIM)
                kh, vh = k2[:, ks], v2[:, ks]
                q = stack(q_ref) * jnp.asarray(Q_SCALE, BF16)
                do = stack(do_ref)
                lse = jnp.concatenate([ls_ref[:, hs.start:hs.start + 1] for hs in heads], axis=0)
                dl = jnp.sum(do.astype(F32) * stack(o_ref).astype(F32), axis=-1, keepdims=True)
                p = jnp.exp(_dot(q, kh, NT) + bias - lse)
                ds = (p * (_dot(do, vh, NT) - dl)).astype(BF16)
                dq = (_dot(ds, kh) * Q_SCALE).astype(BF16)
                for g, hs in enumerate(heads):
                    dq_ref[:, hs] = dq[g * BLK:(g + 1) * BLK]
                if transposed:
                    dk, dv = _dot(q, ds, TN).T, _dot(do, p.astype(BF16), TN).T
                else:
                    dk, dv = _dot(ds, q, TN), _dot(p.astype(BF16), do, TN)
                if has_sinks:
                    sk = jnp.concatenate([jnp.full((BLK, 1), sk_ref[kv * G + g], F32) for g in range(G)], axis=0)
                    lost = jnp.exp(sk - lse) * dl
                    for g in range(G):
                        val = -jnp.sum(lost[g * BLK:(g + 1) * BLK], axis=0, keepdims=True)
                        dsk = dsk + jnp.where(lane == kv * G + g, val, 0.0)
                dk_ref[:, ks] = (dk_car[:, ks] + dk[:BLK]).astype(BF16)
                dv_ref[:, ks] = (dv_car[:, ks] + dv[:BLK]).astype(BF16)
                dk_car[:, ks] = dk[BLK:]
                dv_car[:, ks] = dv[BLK:]
            if has_sinks:
                dsk_ref[...] += dsk

    return _call(body, name=name, grid=(d, nb + 1), in_specs=specs, out_specs=out_specs, out_shape=out_shape,
                 scratch=[pltpu.VMEM((BLK, kw), F32), pltpu.VMEM((BLK, kw), F32)], sem=("arbitrary", "arbitrary"))(*ins)


def _attn_grad_combine(branches, tabs, *, name, tm=512):
    L, qw = branches[0][0].shape
    kw = branches[0][1].shape[1]
    nbr = len(branches)
    row = lambda w: pl.BlockSpec((tm, w), lambda i: (i, 0))
    ins, specs = [], []
    for dq, dk, dv in branches:
        ins += [dq, dk, dv]; specs += [row(qw), row(kw), row(kw)]
    ins += list(tabs); specs += [row(128)] * 3

    def body(*refs):
        c, s1, s2 = (t[...] for t in refs[3 * nbr:3 * nbr + 3])
        o_ref = refs[-1]
        for part, (w, off, rot) in enumerate(((qw, 0, True), (kw, qw, True), (kw, qw + kw, False))):
            for cb in range(w // 128):
                cs = slice(cb * 128, (cb + 1) * 128)
                v = refs[part][:, cs].astype(F32)
                for b in range(1, nbr):
                    v = v + refs[3 * b + part][:, cs].astype(F32)
                if rot:
                    v = _rope_bwd(v, c, s1, s2)
                o_ref[:, off + cb * 128:off + (cb + 1) * 128] = v.astype(BF16)

    return _call(body, name=name, grid=(L // tm,), in_specs=specs, out_specs=row(qw + 2 * kw),
                 out_shape=jax.ShapeDtypeStruct((L, qw + 2 * kw), BF16), sem=("parallel",))(*ins)


def _xattn_fwd(q, kv, *, name, tq=1024):
    L, W = q.shape
    scale = XA_HEAD_DIM ** -0.5
    row = pl.BlockSpec((tq, W), lambda i: (i, 0))
    kvs = pl.BlockSpec((N_MEM, 2 * W), lambda i: (0, 0))

    def body(q_ref, kv_ref, o_ref, lse_ref):
        kv = kv_ref[...]
        k3, v3 = _split_heads(kv[:, :W], XA_HEADS, XA_HEAD_DIM), _split_heads(kv[:, W:], XA_HEADS, XA_HEAD_DIM)
        s = _bdot(_split_heads(q_ref[...], XA_HEADS, XA_HEAD_DIM), k3, BNT) * scale
        m = jnp.max(s, axis=-1, keepdims=True)
        p = jnp.exp(s - m)
        l = jnp.sum(p, axis=-1, keepdims=True)
        o = (_bdot(p.astype(BF16), v3, BNN) / l).astype(BF16)
        lse = m + jnp.log(l)
        for h in range(XA_HEADS):
            hs = slice(h * XA_HEAD_DIM, (h + 1) * XA_HEAD_DIM)
            o_ref[:, hs] = o[h]
            lse_ref[:, hs] = jnp.broadcast_to(lse[h], (tq, XA_HEAD_DIM))

    return _call(body, name=name, grid=(L // tq,), in_specs=[row, kvs], out_specs=[row, row],
                 out_shape=[jax.ShapeDtypeStruct((L, W), BF16), jax.ShapeDtypeStruct((L, W), F32)], sem=("parallel",))(q, kv)


def _xattn_bwd(q, kv, o, lse, do, *, name, tq=1024):
    L, W = q.shape
    scale = XA_HEAD_DIM ** -0.5
    row = pl.BlockSpec((tq, W), lambda i: (i, 0))
    kvs = pl.BlockSpec((N_MEM, 2 * W), lambda i: (0, 0))

    def body(q_ref, kv_ref, o_ref, lse_ref, do_ref, dq_ref, dkv_ref):
        @pl.when(pl.program_id(0) == 0)
        def _():
            dkv_ref[...] = jnp.zeros_like(dkv_ref)

        split = lambda x: _split_heads(x, XA_HEADS, XA_HEAD_DIM)
        kv = kv_ref[...]
        q3, k3, v3, do3 = split(q_ref[...]), split(kv[:, :W]), split(kv[:, W:]), split(do_ref[...])
        lse = jnp.stack([lse_ref[:, h * XA_HEAD_DIM:h * XA_HEAD_DIM + 1] for h in range(XA_HEADS)], axis=0)
        p = jnp.exp(_bdot(q3, k3, BNT) * scale - lse)
        dl = jnp.sum(do3.astype(F32) * split(o_ref[...]).astype(F32), axis=-1, keepdims=True)
        ds = (p * (_bdot(do3, v3, BNT) - dl) * scale).astype(BF16)
        dq = _bdot(ds, k3, BNN).astype(BF16)
        dk, dv = _bdot(ds, q3, BTN), _bdot(p.astype(BF16), do3, BTN)
        for h in range(XA_HEADS):
            hs = slice(h * XA_HEAD_DIM, (h + 1) * XA_HEAD_DIM)
            vs = slice(W + h * XA_HEAD_DIM, W + (h + 1) * XA_HEAD_DIM)
            dq_ref[:, hs] = dq[h]
            dkv_ref[:, hs] += dk[h]
            dkv_ref[:, vs] += dv[h]

    return _call(body, name=name, grid=(L // tq,), in_specs=[row, kvs, row, row, row], out_specs=[row, kvs],
                 out_shape=[jax.ShapeDtypeStruct((L, W), BF16), jax.ShapeDtypeStruct((N_MEM, 2 * W), F32)],
                 sem=("arbitrary",))(q, kv, o, lse, do)


def _neg_expm1(z):
    series = -(z * (1.0 + z * (0.5 + z * (1.0 / 6.0 + z * (1.0 / 24.0 + z * (1.0 / 120.0))))))
    return jnp.where(z > -0.05, series, 1.0 - jnp.exp(z))


def _softplus(z):
    return jnp.maximum(z, 0.0) + jnp.log(1.0 + jnp.exp(-jnp.abs(z)))


def _gelu_parts(y):
    c = 0.7978845608028654
    t = jnp.tanh(c * (y + 0.044715 * y * y * y))
    gy = 0.5 * y * (1.0 + t)
    dgy = 0.5 * (1.0 + t) + 0.5 * y * (1.0 - t * t) * c * (1.0 + 3.0 * 0.044715 * y * y)
    return gy, dgy


def _lru_gates(xc, wa_ref, ba, wx_ref, bx, sp):
    rs, igs = [], []
    for hd in range(LRU_HEADS):
        sl = slice(hd * LRU_HEAD_DIM, (hd + 1) * LRU_HEAD_DIM)
        xh = xc[:, sl].astype(BF16)
        rs.append(_sigmoid(_dot(xh, wa_ref[hd]) + ba[:, sl]))
        igs.append(_sigmoid(_dot(xh, wx_ref[hd]) + bx[:, sl]))
    r, ig = jnp.concatenate(rs, axis=1), jnp.concatenate(igs, axis=1)
    la = -LRU_C * r * sp
    return r, ig, jnp.exp(la), _neg_expm1(2.0 * la)


def _conv_taps(x_ext, halo):
    n = x_ext.shape[0]
    return [x_ext[halo:] if k == CONV_WIDTH - 1 else pltpu.roll(x_ext, CONV_WIDTH - 1 - k, 0)[halo:]
            for k in range(CONV_WIDTH)]


def _lru_fwd(proj, cw, cb, wa, ba, wx, bx, lam, *, name, tc=512):
    L = proj.shape[0]
    W = LRU_HEADS * LRU_HEAD_DIM
    nb = L // tc
    whole = lambda shape: pl.BlockSpec(shape, lambda i: (0,) * len(shape))
    specs = [pl.BlockSpec((tc, W), lambda i: (i, 0)), pl.BlockSpec((tc, W), lambda i: (i, 1)),
             pl.BlockSpec((16, W), lambda i: (jnp.maximum(i * (tc // 16) - 1, 0), 0)),
             whole((CONV_WIDTH, W)), whole((1, W)), whole((LRU_HEADS, LRU_HEAD_DIM, LRU_HEAD_DIM)), whole((1, W)),
             whole((LRU_HEADS, LRU_HEAD_DIM, LRU_HEAD_DIM)), whole((1, W)), whole((1, W))]
    out_specs = [pl.BlockSpec((tc, W), lambda i: (i, 0))] * 2
    out_shape = [jax.ShapeDtypeStruct((L, W), BF16), jax.ShapeDtypeStruct((L, W), F32)]

    def body(x_ref, y_ref, xh_ref, cw_ref, cb_ref, wa_ref, ba_ref, wx_ref, bx_ref, lam_ref, rec_ref, hs_ref,
             hcar, a_scr, b_scr):
        i = pl.program_id(0)

        @pl.when(i == 0)
        def _():
            hcar[...] = jnp.zeros_like(hcar)

        halo = jnp.where(i > 0, xh_ref[...].astype(F32), 0.0)
        taps = _conv_taps(jnp.concatenate([halo, x_ref[...].astype(F32)], axis=0), 16)
        xc = cb_ref[...] + sum(cw_ref[k:k + 1, :] * taps[k] for k in range(CONV_WIDTH))
        _, ig, a, om = _lru_gates(xc, wa_ref, ba_ref[...], wx_ref, bx_ref[...], _softplus(-lam_ref[...]))
        b = jnp.sqrt(om) * (ig * xc)
        rowmod = lax.broadcasted_iota(jnp.int32, (tc, W), 0) & 7
        for s in (1, 2, 4):
            keep = rowmod >= s
            b = jnp.where(keep, a * pltpu.roll(b, s, 0) + b, b)
            a = jnp.where(keep, a * pltpu.roll(a, s, 0), a)
        a_scr[...] = a
        b_scr[...] = b

        def tile(j, hc):
            rows = pl.ds(pl.multiple_of(j * 8, 8), 8)
            ht = a_scr[rows, :] * hc + b_scr[rows, :]
            hs_ref[rows, :] = ht
            return jnp.broadcast_to(ht[7:8, :], (8, W))

        hcar[...] = lax.fori_loop(0, tc // 8, tile, hcar[...])
        gy, _ = _gelu_parts(y_ref[...].astype(F32))
        rec_ref[...] = (hs_ref[...] * gy).astype(BF16)

    return _call(body, name=name, grid=(nb,), in_specs=specs, out_specs=out_specs, out_shape=out_shape,
                 scratch=[pltpu.VMEM((8, W), F32), pltpu.VMEM((tc, W), F32), pltpu.VMEM((tc, W), F32)],
                 sem=("arbitrary",))(proj, proj, proj, cw, cb, wa, ba, wx, bx, lam)


def _lru_bwd(proj, hs, drec_src, cw, cb, wa, ba, wx, bx, lam, *, name, tc=256):
    L = proj.shape[0]
    W = LRU_HEADS * LRU_HEAD_DIM
    nb = L // tc
    tb = lambda i: nb - 1 - i
    whole = lambda shape: pl.BlockSpec(shape, lambda i: (0,) * len(shape))
    gate_w = (LRU_HEADS, LRU_HEAD_DIM, LRU_HEAD_DIM)
    specs = [pl.BlockSpec((tc, W), lambda i: (tb(i), 0)), pl.BlockSpec((tc, W), lambda i: (tb(i), 1)),
             pl.BlockSpec((16, W), lambda i: (jnp.maximum(tb(i) * (tc // 16) - 1, 0), 0)),
             pl.BlockSpec((tc, W), lambda i: (tb(i), 0)),
             pl.BlockSpec((8, W), lambda i: (jnp.maximum(tb(i) * (tc // 8) - 1, 0), 0)),
             pl.BlockSpec((tc, W), lambda i: (tb(i), 0)),
             whole((CONV_WIDTH, W)), whole((1, W)), whole(gate_w), whole((1, W)), whole(gate_w), whole((1, W)), whole((1, W))]
    out_specs = [pl.BlockSpec((tc, 2 * W), lambda i: (tb(i), 0)), whole((CONV_WIDTH, W)), whole((1, W)), whole(gate_w),
                 whole((1, W)), whole(gate_w), whole((1, W)), whole((1, W))]
    vec = jax.ShapeDtypeStruct((1, W), F32)
    out_shape = [jax.ShapeDtypeStruct((L, 2 * W), BF16), jax.ShapeDtypeStruct((CONV_WIDTH, W), F32), vec,
                 jax.ShapeDtypeStruct(gate_w, F32), vec, jax.ShapeDtypeStruct(gate_w, F32), vec, vec]

    def body(x_ref, y_ref, xh_ref, hs_ref, hh_ref, dr_ref, cw_ref, cb_ref, wa_ref, ba_ref, wx_ref, bx_ref, lam_ref,
             dxy_ref, dcw_ref, dcb_ref, dwa_ref, dba_ref, dwx_ref, dbx_ref, dlam_ref, gcar, dxc_car, a_scr, b_scr, g_scr):
        pid = pl.program_id(0)
        t = tb(pid)
        accs = (dcw_ref, dcb_ref, dwa_ref, dba_ref, dwx_ref, dbx_ref, dlam_ref)

        @pl.when(pid == 0)
        def _():
            gcar[...] = jnp.zeros_like(gcar)
            dxc_car[...] = jnp.zeros_like(dxc_car)
            for r in accs:
                r[...] = jnp.zeros_like(r)

        halo = jnp.where(t > 0, xh_ref[...].astype(F32), 0.0)
        taps = _conv_taps(jnp.concatenate([halo, x_ref[...].astype(F32)], axis=0), 16)
        xc = cb_ref[...] + sum(cw_ref[k:k + 1, :] * taps[k] for k in range(CONV_WIDTH))
        lam = lam_ref[...]
        sp = _softplus(-lam)
        r, ig, a, om = _lru_gates(xc, wa_ref, ba_ref[...], wx_ref, bx_ref[...], sp)
        sq = jnp.sqrt(om)
        hblk = hs_ref[...]
        hprev = pltpu.roll(jnp.concatenate([jnp.where(t > 0, hh_ref[...], 0.0), hblk], axis=0), 1, 0)[8:]
        gy, dgy = _gelu_parts(y_ref[...].astype(F32))
        drec = dr_ref[...].astype(F32)
        dxy_ref[:, W:] = (drec * hblk * dgy).astype(BF16)

        rowidx = lax.broadcasted_iota(jnp.int32, (tc, W), 0)
        rowmod = rowidx & 7
        ca = jnp.where(rowidx == tc - 1, 1.0, pltpu.roll(a, tc - 1, 0))
        cbv = drec * gy
        for s in (1, 2, 4):
            keep = rowmod < 8 - s
            cbv = jnp.where(keep, ca * pltpu.roll(cbv, tc - s, 0) + cbv, cbv)
            ca = jnp.where(keep, ca * pltpu.roll(ca, tc - s, 0), ca)
        a_scr[...] = ca
        b_scr[...] = cbv

        def tile(k, gc):
            j = tc // 8 - 1 - k
            rows = pl.ds(pl.multiple_of(j * 8, 8), 8)
            gt = a_scr[rows, :] * gc + b_scr[rows, :]
            g_scr[rows, :] = gt
            return jnp.broadcast_to(gt[0:1, :], (8, W))

        lax.fori_loop(0, tc // 8, tile, gcar[...])
        G = g_scr[...]
        gcar[...] = jnp.broadcast_to(a[0:1, :] * G[0:1, :], (8, W))

        da = G * hprev
        dsq = G * (ig * xc)
        di = G * (sq * xc)
        dxc = G * (sq * ig)
        dla = da * a - 2.0 * a * a * (dsq * 0.5 * lax.rsqrt(om))
        dlam_ref[...] += jnp.sum(dla * (-LRU_C * r), axis=0, keepdims=True) * (-_sigmoid(-lam))
        dpr = dla * (-LRU_C * sp) * r * (1.0 - r)
        dpi = di * ig * (1.0 - ig)
        dba_ref[...] += jnp.sum(dpr, axis=0, keepdims=True)
        dbx_ref[...] += jnp.sum(dpi, axis=0, keepdims=True)
        back = []
        for hd in range(LRU_HEADS):
            sl = slice(hd * LRU_HEAD_DIM, (hd + 1) * LRU_HEAD_DIM)
            xh, dprh, dpih = xc[:, sl].astype(BF16), dpr[:, sl].astype(BF16), dpi[:, sl].astype(BF16)
            back.append(_dot(dprh, wa_ref[hd], NT) + _dot(dpih, wx_ref[hd], NT))
            dwa_ref[hd] += _dot(xh, dprh, TN)
            dwx_ref[hd] += _dot(xh, dpih, TN)
        dxc = dxc + jnp.concatenate(back, axis=1)
        dcb_ref[...] += jnp.sum(dxc, axis=0, keepdims=True)
        for k in range(CONV_WIDTH):
            dcw_ref[k:k + 1, :] += jnp.sum(dxc * taps[k], axis=0, keepdims=True)
        ext = jnp.concatenate([dxc, dxc_car[...]], axis=0)
        dx = cw_ref[CONV_WIDTH - 1:CONV_WIDTH, :] * dxc
        for k in range(CONV_WIDTH - 1):
            dx = dx + cw_ref[k:k + 1, :] * pltpu.roll(ext, tc + 8 - (CONV_WIDTH - 1 - k), 0)[:tc]
        dxc_car[...] = dxc[0:8, :]
        dxy_ref[:, :W] = dx.astype(BF16)

    scratch = [pltpu.VMEM((8, W), F32), pltpu.VMEM((8, W), F32)] + [pltpu.VMEM((tc, W), F32)] * 3
    return _call(body, name=name, grid=(nb,), in_specs=specs, out_specs=out_specs, out_shape=out_shape, scratch=scratch,
                 sem=("arbitrary",))(proj, proj, proj, hs, hs, drec_src, cw, cb, wa, ba, wx, bx, lam)


def _final_loss(h, gain, target, *, name, tm=512):
    M, K = h.shape
    row = pl.BlockSpec((tm, K), lambda i: (i, 0))
    vec = pl.BlockSpec((1, K), lambda i: (0, 0))
    one = pl.BlockSpec((1, 128), lambda i: (0, 0))

    def body(h_ref, g_ref, t_ref, dh_ref, dg_ref, loss_ref):
        @pl.when(pl.program_id(0) == 0)
        def _():
            dg_ref[...] = jnp.zeros_like(dg_ref)
            loss_ref[...] = jnp.zeros_like(loss_ref)

        x = h_ref[...]
        r = lax.rsqrt(jnp.mean(x * x, axis=-1, keepdims=True) + NORM_EPS)
        xhat = x * r
        err = xhat * g_ref[...] - t_ref[...]
        loss_ref[...] += 0.5 / K * jnp.sum(err * err)
        dy = err * (1.0 / K)
        dg_ref[...] += jnp.sum(dy * xhat, axis=0, keepdims=True)
        dxh = dy * g_ref[...]
        dh_ref[...] = r * (dxh - xhat * jnp.mean(dxh * xhat, axis=-1, keepdims=True))

    return _call(body, name=name, grid=(M // tm,), in_specs=[row, vec, row], out_specs=[row, vec, one],
                 out_shape=[jax.ShapeDtypeStruct((M, K), F32), jax.ShapeDtypeStruct((1, K), F32),
                            jax.ShapeDtypeStruct((1, 128), F32)], sem=("arbitrary",))(h, gain.reshape(1, K), target)


def _dilated_merge(branches, *, name, tm=512):
    L, W = branches[0].shape
    nbr = len(branches) // 2
    row = pl.BlockSpec((tm, W), lambda i: (i, 0))

    def body(*refs):
        o_ref, lse_ref = refs[-2], refs[-1]
        lses = [refs[2 * b + 1][...] for b in range(nbr)]
        m = lses[0]
        for t in lses[1:]:
            m = jnp.maximum(m, t)
        ws = [jnp.exp(t - m) for t in lses]
        den = ws[0]
        for t in ws[1:]:
            den = den + t
        acc = ws[0] * refs[0][...].astype(F32)
        for b in range(1, nbr):
            acc = acc + ws[b] * refs[2 * b][...].astype(F32)
        o_ref[...] = (acc / den).astype(BF16)
        lse_ref[...] = m + jnp.log(den)

    return _call(body, name=name, grid=(L // tm,), in_specs=[row] * (2 * nbr), out_specs=[row, row],
                 out_shape=[jax.ShapeDtypeStruct((L, W), BF16), jax.ShapeDtypeStruct((L, W), F32)], sem=("parallel",))(*branches)


def _dilated_fwd(proj0):
    L = proj0.shape[0]
    qkv = proj0[:, 2 * D_MODEL:]
    W = B_HEADS * HEAD_DIM
    outs = []
    for window, d in DILATED_PATTERN:
        view = qkv.reshape(L // d, d * 3 * W)
        o, lse = _band_fwd(view, view, view, d=d, nq=B_HEADS, nkv=B_HEADS, qcol=lambda r: 3 * r, kcol=lambda r: 3 * r + 1,
                           vcol=lambda r: 3 * r + 2, max_dist=window // d, fused=True, name=f"dilated_fwd_d{d}")
        outs += [o.reshape(L, W), lse.reshape(L, W)]
    return _dilated_merge(outs, name="dilated_merge")


def _dilated_bwd(proj0, att, lse, datt, tabs):
    L = proj0.shape[0]
    qkv = proj0[:, 2 * D_MODEL:]
    Wh = B_HEADS * HEAD_DIM
    branches = []
    for window, d in DILATED_PATTERN:
        view = qkv.reshape(L // d, d * 3 * Wh)
        v1 = lambda t: t.reshape(L // d, d * Wh)
        outs = _band_bwd(view, view, view, v1(datt), v1(att), v1(lse), d=d, nq=B_HEADS, nkv=B_HEADS,
                         qcol=lambda r: 3 * r, kcol=lambda r: 3 * r + 1, vcol=lambda r: 3 * r + 2, docol=lambda r: r,
                         max_dist=window // d, fused=True, name=f"dilated_bwd_d{d}")
        branches.append([o.reshape(L, Wh) for o in outs])
    return _attn_grad_combine(branches, tabs, name="dilated_grad_combine")


def _device_step(x, mem, target, w, on_grads=None):
    L = x.shape[0]
    tabs = _rope_tables(L)
    g = {}
    saved = []
    h = x
    for layer in range(2):
        sv = {"h_mix": h}
        if layer == 0:
            proj, n = _rowmm(h, w["ab_w_in"], name="l0_in_proj", gain=w["mix_norm"][0],
                             rope=(2 * D_MODEL, 2 * D_MODEL + 2 * B_HEADS * HEAD_DIM, tabs))
            rec, hs = _lru_fwd(proj, w["lru_conv_w"], w["lru_conv_b"], w["lru_wa"], w["lru_ba"], w["lru_wx"], w["lru_bx"],
                               w["lru_lambda"], name="lru_fwd")
            att, lse = _dilated_fwd(proj)
            mix = jnp.concatenate([rec, att], axis=1)
            (h,) = _rowmm(mix, w["ab_w_out"], name="l0_out_proj", res=h)
            sv.update(hs=hs)
        else:
            proj, n = _rowmm(h, w["c_w_qkv"], name="l1_qkv_proj", gain=w["mix_norm"][1], bias=w["c_b_qkv"],
                             rope=(0, (C_HEADS + C_KV_HEADS) * HEAD_DIM, tabs))
            mix, lse = _band_fwd(proj, proj, proj, d=1, nq=C_HEADS, nkv=C_KV_HEADS, qcol=lambda r: 0, kcol=lambda r: 8,
                                 vcol=lambda r: 9, max_dist=C_WINDOW - 1, sinks=w["c_sinks"], name="swa_fwd")
            (h,) = _rowmm(mix, w["c_w_out"], name="l1_out_proj", res=h, bias=w["c_b_out"])
        sv.update(proj=proj, n_mix=n, mix=mix, lse=lse, h_xa=h)
        xq, nx = _rowmm(h, w["xa_wq"][layer][None], name=f"xa_q_proj{layer}", gain=w["xa_norm"][layer])
        kv, nm = _rowmm(mem, w["xa_wkv"][layer][None], name=f"xa_kv_proj{layer}", gain=w["xa_mem_norm"][layer])
        xo, xlse = _xattn_fwd(xq, kv, name=f"xa_fwd{layer}")
        (h,) = _rowmm(xo, w["xa_wo"][layer], name=f"xa_out_proj{layer}", res=h)
        sv.update(xq=xq, nx=nx, kv=kv, nm=nm, xo=xo, xlse=xlse, h_ffn=h)
        gu, nf, act = _rowmm(h, w["ffn_w_gate_up"], layer=layer, name=f"ffn_in{layer}", gain=w["ffn_norm"][layer], swiglu=True)
        (h,) = _rowmm(act, w["ffn_w_down"][layer][None], name=f"ffn_out{layer}", res=h, tm=512)
        sv.update(gu=gu, nf=nf, act=act)
        saved.append(sv)

    dh, g["final_norm"], loss = _final_loss(h, w["final_norm"], target, name="final_loss")

    stk = {k: [None, None] for k in ("xa_norm", "xa_mem_norm", "ffn_norm", "mix_norm")}
    after = None
    for layer in (1, 0):
        sv = saved[layer]
        (g["ffn_w_down", layer],) = _mm_tn(sv["act"], dh, S=1, name=f"ffn_down_dw{layer}", kk=D_FF // 2)
        (dgu,) = _mm_nt(dh, w["ffn_w_down"][layer][None], name=f"ffn_dact{layer}", mode="swiglu", kchunk=D_FF // 2, gu=sv["gu"],
                        after=after)
        (g["ffn_w_gate_up", layer],) = _mm_tn(sv["nf"], dgu, S=N_CHIPS, name=f"ffn_gu_dw{layer}")
        dh, stk["ffn_norm"][layer] = _mm_nt(dgu, w["ffn_w_gate_up"], layer=layer, name=f"ffn_dx{layer}", mode="norm",
                                            h=sv["h_ffn"], gain=w["ffn_norm"][layer], dh=dh)
        (g["xa_wo", layer],) = _mm_tn(sv["xo"], dh, S=N_CHIPS, name=f"xa_wo_dw{layer}")
        (dxo,) = _mm_nt(dh, w["xa_wo"][layer], name=f"xa_dxo{layer}", mode="plain")
        dxq, dkv = _xattn_bwd(sv["xq"], sv["kv"], sv["xo"], sv["xlse"], dxo, name=f"xa_bwd{layer}")
        (g["xa_wq", layer],) = _mm_tn(sv["nx"], dxq, S=1, name=f"xa_wq_dw{layer}")
        dh, stk["xa_norm"][layer] = _mm_nt(dxq, w["xa_wq"][layer][None], name=f"xa_dx{layer}", mode="norm", h=sv["h_xa"],
                                           gain=w["xa_norm"][layer], dh=dh)
        (g["xa_wkv", layer],) = _mm_tn(sv["nm"], dkv, S=1, name=f"xa_wkv_dw{layer}")
        _, stk["xa_mem_norm"][layer] = _mm_nt(dkv, w["xa_wkv"][layer][None], name=f"xa_dmem{layer}", mode="norm", h=mem,
                                              gain=w["xa_mem_norm"][layer])
        if layer == 1:
            g["c_w_out"], g["c_b_out"] = _mm_tn(sv["mix"], dh, S=1, name="l1_out_dw", bias=True)
            (dmix,) = _mm_nt(dh, w["c_w_out"], name="l1_dmix", mode="plain")
            dq, dk, dv, dsk = _band_bwd(sv["proj"], sv["proj"], sv["proj"], dmix, sv["mix"], sv["lse"], d=1, nq=C_HEADS,
                                        nkv=C_KV_HEADS, qcol=lambda r: 0, kcol=lambda r: 8, vcol=lambda r: 9,
                                        docol=lambda r: 0, max_dist=C_WINDOW - 1, sinks=w["c_sinks"], name="swa_bwd")
            g["c_sinks"] = dsk[0, :C_HEADS]
            dproj = _attn_grad_combine([(dq, dk, dv)], tabs, name="swa_grad_combine")
            g["c_w_qkv"], g["c_b_qkv"] = _mm_tn(sv["n_mix"], dproj, S=1, name="l1_qkv_dw", bias=True)
            dh, stk["mix_norm"][1] = _mm_nt(dproj, w["c_w_qkv"], name="l1_dx", mode="norm", h=sv["h_mix"],
                                            gain=w["mix_norm"][1], dh=dh)
            if on_grads is not None:
                after = on_grads("layer1", g)
        else:
            if on_grads is not None:
                after = on_grads("layer0_ffn_xa", g)
            (g["ab_w_out"],) = _mm_tn(sv["mix"], dh, S=1, name="l0_out_dw", kk=768)
            (dmix,) = _mm_nt(dh, w["ab_w_out"], name="l0_dmix", mode="plain", kchunk=768, after=after)
            (dxy, g["lru_conv_w"], g["lru_conv_b"], g["lru_wa"], g["lru_ba"], g["lru_wx"], g["lru_bx"],
             g["lru_lambda"]) = _lru_bwd(sv["proj"], sv["hs"], dmix, w["lru_conv_w"], w["lru_conv_b"], w["lru_wa"],
                                         w["lru_ba"], w["lru_wx"], w["lru_bx"], w["lru_lambda"], name="lru_bwd")
            dqkv = _dilated_bwd(sv["proj"], sv["mix"][:, D_MODEL:], sv["lse"], dmix[:, D_MODEL:], tabs)
            dproj = jnp.concatenate([dxy, dqkv], axis=1)
            (g["ab_w_in"],) = _mm_tn(sv["n_mix"], dproj, S=N_CHIPS, name="l0_in_dw")
            dh, stk["mix_norm"][0] = _mm_nt(dproj, w["ab_w_in"], name="l0_dx", mode="norm", h=sv["h_mix"],
                                            gain=w["mix_norm"][0], dh=dh)
    for k, v in stk.items():
        g[k] = jnp.concatenate(v, axis=0)
    return loss[0, 0], dh, g


ANY = pl.BlockSpec(memory_space=pl.ANY)
MESH = pl.DeviceIdType.MESH


def _place():
    x, y, c = lax.axis_index("x"), lax.axis_index("y"), lax.axis_index("c")
    return x, y, c, [(1 - x, y), (x, 1 - y), (1 - x, 1 - y)]


def _remote(send_sems, recv_sems):
    def copy(k, src, dst, to):
        return pltpu.make_async_remote_copy(src_ref=src, dst_ref=dst, send_sem=send_sems.at[k], recv_sem=recv_sems.at[k],
                                            device_id=to, device_id_type=MESH)
    return copy


def _halves(ref, n_rows):
    rh = n_rows // 2
    return lambda lead, hh: ref.at[(*lead, pl.ds(hh * rh, rh), slice(None))]


def _gather_weights(packs, spack):
    n = len(packs)

    def body(*refs):
        w_refs, s_ref, wf_refs, sf_ref = refs[:n], refs[n], refs[n + 1:2 * n + 1], refs[2 * n + 1]
        x, y, c, chips = _place()
        me, sib = 2 * x + y, (x, y, 1 - c)
        copy = _remote(*refs[-2:])
        src = [_halves(w_refs[g], packs[g].shape[0]) for g in range(n)]
        dst = [_halves(wf_refs[g], packs[g].shape[0]) for g in range(n)]
        sends = []
        for g in range(n):
            for j, (cx, cy) in enumerate(chips):
                sends.append(copy(3 * g + j, src[g]((), c), dst[g]((me,), c), (cx, cy, c)))
        for j, (cx, cy) in enumerate(chips):
            sends.append(copy(6 * n + j, s_ref, sf_ref.at[me], (cx, cy, c)))
        for cp in sends:
            cp.start()
        for g in range(n):
            for j, (cx, cy) in enumerate(chips):
                got = dst[g]((2 * cx + cy,), c)
                copy(3 * g + j, got, got, sib).wait_recv()
                fwd = copy(3 * n + 3 * g + j, got, got, sib)
                fwd.start()
                sends.append(fwd)
        for g in range(n):
            for j, (cx, cy) in enumerate(chips):
                got = dst[g]((2 * cx + cy,), 1 - c)
                copy(3 * n + 3 * g + j, got, got, sib).wait_recv()
        for j, (cx, cy) in enumerate(chips):
            copy(6 * n + j, s_ref, sf_ref.at[2 * cx + cy], sib).wait_recv()
        for cp in sends:
            cp.wait_send()

    ins = list(packs) + [spack]
    out_shape = [jax.ShapeDtypeStruct((N_CHIPS,) + a.shape, a.dtype) for a in ins]
    n_sems = 6 * n + 3
    outs = pl.pallas_call(body, name="gather_weights", out_shape=out_shape, in_specs=[ANY] * len(ins),
                          out_specs=[ANY] * len(ins),
                          scratch_shapes=[pltpu.SemaphoreType.DMA((n_sems,)), pltpu.SemaphoreType.DMA((n_sems,))])(*ins)
    chip = 2 * lax.axis_index("x") + lax.axis_index("y")
    outs = [lax.dynamic_update_index_in_dim(o, a, chip, 0) for o, a in zip(outs, ins)]
    return outs[:n], outs[n]


SEQUENCER_GATHER_IDS = {"mid": 1, "late": 5}


def _gather_weights_behind(packs, *, tag):
    n = len(packs)

    def body(*refs):
        w_refs, wf_refs = refs[:n], refs[n:2 * n]
        x, y, c, chips = _place()
        me, sib = 2 * x + y, (x, y, 1 - c)
        barrier = pltpu.get_barrier_semaphore()
        for peer in [(cx, cy, c) for cx, cy in chips] + [sib]:
            pl.semaphore_signal(barrier, inc=1, device_id=peer, device_id_type=MESH)
        pl.semaphore_wait(barrier, len(chips) + 1)
        copy = _remote(*refs[-2:])
        src = [_halves(w_refs[g], packs[g].shape[0]) for g in range(n)]
        dst = [_halves(wf_refs[g], packs[g].shape[0]) for g in range(n)]
        sends = []
        for g in range(n):
            for j, (cx, cy) in enumerate(chips):
                sends.append(copy(3 * g + j, src[g]((), c), dst[g]((me,), c), (cx, cy, c)))
        for cp in sends:
            cp.start()
        for g in range(n):
            for j, (cx, cy) in enumerate(chips):
                got = dst[g]((2 * cx + cy,), c)
                copy(3 * g + j, got, got, sib).wait_recv()
                fwd = copy(3 * n + 3 * g + j, got, got, sib)
                fwd.start()
                sends.append(fwd)
        for g in range(n):
            for j, (cx, cy) in enumerate(chips):
                got = dst[g]((2 * cx + cy,), 1 - c)
                copy(3 * n + 3 * g + j, got, got, sib).wait_recv()
        for cp in sends:
            cp.wait_send()

    out_type = [jax.ShapeDtypeStruct((N_CHIPS,) + a.shape, a.dtype) for a in packs]
    outs = pl.kernel(body, out_type=out_type, mesh=plsc.ScalarSubcoreMesh(axis_name="sequencer", num_cores=1),
                     name="gather_weights_behind_" + tag,
                     scratch_types=[pltpu.SemaphoreType.DMA((6 * n,)), pltpu.SemaphoreType.DMA((6 * n,))],
                     compiler_params=pltpu.CompilerParams(collective_id=SEQUENCER_GATHER_IDS[tag]))(*packs)
    chip = 2 * lax.axis_index("x") + lax.axis_index("y")
    return [lax.dynamic_update_index_in_dim(o, a, chip, 0) for o, a in zip(outs, packs)]


def _rs_pair_exchange(gpacks, *, name):
    n = len(gpacks)

    def body(*refs):
        g_refs, ra_refs = refs[:n], refs[n:2 * n]
        x, y, c, _ = _place()
        copy = _remote(*refs[-2:])
        cps = []
        for g in range(n):
            half = _halves(g_refs[g], gpacks[g].shape[1])
            cps += [copy(N_CHIPS * g + j, half((j,), 1 - c), ra_refs[g].at[j], (x, y, 1 - c)) for j in range(N_CHIPS)]
        for cp in cps:
            cp.start()
        for cp in cps:
            cp.wait()

    out_shape = [jax.ShapeDtypeStruct((N_CHIPS, a.shape[1] // 2, a.shape[2]), a.dtype) for a in gpacks]
    n_sems = N_CHIPS * n
    return pl.pallas_call(body, name=name, out_shape=out_shape, in_specs=[ANY] * n, out_specs=[ANY] * n,
                          scratch_shapes=[pltpu.SemaphoreType.DMA((n_sems,)), pltpu.SemaphoreType.DMA((n_sems,))])(*gpacks)


def _row_tile(rows, cap=512):
    return max(t for t in range(16, min(rows, cap) + 1, 16) if rows % t == 0)


def _rs_pair_add(place, gpack, ra, *, name):
    _, R, C = gpack.shape
    Rh = R // 2
    tr = _row_tile(Rh)
    nrb = Rh // tr

    def body(p_ref, g_ref, ra_ref, pair_ref, own_ref):
        s = g_ref[...].astype(F32) + ra_ref[...].astype(F32)
        pair_ref[...] = s.astype(BF16)

        @pl.when(pl.program_id(1) == p_ref[1])
        def _():
            own_ref[...] = s

    grid_spec = pltpu.PrefetchScalarGridSpec(
        num_scalar_prefetch=1, grid=(nrb, N_CHIPS),
        in_specs=[pl.BlockSpec((None, tr, C), lambda i, j, p: (j, p[0] * nrb + i, 0)),
                  pl.BlockSpec((None, tr, C), lambda i, j, p: (j, i, 0))],
        out_specs=[pl.BlockSpec((None, tr, C), lambda i, j, p: (j, i, 0)), pl.BlockSpec((tr, C), lambda i, j, p: (i, 0))])
    return pl.pallas_call(
        body, name=name, grid_spec=grid_spec,
        out_shape=[jax.ShapeDtypeStruct((N_CHIPS, Rh, C), BF16), jax.ShapeDtypeStruct((Rh, C), F32)],
        compiler_params=pltpu.CompilerParams(dimension_semantics=("arbitrary", "arbitrary"),
                                             vmem_limit_bytes=VMEM_LIMIT_V7X))(place, gpack, ra)


SEQUENCER_EXCHANGE_IDS = {"l1": 2, "l0a": 3, "l0b": 4}


def _rs_chip_exchange_behind(pairs, *, tag, small=None):
    n = len(pairs)
    has_small = small is not None

    def body(*refs):
        p_refs = refs[:n]
        s_ref = refs[n] if has_small else None
        rb_refs = refs[n + has_small:2 * n + has_small]
        rs_ref = refs[2 * n + 1] if has_small else None
        x, y, c, chips = _place()
        peers = [(1 - x if k & 4 else x, 1 - y if k & 2 else y, 1 - c if k & 1 else c) for k in range(1, 8)]
        shake = peers if has_small else [(cx, cy, c) for cx, cy in chips]
        barrier = pltpu.get_barrier_semaphore()
        for peer in shake:
            pl.semaphore_signal(barrier, inc=1, device_id=peer, device_id_type=MESH)
        pl.semaphore_wait(barrier, len(shake))
        copy = _remote(*refs[-2:])
        cps = []
        for g in range(n):
            cps += [copy(3 * g + j, p_refs[g].at[2 * cx + cy], rb_refs[g].at[j], (cx, cy, c)) for j, (cx, cy) in enumerate(chips)]
        if has_small:
            dev = 4 * x + 2 * y + c
            cps += [copy(3 * n + k, s_ref, rs_ref.at[dev], peer) for k, peer in enumerate(peers)]
        for cp in cps:
            cp.start()
        for g in range(n):
            for j in range(3):
                copy(3 * g + j, p_refs[g].at[0], rb_refs[g].at[j], (x, y, c)).wait_recv()
        if has_small:
            for k, (px, py, pc) in enumerate(peers):
                copy(3 * n + k, s_ref, rs_ref.at[4 * px + 2 * py + pc], (x, y, c)).wait_recv()
        for cp in cps:
            cp.wait_send()

    ins = list(pairs) + ([small] if has_small else [])
    out_type = [jax.ShapeDtypeStruct((3,) + p.shape[1:], p.dtype) for p in pairs]
    if has_small:
        out_type.append(jax.ShapeDtypeStruct((8,) + small.shape, small.dtype))
    n_sems = 3 * n + 7 * has_small
    outs = pl.kernel(body, out_type=out_type, mesh=plsc.ScalarSubcoreMesh(axis_name="sequencer", num_cores=1),
                     name="rs_chip_exchange_behind_" + tag,
                     scratch_types=[pltpu.SemaphoreType.DMA((n_sems,)), pltpu.SemaphoreType.DMA((n_sems,))],
                     compiler_params=pltpu.CompilerParams(collective_id=SEQUENCER_EXCHANGE_IDS[tag]))(*ins)
    if has_small:
        dev = 4 * lax.axis_index("x") + 2 * lax.axis_index("y") + lax.axis_index("c")
        outs = list(outs[:n]) + [lax.dynamic_update_index_in_dim(outs[n], small, dev, 0)]
    return outs


def _rs_final_add(place, own, rb, *, name):
    Rh, C = own.shape
    tr = _row_tile(Rh)
    nrb = Rh // tr

    def body(p_ref, o_ref, rb_ref, f_ref):
        f_ref[...] = ((o_ref[...] + rb_ref[0].astype(F32)) + rb_ref[1].astype(F32)) + rb_ref[2].astype(F32)

    grid_spec = pltpu.PrefetchScalarGridSpec(
        num_scalar_prefetch=1, grid=(nrb,),
        in_specs=[pl.BlockSpec((tr, C), lambda i, p: (i, 0)), pl.BlockSpec((3, tr, C), lambda i, p: (0, i, 0))],
        out_specs=pl.BlockSpec((tr, C), lambda i, p: (p[0] * nrb + i, 0)))
    return pl.pallas_call(
        body, name=name, grid_spec=grid_spec, out_shape=jax.ShapeDtypeStruct((2 * Rh, C), F32),
        compiler_params=pltpu.CompilerParams(dimension_semantics=("arbitrary",), vmem_limit_bytes=VMEM_LIMIT_V7X))(place, own, rb)


def _sum_slots(rs):
    n, rows, C = rs.shape

    def body(r_ref, o_ref):
        acc = r_ref[0]
        for k in range(1, n):
            acc = acc + r_ref[k]
        o_ref[...] = acc

    return _call(body, name="small_grad_sum", grid=(1,), in_specs=[pl.BlockSpec((n, rows, C), lambda i: (0, 0, 0))],
                 out_specs=pl.BlockSpec((rows, C), lambda i: (0, 0)), out_shape=jax.ShapeDtypeStruct((rows, C), F32),
                 sem=("arbitrary",))(rs)


def _rs_sibling_share(gbufs, *, name):
    n = len(gbufs)

    def body(*refs):
        g_refs = refs[n:2 * n]
        x, y, c, _ = _place()
        copy = _remote(*refs[-2:])
        halves = [_halves(g_refs[g], gbufs[g].shape[0]) for g in range(n)]
        outs = [copy(g, halves[g]((), c), halves[g]((), c), (x, y, 1 - c)) for g in range(n)]
        for cp in outs:
            cp.start()
        for g in range(n):
            copy(g, halves[g]((), 1 - c), halves[g]((), 1 - c), (x, y, c)).wait_recv()
        for cp in outs:
            cp.wait_send()

    return pl.pallas_call(body, name=name, out_shape=[jax.ShapeDtypeStruct(a.shape, a.dtype) for a in gbufs],
                          in_specs=[ANY] * n, out_specs=[ANY] * n, input_output_aliases={g: g for g in range(n)},
                          scratch_shapes=[pltpu.SemaphoreType.DMA((n,)), pltpu.SemaphoreType.DMA((n,))])(*gbufs)


def _adamw(w, g, m, v, *, name, g_row=0):
    rows, cols = w.shape
    tr = rows
    for cand in range(min(rows, 512), 7, -8):
        if rows % cand == 0 and g_row % cand == 0:
            tr = cand
            break
    spec = pl.BlockSpec((tr, cols), lambda i: (i, 0))
    g_spec = pl.BlockSpec((tr, cols), lambda i: (g_row // tr + i, 0))

    def body(w_ref, g_ref, m_ref, v_ref, d_ref, nm_ref, nv_ref):
        gg = g_ref[...]
        nm = ADAM_B1 * m_ref[...] + (1.0 - ADAM_B1) * gg
        nv = ADAM_B2 * v_ref[...] + (1.0 - ADAM_B2) * (gg * gg)
        m_hat = nm / (1.0 - ADAM_B1 ** ADAM_STEP)
        v_hat = nv / (1.0 - ADAM_B2 ** ADAM_STEP)
        d_ref[...] = -ADAM_LR * (m_hat / (jnp.sqrt(v_hat) + ADAM_EPS) + ADAM_WD * w_ref[...])
        nm_ref[...] = nm
        nv_ref[...] = nv

    return _call(body, name=name, grid=(rows // tr,), in_specs=[spec, g_spec, spec, spec], out_specs=[spec] * 3,
                 out_shape=[jax.ShapeDtypeStruct((rows, cols), F32)] * 3, sem=("parallel",))(w, g, m, v)


WEIGHT_NAMES = ("mix_norm", "ab_w_in", "lru_conv_w", "lru_conv_b", "lru_wa", "lru_ba", "lru_wx", "lru_bx", "lru_lambda",
                "ab_w_out", "c_w_qkv", "c_b_qkv", "c_sinks", "c_w_out", "c_b_out", "xa_norm", "xa_mem_norm", "xa_wq",
                "xa_wkv", "xa_wo", "ffn_norm", "ffn_w_gate_up", "ffn_w_down", "final_norm")
EARLY_GROUPS = (("ab_w_in",),)
MID_GROUPS = (("ab_w_out",), ("lru_wa", "lru_wx"))
LATE_GROUPS = (("c_w_out", "xa_wkv", "ffn_w_down"), ("ffn_w_gate_up",), ("xa_wo",), ("xa_wq",), ("c_w_qkv",))
GROUPS = EARLY_GROUPS + MID_GROUPS + LATE_GROUPS
REPLICATED = ("mix_norm", "lru_conv_b", "lru_lambda", "c_sinks", "xa_norm", "xa_mem_norm", "ffn_norm", "final_norm")
SMALL_SHARDED = ("lru_conv_w", "lru_ba", "lru_bx", "c_b_qkv", "c_b_out")
LANES = 1024


def _rows(v):
    flat = v.reshape(-1)
    return jnp.pad(flat, (0, -flat.shape[0] % LANES)).reshape(-1, LANES)


def _pack_small(parts, total, *, name):
    def body(*refs):
        o_ref = refs[-1]
        o_ref[...] = jnp.zeros_like(o_ref)
        row = 0
        for p_ref in refs[:-1]:
            o_ref[row:row + p_ref.shape[0], :] = p_ref[...]
            row += p_ref.shape[0]

    return _call(body, name=name, grid=(1,), in_specs=[pl.BlockSpec(p.shape, lambda i: (0, 0)) for p in parts],
                 out_specs=pl.BlockSpec((total, LANES), lambda i: (0, 0)),
                 out_shape=jax.ShapeDtypeStruct((total, LANES), F32), sem=("arbitrary",))(*parts)


def _from_shards(name, t):
    minor = t.shape[-1]
    if name == "ab_w_in":
        return t
    if name in ("ab_w_out", "c_w_out"):
        return t.reshape(1, -1, minor)
    if name == "ffn_w_gate_up":
        return t.reshape(N_CHIPS, 2, -1, minor)
    if name in ("xa_wq", "xa_wkv", "ffn_w_down"):
        return t.reshape(N_CHIPS, 2, -1, minor).transpose(1, 0, 2, 3).reshape(2, -1, minor)
    if name in ("lru_wa", "lru_wx"):
        return t.reshape(N_CHIPS, LRU_HEADS, -1, minor).transpose(1, 0, 2, 3).reshape(LRU_HEADS, LRU_HEAD_DIM, minor)
    if name == "xa_wo":
        return t.reshape(N_CHIPS, 2, -1, minor).transpose(1, 0, 2, 3)
    assert name == "c_w_qkv"
    return t.transpose(1, 0, 2).reshape(1, D_MODEL, -1)


def _piece_shards(name, g):
    minor = g.shape[-1]
    if name in ("ab_w_in", "ffn_w_gate_up", "xa_wo"):
        return g
    if name in ("ab_w_out", "c_w_out", "xa_wq", "xa_wkv", "ffn_w_down"):
        return g.reshape(N_CHIPS, -1, minor)
    if name in ("lru_wa", "lru_wx"):
        return g.reshape(LRU_HEADS, N_CHIPS, -1, minor).transpose(1, 0, 2, 3).reshape(N_CHIPS, -1, minor)
    assert name == "c_w_qkv"
    return g.reshape(D_MODEL, N_CHIPS, -1).transpose(1, 0, 2)


RS_SETS = {
    "l1": ((("c_w_out", None), ("xa_wkv", 1), ("ffn_w_down", 1)), (("ffn_w_gate_up", 1),), (("xa_wo", 1),),
           (("xa_wq", 1),), (("c_w_qkv", None),)),
    "l0a": ((("xa_wkv", 0), ("ffn_w_down", 0)), (("ffn_w_gate_up", 0),), (("xa_wo", 0),), (("xa_wq", 0),)),
    "l0b": ((("ab_w_out", None),), (("ab_w_in", None),), (("lru_wa", None), ("lru_wx", None))),
}
RS_STAGE = {"layer1": "l1", "layer0_ffn_xa": "l0a"}


def kernel(x, mem, mix_norm, ab_w_in, lru_conv_w, lru_conv_b, lru_wa, lru_ba, lru_wx, lru_bx, lru_lambda, ab_w_out, c_w_qkv, c_b_qkv, c_sinks, c_w_out, c_b_out, xa_norm, xa_mem_norm, xa_wq, xa_wkv, xa_wo, ffn_norm, ffn_w_gate_up, ffn_w_down, final_norm, loss_target, m_mix_norm, m_ab_w_in, m_lru_conv_w, m_lru_conv_b, m_lru_wa, m_lru_ba, m_lru_wx, m_lru_bx, m_lru_lambda, m_ab_w_out, m_c_w_qkv, m_c_b_qkv, m_c_sinks, m_c_w_out, m_c_b_out, m_xa_norm, m_xa_mem_norm, m_xa_wq, m_xa_wkv, m_xa_wo, m_ffn_norm, m_ffn_w_gate_up, m_ffn_w_down, m_final_norm, v_mix_norm, v_ab_w_in, v_lru_conv_w, v_lru_conv_b, v_lru_wa, v_lru_ba, v_lru_wx, v_lru_bx, v_lru_lambda, v_ab_w_out, v_c_w_qkv, v_c_b_qkv, v_c_sinks, v_c_w_out, v_c_b_out, v_xa_norm, v_xa_mem_norm, v_xa_wq, v_xa_wkv, v_xa_wo, v_ffn_norm, v_ffn_w_gate_up, v_ffn_w_down, v_final_norm):
    given = dict(locals())
    wl = {n: given[n] for n in WEIGHT_NAMES}
    ml = {n: given["m_" + n] for n in WEIGHT_NAMES}
    vl = {n: given["v_" + n] for n in WEIGHT_NAMES}
    xi, yi, ci = lax.axis_index("x"), lax.axis_index("y"), lax.axis_index("c")
    chip = 2 * xi + yi

    def join(parts, axis):
        return parts[0] if len(parts) == 1 else jnp.concatenate(parts, axis=axis)

    local_rows = {n: wl[n].size // wl[n].shape[-1] for grp in GROUPS for n in grp}
    packs = [join([wl[n].astype(BF16).reshape(local_rows[n], wl[n].shape[-1]) for n in grp], 0) for grp in GROUPS]
    spack = _pack_small([_rows(wl[n]) for n in SMALL_SHARDED], 8, name="pack_small_weights")
    n_early, n_mid = len(EARLY_GROUPS), len(EARLY_GROUPS) + len(MID_GROUPS)
    early, sfull = _gather_weights(packs[:n_early], spack)
    early, sfull, mid_packs = lax.optimization_barrier((early, sfull, packs[n_early:n_mid]))
    mid = _gather_weights_behind(mid_packs, tag="mid")
    mid, late_packs = lax.optimization_barrier((mid, packs[n_mid:]))
    gathered = early + mid + _gather_weights_behind(late_packs, tag="late")
    w = {n: wl[n] for n in REPLICATED}
    w["c_sinks"] = wl["c_sinks"][0]
    for grp, full in zip(GROUPS, gathered):
        off = 0
        for n in grp:
            w[n] = _from_shards(n, full if len(grp) == 1 else full[:, off:off + local_rows[n]])
            off += local_rows[n]
    for r, n in enumerate(SMALL_SHARDED):
        loc = wl[n].shape[1:]
        t = sfull[:, r, :wl[n].size].reshape((N_CHIPS,) + loc)
        if n == "lru_conv_w":
            w[n] = t.transpose(1, 0, 2).reshape(CONV_WIDTH, -1)
        elif n in ("lru_ba", "lru_bx"):
            w[n] = t.transpose(1, 0, 2).reshape(1, -1)
        else:
            w[n] = t.reshape(1, -1)

    place = jnp.stack([ci, chip]).astype(jnp.int32)

    def pair_stage(spec, g, tag):
        piece = lambda n, l: (g[n] if l is None else g[n, l]).astype(BF16)
        gpacks = [join([_piece_shards(n, piece(n, l)) for n, l in grp], 1) for grp in spec]
        ras = _rs_pair_exchange(gpacks, name=f"rs_pair_exchange_{tag}")
        sums = [_rs_pair_add(place, gp, ra, name=f"rs_pair_add_{tag}_{i}") for i, (gp, ra) in enumerate(zip(gpacks, ras))]
        return [pair for pair, _ in sums], [own for _, own in sums]

    reduced, in_flight = [], []

    def take_up():
        done = [_rs_final_add(place, o, r, name=f"rs_final_add_{len(reduced) + i}") for i, (o, r) in enumerate(in_flight)]
        reduced.extend(done)
        in_flight.clear()
        return done

    def reduce_behind(stage, g):
        done = take_up()
        tag = RS_STAGE[stage]
        pairs, own = pair_stage(RS_SETS[tag], g, tag)
        in_flight.extend(zip(own, _rs_chip_exchange_behind(pairs, tag=tag)))
        return own + done

    loss_part, grad_x, g = _device_step(x[0], mem[0], loss_target[0], w, on_grads=reduce_behind)

    small_parts = [_rows(g[n]) for n in REPLICATED] + [_rows(jnp.broadcast_to(loss_part, (LANES,)))]
    small_parts += [_rows(g[n]) for n in SMALL_SHARDED]
    small = _pack_small(small_parts, 24, name="pack_small_grads")
    take_up()
    pairs, own = pair_stage(RS_SETS["l0b"], g, "l0b")
    *rb, rs = _rs_chip_exchange_behind(pairs, tag="l0b", small=small)
    gsums = list(_rs_sibling_share(list(reduced), name="rs_sibling_share_behind"))
    in_flight.extend(zip(own, rb))
    gsums += list(_rs_sibling_share(take_up(), name="rs_sibling_share_last"))
    ssum = _sum_slots(rs)

    where = {}
    for grp, gsum in zip(RS_SETS["l1"] + RS_SETS["l0a"] + RS_SETS["l0b"], gsums):
        off = 0
        for n, l in grp:
            rows = local_rows[n] if l is None else local_rows[n] // 2
            where[n, l] = (gsum, off, rows, len(grp) == 1)
            off += rows
    take = lambda gsum, off, rows, whole: gsum if whole else gsum[off:off + rows]
    grads, grad_rows = {}, {}
    for grp in LATE_GROUPS + EARLY_GROUPS + MID_GROUPS:
        for n in grp:
            if (n, None) in where:
                grads[n] = take(*where[n, None]).reshape(wl[n].shape)
                grad_rows[n] = where[n, None][:2]
            else:
                grads[n] = jnp.stack([take(*where[n, l]).reshape(wl[n].shape[1:]) for l in range(2)])
                grad_rows[n] = (grads[n].reshape(local_rows[n], wl[n].shape[-1]), 0)
    row = 0
    for n in REPLICATED:
        k = _rows(g[n]).shape[0]
        grads[n] = ssum[row:row + k].reshape(-1)[:wl[n].size].reshape(wl[n].shape)
        row += k
    loss = ssum[row, 0]
    row += 1
    for n in SMALL_SHARDED:
        k = _rows(g[n]).shape[0]
        full = ssum[row:row + k].reshape(-1)[:g[n].size]
        row += k
        loc = wl[n].shape
        if n == "lru_conv_w":
            sh = full.reshape(CONV_WIDTH, N_CHIPS, -1)
        elif n in ("lru_ba", "lru_bx"):
            sh = full.reshape(LRU_HEADS, N_CHIPS, -1)
        else:
            sh = full.reshape(1, N_CHIPS, -1)
        grads[n] = lax.dynamic_index_in_dim(sh, chip, axis=1, keepdims=False).reshape(loc)

    delta, new_m, new_v = {}, {}, {}
    for n, (gsum, off) in grad_rows.items():
        shape2 = (local_rows[n], wl[n].shape[-1])
        d, nm, nv = _adamw(wl[n].reshape(shape2), gsum, ml[n].reshape(shape2), vl[n].reshape(shape2), g_row=off,
                           name="adamw_" + n)
        delta[n], new_m[n], new_v[n] = (t.reshape(wl[n].shape) for t in (d, nm, nv))
    smalls = REPLICATED + SMALL_SHARDED
    packs = [_pack_small([_rows(src[n]) for n in smalls], 24, name="pack_adamw_" + tag)
             for tag, src in (("w", wl), ("g", grads), ("m", ml), ("v", vl))]
    outs = _adamw(*packs, name="adamw_small")
    row = 0
    for n in smalls:
        k = _rows(wl[n]).shape[0]
        for dst, o in zip((delta, new_m, new_v), outs):
            dst[n] = o[row:row + k].reshape(-1)[:wl[n].size].reshape(wl[n].shape)
        row += k

    return (loss, grad_x[None], *[grads[n] for n in WEIGHT_NAMES], *[delta[n] for n in WEIGHT_NAMES],
            *[new_m[n] for n in WEIGHT_NAMES], *[new_v[n] for n in WEIGHT_NAMES])
```

```python
import jax
import jax.numpy as jnp
from jax import lax
from jax.experimental import pallas as pl
from jax.experimental.pallas import tpu as pltpu
from jax.experimental.pallas import tpu_sc as plsc

F32, BF16 = jnp.float32, jnp.bfloat16
D_MODEL = 1024
NORM_EPS = 1e-6
ROPE_THETA = 500000.0
HEAD_DIM = 64
ROT_DIM = 16
BLK = 128
LRU_HEADS, LRU_HEAD_DIM, CONV_WIDTH, LRU_C = 4, 256, 4, 8.0
DILATED_PATTERN = ((128, 1), (512, 4), (2048, 16))
B_HEADS, C_HEADS, C_KV_HEADS, C_WINDOW = 8, 16, 2, 128
XA_HEADS, XA_HEAD_DIM, N_MEM = 4, 128, 256
D_FF = 2816
NEG = -1e30
ADAM_LR, ADAM_B1, ADAM_B2, ADAM_EPS, ADAM_WD, ADAM_STEP = 0.001, 0.9, 0.999, 1e-08, 0.01, 10
N_CHIPS = 4
VMEM_LIMIT_V7X = 56 * 1024 * 1024

NN = (((1,), (0,)), ((), ()))
NT = (((1,), (1,)), ((), ()))
TN = (((0,), (0,)), ((), ()))


def _dot(a, b, dims=NN):
    return lax.dot_general(a, b, dims, preferred_element_type=F32)


def _sigmoid(x):
    return 0.5 * jnp.tanh(0.5 * x) + 0.5


def _call(body, *, name, grid, in_specs, out_specs, out_shape, scratch=(), sem=None):
    return pl.pallas_call(
        body, name=name, grid=grid, in_specs=in_specs, out_specs=out_specs, out_shape=out_shape,
        scratch_shapes=list(scratch),
        compiler_params=pltpu.CompilerParams(dimension_semantics=sem, vmem_limit_bytes=VMEM_LIMIT_V7X))


def _rope_tables(L):
    half = ROT_DIM // 2
    inv = ROPE_THETA ** (-jnp.arange(0, ROT_DIM, 2, dtype=F32) / ROT_DIM)
    j = jnp.arange(2 * HEAD_DIM) % HEAD_DIM
    ang = jnp.arange(L, dtype=F32)[:, None] * inv[j % half][None, :]
    cos, sin = jnp.cos(ang), jnp.sin(ang)
    c = jnp.where(j < ROT_DIM, cos, 1.0)
    s1 = jnp.where(j < half, -sin, 0.0)
    s2 = jnp.where((j >= half) & (j < ROT_DIM), sin, 0.0)
    return c, s1, s2


def _rope_fwd(v, c, s1, s2):
    return v * c + pltpu.roll(v, 120, 1) * s1 + pltpu.roll(v, 8, 1) * s2


def _rope_bwd(dv, c, s1, s2):
    return dv * c + pltpu.roll(dv * s1, 8, 1) + pltpu.roll(dv * s2, 120, 1)


def _weight_spec(w, layer):
    once = pl.Buffered(1)
    if layer is None:
        return w.shape, pl.BlockSpec(w.shape, lambda i: (0, 0, 0), pipeline_mode=once)
    S, _, K, Ns = w.shape
    return (S, K, Ns), pl.BlockSpec((S, None, K, Ns), lambda i: (0, layer, 0, 0), pipeline_mode=once)


def _rowmm(a, w3, *, name, tm=512, gain=None, bias=None, res=None, swiglu=False, rope=None, layer=None):
    M, K = a.shape
    (S, _, Ns), w_spec = _weight_spec(w3, layer)
    N = S * Ns
    tm = min(tm, M)
    has_norm, has_bias, has_res, has_rope = gain is not None, bias is not None, res is not None, rope is not None
    row = lambda w: pl.BlockSpec((tm, w), lambda i: (i, 0))
    whole = lambda shape: pl.BlockSpec(shape, lambda i: (0,) * len(shape))
    ins, specs = [a], [row(K)]
    if has_norm:
        ins.append(gain.reshape(1, K)); specs.append(whole((1, K)))
    ins.append(w3); specs.append(w_spec)
    if has_bias:
        ins.append(bias.reshape(1, N)); specs.append(whole((1, N)))
    if has_res:
        ins.append(res); specs.append(row(N))
    if has_rope:
        ins += list(rope[2]); specs += [row(128)] * 3
    y_dtype = F32 if has_res else BF16
    out_shape, out_specs = [jax.ShapeDtypeStruct((M, N), y_dtype)], [row(N)]
    if has_norm:
        out_shape.append(jax.ShapeDtypeStruct((M, K), BF16)); out_specs.append(row(K))
    if swiglu:
        out_shape.append(jax.ShapeDtypeStruct((M, N // 2), BF16)); out_specs.append(row(N // 2))
    scratch = [pltpu.VMEM((tm, N), F32)] if has_rope else []

    def body(*refs):
        it = iter(refs)
        a_ref = next(it)
        g_ref = next(it) if has_norm else None
        w_ref = next(it)
        b_ref = next(it) if has_bias else None
        r_ref = next(it) if has_res else None
        tabs = [next(it) for _ in range(3)] if has_rope else None
        y_ref = next(it)
        n_ref = next(it) if has_norm else None
        act_ref = next(it) if swiglu else None
        ys_ref = next(it) if has_rope else None
        if has_norm:
            x = a_ref[...].astype(F32)
            ms = jnp.mean(x * x, axis=-1, keepdims=True)
            xb = (x * lax.rsqrt(ms + NORM_EPS) * g_ref[...]).astype(BF16)
            n_ref[...] = xb
        else:
            xb = a_ref[...].astype(BF16)
        if swiglu:
            for s in range(S // 2):
                g = _dot(xb, w_ref[s])
                u = _dot(xb, w_ref[s + S // 2])
                y_ref[:, s * Ns:(s + 1) * Ns] = g.astype(BF16)
                y_ref[:, N // 2 + s * Ns:N // 2 + (s + 1) * Ns] = u.astype(BF16)
                act_ref[:, s * Ns:(s + 1) * Ns] = (g * _sigmoid(g) * u).astype(BF16)
            return
        for s in range(S):
            sl = slice(s * Ns, (s + 1) * Ns)
            acc = _dot(xb, w_ref[s])
            if has_bias:
                acc = acc + b_ref[:, sl]
            if has_res:
                acc = acc + r_ref[:, sl]
            if has_rope:
                ys_ref[:, sl] = acc
            else:
                y_ref[:, sl] = acc.astype(y_dtype)
        if has_rope:
            c, s1, s2 = (t[...] for t in tabs)
            for cb in range(N // 128):
                cs = slice(cb * 128, (cb + 1) * 128)
                v = ys_ref[:, cs]
                if rope[0] <= cb * 128 < rope[1]:
                    v = _rope_fwd(v, c, s1, s2)
                y_ref[:, cs] = v.astype(BF16)

    return _call(body, name=name, grid=(M // tm,), in_specs=specs, out_specs=out_specs, out_shape=out_shape,
                 scratch=scratch, sem=("parallel",))(*ins)


def _mm_nt(dy, w3, *, name, mode, tm=512, kchunk=None, h=None, gain=None, dh=None, gu=None, layer=None, after=None):
    M, N = dy.shape
    (S, K, Ns), w_spec = _weight_spec(w3, layer)
    kchunk = kchunk or K
    tm = min(tm, M)
    row = lambda w: pl.BlockSpec((tm, w), lambda i: (i, 0))
    whole = lambda shape: pl.BlockSpec(shape, lambda i: (0,) * len(shape))
    ins, specs = [dy, w3], [row(N), w_spec]
    after = list(after or ())
    ins = after + ins
    specs = [pl.BlockSpec((8, a.shape[1]), lambda i: (0, 0)) for a in after] + specs
    has_dh = dh is not None
    if mode == "norm":
        ins += [h, gain.reshape(1, K)]; specs += [row(K), whole((1, K))]
        if has_dh:
            ins.append(dh); specs.append(row(K))
        out_shape = [jax.ShapeDtypeStruct((M, K), F32), jax.ShapeDtypeStruct((1, K), F32)]
        out_specs = [row(K), whole((1, K))]
    elif mode == "swiglu":
        ins.append(gu); specs.append(row(2 * K))
        out_shape, out_specs = [jax.ShapeDtypeStruct((M, 2 * K), BF16)], [row(2 * K)]
    else:
        out_shape, out_specs = [jax.ShapeDtypeStruct((M, K), BF16)], [row(K)]

    def body(*refs):
        it = iter(refs[len(after):])
        dy_ref, w_ref = next(it), next(it)
        if mode == "norm":
            h_ref, g_ref = next(it), next(it)
            dh_ref = next(it) if has_dh else None
            o_ref, dg_ref = next(it), next(it)
        elif mode == "swiglu":
            gu_ref, o_ref = next(it), next(it)
        else:
            o_ref = next(it)
        for kc in range(K // kchunk):
            ks = slice(kc * kchunk, (kc + 1) * kchunk)
            acc = None
            for s in range(S):
                t = _dot(dy_ref[:, s * Ns:(s + 1) * Ns].astype(BF16), w_ref[s, ks, :], NT)
                acc = t if acc is None else acc + t
            if mode == "plain":
                o_ref[:, ks] = acc.astype(BF16)
            elif mode == "swiglu":
                us = slice(K + kc * kchunk, K + (kc + 1) * kchunk)
                g = gu_ref[:, ks].astype(F32)
                u = gu_ref[:, us].astype(F32)
                sg = _sigmoid(g)
                o_ref[:, ks] = (acc * u * (sg * (1.0 + g * (1.0 - sg)))).astype(BF16)
                o_ref[:, us] = (acc * (g * sg)).astype(BF16)
            else:
                x = h_ref[...].astype(F32)
                r = lax.rsqrt(jnp.mean(x * x, axis=-1, keepdims=True) + NORM_EPS)
                xhat = x * r
                dxh = acc * g_ref[...]
                dx = r * (dxh - xhat * jnp.mean(dxh * xhat, axis=-1, keepdims=True))
                o_ref[...] = dx + dh_ref[...] if has_dh else dx

                @pl.when(pl.program_id(0) == 0)
                def _():
                    dg_ref[...] = jnp.zeros_like(dg_ref)

                dg_ref[...] += jnp.sum(acc * xhat, axis=0, keepdims=True)

    sem = ("arbitrary",) if mode == "norm" else ("parallel",)
    return _call(body, name=name, grid=(M // tm,), in_specs=specs, out_specs=out_specs, out_shape=out_shape, sem=sem)(*ins)


def _mm_tn(x, dy, *, S, name, tk=2048, kk=None, bias=False):
    M, K = x.shape
    N = dy.shape[1]
    Ns = N // S
    kk = kk or K
    tk = min(tk, M)
    nl = M // tk
    in_specs = [pl.BlockSpec((tk, kk), lambda s, kc, l: (l, kc)), pl.BlockSpec((tk, Ns), lambda s, kc, l: (l, s))]
    out_shape = [jax.ShapeDtypeStruct((S, K, Ns), BF16)]
    out_specs = [pl.BlockSpec((None, kk, Ns), lambda s, kc, l: (s, kc, 0))]
    if bias:
        out_shape.append(jax.ShapeDtypeStruct((1, N), F32))
        out_specs.append(pl.BlockSpec((1, Ns), lambda s, kc, l: (0, s)))

    def body(x_ref, dy_ref, o_ref, *rest):
        acc_ref = rest[-1]
        kc, l = pl.program_id(1), pl.program_id(2)

        @pl.when(l == 0)
        def _():
            acc_ref[...] = jnp.zeros_like(acc_ref)

        acc_ref[...] += _dot(x_ref[...].astype(BF16), dy_ref[...].astype(BF16), TN)
        if bias:
            b_ref = rest[0]

            @pl.when((kc == 0) & (l == 0))
            def _():
                b_ref[...] = jnp.zeros_like(b_ref)

            @pl.when(kc == 0)
            def _():
                b_ref[...] += jnp.sum(dy_ref[...].astype(F32), axis=0, keepdims=True)

        @pl.when(l == nl - 1)
        def _():
            o_ref[...] = acc_ref[...].astype(BF16)

    return _call(body, name=name, grid=(S, K // kk, nl), in_specs=in_specs, out_specs=out_specs, out_shape=out_shape,
                 scratch=[pltpu.VMEM((kk, Ns), F32)], sem=("arbitrary", "arbitrary", "arbitrary"))(x, dy)


def _band_bias(max_dist, has_prev):
    rows = lax.broadcasted_iota(jnp.int32, (BLK, 2 * BLK), 0)
    cols = lax.broadcasted_iota(jnp.int32, (BLK, 2 * BLK), 1)
    dist = rows - cols + BLK
    ok = (dist >= 0) & (dist <= max_dist) & ((cols >= BLK) | has_prev)
    return jnp.where(ok, 0.0, NEG)


Q_SCALE = HEAD_DIM ** -0.5
BNT = (((2,), (2,)), ((0,), (0,)))
BNN = (((2,), (1,)), ((0,), (0,)))
BTN = (((1,), (1,)), ((0,), (0,)))


def _bdot(a, b, dims):
    return lax.dot_general(a, b, dims, preferred_element_type=F32)


def _split_heads(x, n, width=HEAD_DIM, group=1):
    return jnp.stack([x[:, (h // group) * width:(h // group + 1) * width] for h in range(n)], axis=0)


def _band_fwd(qa, ka, va, *, d, nq, nkv, qcol, kcol, vcol, max_dist, sinks=None, name):
    Lr = qa.shape[0]
    nb = Lr // BLK
    qw, kw, G = nq * HEAD_DIM, nkv * HEAD_DIM, nq // nkv
    cur = lambda colf, w: pl.BlockSpec((BLK, w), lambda r, i: (i, colf(r)))
    prv = lambda colf, w: pl.BlockSpec((BLK, w), lambda r, i: (jnp.maximum(i - 1, 0), colf(r)))
    out = pl.BlockSpec((BLK, qw), lambda r, i: (i, r))
    ins, specs = [qa, ka, ka, va, va], [cur(qcol, qw), cur(kcol, kw), prv(kcol, kw), cur(vcol, kw), prv(vcol, kw)]
    has_sinks = sinks is not None
    if has_sinks:
        ins.append(sinks); specs.append(pl.BlockSpec(memory_space=pltpu.SMEM))

    def body(*refs):
        q_ref, kc_ref, kp_ref, vc_ref, vp_ref = refs[:5]
        sk_ref = refs[5] if has_sinks else None
        o_ref, lse_ref = refs[-2], refs[-1]
        bias = _band_bias(max_dist, pl.program_id(1) > 0)
        k2 = jnp.concatenate([kp_ref[...], kc_ref[...]], axis=0)
        v2 = jnp.concatenate([vp_ref[...], vc_ref[...]], axis=0)
        if G == 1 and not has_sinks:
            q3 = _split_heads(q_ref[...], nq) * jnp.asarray(Q_SCALE, BF16)
            s = _bdot(q3, _split_heads(k2, nq), BNT) + bias
            m = jnp.max(s, axis=-1, keepdims=True)
            p = jnp.exp(s - m)
            l = jnp.sum(p, axis=-1, keepdims=True)
            o = (_bdot(p.astype(BF16), _split_heads(v2, nq), BNN) / l).astype(BF16)
            lse = m + jnp.log(l)
            for h in range(nq):
                hs = slice(h * HEAD_DIM, (h + 1) * HEAD_DIM)
                o_ref[:, hs] = o[h]
                lse_ref[:, hs] = jnp.broadcast_to(lse[h], (BLK, HEAD_DIM))
            return
        for h in range(nq):
            hs = slice(h * HEAD_DIM, (h + 1) * HEAD_DIM)
            ks = slice((h // G) * HEAD_DIM, (h // G + 1) * HEAD_DIM)
            s = _dot(q_ref[:, hs] * jnp.asarray(Q_SCALE, BF16), k2[:, ks], NT) + bias
            m = jnp.max(s, axis=-1, keepdims=True)
            if has_sinks:
                m = jnp.maximum(m, sk_ref[h])
            p = jnp.exp(s - m)
            l = jnp.sum(p, axis=-1, keepdims=True)
            if has_sinks:
                l = l + jnp.exp(sk_ref[h] - m)
            o_ref[:, hs] = (_dot(p.astype(BF16), v2[:, ks]) / l).astype(BF16)
            lse_ref[:, hs] = jnp.broadcast_to(m + jnp.log(l), (BLK, HEAD_DIM))

    return _call(body, name=name, grid=(d, nb), in_specs=specs, out_specs=[out, out],
                 out_shape=[jax.ShapeDtypeStruct((Lr, d * qw), BF16), jax.ShapeDtypeStruct((Lr, d * qw), F32)],
                 sem=("parallel", "parallel"))(*ins)


def _band_bwd(qa, ka, va, doa, oa, lsea, *, d, nq, nkv, qcol, kcol, vcol, docol, max_dist, sinks=None, name):
    Lr = qa.shape[0]
    nb = Lr // BLK
    qw, kw, G = nq * HEAD_DIM, nkv * HEAD_DIM, nq // nkv
    transposed = G > 1
    last = lambda i: jnp.minimum(i, nb - 1)
    cur = lambda colf, w: pl.BlockSpec((BLK, w), lambda r, i: (last(i), colf(r)))
    prv = lambda colf, w: pl.BlockSpec((BLK, w), lambda r, i: (jnp.maximum(last(i) - 1, 0), colf(r)))
    own = lambda r: r
    ins = [qa, ka, ka, va, va, doa, oa, lsea]
    specs = [cur(qcol, qw), cur(kcol, kw), prv(kcol, kw), cur(vcol, kw), prv(vcol, kw), cur(docol, qw), cur(own, qw),
             cur(own, qw)]
    has_sinks = sinks is not None
    if has_sinks:
        ins.append(sinks); specs.append(pl.BlockSpec(memory_space=pltpu.SMEM))
    out_shape = [jax.ShapeDtypeStruct((Lr, d * qw), BF16), jax.ShapeDtypeStruct((Lr, d * kw), BF16),
                 jax.ShapeDtypeStruct((Lr, d * kw), BF16)]
    behind = lambda r, i: (jnp.maximum(i - 1, 0), r)
    out_specs = [pl.BlockSpec((BLK, qw), lambda r, i: (last(i), r)), pl.BlockSpec((BLK, kw), behind),
                 pl.BlockSpec((BLK, kw), behind)]
    if has_sinks:
        out_shape.append(jax.ShapeDtypeStruct((8, 128), F32))
        out_specs.append(pl.BlockSpec((8, 128), lambda r, i: (0, 0)))

    def body(*refs):
        it = iter(refs)
        q_ref, kc_ref, kp_ref, vc_ref, vp_ref, do_ref, o_ref, ls_ref = (next(it) for _ in range(8))
        sk_ref = next(it) if has_sinks else None
        dq_ref, dk_ref, dv_ref = next(it), next(it), next(it)
        dsk_ref = next(it) if has_sinks else None
        dk_car, dv_car = next(it), next(it)
        r_id, i = pl.program_id(0), pl.program_id(1)

        @pl.when(i == 0)
        def _():
            dk_car[...] = jnp.zeros_like(dk_car)
            dv_car[...] = jnp.zeros_like(dv_car)

        if has_sinks:
            @pl.when((r_id == 0) & (i == 0))
            def _():
                dsk_ref[...] = jnp.zeros_like(dsk_ref)

        @pl.when(i == nb)
        def _():
            dk_ref[...] = dk_car[...].astype(BF16)
            dv_ref[...] = dv_car[...].astype(BF16)

        @pl.when(i < nb)
        def _():
            bias = jnp.tile(_band_bias(max_dist, i > 0), (G, 1))
            k2 = jnp.concatenate([kp_ref[...], kc_ref[...]], axis=0)
            v2 = jnp.concatenate([vp_ref[...], vc_ref[...]], axis=0)
            if G == 1 and not has_sinks:
                scale = jnp.asarray(Q_SCALE, BF16)
                q3, do3 = _split_heads(q_ref[...], nq) * scale, _split_heads(do_ref[...], nq)
                k3, v3 = _split_heads(k2, nq), _split_heads(v2, nq)
                lse = jnp.stack([ls_ref[:, h * HEAD_DIM:h * HEAD_DIM + 1] for h in range(nq)], axis=0)
                dl = jnp.sum(do3.astype(F32) * _split_heads(o_ref[...], nq).astype(F32), axis=-1, keepdims=True)
                p = jnp.exp(_bdot(q3, k3, BNT) + bias - lse)
                ds = (p * (_bdot(do3, v3, BNT) - dl)).astype(BF16)
                dq = (_bdot(ds, k3, BNN) * Q_SCALE).astype(BF16)
                dk, dv = _bdot(ds, q3, BTN), _bdot(p.astype(BF16), do3, BTN)
                for h in range(nq):
                    hs = slice(h * HEAD_DIM, (h + 1) * HEAD_DIM)
                    dq_ref[:, hs] = dq[h]
                    dk_ref[:, hs] = (dk_car[:, hs] + dk[h, :BLK]).astype(BF16)
                    dv_ref[:, hs] = (dv_car[:, hs] + dv[h, :BLK]).astype(BF16)
                    dk_car[:, hs] = dk[h, BLK:]
                    dv_car[:, hs] = dv[h, BLK:]
                return
            if has_sinks:
                lane = lax.broadcasted_iota(jnp.int32, (8, 128), 1)
                dsk = jnp.zeros((8, 128), F32)
            for kv in range(nkv):
                heads = [slice((kv * G + g) * HEAD_DIM, (kv * G + g + 1) * HEAD_DIM) for g in range(G)]
                stack = lambda ref: jnp.concatenate([ref[:, hs] for hs in heads], axis=0)
                ks = slice(kv * HEAD_DIM, (kv + 1) * HEAD_DIM)
                kh, vh = k2[:, ks], v2[:, ks]
                q = stack(q_ref) * jnp.asarray(Q_SCALE, BF16)
                do = stack(do_ref)
                lse = jnp.concatenate([ls_ref[:, hs.start:hs.start + 1] for hs in heads], axis=0)
                dl = jnp.sum(do.astype(F32) * stack(o_ref).astype(F32), axis=-1, keepdims=True)
                p = jnp.exp(_dot(q, kh, NT) + bias - lse)
                ds = (p * (_dot(do, vh, NT) - dl)).astype(BF16)
                dq = (_dot(ds, kh) * Q_SCALE).astype(BF16)
                for g, hs in enumerate(heads):
                    dq_ref[:, hs] = dq[g * BLK:(g + 1) * BLK]
                if transposed:
                    dk, dv = _dot(q, ds, TN).T, _dot(do, p.astype(BF16), TN).T
                else:
                    dk, dv = _dot(ds, q, TN), _dot(p.astype(BF16), do, TN)
                if has_sinks:
                    sk = jnp.concatenate([jnp.full((BLK, 1), sk_ref[kv * G + g], F32) for g in range(G)], axis=0)
                    lost = jnp.exp(sk - lse) * dl
                    for g in range(G):
                        val = -jnp.sum(lost[g * BLK:(g + 1) * BLK], axis=0, keepdims=True)
                        dsk = dsk + jnp.where(lane == kv * G + g, val, 0.0)
                dk_ref[:, ks] = (dk_car[:, ks] + dk[:BLK]).astype(BF16)
                dv_ref[:, ks] = (dv_car[:, ks] + dv[:BLK]).astype(BF16)
                dk_car[:, ks] = dk[BLK:]
                dv_car[:, ks] = dv[BLK:]
            if has_sinks:
                dsk_ref[...] += dsk

    return _call(body, name=name, grid=(d, nb + 1), in_specs=specs, out_specs=out_specs, out_shape=out_shape,
                 scratch=[pltpu.VMEM((BLK, kw), F32), pltpu.VMEM((BLK, kw), F32)], sem=("arbitrary", "arbitrary"))(*ins)


def _attn_grad_combine(branches, tabs, *, name, tm=512):
    L, qw = branches[0][0].shape
    kw = branches[0][1].shape[1]
    nbr = len(branches)
    row = lambda w: pl.BlockSpec((tm, w), lambda i: (i, 0))
    ins, specs = [], []
    for dq, dk, dv in branches:
        ins += [dq, dk, dv]; specs += [row(qw), row(kw), row(kw)]
    ins += list(tabs); specs += [row(128)] * 3

    def body(*refs):
        c, s1, s2 = (t[...] for t in refs[3 * nbr:3 * nbr + 3])
        o_ref = refs[-1]
        for part, (w, off, rot) in enumerate(((qw, 0, True), (kw, qw, True), (kw, qw + kw, False))):
            for cb in range(w // 128):
                cs = slice(cb * 128, (cb + 1) * 128)
                v = refs[part][:, cs].astype(F32)
                for b in range(1, nbr):
                    v = v + refs[3 * b + part][:, cs].astype(F32)
                if rot:
                    v = _rope_bwd(v, c, s1, s2)
                o_ref[:, off + cb * 128:off + (cb + 1) * 128] = v.astype(BF16)

    return _call(body, name=name, grid=(L // tm,), in_specs=specs, out_specs=row(qw + 2 * kw),
                 out_shape=jax.ShapeDtypeStruct((L, qw + 2 * kw), BF16), sem=("parallel",))(*ins)


def _xattn_fwd(q, kv, *, name, tq=1024):
    L, W = q.shape
    scale = XA_HEAD_DIM ** -0.5
    row = pl.BlockSpec((tq, W), lambda i: (i, 0))
    kvs = pl.BlockSpec((N_MEM, 2 * W), lambda i: (0, 0))

    def body(q_ref, kv_ref, o_ref, lse_ref):
        kv = kv_ref[...]
        k3, v3 = _split_heads(kv[:, :W], XA_HEADS, XA_HEAD_DIM), _split_heads(kv[:, W:], XA_HEADS, XA_HEAD_DIM)
        s = _bdot(_split_heads(q_ref[...], XA_HEADS, XA_HEAD_DIM), k3, BNT) * scale
        m = jnp.max(s, axis=-1, keepdims=True)
        p = jnp.exp(s - m)
        l = jnp.sum(p, axis=-1, keepdims=True)
        o = (_bdot(p.astype(BF16), v3, BNN) / l).astype(BF16)
        lse = m + jnp.log(l)
        for h in range(XA_HEADS):
            hs = slice(h * XA_HEAD_DIM, (h + 1) * XA_HEAD_DIM)
            o_ref[:, hs] = o[h]
            lse_ref[:, hs] = jnp.broadcast_to(lse[h], (tq, XA_HEAD_DIM))

    return _call(body, name=name, grid=(L // tq,), in_specs=[row, kvs], out_specs=[row, row],
                 out_shape=[jax.ShapeDtypeStruct((L, W), BF16), jax.ShapeDtypeStruct((L, W), F32)], sem=("parallel",))(q, kv)


def _xattn_bwd(q, kv, o, lse, do, *, name, tq=1024):
    L, W = q.shape
    scale = XA_HEAD_DIM ** -0.5
    row = pl.BlockSpec((tq, W), lambda i: (i, 0))
    kvs = pl.BlockSpec((N_MEM, 2 * W), lambda i: (0, 0))

    def body(q_ref, kv_ref, o_ref, lse_ref, do_ref, dq_ref, dkv_ref):
        @pl.when(pl.program_id(0) == 0)
        def _():
            dkv_ref[...] = jnp.zeros_like(dkv_ref)

        split = lambda x: _split_heads(x, XA_HEADS, XA_HEAD_DIM)
        kv = kv_ref[...]
        q3, k3, v3, do3 = split(q_ref[...]), split(kv[:, :W]), split(kv[:, W:]), split(do_ref[...])
        lse = jnp.stack([lse_ref[:, h * XA_HEAD_DIM:h * XA_HEAD_DIM + 1] for h in range(XA_HEADS)], axis=0)
        p = jnp.exp(_bdot(q3, k3, BNT) * scale - lse)
        dl = jnp.sum(do3.astype(F32) * split(o_ref[...]).astype(F32), axis=-1, keepdims=True)
        ds = (p * (_bdot(do3, v3, BNT) - dl) * scale).astype(BF16)
        dq = _bdot(ds, k3, BNN).astype(BF16)
        dk, dv = _bdot(ds, q3, BTN), _bdot(p.astype(BF16), do3, BTN)
        for h in range(XA_HEADS):
            hs = slice(h * XA_HEAD_DIM, (h + 1) * XA_HEAD_DIM)
            vs = slice(W + h * XA_HEAD_DIM, W + (h + 1) * XA_HEAD_DIM)
            dq_ref[:, hs] = dq[h]
            dkv_ref[:, hs] += dk[h]
            dkv_ref[:, vs] += dv[h]

    return _call(body, name=name, grid=(L // tq,), in_specs=[row, kvs, row, row, row], out_specs=[row, kvs],
                 out_shape=[jax.ShapeDtypeStruct((L, W), BF16), jax.ShapeDtypeStruct((N_MEM, 2 * W), F32)],
                 sem=("arbitrary",))(q, kv, o, lse, do)


def _neg_expm1(z):
    series = -(z * (1.0 + z * (0.5 + z * (1.0 / 6.0 + z * (1.0 / 24.0 + z * (1.0 / 120.0))))))
    return jnp.where(z > -0.05, series, 1.0 - jnp.exp(z))


def _softplus(z):
    return jnp.maximum(z, 0.0) + jnp.log(1.0 + jnp.exp(-jnp.abs(z)))


def _gelu_parts(y):
    c = 0.7978845608028654
    t = jnp.tanh(c * (y + 0.044715 * y * y * y))
    gy = 0.5 * y * (1.0 + t)
    dgy = 0.5 * (1.0 + t) + 0.5 * y * (1.0 - t * t) * c * (1.0 + 3.0 * 0.044715 * y * y)
    return gy, dgy


def _lru_gates(xc, wa_ref, ba, wx_ref, bx, sp):
    x3 = _split_heads(xc.astype(BF16), LRU_HEADS, LRU_HEAD_DIM)
    merge = lambda t: jnp.concatenate([t[hd] for hd in range(LRU_HEADS)], axis=1)
    r = _sigmoid(merge(_bdot(x3, wa_ref[...], BNN)) + ba)
    ig = _sigmoid(merge(_bdot(x3, wx_ref[...], BNN)) + bx)
    la = -LRU_C * r * sp
    return r, ig, jnp.exp(la), _neg_expm1(2.0 * la)


def _conv_taps(x_ext, halo):
    n = x_ext.shape[0]
    return [x_ext[halo:] if k == CONV_WIDTH - 1 else pltpu.roll(x_ext, CONV_WIDTH - 1 - k, 0)[halo:]
            for k in range(CONV_WIDTH)]


def _lru_fwd(proj, cw, cb, wa, ba, wx, bx, lam, *, name, tc=512):
    L = proj.shape[0]
    W = LRU_HEADS * LRU_HEAD_DIM
    nb = L // tc
    whole = lambda shape: pl.BlockSpec(shape, lambda i: (0,) * len(shape))
    specs = [pl.BlockSpec((tc, W), lambda i: (i, 0)), pl.BlockSpec((tc, W), lambda i: (i, 1)),
             pl.BlockSpec((16, W), lambda i: (jnp.maximum(i * (tc // 16) - 1, 0), 0)),
             whole((CONV_WIDTH, W)), whole((1, W)), whole((LRU_HEADS, LRU_HEAD_DIM, LRU_HEAD_DIM)), whole((1, W)),
             whole((LRU_HEADS, LRU_HEAD_DIM, LRU_HEAD_DIM)), whole((1, W)), whole((1, W))]
    out_specs = [pl.BlockSpec((tc, W), lambda i: (i, 0))] * 2
    out_shape = [jax.ShapeDtypeStruct((L, W), BF16), jax.ShapeDtypeStruct((L, W), F32)]

    def body(x_ref, y_ref, xh_ref, cw_ref, cb_ref, wa_ref, ba_ref, wx_ref, bx_ref, lam_ref, rec_ref, hs_ref,
             hcar, a_scr, b_scr):
        i = pl.program_id(0)

        @pl.when(i == 0)
        def _():
            hcar[...] = jnp.zeros_like(hcar)

        halo = jnp.where(i > 0, xh_ref[...].astype(F32), 0.0)
        taps = _conv_taps(jnp.concatenate([halo, x_ref[...].astype(F32)], axis=0), 16)
        xc = cb_ref[...] + sum(cw_ref[k:k + 1, :] * taps[k] for k in range(CONV_WIDTH))
        _, ig, a, om = _lru_gates(xc, wa_ref, ba_ref[...], wx_ref, bx_ref[...], _softplus(-lam_ref[...]))
        b = jnp.sqrt(om) * (ig * xc)
        rowmod = lax.broadcasted_iota(jnp.int32, (tc, W), 0) & 7
        for s in (1, 2, 4):
            keep = rowmod >= s
            b = jnp.where(keep, a * pltpu.roll(b, s, 0) + b, b)
            a = jnp.where(keep, a * pltpu.roll(a, s, 0), a)
        a_scr[...] = a
        b_scr[...] = b

        def tile(j, hc):
            rows = pl.ds(pl.multiple_of(j * 8, 8), 8)
            ht = a_scr[rows, :] * hc + b_scr[rows, :]
            hs_ref[rows, :] = ht
            return jnp.broadcast_to(ht[7:8, :], (8, W))

        hcar[...] = lax.fori_loop(0, tc // 8, tile, hcar[...])
        gy, _ = _gelu_parts(y_ref[...].astype(F32))
        rec_ref[...] = (hs_ref[...] * gy).astype(BF16)

    return _call(body, name=name, grid=(nb,), in_specs=specs, out_specs=out_specs, out_shape=out_shape,
                 scratch=[pltpu.VMEM((8, W), F32), pltpu.VMEM((tc, W), F32), pltpu.VMEM((tc, W), F32)],
                 sem=("arbitrary",))(proj, proj, proj, cw, cb, wa, ba, wx, bx, lam)


def _lru_bwd(proj, hs, drec_src, cw, cb, wa, ba, wx, bx, lam, *, name, tc=256):
    L = proj.shape[0]
    W = LRU_HEADS * LRU_HEAD_DIM
    nb = L // tc
    tb = lambda i: nb - 1 - i
    whole = lambda shape: pl.BlockSpec(shape, lambda i: (0,) * len(shape))
    gate_w = (LRU_HEADS, LRU_HEAD_DIM, LRU_HEAD_DIM)
    specs = [pl.BlockSpec((tc, W), lambda i: (tb(i), 0)), pl.BlockSpec((tc, W), lambda i: (tb(i), 1)),
             pl.BlockSpec((16, W), lambda i: (jnp.maximum(tb(i) * (tc // 16) - 1, 0), 0)),
             pl.BlockSpec((tc, W), lambda i: (tb(i), 0)),
             pl.BlockSpec((8, W), lambda i: (jnp.maximum(tb(i) * (tc // 8) - 1, 0), 0)),
             pl.BlockSpec((tc, W), lambda i: (tb(i), 0)),
             whole((CONV_WIDTH, W)), whole((1, W)), whole(gate_w), whole((1, W)), whole(gate_w), whole((1, W)), whole((1, W))]
    out_specs = [pl.BlockSpec((tc, 2 * W), lambda i: (tb(i), 0)), whole((CONV_WIDTH, W)), whole((1, W)), whole(gate_w),
                 whole((1, W)), whole(gate_w), whole((1, W)), whole((1, W))]
    vec = jax.ShapeDtypeStruct((1, W), F32)
    out_shape = [jax.ShapeDtypeStruct((L, 2 * W), BF16), jax.ShapeDtypeStruct((CONV_WIDTH, W), F32), vec,
                 jax.ShapeDtypeStruct(gate_w, F32), vec, jax.ShapeDtypeStruct(gate_w, F32), vec, vec]

    def body(x_ref, y_ref, xh_ref, hs_ref, hh_ref, dr_ref, cw_ref, cb_ref, wa_ref, ba_ref, wx_ref, bx_ref, lam_ref,
             dxy_ref, dcw_ref, dcb_ref, dwa_ref, dba_ref, dwx_ref, dbx_ref, dlam_ref, gcar, dxc_car, a_scr, b_scr, g_scr):
        pid = pl.program_id(0)
        t = tb(pid)
        accs = (dcw_ref, dcb_ref, dwa_ref, dba_ref, dwx_ref, dbx_ref, dlam_ref)

        @pl.when(pid == 0)
        def _():
            gcar[...] = jnp.zeros_like(gcar)
            dxc_car[...] = jnp.zeros_like(dxc_car)
            for r in accs:
                r[...] = jnp.zeros_like(r)

        halo = jnp.where(t > 0, xh_ref[...].astype(F32), 0.0)
        taps = _conv_taps(jnp.concatenate([halo, x_ref[...].astype(F32)], axis=0), 16)
        xc = cb_ref[...] + sum(cw_ref[k:k + 1, :] * taps[k] for k in range(CONV_WIDTH))
        lam = lam_ref[...]
        sp = _softplus(-lam)
        r, ig, a, om = _lru_gates(xc, wa_ref, ba_ref[...], wx_ref, bx_ref[...], sp)
        sq = jnp.sqrt(om)
        hblk = hs_ref[...]
        hprev = pltpu.roll(jnp.concatenate([jnp.where(t > 0, hh_ref[...], 0.0), hblk], axis=0), 1, 0)[8:]
        gy, dgy = _gelu_parts(y_ref[...].astype(F32))
        drec = dr_ref[...].astype(F32)
        dxy_ref[:, W:] = (drec * hblk * dgy).astype(BF16)

        rowidx = lax.broadcasted_iota(jnp.int32, (tc, W), 0)
        rowmod = rowidx & 7
        ca = jnp.where(rowidx == tc - 1, 1.0, pltpu.roll(a, tc - 1, 0))
        cbv = drec * gy
        for s in (1, 2, 4):
            keep = rowmod < 8 - s
            cbv = jnp.where(keep, ca * pltpu.roll(cbv, tc - s, 0) + cbv, cbv)
            ca = jnp.where(keep, ca * pltpu.roll(ca, tc - s, 0), ca)
        a_scr[...] = ca
        b_scr[...] = cbv

        def tile(k, gc):
            j = tc // 8 - 1 - k
            rows = pl.ds(pl.multiple_of(j * 8, 8), 8)
            gt = a_scr[rows, :] * gc + b_scr[rows, :]
            g_scr[rows, :] = gt
            return jnp.broadcast_to(gt[0:1, :], (8, W))

        lax.fori_loop(0, tc // 8, tile, gcar[...])
        G = g_scr[...]
        gcar[...] = jnp.broadcast_to(a[0:1, :] * G[0:1, :], (8, W))

        da = G * hprev
        dsq = G * (ig * xc)
        di = G * (sq * xc)
        dxc = G * (sq * ig)
        dla = da * a - 2.0 * a * a * (dsq * 0.5 * lax.rsqrt(om))
        dlam_ref[...] += jnp.sum(dla * (-LRU_C * r), axis=0, keepdims=True) * (-_sigmoid(-lam))
        dpr = dla * (-LRU_C * sp) * r * (1.0 - r)
        dpi = di * ig * (1.0 - ig)
        dba_ref[...] += jnp.sum(dpr, axis=0, keepdims=True)
        dbx_ref[...] += jnp.sum(dpi, axis=0, keepdims=True)
        back = []
        for hd in range(LRU_HEADS):
            sl = slice(hd * LRU_HEAD_DIM, (hd + 1) * LRU_HEAD_DIM)
            xh, dprh, dpih = xc[:, sl].astype(BF16), dpr[:, sl].astype(BF16), dpi[:, sl].astype(BF16)
            back.append(_dot(dprh, wa_ref[hd], NT) + _dot(dpih, wx_ref[hd], NT))
            dwa_ref[hd] += _dot(xh, dprh, TN)
            dwx_ref[hd] += _dot(xh, dpih, TN)
        dxc = dxc + jnp.concatenate(back, axis=1)
        dcb_ref[...] += jnp.sum(dxc, axis=0, keepdims=True)
        for k in range(CONV_WIDTH):
            dcw_ref[k:k + 1, :] += jnp.sum(dxc * taps[k], axis=0, keepdims=True)
        ext = jnp.concatenate([dxc, dxc_car[...]], axis=0)
        dx = cw_ref[CONV_WIDTH - 1:CONV_WIDTH, :] * dxc
        for k in range(CONV_WIDTH - 1):
            dx = dx + cw_ref[k:k + 1, :] * pltpu.roll(ext, tc + 8 - (CONV_WIDTH - 1 - k), 0)[:tc]
        dxc_car[...] = dxc[0:8, :]
        dxy_ref[:, :W] = dx.astype(BF16)

    scratch = [pltpu.VMEM((8, W), F32), pltpu.VMEM((8, W), F32)] + [pltpu.VMEM((tc, W), F32)] * 3
    return _call(body, name=name, grid=(nb,), in_specs=specs, out_specs=out_specs, out_shape=out_shape, scratch=scratch,
                 sem=("arbitrary",))(proj, proj, proj, hs, hs, drec_src, cw, cb, wa, ba, wx, bx, lam)


def _final_loss(h, gain, target, *, name, tm=512):
    M, K = h.shape
    row = pl.BlockSpec((tm, K), lambda i: (i, 0))
    vec = pl.BlockSpec((1, K), lambda i: (0, 0))
    one = pl.BlockSpec((1, 128), lambda i: (0, 0))

    def body(h_ref, g_ref, t_ref, dh_ref, dg_ref, loss_ref):
        @pl.when(pl.program_id(0) == 0)
        def _():
            dg_ref[...] = jnp.zeros_like(dg_ref)
            loss_ref[...] = jnp.zeros_like(loss_ref)

        x = h_ref[...]
        r = lax.rsqrt(jnp.mean(x * x, axis=-1, keepdims=True) + NORM_EPS)
        xhat = x * r
        err = xhat * g_ref[...] - t_ref[...]
        loss_ref[...] += 0.5 / K * jnp.sum(err * err)
        dy = err * (1.0 / K)
        dg_ref[...] += jnp.sum(dy * xhat, axis=0, keepdims=True)
        dxh = dy * g_ref[...]
        dh_ref[...] = r * (dxh - xhat * jnp.mean(dxh * xhat, axis=-1, keepdims=True))

    return _call(body, name=name, grid=(M // tm,), in_specs=[row, vec, row], out_specs=[row, vec, one],
                 out_shape=[jax.ShapeDtypeStruct((M, K), F32), jax.ShapeDtypeStruct((1, K), F32),
                            jax.ShapeDtypeStruct((1, 128), F32)], sem=("arbitrary",))(h, gain.reshape(1, K), target)


def _dilated_merge(branches, *, name, tm=512):
    L, W = branches[0].shape
    nbr = len(branches) // 2
    row = pl.BlockSpec((tm, W), lambda i: (i, 0))

    def body(*refs):
        o_ref, lse_ref = refs[-2], refs[-1]
        lses = [refs[2 * b + 1][...] for b in range(nbr)]
        m = lses[0]
        for t in lses[1:]:
            m = jnp.maximum(m, t)
        ws = [jnp.exp(t - m) for t in lses]
        den = ws[0]
        for t in ws[1:]:
            den = den + t
        acc = ws[0] * refs[0][...].astype(F32)
        for b in range(1, nbr):
            acc = acc + ws[b] * refs[2 * b][...].astype(F32)
        o_ref[...] = (acc / den).astype(BF16)
        lse_ref[...] = m + jnp.log(den)

    return _call(body, name=name, grid=(L // tm,), in_specs=[row] * (2 * nbr), out_specs=[row, row],
                 out_shape=[jax.ShapeDtypeStruct((L, W), BF16), jax.ShapeDtypeStruct((L, W), F32)], sem=("parallel",))(*branches)


def _dilated_fwd(proj0):
    L = proj0.shape[0]
    qkv = proj0[:, 2 * D_MODEL:]
    W = B_HEADS * HEAD_DIM
    outs = []
    for window, d in DILATED_PATTERN:
        view = qkv.reshape(L // d, d * 3 * W)
        o, lse = _band_fwd(view, view, view, d=d, nq=B_HEADS, nkv=B_HEADS, qcol=lambda r: 3 * r, kcol=lambda r: 3 * r + 1,
                           vcol=lambda r: 3 * r + 2, max_dist=window // d, name=f"dilated_fwd_d{d}")
        outs += [o.reshape(L, W), lse.reshape(L, W)]
    return _dilated_merge(outs, name="dilated_merge")


def _dilated_bwd(proj0, att, lse, datt, tabs):
    L = proj0.shape[0]
    qkv = proj0[:, 2 * D_MODEL:]
    Wh = B_HEADS * HEAD_DIM
    branches = []
    for window, d in DILATED_PATTERN:
        view = qkv.reshape(L // d, d * 3 * Wh)
        v1 = lambda t: t.reshape(L // d, d * Wh)
        outs = _band_bwd(view, view, view, v1(datt), v1(att), v1(lse), d=d, nq=B_HEADS, nkv=B_HEADS,
                         qcol=lambda r: 3 * r, kcol=lambda r: 3 * r + 1, vcol=lambda r: 3 * r + 2, docol=lambda r: r,
                         max_dist=window // d, name=f"dilated_bwd_d{d}")
        branches.append([o.reshape(L, Wh) for o in outs])
    return _attn_grad_combine(branches, tabs, name="dilated_grad_combine")


def _device_step(x, mem, target, w, on_grads=None):
    L = x.shape[0]
    tabs = _rope_tables(L)
    g = {}
    saved = []
    h = x
    for layer in range(2):
        sv = {"h_mix": h}
        if layer == 0:
            proj, n = _rowmm(h, w["ab_w_in"], name="l0_in_proj", gain=w["mix_norm"][0],
                             rope=(2 * D_MODEL, 2 * D_MODEL + 2 * B_HEADS * HEAD_DIM, tabs))
            rec, hs = _lru_fwd(proj, w["lru_conv_w"], w["lru_conv_b"], w["lru_wa"], w["lru_ba"], w["lru_wx"], w["lru_bx"],
                               w["lru_lambda"], name="lru_fwd")
            att, lse = _dilated_fwd(proj)
            mix = jnp.concatenate([rec, att], axis=1)
            (h,) = _rowmm(mix, w["ab_w_out"], name="l0_out_proj", res=h)
            sv.update(hs=hs)
        else:
            proj, n = _rowmm(h, w["c_w_qkv"], name="l1_qkv_proj", gain=w["mix_norm"][1], bias=w["c_b_qkv"],
                             rope=(0, (C_HEADS + C_KV_HEADS) * HEAD_DIM, tabs))
            mix, lse = _band_fwd(proj, proj, proj, d=1, nq=C_HEADS, nkv=C_KV_HEADS, qcol=lambda r: 0, kcol=lambda r: 8,
                                 vcol=lambda r: 9, max_dist=C_WINDOW - 1, sinks=w["c_sinks"], name="swa_fwd")
            (h,) = _rowmm(mix, w["c_w_out"], name="l1_out_proj", res=h, bias=w["c_b_out"])
        sv.update(proj=proj, n_mix=n, mix=mix, lse=lse, h_xa=h)
        xq, nx = _rowmm(h, w["xa_wq"][layer][None], name=f"xa_q_proj{layer}", gain=w["xa_norm"][layer])
        kv, nm = _rowmm(mem, w["xa_wkv"][layer][None], name=f"xa_kv_proj{layer}", gain=w["xa_mem_norm"][layer])
        xo, xlse = _xattn_fwd(xq, kv, name=f"xa_fwd{layer}")
        (h,) = _rowmm(xo, w["xa_wo"][layer], name=f"xa_out_proj{layer}", res=h)
        sv.update(xq=xq, nx=nx, kv=kv, nm=nm, xo=xo, xlse=xlse, h_ffn=h)
        gu, nf, act = _rowmm(h, w["ffn_w_gate_up"], layer=layer, name=f"ffn_in{layer}", gain=w["ffn_norm"][layer], swiglu=True)
        (h,) = _rowmm(act, w["ffn_w_down"][layer][None], name=f"ffn_out{layer}", res=h, tm=512)
        sv.update(gu=gu, nf=nf, act=act)
        saved.append(sv)

    dh, g["final_norm"], loss = _final_loss(h, w["final_norm"], target, name="final_loss")

    stk = {k: [None, None] for k in ("xa_norm", "xa_mem_norm", "ffn_norm", "mix_norm")}
    after = None
    for layer in (1, 0):
        sv = saved[layer]
        (g["ffn_w_down", layer],) = _mm_tn(sv["act"], dh, S=1, name=f"ffn_down_dw{layer}", kk=D_FF // 2)
        (dgu,) = _mm_nt(dh, w["ffn_w_down"][layer][None], name=f"ffn_dact{layer}", mode="swiglu", kchunk=D_FF // 2, gu=sv["gu"],
                        after=after)
        (g["ffn_w_gate_up", layer],) = _mm_tn(sv["nf"], dgu, S=N_CHIPS, name=f"ffn_gu_dw{layer}")
        dh, stk["ffn_norm"][layer] = _mm_nt(dgu, w["ffn_w_gate_up"], layer=layer, name=f"ffn_dx{layer}", mode="norm",
                                            h=sv["h_ffn"], gain=w["ffn_norm"][layer], dh=dh)
        (g["xa_wo", layer],) = _mm_tn(sv["xo"], dh, S=N_CHIPS, name=f"xa_wo_dw{layer}")
        (dxo,) = _mm_nt(dh, w["xa_wo"][layer], name=f"xa_dxo{layer}", mode="plain")
        dxq, dkv = _xattn_bwd(sv["xq"], sv["kv"], sv["xo"], sv["xlse"], dxo, name=f"xa_bwd{layer}")
        (g["xa_wq", layer],) = _mm_tn(sv["nx"], dxq, S=1, name=f"xa_wq_dw{layer}")
        dh, stk["xa_norm"][layer] = _mm_nt(dxq, w["xa_wq"][layer][None], name=f"xa_dx{layer}", mode="norm", h=sv["h_xa"],
                                           gain=w["xa_norm"][layer], dh=dh)
        (g["xa_wkv", layer],) = _mm_tn(sv["nm"], dkv, S=1, name=f"xa_wkv_dw{layer}")
        _, stk["xa_mem_norm"][layer] = _mm_nt(dkv, w["xa_wkv"][layer][None], name=f"xa_dmem{layer}", mode="norm", h=mem,
                                              gain=w["xa_mem_norm"][layer])
        if layer == 1:
            g["c_w_out"], g["c_b_out"] = _mm_tn(sv["mix"], dh, S=1, name="l1_out_dw", bias=True)
            (dmix,) = _mm_nt(dh, w["c_w_out"], name="l1_dmix", mode="plain")
            dq, dk, dv, dsk = _band_bwd(sv["proj"], sv["proj"], sv["proj"], dmix, sv["mix"], sv["lse"], d=1, nq=C_HEADS,
                                        nkv=C_KV_HEADS, qcol=lambda r: 0, kcol=lambda r: 8, vcol=lambda r: 9,
                                        docol=lambda r: 0, max_dist=C_WINDOW - 1, sinks=w["c_sinks"], name="swa_bwd")
            g["c_sinks"] = dsk[0, :C_HEADS]
            dproj = _attn_grad_combine([(dq, dk, dv)], tabs, name="swa_grad_combine")
            g["c_w_qkv"], g["c_b_qkv"] = _mm_tn(sv["n_mix"], dproj, S=1, name="l1_qkv_dw", bias=True)
            dh, stk["mix_norm"][1] = _mm_nt(dproj, w["c_w_qkv"], name="l1_dx", mode="norm", h=sv["h_mix"],
                                            gain=w["mix_norm"][1], dh=dh)
            if on_grads is not None:
                after = on_grads("layer1", g)
        else:
            if on_grads is not None:
                after = on_grads("layer0_ffn_xa", g)
            (g["ab_w_out"],) = _mm_tn(sv["mix"], dh, S=1, name="l0_out_dw", kk=768)
            (dmix,) = _mm_nt(dh, w["ab_w_out"], name="l0_dmix", mode="plain", kchunk=768, after=after)
            (dxy, g["lru_conv_w"], g["lru_conv_b"], g["lru_wa"], g["lru_ba"], g["lru_wx"], g["lru_bx"],
             g["lru_lambda"]) = _lru_bwd(sv["proj"], sv["hs"], dmix, w["lru_conv_w"], w["lru_conv_b"], w["lru_wa"],
                                         w["lru_ba"], w["lru_wx"], w["lru_bx"], w["lru_lambda"], name="lru_bwd")
            dqkv = _dilated_bwd(sv["proj"], sv["mix"][:, D_MODEL:], sv["lse"], dmix[:, D_MODEL:], tabs)
            dproj = jnp.concatenate([dxy, dqkv], axis=1)
            (g["ab_w_in"],) = _mm_tn(sv["n_mix"], dproj, S=N_CHIPS, name="l0_in_dw")
            dh, stk["mix_norm"][0] = _mm_nt(dproj, w["ab_w_in"], name="l0_dx", mode="norm", h=sv["h_mix"],
                                            gain=w["mix_norm"][0], dh=dh)
    for k, v in stk.items():
        g[k] = jnp.concatenate(v, axis=0)
    return loss[0, 0], dh, g


ANY = pl.BlockSpec(memory_space=pl.ANY)
MESH = pl.DeviceIdType.MESH


def _place():
    x, y, c = lax.axis_index("x"), lax.axis_index("y"), lax.axis_index("c")
    return x, y, c, [(1 - x, y), (x, 1 - y), (1 - x, 1 - y)]


def _remote(send_sems, recv_sems):
    def copy(k, src, dst, to):
        return pltpu.make_async_remote_copy(src_ref=src, dst_ref=dst, send_sem=send_sems.at[k], recv_sem=recv_sems.at[k],
                                            device_id=to, device_id_type=MESH)
    return copy


def _halves(ref, n_rows):
    rh = n_rows // 2
    return lambda lead, hh: ref.at[(*lead, pl.ds(hh * rh, rh), slice(None))]


def _gather_weights(packs, spack):
    n = len(packs)

    def body(*refs):
        w_refs, s_ref, wf_refs, sf_ref = refs[:n], refs[n], refs[n + 1:2 * n + 1], refs[2 * n + 1]
        x, y, c, chips = _place()
        me, sib = 2 * x + y, (x, y, 1 - c)
        copy = _remote(*refs[-2:])
        src = [_halves(w_refs[g], packs[g].shape[0]) for g in range(n)]
        dst = [_halves(wf_refs[g], packs[g].shape[0]) for g in range(n)]
        sends = []
        for g in range(n):
            for j, (cx, cy) in enumerate(chips):
                sends.append(copy(3 * g + j, src[g]((), c), dst[g]((me,), c), (cx, cy, c)))
        for j, (cx, cy) in enumerate(chips):
            sends.append(copy(6 * n + j, s_ref, sf_ref.at[me], (cx, cy, c)))
        for cp in sends:
            cp.start()
        for g in range(n):
            for j, (cx, cy) in enumerate(chips):
                got = dst[g]((2 * cx + cy,), c)
                copy(3 * g + j, got, got, sib).wait_recv()
                fwd = copy(3 * n + 3 * g + j, got, got, sib)
                fwd.start()
                sends.append(fwd)
        for g in range(n):
            for j, (cx, cy) in enumerate(chips):
                got = dst[g]((2 * cx + cy,), 1 - c)
                copy(3 * n + 3 * g + j, got, got, sib).wait_recv()
        for j, (cx, cy) in enumerate(chips):
            copy(6 * n + j, s_ref, sf_ref.at[2 * cx + cy], sib).wait_recv()
        for cp in sends:
            cp.wait_send()

    ins = list(packs) + [spack]
    out_shape = [jax.ShapeDtypeStruct((N_CHIPS,) + a.shape, a.dtype) for a in ins]
    n_sems = 6 * n + 3
    outs = pl.pallas_call(body, name="gather_weights", out_shape=out_shape, in_specs=[ANY] * len(ins),
                          out_specs=[ANY] * len(ins),
                          scratch_shapes=[pltpu.SemaphoreType.DMA((n_sems,)), pltpu.SemaphoreType.DMA((n_sems,))])(*ins)
    chip = 2 * lax.axis_index("x") + lax.axis_index("y")
    outs = [lax.dynamic_update_index_in_dim(o, a, chip, 0) for o, a in zip(outs, ins)]
    return outs[:n], outs[n]


SEQUENCER_GATHER_IDS = {"mid": 1, "late": 5}


def _gather_weights_behind(packs, *, tag):
    n = len(packs)

    def body(*refs):
        w_refs, wf_refs = refs[:n], refs[n:2 * n]
        x, y, c, chips = _place()
        me, sib = 2 * x + y, (x, y, 1 - c)
        barrier = pltpu.get_barrier_semaphore()
        for peer in [(cx, cy, c) for cx, cy in chips] + [sib]:
            pl.semaphore_signal(barrier, inc=1, device_id=peer, device_id_type=MESH)
        pl.semaphore_wait(barrier, len(chips) + 1)
        copy = _remote(*refs[-2:])
        src = [_halves(w_refs[g], packs[g].shape[0]) for g in range(n)]
        dst = [_halves(wf_refs[g], packs[g].shape[0]) for g in range(n)]
        sends = []
        for g in range(n):
            for j, (cx, cy) in enumerate(chips):
                sends.append(copy(3 * g + j, src[g]((), c), dst[g]((me,), c), (cx, cy, c)))
        for cp in sends:
            cp.start()
        for g in range(n):
            for j, (cx, cy) in enumerate(chips):
                got = dst[g]((2 * cx + cy,), c)
                copy(3 * g + j, got, got, sib).wait_recv()
                fwd = copy(3 * n + 3 * g + j, got, got, sib)
                fwd.start()
                sends.append(fwd)
        for g in range(n):
            for j, (cx, cy) in enumerate(chips):
                got = dst[g]((2 * cx + cy,), 1 - c)
                copy(3 * n + 3 * g + j, got, got, sib).wait_recv()
        for cp in sends:
            cp.wait_send()

    out_type = [jax.ShapeDtypeStruct((N_CHIPS,) + a.shape, a.dtype) for a in packs]
    outs = pl.kernel(body, out_type=out_type, mesh=plsc.ScalarSubcoreMesh(axis_name="sequencer", num_cores=1),
                     name="gather_weights_behind_" + tag,
                     scratch_types=[pltpu.SemaphoreType.DMA((6 * n,)), pltpu.SemaphoreType.DMA((6 * n,))],
                     compiler_params=pltpu.CompilerParams(collective_id=SEQUENCER_GATHER_IDS[tag]))(*packs)
    chip = 2 * lax.axis_index("x") + lax.axis_index("y")
    return [lax.dynamic_update_index_in_dim(o, a, chip, 0) for o, a in zip(outs, packs)]


def _rs_pair_exchange(gpacks, *, name):
    n = len(gpacks)

    def body(*refs):
        g_refs, ra_refs = refs[:n], refs[n:2 * n]
        x, y, c, _ = _place()
        copy = _remote(*refs[-2:])
        cps = []
        for g in range(n):
            half = _halves(g_refs[g], gpacks[g].shape[1])
            cps += [copy(N_CHIPS * g + j, half((j,), 1 - c), ra_refs[g].at[j], (x, y, 1 - c)) for j in range(N_CHIPS)]
        for cp in cps:
            cp.start()
        for cp in cps:
            cp.wait()

    out_shape = [jax.ShapeDtypeStruct((N_CHIPS, a.shape[1] // 2, a.shape[2]), a.dtype) for a in gpacks]
    n_sems = N_CHIPS * n
    return pl.pallas_call(body, name=name, out_shape=out_shape, in_specs=[ANY] * n, out_specs=[ANY] * n,
                          scratch_shapes=[pltpu.SemaphoreType.DMA((n_sems,)), pltpu.SemaphoreType.DMA((n_sems,))])(*gpacks)


def _row_tile(rows, cap=512):
    return max(t for t in range(16, min(rows, cap) + 1, 16) if rows % t == 0)


def _rs_pair_add(place, gpack, ra, *, name):
    _, R, C = gpack.shape
    Rh = R // 2
    tr = _row_tile(Rh)
    nrb = Rh // tr

    def body(p_ref, g_ref, ra_ref, pair_ref, own_ref):
        s = g_ref[...].astype(F32) + ra_ref[...].astype(F32)
        pair_ref[...] = s.astype(BF16)

        @pl.when(pl.program_id(1) == p_ref[1])
        def _():
            own_ref[...] = s

    grid_spec = pltpu.PrefetchScalarGridSpec(
        num_scalar_prefetch=1, grid=(nrb, N_CHIPS),
        in_specs=[pl.BlockSpec((None, tr, C), lambda i, j, p: (j, p[0] * nrb + i, 0)),
                  pl.BlockSpec((None, tr, C), lambda i, j, p: (j, i, 0))],
        out_specs=[pl.BlockSpec((None, tr, C), lambda i, j, p: (j, i, 0)), pl.BlockSpec((tr, C), lambda i, j, p: (i, 0))])
    return pl.pallas_call(
        body, name=name, grid_spec=grid_spec,
        out_shape=[jax.ShapeDtypeStruct((N_CHIPS, Rh, C), BF16), jax.ShapeDtypeStruct((Rh, C), F32)],
        compiler_params=pltpu.CompilerParams(dimension_semantics=("arbitrary", "arbitrary"),
                                             vmem_limit_bytes=VMEM_LIMIT_V7X))(place, gpack, ra)


SEQUENCER_EXCHANGE_IDS = {"l1": 2, "l0a": 3, "l0b": 4}


def _rs_chip_exchange_behind(pairs, *, tag, small=None):
    n = len(pairs)
    has_small = small is not None

    def body(*refs):
        p_refs = refs[:n]
        s_ref = refs[n] if has_small else None
        rb_refs = refs[n + has_small:2 * n + has_small]
        rs_ref = refs[2 * n + 1] if has_small else None
        x, y, c, chips = _place()
        peers = [(1 - x if k & 4 else x, 1 - y if k & 2 else y, 1 - c if k & 1 else c) for k in range(1, 8)]
        shake = peers if has_small else [(cx, cy, c) for cx, cy in chips]
        barrier = pltpu.get_barrier_semaphore()
        for peer in shake:
            pl.semaphore_signal(barrier, inc=1, device_id=peer, device_id_type=MESH)
        pl.semaphore_wait(barrier, len(shake))
        copy = _remote(*refs[-2:])
        cps = []
        for g in range(n):
            cps += [copy(3 * g + j, p_refs[g].at[2 * cx + cy], rb_refs[g].at[j], (cx, cy, c)) for j, (cx, cy) in enumerate(chips)]
        if has_small:
            dev = 4 * x + 2 * y + c
            cps += [copy(3 * n + k, s_ref, rs_ref.at[dev], peer) for k, peer in enumerate(peers)]
        for cp in cps:
            cp.start()
        for g in range(n):
            for j in range(3):
                copy(3 * g + j, p_refs[g].at[0], rb_refs[g].at[j], (x, y, c)).wait_recv()
        if has_small:
            for k, (px, py, pc) in enumerate(peers):
                copy(3 * n + k, s_ref, rs_ref.at[4 * px + 2 * py + pc], (x, y, c)).wait_recv()
        for cp in cps:
            cp.wait_send()

    ins = list(pairs) + ([small] if has_small else [])
    out_type = [jax.ShapeDtypeStruct((3,) + p.shape[1:], p.dtype) for p in pairs]
    if has_small:
        out_type.append(jax.ShapeDtypeStruct((8,) + small.shape, small.dtype))
    n_sems = 3 * n + 7 * has_small
    outs = pl.kernel(body, out_type=out_type, mesh=plsc.ScalarSubcoreMesh(axis_name="sequencer", num_cores=1),
                     name="rs_chip_exchange_behind_" + tag,
                     scratch_types=[pltpu.SemaphoreType.DMA((n_sems,)), pltpu.SemaphoreType.DMA((n_sems,))],
                     compiler_params=pltpu.CompilerParams(collective_id=SEQUENCER_EXCHANGE_IDS[tag]))(*ins)
    if has_small:
        dev = 4 * lax.axis_index("x") + 2 * lax.axis_index("y") + lax.axis_index("c")
        outs = list(outs[:n]) + [lax.dynamic_update_index_in_dim(outs[n], small, dev, 0)]
    return outs


def _rs_final_add(place, own, rb, *, name):
    Rh, C = own.shape
    tr = _row_tile(Rh)
    nrb = Rh // tr

    def body(p_ref, o_ref, rb_ref, f_ref):
        f_ref[...] = ((o_ref[...] + rb_ref[0].astype(F32)) + rb_ref[1].astype(F32)) + rb_ref[2].astype(F32)

    grid_spec = pltpu.PrefetchScalarGridSpec(
        num_scalar_prefetch=1, grid=(nrb,),
        in_specs=[pl.BlockSpec((tr, C), lambda i, p: (i, 0)), pl.BlockSpec((3, tr, C), lambda i, p: (0, i, 0))],
        out_specs=pl.BlockSpec((tr, C), lambda i, p: (p[0] * nrb + i, 0)))
    return pl.pallas_call(
        body, name=name, grid_spec=grid_spec, out_shape=jax.ShapeDtypeStruct((2 * Rh, C), F32),
        compiler_params=pltpu.CompilerParams(dimension_semantics=("arbitrary",), vmem_limit_bytes=VMEM_LIMIT_V7X))(place, own, rb)


def _sum_slots(rs):
    n, rows, C = rs.shape

    def body(r_ref, o_ref):
        acc = r_ref[0]
        for k in range(1, n):
            acc = acc + r_ref[k]
        o_ref[...] = acc

    return _call(body, name="small_grad_sum", grid=(1,), in_specs=[pl.BlockSpec((n, rows, C), lambda i: (0, 0, 0))],
                 out_specs=pl.BlockSpec((rows, C), lambda i: (0, 0)), out_shape=jax.ShapeDtypeStruct((rows, C), F32),
                 sem=("arbitrary",))(rs)


def _rs_sibling_share(gbufs, *, name):
    n = len(gbufs)

    def body(*refs):
        g_refs = refs[n:2 * n]
        x, y, c, _ = _place()
        copy = _remote(*refs[-2:])
        halves = [_halves(g_refs[g], gbufs[g].shape[0]) for g in range(n)]
        outs = [copy(g, halves[g]((), c), halves[g]((), c), (x, y, 1 - c)) for g in range(n)]
        for cp in outs:
            cp.start()
        for g in range(n):
            copy(g, halves[g]((), 1 - c), halves[g]((), 1 - c), (x, y, c)).wait_recv()
        for cp in outs:
            cp.wait_send()

    return pl.pallas_call(body, name=name, out_shape=[jax.ShapeDtypeStruct(a.shape, a.dtype) for a in gbufs],
                          in_specs=[ANY] * n, out_specs=[ANY] * n, input_output_aliases={g: g for g in range(n)},
                          scratch_shapes=[pltpu.SemaphoreType.DMA((n,)), pltpu.SemaphoreType.DMA((n,))])(*gbufs)


def _adamw(w, g, m, v, *, name, g_row=0):
    rows, cols = w.shape
    tr = rows
    for cand in range(min(rows, 512), 7, -8):
        if rows % cand == 0 and g_row % cand == 0:
            tr = cand
            break
    spec = pl.BlockSpec((tr, cols), lambda i: (i, 0))
    g_spec = pl.BlockSpec((tr, cols), lambda i: (g_row // tr + i, 0))

    def body(w_ref, g_ref, m_ref, v_ref, d_ref, nm_ref, nv_ref):
        gg = g_ref[...]
        nm = ADAM_B1 * m_ref[...] + (1.0 - ADAM_B1) * gg
        nv = ADAM_B2 * v_ref[...] + (1.0 - ADAM_B2) * (gg * gg)
        m_hat = nm / (1.0 - ADAM_B1 ** ADAM_STEP)
        v_hat = nv / (1.0 - ADAM_B2 ** ADAM_STEP)
        d_ref[...] = -ADAM_LR * (m_hat / (jnp.sqrt(v_hat) + ADAM_EPS) + ADAM_WD * w_ref[...])
        nm_ref[...] = nm
        nv_ref[...] = nv

    return _call(body, name=name, grid=(rows // tr,), in_specs=[spec, g_spec, spec, spec], out_specs=[spec] * 3,
                 out_shape=[jax.ShapeDtypeStruct((rows, cols), F32)] * 3, sem=("parallel",))(w, g, m, v)


WEIGHT_NAMES = ("mix_norm", "ab_w_in", "lru_conv_w", "lru_conv_b", "lru_wa", "lru_ba", "lru_wx", "lru_bx", "lru_lambda",
                "ab_w_out", "c_w_qkv", "c_b_qkv", "c_sinks", "c_w_out", "c_b_out", "xa_norm", "xa_mem_norm", "xa_wq",
                "xa_wkv", "xa_wo", "ffn_norm", "ffn_w_gate_up", "ffn_w_down", "final_norm")
EARLY_GROUPS = (("ab_w_in",),)
MID_GROUPS = (("ab_w_out",), ("lru_wa", "lru_wx"))
LATE_GROUPS = (("c_w_out", "xa_wkv", "ffn_w_down"), ("ffn_w_gate_up",), ("xa_wo",), ("xa_wq",), ("c_w_qkv",))
GROUPS = EARLY_GROUPS + MID_GROUPS + LATE_GROUPS
REPLICATED = ("mix_norm", "lru_conv_b", "lru_lambda", "c_sinks", "xa_norm", "xa_mem_norm", "ffn_norm", "final_norm")
SMALL_SHARDED = ("lru_conv_w", "lru_ba", "lru_bx", "c_b_qkv", "c_b_out")
LANES = 1024


def _rows(v):
    flat = v.reshape(-1)
    return jnp.pad(flat, (0, -flat.shape[0] % LANES)).reshape(-1, LANES)


def _pack_small(parts, total, *, name):
    def body(*refs):
        o_ref = refs[-1]
        o_ref[...] = jnp.zeros_like(o_ref)
        row = 0
        for p_ref in refs[:-1]:
            o_ref[row:row + p_ref.shape[0], :] = p_ref[...]
            row += p_ref.shape[0]

    return _call(body, name=name, grid=(1,), in_specs=[pl.BlockSpec(p.shape, lambda i: (0, 0)) for p in parts],
                 out_specs=pl.BlockSpec((total, LANES), lambda i: (0, 0)),
                 out_shape=jax.ShapeDtypeStruct((total, LANES), F32), sem=("arbitrary",))(*parts)


def _from_shards(name, t):
    minor = t.shape[-1]
    if name == "ab_w_in":
        return t
    if name in ("ab_w_out", "c_w_out"):
        return t.reshape(1, -1, minor)
    if name == "ffn_w_gate_up":
        return t.reshape(N_CHIPS, 2, -1, minor)
    if name in ("xa_wq", "xa_wkv", "ffn_w_down"):
        return t.reshape(N_CHIPS, 2, -1, minor).transpose(1, 0, 2, 3).reshape(2, -1, minor)
    if name in ("lru_wa", "lru_wx"):
        return t.reshape(N_CHIPS, LRU_HEADS, -1, minor).transpose(1, 0, 2, 3).reshape(LRU_HEADS, LRU_HEAD_DIM, minor)
    if name == "xa_wo":
        return t.reshape(N_CHIPS, 2, -1, minor).transpose(1, 0, 2, 3)
    assert name == "c_w_qkv"
    return t.transpose(1, 0, 2).reshape(1, D_MODEL, -1)


def _piece_shards(name, g):
    minor = g.shape[-1]
    if name in ("ab_w_in", "ffn_w_gate_up", "xa_wo"):
        return g
    if name in ("ab_w_out", "c_w_out", "xa_wq", "xa_wkv", "ffn_w_down"):
        return g.reshape(N_CHIPS, -1, minor)
    if name in ("lru_wa", "lru_wx"):
        return g.reshape(LRU_HEADS, N_CHIPS, -1, minor).transpose(1, 0, 2, 3).reshape(N_CHIPS, -1, minor)
    assert name == "c_w_qkv"
    return g.reshape(D_MODEL, N_CHIPS, -1).transpose(1, 0, 2)


RS_SETS = {
    "l1": ((("c_w_out", None), ("xa_wkv", 1), ("ffn_w_down", 1)), (("ffn_w_gate_up", 1),), (("xa_wo", 1),),
           (("xa_wq", 1),), (("c_w_qkv", None),)),
    "l0a": ((("xa_wkv", 0), ("ffn_w_down", 0)), (("ffn_w_gate_up", 0),), (("xa_wo", 0),), (("xa_wq", 0),)),
    "l0b": ((("ab_w_out", None),), (("ab_w_in", None),), (("lru_wa", None), ("lru_wx", None))),
}
RS_STAGE = {"layer1": "l1", "layer0_ffn_xa": "l0a"}


def kernel(x, mem, mix_norm, ab_w_in, lru_conv_w, lru_conv_b, lru_wa, lru_ba, lru_wx, lru_bx, lru_lambda, ab_w_out, c_w_qkv, c_b_qkv, c_sinks, c_w_out, c_b_out, xa_norm, xa_mem_norm, xa_wq, xa_wkv, xa_wo, ffn_norm, ffn_w_gate_up, ffn_w_down, final_norm, loss_target, m_mix_norm, m_ab_w_in, m_lru_conv_w, m_lru_conv_b, m_lru_wa, m_lru_ba, m_lru_wx, m_lru_bx, m_lru_lambda, m_ab_w_out, m_c_w_qkv, m_c_b_qkv, m_c_sinks, m_c_w_out, m_c_b_out, m_xa_norm, m_xa_mem_norm, m_xa_wq, m_xa_wkv, m_xa_wo, m_ffn_norm, m_ffn_w_gate_up, m_ffn_w_down, m_final_norm, v_mix_norm, v_ab_w_in, v_lru_conv_w, v_lru_conv_b, v_lru_wa, v_lru_ba, v_lru_wx, v_lru_bx, v_lru_lambda, v_ab_w_out, v_c_w_qkv, v_c_b_qkv, v_c_sinks, v_c_w_out, v_c_b_out, v_xa_norm, v_xa_mem_norm, v_xa_wq, v_xa_wkv, v_xa_wo, v_ffn_norm, v_ffn_w_gate_up, v_ffn_w_down, v_final_norm):
    given = dict(locals())
    wl = {n: given[n] for n in WEIGHT_NAMES}
    ml = {n: given["m_" + n] for n in WEIGHT_NAMES}
    vl = {n: given["v_" + n] for n in WEIGHT_NAMES}
    xi, yi, ci = lax.axis_index("x"), lax.axis_index("y"), lax.axis_index("c")
    chip = 2 * xi + yi

    def join(parts, axis):
        return parts[0] if len(parts) == 1 else jnp.concatenate(parts, axis=axis)

    local_rows = {n: wl[n].size // wl[n].shape[-1] for grp in GROUPS for n in grp}
    packs = [join([wl[n].astype(BF16).reshape(local_rows[n], wl[n].shape[-1]) for n in grp], 0) for grp in GROUPS]
    spack = _pack_small([_rows(wl[n]) for n in SMALL_SHARDED], 8, name="pack_small_weights")
    n_early, n_mid = len(EARLY_GROUPS), len(EARLY_GROUPS) + len(MID_GROUPS)
    early, sfull = _gather_weights(packs[:n_early], spack)
    early, sfull, mid_packs = lax.optimization_barrier((early, sfull, packs[n_early:n_mid]))
    mid = _gather_weights_behind(mid_packs, tag="mid")
    mid, late_packs = lax.optimization_barrier((mid, packs[n_mid:]))
    gathered = early + mid + _gather_weights_behind(late_packs, tag="late")
    w = {n: wl[n] for n in REPLICATED}
    w["c_sinks"] = wl["c_sinks"][0]
    for grp, full in zip(GROUPS, gathered):
        off = 0
        for n in grp:
            w[n] = _from_shards(n, full if len(grp) == 1 else full[:, off:off + local_rows[n]])
            off += local_rows[n]
    for r, n in enumerate(SMALL_SHARDED):
        loc = wl[n].shape[1:]
        t = sfull[:, r, :wl[n].size].reshape((N_CHIPS,) + loc)
        if n == "lru_conv_w":
            w[n] = t.transpose(1, 0, 2).reshape(CONV_WIDTH, -1)
        elif n in ("lru_ba", "lru_bx"):
            w[n] = t.transpose(1, 0, 2).reshape(1, -1)
        else:
            w[n] = t.reshape(1, -1)

    place = jnp.stack([ci, chip]).astype(jnp.int32)

    def pair_stage(spec, g, tag):
        piece = lambda n, l: (g[n] if l is None else g[n, l]).astype(BF16)
        gpacks = [join([_piece_shards(n, piece(n, l)) for n, l in grp], 1) for grp in spec]
        ras = _rs_pair_exchange(gpacks, name=f"rs_pair_exchange_{tag}")
        sums = [_rs_pair_add(place, gp, ra, name=f"rs_pair_add_{tag}_{i}") for i, (gp, ra) in enumerate(zip(gpacks, ras))]
        return [pair for pair, _ in sums], [own for _, own in sums]

    reduced, in_flight = [], []

    def take_up():
        done = [_rs_final_add(place, o, r, name=f"rs_final_add_{len(reduced) + i}") for i, (o, r) in enumerate(in_flight)]
        reduced.extend(done)
        in_flight.clear()
        return done

    def reduce_behind(stage, g):
        done = take_up()
        tag = RS_STAGE[stage]
        pairs, own = pair_stage(RS_SETS[tag], g, tag)
        in_flight.extend(zip(own, _rs_chip_exchange_behind(pairs, tag=tag)))
        return own + done

    loss_part, grad_x, g = _device_step(x[0], mem[0], loss_target[0], w, on_grads=reduce_behind)

    small_parts = [_rows(g[n]) for n in REPLICATED] + [_rows(jnp.broadcast_to(loss_part, (LANES,)))]
    small_parts += [_rows(g[n]) for n in SMALL_SHARDED]
    small = _pack_small(small_parts, 24, name="pack_small_grads")
    take_up()
    pairs, own = pair_stage(RS_SETS["l0b"], g, "l0b")
    *rb, rs = _rs_chip_exchange_behind(pairs, tag="l0b", small=small)
    gsums = list(_rs_sibling_share(list(reduced), name="rs_sibling_share_behind"))
    in_flight.extend(zip(own, rb))
    gsums += list(_rs_sibling_share(take_up(), name="rs_sibling_share_last"))
    ssum = _sum_slots(rs)

    where = {}
    for grp, gsum in zip(RS_SETS["l1"] + RS_SETS["l0a"] + RS_SETS["l0b"], gsums):
        off = 0
        for n, l in grp:
            rows = local_rows[n] if l is None else local_rows[n] // 2
            where[n, l] = (gsum, off, rows, len(grp) == 1)
            off += rows
    take = lambda gsum, off, rows, whole: gsum if whole else gsum[off:off + rows]
    grads, grad_rows = {}, {}
    for grp in LATE_GROUPS + EARLY_GROUPS + MID_GROUPS:
        for n in grp:
            if (n, None) in where:
                grads[n] = take(*where[n, None]).reshape(wl[n].shape)
                grad_rows[n] = where[n, None][:2]
            else:
                grads[n] = jnp.stack([take(*where[n, l]).reshape(wl[n].shape[1:]) for l in range(2)])
                grad_rows[n] = (grads[n].reshape(local_rows[n], wl[n].shape[-1]), 0)
    row = 0
    for n in REPLICATED:
        k = _rows(g[n]).shape[0]
        grads[n] = ssum[row:row + k].reshape(-1)[:wl[n].size].reshape(wl[n].shape)
        row += k
    loss = ssum[row, 0]
    row += 1
    for n in SMALL_SHARDED:
        k = _rows(g[n]).shape[0]
        full = ssum[row:row + k].reshape(-1)[:g[n].size]
        row += k
        loc = wl[n].shape
        if n == "lru_conv_w":
            sh = full.reshape(CONV_WIDTH, N_CHIPS, -1)
        elif n in ("lru_ba", "lru_bx"):
            sh = full.reshape(LRU_HEADS, N_CHIPS, -1)
        else:
            sh = full.reshape(1, N_CHIPS, -1)
        grads[n] = lax.dynamic_index_in_dim(sh, chip, axis=1, keepdims=False).reshape(loc)

    delta, new_m, new_v = {}, {}, {}
    for n, (gsum, off) in grad_rows.items():
        shape2 = (local_rows[n], wl[n].shape[-1])
        d, nm, nv = _adamw(wl[n].reshape(shape2), gsum, ml[n].reshape(shape2), vl[n].reshape(shape2), g_row=off,
                           name="adamw_" + n)
        delta[n], new_m[n], new_v[n] = (t.reshape(wl[n].shape) for t in (d, nm, nv))
    smalls = REPLICATED + SMALL_SHARDED
    packs = [_pack_small([_rows(src[n]) for n in smalls], 24, name="pack_adamw_" + tag)
             for tag, src in (("w", wl), ("g", grads), ("m", ml), ("v", vl))]
    outs = _adamw(*packs, name="adamw_small")
    row = 0
    for n in smalls:
        k = _rows(wl[n]).shape[0]
        for dst, o in zip((delta, new_m, new_v), outs):
            dst[n] = o[row:row + k].reshape(-1)[:wl[n].size].reshape(wl[n].shape)
        row += k

    return (loss, grad_x[None], *[grads[n] for n in WEIGHT_NAMES], *[delta[n] for n in WEIGHT_NAMES],
            *[new_m[n] for n in WEIGHT_NAMES], *[new_v[n] for n in WEIGHT_NAMES])
```
